```python
import math
import jax, jax.numpy as jnp
from jax import lax
import numpy as np

D_MODEL = 1024
BATCH = 8
SEQ = 8192
DEPTH = 2

N_MIXERS = 2
POOL_WINDOWS = (2, 4, 8, 16)
POOL_GROUPS = len(POOL_WINDOWS)
POOL_GROUP_DIM = D_MODEL // POOL_GROUPS
RET_HEADS = 4
RET_QK_DIM = D_MODEL // RET_HEADS
RET_V_DIM = 2 * D_MODEL // RET_HEADS
RET_QK_TOTAL = RET_HEADS * RET_QK_DIM
RET_V_TOTAL = RET_HEADS * RET_V_DIM
RET_IN_DIM = 2 * RET_QK_TOTAL + 2 * RET_V_TOTAL
RET_CHUNK = 128
ROPE_BASE = 10000.0
D_FF = 2816
CONV_WIDTH = 3
EPS = 1e-6
N_NORMS = 4
N_POOL_LAYERS = (DEPTH + N_MIXERS - 1) // N_MIXERS
N_RET_LAYERS = DEPTH // N_MIXERS

kernel_name = "hybrid_pool_retention_convglu"


def rms_norm(x, gain):
    xf = x.astype(jnp.float32)
    var = jnp.mean(xf * xf, axis=-1, keepdims=True)
    return (xf * lax.rsqrt(var + EPS) * gain.astype(jnp.float32)).astype(x.dtype)


def pool_mixer(h, w_groups, scale):
    b, s, d = h.shape
    hf = h.astype(jnp.float32)
    cs = jnp.cumsum(hf, axis=1)
    t = jnp.arange(s)
    parts = []
    for g, win in enumerate(POOL_WINDOWS):
        lo, hi = g * POOL_GROUP_DIM, (g + 1) * POOL_GROUP_DIM
        c = cs[..., lo:hi]
        c_shift = jnp.pad(c, ((0, 0), (win, 0), (0, 0)))[:, :s]
        count = jnp.minimum(t + 1, win).astype(jnp.float32)[None, :, None]
        parts.append((c - c_shift) / count - hf[..., lo:hi])
    diff = jnp.concatenate(parts, axis=-1).reshape(b, s, POOL_GROUPS, POOL_GROUP_DIM)
    y = jnp.einsum('bsgc,gcd->bsgd', diff, w_groups.astype(jnp.float32)).reshape(b, s, d)
    return (y * scale.astype(jnp.float32)).astype(h.dtype)


def rotary(x, positions):
    dh = x.shape[-1]
    inv_freq = ROPE_BASE ** (-jnp.arange(0, dh, 2, dtype=jnp.float32) / dh)
    ang = positions.astype(jnp.float32)[..., None] * inv_freq
    cos = jnp.cos(ang)[:, :, None, :]
    sin = jnp.sin(ang)[:, :, None, :]
    x1, x2 = x[..., : dh // 2], x[..., dh // 2:]
    return jnp.concatenate([x1 * cos - x2 * sin, x2 * cos + x1 * sin], axis=-1)


def retention(h, positions, w_in, gn_gain, w_out):
    b, s, _ = h.shape
    n_chunks = s // RET_CHUNK
    proj = (h @ w_in).astype(jnp.float32)
    q, k, v, g = jnp.split(proj, [RET_QK_TOTAL, 2 * RET_QK_TOTAL, 2 * RET_QK_TOTAL + RET_V_TOTAL], axis=-1)
    q = rotary(q.reshape(b, s, RET_HEADS, RET_QK_DIM), positions) * (RET_QK_DIM ** -0.5)
    k = rotary(k.reshape(b, s, RET_HEADS, RET_QK_DIM), positions)
    v = v.reshape(b, s, RET_HEADS, RET_V_DIM)

    def to_chunks(a):
        return a.reshape(b, n_chunks, RET_CHUNK, RET_HEADS, a.shape[-1]).transpose(1, 0, 3, 2, 4)

    qc, kc, vc = to_chunks(q), to_chunks(k), to_chunks(v)

    log_gamma = jnp.log(1.0 - 2.0 ** (-5.0 - jnp.arange(RET_HEADS, dtype=jnp.float32)))
    i = jnp.arange(RET_CHUNK, dtype=jnp.float32)
    rel = i[:, None] - i[None, :]
    intra_decay = jnp.where(rel >= 0, jnp.exp(jnp.maximum(rel, 0.0) * log_gamma[:, None, None]), 0.0)
    cross_decay = jnp.exp((i + 1.0) * log_gamma[:, None])[None, :, :, None]
    in_decay = jnp.exp((RET_CHUNK - 1.0 - i) * log_gamma[:, None])[None, :, :, None]
    chunk_decay = jnp.exp(RET_CHUNK * log_gamma)[None, :, None, None]

    def step(state, xs):
        qb, kb, vb = xs
        scores = jnp.einsum('bhid,bhjd->bhij', qb, kb) * intra_decay[None]
        out = jnp.einsum('bhij,bhjv->bhiv', scores, vb) \
            + jnp.einsum('bhid,bhdv->bhiv', qb, state) * cross_decay
        state = state * chunk_decay + jnp.einsum('bhjd,bhjv->bhdv', kb * in_decay, vb)
        return state, out

    state0 = jnp.zeros((b, RET_HEADS, RET_QK_DIM, RET_V_DIM), jnp.float32)
    _, o = lax.scan(step, state0, (qc, kc, vc))
    o = o.transpose(1, 0, 3, 2, 4).reshape(b, s, RET_HEADS, RET_V_DIM)
    mu = jnp.mean(o, axis=-1, keepdims=True)
    var = jnp.mean(jnp.square(o - mu), axis=-1, keepdims=True)
    o = ((o - mu) * lax.rsqrt(var + EPS)).reshape(b, s, RET_V_TOTAL) * gn_gain.astype(jnp.float32)
    y = jax.nn.silu(g) * o
    return (y.astype(h.dtype) @ w_out)


def conv_glu(h, w_up, conv_w, conv_b, w_down):
    s = h.shape[1]
    u = h @ w_up
    up = jnp.pad(u, ((0, 0), (CONV_WIDTH - 1, 0), (0, 0)))
    c = conv_b
    for tap in range(CONV_WIDTH):
        c = c + up[:, tap:tap + s] * conv_w[tap]
    gate, val = jnp.split(c, 2, axis=-1)
    return (jax.nn.silu(gate) * val) @ w_down


def _fwd_setup_inputs(seed: int = 0) -> dict:
    key = jax.random.key(seed)
    ks = jax.random.split(key, 14)
    f32 = jnp.float32
    x = jax.random.normal(ks[0], (BATCH, SEQ, D_MODEL), f32)
    offsets = jax.random.randint(ks[1], (BATCH, 1), 0, 1024, dtype=jnp.int32)
    positions = (jnp.arange(SEQ, dtype=jnp.int32)[None, :] + offsets).astype(jnp.int32)
    norm_gain = 1.0 + 0.05 * jax.random.normal(ks[2], (DEPTH, N_NORMS, D_MODEL), f32)
    pool_w = jax.random.normal(ks[3], (N_POOL_LAYERS, POOL_GROUPS, POOL_GROUP_DIM, POOL_GROUP_DIM), f32) * POOL_GROUP_DIM ** -0.5
    pool_scale = 1.0 + 0.1 * jax.random.normal(ks[4], (N_POOL_LAYERS, D_MODEL), f32)
    ret_w_in = jax.random.normal(ks[5], (N_RET_LAYERS, D_MODEL, RET_IN_DIM), f32) * D_MODEL ** -0.5
    ret_gn_gain = 1.0 + 0.05 * jax.random.normal(ks[6], (N_RET_LAYERS, RET_V_TOTAL), f32)
    ret_w_out = jax.random.normal(ks[7], (N_RET_LAYERS, RET_V_TOTAL, D_MODEL), f32) * RET_V_TOTAL ** -0.5
    mlp_w_up = jax.random.normal(ks[8], (DEPTH, D_MODEL, 2 * D_FF), f32) * D_MODEL ** -0.5
    mlp_conv_w = jax.random.normal(ks[9], (DEPTH, CONV_WIDTH, 2 * D_FF), f32) * CONV_WIDTH ** -0.5
    mlp_conv_b = 0.02 * jax.random.normal(ks[10], (DEPTH, 2 * D_FF), f32)
    mlp_w_down = jax.random.normal(ks[11], (DEPTH, D_FF, D_MODEL), f32) * D_FF ** -0.5
    return {"x": x, "positions": positions, "norm_gain": norm_gain,
            "pool_w": pool_w, "pool_scale": pool_scale,
            "ret_w_in": ret_w_in, "ret_gn_gain": ret_gn_gain, "ret_w_out": ret_w_out,
            "mlp_w_up": mlp_w_up, "mlp_conv_w": mlp_conv_w, "mlp_conv_b": mlp_conv_b,
            "mlp_w_down": mlp_w_down}


def _fwd_reference(x, positions, norm_gain, pool_w, pool_scale, ret_w_in, ret_gn_gain,
              ret_w_out, mlp_w_up, mlp_conv_w, mlp_conv_b, mlp_w_down):
    for layer in range(DEPTH):
        gains = norm_gain[layer]
        h = rms_norm(x, gains[0])
        j = layer // N_MIXERS
        if layer % N_MIXERS == 0:
            m = pool_mixer(h, pool_w[j], pool_scale[j])
        else:
            m = retention(h, positions, ret_w_in[j], ret_gn_gain[j], ret_w_out[j])
        x = x + rms_norm(m, gains[1])
        h = rms_norm(x, gains[2])
        f = conv_glu(h, mlp_w_up[layer], mlp_conv_w[layer], mlp_conv_b[layer], mlp_w_down[layer])
        x = x + rms_norm(f, gains[3])
    return x


import jax as _jax
import jax.numpy as _jnp

TWIN_FORMAT = 'train_step'
FWD_PARAMS = ['x', 'positions', 'norm_gain', 'pool_w', 'pool_scale', 'ret_w_in', 'ret_gn_gain', 'ret_w_out', 'mlp_w_up', 'mlp_conv_w', 'mlp_conv_b', 'mlp_w_down']
TWIN_WEIGHTS = ['norm_gain', 'pool_w', 'pool_scale', 'ret_w_in', 'ret_gn_gain', 'ret_w_out', 'mlp_w_up', 'mlp_conv_w', 'mlp_conv_b', 'mlp_w_down']
TWIN_DIFF_INPUT = 'x'
TWIN_INPUTS = ['x', 'positions', 'norm_gain', 'pool_w', 'pool_scale', 'ret_w_in', 'ret_gn_gain', 'ret_w_out', 'mlp_w_up', 'mlp_conv_w', 'mlp_conv_b', 'mlp_w_down', 'loss_target', 'm_norm_gain', 'm_pool_w', 'm_pool_scale', 'm_ret_w_in', 'm_ret_gn_gain', 'm_ret_w_out', 'm_mlp_w_up', 'm_mlp_conv_w', 'm_mlp_conv_b', 'm_mlp_w_down', 'v_norm_gain', 'v_pool_w', 'v_pool_scale', 'v_ret_w_in', 'v_ret_gn_gain', 'v_ret_w_out', 'v_mlp_w_up', 'v_mlp_conv_w', 'v_mlp_conv_b', 'v_mlp_w_down']
TWIN_OUTPUTS = ['loss', 'grad_x', 'grad_norm_gain', 'grad_pool_w', 'grad_pool_scale', 'grad_ret_w_in', 'grad_ret_gn_gain', 'grad_ret_w_out', 'grad_mlp_w_up', 'grad_mlp_conv_w', 'grad_mlp_conv_b', 'grad_mlp_w_down', 'delta_norm_gain', 'delta_pool_w', 'delta_pool_scale', 'delta_ret_w_in', 'delta_ret_gn_gain', 'delta_ret_w_out', 'delta_mlp_w_up', 'delta_mlp_conv_w', 'delta_mlp_conv_b', 'delta_mlp_w_down', 'new_m_norm_gain', 'new_m_pool_w', 'new_m_pool_scale', 'new_m_ret_w_in', 'new_m_ret_gn_gain', 'new_m_ret_w_out', 'new_m_mlp_w_up', 'new_m_mlp_conv_w', 'new_m_mlp_conv_b', 'new_m_mlp_w_down', 'new_v_norm_gain', 'new_v_pool_w', 'new_v_pool_scale', 'new_v_ret_w_in', 'new_v_ret_gn_gain', 'new_v_ret_w_out', 'new_v_mlp_w_up', 'new_v_mlp_conv_w', 'new_v_mlp_conv_b', 'new_v_mlp_w_down']
TWIN_LEAF_KINDS = {'loss': 'loss', 'grad_x': 'grad_x', 'grad_norm_gain': 'grad_w', 'grad_pool_w': 'grad_w', 'grad_pool_scale': 'grad_w', 'grad_ret_w_in': 'grad_w', 'grad_ret_gn_gain': 'grad_w', 'grad_ret_w_out': 'grad_w', 'grad_mlp_w_up': 'grad_w', 'grad_mlp_conv_w': 'grad_w', 'grad_mlp_conv_b': 'grad_w', 'grad_mlp_w_down': 'grad_w', 'delta_norm_gain': 'delta_w', 'delta_pool_w': 'delta_w', 'delta_pool_scale': 'delta_w', 'delta_ret_w_in': 'delta_w', 'delta_ret_gn_gain': 'delta_w', 'delta_ret_w_out': 'delta_w', 'delta_mlp_w_up': 'delta_w', 'delta_mlp_conv_w': 'delta_w', 'delta_mlp_conv_b': 'delta_w', 'delta_mlp_w_down': 'delta_w', 'new_m_norm_gain': 'new_m', 'new_m_pool_w': 'new_m', 'new_m_pool_scale': 'new_m', 'new_m_ret_w_in': 'new_m', 'new_m_ret_gn_gain': 'new_m', 'new_m_ret_w_out': 'new_m', 'new_m_mlp_w_up': 'new_m', 'new_m_mlp_conv_w': 'new_m', 'new_m_mlp_conv_b': 'new_m', 'new_m_mlp_w_down': 'new_m', 'new_v_norm_gain': 'new_v', 'new_v_pool_w': 'new_v', 'new_v_pool_scale': 'new_v', 'new_v_ret_w_in': 'new_v', 'new_v_ret_gn_gain': 'new_v', 'new_v_ret_w_out': 'new_v', 'new_v_mlp_w_up': 'new_v', 'new_v_mlp_conv_w': 'new_v', 'new_v_mlp_conv_b': 'new_v', 'new_v_mlp_w_down': 'new_v'}


def _forward(args):
    return _fwd_reference(*[args[k] for k in FWD_PARAMS])


def _output_shape():
    def fwd():
        inp = _fwd_setup_inputs(0)
        return _fwd_reference(*[inp[k] for k in FWD_PARAMS])
    out = _jax.eval_shape(fwd)
    return out.shape, out.dtype

N_MICROBATCH = 1
ADAM_LR = 0.001
ADAM_B1 = 0.9
ADAM_B2 = 0.999
ADAM_EPS = 1e-08
ADAM_WD = 0.01
ADAM_STEP = 10
PER_EXAMPLE_BATCH_AXIS = {'x': 0, 'positions': 0, 'loss_target': 0}
SHARED_INPUTS = []
_WEIGHT_DTYPES = {'norm_gain': _jnp.float32, 'pool_w': _jnp.float32, 'pool_scale': _jnp.float32, 'ret_w_in': _jnp.float32, 'ret_gn_gain': _jnp.float32, 'ret_w_out': _jnp.float32, 'mlp_w_up': _jnp.float32, 'mlp_conv_w': _jnp.float32, 'mlp_conv_b': _jnp.float32, 'mlp_w_down': _jnp.float32}
MOMENT_SCALE = {'norm_gain': 4.565385e+01, 'pool_w': 2.664478e+00, 'pool_scale': 5.678147e+00, 'ret_w_in': 5.980929e-01, 'ret_gn_gain': 5.321706e-01, 'ret_w_out': 7.547096e-01, 'mlp_w_up': 5.564239e-01, 'mlp_conv_w': 6.025351e-01, 'mlp_conv_b': 9.490891e-01, 'mlp_w_down': 1.003907e+00}


def _to_microbatches(a, axis):
    t = _jnp.moveaxis(a, axis, 0)
    t = t.reshape((N_MICROBATCH, t.shape[0] // N_MICROBATCH) + t.shape[1:])
    return _jnp.moveaxis(t, 1, axis + 1)


def setup_inputs(seed: int = 0) -> dict:
    inp = _fwd_setup_inputs(seed)
    key = _jax.random.fold_in(_jax.random.key(seed), 7919)
    shape, _ = _output_shape()
    out = dict(inp)
    out["loss_target"] = _jax.random.normal(_jax.random.fold_in(key, 0), shape, _jnp.float32)
    for i, name in enumerate(TWIN_WEIGHTS):
        w = inp[name].astype(_jnp.float32)
        if MOMENT_SCALE is None:
            s = _jnp.sqrt(_jnp.mean(_jnp.square(w)) + 1e-30)
        else:
            s = MOMENT_SCALE[name]
        km, kv = _jax.random.split(_jax.random.fold_in(key, i + 1))
        out[name] = w
        out["m_" + name] = s * _jax.random.normal(km, w.shape, _jnp.float32)
        out["v_" + name] = (s * s) * _jax.random.uniform(kv, w.shape, _jnp.float32, 0.5, 1.5)
    if N_MICROBATCH > 1:
        for name, axis in PER_EXAMPLE_BATCH_AXIS.items():
            out[name] = _to_microbatches(out[name], axis)
    return {'x': out['x'], 'positions': out['positions'], 'norm_gain': out['norm_gain'], 'pool_w': out['pool_w'], 'pool_scale': out['pool_scale'], 'ret_w_in': out['ret_w_in'], 'ret_gn_gain': out['ret_gn_gain'], 'ret_w_out': out['ret_w_out'], 'mlp_w_up': out['mlp_w_up'], 'mlp_conv_w': out['mlp_conv_w'], 'mlp_conv_b': out['mlp_conv_b'], 'mlp_w_down': out['mlp_w_down'], 'loss_target': out['loss_target'], 'm_norm_gain': out['m_norm_gain'], 'm_pool_w': out['m_pool_w'], 'm_pool_scale': out['m_pool_scale'], 'm_ret_w_in': out['m_ret_w_in'], 'm_ret_gn_gain': out['m_ret_gn_gain'], 'm_ret_w_out': out['m_ret_w_out'], 'm_mlp_w_up': out['m_mlp_w_up'], 'm_mlp_conv_w': out['m_mlp_conv_w'], 'm_mlp_conv_b': out['m_mlp_conv_b'], 'm_mlp_w_down': out['m_mlp_w_down'], 'v_norm_gain': out['v_norm_gain'], 'v_pool_w': out['v_pool_w'], 'v_pool_scale': out['v_pool_scale'], 'v_ret_w_in': out['v_ret_w_in'], 'v_ret_gn_gain': out['v_ret_gn_gain'], 'v_ret_w_out': out['v_ret_w_out'], 'v_mlp_w_up': out['v_mlp_w_up'], 'v_mlp_conv_w': out['v_mlp_conv_w'], 'v_mlp_conv_b': out['v_mlp_conv_b'], 'v_mlp_w_down': out['v_mlp_w_down']}


def _loss(weights, diff, rest, loss_target):
    with _jax.named_scope("forward"):
        args = {**rest, TWIN_DIFF_INPUT: diff, **{k: w.astype(_WEIGHT_DTYPES[k]) for k, w in weights.items()}}
        y = _forward(args)
    with _jax.named_scope("loss_head"):
        err = _jnp.square(y.astype(_jnp.float32) - loss_target)
        return 0.5 * _jnp.sum(_jnp.mean(err, axis=-1)) if err.ndim else 0.5 * err


def _adamw(w, g, m, v):
    m = ADAM_B1 * m + (1.0 - ADAM_B1) * g
    v = ADAM_B2 * v + (1.0 - ADAM_B2) * _jnp.square(g)
    m_hat = m / (1.0 - ADAM_B1 ** ADAM_STEP)
    v_hat = v / (1.0 - ADAM_B2 ** ADAM_STEP)
    delta = -ADAM_LR * (m_hat / (_jnp.sqrt(v_hat) + ADAM_EPS) + ADAM_WD * w)
    return delta, m, v


def reference(x, positions, norm_gain, pool_w, pool_scale, ret_w_in, ret_gn_gain, ret_w_out, mlp_w_up, mlp_conv_w, mlp_conv_b, mlp_w_down, loss_target, m_norm_gain, m_pool_w, m_pool_scale, m_ret_w_in, m_ret_gn_gain, m_ret_w_out, m_mlp_w_up, m_mlp_conv_w, m_mlp_conv_b, m_mlp_w_down, v_norm_gain, v_pool_w, v_pool_scale, v_ret_w_in, v_ret_gn_gain, v_ret_w_out, v_mlp_w_up, v_mlp_conv_w, v_mlp_conv_b, v_mlp_w_down):
    given = dict(x=x, positions=positions, norm_gain=norm_gain, pool_w=pool_w, pool_scale=pool_scale, ret_w_in=ret_w_in, ret_gn_gain=ret_gn_gain, ret_w_out=ret_w_out, mlp_w_up=mlp_w_up, mlp_conv_w=mlp_conv_w, mlp_conv_b=mlp_conv_b, mlp_w_down=mlp_w_down, loss_target=loss_target, m_norm_gain=m_norm_gain, m_pool_w=m_pool_w, m_pool_scale=m_pool_scale, m_ret_w_in=m_ret_w_in, m_ret_gn_gain=m_ret_gn_gain, m_ret_w_out=m_ret_w_out, m_mlp_w_up=m_mlp_w_up, m_mlp_conv_w=m_mlp_conv_w, m_mlp_conv_b=m_mlp_conv_b, m_mlp_w_down=m_mlp_w_down, v_norm_gain=v_norm_gain, v_pool_w=v_pool_w, v_pool_scale=v_pool_scale, v_ret_w_in=v_ret_w_in, v_ret_gn_gain=v_ret_gn_gain, v_ret_w_out=v_ret_w_out, v_mlp_w_up=v_mlp_w_up, v_mlp_conv_w=v_mlp_conv_w, v_mlp_conv_b=v_mlp_conv_b, v_mlp_w_down=v_mlp_w_down)
    weights = {n: given[n] for n in TWIN_WEIGHTS}
    shared = {n: given[n] for n in SHARED_INPUTS}
    per_example = {n: given[n] for n in ['x', 'positions']}
    grad_fn = _jax.value_and_grad(_loss, argnums=(0, 1))

    def one_microbatch(ex, loss_target):
        ex = dict(ex)
        diff = ex.pop(TWIN_DIFF_INPUT)
        return grad_fn(weights, diff, {**shared, **ex}, loss_target)

    if N_MICROBATCH == 1:
        loss, (grad_w, grad_x) = one_microbatch(per_example, given["loss_target"])
    else:
        def body(carry, xs):
            loss_sum, grad_sum = carry
            l_k, (gw_k, gx_k) = one_microbatch(xs[0], xs[1])
            with _jax.named_scope("update"):
                return (loss_sum + l_k, _jax.tree.map(_jnp.add, grad_sum, gw_k)), gx_k

        init = (_jnp.zeros((), _jnp.float32), _jax.tree.map(_jnp.zeros_like, weights))
        (loss, grad_w), grad_x = _jax.lax.scan(body, init, (per_example, given["loss_target"]))
    with _jax.named_scope("update"):
        delta_w, new_m, new_v = {}, {}, {}
        for n in TWIN_WEIGHTS:
            delta_w[n], new_m[n], new_v[n] = _adamw(weights[n], grad_w[n], given["m_" + n], given["v_" + n])
    return (loss, grad_x, *[grad_w[n] for n in TWIN_WEIGHTS], *[delta_w[n] for n in TWIN_WEIGHTS],
            *[new_m[n] for n in TWIN_WEIGHTS], *[new_v[n] for n in TWIN_WEIGHTS])
```

```python
import numpy as np
import jax
import jax.numpy as jnp
from jax import lax
from jax.experimental import pallas as pl
from jax.experimental.pallas import tpu as pltpu

F32 = jnp.float32
BF16 = jnp.bfloat16

D_MODEL = 1024
D_FF = 2816
FF_CHUNK = 1408
N_SHARD = 4
POOL_WINDOWS = (2, 4, 8, 16)
POOL_DIM = 256
POOL_HALO = 16
RET_HEADS = 4
RET_QK = 256
RET_V = 512
RET_CHUNK = 128
RET_IN = 6144
RET_IN_SHARD = 1536
ROPE_BASE = 10000.0
EPS = 1e-6
CONV_HALO = 8

ADAM_LR, ADAM_B1, ADAM_B2, ADAM_EPS, ADAM_WD, ADAM_STEP = 0.001, 0.9, 0.999, 1e-08, 0.01, 10

VMEM_LIMIT = 56 * 1024 * 1024
MESH = pl.DeviceIdType.MESH
ANY = pl.BlockSpec(memory_space=pl.ANY)


def _params(n_grid=1, limit=VMEM_LIMIT):
    return pltpu.CompilerParams(dimension_semantics=("arbitrary",) * n_grid, vmem_limit_bytes=limit)


def _dot(a, b):
    return jnp.dot(a, b, preferred_element_type=F32)


def _dot_nt(a, b):
    return lax.dot_general(a, b, (((1,), (1,)), ((), ())), preferred_element_type=F32)


def _dot_tn(a, b):
    return lax.dot_general(a, b, (((0,), (0,)), ((), ())), preferred_element_type=F32)


def _rms_fwd(x, gain):
    r = lax.rsqrt(jnp.mean(x * x, axis=-1, keepdims=True) + EPS)
    xh = x * r
    return xh * gain, xh, r


def _rms_bwd(xh, r, gain, dy):
    dxh = dy * gain
    return r * (dxh - xh * jnp.mean(dxh * xh, axis=-1, keepdims=True))


def _colsum(v):
    return jnp.sum(v, axis=0, keepdims=True)


def _full(shape):
    nd = len(shape)
    return pl.BlockSpec(shape, lambda *_: (0,) * nd)


def rope_tables(pos_col, inv_freq):
    t = pos_col.shape[0]
    tm = min(t, 1024)

    def body(p_ref, f_ref, c_ref, s_ref):
        ang = p_ref[...] * f_ref[...]
        c_ref[...] = jnp.cos(ang)
        s_ref[...] = jnp.sin(ang)

    return pl.pallas_call(
        body, grid=(t // tm,),
        in_specs=[pl.BlockSpec((tm, 1), lambda i: (i, 0)), _full((1, 128))],
        out_specs=[pl.BlockSpec((tm, 128), lambda i: (i, 0))] * 2,
        out_shape=[jax.ShapeDtypeStruct((t, 128), F32)] * 2,
        compiler_params=_params(1), name="rope_tables")(pos_col, inv_freq)


def _pool_diff(hext_ref, h, row0, tm):
    t_idx = row0 + lax.broadcasted_iota(jnp.int32, (tm, 1), 0)
    parts, inv_counts = [], []
    for g, win in enumerate(POOL_WINDOWS):
        cols = slice(g * POOL_DIM, (g + 1) * POOL_DIM)
        s = h[:, cols]
        for j in range(1, win):
            s = s + hext_ref[pl.ds(POOL_HALO - j, tm), cols]
        inv = 1.0 / jnp.minimum(t_idx + 1, win).astype(F32)
        parts.append(s * inv - h[:, cols])
        inv_counts.append(inv)
    return parts, inv_counts


def pool_fwd(x, g_pre, g_post, pool_w, pool_scale, tm=256):
    t, d = x.shape
    nt = t // tm

    def body(x_ref, g0_ref, g1_ref, w_ref, sc_ref, o_ref, hext):
        i = pl.program_id(0)

        @pl.when(i == 0)
        def _():
            hext[0:POOL_HALO, :] = jnp.zeros((POOL_HALO, d), F32)

        xv = x_ref[...]
        h, _, _ = _rms_fwd(xv, g0_ref[...])
        hext[POOL_HALO:POOL_HALO + tm, :] = h
        parts, _ = _pool_diff(hext, h, i * tm, tm)
        ys = [_dot(parts[g].astype(BF16), w_ref[g]) for g in range(len(POOL_WINDOWS))]
        y = jnp.concatenate(ys, axis=-1) * sc_ref[...]
        m, _, _ = _rms_fwd(y, g1_ref[...])
        o_ref[...] = xv + m
        hext[0:POOL_HALO, :] = hext[tm:tm + POOL_HALO, :]

    row = pl.BlockSpec((tm, d), lambda i: (i, 0))
    return pl.pallas_call(
        body, grid=(nt,),
        in_specs=[row, _full((1, d)), _full((1, d)), _full(pool_w.shape), _full((1, d))],
        out_specs=row, out_shape=jax.ShapeDtypeStruct((t, d), F32),
        scratch_shapes=[pltpu.VMEM((POOL_HALO + tm, d), F32)],
        compiler_params=_params(1), name="pool_fwd")(x, g_pre, g_post, pool_w, pool_scale)


def pool_bwd(dx1, x, g_pre, g_post, pool_w, pool_scale, tm=256):
    t, d = x.shape
    nt = t // tm
    ng = len(POOL_WINDOWS)

    def body(dx1_ref, x_ref, xh_ref, g0_ref, g1_ref, w_ref, sc_ref,
             dx_ref, dg0_ref, dg1_ref, dsc_ref, dw_ref, hext, eext):
        i = pl.program_id(0)
        r = nt - 1 - i

        @pl.when(i == 0)
        def _():
            eext[tm:tm + POOL_HALO, :] = jnp.zeros((POOL_HALO, d), F32)
            dg0_ref[...] = jnp.zeros_like(dg0_ref)
            dg1_ref[...] = jnp.zeros_like(dg1_ref)
            dsc_ref[...] = jnp.zeros_like(dsc_ref)
            dw_ref[...] = jnp.zeros_like(dw_ref)

        g0 = g0_ref[...]
        g1 = g1_ref[...]
        sc = sc_ref[...]
        xv = x_ref[...]
        h, xh, rx = _rms_fwd(xv, g0)
        h_halo, _, _ = _rms_fwd(xh_ref[...], g0)
        hext[0:POOL_HALO, :] = h_halo * jnp.where(r > 0, 1.0, 0.0)
        hext[POOL_HALO:POOL_HALO + tm, :] = h
        parts, inv_counts = _pool_diff(hext, h, r * tm, tm)
        parts_b = [p.astype(BF16) for p in parts]
        ypre = jnp.concatenate([_dot(parts_b[g], w_ref[g]) for g in range(ng)], axis=-1)
        _, yh, ry = _rms_fwd(ypre * sc, g1)
        dm = dx1_ref[...]
        dg1_ref[...] += _colsum(dm * yh)
        dy = _rms_bwd(yh, ry, g1, dm)
        dsc_ref[...] += _colsum(dy * ypre)
        dyp = (dy * sc).astype(BF16)
        ddiffs = []
        for g in range(ng):
            cols = slice(g * POOL_DIM, (g + 1) * POOL_DIM)
            dw_ref[g] += _dot_tn(parts_b[g], dyp[:, cols])
            dd = _dot_nt(dyp[:, cols], w_ref[g])
            ddiffs.append(dd)
            eext[0:tm, cols] = dd * inv_counts[g]
        dhs = []
        for g, win in enumerate(POOL_WINDOWS):
            cols = slice(g * POOL_DIM, (g + 1) * POOL_DIM)
            s = eext[0:tm, cols]
            for j in range(1, win):
                s = s + eext[pl.ds(j, tm), cols]
            dhs.append(s - ddiffs[g])
        dh = jnp.concatenate(dhs, axis=-1)
        eext[tm:tm + POOL_HALO, :] = eext[0:POOL_HALO, :]
        dg0_ref[...] += _colsum(dh * xh)
        dx_ref[...] = dm + _rms_bwd(xh, rx, g0, dh)

    row = pl.BlockSpec((tm, d), lambda i: (nt - 1 - i, 0))
    halo = pl.BlockSpec((POOL_HALO, d), lambda i: (jnp.maximum((nt - 1 - i) * (tm // POOL_HALO) - 1, 0), 0))
    vec = _full((1, d))
    return pl.pallas_call(
        body, grid=(nt,),
        in_specs=[row, row, halo, vec, vec, _full(pool_w.shape), vec],
        out_specs=[row, vec, vec, vec, _full((ng, POOL_DIM, POOL_DIM))],
        out_shape=[jax.ShapeDtypeStruct((t, d), F32)] + [jax.ShapeDtypeStruct((1, d), F32)] * 3
        + [jax.ShapeDtypeStruct((ng, POOL_DIM, POOL_DIM), F32)],
        scratch_shapes=[pltpu.VMEM((POOL_HALO + tm, d), F32), pltpu.VMEM((tm + POOL_HALO, d), F32)],
        compiler_params=_params(1), name="pool_bwd")(dx1, x, x, g_pre, g_post, pool_w, pool_scale)


def _conv_taps(cw_ref, j):
    return cw_ref[j, 0:1, :], cw_ref[j, 1:2, :], cw_ref[j, 2:3, :]


def _row_block(m, target=256):
    if m <= target:
        return m
    for b in range(target, 7, -8):
        if m % b == 0:
            return b
    return m


def mlp_fwd(x, g_pre, g_post, w_up, w_down, conv_w, conv_b, tm=256):
    t, d = x.shape
    nt = t // tm
    h8 = CONV_HALO

    def body(x_ref, g2_ref, g3_ref, wup_hbm, wdn_hbm, cw_ref, cb_ref,
             xo_ref, u_ref, f_ref, h_ref, wup_v, wdn_v, uext, sem):
        i = pl.program_id(0)

        @pl.when(i == 0)
        def _():
            c1 = pltpu.make_async_copy(wup_hbm, wup_v, sem.at[0])
            c2 = pltpu.make_async_copy(wdn_hbm, wdn_v, sem.at[1])
            c1.start()
            c2.start()
            uext[:, 0:h8, :] = jnp.zeros((N_SHARD, h8, FF_CHUNK), F32)
            c1.wait()
            c2.wait()

        xv = x_ref[...]
        h, _, _ = _rms_fwd(xv, g2_ref[...])
        hb = h.astype(BF16)
        h_ref[...] = hb
        acc = jnp.zeros((tm, d), F32)
        for k in range(2):
            cs = []
            for j in (k, 2 + k):
                ub = _dot(hb, wup_v[j]).astype(BF16)
                u_ref[:, j * FF_CHUNK:(j + 1) * FF_CHUNK] = ub
                uf = ub.astype(F32)
                uext[j, h8:h8 + tm, :] = uf
                w0, w1, w2 = _conv_taps(cw_ref, j)
                c = (cb_ref[j] + w2 * uf + w1 * uext[j, pl.ds(h8 - 1, tm), :]
                     + w0 * uext[j, pl.ds(h8 - 2, tm), :])
                uext[j, 0:h8, :] = uext[j, tm:tm + h8, :]
                cs.append(c)
            a = cs[0] * jax.nn.sigmoid(cs[0]) * cs[1]
            acc = acc + _dot(a.astype(BF16), wdn_v[k * FF_CHUNK:(k + 1) * FF_CHUNK, :])
        f_ref[...] = acc
        y, _, _ = _rms_fwd(acc, g3_ref[...])
        xo_ref[...] = xv + y

    row = pl.BlockSpec((tm, d), lambda i: (i, 0))
    vec = _full((1, d))
    return pl.pallas_call(
        body, grid=(nt,),
        in_specs=[row, vec, vec, ANY, ANY, _full(conv_w.shape), _full(conv_b.shape)],
        out_specs=[row, pl.BlockSpec((tm, 2 * D_FF), lambda i: (i, 0)), row, row],
        out_shape=[jax.ShapeDtypeStruct((t, d), F32), jax.ShapeDtypeStruct((t, 2 * D_FF), BF16),
                   jax.ShapeDtypeStruct((t, d), F32), jax.ShapeDtypeStruct((t, d), BF16)],
        scratch_shapes=[pltpu.VMEM(w_up.shape, BF16), pltpu.VMEM(w_down.shape, BF16),
                        pltpu.VMEM((N_SHARD, h8 + tm, FF_CHUNK), F32), pltpu.SemaphoreType.DMA((2,))],
        compiler_params=_params(1), name="mlp_fwd")(x, g_pre, g_post, w_up, w_down, conv_w, conv_b)


def mlp_bwd(dxo, f, x, u, g_pre, g_post, w_up, w_down, conv_w, conv_b, tm=128):
    t, d = x.shape
    nt = t // tm
    h8 = CONV_HALO

    def body(dxo_ref, f_ref, x_ref, u_ref, up_ref, g2_ref, g3_ref, wup_hbm, wdn_hbm, cw_ref, cb_ref,
             dx_ref, du_ref, a_ref, df_ref, dg2_ref, dg3_ref, dcw_ref, dcb_ref,
             wup_v, wdn_v, uext, dcext, sem):
        i = pl.program_id(0)
        r = nt - 1 - i

        @pl.when(i == 0)
        def _():
            c1 = pltpu.make_async_copy(wup_hbm, wup_v, sem.at[0])
            c2 = pltpu.make_async_copy(wdn_hbm, wdn_v, sem.at[1])
            c1.start()
            c2.start()
            dcext[:, tm:tm + h8, :] = jnp.zeros((N_SHARD, h8, FF_CHUNK), F32)
            dg2_ref[...] = jnp.zeros_like(dg2_ref)
            dg3_ref[...] = jnp.zeros_like(dg3_ref)
            dcw_ref[...] = jnp.zeros_like(dcw_ref)
            dcb_ref[...] = jnp.zeros_like(dcb_ref)
            c1.wait()
            c2.wait()

        g2 = g2_ref[...]
        g3 = g3_ref[...]
        dxo = dxo_ref[...]
        _, fh, rf = _rms_fwd(f_ref[...], g3)
        dg3_ref[...] += _colsum(dxo * fh)
        dfb = _rms_bwd(fh, rf, g3, dxo).astype(BF16)
        df_ref[...] = dfb
        not_first = jnp.where(r > 0, 1.0, 0.0)
        dh = jnp.zeros((tm, d), F32)
        for k in range(2):
            cs = []
            for s, j in enumerate((k, 2 + k)):
                cols = slice(j * FF_CHUNK, (j + 1) * FF_CHUNK)
                uext[s, 0:h8, :] = up_ref[:, cols].astype(F32) * not_first
                uext[s, h8:h8 + tm, :] = u_ref[:, cols].astype(F32)
                w0, w1, w2 = _conv_taps(cw_ref, j)
                cs.append(cb_ref[j] + w2 * uext[s, pl.ds(h8, tm), :] + w1 * uext[s, pl.ds(h8 - 1, tm), :]
                          + w0 * uext[s, pl.ds(h8 - 2, tm), :])
            cg, cv = cs
            sg = jax.nn.sigmoid(cg)
            sil = cg * sg
            a_ref[:, k * FF_CHUNK:(k + 1) * FF_CHUNK] = (sil * cv).astype(BF16)
            da = _dot_nt(dfb, wdn_v[k * FF_CHUNK:(k + 1) * FF_CHUNK, :])
            dcs = (da * cv * (sg * (1.0 + cg * (1.0 - sg))), da * sil)
            for s, j in enumerate((k, 2 + k)):
                dc = dcs[s]
                w0, w1, w2 = _conv_taps(cw_ref, j)
                dcb_ref[j] += _colsum(dc)
                dcw_ref[j, 2:3, :] += _colsum(dc * uext[s, pl.ds(h8, tm), :])
                dcw_ref[j, 1:2, :] += _colsum(dc * uext[s, pl.ds(h8 - 1, tm), :])
                dcw_ref[j, 0:1, :] += _colsum(dc * uext[s, pl.ds(h8 - 2, tm), :])
                dcext[j, 0:tm, :] = dc
                dub = (w2 * dc + w1 * dcext[j, pl.ds(1, tm), :] + w0 * dcext[j, pl.ds(2, tm), :]).astype(BF16)
                dcext[j, tm:tm + h8, :] = dcext[j, 0:h8, :]
                du_ref[:, j * FF_CHUNK:(j + 1) * FF_CHUNK] = dub
                dh = dh + _dot_nt(dub, wup_v[j])
        _, xh, rx = _rms_fwd(x_ref[...], g2)
        dg2_ref[...] += _colsum(dh * xh)
        dx_ref[...] = dxo + _rms_bwd(xh, rx, g2, dh)

    row = pl.BlockSpec((tm, d), lambda i: (nt - 1 - i, 0))
    urow = pl.BlockSpec((tm, 2 * D_FF), lambda i: (nt - 1 - i, 0))
    uprev = pl.BlockSpec((h8, 2 * D_FF), lambda i: (jnp.maximum((nt - 1 - i) * (tm // h8) - 1, 0), 0))
    vec = _full((1, d))
    return pl.pallas_call(
        body, grid=(nt,),
        in_specs=[row, row, row, urow, uprev, vec, vec, ANY, ANY, _full(conv_w.shape), _full(conv_b.shape)],
        out_specs=[row, urow, pl.BlockSpec((tm, D_FF), lambda i: (nt - 1 - i, 0)), row, vec, vec,
                   _full(conv_w.shape), _full(conv_b.shape)],
        out_shape=[jax.ShapeDtypeStruct((t, d), F32), jax.ShapeDtypeStruct((t, 2 * D_FF), BF16),
                   jax.ShapeDtypeStruct((t, D_FF), BF16), jax.ShapeDtypeStruct((t, d), BF16),
                   jax.ShapeDtypeStruct((1, d), F32), jax.ShapeDtypeStruct((1, d), F32),
                   jax.ShapeDtypeStruct(conv_w.shape, F32), jax.ShapeDtypeStruct(conv_b.shape, F32)],
        scratch_shapes=[pltpu.VMEM(w_up.shape, BF16), pltpu.VMEM(w_down.shape, BF16),
                        pltpu.VMEM((2, h8 + tm, FF_CHUNK), F32), pltpu.VMEM((N_SHARD, tm + h8, FF_CHUNK), F32),
                        pltpu.SemaphoreType.DMA((2,))],
        compiler_params=_params(1), name="mlp_bwd")(dxo, f, x, u, u, g_pre, g_post, w_up, w_down, conv_w, conv_b)


def grad_matmul(a, b, bm, bn, name, tk=512):
    t, m = a.shape
    n = b.shape[1]
    nk = t // tk

    def body(a_ref, b_ref, o_ref, ob_ref):
        kk = pl.program_id(2)

        @pl.when(kk == 0)
        def _():
            o_ref[...] = jnp.zeros_like(o_ref)

        o_ref[...] += _dot_tn(a_ref[...], b_ref[...])

        @pl.when(kk == nk - 1)
        def _():
            ob_ref[...] = o_ref[...].astype(BF16)

    ospec = pl.BlockSpec((None, bm, bn), lambda j, i, kk: (j, i, 0))
    return pl.pallas_call(
        body, grid=(n // bn, m // bm, nk),
        in_specs=[pl.BlockSpec((tk, bm), lambda j, i, kk: (kk, i)), pl.BlockSpec((tk, bn), lambda j, i, kk: (kk, j))],
        out_specs=[ospec, ospec],
        out_shape=[jax.ShapeDtypeStruct((n // bn, m, bn), F32), jax.ShapeDtypeStruct((n // bn, m, bn), BF16)],
        compiler_params=_params(3), name=name)(a, b)


def _decay_tables():
    log_gamma = jnp.log(1.0 - 2.0 ** (-5.0 - jnp.arange(RET_HEADS, dtype=F32)))
    i = jnp.arange(RET_CHUNK, dtype=F32)
    rel = i[:, None] - i[None, :]
    intra = jnp.where(rel >= 0, jnp.exp(jnp.maximum(rel, 0.0) * log_gamma[:, None, None]), 0.0)
    cross = jnp.exp((i + 1.0) * log_gamma[:, None])[:, :, None]
    inner = jnp.exp((RET_CHUNK - 1.0 - i) * log_gamma[:, None])[:, :, None]
    chunk = [float(np.exp(np.float32(RET_CHUNK) * np.log(np.float32(1.0 - 2.0 ** (-5.0 - h))).astype(np.float32)))
             for h in range(RET_HEADS)]
    return intra, cross, inner, chunk


def ret_proj(x, g_pre, w_in, cos, sin, tm=256):
    t, d = x.shape
    nt = t // tm
    per = RET_IN_SHARD // RET_QK

    def body(x_ref, g_ref, win_hbm, c_ref, s_ref, pj_ref, h_ref, win_v, sem):
        @pl.when(pl.program_id(0) == 0)
        def _():
            cp = pltpu.make_async_copy(win_hbm, win_v, sem)
            cp.start()
            cp.wait()

        h, _, _ = _rms_fwd(x_ref[...], g_ref[...])
        hb = h.astype(BF16)
        h_ref[...] = hb
        c = c_ref[...]
        s = s_ref[...]
        for j in range(N_SHARD):
            pjj = _dot(hb, win_v[j])
            for bb in range(per):
                b = per * j + bb
                blk = pjj[:, bb * RET_QK:(bb + 1) * RET_QK]
                if b < 2 * RET_HEADS:
                    x1, x2 = blk[:, :128], blk[:, 128:]
                    o1 = x1 * c - x2 * s
                    o2 = x2 * c + x1 * s
                    if b < RET_HEADS:
                        o1 = o1 * (RET_QK ** -0.5)
                        o2 = o2 * (RET_QK ** -0.5)
                    pj_ref[:, b * RET_QK:b * RET_QK + 128] = o1.astype(BF16)
                    pj_ref[:, b * RET_QK + 128:(b + 1) * RET_QK] = o2.astype(BF16)
                else:
                    pj_ref[:, b * RET_QK:(b + 1) * RET_QK] = blk.astype(BF16)

    row = pl.BlockSpec((tm, d), lambda i: (i, 0))
    tab = pl.BlockSpec((tm, 128), lambda i: (i, 0))
    return pl.pallas_call(
        body, grid=(nt,),
        in_specs=[row, _full((1, d)), ANY, tab, tab],
        out_specs=[pl.BlockSpec((tm, RET_IN), lambda i: (i, 0)), row],
        out_shape=[jax.ShapeDtypeStruct((t, RET_IN), BF16), jax.ShapeDtypeStruct((t, d), BF16)],
        scratch_shapes=[pltpu.VMEM(w_in.shape, BF16), pltpu.SemaphoreType.DMA],
        compiler_params=_params(1), name="ret_proj")(x, g_pre, w_in, cos, sin)


def ret_core_fwd(pj, intra, cross, inner, chunk_decay):
    t = pj.shape[0]
    nc = t // RET_CHUNK
    c = RET_CHUNK
    qk_all = RET_HEADS * RET_QK
    v_all = RET_HEADS * RET_V

    def body(q_ref, k_ref, v_ref, dm_ref, cr_ref, in_ref, o_ref, sp_ref, state):
        @pl.when(pl.program_id(0) == 0)
        def _():
            state[...] = jnp.zeros_like(state)

        for h in range(RET_HEADS):
            q = q_ref[:, h * RET_QK:(h + 1) * RET_QK]
            k = k_ref[:, h * RET_QK:(h + 1) * RET_QK]
            v = v_ref[:, h * RET_V:(h + 1) * RET_V]
            sb = state[h].astype(BF16)
            sp_ref[h] = sb
            sc = _dot_nt(q, k) * dm_ref[h]
            o_ref[:, h * RET_V:(h + 1) * RET_V] = _dot(sc.astype(BF16), v) + _dot(q, sb) * cr_ref[h]
            kd = (k.astype(F32) * in_ref[h]).astype(BF16)
            state[h] = state[h] * chunk_decay[h] + _dot_tn(kd, v)

    return pl.pallas_call(
        body, grid=(nc,),
        in_specs=[pl.BlockSpec((c, qk_all), lambda n: (n, 0)), pl.BlockSpec((c, qk_all), lambda n: (n, 1)),
                  pl.BlockSpec((c, v_all), lambda n: (n, 1)),
                  _full(intra.shape), _full(cross.shape), _full(inner.shape)],
        out_specs=[pl.BlockSpec((c, v_all), lambda n: (n, 0)),
                   pl.BlockSpec((None, RET_HEADS, RET_QK, RET_V), lambda n: (n, 0, 0, 0))],
        out_shape=[jax.ShapeDtypeStruct((t, v_all), F32),
                   jax.ShapeDtypeStruct((nc, RET_HEADS, RET_QK, RET_V), BF16)],
        scratch_shapes=[pltpu.VMEM((RET_HEADS, RET_QK, RET_V), F32)],
        compiler_params=_params(1), name="ret_core_fwd")(pj, pj, pj, intra, cross, inner)


def _group_norm(o_h):
    mu = jnp.mean(o_h, axis=-1, keepdims=True)
    dev = o_h - mu
    rstd = lax.rsqrt(jnp.mean(dev * dev, axis=-1, keepdims=True) + EPS)
    return dev * rstd, rstd


def ret_out_fwd(o, pj, x, gn_gain, g_post, w_out, tm=256):
    t, d = x.shape
    nt = t // tm
    v_all = RET_HEADS * RET_V

    def body(o_ref, g_ref, x_ref, gn_ref, g1_ref, w_ref, xo_ref, y_ref, out_ref):
        for h in range(RET_HEADS):
            cols = slice(h * RET_V, (h + 1) * RET_V)
            ohat, _ = _group_norm(o_ref[:, cols])
            g = g_ref[:, cols].astype(F32)
            y_ref[:, cols] = (g * jax.nn.sigmoid(g) * (ohat * gn_ref[:, cols])).astype(BF16)
        out = _dot(y_ref[...], w_ref[...])
        out_ref[...] = out
        m, _, _ = _rms_fwd(out, g1_ref[...])
        xo_ref[...] = x_ref[...] + m

    row = pl.BlockSpec((tm, d), lambda i: (i, 0))
    wide = pl.BlockSpec((tm, v_all), lambda i: (i, 0))
    return pl.pallas_call(
        body, grid=(nt,),
        in_specs=[wide, pl.BlockSpec((tm, v_all), lambda i: (i, 2)), row, _full((1, v_all)), _full((1, d)),
                  _full(w_out.shape)],
        out_specs=[row, wide, row],
        out_shape=[jax.ShapeDtypeStruct((t, d), F32), jax.ShapeDtypeStruct((t, v_all), BF16),
                   jax.ShapeDtypeStruct((t, d), F32)],
        compiler_params=_params(1), name="ret_out_fwd")(o, pj, x, gn_gain, g_post, w_out)


def ret_out_bwd(dxo, out, o, pj, gn_gain, g_post, w_out, tm=256):
    t, d = out.shape
    nt = t // tm
    v_all = RET_HEADS * RET_V

    def body(dxo_ref, out_ref, o_ref, g_ref, gn_ref, g1_ref, w_ref,
             dout_ref, dgate_ref, do_ref, dg1_ref, dgn_ref):
        @pl.when(pl.program_id(0) == 0)
        def _():
            dg1_ref[...] = jnp.zeros_like(dg1_ref)
            dgn_ref[...] = jnp.zeros_like(dgn_ref)

        g1 = g1_ref[...]
        dxo = dxo_ref[...]
        _, oh_, r_ = _rms_fwd(out_ref[...], g1)
        dg1_ref[...] += _colsum(dxo * oh_)
        doutb = _rms_bwd(oh_, r_, g1, dxo).astype(BF16)
        dout_ref[...] = doutb
        dy = _dot_nt(doutb, w_ref[...])
        for h in range(RET_HEADS):
            cols = slice(h * RET_V, (h + 1) * RET_V)
            gn = gn_ref[:, cols]
            ohat, rstd = _group_norm(o_ref[:, cols])
            g = g_ref[:, cols].astype(F32)
            sg = jax.nn.sigmoid(g)
            dyh = dy[:, cols]
            dgate_ref[:, cols] = (dyh * (ohat * gn) * (sg * (1.0 + g * (1.0 - sg)))).astype(BF16)
            don = dyh * (g * sg)
            dgn_ref[:, cols] += _colsum(don * ohat)
            dohat = don * gn
            do_ref[:, cols] = (rstd * (dohat - jnp.mean(dohat, axis=-1, keepdims=True)
                                       - ohat * jnp.mean(dohat * ohat, axis=-1, keepdims=True))).astype(BF16)

    row = pl.BlockSpec((tm, d), lambda i: (i, 0))
    wide = pl.BlockSpec((tm, v_all), lambda i: (i, 0))
    gate = pl.BlockSpec((tm, v_all), lambda i: (i, 2))
    return pl.pallas_call(
        body, grid=(nt,),
        in_specs=[row, row, wide, gate, _full((1, v_all)), _full((1, d)), _full(w_out.shape)],
        out_specs=[row, gate, wide, _full((1, d)), _full((1, v_all))],
        out_shape=[jax.ShapeDtypeStruct((t, d), BF16), jax.ShapeDtypeStruct((t, RET_IN), BF16),
                   jax.ShapeDtypeStruct((t, v_all), BF16), jax.ShapeDtypeStruct((1, d), F32),
                   jax.ShapeDtypeStruct((1, v_all), F32)],
        compiler_params=_params(1), name="ret_out_bwd")(dxo, out, o, pj, gn_gain, g_post, w_out)


def ret_core_bwd(pj, do, sprev, cos, sin, dpj, intra, cross, inner, chunk_decay):
    t = pj.shape[0]
    nc = t // RET_CHUNK
    c = RET_CHUNK
    qk_all = RET_HEADS * RET_QK
    v_all = RET_HEADS * RET_V
    scale = RET_QK ** -0.5

    def body(q_ref, k_ref, v_ref, do_ref, sp_ref, c_ref, s_ref, dm_ref, cr_ref, in_ref, dpj_in, dpj_ref, dstate):
        @pl.when(pl.program_id(0) == 0)
        def _():
            dstate[...] = jnp.zeros_like(dstate)

        cs = c_ref[...]
        sn = s_ref[...]
        for h in range(RET_HEADS):
            q = q_ref[:, h * RET_QK:(h + 1) * RET_QK]
            k = k_ref[:, h * RET_QK:(h + 1) * RET_QK]
            v = v_ref[:, h * RET_V:(h + 1) * RET_V]
            doh = do_ref[:, h * RET_V:(h + 1) * RET_V]
            dm = dm_ref[h]
            ab = (_dot_nt(q, k) * dm).astype(BF16)
            dab = (_dot_nt(doh, v) * dm).astype(BF16)
            dsb = dstate[h].astype(BF16)
            kd = (k.astype(F32) * in_ref[h]).astype(BF16)
            dv = _dot_tn(ab, doh) + _dot(kd, dsb)
            dq = _dot(dab, k) + cr_ref[h] * _dot_nt(doh, sp_ref[h])
            dk = _dot_tn(dab, q) + in_ref[h] * _dot_nt(v, dsb)
            qd = (q.astype(F32) * cr_ref[h]).astype(BF16)
            dstate[h] = dstate[h] * chunk_decay[h] + _dot_tn(qd, doh)
            for base, dd, sc in ((h * RET_QK, dq, scale), (qk_all + h * RET_QK, dk, 1.0)):
                d1, d2 = dd[:, :128], dd[:, 128:]
                dpj_ref[:, base:base + 128] = ((d1 * cs + d2 * sn) * sc).astype(BF16)
                dpj_ref[:, base + 128:base + RET_QK] = ((d2 * cs - d1 * sn) * sc).astype(BF16)
            dpj_ref[:, 2 * qk_all + h * RET_V:2 * qk_all + (h + 1) * RET_V] = dv.astype(BF16)

    rev = lambda n: nc - 1 - n
    tab = pl.BlockSpec((c, 128), lambda n: (rev(n), 0))
    return pl.pallas_call(
        body, grid=(nc,),
        in_specs=[pl.BlockSpec((c, qk_all), lambda n: (rev(n), 0)), pl.BlockSpec((c, qk_all), lambda n: (rev(n), 1)),
                  pl.BlockSpec((c, v_all), lambda n: (rev(n), 1)), pl.BlockSpec((c, v_all), lambda n: (rev(n), 0)),
                  pl.BlockSpec((None, RET_HEADS, RET_QK, RET_V), lambda n: (rev(n), 0, 0, 0)),
                  tab, tab, _full(intra.shape), _full(cross.shape), _full(inner.shape), ANY],
        out_specs=pl.BlockSpec((c, 2 * qk_all + v_all), lambda n: (rev(n), 0)),
        out_shape=jax.ShapeDtypeStruct((t, RET_IN), BF16),
        scratch_shapes=[pltpu.VMEM((RET_HEADS, RET_QK, RET_V), F32)],
        input_output_aliases={10: 0},
        compiler_params=_params(1), name="ret_core_bwd")(pj, pj, pj, do, sprev, cos, sin, intra, cross, inner, dpj)


def ret_in_bwd(dpj, dres, x, g_pre, w_in, tm=256):
    t, d = x.shape
    nt = t // tm

    def body(dpj_ref, dres_ref, x_ref, g_ref, win_hbm, dx_ref, dg_ref, win_v, sem):
        @pl.when(pl.program_id(0) == 0)
        def _():
            cp = pltpu.make_async_copy(win_hbm, win_v, sem)
            cp.start()
            dg_ref[...] = jnp.zeros_like(dg_ref)
            cp.wait()

        g = g_ref[...]
        dh = jnp.zeros((tm, d), F32)
        for j in range(N_SHARD):
            dh = dh + _dot_nt(dpj_ref[:, j * RET_IN_SHARD:(j + 1) * RET_IN_SHARD], win_v[j])
        _, xh, rx = _rms_fwd(x_ref[...], g)
        dg_ref[...] += _colsum(dh * xh)
        dx_ref[...] = dres_ref[...] + _rms_bwd(xh, rx, g, dh)

    row = pl.BlockSpec((tm, d), lambda i: (i, 0))
    return pl.pallas_call(
        body, grid=(nt,),
        in_specs=[pl.BlockSpec((tm, RET_IN), lambda i: (i, 0)), row, row, _full((1, d)), ANY],
        out_specs=[row, _full((1, d))],
        out_shape=[jax.ShapeDtypeStruct((t, d), F32), jax.ShapeDtypeStruct((1, d), F32)],
        scratch_shapes=[pltpu.VMEM(w_in.shape, BF16), pltpu.SemaphoreType.DMA],
        compiler_params=_params(1), name="ret_in_bwd")(dpj, dres, x, g_pre, w_in)


def loss_head(y, target, tm=512):
    t, d = y.shape

    def body(y_ref, t_ref, l_ref, dy_ref):
        @pl.when(pl.program_id(0) == 0)
        def _():
            l_ref[...] = jnp.zeros_like(l_ref)

        err = y_ref[...] - t_ref[...]
        dy_ref[...] = err * (1.0 / d)
        l_ref[...] += 0.5 * jnp.sum(jnp.mean(err * err, axis=-1, keepdims=True), axis=0, keepdims=True)

    row = pl.BlockSpec((tm, d), lambda i: (i, 0))
    return pl.pallas_call(
        body, grid=(t // tm,), in_specs=[row, row], out_specs=[_full((1, 1)), row],
        out_shape=[jax.ShapeDtypeStruct((1, 1), F32), jax.ShapeDtypeStruct((t, d), F32)],
        compiler_params=_params(1), name="loss_head")(y, target)


_CHIP_FLIPS = ((1, 0), (0, 1), (1, 1))


def _flip(v, b):
    return 1 - v if b else v


def gather_weights(shards):
    n = len(shards)

    def body(*refs):
        ins, outs = refs[:n], refs[n:2 * n]
        send_sems, recv_sems, local_sems = refs[2 * n:]
        x, y, c = lax.axis_index("x"), lax.axis_index("y"), lax.axis_index("c")
        me = 2 * x + y
        copies = []
        for t in range(n):
            lc = pltpu.make_async_copy(ins[t], outs[t].at[me], local_sems.at[t])
            lc.start()
            copies.append(lc)
            for k, (bx, by) in enumerate(_CHIP_FLIPS):
                cp = pltpu.make_async_remote_copy(
                    src_ref=ins[t], dst_ref=outs[t].at[me], send_sem=send_sems.at[3 * t + k],
                    recv_sem=recv_sems.at[3 * t + k], device_id=(_flip(x, bx), _flip(y, by), c), device_id_type=MESH)
                cp.start()
                copies.append(cp)
        for cp in copies:
            cp.wait()

    return pl.pallas_call(
        body, in_specs=[ANY] * n, out_specs=[ANY] * n,
        out_shape=[jax.ShapeDtypeStruct((N_SHARD,) + s.shape, s.dtype) for s in shards],
        scratch_shapes=[pltpu.SemaphoreType.DMA((3 * n,)), pltpu.SemaphoreType.DMA((3 * n,)),
                        pltpu.SemaphoreType.DMA((n,))],
        name="gather_weights")(*shards)


def scatter_grads(big, small):
    n = len(big)

    def body(*refs):
        ins, small_in = refs[:n], refs[n]
        outs, small_out = refs[n + 1:2 * n + 1], refs[2 * n + 1]
        send_sems, recv_sems, ssend_sems, srecv_sems, local_sem = refs[2 * n + 2:]
        x, y, c = lax.axis_index("x"), lax.axis_index("y"), lax.axis_index("c")
        mine = 4 * x + 2 * y + c
        copies = [pltpu.make_async_copy(small_in, small_out.at[mine], local_sem)]
        copies[0].start()
        for m in range(1, 8):
            bx, by, bc = (m >> 2) & 1, (m >> 1) & 1, m & 1
            cp = pltpu.make_async_remote_copy(
                src_ref=small_in, dst_ref=small_out.at[mine], send_sem=ssend_sems.at[m - 1],
                recv_sem=srecv_sems.at[m - 1], device_id=(_flip(x, bx), _flip(y, by), _flip(c, bc)),
                device_id_type=MESH)
            cp.start()
            copies.append(cp)
        for t in range(n):
            for k, (bx, by) in enumerate(_CHIP_FLIPS):
                px, py = _flip(x, bx), _flip(y, by)
                cp = pltpu.make_async_remote_copy(
                    src_ref=ins[t].at[2 * px + py], dst_ref=outs[t].at[k], send_sem=send_sems.at[3 * t + k],
                    recv_sem=recv_sems.at[3 * t + k], device_id=(px, py, c), device_id_type=MESH)
                cp.start()
                copies.append(cp)
        for cp in copies:
            cp.wait()

    return pl.pallas_call(
        body, in_specs=[ANY] * (n + 1), out_specs=[ANY] * (n + 1),
        out_shape=[jax.ShapeDtypeStruct((3,) + b.shape[1:], b.dtype) for b in big]
        + [jax.ShapeDtypeStruct((8,) + small.shape, small.dtype)],
        scratch_shapes=[pltpu.SemaphoreType.DMA((3 * n,)), pltpu.SemaphoreType.DMA((3 * n,)),
                        pltpu.SemaphoreType.DMA((7,)), pltpu.SemaphoreType.DMA((7,)), pltpu.SemaphoreType.DMA],
        name="scatter_grads")(*big, small)


def swap_cores(arrays):
    n = len(arrays)

    def body(*refs):
        ins, outs = refs[:n], refs[n:2 * n]
        send_sems, recv_sems = refs[2 * n:]
        sibling = (lax.axis_index("x"), lax.axis_index("y"), 1 - lax.axis_index("c"))
        copies = []
        for t in range(n):
            cp = pltpu.make_async_remote_copy(
                src_ref=ins[t], dst_ref=outs[t], send_sem=send_sems.at[t], recv_sem=recv_sems.at[t],
                device_id=sibling, device_id_type=MESH)
            cp.start()
            copies.append(cp)
        for cp in copies:
            cp.wait()

    return pl.pallas_call(
        body, in_specs=[ANY] * n, out_specs=[ANY] * n,
        out_shape=[jax.ShapeDtypeStruct(a.shape, a.dtype) for a in arrays],
        scratch_shapes=[pltpu.SemaphoreType.DMA((n,)), pltpu.SemaphoreType.DMA((n,))],
        name="swap_cores")(*arrays)


def plane_sum(own, recv, name, bm=256):
    m, n = own.shape
    bm = _row_block(m, bm)

    def body(o_ref, r_ref, s_ref):
        s_ref[...] = ((o_ref[...] + r_ref[0].astype(F32)) + r_ref[1].astype(F32)) + r_ref[2].astype(F32)

    return pl.pallas_call(
        body, grid=(m // bm,),
        in_specs=[pl.BlockSpec((bm, n), lambda i: (i, 0)), pl.BlockSpec((3, bm, n), lambda i: (0, i, 0))],
        out_specs=pl.BlockSpec((bm, n), lambda i: (i, 0)), out_shape=jax.ShapeDtypeStruct((m, n), F32),
        compiler_params=_params(1), name=name)(own, recv)


def sum_slots(parts, name, bm=312):
    _, r, n = parts.shape
    bm = bm if r % bm == 0 else r

    def body(p_ref, s_ref):
        acc = p_ref[0]
        for k in range(1, 8):
            acc = acc + p_ref[k]
        s_ref[...] = acc

    return pl.pallas_call(
        body, grid=(r // bm,), in_specs=[pl.BlockSpec((8, bm, n), lambda i: (0, i, 0))],
        out_specs=pl.BlockSpec((bm, n), lambda i: (i, 0)), out_shape=jax.ShapeDtypeStruct((r, n), F32),
        compiler_params=_params(1), name=name)(parts)


def _adamw_math(w, g, m, v):
    m = ADAM_B1 * m + (1.0 - ADAM_B1) * g
    v = ADAM_B2 * v + (1.0 - ADAM_B2) * (g * g)
    m_hat = m / (1.0 - ADAM_B1 ** ADAM_STEP)
    v_hat = v / (1.0 - ADAM_B2 ** ADAM_STEP)
    delta = -ADAM_LR * (m_hat / (jnp.sqrt(v_hat) + ADAM_EPS) + ADAM_WD * w)
    return delta, m, v


def adamw(w, m, v, grads, layer, prev, name, bm=256):
    _, mm, n = w.shape
    bm = _row_block(mm, bm)
    ng = len(grads)

    def body(*refs):
        w_ref, m_ref, v_ref = refs[:3]
        g_refs = refs[3:3 + ng]
        g_out, d_out, m_out, v_out = refs[-4:]
        g = g_refs[0][...]
        for gr in g_refs[1:]:
            g = g + gr[...]
        delta, mn, vn = _adamw_math(w_ref[...], g, m_ref[...], v_ref[...])
        g_out[...] = g
        d_out[...] = delta
        m_out[...] = mn
        v_out[...] = vn

    slab = pl.BlockSpec((None, bm, n), lambda i: (layer, i, 0))
    flat = pl.BlockSpec((bm, n), lambda i: (i, 0))
    in_specs = [slab] * 3 + [flat] * ng
    args = [w, m, v, *grads]
    aliases = {}
    if prev is not None:
        in_specs += [ANY] * 4
        aliases = {3 + ng + q: q for q in range(4)}
        args += list(prev)
    return pl.pallas_call(
        body, grid=(mm // bm,), in_specs=in_specs, out_specs=[slab] * 4,
        out_shape=[jax.ShapeDtypeStruct(w.shape, F32)] * 4, input_output_aliases=aliases,
        compiler_params=_params(1), name=name)(*args)


def _adamw_all_layers(w, m, v, grads_per_layer, name):
    prev = None
    for layer, grads in enumerate(grads_per_layer):
        prev = adamw(w, m, v, grads, layer, prev, f"{name}_{layer}")
    return prev


def _pack_rows(parts, rows):
    flat = jnp.concatenate([p.reshape(-1) for p in parts])
    return jnp.pad(flat, (0, rows * 128 - flat.shape[0])).reshape(rows, 128)


def _local_step(x, pos_col, target, gains, pool_w, pool_scale, w_in, gn_gain, w_out, w_up, w_down, conv_w, conv_b):
    gain = lambda l, n: gains[l, n].reshape(1, D_MODEL)
    inv_freq = (ROPE_BASE ** (-jnp.arange(0, RET_QK, 2, dtype=F32) / RET_QK)).reshape(1, RET_QK // 2)
    cos, sin = rope_tables(pos_col, inv_freq)
    intra, cross, inner, chunk_decay = _decay_tables()

    x1 = pool_fwd(x, gain(0, 0), gain(0, 1), pool_w, pool_scale)
    x2, u0, f0, h0 = mlp_fwd(x1, gain(0, 2), gain(0, 3), w_up[0], w_down[0], conv_w[0], conv_b[0])
    pj, hr = ret_proj(x2, gain(1, 0), w_in, cos, sin)
    o, sprev = ret_core_fwd(pj, intra, cross, inner, chunk_decay)
    x3, yb, out = ret_out_fwd(o, pj, x2, gn_gain, gain(1, 1), w_out)
    x4, u1, f1, h1 = mlp_fwd(x3, gain(1, 2), gain(1, 3), w_up[1], w_down[1], conv_w[1], conv_b[1])
    loss, dx4 = loss_head(x4, target)

    dx3, du1, a1, df1, dg12, dg13, dcw1, dcb1 = mlp_bwd(
        dx4, f1, x3, u1, gain(1, 2), gain(1, 3), w_up[1], w_down[1], conv_w[1], conv_b[1])
    dwup1 = grad_matmul(h1, du1, D_MODEL, FF_CHUNK, "grad_w_up_1")
    dwdn1 = grad_matmul(a1, df1, FF_CHUNK, D_MODEL, "grad_w_down_1")
    dout, dpj, do, dg11, dgn = ret_out_bwd(dx3, out, o, pj, gn_gain, gain(1, 1), w_out)
    dwout = grad_matmul(yb, dout, 1024, D_MODEL, "grad_w_out")
    dpj = ret_core_bwd(pj, do, sprev, cos, sin, dpj, intra, cross, inner, chunk_decay)
    dwin = grad_matmul(hr, dpj, D_MODEL, RET_IN_SHARD, "grad_w_in")
    dx2, dg10 = ret_in_bwd(dpj, dx3, x2, gain(1, 0), w_in)
    dx1, du0, a0, df0, dg02, dg03, dcw0, dcb0 = mlp_bwd(
        dx2, f0, x1, u0, gain(0, 2), gain(0, 3), w_up[0], w_down[0], conv_w[0], conv_b[0])
    dwup0 = grad_matmul(h0, du0, D_MODEL, FF_CHUNK, "grad_w_up_0")
    dwdn0 = grad_matmul(a0, df0, FF_CHUNK, D_MODEL, "grad_w_down_0")
    dx0, dg00, dg01, dpscale, dpw = pool_bwd(dx1, x, gain(0, 0), gain(0, 1), pool_w, pool_scale)

    dgains = jnp.concatenate([dg00, dg01, dg02, dg03, dg10, dg11, dg12, dg13], axis=0).reshape(2, 4, D_MODEL)
    big = {"w_up": (dwup0, dwup1), "w_down": (dwdn0, dwdn1), "w_in": dwin, "w_out": dwout}
    small = {"gains": dgains, "pool_scale": dpscale, "gn": dgn, "conv_w": jnp.stack([dcw0, dcw1]),
             "conv_b": jnp.stack([dcb0, dcb1]), "pool_w": dpw}
    return loss, dx0, big, small


def kernel(x, positions, norm_gain, pool_w, pool_scale, ret_w_in, ret_gn_gain, ret_w_out, mlp_w_up, mlp_conv_w, mlp_conv_b, mlp_w_down, loss_target, m_norm_gain, m_pool_w, m_pool_scale, m_ret_w_in, m_ret_gn_gain, m_ret_w_out, m_mlp_w_up, m_mlp_conv_w, m_mlp_conv_b, m_mlp_w_down, v_norm_gain, v_pool_w, v_pool_scale, v_ret_w_in, v_ret_gn_gain, v_ret_w_out, v_mlp_w_up, v_mlp_conv_w, v_mlp_conv_b, v_mlp_w_down):
    t = x.shape[1]
    me = 2 * lax.axis_index("x") + lax.axis_index("y")

    small_parts = [norm_gain, ret_gn_gain, mlp_conv_w, pool_w]
    small_sizes = [p.size for p in small_parts]
    small_rows = -(-sum(small_sizes) // (128 * 8)) * 8
    shards = [mlp_w_up[0].astype(BF16), mlp_w_up[1].astype(BF16), mlp_w_down[0].astype(BF16),
              mlp_w_down[1].astype(BF16), ret_w_in[0].astype(BF16), ret_w_out[0].astype(BF16),
              _pack_rows(small_parts, small_rows)]
    wup0, wup1, wdn0, wdn1, win, wout, smallg = gather_weights(shards)
    smallg = smallg.reshape(N_SHARD, -1)
    offs = np.cumsum([0] + small_sizes)
    piece = lambda i, shape: smallg[:, offs[i]:offs[i + 1]].reshape((N_SHARD,) + shape)
    gains = piece(0, (2, 4, 256)).transpose(1, 2, 0, 3).reshape(2, 4, D_MODEL)
    gn_full = piece(1, (512,)).reshape(1, RET_HEADS * RET_V)
    cw_full = piece(2, (2, 3, FF_CHUNK)).transpose(1, 0, 2, 3)
    pw_full = piece(3, (4, 64, 256)).transpose(1, 0, 2, 3).reshape(4, 256, 256).astype(BF16)
    cb_full = mlp_conv_b.reshape(2, N_SHARD, 1, FF_CHUNK)

    loss, dx0, big, small = _local_step(
        x[0], positions.reshape(t, 1).astype(F32), loss_target[0], gains, pw_full, pool_scale,
        win, gn_full, wout.reshape(RET_HEADS * RET_V, D_MODEL), (wup0, wup1),
        (wdn0.reshape(D_FF, D_MODEL), wdn1.reshape(D_FF, D_MODEL)), cw_full, cb_full)

    (up0_f, up0_b), (up1_f, up1_b) = big["w_up"]
    (dn0_f, dn0_b), (dn1_f, dn1_b) = big["w_down"]
    in_f, in_b = big["w_in"]
    out_f, out_b = big["w_out"]
    as_shards = lambda a, shape: a.reshape((N_SHARD,) + shape)
    dn_shape, out_shape = (D_FF // N_SHARD, D_MODEL), (RET_V, D_MODEL)
    full_f = [up0_f, up1_f, as_shards(dn0_f, dn_shape), as_shards(dn1_f, dn_shape), in_f, as_shards(out_f, out_shape)]
    full_b = [up0_b, up1_b, as_shards(dn0_b, dn_shape), as_shards(dn1_b, dn_shape), in_b, as_shards(out_b, out_shape)]
    small_order = ["gains", "pool_scale", "gn", "conv_w", "conv_b", "pool_w"]
    gsmall_sizes = [small[k].size for k in small_order]
    gsmall_rows = -(-sum(gsmall_sizes) // (128 * 8)) * 8
    *recv, small_recv = scatter_grads(full_b, _pack_rows([small[k] for k in small_order], gsmall_rows))
    names = ["w_up_0", "w_up_1", "w_down_0", "w_down_1", "w_in", "w_out"]
    planes = [plane_sum(lax.dynamic_index_in_dim(f, me, 0, keepdims=False), r, "plane_sum_" + nm)
              for f, r, nm in zip(full_f, recv, names)]
    others = swap_cores(planes)
    gsmall = sum_slots(small_recv, "sum_small").reshape(-1)
    goffs = np.cumsum([0] + gsmall_sizes)
    gpiece = lambda i: gsmall[goffs[i]:goffs[i + 1]].reshape(small[small_order[i]].shape)

    res = {}
    res["mlp_w_up"] = _adamw_all_layers(mlp_w_up, m_mlp_w_up, v_mlp_w_up,
                                        [(planes[0], others[0]), (planes[1], others[1])], "adamw_w_up")
    res["mlp_w_down"] = _adamw_all_layers(mlp_w_down, m_mlp_w_down, v_mlp_w_down,
                                          [(planes[2], others[2]), (planes[3], others[3])], "adamw_w_down")
    res["ret_w_in"] = _adamw_all_layers(ret_w_in, m_ret_w_in, v_ret_w_in, [(planes[4], others[4])], "adamw_w_in")
    res["ret_w_out"] = _adamw_all_layers(ret_w_out, m_ret_w_out, v_ret_w_out, [(planes[5], others[5])], "adamw_w_out")

    def small_adamw(w, m, v, g, name):
        w3 = w.reshape(1, -1, w.shape[-1])
        out = adamw(w3, m.reshape(w3.shape), v.reshape(w3.shape), [g.reshape(w3.shape[1:])], 0, None, name)
        return [o.reshape(w.shape) for o in out]

    g_gains = lax.dynamic_slice_in_dim(gpiece(0), me * 256, 256, axis=2)
    g_gn = lax.dynamic_slice_in_dim(gpiece(2), me * RET_V, RET_V, axis=1)
    g_cw = lax.dynamic_index_in_dim(gpiece(3), me, 1, keepdims=False)
    g_pw = lax.dynamic_slice_in_dim(gpiece(5), me * 64, 64, axis=1).reshape(1, 4, 64, 256)
    res["norm_gain"] = small_adamw(norm_gain, m_norm_gain, v_norm_gain, g_gains, "adamw_norm_gain")
    res["pool_w"] = small_adamw(pool_w, m_pool_w, v_pool_w, g_pw, "adamw_pool_w")
    res["pool_scale"] = small_adamw(pool_scale, m_pool_scale, v_pool_scale, gpiece(1), "adamw_pool_scale")
    res["ret_gn_gain"] = small_adamw(ret_gn_gain, m_ret_gn_gain, v_ret_gn_gain, g_gn, "adamw_gn_gain")
    res["mlp_conv_w"] = small_adamw(mlp_conv_w, m_mlp_conv_w, v_mlp_conv_w, g_cw, "adamw_conv_w")
    res["mlp_conv_b"] = small_adamw(mlp_conv_b, m_mlp_conv_b, v_mlp_conv_b,
                                    gpiece(4).reshape(mlp_conv_b.shape), "adamw_conv_b")

    order = ["norm_gain", "pool_w", "pool_scale", "ret_w_in", "ret_gn_gain", "ret_w_out", "mlp_w_up", "mlp_conv_w",
             "mlp_conv_b", "mlp_w_down"]
    total_loss = lax.psum(loss[0, 0], ("x", "y", "c"))
    outs = [total_loss, dx0.reshape(x.shape)]
    for q in range(4):
        outs += [res[k][q] for k in order]
    return tuple(outs)
```

```python
import numpy as np
import jax
import jax.numpy as jnp
from jax import lax
from jax.experimental import pallas as pl
from jax.experimental.pallas import tpu as pltpu

F32 = jnp.float32
BF16 = jnp.bfloat16

D_MODEL = 1024
D_FF = 2816
FF_CHUNK = 1408
N_SHARD = 4
POOL_WINDOWS = (2, 4, 8, 16)
POOL_DIM = 256
POOL_HALO = 16
RET_HEADS = 4
RET_QK = 256
RET_V = 512
RET_CHUNK = 128
RET_IN = 6144
RET_IN_SHARD = 1536
ROPE_BASE = 10000.0
EPS = 1e-6
CONV_HALO = 8

ADAM_LR, ADAM_B1, ADAM_B2, ADAM_EPS, ADAM_WD, ADAM_STEP = 0.001, 0.9, 0.999, 1e-08, 0.01, 10

VMEM_LIMIT = 56 * 1024 * 1024
MESH = pl.DeviceIdType.MESH
ANY = pl.BlockSpec(memory_space=pl.ANY)


def _params(n_grid=1, limit=VMEM_LIMIT):
    return pltpu.CompilerParams(dimension_semantics=("arbitrary",) * n_grid, vmem_limit_bytes=limit)


def _dot(a, b):
    return jnp.dot(a, b, preferred_element_type=F32)


def _dot_nt(a, b):
    return lax.dot_general(a, b, (((1,), (1,)), ((), ())), preferred_element_type=F32)


def _dot_tn(a, b):
    return lax.dot_general(a, b, (((0,), (0,)), ((), ())), preferred_element_type=F32)


def _rms_fwd(x, gain):
    r = lax.rsqrt(jnp.mean(x * x, axis=-1, keepdims=True) + EPS)
    xh = x * r
    return xh * gain, xh, r


def _rms_bwd(xh, r, gain, dy):
    dxh = dy * gain
    return r * (dxh - xh * jnp.mean(dxh * xh, axis=-1, keepdims=True))


def _colsum(v):
    return jnp.sum(v, axis=0, keepdims=True)


def _full(shape):
    nd = len(shape)
    return pl.BlockSpec(shape, lambda *_: (0,) * nd)


def rope_tables(pos_col, inv_freq):
    t = pos_col.shape[0]
    tm = min(t, 1024)

    def body(p_ref, f_ref, c_ref, s_ref):
        ang = p_ref[...] * f_ref[...]
        c_ref[...] = jnp.cos(ang)
        s_ref[...] = jnp.sin(ang)

    return pl.pallas_call(
        body, grid=(t // tm,),
        in_specs=[pl.BlockSpec((tm, 1), lambda i: (i, 0)), _full((1, 128))],
        out_specs=[pl.BlockSpec((tm, 128), lambda i: (i, 0))] * 2,
        out_shape=[jax.ShapeDtypeStruct((t, 128), F32)] * 2,
        compiler_params=_params(1), name="rope_tables")(pos_col, inv_freq)


def _pool_diff(hext_ref, h, row0, tm):
    t_idx = row0 + lax.broadcasted_iota(jnp.int32, (tm, 1), 0)
    parts, inv_counts = [], []
    for g, win in enumerate(POOL_WINDOWS):
        cols = slice(g * POOL_DIM, (g + 1) * POOL_DIM)
        s = h[:, cols]
        for j in range(1, win):
            s = s + hext_ref[pl.ds(POOL_HALO - j, tm), cols]
        inv = 1.0 / jnp.minimum(t_idx + 1, win).astype(F32)
        parts.append(s * inv - h[:, cols])
        inv_counts.append(inv)
    return parts, inv_counts


def pool_fwd(x, g_pre, g_post, pool_w, pool_scale, tm=256):
    t, d = x.shape
    nt = t // tm

    def body(x_ref, g0_ref, g1_ref, w_ref, sc_ref, o_ref, hext):
        i = pl.program_id(0)

        @pl.when(i == 0)
        def _():
            hext[0:POOL_HALO, :] = jnp.zeros((POOL_HALO, d), F32)

        xv = x_ref[...]
        h, _, _ = _rms_fwd(xv, g0_ref[...])
        hext[POOL_HALO:POOL_HALO + tm, :] = h
        parts, _ = _pool_diff(hext, h, i * tm, tm)
        ys = [_dot(parts[g].astype(BF16), w_ref[g]) for g in range(len(POOL_WINDOWS))]
        y = jnp.concatenate(ys, axis=-1) * sc_ref[...]
        m, _, _ = _rms_fwd(y, g1_ref[...])
        o_ref[...] = xv + m
        hext[0:POOL_HALO, :] = hext[tm:tm + POOL_HALO, :]

    row = pl.BlockSpec((tm, d), lambda i: (i, 0))
    return pl.pallas_call(
        body, grid=(nt,),
        in_specs=[row, _full((1, d)), _full((1, d)), _full(pool_w.shape), _full((1, d))],
        out_specs=row, out_shape=jax.ShapeDtypeStruct((t, d), F32),
        scratch_shapes=[pltpu.VMEM((POOL_HALO + tm, d), F32)],
        compiler_params=_params(1), name="pool_fwd")(x, g_pre, g_post, pool_w, pool_scale)


def pool_bwd(dx1, x, g_pre, g_post, pool_w, pool_scale, tm=256):
    t, d = x.shape
    nt = t // tm
    ng = len(POOL_WINDOWS)

    def body(dx1_ref, x_ref, xh_ref, g0_ref, g1_ref, w_ref, sc_ref,
             dx_ref, dg0_ref, dg1_ref, dsc_ref, dw_ref, hext, eext):
        i = pl.program_id(0)
        r = nt - 1 - i

        @pl.when(i == 0)
        def _():
            eext[tm:tm + POOL_HALO, :] = jnp.zeros((POOL_HALO, d), F32)
            dg0_ref[...] = jnp.zeros_like(dg0_ref)
            dg1_ref[...] = jnp.zeros_like(dg1_ref)
            dsc_ref[...] = jnp.zeros_like(dsc_ref)
            dw_ref[...] = jnp.zeros_like(dw_ref)

        g0 = g0_ref[...]
        g1 = g1_ref[...]
        sc = sc_ref[...]
        xv = x_ref[...]
        h, xh, rx = _rms_fwd(xv, g0)
        h_halo, _, _ = _rms_fwd(xh_ref[...], g0)
        hext[0:POOL_HALO, :] = h_halo * jnp.where(r > 0, 1.0, 0.0)
        hext[POOL_HALO:POOL_HALO + tm, :] = h
        parts, inv_counts = _pool_diff(hext, h, r * tm, tm)
        parts_b = [p.astype(BF16) for p in parts]
        ypre = jnp.concatenate([_dot(parts_b[g], w_ref[g]) for g in range(ng)], axis=-1)
        _, yh, ry = _rms_fwd(ypre * sc, g1)
        dm = dx1_ref[...]
        dg1_ref[...] += _colsum(dm * yh)
        dy = _rms_bwd(yh, ry, g1, dm)
        dsc_ref[...] += _colsum(dy * ypre)
        dyp = (dy * sc).astype(BF16)
        ddiffs = []
        for g in range(ng):
            cols = slice(g * POOL_DIM, (g + 1) * POOL_DIM)
            dw_ref[g] += _dot_tn(parts_b[g], dyp[:, cols])
            dd = _dot_nt(dyp[:, cols], w_ref[g])
            ddiffs.append(dd)
            eext[0:tm, cols] = dd * inv_counts[g]
        dhs = []
        for g, win in enumerate(POOL_WINDOWS):
            cols = slice(g * POOL_DIM, (g + 1) * POOL_DIM)
            s = eext[0:tm, cols]
            for j in range(1, win):
                s = s + eext[pl.ds(j, tm), cols]
            dhs.append(s - ddiffs[g])
        dh = jnp.concatenate(dhs, axis=-1)
        eext[tm:tm + POOL_HALO, :] = eext[0:POOL_HALO, :]
        dg0_ref[...] += _colsum(dh * xh)
        dx_ref[...] = dm + _rms_bwd(xh, rx, g0, dh)

    row = pl.BlockSpec((tm, d), lambda i: (nt - 1 - i, 0))
    halo = pl.BlockSpec((POOL_HALO, d), lambda i: (jnp.maximum((nt - 1 - i) * (tm // POOL_HALO) - 1, 0), 0))
    vec = _full((1, d))
    return pl.pallas_call(
        body, grid=(nt,),
        in_specs=[row, row, halo, vec, vec, _full(pool_w.shape), vec],
        out_specs=[row, vec, vec, vec, _full((ng, POOL_DIM, POOL_DIM))],
        out_shape=[jax.ShapeDtypeStruct((t, d), F32)] + [jax.ShapeDtypeStruct((1, d), F32)] * 3
        + [jax.ShapeDtypeStruct((ng, POOL_DIM, POOL_DIM), F32)],
        scratch_shapes=[pltpu.VMEM((POOL_HALO + tm, d), F32), pltpu.VMEM((tm + POOL_HALO, d), F32)],
        compiler_params=_params(1), name="pool_bwd")(dx1, x, x, g_pre, g_post, pool_w, pool_scale)


def _conv_taps(cw_ref, j):
    return cw_ref[j, 0:1, :], cw_ref[j, 1:2, :], cw_ref[j, 2:3, :]


def _row_block(m, target=256):
    if m <= target:
        return m
    for b in range(target, 7, -8):
        if m % b == 0:
            return b
    return m


def mlp_fwd(x, g_pre, g_post, w_up, w_down, conv_w, conv_b, tm=256):
    t, d = x.shape
    nt = t // tm
    h8 = CONV_HALO

    def body(x_ref, g2_ref, g3_ref, wup_hbm, wdn_hbm, cw_ref, cb_ref,
             xo_ref, u_ref, f_ref, h_ref, wup_v, wdn_v, uext, sem):
        i = pl.program_id(0)

        @pl.when(i == 0)
        def _():
            c1 = pltpu.make_async_copy(wup_hbm, wup_v, sem.at[0])
            c2 = pltpu.make_async_copy(wdn_hbm, wdn_v, sem.at[1])
            c1.start()
            c2.start()
            uext[:, 0:h8, :] = jnp.zeros((N_SHARD, h8, FF_CHUNK), F32)
            c1.wait()
            c2.wait()

        xv = x_ref[...]
        h, _, _ = _rms_fwd(xv, g2_ref[...])
        hb = h.astype(BF16)
        h_ref[...] = hb
        acc = jnp.zeros((tm, d), F32)
        for k in range(2):
            cs = []
            for j in (k, 2 + k):
                ub = _dot(hb, wup_v[j]).astype(BF16)
                u_ref[:, j * FF_CHUNK:(j + 1) * FF_CHUNK] = ub
                uf = ub.astype(F32)
                uext[j, h8:h8 + tm, :] = uf
                w0, w1, w2 = _conv_taps(cw_ref, j)
                c = (cb_ref[j] + w2 * uf + w1 * uext[j, pl.ds(h8 - 1, tm), :]
                     + w0 * uext[j, pl.ds(h8 - 2, tm), :])
                uext[j, 0:h8, :] = uext[j, tm:tm + h8, :]
                cs.append(c)
            a = cs[0] * jax.nn.sigmoid(cs[0]) * cs[1]
            acc = acc + _dot(a.astype(BF16), wdn_v[k * FF_CHUNK:(k + 1) * FF_CHUNK, :])
        f_ref[...] = acc
        y, _, _ = _rms_fwd(acc, g3_ref[...])
        xo_ref[...] = xv + y

    row = pl.BlockSpec((tm, d), lambda i: (i, 0))
    vec = _full((1, d))
    return pl.pallas_call(
        body, grid=(nt,),
        in_specs=[row, vec, vec, ANY, ANY, _full(conv_w.shape), _full(conv_b.shape)],
        out_specs=[row, pl.BlockSpec((tm, 2 * D_FF), lambda i: (i, 0)), row, row],
        out_shape=[jax.ShapeDtypeStruct((t, d), F32), jax.ShapeDtypeStruct((t, 2 * D_FF), BF16),
                   jax.ShapeDtypeStruct((t, d), F32), jax.ShapeDtypeStruct((t, d), BF16)],
        scratch_shapes=[pltpu.VMEM(w_up.shape, BF16), pltpu.VMEM(w_down.shape, BF16),
                        pltpu.VMEM((N_SHARD, h8 + tm, FF_CHUNK), F32), pltpu.SemaphoreType.DMA((2,))],
        compiler_params=_params(1), name="mlp_fwd")(x, g_pre, g_post, w_up, w_down, conv_w, conv_b)


def mlp_bwd(dxo, f, x, u, g_pre, g_post, w_up, w_down, conv_w, conv_b, tm=128):
    t, d = x.shape
    nt = t // tm
    h8 = CONV_HALO

    def body(dxo_ref, f_ref, x_ref, u_ref, up_ref, g2_ref, g3_ref, wup_hbm, wdn_hbm, cw_ref, cb_ref,
             dx_ref, du_ref, a_ref, df_ref, dg2_ref, dg3_ref, dcw_ref, dcb_ref,
             wup_v, wdn_v, uext, dcext, sem):
        i = pl.program_id(0)
        r = nt - 1 - i

        @pl.when(i == 0)
        def _():
            c1 = pltpu.make_async_copy(wup_hbm, wup_v, sem.at[0])
            c2 = pltpu.make_async_copy(wdn_hbm, wdn_v, sem.at[1])
            c1.start()
            c2.start()
            dcext[:, tm:tm + h8, :] = jnp.zeros((N_SHARD, h8, FF_CHUNK), F32)
            dg2_ref[...] = jnp.zeros_like(dg2_ref)
            dg3_ref[...] = jnp.zeros_like(dg3_ref)
            dcw_ref[...] = jnp.zeros_like(dcw_ref)
            dcb_ref[...] = jnp.zeros_like(dcb_ref)
            c1.wait()
            c2.wait()

        g2 = g2_ref[...]
        g3 = g3_ref[...]
        dxo = dxo_ref[...]
        _, fh, rf = _rms_fwd(f_ref[...], g3)
        dg3_ref[...] += _colsum(dxo * fh)
        dfb = _rms_bwd(fh, rf, g3, dxo).astype(BF16)
        df_ref[...] = dfb
        not_first = jnp.where(r > 0, 1.0, 0.0)
        dh = jnp.zeros((tm, d), F32)
        for k in range(2):
            cs = []
            for s, j in enumerate((k, 2 + k)):
                cols = slice(j * FF_CHUNK, (j + 1) * FF_CHUNK)
                uext[s, 0:h8, :] = up_ref[:, cols].astype(F32) * not_first
                uext[s, h8:h8 + tm, :] = u_ref[:, cols].astype(F32)
                w0, w1, w2 = _conv_taps(cw_ref, j)
                cs.append(cb_ref[j] + w2 * uext[s, pl.ds(h8, tm), :] + w1 * uext[s, pl.ds(h8 - 1, tm), :]
                          + w0 * uext[s, pl.ds(h8 - 2, tm), :])
            cg, cv = cs
            sg = jax.nn.sigmoid(cg)
            sil = cg * sg
            a_ref[:, k * FF_CHUNK:(k + 1) * FF_CHUNK] = (sil * cv).astype(BF16)
            da = _dot_nt(dfb, wdn_v[k * FF_CHUNK:(k + 1) * FF_CHUNK, :])
            dcs = (da * cv * (sg * (1.0 + cg * (1.0 - sg))), da * sil)
            for s, j in enumerate((k, 2 + k)):
                dc = dcs[s]
                w0, w1, w2 = _conv_taps(cw_ref, j)
                dcb_ref[j] += _colsum(dc)
                dcw_ref[j, 2:3, :] += _colsum(dc * uext[s, pl.ds(h8, tm), :])
                dcw_ref[j, 1:2, :] += _colsum(dc * uext[s, pl.ds(h8 - 1, tm), :])
                dcw_ref[j, 0:1, :] += _colsum(dc * uext[s, pl.ds(h8 - 2, tm), :])
                dcext[j, 0:tm, :] = dc
                dub = (w2 * dc + w1 * dcext[j, pl.ds(1, tm), :] + w0 * dcext[j, pl.ds(2, tm), :]).astype(BF16)
                dcext[j, tm:tm + h8, :] = dcext[j, 0:h8, :]
                du_ref[:, j * FF_CHUNK:(j + 1) * FF_CHUNK] = dub
                dh = dh + _dot_nt(dub, wup_v[j])
        _, xh, rx = _rms_fwd(x_ref[...], g2)
        dg2_ref[...] += _colsum(dh * xh)
        dx_ref[...] = dxo + _rms_bwd(xh, rx, g2, dh)

    row = pl.BlockSpec((tm, d), lambda i: (nt - 1 - i, 0))
    urow = pl.BlockSpec((tm, 2 * D_FF), lambda i: (nt - 1 - i, 0))
    uprev = pl.BlockSpec((h8, 2 * D_FF), lambda i: (jnp.maximum((nt - 1 - i) * (tm // h8) - 1, 0), 0))
    vec = _full((1, d))
    return pl.pallas_call(
        body, grid=(nt,),
        in_specs=[row, row, row, urow, uprev, vec, vec, ANY, ANY, _full(conv_w.shape), _full(conv_b.shape)],
        out_specs=[row, urow, pl.BlockSpec((tm, D_FF), lambda i: (nt - 1 - i, 0)), row, vec, vec,
                   _full(conv_w.shape), _full(conv_b.shape)],
        out_shape=[jax.ShapeDtypeStruct((t, d), F32), jax.ShapeDtypeStruct((t, 2 * D_FF), BF16),
                   jax.ShapeDtypeStruct((t, D_FF), BF16), jax.ShapeDtypeStruct((t, d), BF16),
                   jax.ShapeDtypeStruct((1, d), F32), jax.ShapeDtypeStruct((1, d), F32),
                   jax.ShapeDtypeStruct(conv_w.shape, F32), jax.ShapeDtypeStruct(conv_b.shape, F32)],
        scratch_shapes=[pltpu.VMEM(w_up.shape, BF16), pltpu.VMEM(w_down.shape, BF16),
                        pltpu.VMEM((2, h8 + tm, FF_CHUNK), F32), pltpu.VMEM((N_SHARD, tm + h8, FF_CHUNK), F32),
                        pltpu.SemaphoreType.DMA((2,))],
        compiler_params=_params(1), name="mlp_bwd")(dxo, f, x, u, u, g_pre, g_post, w_up, w_down, conv_w, conv_b)


def grad_matmul(a, b, bm, bn, name, tk=512):
    t, m = a.shape
    n = b.shape[1]
    nk = t // tk

    def body(a_ref, b_ref, o_ref, ob_ref):
        kk = pl.program_id(2)

        @pl.when(kk == 0)
        def _():
            o_ref[...] = jnp.zeros_like(o_ref)

        o_ref[...] += _dot_tn(a_ref[...], b_ref[...])

        @pl.when(kk == nk - 1)
        def _():
            ob_ref[...] = o_ref[...].astype(BF16)

    ospec = pl.BlockSpec((None, bm, bn), lambda j, i, kk: (j, i, 0))
    return pl.pallas_call(
        body, grid=(n // bn, m // bm, nk),
        in_specs=[pl.BlockSpec((tk, bm), lambda j, i, kk: (kk, i)), pl.BlockSpec((tk, bn), lambda j, i, kk: (kk, j))],
        out_specs=[ospec, ospec],
        out_shape=[jax.ShapeDtypeStruct((n // bn, m, bn), F32), jax.ShapeDtypeStruct((n // bn, m, bn), BF16)],
        compiler_params=_params(3), name=name)(a, b)


def _decay_tables():
    log_gamma = jnp.log(1.0 - 2.0 ** (-5.0 - jnp.arange(RET_HEADS, dtype=F32)))
    i = jnp.arange(RET_CHUNK, dtype=F32)
    rel = i[:, None] - i[None, :]
    intra = jnp.where(rel >= 0, jnp.exp(jnp.maximum(rel, 0.0) * log_gamma[:, None, None]), 0.0)
    cross = jnp.exp((i + 1.0) * log_gamma[:, None])[:, :, None]
    inner = jnp.exp((RET_CHUNK - 1.0 - i) * log_gamma[:, None])[:, :, None]
    chunk = [float(np.exp(np.float32(RET_CHUNK) * np.log(np.float32(1.0 - 2.0 ** (-5.0 - h))).astype(np.float32)))
             for h in range(RET_HEADS)]
    return intra, cross, inner, chunk


def ret_proj(x, g_pre, w_in, cos, sin, tm=256):
    t, d = x.shape
    nt = t // tm
    per = RET_IN_SHARD // RET_QK

    def body(x_ref, g_ref, win_hbm, c_ref, s_ref, pj_ref, h_ref, win_v, sem):
        @pl.when(pl.program_id(0) == 0)
        def _():
            cp = pltpu.make_async_copy(win_hbm, win_v, sem)
            cp.start()
            cp.wait()

        h, _, _ = _rms_fwd(x_ref[...], g_ref[...])
        hb = h.astype(BF16)
        h_ref[...] = hb
        c = c_ref[...]
        s = s_ref[...]
        for j in range(N_SHARD):
            pjj = _dot(hb, win_v[j])
            for bb in range(per):
                b = per * j + bb
                blk = pjj[:, bb * RET_QK:(bb + 1) * RET_QK]
                if b < 2 * RET_HEADS:
                    x1, x2 = blk[:, :128], blk[:, 128:]
                    o1 = x1 * c - x2 * s
                    o2 = x2 * c + x1 * s
                    if b < RET_HEADS:
                        o1 = o1 * (RET_QK ** -0.5)
                        o2 = o2 * (RET_QK ** -0.5)
                    pj_ref[:, b * RET_QK:b * RET_QK + 128] = o1.astype(BF16)
                    pj_ref[:, b * RET_QK + 128:(b + 1) * RET_QK] = o2.astype(BF16)
                else:
                    pj_ref[:, b * RET_QK:(b + 1) * RET_QK] = blk.astype(BF16)

    row = pl.BlockSpec((tm, d), lambda i: (i, 0))
    tab = pl.BlockSpec((tm, 128), lambda i: (i, 0))
    return pl.pallas_call(
        body, grid=(nt,),
        in_specs=[row, _full((1, d)), ANY, tab, tab],
        out_specs=[pl.BlockSpec((tm, RET_IN), lambda i: (i, 0)), row],
        out_shape=[jax.ShapeDtypeStruct((t, RET_IN), BF16), jax.ShapeDtypeStruct((t, d), BF16)],
        scratch_shapes=[pltpu.VMEM(w_in.shape, BF16), pltpu.SemaphoreType.DMA],
        compiler_params=_params(1), name="ret_proj")(x, g_pre, w_in, cos, sin)


def ret_core_fwd(pj, intra, cross, inner, chunk_decay):
    t = pj.shape[0]
    nc = t // RET_CHUNK
    c = RET_CHUNK
    qk_all = RET_HEADS * RET_QK
    v_all = RET_HEADS * RET_V

    def body(q_ref, k_ref, v_ref, dm_ref, cr_ref, in_ref, o_ref, sp_ref, state):
        @pl.when(pl.program_id(0) == 0)
        def _():
            state[...] = jnp.zeros_like(state)

        for h in range(RET_HEADS):
            q = q_ref[:, h * RET_QK:(h + 1) * RET_QK]
            k = k_ref[:, h * RET_QK:(h + 1) * RET_QK]
            v = v_ref[:, h * RET_V:(h + 1) * RET_V]
            sb = state[h].astype(BF16)
            sp_ref[h] = sb
            sc = _dot_nt(q, k) * dm_ref[h]
            o_ref[:, h * RET_V:(h + 1) * RET_V] = _dot(sc.astype(BF16), v) + _dot(q, sb) * cr_ref[h]
            kd = (k.astype(F32) * in_ref[h]).astype(BF16)
            state[h] = state[h] * chunk_decay[h] + _dot_tn(kd, v)

    return pl.pallas_call(
        body, grid=(nc,),
        in_specs=[pl.BlockSpec((c, qk_all), lambda n: (n, 0)), pl.BlockSpec((c, qk_all), lambda n: (n, 1)),
                  pl.BlockSpec((c, v_all), lambda n: (n, 1)),
                  _full(intra.shape), _full(cross.shape), _full(inner.shape)],
        out_specs=[pl.BlockSpec((c, v_all), lambda n: (n, 0)),
                   pl.BlockSpec((None, RET_HEADS, RET_QK, RET_V), lambda n: (n, 0, 0, 0))],
        out_shape=[jax.ShapeDtypeStruct((t, v_all), F32),
                   jax.ShapeDtypeStruct((nc, RET_HEADS, RET_QK, RET_V), BF16)],
        scratch_shapes=[pltpu.VMEM((RET_HEADS, RET_QK, RET_V), F32)],
        compiler_params=_params(1), name="ret_core_fwd")(pj, pj, pj, intra, cross, inner)


def _group_norm(o_h):
    mu = jnp.mean(o_h, axis=-1, keepdims=True)
    dev = o_h - mu
    rstd = lax.rsqrt(jnp.mean(dev * dev, axis=-1, keepdims=True) + EPS)
    return dev * rstd, rstd


def ret_out_fwd(o, pj, x, gn_gain, g_post, w_out, tm=256):
    t, d = x.shape
    nt = t // tm
    v_all = RET_HEADS * RET_V

    def body(o_ref, g_ref, x_ref, gn_ref, g1_ref, w_ref, xo_ref, y_ref, out_ref):
        for h in range(RET_HEADS):
            cols = slice(h * RET_V, (h + 1) * RET_V)
            ohat, _ = _group_norm(o_ref[:, cols])
            g = g_ref[:, cols].astype(F32)
            y_ref[:, cols] = (g * jax.nn.sigmoid(g) * (ohat * gn_ref[:, cols])).astype(BF16)
        out = _dot(y_ref[...], w_ref[...])
        out_ref[...] = out
        m, _, _ = _rms_fwd(out, g1_ref[...])
        xo_ref[...] = x_ref[...] + m

    row = pl.BlockSpec((tm, d), lambda i: (i, 0))
    wide = pl.BlockSpec((tm, v_all), lambda i: (i, 0))
    return pl.pallas_call(
        body, grid=(nt,),
        in_specs=[wide, pl.BlockSpec((tm, v_all), lambda i: (i, 2)), row, _full((1, v_all)), _full((1, d)),
                  _full(w_out.shape)],
        out_specs=[row, wide, row],
        out_shape=[jax.ShapeDtypeStruct((t, d), F32), jax.ShapeDtypeStruct((t, v_all), BF16),
                   jax.ShapeDtypeStruct((t, d), F32)],
        compiler_params=_params(1), name="ret_out_fwd")(o, pj, x, gn_gain, g_post, w_out)


def ret_out_bwd(dxo, out, o, pj, gn_gain, g_post, w_out, tm=256):
    t, d = out.shape
    nt = t // tm
    v_all = RET_HEADS * RET_V

    def body(dxo_ref, out_ref, o_ref, g_ref, gn_ref, g1_ref, w_ref,
             dout_ref, dgate_ref, do_ref, dg1_ref, dgn_ref):
        @pl.when(pl.program_id(0) == 0)
        def _():
            dg1_ref[...] = jnp.zeros_like(dg1_ref)
            dgn_ref[...] = jnp.zeros_like(dgn_ref)

        g1 = g1_ref[...]
        dxo = dxo_ref[...]
        _, oh_, r_ = _rms_fwd(out_ref[...], g1)
        dg1_ref[...] += _colsum(dxo * oh_)
        doutb = _rms_bwd(oh_, r_, g1, dxo).astype(BF16)
        dout_ref[...] = doutb
        dy = _dot_nt(doutb, w_ref[...])
        for h in range(RET_HEADS):
            cols = slice(h * RET_V, (h + 1) * RET_V)
            gn = gn_ref[:, cols]
            ohat, rstd = _group_norm(o_ref[:, cols])
            g = g_ref[:, cols].astype(F32)
            sg = jax.nn.sigmoid(g)
            dyh = dy[:, cols]
            dgate_ref[:, cols] = (dyh * (ohat * gn) * (sg * (1.0 + g * (1.0 - sg)))).astype(BF16)
            don = dyh * (g * sg)
            dgn_ref[:, cols] += _colsum(don * ohat)
            dohat = don * gn
            do_ref[:, cols] = (rstd * (dohat - jnp.mean(dohat, axis=-1, keepdims=True)
                                       - ohat * jnp.mean(dohat * ohat, axis=-1, keepdims=True))).astype(BF16)

    row = pl.BlockSpec((tm, d), lambda i: (i, 0))
    wide = pl.BlockSpec((tm, v_all), lambda i: (i, 0))
    gate = pl.BlockSpec((tm, v_all), lambda i: (i, 2))
    return pl.pallas_call(
        body, grid=(nt,),
        in_specs=[row, row, wide, gate, _full((1, v_all)), _full((1, d)), _full(w_out.shape)],
        out_specs=[row, gate, wide, _full((1, d)), _full((1, v_all))],
        out_shape=[jax.ShapeDtypeStruct((t, d), BF16), jax.ShapeDtypeStruct((t, RET_IN), BF16),
                   jax.ShapeDtypeStruct((t, v_all), BF16), jax.ShapeDtypeStruct((1, d), F32),
                   jax.ShapeDtypeStruct((1, v_all), F32)],
        compiler_params=_params(1), name="ret_out_bwd")(dxo, out, o, pj, gn_gain, g_post, w_out)


def ret_core_bwd(pj, do, sprev, cos, sin, dpj, intra, cross, inner, chunk_decay):
    t = pj.shape[0]
    nc = t // RET_CHUNK
    c = RET_CHUNK
    qk_all = RET_HEADS * RET_QK
    v_all = RET_HEADS * RET_V
    scale = RET_QK ** -0.5

    def body(q_ref, k_ref, v_ref, do_ref, sp_ref, c_ref, s_ref, dm_ref, cr_ref, in_ref, dpj_in, dpj_ref, dstate):
        @pl.when(pl.program_id(0) == 0)
        def _():
            dstate[...] = jnp.zeros_like(dstate)

        cs = c_ref[...]
        sn = s_ref[...]
        for h in range(RET_HEADS):
            q = q_ref[:, h * RET_QK:(h + 1) * RET_QK]
            k = k_ref[:, h * RET_QK:(h + 1) * RET_QK]
            v = v_ref[:, h * RET_V:(h + 1) * RET_V]
            doh = do_ref[:, h * RET_V:(h + 1) * RET_V]
            dm = dm_ref[h]
            ab = (_dot_nt(q, k) * dm).astype(BF16)
            dab = (_dot_nt(doh, v) * dm).astype(BF16)
            dsb = dstate[h].astype(BF16)
            kd = (k.astype(F32) * in_ref[h]).astype(BF16)
            dv = _dot_tn(ab, doh) + _dot(kd, dsb)
            dq = _dot(dab, k) + cr_ref[h] * _dot_nt(doh, sp_ref[h])
            dk = _dot_tn(dab, q) + in_ref[h] * _dot_nt(v, dsb)
            qd = (q.astype(F32) * cr_ref[h]).astype(BF16)
            dstate[h] = dstate[h] * chunk_decay[h] + _dot_tn(qd, doh)
            for base, dd, sc in ((h * RET_QK, dq, scale), (qk_all + h * RET_QK, dk, 1.0)):
                d1, d2 = dd[:, :128], dd[:, 128:]
                dpj_ref[:, base:base + 128] = ((d1 * cs + d2 * sn) * sc).astype(BF16)
                dpj_ref[:, base + 128:base + RET_QK] = ((d2 * cs - d1 * sn) * sc).astype(BF16)
            dpj_ref[:, 2 * qk_all + h * RET_V:2 * qk_all + (h + 1) * RET_V] = dv.astype(BF16)

    rev = lambda n: nc - 1 - n
    tab = pl.BlockSpec((c, 128), lambda n: (rev(n), 0))
    return pl.pallas_call(
        body, grid=(nc,),
        in_specs=[pl.BlockSpec((c, qk_all), lambda n: (rev(n), 0)), pl.BlockSpec((c, qk_all), lambda n: (rev(n), 1)),
                  pl.BlockSpec((c, v_all), lambda n: (rev(n), 1)), pl.BlockSpec((c, v_all), lambda n: (rev(n), 0)),
                  pl.BlockSpec((None, RET_HEADS, RET_QK, RET_V), lambda n: (rev(n), 0, 0, 0)),
                  tab, tab, _full(intra.shape), _full(cross.shape), _full(inner.shape), ANY],
        out_specs=pl.BlockSpec((c, 2 * qk_all + v_all), lambda n: (rev(n), 0)),
        out_shape=jax.ShapeDtypeStruct((t, RET_IN), BF16),
        scratch_shapes=[pltpu.VMEM((RET_HEADS, RET_QK, RET_V), F32)],
        input_output_aliases={10: 0},
        compiler_params=_params(1), name="ret_core_bwd")(pj, pj, pj, do, sprev, cos, sin, intra, cross, inner, dpj)


def ret_in_bwd(dpj, dres, x, g_pre, w_in, tm=256):
    t, d = x.shape
    nt = t // tm

    def body(dpj_ref, dres_ref, x_ref, g_ref, win_hbm, dx_ref, dg_ref, win_v, sem):
        @pl.when(pl.program_id(0) == 0)
        def _():
            cp = pltpu.make_async_copy(win_hbm, win_v, sem)
            cp.start()
            dg_ref[...] = jnp.zeros_like(dg_ref)
            cp.wait()

        g = g_ref[...]
        dh = jnp.zeros((tm, d), F32)
        for j in range(N_SHARD):
            dh = dh + _dot_nt(dpj_ref[:, j * RET_IN_SHARD:(j + 1) * RET_IN_SHARD], win_v[j])
        _, xh, rx = _rms_fwd(x_ref[...], g)
        dg_ref[...] += _colsum(dh * xh)
        dx_ref[...] = dres_ref[...] + _rms_bwd(xh, rx, g, dh)

    row = pl.BlockSpec((tm, d), lambda i: (i, 0))
    return pl.pallas_call(
        body, grid=(nt,),
        in_specs=[pl.BlockSpec((tm, RET_IN), lambda i: (i, 0)), row, row, _full((1, d)), ANY],
        out_specs=[row, _full((1, d))],
        out_shape=[jax.ShapeDtypeStruct((t, d), F32), jax.ShapeDtypeStruct((1, d), F32)],
        scratch_shapes=[pltpu.VMEM(w_in.shape, BF16), pltpu.SemaphoreType.DMA],
        compiler_params=_params(1), name="ret_in_bwd")(dpj, dres, x, g_pre, w_in)


def loss_head(y, target, tm=512):
    t, d = y.shape

    def body(y_ref, t_ref, l_ref, dy_ref):
        @pl.when(pl.program_id(0) == 0)
        def _():
            l_ref[...] = jnp.zeros_like(l_ref)

        err = y_ref[...] - t_ref[...]
        dy_ref[...] = err * (1.0 / d)
        l_ref[...] += 0.5 * jnp.sum(jnp.mean(err * err, axis=-1, keepdims=True), axis=0, keepdims=True)

    row = pl.BlockSpec((tm, d), lambda i: (i, 0))
    return pl.pallas_call(
        body, grid=(t // tm,), in_specs=[row, row], out_specs=[_full((1, 1)), row],
        out_shape=[jax.ShapeDtypeStruct((1, 1), F32), jax.ShapeDtypeStruct((t, d), F32)],
        compiler_params=_params(1), name="loss_head")(y, target)


_CHIP_FLIPS = ((1, 0), (0, 1), (1, 1))


def _flip(v, b):
    return 1 - v if b else v


def scatter_grads(big, small):
    n = len(big)

    def body(*refs):
        ins, small_in = refs[:n], refs[n]
        outs, small_out = refs[n + 1:2 * n + 1], refs[2 * n + 1]
        send_sems, recv_sems, ssend_sems, srecv_sems, local_sem = refs[2 * n + 2:]
        x, y, c = lax.axis_index("x"), lax.axis_index("y"), lax.axis_index("c")
        mine = 4 * x + 2 * y + c
        copies = [pltpu.make_async_copy(small_in, small_out.at[mine], local_sem)]
        copies[0].start()
        for m in range(1, 8):
            bx, by, bc = (m >> 2) & 1, (m >> 1) & 1, m & 1
            cp = pltpu.make_async_remote_copy(
                src_ref=small_in, dst_ref=small_out.at[mine], send_sem=ssend_sems.at[m - 1],
                recv_sem=srecv_sems.at[m - 1], device_id=(_flip(x, bx), _flip(y, by), _flip(c, bc)),
                device_id_type=MESH)
            cp.start()
            copies.append(cp)
        for t in range(n):
            for k, (bx, by) in enumerate(_CHIP_FLIPS):
                px, py = _flip(x, bx), _flip(y, by)
                cp = pltpu.make_async_remote_copy(
                    src_ref=ins[t].at[2 * px + py], dst_ref=outs[t].at[k], send_sem=send_sems.at[3 * t + k],
                    recv_sem=recv_sems.at[3 * t + k], device_id=(px, py, c), device_id_type=MESH)
                cp.start()
                copies.append(cp)
        for cp in copies:
            cp.wait()

    return pl.pallas_call(
        body, in_specs=[ANY] * (n + 1), out_specs=[ANY] * (n + 1),
        out_shape=[jax.ShapeDtypeStruct((3,) + b.shape[1:], b.dtype) for b in big]
        + [jax.ShapeDtypeStruct((8,) + small.shape, small.dtype)],
        scratch_shapes=[pltpu.SemaphoreType.DMA((3 * n,)), pltpu.SemaphoreType.DMA((3 * n,)),
                        pltpu.SemaphoreType.DMA((7,)), pltpu.SemaphoreType.DMA((7,)), pltpu.SemaphoreType.DMA],
        name="scatter_grads")(*big, small)


def swap_cores(arrays, name):
    n = len(arrays)

    def body(*refs):
        ins, outs = refs[:n], refs[n:2 * n]
        send_sems, recv_sems = refs[2 * n:]
        sibling = (lax.axis_index("x"), lax.axis_index("y"), 1 - lax.axis_index("c"))
        copies = []
        for t in range(n):
            cp = pltpu.make_async_remote_copy(
                src_ref=ins[t], dst_ref=outs[t], send_sem=send_sems.at[t], recv_sem=recv_sems.at[t],
                device_id=sibling, device_id_type=MESH)
            cp.start()
            copies.append(cp)
        for cp in copies:
            cp.wait()

    return pl.pallas_call(
        body, in_specs=[ANY] * n, out_specs=[ANY] * n,
        out_shape=[jax.ShapeDtypeStruct(a.shape, a.dtype) for a in arrays],
        scratch_shapes=[pltpu.SemaphoreType.DMA((n,)), pltpu.SemaphoreType.DMA((n,))],
        name=name)(*arrays)


_HBM = pl.BlockSpec(memory_space=pltpu.HBM)
_SEM = pl.BlockSpec(memory_space=pltpu.SEMAPHORE)
_EFFECT = pltpu.SideEffectType.DATAFLOW_SIDE_EFFECTING


def _chip_copies(mode, srcs, lands, send_sems, recv_sems):
    x, y, c = lax.axis_index("x"), lax.axis_index("y"), lax.axis_index("c")
    copies = []
    for t in range(len(srcs)):
        for k, (bx, by) in enumerate(_CHIP_FLIPS):
            px, py = _flip(x, bx), _flip(y, by)
            if mode == "gather":
                src, dst = srcs[t], lands[t].at[2 * x + y]
            else:
                src, dst = srcs[t].at[2 * px + py], lands[t].at[k]
            copies.append(pltpu.make_async_remote_copy(
                src_ref=src, dst_ref=dst, send_sem=send_sems.at[3 * t + k], recv_sem=recv_sems.at[3 * t + k],
                device_id=(px, py, c), device_id_type=MESH))
    return copies


def exchange_start(mode, srcs, lands, name, after=None):
    n = len(srcs)
    extra = [] if after is None else [after]

    def body(*refs):
        ins, lnd = refs[:n], refs[n:2 * n]
        send_sems, recv_sems = refs[2 * n + len(extra)], refs[2 * n + len(extra) + 1]
        token = refs[-1]
        for cp in _chip_copies(mode, ins, lnd, send_sems, recv_sems):
            cp.start()
        token[...] = jnp.zeros(token.shape, token.dtype)

    hbm = lambda a: pltpu.with_memory_space_constraint(a, pltpu.HBM)
    passed = list(srcs) + list(lands)
    return pl.pallas_call(
        body, name=name,
        out_shape=(pltpu.SemaphoreType.DMA((3 * n,)), pltpu.SemaphoreType.DMA((3 * n,)),
                   *[pltpu.HBM(a.shape, a.dtype) for a in passed], jax.ShapeDtypeStruct((8, 128), F32)),
        in_specs=[_HBM] * (2 * n) + [ANY] * len(extra),
        out_specs=(_SEM, _SEM, *[_HBM] * (2 * n), pl.BlockSpec(memory_space=pltpu.VMEM)),
        input_output_aliases={i: 2 + i for i in range(2 * n)},
        compiler_params=pltpu.CompilerParams(has_side_effects=_EFFECT))(*[hbm(a) for a in passed], *extra)


def exchange_wait(mode, started, after, name):
    send_sems, recv_sems = started[0], started[1]
    passed = list(started[2:-1])
    n = len(passed) // 2

    def body(*refs):
        ins, lnd = refs[:n], refs[n:2 * n]
        for cp in _chip_copies(mode, ins, lnd, refs[2 * n], refs[2 * n + 1]):
            cp.wait_send()
            cp.wait_recv()

    outs = pl.pallas_call(
        body, name=name, out_shape=tuple(pltpu.HBM(a.shape, a.dtype) for a in passed),
        in_specs=[_HBM] * (2 * n) + [_SEM, _SEM, ANY], out_specs=tuple([_HBM] * (2 * n)),
        input_output_aliases={i: i for i in range(2 * n)},
        compiler_params=pltpu.CompilerParams(has_side_effects=_EFFECT))(*passed, send_sems, recv_sems, after)
    return list(outs[n:])


def plane_sum(own, recv, name, bm=256):
    m, n = own.shape
    bm = _row_block(m, bm)

    def body(o_ref, r_ref, s_ref):
        s_ref[...] = ((o_ref[...] + r_ref[0].astype(F32)) + r_ref[1].astype(F32)) + r_ref[2].astype(F32)

    return pl.pallas_call(
        body, grid=(m // bm,),
        in_specs=[pl.BlockSpec((bm, n), lambda i: (i, 0)), pl.BlockSpec((3, bm, n), lambda i: (0, i, 0))],
        out_specs=pl.BlockSpec((bm, n), lambda i: (i, 0)), out_shape=jax.ShapeDtypeStruct((m, n), F32),
        compiler_params=_params(1), name=name)(own, recv)


def sum_slots(parts, name, bm=312):
    _, r, n = parts.shape
    bm = bm if r % bm == 0 else r

    def body(p_ref, s_ref):
        acc = p_ref[0]
        for k in range(1, 8):
            acc = acc + p_ref[k]
        s_ref[...] = acc

    return pl.pallas_call(
        body, grid=(r // bm,), in_specs=[pl.BlockSpec((8, bm, n), lambda i: (0, i, 0))],
        out_specs=pl.BlockSpec((bm, n), lambda i: (i, 0)), out_shape=jax.ShapeDtypeStruct((r, n), F32),
        compiler_params=_params(1), name=name)(parts)


def _adamw_math(w, g, m, v):
    m = ADAM_B1 * m + (1.0 - ADAM_B1) * g
    v = ADAM_B2 * v + (1.0 - ADAM_B2) * (g * g)
    m_hat = m / (1.0 - ADAM_B1 ** ADAM_STEP)
    v_hat = v / (1.0 - ADAM_B2 ** ADAM_STEP)
    delta = -ADAM_LR * (m_hat / (jnp.sqrt(v_hat) + ADAM_EPS) + ADAM_WD * w)
    return delta, m, v


def adamw(w, m, v, grads, layer, prev, name, bm=256):
    _, mm, n = w.shape
    bm = _row_block(mm, bm)
    ng = len(grads)

    def body(*refs):
        w_ref, m_ref, v_ref = refs[:3]
        g_refs = refs[3:3 + ng]
        g_out, d_out, m_out, v_out = refs[-4:]
        g = g_refs[0][...]
        for gr in g_refs[1:]:
            g = g + gr[...]
        delta, mn, vn = _adamw_math(w_ref[...], g, m_ref[...], v_ref[...])
        g_out[...] = g
        d_out[...] = delta
        m_out[...] = mn
        v_out[...] = vn

    slab = pl.BlockSpec((None, bm, n), lambda i: (layer, i, 0))
    flat = pl.BlockSpec((bm, n), lambda i: (i, 0))
    in_specs = [slab] * 3 + [flat] * ng
    args = [w, m, v, *grads]
    aliases = {}
    if prev is not None:
        in_specs += [ANY] * 4
        aliases = {3 + ng + q: q for q in range(4)}
        args += list(prev)
    return pl.pallas_call(
        body, grid=(mm // bm,), in_specs=in_specs, out_specs=[slab] * 4,
        out_shape=[jax.ShapeDtypeStruct(w.shape, F32)] * 4, input_output_aliases=aliases,
        compiler_params=_params(1), name=name)(*args)


def _pack_rows(parts, rows):
    flat = jnp.concatenate([p.reshape(-1) for p in parts])
    return jnp.pad(flat, (0, rows * 128 - flat.shape[0])).reshape(rows, 128)


def _as_shards(a, rows):
    return a.reshape(N_SHARD, rows, a.shape[-1])


def _local_step(x, pos_col, target, gains, pool_w, pool_scale, gn_gain, conv_w, conv_b, weights, send_grads):
    def gain(l, n, token=None):
        g = gains[l, n].reshape(1, D_MODEL)
        return g if token is None else g + token[0:1, 0:1]

    inv_freq = (ROPE_BASE ** (-jnp.arange(0, RET_QK, 2, dtype=F32) / RET_QK)).reshape(1, RET_QK // 2)
    cos, sin = rope_tables(pos_col, inv_freq)
    intra, cross, inner, chunk_decay = _decay_tables()
    dn_rows = D_FF // N_SHARD

    x1 = pool_fwd(x, gain(0, 0), gain(0, 1), pool_w, pool_scale)
    w_up0, w_dn0 = weights("mlp0", x1)
    w_dn0 = w_dn0.reshape(D_FF, D_MODEL)
    x2, u0, f0, h0 = mlp_fwd(x1, gain(0, 2), gain(0, 3), w_up0, w_dn0, conv_w[0], conv_b[0])
    w_in, w_out = weights("ret", x2)
    w_out = w_out.reshape(RET_HEADS * RET_V, D_MODEL)
    pj, hr = ret_proj(x2, gain(1, 0), w_in, cos, sin)
    o, sprev = ret_core_fwd(pj, intra, cross, inner, chunk_decay)
    x3, yb, out = ret_out_fwd(o, pj, x2, gn_gain, gain(1, 1), w_out)
    w_up1, w_dn1 = weights("mlp1", x3)
    w_dn1 = w_dn1.reshape(D_FF, D_MODEL)
    x4, u1, f1, h1 = mlp_fwd(x3, gain(1, 2), gain(1, 3), w_up1, w_dn1, conv_w[1], conv_b[1])
    loss, dx4 = loss_head(x4, target)

    dx3, du1, a1, df1, dg12, dg13, dcw1, dcb1 = mlp_bwd(
        dx4, f1, x3, u1, gain(1, 2), gain(1, 3), w_up1, w_dn1, conv_w[1], conv_b[1])
    dwup1 = grad_matmul(h1, du1, D_MODEL, FF_CHUNK, "grad_w_up_1")
    dwdn1 = grad_matmul(a1, df1, FF_CHUNK, D_MODEL, "grad_w_down_1")
    tok = send_grads("mlp1", [dwup1, [_as_shards(g, dn_rows) for g in dwdn1]])
    dout, dpj, do, dg11, dgn = ret_out_bwd(dx3, out, o, pj, gn_gain, gain(1, 1, tok), w_out)
    dwout = grad_matmul(yb, dout, 1024, D_MODEL, "grad_w_out")
    dpj = ret_core_bwd(pj, do, sprev, cos, sin, dpj, intra, cross, inner, chunk_decay)
    dwin = grad_matmul(hr, dpj, D_MODEL, RET_IN_SHARD, "grad_w_in")
    tok = send_grads("ret", [dwin, [_as_shards(g, RET_V) for g in dwout]])
    dx2, dg10 = ret_in_bwd(dpj, dx3, x2, gain(1, 0, tok), w_in)
    dx1, du0, a0, df0, dg02, dg03, dcw0, dcb0 = mlp_bwd(
        dx2, f0, x1, u0, gain(0, 2), gain(0, 3), w_up0, w_dn0, conv_w[0], conv_b[0])
    dwup0 = grad_matmul(h0, du0, D_MODEL, FF_CHUNK, "grad_w_up_0")
    dwdn0 = grad_matmul(a0, df0, FF_CHUNK, D_MODEL, "grad_w_down_0")
    tok = send_grads("mlp0", [dwup0, [_as_shards(g, dn_rows) for g in dwdn0]])
    dx0, dg00, dg01, dpscale, dpw = pool_bwd(dx1, x, gain(0, 0, tok), gain(0, 1), pool_w, pool_scale)

    dgains = jnp.concatenate([dg00, dg01, dg02, dg03, dg10, dg11, dg12, dg13], axis=0).reshape(2, 4, D_MODEL)
    small = {"gains": dgains, "pool_scale": dpscale, "gn": dgn, "conv_w": jnp.stack([dcw0, dcw1]),
             "conv_b": jnp.stack([dcb0, dcb1]), "pool_w": dpw}
    return loss, dx0, small


def kernel(x, positions, norm_gain, pool_w, pool_scale, ret_w_in, ret_gn_gain, ret_w_out, mlp_w_up, mlp_conv_w, mlp_conv_b, mlp_w_down, loss_target, m_norm_gain, m_pool_w, m_pool_scale, m_ret_w_in, m_ret_gn_gain, m_ret_w_out, m_mlp_w_up, m_mlp_conv_w, m_mlp_conv_b, m_mlp_w_down, v_norm_gain, v_pool_w, v_pool_scale, v_ret_w_in, v_ret_gn_gain, v_ret_w_out, v_mlp_w_up, v_mlp_conv_w, v_mlp_conv_b, v_mlp_w_down):
    t = x.shape[1]
    me = 2 * lax.axis_index("x") + lax.axis_index("y")

    small_parts = [norm_gain, ret_gn_gain, mlp_conv_w, pool_w]
    small_sizes = [p.size for p in small_parts]
    small_rows = -(-sum(small_sizes) // (128 * 8)) * 8
    groups = {"small": [_pack_rows(small_parts, small_rows)],
              "mlp0": [mlp_w_up[0].astype(BF16), mlp_w_down[0].astype(BF16)],
              "ret": [ret_w_in[0].astype(BF16), ret_w_out[0].astype(BF16)],
              "mlp1": [mlp_w_up[1].astype(BF16), mlp_w_down[1].astype(BF16)]}
    gathers, token = {}, None
    for group, srcs in groups.items():
        lands = [lax.dynamic_update_index_in_dim(lax.empty((N_SHARD,) + s.shape, s.dtype), s, me, 0) for s in srcs]
        gathers[group] = exchange_start("gather", srcs, lands, "gather_start_" + group, after=token)
        token = gathers[group][-1]

    def weights(group, after):
        return exchange_wait("gather", gathers[group], after, "gather_wait_" + group)

    sent = {}

    def send_grads(group, pairs):
        lands = [lax.empty((3,) + b.shape[1:], BF16) for _, b in pairs]
        sent[group] = (exchange_start("scatter", [b for _, b in pairs], lands, "scatter_start_" + group),
                       [f for f, _ in pairs])
        return sent[group][0][-1]

    def reduced(group, after, names):
        started, own = sent[group]
        recv = exchange_wait("scatter", started, after, "scatter_wait_" + group)
        return [plane_sum(lax.dynamic_index_in_dim(f, me, 0, keepdims=False), r, "plane_sum_" + nm)
                for f, r, nm in zip(own, recv, names)]

    (smallg,) = weights("small", token)
    smallg = smallg.reshape(N_SHARD, -1)
    offs = np.cumsum([0] + small_sizes)
    piece = lambda i, shape: smallg[:, offs[i]:offs[i + 1]].reshape((N_SHARD,) + shape)
    gains = piece(0, (2, 4, 256)).transpose(1, 2, 0, 3).reshape(2, 4, D_MODEL)
    gn_full = piece(1, (512,)).reshape(1, RET_HEADS * RET_V)
    cw_full = piece(2, (2, 3, FF_CHUNK)).transpose(1, 0, 2, 3)
    pw_full = piece(3, (4, 64, 256)).transpose(1, 0, 2, 3).reshape(4, 256, 256).astype(BF16)
    cb_full = mlp_conv_b.reshape(2, N_SHARD, 1, FF_CHUNK)

    loss, dx0, small = _local_step(
        x[0], positions.reshape(t, 1).astype(F32), loss_target[0], gains, pw_full, pool_scale, gn_full,
        cw_full, cb_full, weights, send_grads)

    def small_adamw(w, m, v, grads, name):
        w3 = w.reshape(1, -1, w.shape[-1])
        out = adamw(w3, m.reshape(w3.shape), v.reshape(w3.shape), [g.reshape(w3.shape[1:]) for g in grads], 0, None, name)
        return [o.reshape(w.shape) for o in out]

    res = {}
    planes_a = reduced("mlp1", dx0, ["w_up_1", "w_down_1"]) + reduced("ret", dx0, ["w_in", "w_out"])
    others_a = swap_cores(planes_a, "swap_cores_a")
    res["ret_w_in"] = adamw(ret_w_in, m_ret_w_in, v_ret_w_in, (planes_a[2], others_a[2]), 0, None, "adamw_w_in")
    res["ret_w_out"] = adamw(ret_w_out, m_ret_w_out, v_ret_w_out, (planes_a[3], others_a[3]), 0, None, "adamw_w_out")
    up1 = adamw(mlp_w_up, m_mlp_w_up, v_mlp_w_up, (planes_a[0], others_a[0]), 1, None, "adamw_w_up_1")
    dn1 = adamw(mlp_w_down, m_mlp_w_down, v_mlp_w_down, (planes_a[1], others_a[1]), 1, None, "adamw_w_down_1")

    pw_f = small["pool_w"].reshape(4, N_SHARD, 64, 256).transpose(1, 0, 2, 3).reshape(N_SHARD, 256, 256)
    small_order = ["gains", "pool_scale", "gn", "conv_w", "conv_b"]
    gsmall_sizes = [small[k].size for k in small_order]
    gsmall_rows = -(-sum(gsmall_sizes) // (128 * 8)) * 8
    planes_b = reduced("mlp0", dn1[0], ["w_up_0", "w_down_0"])
    pw_recv, small_recv = scatter_grads([pw_f.astype(BF16)], _pack_rows([small[k] for k in small_order], gsmall_rows))
    planes_b.append(plane_sum(lax.dynamic_index_in_dim(pw_f, me, 0, keepdims=False), pw_recv, "plane_sum_pool_w"))
    others_b = swap_cores(planes_b, "swap_cores_b")
    res["mlp_w_up"] = adamw(mlp_w_up, m_mlp_w_up, v_mlp_w_up, (planes_b[0], others_b[0]), 0, up1, "adamw_w_up_0")
    res["mlp_w_down"] = adamw(mlp_w_down, m_mlp_w_down, v_mlp_w_down, (planes_b[1], others_b[1]), 0, dn1,
                              "adamw_w_down_0")
    res["pool_w"] = small_adamw(pool_w, m_pool_w, v_pool_w, (planes_b[2], others_b[2]), "adamw_pool_w")

    gsmall = sum_slots(small_recv, "sum_small").reshape(-1)
    goffs = np.cumsum([0] + gsmall_sizes)
    gpiece = lambda i: gsmall[goffs[i]:goffs[i + 1]].reshape(small[small_order[i]].shape)
    g_gains = lax.dynamic_slice_in_dim(gpiece(0), me * 256, 256, axis=2)
    g_gn = lax.dynamic_slice_in_dim(gpiece(2), me * RET_V, RET_V, axis=1)
    g_cw = lax.dynamic_index_in_dim(gpiece(3), me, 1, keepdims=False)
    res["norm_gain"] = small_adamw(norm_gain, m_norm_gain, v_norm_gain, [g_gains], "adamw_norm_gain")
    res["pool_scale"] = small_adamw(pool_scale, m_pool_scale, v_pool_scale, [gpiece(1)], "adamw_pool_scale")
    res["ret_gn_gain"] = small_adamw(ret_gn_gain, m_ret_gn_gain, v_ret_gn_gain, [g_gn], "adamw_gn_gain")
    res["mlp_conv_w"] = small_adamw(mlp_conv_w, m_mlp_conv_w, v_mlp_conv_w, [g_cw], "adamw_conv_w")
    res["mlp_conv_b"] = small_adamw(mlp_conv_b, m_mlp_conv_b, v_mlp_conv_b, [gpiece(4)], "adamw_conv_b")

    order = ["norm_gain", "pool_w", "pool_scale", "ret_w_in", "ret_gn_gain", "ret_w_out", "mlp_w_up", "mlp_conv_w",
             "mlp_conv_b", "mlp_w_down"]
    total_loss = lax.psum(loss[0, 0], ("x", "y", "c"))
    outs = [total_loss, dx0.reshape(x.shape)]
    for q in range(4):
        outs += [res[k][q] for k in order]
    return tuple(outs)
```

```python
import numpy as np
import jax
import jax.numpy as jnp
from jax import lax
from jax.experimental import pallas as pl
from jax.experimental.pallas import tpu as pltpu

F32 = jnp.float32
BF16 = jnp.bfloat16

D_MODEL = 1024
D_FF = 2816
FF_CHUNK = 1408
N_SHARD = 4
POOL_WINDOWS = (2, 4, 8, 16)
POOL_DIM = 256
POOL_HALO = 16
RET_HEADS = 4
RET_QK = 256
RET_V = 512
RET_CHUNK = 128
RET_IN = 6144
RET_IN_SHARD = 1536
ROPE_BASE = 10000.0
EPS = 1e-6
CONV_HALO = 8

ADAM_LR, ADAM_B1, ADAM_B2, ADAM_EPS, ADAM_WD, ADAM_STEP = 0.001, 0.9, 0.999, 1e-08, 0.01, 10

VMEM_LIMIT = 56 * 1024 * 1024
MESH = pl.DeviceIdType.MESH
ANY = pl.BlockSpec(memory_space=pl.ANY)


def _params(n_grid=1, limit=VMEM_LIMIT):
    return pltpu.CompilerParams(dimension_semantics=("arbitrary",) * n_grid, vmem_limit_bytes=limit)


def _dot(a, b):
    return jnp.dot(a, b, preferred_element_type=F32)


def _dot_nt(a, b):
    return lax.dot_general(a, b, (((1,), (1,)), ((), ())), preferred_element_type=F32)


def _dot_tn(a, b):
    return lax.dot_general(a, b, (((0,), (0,)), ((), ())), preferred_element_type=F32)


def _rms_fwd(x, gain):
    r = lax.rsqrt(jnp.mean(x * x, axis=-1, keepdims=True) + EPS)
    xh = x * r
    return xh * gain, xh, r


def _rms_bwd(xh, r, gain, dy):
    dxh = dy * gain
    return r * (dxh - xh * jnp.mean(dxh * xh, axis=-1, keepdims=True))


def _colsum(v):
    return jnp.sum(v, axis=0, keepdims=True)


def _full(shape):
    nd = len(shape)
    return pl.BlockSpec(shape, lambda *_: (0,) * nd)


def rope_tables(pos_col, inv_freq):
    t = pos_col.shape[0]
    tm = min(t, 1024)

    def body(p_ref, f_ref, c_ref, s_ref):
        ang = p_ref[...] * f_ref[...]
        c_ref[...] = jnp.cos(ang)
        s_ref[...] = jnp.sin(ang)

    return pl.pallas_call(
        body, grid=(t // tm,),
        in_specs=[pl.BlockSpec((tm, 1), lambda i: (i, 0)), _full((1, 128))],
        out_specs=[pl.BlockSpec((tm, 128), lambda i: (i, 0))] * 2,
        out_shape=[jax.ShapeDtypeStruct((t, 128), F32)] * 2,
        compiler_params=_params(1), name="rope_tables")(pos_col, inv_freq)


def _window_sums(ext, backward):
    n = ext.shape[0]
    cur, sums = ext, []
    for g, win in enumerate(POOL_WINDOWS):
        if g > 0:
            cur = cur[:, POOL_DIM:]
        half = win // 2
        cur = cur + pltpu.roll(cur, n - half if backward else half, axis=0)
        sums.append(cur[:, 0:POOL_DIM])
    return sums


def _pool_diff(h_halo, h, row0, tm):
    t_idx = row0 + lax.broadcasted_iota(jnp.int32, (tm, 1), 0)
    sums = _window_sums(jnp.concatenate([h_halo, h], axis=0), backward=False)
    parts, inv_counts = [], []
    for g, win in enumerate(POOL_WINDOWS):
        inv = 1.0 / jnp.minimum(t_idx + 1, win).astype(F32)
        parts.append(sums[g][POOL_HALO:, :] * inv - h[:, g * POOL_DIM:(g + 1) * POOL_DIM])
        inv_counts.append(inv)
    return parts, inv_counts


def pool_fwd(x, g_pre, g_post, pool_w, pool_scale, tm=256):
    t, d = x.shape
    nt = t // tm

    def body(x_ref, g0_ref, g1_ref, w_ref, sc_ref, o_ref, hext):
        i = pl.program_id(0)

        @pl.when(i == 0)
        def _():
            hext[...] = jnp.zeros((POOL_HALO, d), F32)

        xv = x_ref[...]
        h, _, _ = _rms_fwd(xv, g0_ref[...])
        parts, _ = _pool_diff(hext[...], h, i * tm, tm)
        hext[...] = h[tm - POOL_HALO:tm, :]
        ys = [_dot(parts[g].astype(BF16), w_ref[g]) for g in range(len(POOL_WINDOWS))]
        y = jnp.concatenate(ys, axis=-1) * sc_ref[...]
        m, _, _ = _rms_fwd(y, g1_ref[...])
        o_ref[...] = xv + m

    row = pl.BlockSpec((tm, d), lambda i: (i, 0))
    return pl.pallas_call(
        body, grid=(nt,),
        in_specs=[row, _full((1, d)), _full((1, d)), _full(pool_w.shape), _full((1, d))],
        out_specs=row, out_shape=jax.ShapeDtypeStruct((t, d), F32),
        scratch_shapes=[pltpu.VMEM((POOL_HALO, d), F32)],
        compiler_params=_params(1), name="pool_fwd")(x, g_pre, g_post, pool_w, pool_scale)


def pool_bwd(dx1, x, g_pre, g_post, pool_w, pool_scale, tm=256):
    t, d = x.shape
    nt = t // tm
    ng = len(POOL_WINDOWS)

    def body(dx1_ref, x_ref, xh_ref, g0_ref, g1_ref, w_ref, sc_ref,
             dx_ref, dg0_ref, dg1_ref, dsc_ref, dw_ref, enext):
        i = pl.program_id(0)
        r = nt - 1 - i

        @pl.when(i == 0)
        def _():
            enext[...] = jnp.zeros((POOL_HALO, d), F32)
            dg0_ref[...] = jnp.zeros_like(dg0_ref)
            dg1_ref[...] = jnp.zeros_like(dg1_ref)
            dsc_ref[...] = jnp.zeros_like(dsc_ref)
            dw_ref[...] = jnp.zeros_like(dw_ref)

        g0 = g0_ref[...]
        g1 = g1_ref[...]
        sc = sc_ref[...]
        xv = x_ref[...]
        h, xh, rx = _rms_fwd(xv, g0)
        h_halo, _, _ = _rms_fwd(xh_ref[...], g0)
        parts, inv_counts = _pool_diff(h_halo * jnp.where(r > 0, 1.0, 0.0), h, r * tm, tm)
        parts_b = [p.astype(BF16) for p in parts]
        ypre = jnp.concatenate([_dot(parts_b[g], w_ref[g]) for g in range(ng)], axis=-1)
        _, yh, ry = _rms_fwd(ypre * sc, g1)
        dm = dx1_ref[...]
        dg1_ref[...] += _colsum(dm * yh)
        dy = _rms_bwd(yh, ry, g1, dm)
        dsc_ref[...] += _colsum(dy * ypre)
        dyp = (dy * sc).astype(BF16)
        ddiffs = []
        for g in range(ng):
            cols = slice(g * POOL_DIM, (g + 1) * POOL_DIM)
            dw_ref[g] += _dot_tn(parts_b[g], dyp[:, cols])
            ddiffs.append(_dot_nt(dyp[:, cols], w_ref[g]))
        e = jnp.concatenate([ddiffs[g] * inv_counts[g] for g in range(ng)], axis=-1)
        sums = _window_sums(jnp.concatenate([e, enext[...]], axis=0), backward=True)
        enext[...] = e[0:POOL_HALO, :]
        dh = jnp.concatenate([sums[g][0:tm, :] - ddiffs[g] for g in range(ng)], axis=-1)
        dg0_ref[...] += _colsum(dh * xh)
        dx_ref[...] = dm + _rms_bwd(xh, rx, g0, dh)

    row = pl.BlockSpec((tm, d), lambda i: (nt - 1 - i, 0))
    halo = pl.BlockSpec((POOL_HALO, d), lambda i: (jnp.maximum((nt - 1 - i) * (tm // POOL_HALO) - 1, 0), 0))
    vec = _full((1, d))
    return pl.pallas_call(
        body, grid=(nt,),
        in_specs=[row, row, halo, vec, vec, _full(pool_w.shape), vec],
        out_specs=[row, vec, vec, vec, _full((ng, POOL_DIM, POOL_DIM))],
        out_shape=[jax.ShapeDtypeStruct((t, d), F32)] + [jax.ShapeDtypeStruct((1, d), F32)] * 3
        + [jax.ShapeDtypeStruct((ng, POOL_DIM, POOL_DIM), F32)],
        scratch_shapes=[pltpu.VMEM((POOL_HALO, d), F32)],
        compiler_params=_params(1), name="pool_bwd")(dx1, x, x, g_pre, g_post, pool_w, pool_scale)


def _conv_taps(cw_ref, j):
    return cw_ref[j, 0:1, :], cw_ref[j, 1:2, :], cw_ref[j, 2:3, :]


def _row_block(m, target=256):
    if m <= target:
        return m
    for b in range(target, 7, -8):
        if m % b == 0:
            return b
    return m


def mlp_fwd(x, g_pre, g_post, w_up, w_down, conv_w, conv_b, tm=256):
    t, d = x.shape
    nt = t // tm
    h8 = CONV_HALO

    def body(x_ref, g2_ref, g3_ref, wup_hbm, wdn_hbm, cw_ref, cb_ref,
             xo_ref, u_ref, c_ref, f_ref, h_ref, wup_v, wdn_v, tail, sem):
        i = pl.program_id(0)

        @pl.when(i == 0)
        def _():
            c1 = pltpu.make_async_copy(wup_hbm, wup_v, sem.at[0])
            c2 = pltpu.make_async_copy(wdn_hbm, wdn_v, sem.at[1])
            c1.start()
            c2.start()
            tail[...] = jnp.zeros_like(tail)
            c1.wait()
            c2.wait()

        xv = x_ref[...]
        h, _, _ = _rms_fwd(xv, g2_ref[...])
        hb = h.astype(BF16)
        h_ref[...] = hb
        acc = jnp.zeros((tm, d), F32)
        for k in range(2):
            cs = []
            for s in range(2):
                j, cols = k + 2 * s, slice((2 * k + s) * FF_CHUNK, (2 * k + s + 1) * FF_CHUNK)
                ub = _dot(hb, wup_v[j]).astype(BF16)
                u_ref[:, cols] = ub
                uf = ub.astype(F32)
                ext = jnp.concatenate([tail[j], uf], axis=0)
                tail[j] = uf[tm - h8:tm, :]
                w0, w1, w2 = _conv_taps(cw_ref, j)
                cb = (cb_ref[j] + w2 * uf + w1 * pltpu.roll(ext, 1, axis=0)[h8:, :]
                      + w0 * pltpu.roll(ext, 2, axis=0)[h8:, :]).astype(BF16)
                c_ref[:, cols] = cb
                cs.append(cb.astype(F32))
            a = cs[0] * jax.nn.sigmoid(cs[0]) * cs[1]
            acc = acc + _dot(a.astype(BF16), wdn_v[k * FF_CHUNK:(k + 1) * FF_CHUNK, :])
        f_ref[...] = acc
        y, _, _ = _rms_fwd(acc, g3_ref[...])
        xo_ref[...] = xv + y

    row = pl.BlockSpec((tm, d), lambda i: (i, 0))
    wide = pl.BlockSpec((tm, 2 * D_FF), lambda i: (i, 0))
    vec = _full((1, d))
    return pl.pallas_call(
        body, grid=(nt,),
        in_specs=[row, vec, vec, ANY, ANY, _full(conv_w.shape), _full(conv_b.shape)],
        out_specs=[row, wide, wide, row, row],
        out_shape=[jax.ShapeDtypeStruct((t, d), F32), jax.ShapeDtypeStruct((t, 2 * D_FF), BF16),
                   jax.ShapeDtypeStruct((t, 2 * D_FF), BF16),
                   jax.ShapeDtypeStruct((t, d), F32), jax.ShapeDtypeStruct((t, d), BF16)],
        scratch_shapes=[pltpu.VMEM(w_up.shape, BF16), pltpu.VMEM(w_down.shape, BF16),
                        pltpu.VMEM((N_SHARD, h8, FF_CHUNK), F32), pltpu.SemaphoreType.DMA((2,))],
        compiler_params=_params(1), name="mlp_fwd")(x, g_pre, g_post, w_up, w_down, conv_w, conv_b)


def _rowsum8(v):
    return jnp.sum(v.reshape(v.shape[0] // 8, 8, v.shape[1]), axis=0)


def mlp_bwd(dxo, f, x, u, c, g_pre, g_post, w_up, w_down, conv_w, tm=256):
    t, d = x.shape
    nt = t // tm
    h8 = CONV_HALO

    def body(dxo_ref, f_ref, x_ref, u_ref, c_ref, g2_ref, g3_ref, wup_hbm, wdn_hbm, cw_ref,
             dx_ref, du_ref, a_ref, df_ref, dg2_ref, dg3_ref, dcw_ref, dcb_ref,
             wup_v, wdn_v, dfb_s, dh_s, carry, sem):
        i = pl.program_id(0)
        k = pl.program_id(1)

        @pl.when((i == 0) & (k == 0))
        def _():
            c1 = pltpu.make_async_copy(wup_hbm, wup_v, sem.at[0])
            c2 = pltpu.make_async_copy(wdn_hbm, wdn_v, sem.at[1])
            c1.start()
            c2.start()
            carry[...] = jnp.zeros_like(carry)
            dg2_ref[...] = jnp.zeros_like(dg2_ref)
            dg3_ref[...] = jnp.zeros_like(dg3_ref)
            dcw_ref[...] = jnp.zeros_like(dcw_ref)
            dcb_ref[...] = jnp.zeros_like(dcb_ref)
            c1.wait()
            c2.wait()

        @pl.when(k == 0)
        def _():
            g3 = g3_ref[...]
            dxo = dxo_ref[...]
            _, fh, rf = _rms_fwd(f_ref[...], g3)
            dg3_ref[...] += _rowsum8(dxo * fh)
            dfb = _rms_bwd(fh, rf, g3, dxo).astype(BF16)
            df_ref[...] = dfb
            dfb_s[...] = dfb
            dh_s[...] = jnp.zeros_like(dh_s)

        cg = c_ref[:, 0:FF_CHUNK].astype(F32)
        cv = c_ref[:, FF_CHUNK:2 * FF_CHUNK].astype(F32)
        sg = jax.nn.sigmoid(cg)
        sil = cg * sg
        a_ref[...] = (sil * cv).astype(BF16)
        da = _dot_nt(dfb_s[...], wdn_v[pl.ds(pl.multiple_of(k * FF_CHUNK, FF_CHUNK), FF_CHUNK), :])
        dcs = (da * cv * (sg * (1.0 + cg * (1.0 - sg))), da * sil)
        dh = dh_s[...]
        for s in range(2):
            j = k + 2 * s
            cols = slice(s * FF_CHUNK, (s + 1) * FF_CHUNK)
            uf = u_ref[:, cols].astype(F32)
            ext = jnp.concatenate([dcs[s], carry[j]], axis=0)
            carry[j] = dcs[s][0:h8, :]
            dc1 = pltpu.roll(ext, tm + h8 - 1, axis=0)[0:tm, :]
            dc2 = pltpu.roll(ext, tm + h8 - 2, axis=0)[0:tm, :]
            dcb_ref[j] += _rowsum8(dcs[s])
            dcw_ref[j, 2] += _rowsum8(dcs[s] * uf)
            dcw_ref[j, 1] += _rowsum8(dc1 * uf)
            dcw_ref[j, 0] += _rowsum8(dc2 * uf)
            dub = (cw_ref[j, 2:3, :] * dcs[s] + cw_ref[j, 1:2, :] * dc1 + cw_ref[j, 0:1, :] * dc2).astype(BF16)
            du_ref[:, cols] = dub
            dh = dh + _dot_nt(dub, wup_v[j])
        dh_s[...] = dh

        @pl.when(k == 1)
        def _():
            g2 = g2_ref[...]
            _, xh, rx = _rms_fwd(x_ref[...], g2)
            dg2_ref[...] += _rowsum8(dh * xh)
            dx_ref[...] = dxo_ref[...] + _rms_bwd(xh, rx, g2, dh)

    row = pl.BlockSpec((tm, d), lambda i, k: (nt - 1 - i, 0))
    half = pl.BlockSpec((tm, 2 * FF_CHUNK), lambda i, k: (nt - 1 - i, k))
    vec = _full((1, d))
    acc = _full((8, d))
    dcw_shape, dcb_shape = (N_SHARD, 3, 8, FF_CHUNK), (N_SHARD, 8, FF_CHUNK)
    return pl.pallas_call(
        body, grid=(nt, 2),
        in_specs=[row, row, row, half, half, vec, vec, ANY, ANY, _full(conv_w.shape)],
        out_specs=[row, half, pl.BlockSpec((tm, FF_CHUNK), lambda i, k: (nt - 1 - i, k)), row, acc, acc,
                   _full(dcw_shape), _full(dcb_shape)],
        out_shape=[jax.ShapeDtypeStruct((t, d), F32), jax.ShapeDtypeStruct((t, 2 * D_FF), BF16),
                   jax.ShapeDtypeStruct((t, D_FF), BF16), jax.ShapeDtypeStruct((t, d), BF16),
                   jax.ShapeDtypeStruct((8, d), F32), jax.ShapeDtypeStruct((8, d), F32),
                   jax.ShapeDtypeStruct(dcw_shape, F32), jax.ShapeDtypeStruct(dcb_shape, F32)],
        scratch_shapes=[pltpu.VMEM(w_up.shape, BF16), pltpu.VMEM(w_down.shape, BF16),
                        pltpu.VMEM((tm, d), BF16), pltpu.VMEM((tm, d), F32),
                        pltpu.VMEM((N_SHARD, h8, FF_CHUNK), F32), pltpu.SemaphoreType.DMA((2,))],
        compiler_params=_params(2), name="mlp_bwd")(dxo, f, x, u, c, g_pre, g_post, w_up, w_down, conv_w)


def grad_matmul(a, b, bm, bn, name, tk=512, interleaved=False):
    t, m = a.shape
    n = b.shape[1]
    nk = t // tk
    place = (lambda j: (j % 2) * 2 + j // 2) if interleaved else (lambda j: j)

    def body(a_ref, b_ref, o_ref, ob_ref):
        kk = pl.program_id(2)

        @pl.when(kk == 0)
        def _():
            o_ref[...] = jnp.zeros_like(o_ref)

        o_ref[...] += _dot_tn(a_ref[...], b_ref[...])

        @pl.when(kk == nk - 1)
        def _():
            ob_ref[...] = o_ref[...].astype(BF16)

    ospec = pl.BlockSpec((None, bm, bn), lambda j, i, kk: (place(j), i, 0))
    return pl.pallas_call(
        body, grid=(n // bn, m // bm, nk),
        in_specs=[pl.BlockSpec((tk, bm), lambda j, i, kk: (kk, i)), pl.BlockSpec((tk, bn), lambda j, i, kk: (kk, j))],
        out_specs=[ospec, ospec],
        out_shape=[jax.ShapeDtypeStruct((n // bn, m, bn), F32), jax.ShapeDtypeStruct((n // bn, m, bn), BF16)],
        compiler_params=_params(3), name=name)(a, b)


def _decay_tables():
    log_gamma = jnp.log(1.0 - 2.0 ** (-5.0 - jnp.arange(RET_HEADS, dtype=F32)))
    i = jnp.arange(RET_CHUNK, dtype=F32)
    rel = i[:, None] - i[None, :]
    intra = jnp.where(rel >= 0, jnp.exp(jnp.maximum(rel, 0.0) * log_gamma[:, None, None]), 0.0)
    cross = jnp.exp((i + 1.0) * log_gamma[:, None])[:, :, None]
    inner = jnp.exp((RET_CHUNK - 1.0 - i) * log_gamma[:, None])[:, :, None]
    chunk = [float(np.exp(np.float32(RET_CHUNK) * np.log(np.float32(1.0 - 2.0 ** (-5.0 - h))).astype(np.float32)))
             for h in range(RET_HEADS)]
    return intra, cross, inner, chunk


def ret_proj(x, g_pre, w_in, cos, sin, tm=256):
    t, d = x.shape
    nt = t // tm
    per = RET_IN_SHARD // RET_QK

    def body(x_ref, g_ref, win_hbm, c_ref, s_ref, pj_ref, h_ref, win_v, sem):
        @pl.when(pl.program_id(0) == 0)
        def _():
            cp = pltpu.make_async_copy(win_hbm, win_v, sem)
            cp.start()
            cp.wait()

        h, _, _ = _rms_fwd(x_ref[...], g_ref[...])
        hb = h.astype(BF16)
        h_ref[...] = hb
        c = c_ref[...]
        s = s_ref[...]
        for j in range(N_SHARD):
            pjj = _dot(hb, win_v[j])
            for bb in range(per):
                b = per * j + bb
                blk = pjj[:, bb * RET_QK:(bb + 1) * RET_QK]
                if b < 2 * RET_HEADS:
                    x1, x2 = blk[:, :128], blk[:, 128:]
                    o1 = x1 * c - x2 * s
                    o2 = x2 * c + x1 * s
                    if b < RET_HEADS:
                        o1 = o1 * (RET_QK ** -0.5)
                        o2 = o2 * (RET_QK ** -0.5)
                    pj_ref[:, b * RET_QK:b * RET_QK + 128] = o1.astype(BF16)
                    pj_ref[:, b * RET_QK + 128:(b + 1) * RET_QK] = o2.astype(BF16)
                else:
                    pj_ref[:, b * RET_QK:(b + 1) * RET_QK] = blk.astype(BF16)

    row = pl.BlockSpec((tm, d), lambda i: (i, 0))
    tab = pl.BlockSpec((tm, 128), lambda i: (i, 0))
    return pl.pallas_call(
        body, grid=(nt,),
        in_specs=[row, _full((1, d)), ANY, tab, tab],
        out_specs=[pl.BlockSpec((tm, RET_IN), lambda i: (i, 0)), row],
        out_shape=[jax.ShapeDtypeStruct((t, RET_IN), BF16), jax.ShapeDtypeStruct((t, d), BF16)],
        scratch_shapes=[pltpu.VMEM(w_in.shape, BF16), pltpu.SemaphoreType.DMA],
        compiler_params=_params(1), name="ret_proj")(x, g_pre, w_in, cos, sin)


def ret_core_fwd(pj, intra, cross, inner, chunk_decay):
    t = pj.shape[0]
    nc = t // RET_CHUNK
    c = RET_CHUNK
    qk_all = RET_HEADS * RET_QK
    v_all = RET_HEADS * RET_V

    def body(q_ref, k_ref, v_ref, dm_ref, cr_ref, in_ref, o_ref, sp_ref, state):
        @pl.when(pl.program_id(0) == 0)
        def _():
            state[...] = jnp.zeros_like(state)

        for h in range(RET_HEADS):
            q = q_ref[:, h * RET_QK:(h + 1) * RET_QK]
            k = k_ref[:, h * RET_QK:(h + 1) * RET_QK]
            v = v_ref[:, h * RET_V:(h + 1) * RET_V]
            sb = state[h].astype(BF16)
            sp_ref[h] = sb
            sc = _dot_nt(q, k) * dm_ref[h]
            o_ref[:, h * RET_V:(h + 1) * RET_V] = _dot(sc.astype(BF16), v) + _dot(q, sb) * cr_ref[h]
            kd = (k.astype(F32) * in_ref[h]).astype(BF16)
            state[h] = state[h] * chunk_decay[h] + _dot_tn(kd, v)

    return pl.pallas_call(
        body, grid=(nc,),
        in_specs=[pl.BlockSpec((c, qk_all), lambda n: (n, 0)), pl.BlockSpec((c, qk_all), lambda n: (n, 1)),
                  pl.BlockSpec((c, v_all), lambda n: (n, 1)),
                  _full(intra.shape), _full(cross.shape), _full(inner.shape)],
        out_specs=[pl.BlockSpec((c, v_all), lambda n: (n, 0)),
                   pl.BlockSpec((None, RET_HEADS, RET_QK, RET_V), lambda n: (n, 0, 0, 0))],
        out_shape=[jax.ShapeDtypeStruct((t, v_all), F32),
                   jax.ShapeDtypeStruct((nc, RET_HEADS, RET_QK, RET_V), BF16)],
        scratch_shapes=[pltpu.VMEM((RET_HEADS, RET_QK, RET_V), F32)],
        compiler_params=_params(1), name="ret_core_fwd")(pj, pj, pj, intra, cross, inner)


def _group_norm(o_h):
    mu = jnp.mean(o_h, axis=-1, keepdims=True)
    dev = o_h - mu
    rstd = lax.rsqrt(jnp.mean(dev * dev, axis=-1, keepdims=True) + EPS)
    return dev * rstd, rstd


def ret_out_fwd(o, pj, x, gn_gain, g_post, w_out, tm=256):
    t, d = x.shape
    nt = t // tm
    v_all = RET_HEADS * RET_V

    def body(o_ref, g_ref, x_ref, gn_ref, g1_ref, w_ref, xo_ref, y_ref, out_ref):
        for h in range(RET_HEADS):
            cols = slice(h * RET_V, (h + 1) * RET_V)
            ohat, _ = _group_norm(o_ref[:, cols])
            g = g_ref[:, cols].astype(F32)
            y_ref[:, cols] = (g * jax.nn.sigmoid(g) * (ohat * gn_ref[:, cols])).astype(BF16)
        out = _dot(y_ref[...], w_ref[...])
        out_ref[...] = out
        m, _, _ = _rms_fwd(out, g1_ref[...])
        xo_ref[...] = x_ref[...] + m

    row = pl.BlockSpec((tm, d), lambda i: (i, 0))
    wide = pl.BlockSpec((tm, v_all), lambda i: (i, 0))
    return pl.pallas_call(
        body, grid=(nt,),
        in_specs=[wide, pl.BlockSpec((tm, v_all), lambda i: (i, 2)), row, _full((1, v_all)), _full((1, d)),
                  _full(w_out.shape)],
        out_specs=[row, wide, row],
        out_shape=[jax.ShapeDtypeStruct((t, d), F32), jax.ShapeDtypeStruct((t, v_all), BF16),
                   jax.ShapeDtypeStruct((t, d), F32)],
        compiler_params=_params(1), name="ret_out_fwd")(o, pj, x, gn_gain, g_post, w_out)


def ret_out_bwd(dxo, out, o, pj, gn_gain, g_post, w_out, tm=256):
    t, d = out.shape
    nt = t // tm
    v_all = RET_HEADS * RET_V

    def body(dxo_ref, out_ref, o_ref, g_ref, gn_ref, g1_ref, w_ref,
             dout_ref, dgate_ref, do_ref, dg1_ref, dgn_ref):
        @pl.when(pl.program_id(0) == 0)
        def _():
            dg1_ref[...] = jnp.zeros_like(dg1_ref)
            dgn_ref[...] = jnp.zeros_like(dgn_ref)

        g1 = g1_ref[...]
        dxo = dxo_ref[...]
        _, oh_, r_ = _rms_fwd(out_ref[...], g1)
        dg1_ref[...] += _colsum(dxo * oh_)
        doutb = _rms_bwd(oh_, r_, g1, dxo).astype(BF16)
        dout_ref[...] = doutb
        dy = _dot_nt(doutb, w_ref[...])
        for h in range(RET_HEADS):
            cols = slice(h * RET_V, (h + 1) * RET_V)
            gn = gn_ref[:, cols]
            ohat, rstd = _group_norm(o_ref[:, cols])
            g = g_ref[:, cols].astype(F32)
            sg = jax.nn.sigmoid(g)
            dyh = dy[:, cols]
            dgate_ref[:, cols] = (dyh * (ohat * gn) * (sg * (1.0 + g * (1.0 - sg)))).astype(BF16)
            don = dyh * (g * sg)
            dgn_ref[:, cols] += _colsum(don * ohat)
            dohat = don * gn
            do_ref[:, cols] = (rstd * (dohat - jnp.mean(dohat, axis=-1, keepdims=True)
                                       - ohat * jnp.mean(dohat * ohat, axis=-1, keepdims=True))).astype(BF16)

    row = pl.BlockSpec((tm, d), lambda i: (i, 0))
    wide = pl.BlockSpec((tm, v_all), lambda i: (i, 0))
    gate = pl.BlockSpec((tm, v_all), lambda i: (i, 2))
    return pl.pallas_call(
        body, grid=(nt,),
        in_specs=[row, row, wide, gate, _full((1, v_all)), _full((1, d)), _full(w_out.shape)],
        out_specs=[row, gate, wide, _full((1, d)), _full((1, v_all))],
        out_shape=[jax.ShapeDtypeStruct((t, d), BF16), jax.ShapeDtypeStruct((t, RET_IN), BF16),
                   jax.ShapeDtypeStruct((t, v_all), BF16), jax.ShapeDtypeStruct((1, d), F32),
                   jax.ShapeDtypeStruct((1, v_all), F32)],
        compiler_params=_params(1), name="ret_out_bwd")(dxo, out, o, pj, gn_gain, g_post, w_out)


def ret_core_bwd(pj, do, sprev, cos, sin, dpj, intra, cross, inner, chunk_decay):
    t = pj.shape[0]
    nc = t // RET_CHUNK
    c = RET_CHUNK
    qk_all = RET_HEADS * RET_QK
    v_all = RET_HEADS * RET_V
    scale = RET_QK ** -0.5

    def body(q_ref, k_ref, v_ref, do_ref, sp_ref, c_ref, s_ref, dm_ref, cr_ref, in_ref, dpj_in, dpj_ref, dstate):
        @pl.when(pl.program_id(0) == 0)
        def _():
            dstate[...] = jnp.zeros_like(dstate)

        cs = c_ref[...]
        sn = s_ref[...]
        for h in range(RET_HEADS):
            q = q_ref[:, h * RET_QK:(h + 1) * RET_QK]
            k = k_ref[:, h * RET_QK:(h + 1) * RET_QK]
            v = v_ref[:, h * RET_V:(h + 1) * RET_V]
            doh = do_ref[:, h * RET_V:(h + 1) * RET_V]
            dm = dm_ref[h]
            ab = (_dot_nt(q, k) * dm).astype(BF16)
            dab = (_dot_nt(doh, v) * dm).astype(BF16)
            dsb = dstate[h].astype(BF16)
            kd = (k.astype(F32) * in_ref[h]).astype(BF16)
            dv = _dot_tn(ab, doh) + _dot(kd, dsb)
            dq = _dot(dab, k) + cr_ref[h] * _dot_nt(doh, sp_ref[h])
            dk = _dot_tn(dab, q) + in_ref[h] * _dot_nt(v, dsb)
            qd = (q.astype(F32) * cr_ref[h]).astype(BF16)
            dstate[h] = dstate[h] * chunk_decay[h] + _dot_tn(qd, doh)
            for base, dd, sc in ((h * RET_QK, dq, scale), (qk_all + h * RET_QK, dk, 1.0)):
                d1, d2 = dd[:, :128], dd[:, 128:]
                dpj_ref[:, base:base + 128] = ((d1 * cs + d2 * sn) * sc).astype(BF16)
                dpj_ref[:, base + 128:base + RET_QK] = ((d2 * cs - d1 * sn) * sc).astype(BF16)
            dpj_ref[:, 2 * qk_all + h * RET_V:2 * qk_all + (h + 1) * RET_V] = dv.astype(BF16)

    rev = lambda n: nc - 1 - n
    tab = pl.BlockSpec((c, 128), lambda n: (rev(n), 0))
    return pl.pallas_call(
        body, grid=(nc,),
        in_specs=[pl.BlockSpec((c, qk_all), lambda n: (rev(n), 0)), pl.BlockSpec((c, qk_all), lambda n: (rev(n), 1)),
                  pl.BlockSpec((c, v_all), lambda n: (rev(n), 1)), pl.BlockSpec((c, v_all), lambda n: (rev(n), 0)),
                  pl.BlockSpec((None, RET_HEADS, RET_QK, RET_V), lambda n: (rev(n), 0, 0, 0)),
                  tab, tab, _full(intra.shape), _full(cross.shape), _full(inner.shape), ANY],
        out_specs=pl.BlockSpec((c, 2 * qk_all + v_all), lambda n: (rev(n), 0)),
        out_shape=jax.ShapeDtypeStruct((t, RET_IN), BF16),
        scratch_shapes=[pltpu.VMEM((RET_HEADS, RET_QK, RET_V), F32)],
        input_output_aliases={10: 0},
        compiler_params=_params(1), name="ret_core_bwd")(pj, pj, pj, do, sprev, cos, sin, intra, cross, inner, dpj)


def ret_in_bwd(dpj, dres, x, g_pre, w_in, tm=256):
    t, d = x.shape
    nt = t // tm

    def body(dpj_ref, dres_ref, x_ref, g_ref, win_hbm, dx_ref, dg_ref, win_v, sem):
        @pl.when(pl.program_id(0) == 0)
        def _():
            cp = pltpu.make_async_copy(win_hbm, win_v, sem)
            cp.start()
            dg_ref[...] = jnp.zeros_like(dg_ref)
            cp.wait()

        g = g_ref[...]
        dh = jnp.zeros((tm, d), F32)
        for j in range(N_SHARD):
            dh = dh + _dot_nt(dpj_ref[:, j * RET_IN_SHARD:(j + 1) * RET_IN_SHARD], win_v[j])
        _, xh, rx = _rms_fwd(x_ref[...], g)
        dg_ref[...] += _colsum(dh * xh)
        dx_ref[...] = dres_ref[...] + _rms_bwd(xh, rx, g, dh)

    row = pl.BlockSpec((tm, d), lambda i: (i, 0))
    return pl.pallas_call(
        body, grid=(nt,),
        in_specs=[pl.BlockSpec((tm, RET_IN), lambda i: (i, 0)), row, row, _full((1, d)), ANY],
        out_specs=[row, _full((1, d))],
        out_shape=[jax.ShapeDtypeStruct((t, d), F32), jax.ShapeDtypeStruct((1, d), F32)],
        scratch_shapes=[pltpu.VMEM(w_in.shape, BF16), pltpu.SemaphoreType.DMA],
        compiler_params=_params(1), name="ret_in_bwd")(dpj, dres, x, g_pre, w_in)


def loss_head(y, target, tm=512):
    t, d = y.shape

    def body(y_ref, t_ref, l_ref, dy_ref):
        @pl.when(pl.program_id(0) == 0)
        def _():
            l_ref[...] = jnp.zeros_like(l_ref)

        err = y_ref[...] - t_ref[...]
        dy_ref[...] = err * (1.0 / d)
        l_ref[...] += 0.5 * jnp.sum(jnp.mean(err * err, axis=-1, keepdims=True), axis=0, keepdims=True)

    row = pl.BlockSpec((tm, d), lambda i: (i, 0))
    return pl.pallas_call(
        body, grid=(t // tm,), in_specs=[row, row], out_specs=[_full((1, 1)), row],
        out_shape=[jax.ShapeDtypeStruct((1, 1), F32), jax.ShapeDtypeStruct((t, d), F32)],
        compiler_params=_params(1), name="loss_head")(y, target)


_CHIP_FLIPS = ((1, 0), (0, 1), (1, 1))


def _flip(v, b):
    return 1 - v if b else v


def scatter_grads(big, small):
    n = len(big)

    def body(*refs):
        ins, small_in = refs[:n], refs[n]
        outs, small_out = refs[n + 1:2 * n + 1], refs[2 * n + 1]
        send_sems, recv_sems, ssend_sems, srecv_sems, local_sem = refs[2 * n + 2:]
        x, y, c = lax.axis_index("x"), lax.axis_index("y"), lax.axis_index("c")
        mine = 4 * x + 2 * y + c
        copies = [pltpu.make_async_copy(small_in, small_out.at[mine], local_sem)]
        copies[0].start()
        for m in range(1, 8):
            bx, by, bc = (m >> 2) & 1, (m >> 1) & 1, m & 1
            cp = pltpu.make_async_remote_copy(
                src_ref=small_in, dst_ref=small_out.at[mine], send_sem=ssend_sems.at[m - 1],
                recv_sem=srecv_sems.at[m - 1], device_id=(_flip(x, bx), _flip(y, by), _flip(c, bc)),
                device_id_type=MESH)
            cp.start()
            copies.append(cp)
        for t in range(n):
            for k, (bx, by) in enumerate(_CHIP_FLIPS):
                px, py = _flip(x, bx), _flip(y, by)
                cp = pltpu.make_async_remote_copy(
                    src_ref=ins[t].at[2 * px + py], dst_ref=outs[t].at[k], send_sem=send_sems.at[3 * t + k],
                    recv_sem=recv_sems.at[3 * t + k], device_id=(px, py, c), device_id_type=MESH)
                cp.start()
                copies.append(cp)
        for cp in copies:
            cp.wait()

    return pl.pallas_call(
        body, in_specs=[ANY] * (n + 1), out_specs=[ANY] * (n + 1),
        out_shape=[jax.ShapeDtypeStruct((3,) + b.shape[1:], b.dtype) for b in big]
        + [jax.ShapeDtypeStruct((8,) + small.shape, small.dtype)],
        scratch_shapes=[pltpu.SemaphoreType.DMA((3 * n,)), pltpu.SemaphoreType.DMA((3 * n,)),
                        pltpu.SemaphoreType.DMA((7,)), pltpu.SemaphoreType.DMA((7,)), pltpu.SemaphoreType.DMA],
        name="scatter_grads")(*big, small)


def swap_cores(arrays, name):
    n = len(arrays)

    def body(*refs):
        ins, outs = refs[:n], refs[n:2 * n]
        send_sems, recv_sems = refs[2 * n:]
        sibling = (lax.axis_index("x"), lax.axis_index("y"), 1 - lax.axis_index("c"))
        copies = []
        for t in range(n):
            cp = pltpu.make_async_remote_copy(
                src_ref=ins[t], dst_ref=outs[t], send_sem=send_sems.at[t], recv_sem=recv_sems.at[t],
                device_id=sibling, device_id_type=MESH)
            cp.start()
            copies.append(cp)
        for cp in copies:
            cp.wait()

    return pl.pallas_call(
        body, in_specs=[ANY] * n, out_specs=[ANY] * n,
        out_shape=[jax.ShapeDtypeStruct(a.shape, a.dtype) for a in arrays],
        scratch_shapes=[pltpu.SemaphoreType.DMA((n,)), pltpu.SemaphoreType.DMA((n,))],
        name=name)(*arrays)


_HBM = pl.BlockSpec(memory_space=pltpu.HBM)
_SEM = pl.BlockSpec(memory_space=pltpu.SEMAPHORE)
_EFFECT = pltpu.SideEffectType.DATAFLOW_SIDE_EFFECTING


def _chip_copies(mode, srcs, lands, send_sems, recv_sems):
    x, y, c = lax.axis_index("x"), lax.axis_index("y"), lax.axis_index("c")
    copies = []
    for t in range(len(srcs)):
        for k, (bx, by) in enumerate(_CHIP_FLIPS):
            px, py = _flip(x, bx), _flip(y, by)
            if mode == "gather":
                src, dst = srcs[t], lands[t].at[2 * x + y]
            else:
                src, dst = srcs[t].at[2 * px + py], lands[t].at[k]
            copies.append(pltpu.make_async_remote_copy(
                src_ref=src, dst_ref=dst, send_sem=send_sems.at[3 * t + k], recv_sem=recv_sems.at[3 * t + k],
                device_id=(px, py, c), device_id_type=MESH))
    return copies


def exchange_start(mode, srcs, lands, name, after=None):
    n = len(srcs)
    extra = [] if after is None else [after]

    def body(*refs):
        ins, lnd = refs[:n], refs[n:2 * n]
        send_sems, recv_sems = refs[2 * n + len(extra)], refs[2 * n + len(extra) + 1]
        token = refs[-1]
        for cp in _chip_copies(mode, ins, lnd, send_sems, recv_sems):
            cp.start()
        token[...] = jnp.zeros(token.shape, token.dtype)

    hbm = lambda a: pltpu.with_memory_space_constraint(a, pltpu.HBM)
    passed = list(srcs) + list(lands)
    return pl.pallas_call(
        body, name=name,
        out_shape=(pltpu.SemaphoreType.DMA((3 * n,)), pltpu.SemaphoreType.DMA((3 * n,)),
                   *[pltpu.HBM(a.shape, a.dtype) for a in passed], jax.ShapeDtypeStruct((8, 128), F32)),
        in_specs=[_HBM] * (2 * n) + [ANY] * len(extra),
        out_specs=(_SEM, _SEM, *[_HBM] * (2 * n), pl.BlockSpec(memory_space=pltpu.VMEM)),
        input_output_aliases={i: 2 + i for i in range(2 * n)},
        compiler_params=pltpu.CompilerParams(has_side_effects=_EFFECT))(*[hbm(a) for a in passed], *extra)


def exchange_wait(mode, started, after, name):
    send_sems, recv_sems = started[0], started[1]
    passed = list(started[2:-1])
    n = len(passed) // 2

    def body(*refs):
        ins, lnd = refs[:n], refs[n:2 * n]
        for cp in _chip_copies(mode, ins, lnd, refs[2 * n], refs[2 * n + 1]):
            cp.wait_send()
            cp.wait_recv()

    outs = pl.pallas_call(
        body, name=name, out_shape=tuple(pltpu.HBM(a.shape, a.dtype) for a in passed),
        in_specs=[_HBM] * (2 * n) + [_SEM, _SEM, ANY], out_specs=tuple([_HBM] * (2 * n)),
        input_output_aliases={i: i for i in range(2 * n)},
        compiler_params=pltpu.CompilerParams(has_side_effects=_EFFECT))(*passed, send_sems, recv_sems, after)
    return list(outs[n:])


def plane_sum(own, recv, name, bm=256):
    m, n = own.shape
    bm = _row_block(m, bm)

    def body(o_ref, r_ref, s_ref):
        s_ref[...] = ((o_ref[...] + r_ref[0].astype(F32)) + r_ref[1].astype(F32)) + r_ref[2].astype(F32)

    return pl.pallas_call(
        body, grid=(m // bm,),
        in_specs=[pl.BlockSpec((bm, n), lambda i: (i, 0)), pl.BlockSpec((3, bm, n), lambda i: (0, i, 0))],
        out_specs=pl.BlockSpec((bm, n), lambda i: (i, 0)), out_shape=jax.ShapeDtypeStruct((m, n), F32),
        compiler_params=_params(1), name=name)(own, recv)


def sum_slots(parts, name, bm=312):
    _, r, n = parts.shape
    bm = bm if r % bm == 0 else r

    def body(p_ref, s_ref):
        acc = p_ref[0]
        for k in range(1, 8):
            acc = acc + p_ref[k]
        s_ref[...] = acc

    return pl.pallas_call(
        body, grid=(r // bm,), in_specs=[pl.BlockSpec((8, bm, n), lambda i: (0, i, 0))],
        out_specs=pl.BlockSpec((bm, n), lambda i: (i, 0)), out_shape=jax.ShapeDtypeStruct((r, n), F32),
        compiler_params=_params(1), name=name)(parts)


def _adamw_math(w, g, m, v):
    m = ADAM_B1 * m + (1.0 - ADAM_B1) * g
    v = ADAM_B2 * v + (1.0 - ADAM_B2) * (g * g)
    m_hat = m / (1.0 - ADAM_B1 ** ADAM_STEP)
    v_hat = v / (1.0 - ADAM_B2 ** ADAM_STEP)
    delta = -ADAM_LR * (m_hat / (jnp.sqrt(v_hat) + ADAM_EPS) + ADAM_WD * w)
    return delta, m, v


def adamw(w, m, v, grads, layer, prev, name, bm=256):
    _, mm, n = w.shape
    bm = _row_block(mm, bm)
    ng = len(grads)

    def body(*refs):
        w_ref, m_ref, v_ref = refs[:3]
        g_refs = refs[3:3 + ng]
        g_out, d_out, m_out, v_out = refs[-4:]
        g = g_refs[0][...]
        for gr in g_refs[1:]:
            g = g + gr[...]
        delta, mn, vn = _adamw_math(w_ref[...], g, m_ref[...], v_ref[...])
        g_out[...] = g
        d_out[...] = delta
        m_out[...] = mn
        v_out[...] = vn

    slab = pl.BlockSpec((None, bm, n), lambda i: (layer, i, 0))
    flat = pl.BlockSpec((bm, n), lambda i: (i, 0))
    in_specs = [slab] * 3 + [flat] * ng
    args = [w, m, v, *grads]
    aliases = {}
    if prev is not None:
        in_specs += [ANY] * 4
        aliases = {3 + ng + q: q for q in range(4)}
        args += list(prev)
    return pl.pallas_call(
        body, grid=(mm // bm,), in_specs=in_specs, out_specs=[slab] * 4,
        out_shape=[jax.ShapeDtypeStruct(w.shape, F32)] * 4, input_output_aliases=aliases,
        compiler_params=_params(1), name=name)(*args)


def _pack_rows(parts, rows):
    flat = jnp.concatenate([p.reshape(-1) for p in parts])
    return jnp.pad(flat, (0, rows * 128 - flat.shape[0])).reshape(rows, 128)


def _as_shards(a, rows):
    return a.reshape(N_SHARD, rows, a.shape[-1])


def _local_step(x, pos_col, target, gains, pool_w, pool_scale, gn_gain, conv_w, conv_b, weights, send_grads):
    def gain(l, n, token=None):
        g = gains[l, n].reshape(1, D_MODEL)
        return g if token is None else g + token[0:1, 0:1]

    inv_freq = (ROPE_BASE ** (-jnp.arange(0, RET_QK, 2, dtype=F32) / RET_QK)).reshape(1, RET_QK // 2)
    cos, sin = rope_tables(pos_col, inv_freq)
    intra, cross, inner, chunk_decay = _decay_tables()
    dn_rows = D_FF // N_SHARD

    x1 = pool_fwd(x, gain(0, 0), gain(0, 1), pool_w, pool_scale)
    w_up0, w_dn0 = weights("mlp0", x1)
    w_dn0 = w_dn0.reshape(D_FF, D_MODEL)
    x2, u0, c0, f0, h0 = mlp_fwd(x1, gain(0, 2), gain(0, 3), w_up0, w_dn0, conv_w[0], conv_b[0])
    w_in, w_out = weights("ret", x2)
    w_out = w_out.reshape(RET_HEADS * RET_V, D_MODEL)
    pj, hr = ret_proj(x2, gain(1, 0), w_in, cos, sin)
    o, sprev = ret_core_fwd(pj, intra, cross, inner, chunk_decay)
    x3, yb, out = ret_out_fwd(o, pj, x2, gn_gain, gain(1, 1), w_out)
    w_up1, w_dn1 = weights("mlp1", x3)
    w_dn1 = w_dn1.reshape(D_FF, D_MODEL)
    x4, u1, c1, f1, h1 = mlp_fwd(x3, gain(1, 2), gain(1, 3), w_up1, w_dn1, conv_w[1], conv_b[1])
    loss, dx4 = loss_head(x4, target)

    dx3, du1, a1, df1, dg12, dg13, dcw1, dcb1 = mlp_bwd(
        dx4, f1, x3, u1, c1, gain(1, 2), gain(1, 3), w_up1, w_dn1, conv_w[1])
    dwup1 = grad_matmul(h1, du1, D_MODEL, FF_CHUNK, "grad_w_up_1", interleaved=True)
    dwdn1 = grad_matmul(a1, df1, FF_CHUNK, D_MODEL, "grad_w_down_1")
    tok = send_grads("mlp1", [dwup1, [_as_shards(g, dn_rows) for g in dwdn1]])
    dout, dpj, do, dg11, dgn = ret_out_bwd(dx3, out, o, pj, gn_gain, gain(1, 1, tok), w_out)
    dwout = grad_matmul(yb, dout, 1024, D_MODEL, "grad_w_out")
    dpj = ret_core_bwd(pj, do, sprev, cos, sin, dpj, intra, cross, inner, chunk_decay)
    dwin = grad_matmul(hr, dpj, D_MODEL, RET_IN_SHARD, "grad_w_in")
    tok = send_grads("ret", [dwin, [_as_shards(g, RET_V) for g in dwout]])
    dx2, dg10 = ret_in_bwd(dpj, dx3, x2, gain(1, 0, tok), w_in)
    dx1, du0, a0, df0, dg02, dg03, dcw0, dcb0 = mlp_bwd(
        dx2, f0, x1, u0, c0, gain(0, 2), gain(0, 3), w_up0, w_dn0, conv_w[0])
    dwup0 = grad_matmul(h0, du0, D_MODEL, FF_CHUNK, "grad_w_up_0", interleaved=True)
    dwdn0 = grad_matmul(a0, df0, FF_CHUNK, D_MODEL, "grad_w_down_0")
    tok = send_grads("mlp0", [dwup0, [_as_shards(g, dn_rows) for g in dwdn0]])
    dx0, dg00, dg01, dpscale, dpw = pool_bwd(dx1, x, gain(0, 0, tok), gain(0, 1), pool_w, pool_scale)

    rows = lambda g: jnp.sum(g, axis=0, keepdims=True)
    dgains = jnp.concatenate([dg00, dg01, rows(dg02), rows(dg03), dg10, dg11, rows(dg12), rows(dg13)],
                             axis=0).reshape(2, 4, D_MODEL)
    small = {"gains": dgains, "pool_scale": dpscale, "gn": dgn,
             "conv_w": jnp.sum(jnp.stack([dcw0, dcw1]), axis=3),
             "conv_b": jnp.sum(jnp.stack([dcb0, dcb1]), axis=2, keepdims=True), "pool_w": dpw}
    return loss, dx0, small


def kernel(x, positions, norm_gain, pool_w, pool_scale, ret_w_in, ret_gn_gain, ret_w_out, mlp_w_up, mlp_conv_w, mlp_conv_b, mlp_w_down, loss_target, m_norm_gain, m_pool_w, m_pool_scale, m_ret_w_in, m_ret_gn_gain, m_ret_w_out, m_mlp_w_up, m_mlp_conv_w, m_mlp_conv_b, m_mlp_w_down, v_norm_gain, v_pool_w, v_pool_scale, v_ret_w_in, v_ret_gn_gain, v_ret_w_out, v_mlp_w_up, v_mlp_conv_w, v_mlp_conv_b, v_mlp_w_down):
    t = x.shape[1]
    me = 2 * lax.axis_index("x") + lax.axis_index("y")

    small_parts = [norm_gain, ret_gn_gain, mlp_conv_w, pool_w]
    small_sizes = [p.size for p in small_parts]
    small_rows = -(-sum(small_sizes) // (128 * 8)) * 8
    groups = {"small": [_pack_rows(small_parts, small_rows)],
              "mlp0": [mlp_w_up[0].astype(BF16), mlp_w_down[0].astype(BF16)],
              "ret": [ret_w_in[0].astype(BF16), ret_w_out[0].astype(BF16)],
              "mlp1": [mlp_w_up[1].astype(BF16), mlp_w_down[1].astype(BF16)]}
    gathers, token = {}, None
    for group, srcs in groups.items():
        lands = [lax.dynamic_update_index_in_dim(lax.empty((N_SHARD,) + s.shape, s.dtype), s, me, 0) for s in srcs]
        gathers[group] = exchange_start("gather", srcs, lands, "gather_start_" + group, after=token)
        token = gathers[group][-1]

    def weights(group, after):
        return exchange_wait("gather", gathers[group], after, "gather_wait_" + group)

    sent = {}

    def send_grads(group, pairs):
        lands = [lax.empty((3,) + b.shape[1:], BF16) for _, b in pairs]
        sent[group] = (exchange_start("scatter", [b for _, b in pairs], lands, "scatter_start_" + group),
                       [f for f, _ in pairs])
        return sent[group][0][-1]

    def reduced(group, after, names):
        started, own = sent[group]
        recv = exchange_wait("scatter", started, after, "scatter_wait_" + group)
        return [plane_sum(lax.dynamic_index_in_dim(f, me, 0, keepdims=False), r, "plane_sum_" + nm)
                for f, r, nm in zip(own, recv, names)]

    (smallg,) = weights("small", token)
    smallg = smallg.reshape(N_SHARD, -1)
    offs = np.cumsum([0] + small_sizes)
    piece = lambda i, shape: smallg[:, offs[i]:offs[i + 1]].reshape((N_SHARD,) + shape)
    gains = piece(0, (2, 4, 256)).transpose(1, 2, 0, 3).reshape(2, 4, D_MODEL)
    gn_full = piece(1, (512,)).reshape(1, RET_HEADS * RET_V)
    cw_full = piece(2, (2, 3, FF_CHUNK)).transpose(1, 0, 2, 3)
    pw_full = piece(3, (4, 64, 256)).transpose(1, 0, 2, 3).reshape(4, 256, 256).astype(BF16)
    cb_full = mlp_conv_b.reshape(2, N_SHARD, 1, FF_CHUNK)

    loss, dx0, small = _local_step(
        x[0], positions.reshape(t, 1).astype(F32), loss_target[0], gains, pw_full, pool_scale, gn_full,
        cw_full, cb_full, weights, send_grads)

    def small_adamw(w, m, v, grads, name):
        w3 = w.reshape(1, -1, w.shape[-1])
        out = adamw(w3, m.reshape(w3.shape), v.reshape(w3.shape), [g.reshape(w3.shape[1:]) for g in grads], 0, None, name)
        return [o.reshape(w.shape) for o in out]

    res = {}
    planes_a = reduced("mlp1", dx0, ["w_up_1", "w_down_1"]) + reduced("ret", dx0, ["w_in", "w_out"])
    others_a = swap_cores(planes_a, "swap_cores_a")
    res["ret_w_in"] = adamw(ret_w_in, m_ret_w_in, v_ret_w_in, (planes_a[2], others_a[2]), 0, None, "adamw_w_in")
    res["ret_w_out"] = adamw(ret_w_out, m_ret_w_out, v_ret_w_out, (planes_a[3], others_a[3]), 0, None, "adamw_w_out")
    up1 = adamw(mlp_w_up, m_mlp_w_up, v_mlp_w_up, (planes_a[0], others_a[0]), 1, None, "adamw_w_up_1")
    dn1 = adamw(mlp_w_down, m_mlp_w_down, v_mlp_w_down, (planes_a[1], others_a[1]), 1, None, "adamw_w_down_1")

    pw_f = small["pool_w"].reshape(4, N_SHARD, 64, 256).transpose(1, 0, 2, 3).reshape(N_SHARD, 256, 256)
    small_order = ["gains", "pool_scale", "gn", "conv_w", "conv_b"]
    gsmall_sizes = [small[k].size for k in small_order]
    gsmall_rows = -(-sum(gsmall_sizes) // (128 * 8)) * 8
    planes_b = reduced("mlp0", dn1[0], ["w_up_0", "w_down_0"])
    pw_recv, small_recv = scatter_grads([pw_f.astype(BF16)], _pack_rows([small[k] for k in small_order], gsmall_rows))
    planes_b.append(plane_sum(lax.dynamic_index_in_dim(pw_f, me, 0, keepdims=False), pw_recv, "plane_sum_pool_w"))
    others_b = swap_cores(planes_b, "swap_cores_b")
    res["mlp_w_up"] = adamw(mlp_w_up, m_mlp_w_up, v_mlp_w_up, (planes_b[0], others_b[0]), 0, up1, "adamw_w_up_0")
    res["mlp_w_down"] = adamw(mlp_w_down, m_mlp_w_down, v_mlp_w_down, (planes_b[1], others_b[1]), 0, dn1,
                              "adamw_w_down_0")
    res["pool_w"] = small_adamw(pool_w, m_pool_w, v_pool_w, (planes_b[2], others_b[2]), "adamw_pool_w")

    gsmall = sum_slots(small_recv, "sum_small").reshape(-1)
    goffs = np.cumsum([0] + gsmall_sizes)
    gpiece = lambda i: gsmall[goffs[i]:goffs[i + 1]].reshape(small[small_order[i]].shape)
    g_gains = lax.dynamic_slice_in_dim(gpiece(0), me * 256, 256, axis=2)
    g_gn = lax.dynamic_slice_in_dim(gpiece(2), me * RET_V, RET_V, axis=1)
    g_cw = lax.dynamic_index_in_dim(gpiece(3), me, 1, keepdims=False)
    res["norm_gain"] = small_adamw(norm_gain, m_norm_gain, v_norm_gain, [g_gains], "adamw_norm_gain")
    res["pool_scale"] = small_adamw(pool_scale, m_pool_scale, v_pool_scale, [gpiece(1)], "adamw_pool_scale")
    res["ret_gn_gain"] = small_adamw(ret_gn_gain, m_ret_gn_gain, v_ret_gn_gain, [g_gn], "adamw_gn_gain")
    res["mlp_conv_w"] = small_adamw(mlp_conv_w, m_mlp_conv_w, v_mlp_conv_w, [g_cw], "adamw_conv_w")
    res["mlp_conv_b"] = small_adamw(mlp_conv_b, m_mlp_conv_b, v_mlp_conv_b, [gpiece(4)], "adamw_conv_b")

    order = ["norm_gain", "pool_w", "pool_scale", "ret_w_in", "ret_gn_gain", "ret_w_out", "mlp_w_up", "mlp_conv_w",
             "mlp_conv_b", "mlp_w_down"]
    total_loss = lax.psum(loss[0, 0], ("x", "y", "c"))
    outs = [total_loss, dx0.reshape(x.shape)]
    for q in range(4):
        outs += [res[k][q] for k in order]
    return tuple(outs)
```

```python
import numpy as np
import jax
import jax.numpy as jnp
from jax import lax
from jax.experimental import pallas as pl
from jax.experimental.pallas import tpu as pltpu

F32 = jnp.float32
BF16 = jnp.bfloat16

D_MODEL = 1024
D_FF = 2816
FF_CHUNK = 1408
N_SHARD = 4
POOL_WINDOWS = (2, 4, 8, 16)
POOL_DIM = 256
POOL_HALO = 16
RET_HEADS = 4
RET_QK = 256
RET_V = 512
RET_CHUNK = 128
RET_IN = 6144
RET_IN_SHARD = 1536
ROPE_BASE = 10000.0
EPS = 1e-6
CONV_HALO = 8
MLP_PIECE = FF_CHUNK

ADAM_LR, ADAM_B1, ADAM_B2, ADAM_EPS, ADAM_WD, ADAM_STEP = 0.001, 0.9, 0.999, 1e-08, 0.01, 10

VMEM_LIMIT = 56 * 1024 * 1024
MESH = pl.DeviceIdType.MESH
ANY = pl.BlockSpec(memory_space=pl.ANY)


def _params(n_grid=1, limit=VMEM_LIMIT):
    return pltpu.CompilerParams(dimension_semantics=("arbitrary",) * n_grid, vmem_limit_bytes=limit)


def _dot(a, b):
    return jnp.dot(a, b, preferred_element_type=F32)


def _dot_nt(a, b):
    return lax.dot_general(a, b, (((1,), (1,)), ((), ())), preferred_element_type=F32)


def _dot_tn(a, b):
    return lax.dot_general(a, b, (((0,), (0,)), ((), ())), preferred_element_type=F32)


def _rms_fwd(x, gain):
    r = lax.rsqrt(jnp.mean(x * x, axis=-1, keepdims=True) + EPS)
    xh = x * r
    return xh * gain, xh, r


def _rms_bwd(xh, r, gain, dy):
    dxh = dy * gain
    return r * (dxh - xh * jnp.mean(dxh * xh, axis=-1, keepdims=True))


def _colsum(v):
    return jnp.sum(v, axis=0, keepdims=True)


def _full(shape):
    nd = len(shape)
    return pl.BlockSpec(shape, lambda *_: (0,) * nd)


def rope_tables(pos_col, inv_freq):
    t = pos_col.shape[0]
    tm = min(t, 1024)

    def body(p_ref, f_ref, c_ref, s_ref):
        ang = p_ref[...] * f_ref[...]
        c_ref[...] = jnp.cos(ang)
        s_ref[...] = jnp.sin(ang)

    return pl.pallas_call(
        body, grid=(t // tm,),
        in_specs=[pl.BlockSpec((tm, 1), lambda i: (i, 0)), _full((1, 128))],
        out_specs=[pl.BlockSpec((tm, 128), lambda i: (i, 0))] * 2,
        out_shape=[jax.ShapeDtypeStruct((t, 128), F32)] * 2,
        compiler_params=_params(1), name="rope_tables")(pos_col, inv_freq)


def _window_sums(ext, backward):
    n = ext.shape[0]
    cur, sums = ext, []
    for g, win in enumerate(POOL_WINDOWS):
        if g > 0:
            cur = cur[:, POOL_DIM:]
        half = win // 2
        cur = cur + pltpu.roll(cur, n - half if backward else half, axis=0)
        sums.append(cur[:, 0:POOL_DIM])
    return sums


def _pool_diff(h_halo, h, row0, tm):
    t_idx = row0 + lax.broadcasted_iota(jnp.int32, (tm, 1), 0)
    sums = _window_sums(jnp.concatenate([h_halo, h], axis=0), backward=False)
    parts, inv_counts = [], []
    for g, win in enumerate(POOL_WINDOWS):
        inv = 1.0 / jnp.minimum(t_idx + 1, win).astype(F32)
        parts.append(sums[g][POOL_HALO:, :] * inv - h[:, g * POOL_DIM:(g + 1) * POOL_DIM])
        inv_counts.append(inv)
    return parts, inv_counts


def pool_fwd(x, g_pre, g_post, pool_w, pool_scale, tm=256):
    t, d = x.shape
    nt = t // tm

    def body(x_ref, g0_ref, g1_ref, w_ref, sc_ref, o_ref, hext):
        i = pl.program_id(0)

        @pl.when(i == 0)
        def _():
            hext[...] = jnp.zeros((POOL_HALO, d), F32)

        xv = x_ref[...]
        h, _, _ = _rms_fwd(xv, g0_ref[...])
        parts, _ = _pool_diff(hext[...], h, i * tm, tm)
        hext[...] = h[tm - POOL_HALO:tm, :]
        ys = [_dot(parts[g].astype(BF16), w_ref[g]) for g in range(len(POOL_WINDOWS))]
        y = jnp.concatenate(ys, axis=-1) * sc_ref[...]
        m, _, _ = _rms_fwd(y, g1_ref[...])
        o_ref[...] = xv + m

    row = pl.BlockSpec((tm, d), lambda i: (i, 0))
    return pl.pallas_call(
        body, grid=(nt,),
        in_specs=[row, _full((1, d)), _full((1, d)), _full(pool_w.shape), _full((1, d))],
        out_specs=row, out_shape=jax.ShapeDtypeStruct((t, d), F32),
        scratch_shapes=[pltpu.VMEM((POOL_HALO, d), F32)],
        compiler_params=_params(1), name="pool_fwd")(x, g_pre, g_post, pool_w, pool_scale)


def pool_bwd(dx1, x, g_pre, g_post, pool_w, pool_scale, tm=256):
    t, d = x.shape
    nt = t // tm
    ng = len(POOL_WINDOWS)

    def body(dx1_ref, x_ref, xh_ref, g0_ref, g1_ref, w_ref, sc_ref,
             dx_ref, dg0_ref, dg1_ref, dsc_ref, dw_ref, enext):
        i = pl.program_id(0)
        r = nt - 1 - i

        @pl.when(i == 0)
        def _():
            enext[...] = jnp.zeros((POOL_HALO, d), F32)
            dg0_ref[...] = jnp.zeros_like(dg0_ref)
            dg1_ref[...] = jnp.zeros_like(dg1_ref)
            dsc_ref[...] = jnp.zeros_like(dsc_ref)
            dw_ref[...] = jnp.zeros_like(dw_ref)

        g0 = g0_ref[...]
        g1 = g1_ref[...]
        sc = sc_ref[...]
        xv = x_ref[...]
        h, xh, rx = _rms_fwd(xv, g0)
        h_halo, _, _ = _rms_fwd(xh_ref[...], g0)
        parts, inv_counts = _pool_diff(h_halo * jnp.where(r > 0, 1.0, 0.0), h, r * tm, tm)
        parts_b = [p.astype(BF16) for p in parts]
        ypre = jnp.concatenate([_dot(parts_b[g], w_ref[g]) for g in range(ng)], axis=-1)
        _, yh, ry = _rms_fwd(ypre * sc, g1)
        dm = dx1_ref[...]
        dg1_ref[...] += _colsum(dm * yh)
        dy = _rms_bwd(yh, ry, g1, dm)
        dsc_ref[...] += _colsum(dy * ypre)
        dyp = (dy * sc).astype(BF16)
        ddiffs = []
        for g in range(ng):
            cols = slice(g * POOL_DIM, (g + 1) * POOL_DIM)
            dw_ref[g] += _dot_tn(parts_b[g], dyp[:, cols])
            ddiffs.append(_dot_nt(dyp[:, cols], w_ref[g]))
        e = jnp.concatenate([ddiffs[g] * inv_counts[g] for g in range(ng)], axis=-1)
        sums = _window_sums(jnp.concatenate([e, enext[...]], axis=0), backward=True)
        enext[...] = e[0:POOL_HALO, :]
        dh = jnp.concatenate([sums[g][0:tm, :] - ddiffs[g] for g in range(ng)], axis=-1)
        dg0_ref[...] += _colsum(dh * xh)
        dx_ref[...] = dm + _rms_bwd(xh, rx, g0, dh)

    row = pl.BlockSpec((tm, d), lambda i: (nt - 1 - i, 0))
    halo = pl.BlockSpec((POOL_HALO, d), lambda i: (jnp.maximum((nt - 1 - i) * (tm // POOL_HALO) - 1, 0), 0))
    vec = _full((1, d))
    return pl.pallas_call(
        body, grid=(nt,),
        in_specs=[row, row, halo, vec, vec, _full(pool_w.shape), vec],
        out_specs=[row, vec, vec, vec, _full((ng, POOL_DIM, POOL_DIM))],
        out_shape=[jax.ShapeDtypeStruct((t, d), F32)] + [jax.ShapeDtypeStruct((1, d), F32)] * 3
        + [jax.ShapeDtypeStruct((ng, POOL_DIM, POOL_DIM), F32)],
        scratch_shapes=[pltpu.VMEM((POOL_HALO, d), F32)],
        compiler_params=_params(1), name="pool_bwd")(dx1, x, x, g_pre, g_post, pool_w, pool_scale)


def _conv_taps(cw_ref, j):
    return cw_ref[j, 0:1, :], cw_ref[j, 1:2, :], cw_ref[j, 2:3, :]


def _row_block(m, target=256):
    if m <= target:
        return m
    for b in range(target, 7, -8):
        if m % b == 0:
            return b
    return m


def mlp_fwd(x, g_pre, g_post, w_up, w_down, conv_w, conv_b, tm=256):
    t, d = x.shape
    nt = t // tm
    h8 = CONV_HALO

    def body(x_ref, g2_ref, g3_ref, wup_hbm, wdn_hbm, cw_ref, cb_ref,
             xo_ref, u_ref, c_ref, f_ref, h_ref, wup_v, wdn_v, tail, sem):
        i = pl.program_id(0)

        @pl.when(i == 0)
        def _():
            c1 = pltpu.make_async_copy(wup_hbm, wup_v, sem.at[0])
            c2 = pltpu.make_async_copy(wdn_hbm, wdn_v, sem.at[1])
            c1.start()
            c2.start()
            tail[...] = jnp.zeros_like(tail)
            c1.wait()
            c2.wait()

        xv = x_ref[...]
        h, _, _ = _rms_fwd(xv, g2_ref[...])
        hb = h.astype(BF16)
        h_ref[...] = hb
        acc = jnp.zeros((tm, d), F32)
        for k in range(2):
            cs = []
            for s in range(2):
                j, cols = k + 2 * s, slice((2 * k + s) * FF_CHUNK, (2 * k + s + 1) * FF_CHUNK)
                ub = _dot(hb, wup_v[j]).astype(BF16)
                u_ref[:, cols] = ub
                uf = ub.astype(F32)
                ext = jnp.concatenate([tail[j], uf], axis=0)
                tail[j] = uf[tm - h8:tm, :]
                w0, w1, w2 = _conv_taps(cw_ref, j)
                cb = (cb_ref[j] + w2 * uf + w1 * pltpu.roll(ext, 1, axis=0)[h8:, :]
                      + w0 * pltpu.roll(ext, 2, axis=0)[h8:, :]).astype(BF16)
                c_ref[:, cols] = cb
                cs.append(cb.astype(F32))
            a = cs[0] * jax.nn.sigmoid(cs[0]) * cs[1]
            acc = acc + _dot(a.astype(BF16), wdn_v[k * FF_CHUNK:(k + 1) * FF_CHUNK, :])
        f_ref[...] = acc
        y, _, _ = _rms_fwd(acc, g3_ref[...])
        xo_ref[...] = xv + y

    row = pl.BlockSpec((tm, d), lambda i: (i, 0))
    wide = pl.BlockSpec((tm, 2 * D_FF), lambda i: (i, 0))
    vec = _full((1, d))
    return pl.pallas_call(
        body, grid=(nt,),
        in_specs=[row, vec, vec, ANY, ANY, _full(conv_w.shape), _full(conv_b.shape)],
        out_specs=[row, wide, wide, row, row],
        out_shape=[jax.ShapeDtypeStruct((t, d), F32), jax.ShapeDtypeStruct((t, 2 * D_FF), BF16),
                   jax.ShapeDtypeStruct((t, 2 * D_FF), BF16),
                   jax.ShapeDtypeStruct((t, d), F32), jax.ShapeDtypeStruct((t, d), BF16)],
        scratch_shapes=[pltpu.VMEM(w_up.shape, BF16), pltpu.VMEM(w_down.shape, BF16),
                        pltpu.VMEM((N_SHARD, h8, FF_CHUNK), F32), pltpu.SemaphoreType.DMA((2,))],
        compiler_params=_params(1), name="mlp_fwd")(x, g_pre, g_post, w_up, w_down, conv_w, conv_b)


def _rowsum8(v):
    return jnp.sum(v.reshape(v.shape[0] // 8, 8, v.shape[1]), axis=0)


def mlp_bwd(dxo, f, x, u, c, g_pre, g_post, w_up, w_down, conv_w, tm=256):
    t, d = x.shape
    nt = t // tm
    h8 = CONV_HALO

    def body(dxo_ref, f_ref, x_ref, u_ref, c_ref, g2_ref, g3_ref, wup_hbm, wdn_hbm, cw_ref,
             dx_ref, du_ref, a_ref, df_ref, dg2_ref, dg3_ref, dcw_ref, dcb_ref,
             wup_v, wdn_v, dfb_s, dh_s, carry, sem):
        i = pl.program_id(0)
        k = pl.program_id(1)

        @pl.when((i == 0) & (k == 0))
        def _():
            c1 = pltpu.make_async_copy(wup_hbm, wup_v, sem.at[0])
            c2 = pltpu.make_async_copy(wdn_hbm, wdn_v, sem.at[1])
            c1.start()
            c2.start()
            carry[...] = jnp.zeros_like(carry)
            dg2_ref[...] = jnp.zeros_like(dg2_ref)
            dg3_ref[...] = jnp.zeros_like(dg3_ref)
            dcw_ref[...] = jnp.zeros_like(dcw_ref)
            dcb_ref[...] = jnp.zeros_like(dcb_ref)
            c1.wait()
            c2.wait()

        @pl.when(k == 0)
        def _():
            g3 = g3_ref[...]
            dxo = dxo_ref[...]
            _, fh, rf = _rms_fwd(f_ref[...], g3)
            dg3_ref[...] += _rowsum8(dxo * fh)
            dfb = _rms_bwd(fh, rf, g3, dxo).astype(BF16)
            df_ref[...] = dfb
            dfb_s[...] = dfb
            dh_s[...] = jnp.zeros_like(dh_s)

        dfb = dfb_s[...]
        dh = dh_s[...]
        for q0 in range(0, FF_CHUNK, MLP_PIECE):
            q1 = min(q0 + MLP_PIECE, FF_CHUNK)
            cg = c_ref[:, q0:q1].astype(F32)
            cv = c_ref[:, FF_CHUNK + q0:FF_CHUNK + q1].astype(F32)
            sg = jax.nn.sigmoid(cg)
            sil = cg * sg
            a_ref[:, q0:q1] = (sil * cv).astype(BF16)
            da = _dot_nt(dfb, wdn_v[pl.ds(pl.multiple_of(k * FF_CHUNK + q0, 128), q1 - q0), :])
            dcs = (da * cv * (sg * (1.0 + cg * (1.0 - sg))), da * sil)
            for s in range(2):
                j = k + 2 * s
                uf = u_ref[:, s * FF_CHUNK + q0:s * FF_CHUNK + q1].astype(F32)
                ext = jnp.concatenate([dcs[s], carry[j, :, q0:q1]], axis=0)
                carry[j, :, q0:q1] = dcs[s][0:h8, :]
                dc1 = pltpu.roll(ext, tm + h8 - 1, axis=0)[0:tm, :]
                dc2 = pltpu.roll(ext, tm + h8 - 2, axis=0)[0:tm, :]
                dcb_ref[j, :, q0:q1] += _rowsum8(dcs[s])
                dcw_ref[j, 2, :, q0:q1] += _rowsum8(dcs[s] * uf)
                dcw_ref[j, 1, :, q0:q1] += _rowsum8(dc1 * uf)
                dcw_ref[j, 0, :, q0:q1] += _rowsum8(dc2 * uf)
                dub = (cw_ref[j, 2:3, q0:q1] * dcs[s] + cw_ref[j, 1:2, q0:q1] * dc1
                       + cw_ref[j, 0:1, q0:q1] * dc2).astype(BF16)
                du_ref[:, s * FF_CHUNK + q0:s * FF_CHUNK + q1] = dub
                dh = dh + _dot_nt(dub, wup_v[j, :, q0:q1])
        dh_s[...] = dh

        @pl.when(k == 1)
        def _():
            g2 = g2_ref[...]
            _, xh, rx = _rms_fwd(x_ref[...], g2)
            dg2_ref[...] += _rowsum8(dh * xh)
            dx_ref[...] = dxo_ref[...] + _rms_bwd(xh, rx, g2, dh)

    row = pl.BlockSpec((tm, d), lambda i, k: (nt - 1 - i, 0))
    half = pl.BlockSpec((tm, 2 * FF_CHUNK), lambda i, k: (nt - 1 - i, k))
    vec = _full((1, d))
    acc = _full((8, d))
    dcw_shape, dcb_shape = (N_SHARD, 3, 8, FF_CHUNK), (N_SHARD, 8, FF_CHUNK)
    return pl.pallas_call(
        body, grid=(nt, 2),
        in_specs=[row, row, row, half, half, vec, vec, ANY, ANY, _full(conv_w.shape)],
        out_specs=[row, half, pl.BlockSpec((tm, FF_CHUNK), lambda i, k: (nt - 1 - i, k)), row, acc, acc,
                   _full(dcw_shape), _full(dcb_shape)],
        out_shape=[jax.ShapeDtypeStruct((t, d), F32), jax.ShapeDtypeStruct((t, 2 * D_FF), BF16),
                   jax.ShapeDtypeStruct((t, D_FF), BF16), jax.ShapeDtypeStruct((t, d), BF16),
                   jax.ShapeDtypeStruct((8, d), F32), jax.ShapeDtypeStruct((8, d), F32),
                   jax.ShapeDtypeStruct(dcw_shape, F32), jax.ShapeDtypeStruct(dcb_shape, F32)],
        scratch_shapes=[pltpu.VMEM(w_up.shape, BF16), pltpu.VMEM(w_down.shape, BF16),
                        pltpu.VMEM((tm, d), BF16), pltpu.VMEM((tm, d), F32),
                        pltpu.VMEM((N_SHARD, h8, FF_CHUNK), F32), pltpu.SemaphoreType.DMA((2,))],
        compiler_params=_params(2), name="mlp_bwd")(dxo, f, x, u, c, g_pre, g_post, w_up, w_down, conv_w)


def grad_matmul(a, b, bm, bn, name, tk=2048, interleaved=False):
    t, m = a.shape
    n = b.shape[1]
    tk = min(tk, t)
    nk = t // tk
    place = (lambda j: (j % 2) * 2 + j // 2) if interleaved else (lambda j: j)

    def body(a_ref, b_ref, o_ref, ob_ref):
        kk = pl.program_id(2)

        @pl.when(kk == 0)
        def _():
            o_ref[...] = jnp.zeros_like(o_ref)

        o_ref[...] += _dot_tn(a_ref[...], b_ref[...])

        @pl.when(kk == nk - 1)
        def _():
            ob_ref[...] = o_ref[...].astype(BF16)

    ospec = pl.BlockSpec((None, bm, bn), lambda j, i, kk: (place(j), i, 0))
    return pl.pallas_call(
        body, grid=(n // bn, m // bm, nk),
        in_specs=[pl.BlockSpec((tk, bm), lambda j, i, kk: (kk, i)), pl.BlockSpec((tk, bn), lambda j, i, kk: (kk, j))],
        out_specs=[ospec, ospec],
        out_shape=[jax.ShapeDtypeStruct((n // bn, m, bn), F32), jax.ShapeDtypeStruct((n // bn, m, bn), BF16)],
        compiler_params=_params(3), name=name)(a, b)


def _decay_tables():
    log_gamma = jnp.log(1.0 - 2.0 ** (-5.0 - jnp.arange(RET_HEADS, dtype=F32)))
    i = jnp.arange(RET_CHUNK, dtype=F32)
    rel = i[:, None] - i[None, :]
    intra = jnp.where(rel >= 0, jnp.exp(jnp.maximum(rel, 0.0) * log_gamma[:, None, None]), 0.0)
    cross = jnp.exp((i + 1.0) * log_gamma[:, None])[:, :, None]
    inner = jnp.exp((RET_CHUNK - 1.0 - i) * log_gamma[:, None])[:, :, None]
    chunk = [float(np.exp(np.float32(RET_CHUNK) * np.log(np.float32(1.0 - 2.0 ** (-5.0 - h))).astype(np.float32)))
             for h in range(RET_HEADS)]
    return intra, cross, inner, chunk


def ret_proj(x, g_pre, w_in, cos, sin, tm=256):
    t, d = x.shape
    nt = t // tm
    per = RET_IN_SHARD // RET_QK

    def body(x_ref, g_ref, win_hbm, c_ref, s_ref, pj_ref, h_ref, win_v, sem):
        @pl.when(pl.program_id(0) == 0)
        def _():
            cp = pltpu.make_async_copy(win_hbm, win_v, sem)
            cp.start()
            cp.wait()

        h, _, _ = _rms_fwd(x_ref[...], g_ref[...])
        hb = h.astype(BF16)
        h_ref[...] = hb
        c = c_ref[...]
        s = s_ref[...]
        for j in range(N_SHARD):
            pjj = _dot(hb, win_v[j])
            for bb in range(per):
                b = per * j + bb
                blk = pjj[:, bb * RET_QK:(bb + 1) * RET_QK]
                if b < 2 * RET_HEADS:
                    x1, x2 = blk[:, :128], blk[:, 128:]
                    o1 = x1 * c - x2 * s
                    o2 = x2 * c + x1 * s
                    if b < RET_HEADS:
                        o1 = o1 * (RET_QK ** -0.5)
                        o2 = o2 * (RET_QK ** -0.5)
                    pj_ref[:, b * RET_QK:b * RET_QK + 128] = o1.astype(BF16)
                    pj_ref[:, b * RET_QK + 128:(b + 1) * RET_QK] = o2.astype(BF16)
                else:
                    pj_ref[:, b * RET_QK:(b + 1) * RET_QK] = blk.astype(BF16)

    row = pl.BlockSpec((tm, d), lambda i: (i, 0))
    tab = pl.BlockSpec((tm, 128), lambda i: (i, 0))
    return pl.pallas_call(
        body, grid=(nt,),
        in_specs=[row, _full((1, d)), ANY, tab, tab],
        out_specs=[pl.BlockSpec((tm, RET_IN), lambda i: (i, 0)), row],
        out_shape=[jax.ShapeDtypeStruct((t, RET_IN), BF16), jax.ShapeDtypeStruct((t, d), BF16)],
        scratch_shapes=[pltpu.VMEM(w_in.shape, BF16), pltpu.SemaphoreType.DMA],
        compiler_params=_params(1), name="ret_proj")(x, g_pre, w_in, cos, sin)


def ret_core_fwd(pj, intra, cross, inner, chunk_decay):
    t = pj.shape[0]
    nc = t // RET_CHUNK
    c = RET_CHUNK
    qk_all = RET_HEADS * RET_QK
    v_all = RET_HEADS * RET_V

    def body(q_ref, k_ref, v_ref, dm_ref, cr_ref, in_ref, o_ref, sp_ref, state):
        @pl.when(pl.program_id(0) == 0)
        def _():
            state[...] = jnp.zeros_like(state)

        for h in range(RET_HEADS):
            q = q_ref[:, h * RET_QK:(h + 1) * RET_QK]
            k = k_ref[:, h * RET_QK:(h + 1) * RET_QK]
            v = v_ref[:, h * RET_V:(h + 1) * RET_V]
            sb = state[h].astype(BF16)
            sp_ref[h] = sb
            sc = _dot_nt(q, k) * dm_ref[h]
            o_ref[:, h * RET_V:(h + 1) * RET_V] = _dot(sc.astype(BF16), v) + _dot(q, sb) * cr_ref[h]
            kd = (k.astype(F32) * in_ref[h]).astype(BF16)
            state[h] = state[h] * chunk_decay[h] + _dot_tn(kd, v)

    return pl.pallas_call(
        body, grid=(nc,),
        in_specs=[pl.BlockSpec((c, qk_all), lambda n: (n, 0)), pl.BlockSpec((c, qk_all), lambda n: (n, 1)),
                  pl.BlockSpec((c, v_all), lambda n: (n, 1)),
                  _full(intra.shape), _full(cross.shape), _full(inner.shape)],
        out_specs=[pl.BlockSpec((c, v_all), lambda n: (n, 0)),
                   pl.BlockSpec((None, RET_HEADS, RET_QK, RET_V), lambda n: (n, 0, 0, 0))],
        out_shape=[jax.ShapeDtypeStruct((t, v_all), F32),
                   jax.ShapeDtypeStruct((nc, RET_HEADS, RET_QK, RET_V), BF16)],
        scratch_shapes=[pltpu.VMEM((RET_HEADS, RET_QK, RET_V), F32)],
        compiler_params=_params(1), name="ret_core_fwd")(pj, pj, pj, intra, cross, inner)


def _group_norm(o_h):
    mu = jnp.mean(o_h, axis=-1, keepdims=True)
    dev = o_h - mu
    rstd = lax.rsqrt(jnp.mean(dev * dev, axis=-1, keepdims=True) + EPS)
    return dev * rstd, rstd


def ret_out_fwd(o, pj, x, gn_gain, g_post, w_out, tm=256):
    t, d = x.shape
    nt = t // tm
    v_all = RET_HEADS * RET_V

    def body(o_ref, g_ref, x_ref, gn_ref, g1_ref, w_ref, xo_ref, y_ref, out_ref):
        for h in range(RET_HEADS):
            cols = slice(h * RET_V, (h + 1) * RET_V)
            ohat, _ = _group_norm(o_ref[:, cols])
            g = g_ref[:, cols].astype(F32)
            y_ref[:, cols] = (g * jax.nn.sigmoid(g) * (ohat * gn_ref[:, cols])).astype(BF16)
        out = _dot(y_ref[...], w_ref[...])
        out_ref[...] = out
        m, _, _ = _rms_fwd(out, g1_ref[...])
        xo_ref[...] = x_ref[...] + m

    row = pl.BlockSpec((tm, d), lambda i: (i, 0))
    wide = pl.BlockSpec((tm, v_all), lambda i: (i, 0))
    return pl.pallas_call(
        body, grid=(nt,),
        in_specs=[wide, pl.BlockSpec((tm, v_all), lambda i: (i, 2)), row, _full((1, v_all)), _full((1, d)),
                  _full(w_out.shape)],
        out_specs=[row, wide, row],
        out_shape=[jax.ShapeDtypeStruct((t, d), F32), jax.ShapeDtypeStruct((t, v_all), BF16),
                   jax.ShapeDtypeStruct((t, d), F32)],
        compiler_params=_params(1), name="ret_out_fwd")(o, pj, x, gn_gain, g_post, w_out)


def ret_out_bwd(dxo, out, o, pj, gn_gain, g_post, w_out, tm=256):
    t, d = out.shape
    nt = t // tm
    v_all = RET_HEADS * RET_V

    def body(dxo_ref, out_ref, o_ref, g_ref, gn_ref, g1_ref, w_ref,
             dout_ref, dgate_ref, do_ref, dg1_ref, dgn_ref):
        @pl.when(pl.program_id(0) == 0)
        def _():
            dg1_ref[...] = jnp.zeros_like(dg1_ref)
            dgn_ref[...] = jnp.zeros_like(dgn_ref)

        g1 = g1_ref[...]
        dxo = dxo_ref[...]
        _, oh_, r_ = _rms_fwd(out_ref[...], g1)
        dg1_ref[...] += _colsum(dxo * oh_)
        doutb = _rms_bwd(oh_, r_, g1, dxo).astype(BF16)
        dout_ref[...] = doutb
        dy = _dot_nt(doutb, w_ref[...])
        for h in range(RET_HEADS):
            cols = slice(h * RET_V, (h + 1) * RET_V)
            gn = gn_ref[:, cols]
            ohat, rstd = _group_norm(o_ref[:, cols])
            g = g_ref[:, cols].astype(F32)
            sg = jax.nn.sigmoid(g)
            dyh = dy[:, cols]
            dgate_ref[:, cols] = (dyh * (ohat * gn) * (sg * (1.0 + g * (1.0 - sg)))).astype(BF16)
            don = dyh * (g * sg)
            dgn_ref[:, cols] += _colsum(don * ohat)
            dohat = don * gn
            do_ref[:, cols] = (rstd * (dohat - jnp.mean(dohat, axis=-1, keepdims=True)
                                       - ohat * jnp.mean(dohat * ohat, axis=-1, keepdims=True))).astype(BF16)

    row = pl.BlockSpec((tm, d), lambda i: (i, 0))
    wide = pl.BlockSpec((tm, v_all), lambda i: (i, 0))
    gate = pl.BlockSpec((tm, v_all), lambda i: (i, 2))
    return pl.pallas_call(
        body, grid=(nt,),
        in_specs=[row, row, wide, gate, _full((1, v_all)), _full((1, d)), _full(w_out.shape)],
        out_specs=[row, gate, wide, _full((1, d)), _full((1, v_all))],
        out_shape=[jax.ShapeDtypeStruct((t, d), BF16), jax.ShapeDtypeStruct((t, RET_IN), BF16),
                   jax.ShapeDtypeStruct((t, v_all), BF16), jax.ShapeDtypeStruct((1, d), F32),
                   jax.ShapeDtypeStruct((1, v_all), F32)],
        compiler_params=_params(1), name="ret_out_bwd")(dxo, out, o, pj, gn_gain, g_post, w_out)


def ret_core_bwd(pj, do, sprev, cos, sin, dpj, intra, cross, inner, chunk_decay):
    t = pj.shape[0]
    nc = t // RET_CHUNK
    c = RET_CHUNK
    qk_all = RET_HEADS * RET_QK
    v_all = RET_HEADS * RET_V
    scale = RET_QK ** -0.5

    def body(q_ref, k_ref, v_ref, do_ref, sp_ref, c_ref, s_ref, dm_ref, cr_ref, in_ref, dpj_in, dpj_ref, dstate):
        @pl.when(pl.program_id(0) == 0)
        def _():
            dstate[...] = jnp.zeros_like(dstate)

        cs = c_ref[...]
        sn = s_ref[...]
        for h in range(RET_HEADS):
            q = q_ref[:, h * RET_QK:(h + 1) * RET_QK]
            k = k_ref[:, h * RET_QK:(h + 1) * RET_QK]
            v = v_ref[:, h * RET_V:(h + 1) * RET_V]
            doh = do_ref[:, h * RET_V:(h + 1) * RET_V]
            dm = dm_ref[h]
            ab = (_dot_nt(q, k) * dm).astype(BF16)
            dab = (_dot_nt(doh, v) * dm).astype(BF16)
            dsb = dstate[h].astype(BF16)
            kd = (k.astype(F32) * in_ref[h]).astype(BF16)
            dv = _dot_tn(ab, doh) + _dot(kd, dsb)
            dq = _dot(dab, k) + cr_ref[h] * _dot_nt(doh, sp_ref[h])
            dk = _dot_tn(dab, q) + in_ref[h] * _dot_nt(v, dsb)
            qd = (q.astype(F32) * cr_ref[h]).astype(BF16)
            dstate[h] = dstate[h] * chunk_decay[h] + _dot_tn(qd, doh)
            for base, dd, sc in ((h * RET_QK, dq, scale), (qk_all + h * RET_QK, dk, 1.0)):
                d1, d2 = dd[:, :128], dd[:, 128:]
                dpj_ref[:, base:base + 128] = ((d1 * cs + d2 * sn) * sc).astype(BF16)
                dpj_ref[:, base + 128:base + RET_QK] = ((d2 * cs - d1 * sn) * sc).astype(BF16)
            dpj_ref[:, 2 * qk_all + h * RET_V:2 * qk_all + (h + 1) * RET_V] = dv.astype(BF16)

    rev = lambda n: nc - 1 - n
    tab = pl.BlockSpec((c, 128), lambda n: (rev(n), 0))
    return pl.pallas_call(
        body, grid=(nc,),
        in_specs=[pl.BlockSpec((c, qk_all), lambda n: (rev(n), 0)), pl.BlockSpec((c, qk_all), lambda n: (rev(n), 1)),
                  pl.BlockSpec((c, v_all), lambda n: (rev(n), 1)), pl.BlockSpec((c, v_all), lambda n: (rev(n), 0)),
                  pl.BlockSpec((None, RET_HEADS, RET_QK, RET_V), lambda n: (rev(n), 0, 0, 0)),
                  tab, tab, _full(intra.shape), _full(cross.shape), _full(inner.shape), ANY],
        out_specs=pl.BlockSpec((c, 2 * qk_all + v_all), lambda n: (rev(n), 0)),
        out_shape=jax.ShapeDtypeStruct((t, RET_IN), BF16),
        scratch_shapes=[pltpu.VMEM((RET_HEADS, RET_QK, RET_V), F32)],
        input_output_aliases={10: 0},
        compiler_params=_params(1), name="ret_core_bwd")(pj, pj, pj, do, sprev, cos, sin, intra, cross, inner, dpj)


def ret_in_bwd(dpj, dres, x, g_pre, w_in, tm=256):
    t, d = x.shape
    nt = t // tm

    def body(dpj_ref, dres_ref, x_ref, g_ref, win_hbm, dx_ref, dg_ref, win_v, sem):
        @pl.when(pl.program_id(0) == 0)
        def _():
            cp = pltpu.make_async_copy(win_hbm, win_v, sem)
            cp.start()
            dg_ref[...] = jnp.zeros_like(dg_ref)
            cp.wait()

        g = g_ref[...]
        dh = jnp.zeros((tm, d), F32)
        for j in range(N_SHARD):
            dh = dh + _dot_nt(dpj_ref[:, j * RET_IN_SHARD:(j + 1) * RET_IN_SHARD], win_v[j])
        _, xh, rx = _rms_fwd(x_ref[...], g)
        dg_ref[...] += _colsum(dh * xh)
        dx_ref[...] = dres_ref[...] + _rms_bwd(xh, rx, g, dh)

    row = pl.BlockSpec((tm, d), lambda i: (i, 0))
    return pl.pallas_call(
        body, grid=(nt,),
        in_specs=[pl.BlockSpec((tm, RET_IN), lambda i: (i, 0)), row, row, _full((1, d)), ANY],
        out_specs=[row, _full((1, d))],
        out_shape=[jax.ShapeDtypeStruct((t, d), F32), jax.ShapeDtypeStruct((1, d), F32)],
        scratch_shapes=[pltpu.VMEM(w_in.shape, BF16), pltpu.SemaphoreType.DMA],
        compiler_params=_params(1), name="ret_in_bwd")(dpj, dres, x, g_pre, w_in)


def loss_head(y, target, tm=512):
    t, d = y.shape

    def body(y_ref, t_ref, l_ref, dy_ref):
        @pl.when(pl.program_id(0) == 0)
        def _():
            l_ref[...] = jnp.zeros_like(l_ref)

        err = y_ref[...] - t_ref[...]
        dy_ref[...] = err * (1.0 / d)
        l_ref[...] += 0.5 * jnp.sum(jnp.mean(err * err, axis=-1, keepdims=True), axis=0, keepdims=True)

    row = pl.BlockSpec((tm, d), lambda i: (i, 0))
    return pl.pallas_call(
        body, grid=(t // tm,), in_specs=[row, row], out_specs=[_full((1, 1)), row],
        out_shape=[jax.ShapeDtypeStruct((1, 1), F32), jax.ShapeDtypeStruct((t, d), F32)],
        compiler_params=_params(1), name="loss_head")(y, target)


_CHIP_FLIPS = ((1, 0), (0, 1), (1, 1))


def _flip(v, b):
    return 1 - v if b else v


def scatter_grads(big, small):
    n = len(big)

    def body(*refs):
        ins, small_in = refs[:n], refs[n]
        outs, small_out = refs[n + 1:2 * n + 1], refs[2 * n + 1]
        send_sems, recv_sems, ssend_sems, srecv_sems, local_sem = refs[2 * n + 2:]
        x, y, c = lax.axis_index("x"), lax.axis_index("y"), lax.axis_index("c")
        mine = 4 * x + 2 * y + c
        copies = [pltpu.make_async_copy(small_in, small_out.at[mine], local_sem)]
        copies[0].start()
        for m in range(1, 8):
            bx, by, bc = (m >> 2) & 1, (m >> 1) & 1, m & 1
            cp = pltpu.make_async_remote_copy(
                src_ref=small_in, dst_ref=small_out.at[mine], send_sem=ssend_sems.at[m - 1],
                recv_sem=srecv_sems.at[m - 1], device_id=(_flip(x, bx), _flip(y, by), _flip(c, bc)),
                device_id_type=MESH)
            cp.start()
            copies.append(cp)
        for t in range(n):
            for k, (bx, by) in enumerate(_CHIP_FLIPS):
                px, py = _flip(x, bx), _flip(y, by)
                cp = pltpu.make_async_remote_copy(
                    src_ref=ins[t].at[2 * px + py], dst_ref=outs[t].at[k], send_sem=send_sems.at[3 * t + k],
                    recv_sem=recv_sems.at[3 * t + k], device_id=(px, py, c), device_id_type=MESH)
                cp.start()
                copies.append(cp)
        for cp in copies:
            cp.wait()

    return pl.pallas_call(
        body, in_specs=[ANY] * (n + 1), out_specs=[ANY] * (n + 1),
        out_shape=[jax.ShapeDtypeStruct((3,) + b.shape[1:], b.dtype) for b in big]
        + [jax.ShapeDtypeStruct((8,) + small.shape, small.dtype)],
        scratch_shapes=[pltpu.SemaphoreType.DMA((3 * n,)), pltpu.SemaphoreType.DMA((3 * n,)),
                        pltpu.SemaphoreType.DMA((7,)), pltpu.SemaphoreType.DMA((7,)), pltpu.SemaphoreType.DMA],
        name="scatter_grads")(*big, small)


def swap_cores(arrays, name):
    n = len(arrays)

    def body(*refs):
        ins, outs = refs[:n], refs[n:2 * n]
        send_sems, recv_sems = refs[2 * n:]
        sibling = (lax.axis_index("x"), lax.axis_index("y"), 1 - lax.axis_index("c"))
        copies = []
        for t in range(n):
            cp = pltpu.make_async_remote_copy(
                src_ref=ins[t], dst_ref=outs[t], send_sem=send_sems.at[t], recv_sem=recv_sems.at[t],
                device_id=sibling, device_id_type=MESH)
            cp.start()
            copies.append(cp)
        for cp in copies:
            cp.wait()

    return pl.pallas_call(
        body, in_specs=[ANY] * n, out_specs=[ANY] * n,
        out_shape=[jax.ShapeDtypeStruct(a.shape, a.dtype) for a in arrays],
        scratch_shapes=[pltpu.SemaphoreType.DMA((n,)), pltpu.SemaphoreType.DMA((n,))],
        name=name)(*arrays)


_HBM = pl.BlockSpec(memory_space=pltpu.HBM)
_SEM = pl.BlockSpec(memory_space=pltpu.SEMAPHORE)
_EFFECT = pltpu.SideEffectType.DATAFLOW_SIDE_EFFECTING


def _chip_copies(mode, srcs, lands, send_sems, recv_sems):
    x, y, c = lax.axis_index("x"), lax.axis_index("y"), lax.axis_index("c")
    copies = []
    for t in range(len(srcs)):
        for k, (bx, by) in enumerate(_CHIP_FLIPS):
            px, py = _flip(x, bx), _flip(y, by)
            if mode == "gather":
                src, dst = srcs[t], lands[t].at[2 * x + y]
            else:
                src, dst = srcs[t].at[2 * px + py], lands[t].at[k]
            copies.append(pltpu.make_async_remote_copy(
                src_ref=src, dst_ref=dst, send_sem=send_sems.at[3 * t + k], recv_sem=recv_sems.at[3 * t + k],
                device_id=(px, py, c), device_id_type=MESH))
    return copies


def exchange_start(mode, srcs, lands, name, after=None):
    n = len(srcs)
    extra = [] if after is None else [after]

    def body(*refs):
        ins, lnd = refs[:n], refs[n:2 * n]
        send_sems, recv_sems = refs[2 * n + len(extra)], refs[2 * n + len(extra) + 1]
        token = refs[-1]
        for cp in _chip_copies(mode, ins, lnd, send_sems, recv_sems):
            cp.start()
        token[...] = jnp.zeros(token.shape, token.dtype)

    hbm = lambda a: pltpu.with_memory_space_constraint(a, pltpu.HBM)
    passed = list(srcs) + list(lands)
    return pl.pallas_call(
        body, name=name,
        out_shape=(pltpu.SemaphoreType.DMA((3 * n,)), pltpu.SemaphoreType.DMA((3 * n,)),
                   *[pltpu.HBM(a.shape, a.dtype) for a in passed], jax.ShapeDtypeStruct((8, 128), F32)),
        in_specs=[_HBM] * (2 * n) + [ANY] * len(extra),
        out_specs=(_SEM, _SEM, *[_HBM] * (2 * n), pl.BlockSpec(memory_space=pltpu.VMEM)),
        input_output_aliases={i: 2 + i for i in range(2 * n)},
        compiler_params=pltpu.CompilerParams(has_side_effects=_EFFECT))(*[hbm(a) for a in passed], *extra)


def exchange_wait(mode, started, after, name):
    send_sems, recv_sems = started[0], started[1]
    passed = list(started[2:-1])
    n = len(passed) // 2

    def body(*refs):
        ins, lnd = refs[:n], refs[n:2 * n]
        for cp in _chip_copies(mode, ins, lnd, refs[2 * n], refs[2 * n + 1]):
            cp.wait_send()
            cp.wait_recv()

    outs = pl.pallas_call(
        body, name=name, out_shape=tuple(pltpu.HBM(a.shape, a.dtype) for a in passed),
        in_specs=[_HBM] * (2 * n) + [_SEM, _SEM, ANY], out_specs=tuple([_HBM] * (2 * n)),
        input_output_aliases={i: i for i in range(2 * n)},
        compiler_params=pltpu.CompilerParams(has_side_effects=_EFFECT))(*passed, send_sems, recv_sems, after)
    return list(outs[n:])


def plane_sum(slot, full, recv, name, bm=256):
    _, m, n = full.shape
    bm = _row_block(m, bm)

    def body(slot_ref, o_ref, r_ref, s_ref):
        s_ref[...] = ((o_ref[...] + r_ref[0].astype(F32)) + r_ref[1].astype(F32)) + r_ref[2].astype(F32)

    return pl.pallas_call(
        body,
        grid_spec=pltpu.PrefetchScalarGridSpec(
            num_scalar_prefetch=1, grid=(m // bm,),
            in_specs=[pl.BlockSpec((None, bm, n), lambda i, s: (s[0], i, 0)),
                      pl.BlockSpec((3, bm, n), lambda i, s: (0, i, 0))],
            out_specs=pl.BlockSpec((bm, n), lambda i, s: (i, 0))),
        out_shape=jax.ShapeDtypeStruct((m, n), F32), compiler_params=_params(1), name=name)(slot, full, recv)


def sum_slots(parts, name, bm=312):
    _, r, n = parts.shape
    bm = bm if r % bm == 0 else r

    def body(p_ref, s_ref):
        acc = p_ref[0]
        for k in range(1, 8):
            acc = acc + p_ref[k]
        s_ref[...] = acc

    return pl.pallas_call(
        body, grid=(r // bm,), in_specs=[pl.BlockSpec((8, bm, n), lambda i: (0, i, 0))],
        out_specs=pl.BlockSpec((bm, n), lambda i: (i, 0)), out_shape=jax.ShapeDtypeStruct((r, n), F32),
        compiler_params=_params(1), name=name)(parts)


def _adamw_math(w, g, m, v):
    m = ADAM_B1 * m + (1.0 - ADAM_B1) * g
    v = ADAM_B2 * v + (1.0 - ADAM_B2) * (g * g)
    m_hat = m / (1.0 - ADAM_B1 ** ADAM_STEP)
    v_hat = v / (1.0 - ADAM_B2 ** ADAM_STEP)
    delta = -ADAM_LR * (m_hat / (jnp.sqrt(v_hat) + ADAM_EPS) + ADAM_WD * w)
    return delta, m, v


def adamw(w, m, v, grads, layer, prev, name, bm=256):
    _, mm, n = w.shape
    bm = _row_block(mm, bm)
    ng = len(grads)

    def body(*refs):
        w_ref, m_ref, v_ref = refs[:3]
        g_refs = refs[3:3 + ng]
        g_out, d_out, m_out, v_out = refs[-4:]
        g = g_refs[0][...]
        for gr in g_refs[1:]:
            g = g + gr[...]
        delta, mn, vn = _adamw_math(w_ref[...], g, m_ref[...], v_ref[...])
        g_out[...] = g
        d_out[...] = delta
        m_out[...] = mn
        v_out[...] = vn

    slab = pl.BlockSpec((None, bm, n), lambda i: (layer, i, 0))
    flat = pl.BlockSpec((bm, n), lambda i: (i, 0))
    in_specs = [slab] * 3 + [flat] * ng
    args = [w, m, v, *grads]
    aliases = {}
    if prev is not None:
        in_specs += [ANY] * 4
        aliases = {3 + ng + q: q for q in range(4)}
        args += list(prev)
    return pl.pallas_call(
        body, grid=(mm // bm,), in_specs=in_specs, out_specs=[slab] * 4,
        out_shape=[jax.ShapeDtypeStruct(w.shape, F32)] * 4, input_output_aliases=aliases,
        compiler_params=_params(1), name=name)(*args)


def _pack_rows(parts, rows):
    flat = jnp.concatenate([p.reshape(-1) for p in parts])
    return jnp.pad(flat, (0, rows * 128 - flat.shape[0])).reshape(rows, 128)


def _as_shards(a, rows):
    return a.reshape(N_SHARD, rows, a.shape[-1])


def _local_step(x, pos_col, target, gains, pool_w, pool_scale, gn_gain, conv_w, conv_b, weights, send_grads):
    def gain(l, n, token=None):
        g = gains[l, n].reshape(1, D_MODEL)
        return g if token is None else g + token[0:1, 0:1]

    inv_freq = (ROPE_BASE ** (-jnp.arange(0, RET_QK, 2, dtype=F32) / RET_QK)).reshape(1, RET_QK // 2)
    cos, sin = rope_tables(pos_col, inv_freq)
    intra, cross, inner, chunk_decay = _decay_tables()
    dn_rows = D_FF // N_SHARD

    x1 = pool_fwd(x, gain(0, 0), gain(0, 1), pool_w, pool_scale)
    w_up0, w_dn0 = weights("mlp0", x1)
    w_dn0 = w_dn0.reshape(D_FF, D_MODEL)
    x2, u0, c0, f0, h0 = mlp_fwd(x1, gain(0, 2), gain(0, 3), w_up0, w_dn0, conv_w[0], conv_b[0])
    w_in, w_out = weights("ret", x2)
    w_out = w_out.reshape(RET_HEADS * RET_V, D_MODEL)
    pj, hr = ret_proj(x2, gain(1, 0), w_in, cos, sin)
    o, sprev = ret_core_fwd(pj, intra, cross, inner, chunk_decay)
    x3, yb, out = ret_out_fwd(o, pj, x2, gn_gain, gain(1, 1), w_out)
    w_up1, w_dn1 = weights("mlp1", x3)
    w_dn1 = w_dn1.reshape(D_FF, D_MODEL)
    x4, u1, c1, f1, h1 = mlp_fwd(x3, gain(1, 2), gain(1, 3), w_up1, w_dn1, conv_w[1], conv_b[1])
    loss, dx4 = loss_head(x4, target)

    dx3, du1, a1, df1, dg12, dg13, dcw1, dcb1 = mlp_bwd(
        dx4, f1, x3, u1, c1, gain(1, 2), gain(1, 3), w_up1, w_dn1, conv_w[1])
    dwup1 = grad_matmul(h1, du1, D_MODEL, FF_CHUNK, "grad_w_up_1", interleaved=True)
    dwdn1 = grad_matmul(a1, df1, FF_CHUNK, D_MODEL, "grad_w_down_1")
    tok = send_grads("mlp1", [dwup1, [_as_shards(g, dn_rows) for g in dwdn1]])
    dout, dpj, do, dg11, dgn = ret_out_bwd(dx3, out, o, pj, gn_gain, gain(1, 1, tok), w_out)
    dwout = grad_matmul(yb, dout, 1024, D_MODEL, "grad_w_out")
    dpj = ret_core_bwd(pj, do, sprev, cos, sin, dpj, intra, cross, inner, chunk_decay)
    dwin = grad_matmul(hr, dpj, D_MODEL, RET_IN_SHARD, "grad_w_in")
    tok = send_grads("ret", [dwin, [_as_shards(g, RET_V) for g in dwout]])
    dx2, dg10 = ret_in_bwd(dpj, dx3, x2, gain(1, 0, tok), w_in)
    dx1, du0, a0, df0, dg02, dg03, dcw0, dcb0 = mlp_bwd(
        dx2, f0, x1, u0, c0, gain(0, 2), gain(0, 3), w_up0, w_dn0, conv_w[0])
    dwup0 = grad_matmul(h0, du0, D_MODEL, FF_CHUNK, "grad_w_up_0", interleaved=True)
    dwdn0 = grad_matmul(a0, df0, FF_CHUNK, D_MODEL, "grad_w_down_0")
    tok = send_grads("mlp0", [dwup0, [_as_shards(g, dn_rows) for g in dwdn0]])
    dx0, dg00, dg01, dpscale, dpw = pool_bwd(dx1, x, gain(0, 0, tok), gain(0, 1), pool_w, pool_scale)

    rows = lambda g: jnp.sum(g, axis=0, keepdims=True)
    dgains = jnp.concatenate([dg00, dg01, rows(dg02), rows(dg03), dg10, dg11, rows(dg12), rows(dg13)],
                             axis=0).reshape(2, 4, D_MODEL)
    small = {"gains": dgains, "pool_scale": dpscale, "gn": dgn,
             "conv_w": jnp.sum(jnp.stack([dcw0, dcw1]), axis=3),
             "conv_b": jnp.sum(jnp.stack([dcb0, dcb1]), axis=2, keepdims=True), "pool_w": dpw}
    return loss, dx0, small


def kernel(x, positions, norm_gain, pool_w, pool_scale, ret_w_in, ret_gn_gain, ret_w_out, mlp_w_up, mlp_conv_w, mlp_conv_b, mlp_w_down, loss_target, m_norm_gain, m_pool_w, m_pool_scale, m_ret_w_in, m_ret_gn_gain, m_ret_w_out, m_mlp_w_up, m_mlp_conv_w, m_mlp_conv_b, m_mlp_w_down, v_norm_gain, v_pool_w, v_pool_scale, v_ret_w_in, v_ret_gn_gain, v_ret_w_out, v_mlp_w_up, v_mlp_conv_w, v_mlp_conv_b, v_mlp_w_down):
    t = x.shape[1]
    me = 2 * lax.axis_index("x") + lax.axis_index("y")
    me_slot = jnp.reshape(me, (1,)).astype(jnp.int32)

    small_parts = [norm_gain, ret_gn_gain, mlp_conv_w, pool_w]
    small_sizes = [p.size for p in small_parts]
    small_rows = -(-sum(small_sizes) // (128 * 8)) * 8
    groups = {"small": [_pack_rows(small_parts, small_rows)],
              "mlp0": [mlp_w_up[0].astype(BF16), mlp_w_down[0].astype(BF16)],
              "ret": [ret_w_in[0].astype(BF16), ret_w_out[0].astype(BF16)],
              "mlp1": [mlp_w_up[1].astype(BF16), mlp_w_down[1].astype(BF16)]}
    gathers, token = {}, None
    for group, srcs in groups.items():
        lands = [lax.dynamic_update_index_in_dim(lax.empty((N_SHARD,) + s.shape, s.dtype), s, me, 0) for s in srcs]
        gathers[group] = exchange_start("gather", srcs, lands, "gather_start_" + group, after=token)
        token = gathers[group][-1]

    def weights(group, after):
        return exchange_wait("gather", gathers[group], after, "gather_wait_" + group)

    sent = {}

    def send_grads(group, pairs):
        lands = [lax.empty((3,) + b.shape[1:], BF16) for _, b in pairs]
        sent[group] = (exchange_start("scatter", [b for _, b in pairs], lands, "scatter_start_" + group),
                       [f for f, _ in pairs])
        return sent[group][0][-1]

    def reduced(group, after, names):
        started, own = sent[group]
        recv = exchange_wait("scatter", started, after, "scatter_wait_" + group)
        return [plane_sum(me_slot, f, r, "plane_sum_" + nm)
                for f, r, nm in zip(own, recv, names)]

    (smallg,) = weights("small", token)
    smallg = smallg.reshape(N_SHARD, -1)
    offs = np.cumsum([0] + small_sizes)
    piece = lambda i, shape: smallg[:, offs[i]:offs[i + 1]].reshape((N_SHARD,) + shape)
    gains = piece(0, (2, 4, 256)).transpose(1, 2, 0, 3).reshape(2, 4, D_MODEL)
    gn_full = piece(1, (512,)).reshape(1, RET_HEADS * RET_V)
    cw_full = piece(2, (2, 3, FF_CHUNK)).transpose(1, 0, 2, 3)
    pw_full = piece(3, (4, 64, 256)).transpose(1, 0, 2, 3).reshape(4, 256, 256).astype(BF16)
    cb_full = mlp_conv_b.reshape(2, N_SHARD, 1, FF_CHUNK)

    loss, dx0, small = _local_step(
        x[0], positions.reshape(t, 1).astype(F32), loss_target[0], gains, pw_full, pool_scale, gn_full,
        cw_full, cb_full, weights, send_grads)

    def small_adamw(w, m, v, grads, name):
        w3 = w.reshape(1, -1, w.shape[-1])
        out = adamw(w3, m.reshape(w3.shape), v.reshape(w3.shape), [g.reshape(w3.shape[1:]) for g in grads], 0, None, name)
        return [o.reshape(w.shape) for o in out]

    res = {}
    planes_a = reduced("mlp1", dx0, ["w_up_1", "w_down_1"]) + reduced("ret", dx0, ["w_in", "w_out"])
    others_a = swap_cores(planes_a, "swap_cores_a")
    res["ret_w_in"] = adamw(ret_w_in, m_ret_w_in, v_ret_w_in, (planes_a[2], others_a[2]), 0, None, "adamw_w_in")
    res["ret_w_out"] = adamw(ret_w_out, m_ret_w_out, v_ret_w_out, (planes_a[3], others_a[3]), 0, None, "adamw_w_out")
    up1 = adamw(mlp_w_up, m_mlp_w_up, v_mlp_w_up, (planes_a[0], others_a[0]), 1, None, "adamw_w_up_1")
    dn1 = adamw(mlp_w_down, m_mlp_w_down, v_mlp_w_down, (planes_a[1], others_a[1]), 1, None, "adamw_w_down_1")

    pw_f = small["pool_w"].reshape(4, N_SHARD, 64, 256).transpose(1, 0, 2, 3).reshape(N_SHARD, 256, 256)
    small_order = ["gains", "pool_scale", "gn", "conv_w", "conv_b"]
    gsmall_sizes = [small[k].size for k in small_order]
    gsmall_rows = -(-sum(gsmall_sizes) // (128 * 8)) * 8
    planes_b = reduced("mlp0", dn1[0], ["w_up_0", "w_down_0"])
    pw_recv, small_recv = scatter_grads([pw_f.astype(BF16)], _pack_rows([small[k] for k in small_order], gsmall_rows))
    planes_b.append(plane_sum(me_slot, pw_f, pw_recv, "plane_sum_pool_w"))
    others_b = swap_cores(planes_b, "swap_cores_b")
    res["mlp_w_up"] = adamw(mlp_w_up, m_mlp_w_up, v_mlp_w_up, (planes_b[0], others_b[0]), 0, up1, "adamw_w_up_0")
    res["mlp_w_down"] = adamw(mlp_w_down, m_mlp_w_down, v_mlp_w_down, (planes_b[1], others_b[1]), 0, dn1,
                              "adamw_w_down_0")
    res["pool_w"] = small_adamw(pool_w, m_pool_w, v_pool_w, (planes_b[2], others_b[2]), "adamw_pool_w")

    gsmall = sum_slots(small_recv, "sum_small").reshape(-1)
    goffs = np.cumsum([0] + gsmall_sizes)
    gpiece = lambda i: gsmall[goffs[i]:goffs[i + 1]].reshape(small[small_order[i]].shape)
    g_gains = lax.dynamic_slice_in_dim(gpiece(0), me * 256, 256, axis=2)
    g_gn = lax.dynamic_slice_in_dim(gpiece(2), me * RET_V, RET_V, axis=1)
    g_cw = lax.dynamic_index_in_dim(gpiece(3), me, 1, keepdims=False)
    res["norm_gain"] = small_adamw(norm_gain, m_norm_gain, v_norm_gain, [g_gains], "adamw_norm_gain")
    res["pool_scale"] = small_adamw(pool_scale, m_pool_scale, v_pool_scale, [gpiece(1)], "adamw_pool_scale")
    res["ret_gn_gain"] = small_adamw(ret_gn_gain, m_ret_gn_gain, v_ret_gn_gain, [g_gn], "adamw_gn_gain")
    res["mlp_conv_w"] = small_adamw(mlp_conv_w, m_mlp_conv_w, v_mlp_conv_w, [g_cw], "adamw_conv_w")
    res["mlp_conv_b"] = small_adamw(mlp_conv_b, m_mlp_conv_b, v_mlp_conv_b, [gpiece(4)], "adamw_conv_b")

    order = ["norm_gain", "pool_w", "pool_scale", "ret_w_in", "ret_gn_gain", "ret_w_out", "mlp_w_up", "mlp_conv_w",
             "mlp_conv_b", "mlp_w_down"]
    total_loss = lax.psum(loss[0, 0], ("x", "y", "c"))
    outs = [total_loss, dx0.reshape(x.shape)]
    for q in range(4):
        outs += [res[k][q] for k in order]
    return tuple(outs)
```

```python
import numpy as np
import jax
import jax.numpy as jnp
from jax import lax
from jax.experimental import pallas as pl
from jax.experimental.pallas import tpu as pltpu

F32 = jnp.float32
BF16 = jnp.bfloat16

D_MODEL = 1024
D_FF = 2816
FF_CHUNK = 1408
N_SHARD = 4
POOL_WINDOWS = (2, 4, 8, 16)
POOL_DIM = 256
POOL_HALO = 16
RET_HEADS = 4
RET_QK = 256
RET_V = 512
RET_CHUNK = 128
RET_IN = 6144
RET_IN_SHARD = 1536
ROPE_BASE = 10000.0
EPS = 1e-6
CONV_HALO = 8
MLP_PIECE = FF_CHUNK

ADAM_LR, ADAM_B1, ADAM_B2, ADAM_EPS, ADAM_WD, ADAM_STEP = 0.001, 0.9, 0.999, 1e-08, 0.01, 10

VMEM_LIMIT = 56 * 1024 * 1024
MESH = pl.DeviceIdType.MESH
ANY = pl.BlockSpec(memory_space=pl.ANY)


def _params(n_grid=1, limit=VMEM_LIMIT):
    return pltpu.CompilerParams(dimension_semantics=("arbitrary",) * n_grid, vmem_limit_bytes=limit)


def _dot(a, b):
    return jnp.dot(a, b, preferred_element_type=F32)


def _dot_nt(a, b):
    return lax.dot_general(a, b, (((1,), (1,)), ((), ())), preferred_element_type=F32)


def _dot_tn(a, b):
    return lax.dot_general(a, b, (((0,), (0,)), ((), ())), preferred_element_type=F32)


def _rms_fwd(x, gain):
    r = lax.rsqrt(jnp.mean(x * x, axis=-1, keepdims=True) + EPS)
    xh = x * r
    return xh * gain, xh, r


def _rms_bwd(xh, r, gain, dy):
    dxh = dy * gain
    return r * (dxh - xh * jnp.mean(dxh * xh, axis=-1, keepdims=True))


def _colsum(v):
    return jnp.sum(v, axis=0, keepdims=True)


def _full(shape):
    nd = len(shape)
    return pl.BlockSpec(shape, lambda *_: (0,) * nd)


def rope_tables(pos_col, inv_freq):
    t = pos_col.shape[0]
    tm = min(t, 1024)

    def body(p_ref, f_ref, c_ref, s_ref):
        ang = p_ref[...] * f_ref[...]
        c_ref[...] = jnp.cos(ang)
        s_ref[...] = jnp.sin(ang)

    return pl.pallas_call(
        body, grid=(t // tm,),
        in_specs=[pl.BlockSpec((tm, 1), lambda i: (i, 0)), _full((1, 128))],
        out_specs=[pl.BlockSpec((tm, 128), lambda i: (i, 0))] * 2,
        out_shape=[jax.ShapeDtypeStruct((t, 128), F32)] * 2,
        compiler_params=_params(1), name="rope_tables")(pos_col, inv_freq)


def _window_sums(ext, backward):
    n = ext.shape[0]
    cur, sums = ext, []
    for g, win in enumerate(POOL_WINDOWS):
        if g > 0:
            cur = cur[:, POOL_DIM:]
        half = win // 2
        cur = cur + pltpu.roll(cur, n - half if backward else half, axis=0)
        sums.append(cur[:, 0:POOL_DIM])
    return sums


def _pool_diff(h_halo, h, row0, tm):
    t_idx = row0 + lax.broadcasted_iota(jnp.int32, (tm, 1), 0)
    sums = _window_sums(jnp.concatenate([h_halo, h], axis=0), backward=False)
    parts, inv_counts = [], []
    for g, win in enumerate(POOL_WINDOWS):
        inv = 1.0 / jnp.minimum(t_idx + 1, win).astype(F32)
        parts.append(sums[g][POOL_HALO:, :] * inv - h[:, g * POOL_DIM:(g + 1) * POOL_DIM])
        inv_counts.append(inv)
    return parts, inv_counts


def pool_fwd(x, g_pre, g_post, pool_w, pool_scale, tm=256):
    t, d = x.shape
    nt = t // tm

    def body(x_ref, g0_ref, g1_ref, w_ref, sc_ref, o_ref, hext):
        i = pl.program_id(0)

        @pl.when(i == 0)
        def _():
            hext[...] = jnp.zeros((POOL_HALO, d), F32)

        xv = x_ref[...]
        h, _, _ = _rms_fwd(xv, g0_ref[...])
        parts, _ = _pool_diff(hext[...], h, i * tm, tm)
        hext[...] = h[tm - POOL_HALO:tm, :]
        ys = [_dot(parts[g].astype(BF16), w_ref[g]) for g in range(len(POOL_WINDOWS))]
        y = jnp.concatenate(ys, axis=-1) * sc_ref[...]
        m, _, _ = _rms_fwd(y, g1_ref[...])
        o_ref[...] = xv + m

    row = pl.BlockSpec((tm, d), lambda i: (i, 0))
    return pl.pallas_call(
        body, grid=(nt,),
        in_specs=[row, _full((1, d)), _full((1, d)), _full(pool_w.shape), _full((1, d))],
        out_specs=row, out_shape=jax.ShapeDtypeStruct((t, d), F32),
        scratch_shapes=[pltpu.VMEM((POOL_HALO, d), F32)],
        compiler_params=_params(1), name="pool_fwd")(x, g_pre, g_post, pool_w, pool_scale)


def pool_bwd(dx1, x, g_pre, g_post, pool_w, pool_scale, tm=256):
    t, d = x.shape
    nt = t // tm
    ng = len(POOL_WINDOWS)

    def body(dx1_ref, x_ref, xh_ref, g0_ref, g1_ref, w_ref, sc_ref,
             dx_ref, dg0_ref, dg1_ref, dsc_ref, dw_ref, enext):
        i = pl.program_id(0)
        r = nt - 1 - i

        @pl.when(i == 0)
        def _():
            enext[...] = jnp.zeros((POOL_HALO, d), F32)
            dg0_ref[...] = jnp.zeros_like(dg0_ref)
            dg1_ref[...] = jnp.zeros_like(dg1_ref)
            dsc_ref[...] = jnp.zeros_like(dsc_ref)
            dw_ref[...] = jnp.zeros_like(dw_ref)

        g0 = g0_ref[...]
        g1 = g1_ref[...]
        sc = sc_ref[...]
        xv = x_ref[...]
        h, xh, rx = _rms_fwd(xv, g0)
        h_halo, _, _ = _rms_fwd(xh_ref[...], g0)
        parts, inv_counts = _pool_diff(h_halo * jnp.where(r > 0, 1.0, 0.0), h, r * tm, tm)
        parts_b = [p.astype(BF16) for p in parts]
        ypre = jnp.concatenate([_dot(parts_b[g], w_ref[g]) for g in range(ng)], axis=-1)
        _, yh, ry = _rms_fwd(ypre * sc, g1)
        dm = dx1_ref[...]
        dg1_ref[...] += _colsum(dm * yh)
        dy = _rms_bwd(yh, ry, g1, dm)
        dsc_ref[...] += _colsum(dy * ypre)
        dyp = (dy * sc).astype(BF16)
        ddiffs = []
        for g in range(ng):
            cols = slice(g * POOL_DIM, (g + 1) * POOL_DIM)
            dw_ref[g] += _dot_tn(parts_b[g], dyp[:, cols])
            ddiffs.append(_dot_nt(dyp[:, cols], w_ref[g]))
        e = jnp.concatenate([ddiffs[g] * inv_counts[g] for g in range(ng)], axis=-1)
        sums = _window_sums(jnp.concatenate([e, enext[...]], axis=0), backward=True)
        enext[...] = e[0:POOL_HALO, :]
        dh = jnp.concatenate([sums[g][0:tm, :] - ddiffs[g] for g in range(ng)], axis=-1)
        dg0_ref[...] += _colsum(dh * xh)
        dx_ref[...] = dm + _rms_bwd(xh, rx, g0, dh)

    row = pl.BlockSpec((tm, d), lambda i: (nt - 1 - i, 0))
    halo = pl.BlockSpec((POOL_HALO, d), lambda i: (jnp.maximum((nt - 1 - i) * (tm // POOL_HALO) - 1, 0), 0))
    vec = _full((1, d))
    return pl.pallas_call(
        body, grid=(nt,),
        in_specs=[row, row, halo, vec, vec, _full(pool_w.shape), vec],
        out_specs=[row, vec, vec, vec, _full((ng, POOL_DIM, POOL_DIM))],
        out_shape=[jax.ShapeDtypeStruct((t, d), F32)] + [jax.ShapeDtypeStruct((1, d), F32)] * 3
        + [jax.ShapeDtypeStruct((ng, POOL_DIM, POOL_DIM), F32)],
        scratch_shapes=[pltpu.VMEM((POOL_HALO, d), F32)],
        compiler_params=_params(1), name="pool_bwd")(dx1, x, x, g_pre, g_post, pool_w, pool_scale)


def _conv_taps(cw_ref, j):
    return cw_ref[j, 0:1, :], cw_ref[j, 1:2, :], cw_ref[j, 2:3, :]


def _row_block(m, target=256):
    if m <= target:
        return m
    for b in range(target, 7, -8):
        if m % b == 0:
            return b
    return m


def mlp_fwd(x, g_pre, g_post, w_up, w_down, conv_w, conv_b, tm=256):
    t, d = x.shape
    nt = t // tm
    h8 = CONV_HALO

    def body(x_ref, g2_ref, g3_ref, wup_hbm, wdn_hbm, cw_ref, cb_ref,
             xo_ref, u_ref, c_ref, f_ref, h_ref, wup_v, wdn_v, tail, sem):
        i = pl.program_id(0)

        @pl.when(i == 0)
        def _():
            c1 = pltpu.make_async_copy(wup_hbm, wup_v, sem.at[0])
            c2 = pltpu.make_async_copy(wdn_hbm, wdn_v, sem.at[1])
            c1.start()
            c2.start()
            tail[...] = jnp.zeros_like(tail)
            c1.wait()
            c2.wait()

        xv = x_ref[...]
        h, _, _ = _rms_fwd(xv, g2_ref[...])
        hb = h.astype(BF16)
        h_ref[...] = hb
        acc = jnp.zeros((tm, d), F32)
        for k in range(2):
            cs = []
            for s in range(2):
                j, cols = k + 2 * s, slice((2 * k + s) * FF_CHUNK, (2 * k + s + 1) * FF_CHUNK)
                ub = _dot(hb, wup_v[j]).astype(BF16)
                u_ref[:, cols] = ub
                uf = ub.astype(F32)
                ext = jnp.concatenate([tail[j], uf], axis=0)
                tail[j] = uf[tm - h8:tm, :]
                w0, w1, w2 = _conv_taps(cw_ref, j)
                cb = (cb_ref[j] + w2 * uf + w1 * pltpu.roll(ext, 1, axis=0)[h8:, :]
                      + w0 * pltpu.roll(ext, 2, axis=0)[h8:, :]).astype(BF16)
                c_ref[:, cols] = cb
                cs.append(cb.astype(F32))
            a = cs[0] * jax.nn.sigmoid(cs[0]) * cs[1]
            acc = acc + _dot(a.astype(BF16), wdn_v[k * FF_CHUNK:(k + 1) * FF_CHUNK, :])
        f_ref[...] = acc
        y, _, _ = _rms_fwd(acc, g3_ref[...])
        xo_ref[...] = xv + y

    row = pl.BlockSpec((tm, d), lambda i: (i, 0))
    wide = pl.BlockSpec((tm, 2 * D_FF), lambda i: (i, 0))
    vec = _full((1, d))
    return pl.pallas_call(
        body, grid=(nt,),
        in_specs=[row, vec, vec, ANY, ANY, _full(conv_w.shape), _full(conv_b.shape)],
        out_specs=[row, wide, wide, row, row],
        out_shape=[jax.ShapeDtypeStruct((t, d), F32), jax.ShapeDtypeStruct((t, 2 * D_FF), BF16),
                   jax.ShapeDtypeStruct((t, 2 * D_FF), BF16),
                   jax.ShapeDtypeStruct((t, d), F32), jax.ShapeDtypeStruct((t, d), BF16)],
        scratch_shapes=[pltpu.VMEM(w_up.shape, BF16), pltpu.VMEM(w_down.shape, BF16),
                        pltpu.VMEM((N_SHARD, h8, FF_CHUNK), F32), pltpu.SemaphoreType.DMA((2,))],
        compiler_params=_params(1), name="mlp_fwd")(x, g_pre, g_post, w_up, w_down, conv_w, conv_b)


def _rowsum8(v):
    return jnp.sum(v.reshape(v.shape[0] // 8, 8, v.shape[1]), axis=0)


def mlp_bwd(dxo, f, x, u, c, g_pre, g_post, w_up, w_down, conv_w, tm=256):
    t, d = x.shape
    nt = t // tm
    h8 = CONV_HALO

    def body(dxo_ref, f_ref, x_ref, u_ref, c_ref, g2_ref, g3_ref, wup_hbm, wdn_hbm, cw_ref,
             dx_ref, du_ref, a_ref, df_ref, dg2_ref, dg3_ref, dcw_ref, dcb_ref,
             wup_v, wdn_v, dfb_s, dh_s, carry, sem):
        i = pl.program_id(0)
        k = pl.program_id(1)

        @pl.when((i == 0) & (k == 0))
        def _():
            c1 = pltpu.make_async_copy(wup_hbm, wup_v, sem.at[0])
            c2 = pltpu.make_async_copy(wdn_hbm, wdn_v, sem.at[1])
            c1.start()
            c2.start()
            carry[...] = jnp.zeros_like(carry)
            dg2_ref[...] = jnp.zeros_like(dg2_ref)
            dg3_ref[...] = jnp.zeros_like(dg3_ref)
            dcw_ref[...] = jnp.zeros_like(dcw_ref)
            dcb_ref[...] = jnp.zeros_like(dcb_ref)
            c1.wait()
            c2.wait()

        @pl.when(k == 0)
        def _():
            g3 = g3_ref[...]
            dxo = dxo_ref[...]
            _, fh, rf = _rms_fwd(f_ref[...], g3)
            dg3_ref[...] += _rowsum8(dxo * fh)
            dfb = _rms_bwd(fh, rf, g3, dxo).astype(BF16)
            df_ref[...] = dfb
            dfb_s[...] = dfb
            dh_s[...] = jnp.zeros_like(dh_s)

        dfb = dfb_s[...]
        dh = dh_s[...]
        for q0 in range(0, FF_CHUNK, MLP_PIECE):
            q1 = min(q0 + MLP_PIECE, FF_CHUNK)
            cg = c_ref[:, q0:q1].astype(F32)
            cv = c_ref[:, FF_CHUNK + q0:FF_CHUNK + q1].astype(F32)
            sg = jax.nn.sigmoid(cg)
            sil = cg * sg
            a_ref[:, q0:q1] = (sil * cv).astype(BF16)
            da = _dot_nt(dfb, wdn_v[pl.ds(pl.multiple_of(k * FF_CHUNK + q0, 128), q1 - q0), :])
            dcs = (da * cv * (sg * (1.0 + cg * (1.0 - sg))), da * sil)
            for s in range(2):
                j = k + 2 * s
                uf = u_ref[:, s * FF_CHUNK + q0:s * FF_CHUNK + q1].astype(F32)
                ext = jnp.concatenate([dcs[s], carry[j, :, q0:q1]], axis=0)
                carry[j, :, q0:q1] = dcs[s][0:h8, :]
                dc1 = pltpu.roll(ext, tm + h8 - 1, axis=0)[0:tm, :]
                dc2 = pltpu.roll(ext, tm + h8 - 2, axis=0)[0:tm, :]
                dcb_ref[j, :, q0:q1] += _rowsum8(dcs[s])
                dcw_ref[j, 2, :, q0:q1] += _rowsum8(dcs[s] * uf)
                dcw_ref[j, 1, :, q0:q1] += _rowsum8(dc1 * uf)
                dcw_ref[j, 0, :, q0:q1] += _rowsum8(dc2 * uf)
                dub = (cw_ref[j, 2:3, q0:q1] * dcs[s] + cw_ref[j, 1:2, q0:q1] * dc1
                       + cw_ref[j, 0:1, q0:q1] * dc2).astype(BF16)
                du_ref[:, s * FF_CHUNK + q0:s * FF_CHUNK + q1] = dub
                dh = dh + _dot_nt(dub, wup_v[j, :, q0:q1])
        dh_s[...] = dh

        @pl.when(k == 1)
        def _():
            g2 = g2_ref[...]
            _, xh, rx = _rms_fwd(x_ref[...], g2)
            dg2_ref[...] += _rowsum8(dh * xh)
            dx_ref[...] = dxo_ref[...] + _rms_bwd(xh, rx, g2, dh)

    row = pl.BlockSpec((tm, d), lambda i, k: (nt - 1 - i, 0))
    half = pl.BlockSpec((tm, 2 * FF_CHUNK), lambda i, k: (nt - 1 - i, k))
    vec = _full((1, d))
    acc = _full((8, d))
    dcw_shape, dcb_shape = (N_SHARD, 3, 8, FF_CHUNK), (N_SHARD, 8, FF_CHUNK)
    return pl.pallas_call(
        body, grid=(nt, 2),
        in_specs=[row, row, row, half, half, vec, vec, ANY, ANY, _full(conv_w.shape)],
        out_specs=[row, half, pl.BlockSpec((tm, FF_CHUNK), lambda i, k: (nt - 1 - i, k)), row, acc, acc,
                   _full(dcw_shape), _full(dcb_shape)],
        out_shape=[jax.ShapeDtypeStruct((t, d), F32), jax.ShapeDtypeStruct((t, 2 * D_FF), BF16),
                   jax.ShapeDtypeStruct((t, D_FF), BF16), jax.ShapeDtypeStruct((t, d), BF16),
                   jax.ShapeDtypeStruct((8, d), F32), jax.ShapeDtypeStruct((8, d), F32),
                   jax.ShapeDtypeStruct(dcw_shape, F32), jax.ShapeDtypeStruct(dcb_shape, F32)],
        scratch_shapes=[pltpu.VMEM(w_up.shape, BF16), pltpu.VMEM(w_down.shape, BF16),
                        pltpu.VMEM((tm, d), BF16), pltpu.VMEM((tm, d), F32),
                        pltpu.VMEM((N_SHARD, h8, FF_CHUNK), F32), pltpu.SemaphoreType.DMA((2,))],
        compiler_params=_params(2), name="mlp_bwd")(dxo, f, x, u, c, g_pre, g_post, w_up, w_down, conv_w)


def grad_matmul(a, b, bm, bn, name, tk=2048, interleaved=False, after=None):
    t, m = a.shape
    n = b.shape[1]
    tk = min(tk, t)
    nk = t // tk
    place = (lambda j: (j % 2) * 2 + j // 2) if interleaved else (lambda j: j)
    extra = [] if after is None else [after]

    def body(a_ref, b_ref, *rest):
        o_ref, ob_ref = rest[len(extra):]
        kk = pl.program_id(2)

        @pl.when(kk == 0)
        def _():
            o_ref[...] = jnp.zeros_like(o_ref)

        o_ref[...] += _dot_tn(a_ref[...], b_ref[...])

        @pl.when(kk == nk - 1)
        def _():
            ob_ref[...] = o_ref[...].astype(BF16)

    ospec = pl.BlockSpec((None, bm, bn), lambda j, i, kk: (place(j), i, 0))
    return pl.pallas_call(
        body, grid=(n // bn, m // bm, nk),
        in_specs=[pl.BlockSpec((tk, bm), lambda j, i, kk: (kk, i)), pl.BlockSpec((tk, bn), lambda j, i, kk: (kk, j))]
        + [ANY] * len(extra),
        out_specs=[ospec, ospec],
        out_shape=[jax.ShapeDtypeStruct((n // bn, m, bn), F32), jax.ShapeDtypeStruct((n // bn, m, bn), BF16)],
        compiler_params=_params(3), name=name)(a, b, *extra)


def _decay_tables():
    log_gamma = jnp.log(1.0 - 2.0 ** (-5.0 - jnp.arange(RET_HEADS, dtype=F32)))
    i = jnp.arange(RET_CHUNK, dtype=F32)
    rel = i[:, None] - i[None, :]
    intra = jnp.where(rel >= 0, jnp.exp(jnp.maximum(rel, 0.0) * log_gamma[:, None, None]), 0.0)
    cross = jnp.exp((i + 1.0) * log_gamma[:, None])[:, :, None]
    inner = jnp.exp((RET_CHUNK - 1.0 - i) * log_gamma[:, None])[:, :, None]
    chunk = [float(np.exp(np.float32(RET_CHUNK) * np.log(np.float32(1.0 - 2.0 ** (-5.0 - h))).astype(np.float32)))
             for h in range(RET_HEADS)]
    return intra, cross, inner, chunk


def ret_proj(x, g_pre, w_in, cos, sin, tm=256):
    t, d = x.shape
    nt = t // tm
    per = RET_IN_SHARD // RET_QK

    def body(x_ref, g_ref, win_hbm, c_ref, s_ref, pj_ref, h_ref, win_v, sem):
        @pl.when(pl.program_id(0) == 0)
        def _():
            cp = pltpu.make_async_copy(win_hbm, win_v, sem)
            cp.start()
            cp.wait()

        h, _, _ = _rms_fwd(x_ref[...], g_ref[...])
        hb = h.astype(BF16)
        h_ref[...] = hb
        c = c_ref[...]
        s = s_ref[...]
        for j in range(N_SHARD):
            pjj = _dot(hb, win_v[j])
            for bb in range(per):
                b = per * j + bb
                blk = pjj[:, bb * RET_QK:(bb + 1) * RET_QK]
                if b < 2 * RET_HEADS:
                    x1, x2 = blk[:, :128], blk[:, 128:]
                    o1 = x1 * c - x2 * s
                    o2 = x2 * c + x1 * s
                    if b < RET_HEADS:
                        o1 = o1 * (RET_QK ** -0.5)
                        o2 = o2 * (RET_QK ** -0.5)
                    pj_ref[:, b * RET_QK:b * RET_QK + 128] = o1.astype(BF16)
                    pj_ref[:, b * RET_QK + 128:(b + 1) * RET_QK] = o2.astype(BF16)
                else:
                    pj_ref[:, b * RET_QK:(b + 1) * RET_QK] = blk.astype(BF16)

    row = pl.BlockSpec((tm, d), lambda i: (i, 0))
    tab = pl.BlockSpec((tm, 128), lambda i: (i, 0))
    return pl.pallas_call(
        body, grid=(nt,),
        in_specs=[row, _full((1, d)), ANY, tab, tab],
        out_specs=[pl.BlockSpec((tm, RET_IN), lambda i: (i, 0)), row],
        out_shape=[jax.ShapeDtypeStruct((t, RET_IN), BF16), jax.ShapeDtypeStruct((t, d), BF16)],
        scratch_shapes=[pltpu.VMEM(w_in.shape, BF16), pltpu.SemaphoreType.DMA],
        compiler_params=_params(1), name="ret_proj")(x, g_pre, w_in, cos, sin)


def ret_core_fwd(pj, intra, cross, inner, chunk_decay):
    t = pj.shape[0]
    nc = t // RET_CHUNK
    c = RET_CHUNK
    qk_all = RET_HEADS * RET_QK
    v_all = RET_HEADS * RET_V

    def body(q_ref, k_ref, v_ref, dm_ref, cr_ref, in_ref, o_ref, sp_ref, state):
        @pl.when(pl.program_id(0) == 0)
        def _():
            state[...] = jnp.zeros_like(state)

        for h in range(RET_HEADS):
            q = q_ref[:, h * RET_QK:(h + 1) * RET_QK]
            k = k_ref[:, h * RET_QK:(h + 1) * RET_QK]
            v = v_ref[:, h * RET_V:(h + 1) * RET_V]
            sb = state[h].astype(BF16)
            sp_ref[h] = sb
            sc = _dot_nt(q, k) * dm_ref[h]
            o_ref[:, h * RET_V:(h + 1) * RET_V] = _dot(sc.astype(BF16), v) + _dot(q, sb) * cr_ref[h]
            kd = (k.astype(F32) * in_ref[h]).astype(BF16)
            state[h] = state[h] * chunk_decay[h] + _dot_tn(kd, v)

    return pl.pallas_call(
        body, grid=(nc,),
        in_specs=[pl.BlockSpec((c, qk_all), lambda n: (n, 0)), pl.BlockSpec((c, qk_all), lambda n: (n, 1)),
                  pl.BlockSpec((c, v_all), lambda n: (n, 1)),
                  _full(intra.shape), _full(cross.shape), _full(inner.shape)],
        out_specs=[pl.BlockSpec((c, v_all), lambda n: (n, 0)),
                   pl.BlockSpec((None, RET_HEADS, RET_QK, RET_V), lambda n: (n, 0, 0, 0))],
        out_shape=[jax.ShapeDtypeStruct((t, v_all), F32),
                   jax.ShapeDtypeStruct((nc, RET_HEADS, RET_QK, RET_V), BF16)],
        scratch_shapes=[pltpu.VMEM((RET_HEADS, RET_QK, RET_V), F32)],
        compiler_params=_params(1), name="ret_core_fwd")(pj, pj, pj, intra, cross, inner)


def _group_norm(o_h):
    mu = jnp.mean(o_h, axis=-1, keepdims=True)
    dev = o_h - mu
    rstd = lax.rsqrt(jnp.mean(dev * dev, axis=-1, keepdims=True) + EPS)
    return dev * rstd, rstd


def ret_out_fwd(o, pj, x, gn_gain, g_post, w_out, tm=256):
    t, d = x.shape
    nt = t // tm
    v_all = RET_HEADS * RET_V

    def body(o_ref, g_ref, x_ref, gn_ref, g1_ref, w_ref, xo_ref, y_ref, out_ref):
        for h in range(RET_HEADS):
            cols = slice(h * RET_V, (h + 1) * RET_V)
            ohat, _ = _group_norm(o_ref[:, cols])
            g = g_ref[:, cols].astype(F32)
            y_ref[:, cols] = (g * jax.nn.sigmoid(g) * (ohat * gn_ref[:, cols])).astype(BF16)
        out = _dot(y_ref[...], w_ref[...])
        out_ref[...] = out
        m, _, _ = _rms_fwd(out, g1_ref[...])
        xo_ref[...] = x_ref[...] + m

    row = pl.BlockSpec((tm, d), lambda i: (i, 0))
    wide = pl.BlockSpec((tm, v_all), lambda i: (i, 0))
    return pl.pallas_call(
        body, grid=(nt,),
        in_specs=[wide, pl.BlockSpec((tm, v_all), lambda i: (i, 2)), row, _full((1, v_all)), _full((1, d)),
                  _full(w_out.shape)],
        out_specs=[row, wide, row],
        out_shape=[jax.ShapeDtypeStruct((t, d), F32), jax.ShapeDtypeStruct((t, v_all), BF16),
                   jax.ShapeDtypeStruct((t, d), F32)],
        compiler_params=_params(1), name="ret_out_fwd")(o, pj, x, gn_gain, g_post, w_out)


def ret_out_bwd(dxo, out, o, pj, gn_gain, g_post, w_out, tm=256):
    t, d = out.shape
    nt = t // tm
    v_all = RET_HEADS * RET_V

    def body(dxo_ref, out_ref, o_ref, g_ref, gn_ref, g1_ref, w_ref,
             dout_ref, dgate_ref, do_ref, dg1_ref, dgn_ref):
        @pl.when(pl.program_id(0) == 0)
        def _():
            dg1_ref[...] = jnp.zeros_like(dg1_ref)
            dgn_ref[...] = jnp.zeros_like(dgn_ref)

        g1 = g1_ref[...]
        dxo = dxo_ref[...]
        _, oh_, r_ = _rms_fwd(out_ref[...], g1)
        dg1_ref[...] += _colsum(dxo * oh_)
        doutb = _rms_bwd(oh_, r_, g1, dxo).astype(BF16)
        dout_ref[...] = doutb
        dy = _dot_nt(doutb, w_ref[...])
        for h in range(RET_HEADS):
            cols = slice(h * RET_V, (h + 1) * RET_V)
            gn = gn_ref[:, cols]
            ohat, rstd = _group_norm(o_ref[:, cols])
            g = g_ref[:, cols].astype(F32)
            sg = jax.nn.sigmoid(g)
            dyh = dy[:, cols]
            dgate_ref[:, cols] = (dyh * (ohat * gn) * (sg * (1.0 + g * (1.0 - sg)))).astype(BF16)
            don = dyh * (g * sg)
            dgn_ref[:, cols] += _colsum(don * ohat)
            dohat = don * gn
            do_ref[:, cols] = (rstd * (dohat - jnp.mean(dohat, axis=-1, keepdims=True)
                                       - ohat * jnp.mean(dohat * ohat, axis=-1, keepdims=True))).astype(BF16)

    row = pl.BlockSpec((tm, d), lambda i: (i, 0))
    wide = pl.BlockSpec((tm, v_all), lambda i: (i, 0))
    gate = pl.BlockSpec((tm, v_all), lambda i: (i, 2))
    return pl.pallas_call(
        body, grid=(nt,),
        in_specs=[row, row, wide, gate, _full((1, v_all)), _full((1, d)), _full(w_out.shape)],
        out_specs=[row, gate, wide, _full((1, d)), _full((1, v_all))],
        out_shape=[jax.ShapeDtypeStruct((t, d), BF16), jax.ShapeDtypeStruct((t, RET_IN), BF16),
                   jax.ShapeDtypeStruct((t, v_all), BF16), jax.ShapeDtypeStruct((1, d), F32),
                   jax.ShapeDtypeStruct((1, v_all), F32)],
        compiler_params=_params(1), name="ret_out_bwd")(dxo, out, o, pj, gn_gain, g_post, w_out)


def ret_core_bwd(pj, do, sprev, cos, sin, dpj, intra, cross, inner, chunk_decay):
    t = pj.shape[0]
    nc = t // RET_CHUNK
    c = RET_CHUNK
    qk_all = RET_HEADS * RET_QK
    v_all = RET_HEADS * RET_V
    scale = RET_QK ** -0.5

    def body(q_ref, k_ref, v_ref, do_ref, sp_ref, c_ref, s_ref, dm_ref, cr_ref, in_ref, dpj_in, dpj_ref, dstate):
        @pl.when(pl.program_id(0) == 0)
        def _():
            dstate[...] = jnp.zeros_like(dstate)

        cs = c_ref[...]
        sn = s_ref[...]
        for h in range(RET_HEADS):
            q = q_ref[:, h * RET_QK:(h + 1) * RET_QK]
            k = k_ref[:, h * RET_QK:(h + 1) * RET_QK]
            v = v_ref[:, h * RET_V:(h + 1) * RET_V]
            doh = do_ref[:, h * RET_V:(h + 1) * RET_V]
            dm = dm_ref[h]
            ab = (_dot_nt(q, k) * dm).astype(BF16)
            dab = (_dot_nt(doh, v) * dm).astype(BF16)
            dsb = dstate[h].astype(BF16)
            kd = (k.astype(F32) * in_ref[h]).astype(BF16)
            dv = _dot_tn(ab, doh) + _dot(kd, dsb)
            dq = _dot(dab, k) + cr_ref[h] * _dot_nt(doh, sp_ref[h])
            dk = _dot_tn(dab, q) + in_ref[h] * _dot_nt(v, dsb)
            qd = (q.astype(F32) * cr_ref[h]).astype(BF16)
            dstate[h] = dstate[h] * chunk_decay[h] + _dot_tn(qd, doh)
            for base, dd, sc in ((h * RET_QK, dq, scale), (qk_all + h * RET_QK, dk, 1.0)):
                d1, d2 = dd[:, :128], dd[:, 128:]
                dpj_ref[:, base:base + 128] = ((d1 * cs + d2 * sn) * sc).astype(BF16)
                dpj_ref[:, base + 128:base + RET_QK] = ((d2 * cs - d1 * sn) * sc).astype(BF16)
            dpj_ref[:, 2 * qk_all + h * RET_V:2 * qk_all + (h + 1) * RET_V] = dv.astype(BF16)

    rev = lambda n: nc - 1 - n
    tab = pl.BlockSpec((c, 128), lambda n: (rev(n), 0))
    return pl.pallas_call(
        body, grid=(nc,),
        in_specs=[pl.BlockSpec((c, qk_all), lambda n: (rev(n), 0)), pl.BlockSpec((c, qk_all), lambda n: (rev(n), 1)),
                  pl.BlockSpec((c, v_all), lambda n: (rev(n), 1)), pl.BlockSpec((c, v_all), lambda n: (rev(n), 0)),
                  pl.BlockSpec((None, RET_HEADS, RET_QK, RET_V), lambda n: (rev(n), 0, 0, 0)),
                  tab, tab, _full(intra.shape), _full(cross.shape), _full(inner.shape), ANY],
        out_specs=pl.BlockSpec((c, 2 * qk_all + v_all), lambda n: (rev(n), 0)),
        out_shape=jax.ShapeDtypeStruct((t, RET_IN), BF16),
        scratch_shapes=[pltpu.VMEM((RET_HEADS, RET_QK, RET_V), F32)],
        input_output_aliases={10: 0},
        compiler_params=_params(1), name="ret_core_bwd")(pj, pj, pj, do, sprev, cos, sin, intra, cross, inner, dpj)


def ret_in_bwd(dpj, dres, x, g_pre, w_in, tm=256):
    t, d = x.shape
    nt = t // tm

    def body(dpj_ref, dres_ref, x_ref, g_ref, win_hbm, dx_ref, dg_ref, win_v, sem):
        @pl.when(pl.program_id(0) == 0)
        def _():
            cp = pltpu.make_async_copy(win_hbm, win_v, sem)
            cp.start()
            dg_ref[...] = jnp.zeros_like(dg_ref)
            cp.wait()

        g = g_ref[...]
        dh = jnp.zeros((tm, d), F32)
        for j in range(N_SHARD):
            dh = dh + _dot_nt(dpj_ref[:, j * RET_IN_SHARD:(j + 1) * RET_IN_SHARD], win_v[j])
        _, xh, rx = _rms_fwd(x_ref[...], g)
        dg_ref[...] += _colsum(dh * xh)
        dx_ref[...] = dres_ref[...] + _rms_bwd(xh, rx, g, dh)

    row = pl.BlockSpec((tm, d), lambda i: (i, 0))
    return pl.pallas_call(
        body, grid=(nt,),
        in_specs=[pl.BlockSpec((tm, RET_IN), lambda i: (i, 0)), row, row, _full((1, d)), ANY],
        out_specs=[row, _full((1, d))],
        out_shape=[jax.ShapeDtypeStruct((t, d), F32), jax.ShapeDtypeStruct((1, d), F32)],
        scratch_shapes=[pltpu.VMEM(w_in.shape, BF16), pltpu.SemaphoreType.DMA],
        compiler_params=_params(1), name="ret_in_bwd")(dpj, dres, x, g_pre, w_in)


def loss_head(y, target, tm=512):
    t, d = y.shape

    def body(y_ref, t_ref, l_ref, dy_ref):
        @pl.when(pl.program_id(0) == 0)
        def _():
            l_ref[...] = jnp.zeros_like(l_ref)

        err = y_ref[...] - t_ref[...]
        dy_ref[...] = err * (1.0 / d)
        l_ref[...] += 0.5 * jnp.sum(jnp.mean(err * err, axis=-1, keepdims=True), axis=0, keepdims=True)

    row = pl.BlockSpec((tm, d), lambda i: (i, 0))
    return pl.pallas_call(
        body, grid=(t // tm,), in_specs=[row, row], out_specs=[_full((1, 1)), row],
        out_shape=[jax.ShapeDtypeStruct((1, 1), F32), jax.ShapeDtypeStruct((t, d), F32)],
        compiler_params=_params(1), name="loss_head")(y, target)


_CHIP_FLIPS = ((1, 0), (0, 1), (1, 1))


def _flip(v, b):
    return 1 - v if b else v


def scatter_grads(big, small):
    n = len(big)

    def body(*refs):
        ins, small_in = refs[:n], refs[n]
        outs, small_out = refs[n + 1:2 * n + 1], refs[2 * n + 1]
        send_sems, recv_sems, ssend_sems, srecv_sems, local_sem = refs[2 * n + 2:]
        x, y, c = lax.axis_index("x"), lax.axis_index("y"), lax.axis_index("c")
        mine = 4 * x + 2 * y + c
        copies = [pltpu.make_async_copy(small_in, small_out.at[mine], local_sem)]
        copies[0].start()
        for m in range(1, 8):
            bx, by, bc = (m >> 2) & 1, (m >> 1) & 1, m & 1
            cp = pltpu.make_async_remote_copy(
                src_ref=small_in, dst_ref=small_out.at[mine], send_sem=ssend_sems.at[m - 1],
                recv_sem=srecv_sems.at[m - 1], device_id=(_flip(x, bx), _flip(y, by), _flip(c, bc)),
                device_id_type=MESH)
            cp.start()
            copies.append(cp)
        for t in range(n):
            for k, (bx, by) in enumerate(_CHIP_FLIPS):
                px, py = _flip(x, bx), _flip(y, by)
                cp = pltpu.make_async_remote_copy(
                    src_ref=ins[t].at[2 * px + py], dst_ref=outs[t].at[k], send_sem=send_sems.at[3 * t + k],
                    recv_sem=recv_sems.at[3 * t + k], device_id=(px, py, c), device_id_type=MESH)
                cp.start()
                copies.append(cp)
        for cp in copies:
            cp.wait()

    return pl.pallas_call(
        body, in_specs=[ANY] * (n + 1), out_specs=[ANY] * (n + 1),
        out_shape=[jax.ShapeDtypeStruct((3,) + b.shape[1:], b.dtype) for b in big]
        + [jax.ShapeDtypeStruct((8,) + small.shape, small.dtype)],
        scratch_shapes=[pltpu.SemaphoreType.DMA((3 * n,)), pltpu.SemaphoreType.DMA((3 * n,)),
                        pltpu.SemaphoreType.DMA((7,)), pltpu.SemaphoreType.DMA((7,)), pltpu.SemaphoreType.DMA],
        name="scatter_grads")(*big, small)


def swap_cores(arrays, name):
    n = len(arrays)

    def body(*refs):
        ins, outs = refs[:n], refs[n:2 * n]
        send_sems, recv_sems = refs[2 * n:]
        sibling = (lax.axis_index("x"), lax.axis_index("y"), 1 - lax.axis_index("c"))
        copies = []
        for t in range(n):
            cp = pltpu.make_async_remote_copy(
                src_ref=ins[t], dst_ref=outs[t], send_sem=send_sems.at[t], recv_sem=recv_sems.at[t],
                device_id=sibling, device_id_type=MESH)
            cp.start()
            copies.append(cp)
        for cp in copies:
            cp.wait()

    return pl.pallas_call(
        body, in_specs=[ANY] * n, out_specs=[ANY] * n,
        out_shape=[jax.ShapeDtypeStruct(a.shape, a.dtype) for a in arrays],
        scratch_shapes=[pltpu.SemaphoreType.DMA((n,)), pltpu.SemaphoreType.DMA((n,))],
        name=name)(*arrays)


_HBM = pl.BlockSpec(memory_space=pltpu.HBM)
_SEM = pl.BlockSpec(memory_space=pltpu.SEMAPHORE)
_EFFECT = pltpu.SideEffectType.DATAFLOW_SIDE_EFFECTING


def _chip_copies(mode, srcs, lands, send_sems, recv_sems):
    x, y, c = lax.axis_index("x"), lax.axis_index("y"), lax.axis_index("c")
    copies = []
    for t in range(len(lands)):
        for k, (bx, by) in enumerate(_CHIP_FLIPS):
            px, py = _flip(x, bx), _flip(y, by)
            target = (px, py, c)
            if mode == "gather":
                src, dst = srcs[t], lands[t].at[2 * x + y]
            elif mode == "gather_half":
                half = pl.ds(c * (srcs[t].shape[0] // 2), srcs[t].shape[0] // 2)
                src, dst = srcs[t].at[half], lands[t].at[2 * x + y, half]
            elif mode == "forward_half":
                half = pl.ds(c * (lands[t].shape[1] // 2), lands[t].shape[1] // 2)
                src = dst = lands[t].at[2 * px + py, half]
                target = (x, y, 1 - c)
            else:
                src, dst = srcs[t].at[2 * px + py], lands[t].at[k]
            copies.append(pltpu.make_async_remote_copy(
                src_ref=src, dst_ref=dst, send_sem=send_sems.at[3 * t + k], recv_sem=recv_sems.at[3 * t + k],
                device_id=target, device_id_type=MESH))
    return copies


def exchange_start(mode, srcs, lands, name, after=None):
    n, ns = len(lands), len(srcs)
    extra = [] if after is None else [after]

    def body(*refs):
        ins, lnd = refs[:ns], refs[ns:ns + n]
        send_sems, recv_sems = refs[ns + n + len(extra)], refs[ns + n + len(extra) + 1]
        token = refs[-1]
        for cp in _chip_copies(mode, ins, lnd, send_sems, recv_sems):
            cp.start()
        token[...] = jnp.zeros(token.shape, token.dtype)

    hbm = lambda a: pltpu.with_memory_space_constraint(a, pltpu.HBM)
    passed = list(srcs) + list(lands)
    return pl.pallas_call(
        body, name=name,
        out_shape=(pltpu.SemaphoreType.DMA((3 * n,)), pltpu.SemaphoreType.DMA((3 * n,)),
                   *[pltpu.HBM(a.shape, a.dtype) for a in passed], jax.ShapeDtypeStruct((8, 128), F32)),
        in_specs=[_HBM] * (ns + n) + [ANY] * len(extra),
        out_specs=(_SEM, _SEM, *[_HBM] * (ns + n), pl.BlockSpec(memory_space=pltpu.VMEM)),
        input_output_aliases={i: 2 + i for i in range(ns + n)},
        compiler_params=pltpu.CompilerParams(has_side_effects=_EFFECT))(*[hbm(a) for a in passed], *extra)


def exchange_wait(mode, started, after, name):
    send_sems, recv_sems = started[0], started[1]
    passed = list(started[2:-1])
    n = len(passed) if mode == "forward_half" else len(passed) // 2
    ns = len(passed) - n

    def body(*refs):
        ins, lnd = refs[:ns], refs[ns:ns + n]
        for cp in _chip_copies(mode, ins, lnd, refs[ns + n], refs[ns + n + 1]):
            cp.wait_send()
            cp.wait_recv()

    outs = pl.pallas_call(
        body, name=name, out_shape=tuple(pltpu.HBM(a.shape, a.dtype) for a in passed),
        in_specs=[_HBM] * (ns + n) + [_SEM, _SEM, ANY], out_specs=tuple([_HBM] * (ns + n)),
        input_output_aliases={i: i for i in range(ns + n)},
        compiler_params=pltpu.CompilerParams(has_side_effects=_EFFECT))(*passed, send_sems, recv_sems, after)
    return list(outs[ns:])


def plane_sum(slot, full, recv, name, bm=256):
    _, m, n = full.shape
    bm = _row_block(m, bm)

    def body(slot_ref, o_ref, r_ref, s_ref):
        s_ref[...] = ((o_ref[...] + r_ref[0].astype(F32)) + r_ref[1].astype(F32)) + r_ref[2].astype(F32)

    return pl.pallas_call(
        body,
        grid_spec=pltpu.PrefetchScalarGridSpec(
            num_scalar_prefetch=1, grid=(m // bm,),
            in_specs=[pl.BlockSpec((None, bm, n), lambda i, s: (s[0], i, 0)),
                      pl.BlockSpec((3, bm, n), lambda i, s: (0, i, 0))],
            out_specs=pl.BlockSpec((bm, n), lambda i, s: (i, 0))),
        out_shape=jax.ShapeDtypeStruct((m, n), F32), compiler_params=_params(1), name=name)(slot, full, recv)


def sum_slots(parts, name, bm=312):
    _, r, n = parts.shape
    bm = bm if r % bm == 0 else r

    def body(p_ref, s_ref):
        acc = p_ref[0]
        for k in range(1, 8):
            acc = acc + p_ref[k]
        s_ref[...] = acc

    return pl.pallas_call(
        body, grid=(r // bm,), in_specs=[pl.BlockSpec((8, bm, n), lambda i: (0, i, 0))],
        out_specs=pl.BlockSpec((bm, n), lambda i: (i, 0)), out_shape=jax.ShapeDtypeStruct((r, n), F32),
        compiler_params=_params(1), name=name)(parts)


def _adamw_math(w, g, m, v):
    m = ADAM_B1 * m + (1.0 - ADAM_B1) * g
    v = ADAM_B2 * v + (1.0 - ADAM_B2) * (g * g)
    m_hat = m / (1.0 - ADAM_B1 ** ADAM_STEP)
    v_hat = v / (1.0 - ADAM_B2 ** ADAM_STEP)
    delta = -ADAM_LR * (m_hat / (jnp.sqrt(v_hat) + ADAM_EPS) + ADAM_WD * w)
    return delta, m, v


def adamw(w, m, v, grads, layer, prev, name, bm=256):
    _, mm, n = w.shape
    bm = _row_block(mm, bm)
    ng = len(grads)

    def body(*refs):
        w_ref, m_ref, v_ref = refs[:3]
        g_refs = refs[3:3 + ng]
        g_out, d_out, m_out, v_out = refs[-4:]
        g = g_refs[0][...]
        for gr in g_refs[1:]:
            g = g + gr[...]
        delta, mn, vn = _adamw_math(w_ref[...], g, m_ref[...], v_ref[...])
        g_out[...] = g
        d_out[...] = delta
        m_out[...] = mn
        v_out[...] = vn

    slab = pl.BlockSpec((None, bm, n), lambda i: (layer, i, 0))
    flat = pl.BlockSpec((bm, n), lambda i: (i, 0))
    in_specs = [slab] * 3 + [flat] * ng
    args = [w, m, v, *grads]
    aliases = {}
    if prev is not None:
        in_specs += [ANY] * 4
        aliases = {3 + ng + q: q for q in range(4)}
        args += list(prev)
    return pl.pallas_call(
        body, grid=(mm // bm,), in_specs=in_specs, out_specs=[slab] * 4,
        out_shape=[jax.ShapeDtypeStruct(w.shape, F32)] * 4, input_output_aliases=aliases,
        compiler_params=_params(1), name=name)(*args)


def _pack_rows(parts, rows):
    flat = jnp.concatenate([p.reshape(-1) for p in parts])
    return jnp.pad(flat, (0, rows * 128 - flat.shape[0])).reshape(rows, 128)


def _as_shards(a, rows):
    return a.reshape(N_SHARD, rows, a.shape[-1])


def _local_step(x, pos_col, target, gains, pool_w, pool_scale, gn_gain, conv_w, conv_b, weights, send_grads):
    def gain(l, n, token=None):
        g = gains[l, n].reshape(1, D_MODEL)
        return g if token is None else g + token[0:1, 0:1]

    inv_freq = (ROPE_BASE ** (-jnp.arange(0, RET_QK, 2, dtype=F32) / RET_QK)).reshape(1, RET_QK // 2)
    cos, sin = rope_tables(pos_col, inv_freq)
    intra, cross, inner, chunk_decay = _decay_tables()
    dn_rows = D_FF // N_SHARD

    x1 = pool_fwd(x, gain(0, 0), gain(0, 1), pool_w, pool_scale)
    w_up0, w_dn0 = weights("mlp0", x1)
    w_dn0 = w_dn0.reshape(D_FF, D_MODEL)
    x2, u0, c0, f0, h0 = mlp_fwd(x1, gain(0, 2), gain(0, 3), w_up0, w_dn0, conv_w[0], conv_b[0])
    w_in, w_out = weights("ret", x2)
    w_out = w_out.reshape(RET_HEADS * RET_V, D_MODEL)
    pj, hr = ret_proj(x2, gain(1, 0), w_in, cos, sin)
    o, sprev = ret_core_fwd(pj, intra, cross, inner, chunk_decay)
    x3, yb, out = ret_out_fwd(o, pj, x2, gn_gain, gain(1, 1), w_out)
    w_up1, w_dn1 = weights("mlp1", x3)
    w_dn1 = w_dn1.reshape(D_FF, D_MODEL)
    x4, u1, c1, f1, h1 = mlp_fwd(x3, gain(1, 2), gain(1, 3), w_up1, w_dn1, conv_w[1], conv_b[1])
    loss, dx4 = loss_head(x4, target)

    dx3, du1, a1, df1, dg12, dg13, dcw1, dcb1 = mlp_bwd(
        dx4, f1, x3, u1, c1, gain(1, 2), gain(1, 3), w_up1, w_dn1, conv_w[1])
    dwup1 = grad_matmul(h1, du1, D_MODEL, FF_CHUNK, "grad_w_up_1", interleaved=True)
    dwdn1 = grad_matmul(a1, df1, FF_CHUNK, D_MODEL, "grad_w_down_1")
    tok = send_grads("mlp1", [dwup1, [_as_shards(g, dn_rows) for g in dwdn1]])
    dout, dpj, do, dg11, dgn = ret_out_bwd(dx3, out, o, pj, gn_gain, gain(1, 1, tok), w_out)
    dwout = grad_matmul(yb, dout, 1024, D_MODEL, "grad_w_out")
    dpj = ret_core_bwd(pj, do, sprev, cos, sin, dpj, intra, cross, inner, chunk_decay)
    dwin = grad_matmul(hr, dpj, D_MODEL, RET_IN_SHARD, "grad_w_in")
    tok = send_grads("ret", [dwin, [_as_shards(g, RET_V) for g in dwout]])
    dx2, dg10 = ret_in_bwd(dpj, dx3, x2, gain(1, 0, tok), w_in)
    dx1, du0, a0, df0, dg02, dg03, dcw0, dcb0 = mlp_bwd(
        dx2, f0, x1, u0, c0, gain(0, 2), gain(0, 3), w_up0, w_dn0, conv_w[0])
    dwdn0 = grad_matmul(a0, df0, FF_CHUNK, D_MODEL, "grad_w_down_0")
    tok = send_grads("down0", [[_as_shards(g, dn_rows) for g in dwdn0]])
    dwup0 = grad_matmul(h0, du0, D_MODEL, FF_CHUNK, "grad_w_up_0", interleaved=True, after=tok)
    tok = send_grads("up0", [dwup0])
    dx0, dg00, dg01, dpscale, dpw = pool_bwd(dx1, x, gain(0, 0, tok), gain(0, 1), pool_w, pool_scale)

    rows = lambda g: jnp.sum(g, axis=0, keepdims=True)
    dgains = jnp.concatenate([dg00, dg01, rows(dg02), rows(dg03), dg10, dg11, rows(dg12), rows(dg13)],
                             axis=0).reshape(2, 4, D_MODEL)
    small = {"gains": dgains, "pool_scale": dpscale, "gn": dgn,
             "conv_w": jnp.sum(jnp.stack([dcw0, dcw1]), axis=3),
             "conv_b": jnp.sum(jnp.stack([dcb0, dcb1]), axis=2, keepdims=True), "pool_w": dpw}
    return loss, dx0, small


def kernel(x, positions, norm_gain, pool_w, pool_scale, ret_w_in, ret_gn_gain, ret_w_out, mlp_w_up, mlp_conv_w, mlp_conv_b, mlp_w_down, loss_target, m_norm_gain, m_pool_w, m_pool_scale, m_ret_w_in, m_ret_gn_gain, m_ret_w_out, m_mlp_w_up, m_mlp_conv_w, m_mlp_conv_b, m_mlp_w_down, v_norm_gain, v_pool_w, v_pool_scale, v_ret_w_in, v_ret_gn_gain, v_ret_w_out, v_mlp_w_up, v_mlp_conv_w, v_mlp_conv_b, v_mlp_w_down):
    t = x.shape[1]
    me = 2 * lax.axis_index("x") + lax.axis_index("y")
    me_slot = jnp.reshape(me, (1,)).astype(jnp.int32)

    small_parts = [norm_gain, ret_gn_gain, mlp_conv_w, pool_w]
    small_sizes = [p.size for p in small_parts]
    small_rows = -(-sum(small_sizes) // (128 * 8)) * 8
    groups = {"small": [_pack_rows(small_parts, small_rows)],
              "mlp0": [mlp_w_up[0].astype(BF16), mlp_w_down[0].astype(BF16)],
              "ret": [ret_w_in[0].astype(BF16), ret_w_out[0].astype(BF16)],
              "mlp1": [mlp_w_up[1].astype(BF16), mlp_w_down[1].astype(BF16)]}
    gathers, token = {}, None
    for group, srcs in groups.items():
        lands = [lax.dynamic_update_index_in_dim(lax.empty((N_SHARD,) + s.shape, s.dtype), s, me, 0) for s in srcs]
        mode = "gather_half" if group == "mlp0" else "gather"
        gathers[group] = (mode, exchange_start(mode, srcs, lands, "gather_start_" + group, after=token))
        token = gathers[group][1][-1]

    def weights(group, after):
        mode, started = gathers[group]
        lands = exchange_wait(mode, started, after, "gather_wait_" + group)
        if mode == "gather_half":
            forward = exchange_start("forward_half", [], lands, "forward_start_" + group)
            lands = exchange_wait("forward_half", forward, forward[-1], "forward_wait_" + group)
        return lands

    sent = {}

    def send_grads(group, pairs):
        lands = [lax.empty((3,) + b.shape[1:], BF16) for _, b in pairs]
        sent[group] = (exchange_start("scatter", [b for _, b in pairs], lands, "scatter_start_" + group),
                       [f for f, _ in pairs])
        return sent[group][0][-1]

    def reduced(group, after, names):
        started, own = sent[group]
        recv = exchange_wait("scatter", started, after, "scatter_wait_" + group)
        return [plane_sum(me_slot, f, r, "plane_sum_" + nm)
                for f, r, nm in zip(own, recv, names)]

    (smallg,) = weights("small", token)
    smallg = smallg.reshape(N_SHARD, -1)
    offs = np.cumsum([0] + small_sizes)
    piece = lambda i, shape: smallg[:, offs[i]:offs[i + 1]].reshape((N_SHARD,) + shape)
    gains = piece(0, (2, 4, 256)).transpose(1, 2, 0, 3).reshape(2, 4, D_MODEL)
    gn_full = piece(1, (512,)).reshape(1, RET_HEADS * RET_V)
    cw_full = piece(2, (2, 3, FF_CHUNK)).transpose(1, 0, 2, 3)
    pw_full = piece(3, (4, 64, 256)).transpose(1, 0, 2, 3).reshape(4, 256, 256).astype(BF16)
    cb_full = mlp_conv_b.reshape(2, N_SHARD, 1, FF_CHUNK)

    loss, dx0, small = _local_step(
        x[0], positions.reshape(t, 1).astype(F32), loss_target[0], gains, pw_full, pool_scale, gn_full,
        cw_full, cb_full, weights, send_grads)

    def small_adamw(w, m, v, grads, name):
        w3 = w.reshape(1, -1, w.shape[-1])
        out = adamw(w3, m.reshape(w3.shape), v.reshape(w3.shape), [g.reshape(w3.shape[1:]) for g in grads], 0, None, name)
        return [o.reshape(w.shape) for o in out]

    res = {}
    planes_a = reduced("mlp1", dx0, ["w_up_1", "w_down_1"]) + reduced("ret", dx0, ["w_in", "w_out"])
    others_a = swap_cores(planes_a, "swap_cores_a")
    res["ret_w_in"] = adamw(ret_w_in, m_ret_w_in, v_ret_w_in, (planes_a[2], others_a[2]), 0, None, "adamw_w_in")
    res["ret_w_out"] = adamw(ret_w_out, m_ret_w_out, v_ret_w_out, (planes_a[3], others_a[3]), 0, None, "adamw_w_out")
    up1 = adamw(mlp_w_up, m_mlp_w_up, v_mlp_w_up, (planes_a[0], others_a[0]), 1, None, "adamw_w_up_1")
    dn1 = adamw(mlp_w_down, m_mlp_w_down, v_mlp_w_down, (planes_a[1], others_a[1]), 1, None, "adamw_w_down_1")

    pw_f = small["pool_w"].reshape(4, N_SHARD, 64, 256).transpose(1, 0, 2, 3).reshape(N_SHARD, 256, 256)
    small_order = ["gains", "pool_scale", "gn", "conv_w", "conv_b"]
    gsmall_sizes = [small[k].size for k in small_order]
    gsmall_rows = -(-sum(gsmall_sizes) // (128 * 8)) * 8
    planes_b = reduced("up0", dn1[0], ["w_up_0"]) + reduced("down0", dn1[0], ["w_down_0"])
    pw_recv, small_recv = scatter_grads([pw_f.astype(BF16)], _pack_rows([small[k] for k in small_order], gsmall_rows))
    planes_b.append(plane_sum(me_slot, pw_f, pw_recv, "plane_sum_pool_w"))
    others_b = swap_cores(planes_b, "swap_cores_b")
    res["mlp_w_up"] = adamw(mlp_w_up, m_mlp_w_up, v_mlp_w_up, (planes_b[0], others_b[0]), 0, up1, "adamw_w_up_0")
    res["mlp_w_down"] = adamw(mlp_w_down, m_mlp_w_down, v_mlp_w_down, (planes_b[1], others_b[1]), 0, dn1,
                              "adamw_w_down_0")
    res["pool_w"] = small_adamw(pool_w, m_pool_w, v_pool_w, (planes_b[2], others_b[2]), "adamw_pool_w")

    gsmall = sum_slots(small_recv, "sum_small").reshape(-1)
    goffs = np.cumsum([0] + gsmall_sizes)
    gpiece = lambda i: gsmall[goffs[i]:goffs[i + 1]].reshape(small[small_order[i]].shape)
    g_gains = lax.dynamic_slice_in_dim(gpiece(0), me * 256, 256, axis=2)
    g_gn = lax.dynamic_slice_in_dim(gpiece(2), me * RET_V, RET_V, axis=1)
    g_cw = lax.dynamic_index_in_dim(gpiece(3), me, 1, keepdims=False)
    res["norm_gain"] = small_adamw(norm_gain, m_norm_gain, v_norm_gain, [g_gains], "adamw_norm_gain")
    res["pool_scale"] = small_adamw(pool_scale, m_pool_scale, v_pool_scale, [gpiece(1)], "adamw_pool_scale")
    res["ret_gn_gain"] = small_adamw(ret_gn_gain, m_ret_gn_gain, v_ret_gn_gain, [g_gn], "adamw_gn_gain")
    res["mlp_conv_w"] = small_adamw(mlp_conv_w, m_mlp_conv_w, v_mlp_conv_w, [g_cw], "adamw_conv_w")
    res["mlp_conv_b"] = small_adamw(mlp_conv_b, m_mlp_conv_b, v_mlp_conv_b, [gpiece(4)], "adamw_conv_b")

    order = ["norm_gain", "pool_w", "pool_scale", "ret_w_in", "ret_gn_gain", "ret_w_out", "mlp_w_up", "mlp_conv_w",
             "mlp_conv_b", "mlp_w_down"]
    total_loss = lax.psum(loss[0, 0], ("x", "y", "c"))
    outs = [total_loss, dx0.reshape(x.shape)]
    for q in range(4):
        outs += [res[k][q] for k in order]
    return tuple(outs)
```

```python
import numpy as np
import jax
import jax.numpy as jnp
from jax import lax
from jax.experimental import pallas as pl
from jax.experimental.pallas import tpu as pltpu

F32 = jnp.float32
BF16 = jnp.bfloat16

D_MODEL = 1024
D_FF = 2816
FF_CHUNK = 1408
N_SHARD = 4
POOL_WINDOWS = (2, 4, 8, 16)
POOL_DIM = 256
POOL_HALO = 16
RET_HEADS = 4
RET_QK = 256
RET_V = 512
RET_CHUNK = 256
RET_IN = 6144
RET_IN_SHARD = 1536
ROPE_BASE = 10000.0
EPS = 1e-6
CONV_HALO = 8
MLP_PIECE = FF_CHUNK

ADAM_LR, ADAM_B1, ADAM_B2, ADAM_EPS, ADAM_WD, ADAM_STEP = 0.001, 0.9, 0.999, 1e-08, 0.01, 10

VMEM_LIMIT = 56 * 1024 * 1024
MESH = pl.DeviceIdType.MESH
ANY = pl.BlockSpec(memory_space=pl.ANY)


def _params(n_grid=1, limit=VMEM_LIMIT):
    return pltpu.CompilerParams(dimension_semantics=("arbitrary",) * n_grid, vmem_limit_bytes=limit)


def _dot(a, b):
    return jnp.dot(a, b, preferred_element_type=F32)


def _dot_nt(a, b):
    return lax.dot_general(a, b, (((1,), (1,)), ((), ())), preferred_element_type=F32)


def _dot_tn(a, b):
    return lax.dot_general(a, b, (((0,), (0,)), ((), ())), preferred_element_type=F32)


def _rms_fwd(x, gain):
    r = lax.rsqrt(jnp.mean(x * x, axis=-1, keepdims=True) + EPS)
    xh = x * r
    return xh * gain, xh, r


def _rms_bwd(xh, r, gain, dy):
    dxh = dy * gain
    return r * (dxh - xh * jnp.mean(dxh * xh, axis=-1, keepdims=True))


def _colsum(v):
    return jnp.sum(v, axis=0, keepdims=True)


def _full(shape):
    nd = len(shape)
    return pl.BlockSpec(shape, lambda *_: (0,) * nd)


def rope_tables(pos_col, inv_freq):
    t = pos_col.shape[0]
    tm = min(t, 1024)

    def body(p_ref, f_ref, c_ref, s_ref):
        ang = p_ref[...] * f_ref[...]
        c_ref[...] = jnp.cos(ang)
        s_ref[...] = jnp.sin(ang)

    return pl.pallas_call(
        body, grid=(t // tm,),
        in_specs=[pl.BlockSpec((tm, 1), lambda i: (i, 0)), _full((1, 128))],
        out_specs=[pl.BlockSpec((tm, 128), lambda i: (i, 0))] * 2,
        out_shape=[jax.ShapeDtypeStruct((t, 128), F32)] * 2,
        compiler_params=_params(1), name="rope_tables")(pos_col, inv_freq)


def _window_sums(ext, backward):
    n = ext.shape[0]
    cur, sums = ext, []
    for g, win in enumerate(POOL_WINDOWS):
        if g > 0:
            cur = cur[:, POOL_DIM:]
        half = win // 2
        cur = cur + pltpu.roll(cur, n - half if backward else half, axis=0)
        sums.append(cur[:, 0:POOL_DIM])
    return sums


def _pool_diff(h_halo, h, row0, tm):
    t_idx = row0 + lax.broadcasted_iota(jnp.int32, (tm, 1), 0)
    sums = _window_sums(jnp.concatenate([h_halo, h], axis=0), backward=False)
    parts, inv_counts = [], []
    for g, win in enumerate(POOL_WINDOWS):
        inv = 1.0 / jnp.minimum(t_idx + 1, win).astype(F32)
        parts.append(sums[g][POOL_HALO:, :] * inv - h[:, g * POOL_DIM:(g + 1) * POOL_DIM])
        inv_counts.append(inv)
    return parts, inv_counts


def pool_fwd(x, g_pre, g_post, pool_w, pool_scale, tm=512):
    t, d = x.shape
    nt = t // tm

    def body(x_ref, g0_ref, g1_ref, w_ref, sc_ref, o_ref, hext):
        i = pl.program_id(0)

        @pl.when(i == 0)
        def _():
            hext[...] = jnp.zeros((POOL_HALO, d), F32)

        xv = x_ref[...]
        h, _, _ = _rms_fwd(xv, g0_ref[...])
        parts, _ = _pool_diff(hext[...], h, i * tm, tm)
        hext[...] = h[tm - POOL_HALO:tm, :]
        ys = [_dot(parts[g].astype(BF16), w_ref[g]) for g in range(len(POOL_WINDOWS))]
        y = jnp.concatenate(ys, axis=-1) * sc_ref[...]
        m, _, _ = _rms_fwd(y, g1_ref[...])
        o_ref[...] = xv + m

    row = pl.BlockSpec((tm, d), lambda i: (i, 0))
    return pl.pallas_call(
        body, grid=(nt,),
        in_specs=[row, _full((1, d)), _full((1, d)), _full(pool_w.shape), _full((1, d))],
        out_specs=row, out_shape=jax.ShapeDtypeStruct((t, d), F32),
        scratch_shapes=[pltpu.VMEM((POOL_HALO, d), F32)],
        compiler_params=_params(1), name="pool_fwd")(x, g_pre, g_post, pool_w, pool_scale)


def pool_bwd(dx1, x, g_pre, g_post, pool_w, pool_scale, tm=512):
    t, d = x.shape
    nt = t // tm
    ng = len(POOL_WINDOWS)

    def body(dx1_ref, x_ref, xh_ref, g0_ref, g1_ref, w_ref, sc_ref,
             dx_ref, dg0_ref, dg1_ref, dsc_ref, dw_ref, enext):
        i = pl.program_id(0)
        r = nt - 1 - i

        @pl.when(i == 0)
        def _():
            enext[...] = jnp.zeros((POOL_HALO, d), F32)
            dg0_ref[...] = jnp.zeros_like(dg0_ref)
            dg1_ref[...] = jnp.zeros_like(dg1_ref)
            dsc_ref[...] = jnp.zeros_like(dsc_ref)
            dw_ref[...] = jnp.zeros_like(dw_ref)

        g0 = g0_ref[...]
        g1 = g1_ref[...]
        sc = sc_ref[...]
        xv = x_ref[...]
        h, xh, rx = _rms_fwd(xv, g0)
        h_halo, _, _ = _rms_fwd(xh_ref[...], g0)
        parts, inv_counts = _pool_diff(h_halo * jnp.where(r > 0, 1.0, 0.0), h, r * tm, tm)
        parts_b = [p.astype(BF16) for p in parts]
        ypre = jnp.concatenate([_dot(parts_b[g], w_ref[g]) for g in range(ng)], axis=-1)
        _, yh, ry = _rms_fwd(ypre * sc, g1)
        dm = dx1_ref[...]
        dg1_ref[...] += _colsum(dm * yh)
        dy = _rms_bwd(yh, ry, g1, dm)
        dsc_ref[...] += _colsum(dy * ypre)
        dyp = (dy * sc).astype(BF16)
        ddiffs = []
        for g in range(ng):
            cols = slice(g * POOL_DIM, (g + 1) * POOL_DIM)
            dw_ref[g] += _dot_tn(parts_b[g], dyp[:, cols])
            ddiffs.append(_dot_nt(dyp[:, cols], w_ref[g]))
        e = jnp.concatenate([ddiffs[g] * inv_counts[g] for g in range(ng)], axis=-1)
        sums = _window_sums(jnp.concatenate([e, enext[...]], axis=0), backward=True)
        enext[...] = e[0:POOL_HALO, :]
        dh = jnp.concatenate([sums[g][0:tm, :] - ddiffs[g] for g in range(ng)], axis=-1)
        dg0_ref[...] += _colsum(dh * xh)
        dx_ref[...] = dm + _rms_bwd(xh, rx, g0, dh)

    row = pl.BlockSpec((tm, d), lambda i: (nt - 1 - i, 0))
    halo = pl.BlockSpec((POOL_HALO, d), lambda i: (jnp.maximum((nt - 1 - i) * (tm // POOL_HALO) - 1, 0), 0))
    vec = _full((1, d))
    return pl.pallas_call(
        body, grid=(nt,),
        in_specs=[row, row, halo, vec, vec, _full(pool_w.shape), vec],
        out_specs=[row, vec, vec, vec, _full((ng, POOL_DIM, POOL_DIM))],
        out_shape=[jax.ShapeDtypeStruct((t, d), F32)] + [jax.ShapeDtypeStruct((1, d), F32)] * 3
        + [jax.ShapeDtypeStruct((ng, POOL_DIM, POOL_DIM), F32)],
        scratch_shapes=[pltpu.VMEM((POOL_HALO, d), F32)],
        compiler_params=_params(1), name="pool_bwd")(dx1, x, x, g_pre, g_post, pool_w, pool_scale)


def _conv_taps(cw_ref, j):
    return cw_ref[j, 0:1, :], cw_ref[j, 1:2, :], cw_ref[j, 2:3, :]


def _row_block(m, target=256):
    if m <= target:
        return m
    for b in range(target, 7, -8):
        if m % b == 0:
            return b
    return m


def mlp_fwd(x, g_pre, g_post, w_up, w_down, conv_w, conv_b, target=None, tm=256):
    t, d = x.shape
    nt = t // tm
    h8 = CONV_HALO
    with_loss = target is not None
    n_extra = 1 if with_loss else 0

    def body(x_ref, g2_ref, g3_ref, wup_hbm, wdn_hbm, cw_ref, cb_ref, *rest):
        tgt_ref = rest[0] if with_loss else None
        xo_ref, u_ref, c_ref, f_ref, h_ref = rest[n_extra:n_extra + 5]
        loss_ref = rest[n_extra + 5] if with_loss else None
        wup_v, wdn_v, tail, sem = rest[-4:]
        i = pl.program_id(0)

        @pl.when(i == 0)
        def _():
            c1 = pltpu.make_async_copy(wup_hbm, wup_v, sem.at[0])
            c2 = pltpu.make_async_copy(wdn_hbm, wdn_v, sem.at[1])
            c1.start()
            c2.start()
            tail[...] = jnp.zeros_like(tail)
            if with_loss:
                loss_ref[...] = jnp.zeros_like(loss_ref)
            c1.wait()
            c2.wait()

        xv = x_ref[...]
        h, _, _ = _rms_fwd(xv, g2_ref[...])
        hb = h.astype(BF16)
        h_ref[...] = hb
        acc = jnp.zeros((tm, d), F32)
        for k in range(2):
            cs = []
            for s in range(2):
                j, cols = k + 2 * s, slice((2 * k + s) * FF_CHUNK, (2 * k + s + 1) * FF_CHUNK)
                uf = _dot(hb, wup_v[j])
                u_ref[:, cols] = uf.astype(BF16)
                ext = jnp.concatenate([tail[j], uf], axis=0)
                tail[j] = uf[tm - h8:tm, :]
                w0, w1, w2 = _conv_taps(cw_ref, j)
                cf = (cb_ref[j] + w2 * uf + w1 * pltpu.roll(ext, 1, axis=0)[h8:, :]
                      + w0 * pltpu.roll(ext, 2, axis=0)[h8:, :])
                c_ref[:, cols] = cf.astype(BF16)
                cs.append(cf)
            a = cs[0] * jax.nn.sigmoid(cs[0]) * cs[1]
            acc = acc + _dot(a.astype(BF16), wdn_v[k * FF_CHUNK:(k + 1) * FF_CHUNK, :])
        f_ref[...] = acc
        y, _, _ = _rms_fwd(acc, g3_ref[...])
        if with_loss:
            err = (xv + y) - tgt_ref[...]
            xo_ref[...] = err * (1.0 / d)
            loss_ref[...] += 0.5 * jnp.sum(jnp.mean(err * err, axis=-1, keepdims=True), axis=0, keepdims=True)
        else:
            xo_ref[...] = xv + y

    row = pl.BlockSpec((tm, d), lambda i: (i, 0))
    wide = pl.BlockSpec((tm, 2 * D_FF), lambda i: (i, 0))
    vec = _full((1, d))
    extra = [target] if with_loss else []
    return pl.pallas_call(
        body, grid=(nt,),
        in_specs=[row, vec, vec, ANY, ANY, _full(conv_w.shape), _full(conv_b.shape)] + [row] * n_extra,
        out_specs=[row, wide, wide, row, row] + [_full((1, 1))] * n_extra,
        out_shape=[jax.ShapeDtypeStruct((t, d), F32), jax.ShapeDtypeStruct((t, 2 * D_FF), BF16),
                   jax.ShapeDtypeStruct((t, 2 * D_FF), BF16),
                   jax.ShapeDtypeStruct((t, d), F32), jax.ShapeDtypeStruct((t, d), BF16)]
        + [jax.ShapeDtypeStruct((1, 1), F32)] * n_extra,
        scratch_shapes=[pltpu.VMEM(w_up.shape, BF16), pltpu.VMEM(w_down.shape, BF16),
                        pltpu.VMEM((N_SHARD, h8, FF_CHUNK), F32), pltpu.SemaphoreType.DMA((2,))],
        compiler_params=_params(1), name="mlp_fwd_loss" if with_loss else "mlp_fwd")(
            x, g_pre, g_post, w_up, w_down, conv_w, conv_b, *extra)


def _rowsum8(v):
    return jnp.sum(v.reshape(v.shape[0] // 8, 8, v.shape[1]), axis=0)


def mlp_bwd(dxo, f, x, u, c, g_pre, g_post, w_up, w_down, conv_w, tm=256):
    t, d = x.shape
    nt = t // tm
    h8 = CONV_HALO

    def body(dxo_ref, f_ref, x_ref, u_ref, c_ref, g2_ref, g3_ref, wup_hbm, wdn_hbm, cw_ref,
             dx_ref, du_ref, a_ref, df_ref, dg2_ref, dg3_ref, dcw_ref, dcb_ref,
             wup_v, wdn_v, dfb_s, dh_s, carry, sem):
        i = pl.program_id(0)
        k = pl.program_id(1)

        @pl.when((i == 0) & (k == 0))
        def _():
            c1 = pltpu.make_async_copy(wup_hbm, wup_v, sem.at[0])
            c2 = pltpu.make_async_copy(wdn_hbm, wdn_v, sem.at[1])
            c1.start()
            c2.start()
            carry[...] = jnp.zeros_like(carry)
            dg2_ref[...] = jnp.zeros_like(dg2_ref)
            dg3_ref[...] = jnp.zeros_like(dg3_ref)
            dcw_ref[...] = jnp.zeros_like(dcw_ref)
            dcb_ref[...] = jnp.zeros_like(dcb_ref)
            c1.wait()
            c2.wait()

        @pl.when(k == 0)
        def _():
            g3 = g3_ref[...]
            dxo = dxo_ref[...]
            _, fh, rf = _rms_fwd(f_ref[...], g3)
            dg3_ref[...] += _rowsum8(dxo * fh)
            dfb = _rms_bwd(fh, rf, g3, dxo).astype(BF16)
            df_ref[...] = dfb
            dfb_s[...] = dfb
            dh_s[...] = jnp.zeros_like(dh_s)

        dfb = dfb_s[...]
        dh = dh_s[...]
        for q0 in range(0, FF_CHUNK, MLP_PIECE):
            q1 = min(q0 + MLP_PIECE, FF_CHUNK)
            cg = c_ref[:, q0:q1].astype(F32)
            cv = c_ref[:, FF_CHUNK + q0:FF_CHUNK + q1].astype(F32)
            sg = jax.nn.sigmoid(cg)
            sil = cg * sg
            a_ref[:, q0:q1] = (sil * cv).astype(BF16)
            da = _dot_nt(dfb, wdn_v[pl.ds(pl.multiple_of(k * FF_CHUNK + q0, 128), q1 - q0), :])
            dcs = (da * cv * (sg * (1.0 + cg * (1.0 - sg))), da * sil)
            for s in range(2):
                j = k + 2 * s
                uf = u_ref[:, s * FF_CHUNK + q0:s * FF_CHUNK + q1].astype(F32)
                ext = jnp.concatenate([dcs[s], carry[j, :, q0:q1]], axis=0)
                carry[j, :, q0:q1] = dcs[s][0:h8, :]
                dc1 = pltpu.roll(ext, tm + h8 - 1, axis=0)[0:tm, :]
                dc2 = pltpu.roll(ext, tm + h8 - 2, axis=0)[0:tm, :]
                dcb_ref[j, :, q0:q1] += _rowsum8(dcs[s])
                dcw_ref[j, 2, :, q0:q1] += _rowsum8(dcs[s] * uf)
                dcw_ref[j, 1, :, q0:q1] += _rowsum8(dc1 * uf)
                dcw_ref[j, 0, :, q0:q1] += _rowsum8(dc2 * uf)
                dub = (cw_ref[j, 2:3, q0:q1] * dcs[s] + cw_ref[j, 1:2, q0:q1] * dc1
                       + cw_ref[j, 0:1, q0:q1] * dc2).astype(BF16)
                du_ref[:, s * FF_CHUNK + q0:s * FF_CHUNK + q1] = dub
                dh = dh + _dot_nt(dub, wup_v[j, :, q0:q1])
        dh_s[...] = dh

        @pl.when(k == 1)
        def _():
            g2 = g2_ref[...]
            _, xh, rx = _rms_fwd(x_ref[...], g2)
            dg2_ref[...] += _rowsum8(dh * xh)
            dx_ref[...] = dxo_ref[...] + _rms_bwd(xh, rx, g2, dh)

    row = pl.BlockSpec((tm, d), lambda i, k: (nt - 1 - i, 0))
    half = pl.BlockSpec((tm, 2 * FF_CHUNK), lambda i, k: (nt - 1 - i, k))
    vec = _full((1, d))
    acc = _full((8, d))
    dcw_shape, dcb_shape = (N_SHARD, 3, 8, FF_CHUNK), (N_SHARD, 8, FF_CHUNK)
    return pl.pallas_call(
        body, grid=(nt, 2),
        in_specs=[row, row, row, half, half, vec, vec, ANY, ANY, _full(conv_w.shape)],
        out_specs=[row, half, pl.BlockSpec((tm, FF_CHUNK), lambda i, k: (nt - 1 - i, k)), row, acc, acc,
                   _full(dcw_shape), _full(dcb_shape)],
        out_shape=[jax.ShapeDtypeStruct((t, d), F32), jax.ShapeDtypeStruct((t, 2 * D_FF), BF16),
                   jax.ShapeDtypeStruct((t, D_FF), BF16), jax.ShapeDtypeStruct((t, d), BF16),
                   jax.ShapeDtypeStruct((8, d), F32), jax.ShapeDtypeStruct((8, d), F32),
                   jax.ShapeDtypeStruct(dcw_shape, F32), jax.ShapeDtypeStruct(dcb_shape, F32)],
        scratch_shapes=[pltpu.VMEM(w_up.shape, BF16), pltpu.VMEM(w_down.shape, BF16),
                        pltpu.VMEM((tm, d), BF16), pltpu.VMEM((tm, d), F32),
                        pltpu.VMEM((N_SHARD, h8, FF_CHUNK), F32), pltpu.SemaphoreType.DMA((2,))],
        compiler_params=_params(2), name="mlp_bwd")(dxo, f, x, u, c, g_pre, g_post, w_up, w_down, conv_w)


def grad_matmul(a, b, bm, bn, name, tk=2048, interleaved=False, after=None):
    t, m = a.shape
    n = b.shape[1]
    tk = min(tk, t)
    nk = t // tk
    place = (lambda j: (j % 2) * 2 + j // 2) if interleaved else (lambda j: j)
    extra = [] if after is None else [after]

    def body(a_ref, b_ref, *rest):
        o_ref, ob_ref = rest[len(extra):]
        kk = pl.program_id(2)

        @pl.when(kk == 0)
        def _():
            o_ref[...] = jnp.zeros_like(o_ref)

        o_ref[...] += _dot_tn(a_ref[...], b_ref[...])

        @pl.when(kk == nk - 1)
        def _():
            ob_ref[...] = o_ref[...].astype(BF16)

    ospec = pl.BlockSpec((None, bm, bn), lambda j, i, kk: (place(j), i, 0))
    return pl.pallas_call(
        body, grid=(n // bn, m // bm, nk),
        in_specs=[pl.BlockSpec((tk, bm), lambda j, i, kk: (kk, i)), pl.BlockSpec((tk, bn), lambda j, i, kk: (kk, j))]
        + [ANY] * len(extra),
        out_specs=[ospec, ospec],
        out_shape=[jax.ShapeDtypeStruct((n // bn, m, bn), F32), jax.ShapeDtypeStruct((n // bn, m, bn), BF16)],
        compiler_params=_params(3), name=name)(a, b, *extra)


def _decay_tables():
    log_gamma = jnp.log(1.0 - 2.0 ** (-5.0 - jnp.arange(RET_HEADS, dtype=F32)))
    i = jnp.arange(RET_CHUNK, dtype=F32)
    rel = i[:, None] - i[None, :]
    intra = jnp.where(rel >= 0, jnp.exp(jnp.maximum(rel, 0.0) * log_gamma[:, None, None]), 0.0)
    cross = jnp.exp((i + 1.0) * log_gamma[:, None])[:, :, None]
    inner = jnp.exp((RET_CHUNK - 1.0 - i) * log_gamma[:, None])[:, :, None]
    chunk = [float(np.exp(np.float32(RET_CHUNK) * np.log(np.float32(1.0 - 2.0 ** (-5.0 - h))).astype(np.float32)))
             for h in range(RET_HEADS)]
    return intra, cross, inner, chunk


def ret_proj(x, g_pre, w_in, cos, sin, tm=512):
    t, d = x.shape
    nt = t // tm
    per = RET_IN_SHARD // RET_QK

    def body(x_ref, g_ref, win_hbm, c_ref, s_ref, pj_ref, h_ref, win_v, sem):
        @pl.when(pl.program_id(0) == 0)
        def _():
            cp = pltpu.make_async_copy(win_hbm, win_v, sem)
            cp.start()
            cp.wait()

        h, _, _ = _rms_fwd(x_ref[...], g_ref[...])
        hb = h.astype(BF16)
        h_ref[...] = hb
        c = c_ref[...]
        s = s_ref[...]
        for j in range(N_SHARD):
            pjj = _dot(hb, win_v[j])
            for bb in range(per):
                b = per * j + bb
                blk = pjj[:, bb * RET_QK:(bb + 1) * RET_QK]
                if b < 2 * RET_HEADS:
                    x1, x2 = blk[:, :128], blk[:, 128:]
                    o1 = x1 * c - x2 * s
                    o2 = x2 * c + x1 * s
                    if b < RET_HEADS:
                        o1 = o1 * (RET_QK ** -0.5)
                        o2 = o2 * (RET_QK ** -0.5)
                    pj_ref[:, b * RET_QK:b * RET_QK + 128] = o1.astype(BF16)
                    pj_ref[:, b * RET_QK + 128:(b + 1) * RET_QK] = o2.astype(BF16)
                else:
                    pj_ref[:, b * RET_QK:(b + 1) * RET_QK] = blk.astype(BF16)

    row = pl.BlockSpec((tm, d), lambda i: (i, 0))
    tab = pl.BlockSpec((tm, 128), lambda i: (i, 0))
    return pl.pallas_call(
        body, grid=(nt,),
        in_specs=[row, _full((1, d)), ANY, tab, tab],
        out_specs=[pl.BlockSpec((tm, RET_IN), lambda i: (i, 0)), row],
        out_shape=[jax.ShapeDtypeStruct((t, RET_IN), BF16), jax.ShapeDtypeStruct((t, d), BF16)],
        scratch_shapes=[pltpu.VMEM(w_in.shape, BF16), pltpu.SemaphoreType.DMA],
        compiler_params=_params(1), name="ret_proj")(x, g_pre, w_in, cos, sin)


def ret_core_fwd(pj, intra, cross, inner, chunk_decay):
    t = pj.shape[0]
    nc = t // RET_CHUNK
    c = RET_CHUNK
    qk_all = RET_HEADS * RET_QK
    v_all = RET_HEADS * RET_V

    def body(q_ref, k_ref, v_ref, dm_ref, cr_ref, in_ref, o_ref, sp_ref, state):
        @pl.when(pl.program_id(0) == 0)
        def _():
            state[...] = jnp.zeros_like(state)

        for h in range(RET_HEADS):
            q = q_ref[:, h * RET_QK:(h + 1) * RET_QK]
            k = k_ref[:, h * RET_QK:(h + 1) * RET_QK]
            v = v_ref[:, h * RET_V:(h + 1) * RET_V]
            sb = state[h].astype(BF16)
            sp_ref[h] = sb
            sc = _dot_nt(q, k) * dm_ref[h]
            o_ref[:, h * RET_V:(h + 1) * RET_V] = _dot(sc.astype(BF16), v) + _dot(q, sb) * cr_ref[h]
            kd = (k.astype(F32) * in_ref[h]).astype(BF16)
            state[h] = state[h] * chunk_decay[h] + _dot_tn(kd, v)

    return pl.pallas_call(
        body, grid=(nc,),
        in_specs=[pl.BlockSpec((c, qk_all), lambda n: (n, 0)), pl.BlockSpec((c, qk_all), lambda n: (n, 1)),
                  pl.BlockSpec((c, v_all), lambda n: (n, 1)),
                  _full(intra.shape), _full(cross.shape), _full(inner.shape)],
        out_specs=[pl.BlockSpec((c, v_all), lambda n: (n, 0)),
                   pl.BlockSpec((None, RET_HEADS, RET_QK, RET_V), lambda n: (n, 0, 0, 0))],
        out_shape=[jax.ShapeDtypeStruct((t, v_all), F32),
                   jax.ShapeDtypeStruct((nc, RET_HEADS, RET_QK, RET_V), BF16)],
        scratch_shapes=[pltpu.VMEM((RET_HEADS, RET_QK, RET_V), F32)],
        compiler_params=_params(1), name="ret_core_fwd")(pj, pj, pj, intra, cross, inner)


def _group_norm(o_h):
    mu = jnp.mean(o_h, axis=-1, keepdims=True)
    dev = o_h - mu
    rstd = lax.rsqrt(jnp.mean(dev * dev, axis=-1, keepdims=True) + EPS)
    return dev * rstd, rstd


def ret_out_fwd(o, pj, x, gn_gain, g_post, w_out, tm=512):
    t, d = x.shape
    nt = t // tm
    v_all = RET_HEADS * RET_V

    def body(o_ref, g_ref, x_ref, gn_ref, g1_ref, w_ref, xo_ref, y_ref, out_ref):
        for h in range(RET_HEADS):
            cols = slice(h * RET_V, (h + 1) * RET_V)
            ohat, _ = _group_norm(o_ref[:, cols])
            g = g_ref[:, cols].astype(F32)
            y_ref[:, cols] = (g * jax.nn.sigmoid(g) * (ohat * gn_ref[:, cols])).astype(BF16)
        out = _dot(y_ref[...], w_ref[...])
        out_ref[...] = out
        m, _, _ = _rms_fwd(out, g1_ref[...])
        xo_ref[...] = x_ref[...] + m

    row = pl.BlockSpec((tm, d), lambda i: (i, 0))
    wide = pl.BlockSpec((tm, v_all), lambda i: (i, 0))
    return pl.pallas_call(
        body, grid=(nt,),
        in_specs=[wide, pl.BlockSpec((tm, v_all), lambda i: (i, 2)), row, _full((1, v_all)), _full((1, d)),
                  _full(w_out.shape)],
        out_specs=[row, wide, row],
        out_shape=[jax.ShapeDtypeStruct((t, d), F32), jax.ShapeDtypeStruct((t, v_all), BF16),
                   jax.ShapeDtypeStruct((t, d), F32)],
        compiler_params=_params(1), name="ret_out_fwd")(o, pj, x, gn_gain, g_post, w_out)


def ret_out_bwd(dxo, out, o, pj, gn_gain, g_post, w_out, tm=256):
    t, d = out.shape
    nt = t // tm
    v_all = RET_HEADS * RET_V

    def body(dxo_ref, out_ref, o_ref, g_ref, gn_ref, g1_ref, w_ref,
             dout_ref, dgate_ref, do_ref, dg1_ref, dgn_ref):
        @pl.when(pl.program_id(0) == 0)
        def _():
            dg1_ref[...] = jnp.zeros_like(dg1_ref)
            dgn_ref[...] = jnp.zeros_like(dgn_ref)

        g1 = g1_ref[...]
        dxo = dxo_ref[...]
        _, oh_, r_ = _rms_fwd(out_ref[...], g1)
        dg1_ref[...] += _colsum(dxo * oh_)
        doutb = _rms_bwd(oh_, r_, g1, dxo).astype(BF16)
        dout_ref[...] = doutb
        dy = _dot_nt(doutb, w_ref[...])
        for h in range(RET_HEADS):
            cols = slice(h * RET_V, (h + 1) * RET_V)
            gn = gn_ref[:, cols]
            ohat, rstd = _group_norm(o_ref[:, cols])
            g = g_ref[:, cols].astype(F32)
            sg = jax.nn.sigmoid(g)
            dyh = dy[:, cols]
            dgate_ref[:, cols] = (dyh * (ohat * gn) * (sg * (1.0 + g * (1.0 - sg)))).astype(BF16)
            don = dyh * (g * sg)
            dgn_ref[:, cols] += _colsum(don * ohat)
            dohat = don * gn
            do_ref[:, cols] = (rstd * (dohat - jnp.mean(dohat, axis=-1, keepdims=True)
                                       - ohat * jnp.mean(dohat * ohat, axis=-1, keepdims=True))).astype(BF16)

    row = pl.BlockSpec((tm, d), lambda i: (i, 0))
    wide = pl.BlockSpec((tm, v_all), lambda i: (i, 0))
    gate = pl.BlockSpec((tm, v_all), lambda i: (i, 2))
    return pl.pallas_call(
        body, grid=(nt,),
        in_specs=[row, row, wide, gate, _full((1, v_all)), _full((1, d)), _full(w_out.shape)],
        out_specs=[row, gate, wide, _full((1, d)), _full((1, v_all))],
        out_shape=[jax.ShapeDtypeStruct((t, d), BF16), jax.ShapeDtypeStruct((t, RET_IN), BF16),
                   jax.ShapeDtypeStruct((t, v_all), BF16), jax.ShapeDtypeStruct((1, d), F32),
                   jax.ShapeDtypeStruct((1, v_all), F32)],
        compiler_params=_params(1), name="ret_out_bwd")(dxo, out, o, pj, gn_gain, g_post, w_out)


def ret_core_bwd(pj, do, sprev, cos, sin, dpj, intra, cross, inner, chunk_decay):
    t = pj.shape[0]
    nc = t // RET_CHUNK
    c = RET_CHUNK
    qk_all = RET_HEADS * RET_QK
    v_all = RET_HEADS * RET_V
    scale = RET_QK ** -0.5

    def body(q_ref, k_ref, v_ref, do_ref, sp_ref, c_ref, s_ref, dm_ref, cr_ref, in_ref, dpj_in, dpj_ref, dstate):
        @pl.when(pl.program_id(0) == 0)
        def _():
            dstate[...] = jnp.zeros_like(dstate)

        cs = c_ref[...]
        sn = s_ref[...]
        for h in range(RET_HEADS):
            q = q_ref[:, h * RET_QK:(h + 1) * RET_QK]
            k = k_ref[:, h * RET_QK:(h + 1) * RET_QK]
            v = v_ref[:, h * RET_V:(h + 1) * RET_V]
            doh = do_ref[:, h * RET_V:(h + 1) * RET_V]
            dm = dm_ref[h]
            ab = (_dot_nt(q, k) * dm).astype(BF16)
            dab = (_dot_nt(doh, v) * dm).astype(BF16)
            dsb = dstate[h].astype(BF16)
            kd = (k.astype(F32) * in_ref[h]).astype(BF16)
            dv = _dot_tn(ab, doh) + _dot(kd, dsb)
            dq = _dot(dab, k) + cr_ref[h] * _dot_nt(doh, sp_ref[h])
            dk = _dot_tn(dab, q) + in_ref[h] * _dot_nt(v, dsb)
            qd = (q.astype(F32) * cr_ref[h]).astype(BF16)
            dstate[h] = dstate[h] * chunk_decay[h] + _dot_tn(qd, doh)
            for base, dd, sc in ((h * RET_QK, dq, scale), (qk_all + h * RET_QK, dk, 1.0)):
                d1, d2 = dd[:, :128], dd[:, 128:]
                dpj_ref[:, base:base + 128] = ((d1 * cs + d2 * sn) * sc).astype(BF16)
                dpj_ref[:, base + 128:base + RET_QK] = ((d2 * cs - d1 * sn) * sc).astype(BF16)
            dpj_ref[:, 2 * qk_all + h * RET_V:2 * qk_all + (h + 1) * RET_V] = dv.astype(BF16)

    rev = lambda n: nc - 1 - n
    tab = pl.BlockSpec((c, 128), lambda n: (rev(n), 0))
    return pl.pallas_call(
        body, grid=(nc,),
        in_specs=[pl.BlockSpec((c, qk_all), lambda n: (rev(n), 0)), pl.BlockSpec((c, qk_all), lambda n: (rev(n), 1)),
                  pl.BlockSpec((c, v_all), lambda n: (rev(n), 1)), pl.BlockSpec((c, v_all), lambda n: (rev(n), 0)),
                  pl.BlockSpec((None, RET_HEADS, RET_QK, RET_V), lambda n: (rev(n), 0, 0, 0)),
                  tab, tab, _full(intra.shape), _full(cross.shape), _full(inner.shape), ANY],
        out_specs=pl.BlockSpec((c, 2 * qk_all + v_all), lambda n: (rev(n), 0)),
        out_shape=jax.ShapeDtypeStruct((t, RET_IN), BF16),
        scratch_shapes=[pltpu.VMEM((RET_HEADS, RET_QK, RET_V), F32)],
        input_output_aliases={10: 0},
        compiler_params=_params(1), name="ret_core_bwd")(pj, pj, pj, do, sprev, cos, sin, intra, cross, inner, dpj)


def ret_in_bwd(dpj, dres, x, g_pre, w_in, tm=512):
    t, d = x.shape
    nt = t // tm

    def body(dpj_ref, dres_ref, x_ref, g_ref, win_hbm, dx_ref, dg_ref, win_v, sem):
        @pl.when(pl.program_id(0) == 0)
        def _():
            cp = pltpu.make_async_copy(win_hbm, win_v, sem)
            cp.start()
            dg_ref[...] = jnp.zeros_like(dg_ref)
            cp.wait()

        g = g_ref[...]
        dh = jnp.zeros((tm, d), F32)
        for j in range(N_SHARD):
            dh = dh + _dot_nt(dpj_ref[:, j * RET_IN_SHARD:(j + 1) * RET_IN_SHARD], win_v[j])
        _, xh, rx = _rms_fwd(x_ref[...], g)
        dg_ref[...] += _colsum(dh * xh)
        dx_ref[...] = dres_ref[...] + _rms_bwd(xh, rx, g, dh)

    row = pl.BlockSpec((tm, d), lambda i: (i, 0))
    return pl.pallas_call(
        body, grid=(nt,),
        in_specs=[pl.BlockSpec((tm, RET_IN), lambda i: (i, 0)), row, row, _full((1, d)), ANY],
        out_specs=[row, _full((1, d))],
        out_shape=[jax.ShapeDtypeStruct((t, d), F32), jax.ShapeDtypeStruct((1, d), F32)],
        scratch_shapes=[pltpu.VMEM(w_in.shape, BF16), pltpu.SemaphoreType.DMA],
        compiler_params=_params(1), name="ret_in_bwd")(dpj, dres, x, g_pre, w_in)


_CHIP_FLIPS = ((1, 0), (0, 1), (1, 1))


def _flip(v, b):
    return 1 - v if b else v


def scatter_grads(big, small):
    n = len(big)

    def body(*refs):
        ins, small_in = refs[:n], refs[n]
        outs, small_out = refs[n + 1:2 * n + 1], refs[2 * n + 1]
        send_sems, recv_sems, ssend_sems, srecv_sems, local_sem = refs[2 * n + 2:]
        x, y, c = lax.axis_index("x"), lax.axis_index("y"), lax.axis_index("c")
        mine = 4 * x + 2 * y + c
        copies = [pltpu.make_async_copy(small_in, small_out.at[mine], local_sem)]
        copies[0].start()
        for m in range(1, 8):
            bx, by, bc = (m >> 2) & 1, (m >> 1) & 1, m & 1
            cp = pltpu.make_async_remote_copy(
                src_ref=small_in, dst_ref=small_out.at[mine], send_sem=ssend_sems.at[m - 1],
                recv_sem=srecv_sems.at[m - 1], device_id=(_flip(x, bx), _flip(y, by), _flip(c, bc)),
                device_id_type=MESH)
            cp.start()
            copies.append(cp)
        for t in range(n):
            for k, (bx, by) in enumerate(_CHIP_FLIPS):
                px, py = _flip(x, bx), _flip(y, by)
                cp = pltpu.make_async_remote_copy(
                    src_ref=ins[t].at[2 * px + py], dst_ref=outs[t].at[k], send_sem=send_sems.at[3 * t + k],
                    recv_sem=recv_sems.at[3 * t + k], device_id=(px, py, c), device_id_type=MESH)
                cp.start()
                copies.append(cp)
        for cp in copies:
            cp.wait()

    return pl.pallas_call(
        body, in_specs=[ANY] * (n + 1), out_specs=[ANY] * (n + 1),
        out_shape=[jax.ShapeDtypeStruct((3,) + b.shape[1:], b.dtype) for b in big]
        + [jax.ShapeDtypeStruct((8,) + small.shape, small.dtype)],
        scratch_shapes=[pltpu.SemaphoreType.DMA((3 * n,)), pltpu.SemaphoreType.DMA((3 * n,)),
                        pltpu.SemaphoreType.DMA((7,)), pltpu.SemaphoreType.DMA((7,)), pltpu.SemaphoreType.DMA],
        name="scatter_grads")(*big, small)


def swap_cores(arrays, name):
    n = len(arrays)

    def body(*refs):
        ins, outs = refs[:n], refs[n:2 * n]
        send_sems, recv_sems = refs[2 * n:]
        sibling = (lax.axis_index("x"), lax.axis_index("y"), 1 - lax.axis_index("c"))
        copies = []
        for t in range(n):
            cp = pltpu.make_async_remote_copy(
                src_ref=ins[t], dst_ref=outs[t], send_sem=send_sems.at[t], recv_sem=recv_sems.at[t],
                device_id=sibling, device_id_type=MESH)
            cp.start()
            copies.append(cp)
        for cp in copies:
            cp.wait()

    return pl.pallas_call(
        body, in_specs=[ANY] * n, out_specs=[ANY] * n,
        out_shape=[jax.ShapeDtypeStruct(a.shape, a.dtype) for a in arrays],
        scratch_shapes=[pltpu.SemaphoreType.DMA((n,)), pltpu.SemaphoreType.DMA((n,))],
        name=name)(*arrays)


_HBM = pl.BlockSpec(memory_space=pltpu.HBM)
_SEM = pl.BlockSpec(memory_space=pltpu.SEMAPHORE)
_EFFECT = pltpu.SideEffectType.DATAFLOW_SIDE_EFFECTING


def _chip_copies(mode, srcs, lands, send_sems, recv_sems):
    x, y, c = lax.axis_index("x"), lax.axis_index("y"), lax.axis_index("c")
    copies = []
    for t in range(len(lands)):
        for k, (bx, by) in enumerate(_CHIP_FLIPS):
            px, py = _flip(x, bx), _flip(y, by)
            target = (px, py, c)
            if mode == "gather":
                src, dst = srcs[t], lands[t].at[2 * x + y]
            elif mode == "gather_half":
                half = pl.ds(c * (srcs[t].shape[0] // 2), srcs[t].shape[0] // 2)
                src, dst = srcs[t].at[half], lands[t].at[2 * x + y, half]
            elif mode == "forward_half":
                half = pl.ds(c * (lands[t].shape[1] // 2), lands[t].shape[1] // 2)
                src = dst = lands[t].at[2 * px + py, half]
                target = (x, y, 1 - c)
            else:
                src, dst = srcs[t].at[2 * px + py], lands[t].at[k]
            copies.append(pltpu.make_async_remote_copy(
                src_ref=src, dst_ref=dst, send_sem=send_sems.at[3 * t + k], recv_sem=recv_sems.at[3 * t + k],
                device_id=target, device_id_type=MESH))
    return copies


def exchange_start(mode, srcs, lands, name, after=None):
    n, ns = len(lands), len(srcs)
    extra = [] if after is None else [after]

    def body(*refs):
        ins, lnd = refs[:ns], refs[ns:ns + n]
        send_sems, recv_sems = refs[ns + n + len(extra)], refs[ns + n + len(extra) + 1]
        token = refs[-1]
        for cp in _chip_copies(mode, ins, lnd, send_sems, recv_sems):
            cp.start()
        token[...] = jnp.zeros(token.shape, token.dtype)

    hbm = lambda a: pltpu.with_memory_space_constraint(a, pltpu.HBM)
    passed = list(srcs) + list(lands)
    return pl.pallas_call(
        body, name=name,
        out_shape=(pltpu.SemaphoreType.DMA((3 * n,)), pltpu.SemaphoreType.DMA((3 * n,)),
                   *[pltpu.HBM(a.shape, a.dtype) for a in passed], jax.ShapeDtypeStruct((8, 128), F32)),
        in_specs=[_HBM] * (ns + n) + [ANY] * len(extra),
        out_specs=(_SEM, _SEM, *[_HBM] * (ns + n), pl.BlockSpec(memory_space=pltpu.VMEM)),
        input_output_aliases={i: 2 + i for i in range(ns + n)},
        compiler_params=pltpu.CompilerParams(has_side_effects=_EFFECT))(*[hbm(a) for a in passed], *extra)


def exchange_wait(mode, started, after, name):
    send_sems, recv_sems = started[0], started[1]
    passed = list(started[2:-1])
    n = len(passed) if mode == "forward_half" else len(passed) // 2
    ns = len(passed) - n

    def body(*refs):
        ins, lnd = refs[:ns], refs[ns:ns + n]
        for cp in _chip_copies(mode, ins, lnd, refs[ns + n], refs[ns + n + 1]):
            cp.wait_send()
            cp.wait_recv()

    outs = pl.pallas_call(
        body, name=name, out_shape=tuple(pltpu.HBM(a.shape, a.dtype) for a in passed),
        in_specs=[_HBM] * (ns + n) + [_SEM, _SEM, ANY], out_specs=tuple([_HBM] * (ns + n)),
        input_output_aliases={i: i for i in range(ns + n)},
        compiler_params=pltpu.CompilerParams(has_side_effects=_EFFECT))(*passed, send_sems, recv_sems, after)
    return list(outs[ns:])


def plane_sum(slot, full, recv, name, bm=256):
    _, m, n = full.shape
    bm = _row_block(m, bm)

    def body(slot_ref, o_ref, r_ref, s_ref):
        s_ref[...] = ((o_ref[...] + r_ref[0].astype(F32)) + r_ref[1].astype(F32)) + r_ref[2].astype(F32)

    return pl.pallas_call(
        body,
        grid_spec=pltpu.PrefetchScalarGridSpec(
            num_scalar_prefetch=1, grid=(m // bm,),
            in_specs=[pl.BlockSpec((None, bm, n), lambda i, s: (s[0], i, 0)),
                      pl.BlockSpec((3, bm, n), lambda i, s: (0, i, 0))],
            out_specs=pl.BlockSpec((bm, n), lambda i, s: (i, 0))),
        out_shape=jax.ShapeDtypeStruct((m, n), F32), compiler_params=_params(1), name=name)(slot, full, recv)


def sum_slots(parts, name, bm=312):
    _, r, n = parts.shape
    bm = bm if r % bm == 0 else r

    def body(p_ref, s_ref):
        acc = p_ref[0]
        for k in range(1, 8):
            acc = acc + p_ref[k]
        s_ref[...] = acc

    return pl.pallas_call(
        body, grid=(r // bm,), in_specs=[pl.BlockSpec((8, bm, n), lambda i: (0, i, 0))],
        out_specs=pl.BlockSpec((bm, n), lambda i: (i, 0)), out_shape=jax.ShapeDtypeStruct((r, n), F32),
        compiler_params=_params(1), name=name)(parts)


def _adamw_math(w, g, m, v):
    m = ADAM_B1 * m + (1.0 - ADAM_B1) * g
    v = ADAM_B2 * v + (1.0 - ADAM_B2) * (g * g)
    m_hat = m / (1.0 - ADAM_B1 ** ADAM_STEP)
    v_hat = v / (1.0 - ADAM_B2 ** ADAM_STEP)
    delta = -ADAM_LR * (m_hat / (jnp.sqrt(v_hat) + ADAM_EPS) + ADAM_WD * w)
    return delta, m, v


def adamw(w, m, v, grads, layer, prev, name, bm=256):
    _, mm, n = w.shape
    bm = _row_block(mm, bm)
    ng = len(grads)

    def body(*refs):
        w_ref, m_ref, v_ref = refs[:3]
        g_refs = refs[3:3 + ng]
        g_out, d_out, m_out, v_out = refs[-4:]
        g = g_refs[0][...]
        for gr in g_refs[1:]:
            g = g + gr[...]
        delta, mn, vn = _adamw_math(w_ref[...], g, m_ref[...], v_ref[...])
        g_out[...] = g
        d_out[...] = delta
        m_out[...] = mn
        v_out[...] = vn

    slab = pl.BlockSpec((None, bm, n), lambda i: (layer, i, 0))
    flat = pl.BlockSpec((bm, n), lambda i: (i, 0))
    in_specs = [slab] * 3 + [flat] * ng
    args = [w, m, v, *grads]
    aliases = {}
    if prev is not None:
        in_specs += [ANY] * 4
        aliases = {3 + ng + q: q for q in range(4)}
        args += list(prev)
    return pl.pallas_call(
        body, grid=(mm // bm,), in_specs=in_specs, out_specs=[slab] * 4,
        out_shape=[jax.ShapeDtypeStruct(w.shape, F32)] * 4, input_output_aliases=aliases,
        compiler_params=_params(1), name=name)(*args)


def _pack_rows(parts, rows):
    flat = jnp.concatenate([p.reshape(-1) for p in parts])
    return jnp.pad(flat, (0, rows * 128 - flat.shape[0])).reshape(rows, 128)


def _as_shards(a, rows):
    return a.reshape(N_SHARD, rows, a.shape[-1])


def _local_step(x, pos_col, target, gains, pool_w, pool_scale, gn_gain, conv_w, conv_b, weights, send_grads):
    def gain(l, n, token=None):
        g = gains[l, n].reshape(1, D_MODEL)
        return g if token is None else g + token[0:1, 0:1]

    inv_freq = (ROPE_BASE ** (-jnp.arange(0, RET_QK, 2, dtype=F32) / RET_QK)).reshape(1, RET_QK // 2)
    cos, sin = rope_tables(pos_col, inv_freq)
    intra, cross, inner, chunk_decay = _decay_tables()
    dn_rows = D_FF // N_SHARD

    x1 = pool_fwd(x, gain(0, 0), gain(0, 1), pool_w, pool_scale)
    w_up0, w_dn0 = weights("mlp0", x1)
    w_dn0 = w_dn0.reshape(D_FF, D_MODEL)
    x2, u0, c0, f0, h0 = mlp_fwd(x1, gain(0, 2), gain(0, 3), w_up0, w_dn0, conv_w[0], conv_b[0])
    w_in, w_out = weights("ret", x2)
    w_out = w_out.reshape(RET_HEADS * RET_V, D_MODEL)
    pj, hr = ret_proj(x2, gain(1, 0), w_in, cos, sin)
    o, sprev = ret_core_fwd(pj, intra, cross, inner, chunk_decay)
    x3, yb, out = ret_out_fwd(o, pj, x2, gn_gain, gain(1, 1), w_out)
    w_up1, w_dn1 = weights("mlp1", x3)
    w_dn1 = w_dn1.reshape(D_FF, D_MODEL)
    dx4, u1, c1, f1, h1, loss = mlp_fwd(x3, gain(1, 2), gain(1, 3), w_up1, w_dn1, conv_w[1], conv_b[1], target)

    dx3, du1, a1, df1, dg12, dg13, dcw1, dcb1 = mlp_bwd(
        dx4, f1, x3, u1, c1, gain(1, 2), gain(1, 3), w_up1, w_dn1, conv_w[1])
    dwup1 = grad_matmul(h1, du1, D_MODEL, FF_CHUNK, "grad_w_up_1", interleaved=True)
    dwdn1 = grad_matmul(a1, df1, FF_CHUNK, D_MODEL, "grad_w_down_1")
    tok = send_grads("mlp1", [dwup1, [_as_shards(g, dn_rows) for g in dwdn1]])
    dout, dpj, do, dg11, dgn = ret_out_bwd(dx3, out, o, pj, gn_gain, gain(1, 1, tok), w_out)
    dwout = grad_matmul(yb, dout, 1024, D_MODEL, "grad_w_out")
    dpj = ret_core_bwd(pj, do, sprev, cos, sin, dpj, intra, cross, inner, chunk_decay)
    dwin = grad_matmul(hr, dpj, D_MODEL, RET_IN_SHARD, "grad_w_in")
    tok = send_grads("ret", [dwin, [_as_shards(g, RET_V) for g in dwout]])
    dx2, dg10 = ret_in_bwd(dpj, dx3, x2, gain(1, 0, tok), w_in)
    dx1, du0, a0, df0, dg02, dg03, dcw0, dcb0 = mlp_bwd(
        dx2, f0, x1, u0, c0, gain(0, 2), gain(0, 3), w_up0, w_dn0, conv_w[0])
    dwdn0 = grad_matmul(a0, df0, FF_CHUNK, D_MODEL, "grad_w_down_0")
    tok = send_grads("down0", [[_as_shards(g, dn_rows) for g in dwdn0]])
    dwup0 = grad_matmul(h0, du0, D_MODEL, FF_CHUNK, "grad_w_up_0", interleaved=True, after=tok)
    tok = send_grads("up0", [dwup0])
    dx0, dg00, dg01, dpscale, dpw = pool_bwd(dx1, x, gain(0, 0, tok), gain(0, 1), pool_w, pool_scale)

    rows = lambda g: jnp.sum(g, axis=0, keepdims=True)
    dgains = jnp.concatenate([dg00, dg01, rows(dg02), rows(dg03), dg10, dg11, rows(dg12), rows(dg13)],
                             axis=0).reshape(2, 4, D_MODEL)
    small = {"gains": dgains, "pool_scale": dpscale, "gn": dgn,
             "conv_w": jnp.sum(jnp.stack([dcw0, dcw1]), axis=3),
             "conv_b": jnp.sum(jnp.stack([dcb0, dcb1]), axis=2, keepdims=True), "pool_w": dpw}
    return loss, dx0, small


def kernel(x, positions, norm_gain, pool_w, pool_scale, ret_w_in, ret_gn_gain, ret_w_out, mlp_w_up, mlp_conv_w, mlp_conv_b, mlp_w_down, loss_target, m_norm_gain, m_pool_w, m_pool_scale, m_ret_w_in, m_ret_gn_gain, m_ret_w_out, m_mlp_w_up, m_mlp_conv_w, m_mlp_conv_b, m_mlp_w_down, v_norm_gain, v_pool_w, v_pool_scale, v_ret_w_in, v_ret_gn_gain, v_ret_w_out, v_mlp_w_up, v_mlp_conv_w, v_mlp_conv_b, v_mlp_w_down):
    t = x.shape[1]
    me = 2 * lax.axis_index("x") + lax.axis_index("y")
    me_slot = jnp.reshape(me, (1,)).astype(jnp.int32)

    small_parts = [norm_gain, ret_gn_gain, mlp_conv_w, pool_w]
    small_sizes = [p.size for p in small_parts]
    small_rows = -(-sum(small_sizes) // (128 * 8)) * 8
    groups = {"small": [_pack_rows(small_parts, small_rows)],
              "mlp0": [mlp_w_up[0].astype(BF16), mlp_w_down[0].astype(BF16)],
              "ret": [ret_w_in[0].astype(BF16), ret_w_out[0].astype(BF16)],
              "mlp1": [mlp_w_up[1].astype(BF16), mlp_w_down[1].astype(BF16)]}
    gathers, token = {}, None
    for group, srcs in groups.items():
        lands = [lax.dynamic_update_index_in_dim(lax.empty((N_SHARD,) + s.shape, s.dtype), s, me, 0) for s in srcs]
        mode = "gather_half" if group == "mlp0" else "gather"
        gathers[group] = (mode, exchange_start(mode, srcs, lands, "gather_start_" + group, after=token))
        token = gathers[group][1][-1]

    def weights(group, after):
        mode, started = gathers[group]
        lands = exchange_wait(mode, started, after, "gather_wait_" + group)
        if mode == "gather_half":
            forward = exchange_start("forward_half", [], lands, "forward_start_" + group)
            lands = exchange_wait("forward_half", forward, forward[-1], "forward_wait_" + group)
        return lands

    sent = {}

    def send_grads(group, pairs):
        lands = [lax.empty((3,) + b.shape[1:], BF16) for _, b in pairs]
        sent[group] = (exchange_start("scatter", [b for _, b in pairs], lands, "scatter_start_" + group),
                       [f for f, _ in pairs])
        return sent[group][0][-1]

    def reduced(group, after, names):
        started, own = sent[group]
        recv = exchange_wait("scatter", started, after, "scatter_wait_" + group)
        return [plane_sum(me_slot, f, r, "plane_sum_" + nm)
                for f, r, nm in zip(own, recv, names)]

    (smallg,) = weights("small", token)
    smallg = smallg.reshape(N_SHARD, -1)
    offs = np.cumsum([0] + small_sizes)
    piece = lambda i, shape: smallg[:, offs[i]:offs[i + 1]].reshape((N_SHARD,) + shape)
    gains = piece(0, (2, 4, 256)).transpose(1, 2, 0, 3).reshape(2, 4, D_MODEL)
    gn_full = piece(1, (512,)).reshape(1, RET_HEADS * RET_V)
    cw_full = piece(2, (2, 3, FF_CHUNK)).transpose(1, 0, 2, 3)
    pw_full = piece(3, (4, 64, 256)).transpose(1, 0, 2, 3).reshape(4, 256, 256).astype(BF16)
    cb_full = mlp_conv_b.reshape(2, N_SHARD, 1, FF_CHUNK)

    loss, dx0, small = _local_step(
        x[0], positions.reshape(t, 1).astype(F32), loss_target[0], gains, pw_full, pool_scale, gn_full,
        cw_full, cb_full, weights, send_grads)

    def small_adamw(w, m, v, grads, name):
        w3 = w.reshape(1, -1, w.shape[-1])
        out = adamw(w3, m.reshape(w3.shape), v.reshape(w3.shape), [g.reshape(w3.shape[1:]) for g in grads], 0, None, name)
        return [o.reshape(w.shape) for o in out]

    res = {}
    planes_a = reduced("mlp1", dx0, ["w_up_1", "w_down_1"]) + reduced("ret", dx0, ["w_in", "w_out"])
    others_a = swap_cores(planes_a, "swap_cores_a")
    res["ret_w_in"] = adamw(ret_w_in, m_ret_w_in, v_ret_w_in, (planes_a[2], others_a[2]), 0, None, "adamw_w_in")
    res["ret_w_out"] = adamw(ret_w_out, m_ret_w_out, v_ret_w_out, (planes_a[3], others_a[3]), 0, None, "adamw_w_out")
    up1 = adamw(mlp_w_up, m_mlp_w_up, v_mlp_w_up, (planes_a[0], others_a[0]), 1, None, "adamw_w_up_1")
    dn1 = adamw(mlp_w_down, m_mlp_w_down, v_mlp_w_down, (planes_a[1], others_a[1]), 1, None, "adamw_w_down_1")

    pw_f = small["pool_w"].reshape(4, N_SHARD, 64, 256).transpose(1, 0, 2, 3).reshape(N_SHARD, 256, 256)
    small_order = ["gains", "pool_scale", "gn", "conv_w", "conv_b"]
    gsmall_sizes = [small[k].size for k in small_order]
    gsmall_rows = -(-sum(gsmall_sizes) // (128 * 8)) * 8
    planes_b = reduced("up0", dn1[0], ["w_up_0"]) + reduced("down0", dn1[0], ["w_down_0"])
    pw_recv, small_recv = scatter_grads([pw_f.astype(BF16)], _pack_rows([small[k] for k in small_order], gsmall_rows))
    planes_b.append(plane_sum(me_slot, pw_f, pw_recv, "plane_sum_pool_w"))
    others_b = swap_cores(planes_b, "swap_cores_b")
    res["mlp_w_up"] = adamw(mlp_w_up, m_mlp_w_up, v_mlp_w_up, (planes_b[0], others_b[0]), 0, up1, "adamw_w_up_0")
    res["mlp_w_down"] = adamw(mlp_w_down, m_mlp_w_down, v_mlp_w_down, (planes_b[1], others_b[1]), 0, dn1,
                              "adamw_w_down_0")
    res["pool_w"] = small_adamw(pool_w, m_pool_w, v_pool_w, (planes_b[2], others_b[2]), "adamw_pool_w")

    gsmall = sum_slots(small_recv, "sum_small").reshape(-1)
    goffs = np.cumsum([0] + gsmall_sizes)
    gpiece = lambda i: gsmall[goffs[i]:goffs[i + 1]].reshape(small[small_order[i]].shape)
    g_gains = lax.dynamic_slice_in_dim(gpiece(0), me * 256, 256, axis=2)
    g_gn = lax.dynamic_slice_in_dim(gpiece(2), me * RET_V, RET_V, axis=1)
    g_cw = lax.dynamic_index_in_dim(gpiece(3), me, 1, keepdims=False)
    res["norm_gain"] = small_adamw(norm_gain, m_norm_gain, v_norm_gain, [g_gains], "adamw_norm_gain")
    res["pool_scale"] = small_adamw(pool_scale, m_pool_scale, v_pool_scale, [gpiece(1)], "adamw_pool_scale")
    res["ret_gn_gain"] = small_adamw(ret_gn_gain, m_ret_gn_gain, v_ret_gn_gain, [g_gn], "adamw_gn_gain")
    res["mlp_conv_w"] = small_adamw(mlp_conv_w, m_mlp_conv_w, v_mlp_conv_w, [g_cw], "adamw_conv_w")
    res["mlp_conv_b"] = small_adamw(mlp_conv_b, m_mlp_conv_b, v_mlp_conv_b, [gpiece(4)], "adamw_conv_b")

    order = ["norm_gain", "pool_w", "pool_scale", "ret_w_in", "ret_gn_gain", "ret_w_out", "mlp_w_up", "mlp_conv_w",
             "mlp_conv_b", "mlp_w_down"]
    total_loss = lax.psum(loss[0, 0], ("x", "y", "c"))
    outs = [total_loss, dx0.reshape(x.shape)]
    for q in range(4):
        outs += [res[k][q] for k in order]
    return tuple(outs)
```

```python
import numpy as np
import jax
import jax.numpy as jnp
from jax import lax
from jax.experimental import pallas as pl
from jax.experimental.pallas import tpu as pltpu

F32 = jnp.float32
BF16 = jnp.bfloat16

D_MODEL = 1024
D_FF = 2816
FF_CHUNK = 1408
N_SHARD = 4
POOL_WINDOWS = (2, 4, 8, 16)
POOL_DIM = 256
POOL_HALO = 16
RET_HEADS = 4
RET_QK = 256
RET_V = 512
RET_CHUNK = 256
RET_IN = 6144
RET_IN_SHARD = 1536
ROPE_BASE = 10000.0
EPS = 1e-6
CONV_HALO = 8

ADAM_LR, ADAM_B1, ADAM_B2, ADAM_EPS, ADAM_WD, ADAM_STEP = 0.001, 0.9, 0.999, 1e-08, 0.01, 10

VMEM_LIMIT = 56 * 1024 * 1024
MESH = pl.DeviceIdType.MESH
ANY = pl.BlockSpec(memory_space=pl.ANY)


def _params(n_grid=1, limit=VMEM_LIMIT):
    return pltpu.CompilerParams(dimension_semantics=("arbitrary",) * n_grid, vmem_limit_bytes=limit)


def _dot(a, b):
    return jnp.dot(a, b, preferred_element_type=F32)


def _dot_nt(a, b):
    return lax.dot_general(a, b, (((1,), (1,)), ((), ())), preferred_element_type=F32)


def _dot_tn(a, b):
    return lax.dot_general(a, b, (((0,), (0,)), ((), ())), preferred_element_type=F32)


def _rms_fwd(x, gain):
    r = lax.rsqrt(jnp.mean(x * x, axis=-1, keepdims=True) + EPS)
    xh = x * r
    return xh * gain, xh, r


def _rms_bwd(xh, r, gain, dy):
    dxh = dy * gain
    return r * (dxh - xh * jnp.mean(dxh * xh, axis=-1, keepdims=True))


def _colsum(v):
    return jnp.sum(v, axis=0, keepdims=True)


def _full(shape):
    nd = len(shape)
    return pl.BlockSpec(shape, lambda *_: (0,) * nd)


def rope_tables(pos_col, inv_freq):
    t = pos_col.shape[0]
    tm = min(t, 1024)

    def body(p_ref, f_ref, c_ref, s_ref):
        ang = p_ref[...] * f_ref[...]
        c_ref[...] = jnp.cos(ang)
        s_ref[...] = jnp.sin(ang)

    return pl.pallas_call(
        body, grid=(t // tm,),
        in_specs=[pl.BlockSpec((tm, 1), lambda i: (i, 0)), _full((1, 128))],
        out_specs=[pl.BlockSpec((tm, 128), lambda i: (i, 0))] * 2,
        out_shape=[jax.ShapeDtypeStruct((t, 128), F32)] * 2,
        compiler_params=_params(1), name="rope_tables")(pos_col, inv_freq)


def _window_sums(ext, backward):
    n = ext.shape[0]
    cur, sums = ext, []
    for g, win in enumerate(POOL_WINDOWS):
        if g > 0:
            cur = cur[:, POOL_DIM:]
        half = win // 2
        cur = cur + pltpu.roll(cur, n - half if backward else half, axis=0)
        sums.append(cur[:, 0:POOL_DIM])
    return sums


def _pool_diff(h_halo, h, row0, tm):
    t_idx = row0 + lax.broadcasted_iota(jnp.int32, (tm, 1), 0)
    sums = _window_sums(jnp.concatenate([h_halo, h], axis=0), backward=False)
    parts, inv_counts = [], []
    for g, win in enumerate(POOL_WINDOWS):
        inv = 1.0 / jnp.minimum(t_idx + 1, win).astype(F32)
        parts.append(sums[g][POOL_HALO:, :] * inv - h[:, g * POOL_DIM:(g + 1) * POOL_DIM])
        inv_counts.append(inv)
    return parts, inv_counts


def pool_fwd(x, g_pre, g_post, pool_w, pool_scale, tm=512):
    t, d = x.shape
    nt = t // tm

    def body(x_ref, g0_ref, g1_ref, w_ref, sc_ref, o_ref, hext):
        i = pl.program_id(0)

        @pl.when(i == 0)
        def _():
            hext[...] = jnp.zeros((POOL_HALO, d), F32)

        xv = x_ref[...]
        h, _, _ = _rms_fwd(xv, g0_ref[...])
        parts, _ = _pool_diff(hext[...], h, i * tm, tm)
        hext[...] = h[tm - POOL_HALO:tm, :]
        ys = [_dot(parts[g].astype(BF16), w_ref[g]) for g in range(len(POOL_WINDOWS))]
        y = jnp.concatenate(ys, axis=-1) * sc_ref[...]
        m, _, _ = _rms_fwd(y, g1_ref[...])
        o_ref[...] = xv + m

    row = pl.BlockSpec((tm, d), lambda i: (i, 0))
    return pl.pallas_call(
        body, grid=(nt,),
        in_specs=[row, _full((1, d)), _full((1, d)), _full(pool_w.shape), _full((1, d))],
        out_specs=row, out_shape=jax.ShapeDtypeStruct((t, d), F32),
        scratch_shapes=[pltpu.VMEM((POOL_HALO, d), F32)],
        compiler_params=_params(1), name="pool_fwd")(x, g_pre, g_post, pool_w, pool_scale)


def pool_bwd(dx1, x, g_pre, g_post, pool_w, pool_scale, tm=512):
    t, d = x.shape
    nt = t // tm
    ng = len(POOL_WINDOWS)

    def body(dx1_ref, x_ref, xh_ref, g0_ref, g1_ref, w_ref, sc_ref,
             dx_ref, dg0_ref, dg1_ref, dsc_ref, dw_ref, enext):
        i = pl.program_id(0)
        r = nt - 1 - i

        @pl.when(i == 0)
        def _():
            enext[...] = jnp.zeros((POOL_HALO, d), F32)
            dg0_ref[...] = jnp.zeros_like(dg0_ref)
            dg1_ref[...] = jnp.zeros_like(dg1_ref)
            dsc_ref[...] = jnp.zeros_like(dsc_ref)
            dw_ref[...] = jnp.zeros_like(dw_ref)

        g0 = g0_ref[...]
        g1 = g1_ref[...]
        sc = sc_ref[...]
        xv = x_ref[...]
        h, xh, rx = _rms_fwd(xv, g0)
        h_halo, _, _ = _rms_fwd(xh_ref[...], g0)
        parts, inv_counts = _pool_diff(h_halo * jnp.where(r > 0, 1.0, 0.0), h, r * tm, tm)
        parts_b = [p.astype(BF16) for p in parts]
        ypre = jnp.concatenate([_dot(parts_b[g], w_ref[g]) for g in range(ng)], axis=-1)
        _, yh, ry = _rms_fwd(ypre * sc, g1)
        dm = dx1_ref[...]
        dg1_ref[...] += _colsum(dm * yh)
        dy = _rms_bwd(yh, ry, g1, dm)
        dsc_ref[...] += _colsum(dy * ypre)
        dyp = (dy * sc).astype(BF16)
        ddiffs = []
        for g in range(ng):
            cols = slice(g * POOL_DIM, (g + 1) * POOL_DIM)
            dw_ref[g] += _dot_tn(parts_b[g], dyp[:, cols])
            ddiffs.append(_dot_nt(dyp[:, cols], w_ref[g]))
        e = jnp.concatenate([ddiffs[g] * inv_counts[g] for g in range(ng)], axis=-1)
        sums = _window_sums(jnp.concatenate([e, enext[...]], axis=0), backward=True)
        enext[...] = e[0:POOL_HALO, :]
        dh = jnp.concatenate([sums[g][0:tm, :] - ddiffs[g] for g in range(ng)], axis=-1)
        dg0_ref[...] += _colsum(dh * xh)
        dx_ref[...] = dm + _rms_bwd(xh, rx, g0, dh)

    row = pl.BlockSpec((tm, d), lambda i: (nt - 1 - i, 0))
    halo = pl.BlockSpec((POOL_HALO, d), lambda i: (jnp.maximum((nt - 1 - i) * (tm // POOL_HALO) - 1, 0), 0))
    vec = _full((1, d))
    return pl.pallas_call(
        body, grid=(nt,),
        in_specs=[row, row, halo, vec, vec, _full(pool_w.shape), vec],
        out_specs=[row, vec, vec, vec, _full((ng, POOL_DIM, POOL_DIM))],
        out_shape=[jax.ShapeDtypeStruct((t, d), F32)] + [jax.ShapeDtypeStruct((1, d), F32)] * 3
        + [jax.ShapeDtypeStruct((ng, POOL_DIM, POOL_DIM), F32)],
        scratch_shapes=[pltpu.VMEM((POOL_HALO, d), F32)],
        compiler_params=_params(1), name="pool_bwd")(dx1, x, x, g_pre, g_post, pool_w, pool_scale)


def _conv_taps(cw_ref, j):
    return cw_ref[j, 0:1, :], cw_ref[j, 1:2, :], cw_ref[j, 2:3, :]


def _row_block(m, target=256):
    if m <= target:
        return m
    for b in range(target, 7, -8):
        if m % b == 0:
            return b
    return m


def mlp_fwd(x, g_pre, g_post, w_up, w_down, conv_w, conv_b, target=None, tm=256):
    t, d = x.shape
    nt = t // tm
    h8 = CONV_HALO
    with_loss = target is not None
    n_extra = 1 if with_loss else 0

    def body(x_ref, g2_ref, g3_ref, wup_hbm, wdn_hbm, cw_ref, cb_ref, *rest):
        tgt_ref = rest[0] if with_loss else None
        xo_ref, u_ref, s_ref, a_ref, f_ref, h_ref = rest[n_extra:n_extra + 6]
        loss_ref = rest[n_extra + 6] if with_loss else None
        wup_v, wdn_v, tail, sem = rest[-4:]
        i = pl.program_id(0)

        @pl.when(i == 0)
        def _():
            c1 = pltpu.make_async_copy(wup_hbm, wup_v, sem.at[0])
            c2 = pltpu.make_async_copy(wdn_hbm, wdn_v, sem.at[1])
            c1.start()
            c2.start()
            tail[...] = jnp.zeros_like(tail)
            if with_loss:
                loss_ref[...] = jnp.zeros_like(loss_ref)
            c1.wait()
            c2.wait()

        xv = x_ref[...]
        h, _, _ = _rms_fwd(xv, g2_ref[...])
        hb = h.astype(BF16)
        h_ref[...] = hb
        acc = jnp.zeros((tm, d), F32)
        for k in range(2):
            cs = []
            for s in range(2):
                j, cols = k + 2 * s, slice((2 * k + s) * FF_CHUNK, (2 * k + s + 1) * FF_CHUNK)
                uf = _dot(hb, wup_v[j])
                u_ref[:, cols] = uf.astype(BF16)
                ext = jnp.concatenate([tail[j], uf], axis=0)
                tail[j] = uf[tm - h8:tm, :]
                w0, w1, w2 = _conv_taps(cw_ref, j)
                cs.append(cb_ref[j] + w2 * uf + w1 * pltpu.roll(ext, 1, axis=0)[h8:, :]
                          + w0 * pltpu.roll(ext, 2, axis=0)[h8:, :])
            cg, cv = cs
            sg = jax.nn.sigmoid(cg)
            sil = cg * sg
            ab = (sil * cv).astype(BF16)
            a_ref[:, k * FF_CHUNK:(k + 1) * FF_CHUNK] = ab
            s_ref[:, 2 * k * FF_CHUNK:(2 * k + 1) * FF_CHUNK] = sil.astype(BF16)
            s_ref[:, (2 * k + 1) * FF_CHUNK:(2 * k + 2) * FF_CHUNK] = (cv * (sg * (1.0 + cg * (1.0 - sg)))).astype(BF16)
            acc = acc + _dot(ab, wdn_v[k * FF_CHUNK:(k + 1) * FF_CHUNK, :])
        f_ref[...] = acc
        y, _, _ = _rms_fwd(acc, g3_ref[...])
        if with_loss:
            err = (xv + y) - tgt_ref[...]
            xo_ref[...] = err * (1.0 / d)
            loss_ref[...] += 0.5 * jnp.sum(jnp.mean(err * err, axis=-1, keepdims=True), axis=0, keepdims=True)
        else:
            xo_ref[...] = xv + y

    row = pl.BlockSpec((tm, d), lambda i: (i, 0))
    wide = pl.BlockSpec((tm, 2 * D_FF), lambda i: (i, 0))
    vec = _full((1, d))
    extra = [target] if with_loss else []
    return pl.pallas_call(
        body, grid=(nt,),
        in_specs=[row, vec, vec, ANY, ANY, _full(conv_w.shape), _full(conv_b.shape)] + [row] * n_extra,
        out_specs=[row, wide, wide, pl.BlockSpec((tm, D_FF), lambda i: (i, 0)), row, row] + [_full((1, 1))] * n_extra,
        out_shape=[jax.ShapeDtypeStruct((t, d), F32), jax.ShapeDtypeStruct((t, 2 * D_FF), BF16),
                   jax.ShapeDtypeStruct((t, 2 * D_FF), BF16), jax.ShapeDtypeStruct((t, D_FF), BF16),
                   jax.ShapeDtypeStruct((t, d), F32), jax.ShapeDtypeStruct((t, d), BF16)]
        + [jax.ShapeDtypeStruct((1, 1), F32)] * n_extra,
        scratch_shapes=[pltpu.VMEM(w_up.shape, BF16), pltpu.VMEM(w_down.shape, BF16),
                        pltpu.VMEM((N_SHARD, h8, FF_CHUNK), F32), pltpu.SemaphoreType.DMA((2,))],
        compiler_params=_params(1), name="mlp_fwd_loss" if with_loss else "mlp_fwd")(
            x, g_pre, g_post, w_up, w_down, conv_w, conv_b, *extra)


def _rowsum8(v):
    return jnp.sum(v.reshape(v.shape[0] // 8, 8, v.shape[1]), axis=0)


def mlp_bwd(dxo, f, x, u, sp, g_pre, g_post, w_up, w_down, conv_w, tm=256):
    t, d = x.shape
    nt = t // tm
    h8 = CONV_HALO

    def body(dxo_ref, f_ref, x_ref, u_ref, s_ref, g2_ref, g3_ref, wup_hbm, wdn_hbm, cw_ref,
             dx_ref, du_ref, df_ref, dg2_ref, dg3_ref, dcw_ref, dcb_ref,
             wup_v, wdn_v, dfb_s, dh_s, carry, sem):
        i = pl.program_id(0)
        k = pl.program_id(1)

        @pl.when((i == 0) & (k == 0))
        def _():
            c1 = pltpu.make_async_copy(wup_hbm, wup_v, sem.at[0])
            c2 = pltpu.make_async_copy(wdn_hbm, wdn_v, sem.at[1])
            c1.start()
            c2.start()
            carry[...] = jnp.zeros_like(carry)
            dg2_ref[...] = jnp.zeros_like(dg2_ref)
            dg3_ref[...] = jnp.zeros_like(dg3_ref)
            dcw_ref[...] = jnp.zeros_like(dcw_ref)
            dcb_ref[...] = jnp.zeros_like(dcb_ref)
            c1.wait()
            c2.wait()

        @pl.when(k == 0)
        def _():
            g3 = g3_ref[...]
            dxo = dxo_ref[...]
            _, fh, rf = _rms_fwd(f_ref[...], g3)
            dg3_ref[...] += _rowsum8(dxo * fh)
            dfb = _rms_bwd(fh, rf, g3, dxo).astype(BF16)
            df_ref[...] = dfb
            dfb_s[...] = dfb
            dh_s[...] = jnp.zeros_like(dh_s)

        da = _dot_nt(dfb_s[...], wdn_v[pl.ds(pl.multiple_of(k * FF_CHUNK, 128), FF_CHUNK), :])
        dh = dh_s[...]
        for s in range(2):
            j = k + 2 * s
            cols = slice(s * FF_CHUNK, (s + 1) * FF_CHUNK)
            dc = da * s_ref[:, (1 - s) * FF_CHUNK:(2 - s) * FF_CHUNK].astype(F32)
            uf = u_ref[:, cols].astype(F32)
            ext = jnp.concatenate([dc, carry[j]], axis=0)
            carry[j] = dc[0:h8, :]
            dc1 = pltpu.roll(ext, tm + h8 - 1, axis=0)[0:tm, :]
            dc2 = pltpu.roll(ext, tm + h8 - 2, axis=0)[0:tm, :]
            dcb_ref[j] += _rowsum8(dc)
            dcw_ref[j, 2] += _rowsum8(dc * uf)
            dcw_ref[j, 1] += _rowsum8(dc1 * uf)
            dcw_ref[j, 0] += _rowsum8(dc2 * uf)
            dub = (cw_ref[j, 2:3, :] * dc + cw_ref[j, 1:2, :] * dc1 + cw_ref[j, 0:1, :] * dc2).astype(BF16)
            du_ref[:, cols] = dub
            dh = dh + _dot_nt(dub, wup_v[j])
        dh_s[...] = dh

        @pl.when(k == 1)
        def _():
            g2 = g2_ref[...]
            _, xh, rx = _rms_fwd(x_ref[...], g2)
            dg2_ref[...] += _rowsum8(dh * xh)
            dx_ref[...] = dxo_ref[...] + _rms_bwd(xh, rx, g2, dh)

    row = pl.BlockSpec((tm, d), lambda i, k: (nt - 1 - i, 0))
    half = pl.BlockSpec((tm, 2 * FF_CHUNK), lambda i, k: (nt - 1 - i, k))
    vec = _full((1, d))
    acc = _full((8, d))
    dcw_shape, dcb_shape = (N_SHARD, 3, 8, FF_CHUNK), (N_SHARD, 8, FF_CHUNK)
    return pl.pallas_call(
        body, grid=(nt, 2),
        in_specs=[row, row, row, half, half, vec, vec, ANY, ANY, _full(conv_w.shape)],
        out_specs=[row, half, row, acc, acc, _full(dcw_shape), _full(dcb_shape)],
        out_shape=[jax.ShapeDtypeStruct((t, d), F32), jax.ShapeDtypeStruct((t, 2 * D_FF), BF16),
                   jax.ShapeDtypeStruct((t, d), BF16),
                   jax.ShapeDtypeStruct((8, d), F32), jax.ShapeDtypeStruct((8, d), F32),
                   jax.ShapeDtypeStruct(dcw_shape, F32), jax.ShapeDtypeStruct(dcb_shape, F32)],
        scratch_shapes=[pltpu.VMEM(w_up.shape, BF16), pltpu.VMEM(w_down.shape, BF16),
                        pltpu.VMEM((tm, d), BF16), pltpu.VMEM((tm, d), F32),
                        pltpu.VMEM((N_SHARD, h8, FF_CHUNK), F32), pltpu.SemaphoreType.DMA((2,))],
        compiler_params=_params(2), name="mlp_bwd")(dxo, f, x, u, sp, g_pre, g_post, w_up, w_down, conv_w)


def grad_matmul(a, b, bm, bn, name, tk=2048, interleaved=False, after=None):
    t, m = a.shape
    n = b.shape[1]
    tk = min(tk, t)
    nk = t // tk
    place = (lambda j: (j % 2) * 2 + j // 2) if interleaved else (lambda j: j)
    extra = [] if after is None else [after]

    def body(a_ref, b_ref, *rest):
        o_ref, ob_ref = rest[len(extra):]
        kk = pl.program_id(2)

        @pl.when(kk == 0)
        def _():
            o_ref[...] = jnp.zeros_like(o_ref)

        o_ref[...] += _dot_tn(a_ref[...], b_ref[...])

        @pl.when(kk == nk - 1)
        def _():
            ob_ref[...] = o_ref[...].astype(BF16)

    ospec = pl.BlockSpec((None, bm, bn), lambda j, i, kk: (place(j), i, 0))
    return pl.pallas_call(
        body, grid=(n // bn, m // bm, nk),
        in_specs=[pl.BlockSpec((tk, bm), lambda j, i, kk: (kk, i)), pl.BlockSpec((tk, bn), lambda j, i, kk: (kk, j))]
        + [ANY] * len(extra),
        out_specs=[ospec, ospec],
        out_shape=[jax.ShapeDtypeStruct((n // bn, m, bn), F32), jax.ShapeDtypeStruct((n // bn, m, bn), BF16)],
        compiler_params=_params(3), name=name)(a, b, *extra)


def _decay_tables():
    log_gamma = jnp.log(1.0 - 2.0 ** (-5.0 - jnp.arange(RET_HEADS, dtype=F32)))
    i = jnp.arange(RET_CHUNK, dtype=F32)
    rel = i[:, None] - i[None, :]
    intra = jnp.where(rel >= 0, jnp.exp(jnp.maximum(rel, 0.0) * log_gamma[:, None, None]), 0.0)
    cross = jnp.exp((i + 1.0) * log_gamma[:, None])[:, :, None]
    inner = jnp.exp((RET_CHUNK - 1.0 - i) * log_gamma[:, None])[:, :, None]
    chunk = [float(np.exp(np.float32(RET_CHUNK) * np.log(np.float32(1.0 - 2.0 ** (-5.0 - h))).astype(np.float32)))
             for h in range(RET_HEADS)]
    return intra, cross, inner, chunk


def ret_proj(x, g_pre, w_in, cos, sin, tm=512):
    t, d = x.shape
    nt = t // tm
    per = RET_IN_SHARD // RET_QK

    def body(x_ref, g_ref, win_hbm, c_ref, s_ref, pj_ref, h_ref, win_v, sem):
        @pl.when(pl.program_id(0) == 0)
        def _():
            cp = pltpu.make_async_copy(win_hbm, win_v, sem)
            cp.start()
            cp.wait()

        h, _, _ = _rms_fwd(x_ref[...], g_ref[...])
        hb = h.astype(BF16)
        h_ref[...] = hb
        c = c_ref[...]
        s = s_ref[...]
        for j in range(N_SHARD):
            pjj = _dot(hb, win_v[j])
            for bb in range(per):
                b = per * j + bb
                blk = pjj[:, bb * RET_QK:(bb + 1) * RET_QK]
                if b < 2 * RET_HEADS:
                    x1, x2 = blk[:, :128], blk[:, 128:]
                    o1 = x1 * c - x2 * s
                    o2 = x2 * c + x1 * s
                    if b < RET_HEADS:
                        o1 = o1 * (RET_QK ** -0.5)
                        o2 = o2 * (RET_QK ** -0.5)
                    pj_ref[:, b * RET_QK:b * RET_QK + 128] = o1.astype(BF16)
                    pj_ref[:, b * RET_QK + 128:(b + 1) * RET_QK] = o2.astype(BF16)
                else:
                    pj_ref[:, b * RET_QK:(b + 1) * RET_QK] = blk.astype(BF16)

    row = pl.BlockSpec((tm, d), lambda i: (i, 0))
    tab = pl.BlockSpec((tm, 128), lambda i: (i, 0))
    return pl.pallas_call(
        body, grid=(nt,),
        in_specs=[row, _full((1, d)), ANY, tab, tab],
        out_specs=[pl.BlockSpec((tm, RET_IN), lambda i: (i, 0)), row],
        out_shape=[jax.ShapeDtypeStruct((t, RET_IN), BF16), jax.ShapeDtypeStruct((t, d), BF16)],
        scratch_shapes=[pltpu.VMEM(w_in.shape, BF16), pltpu.SemaphoreType.DMA],
        compiler_params=_params(1), name="ret_proj")(x, g_pre, w_in, cos, sin)


def ret_core_fwd(pj, intra, cross, inner, chunk_decay):
    t = pj.shape[0]
    nc = t // RET_CHUNK
    c = RET_CHUNK
    qk_all = RET_HEADS * RET_QK
    v_all = RET_HEADS * RET_V

    def body(q_ref, k_ref, v_ref, dm_ref, cr_ref, in_ref, o_ref, sp_ref, state):
        @pl.when(pl.program_id(0) == 0)
        def _():
            state[...] = jnp.zeros_like(state)

        for h in range(RET_HEADS):
            q = q_ref[:, h * RET_QK:(h + 1) * RET_QK]
            k = k_ref[:, h * RET_QK:(h + 1) * RET_QK]
            v = v_ref[:, h * RET_V:(h + 1) * RET_V]
            sb = state[h].astype(BF16)
            sp_ref[h] = sb
            sc = _dot_nt(q, k) * dm_ref[h]
            o_ref[:, h * RET_V:(h + 1) * RET_V] = _dot(sc.astype(BF16), v) + _dot(q, sb) * cr_ref[h]
            kd = (k.astype(F32) * in_ref[h]).astype(BF16)
            state[h] = state[h] * chunk_decay[h] + _dot_tn(kd, v)

    return pl.pallas_call(
        body, grid=(nc,),
        in_specs=[pl.BlockSpec((c, qk_all), lambda n: (n, 0)), pl.BlockSpec((c, qk_all), lambda n: (n, 1)),
                  pl.BlockSpec((c, v_all), lambda n: (n, 1)),
                  _full(intra.shape), _full(cross.shape), _full(inner.shape)],
        out_specs=[pl.BlockSpec((c, v_all), lambda n: (n, 0)),
                   pl.BlockSpec((None, RET_HEADS, RET_QK, RET_V), lambda n: (n, 0, 0, 0))],
        out_shape=[jax.ShapeDtypeStruct((t, v_all), F32),
                   jax.ShapeDtypeStruct((nc, RET_HEADS, RET_QK, RET_V), BF16)],
        scratch_shapes=[pltpu.VMEM((RET_HEADS, RET_QK, RET_V), F32)],
        compiler_params=_params(1), name="ret_core_fwd")(pj, pj, pj, intra, cross, inner)


def _group_norm(o_h):
    mu = jnp.mean(o_h, axis=-1, keepdims=True)
    dev = o_h - mu
    rstd = lax.rsqrt(jnp.mean(dev * dev, axis=-1, keepdims=True) + EPS)
    return dev * rstd, rstd


def ret_out_fwd(o, pj, x, gn_gain, g_post, w_out, tm=512):
    t, d = x.shape
    nt = t // tm
    v_all = RET_HEADS * RET_V

    def body(o_ref, g_ref, x_ref, gn_ref, g1_ref, w_ref, xo_ref, y_ref, out_ref):
        for h in range(RET_HEADS):
            cols = slice(h * RET_V, (h + 1) * RET_V)
            ohat, _ = _group_norm(o_ref[:, cols])
            g = g_ref[:, cols].astype(F32)
            y_ref[:, cols] = (g * jax.nn.sigmoid(g) * (ohat * gn_ref[:, cols])).astype(BF16)
        out = _dot(y_ref[...], w_ref[...])
        out_ref[...] = out
        m, _, _ = _rms_fwd(out, g1_ref[...])
        xo_ref[...] = x_ref[...] + m

    row = pl.BlockSpec((tm, d), lambda i: (i, 0))
    wide = pl.BlockSpec((tm, v_all), lambda i: (i, 0))
    return pl.pallas_call(
        body, grid=(nt,),
        in_specs=[wide, pl.BlockSpec((tm, v_all), lambda i: (i, 2)), row, _full((1, v_all)), _full((1, d)),
                  _full(w_out.shape)],
        out_specs=[row, wide, row],
        out_shape=[jax.ShapeDtypeStruct((t, d), F32), jax.ShapeDtypeStruct((t, v_all), BF16),
                   jax.ShapeDtypeStruct((t, d), F32)],
        compiler_params=_params(1), name="ret_out_fwd")(o, pj, x, gn_gain, g_post, w_out)


def ret_out_bwd(dxo, out, o, pj, gn_gain, g_post, w_out, tm=256):
    t, d = out.shape
    nt = t // tm
    v_all = RET_HEADS * RET_V

    def body(dxo_ref, out_ref, o_ref, g_ref, gn_ref, g1_ref, w_ref,
             dout_ref, dgate_ref, do_ref, dg1_ref, dgn_ref):
        @pl.when(pl.program_id(0) == 0)
        def _():
            dg1_ref[...] = jnp.zeros_like(dg1_ref)
            dgn_ref[...] = jnp.zeros_like(dgn_ref)

        g1 = g1_ref[...]
        dxo = dxo_ref[...]
        _, oh_, r_ = _rms_fwd(out_ref[...], g1)
        dg1_ref[...] += _colsum(dxo * oh_)
        doutb = _rms_bwd(oh_, r_, g1, dxo).astype(BF16)
        dout_ref[...] = doutb
        dy = _dot_nt(doutb, w_ref[...])
        for h in range(RET_HEADS):
            cols = slice(h * RET_V, (h + 1) * RET_V)
            gn = gn_ref[:, cols]
            ohat, rstd = _group_norm(o_ref[:, cols])
            g = g_ref[:, cols].astype(F32)
            sg = jax.nn.sigmoid(g)
            dyh = dy[:, cols]
            dgate_ref[:, cols] = (dyh * (ohat * gn) * (sg * (1.0 + g * (1.0 - sg)))).astype(BF16)
            don = dyh * (g * sg)
            dgn_ref[:, cols] += _colsum(don * ohat)
            dohat = don * gn
            do_ref[:, cols] = (rstd * (dohat - jnp.mean(dohat, axis=-1, keepdims=True)
                                       - ohat * jnp.mean(dohat * ohat, axis=-1, keepdims=True))).astype(BF16)

    row = pl.BlockSpec((tm, d), lambda i: (i, 0))
    wide = pl.BlockSpec((tm, v_all), lambda i: (i, 0))
    gate = pl.BlockSpec((tm, v_all), lambda i: (i, 2))
    return pl.pallas_call(
        body, grid=(nt,),
        in_specs=[row, row, wide, gate, _full((1, v_all)), _full((1, d)), _full(w_out.shape)],
        out_specs=[row, gate, wide, _full((1, d)), _full((1, v_all))],
        out_shape=[jax.ShapeDtypeStruct((t, d), BF16), jax.ShapeDtypeStruct((t, RET_IN), BF16),
                   jax.ShapeDtypeStruct((t, v_all), BF16), jax.ShapeDtypeStruct((1, d), F32),
                   jax.ShapeDtypeStruct((1, v_all), F32)],
        compiler_params=_params(1), name="ret_out_bwd")(dxo, out, o, pj, gn_gain, g_post, w_out)


def ret_core_bwd(pj, do, sprev, cos, sin, dpj, intra, cross, inner, chunk_decay):
    t = pj.shape[0]
    nc = t // RET_CHUNK
    c = RET_CHUNK
    qk_all = RET_HEADS * RET_QK
    v_all = RET_HEADS * RET_V
    scale = RET_QK ** -0.5

    def body(q_ref, k_ref, v_ref, do_ref, sp_ref, c_ref, s_ref, dm_ref, cr_ref, in_ref, dpj_in, dpj_ref, dstate):
        @pl.when(pl.program_id(0) == 0)
        def _():
            dstate[...] = jnp.zeros_like(dstate)

        cs = c_ref[...]
        sn = s_ref[...]
        for h in range(RET_HEADS):
            q = q_ref[:, h * RET_QK:(h + 1) * RET_QK]
            k = k_ref[:, h * RET_QK:(h + 1) * RET_QK]
            v = v_ref[:, h * RET_V:(h + 1) * RET_V]
            doh = do_ref[:, h * RET_V:(h + 1) * RET_V]
            dm = dm_ref[h]
            ab = (_dot_nt(q, k) * dm).astype(BF16)
            dab = (_dot_nt(doh, v) * dm).astype(BF16)
            dsb = dstate[h].astype(BF16)
            kd = (k.astype(F32) * in_ref[h]).astype(BF16)
            dv = _dot_tn(ab, doh) + _dot(kd, dsb)
            dq = _dot(dab, k) + cr_ref[h] * _dot_nt(doh, sp_ref[h])
            dk = _dot_tn(dab, q) + in_ref[h] * _dot_nt(v, dsb)
            qd = (q.astype(F32) * cr_ref[h]).astype(BF16)
            dstate[h] = dstate[h] * chunk_decay[h] + _dot_tn(qd, doh)
            for base, dd, sc in ((h * RET_QK, dq, scale), (qk_all + h * RET_QK, dk, 1.0)):
                d1, d2 = dd[:, :128], dd[:, 128:]
                dpj_ref[:, base:base + 128] = ((d1 * cs + d2 * sn) * sc).astype(BF16)
                dpj_ref[:, base + 128:base + RET_QK] = ((d2 * cs - d1 * sn) * sc).astype(BF16)
            dpj_ref[:, 2 * qk_all + h * RET_V:2 * qk_all + (h + 1) * RET_V] = dv.astype(BF16)

    rev = lambda n: nc - 1 - n
    tab = pl.BlockSpec((c, 128), lambda n: (rev(n), 0))
    return pl.pallas_call(
        body, grid=(nc,),
        in_specs=[pl.BlockSpec((c, qk_all), lambda n: (rev(n), 0)), pl.BlockSpec((c, qk_all), lambda n: (rev(n), 1)),
                  pl.BlockSpec((c, v_all), lambda n: (rev(n), 1)), pl.BlockSpec((c, v_all), lambda n: (rev(n), 0)),
                  pl.BlockSpec((None, RET_HEADS, RET_QK, RET_V), lambda n: (rev(n), 0, 0, 0)),
                  tab, tab, _full(intra.shape), _full(cross.shape), _full(inner.shape), ANY],
        out_specs=pl.BlockSpec((c, 2 * qk_all + v_all), lambda n: (rev(n), 0)),
        out_shape=jax.ShapeDtypeStruct((t, RET_IN), BF16),
        scratch_shapes=[pltpu.VMEM((RET_HEADS, RET_QK, RET_V), F32)],
        input_output_aliases={10: 0},
        compiler_params=_params(1), name="ret_core_bwd")(pj, pj, pj, do, sprev, cos, sin, intra, cross, inner, dpj)


def ret_in_bwd(dpj, dres, x, g_pre, w_in, tm=512):
    t, d = x.shape
    nt = t // tm

    def body(dpj_ref, dres_ref, x_ref, g_ref, win_hbm, dx_ref, dg_ref, win_v, sem):
        @pl.when(pl.program_id(0) == 0)
        def _():
            cp = pltpu.make_async_copy(win_hbm, win_v, sem)
            cp.start()
            dg_ref[...] = jnp.zeros_like(dg_ref)
            cp.wait()

        g = g_ref[...]
        dh = jnp.zeros((tm, d), F32)
        for j in range(N_SHARD):
            dh = dh + _dot_nt(dpj_ref[:, j * RET_IN_SHARD:(j + 1) * RET_IN_SHARD], win_v[j])
        _, xh, rx = _rms_fwd(x_ref[...], g)
        dg_ref[...] += _colsum(dh * xh)
        dx_ref[...] = dres_ref[...] + _rms_bwd(xh, rx, g, dh)

    row = pl.BlockSpec((tm, d), lambda i: (i, 0))
    return pl.pallas_call(
        body, grid=(nt,),
        in_specs=[pl.BlockSpec((tm, RET_IN), lambda i: (i, 0)), row, row, _full((1, d)), ANY],
        out_specs=[row, _full((1, d))],
        out_shape=[jax.ShapeDtypeStruct((t, d), F32), jax.ShapeDtypeStruct((1, d), F32)],
        scratch_shapes=[pltpu.VMEM(w_in.shape, BF16), pltpu.SemaphoreType.DMA],
        compiler_params=_params(1), name="ret_in_bwd")(dpj, dres, x, g_pre, w_in)


_CHIP_FLIPS = ((1, 0), (0, 1), (1, 1))


def _flip(v, b):
    return 1 - v if b else v


def scatter_grads(big, small):
    n = len(big)

    def body(*refs):
        ins, small_in = refs[:n], refs[n]
        outs, small_out = refs[n + 1:2 * n + 1], refs[2 * n + 1]
        send_sems, recv_sems, ssend_sems, srecv_sems, local_sem = refs[2 * n + 2:]
        x, y, c = lax.axis_index("x"), lax.axis_index("y"), lax.axis_index("c")
        mine = 4 * x + 2 * y + c
        copies = [pltpu.make_async_copy(small_in, small_out.at[mine], local_sem)]
        copies[0].start()
        for m in range(1, 8):
            bx, by, bc = (m >> 2) & 1, (m >> 1) & 1, m & 1
            cp = pltpu.make_async_remote_copy(
                src_ref=small_in, dst_ref=small_out.at[mine], send_sem=ssend_sems.at[m - 1],
                recv_sem=srecv_sems.at[m - 1], device_id=(_flip(x, bx), _flip(y, by), _flip(c, bc)),
                device_id_type=MESH)
            cp.start()
            copies.append(cp)
        for t in range(n):
            for k, (bx, by) in enumerate(_CHIP_FLIPS):
                px, py = _flip(x, bx), _flip(y, by)
                cp = pltpu.make_async_remote_copy(
                    src_ref=ins[t].at[2 * px + py], dst_ref=outs[t].at[k], send_sem=send_sems.at[3 * t + k],
                    recv_sem=recv_sems.at[3 * t + k], device_id=(px, py, c), device_id_type=MESH)
                cp.start()
                copies.append(cp)
        for cp in copies:
            cp.wait()

    return pl.pallas_call(
        body, in_specs=[ANY] * (n + 1), out_specs=[ANY] * (n + 1),
        out_shape=[jax.ShapeDtypeStruct((3,) + b.shape[1:], b.dtype) for b in big]
        + [jax.ShapeDtypeStruct((8,) + small.shape, small.dtype)],
        scratch_shapes=[pltpu.SemaphoreType.DMA((3 * n,)), pltpu.SemaphoreType.DMA((3 * n,)),
                        pltpu.SemaphoreType.DMA((7,)), pltpu.SemaphoreType.DMA((7,)), pltpu.SemaphoreType.DMA],
        name="scatter_grads")(*big, small)


def swap_cores(arrays, name):
    n = len(arrays)

    def body(*refs):
        ins, outs = refs[:n], refs[n:2 * n]
        send_sems, recv_sems = refs[2 * n:]
        sibling = (lax.axis_index("x"), lax.axis_index("y"), 1 - lax.axis_index("c"))
        copies = []
        for t in range(n):
            cp = pltpu.make_async_remote_copy(
                src_ref=ins[t], dst_ref=outs[t], send_sem=send_sems.at[t], recv_sem=recv_sems.at[t],
                device_id=sibling, device_id_type=MESH)
            cp.start()
            copies.append(cp)
        for cp in copies:
            cp.wait()

    return pl.pallas_call(
        body, in_specs=[ANY] * n, out_specs=[ANY] * n,
        out_shape=[jax.ShapeDtypeStruct(a.shape, a.dtype) for a in arrays],
        scratch_shapes=[pltpu.SemaphoreType.DMA((n,)), pltpu.SemaphoreType.DMA((n,))],
        name=name)(*arrays)


_HBM = pl.BlockSpec(memory_space=pltpu.HBM)
_SEM = pl.BlockSpec(memory_space=pltpu.SEMAPHORE)
_EFFECT = pltpu.SideEffectType.DATAFLOW_SIDE_EFFECTING


def _chip_copies(mode, srcs, lands, send_sems, recv_sems):
    x, y, c = lax.axis_index("x"), lax.axis_index("y"), lax.axis_index("c")
    copies = []
    for t in range(len(lands)):
        for k, (bx, by) in enumerate(_CHIP_FLIPS):
            px, py = _flip(x, bx), _flip(y, by)
            target = (px, py, c)
            if mode == "gather":
                src, dst = srcs[t], lands[t].at[2 * x + y]
            elif mode == "gather_half":
                half = pl.ds(c * (srcs[t].shape[0] // 2), srcs[t].shape[0] // 2)
                src, dst = srcs[t].at[half], lands[t].at[2 * x + y, half]
            elif mode == "forward_half":
                half = pl.ds(c * (lands[t].shape[1] // 2), lands[t].shape[1] // 2)
                src = dst = lands[t].at[2 * px + py, half]
                target = (x, y, 1 - c)
            else:
                src, dst = srcs[t].at[2 * px + py], lands[t].at[k]
            copies.append(pltpu.make_async_remote_copy(
                src_ref=src, dst_ref=dst, send_sem=send_sems.at[3 * t + k], recv_sem=recv_sems.at[3 * t + k],
                device_id=target, device_id_type=MESH))
    return copies


def exchange_start(mode, srcs, lands, name, after=None):
    n, ns = len(lands), len(srcs)
    extra = [] if after is None else [after]

    def body(*refs):
        ins, lnd = refs[:ns], refs[ns:ns + n]
        send_sems, recv_sems = refs[ns + n + len(extra)], refs[ns + n + len(extra) + 1]
        token = refs[-1]
        for cp in _chip_copies(mode, ins, lnd, send_sems, recv_sems):
            cp.start()
        token[...] = jnp.zeros(token.shape, token.dtype)

    hbm = lambda a: pltpu.with_memory_space_constraint(a, pltpu.HBM)
    passed = list(srcs) + list(lands)
    return pl.pallas_call(
        body, name=name,
        out_shape=(pltpu.SemaphoreType.DMA((3 * n,)), pltpu.SemaphoreType.DMA((3 * n,)),
                   *[pltpu.HBM(a.shape, a.dtype) for a in passed], jax.ShapeDtypeStruct((8, 128), F32)),
        in_specs=[_HBM] * (ns + n) + [ANY] * len(extra),
        out_specs=(_SEM, _SEM, *[_HBM] * (ns + n), pl.BlockSpec(memory_space=pltpu.VMEM)),
        input_output_aliases={i: 2 + i for i in range(ns + n)},
        compiler_params=pltpu.CompilerParams(has_side_effects=_EFFECT))(*[hbm(a) for a in passed], *extra)


def exchange_wait(mode, started, after, name):
    send_sems, recv_sems = started[0], started[1]
    passed = list(started[2:-1])
    n = len(passed) if mode == "forward_half" else len(passed) // 2
    ns = len(passed) - n

    def body(*refs):
        ins, lnd = refs[:ns], refs[ns:ns + n]
        for cp in _chip_copies(mode, ins, lnd, refs[ns + n], refs[ns + n + 1]):
            cp.wait_send()
            cp.wait_recv()

    outs = pl.pallas_call(
        body, name=name, out_shape=tuple(pltpu.HBM(a.shape, a.dtype) for a in passed),
        in_specs=[_HBM] * (ns + n) + [_SEM, _SEM, ANY], out_specs=tuple([_HBM] * (ns + n)),
        input_output_aliases={i: i for i in range(ns + n)},
        compiler_params=pltpu.CompilerParams(has_side_effects=_EFFECT))(*passed, send_sems, recv_sems, after)
    return list(outs[ns:])


def plane_sum(slot, full, recv, name, bm=256):
    _, m, n = full.shape
    bm = _row_block(m, bm)

    def body(slot_ref, o_ref, r_ref, s_ref):
        s_ref[...] = ((o_ref[...] + r_ref[0].astype(F32)) + r_ref[1].astype(F32)) + r_ref[2].astype(F32)

    return pl.pallas_call(
        body,
        grid_spec=pltpu.PrefetchScalarGridSpec(
            num_scalar_prefetch=1, grid=(m // bm,),
            in_specs=[pl.BlockSpec((None, bm, n), lambda i, s: (s[0], i, 0)),
                      pl.BlockSpec((3, bm, n), lambda i, s: (0, i, 0))],
            out_specs=pl.BlockSpec((bm, n), lambda i, s: (i, 0))),
        out_shape=jax.ShapeDtypeStruct((m, n), F32), compiler_params=_params(1), name=name)(slot, full, recv)


def sum_slots(parts, name, bm=312):
    _, r, n = parts.shape
    bm = bm if r % bm == 0 else r

    def body(p_ref, s_ref):
        acc = p_ref[0]
        for k in range(1, 8):
            acc = acc + p_ref[k]
        s_ref[...] = acc

    return pl.pallas_call(
        body, grid=(r // bm,), in_specs=[pl.BlockSpec((8, bm, n), lambda i: (0, i, 0))],
        out_specs=pl.BlockSpec((bm, n), lambda i: (i, 0)), out_shape=jax.ShapeDtypeStruct((r, n), F32),
        compiler_params=_params(1), name=name)(parts)


def _adamw_math(w, g, m, v):
    m = ADAM_B1 * m + (1.0 - ADAM_B1) * g
    v = ADAM_B2 * v + (1.0 - ADAM_B2) * (g * g)
    m_hat = m / (1.0 - ADAM_B1 ** ADAM_STEP)
    v_hat = v / (1.0 - ADAM_B2 ** ADAM_STEP)
    delta = -ADAM_LR * (m_hat / (jnp.sqrt(v_hat) + ADAM_EPS) + ADAM_WD * w)
    return delta, m, v


def adamw(w, m, v, grads, layer, prev, name, bm=256):
    _, mm, n = w.shape
    bm = _row_block(mm, bm)
    ng = len(grads)

    def body(*refs):
        w_ref, m_ref, v_ref = refs[:3]
        g_refs = refs[3:3 + ng]
        g_out, d_out, m_out, v_out = refs[-4:]
        g = g_refs[0][...]
        for gr in g_refs[1:]:
            g = g + gr[...]
        delta, mn, vn = _adamw_math(w_ref[...], g, m_ref[...], v_ref[...])
        g_out[...] = g
        d_out[...] = delta
        m_out[...] = mn
        v_out[...] = vn

    slab = pl.BlockSpec((None, bm, n), lambda i: (layer, i, 0))
    flat = pl.BlockSpec((bm, n), lambda i: (i, 0))
    in_specs = [slab] * 3 + [flat] * ng
    args = [w, m, v, *grads]
    aliases = {}
    if prev is not None:
        in_specs += [ANY] * 4
        aliases = {3 + ng + q: q for q in range(4)}
        args += list(prev)
    return pl.pallas_call(
        body, grid=(mm // bm,), in_specs=in_specs, out_specs=[slab] * 4,
        out_shape=[jax.ShapeDtypeStruct(w.shape, F32)] * 4, input_output_aliases=aliases,
        compiler_params=_params(1), name=name)(*args)


def _pack_rows(parts, rows):
    flat = jnp.concatenate([p.reshape(-1) for p in parts])
    return jnp.pad(flat, (0, rows * 128 - flat.shape[0])).reshape(rows, 128)


def _as_shards(a, rows):
    return a.reshape(N_SHARD, rows, a.shape[-1])


def _local_step(x, pos_col, target, gains, pool_w, pool_scale, gn_gain, conv_w, conv_b, weights, send_grads):
    def gain(l, n, token=None):
        g = gains[l, n].reshape(1, D_MODEL)
        return g if token is None else g + token[0:1, 0:1]

    inv_freq = (ROPE_BASE ** (-jnp.arange(0, RET_QK, 2, dtype=F32) / RET_QK)).reshape(1, RET_QK // 2)
    cos, sin = rope_tables(pos_col, inv_freq)
    intra, cross, inner, chunk_decay = _decay_tables()
    dn_rows = D_FF // N_SHARD

    x1 = pool_fwd(x, gain(0, 0), gain(0, 1), pool_w, pool_scale)
    w_up0, w_dn0 = weights("mlp0", x1)
    w_dn0 = w_dn0.reshape(D_FF, D_MODEL)
    x2, u0, s0, a0, f0, h0 = mlp_fwd(x1, gain(0, 2), gain(0, 3), w_up0, w_dn0, conv_w[0], conv_b[0])
    w_in, w_out = weights("ret", x2)
    w_out = w_out.reshape(RET_HEADS * RET_V, D_MODEL)
    pj, hr = ret_proj(x2, gain(1, 0), w_in, cos, sin)
    o, sprev = ret_core_fwd(pj, intra, cross, inner, chunk_decay)
    x3, yb, out = ret_out_fwd(o, pj, x2, gn_gain, gain(1, 1), w_out)
    w_up1, w_dn1 = weights("mlp1", x3)
    w_dn1 = w_dn1.reshape(D_FF, D_MODEL)
    dx4, u1, s1, a1, f1, h1, loss = mlp_fwd(x3, gain(1, 2), gain(1, 3), w_up1, w_dn1, conv_w[1], conv_b[1], target)

    dx3, du1, df1, dg12, dg13, dcw1, dcb1 = mlp_bwd(
        dx4, f1, x3, u1, s1, gain(1, 2), gain(1, 3), w_up1, w_dn1, conv_w[1])
    dwup1 = grad_matmul(h1, du1, D_MODEL, FF_CHUNK, "grad_w_up_1", interleaved=True)
    dwdn1 = grad_matmul(a1, df1, FF_CHUNK, D_MODEL, "grad_w_down_1")
    tok = send_grads("mlp1", [dwup1, [_as_shards(g, dn_rows) for g in dwdn1]])
    dout, dpj, do, dg11, dgn = ret_out_bwd(dx3, out, o, pj, gn_gain, gain(1, 1, tok), w_out)
    dwout = grad_matmul(yb, dout, 1024, D_MODEL, "grad_w_out")
    dpj = ret_core_bwd(pj, do, sprev, cos, sin, dpj, intra, cross, inner, chunk_decay)
    dwin = grad_matmul(hr, dpj, D_MODEL, RET_IN_SHARD, "grad_w_in")
    tok = send_grads("ret", [dwin, [_as_shards(g, RET_V) for g in dwout]])
    dx2, dg10 = ret_in_bwd(dpj, dx3, x2, gain(1, 0, tok), w_in)
    dx1, du0, df0, dg02, dg03, dcw0, dcb0 = mlp_bwd(
        dx2, f0, x1, u0, s0, gain(0, 2), gain(0, 3), w_up0, w_dn0, conv_w[0])
    dwdn0 = grad_matmul(a0, df0, FF_CHUNK, D_MODEL, "grad_w_down_0")
    tok = send_grads("down0", [[_as_shards(g, dn_rows) for g in dwdn0]])
    dwup0 = grad_matmul(h0, du0, D_MODEL, FF_CHUNK, "grad_w_up_0", interleaved=True, after=tok)
    tok = send_grads("up0", [dwup0])
    dx0, dg00, dg01, dpscale, dpw = pool_bwd(dx1, x, gain(0, 0, tok), gain(0, 1), pool_w, pool_scale)

    rows = lambda g: jnp.sum(g, axis=0, keepdims=True)
    dgains = jnp.concatenate([dg00, dg01, rows(dg02), rows(dg03), dg10, dg11, rows(dg12), rows(dg13)],
                             axis=0).reshape(2, 4, D_MODEL)
    small = {"gains": dgains, "pool_scale": dpscale, "gn": dgn,
             "conv_w": jnp.sum(jnp.stack([dcw0, dcw1]), axis=3),
             "conv_b": jnp.sum(jnp.stack([dcb0, dcb1]), axis=2, keepdims=True), "pool_w": dpw}
    return loss, dx0, small


def kernel(x, positions, norm_gain, pool_w, pool_scale, ret_w_in, ret_gn_gain, ret_w_out, mlp_w_up, mlp_conv_w, mlp_conv_b, mlp_w_down, loss_target, m_norm_gain, m_pool_w, m_pool_scale, m_ret_w_in, m_ret_gn_gain, m_ret_w_out, m_mlp_w_up, m_mlp_conv_w, m_mlp_conv_b, m_mlp_w_down, v_norm_gain, v_pool_w, v_pool_scale, v_ret_w_in, v_ret_gn_gain, v_ret_w_out, v_mlp_w_up, v_mlp_conv_w, v_mlp_conv_b, v_mlp_w_down):
    t = x.shape[1]
    me = 2 * lax.axis_index("x") + lax.axis_index("y")
    me_slot = jnp.reshape(me, (1,)).astype(jnp.int32)

    small_parts = [norm_gain, ret_gn_gain, mlp_conv_w, pool_w]
    small_sizes = [p.size for p in small_parts]
    small_rows = -(-sum(small_sizes) // (128 * 8)) * 8
    groups = {"small": [_pack_rows(small_parts, small_rows)],
              "mlp0": [mlp_w_up[0].astype(BF16), mlp_w_down[0].astype(BF16)],
              "ret": [ret_w_in[0].astype(BF16), ret_w_out[0].astype(BF16)],
              "mlp1": [mlp_w_up[1].astype(BF16), mlp_w_down[1].astype(BF16)]}
    gathers, token = {}, None
    for group, srcs in groups.items():
        lands = [lax.dynamic_update_index_in_dim(lax.empty((N_SHARD,) + s.shape, s.dtype), s, me, 0) for s in srcs]
        mode = "gather_half" if group == "mlp0" else "gather"
        gathers[group] = (mode, exchange_start(mode, srcs, lands, "gather_start_" + group, after=token))
        token = gathers[group][1][-1]

    def weights(group, after):
        mode, started = gathers[group]
        lands = exchange_wait(mode, started, after, "gather_wait_" + group)
        if mode == "gather_half":
            forward = exchange_start("forward_half", [], lands, "forward_start_" + group)
            lands = exchange_wait("forward_half", forward, forward[-1], "forward_wait_" + group)
        return lands

    sent = {}

    def send_grads(group, pairs):
        lands = [lax.empty((3,) + b.shape[1:], BF16) for _, b in pairs]
        sent[group] = (exchange_start("scatter", [b for _, b in pairs], lands, "scatter_start_" + group),
                       [f for f, _ in pairs])
        return sent[group][0][-1]

    def reduced(group, after, names):
        started, own = sent[group]
        recv = exchange_wait("scatter", started, after, "scatter_wait_" + group)
        return [plane_sum(me_slot, f, r, "plane_sum_" + nm)
                for f, r, nm in zip(own, recv, names)]

    (smallg,) = weights("small", token)
    smallg = smallg.reshape(N_SHARD, -1)
    offs = np.cumsum([0] + small_sizes)
    piece = lambda i, shape: smallg[:, offs[i]:offs[i + 1]].reshape((N_SHARD,) + shape)
    gains = piece(0, (2, 4, 256)).transpose(1, 2, 0, 3).reshape(2, 4, D_MODEL)
    gn_full = piece(1, (512,)).reshape(1, RET_HEADS * RET_V)
    cw_full = piece(2, (2, 3, FF_CHUNK)).transpose(1, 0, 2, 3)
    pw_full = piece(3, (4, 64, 256)).transpose(1, 0, 2, 3).reshape(4, 256, 256).astype(BF16)
    cb_full = mlp_conv_b.reshape(2, N_SHARD, 1, FF_CHUNK)

    loss, dx0, small = _local_step(
        x[0], positions.reshape(t, 1).astype(F32), loss_target[0], gains, pw_full, pool_scale, gn_full,
        cw_full, cb_full, weights, send_grads)

    def small_adamw(w, m, v, grads, name):
        w3 = w.reshape(1, -1, w.shape[-1])
        out = adamw(w3, m.reshape(w3.shape), v.reshape(w3.shape), [g.reshape(w3.shape[1:]) for g in grads], 0, None, name)
        return [o.reshape(w.shape) for o in out]

    res = {}
    planes_a = reduced("mlp1", dx0, ["w_up_1", "w_down_1"]) + reduced("ret", dx0, ["w_in", "w_out"])
    others_a = swap_cores(planes_a, "swap_cores_a")
    res["ret_w_in"] = adamw(ret_w_in, m_ret_w_in, v_ret_w_in, (planes_a[2], others_a[2]), 0, None, "adamw_w_in")
    res["ret_w_out"] = adamw(ret_w_out, m_ret_w_out, v_ret_w_out, (planes_a[3], others_a[3]), 0, None, "adamw_w_out")
    up1 = adamw(mlp_w_up, m_mlp_w_up, v_mlp_w_up, (planes_a[0], others_a[0]), 1, None, "adamw_w_up_1")
    dn1 = adamw(mlp_w_down, m_mlp_w_down, v_mlp_w_down, (planes_a[1], others_a[1]), 1, None, "adamw_w_down_1")

    pw_f = small["pool_w"].reshape(4, N_SHARD, 64, 256).transpose(1, 0, 2, 3).reshape(N_SHARD, 256, 256)
    small_order = ["gains", "pool_scale", "gn", "conv_w", "conv_b"]
    gsmall_sizes = [small[k].size for k in small_order]
    gsmall_rows = -(-sum(gsmall_sizes) // (128 * 8)) * 8
    planes_b = reduced("up0", dn1[0], ["w_up_0"]) + reduced("down0", dn1[0], ["w_down_0"])
    pw_recv, small_recv = scatter_grads([pw_f.astype(BF16)], _pack_rows([small[k] for k in small_order], gsmall_rows))
    planes_b.append(plane_sum(me_slot, pw_f, pw_recv, "plane_sum_pool_w"))
    others_b = swap_cores(planes_b, "swap_cores_b")
    res["mlp_w_up"] = adamw(mlp_w_up, m_mlp_w_up, v_mlp_w_up, (planes_b[0], others_b[0]), 0, up1, "adamw_w_up_0")
    res["mlp_w_down"] = adamw(mlp_w_down, m_mlp_w_down, v_mlp_w_down, (planes_b[1], others_b[1]), 0, dn1,
                              "adamw_w_down_0")
    res["pool_w"] = small_adamw(pool_w, m_pool_w, v_pool_w, (planes_b[2], others_b[2]), "adamw_pool_w")

    gsmall = sum_slots(small_recv, "sum_small").reshape(-1)
    goffs = np.cumsum([0] + gsmall_sizes)
    gpiece = lambda i: gsmall[goffs[i]:goffs[i + 1]].reshape(small[small_order[i]].shape)
    g_gains = lax.dynamic_slice_in_dim(gpiece(0), me * 256, 256, axis=2)
    g_gn = lax.dynamic_slice_in_dim(gpiece(2), me * RET_V, RET_V, axis=1)
    g_cw = lax.dynamic_index_in_dim(gpiece(3), me, 1, keepdims=False)
    res["norm_gain"] = small_adamw(norm_gain, m_norm_gain, v_norm_gain, [g_gains], "adamw_norm_gain")
    res["pool_scale"] = small_adamw(pool_scale, m_pool_scale, v_pool_scale, [gpiece(1)], "adamw_pool_scale")
    res["ret_gn_gain"] = small_adamw(ret_gn_gain, m_ret_gn_gain, v_ret_gn_gain, [g_gn], "adamw_gn_gain")
    res["mlp_conv_w"] = small_adamw(mlp_conv_w, m_mlp_conv_w, v_mlp_conv_w, [g_cw], "adamw_conv_w")
    res["mlp_conv_b"] = small_adamw(mlp_conv_b, m_mlp_conv_b, v_mlp_conv_b, [gpiece(4)], "adamw_conv_b")

    order = ["norm_gain", "pool_w", "pool_scale", "ret_w_in", "ret_gn_gain", "ret_w_out", "mlp_w_up", "mlp_conv_w",
             "mlp_conv_b", "mlp_w_down"]
    total_loss = lax.psum(loss[0, 0], ("x", "y", "c"))
    outs = [total_loss, dx0.reshape(x.shape)]
    for q in range(4):
        outs += [res[k][q] for k in order]
    return tuple(outs)
```

```python
import numpy as np
import jax
import jax.numpy as jnp
from jax import lax
from jax.experimental import pallas as pl
from jax.experimental.pallas import tpu as pltpu

F32 = jnp.float32
BF16 = jnp.bfloat16

D_MODEL = 1024
D_FF = 2816
FF_CHUNK = 1408
N_SHARD = 4
POOL_WINDOWS = (2, 4, 8, 16)
POOL_DIM = 256
POOL_HALO = 16
RET_HEADS = 4
RET_QK = 256
RET_V = 512
RET_CHUNK = 256
RET_IN = 6144
RET_IN_SHARD = 1536
ROPE_BASE = 10000.0
EPS = 1e-6
CONV_HALO = 8

ADAM_LR, ADAM_B1, ADAM_B2, ADAM_EPS, ADAM_WD, ADAM_STEP = 0.001, 0.9, 0.999, 1e-08, 0.01, 10

VMEM_LIMIT = 56 * 1024 * 1024
MESH = pl.DeviceIdType.MESH
ANY = pl.BlockSpec(memory_space=pl.ANY)


def _params(n_grid=1, limit=VMEM_LIMIT):
    return pltpu.CompilerParams(dimension_semantics=("arbitrary",) * n_grid, vmem_limit_bytes=limit)


def _dot(a, b):
    return jnp.dot(a, b, preferred_element_type=F32)


def _dot_nt(a, b):
    return lax.dot_general(a, b, (((1,), (1,)), ((), ())), preferred_element_type=F32)


def _dot_tn(a, b):
    return lax.dot_general(a, b, (((0,), (0,)), ((), ())), preferred_element_type=F32)


def _rms_fwd(x, gain):
    r = lax.rsqrt(jnp.mean(x * x, axis=-1, keepdims=True) + EPS)
    xh = x * r
    return xh * gain, xh, r


def _rms_bwd(xh, r, gain, dy):
    dxh = dy * gain
    return r * (dxh - xh * jnp.mean(dxh * xh, axis=-1, keepdims=True))


def _colsum(v):
    return jnp.sum(v, axis=0, keepdims=True)


def _full(shape):
    nd = len(shape)
    return pl.BlockSpec(shape, lambda *_: (0,) * nd)


def rope_tables(pos_col, inv_freq, after):
    t = pos_col.shape[0]
    tm = min(t, 1024)

    def body(p_ref, f_ref, after_ref, c_ref, s_ref):
        ang = p_ref[...] * f_ref[...]
        c_ref[...] = jnp.cos(ang)
        s_ref[...] = jnp.sin(ang)

    return pl.pallas_call(
        body, grid=(t // tm,),
        in_specs=[pl.BlockSpec((tm, 1), lambda i: (i, 0)), _full((1, 128)), ANY],
        out_specs=[pl.BlockSpec((tm, 128), lambda i: (i, 0))] * 2,
        out_shape=[jax.ShapeDtypeStruct((t, 128), F32)] * 2,
        compiler_params=_params(1), name="rope_tables")(pos_col, inv_freq, after)


def _window_sums(ext, backward):
    n = ext.shape[0]
    cur, sums = ext, []
    for g, win in enumerate(POOL_WINDOWS):
        if g > 0:
            cur = cur[:, POOL_DIM:]
        half = win // 2
        cur = cur + pltpu.roll(cur, n - half if backward else half, axis=0)
        sums.append(cur[:, 0:POOL_DIM])
    return sums


def _pool_diff(h_halo, h, row0, tm):
    t_idx = row0 + lax.broadcasted_iota(jnp.int32, (tm, 1), 0)
    sums = _window_sums(jnp.concatenate([h_halo, h], axis=0), backward=False)
    parts, inv_counts = [], []
    for g, win in enumerate(POOL_WINDOWS):
        inv = 1.0 / jnp.minimum(t_idx + 1, win).astype(F32)
        parts.append(sums[g][POOL_HALO:, :] * inv - h[:, g * POOL_DIM:(g + 1) * POOL_DIM])
        inv_counts.append(inv)
    return parts, inv_counts


def pool_fwd(x, g_pre, g_post, pool_w, pool_scale, tm=512):
    t, d = x.shape
    nt = t // tm

    def body(x_ref, g0_ref, g1_ref, w_ref, sc_ref, o_ref, hext):
        i = pl.program_id(0)

        @pl.when(i == 0)
        def _():
            hext[...] = jnp.zeros((POOL_HALO, d), F32)

        xv = x_ref[...]
        h, _, _ = _rms_fwd(xv, g0_ref[...])
        parts, _ = _pool_diff(hext[...], h, i * tm, tm)
        hext[...] = h[tm - POOL_HALO:tm, :]
        ys = [_dot(parts[g].astype(BF16), w_ref[g]) for g in range(len(POOL_WINDOWS))]
        y = jnp.concatenate(ys, axis=-1) * sc_ref[...]
        m, _, _ = _rms_fwd(y, g1_ref[...])
        o_ref[...] = xv + m

    row = pl.BlockSpec((tm, d), lambda i: (i, 0))
    return pl.pallas_call(
        body, grid=(nt,),
        in_specs=[row, _full((1, d)), _full((1, d)), _full(pool_w.shape), _full((1, d))],
        out_specs=row, out_shape=jax.ShapeDtypeStruct((t, d), F32),
        scratch_shapes=[pltpu.VMEM((POOL_HALO, d), F32)],
        compiler_params=_params(1), name="pool_fwd")(x, g_pre, g_post, pool_w, pool_scale)


def pool_bwd(dx1, x, g_pre, g_post, pool_w, pool_scale, tm=512):
    t, d = x.shape
    nt = t // tm
    ng = len(POOL_WINDOWS)

    def body(dx1_ref, x_ref, xh_ref, g0_ref, g1_ref, w_ref, sc_ref,
             dx_ref, dg0_ref, dg1_ref, dsc_ref, dw_ref, enext):
        i = pl.program_id(0)
        r = nt - 1 - i

        @pl.when(i == 0)
        def _():
            enext[...] = jnp.zeros((POOL_HALO, d), F32)
            dg0_ref[...] = jnp.zeros_like(dg0_ref)
            dg1_ref[...] = jnp.zeros_like(dg1_ref)
            dsc_ref[...] = jnp.zeros_like(dsc_ref)
            dw_ref[...] = jnp.zeros_like(dw_ref)

        g0 = g0_ref[...]
        g1 = g1_ref[...]
        sc = sc_ref[...]
        xv = x_ref[...]
        h, xh, rx = _rms_fwd(xv, g0)
        h_halo, _, _ = _rms_fwd(xh_ref[...], g0)
        parts, inv_counts = _pool_diff(h_halo * jnp.where(r > 0, 1.0, 0.0), h, r * tm, tm)
        parts_b = [p.astype(BF16) for p in parts]
        ypre = jnp.concatenate([_dot(parts_b[g], w_ref[g]) for g in range(ng)], axis=-1)
        _, yh, ry = _rms_fwd(ypre * sc, g1)
        dm = dx1_ref[...]
        dg1_ref[...] += _colsum(dm * yh)
        dy = _rms_bwd(yh, ry, g1, dm)
        dsc_ref[...] += _colsum(dy * ypre)
        dyp = (dy * sc).astype(BF16)
        ddiffs = []
        for g in range(ng):
            cols = slice(g * POOL_DIM, (g + 1) * POOL_DIM)
            dw_ref[g] += _dot_tn(parts_b[g], dyp[:, cols])
            ddiffs.append(_dot_nt(dyp[:, cols], w_ref[g]))
        e = jnp.concatenate([ddiffs[g] * inv_counts[g] for g in range(ng)], axis=-1)
        sums = _window_sums(jnp.concatenate([e, enext[...]], axis=0), backward=True)
        enext[...] = e[0:POOL_HALO, :]
        dh = jnp.concatenate([sums[g][0:tm, :] - ddiffs[g] for g in range(ng)], axis=-1)
        dg0_ref[...] += _colsum(dh * xh)
        dx_ref[...] = dm + _rms_bwd(xh, rx, g0, dh)

    row = pl.BlockSpec((tm, d), lambda i: (nt - 1 - i, 0))
    halo = pl.BlockSpec((POOL_HALO, d), lambda i: (jnp.maximum((nt - 1 - i) * (tm // POOL_HALO) - 1, 0), 0))
    vec = _full((1, d))
    return pl.pallas_call(
        body, grid=(nt,),
        in_specs=[row, row, halo, vec, vec, _full(pool_w.shape), vec],
        out_specs=[row, vec, vec, vec, _full((ng, POOL_DIM, POOL_DIM))],
        out_shape=[jax.ShapeDtypeStruct((t, d), F32)] + [jax.ShapeDtypeStruct((1, d), F32)] * 3
        + [jax.ShapeDtypeStruct((ng, POOL_DIM, POOL_DIM), F32)],
        scratch_shapes=[pltpu.VMEM((POOL_HALO, d), F32)],
        compiler_params=_params(1), name="pool_bwd")(dx1, x, x, g_pre, g_post, pool_w, pool_scale)


def _conv_taps(cw_ref, j):
    return cw_ref[j, 0:1, :], cw_ref[j, 1:2, :], cw_ref[j, 2:3, :]


def _row_block(m, target=256):
    if m <= target:
        return m
    for b in range(target, 7, -8):
        if m % b == 0:
            return b
    return m


def mlp_fwd(x, g_pre, g_post, w_up, w_down, conv_w, conv_b, target=None, tm=256):
    t, d = x.shape
    nt = t // tm
    h8 = CONV_HALO
    with_loss = target is not None
    n_extra = 1 if with_loss else 0

    def body(x_ref, g2_ref, g3_ref, wup_hbm, wdn_hbm, cw_ref, cb_ref, *rest):
        tgt_ref = rest[0] if with_loss else None
        xo_ref, u_ref, s_ref, a_ref, f_ref, h_ref = rest[n_extra:n_extra + 6]
        loss_ref = rest[n_extra + 6] if with_loss else None
        wup_v, wdn_v, tail, sem = rest[-4:]
        i = pl.program_id(0)

        @pl.when(i == 0)
        def _():
            c1 = pltpu.make_async_copy(wup_hbm, wup_v, sem.at[0])
            c2 = pltpu.make_async_copy(wdn_hbm, wdn_v, sem.at[1])
            c1.start()
            c2.start()
            tail[...] = jnp.zeros_like(tail)
            if with_loss:
                loss_ref[...] = jnp.zeros_like(loss_ref)
            c1.wait()
            c2.wait()

        xv = x_ref[...]
        h, _, _ = _rms_fwd(xv, g2_ref[...])
        hb = h.astype(BF16)
        h_ref[...] = hb
        acc = jnp.zeros((tm, d), F32)
        for k in range(2):
            cs = []
            for s in range(2):
                j, cols = k + 2 * s, slice((2 * k + s) * FF_CHUNK, (2 * k + s + 1) * FF_CHUNK)
                uf = _dot(hb, wup_v[j])
                u_ref[:, cols] = uf.astype(BF16)
                ext = jnp.concatenate([tail[j], uf], axis=0)
                tail[j] = uf[tm - h8:tm, :]
                w0, w1, w2 = _conv_taps(cw_ref, j)
                cs.append(cb_ref[j] + w2 * uf + w1 * pltpu.roll(ext, 1, axis=0)[h8:, :]
                          + w0 * pltpu.roll(ext, 2, axis=0)[h8:, :])
            cg, cv = cs
            sg = jax.nn.sigmoid(cg)
            sil = cg * sg
            ab = (sil * cv).astype(BF16)
            a_ref[:, k * FF_CHUNK:(k + 1) * FF_CHUNK] = ab
            s_ref[:, 2 * k * FF_CHUNK:(2 * k + 1) * FF_CHUNK] = sil.astype(BF16)
            s_ref[:, (2 * k + 1) * FF_CHUNK:(2 * k + 2) * FF_CHUNK] = (cv * (sg * (1.0 + cg * (1.0 - sg)))).astype(BF16)
            acc = acc + _dot(ab, wdn_v[k * FF_CHUNK:(k + 1) * FF_CHUNK, :])
        f_ref[...] = acc
        y, _, _ = _rms_fwd(acc, g3_ref[...])
        if with_loss:
            err = (xv + y) - tgt_ref[...]
            xo_ref[...] = err * (1.0 / d)
            loss_ref[...] += 0.5 * jnp.sum(jnp.mean(err * err, axis=-1, keepdims=True), axis=0, keepdims=True)
        else:
            xo_ref[...] = xv + y

    row = pl.BlockSpec((tm, d), lambda i: (i, 0))
    wide = pl.BlockSpec((tm, 2 * D_FF), lambda i: (i, 0))
    vec = _full((1, d))
    extra = [target] if with_loss else []
    return pl.pallas_call(
        body, grid=(nt,),
        in_specs=[row, vec, vec, ANY, ANY, _full(conv_w.shape), _full(conv_b.shape)] + [row] * n_extra,
        out_specs=[row, wide, wide, pl.BlockSpec((tm, D_FF), lambda i: (i, 0)), row, row] + [_full((1, 1))] * n_extra,
        out_shape=[jax.ShapeDtypeStruct((t, d), F32), jax.ShapeDtypeStruct((t, 2 * D_FF), BF16),
                   jax.ShapeDtypeStruct((t, 2 * D_FF), BF16), jax.ShapeDtypeStruct((t, D_FF), BF16),
                   jax.ShapeDtypeStruct((t, d), F32), jax.ShapeDtypeStruct((t, d), BF16)]
        + [jax.ShapeDtypeStruct((1, 1), F32)] * n_extra,
        scratch_shapes=[pltpu.VMEM(w_up.shape, BF16), pltpu.VMEM(w_down.shape, BF16),
                        pltpu.VMEM((N_SHARD, h8, FF_CHUNK), F32), pltpu.SemaphoreType.DMA((2,))],
        compiler_params=_params(1), name="mlp_fwd_loss" if with_loss else "mlp_fwd")(
            x, g_pre, g_post, w_up, w_down, conv_w, conv_b, *extra)


def _rowsum8(v):
    return jnp.sum(v.reshape(v.shape[0] // 8, 8, v.shape[1]), axis=0)


def mlp_bwd(dxo, f, x, u, sp, g_pre, g_post, w_up, w_down, conv_w, tm=256):
    t, d = x.shape
    nt = t // tm
    h8 = CONV_HALO

    def body(dxo_ref, f_ref, x_ref, u_ref, s_ref, g2_ref, g3_ref, wup_hbm, wdn_hbm, cw_ref,
             dx_ref, du_ref, df_ref, dg2_ref, dg3_ref, dcw_ref, dcb_ref,
             wup_v, wdn_v, dfb_s, dh_s, carry, sem):
        i = pl.program_id(0)
        k = pl.program_id(1)

        @pl.when((i == 0) & (k == 0))
        def _():
            c1 = pltpu.make_async_copy(wup_hbm, wup_v, sem.at[0])
            c2 = pltpu.make_async_copy(wdn_hbm, wdn_v, sem.at[1])
            c1.start()
            c2.start()
            carry[...] = jnp.zeros_like(carry)
            dg2_ref[...] = jnp.zeros_like(dg2_ref)
            dg3_ref[...] = jnp.zeros_like(dg3_ref)
            dcw_ref[...] = jnp.zeros_like(dcw_ref)
            dcb_ref[...] = jnp.zeros_like(dcb_ref)
            c1.wait()
            c2.wait()

        @pl.when(k == 0)
        def _():
            g3 = g3_ref[...]
            dxo = dxo_ref[...]
            _, fh, rf = _rms_fwd(f_ref[...], g3)
            dg3_ref[...] += _rowsum8(dxo * fh)
            dfb = _rms_bwd(fh, rf, g3, dxo).astype(BF16)
            df_ref[...] = dfb
            dfb_s[...] = dfb
            dh_s[...] = jnp.zeros_like(dh_s)

        da = _dot_nt(dfb_s[...], wdn_v[pl.ds(pl.multiple_of(k * FF_CHUNK, 128), FF_CHUNK), :])
        dh = dh_s[...]
        for s in range(2):
            j = k + 2 * s
            cols = slice(s * FF_CHUNK, (s + 1) * FF_CHUNK)
            dc = da * s_ref[:, (1 - s) * FF_CHUNK:(2 - s) * FF_CHUNK].astype(F32)
            uf = u_ref[:, cols].astype(F32)
            ext = jnp.concatenate([dc, carry[j]], axis=0)
            carry[j] = dc[0:h8, :]
            dc1 = pltpu.roll(ext, tm + h8 - 1, axis=0)[0:tm, :]
            dc2 = pltpu.roll(ext, tm + h8 - 2, axis=0)[0:tm, :]
            dcb_ref[j] += _rowsum8(dc)
            dcw_ref[j, 2] += _rowsum8(dc * uf)
            dcw_ref[j, 1] += _rowsum8(dc1 * uf)
            dcw_ref[j, 0] += _rowsum8(dc2 * uf)
            dub = (cw_ref[j, 2:3, :] * dc + cw_ref[j, 1:2, :] * dc1 + cw_ref[j, 0:1, :] * dc2).astype(BF16)
            du_ref[:, cols] = dub
            dh = dh + _dot_nt(dub, wup_v[j])
        dh_s[...] = dh

        @pl.when(k == 1)
        def _():
            g2 = g2_ref[...]
            _, xh, rx = _rms_fwd(x_ref[...], g2)
            dg2_ref[...] += _rowsum8(dh * xh)
            dx_ref[...] = dxo_ref[...] + _rms_bwd(xh, rx, g2, dh)

    row = pl.BlockSpec((tm, d), lambda i, k: (nt - 1 - i, 0))
    half = pl.BlockSpec((tm, 2 * FF_CHUNK), lambda i, k: (nt - 1 - i, k))
    vec = _full((1, d))
    acc = _full((8, d))
    dcw_shape, dcb_shape = (N_SHARD, 3, 8, FF_CHUNK), (N_SHARD, 8, FF_CHUNK)
    return pl.pallas_call(
        body, grid=(nt, 2),
        in_specs=[row, row, row, half, half, vec, vec, ANY, ANY, _full(conv_w.shape)],
        out_specs=[row, half, row, acc, acc, _full(dcw_shape), _full(dcb_shape)],
        out_shape=[jax.ShapeDtypeStruct((t, d), F32), jax.ShapeDtypeStruct((t, 2 * D_FF), BF16),
                   jax.ShapeDtypeStruct((t, d), BF16),
                   jax.ShapeDtypeStruct((8, d), F32), jax.ShapeDtypeStruct((8, d), F32),
                   jax.ShapeDtypeStruct(dcw_shape, F32), jax.ShapeDtypeStruct(dcb_shape, F32)],
        scratch_shapes=[pltpu.VMEM(w_up.shape, BF16), pltpu.VMEM(w_down.shape, BF16),
                        pltpu.VMEM((tm, d), BF16), pltpu.VMEM((tm, d), F32),
                        pltpu.VMEM((N_SHARD, h8, FF_CHUNK), F32), pltpu.SemaphoreType.DMA((2,))],
        compiler_params=_params(2), name="mlp_bwd")(dxo, f, x, u, sp, g_pre, g_post, w_up, w_down, conv_w)


def grad_matmul(a, b, bm, bn, name, tk=2048, interleaved=False, after=None):
    t, m = a.shape
    n = b.shape[1]
    tk = min(tk, t)
    nk = t // tk
    place = (lambda j: (j % 2) * 2 + j // 2) if interleaved else (lambda j: j)
    extra = [] if after is None else [after]

    def body(a_ref, b_ref, *rest):
        o_ref, ob_ref = rest[len(extra):]
        kk = pl.program_id(2)

        @pl.when(kk == 0)
        def _():
            o_ref[...] = jnp.zeros_like(o_ref)

        o_ref[...] += _dot_tn(a_ref[...], b_ref[...])

        @pl.when(kk == nk - 1)
        def _():
            ob_ref[...] = o_ref[...].astype(BF16)

    ospec = pl.BlockSpec((None, bm, bn), lambda j, i, kk: (place(j), i, 0))
    return pl.pallas_call(
        body, grid=(n // bn, m // bm, nk),
        in_specs=[pl.BlockSpec((tk, bm), lambda j, i, kk: (kk, i)), pl.BlockSpec((tk, bn), lambda j, i, kk: (kk, j))]
        + [ANY] * len(extra),
        out_specs=[ospec, ospec],
        out_shape=[jax.ShapeDtypeStruct((n // bn, m, bn), F32), jax.ShapeDtypeStruct((n // bn, m, bn), BF16)],
        compiler_params=_params(3), name=name)(a, b, *extra)


def _decay_tables():
    log_gamma = jnp.log(1.0 - 2.0 ** (-5.0 - jnp.arange(RET_HEADS, dtype=F32)))
    i = jnp.arange(RET_CHUNK, dtype=F32)
    rel = i[:, None] - i[None, :]
    intra = jnp.where(rel >= 0, jnp.exp(jnp.maximum(rel, 0.0) * log_gamma[:, None, None]), 0.0)
    cross = jnp.exp((i + 1.0) * log_gamma[:, None])[:, :, None]
    inner = jnp.exp((RET_CHUNK - 1.0 - i) * log_gamma[:, None])[:, :, None]
    chunk = [float(np.exp(np.float32(RET_CHUNK) * np.log(np.float32(1.0 - 2.0 ** (-5.0 - h))).astype(np.float32)))
             for h in range(RET_HEADS)]
    return intra, cross, inner, chunk


def ret_proj(x, g_pre, w_in, cos, sin, tm=512):
    t, d = x.shape
    nt = t // tm
    per = RET_IN_SHARD // RET_QK

    def body(x_ref, g_ref, win_hbm, c_ref, s_ref, pj_ref, h_ref, win_v, sem):
        @pl.when(pl.program_id(0) == 0)
        def _():
            cp = pltpu.make_async_copy(win_hbm, win_v, sem)
            cp.start()
            cp.wait()

        h, _, _ = _rms_fwd(x_ref[...], g_ref[...])
        hb = h.astype(BF16)
        h_ref[...] = hb
        c = c_ref[...]
        s = s_ref[...]
        for j in range(N_SHARD):
            pjj = _dot(hb, win_v[j])
            for bb in range(per):
                b = per * j + bb
                blk = pjj[:, bb * RET_QK:(bb + 1) * RET_QK]
                if b < 2 * RET_HEADS:
                    x1, x2 = blk[:, :128], blk[:, 128:]
                    o1 = x1 * c - x2 * s
                    o2 = x2 * c + x1 * s
                    if b < RET_HEADS:
                        o1 = o1 * (RET_QK ** -0.5)
                        o2 = o2 * (RET_QK ** -0.5)
                    pj_ref[:, b * RET_QK:b * RET_QK + 128] = o1.astype(BF16)
                    pj_ref[:, b * RET_QK + 128:(b + 1) * RET_QK] = o2.astype(BF16)
                else:
                    pj_ref[:, b * RET_QK:(b + 1) * RET_QK] = blk.astype(BF16)

    row = pl.BlockSpec((tm, d), lambda i: (i, 0))
    tab = pl.BlockSpec((tm, 128), lambda i: (i, 0))
    return pl.pallas_call(
        body, grid=(nt,),
        in_specs=[row, _full((1, d)), ANY, tab, tab],
        out_specs=[pl.BlockSpec((tm, RET_IN), lambda i: (i, 0)), row],
        out_shape=[jax.ShapeDtypeStruct((t, RET_IN), BF16), jax.ShapeDtypeStruct((t, d), BF16)],
        scratch_shapes=[pltpu.VMEM(w_in.shape, BF16), pltpu.SemaphoreType.DMA],
        compiler_params=_params(1), name="ret_proj")(x, g_pre, w_in, cos, sin)


def ret_core_fwd(pj, intra, cross, inner, chunk_decay):
    t = pj.shape[0]
    nc = t // RET_CHUNK
    c = RET_CHUNK
    qk_all = RET_HEADS * RET_QK
    v_all = RET_HEADS * RET_V

    def body(q_ref, k_ref, v_ref, dm_ref, cr_ref, in_ref, o_ref, sp_ref, state):
        @pl.when(pl.program_id(0) == 0)
        def _():
            state[...] = jnp.zeros_like(state)

        for h in range(RET_HEADS):
            q = q_ref[:, h * RET_QK:(h + 1) * RET_QK]
            k = k_ref[:, h * RET_QK:(h + 1) * RET_QK]
            v = v_ref[:, h * RET_V:(h + 1) * RET_V]
            sb = state[h].astype(BF16)
            sp_ref[h] = sb
            sc = _dot_nt(q, k) * dm_ref[h]
            o_ref[:, h * RET_V:(h + 1) * RET_V] = _dot(sc.astype(BF16), v) + _dot(q, sb) * cr_ref[h]
            kd = (k.astype(F32) * in_ref[h]).astype(BF16)
            state[h] = state[h] * chunk_decay[h] + _dot_tn(kd, v)

    return pl.pallas_call(
        body, grid=(nc,),
        in_specs=[pl.BlockSpec((c, qk_all), lambda n: (n, 0)), pl.BlockSpec((c, qk_all), lambda n: (n, 1)),
                  pl.BlockSpec((c, v_all), lambda n: (n, 1)),
                  _full(intra.shape), _full(cross.shape), _full(inner.shape)],
        out_specs=[pl.BlockSpec((c, v_all), lambda n: (n, 0)),
                   pl.BlockSpec((None, RET_HEADS, RET_QK, RET_V), lambda n: (n, 0, 0, 0))],
        out_shape=[jax.ShapeDtypeStruct((t, v_all), F32),
                   jax.ShapeDtypeStruct((nc, RET_HEADS, RET_QK, RET_V), BF16)],
        scratch_shapes=[pltpu.VMEM((RET_HEADS, RET_QK, RET_V), F32)],
        compiler_params=_params(1), name="ret_core_fwd")(pj, pj, pj, intra, cross, inner)


def _group_norm(o_h):
    mu = jnp.mean(o_h, axis=-1, keepdims=True)
    dev = o_h - mu
    rstd = lax.rsqrt(jnp.mean(dev * dev, axis=-1, keepdims=True) + EPS)
    return dev * rstd, rstd


def ret_out_fwd(o, pj, x, gn_gain, g_post, w_out, tm=512):
    t, d = x.shape
    nt = t // tm
    v_all = RET_HEADS * RET_V

    def body(o_ref, g_ref, x_ref, gn_ref, g1_ref, w_ref, xo_ref, y_ref, out_ref):
        for h in range(RET_HEADS):
            cols = slice(h * RET_V, (h + 1) * RET_V)
            ohat, _ = _group_norm(o_ref[:, cols])
            g = g_ref[:, cols].astype(F32)
            y_ref[:, cols] = (g * jax.nn.sigmoid(g) * (ohat * gn_ref[:, cols])).astype(BF16)
        out = _dot(y_ref[...], w_ref[...])
        out_ref[...] = out
        m, _, _ = _rms_fwd(out, g1_ref[...])
        xo_ref[...] = x_ref[...] + m

    row = pl.BlockSpec((tm, d), lambda i: (i, 0))
    wide = pl.BlockSpec((tm, v_all), lambda i: (i, 0))
    return pl.pallas_call(
        body, grid=(nt,),
        in_specs=[wide, pl.BlockSpec((tm, v_all), lambda i: (i, 2)), row, _full((1, v_all)), _full((1, d)),
                  _full(w_out.shape)],
        out_specs=[row, wide, row],
        out_shape=[jax.ShapeDtypeStruct((t, d), F32), jax.ShapeDtypeStruct((t, v_all), BF16),
                   jax.ShapeDtypeStruct((t, d), F32)],
        compiler_params=_params(1), name="ret_out_fwd")(o, pj, x, gn_gain, g_post, w_out)


def ret_out_bwd(dxo, out, o, pj, gn_gain, g_post, w_out, tm=256):
    t, d = out.shape
    nt = t // tm
    v_all = RET_HEADS * RET_V

    def body(dxo_ref, out_ref, o_ref, g_ref, gn_ref, g1_ref, w_ref,
             dout_ref, dgate_ref, do_ref, dg1_ref, dgn_ref):
        @pl.when(pl.program_id(0) == 0)
        def _():
            dg1_ref[...] = jnp.zeros_like(dg1_ref)
            dgn_ref[...] = jnp.zeros_like(dgn_ref)

        g1 = g1_ref[...]
        dxo = dxo_ref[...]
        _, oh_, r_ = _rms_fwd(out_ref[...], g1)
        dg1_ref[...] += _colsum(dxo * oh_)
        doutb = _rms_bwd(oh_, r_, g1, dxo).astype(BF16)
        dout_ref[...] = doutb
        dy = _dot_nt(doutb, w_ref[...])
        for h in range(RET_HEADS):
            cols = slice(h * RET_V, (h + 1) * RET_V)
            gn = gn_ref[:, cols]
            ohat, rstd = _group_norm(o_ref[:, cols])
            g = g_ref[:, cols].astype(F32)
            sg = jax.nn.sigmoid(g)
            dyh = dy[:, cols]
            dgate_ref[:, cols] = (dyh * (ohat * gn) * (sg * (1.0 + g * (1.0 - sg)))).astype(BF16)
            don = dyh * (g * sg)
            dgn_ref[:, cols] += _colsum(don * ohat)
            dohat = don * gn
            do_ref[:, cols] = (rstd * (dohat - jnp.mean(dohat, axis=-1, keepdims=True)
                                       - ohat * jnp.mean(dohat * ohat, axis=-1, keepdims=True))).astype(BF16)

    row = pl.BlockSpec((tm, d), lambda i: (i, 0))
    wide = pl.BlockSpec((tm, v_all), lambda i: (i, 0))
    gate = pl.BlockSpec((tm, v_all), lambda i: (i, 2))
    return pl.pallas_call(
        body, grid=(nt,),
        in_specs=[row, row, wide, gate, _full((1, v_all)), _full((1, d)), _full(w_out.shape)],
        out_specs=[row, gate, wide, _full((1, d)), _full((1, v_all))],
        out_shape=[jax.ShapeDtypeStruct((t, d), BF16), jax.ShapeDtypeStruct((t, RET_IN), BF16),
                   jax.ShapeDtypeStruct((t, v_all), BF16), jax.ShapeDtypeStruct((1, d), F32),
                   jax.ShapeDtypeStruct((1, v_all), F32)],
        compiler_params=_params(1), name="ret_out_bwd")(dxo, out, o, pj, gn_gain, g_post, w_out)


def ret_core_bwd(pj, do, sprev, cos, sin, dpj, intra, cross, inner, chunk_decay):
    t = pj.shape[0]
    nc = t // RET_CHUNK
    c = RET_CHUNK
    qk_all = RET_HEADS * RET_QK
    v_all = RET_HEADS * RET_V
    scale = RET_QK ** -0.5

    def body(q_ref, k_ref, v_ref, do_ref, sp_ref, c_ref, s_ref, dm_ref, cr_ref, in_ref, dpj_in, dpj_ref, dstate):
        @pl.when(pl.program_id(0) == 0)
        def _():
            dstate[...] = jnp.zeros_like(dstate)

        cs = c_ref[...]
        sn = s_ref[...]
        for h in range(RET_HEADS):
            q = q_ref[:, h * RET_QK:(h + 1) * RET_QK]
            k = k_ref[:, h * RET_QK:(h + 1) * RET_QK]
            v = v_ref[:, h * RET_V:(h + 1) * RET_V]
            doh = do_ref[:, h * RET_V:(h + 1) * RET_V]
            dm = dm_ref[h]
            ab = (_dot_nt(q, k) * dm).astype(BF16)
            dab = (_dot_nt(doh, v) * dm).astype(BF16)
            dsb = dstate[h].astype(BF16)
            kd = (k.astype(F32) * in_ref[h]).astype(BF16)
            dv = _dot_tn(ab, doh) + _dot(kd, dsb)
            dq = _dot(dab, k) + cr_ref[h] * _dot_nt(doh, sp_ref[h])
            dk = _dot_tn(dab, q) + in_ref[h] * _dot_nt(v, dsb)
            qd = (q.astype(F32) * cr_ref[h]).astype(BF16)
            dstate[h] = dstate[h] * chunk_decay[h] + _dot_tn(qd, doh)
            for base, dd, sc in ((h * RET_QK, dq, scale), (qk_all + h * RET_QK, dk, 1.0)):
                d1, d2 = dd[:, :128], dd[:, 128:]
                dpj_ref[:, base:base + 128] = ((d1 * cs + d2 * sn) * sc).astype(BF16)
                dpj_ref[:, base + 128:base + RET_QK] = ((d2 * cs - d1 * sn) * sc).astype(BF16)
            dpj_ref[:, 2 * qk_all + h * RET_V:2 * qk_all + (h + 1) * RET_V] = dv.astype(BF16)

    rev = lambda n: nc - 1 - n
    tab = pl.BlockSpec((c, 128), lambda n: (rev(n), 0))
    return pl.pallas_call(
        body, grid=(nc,),
        in_specs=[pl.BlockSpec((c, qk_all), lambda n: (rev(n), 0)), pl.BlockSpec((c, qk_all), lambda n: (rev(n), 1)),
                  pl.BlockSpec((c, v_all), lambda n: (rev(n), 1)), pl.BlockSpec((c, v_all), lambda n: (rev(n), 0)),
                  pl.BlockSpec((None, RET_HEADS, RET_QK, RET_V), lambda n: (rev(n), 0, 0, 0)),
                  tab, tab, _full(intra.shape), _full(cross.shape), _full(inner.shape), ANY],
        out_specs=pl.BlockSpec((c, 2 * qk_all + v_all), lambda n: (rev(n), 0)),
        out_shape=jax.ShapeDtypeStruct((t, RET_IN), BF16),
        scratch_shapes=[pltpu.VMEM((RET_HEADS, RET_QK, RET_V), F32)],
        input_output_aliases={10: 0},
        compiler_params=_params(1), name="ret_core_bwd")(pj, pj, pj, do, sprev, cos, sin, intra, cross, inner, dpj)


def ret_in_bwd(dpj, dres, x, g_pre, w_in, tm=512):
    t, d = x.shape
    nt = t // tm

    def body(dpj_ref, dres_ref, x_ref, g_ref, win_hbm, dx_ref, dg_ref, win_v, sem):
        @pl.when(pl.program_id(0) == 0)
        def _():
            cp = pltpu.make_async_copy(win_hbm, win_v, sem)
            cp.start()
            dg_ref[...] = jnp.zeros_like(dg_ref)
            cp.wait()

        g = g_ref[...]
        dh = jnp.zeros((tm, d), F32)
        for j in range(N_SHARD):
            dh = dh + _dot_nt(dpj_ref[:, j * RET_IN_SHARD:(j + 1) * RET_IN_SHARD], win_v[j])
        _, xh, rx = _rms_fwd(x_ref[...], g)
        dg_ref[...] += _colsum(dh * xh)
        dx_ref[...] = dres_ref[...] + _rms_bwd(xh, rx, g, dh)

    row = pl.BlockSpec((tm, d), lambda i: (i, 0))
    return pl.pallas_call(
        body, grid=(nt,),
        in_specs=[pl.BlockSpec((tm, RET_IN), lambda i: (i, 0)), row, row, _full((1, d)), ANY],
        out_specs=[row, _full((1, d))],
        out_shape=[jax.ShapeDtypeStruct((t, d), F32), jax.ShapeDtypeStruct((1, d), F32)],
        scratch_shapes=[pltpu.VMEM(w_in.shape, BF16), pltpu.SemaphoreType.DMA],
        compiler_params=_params(1), name="ret_in_bwd")(dpj, dres, x, g_pre, w_in)


_CHIP_FLIPS = ((1, 0), (0, 1), (1, 1))


def _flip(v, b):
    return 1 - v if b else v


def scatter_grads(big, small):
    n = len(big)

    def body(*refs):
        ins, small_in = refs[:n], refs[n]
        outs, small_out = refs[n + 1:2 * n + 1], refs[2 * n + 1]
        send_sems, recv_sems, ssend_sems, srecv_sems, local_sem = refs[2 * n + 2:]
        x, y, c = lax.axis_index("x"), lax.axis_index("y"), lax.axis_index("c")
        mine = 4 * x + 2 * y + c
        copies = [pltpu.make_async_copy(small_in, small_out.at[mine], local_sem)]
        copies[0].start()
        for m in range(1, 8):
            bx, by, bc = (m >> 2) & 1, (m >> 1) & 1, m & 1
            cp = pltpu.make_async_remote_copy(
                src_ref=small_in, dst_ref=small_out.at[mine], send_sem=ssend_sems.at[m - 1],
                recv_sem=srecv_sems.at[m - 1], device_id=(_flip(x, bx), _flip(y, by), _flip(c, bc)),
                device_id_type=MESH)
            cp.start()
            copies.append(cp)
        for t in range(n):
            for k, (bx, by) in enumerate(_CHIP_FLIPS):
                px, py = _flip(x, bx), _flip(y, by)
                cp = pltpu.make_async_remote_copy(
                    src_ref=ins[t].at[2 * px + py], dst_ref=outs[t].at[k], send_sem=send_sems.at[3 * t + k],
                    recv_sem=recv_sems.at[3 * t + k], device_id=(px, py, c), device_id_type=MESH)
                cp.start()
                copies.append(cp)
        for cp in copies:
            cp.wait()

    return pl.pallas_call(
        body, in_specs=[ANY] * (n + 1), out_specs=[ANY] * (n + 1),
        out_shape=[jax.ShapeDtypeStruct((3,) + b.shape[1:], b.dtype) for b in big]
        + [jax.ShapeDtypeStruct((8,) + small.shape, small.dtype)],
        scratch_shapes=[pltpu.SemaphoreType.DMA((3 * n,)), pltpu.SemaphoreType.DMA((3 * n,)),
                        pltpu.SemaphoreType.DMA((7,)), pltpu.SemaphoreType.DMA((7,)), pltpu.SemaphoreType.DMA],
        name="scatter_grads")(*big, small)


_HBM = pl.BlockSpec(memory_space=pltpu.HBM)
_SEM = pl.BlockSpec(memory_space=pltpu.SEMAPHORE)
_EFFECT = pltpu.SideEffectType.DATAFLOW_SIDE_EFFECTING


def _chip_copies(mode, srcs, lands, send_sems, recv_sems):
    x, y, c = lax.axis_index("x"), lax.axis_index("y"), lax.axis_index("c")
    copies = []
    for t in range(len(lands)):
        if mode == "swap":
            copies.append(pltpu.make_async_remote_copy(
                src_ref=srcs[t], dst_ref=lands[t], send_sem=send_sems.at[t], recv_sem=recv_sems.at[t],
                device_id=(x, y, 1 - c), device_id_type=MESH))
            continue
        for k, (bx, by) in enumerate(_CHIP_FLIPS):
            px, py = _flip(x, bx), _flip(y, by)
            target = (px, py, c)
            if mode == "gather":
                src, dst = srcs[t], lands[t].at[2 * x + y]
            elif mode == "gather_half":
                half = pl.ds(c * (srcs[t].shape[0] // 2), srcs[t].shape[0] // 2)
                src, dst = srcs[t].at[half], lands[t].at[2 * x + y, half]
            elif mode == "forward_half":
                half = pl.ds(c * (lands[t].shape[1] // 2), lands[t].shape[1] // 2)
                src = dst = lands[t].at[2 * px + py, half]
                target = (x, y, 1 - c)
            else:
                src, dst = srcs[t].at[2 * px + py], lands[t].at[k]
            copies.append(pltpu.make_async_remote_copy(
                src_ref=src, dst_ref=dst, send_sem=send_sems.at[3 * t + k], recv_sem=recv_sems.at[3 * t + k],
                device_id=target, device_id_type=MESH))
    return copies


def exchange_start(mode, srcs, lands, name, after=None):
    n, ns = len(lands), len(srcs)
    extra = [] if after is None else [after]

    def body(*refs):
        ins, lnd = refs[:ns], refs[ns:ns + n]
        send_sems, recv_sems = refs[ns + n + len(extra)], refs[ns + n + len(extra) + 1]
        token = refs[-1]
        for cp in _chip_copies(mode, ins, lnd, send_sems, recv_sems):
            cp.start()
        token[...] = jnp.zeros(token.shape, token.dtype)

    hbm = lambda a: pltpu.with_memory_space_constraint(a, pltpu.HBM)
    passed = list(srcs) + list(lands)
    n_sem = n if mode == "swap" else 3 * n
    return pl.pallas_call(
        body, name=name,
        out_shape=(pltpu.SemaphoreType.DMA((n_sem,)), pltpu.SemaphoreType.DMA((n_sem,)),
                   *[pltpu.HBM(a.shape, a.dtype) for a in passed], jax.ShapeDtypeStruct((8, 128), F32)),
        in_specs=[_HBM] * (ns + n) + [ANY] * len(extra),
        out_specs=(_SEM, _SEM, *[_HBM] * (ns + n), pl.BlockSpec(memory_space=pltpu.VMEM)),
        input_output_aliases={i: 2 + i for i in range(ns + n)},
        compiler_params=pltpu.CompilerParams(has_side_effects=_EFFECT))(*[hbm(a) for a in passed], *extra)


def exchange_wait(mode, started, after, name):
    send_sems, recv_sems = started[0], started[1]
    passed = list(started[2:-1])
    n = len(passed) if mode == "forward_half" else len(passed) // 2
    ns = len(passed) - n

    def body(*refs):
        ins, lnd = refs[:ns], refs[ns:ns + n]
        for cp in _chip_copies(mode, ins, lnd, refs[ns + n], refs[ns + n + 1]):
            cp.wait_send()
            cp.wait_recv()

    outs = pl.pallas_call(
        body, name=name, out_shape=tuple(pltpu.HBM(a.shape, a.dtype) for a in passed),
        in_specs=[_HBM] * (ns + n) + [_SEM, _SEM, ANY], out_specs=tuple([_HBM] * (ns + n)),
        input_output_aliases={i: i for i in range(ns + n)},
        compiler_params=pltpu.CompilerParams(has_side_effects=_EFFECT))(*passed, send_sems, recv_sems, after)
    return list(outs[ns:])


def plane_sum(slot, full, recv, name, bm=256):
    _, m, n = full.shape
    bm = _row_block(m, bm)

    def body(slot_ref, o_ref, r_ref, s_ref):
        s_ref[...] = ((o_ref[...] + r_ref[0].astype(F32)) + r_ref[1].astype(F32)) + r_ref[2].astype(F32)

    return pl.pallas_call(
        body,
        grid_spec=pltpu.PrefetchScalarGridSpec(
            num_scalar_prefetch=1, grid=(m // bm,),
            in_specs=[pl.BlockSpec((None, bm, n), lambda i, s: (s[0], i, 0)),
                      pl.BlockSpec((3, bm, n), lambda i, s: (0, i, 0))],
            out_specs=pl.BlockSpec((bm, n), lambda i, s: (i, 0))),
        out_shape=jax.ShapeDtypeStruct((m, n), F32), compiler_params=_params(1), name=name)(slot, full, recv)


def sum_slots(parts, name, bm=312):
    _, r, n = parts.shape
    bm = bm if r % bm == 0 else r

    def body(p_ref, s_ref):
        acc = p_ref[0]
        for k in range(1, 8):
            acc = acc + p_ref[k]
        s_ref[...] = acc

    return pl.pallas_call(
        body, grid=(r // bm,), in_specs=[pl.BlockSpec((8, bm, n), lambda i: (0, i, 0))],
        out_specs=pl.BlockSpec((bm, n), lambda i: (i, 0)), out_shape=jax.ShapeDtypeStruct((r, n), F32),
        compiler_params=_params(1), name=name)(parts)


def _adamw_math(w, g, m, v):
    m = ADAM_B1 * m + (1.0 - ADAM_B1) * g
    v = ADAM_B2 * v + (1.0 - ADAM_B2) * (g * g)
    m_hat = m / (1.0 - ADAM_B1 ** ADAM_STEP)
    v_hat = v / (1.0 - ADAM_B2 ** ADAM_STEP)
    delta = -ADAM_LR * (m_hat / (jnp.sqrt(v_hat) + ADAM_EPS) + ADAM_WD * w)
    return delta, m, v


def adamw(w, m, v, grads, layer, prev, name, bm=256):
    _, mm, n = w.shape
    bm = _row_block(mm, bm)
    ng = len(grads)

    def body(*refs):
        w_ref, m_ref, v_ref = refs[:3]
        g_refs = refs[3:3 + ng]
        g_out, d_out, m_out, v_out = refs[-4:]
        g = g_refs[0][...]
        for gr in g_refs[1:]:
            g = g + gr[...]
        delta, mn, vn = _adamw_math(w_ref[...], g, m_ref[...], v_ref[...])
        g_out[...] = g
        d_out[...] = delta
        m_out[...] = mn
        v_out[...] = vn

    slab = pl.BlockSpec((None, bm, n), lambda i: (layer, i, 0))
    flat = pl.BlockSpec((bm, n), lambda i: (i, 0))
    in_specs = [slab] * 3 + [flat] * ng
    args = [w, m, v, *grads]
    aliases = {}
    if prev is not None:
        in_specs += [ANY] * 4
        aliases = {3 + ng + q: q for q in range(4)}
        args += list(prev)
    return pl.pallas_call(
        body, grid=(mm // bm,), in_specs=in_specs, out_specs=[slab] * 4,
        out_shape=[jax.ShapeDtypeStruct(w.shape, F32)] * 4, input_output_aliases=aliases,
        compiler_params=_params(1), name=name)(*args)


def _pack_rows(parts, rows):
    flat = jnp.concatenate([p.reshape(-1) for p in parts])
    return jnp.pad(flat, (0, rows * 128 - flat.shape[0])).reshape(rows, 128)


def _as_shards(a, rows):
    return a.reshape(N_SHARD, rows, a.shape[-1])


def _local_step(x, pos_col, target, gains, pool_w, pool_scale, gn_gain, conv_w, conv_b, weights, send_grads):
    def gain(l, n, token=None):
        g = gains[l, n].reshape(1, D_MODEL)
        return g if token is None else g + token[0:1, 0:1]

    inv_freq = (ROPE_BASE ** (-jnp.arange(0, RET_QK, 2, dtype=F32) / RET_QK)).reshape(1, RET_QK // 2)
    intra, cross, inner, chunk_decay = _decay_tables()
    dn_rows = D_FF // N_SHARD

    x1 = pool_fwd(x, gain(0, 0), gain(0, 1), pool_w, pool_scale)
    cos, sin = rope_tables(pos_col, inv_freq, x1)
    w_up0, w_dn0 = weights("mlp0", cos)
    w_dn0 = w_dn0.reshape(D_FF, D_MODEL)
    x2, u0, s0, a0, f0, h0 = mlp_fwd(x1, gain(0, 2), gain(0, 3), w_up0, w_dn0, conv_w[0], conv_b[0])
    w_in, w_out = weights("ret", x2)
    w_out = w_out.reshape(RET_HEADS * RET_V, D_MODEL)
    pj, hr = ret_proj(x2, gain(1, 0), w_in, cos, sin)
    o, sprev = ret_core_fwd(pj, intra, cross, inner, chunk_decay)
    x3, yb, out = ret_out_fwd(o, pj, x2, gn_gain, gain(1, 1), w_out)
    w_up1, w_dn1 = weights("mlp1", x3)
    w_dn1 = w_dn1.reshape(D_FF, D_MODEL)
    dx4, u1, s1, a1, f1, h1, loss = mlp_fwd(x3, gain(1, 2), gain(1, 3), w_up1, w_dn1, conv_w[1], conv_b[1], target)

    dx3, du1, df1, dg12, dg13, dcw1, dcb1 = mlp_bwd(
        dx4, f1, x3, u1, s1, gain(1, 2), gain(1, 3), w_up1, w_dn1, conv_w[1])
    dwup1 = grad_matmul(h1, du1, D_MODEL, FF_CHUNK, "grad_w_up_1", interleaved=True)
    dwdn1 = grad_matmul(a1, df1, FF_CHUNK, D_MODEL, "grad_w_down_1")
    tok = send_grads("mlp1", [dwup1, [_as_shards(g, dn_rows) for g in dwdn1]])
    dout, dpj, do, dg11, dgn = ret_out_bwd(dx3, out, o, pj, gn_gain, gain(1, 1, tok), w_out)
    dwout = grad_matmul(yb, dout, 1024, D_MODEL, "grad_w_out")
    dpj = ret_core_bwd(pj, do, sprev, cos, sin, dpj, intra, cross, inner, chunk_decay)
    dwin = grad_matmul(hr, dpj, D_MODEL, RET_IN_SHARD, "grad_w_in")
    tok = send_grads("ret", [dwin, [_as_shards(g, RET_V) for g in dwout]])
    dx2, dg10 = ret_in_bwd(dpj, dx3, x2, gain(1, 0, tok), w_in)
    dx1, du0, df0, dg02, dg03, dcw0, dcb0 = mlp_bwd(
        dx2, f0, x1, u0, s0, gain(0, 2), gain(0, 3), w_up0, w_dn0, conv_w[0])
    dwdn0 = grad_matmul(a0, df0, FF_CHUNK, D_MODEL, "grad_w_down_0")
    tok = send_grads("down0", [[_as_shards(g, dn_rows) for g in dwdn0]])
    dwup0 = grad_matmul(h0, du0, D_MODEL, FF_CHUNK, "grad_w_up_0", interleaved=True, after=tok)
    tok = send_grads("up0", [dwup0])
    dx0, dg00, dg01, dpscale, dpw = pool_bwd(dx1, x, gain(0, 0, tok), gain(0, 1), pool_w, pool_scale)

    rows = lambda g: jnp.sum(g, axis=0, keepdims=True)
    dgains = jnp.concatenate([dg00, dg01, rows(dg02), rows(dg03), dg10, dg11, rows(dg12), rows(dg13)],
                             axis=0).reshape(2, 4, D_MODEL)
    small = {"gains": dgains, "pool_scale": dpscale, "gn": dgn,
             "conv_w": jnp.sum(jnp.stack([dcw0, dcw1]), axis=3),
             "conv_b": jnp.sum(jnp.stack([dcb0, dcb1]), axis=2, keepdims=True), "pool_w": dpw}
    return loss, dx0, small


def kernel(x, positions, norm_gain, pool_w, pool_scale, ret_w_in, ret_gn_gain, ret_w_out, mlp_w_up, mlp_conv_w, mlp_conv_b, mlp_w_down, loss_target, m_norm_gain, m_pool_w, m_pool_scale, m_ret_w_in, m_ret_gn_gain, m_ret_w_out, m_mlp_w_up, m_mlp_conv_w, m_mlp_conv_b, m_mlp_w_down, v_norm_gain, v_pool_w, v_pool_scale, v_ret_w_in, v_ret_gn_gain, v_ret_w_out, v_mlp_w_up, v_mlp_conv_w, v_mlp_conv_b, v_mlp_w_down):
    t = x.shape[1]
    me = 2 * lax.axis_index("x") + lax.axis_index("y")
    me_slot = jnp.reshape(me, (1,)).astype(jnp.int32)

    small_parts = [norm_gain, ret_gn_gain, mlp_conv_w, pool_w]
    small_sizes = [p.size for p in small_parts]
    small_rows = -(-sum(small_sizes) // (128 * 8)) * 8
    groups = {"small": [_pack_rows(small_parts, small_rows)],
              "mlp0": [mlp_w_up[0].astype(BF16), mlp_w_down[0].astype(BF16)],
              "ret": [ret_w_in[0].astype(BF16), ret_w_out[0].astype(BF16)],
              "mlp1": [mlp_w_up[1].astype(BF16), mlp_w_down[1].astype(BF16)]}
    gathers, token = {}, None
    for group, srcs in groups.items():
        lands = [lax.dynamic_update_index_in_dim(lax.empty((N_SHARD,) + s.shape, s.dtype), s, me, 0) for s in srcs]
        mode = "gather_half" if group == "mlp0" else "gather"
        gathers[group] = (mode, exchange_start(mode, srcs, lands, "gather_start_" + group, after=token))
        token = gathers[group][1][-1]

    def weights(group, after):
        mode, started = gathers[group]
        lands = exchange_wait(mode, started, after, "gather_wait_" + group)
        if mode == "gather_half":
            forward = exchange_start("forward_half", [], lands, "forward_start_" + group)
            lands = exchange_wait("forward_half", forward, forward[-1], "forward_wait_" + group)
        return lands

    sent, early = {}, {}

    def reduced(group, after, names):
        started, own = sent[group]
        recv = exchange_wait("scatter", started, after, "scatter_wait_" + group)
        return [plane_sum(me_slot, f, r, "plane_sum_" + nm)
                for f, r, nm in zip(own, recv, names)]

    def swap_start(planes, name):
        return exchange_start("swap", planes, [lax.empty(p.shape, p.dtype) for p in planes], name)

    def send_grads(group, pairs):
        lands = [lax.empty((3,) + b.shape[1:], BF16) for _, b in pairs]
        sent[group] = (exchange_start("scatter", [b for _, b in pairs], lands, "scatter_start_" + group),
                       [f for f, _ in pairs])
        token = sent[group][0][-1]
        if group == "down0":
            marker = pairs[0][1]
            early["planes"] = (reduced("mlp1", marker, ["w_up_1", "w_down_1"])
                               + reduced("ret", marker, ["w_in", "w_out"]))
            early["swap"] = swap_start(early["planes"], "swap_start_a")
            token = token + early["swap"][-1]
        return token

    (smallg,) = weights("small", token)
    smallg = smallg.reshape(N_SHARD, -1)
    offs = np.cumsum([0] + small_sizes)
    piece = lambda i, shape: smallg[:, offs[i]:offs[i + 1]].reshape((N_SHARD,) + shape)
    gains = piece(0, (2, 4, 256)).transpose(1, 2, 0, 3).reshape(2, 4, D_MODEL)
    gn_full = piece(1, (512,)).reshape(1, RET_HEADS * RET_V)
    cw_full = piece(2, (2, 3, FF_CHUNK)).transpose(1, 0, 2, 3)
    pw_full = piece(3, (4, 64, 256)).transpose(1, 0, 2, 3).reshape(4, 256, 256).astype(BF16)
    cb_full = mlp_conv_b.reshape(2, N_SHARD, 1, FF_CHUNK)

    loss, dx0, small = _local_step(
        x[0], positions.reshape(t, 1).astype(F32), loss_target[0], gains, pw_full, pool_scale, gn_full,
        cw_full, cb_full, weights, send_grads)

    def small_adamw(w, m, v, grads, name):
        w3 = w.reshape(1, -1, w.shape[-1])
        out = adamw(w3, m.reshape(w3.shape), v.reshape(w3.shape), [g.reshape(w3.shape[1:]) for g in grads], 0, None, name)
        return [o.reshape(w.shape) for o in out]

    res = {}
    planes_a = early["planes"]
    others_a = exchange_wait("swap", early["swap"], dx0, "swap_wait_a")
    res["ret_w_in"] = adamw(ret_w_in, m_ret_w_in, v_ret_w_in, (planes_a[2], others_a[2]), 0, None, "adamw_w_in")
    res["ret_w_out"] = adamw(ret_w_out, m_ret_w_out, v_ret_w_out, (planes_a[3], others_a[3]), 0, None, "adamw_w_out")
    up1 = adamw(mlp_w_up, m_mlp_w_up, v_mlp_w_up, (planes_a[0], others_a[0]), 1, None, "adamw_w_up_1")
    dn1 = adamw(mlp_w_down, m_mlp_w_down, v_mlp_w_down, (planes_a[1], others_a[1]), 1, None, "adamw_w_down_1")

    pw_f = small["pool_w"].reshape(4, N_SHARD, 64, 256).transpose(1, 0, 2, 3).reshape(N_SHARD, 256, 256)
    small_order = ["gains", "pool_scale", "gn", "conv_w", "conv_b"]
    gsmall_sizes = [small[k].size for k in small_order]
    gsmall_rows = -(-sum(gsmall_sizes) // (128 * 8)) * 8
    planes_b = reduced("up0", dn1[0], ["w_up_0"]) + reduced("down0", dn1[0], ["w_down_0"])
    pw_recv, small_recv = scatter_grads([pw_f.astype(BF16)], _pack_rows([small[k] for k in small_order], gsmall_rows))
    planes_b.append(plane_sum(me_slot, pw_f, pw_recv, "plane_sum_pool_w"))
    swap_b = swap_start(planes_b, "swap_start_b")

    gsmall = sum_slots(small_recv, "sum_small").reshape(-1)
    goffs = np.cumsum([0] + gsmall_sizes)
    gpiece = lambda i: gsmall[goffs[i]:goffs[i + 1]].reshape(small[small_order[i]].shape)
    g_gains = lax.dynamic_slice_in_dim(gpiece(0), me * 256, 256, axis=2)
    g_gn = lax.dynamic_slice_in_dim(gpiece(2), me * RET_V, RET_V, axis=1)
    g_cw = lax.dynamic_index_in_dim(gpiece(3), me, 1, keepdims=False)
    res["norm_gain"] = small_adamw(norm_gain, m_norm_gain, v_norm_gain, [g_gains], "adamw_norm_gain")
    res["pool_scale"] = small_adamw(pool_scale, m_pool_scale, v_pool_scale, [gpiece(1)], "adamw_pool_scale")
    res["ret_gn_gain"] = small_adamw(ret_gn_gain, m_ret_gn_gain, v_ret_gn_gain, [g_gn], "adamw_gn_gain")
    res["mlp_conv_w"] = small_adamw(mlp_conv_w, m_mlp_conv_w, v_mlp_conv_w, [g_cw], "adamw_conv_w")
    res["mlp_conv_b"] = small_adamw(mlp_conv_b, m_mlp_conv_b, v_mlp_conv_b, [gpiece(4)], "adamw_conv_b")

    others_b = exchange_wait("swap", swap_b, res["mlp_conv_b"][0], "swap_wait_b")
    res["mlp_w_up"] = adamw(mlp_w_up, m_mlp_w_up, v_mlp_w_up, (planes_b[0], others_b[0]), 0, up1, "adamw_w_up_0")
    res["mlp_w_down"] = adamw(mlp_w_down, m_mlp_w_down, v_mlp_w_down, (planes_b[1], others_b[1]), 0, dn1,
                              "adamw_w_down_0")
    res["pool_w"] = small_adamw(pool_w, m_pool_w, v_pool_w, (planes_b[2], others_b[2]), "adamw_pool_w")

    order = ["norm_gain", "pool_w", "pool_scale", "ret_w_in", "ret_gn_gain", "ret_w_out", "mlp_w_up", "mlp_conv_w",
             "mlp_conv_b", "mlp_w_down"]
    total_loss = lax.psum(loss[0, 0], ("x", "y", "c"))
    outs = [total_loss, dx0.reshape(x.shape)]
    for q in range(4):
        outs += [res[k][q] for k in order]
    return tuple(outs)
```

```python
import numpy as np
import jax
import jax.numpy as jnp
from jax import lax
from jax.experimental import pallas as pl
from jax.experimental.pallas import tpu as pltpu

F32 = jnp.float32
BF16 = jnp.bfloat16

D_MODEL = 1024
D_FF = 2816
FF_CHUNK = 1408
N_SHARD = 4
POOL_WINDOWS = (2, 4, 8, 16)
POOL_DIM = 256
POOL_HALO = 16
RET_HEADS = 4
RET_QK = 256
RET_V = 512
RET_CHUNK = 256
RET_IN = 6144
RET_IN_SHARD = 1536
ROPE_BASE = 10000.0
EPS = 1e-6
CONV_HALO = 8

ADAM_LR, ADAM_B1, ADAM_B2, ADAM_EPS, ADAM_WD, ADAM_STEP = 0.001, 0.9, 0.999, 1e-08, 0.01, 10

VMEM_LIMIT = 56 * 1024 * 1024
MESH = pl.DeviceIdType.MESH
ANY = pl.BlockSpec(memory_space=pl.ANY)


def _params(n_grid=1, limit=VMEM_LIMIT):
    return pltpu.CompilerParams(dimension_semantics=("arbitrary",) * n_grid, vmem_limit_bytes=limit)


def _dot(a, b):
    return jnp.dot(a, b, preferred_element_type=F32)


def _dot_nt(a, b):
    return lax.dot_general(a, b, (((1,), (1,)), ((), ())), preferred_element_type=F32)


def _dot_tn(a, b):
    return lax.dot_general(a, b, (((0,), (0,)), ((), ())), preferred_element_type=F32)


def _rms_fwd(x, gain):
    r = lax.rsqrt(jnp.mean(x * x, axis=-1, keepdims=True) + EPS)
    xh = x * r
    return xh * gain, xh, r


def _rms_bwd(xh, r, gain, dy):
    dxh = dy * gain
    return r * (dxh - xh * jnp.mean(dxh * xh, axis=-1, keepdims=True))


def _colsum(v):
    return jnp.sum(v, axis=0, keepdims=True)


def _full(shape):
    nd = len(shape)
    return pl.BlockSpec(shape, lambda *_: (0,) * nd)


def rope_tables(pos_col, inv_freq, after):
    t = pos_col.shape[0]
    tm = min(t, 1024)

    def body(p_ref, f_ref, after_ref, c_ref, s_ref):
        ang = p_ref[...] * f_ref[...]
        c_ref[...] = jnp.cos(ang)
        s_ref[...] = jnp.sin(ang)

    return pl.pallas_call(
        body, grid=(t // tm,),
        in_specs=[pl.BlockSpec((tm, 1), lambda i: (i, 0)), _full((1, 128)), ANY],
        out_specs=[pl.BlockSpec((tm, 128), lambda i: (i, 0))] * 2,
        out_shape=[jax.ShapeDtypeStruct((t, 128), F32)] * 2,
        compiler_params=_params(1), name="rope_tables")(pos_col, inv_freq, after)


def _window_sums(ext, backward):
    n = ext.shape[0]
    cur, sums = ext, []
    for g, win in enumerate(POOL_WINDOWS):
        if g > 0:
            cur = cur[:, POOL_DIM:]
        half = win // 2
        cur = cur + pltpu.roll(cur, n - half if backward else half, axis=0)
        sums.append(cur[:, 0:POOL_DIM])
    return sums


def _pool_diff(h_halo, h, row0, tm):
    t_idx = row0 + lax.broadcasted_iota(jnp.int32, (tm, 1), 0)
    sums = _window_sums(jnp.concatenate([h_halo, h], axis=0), backward=False)
    parts, inv_counts = [], []
    for g, win in enumerate(POOL_WINDOWS):
        inv = 1.0 / jnp.minimum(t_idx + 1, win).astype(F32)
        parts.append(sums[g][POOL_HALO:, :] * inv - h[:, g * POOL_DIM:(g + 1) * POOL_DIM])
        inv_counts.append(inv)
    return parts, inv_counts


def pool_fwd(x, g_pre, g_post, pool_w, pool_scale, tm=512):
    t, d = x.shape
    nt = t // tm

    def body(x_ref, g0_ref, g1_ref, w_ref, sc_ref, o_ref, hext):
        i = pl.program_id(0)

        @pl.when(i == 0)
        def _():
            hext[...] = jnp.zeros((POOL_HALO, d), F32)

        xv = x_ref[...]
        h, _, _ = _rms_fwd(xv, g0_ref[...])
        parts, _ = _pool_diff(hext[...], h, i * tm, tm)
        hext[...] = h[tm - POOL_HALO:tm, :]
        ys = [_dot(parts[g].astype(BF16), w_ref[g]) for g in range(len(POOL_WINDOWS))]
        y = jnp.concatenate(ys, axis=-1) * sc_ref[...]
        m, _, _ = _rms_fwd(y, g1_ref[...])
        o_ref[...] = xv + m

    row = pl.BlockSpec((tm, d), lambda i: (i, 0))
    return pl.pallas_call(
        body, grid=(nt,),
        in_specs=[row, _full((1, d)), _full((1, d)), _full(pool_w.shape), _full((1, d))],
        out_specs=row, out_shape=jax.ShapeDtypeStruct((t, d), F32),
        scratch_shapes=[pltpu.VMEM((POOL_HALO, d), F32)],
        compiler_params=_params(1), name="pool_fwd")(x, g_pre, g_post, pool_w, pool_scale)


def pool_bwd(dx1, x, g_pre, g_post, pool_w, pool_scale, tm=512):
    t, d = x.shape
    nt = t // tm
    ng = len(POOL_WINDOWS)

    def body(dx1_ref, x_ref, xh_ref, g0_ref, g1_ref, w_ref, sc_ref,
             dx_ref, dg0_ref, dg1_ref, dsc_ref, dw_ref, enext):
        i = pl.program_id(0)
        r = nt - 1 - i

        @pl.when(i == 0)
        def _():
            enext[...] = jnp.zeros((POOL_HALO, d), F32)
            dg0_ref[...] = jnp.zeros_like(dg0_ref)
            dg1_ref[...] = jnp.zeros_like(dg1_ref)
            dsc_ref[...] = jnp.zeros_like(dsc_ref)
            dw_ref[...] = jnp.zeros_like(dw_ref)

        g0 = g0_ref[...]
        g1 = g1_ref[...]
        sc = sc_ref[...]
        xv = x_ref[...]
        h, xh, rx = _rms_fwd(xv, g0)
        h_halo, _, _ = _rms_fwd(xh_ref[...], g0)
        parts, inv_counts = _pool_diff(h_halo * jnp.where(r > 0, 1.0, 0.0), h, r * tm, tm)
        parts_b = [p.astype(BF16) for p in parts]
        ypre = jnp.concatenate([_dot(parts_b[g], w_ref[g]) for g in range(ng)], axis=-1)
        _, yh, ry = _rms_fwd(ypre * sc, g1)
        dm = dx1_ref[...]
        dg1_ref[...] += _colsum(dm * yh)
        dy = _rms_bwd(yh, ry, g1, dm)
        dsc_ref[...] += _colsum(dy * ypre)
        dyp = (dy * sc).astype(BF16)
        ddiffs = []
        for g in range(ng):
            cols = slice(g * POOL_DIM, (g + 1) * POOL_DIM)
            dw_ref[g] += _dot_tn(parts_b[g], dyp[:, cols])
            ddiffs.append(_dot_nt(dyp[:, cols], w_ref[g]))
        e = jnp.concatenate([ddiffs[g] * inv_counts[g] for g in range(ng)], axis=-1)
        sums = _window_sums(jnp.concatenate([e, enext[...]], axis=0), backward=True)
        enext[...] = e[0:POOL_HALO, :]
        dh = jnp.concatenate([sums[g][0:tm, :] - ddiffs[g] for g in range(ng)], axis=-1)
        dg0_ref[...] += _colsum(dh * xh)
        dx_ref[...] = dm + _rms_bwd(xh, rx, g0, dh)

    row = pl.BlockSpec((tm, d), lambda i: (nt - 1 - i, 0))
    halo = pl.BlockSpec((POOL_HALO, d), lambda i: (jnp.maximum((nt - 1 - i) * (tm // POOL_HALO) - 1, 0), 0))
    vec = _full((1, d))
    return pl.pallas_call(
        body, grid=(nt,),
        in_specs=[row, row, halo, vec, vec, _full(pool_w.shape), vec],
        out_specs=[row, vec, vec, vec, _full((ng, POOL_DIM, POOL_DIM))],
        out_shape=[jax.ShapeDtypeStruct((t, d), F32)] + [jax.ShapeDtypeStruct((1, d), F32)] * 3
        + [jax.ShapeDtypeStruct((ng, POOL_DIM, POOL_DIM), F32)],
        scratch_shapes=[pltpu.VMEM((POOL_HALO, d), F32)],
        compiler_params=_params(1), name="pool_bwd")(dx1, x, x, g_pre, g_post, pool_w, pool_scale)


def _conv_taps(cw_ref, j):
    return cw_ref[j, 0:1, :], cw_ref[j, 1:2, :], cw_ref[j, 2:3, :]


def _row_block(m, target=256):
    if m <= target:
        return m
    for b in range(target, 7, -8):
        if m % b == 0:
            return b
    return m


def mlp_fwd(x, g_pre, g_post, w_up, w_down, conv_w, conv_b, target=None, tm=256):
    t, d = x.shape
    nt = t // tm
    h8 = CONV_HALO
    with_loss = target is not None
    n_extra = 1 if with_loss else 0

    def body(x_ref, g2_ref, g3_ref, wup_hbm, wdn_hbm, cw_ref, cb_ref, *rest):
        tgt_ref = rest[0] if with_loss else None
        xo_ref, u_ref, s_ref, a_ref, f_ref, h_ref = rest[n_extra:n_extra + 6]
        loss_ref = rest[n_extra + 6] if with_loss else None
        wup_v, wdn_v, tail, sem = rest[-4:]
        i = pl.program_id(0)

        @pl.when(i == 0)
        def _():
            c1 = pltpu.make_async_copy(wup_hbm, wup_v, sem.at[0])
            c2 = pltpu.make_async_copy(wdn_hbm, wdn_v, sem.at[1])
            c1.start()
            c2.start()
            tail[...] = jnp.zeros_like(tail)
            if with_loss:
                loss_ref[...] = jnp.zeros_like(loss_ref)
            c1.wait()
            c2.wait()

        xv = x_ref[...]
        h, _, _ = _rms_fwd(xv, g2_ref[...])
        hb = h.astype(BF16)
        h_ref[...] = hb
        acc = jnp.zeros((tm, d), F32)
        for k in range(2):
            cs = []
            for s in range(2):
                j, cols = k + 2 * s, slice((2 * k + s) * FF_CHUNK, (2 * k + s + 1) * FF_CHUNK)
                uf = _dot(hb, wup_v[j])
                u_ref[:, cols] = uf.astype(BF16)
                ext = jnp.concatenate([tail[j], uf], axis=0)
                tail[j] = uf[tm - h8:tm, :]
                w0, w1, w2 = _conv_taps(cw_ref, j)
                cs.append(cb_ref[j] + w2 * uf + w1 * pltpu.roll(ext, 1, axis=0)[h8:, :]
                          + w0 * pltpu.roll(ext, 2, axis=0)[h8:, :])
            cg, cv = cs
            sg = jax.nn.sigmoid(cg)
            sil = cg * sg
            ab = (sil * cv).astype(BF16)
            a_ref[:, k * FF_CHUNK:(k + 1) * FF_CHUNK] = ab
            s_ref[:, 2 * k * FF_CHUNK:(2 * k + 1) * FF_CHUNK] = sil.astype(BF16)
            s_ref[:, (2 * k + 1) * FF_CHUNK:(2 * k + 2) * FF_CHUNK] = (cv * (sg * (1.0 + cg * (1.0 - sg)))).astype(BF16)
            acc = acc + _dot(ab, wdn_v[k * FF_CHUNK:(k + 1) * FF_CHUNK, :])
        f_ref[...] = acc
        y, _, _ = _rms_fwd(acc, g3_ref[...])
        if with_loss:
            err = (xv + y) - tgt_ref[...]
            xo_ref[...] = err * (1.0 / d)
            loss_ref[...] += 0.5 * jnp.sum(jnp.mean(err * err, axis=-1, keepdims=True), axis=0, keepdims=True)
        else:
            xo_ref[...] = xv + y

    row = pl.BlockSpec((tm, d), lambda i: (i, 0))
    wide = pl.BlockSpec((tm, 2 * D_FF), lambda i: (i, 0))
    vec = _full((1, d))
    extra = [target] if with_loss else []
    return pl.pallas_call(
        body, grid=(nt,),
        in_specs=[row, vec, vec, ANY, ANY, _full(conv_w.shape), _full(conv_b.shape)] + [row] * n_extra,
        out_specs=[row, wide, wide, pl.BlockSpec((tm, D_FF), lambda i: (i, 0)), row, row] + [_full((1, 1))] * n_extra,
        out_shape=[jax.ShapeDtypeStruct((t, d), F32), jax.ShapeDtypeStruct((t, 2 * D_FF), BF16),
                   jax.ShapeDtypeStruct((t, 2 * D_FF), BF16), jax.ShapeDtypeStruct((t, D_FF), BF16),
                   jax.ShapeDtypeStruct((t, d), F32), jax.ShapeDtypeStruct((t, d), BF16)]
        + [jax.ShapeDtypeStruct((1, 1), F32)] * n_extra,
        scratch_shapes=[pltpu.VMEM(w_up.shape, BF16), pltpu.VMEM(w_down.shape, BF16),
                        pltpu.VMEM((N_SHARD, h8, FF_CHUNK), F32), pltpu.SemaphoreType.DMA((2,))],
        compiler_params=_params(1), name="mlp_fwd_loss" if with_loss else "mlp_fwd")(
            x, g_pre, g_post, w_up, w_down, conv_w, conv_b, *extra)


def _rowsum8(v):
    return jnp.sum(v.reshape(v.shape[0] // 8, 8, v.shape[1]), axis=0)


def mlp_bwd(dxo, f, x, u, sp, g_pre, g_post, w_up, w_down, conv_w, tm=256):
    t, d = x.shape
    nt = t // tm
    h8 = CONV_HALO

    def body(dxo_ref, f_ref, x_ref, u_ref, s_ref, g2_ref, g3_ref, wup_hbm, wdn_hbm, cw_ref,
             dx_ref, du_ref, df_ref, dg2_ref, dg3_ref, dcw_ref, dcb_ref,
             wup_v, wdn_v, dfb_s, dh_s, carry, sem):
        i = pl.program_id(0)
        k = pl.program_id(1)

        @pl.when((i == 0) & (k == 0))
        def _():
            c1 = pltpu.make_async_copy(wup_hbm, wup_v, sem.at[0])
            c2 = pltpu.make_async_copy(wdn_hbm, wdn_v, sem.at[1])
            c1.start()
            c2.start()
            carry[...] = jnp.zeros_like(carry)
            dg2_ref[...] = jnp.zeros_like(dg2_ref)
            dg3_ref[...] = jnp.zeros_like(dg3_ref)
            dcw_ref[...] = jnp.zeros_like(dcw_ref)
            dcb_ref[...] = jnp.zeros_like(dcb_ref)
            c1.wait()
            c2.wait()

        @pl.when(k == 0)
        def _():
            g3 = g3_ref[...]
            dxo = dxo_ref[...]
            _, fh, rf = _rms_fwd(f_ref[...], g3)
            dg3_ref[...] += _rowsum8(dxo * fh)
            dfb = _rms_bwd(fh, rf, g3, dxo).astype(BF16)
            df_ref[...] = dfb
            dfb_s[...] = dfb
            dh_s[...] = jnp.zeros_like(dh_s)

        da = _dot_nt(dfb_s[...], wdn_v[pl.ds(pl.multiple_of(k * FF_CHUNK, 128), FF_CHUNK), :])
        dh = dh_s[...]
        for s in range(2):
            j = k + 2 * s
            cols = slice(s * FF_CHUNK, (s + 1) * FF_CHUNK)
            dc = da * s_ref[:, (1 - s) * FF_CHUNK:(2 - s) * FF_CHUNK].astype(F32)
            uf = u_ref[:, cols].astype(F32)
            ext = jnp.concatenate([dc, carry[j]], axis=0)
            carry[j] = dc[0:h8, :]
            dc1 = pltpu.roll(ext, tm + h8 - 1, axis=0)[0:tm, :]
            dc2 = pltpu.roll(ext, tm + h8 - 2, axis=0)[0:tm, :]
            dcb_ref[j] += _rowsum8(dc)
            dcw_ref[j, 2] += _rowsum8(dc * uf)
            dcw_ref[j, 1] += _rowsum8(dc1 * uf)
            dcw_ref[j, 0] += _rowsum8(dc2 * uf)
            dub = (cw_ref[j, 2:3, :] * dc + cw_ref[j, 1:2, :] * dc1 + cw_ref[j, 0:1, :] * dc2).astype(BF16)
            du_ref[:, cols] = dub
            dh = dh + _dot_nt(dub, wup_v[j])
        dh_s[...] = dh

        @pl.when(k == 1)
        def _():
            g2 = g2_ref[...]
            _, xh, rx = _rms_fwd(x_ref[...], g2)
            dg2_ref[...] += _rowsum8(dh * xh)
            dx_ref[...] = dxo_ref[...] + _rms_bwd(xh, rx, g2, dh)

    row = pl.BlockSpec((tm, d), lambda i, k: (nt - 1 - i, 0))
    half = pl.BlockSpec((tm, 2 * FF_CHUNK), lambda i, k: (nt - 1 - i, k))
    vec = _full((1, d))
    acc = _full((8, d))
    dcw_shape, dcb_shape = (N_SHARD, 3, 8, FF_CHUNK), (N_SHARD, 8, FF_CHUNK)
    return pl.pallas_call(
        body, grid=(nt, 2),
        in_specs=[row, row, row, half, half, vec, vec, ANY, ANY, _full(conv_w.shape)],
        out_specs=[row, half, row, acc, acc, _full(dcw_shape), _full(dcb_shape)],
        out_shape=[jax.ShapeDtypeStruct((t, d), F32), jax.ShapeDtypeStruct((t, 2 * D_FF), BF16),
                   jax.ShapeDtypeStruct((t, d), BF16),
                   jax.ShapeDtypeStruct((8, d), F32), jax.ShapeDtypeStruct((8, d), F32),
                   jax.ShapeDtypeStruct(dcw_shape, F32), jax.ShapeDtypeStruct(dcb_shape, F32)],
        scratch_shapes=[pltpu.VMEM(w_up.shape, BF16), pltpu.VMEM(w_down.shape, BF16),
                        pltpu.VMEM((tm, d), BF16), pltpu.VMEM((tm, d), F32),
                        pltpu.VMEM((N_SHARD, h8, FF_CHUNK), F32), pltpu.SemaphoreType.DMA((2,))],
        compiler_params=_params(2), name="mlp_bwd")(dxo, f, x, u, sp, g_pre, g_post, w_up, w_down, conv_w)


def grad_matmul(a, b, bm, bn, name, tk=2048, interleaved=False, after=None, cols=None):
    t = a.shape[0]
    m0, m = (0, a.shape[1]) if cols is None else cols
    n = b.shape[1]
    tk = min(tk, t)
    nk = t // tk
    place = (lambda j: (j % 2) * 2 + j // 2) if interleaved else (lambda j: j)
    extra = [] if after is None else [after]
    first = m0 // bm

    def body(a_ref, b_ref, *rest):
        o_ref, ob_ref = rest[len(extra):]
        kk = pl.program_id(2)

        @pl.when(kk == 0)
        def _():
            o_ref[...] = jnp.zeros_like(o_ref)

        o_ref[...] += _dot_tn(a_ref[...], b_ref[...])

        @pl.when(kk == nk - 1)
        def _():
            ob_ref[...] = o_ref[...].astype(BF16)

    ospec = pl.BlockSpec((None, bm, bn), lambda j, i, kk: (place(j), i, 0))
    return pl.pallas_call(
        body, grid=(n // bn, m // bm, nk),
        in_specs=[pl.BlockSpec((tk, bm), lambda j, i, kk: (kk, first + i)),
                  pl.BlockSpec((tk, bn), lambda j, i, kk: (kk, j))]
        + [ANY] * len(extra),
        out_specs=[ospec, ospec],
        out_shape=[jax.ShapeDtypeStruct((n // bn, m, bn), F32), jax.ShapeDtypeStruct((n // bn, m, bn), BF16)],
        compiler_params=_params(3), name=name)(a, b, *extra)


def _decay_tables():
    log_gamma = jnp.log(1.0 - 2.0 ** (-5.0 - jnp.arange(RET_HEADS, dtype=F32)))
    i = jnp.arange(RET_CHUNK, dtype=F32)
    rel = i[:, None] - i[None, :]
    intra = jnp.where(rel >= 0, jnp.exp(jnp.maximum(rel, 0.0) * log_gamma[:, None, None]), 0.0)
    cross = jnp.exp((i + 1.0) * log_gamma[:, None])[:, :, None]
    inner = jnp.exp((RET_CHUNK - 1.0 - i) * log_gamma[:, None])[:, :, None]
    chunk = [float(np.exp(np.float32(RET_CHUNK) * np.log(np.float32(1.0 - 2.0 ** (-5.0 - h))).astype(np.float32)))
             for h in range(RET_HEADS)]
    return intra, cross, inner, chunk


def ret_proj(x, g_pre, w_in, cos, sin, tm=512):
    t, d = x.shape
    nt = t // tm
    per = RET_IN_SHARD // RET_QK

    def body(x_ref, g_ref, win_hbm, c_ref, s_ref, pj_ref, h_ref, win_v, sem):
        @pl.when(pl.program_id(0) == 0)
        def _():
            cp = pltpu.make_async_copy(win_hbm, win_v, sem)
            cp.start()
            cp.wait()

        h, _, _ = _rms_fwd(x_ref[...], g_ref[...])
        hb = h.astype(BF16)
        h_ref[...] = hb
        c = c_ref[...]
        s = s_ref[...]
        for j in range(N_SHARD):
            pjj = _dot(hb, win_v[j])
            for bb in range(per):
                b = per * j + bb
                blk = pjj[:, bb * RET_QK:(bb + 1) * RET_QK]
                if b < 2 * RET_HEADS:
                    x1, x2 = blk[:, :128], blk[:, 128:]
                    o1 = x1 * c - x2 * s
                    o2 = x2 * c + x1 * s
                    if b < RET_HEADS:
                        o1 = o1 * (RET_QK ** -0.5)
                        o2 = o2 * (RET_QK ** -0.5)
                    pj_ref[:, b * RET_QK:b * RET_QK + 128] = o1.astype(BF16)
                    pj_ref[:, b * RET_QK + 128:(b + 1) * RET_QK] = o2.astype(BF16)
                else:
                    pj_ref[:, b * RET_QK:(b + 1) * RET_QK] = blk.astype(BF16)

    row = pl.BlockSpec((tm, d), lambda i: (i, 0))
    tab = pl.BlockSpec((tm, 128), lambda i: (i, 0))
    return pl.pallas_call(
        body, grid=(nt,),
        in_specs=[row, _full((1, d)), ANY, tab, tab],
        out_specs=[pl.BlockSpec((tm, RET_IN), lambda i: (i, 0)), row],
        out_shape=[jax.ShapeDtypeStruct((t, RET_IN), BF16), jax.ShapeDtypeStruct((t, d), BF16)],
        scratch_shapes=[pltpu.VMEM(w_in.shape, BF16), pltpu.SemaphoreType.DMA],
        compiler_params=_params(1), name="ret_proj")(x, g_pre, w_in, cos, sin)


def ret_core_fwd(pj, intra, cross, inner, chunk_decay):
    t = pj.shape[0]
    nc = t // RET_CHUNK
    c = RET_CHUNK
    qk_all = RET_HEADS * RET_QK
    v_all = RET_HEADS * RET_V

    def body(q_ref, k_ref, v_ref, dm_ref, cr_ref, in_ref, o_ref, sp_ref, state):
        @pl.when(pl.program_id(0) == 0)
        def _():
            state[...] = jnp.zeros_like(state)

        for h in range(RET_HEADS):
            q = q_ref[:, h * RET_QK:(h + 1) * RET_QK]
            k = k_ref[:, h * RET_QK:(h + 1) * RET_QK]
            v = v_ref[:, h * RET_V:(h + 1) * RET_V]
            sb = state[h].astype(BF16)
            sp_ref[h] = sb
            sc = _dot_nt(q, k) * dm_ref[h]
            o_ref[:, h * RET_V:(h + 1) * RET_V] = _dot(sc.astype(BF16), v) + _dot(q, sb) * cr_ref[h]
            kd = (k.astype(F32) * in_ref[h]).astype(BF16)
            state[h] = state[h] * chunk_decay[h] + _dot_tn(kd, v)

    return pl.pallas_call(
        body, grid=(nc,),
        in_specs=[pl.BlockSpec((c, qk_all), lambda n: (n, 0)), pl.BlockSpec((c, qk_all), lambda n: (n, 1)),
                  pl.BlockSpec((c, v_all), lambda n: (n, 1)),
                  _full(intra.shape), _full(cross.shape), _full(inner.shape)],
        out_specs=[pl.BlockSpec((c, v_all), lambda n: (n, 0)),
                   pl.BlockSpec((None, RET_HEADS, RET_QK, RET_V), lambda n: (n, 0, 0, 0))],
        out_shape=[jax.ShapeDtypeStruct((t, v_all), F32),
                   jax.ShapeDtypeStruct((nc, RET_HEADS, RET_QK, RET_V), BF16)],
        scratch_shapes=[pltpu.VMEM((RET_HEADS, RET_QK, RET_V), F32)],
        compiler_params=_params(1), name="ret_core_fwd")(pj, pj, pj, intra, cross, inner)


def _group_norm(o_h):
    mu = jnp.mean(o_h, axis=-1, keepdims=True)
    dev = o_h - mu
    rstd = lax.rsqrt(jnp.mean(dev * dev, axis=-1, keepdims=True) + EPS)
    return dev * rstd, rstd


def ret_out_fwd(o, pj, x, gn_gain, g_post, w_out, tm=256):
    t, d = x.shape
    nt = t // tm
    v_all = RET_HEADS * RET_V

    def body(o_ref, g_ref, x_ref, gn_ref, g1_ref, w_ref, xo_ref, y_ref, out_ref, oh_ref, sil_ref, pd_ref, rstd_ref):
        for h in range(RET_HEADS):
            cols = slice(h * RET_V, (h + 1) * RET_V)
            ohat, rstd = _group_norm(o_ref[:, cols])
            g = g_ref[:, cols].astype(F32)
            sg = jax.nn.sigmoid(g)
            sil = g * sg
            on = ohat * gn_ref[:, cols]
            y_ref[:, cols] = (sil * on).astype(BF16)
            oh_ref[:, cols] = ohat.astype(BF16)
            sil_ref[:, cols] = sil.astype(BF16)
            pd_ref[:, cols] = (on * (sg * (1.0 + g * (1.0 - sg)))).astype(BF16)
            rstd_ref[:, h:h + 1] = rstd
        out = _dot(y_ref[...], w_ref[...])
        out_ref[...] = out
        m, _, _ = _rms_fwd(out, g1_ref[...])
        xo_ref[...] = x_ref[...] + m

    row = pl.BlockSpec((tm, d), lambda i: (i, 0))
    wide = pl.BlockSpec((tm, v_all), lambda i: (i, 0))
    wide_bf = jax.ShapeDtypeStruct((t, v_all), BF16)
    return pl.pallas_call(
        body, grid=(nt,),
        in_specs=[wide, pl.BlockSpec((tm, v_all), lambda i: (i, 2)), row, _full((1, v_all)), _full((1, d)),
                  _full(w_out.shape)],
        out_specs=[row, wide, row, wide, wide, wide, pl.BlockSpec((tm, RET_HEADS), lambda i: (i, 0))],
        out_shape=[jax.ShapeDtypeStruct((t, d), F32), wide_bf, jax.ShapeDtypeStruct((t, d), F32),
                   wide_bf, wide_bf, wide_bf, jax.ShapeDtypeStruct((t, RET_HEADS), F32)],
        compiler_params=_params(1), name="ret_out_fwd")(o, pj, x, gn_gain, g_post, w_out)


def ret_out_bwd(dxo, out, ohat, sil, pd, rstd, gn_gain, g_post, w_out, tm=512):
    t, d = out.shape
    tm = min(tm, t)
    nt = t // tm
    v_all = RET_HEADS * RET_V

    def body(dxo_ref, out_ref, oh_ref, sil_ref, pd_ref, rstd_ref, gn_ref, g1_ref, w_ref,
             dout_ref, dgate_ref, do_ref, dg1_ref, dgn_ref):
        @pl.when(pl.program_id(0) == 0)
        def _():
            dg1_ref[...] = jnp.zeros_like(dg1_ref)
            dgn_ref[...] = jnp.zeros_like(dgn_ref)

        g1 = g1_ref[...]
        dxo = dxo_ref[...]
        _, oh_, r_ = _rms_fwd(out_ref[...], g1)
        dg1_ref[...] += _colsum(dxo * oh_)
        doutb = _rms_bwd(oh_, r_, g1, dxo).astype(BF16)
        dout_ref[...] = doutb
        dy = _dot_nt(doutb, w_ref[...])
        for h in range(RET_HEADS):
            cols = slice(h * RET_V, (h + 1) * RET_V)
            ohat = oh_ref[:, cols].astype(F32)
            dyh = dy[:, cols]
            dgate_ref[:, cols] = (dyh * pd_ref[:, cols].astype(F32)).astype(BF16)
            don = dyh * sil_ref[:, cols].astype(F32)
            dgn_ref[:, cols] += _colsum(don * ohat)
            dohat = don * gn_ref[:, cols]
            do_ref[:, cols] = (rstd_ref[:, h:h + 1]
                               * (dohat - jnp.mean(dohat, axis=-1, keepdims=True)
                                  - ohat * jnp.mean(dohat * ohat, axis=-1, keepdims=True))).astype(BF16)

    row = pl.BlockSpec((tm, d), lambda i: (i, 0))
    wide = pl.BlockSpec((tm, v_all), lambda i: (i, 0))
    gate = pl.BlockSpec((tm, v_all), lambda i: (i, 2))
    return pl.pallas_call(
        body, grid=(nt,),
        in_specs=[row, row, wide, wide, wide, pl.BlockSpec((tm, RET_HEADS), lambda i: (i, 0)),
                  _full((1, v_all)), _full((1, d)), _full(w_out.shape)],
        out_specs=[row, gate, wide, _full((1, d)), _full((1, v_all))],
        out_shape=[jax.ShapeDtypeStruct((t, d), BF16), jax.ShapeDtypeStruct((t, RET_IN), BF16),
                   jax.ShapeDtypeStruct((t, v_all), BF16), jax.ShapeDtypeStruct((1, d), F32),
                   jax.ShapeDtypeStruct((1, v_all), F32)],
        compiler_params=_params(1), name="ret_out_bwd")(dxo, out, ohat, sil, pd, rstd, gn_gain, g_post, w_out)


def ret_core_bwd(pj, do, sprev, cos, sin, dpj, intra, cross, inner, chunk_decay):
    t = pj.shape[0]
    nc = t // RET_CHUNK
    c = RET_CHUNK
    qk_all = RET_HEADS * RET_QK
    v_all = RET_HEADS * RET_V
    scale = RET_QK ** -0.5

    def body(q_ref, k_ref, v_ref, do_ref, sp_ref, c_ref, s_ref, dm_ref, cr_ref, in_ref, dpj_in, dpj_ref, dstate):
        @pl.when(pl.program_id(0) == 0)
        def _():
            dstate[...] = jnp.zeros_like(dstate)

        cs = c_ref[...]
        sn = s_ref[...]
        for h in range(RET_HEADS):
            q = q_ref[:, h * RET_QK:(h + 1) * RET_QK]
            k = k_ref[:, h * RET_QK:(h + 1) * RET_QK]
            v = v_ref[:, h * RET_V:(h + 1) * RET_V]
            doh = do_ref[:, h * RET_V:(h + 1) * RET_V]
            dm = dm_ref[h]
            ab = (_dot_nt(q, k) * dm).astype(BF16)
            dab = (_dot_nt(doh, v) * dm).astype(BF16)
            dsb = dstate[h].astype(BF16)
            kd = (k.astype(F32) * in_ref[h]).astype(BF16)
            dv = _dot_tn(ab, doh) + _dot(kd, dsb)
            dq = _dot(dab, k) + cr_ref[h] * _dot_nt(doh, sp_ref[h])
            dk = _dot_tn(dab, q) + in_ref[h] * _dot_nt(v, dsb)
            qd = (q.astype(F32) * cr_ref[h]).astype(BF16)
            dstate[h] = dstate[h] * chunk_decay[h] + _dot_tn(qd, doh)
            for base, dd, sc in ((h * RET_QK, dq, scale), (qk_all + h * RET_QK, dk, 1.0)):
                d1, d2 = dd[:, :128], dd[:, 128:]
                dpj_ref[:, base:base + 128] = ((d1 * cs + d2 * sn) * sc).astype(BF16)
                dpj_ref[:, base + 128:base + RET_QK] = ((d2 * cs - d1 * sn) * sc).astype(BF16)
            dpj_ref[:, 2 * qk_all + h * RET_V:2 * qk_all + (h + 1) * RET_V] = dv.astype(BF16)

    rev = lambda n: nc - 1 - n
    tab = pl.BlockSpec((c, 128), lambda n: (rev(n), 0))
    return pl.pallas_call(
        body, grid=(nc,),
        in_specs=[pl.BlockSpec((c, qk_all), lambda n: (rev(n), 0)), pl.BlockSpec((c, qk_all), lambda n: (rev(n), 1)),
                  pl.BlockSpec((c, v_all), lambda n: (rev(n), 1)), pl.BlockSpec((c, v_all), lambda n: (rev(n), 0)),
                  pl.BlockSpec((None, RET_HEADS, RET_QK, RET_V), lambda n: (rev(n), 0, 0, 0)),
                  tab, tab, _full(intra.shape), _full(cross.shape), _full(inner.shape), ANY],
        out_specs=pl.BlockSpec((c, 2 * qk_all + v_all), lambda n: (rev(n), 0)),
        out_shape=jax.ShapeDtypeStruct((t, RET_IN), BF16),
        scratch_shapes=[pltpu.VMEM((RET_HEADS, RET_QK, RET_V), F32)],
        input_output_aliases={10: 0},
        compiler_params=_params(1), name="ret_core_bwd")(pj, pj, pj, do, sprev, cos, sin, intra, cross, inner, dpj)


def ret_in_bwd(dpj, dres, x, g_pre, w_in, tm=512):
    t, d = x.shape
    nt = t // tm

    def body(dpj_ref, dres_ref, x_ref, g_ref, win_hbm, dx_ref, dg_ref, win_v, sem):
        @pl.when(pl.program_id(0) == 0)
        def _():
            cp = pltpu.make_async_copy(win_hbm, win_v, sem)
            cp.start()
            dg_ref[...] = jnp.zeros_like(dg_ref)
            cp.wait()

        g = g_ref[...]
        dh = jnp.zeros((tm, d), F32)
        for j in range(N_SHARD):
            dh = dh + _dot_nt(dpj_ref[:, j * RET_IN_SHARD:(j + 1) * RET_IN_SHARD], win_v[j])
        _, xh, rx = _rms_fwd(x_ref[...], g)
        dg_ref[...] += _colsum(dh * xh)
        dx_ref[...] = dres_ref[...] + _rms_bwd(xh, rx, g, dh)

    row = pl.BlockSpec((tm, d), lambda i: (i, 0))
    return pl.pallas_call(
        body, grid=(nt,),
        in_specs=[pl.BlockSpec((tm, RET_IN), lambda i: (i, 0)), row, row, _full((1, d)), ANY],
        out_specs=[row, _full((1, d))],
        out_shape=[jax.ShapeDtypeStruct((t, d), F32), jax.ShapeDtypeStruct((1, d), F32)],
        scratch_shapes=[pltpu.VMEM(w_in.shape, BF16), pltpu.SemaphoreType.DMA],
        compiler_params=_params(1), name="ret_in_bwd")(dpj, dres, x, g_pre, w_in)


_CHIP_FLIPS = ((1, 0), (0, 1), (1, 1))


def _flip(v, b):
    return 1 - v if b else v


def scatter_grads(big, small):
    n = len(big)

    def body(*refs):
        ins, small_in = refs[:n], refs[n]
        outs, small_out = refs[n + 1:2 * n + 1], refs[2 * n + 1]
        send_sems, recv_sems, ssend_sems, srecv_sems, local_sem = refs[2 * n + 2:]
        x, y, c = lax.axis_index("x"), lax.axis_index("y"), lax.axis_index("c")
        mine = 4 * x + 2 * y + c
        copies = [pltpu.make_async_copy(small_in, small_out.at[mine], local_sem)]
        copies[0].start()
        for m in range(1, 8):
            bx, by, bc = (m >> 2) & 1, (m >> 1) & 1, m & 1
            cp = pltpu.make_async_remote_copy(
                src_ref=small_in, dst_ref=small_out.at[mine], send_sem=ssend_sems.at[m - 1],
                recv_sem=srecv_sems.at[m - 1], device_id=(_flip(x, bx), _flip(y, by), _flip(c, bc)),
                device_id_type=MESH)
            cp.start()
            copies.append(cp)
        for t in range(n):
            for k, (bx, by) in enumerate(_CHIP_FLIPS):
                px, py = _flip(x, bx), _flip(y, by)
                cp = pltpu.make_async_remote_copy(
                    src_ref=ins[t].at[2 * px + py], dst_ref=outs[t].at[k], send_sem=send_sems.at[3 * t + k],
                    recv_sem=recv_sems.at[3 * t + k], device_id=(px, py, c), device_id_type=MESH)
                cp.start()
                copies.append(cp)
        for cp in copies:
            cp.wait()

    return pl.pallas_call(
        body, in_specs=[ANY] * (n + 1), out_specs=[ANY] * (n + 1),
        out_shape=[jax.ShapeDtypeStruct((3,) + b.shape[1:], b.dtype) for b in big]
        + [jax.ShapeDtypeStruct((8,) + small.shape, small.dtype)],
        scratch_shapes=[pltpu.SemaphoreType.DMA((3 * n,)), pltpu.SemaphoreType.DMA((3 * n,)),
                        pltpu.SemaphoreType.DMA((7,)), pltpu.SemaphoreType.DMA((7,)), pltpu.SemaphoreType.DMA],
        name="scatter_grads")(*big, small)


_HBM = pl.BlockSpec(memory_space=pltpu.HBM)
_SEM = pl.BlockSpec(memory_space=pltpu.SEMAPHORE)
_EFFECT = pltpu.SideEffectType.DATAFLOW_SIDE_EFFECTING


def _chip_copies(mode, srcs, lands, send_sems, recv_sems):
    x, y, c = lax.axis_index("x"), lax.axis_index("y"), lax.axis_index("c")
    copies = []
    for t in range(len(lands)):
        if mode == "swap":
            copies.append(pltpu.make_async_remote_copy(
                src_ref=srcs[t], dst_ref=lands[t], send_sem=send_sems.at[t], recv_sem=recv_sems.at[t],
                device_id=(x, y, 1 - c), device_id_type=MESH))
            continue
        for k, (bx, by) in enumerate(_CHIP_FLIPS):
            px, py = _flip(x, bx), _flip(y, by)
            target = (px, py, c)
            if mode == "gather":
                src, dst = srcs[t], lands[t].at[2 * x + y]
            elif mode == "gather_half":
                half = pl.ds(c * (srcs[t].shape[0] // 2), srcs[t].shape[0] // 2)
                src, dst = srcs[t].at[half], lands[t].at[2 * x + y, half]
            elif mode == "forward_half":
                half = pl.ds(c * (lands[t].shape[1] // 2), lands[t].shape[1] // 2)
                src = dst = lands[t].at[2 * px + py, half]
                target = (x, y, 1 - c)
            else:
                src, dst = srcs[t].at[2 * px + py], lands[t].at[k]
            copies.append(pltpu.make_async_remote_copy(
                src_ref=src, dst_ref=dst, send_sem=send_sems.at[3 * t + k], recv_sem=recv_sems.at[3 * t + k],
                device_id=target, device_id_type=MESH))
    return copies


def exchange_start(mode, srcs, lands, name, after=None):
    n, ns = len(lands), len(srcs)
    extra = [] if after is None else [after]

    def body(*refs):
        ins, lnd = refs[:ns], refs[ns:ns + n]
        send_sems, recv_sems = refs[ns + n + len(extra)], refs[ns + n + len(extra) + 1]
        token = refs[-1]
        for cp in _chip_copies(mode, ins, lnd, send_sems, recv_sems):
            cp.start()
        token[...] = jnp.zeros(token.shape, token.dtype)

    hbm = lambda a: pltpu.with_memory_space_constraint(a, pltpu.HBM)
    passed = list(srcs) + list(lands)
    n_sem = n if mode == "swap" else 3 * n
    return pl.pallas_call(
        body, name=name,
        out_shape=(pltpu.SemaphoreType.DMA((n_sem,)), pltpu.SemaphoreType.DMA((n_sem,)),
                   *[pltpu.HBM(a.shape, a.dtype) for a in passed], jax.ShapeDtypeStruct((8, 128), F32)),
        in_specs=[_HBM] * (ns + n) + [ANY] * len(extra),
        out_specs=(_SEM, _SEM, *[_HBM] * (ns + n), pl.BlockSpec(memory_space=pltpu.VMEM)),
        input_output_aliases={i: 2 + i for i in range(ns + n)},
        compiler_params=pltpu.CompilerParams(has_side_effects=_EFFECT))(*[hbm(a) for a in passed], *extra)


def exchange_wait(mode, started, after, name):
    send_sems, recv_sems = started[0], started[1]
    passed = list(started[2:-1])
    n = len(passed) if mode == "forward_half" else len(passed) // 2
    ns = len(passed) - n

    def body(*refs):
        ins, lnd = refs[:ns], refs[ns:ns + n]
        for cp in _chip_copies(mode, ins, lnd, refs[ns + n], refs[ns + n + 1]):
            cp.wait_send()
            cp.wait_recv()

    outs = pl.pallas_call(
        body, name=name, out_shape=tuple(pltpu.HBM(a.shape, a.dtype) for a in passed),
        in_specs=[_HBM] * (ns + n) + [_SEM, _SEM, ANY], out_specs=tuple([_HBM] * (ns + n)),
        input_output_aliases={i: i for i in range(ns + n)},
        compiler_params=pltpu.CompilerParams(has_side_effects=_EFFECT))(*passed, send_sems, recv_sems, after)
    return list(outs[ns:])


def plane_sum(slot, full, recv, name, bm=256):
    _, m, n = full.shape
    bm = _row_block(m, bm)

    def body(slot_ref, o_ref, r_ref, s_ref):
        s_ref[...] = ((o_ref[...] + r_ref[0].astype(F32)) + r_ref[1].astype(F32)) + r_ref[2].astype(F32)

    return pl.pallas_call(
        body,
        grid_spec=pltpu.PrefetchScalarGridSpec(
            num_scalar_prefetch=1, grid=(m // bm,),
            in_specs=[pl.BlockSpec((None, bm, n), lambda i, s: (s[0], i, 0)),
                      pl.BlockSpec((3, bm, n), lambda i, s: (0, i, 0))],
            out_specs=pl.BlockSpec((bm, n), lambda i, s: (i, 0))),
        out_shape=jax.ShapeDtypeStruct((m, n), F32), compiler_params=_params(1), name=name)(slot, full, recv)


def sum_slots(parts, name, bm=312):
    _, r, n = parts.shape
    bm = bm if r % bm == 0 else r

    def body(p_ref, s_ref):
        acc = p_ref[0]
        for k in range(1, 8):
            acc = acc + p_ref[k]
        s_ref[...] = acc

    return pl.pallas_call(
        body, grid=(r // bm,), in_specs=[pl.BlockSpec((8, bm, n), lambda i: (0, i, 0))],
        out_specs=pl.BlockSpec((bm, n), lambda i: (i, 0)), out_shape=jax.ShapeDtypeStruct((r, n), F32),
        compiler_params=_params(1), name=name)(parts)


def _adamw_math(w, g, m, v):
    m = ADAM_B1 * m + (1.0 - ADAM_B1) * g
    v = ADAM_B2 * v + (1.0 - ADAM_B2) * (g * g)
    m_hat = m / (1.0 - ADAM_B1 ** ADAM_STEP)
    v_hat = v / (1.0 - ADAM_B2 ** ADAM_STEP)
    delta = -ADAM_LR * (m_hat / (jnp.sqrt(v_hat) + ADAM_EPS) + ADAM_WD * w)
    return delta, m, v


def adamw(w, m, v, grads, layer, prev, name, bm=256, row0=0):
    _, _, n = w.shape
    mm = grads[0].shape[0]
    bm = _row_block(mm, bm)
    first = row0 // bm
    ng = len(grads)

    def body(*refs):
        w_ref, m_ref, v_ref = refs[:3]
        g_refs = refs[3:3 + ng]
        g_out, d_out, m_out, v_out = refs[-4:]
        g = g_refs[0][...]
        for gr in g_refs[1:]:
            g = g + gr[...]
        delta, mn, vn = _adamw_math(w_ref[...], g, m_ref[...], v_ref[...])
        g_out[...] = g
        d_out[...] = delta
        m_out[...] = mn
        v_out[...] = vn

    slab = pl.BlockSpec((None, bm, n), lambda i: (layer, first + i, 0))
    flat = pl.BlockSpec((bm, n), lambda i: (i, 0))
    in_specs = [slab] * 3 + [flat] * ng
    args = [w, m, v, *grads]
    aliases = {}
    if prev is not None:
        in_specs += [ANY] * 4
        aliases = {3 + ng + q: q for q in range(4)}
        args += list(prev)
    return pl.pallas_call(
        body, grid=(mm // bm,), in_specs=in_specs, out_specs=[slab] * 4,
        out_shape=[jax.ShapeDtypeStruct(w.shape, F32)] * 4, input_output_aliases=aliases,
        compiler_params=_params(1), name=name)(*args)


def _pack_rows(parts, rows):
    flat = jnp.concatenate([p.reshape(-1) for p in parts])
    return jnp.pad(flat, (0, rows * 128 - flat.shape[0])).reshape(rows, 128)


def _as_shards(a, rows):
    return a.reshape(N_SHARD, rows, a.shape[-1])


def _local_step(x, pos_col, target, gains, pool_w, pool_scale, gn_gain, conv_w, conv_b, weights, send_grads):
    def gain(l, n, token=None):
        g = gains[l, n].reshape(1, D_MODEL)
        return g if token is None else g + token[0:1, 0:1]

    inv_freq = (ROPE_BASE ** (-jnp.arange(0, RET_QK, 2, dtype=F32) / RET_QK)).reshape(1, RET_QK // 2)
    intra, cross, inner, chunk_decay = _decay_tables()
    dn_rows = D_FF // N_SHARD

    x1 = pool_fwd(x, gain(0, 0), gain(0, 1), pool_w, pool_scale)
    cos, sin = rope_tables(pos_col, inv_freq, x1)
    w_up0, w_dn0 = weights("mlp0", cos)
    w_dn0 = w_dn0.reshape(D_FF, D_MODEL)
    x2, u0, s0, a0, f0, h0 = mlp_fwd(x1, gain(0, 2), gain(0, 3), w_up0, w_dn0, conv_w[0], conv_b[0])
    w_in, w_out = weights("ret", x2)
    w_out = w_out.reshape(RET_HEADS * RET_V, D_MODEL)
    pj, hr = ret_proj(x2, gain(1, 0), w_in, cos, sin)
    o, sprev = ret_core_fwd(pj, intra, cross, inner, chunk_decay)
    x3, yb, out, *norm_saved = ret_out_fwd(o, pj, x2, gn_gain, gain(1, 1), w_out)
    w_up1, w_dn1 = weights("mlp1", x3)
    w_dn1 = w_dn1.reshape(D_FF, D_MODEL)
    dx4, u1, s1, a1, f1, h1, loss = mlp_fwd(x3, gain(1, 2), gain(1, 3), w_up1, w_dn1, conv_w[1], conv_b[1], target)

    dx3, du1, df1, dg12, dg13, dcw1, dcb1 = mlp_bwd(
        dx4, f1, x3, u1, s1, gain(1, 2), gain(1, 3), w_up1, w_dn1, conv_w[1])
    dwup1 = grad_matmul(h1, du1, D_MODEL, FF_CHUNK, "grad_w_up_1", interleaved=True)
    dwdn1 = grad_matmul(a1, df1, FF_CHUNK, D_MODEL, "grad_w_down_1")
    tok = send_grads("mlp1", [dwup1, [_as_shards(g, dn_rows) for g in dwdn1]])
    dout, dpj, do, dg11, dgn = ret_out_bwd(dx3, out, *norm_saved, gn_gain, gain(1, 1, tok), w_out)
    dwout = grad_matmul(yb, dout, 1024, D_MODEL, "grad_w_out")
    dpj = ret_core_bwd(pj, do, sprev, cos, sin, dpj, intra, cross, inner, chunk_decay)
    dwin = grad_matmul(hr, dpj, D_MODEL, RET_IN_SHARD, "grad_w_in")
    tok = send_grads("ret", [dwin, [_as_shards(g, RET_V) for g in dwout]])
    dx2, dg10 = ret_in_bwd(dpj, dx3, x2, gain(1, 0, tok), w_in)
    dx1, du0, df0, dg02, dg03, dcw0, dcb0 = mlp_bwd(
        dx2, f0, x1, u0, s0, gain(0, 2), gain(0, 3), w_up0, w_dn0, conv_w[0])
    dwdn0 = grad_matmul(a0, df0, FF_CHUNK, D_MODEL, "grad_w_down_0")
    tok = send_grads("down0", [[_as_shards(g, dn_rows) for g in dwdn0]])
    half = D_MODEL // 2
    for part, first in (("a", 0), ("b", half)):
        dwup0 = grad_matmul(h0, du0, half, FF_CHUNK, "grad_w_up_0" + part, interleaved=True, after=tok,
                            cols=(first, half))
        tok = send_grads("up0" + part, [dwup0])
    dx0, dg00, dg01, dpscale, dpw = pool_bwd(dx1, x, gain(0, 0, tok), gain(0, 1), pool_w, pool_scale)

    rows = lambda g: jnp.sum(g, axis=0, keepdims=True)
    dgains = jnp.concatenate([dg00, dg01, rows(dg02), rows(dg03), dg10, dg11, rows(dg12), rows(dg13)],
                             axis=0).reshape(2, 4, D_MODEL)
    small = {"gains": dgains, "pool_scale": dpscale, "gn": dgn,
             "conv_w": jnp.sum(jnp.stack([dcw0, dcw1]), axis=3),
             "conv_b": jnp.sum(jnp.stack([dcb0, dcb1]), axis=2, keepdims=True), "pool_w": dpw}
    return loss, dx0, small


def kernel(x, positions, norm_gain, pool_w, pool_scale, ret_w_in, ret_gn_gain, ret_w_out, mlp_w_up, mlp_conv_w, mlp_conv_b, mlp_w_down, loss_target, m_norm_gain, m_pool_w, m_pool_scale, m_ret_w_in, m_ret_gn_gain, m_ret_w_out, m_mlp_w_up, m_mlp_conv_w, m_mlp_conv_b, m_mlp_w_down, v_norm_gain, v_pool_w, v_pool_scale, v_ret_w_in, v_ret_gn_gain, v_ret_w_out, v_mlp_w_up, v_mlp_conv_w, v_mlp_conv_b, v_mlp_w_down):
    t = x.shape[1]
    me = 2 * lax.axis_index("x") + lax.axis_index("y")
    me_slot = jnp.reshape(me, (1,)).astype(jnp.int32)

    small_parts = [norm_gain, ret_gn_gain, mlp_conv_w, pool_w]
    small_sizes = [p.size for p in small_parts]
    small_rows = -(-sum(small_sizes) // (128 * 8)) * 8
    groups = {"small": [_pack_rows(small_parts, small_rows)],
              "mlp0": [mlp_w_up[0].astype(BF16), mlp_w_down[0].astype(BF16)],
              "ret": [ret_w_in[0].astype(BF16), ret_w_out[0].astype(BF16)],
              "mlp1": [mlp_w_up[1].astype(BF16), mlp_w_down[1].astype(BF16)]}
    gathers, token = {}, None
    for group, srcs in groups.items():
        lands = [lax.dynamic_update_index_in_dim(lax.empty((N_SHARD,) + s.shape, s.dtype), s, me, 0) for s in srcs]
        mode = "gather_half" if group == "mlp0" else "gather"
        gathers[group] = (mode, exchange_start(mode, srcs, lands, "gather_start_" + group, after=token))
        token = gathers[group][1][-1]

    def weights(group, after):
        mode, started = gathers[group]
        lands = exchange_wait(mode, started, after, "gather_wait_" + group)
        if mode == "gather_half":
            forward = exchange_start("forward_half", [], lands, "forward_start_" + group)
            lands = exchange_wait("forward_half", forward, forward[-1], "forward_wait_" + group)
        return lands

    sent, early = {}, {}

    def reduced(group, after, names):
        started, own = sent[group]
        recv = exchange_wait("scatter", started, after, "scatter_wait_" + group)
        return [plane_sum(me_slot, f, r, "plane_sum_" + nm)
                for f, r, nm in zip(own, recv, names)]

    def swap_start(planes, name):
        return exchange_start("swap", planes, [lax.empty(p.shape, p.dtype) for p in planes], name)

    def send_grads(group, pairs):
        lands = [lax.empty((3,) + b.shape[1:], BF16) for _, b in pairs]
        sent[group] = (exchange_start("scatter", [b for _, b in pairs], lands, "scatter_start_" + group),
                       [f for f, _ in pairs])
        token = sent[group][0][-1]
        if group == "down0":
            marker = pairs[0][1]
            early["planes"] = (reduced("mlp1", marker, ["w_up_1", "w_down_1"])
                               + reduced("ret", marker, ["w_in", "w_out"]))
            early["swap"] = swap_start(early["planes"], "swap_start_a")
            token = token + early["swap"][-1]
        return token

    (smallg,) = weights("small", token)
    smallg = smallg.reshape(N_SHARD, -1)
    offs = np.cumsum([0] + small_sizes)
    piece = lambda i, shape: smallg[:, offs[i]:offs[i + 1]].reshape((N_SHARD,) + shape)
    gains = piece(0, (2, 4, 256)).transpose(1, 2, 0, 3).reshape(2, 4, D_MODEL)
    gn_full = piece(1, (512,)).reshape(1, RET_HEADS * RET_V)
    cw_full = piece(2, (2, 3, FF_CHUNK)).transpose(1, 0, 2, 3)
    pw_full = piece(3, (4, 64, 256)).transpose(1, 0, 2, 3).reshape(4, 256, 256).astype(BF16)
    cb_full = mlp_conv_b.reshape(2, N_SHARD, 1, FF_CHUNK)

    loss, dx0, small = _local_step(
        x[0], positions.reshape(t, 1).astype(F32), loss_target[0], gains, pw_full, pool_scale, gn_full,
        cw_full, cb_full, weights, send_grads)

    def small_adamw(w, m, v, grads, name):
        w3 = w.reshape(1, -1, w.shape[-1])
        out = adamw(w3, m.reshape(w3.shape), v.reshape(w3.shape), [g.reshape(w3.shape[1:]) for g in grads], 0, None, name)
        return [o.reshape(w.shape) for o in out]

    res = {}
    planes_a = early["planes"]
    others_a = exchange_wait("swap", early["swap"], dx0, "swap_wait_a")
    res["ret_w_in"] = adamw(ret_w_in, m_ret_w_in, v_ret_w_in, (planes_a[2], others_a[2]), 0, None, "adamw_w_in")
    res["ret_w_out"] = adamw(ret_w_out, m_ret_w_out, v_ret_w_out, (planes_a[3], others_a[3]), 0, None, "adamw_w_out")
    up1 = adamw(mlp_w_up, m_mlp_w_up, v_mlp_w_up, (planes_a[0], others_a[0]), 1, None, "adamw_w_up_1")
    dn1 = adamw(mlp_w_down, m_mlp_w_down, v_mlp_w_down, (planes_a[1], others_a[1]), 1, None, "adamw_w_down_1")

    pw_f = small["pool_w"].reshape(4, N_SHARD, 64, 256).transpose(1, 0, 2, 3).reshape(N_SHARD, 256, 256)
    small_order = ["gains", "pool_scale", "gn", "conv_w", "conv_b"]
    gsmall_sizes = [small[k].size for k in small_order]
    gsmall_rows = -(-sum(gsmall_sizes) // (128 * 8)) * 8
    planes_b = (reduced("up0a", dn1[0], ["w_up_0a"]) + reduced("up0b", dn1[0], ["w_up_0b"])
                + reduced("down0", dn1[0], ["w_down_0"]))
    pw_recv, small_recv = scatter_grads([pw_f.astype(BF16)], _pack_rows([small[k] for k in small_order], gsmall_rows))
    planes_b.append(plane_sum(me_slot, pw_f, pw_recv, "plane_sum_pool_w"))
    swap_b = swap_start(planes_b, "swap_start_b")

    gsmall = sum_slots(small_recv, "sum_small").reshape(-1)
    goffs = np.cumsum([0] + gsmall_sizes)
    gpiece = lambda i: gsmall[goffs[i]:goffs[i + 1]].reshape(small[small_order[i]].shape)
    g_gains = lax.dynamic_slice_in_dim(gpiece(0), me * 256, 256, axis=2)
    g_gn = lax.dynamic_slice_in_dim(gpiece(2), me * RET_V, RET_V, axis=1)
    g_cw = lax.dynamic_index_in_dim(gpiece(3), me, 1, keepdims=False)
    res["norm_gain"] = small_adamw(norm_gain, m_norm_gain, v_norm_gain, [g_gains], "adamw_norm_gain")
    res["pool_scale"] = small_adamw(pool_scale, m_pool_scale, v_pool_scale, [gpiece(1)], "adamw_pool_scale")
    res["ret_gn_gain"] = small_adamw(ret_gn_gain, m_ret_gn_gain, v_ret_gn_gain, [g_gn], "adamw_gn_gain")
    res["mlp_conv_w"] = small_adamw(mlp_conv_w, m_mlp_conv_w, v_mlp_conv_w, [g_cw], "adamw_conv_w")
    res["mlp_conv_b"] = small_adamw(mlp_conv_b, m_mlp_conv_b, v_mlp_conv_b, [gpiece(4)], "adamw_conv_b")

    others_b = exchange_wait("swap", swap_b, res["mlp_conv_b"][0], "swap_wait_b")
    up0a = adamw(mlp_w_up, m_mlp_w_up, v_mlp_w_up, (planes_b[0], others_b[0]), 0, up1, "adamw_w_up_0a")
    res["mlp_w_up"] = adamw(mlp_w_up, m_mlp_w_up, v_mlp_w_up, (planes_b[1], others_b[1]), 0, up0a, "adamw_w_up_0b",
                            row0=D_MODEL // 2)
    res["mlp_w_down"] = adamw(mlp_w_down, m_mlp_w_down, v_mlp_w_down, (planes_b[2], others_b[2]), 0, dn1,
                              "adamw_w_down_0")
    res["pool_w"] = small_adamw(pool_w, m_pool_w, v_pool_w, (planes_b[3], others_b[3]), "adamw_pool_w")

    order = ["norm_gain", "pool_w", "pool_scale", "ret_w_in", "ret_gn_gain", "ret_w_out", "mlp_w_up", "mlp_conv_w",
             "mlp_conv_b", "mlp_w_down"]
    total_loss = lax.psum(loss[0, 0], ("x", "y", "c"))
    outs = [total_loss, dx0.reshape(x.shape)]
    for q in range(4):
        outs += [res[k][q] for k in order]
    return tuple(outs)
```

```python
import numpy as np
import jax
import jax.numpy as jnp
from jax import lax
from jax.experimental import pallas as pl
from jax.experimental.pallas import tpu as pltpu

F32 = jnp.float32
BF16 = jnp.bfloat16

D_MODEL = 1024
D_FF = 2816
FF_CHUNK = 1408
N_SHARD = 4
POOL_WINDOWS = (2, 4, 8, 16)
POOL_DIM = 256
POOL_HALO = 16
RET_HEADS = 4
RET_QK = 256
RET_V = 512
RET_CHUNK = 256
RET_IN = 6144
RET_IN_SHARD = 1536
ROPE_BASE = 10000.0
EPS = 1e-6
CONV_HALO = 8

ADAM_LR, ADAM_B1, ADAM_B2, ADAM_EPS, ADAM_WD, ADAM_STEP = 0.001, 0.9, 0.999, 1e-08, 0.01, 10

VMEM_LIMIT = 56 * 1024 * 1024
MESH = pl.DeviceIdType.MESH
ANY = pl.BlockSpec(memory_space=pl.ANY)


def _params(n_grid=1, limit=VMEM_LIMIT):
    return pltpu.CompilerParams(dimension_semantics=("arbitrary",) * n_grid, vmem_limit_bytes=limit)


def _dot(a, b):
    return jnp.dot(a, b, preferred_element_type=F32)


def _dot_nt(a, b):
    return lax.dot_general(a, b, (((1,), (1,)), ((), ())), preferred_element_type=F32)


def _dot_tn(a, b):
    return lax.dot_general(a, b, (((0,), (0,)), ((), ())), preferred_element_type=F32)


def _rms_fwd(x, gain):
    r = lax.rsqrt(jnp.mean(x * x, axis=-1, keepdims=True) + EPS)
    xh = x * r
    return xh * gain, xh, r


def _rms_bwd(xh, r, gain, dy):
    dxh = dy * gain
    return r * (dxh - xh * jnp.mean(dxh * xh, axis=-1, keepdims=True))


def _colsum(v):
    return jnp.sum(v, axis=0, keepdims=True)


def _full(shape):
    nd = len(shape)
    return pl.BlockSpec(shape, lambda *_: (0,) * nd)


def rope_tables(pos_col, inv_freq, after):
    t = pos_col.shape[0]
    tm = min(t, 1024)

    def body(p_ref, f_ref, after_ref, c_ref, s_ref):
        ang = p_ref[...] * f_ref[...]
        c_ref[...] = jnp.cos(ang)
        s_ref[...] = jnp.sin(ang)

    return pl.pallas_call(
        body, grid=(t // tm,),
        in_specs=[pl.BlockSpec((tm, 1), lambda i: (i, 0)), _full((1, 128)), ANY],
        out_specs=[pl.BlockSpec((tm, 128), lambda i: (i, 0))] * 2,
        out_shape=[jax.ShapeDtypeStruct((t, 128), F32)] * 2,
        compiler_params=_params(1), name="rope_tables")(pos_col, inv_freq, after)


def _window_sums(ext, backward):
    n = ext.shape[0]
    cur, sums = ext, []
    for g, win in enumerate(POOL_WINDOWS):
        if g > 0:
            cur = cur[:, POOL_DIM:]
        half = win // 2
        cur = cur + pltpu.roll(cur, n - half if backward else half, axis=0)
        sums.append(cur[:, 0:POOL_DIM])
    return sums


def _pool_diff(h_halo, h, row0, tm):
    t_idx = row0 + lax.broadcasted_iota(jnp.int32, (tm, 1), 0)
    sums = _window_sums(jnp.concatenate([h_halo, h], axis=0), backward=False)
    parts, inv_counts = [], []
    for g, win in enumerate(POOL_WINDOWS):
        inv = 1.0 / jnp.minimum(t_idx + 1, win).astype(F32)
        parts.append(sums[g][POOL_HALO:, :] * inv - h[:, g * POOL_DIM:(g + 1) * POOL_DIM])
        inv_counts.append(inv)
    return parts, inv_counts


def pool_fwd(x, g_pre, g_post, pool_w, pool_scale, tm=512):
    t, d = x.shape
    nt = t // tm

    def body(x_ref, g0_ref, g1_ref, w_ref, sc_ref, o_ref, hext):
        i = pl.program_id(0)

        @pl.when(i == 0)
        def _():
            hext[...] = jnp.zeros((POOL_HALO, d), F32)

        xv = x_ref[...]
        h, _, _ = _rms_fwd(xv, g0_ref[...])
        parts, _ = _pool_diff(hext[...], h, i * tm, tm)
        hext[...] = h[tm - POOL_HALO:tm, :]
        ys = [_dot(parts[g].astype(BF16), w_ref[g]) for g in range(len(POOL_WINDOWS))]
        y = jnp.concatenate(ys, axis=-1) * sc_ref[...]
        m, _, _ = _rms_fwd(y, g1_ref[...])
        o_ref[...] = xv + m

    row = pl.BlockSpec((tm, d), lambda i: (i, 0))
    return pl.pallas_call(
        body, grid=(nt,),
        in_specs=[row, _full((1, d)), _full((1, d)), _full(pool_w.shape), _full((1, d))],
        out_specs=row, out_shape=jax.ShapeDtypeStruct((t, d), F32),
        scratch_shapes=[pltpu.VMEM((POOL_HALO, d), F32)],
        compiler_params=_params(1), name="pool_fwd")(x, g_pre, g_post, pool_w, pool_scale)


def pool_bwd(dx1, x, g_pre, g_post, pool_w, pool_scale, tm=512):
    t, d = x.shape
    nt = t // tm
    ng = len(POOL_WINDOWS)

    def body(dx1_ref, x_ref, xh_ref, g0_ref, g1_ref, w_ref, sc_ref,
             dx_ref, dg0_ref, dg1_ref, dsc_ref, dw_ref, enext):
        i = pl.program_id(0)
        r = nt - 1 - i

        @pl.when(i == 0)
        def _():
            enext[...] = jnp.zeros((POOL_HALO, d), F32)
            dg0_ref[...] = jnp.zeros_like(dg0_ref)
            dg1_ref[...] = jnp.zeros_like(dg1_ref)
            dsc_ref[...] = jnp.zeros_like(dsc_ref)
            dw_ref[...] = jnp.zeros_like(dw_ref)

        g0 = g0_ref[...]
        g1 = g1_ref[...]
        sc = sc_ref[...]
        xv = x_ref[...]
        h, xh, rx = _rms_fwd(xv, g0)
        h_halo, _, _ = _rms_fwd(xh_ref[...], g0)
        parts, inv_counts = _pool_diff(h_halo * jnp.where(r > 0, 1.0, 0.0), h, r * tm, tm)
        parts_b = [p.astype(BF16) for p in parts]
        ypre = jnp.concatenate([_dot(parts_b[g], w_ref[g]) for g in range(ng)], axis=-1)
        _, yh, ry = _rms_fwd(ypre * sc, g1)
        dm = dx1_ref[...]
        dg1_ref[...] += _colsum(dm * yh)
        dy = _rms_bwd(yh, ry, g1, dm)
        dsc_ref[...] += _colsum(dy * ypre)
        dyp = (dy * sc).astype(BF16)
        ddiffs = []
        for g in range(ng):
            cols = slice(g * POOL_DIM, (g + 1) * POOL_DIM)
            dw_ref[g] += _dot_tn(parts_b[g], dyp[:, cols])
            ddiffs.append(_dot_nt(dyp[:, cols], w_ref[g]))
        e = jnp.concatenate([ddiffs[g] * inv_counts[g] for g in range(ng)], axis=-1)
        sums = _window_sums(jnp.concatenate([e, enext[...]], axis=0), backward=True)
        enext[...] = e[0:POOL_HALO, :]
        dh = jnp.concatenate([sums[g][0:tm, :] - ddiffs[g] for g in range(ng)], axis=-1)
        dg0_ref[...] += _colsum(dh * xh)
        dx_ref[...] = dm + _rms_bwd(xh, rx, g0, dh)

    row = pl.BlockSpec((tm, d), lambda i: (nt - 1 - i, 0))
    halo = pl.BlockSpec((POOL_HALO, d), lambda i: (jnp.maximum((nt - 1 - i) * (tm // POOL_HALO) - 1, 0), 0))
    vec = _full((1, d))
    return pl.pallas_call(
        body, grid=(nt,),
        in_specs=[row, row, halo, vec, vec, _full(pool_w.shape), vec],
        out_specs=[row, vec, vec, vec, _full((ng, POOL_DIM, POOL_DIM))],
        out_shape=[jax.ShapeDtypeStruct((t, d), F32)] + [jax.ShapeDtypeStruct((1, d), F32)] * 3
        + [jax.ShapeDtypeStruct((ng, POOL_DIM, POOL_DIM), F32)],
        scratch_shapes=[pltpu.VMEM((POOL_HALO, d), F32)],
        compiler_params=_params(1), name="pool_bwd")(dx1, x, x, g_pre, g_post, pool_w, pool_scale)


def _conv_taps(cw_ref, j):
    return cw_ref[j, 0:1, :], cw_ref[j, 1:2, :], cw_ref[j, 2:3, :]


def _row_block(m, target=256):
    if m <= target:
        return m
    for b in range(target, 7, -8):
        if m % b == 0:
            return b
    return m


def mlp_fwd(x, g_pre, g_post, w_up, w_down, conv_w, conv_b, target=None, tm=256):
    t, d = x.shape
    nt = t // tm
    h8 = CONV_HALO
    with_loss = target is not None
    n_extra = 1 if with_loss else 0

    def body(x_ref, g2_ref, g3_ref, wup_hbm, wdn_hbm, cw_ref, cb_ref, *rest):
        tgt_ref = rest[0] if with_loss else None
        xo_ref, u_ref, s_ref, a_ref, f_ref, h_ref = rest[n_extra:n_extra + 6]
        loss_ref = rest[n_extra + 6] if with_loss else None
        wup_v, wdn_v, tail, sem = rest[-4:]
        i = pl.program_id(0)

        @pl.when(i == 0)
        def _():
            c1 = pltpu.make_async_copy(wup_hbm, wup_v, sem.at[0])
            c2 = pltpu.make_async_copy(wdn_hbm, wdn_v, sem.at[1])
            c1.start()
            c2.start()
            tail[...] = jnp.zeros_like(tail)
            if with_loss:
                loss_ref[...] = jnp.zeros_like(loss_ref)
            c1.wait()
            c2.wait()

        xv = x_ref[...]
        h, _, _ = _rms_fwd(xv, g2_ref[...])
        hb = h.astype(BF16)
        h_ref[...] = hb
        acc = jnp.zeros((tm, d), F32)
        for k in range(2):
            cs = []
            for s in range(2):
                j, cols = k + 2 * s, slice((2 * k + s) * FF_CHUNK, (2 * k + s + 1) * FF_CHUNK)
                uf = _dot(hb, wup_v[j])
                u_ref[:, cols] = uf.astype(BF16)
                ext = jnp.concatenate([tail[j], uf], axis=0)
                tail[j] = uf[tm - h8:tm, :]
                w0, w1, w2 = _conv_taps(cw_ref, j)
                cs.append(cb_ref[j] + w2 * uf + w1 * pltpu.roll(ext, 1, axis=0)[h8:, :]
                          + w0 * pltpu.roll(ext, 2, axis=0)[h8:, :])
            cg, cv = cs
            sg = jax.nn.sigmoid(cg)
            sil = cg * sg
            ab = (sil * cv).astype(BF16)
            a_ref[:, k * FF_CHUNK:(k + 1) * FF_CHUNK] = ab
            s_ref[:, 2 * k * FF_CHUNK:(2 * k + 1) * FF_CHUNK] = sil.astype(BF16)
            s_ref[:, (2 * k + 1) * FF_CHUNK:(2 * k + 2) * FF_CHUNK] = (cv * (sg * (1.0 + cg * (1.0 - sg)))).astype(BF16)
            acc = acc + _dot(ab, wdn_v[k * FF_CHUNK:(k + 1) * FF_CHUNK, :])
        f_ref[...] = acc
        y, _, _ = _rms_fwd(acc, g3_ref[...])
        if with_loss:
            err = (xv + y) - tgt_ref[...]
            xo_ref[...] = err * (1.0 / d)
            loss_ref[...] += 0.5 * jnp.sum(jnp.mean(err * err, axis=-1, keepdims=True), axis=0, keepdims=True)
        else:
            xo_ref[...] = xv + y

    row = pl.BlockSpec((tm, d), lambda i: (i, 0))
    wide = pl.BlockSpec((tm, 2 * D_FF), lambda i: (i, 0))
    vec = _full((1, d))
    extra = [target] if with_loss else []
    return pl.pallas_call(
        body, grid=(nt,),
        in_specs=[row, vec, vec, ANY, ANY, _full(conv_w.shape), _full(conv_b.shape)] + [row] * n_extra,
        out_specs=[row, wide, wide, pl.BlockSpec((tm, D_FF), lambda i: (i, 0)), row, row] + [_full((1, 1))] * n_extra,
        out_shape=[jax.ShapeDtypeStruct((t, d), F32), jax.ShapeDtypeStruct((t, 2 * D_FF), BF16),
                   jax.ShapeDtypeStruct((t, 2 * D_FF), BF16), jax.ShapeDtypeStruct((t, D_FF), BF16),
                   jax.ShapeDtypeStruct((t, d), F32), jax.ShapeDtypeStruct((t, d), BF16)]
        + [jax.ShapeDtypeStruct((1, 1), F32)] * n_extra,
        scratch_shapes=[pltpu.VMEM(w_up.shape, BF16), pltpu.VMEM(w_down.shape, BF16),
                        pltpu.VMEM((N_SHARD, h8, FF_CHUNK), F32), pltpu.SemaphoreType.DMA((2,))],
        compiler_params=_params(1), name="mlp_fwd_loss" if with_loss else "mlp_fwd")(
            x, g_pre, g_post, w_up, w_down, conv_w, conv_b, *extra)


def _rowsum8(v):
    return jnp.sum(v.reshape(v.shape[0] // 8, 8, v.shape[1]), axis=0)


def mlp_bwd(dxo, f, x, u, sp, g_pre, g_post, w_up, w_down, conv_w, tm=256):
    t, d = x.shape
    nt = t // tm
    h8 = CONV_HALO

    def body(dxo_ref, f_ref, x_ref, u_ref, s_ref, g2_ref, g3_ref, wup_hbm, wdn_hbm, cw_ref,
             dx_ref, du_ref, df_ref, dg2_ref, dg3_ref, dcw_ref, dcb_ref,
             wup_v, wdn_v, dfb_s, dh_s, carry, sem):
        i = pl.program_id(0)
        k = pl.program_id(1)

        @pl.when((i == 0) & (k == 0))
        def _():
            c1 = pltpu.make_async_copy(wup_hbm, wup_v, sem.at[0])
            c2 = pltpu.make_async_copy(wdn_hbm, wdn_v, sem.at[1])
            c1.start()
            c2.start()
            carry[...] = jnp.zeros_like(carry)
            dg2_ref[...] = jnp.zeros_like(dg2_ref)
            dg3_ref[...] = jnp.zeros_like(dg3_ref)
            dcw_ref[...] = jnp.zeros_like(dcw_ref)
            dcb_ref[...] = jnp.zeros_like(dcb_ref)
            c1.wait()
            c2.wait()

        @pl.when(k == 0)
        def _():
            g3 = g3_ref[...]
            dxo = dxo_ref[...]
            _, fh, rf = _rms_fwd(f_ref[...], g3)
            dg3_ref[...] += _rowsum8(dxo * fh)
            dfb = _rms_bwd(fh, rf, g3, dxo).astype(BF16)
            df_ref[...] = dfb
            dfb_s[...] = dfb
            dh_s[...] = jnp.zeros_like(dh_s)

        da = _dot_nt(dfb_s[...], wdn_v[pl.ds(pl.multiple_of(k * FF_CHUNK, 128), FF_CHUNK), :])
        dh = dh_s[...]
        for s in range(2):
            j = k + 2 * s
            cols = slice(s * FF_CHUNK, (s + 1) * FF_CHUNK)
            dc = da * s_ref[:, (1 - s) * FF_CHUNK:(2 - s) * FF_CHUNK].astype(F32)
            uf = u_ref[:, cols].astype(F32)
            ext = jnp.concatenate([dc, carry[j]], axis=0)
            carry[j] = dc[0:h8, :]
            dc1 = pltpu.roll(ext, tm + h8 - 1, axis=0)[0:tm, :]
            dc2 = pltpu.roll(ext, tm + h8 - 2, axis=0)[0:tm, :]
            dcb_ref[j] += _rowsum8(dc)
            dcw_ref[j, 2] += _rowsum8(dc * uf)
            dcw_ref[j, 1] += _rowsum8(dc1 * uf)
            dcw_ref[j, 0] += _rowsum8(dc2 * uf)
            dub = (cw_ref[j, 2:3, :] * dc + cw_ref[j, 1:2, :] * dc1 + cw_ref[j, 0:1, :] * dc2).astype(BF16)
            du_ref[:, cols] = dub
            dh = dh + _dot_nt(dub, wup_v[j])
        dh_s[...] = dh

        @pl.when(k == 1)
        def _():
            g2 = g2_ref[...]
            _, xh, rx = _rms_fwd(x_ref[...], g2)
            dg2_ref[...] += _rowsum8(dh * xh)
            dx_ref[...] = dxo_ref[...] + _rms_bwd(xh, rx, g2, dh)

    row = pl.BlockSpec((tm, d), lambda i, k: (nt - 1 - i, 0))
    half = pl.BlockSpec((tm, 2 * FF_CHUNK), lambda i, k: (nt - 1 - i, k))
    vec = _full((1, d))
    acc = _full((8, d))
    dcw_shape, dcb_shape = (N_SHARD, 3, 8, FF_CHUNK), (N_SHARD, 8, FF_CHUNK)
    return pl.pallas_call(
        body, grid=(nt, 2),
        in_specs=[row, row, row, half, half, vec, vec, ANY, ANY, _full(conv_w.shape)],
        out_specs=[row, half, row, acc, acc, _full(dcw_shape), _full(dcb_shape)],
        out_shape=[jax.ShapeDtypeStruct((t, d), F32), jax.ShapeDtypeStruct((t, 2 * D_FF), BF16),
                   jax.ShapeDtypeStruct((t, d), BF16),
                   jax.ShapeDtypeStruct((8, d), F32), jax.ShapeDtypeStruct((8, d), F32),
                   jax.ShapeDtypeStruct(dcw_shape, F32), jax.ShapeDtypeStruct(dcb_shape, F32)],
        scratch_shapes=[pltpu.VMEM(w_up.shape, BF16), pltpu.VMEM(w_down.shape, BF16),
                        pltpu.VMEM((tm, d), BF16), pltpu.VMEM((tm, d), F32),
                        pltpu.VMEM((N_SHARD, h8, FF_CHUNK), F32), pltpu.SemaphoreType.DMA((2,))],
        compiler_params=_params(2), name="mlp_bwd")(dxo, f, x, u, sp, g_pre, g_post, w_up, w_down, conv_w)


def grad_matmul(a, b, bm, bn, name, tk=2048, interleaved=False, after=None, cols=None):
    t = a.shape[0]
    m0, m = (0, a.shape[1]) if cols is None else cols
    n = b.shape[1]
    tk = min(tk, t)
    nk = t // tk
    place = (lambda j: (j % 2) * 2 + j // 2) if interleaved else (lambda j: j)
    extra = [] if after is None else [after]
    first = m0 // bm

    def body(a_ref, b_ref, *rest):
        o_ref, ob_ref = rest[len(extra):]
        kk = pl.program_id(2)

        @pl.when(kk == 0)
        def _():
            o_ref[...] = jnp.zeros_like(o_ref)

        o_ref[...] += _dot_tn(a_ref[...], b_ref[...])

        @pl.when(kk == nk - 1)
        def _():
            ob_ref[...] = o_ref[...].astype(BF16)

    ospec = pl.BlockSpec((None, bm, bn), lambda j, i, kk: (place(j), i, 0))
    return pl.pallas_call(
        body, grid=(n // bn, m // bm, nk),
        in_specs=[pl.BlockSpec((tk, bm), lambda j, i, kk: (kk, first + i)),
                  pl.BlockSpec((tk, bn), lambda j, i, kk: (kk, j))]
        + [ANY] * len(extra),
        out_specs=[ospec, ospec],
        out_shape=[jax.ShapeDtypeStruct((n // bn, m, bn), F32), jax.ShapeDtypeStruct((n // bn, m, bn), BF16)],
        compiler_params=_params(3), name=name)(a, b, *extra)


def _decay_tables():
    log_gamma = jnp.log(1.0 - 2.0 ** (-5.0 - jnp.arange(RET_HEADS, dtype=F32)))
    i = jnp.arange(RET_CHUNK, dtype=F32)
    rel = i[:, None] - i[None, :]
    intra = jnp.where(rel >= 0, jnp.exp(jnp.maximum(rel, 0.0) * log_gamma[:, None, None]), 0.0)
    cross = jnp.exp((i + 1.0) * log_gamma[:, None])[:, :, None]
    inner = jnp.exp((RET_CHUNK - 1.0 - i) * log_gamma[:, None])[:, :, None]
    chunk = [float(np.exp(np.float32(RET_CHUNK) * np.log(np.float32(1.0 - 2.0 ** (-5.0 - h))).astype(np.float32)))
             for h in range(RET_HEADS)]
    return intra, cross, inner, chunk


def ret_proj(x, g_pre, w_in, cos, sin, tm=512):
    t, d = x.shape
    nt = t // tm
    per = RET_IN_SHARD // RET_QK

    def body(x_ref, g_ref, win_hbm, c_ref, s_ref, pj_ref, h_ref, win_v, sem):
        @pl.when(pl.program_id(0) == 0)
        def _():
            cp = pltpu.make_async_copy(win_hbm, win_v, sem)
            cp.start()
            cp.wait()

        h, _, _ = _rms_fwd(x_ref[...], g_ref[...])
        hb = h.astype(BF16)
        h_ref[...] = hb
        c = c_ref[...]
        s = s_ref[...]
        for j in range(N_SHARD):
            pjj = _dot(hb, win_v[j])
            for bb in range(per):
                b = per * j + bb
                blk = pjj[:, bb * RET_QK:(bb + 1) * RET_QK]
                if b < 2 * RET_HEADS:
                    x1, x2 = blk[:, :128], blk[:, 128:]
                    o1 = x1 * c - x2 * s
                    o2 = x2 * c + x1 * s
                    if b < RET_HEADS:
                        o1 = o1 * (RET_QK ** -0.5)
                        o2 = o2 * (RET_QK ** -0.5)
                    pj_ref[:, b * RET_QK:b * RET_QK + 128] = o1.astype(BF16)
                    pj_ref[:, b * RET_QK + 128:(b + 1) * RET_QK] = o2.astype(BF16)
                else:
                    pj_ref[:, b * RET_QK:(b + 1) * RET_QK] = blk.astype(BF16)

    row = pl.BlockSpec((tm, d), lambda i: (i, 0))
    tab = pl.BlockSpec((tm, 128), lambda i: (i, 0))
    return pl.pallas_call(
        body, grid=(nt,),
        in_specs=[row, _full((1, d)), ANY, tab, tab],
        out_specs=[pl.BlockSpec((tm, RET_IN), lambda i: (i, 0)), row],
        out_shape=[jax.ShapeDtypeStruct((t, RET_IN), BF16), jax.ShapeDtypeStruct((t, d), BF16)],
        scratch_shapes=[pltpu.VMEM(w_in.shape, BF16), pltpu.SemaphoreType.DMA],
        compiler_params=_params(1), name="ret_proj")(x, g_pre, w_in, cos, sin)


def ret_core_fwd(pj, intra, cross, inner, chunk_decay):
    t = pj.shape[0]
    nc = t // RET_CHUNK
    c = RET_CHUNK
    qk_all = RET_HEADS * RET_QK
    v_all = RET_HEADS * RET_V

    def body(q_ref, k_ref, v_ref, dm_ref, cr_ref, in_ref, o_ref, sp_ref, state):
        @pl.when(pl.program_id(0) == 0)
        def _():
            state[...] = jnp.zeros_like(state)

        for h in range(RET_HEADS):
            q = q_ref[:, h * RET_QK:(h + 1) * RET_QK]
            k = k_ref[:, h * RET_QK:(h + 1) * RET_QK]
            v = v_ref[:, h * RET_V:(h + 1) * RET_V]
            sb = state[h].astype(BF16)
            sp_ref[h] = sb
            sc = _dot_nt(q, k) * dm_ref[h]
            o_ref[:, h * RET_V:(h + 1) * RET_V] = _dot(sc.astype(BF16), v) + _dot(q, sb) * cr_ref[h]
            kd = (k.astype(F32) * in_ref[h]).astype(BF16)
            state[h] = state[h] * chunk_decay[h] + _dot_tn(kd, v)

    return pl.pallas_call(
        body, grid=(nc,),
        in_specs=[pl.BlockSpec((c, qk_all), lambda n: (n, 0)), pl.BlockSpec((c, qk_all), lambda n: (n, 1)),
                  pl.BlockSpec((c, v_all), lambda n: (n, 1)),
                  _full(intra.shape), _full(cross.shape), _full(inner.shape)],
        out_specs=[pl.BlockSpec((c, v_all), lambda n: (n, 0)),
                   pl.BlockSpec((None, RET_HEADS, RET_QK, RET_V), lambda n: (n, 0, 0, 0))],
        out_shape=[jax.ShapeDtypeStruct((t, v_all), F32),
                   jax.ShapeDtypeStruct((nc, RET_HEADS, RET_QK, RET_V), BF16)],
        scratch_shapes=[pltpu.VMEM((RET_HEADS, RET_QK, RET_V), F32)],
        compiler_params=_params(1), name="ret_core_fwd")(pj, pj, pj, intra, cross, inner)


def _group_norm(o_h):
    mu = jnp.mean(o_h, axis=-1, keepdims=True)
    dev = o_h - mu
    rstd = lax.rsqrt(jnp.mean(dev * dev, axis=-1, keepdims=True) + EPS)
    return dev * rstd, rstd


def ret_out_fwd(o, pj, x, gn_gain, g_post, w_out, tm=512):
    t, d = x.shape
    nt = t // tm
    v_all = RET_HEADS * RET_V

    def body(o_ref, g_ref, x_ref, gn_ref, g1_ref, w_ref, xo_ref, y_ref, out_ref):
        for h in range(RET_HEADS):
            cols = slice(h * RET_V, (h + 1) * RET_V)
            ohat, _ = _group_norm(o_ref[:, cols])
            g = g_ref[:, cols].astype(F32)
            y_ref[:, cols] = (g * jax.nn.sigmoid(g) * (ohat * gn_ref[:, cols])).astype(BF16)
        out = _dot(y_ref[...], w_ref[...])
        out_ref[...] = out
        m, _, _ = _rms_fwd(out, g1_ref[...])
        xo_ref[...] = x_ref[...] + m

    row = pl.BlockSpec((tm, d), lambda i: (i, 0))
    wide = pl.BlockSpec((tm, v_all), lambda i: (i, 0))
    return pl.pallas_call(
        body, grid=(nt,),
        in_specs=[wide, pl.BlockSpec((tm, v_all), lambda i: (i, 2)), row, _full((1, v_all)), _full((1, d)),
                  _full(w_out.shape)],
        out_specs=[row, wide, row],
        out_shape=[jax.ShapeDtypeStruct((t, d), F32), jax.ShapeDtypeStruct((t, v_all), BF16),
                   jax.ShapeDtypeStruct((t, d), F32)],
        compiler_params=_params(1), name="ret_out_fwd")(o, pj, x, gn_gain, g_post, w_out)


def ret_out_bwd(dxo, out, o, pj, gn_gain, g_post, w_out, tm=256):
    t, d = out.shape
    nt = t // tm
    v_all = RET_HEADS * RET_V

    def body(dxo_ref, out_ref, o_ref, g_ref, gn_ref, g1_ref, w_ref,
             dout_ref, dgate_ref, do_ref, dg1_ref, dgn_ref):
        @pl.when(pl.program_id(0) == 0)
        def _():
            dg1_ref[...] = jnp.zeros_like(dg1_ref)
            dgn_ref[...] = jnp.zeros_like(dgn_ref)

        g1 = g1_ref[...]
        dxo = dxo_ref[...]
        _, oh_, r_ = _rms_fwd(out_ref[...], g1)
        dg1_ref[...] += _colsum(dxo * oh_)
        doutb = _rms_bwd(oh_, r_, g1, dxo).astype(BF16)
        dout_ref[...] = doutb
        dy = _dot_nt(doutb, w_ref[...])
        for h in range(RET_HEADS):
            cols = slice(h * RET_V, (h + 1) * RET_V)
            gn = gn_ref[:, cols]
            ohat, rstd = _group_norm(o_ref[:, cols])
            g = g_ref[:, cols].astype(F32)
            sg = jax.nn.sigmoid(g)
            dyh = dy[:, cols]
            dgate_ref[:, cols] = (dyh * (ohat * gn) * (sg * (1.0 + g * (1.0 - sg)))).astype(BF16)
            don = dyh * (g * sg)
            dgn_ref[:, cols] += _colsum(don * ohat)
            dohat = don * gn
            do_ref[:, cols] = (rstd * (dohat - jnp.mean(dohat, axis=-1, keepdims=True)
                                       - ohat * jnp.mean(dohat * ohat, axis=-1, keepdims=True))).astype(BF16)

    row = pl.BlockSpec((tm, d), lambda i: (i, 0))
    wide = pl.BlockSpec((tm, v_all), lambda i: (i, 0))
    gate = pl.BlockSpec((tm, v_all), lambda i: (i, 2))
    return pl.pallas_call(
        body, grid=(nt,),
        in_specs=[row, row, wide, gate, _full((1, v_all)), _full((1, d)), _full(w_out.shape)],
        out_specs=[row, gate, wide, _full((1, d)), _full((1, v_all))],
        out_shape=[jax.ShapeDtypeStruct((t, d), BF16), jax.ShapeDtypeStruct((t, RET_IN), BF16),
                   jax.ShapeDtypeStruct((t, v_all), BF16), jax.ShapeDtypeStruct((1, d), F32),
                   jax.ShapeDtypeStruct((1, v_all), F32)],
        compiler_params=_params(1), name="ret_out_bwd")(dxo, out, o, pj, gn_gain, g_post, w_out)


def ret_core_bwd(pj, do, sprev, cos, sin, dpj, intra, cross, inner, chunk_decay):
    t = pj.shape[0]
    nc = t // RET_CHUNK
    c = RET_CHUNK
    qk_all = RET_HEADS * RET_QK
    v_all = RET_HEADS * RET_V
    scale = RET_QK ** -0.5

    def body(q_ref, k_ref, v_ref, do_ref, sp_ref, c_ref, s_ref, dm_ref, cr_ref, in_ref, dpj_in, dpj_ref, dstate):
        @pl.when(pl.program_id(0) == 0)
        def _():
            dstate[...] = jnp.zeros_like(dstate)

        cs = c_ref[...]
        sn = s_ref[...]
        for h in range(RET_HEADS):
            q = q_ref[:, h * RET_QK:(h + 1) * RET_QK]
            k = k_ref[:, h * RET_QK:(h + 1) * RET_QK]
            v = v_ref[:, h * RET_V:(h + 1) * RET_V]
            doh = do_ref[:, h * RET_V:(h + 1) * RET_V]
            dm = dm_ref[h]
            ab = (_dot_nt(q, k) * dm).astype(BF16)
            dab = (_dot_nt(doh, v) * dm).astype(BF16)
            dsb = dstate[h].astype(BF16)
            kd = (k.astype(F32) * in_ref[h]).astype(BF16)
            dv = _dot_tn(ab, doh) + _dot(kd, dsb)
            dq = _dot(dab, k) + cr_ref[h] * _dot_nt(doh, sp_ref[h])
            dk = _dot_tn(dab, q) + in_ref[h] * _dot_nt(v, dsb)
            qd = (q.astype(F32) * cr_ref[h]).astype(BF16)
            dstate[h] = dstate[h] * chunk_decay[h] + _dot_tn(qd, doh)
            for base, dd, sc in ((h * RET_QK, dq, scale), (qk_all + h * RET_QK, dk, 1.0)):
                d1, d2 = dd[:, :128], dd[:, 128:]
                dpj_ref[:, base:base + 128] = ((d1 * cs + d2 * sn) * sc).astype(BF16)
                dpj_ref[:, base + 128:base + RET_QK] = ((d2 * cs - d1 * sn) * sc).astype(BF16)
            dpj_ref[:, 2 * qk_all + h * RET_V:2 * qk_all + (h + 1) * RET_V] = dv.astype(BF16)

    rev = lambda n: nc - 1 - n
    tab = pl.BlockSpec((c, 128), lambda n: (rev(n), 0))
    return pl.pallas_call(
        body, grid=(nc,),
        in_specs=[pl.BlockSpec((c, qk_all), lambda n: (rev(n), 0)), pl.BlockSpec((c, qk_all), lambda n: (rev(n), 1)),
                  pl.BlockSpec((c, v_all), lambda n: (rev(n), 1)), pl.BlockSpec((c, v_all), lambda n: (rev(n), 0)),
                  pl.BlockSpec((None, RET_HEADS, RET_QK, RET_V), lambda n: (rev(n), 0, 0, 0)),
                  tab, tab, _full(intra.shape), _full(cross.shape), _full(inner.shape), ANY],
        out_specs=pl.BlockSpec((c, 2 * qk_all + v_all), lambda n: (rev(n), 0)),
        out_shape=jax.ShapeDtypeStruct((t, RET_IN), BF16),
        scratch_shapes=[pltpu.VMEM((RET_HEADS, RET_QK, RET_V), F32)],
        input_output_aliases={10: 0},
        compiler_params=_params(1), name="ret_core_bwd")(pj, pj, pj, do, sprev, cos, sin, intra, cross, inner, dpj)


def ret_in_bwd(dpj, dres, x, g_pre, w_in, tm=512):
    t, d = x.shape
    nt = t // tm

    def body(dpj_ref, dres_ref, x_ref, g_ref, win_hbm, dx_ref, dg_ref, win_v, sem):
        @pl.when(pl.program_id(0) == 0)
        def _():
            cp = pltpu.make_async_copy(win_hbm, win_v, sem)
            cp.start()
            dg_ref[...] = jnp.zeros_like(dg_ref)
            cp.wait()

        g = g_ref[...]
        dh = jnp.zeros((tm, d), F32)
        for j in range(N_SHARD):
            dh = dh + _dot_nt(dpj_ref[:, j * RET_IN_SHARD:(j + 1) * RET_IN_SHARD], win_v[j])
        _, xh, rx = _rms_fwd(x_ref[...], g)
        dg_ref[...] += _colsum(dh * xh)
        dx_ref[...] = dres_ref[...] + _rms_bwd(xh, rx, g, dh)

    row = pl.BlockSpec((tm, d), lambda i: (i, 0))
    return pl.pallas_call(
        body, grid=(nt,),
        in_specs=[pl.BlockSpec((tm, RET_IN), lambda i: (i, 0)), row, row, _full((1, d)), ANY],
        out_specs=[row, _full((1, d))],
        out_shape=[jax.ShapeDtypeStruct((t, d), F32), jax.ShapeDtypeStruct((1, d), F32)],
        scratch_shapes=[pltpu.VMEM(w_in.shape, BF16), pltpu.SemaphoreType.DMA],
        compiler_params=_params(1), name="ret_in_bwd")(dpj, dres, x, g_pre, w_in)


_CHIP_FLIPS = ((1, 0), (0, 1), (1, 1))


def _flip(v, b):
    return 1 - v if b else v


def scatter_grads(big, small):
    n = len(big)

    def body(*refs):
        ins, small_in = refs[:n], refs[n]
        outs, small_out = refs[n + 1:2 * n + 1], refs[2 * n + 1]
        send_sems, recv_sems, ssend_sems, srecv_sems, local_sem = refs[2 * n + 2:]
        x, y, c = lax.axis_index("x"), lax.axis_index("y"), lax.axis_index("c")
        mine = 4 * x + 2 * y + c
        copies = [pltpu.make_async_copy(small_in, small_out.at[mine], local_sem)]
        copies[0].start()
        for m in range(1, 8):
            bx, by, bc = (m >> 2) & 1, (m >> 1) & 1, m & 1
            cp = pltpu.make_async_remote_copy(
                src_ref=small_in, dst_ref=small_out.at[mine], send_sem=ssend_sems.at[m - 1],
                recv_sem=srecv_sems.at[m - 1], device_id=(_flip(x, bx), _flip(y, by), _flip(c, bc)),
                device_id_type=MESH)
            cp.start()
            copies.append(cp)
        for t in range(n):
            for k, (bx, by) in enumerate(_CHIP_FLIPS):
                px, py = _flip(x, bx), _flip(y, by)
                cp = pltpu.make_async_remote_copy(
                    src_ref=ins[t].at[2 * px + py], dst_ref=outs[t].at[k], send_sem=send_sems.at[3 * t + k],
                    recv_sem=recv_sems.at[3 * t + k], device_id=(px, py, c), device_id_type=MESH)
                cp.start()
                copies.append(cp)
        for cp in copies:
            cp.wait()

    return pl.pallas_call(
        body, in_specs=[ANY] * (n + 1), out_specs=[ANY] * (n + 1),
        out_shape=[jax.ShapeDtypeStruct((3,) + b.shape[1:], b.dtype) for b in big]
        + [jax.ShapeDtypeStruct((8,) + small.shape, small.dtype)],
        scratch_shapes=[pltpu.SemaphoreType.DMA((3 * n,)), pltpu.SemaphoreType.DMA((3 * n,)),
                        pltpu.SemaphoreType.DMA((7,)), pltpu.SemaphoreType.DMA((7,)), pltpu.SemaphoreType.DMA],
        name="scatter_grads")(*big, small)


_HBM = pl.BlockSpec(memory_space=pltpu.HBM)
_SEM = pl.BlockSpec(memory_space=pltpu.SEMAPHORE)
_EFFECT = pltpu.SideEffectType.DATAFLOW_SIDE_EFFECTING


def _chip_copies(mode, srcs, lands, send_sems, recv_sems):
    x, y, c = lax.axis_index("x"), lax.axis_index("y"), lax.axis_index("c")
    copies = []
    for t in range(len(lands)):
        if mode == "swap":
            copies.append(pltpu.make_async_remote_copy(
                src_ref=srcs[t], dst_ref=lands[t], send_sem=send_sems.at[t], recv_sem=recv_sems.at[t],
                device_id=(x, y, 1 - c), device_id_type=MESH))
            continue
        for k, (bx, by) in enumerate(_CHIP_FLIPS):
            px, py = _flip(x, bx), _flip(y, by)
            target = (px, py, c)
            if mode == "gather":
                src, dst = srcs[t], lands[t].at[2 * x + y]
            elif mode == "gather_half":
                half = pl.ds(c * (srcs[t].shape[0] // 2), srcs[t].shape[0] // 2)
                src, dst = srcs[t].at[half], lands[t].at[2 * x + y, half]
            elif mode == "forward_half":
                half = pl.ds(c * (lands[t].shape[1] // 2), lands[t].shape[1] // 2)
                src = dst = lands[t].at[2 * px + py, half]
                target = (x, y, 1 - c)
            else:
                src, dst = srcs[t].at[2 * px + py], lands[t].at[k]
            copies.append(pltpu.make_async_remote_copy(
                src_ref=src, dst_ref=dst, send_sem=send_sems.at[3 * t + k], recv_sem=recv_sems.at[3 * t + k],
                device_id=target, device_id_type=MESH))
    return copies


def exchange_start(mode, srcs, lands, name, after=None):
    n, ns = len(lands), len(srcs)
    extra = [] if after is None else [after]

    def body(*refs):
        ins, lnd = refs[:ns], refs[ns:ns + n]
        send_sems, recv_sems = refs[ns + n + len(extra)], refs[ns + n + len(extra) + 1]
        token = refs[-1]
        for cp in _chip_copies(mode, ins, lnd, send_sems, recv_sems):
            cp.start()
        token[...] = jnp.zeros(token.shape, token.dtype)

    hbm = lambda a: pltpu.with_memory_space_constraint(a, pltpu.HBM)
    passed = list(srcs) + list(lands)
    n_sem = n if mode == "swap" else 3 * n
    return pl.pallas_call(
        body, name=name,
        out_shape=(pltpu.SemaphoreType.DMA((n_sem,)), pltpu.SemaphoreType.DMA((n_sem,)),
                   *[pltpu.HBM(a.shape, a.dtype) for a in passed], jax.ShapeDtypeStruct((8, 128), F32)),
        in_specs=[_HBM] * (ns + n) + [ANY] * len(extra),
        out_specs=(_SEM, _SEM, *[_HBM] * (ns + n), pl.BlockSpec(memory_space=pltpu.VMEM)),
        input_output_aliases={i: 2 + i for i in range(ns + n)},
        compiler_params=pltpu.CompilerParams(has_side_effects=_EFFECT))(*[hbm(a) for a in passed], *extra)


def exchange_wait(mode, started, after, name):
    send_sems, recv_sems = started[0], started[1]
    passed = list(started[2:-1])
    n = len(passed) if mode == "forward_half" else len(passed) // 2
    ns = len(passed) - n

    def body(*refs):
        ins, lnd = refs[:ns], refs[ns:ns + n]
        for cp in _chip_copies(mode, ins, lnd, refs[ns + n], refs[ns + n + 1]):
            cp.wait_send()
            cp.wait_recv()

    outs = pl.pallas_call(
        body, name=name, out_shape=tuple(pltpu.HBM(a.shape, a.dtype) for a in passed),
        in_specs=[_HBM] * (ns + n) + [_SEM, _SEM, ANY], out_specs=tuple([_HBM] * (ns + n)),
        input_output_aliases={i: i for i in range(ns + n)},
        compiler_params=pltpu.CompilerParams(has_side_effects=_EFFECT))(*passed, send_sems, recv_sems, after)
    return list(outs[ns:])


def plane_sum(slot, full, recv, name, bm=256):
    _, m, n = full.shape
    bm = _row_block(m, bm)

    def body(slot_ref, o_ref, r_ref, s_ref):
        s_ref[...] = ((o_ref[...] + r_ref[0].astype(F32)) + r_ref[1].astype(F32)) + r_ref[2].astype(F32)

    return pl.pallas_call(
        body,
        grid_spec=pltpu.PrefetchScalarGridSpec(
            num_scalar_prefetch=1, grid=(m // bm,),
            in_specs=[pl.BlockSpec((None, bm, n), lambda i, s: (s[0], i, 0)),
                      pl.BlockSpec((3, bm, n), lambda i, s: (0, i, 0))],
            out_specs=pl.BlockSpec((bm, n), lambda i, s: (i, 0))),
        out_shape=jax.ShapeDtypeStruct((m, n), F32), compiler_params=_params(1), name=name)(slot, full, recv)


def sum_slots(parts, name, bm=312):
    _, r, n = parts.shape
    bm = bm if r % bm == 0 else r

    def body(p_ref, s_ref):
        acc = p_ref[0]
        for k in range(1, 8):
            acc = acc + p_ref[k]
        s_ref[...] = acc

    return pl.pallas_call(
        body, grid=(r // bm,), in_specs=[pl.BlockSpec((8, bm, n), lambda i: (0, i, 0))],
        out_specs=pl.BlockSpec((bm, n), lambda i: (i, 0)), out_shape=jax.ShapeDtypeStruct((r, n), F32),
        compiler_params=_params(1), name=name)(parts)


def _adamw_math(w, g, m, v):
    m = ADAM_B1 * m + (1.0 - ADAM_B1) * g
    v = ADAM_B2 * v + (1.0 - ADAM_B2) * (g * g)
    m_hat = m / (1.0 - ADAM_B1 ** ADAM_STEP)
    v_hat = v / (1.0 - ADAM_B2 ** ADAM_STEP)
    delta = -ADAM_LR * (m_hat / (jnp.sqrt(v_hat) + ADAM_EPS) + ADAM_WD * w)
    return delta, m, v


def adamw(w, m, v, grads, layer, prev, name, bm=256, row0=0):
    _, _, n = w.shape
    mm = grads[0].shape[0]
    bm = _row_block(mm, bm)
    first = row0 // bm
    ng = len(grads)

    def body(*refs):
        w_ref, m_ref, v_ref = refs[:3]
        g_refs = refs[3:3 + ng]
        g_out, d_out, m_out, v_out = refs[-4:]
        g = g_refs[0][...]
        for gr in g_refs[1:]:
            g = g + gr[...]
        delta, mn, vn = _adamw_math(w_ref[...], g, m_ref[...], v_ref[...])
        g_out[...] = g
        d_out[...] = delta
        m_out[...] = mn
        v_out[...] = vn

    slab = pl.BlockSpec((None, bm, n), lambda i: (layer, first + i, 0))
    flat = pl.BlockSpec((bm, n), lambda i: (i, 0))
    in_specs = [slab] * 3 + [flat] * ng
    args = [w, m, v, *grads]
    aliases = {}
    if prev is not None:
        in_specs += [ANY] * 4
        aliases = {3 + ng + q: q for q in range(4)}
        args += list(prev)
    return pl.pallas_call(
        body, grid=(mm // bm,), in_specs=in_specs, out_specs=[slab] * 4,
        out_shape=[jax.ShapeDtypeStruct(w.shape, F32)] * 4, input_output_aliases=aliases,
        compiler_params=_params(1), name=name)(*args)


def _pack_rows(parts, rows):
    flat = jnp.concatenate([p.reshape(-1) for p in parts])
    return jnp.pad(flat, (0, rows * 128 - flat.shape[0])).reshape(rows, 128)


def _as_shards(a, rows):
    return a.reshape(N_SHARD, rows, a.shape[-1])


def _local_step(x, pos_col, target, gains, pool_w, pool_scale, gn_gain, conv_w, conv_b, weights, send_grads):
    def gain(l, n, token=None):
        g = gains[l, n].reshape(1, D_MODEL)
        return g if token is None else g + token[0:1, 0:1]

    inv_freq = (ROPE_BASE ** (-jnp.arange(0, RET_QK, 2, dtype=F32) / RET_QK)).reshape(1, RET_QK // 2)
    intra, cross, inner, chunk_decay = _decay_tables()
    dn_rows = D_FF // N_SHARD

    x1 = pool_fwd(x, gain(0, 0), gain(0, 1), pool_w, pool_scale)
    cos, sin = rope_tables(pos_col, inv_freq, x1)
    w_up0, w_dn0 = weights("mlp0", cos)
    w_dn0 = w_dn0.reshape(D_FF, D_MODEL)
    x2, u0, s0, a0, f0, h0 = mlp_fwd(x1, gain(0, 2), gain(0, 3), w_up0, w_dn0, conv_w[0], conv_b[0])
    w_in, w_out = weights("ret", x2)
    w_out = w_out.reshape(RET_HEADS * RET_V, D_MODEL)
    pj, hr = ret_proj(x2, gain(1, 0), w_in, cos, sin)
    o, sprev = ret_core_fwd(pj, intra, cross, inner, chunk_decay)
    x3, yb, out = ret_out_fwd(o, pj, x2, gn_gain, gain(1, 1), w_out)
    w_up1, w_dn1 = weights("mlp1", x3)
    w_dn1 = w_dn1.reshape(D_FF, D_MODEL)
    dx4, u1, s1, a1, f1, h1, loss = mlp_fwd(x3, gain(1, 2), gain(1, 3), w_up1, w_dn1, conv_w[1], conv_b[1], target)

    dx3, du1, df1, dg12, dg13, dcw1, dcb1 = mlp_bwd(
        dx4, f1, x3, u1, s1, gain(1, 2), gain(1, 3), w_up1, w_dn1, conv_w[1])
    dwup1 = grad_matmul(h1, du1, D_MODEL, FF_CHUNK, "grad_w_up_1", interleaved=True)
    dwdn1 = grad_matmul(a1, df1, FF_CHUNK, D_MODEL, "grad_w_down_1")
    tok = send_grads("mlp1", [dwup1, [_as_shards(g, dn_rows) for g in dwdn1]])
    dout, dpj, do, dg11, dgn = ret_out_bwd(dx3, out, o, pj, gn_gain, gain(1, 1, tok), w_out)
    dwout = grad_matmul(yb, dout, 1024, D_MODEL, "grad_w_out")
    dpj = ret_core_bwd(pj, do, sprev, cos, sin, dpj, intra, cross, inner, chunk_decay)
    dwin = grad_matmul(hr, dpj, D_MODEL, RET_IN_SHARD, "grad_w_in")
    tok = send_grads("ret", [dwin, [_as_shards(g, RET_V) for g in dwout]])
    dx2, dg10 = ret_in_bwd(dpj, dx3, x2, gain(1, 0, tok), w_in)
    dx1, du0, df0, dg02, dg03, dcw0, dcb0 = mlp_bwd(
        dx2, f0, x1, u0, s0, gain(0, 2), gain(0, 3), w_up0, w_dn0, conv_w[0])
    dwdn0 = grad_matmul(a0, df0, FF_CHUNK, D_MODEL, "grad_w_down_0")
    tok = send_grads("down0", [[_as_shards(g, dn_rows) for g in dwdn0]])
    half = D_MODEL // 2
    for part, first in (("a", 0), ("b", half)):
        dwup0 = grad_matmul(h0, du0, half, FF_CHUNK, "grad_w_up_0" + part, tk=4096, interleaved=True, after=tok,
                            cols=(first, half))
        tok = send_grads("up0" + part, [dwup0])
    dx0, dg00, dg01, dpscale, dpw = pool_bwd(dx1, x, gain(0, 0, tok), gain(0, 1), pool_w, pool_scale)

    rows = lambda g: jnp.sum(g, axis=0, keepdims=True)
    dgains = jnp.concatenate([dg00, dg01, rows(dg02), rows(dg03), dg10, dg11, rows(dg12), rows(dg13)],
                             axis=0).reshape(2, 4, D_MODEL)
    small = {"gains": dgains, "pool_scale": dpscale, "gn": dgn,
             "conv_w": jnp.sum(jnp.stack([dcw0, dcw1]), axis=3),
             "conv_b": jnp.sum(jnp.stack([dcb0, dcb1]), axis=2, keepdims=True), "pool_w": dpw}
    return loss, dx0, small


def kernel(x, positions, norm_gain, pool_w, pool_scale, ret_w_in, ret_gn_gain, ret_w_out, mlp_w_up, mlp_conv_w, mlp_conv_b, mlp_w_down, loss_target, m_norm_gain, m_pool_w, m_pool_scale, m_ret_w_in, m_ret_gn_gain, m_ret_w_out, m_mlp_w_up, m_mlp_conv_w, m_mlp_conv_b, m_mlp_w_down, v_norm_gain, v_pool_w, v_pool_scale, v_ret_w_in, v_ret_gn_gain, v_ret_w_out, v_mlp_w_up, v_mlp_conv_w, v_mlp_conv_b, v_mlp_w_down):
    t = x.shape[1]
    me = 2 * lax.axis_index("x") + lax.axis_index("y")
    me_slot = jnp.reshape(me, (1,)).astype(jnp.int32)

    small_parts = [norm_gain, ret_gn_gain, mlp_conv_w, pool_w]
    small_sizes = [p.size for p in small_parts]
    small_rows = -(-sum(small_sizes) // (128 * 8)) * 8
    groups = {"small": [_pack_rows(small_parts, small_rows)],
              "mlp0": [mlp_w_up[0].astype(BF16), mlp_w_down[0].astype(BF16)],
              "ret": [ret_w_in[0].astype(BF16), ret_w_out[0].astype(BF16)],
              "mlp1": [mlp_w_up[1].astype(BF16), mlp_w_down[1].astype(BF16)]}
    gathers, token = {}, None
    for group, srcs in groups.items():
        lands = [lax.dynamic_update_index_in_dim(lax.empty((N_SHARD,) + s.shape, s.dtype), s, me, 0) for s in srcs]
        mode = "gather_half" if group == "mlp0" else "gather"
        gathers[group] = (mode, exchange_start(mode, srcs, lands, "gather_start_" + group, after=token))
        token = gathers[group][1][-1]

    def weights(group, after):
        mode, started = gathers[group]
        lands = exchange_wait(mode, started, after, "gather_wait_" + group)
        if mode == "gather_half":
            forward = exchange_start("forward_half", [], lands, "forward_start_" + group)
            lands = exchange_wait("forward_half", forward, forward[-1], "forward_wait_" + group)
        return lands

    sent, early = {}, {}

    def reduced(group, after, names):
        started, own = sent[group]
        recv = exchange_wait("scatter", started, after, "scatter_wait_" + group)
        return [plane_sum(me_slot, f, r, "plane_sum_" + nm)
                for f, r, nm in zip(own, recv, names)]

    def swap_start(planes, name):
        return exchange_start("swap", planes, [lax.empty(p.shape, p.dtype) for p in planes], name)

    def send_grads(group, pairs):
        lands = [lax.empty((3,) + b.shape[1:], BF16) for _, b in pairs]
        sent[group] = (exchange_start("scatter", [b for _, b in pairs], lands, "scatter_start_" + group),
                       [f for f, _ in pairs])
        token = sent[group][0][-1]
        if group == "down0":
            marker = pairs[0][1]
            early["planes"] = (reduced("mlp1", marker, ["w_up_1", "w_down_1"])
                               + reduced("ret", marker, ["w_in", "w_out"]))
            early["swap"] = swap_start(early["planes"], "swap_start_a")
            token = token + early["swap"][-1]
        return token

    (smallg,) = weights("small", token)
    smallg = smallg.reshape(N_SHARD, -1)
    offs = np.cumsum([0] + small_sizes)
    piece = lambda i, shape: smallg[:, offs[i]:offs[i + 1]].reshape((N_SHARD,) + shape)
    gains = piece(0, (2, 4, 256)).transpose(1, 2, 0, 3).reshape(2, 4, D_MODEL)
    gn_full = piece(1, (512,)).reshape(1, RET_HEADS * RET_V)
    cw_full = piece(2, (2, 3, FF_CHUNK)).transpose(1, 0, 2, 3)
    pw_full = piece(3, (4, 64, 256)).transpose(1, 0, 2, 3).reshape(4, 256, 256).astype(BF16)
    cb_full = mlp_conv_b.reshape(2, N_SHARD, 1, FF_CHUNK)

    loss, dx0, small = _local_step(
        x[0], positions.reshape(t, 1).astype(F32), loss_target[0], gains, pw_full, pool_scale, gn_full,
        cw_full, cb_full, weights, send_grads)

    def small_adamw(w, m, v, grads, name):
        w3 = w.reshape(1, -1, w.shape[-1])
        out = adamw(w3, m.reshape(w3.shape), v.reshape(w3.shape), [g.reshape(w3.shape[1:]) for g in grads], 0, None, name)
        return [o.reshape(w.shape) for o in out]

    res = {}
    planes_a = early["planes"]
    others_a = exchange_wait("swap", early["swap"], dx0, "swap_wait_a")
    res["ret_w_in"] = adamw(ret_w_in, m_ret_w_in, v_ret_w_in, (planes_a[2], others_a[2]), 0, None, "adamw_w_in")
    res["ret_w_out"] = adamw(ret_w_out, m_ret_w_out, v_ret_w_out, (planes_a[3], others_a[3]), 0, None, "adamw_w_out")
    up1 = adamw(mlp_w_up, m_mlp_w_up, v_mlp_w_up, (planes_a[0], others_a[0]), 1, None, "adamw_w_up_1")
    dn1 = adamw(mlp_w_down, m_mlp_w_down, v_mlp_w_down, (planes_a[1], others_a[1]), 1, None, "adamw_w_down_1")

    pw_f = small["pool_w"].reshape(4, N_SHARD, 64, 256).transpose(1, 0, 2, 3).reshape(N_SHARD, 256, 256)
    small_order = ["gains", "pool_scale", "gn", "conv_w", "conv_b"]
    gsmall_sizes = [small[k].size for k in small_order]
    gsmall_rows = -(-sum(gsmall_sizes) // (128 * 8)) * 8
    planes_b = (reduced("up0a", dn1[0], ["w_up_0a"]) + reduced("up0b", dn1[0], ["w_up_0b"])
                + reduced("down0", dn1[0], ["w_down_0"]))
    pw_recv, small_recv = scatter_grads([pw_f.astype(BF16)], _pack_rows([small[k] for k in small_order], gsmall_rows))
    planes_b.append(plane_sum(me_slot, pw_f, pw_recv, "plane_sum_pool_w"))
    swap_b = swap_start(planes_b, "swap_start_b")

    gsmall = sum_slots(small_recv, "sum_small").reshape(-1)
    goffs = np.cumsum([0] + gsmall_sizes)
    gpiece = lambda i: gsmall[goffs[i]:goffs[i + 1]].reshape(small[small_order[i]].shape)
    g_gains = lax.dynamic_slice_in_dim(gpiece(0), me * 256, 256, axis=2)
    g_gn = lax.dynamic_slice_in_dim(gpiece(2), me * RET_V, RET_V, axis=1)
    g_cw = lax.dynamic_index_in_dim(gpiece(3), me, 1, keepdims=False)
    res["norm_gain"] = small_adamw(norm_gain, m_norm_gain, v_norm_gain, [g_gains], "adamw_norm_gain")
    res["pool_scale"] = small_adamw(pool_scale, m_pool_scale, v_pool_scale, [gpiece(1)], "adamw_pool_scale")
    res["ret_gn_gain"] = small_adamw(ret_gn_gain, m_ret_gn_gain, v_ret_gn_gain, [g_gn], "adamw_gn_gain")
    res["mlp_conv_w"] = small_adamw(mlp_conv_w, m_mlp_conv_w, v_mlp_conv_w, [g_cw], "adamw_conv_w")
    res["mlp_conv_b"] = small_adamw(mlp_conv_b, m_mlp_conv_b, v_mlp_conv_b, [gpiece(4)], "adamw_conv_b")

    others_b = exchange_wait("swap", swap_b, res["mlp_conv_b"][0], "swap_wait_b")
    up0a = adamw(mlp_w_up, m_mlp_w_up, v_mlp_w_up, (planes_b[0], others_b[0]), 0, up1, "adamw_w_up_0a")
    res["mlp_w_up"] = adamw(mlp_w_up, m_mlp_w_up, v_mlp_w_up, (planes_b[1], others_b[1]), 0, up0a, "adamw_w_up_0b",
                            row0=D_MODEL // 2)
    res["mlp_w_down"] = adamw(mlp_w_down, m_mlp_w_down, v_mlp_w_down, (planes_b[2], others_b[2]), 0, dn1,
                              "adamw_w_down_0")
    res["pool_w"] = small_adamw(pool_w, m_pool_w, v_pool_w, (planes_b[3], others_b[3]), "adamw_pool_w")

    order = ["norm_gain", "pool_w", "pool_scale", "ret_w_in", "ret_gn_gain", "ret_w_out", "mlp_w_up", "mlp_conv_w",
             "mlp_conv_b", "mlp_w_down"]
    total_loss = lax.psum(loss[0, 0], ("x", "y", "c"))
    outs = [total_loss, dx0.reshape(x.shape)]
    for q in range(4):
        outs += [res[k][q] for k in order]
    return tuple(outs)
```

```python
import numpy as np
import jax
import jax.numpy as jnp
from jax import lax
from jax.experimental import pallas as pl
from jax.experimental.pallas import tpu as pltpu

F32 = jnp.float32
BF16 = jnp.bfloat16

D_MODEL = 1024
D_FF = 2816
FF_CHUNK = 1408
N_SHARD = 4
POOL_WINDOWS = (2, 4, 8, 16)
POOL_DIM = 256
POOL_HALO = 16
RET_HEADS = 4
RET_QK = 256
RET_V = 512
RET_CHUNK = 256
RET_IN = 6144
RET_IN_SHARD = 1536
ROPE_BASE = 10000.0
EPS = 1e-6
CONV_HALO = 8

ADAM_LR, ADAM_B1, ADAM_B2, ADAM_EPS, ADAM_WD, ADAM_STEP = 0.001, 0.9, 0.999, 1e-08, 0.01, 10

VMEM_LIMIT = 56 * 1024 * 1024
VMEM_LIMIT_MLP_BWD = 62 * 1024 * 1024
MESH = pl.DeviceIdType.MESH
ANY = pl.BlockSpec(memory_space=pl.ANY)


def _params(n_grid=1, limit=VMEM_LIMIT):
    return pltpu.CompilerParams(dimension_semantics=("arbitrary",) * n_grid, vmem_limit_bytes=limit)


def _dot(a, b):
    return jnp.dot(a, b, preferred_element_type=F32)


def _dot_nt(a, b):
    return lax.dot_general(a, b, (((1,), (1,)), ((), ())), preferred_element_type=F32)


def _dot_tn(a, b):
    return lax.dot_general(a, b, (((0,), (0,)), ((), ())), preferred_element_type=F32)


def _rms_fwd(x, gain):
    r = lax.rsqrt(jnp.mean(x * x, axis=-1, keepdims=True) + EPS)
    xh = x * r
    return xh * gain, xh, r


def _rms_bwd(xh, r, gain, dy):
    dxh = dy * gain
    return r * (dxh - xh * jnp.mean(dxh * xh, axis=-1, keepdims=True))


def _colsum(v):
    return jnp.sum(v, axis=0, keepdims=True)


def _full(shape):
    nd = len(shape)
    return pl.BlockSpec(shape, lambda *_: (0,) * nd)


def rope_tables(pos_col, inv_freq, after):
    t = pos_col.shape[0]
    tm = min(t, 1024)

    def body(p_ref, f_ref, after_ref, c_ref, s_ref):
        ang = p_ref[...] * f_ref[...]
        c_ref[...] = jnp.cos(ang)
        s_ref[...] = jnp.sin(ang)

    return pl.pallas_call(
        body, grid=(t // tm,),
        in_specs=[pl.BlockSpec((tm, 1), lambda i: (i, 0)), _full((1, 128)), ANY],
        out_specs=[pl.BlockSpec((tm, 128), lambda i: (i, 0))] * 2,
        out_shape=[jax.ShapeDtypeStruct((t, 128), F32)] * 2,
        compiler_params=_params(1), name="rope_tables")(pos_col, inv_freq, after)


def _window_sums(ext, backward):
    n = ext.shape[0]
    cur, sums = ext, []
    for g, win in enumerate(POOL_WINDOWS):
        if g > 0:
            cur = cur[:, POOL_DIM:]
        half = win // 2
        cur = cur + pltpu.roll(cur, n - half if backward else half, axis=0)
        sums.append(cur[:, 0:POOL_DIM])
    return sums


def _pool_diff(h_halo, h, row0, tm):
    t_idx = row0 + lax.broadcasted_iota(jnp.int32, (tm, 1), 0)
    sums = _window_sums(jnp.concatenate([h_halo, h], axis=0), backward=False)
    parts, inv_counts = [], []
    for g, win in enumerate(POOL_WINDOWS):
        inv = 1.0 / jnp.minimum(t_idx + 1, win).astype(F32)
        parts.append(sums[g][POOL_HALO:, :] * inv - h[:, g * POOL_DIM:(g + 1) * POOL_DIM])
        inv_counts.append(inv)
    return parts, inv_counts


def pool_fwd(x, g_pre, g_post, pool_w, pool_scale, tm=512):
    t, d = x.shape
    nt = t // tm

    def body(x_ref, g0_ref, g1_ref, w_ref, sc_ref, o_ref, hext):
        i = pl.program_id(0)

        @pl.when(i == 0)
        def _():
            hext[...] = jnp.zeros((POOL_HALO, d), F32)

        xv = x_ref[...]
        h, _, _ = _rms_fwd(xv, g0_ref[...])
        parts, _ = _pool_diff(hext[...], h, i * tm, tm)
        hext[...] = h[tm - POOL_HALO:tm, :]
        ys = [_dot(parts[g].astype(BF16), w_ref[g]) for g in range(len(POOL_WINDOWS))]
        y = jnp.concatenate(ys, axis=-1) * sc_ref[...]
        m, _, _ = _rms_fwd(y, g1_ref[...])
        o_ref[...] = xv + m

    row = pl.BlockSpec((tm, d), lambda i: (i, 0))
    return pl.pallas_call(
        body, grid=(nt,),
        in_specs=[row, _full((1, d)), _full((1, d)), _full(pool_w.shape), _full((1, d))],
        out_specs=row, out_shape=jax.ShapeDtypeStruct((t, d), F32),
        scratch_shapes=[pltpu.VMEM((POOL_HALO, d), F32)],
        compiler_params=_params(1), name="pool_fwd")(x, g_pre, g_post, pool_w, pool_scale)


def pool_bwd(dx1, x, g_pre, g_post, pool_w, pool_scale, tm=512):
    t, d = x.shape
    nt = t // tm
    ng = len(POOL_WINDOWS)

    def body(dx1_ref, x_ref, xh_ref, g0_ref, g1_ref, w_ref, sc_ref,
             dx_ref, dg0_ref, dg1_ref, dsc_ref, dw_ref, enext):
        i = pl.program_id(0)
        r = nt - 1 - i

        @pl.when(i == 0)
        def _():
            enext[...] = jnp.zeros((POOL_HALO, d), F32)
            dg0_ref[...] = jnp.zeros_like(dg0_ref)
            dg1_ref[...] = jnp.zeros_like(dg1_ref)
            dsc_ref[...] = jnp.zeros_like(dsc_ref)
            dw_ref[...] = jnp.zeros_like(dw_ref)

        g0 = g0_ref[...]
        g1 = g1_ref[...]
        sc = sc_ref[...]
        xv = x_ref[...]
        h, xh, rx = _rms_fwd(xv, g0)
        h_halo, _, _ = _rms_fwd(xh_ref[...], g0)
        parts, inv_counts = _pool_diff(h_halo * jnp.where(r > 0, 1.0, 0.0), h, r * tm, tm)
        parts_b = [p.astype(BF16) for p in parts]
        ypre = jnp.concatenate([_dot(parts_b[g], w_ref[g]) for g in range(ng)], axis=-1)
        _, yh, ry = _rms_fwd(ypre * sc, g1)
        dm = dx1_ref[...]
        dg1_ref[...] += _colsum(dm * yh)
        dy = _rms_bwd(yh, ry, g1, dm)
        dsc_ref[...] += _colsum(dy * ypre)
        dyp = (dy * sc).astype(BF16)
        ddiffs = []
        for g in range(ng):
            cols = slice(g * POOL_DIM, (g + 1) * POOL_DIM)
            dw_ref[g] += _dot_tn(parts_b[g], dyp[:, cols])
            ddiffs.append(_dot_nt(dyp[:, cols], w_ref[g]))
        e = jnp.concatenate([ddiffs[g] * inv_counts[g] for g in range(ng)], axis=-1)
        sums = _window_sums(jnp.concatenate([e, enext[...]], axis=0), backward=True)
        enext[...] = e[0:POOL_HALO, :]
        dh = jnp.concatenate([sums[g][0:tm, :] - ddiffs[g] for g in range(ng)], axis=-1)
        dg0_ref[...] += _colsum(dh * xh)
        dx_ref[...] = dm + _rms_bwd(xh, rx, g0, dh)

    row = pl.BlockSpec((tm, d), lambda i: (nt - 1 - i, 0))
    halo = pl.BlockSpec((POOL_HALO, d), lambda i: (jnp.maximum((nt - 1 - i) * (tm // POOL_HALO) - 1, 0), 0))
    vec = _full((1, d))
    return pl.pallas_call(
        body, grid=(nt,),
        in_specs=[row, row, halo, vec, vec, _full(pool_w.shape), vec],
        out_specs=[row, vec, vec, vec, _full((ng, POOL_DIM, POOL_DIM))],
        out_shape=[jax.ShapeDtypeStruct((t, d), F32)] + [jax.ShapeDtypeStruct((1, d), F32)] * 3
        + [jax.ShapeDtypeStruct((ng, POOL_DIM, POOL_DIM), F32)],
        scratch_shapes=[pltpu.VMEM((POOL_HALO, d), F32)],
        compiler_params=_params(1), name="pool_bwd")(dx1, x, x, g_pre, g_post, pool_w, pool_scale)


def _conv_taps(cw_ref, j):
    return cw_ref[j, 0:1, :], cw_ref[j, 1:2, :], cw_ref[j, 2:3, :]


def _row_block(m, target=256):
    if m <= target:
        return m
    for b in range(target, 7, -8):
        if m % b == 0:
            return b
    return m


def mlp_fwd(x, g_pre, g_post, w_up, w_down, conv_w, conv_b, target=None, tm=256):
    t, d = x.shape
    nt = t // tm
    h8 = CONV_HALO
    with_loss = target is not None
    n_extra = 1 if with_loss else 0

    def body(x_ref, g2_ref, g3_ref, wup_hbm, wdn_hbm, cw_ref, cb_ref, *rest):
        tgt_ref = rest[0] if with_loss else None
        xo_ref, u_ref, s_ref, a_ref, f_ref, h_ref = rest[n_extra:n_extra + 6]
        loss_ref = rest[n_extra + 6] if with_loss else None
        wup_v, wdn_v, tail, sem = rest[-4:]
        i = pl.program_id(0)

        @pl.when(i == 0)
        def _():
            c1 = pltpu.make_async_copy(wup_hbm, wup_v, sem.at[0])
            c2 = pltpu.make_async_copy(wdn_hbm, wdn_v, sem.at[1])
            c1.start()
            c2.start()
            tail[...] = jnp.zeros_like(tail)
            if with_loss:
                loss_ref[...] = jnp.zeros_like(loss_ref)
            c1.wait()
            c2.wait()

        xv = x_ref[...]
        h, _, _ = _rms_fwd(xv, g2_ref[...])
        hb = h.astype(BF16)
        h_ref[...] = hb
        acc = jnp.zeros((tm, d), F32)
        for k in range(2):
            cs = []
            for s in range(2):
                j, cols = k + 2 * s, slice((2 * k + s) * FF_CHUNK, (2 * k + s + 1) * FF_CHUNK)
                uf = _dot(hb, wup_v[j])
                u_ref[:, cols] = uf.astype(BF16)
                ext = jnp.concatenate([tail[j], uf], axis=0)
                tail[j] = uf[tm - h8:tm, :]
                w0, w1, w2 = _conv_taps(cw_ref, j)
                cs.append(cb_ref[j] + w2 * uf + w1 * pltpu.roll(ext, 1, axis=0)[h8:, :]
                          + w0 * pltpu.roll(ext, 2, axis=0)[h8:, :])
            cg, cv = cs
            sg = jax.nn.sigmoid(cg)
            sil = cg * sg
            ab = (sil * cv).astype(BF16)
            a_ref[:, k * FF_CHUNK:(k + 1) * FF_CHUNK] = ab
            s_ref[:, 2 * k * FF_CHUNK:(2 * k + 1) * FF_CHUNK] = sil.astype(BF16)
            s_ref[:, (2 * k + 1) * FF_CHUNK:(2 * k + 2) * FF_CHUNK] = (cv * (sg * (1.0 + cg * (1.0 - sg)))).astype(BF16)
            acc = acc + _dot(ab, wdn_v[k * FF_CHUNK:(k + 1) * FF_CHUNK, :])
        f_ref[...] = acc
        y, _, _ = _rms_fwd(acc, g3_ref[...])
        if with_loss:
            err = (xv + y) - tgt_ref[...]
            xo_ref[...] = err * (1.0 / d)
            loss_ref[...] += 0.5 * jnp.sum(jnp.mean(err * err, axis=-1, keepdims=True), axis=0, keepdims=True)
        else:
            xo_ref[...] = xv + y

    row = pl.BlockSpec((tm, d), lambda i: (i, 0))
    wide = pl.BlockSpec((tm, 2 * D_FF), lambda i: (i, 0))
    vec = _full((1, d))
    extra = [target] if with_loss else []
    return pl.pallas_call(
        body, grid=(nt,),
        in_specs=[row, vec, vec, ANY, ANY, _full(conv_w.shape), _full(conv_b.shape)] + [row] * n_extra,
        out_specs=[row, wide, wide, pl.BlockSpec((tm, D_FF), lambda i: (i, 0)), row, row] + [_full((1, 1))] * n_extra,
        out_shape=[jax.ShapeDtypeStruct((t, d), F32), jax.ShapeDtypeStruct((t, 2 * D_FF), BF16),
                   jax.ShapeDtypeStruct((t, 2 * D_FF), BF16), jax.ShapeDtypeStruct((t, D_FF), BF16),
                   jax.ShapeDtypeStruct((t, d), F32), jax.ShapeDtypeStruct((t, d), BF16)]
        + [jax.ShapeDtypeStruct((1, 1), F32)] * n_extra,
        scratch_shapes=[pltpu.VMEM(w_up.shape, BF16), pltpu.VMEM(w_down.shape, BF16),
                        pltpu.VMEM((N_SHARD, h8, FF_CHUNK), F32), pltpu.SemaphoreType.DMA((2,))],
        compiler_params=_params(1), name="mlp_fwd_loss" if with_loss else "mlp_fwd")(
            x, g_pre, g_post, w_up, w_down, conv_w, conv_b, *extra)


def _rowsum8(v):
    return jnp.sum(v.reshape(v.shape[0] // 8, 8, v.shape[1]), axis=0)


def mlp_bwd(dxo, f, x, u, sp, g_pre, g_post, w_up, w_down, conv_w, tm=256):
    t, d = x.shape
    nt = t // tm
    h8 = CONV_HALO

    def body(dxo_ref, f_ref, x_ref, u_ref, s_ref, g2_ref, g3_ref, wup_hbm, wdn_hbm, cw_ref,
             dx_ref, du_ref, df_ref, dg2_ref, dg3_ref, dcw_ref, dcb_ref,
             wup_v, wdn_v, carry, sem):
        @pl.when(pl.program_id(0) == 0)
        def _():
            c1 = pltpu.make_async_copy(wup_hbm, wup_v, sem.at[0])
            c2 = pltpu.make_async_copy(wdn_hbm, wdn_v, sem.at[1])
            c1.start()
            c2.start()
            carry[...] = jnp.zeros_like(carry)
            dg2_ref[...] = jnp.zeros_like(dg2_ref)
            dg3_ref[...] = jnp.zeros_like(dg3_ref)
            dcw_ref[...] = jnp.zeros_like(dcw_ref)
            dcb_ref[...] = jnp.zeros_like(dcb_ref)
            c1.wait()
            c2.wait()

        g3 = g3_ref[...]
        dxo = dxo_ref[...]
        _, fh, rf = _rms_fwd(f_ref[...], g3)
        dg3_ref[...] += _rowsum8(dxo * fh)
        dfb = _rms_bwd(fh, rf, g3, dxo).astype(BF16)
        df_ref[...] = dfb
        dh = jnp.zeros((tm, d), F32)
        for k in range(2):
            da = _dot_nt(dfb, wdn_v[k * FF_CHUNK:(k + 1) * FF_CHUNK, :])
            for s in range(2):
                j = k + 2 * s
                cols = slice((2 * k + s) * FF_CHUNK, (2 * k + s + 1) * FF_CHUNK)
                dc = da * s_ref[:, (2 * k + 1 - s) * FF_CHUNK:(2 * k + 2 - s) * FF_CHUNK].astype(F32)
                uf = u_ref[:, cols].astype(F32)
                ext = jnp.concatenate([dc, carry[j]], axis=0)
                carry[j] = dc[0:h8, :]
                dc1 = pltpu.roll(ext, tm + h8 - 1, axis=0)[0:tm, :]
                dc2 = pltpu.roll(ext, tm + h8 - 2, axis=0)[0:tm, :]
                dcb_ref[j] += _rowsum8(dc)
                dcw_ref[j, 2] += _rowsum8(dc * uf)
                dcw_ref[j, 1] += _rowsum8(dc1 * uf)
                dcw_ref[j, 0] += _rowsum8(dc2 * uf)
                dub = (cw_ref[j, 2:3, :] * dc + cw_ref[j, 1:2, :] * dc1 + cw_ref[j, 0:1, :] * dc2).astype(BF16)
                du_ref[:, cols] = dub
                dh = dh + _dot_nt(dub, wup_v[j])
        g2 = g2_ref[...]
        _, xh, rx = _rms_fwd(x_ref[...], g2)
        dg2_ref[...] += _rowsum8(dh * xh)
        dx_ref[...] = dxo + _rms_bwd(xh, rx, g2, dh)

    row = pl.BlockSpec((tm, d), lambda i: (nt - 1 - i, 0))
    wide = pl.BlockSpec((tm, 2 * D_FF), lambda i: (nt - 1 - i, 0))
    vec = _full((1, d))
    acc = _full((8, d))
    dcw_shape, dcb_shape = (N_SHARD, 3, 8, FF_CHUNK), (N_SHARD, 8, FF_CHUNK)
    return pl.pallas_call(
        body, grid=(nt,),
        in_specs=[row, row, row, wide, wide, vec, vec, ANY, ANY, _full(conv_w.shape)],
        out_specs=[row, wide, row, acc, acc, _full(dcw_shape), _full(dcb_shape)],
        out_shape=[jax.ShapeDtypeStruct((t, d), F32), jax.ShapeDtypeStruct((t, 2 * D_FF), BF16),
                   jax.ShapeDtypeStruct((t, d), BF16),
                   jax.ShapeDtypeStruct((8, d), F32), jax.ShapeDtypeStruct((8, d), F32),
                   jax.ShapeDtypeStruct(dcw_shape, F32), jax.ShapeDtypeStruct(dcb_shape, F32)],
        scratch_shapes=[pltpu.VMEM(w_up.shape, BF16), pltpu.VMEM(w_down.shape, BF16),
                        pltpu.VMEM((N_SHARD, h8, FF_CHUNK), F32), pltpu.SemaphoreType.DMA((2,))],
        compiler_params=_params(1, VMEM_LIMIT_MLP_BWD), name="mlp_bwd")(
            dxo, f, x, u, sp, g_pre, g_post, w_up, w_down, conv_w)


def grad_matmul(a, b, bm, bn, name, tk=2048, interleaved=False, after=None, cols=None):
    t = a.shape[0]
    m0, m = (0, a.shape[1]) if cols is None else cols
    n = b.shape[1]
    tk = min(tk, t)
    nk = t // tk
    place = (lambda j: (j % 2) * 2 + j // 2) if interleaved else (lambda j: j)
    extra = [] if after is None else [after]
    first = m0 // bm

    def body(a_ref, b_ref, *rest):
        o_ref, ob_ref = rest[len(extra):]
        kk = pl.program_id(2)

        @pl.when(kk == 0)
        def _():
            o_ref[...] = jnp.zeros_like(o_ref)

        o_ref[...] += _dot_tn(a_ref[...], b_ref[...])

        @pl.when(kk == nk - 1)
        def _():
            ob_ref[...] = o_ref[...].astype(BF16)

    ospec = pl.BlockSpec((None, bm, bn), lambda j, i, kk: (place(j), i, 0))
    return pl.pallas_call(
        body, grid=(n // bn, m // bm, nk),
        in_specs=[pl.BlockSpec((tk, bm), lambda j, i, kk: (kk, first + i)),
                  pl.BlockSpec((tk, bn), lambda j, i, kk: (kk, j))]
        + [ANY] * len(extra),
        out_specs=[ospec, ospec],
        out_shape=[jax.ShapeDtypeStruct((n // bn, m, bn), F32), jax.ShapeDtypeStruct((n // bn, m, bn), BF16)],
        compiler_params=_params(3), name=name)(a, b, *extra)


def _decay_tables():
    log_gamma = jnp.log(1.0 - 2.0 ** (-5.0 - jnp.arange(RET_HEADS, dtype=F32)))
    i = jnp.arange(RET_CHUNK, dtype=F32)
    rel = i[:, None] - i[None, :]
    intra = jnp.where(rel >= 0, jnp.exp(jnp.maximum(rel, 0.0) * log_gamma[:, None, None]), 0.0)
    cross = jnp.exp((i + 1.0) * log_gamma[:, None])[:, :, None]
    inner = jnp.exp((RET_CHUNK - 1.0 - i) * log_gamma[:, None])[:, :, None]
    chunk = [float(np.exp(np.float32(RET_CHUNK) * np.log(np.float32(1.0 - 2.0 ** (-5.0 - h))).astype(np.float32)))
             for h in range(RET_HEADS)]
    return intra, cross, inner, chunk


def ret_proj(x, g_pre, w_in, cos, sin, tm=512):
    t, d = x.shape
    nt = t // tm
    per = RET_IN_SHARD // RET_QK

    def body(x_ref, g_ref, win_hbm, c_ref, s_ref, pj_ref, h_ref, win_v, sem):
        @pl.when(pl.program_id(0) == 0)
        def _():
            cp = pltpu.make_async_copy(win_hbm, win_v, sem)
            cp.start()
            cp.wait()

        h, _, _ = _rms_fwd(x_ref[...], g_ref[...])
        hb = h.astype(BF16)
        h_ref[...] = hb
        c = c_ref[...]
        s = s_ref[...]
        for j in range(N_SHARD):
            pjj = _dot(hb, win_v[j])
            for bb in range(per):
                b = per * j + bb
                blk = pjj[:, bb * RET_QK:(bb + 1) * RET_QK]
                if b < 2 * RET_HEADS:
                    x1, x2 = blk[:, :128], blk[:, 128:]
                    o1 = x1 * c - x2 * s
                    o2 = x2 * c + x1 * s
                    if b < RET_HEADS:
                        o1 = o1 * (RET_QK ** -0.5)
                        o2 = o2 * (RET_QK ** -0.5)
                    pj_ref[:, b * RET_QK:b * RET_QK + 128] = o1.astype(BF16)
                    pj_ref[:, b * RET_QK + 128:(b + 1) * RET_QK] = o2.astype(BF16)
                else:
                    pj_ref[:, b * RET_QK:(b + 1) * RET_QK] = blk.astype(BF16)

    row = pl.BlockSpec((tm, d), lambda i: (i, 0))
    tab = pl.BlockSpec((tm, 128), lambda i: (i, 0))
    return pl.pallas_call(
        body, grid=(nt,),
        in_specs=[row, _full((1, d)), ANY, tab, tab],
        out_specs=[pl.BlockSpec((tm, RET_IN), lambda i: (i, 0)), row],
        out_shape=[jax.ShapeDtypeStruct((t, RET_IN), BF16), jax.ShapeDtypeStruct((t, d), BF16)],
        scratch_shapes=[pltpu.VMEM(w_in.shape, BF16), pltpu.SemaphoreType.DMA],
        compiler_params=_params(1), name="ret_proj")(x, g_pre, w_in, cos, sin)


def ret_core_fwd(pj, intra, cross, inner, chunk_decay):
    t = pj.shape[0]
    nc = t // RET_CHUNK
    c = RET_CHUNK
    qk_all = RET_HEADS * RET_QK
    v_all = RET_HEADS * RET_V

    def body(q_ref, k_ref, v_ref, dm_ref, cr_ref, in_ref, o_ref, sp_ref, state):
        @pl.when(pl.program_id(0) == 0)
        def _():
            state[...] = jnp.zeros_like(state)

        for h in range(RET_HEADS):
            q = q_ref[:, h * RET_QK:(h + 1) * RET_QK]
            k = k_ref[:, h * RET_QK:(h + 1) * RET_QK]
            v = v_ref[:, h * RET_V:(h + 1) * RET_V]
            sb = state[h].astype(BF16)
            sp_ref[h] = sb
            sc = _dot_nt(q, k) * dm_ref[h]
            o_ref[:, h * RET_V:(h + 1) * RET_V] = _dot(sc.astype(BF16), v) + _dot(q, sb) * cr_ref[h]
            kd = (k.astype(F32) * in_ref[h]).astype(BF16)
            state[h] = state[h] * chunk_decay[h] + _dot_tn(kd, v)

    return pl.pallas_call(
        body, grid=(nc,),
        in_specs=[pl.BlockSpec((c, qk_all), lambda n: (n, 0)), pl.BlockSpec((c, qk_all), lambda n: (n, 1)),
                  pl.BlockSpec((c, v_all), lambda n: (n, 1)),
                  _full(intra.shape), _full(cross.shape), _full(inner.shape)],
        out_specs=[pl.BlockSpec((c, v_all), lambda n: (n, 0)),
                   pl.BlockSpec((None, RET_HEADS, RET_QK, RET_V), lambda n: (n, 0, 0, 0))],
        out_shape=[jax.ShapeDtypeStruct((t, v_all), F32),
                   jax.ShapeDtypeStruct((nc, RET_HEADS, RET_QK, RET_V), BF16)],
        scratch_shapes=[pltpu.VMEM((RET_HEADS, RET_QK, RET_V), F32)],
        compiler_params=_params(1), name="ret_core_fwd")(pj, pj, pj, intra, cross, inner)


def _group_norm(o_h):
    mu = jnp.mean(o_h, axis=-1, keepdims=True)
    dev = o_h - mu
    rstd = lax.rsqrt(jnp.mean(dev * dev, axis=-1, keepdims=True) + EPS)
    return dev * rstd, rstd


def ret_out_fwd(o, pj, x, gn_gain, g_post, w_out, tm=512):
    t, d = x.shape
    nt = t // tm
    v_all = RET_HEADS * RET_V

    def body(o_ref, g_ref, x_ref, gn_ref, g1_ref, w_ref, xo_ref, y_ref, out_ref):
        for h in range(RET_HEADS):
            cols = slice(h * RET_V, (h + 1) * RET_V)
            ohat, _ = _group_norm(o_ref[:, cols])
            g = g_ref[:, cols].astype(F32)
            y_ref[:, cols] = (g * jax.nn.sigmoid(g) * (ohat * gn_ref[:, cols])).astype(BF16)
        out = _dot(y_ref[...], w_ref[...])
        out_ref[...] = out
        m, _, _ = _rms_fwd(out, g1_ref[...])
        xo_ref[...] = x_ref[...] + m

    row = pl.BlockSpec((tm, d), lambda i: (i, 0))
    wide = pl.BlockSpec((tm, v_all), lambda i: (i, 0))
    return pl.pallas_call(
        body, grid=(nt,),
        in_specs=[wide, pl.BlockSpec((tm, v_all), lambda i: (i, 2)), row, _full((1, v_all)), _full((1, d)),
                  _full(w_out.shape)],
        out_specs=[row, wide, row],
        out_shape=[jax.ShapeDtypeStruct((t, d), F32), jax.ShapeDtypeStruct((t, v_all), BF16),
                   jax.ShapeDtypeStruct((t, d), F32)],
        compiler_params=_params(1), name="ret_out_fwd")(o, pj, x, gn_gain, g_post, w_out)


def ret_out_bwd(dxo, out, o, pj, gn_gain, g_post, w_out, tm=256):
    t, d = out.shape
    nt = t // tm
    v_all = RET_HEADS * RET_V

    def body(dxo_ref, out_ref, o_ref, g_ref, gn_ref, g1_ref, w_ref,
             dout_ref, dgate_ref, do_ref, dg1_ref, dgn_ref):
        @pl.when(pl.program_id(0) == 0)
        def _():
            dg1_ref[...] = jnp.zeros_like(dg1_ref)
            dgn_ref[...] = jnp.zeros_like(dgn_ref)

        g1 = g1_ref[...]
        dxo = dxo_ref[...]
        _, oh_, r_ = _rms_fwd(out_ref[...], g1)
        dg1_ref[...] += _colsum(dxo * oh_)
        doutb = _rms_bwd(oh_, r_, g1, dxo).astype(BF16)
        dout_ref[...] = doutb
        dy = _dot_nt(doutb, w_ref[...])
        for h in range(RET_HEADS):
            cols = slice(h * RET_V, (h + 1) * RET_V)
            gn = gn_ref[:, cols]
            ohat, rstd = _group_norm(o_ref[:, cols])
            g = g_ref[:, cols].astype(F32)
            sg = jax.nn.sigmoid(g)
            dyh = dy[:, cols]
            dgate_ref[:, cols] = (dyh * (ohat * gn) * (sg * (1.0 + g * (1.0 - sg)))).astype(BF16)
            don = dyh * (g * sg)
            dgn_ref[:, cols] += _colsum(don * ohat)
            dohat = don * gn
            do_ref[:, cols] = (rstd * (dohat - jnp.mean(dohat, axis=-1, keepdims=True)
                                       - ohat * jnp.mean(dohat * ohat, axis=-1, keepdims=True))).astype(BF16)

    row = pl.BlockSpec((tm, d), lambda i: (i, 0))
    wide = pl.BlockSpec((tm, v_all), lambda i: (i, 0))
    gate = pl.BlockSpec((tm, v_all), lambda i: (i, 2))
    return pl.pallas_call(
        body, grid=(nt,),
        in_specs=[row, row, wide, gate, _full((1, v_all)), _full((1, d)), _full(w_out.shape)],
        out_specs=[row, gate, wide, _full((1, d)), _full((1, v_all))],
        out_shape=[jax.ShapeDtypeStruct((t, d), BF16), jax.ShapeDtypeStruct((t, RET_IN), BF16),
                   jax.ShapeDtypeStruct((t, v_all), BF16), jax.ShapeDtypeStruct((1, d), F32),
                   jax.ShapeDtypeStruct((1, v_all), F32)],
        compiler_params=_params(1), name="ret_out_bwd")(dxo, out, o, pj, gn_gain, g_post, w_out)


def ret_core_bwd(pj, do, sprev, cos, sin, dpj, intra, cross, inner, chunk_decay):
    t = pj.shape[0]
    nc = t // RET_CHUNK
    c = RET_CHUNK
    qk_all = RET_HEADS * RET_QK
    v_all = RET_HEADS * RET_V
    scale = RET_QK ** -0.5

    def body(q_ref, k_ref, v_ref, do_ref, sp_ref, c_ref, s_ref, dm_ref, cr_ref, in_ref, dpj_in, dpj_ref, dstate):
        @pl.when(pl.program_id(0) == 0)
        def _():
            dstate[...] = jnp.zeros_like(dstate)

        cs = c_ref[...]
        sn = s_ref[...]
        for h in range(RET_HEADS):
            q = q_ref[:, h * RET_QK:(h + 1) * RET_QK]
            k = k_ref[:, h * RET_QK:(h + 1) * RET_QK]
            v = v_ref[:, h * RET_V:(h + 1) * RET_V]
            doh = do_ref[:, h * RET_V:(h + 1) * RET_V]
            dm = dm_ref[h]
            ab = (_dot_nt(q, k) * dm).astype(BF16)
            dab = (_dot_nt(doh, v) * dm).astype(BF16)
            dsb = dstate[h].astype(BF16)
            kd = (k.astype(F32) * in_ref[h]).astype(BF16)
            dv = _dot_tn(ab, doh) + _dot(kd, dsb)
            dq = _dot(dab, k) + cr_ref[h] * _dot_nt(doh, sp_ref[h])
            dk = _dot_tn(dab, q) + in_ref[h] * _dot_nt(v, dsb)
            qd = (q.astype(F32) * cr_ref[h]).astype(BF16)
            dstate[h] = dstate[h] * chunk_decay[h] + _dot_tn(qd, doh)
            for base, dd, sc in ((h * RET_QK, dq, scale), (qk_all + h * RET_QK, dk, 1.0)):
                d1, d2 = dd[:, :128], dd[:, 128:]
                dpj_ref[:, base:base + 128] = ((d1 * cs + d2 * sn) * sc).astype(BF16)
                dpj_ref[:, base + 128:base + RET_QK] = ((d2 * cs - d1 * sn) * sc).astype(BF16)
            dpj_ref[:, 2 * qk_all + h * RET_V:2 * qk_all + (h + 1) * RET_V] = dv.astype(BF16)

    rev = lambda n: nc - 1 - n
    tab = pl.BlockSpec((c, 128), lambda n: (rev(n), 0))
    return pl.pallas_call(
        body, grid=(nc,),
        in_specs=[pl.BlockSpec((c, qk_all), lambda n: (rev(n), 0)), pl.BlockSpec((c, qk_all), lambda n: (rev(n), 1)),
                  pl.BlockSpec((c, v_all), lambda n: (rev(n), 1)), pl.BlockSpec((c, v_all), lambda n: (rev(n), 0)),
                  pl.BlockSpec((None, RET_HEADS, RET_QK, RET_V), lambda n: (rev(n), 0, 0, 0)),
                  tab, tab, _full(intra.shape), _full(cross.shape), _full(inner.shape), ANY],
        out_specs=pl.BlockSpec((c, 2 * qk_all + v_all), lambda n: (rev(n), 0)),
        out_shape=jax.ShapeDtypeStruct((t, RET_IN), BF16),
        scratch_shapes=[pltpu.VMEM((RET_HEADS, RET_QK, RET_V), F32)],
        input_output_aliases={10: 0},
        compiler_params=_params(1), name="ret_core_bwd")(pj, pj, pj, do, sprev, cos, sin, intra, cross, inner, dpj)


def ret_in_bwd(dpj, dres, x, g_pre, w_in, tm=512):
    t, d = x.shape
    nt = t // tm

    def body(dpj_ref, dres_ref, x_ref, g_ref, win_hbm, dx_ref, dg_ref, win_v, sem):
        @pl.when(pl.program_id(0) == 0)
        def _():
            cp = pltpu.make_async_copy(win_hbm, win_v, sem)
            cp.start()
            dg_ref[...] = jnp.zeros_like(dg_ref)
            cp.wait()

        g = g_ref[...]
        dh = jnp.zeros((tm, d), F32)
        for j in range(N_SHARD):
            dh = dh + _dot_nt(dpj_ref[:, j * RET_IN_SHARD:(j + 1) * RET_IN_SHARD], win_v[j])
        _, xh, rx = _rms_fwd(x_ref[...], g)
        dg_ref[...] += _colsum(dh * xh)
        dx_ref[...] = dres_ref[...] + _rms_bwd(xh, rx, g, dh)

    row = pl.BlockSpec((tm, d), lambda i: (i, 0))
    return pl.pallas_call(
        body, grid=(nt,),
        in_specs=[pl.BlockSpec((tm, RET_IN), lambda i: (i, 0)), row, row, _full((1, d)), ANY],
        out_specs=[row, _full((1, d))],
        out_shape=[jax.ShapeDtypeStruct((t, d), F32), jax.ShapeDtypeStruct((1, d), F32)],
        scratch_shapes=[pltpu.VMEM(w_in.shape, BF16), pltpu.SemaphoreType.DMA],
        compiler_params=_params(1), name="ret_in_bwd")(dpj, dres, x, g_pre, w_in)


_CHIP_FLIPS = ((1, 0), (0, 1), (1, 1))


def _flip(v, b):
    return 1 - v if b else v


def scatter_grads(big, small):
    n = len(big)

    def body(*refs):
        ins, small_in = refs[:n], refs[n]
        outs, small_out = refs[n + 1:2 * n + 1], refs[2 * n + 1]
        send_sems, recv_sems, ssend_sems, srecv_sems, local_sem = refs[2 * n + 2:]
        x, y, c = lax.axis_index("x"), lax.axis_index("y"), lax.axis_index("c")
        mine = 4 * x + 2 * y + c
        copies = [pltpu.make_async_copy(small_in, small_out.at[mine], local_sem)]
        copies[0].start()
        for m in range(1, 8):
            bx, by, bc = (m >> 2) & 1, (m >> 1) & 1, m & 1
            cp = pltpu.make_async_remote_copy(
                src_ref=small_in, dst_ref=small_out.at[mine], send_sem=ssend_sems.at[m - 1],
                recv_sem=srecv_sems.at[m - 1], device_id=(_flip(x, bx), _flip(y, by), _flip(c, bc)),
                device_id_type=MESH)
            cp.start()
            copies.append(cp)
        for t in range(n):
            for k, (bx, by) in enumerate(_CHIP_FLIPS):
                px, py = _flip(x, bx), _flip(y, by)
                cp = pltpu.make_async_remote_copy(
                    src_ref=ins[t].at[2 * px + py], dst_ref=outs[t].at[k], send_sem=send_sems.at[3 * t + k],
                    recv_sem=recv_sems.at[3 * t + k], device_id=(px, py, c), device_id_type=MESH)
                cp.start()
                copies.append(cp)
        for cp in copies:
            cp.wait()

    return pl.pallas_call(
        body, in_specs=[ANY] * (n + 1), out_specs=[ANY] * (n + 1),
        out_shape=[jax.ShapeDtypeStruct((3,) + b.shape[1:], b.dtype) for b in big]
        + [jax.ShapeDtypeStruct((8,) + small.shape, small.dtype)],
        scratch_shapes=[pltpu.SemaphoreType.DMA((3 * n,)), pltpu.SemaphoreType.DMA((3 * n,)),
                        pltpu.SemaphoreType.DMA((7,)), pltpu.SemaphoreType.DMA((7,)), pltpu.SemaphoreType.DMA],
        name="scatter_grads")(*big, small)


_HBM = pl.BlockSpec(memory_space=pltpu.HBM)
_SEM = pl.BlockSpec(memory_space=pltpu.SEMAPHORE)
_EFFECT = pltpu.SideEffectType.DATAFLOW_SIDE_EFFECTING


def _chip_copies(mode, srcs, lands, send_sems, recv_sems):
    x, y, c = lax.axis_index("x"), lax.axis_index("y"), lax.axis_index("c")
    copies = []
    for t in range(len(lands)):
        if mode == "swap":
            copies.append(pltpu.make_async_remote_copy(
                src_ref=srcs[t], dst_ref=lands[t], send_sem=send_sems.at[t], recv_sem=recv_sems.at[t],
                device_id=(x, y, 1 - c), device_id_type=MESH))
            continue
        for k, (bx, by) in enumerate(_CHIP_FLIPS):
            px, py = _flip(x, bx), _flip(y, by)
            target = (px, py, c)
            if mode == "gather":
                src, dst = srcs[t], lands[t].at[2 * x + y]
            elif mode == "gather_half":
                half = pl.ds(c * (srcs[t].shape[0] // 2), srcs[t].shape[0] // 2)
                src, dst = srcs[t].at[half], lands[t].at[2 * x + y, half]
            elif mode == "forward_half":
                half = pl.ds(c * (lands[t].shape[1] // 2), lands[t].shape[1] // 2)
                src = dst = lands[t].at[2 * px + py, half]
                target = (x, y, 1 - c)
            else:
                src, dst = srcs[t].at[2 * px + py], lands[t].at[k]
            copies.append(pltpu.make_async_remote_copy(
                src_ref=src, dst_ref=dst, send_sem=send_sems.at[3 * t + k], recv_sem=recv_sems.at[3 * t + k],
                device_id=target, device_id_type=MESH))
    return copies


def exchange_start(mode, srcs, lands, name, after=None):
    n, ns = len(lands), len(srcs)
    extra = [] if after is None else [after]

    def body(*refs):
        ins, lnd = refs[:ns], refs[ns:ns + n]
        send_sems, recv_sems = refs[ns + n + len(extra)], refs[ns + n + len(extra) + 1]
        token = refs[-1]
        for cp in _chip_copies(mode, ins, lnd, send_sems, recv_sems):
            cp.start()
        token[...] = jnp.zeros(token.shape, token.dtype)

    hbm = lambda a: pltpu.with_memory_space_constraint(a, pltpu.HBM)
    passed = list(srcs) + list(lands)
    n_sem = n if mode == "swap" else 3 * n
    return pl.pallas_call(
        body, name=name,
        out_shape=(pltpu.SemaphoreType.DMA((n_sem,)), pltpu.SemaphoreType.DMA((n_sem,)),
                   *[pltpu.HBM(a.shape, a.dtype) for a in passed], jax.ShapeDtypeStruct((8, 128), F32)),
        in_specs=[_HBM] * (ns + n) + [ANY] * len(extra),
        out_specs=(_SEM, _SEM, *[_HBM] * (ns + n), pl.BlockSpec(memory_space=pltpu.VMEM)),
        input_output_aliases={i: 2 + i for i in range(ns + n)},
        compiler_params=pltpu.CompilerParams(has_side_effects=_EFFECT))(*[hbm(a) for a in passed], *extra)


def exchange_wait(mode, started, after, name):
    send_sems, recv_sems = started[0], started[1]
    passed = list(started[2:-1])
    n = len(passed) if mode == "forward_half" else len(passed) // 2
    ns = len(passed) - n

    def body(*refs):
        ins, lnd = refs[:ns], refs[ns:ns + n]
        for cp in _chip_copies(mode, ins, lnd, refs[ns + n], refs[ns + n + 1]):
            cp.wait_send()
            cp.wait_recv()

    outs = pl.pallas_call(
        body, name=name, out_shape=tuple(pltpu.HBM(a.shape, a.dtype) for a in passed),
        in_specs=[_HBM] * (ns + n) + [_SEM, _SEM, ANY], out_specs=tuple([_HBM] * (ns + n)),
        input_output_aliases={i: i for i in range(ns + n)},
        compiler_params=pltpu.CompilerParams(has_side_effects=_EFFECT))(*passed, send_sems, recv_sems, after)
    return list(outs[ns:])


def plane_sum(slot, full, recv, name, bm=256):
    _, m, n = full.shape
    bm = _row_block(m, bm)

    def body(slot_ref, o_ref, r_ref, s_ref):
        s_ref[...] = ((o_ref[...] + r_ref[0].astype(F32)) + r_ref[1].astype(F32)) + r_ref[2].astype(F32)

    return pl.pallas_call(
        body,
        grid_spec=pltpu.PrefetchScalarGridSpec(
            num_scalar_prefetch=1, grid=(m // bm,),
            in_specs=[pl.BlockSpec((None, bm, n), lambda i, s: (s[0], i, 0)),
                      pl.BlockSpec((3, bm, n), lambda i, s: (0, i, 0))],
            out_specs=pl.BlockSpec((bm, n), lambda i, s: (i, 0))),
        out_shape=jax.ShapeDtypeStruct((m, n), F32), compiler_params=_params(1), name=name)(slot, full, recv)


def sum_slots(parts, name, bm=312):
    _, r, n = parts.shape
    bm = bm if r % bm == 0 else r

    def body(p_ref, s_ref):
        acc = p_ref[0]
        for k in range(1, 8):
            acc = acc + p_ref[k]
        s_ref[...] = acc

    return pl.pallas_call(
        body, grid=(r // bm,), in_specs=[pl.BlockSpec((8, bm, n), lambda i: (0, i, 0))],
        out_specs=pl.BlockSpec((bm, n), lambda i: (i, 0)), out_shape=jax.ShapeDtypeStruct((r, n), F32),
        compiler_params=_params(1), name=name)(parts)


def _adamw_math(w, g, m, v):
    m = ADAM_B1 * m + (1.0 - ADAM_B1) * g
    v = ADAM_B2 * v + (1.0 - ADAM_B2) * (g * g)
    m_hat = m / (1.0 - ADAM_B1 ** ADAM_STEP)
    v_hat = v / (1.0 - ADAM_B2 ** ADAM_STEP)
    delta = -ADAM_LR * (m_hat / (jnp.sqrt(v_hat) + ADAM_EPS) + ADAM_WD * w)
    return delta, m, v


def adamw(w, m, v, grads, layer, prev, name, bm=256, row0=0):
    _, _, n = w.shape
    mm = grads[0].shape[0]
    bm = _row_block(mm, bm)
    first = row0 // bm
    ng = len(grads)

    def body(*refs):
        w_ref, m_ref, v_ref = refs[:3]
        g_refs = refs[3:3 + ng]
        g_out, d_out, m_out, v_out = refs[-4:]
        g = g_refs[0][...]
        for gr in g_refs[1:]:
            g = g + gr[...]
        delta, mn, vn = _adamw_math(w_ref[...], g, m_ref[...], v_ref[...])
        g_out[...] = g
        d_out[...] = delta
        m_out[...] = mn
        v_out[...] = vn

    slab = pl.BlockSpec((None, bm, n), lambda i: (layer, first + i, 0))
    flat = pl.BlockSpec((bm, n), lambda i: (i, 0))
    in_specs = [slab] * 3 + [flat] * ng
    args = [w, m, v, *grads]
    aliases = {}
    if prev is not None:
        in_specs += [ANY] * 4
        aliases = {3 + ng + q: q for q in range(4)}
        args += list(prev)
    return pl.pallas_call(
        body, grid=(mm // bm,), in_specs=in_specs, out_specs=[slab] * 4,
        out_shape=[jax.ShapeDtypeStruct(w.shape, F32)] * 4, input_output_aliases=aliases,
        compiler_params=_params(1), name=name)(*args)


def _pack_rows(parts, rows):
    flat = jnp.concatenate([p.reshape(-1) for p in parts])
    return jnp.pad(flat, (0, rows * 128 - flat.shape[0])).reshape(rows, 128)


def _as_shards(a, rows):
    return a.reshape(N_SHARD, rows, a.shape[-1])


def _local_step(x, pos_col, target, gains, pool_w, pool_scale, gn_gain, conv_w, conv_b, weights, send_grads):
    def gain(l, n, token=None):
        g = gains[l, n].reshape(1, D_MODEL)
        return g if token is None else g + token[0:1, 0:1]

    inv_freq = (ROPE_BASE ** (-jnp.arange(0, RET_QK, 2, dtype=F32) / RET_QK)).reshape(1, RET_QK // 2)
    intra, cross, inner, chunk_decay = _decay_tables()
    dn_rows = D_FF // N_SHARD

    x1 = pool_fwd(x, gain(0, 0), gain(0, 1), pool_w, pool_scale)
    cos, sin = rope_tables(pos_col, inv_freq, x1)
    w_up0, w_dn0 = weights("mlp0", cos)
    w_dn0 = w_dn0.reshape(D_FF, D_MODEL)
    x2, u0, s0, a0, f0, h0 = mlp_fwd(x1, gain(0, 2), gain(0, 3), w_up0, w_dn0, conv_w[0], conv_b[0])
    w_in, w_out = weights("ret", x2)
    w_out = w_out.reshape(RET_HEADS * RET_V, D_MODEL)
    pj, hr = ret_proj(x2, gain(1, 0), w_in, cos, sin)
    o, sprev = ret_core_fwd(pj, intra, cross, inner, chunk_decay)
    x3, yb, out = ret_out_fwd(o, pj, x2, gn_gain, gain(1, 1), w_out)
    w_up1, w_dn1 = weights("mlp1", x3)
    w_dn1 = w_dn1.reshape(D_FF, D_MODEL)
    dx4, u1, s1, a1, f1, h1, loss = mlp_fwd(x3, gain(1, 2), gain(1, 3), w_up1, w_dn1, conv_w[1], conv_b[1], target)

    dx3, du1, df1, dg12, dg13, dcw1, dcb1 = mlp_bwd(
        dx4, f1, x3, u1, s1, gain(1, 2), gain(1, 3), w_up1, w_dn1, conv_w[1])
    dwup1 = grad_matmul(h1, du1, D_MODEL, FF_CHUNK, "grad_w_up_1", interleaved=True)
    dwdn1 = grad_matmul(a1, df1, FF_CHUNK, D_MODEL, "grad_w_down_1")
    tok = send_grads("mlp1", [dwup1, [_as_shards(g, dn_rows) for g in dwdn1]])
    dout, dpj, do, dg11, dgn = ret_out_bwd(dx3, out, o, pj, gn_gain, gain(1, 1, tok), w_out)
    dwout = grad_matmul(yb, dout, 1024, D_MODEL, "grad_w_out")
    dpj = ret_core_bwd(pj, do, sprev, cos, sin, dpj, intra, cross, inner, chunk_decay)
    dwin = grad_matmul(hr, dpj, D_MODEL, RET_IN_SHARD, "grad_w_in")
    tok = send_grads("ret", [dwin, [_as_shards(g, RET_V) for g in dwout]])
    dx2, dg10 = ret_in_bwd(dpj, dx3, x2, gain(1, 0, tok), w_in)
    dx1, du0, df0, dg02, dg03, dcw0, dcb0 = mlp_bwd(
        dx2, f0, x1, u0, s0, gain(0, 2), gain(0, 3), w_up0, w_dn0, conv_w[0])
    dwdn0 = grad_matmul(a0, df0, FF_CHUNK, D_MODEL, "grad_w_down_0")
    tok = send_grads("down0", [[_as_shards(g, dn_rows) for g in dwdn0]])
    half = D_MODEL // 2
    for part, first in (("a", 0), ("b", half)):
        dwup0 = grad_matmul(h0, du0, half, FF_CHUNK, "grad_w_up_0" + part, tk=4096, interleaved=True, after=tok,
                            cols=(first, half))
        tok = send_grads("up0" + part, [dwup0])
    dx0, dg00, dg01, dpscale, dpw = pool_bwd(dx1, x, gain(0, 0, tok), gain(0, 1), pool_w, pool_scale)

    rows = lambda g: jnp.sum(g, axis=0, keepdims=True)
    dgains = jnp.concatenate([dg00, dg01, rows(dg02), rows(dg03), dg10, dg11, rows(dg12), rows(dg13)],
                             axis=0).reshape(2, 4, D_MODEL)
    small = {"gains": dgains, "pool_scale": dpscale, "gn": dgn,
             "conv_w": jnp.sum(jnp.stack([dcw0, dcw1]), axis=3),
             "conv_b": jnp.sum(jnp.stack([dcb0, dcb1]), axis=2, keepdims=True), "pool_w": dpw}
    return loss, dx0, small


def kernel(x, positions, norm_gain, pool_w, pool_scale, ret_w_in, ret_gn_gain, ret_w_out, mlp_w_up, mlp_conv_w, mlp_conv_b, mlp_w_down, loss_target, m_norm_gain, m_pool_w, m_pool_scale, m_ret_w_in, m_ret_gn_gain, m_ret_w_out, m_mlp_w_up, m_mlp_conv_w, m_mlp_conv_b, m_mlp_w_down, v_norm_gain, v_pool_w, v_pool_scale, v_ret_w_in, v_ret_gn_gain, v_ret_w_out, v_mlp_w_up, v_mlp_conv_w, v_mlp_conv_b, v_mlp_w_down):
    t = x.shape[1]
    me = 2 * lax.axis_index("x") + lax.axis_index("y")
    me_slot = jnp.reshape(me, (1,)).astype(jnp.int32)

    small_parts = [norm_gain, ret_gn_gain, mlp_conv_w, pool_w]
    small_sizes = [p.size for p in small_parts]
    small_rows = -(-sum(small_sizes) // (128 * 8)) * 8
    groups = {"small": [_pack_rows(small_parts, small_rows)],
              "mlp0": [mlp_w_up[0].astype(BF16), mlp_w_down[0].astype(BF16)],
              "ret": [ret_w_in[0].astype(BF16), ret_w_out[0].astype(BF16)],
              "mlp1": [mlp_w_up[1].astype(BF16), mlp_w_down[1].astype(BF16)]}
    gathers, token = {}, None
    for group, srcs in groups.items():
        lands = [lax.dynamic_update_index_in_dim(lax.empty((N_SHARD,) + s.shape, s.dtype), s, me, 0) for s in srcs]
        mode = "gather_half" if group == "mlp0" else "gather"
        gathers[group] = (mode, exchange_start(mode, srcs, lands, "gather_start_" + group, after=token))
        token = gathers[group][1][-1]

    def weights(group, after):
        mode, started = gathers[group]
        lands = exchange_wait(mode, started, after, "gather_wait_" + group)
        if mode == "gather_half":
            forward = exchange_start("forward_half", [], lands, "forward_start_" + group)
            lands = exchange_wait("forward_half", forward, forward[-1], "forward_wait_" + group)
        return lands

    sent, early = {}, {}

    def reduced(group, after, names):
        started, own = sent[group]
        recv = exchange_wait("scatter", started, after, "scatter_wait_" + group)
        return [plane_sum(me_slot, f, r, "plane_sum_" + nm)
                for f, r, nm in zip(own, recv, names)]

    def swap_start(planes, name):
        return exchange_start("swap", planes, [lax.empty(p.shape, p.dtype) for p in planes], name)

    def send_grads(group, pairs):
        lands = [lax.empty((3,) + b.shape[1:], BF16) for _, b in pairs]
        sent[group] = (exchange_start("scatter", [b for _, b in pairs], lands, "scatter_start_" + group),
                       [f for f, _ in pairs])
        token = sent[group][0][-1]
        if group == "down0":
            marker = pairs[0][1]
            early["planes"] = (reduced("mlp1", marker, ["w_up_1", "w_down_1"])
                               + reduced("ret", marker, ["w_in", "w_out"]))
            early["swap"] = swap_start(early["planes"], "swap_start_a")
            token = token + early["swap"][-1]
        return token

    (smallg,) = weights("small", token)
    smallg = smallg.reshape(N_SHARD, -1)
    offs = np.cumsum([0] + small_sizes)
    piece = lambda i, shape: smallg[:, offs[i]:offs[i + 1]].reshape((N_SHARD,) + shape)
    gains = piece(0, (2, 4, 256)).transpose(1, 2, 0, 3).reshape(2, 4, D_MODEL)
    gn_full = piece(1, (512,)).reshape(1, RET_HEADS * RET_V)
    cw_full = piece(2, (2, 3, FF_CHUNK)).transpose(1, 0, 2, 3)
    pw_full = piece(3, (4, 64, 256)).transpose(1, 0, 2, 3).reshape(4, 256, 256).astype(BF16)
    cb_full = mlp_conv_b.reshape(2, N_SHARD, 1, FF_CHUNK)

    loss, dx0, small = _local_step(
        x[0], positions.reshape(t, 1).astype(F32), loss_target[0], gains, pw_full, pool_scale, gn_full,
        cw_full, cb_full, weights, send_grads)

    def small_adamw(w, m, v, grads, name):
        w3 = w.reshape(1, -1, w.shape[-1])
        out = adamw(w3, m.reshape(w3.shape), v.reshape(w3.shape), [g.reshape(w3.shape[1:]) for g in grads], 0, None, name)
        return [o.reshape(w.shape) for o in out]

    res = {}
    planes_a = early["planes"]
    others_a = exchange_wait("swap", early["swap"], dx0, "swap_wait_a")
    res["ret_w_in"] = adamw(ret_w_in, m_ret_w_in, v_ret_w_in, (planes_a[2], others_a[2]), 0, None, "adamw_w_in")
    res["ret_w_out"] = adamw(ret_w_out, m_ret_w_out, v_ret_w_out, (planes_a[3], others_a[3]), 0, None, "adamw_w_out")
    up1 = adamw(mlp_w_up, m_mlp_w_up, v_mlp_w_up, (planes_a[0], others_a[0]), 1, None, "adamw_w_up_1")
    dn1 = adamw(mlp_w_down, m_mlp_w_down, v_mlp_w_down, (planes_a[1], others_a[1]), 1, None, "adamw_w_down_1")

    pw_f = small["pool_w"].reshape(4, N_SHARD, 64, 256).transpose(1, 0, 2, 3).reshape(N_SHARD, 256, 256)
    small_order = ["gains", "pool_scale", "gn", "conv_w", "conv_b"]
    gsmall_sizes = [small[k].size for k in small_order]
    gsmall_rows = -(-sum(gsmall_sizes) // (128 * 8)) * 8
    planes_b = (reduced("up0a", dn1[0], ["w_up_0a"]) + reduced("up0b", dn1[0], ["w_up_0b"])
                + reduced("down0", dn1[0], ["w_down_0"]))
    pw_recv, small_recv = scatter_grads([pw_f.astype(BF16)], _pack_rows([small[k] for k in small_order], gsmall_rows))
    planes_b.append(plane_sum(me_slot, pw_f, pw_recv, "plane_sum_pool_w"))
    swap_b = swap_start(planes_b, "swap_start_b")

    gsmall = sum_slots(small_recv, "sum_small").reshape(-1)
    goffs = np.cumsum([0] + gsmall_sizes)
    gpiece = lambda i: gsmall[goffs[i]:goffs[i + 1]].reshape(small[small_order[i]].shape)
    g_gains = lax.dynamic_slice_in_dim(gpiece(0), me * 256, 256, axis=2)
    g_gn = lax.dynamic_slice_in_dim(gpiece(2), me * RET_V, RET_V, axis=1)
    g_cw = lax.dynamic_index_in_dim(gpiece(3), me, 1, keepdims=False)
    res["norm_gain"] = small_adamw(norm_gain, m_norm_gain, v_norm_gain, [g_gains], "adamw_norm_gain")
    res["pool_scale"] = small_adamw(pool_scale, m_pool_scale, v_pool_scale, [gpiece(1)], "adamw_pool_scale")
    res["ret_gn_gain"] = small_adamw(ret_gn_gain, m_ret_gn_gain, v_ret_gn_gain, [g_gn], "adamw_gn_gain")
    res["mlp_conv_w"] = small_adamw(mlp_conv_w, m_mlp_conv_w, v_mlp_conv_w, [g_cw], "adamw_conv_w")
    res["mlp_conv_b"] = small_adamw(mlp_conv_b, m_mlp_conv_b, v_mlp_conv_b, [gpiece(4)], "adamw_conv_b")

    others_b = exchange_wait("swap", swap_b, res["mlp_conv_b"][0], "swap_wait_b")
    up0a = adamw(mlp_w_up, m_mlp_w_up, v_mlp_w_up, (planes_b[0], others_b[0]), 0, up1, "adamw_w_up_0a")
    res["mlp_w_up"] = adamw(mlp_w_up, m_mlp_w_up, v_mlp_w_up, (planes_b[1], others_b[1]), 0, up0a, "adamw_w_up_0b",
                            row0=D_MODEL // 2)
    res["mlp_w_down"] = adamw(mlp_w_down, m_mlp_w_down, v_mlp_w_down, (planes_b[2], others_b[2]), 0, dn1,
                              "adamw_w_down_0")
    res["pool_w"] = small_adamw(pool_w, m_pool_w, v_pool_w, (planes_b[3], others_b[3]), "adamw_pool_w")

    order = ["norm_gain", "pool_w", "pool_scale", "ret_w_in", "ret_gn_gain", "ret_w_out", "mlp_w_up", "mlp_conv_w",
             "mlp_conv_b", "mlp_w_down"]
    total_loss = lax.psum(loss[0, 0], ("x", "y", "c"))
    outs = [total_loss, dx0.reshape(x.shape)]
    for q in range(4):
        outs += [res[k][q] for k in order]
    return tuple(outs)
```

```python
import numpy as np
import jax
import jax.numpy as jnp
from jax import lax
from jax.experimental import pallas as pl
from jax.experimental.pallas import tpu as pltpu

F32 = jnp.float32
BF16 = jnp.bfloat16

D_MODEL = 1024
D_FF = 2816
FF_CHUNK = 1408
N_SHARD = 4
POOL_WINDOWS = (2, 4, 8, 16)
POOL_DIM = 256
POOL_HALO = 16
RET_HEADS = 4
RET_QK = 256
RET_V = 512
RET_CHUNK = 256
RET_IN = 6144
RET_IN_SHARD = 1536
ROPE_BASE = 10000.0
EPS = 1e-6
CONV_HALO = 8

ADAM_LR, ADAM_B1, ADAM_B2, ADAM_EPS, ADAM_WD, ADAM_STEP = 0.001, 0.9, 0.999, 1e-08, 0.01, 10

VMEM_LIMIT = 56 * 1024 * 1024
VMEM_LIMIT_MLP_BWD = 62 * 1024 * 1024
MESH = pl.DeviceIdType.MESH
ANY = pl.BlockSpec(memory_space=pl.ANY)


def _params(n_grid=1, limit=VMEM_LIMIT):
    return pltpu.CompilerParams(dimension_semantics=("arbitrary",) * n_grid, vmem_limit_bytes=limit)


def _dot(a, b):
    return jnp.dot(a, b, preferred_element_type=F32)


def _dot_nt(a, b):
    return lax.dot_general(a, b, (((1,), (1,)), ((), ())), preferred_element_type=F32)


def _dot_tn(a, b):
    return lax.dot_general(a, b, (((0,), (0,)), ((), ())), preferred_element_type=F32)


def _rms_fwd(x, gain):
    r = lax.rsqrt(jnp.mean(x * x, axis=-1, keepdims=True) + EPS)
    xh = x * r
    return xh * gain, xh, r


def _rms_bwd(xh, r, gain, dy):
    dxh = dy * gain
    return r * (dxh - xh * jnp.mean(dxh * xh, axis=-1, keepdims=True))


def _colsum(v):
    return jnp.sum(v, axis=0, keepdims=True)


def _full(shape):
    nd = len(shape)
    return pl.BlockSpec(shape, lambda *_: (0,) * nd)


def rope_tables(pos_col, inv_freq, after):
    t = pos_col.shape[0]
    tm = min(t, 1024)

    def body(p_ref, f_ref, after_ref, c_ref, s_ref):
        ang = p_ref[...] * f_ref[...]
        c_ref[...] = jnp.cos(ang)
        s_ref[...] = jnp.sin(ang)

    return pl.pallas_call(
        body, grid=(t // tm,),
        in_specs=[pl.BlockSpec((tm, 1), lambda i: (i, 0)), _full((1, 128)), ANY],
        out_specs=[pl.BlockSpec((tm, 128), lambda i: (i, 0))] * 2,
        out_shape=[jax.ShapeDtypeStruct((t, 128), F32)] * 2,
        compiler_params=_params(1), name="rope_tables")(pos_col, inv_freq, after)


def _window_sums(ext, backward):
    n = ext.shape[0]
    cur, sums = ext, []
    for g, win in enumerate(POOL_WINDOWS):
        if g > 0:
            cur = cur[:, POOL_DIM:]
        half = win // 2
        cur = cur + pltpu.roll(cur, n - half if backward else half, axis=0)
        sums.append(cur[:, 0:POOL_DIM])
    return sums


def _pool_diff(h_halo, h, row0, tm):
    t_idx = row0 + lax.broadcasted_iota(jnp.int32, (tm, 1), 0)
    sums = _window_sums(jnp.concatenate([h_halo, h], axis=0), backward=False)
    parts, inv_counts = [], []
    for g, win in enumerate(POOL_WINDOWS):
        inv = 1.0 / jnp.minimum(t_idx + 1, win).astype(F32)
        parts.append(sums[g][POOL_HALO:, :] * inv - h[:, g * POOL_DIM:(g + 1) * POOL_DIM])
        inv_counts.append(inv)
    return parts, inv_counts


def pool_fwd(x, g_pre, g_post, pool_w, pool_scale, tm=512):
    t, d = x.shape
    nt = t // tm

    def body(x_ref, g0_ref, g1_ref, w_ref, sc_ref, o_ref, hext):
        i = pl.program_id(0)

        @pl.when(i == 0)
        def _():
            hext[...] = jnp.zeros((POOL_HALO, d), F32)

        xv = x_ref[...]
        h, _, _ = _rms_fwd(xv, g0_ref[...])
        parts, _ = _pool_diff(hext[...], h, i * tm, tm)
        hext[...] = h[tm - POOL_HALO:tm, :]
        ys = [_dot(parts[g].astype(BF16), w_ref[g]) for g in range(len(POOL_WINDOWS))]
        y = jnp.concatenate(ys, axis=-1) * sc_ref[...]
        m, _, _ = _rms_fwd(y, g1_ref[...])
        o_ref[...] = xv + m

    row = pl.BlockSpec((tm, d), lambda i: (i, 0))
    return pl.pallas_call(
        body, grid=(nt,),
        in_specs=[row, _full((1, d)), _full((1, d)), _full(pool_w.shape), _full((1, d))],
        out_specs=row, out_shape=jax.ShapeDtypeStruct((t, d), F32),
        scratch_shapes=[pltpu.VMEM((POOL_HALO, d), F32)],
        compiler_params=_params(1), name="pool_fwd")(x, g_pre, g_post, pool_w, pool_scale)


def pool_bwd(dx1, x, g_pre, g_post, pool_w, pool_scale, tm=512):
    t, d = x.shape
    nt = t // tm
    ng = len(POOL_WINDOWS)

    def body(dx1_ref, x_ref, xh_ref, g0_ref, g1_ref, w_ref, sc_ref,
             dx_ref, dg0_ref, dg1_ref, dsc_ref, dw_ref, enext):
        i = pl.program_id(0)
        r = nt - 1 - i

        @pl.when(i == 0)
        def _():
            enext[...] = jnp.zeros((POOL_HALO, d), F32)
            dg0_ref[...] = jnp.zeros_like(dg0_ref)
            dg1_ref[...] = jnp.zeros_like(dg1_ref)
            dsc_ref[...] = jnp.zeros_like(dsc_ref)
            dw_ref[...] = jnp.zeros_like(dw_ref)

        g0 = g0_ref[...]
        g1 = g1_ref[...]
        sc = sc_ref[...]
        xv = x_ref[...]
        h, xh, rx = _rms_fwd(xv, g0)
        h_halo, _, _ = _rms_fwd(xh_ref[...], g0)
        parts, inv_counts = _pool_diff(h_halo * jnp.where(r > 0, 1.0, 0.0), h, r * tm, tm)
        parts_b = [p.astype(BF16) for p in parts]
        ypre = jnp.concatenate([_dot(parts_b[g], w_ref[g]) for g in range(ng)], axis=-1)
        _, yh, ry = _rms_fwd(ypre * sc, g1)
        dm = dx1_ref[...]
        dg1_ref[...] += _colsum(dm * yh)
        dy = _rms_bwd(yh, ry, g1, dm)
        dsc_ref[...] += _colsum(dy * ypre)
        dyp = (dy * sc).astype(BF16)
        ddiffs = []
        for g in range(ng):
            cols = slice(g * POOL_DIM, (g + 1) * POOL_DIM)
            dw_ref[g] += _dot_tn(parts_b[g], dyp[:, cols])
            ddiffs.append(_dot_nt(dyp[:, cols], w_ref[g]))
        e = jnp.concatenate([ddiffs[g] * inv_counts[g] for g in range(ng)], axis=-1)
        sums = _window_sums(jnp.concatenate([e, enext[...]], axis=0), backward=True)
        enext[...] = e[0:POOL_HALO, :]
        dh = jnp.concatenate([sums[g][0:tm, :] - ddiffs[g] for g in range(ng)], axis=-1)
        dg0_ref[...] += _colsum(dh * xh)
        dx_ref[...] = dm + _rms_bwd(xh, rx, g0, dh)

    row = pl.BlockSpec((tm, d), lambda i: (nt - 1 - i, 0))
    halo = pl.BlockSpec((POOL_HALO, d), lambda i: (jnp.maximum((nt - 1 - i) * (tm // POOL_HALO) - 1, 0), 0))
    vec = _full((1, d))
    return pl.pallas_call(
        body, grid=(nt,),
        in_specs=[row, row, halo, vec, vec, _full(pool_w.shape), vec],
        out_specs=[row, vec, vec, vec, _full((ng, POOL_DIM, POOL_DIM))],
        out_shape=[jax.ShapeDtypeStruct((t, d), F32)] + [jax.ShapeDtypeStruct((1, d), F32)] * 3
        + [jax.ShapeDtypeStruct((ng, POOL_DIM, POOL_DIM), F32)],
        scratch_shapes=[pltpu.VMEM((POOL_HALO, d), F32)],
        compiler_params=_params(1), name="pool_bwd")(dx1, x, x, g_pre, g_post, pool_w, pool_scale)


def _conv_taps(cw_ref, j):
    return cw_ref[j, 0:1, :], cw_ref[j, 1:2, :], cw_ref[j, 2:3, :]


def _row_block(m, target=256):
    if m <= target:
        return m
    for b in range(target, 7, -8):
        if m % b == 0:
            return b
    return m


def mlp_fwd(x, g_pre, g_post, w_up, w_down, conv_w, conv_b, target=None, tm=256):
    t, d = x.shape
    nt = t // tm
    h8 = CONV_HALO
    with_loss = target is not None
    n_extra = 1 if with_loss else 0

    def body(x_ref, g2_ref, g3_ref, wup_hbm, wdn_hbm, cw_ref, cb_ref, *rest):
        tgt_ref = rest[0] if with_loss else None
        xo_ref, u_ref, s_ref, a_ref, f_ref, h_ref = rest[n_extra:n_extra + 6]
        loss_ref = rest[n_extra + 6] if with_loss else None
        wup_v, wdn_v, tail, sem = rest[-4:]
        i = pl.program_id(0)

        @pl.when(i == 0)
        def _():
            c1 = pltpu.make_async_copy(wup_hbm, wup_v, sem.at[0])
            c2 = pltpu.make_async_copy(wdn_hbm, wdn_v, sem.at[1])
            c1.start()
            c2.start()
            tail[...] = jnp.zeros_like(tail)
            if with_loss:
                loss_ref[...] = jnp.zeros_like(loss_ref)
            c1.wait()
            c2.wait()

        xv = x_ref[...]
        h, _, _ = _rms_fwd(xv, g2_ref[...])
        hb = h.astype(BF16)
        h_ref[...] = hb
        acc = jnp.zeros((tm, d), F32)
        for k in range(2):
            cs = []
            for s in range(2):
                j, cols = k + 2 * s, slice((2 * k + s) * FF_CHUNK, (2 * k + s + 1) * FF_CHUNK)
                uf = _dot(hb, wup_v[j])
                u_ref[:, cols] = uf.astype(BF16)
                ext = jnp.concatenate([tail[j], uf], axis=0)
                tail[j] = uf[tm - h8:tm, :]
                w0, w1, w2 = _conv_taps(cw_ref, j)
                cs.append(cb_ref[j] + w2 * uf + w1 * pltpu.roll(ext, 1, axis=0)[h8:, :]
                          + w0 * pltpu.roll(ext, 2, axis=0)[h8:, :])
            cg, cv = cs
            sg = jax.nn.sigmoid(cg)
            sil = cg * sg
            ab = (sil * cv).astype(BF16)
            a_ref[:, k * FF_CHUNK:(k + 1) * FF_CHUNK] = ab
            s_ref[:, 2 * k * FF_CHUNK:(2 * k + 1) * FF_CHUNK] = sil.astype(BF16)
            s_ref[:, (2 * k + 1) * FF_CHUNK:(2 * k + 2) * FF_CHUNK] = (cv * (sg * (1.0 + cg * (1.0 - sg)))).astype(BF16)
            acc = acc + _dot(ab, wdn_v[k * FF_CHUNK:(k + 1) * FF_CHUNK, :])
        f_ref[...] = acc
        y, _, _ = _rms_fwd(acc, g3_ref[...])
        if with_loss:
            err = (xv + y) - tgt_ref[...]
            xo_ref[...] = err * (1.0 / d)
            loss_ref[...] += 0.5 * jnp.sum(jnp.mean(err * err, axis=-1, keepdims=True), axis=0, keepdims=True)
        else:
            xo_ref[...] = xv + y

    row = pl.BlockSpec((tm, d), lambda i: (i, 0))
    wide = pl.BlockSpec((tm, 2 * D_FF), lambda i: (i, 0))
    vec = _full((1, d))
    extra = [target] if with_loss else []
    return pl.pallas_call(
        body, grid=(nt,),
        in_specs=[row, vec, vec, ANY, ANY, _full(conv_w.shape), _full(conv_b.shape)] + [row] * n_extra,
        out_specs=[row, wide, wide, pl.BlockSpec((tm, D_FF), lambda i: (i, 0)), row, row] + [_full((1, 1))] * n_extra,
        out_shape=[jax.ShapeDtypeStruct((t, d), F32), jax.ShapeDtypeStruct((t, 2 * D_FF), BF16),
                   jax.ShapeDtypeStruct((t, 2 * D_FF), BF16), jax.ShapeDtypeStruct((t, D_FF), BF16),
                   jax.ShapeDtypeStruct((t, d), F32), jax.ShapeDtypeStruct((t, d), BF16)]
        + [jax.ShapeDtypeStruct((1, 1), F32)] * n_extra,
        scratch_shapes=[pltpu.VMEM(w_up.shape, BF16), pltpu.VMEM(w_down.shape, BF16),
                        pltpu.VMEM((N_SHARD, h8, FF_CHUNK), F32), pltpu.SemaphoreType.DMA((2,))],
        compiler_params=_params(1), name="mlp_fwd_loss" if with_loss else "mlp_fwd")(
            x, g_pre, g_post, w_up, w_down, conv_w, conv_b, *extra)


def _rowsum8(v):
    return jnp.sum(v.reshape(v.shape[0] // 8, 8, v.shape[1]), axis=0)


def mlp_bwd(dxo, f, x, u, sp, g_pre, g_post, w_up, w_down, conv_w, tm=256):
    t, d = x.shape
    nt = t // tm
    h8 = CONV_HALO

    def body(dxo_ref, f_ref, x_ref, u_ref, s_ref, g2_ref, g3_ref, wup_hbm, wdn_hbm, cw_ref,
             dx_ref, du_ref, df_ref, dg2_ref, dg3_ref, dcw_ref, dcb_ref,
             wup_v, wdn_v, carry, sem):
        @pl.when(pl.program_id(0) == 0)
        def _():
            c1 = pltpu.make_async_copy(wup_hbm, wup_v, sem.at[0])
            c2 = pltpu.make_async_copy(wdn_hbm, wdn_v, sem.at[1])
            c1.start()
            c2.start()
            carry[...] = jnp.zeros_like(carry)
            dg2_ref[...] = jnp.zeros_like(dg2_ref)
            dg3_ref[...] = jnp.zeros_like(dg3_ref)
            dcw_ref[...] = jnp.zeros_like(dcw_ref)
            dcb_ref[...] = jnp.zeros_like(dcb_ref)
            c1.wait()
            c2.wait()

        g3 = g3_ref[...]
        dxo = dxo_ref[...]
        _, fh, rf = _rms_fwd(f_ref[...], g3)
        dg3_ref[...] += _rowsum8(dxo * fh)
        dfb = _rms_bwd(fh, rf, g3, dxo).astype(BF16)
        df_ref[...] = dfb
        dh = jnp.zeros((tm, d), F32)
        for k in range(2):
            da = _dot_nt(dfb, wdn_v[k * FF_CHUNK:(k + 1) * FF_CHUNK, :])
            for s in range(2):
                j = k + 2 * s
                cols = slice((2 * k + s) * FF_CHUNK, (2 * k + s + 1) * FF_CHUNK)
                dc = da * s_ref[:, (2 * k + 1 - s) * FF_CHUNK:(2 * k + 2 - s) * FF_CHUNK].astype(F32)
                uf = u_ref[:, cols].astype(F32)
                ext = jnp.concatenate([dc, carry[j]], axis=0)
                carry[j] = dc[0:h8, :]
                dc1 = pltpu.roll(ext, tm + h8 - 1, axis=0)[0:tm, :]
                dc2 = pltpu.roll(ext, tm + h8 - 2, axis=0)[0:tm, :]
                dcb_ref[j] += _rowsum8(dc)
                dcw_ref[j, 2] += _rowsum8(dc * uf)
                dcw_ref[j, 1] += _rowsum8(dc1 * uf)
                dcw_ref[j, 0] += _rowsum8(dc2 * uf)
                dub = (cw_ref[j, 2:3, :] * dc + cw_ref[j, 1:2, :] * dc1 + cw_ref[j, 0:1, :] * dc2).astype(BF16)
                du_ref[:, cols] = dub
                dh = dh + _dot_nt(dub, wup_v[j])
        g2 = g2_ref[...]
        _, xh, rx = _rms_fwd(x_ref[...], g2)
        dg2_ref[...] += _rowsum8(dh * xh)
        dx_ref[...] = dxo + _rms_bwd(xh, rx, g2, dh)

    row = pl.BlockSpec((tm, d), lambda i: (nt - 1 - i, 0))
    wide = pl.BlockSpec((tm, 2 * D_FF), lambda i: (nt - 1 - i, 0))
    vec = _full((1, d))
    acc = _full((8, d))
    dcw_shape, dcb_shape = (N_SHARD, 3, 8, FF_CHUNK), (N_SHARD, 8, FF_CHUNK)
    return pl.pallas_call(
        body, grid=(nt,),
        in_specs=[row, row, row, wide, wide, vec, vec, ANY, ANY, _full(conv_w.shape)],
        out_specs=[row, wide, row, acc, acc, _full(dcw_shape), _full(dcb_shape)],
        out_shape=[jax.ShapeDtypeStruct((t, d), F32), jax.ShapeDtypeStruct((t, 2 * D_FF), BF16),
                   jax.ShapeDtypeStruct((t, d), BF16),
                   jax.ShapeDtypeStruct((8, d), F32), jax.ShapeDtypeStruct((8, d), F32),
                   jax.ShapeDtypeStruct(dcw_shape, F32), jax.ShapeDtypeStruct(dcb_shape, F32)],
        scratch_shapes=[pltpu.VMEM(w_up.shape, BF16), pltpu.VMEM(w_down.shape, BF16),
                        pltpu.VMEM((N_SHARD, h8, FF_CHUNK), F32), pltpu.SemaphoreType.DMA((2,))],
        compiler_params=_params(1, VMEM_LIMIT_MLP_BWD), name="mlp_bwd")(
            dxo, f, x, u, sp, g_pre, g_post, w_up, w_down, conv_w)


def grad_matmul(a, b, bm, bn, name, tk=2048, interleaved=False, after=None, cols=None):
    t = a.shape[0]
    m0, m = (0, a.shape[1]) if cols is None else cols
    n = b.shape[1]
    tk = min(tk, t)
    nk = t // tk
    place = (lambda j: (j % 2) * 2 + j // 2) if interleaved else (lambda j: j)
    extra = [] if after is None else [after]
    first = m0 // bm

    def body(a_ref, b_ref, *rest):
        o_ref, ob_ref = rest[len(extra):]
        kk = pl.program_id(2)

        @pl.when(kk == 0)
        def _():
            o_ref[...] = jnp.zeros_like(o_ref)

        o_ref[...] += _dot_tn(a_ref[...], b_ref[...])

        @pl.when(kk == nk - 1)
        def _():
            ob_ref[...] = o_ref[...].astype(BF16)

    ospec = pl.BlockSpec((None, bm, bn), lambda j, i, kk: (place(j), i, 0))
    return pl.pallas_call(
        body, grid=(n // bn, m // bm, nk),
        in_specs=[pl.BlockSpec((tk, bm), lambda j, i, kk: (kk, first + i)),
                  pl.BlockSpec((tk, bn), lambda j, i, kk: (kk, j))]
        + [ANY] * len(extra),
        out_specs=[ospec, ospec],
        out_shape=[jax.ShapeDtypeStruct((n // bn, m, bn), F32), jax.ShapeDtypeStruct((n // bn, m, bn), BF16)],
        compiler_params=_params(3), name=name)(a, b, *extra)


def _decay_tables():
    log_gamma = jnp.log(1.0 - 2.0 ** (-5.0 - jnp.arange(RET_HEADS, dtype=F32)))
    i = jnp.arange(RET_CHUNK, dtype=F32)
    rel = i[:, None] - i[None, :]
    intra = jnp.where(rel >= 0, jnp.exp(jnp.maximum(rel, 0.0) * log_gamma[:, None, None]), 0.0)
    cross = jnp.exp((i + 1.0) * log_gamma[:, None])[:, :, None]
    inner = jnp.exp((RET_CHUNK - 1.0 - i) * log_gamma[:, None])[:, :, None]
    chunk = [float(np.exp(np.float32(RET_CHUNK) * np.log(np.float32(1.0 - 2.0 ** (-5.0 - h))).astype(np.float32)))
             for h in range(RET_HEADS)]
    return intra, cross, inner, chunk


def ret_proj(x, g_pre, w_in, cos, sin, tm=512):
    t, d = x.shape
    nt = t // tm
    per = RET_IN_SHARD // RET_QK

    def body(x_ref, g_ref, win_hbm, c_ref, s_ref, pj_ref, h_ref, win_v, sem):
        @pl.when(pl.program_id(0) == 0)
        def _():
            cp = pltpu.make_async_copy(win_hbm, win_v, sem)
            cp.start()
            cp.wait()

        h, _, _ = _rms_fwd(x_ref[...], g_ref[...])
        hb = h.astype(BF16)
        h_ref[...] = hb
        c = c_ref[...]
        s = s_ref[...]
        for j in range(N_SHARD):
            pjj = _dot(hb, win_v[j])
            for bb in range(per):
                b = per * j + bb
                blk = pjj[:, bb * RET_QK:(bb + 1) * RET_QK]
                if b < 2 * RET_HEADS:
                    x1, x2 = blk[:, :128], blk[:, 128:]
                    o1 = x1 * c - x2 * s
                    o2 = x2 * c + x1 * s
                    if b < RET_HEADS:
                        o1 = o1 * (RET_QK ** -0.5)
                        o2 = o2 * (RET_QK ** -0.5)
                    pj_ref[:, b * RET_QK:b * RET_QK + 128] = o1.astype(BF16)
                    pj_ref[:, b * RET_QK + 128:(b + 1) * RET_QK] = o2.astype(BF16)
                else:
                    pj_ref[:, b * RET_QK:(b + 1) * RET_QK] = blk.astype(BF16)

    row = pl.BlockSpec((tm, d), lambda i: (i, 0))
    tab = pl.BlockSpec((tm, 128), lambda i: (i, 0))
    return pl.pallas_call(
        body, grid=(nt,),
        in_specs=[row, _full((1, d)), ANY, tab, tab],
        out_specs=[pl.BlockSpec((tm, RET_IN), lambda i: (i, 0)), row],
        out_shape=[jax.ShapeDtypeStruct((t, RET_IN), BF16), jax.ShapeDtypeStruct((t, d), BF16)],
        scratch_shapes=[pltpu.VMEM(w_in.shape, BF16), pltpu.SemaphoreType.DMA],
        compiler_params=_params(1), name="ret_proj")(x, g_pre, w_in, cos, sin)


def ret_core_fwd(pj, intra, cross, inner, chunk_decay):
    t = pj.shape[0]
    nc = t // RET_CHUNK
    c = RET_CHUNK
    qk_all = RET_HEADS * RET_QK
    v_all = RET_HEADS * RET_V

    def body(q_ref, k_ref, v_ref, dm_ref, cr_ref, in_ref, o_ref, sp_ref, state):
        @pl.when(pl.program_id(0) == 0)
        def _():
            state[...] = jnp.zeros_like(state)

        for h in range(RET_HEADS):
            q = q_ref[:, h * RET_QK:(h + 1) * RET_QK]
            k = k_ref[:, h * RET_QK:(h + 1) * RET_QK]
            v = v_ref[:, h * RET_V:(h + 1) * RET_V]
            sb = state[h].astype(BF16)
            sp_ref[h] = sb
            sc = _dot_nt(q, k) * dm_ref[h]
            o_ref[:, h * RET_V:(h + 1) * RET_V] = _dot(sc.astype(BF16), v) + _dot(q, sb) * cr_ref[h]
            kd = (k.astype(F32) * in_ref[h]).astype(BF16)
            state[h] = state[h] * chunk_decay[h] + _dot_tn(kd, v)

    return pl.pallas_call(
        body, grid=(nc,),
        in_specs=[pl.BlockSpec((c, qk_all), lambda n: (n, 0)), pl.BlockSpec((c, qk_all), lambda n: (n, 1)),
                  pl.BlockSpec((c, v_all), lambda n: (n, 1)),
                  _full(intra.shape), _full(cross.shape), _full(inner.shape)],
        out_specs=[pl.BlockSpec((c, v_all), lambda n: (n, 0)),
                   pl.BlockSpec((None, RET_HEADS, RET_QK, RET_V), lambda n: (n, 0, 0, 0))],
        out_shape=[jax.ShapeDtypeStruct((t, v_all), F32),
                   jax.ShapeDtypeStruct((nc, RET_HEADS, RET_QK, RET_V), BF16)],
        scratch_shapes=[pltpu.VMEM((RET_HEADS, RET_QK, RET_V), F32)],
        compiler_params=_params(1), name="ret_core_fwd")(pj, pj, pj, intra, cross, inner)


def _group_norm(o_h):
    mu = jnp.mean(o_h, axis=-1, keepdims=True)
    dev = o_h - mu
    rstd = lax.rsqrt(jnp.mean(dev * dev, axis=-1, keepdims=True) + EPS)
    return dev * rstd, rstd


def ret_out_fwd(o, pj, x, gn_gain, g_post, w_out, tm=512):
    t, d = x.shape
    nt = t // tm
    v_all = RET_HEADS * RET_V

    def body(o_ref, g_ref, x_ref, gn_ref, g1_ref, w_ref, xo_ref, y_ref, out_ref):
        for h in range(RET_HEADS):
            cols = slice(h * RET_V, (h + 1) * RET_V)
            ohat, _ = _group_norm(o_ref[:, cols])
            g = g_ref[:, cols].astype(F32)
            y_ref[:, cols] = (g * jax.nn.sigmoid(g) * (ohat * gn_ref[:, cols])).astype(BF16)
        out = _dot(y_ref[...], w_ref[...])
        out_ref[...] = out
        m, _, _ = _rms_fwd(out, g1_ref[...])
        xo_ref[...] = x_ref[...] + m

    row = pl.BlockSpec((tm, d), lambda i: (i, 0))
    wide = pl.BlockSpec((tm, v_all), lambda i: (i, 0))
    return pl.pallas_call(
        body, grid=(nt,),
        in_specs=[wide, pl.BlockSpec((tm, v_all), lambda i: (i, 2)), row, _full((1, v_all)), _full((1, d)),
                  _full(w_out.shape)],
        out_specs=[row, wide, row],
        out_shape=[jax.ShapeDtypeStruct((t, d), F32), jax.ShapeDtypeStruct((t, v_all), BF16),
                   jax.ShapeDtypeStruct((t, d), F32)],
        compiler_params=_params(1), name="ret_out_fwd")(o, pj, x, gn_gain, g_post, w_out)


def ret_out_bwd(dxo, out, o, pj, gn_gain, g_post, w_out, tm=256):
    t, d = out.shape
    nt = t // tm
    v_all = RET_HEADS * RET_V

    def body(dxo_ref, out_ref, o_ref, g_ref, gn_ref, g1_ref, w_ref,
             dout_ref, dgate_ref, do_ref, dg1_ref, dgn_ref):
        @pl.when(pl.program_id(0) == 0)
        def _():
            dg1_ref[...] = jnp.zeros_like(dg1_ref)
            dgn_ref[...] = jnp.zeros_like(dgn_ref)

        g1 = g1_ref[...]
        dxo = dxo_ref[...]
        _, oh_, r_ = _rms_fwd(out_ref[...], g1)
        dg1_ref[...] += _colsum(dxo * oh_)
        doutb = _rms_bwd(oh_, r_, g1, dxo).astype(BF16)
        dout_ref[...] = doutb
        dy = _dot_nt(doutb, w_ref[...])
        for h in range(RET_HEADS):
            cols = slice(h * RET_V, (h + 1) * RET_V)
            gn = gn_ref[:, cols]
            ohat, rstd = _group_norm(o_ref[:, cols])
            g = g_ref[:, cols].astype(F32)
            sg = jax.nn.sigmoid(g)
            dyh = dy[:, cols]
            dgate_ref[:, cols] = (dyh * (ohat * gn) * (sg * (1.0 + g * (1.0 - sg)))).astype(BF16)
            don = dyh * (g * sg)
            dgn_ref[:, cols] += _colsum(don * ohat)
            dohat = don * gn
            do_ref[:, cols] = (rstd * (dohat - jnp.mean(dohat, axis=-1, keepdims=True)
                                       - ohat * jnp.mean(dohat * ohat, axis=-1, keepdims=True))).astype(BF16)

    row = pl.BlockSpec((tm, d), lambda i: (i, 0))
    wide = pl.BlockSpec((tm, v_all), lambda i: (i, 0))
    gate = pl.BlockSpec((tm, v_all), lambda i: (i, 2))
    return pl.pallas_call(
        body, grid=(nt,),
        in_specs=[row, row, wide, gate, _full((1, v_all)), _full((1, d)), _full(w_out.shape)],
        out_specs=[row, gate, wide, _full((1, d)), _full((1, v_all))],
        out_shape=[jax.ShapeDtypeStruct((t, d), BF16), jax.ShapeDtypeStruct((t, RET_IN), BF16),
                   jax.ShapeDtypeStruct((t, v_all), BF16), jax.ShapeDtypeStruct((1, d), F32),
                   jax.ShapeDtypeStruct((1, v_all), F32)],
        compiler_params=_params(1), name="ret_out_bwd")(dxo, out, o, pj, gn_gain, g_post, w_out)


def ret_core_bwd(pj, do, sprev, cos, sin, dpj, intra, cross, inner, chunk_decay):
    t = pj.shape[0]
    nc = t // RET_CHUNK
    c = RET_CHUNK
    qk_all = RET_HEADS * RET_QK
    v_all = RET_HEADS * RET_V
    scale = RET_QK ** -0.5

    def body(q_ref, k_ref, v_ref, do_ref, sp_ref, c_ref, s_ref, dm_ref, cr_ref, in_ref, dpj_in, dpj_ref, dstate):
        @pl.when(pl.program_id(0) == 0)
        def _():
            dstate[...] = jnp.zeros_like(dstate)

        cs = c_ref[...]
        sn = s_ref[...]
        for h in range(RET_HEADS):
            q = q_ref[:, h * RET_QK:(h + 1) * RET_QK]
            k = k_ref[:, h * RET_QK:(h + 1) * RET_QK]
            v = v_ref[:, h * RET_V:(h + 1) * RET_V]
            doh = do_ref[:, h * RET_V:(h + 1) * RET_V]
            dm = dm_ref[h]
            ab = (_dot_nt(q, k) * dm).astype(BF16)
            dab = (_dot_nt(doh, v) * dm).astype(BF16)
            dsb = dstate[h].astype(BF16)
            kd = (k.astype(F32) * in_ref[h]).astype(BF16)
            dv = _dot_tn(ab, doh) + _dot(kd, dsb)
            dq = _dot(dab, k) + cr_ref[h] * _dot_nt(doh, sp_ref[h])
            dk = _dot_tn(dab, q) + in_ref[h] * _dot_nt(v, dsb)
            qd = (q.astype(F32) * cr_ref[h]).astype(BF16)
            dstate[h] = dstate[h] * chunk_decay[h] + _dot_tn(qd, doh)
            for base, dd, sc in ((h * RET_QK, dq, scale), (qk_all + h * RET_QK, dk, 1.0)):
                d1, d2 = dd[:, :128], dd[:, 128:]
                dpj_ref[:, base:base + 128] = ((d1 * cs + d2 * sn) * sc).astype(BF16)
                dpj_ref[:, base + 128:base + RET_QK] = ((d2 * cs - d1 * sn) * sc).astype(BF16)
            dpj_ref[:, 2 * qk_all + h * RET_V:2 * qk_all + (h + 1) * RET_V] = dv.astype(BF16)

    rev = lambda n: nc - 1 - n
    tab = pl.BlockSpec((c, 128), lambda n: (rev(n), 0))
    return pl.pallas_call(
        body, grid=(nc,),
        in_specs=[pl.BlockSpec((c, qk_all), lambda n: (rev(n), 0)), pl.BlockSpec((c, qk_all), lambda n: (rev(n), 1)),
                  pl.BlockSpec((c, v_all), lambda n: (rev(n), 1)), pl.BlockSpec((c, v_all), lambda n: (rev(n), 0)),
                  pl.BlockSpec((None, RET_HEADS, RET_QK, RET_V), lambda n: (rev(n), 0, 0, 0)),
                  tab, tab, _full(intra.shape), _full(cross.shape), _full(inner.shape), ANY],
        out_specs=pl.BlockSpec((c, 2 * qk_all + v_all), lambda n: (rev(n), 0)),
        out_shape=jax.ShapeDtypeStruct((t, RET_IN), BF16),
        scratch_shapes=[pltpu.VMEM((RET_HEADS, RET_QK, RET_V), F32)],
        input_output_aliases={10: 0},
        compiler_params=_params(1), name="ret_core_bwd")(pj, pj, pj, do, sprev, cos, sin, intra, cross, inner, dpj)


def ret_in_bwd(dpj, dres, x, g_pre, w_in, tm=512):
    t, d = x.shape
    nt = t // tm

    def body(dpj_ref, dres_ref, x_ref, g_ref, win_hbm, dx_ref, dg_ref, win_v, sem):
        @pl.when(pl.program_id(0) == 0)
        def _():
            cp = pltpu.make_async_copy(win_hbm, win_v, sem)
            cp.start()
            dg_ref[...] = jnp.zeros_like(dg_ref)
            cp.wait()

        g = g_ref[...]
        dh = jnp.zeros((tm, d), F32)
        for j in range(N_SHARD):
            dh = dh + _dot_nt(dpj_ref[:, j * RET_IN_SHARD:(j + 1) * RET_IN_SHARD], win_v[j])
        _, xh, rx = _rms_fwd(x_ref[...], g)
        dg_ref[...] += _colsum(dh * xh)
        dx_ref[...] = dres_ref[...] + _rms_bwd(xh, rx, g, dh)

    row = pl.BlockSpec((tm, d), lambda i: (i, 0))
    return pl.pallas_call(
        body, grid=(nt,),
        in_specs=[pl.BlockSpec((tm, RET_IN), lambda i: (i, 0)), row, row, _full((1, d)), ANY],
        out_specs=[row, _full((1, d))],
        out_shape=[jax.ShapeDtypeStruct((t, d), F32), jax.ShapeDtypeStruct((1, d), F32)],
        scratch_shapes=[pltpu.VMEM(w_in.shape, BF16), pltpu.SemaphoreType.DMA],
        compiler_params=_params(1), name="ret_in_bwd")(dpj, dres, x, g_pre, w_in)


_CHIP_FLIPS = ((1, 0), (0, 1), (1, 1))


def _flip(v, b):
    return 1 - v if b else v


_HBM = pl.BlockSpec(memory_space=pltpu.HBM)
_SEM = pl.BlockSpec(memory_space=pltpu.SEMAPHORE)
_EFFECT = pltpu.SideEffectType.DATAFLOW_SIDE_EFFECTING


def _chip_copies(mode, srcs, lands, send_sems, recv_sems):
    x, y, c = lax.axis_index("x"), lax.axis_index("y"), lax.axis_index("c")
    copies = []
    for t in range(len(lands)):
        if mode == "swap":
            copies.append(pltpu.make_async_remote_copy(
                src_ref=srcs[t], dst_ref=lands[t], send_sem=send_sems.at[t], recv_sem=recv_sems.at[t],
                device_id=(x, y, 1 - c), device_id_type=MESH))
            continue
        if mode == "everyone":
            for m in range(1, 8):
                bx, by, bc = (m >> 2) & 1, (m >> 1) & 1, m & 1
                copies.append(pltpu.make_async_remote_copy(
                    src_ref=srcs[t], dst_ref=lands[t].at[4 * x + 2 * y + c], send_sem=send_sems.at[7 * t + m - 1],
                    recv_sem=recv_sems.at[7 * t + m - 1], device_id=(_flip(x, bx), _flip(y, by), _flip(c, bc)),
                    device_id_type=MESH))
            continue
        for k, (bx, by) in enumerate(_CHIP_FLIPS):
            px, py = _flip(x, bx), _flip(y, by)
            target = (px, py, c)
            if mode == "gather":
                src, dst = srcs[t], lands[t].at[2 * x + y]
            elif mode == "gather_half":
                half = pl.ds(c * (srcs[t].shape[0] // 2), srcs[t].shape[0] // 2)
                src, dst = srcs[t].at[half], lands[t].at[2 * x + y, half]
            elif mode == "forward_half":
                half = pl.ds(c * (lands[t].shape[1] // 2), lands[t].shape[1] // 2)
                src = dst = lands[t].at[2 * px + py, half]
                target = (x, y, 1 - c)
            else:
                src, dst = srcs[t].at[2 * px + py], lands[t].at[k]
            copies.append(pltpu.make_async_remote_copy(
                src_ref=src, dst_ref=dst, send_sem=send_sems.at[3 * t + k], recv_sem=recv_sems.at[3 * t + k],
                device_id=target, device_id_type=MESH))
    return copies


def exchange_start(mode, srcs, lands, name, after=None):
    n, ns = len(lands), len(srcs)
    extra = [] if after is None else [after]

    def body(*refs):
        ins, lnd = refs[:ns], refs[ns:ns + n]
        send_sems, recv_sems = refs[ns + n + len(extra)], refs[ns + n + len(extra) + 1]
        token = refs[-1]
        for cp in _chip_copies(mode, ins, lnd, send_sems, recv_sems):
            cp.start()
        token[...] = jnp.zeros(token.shape, token.dtype)

    hbm = lambda a: pltpu.with_memory_space_constraint(a, pltpu.HBM)
    passed = list(srcs) + list(lands)
    n_sem = {"swap": 1, "everyone": 7}.get(mode, 3) * n
    return pl.pallas_call(
        body, name=name,
        out_shape=(pltpu.SemaphoreType.DMA((n_sem,)), pltpu.SemaphoreType.DMA((n_sem,)),
                   *[pltpu.HBM(a.shape, a.dtype) for a in passed], jax.ShapeDtypeStruct((8, 128), F32)),
        in_specs=[_HBM] * (ns + n) + [ANY] * len(extra),
        out_specs=(_SEM, _SEM, *[_HBM] * (ns + n), pl.BlockSpec(memory_space=pltpu.VMEM)),
        input_output_aliases={i: 2 + i for i in range(ns + n)},
        compiler_params=pltpu.CompilerParams(has_side_effects=_EFFECT))(*[hbm(a) for a in passed], *extra)


def exchange_wait(mode, started, after, name):
    send_sems, recv_sems = started[0], started[1]
    passed = list(started[2:-1])
    n = len(passed) if mode == "forward_half" else len(passed) // 2
    ns = len(passed) - n

    def body(*refs):
        ins, lnd = refs[:ns], refs[ns:ns + n]
        for cp in _chip_copies(mode, ins, lnd, refs[ns + n], refs[ns + n + 1]):
            cp.wait_send()
            cp.wait_recv()

    outs = pl.pallas_call(
        body, name=name, out_shape=tuple(pltpu.HBM(a.shape, a.dtype) for a in passed),
        in_specs=[_HBM] * (ns + n) + [_SEM, _SEM, ANY], out_specs=tuple([_HBM] * (ns + n)),
        input_output_aliases={i: i for i in range(ns + n)},
        compiler_params=pltpu.CompilerParams(has_side_effects=_EFFECT))(*passed, send_sems, recv_sems, after)
    return list(outs[:ns]), list(outs[ns:])


def plane_sum(slot, full, recv, name, bm=256):
    _, m, n = full.shape
    bm = _row_block(m, bm)

    def body(slot_ref, o_ref, r_ref, s_ref):
        s_ref[...] = ((o_ref[...] + r_ref[0].astype(F32)) + r_ref[1].astype(F32)) + r_ref[2].astype(F32)

    return pl.pallas_call(
        body,
        grid_spec=pltpu.PrefetchScalarGridSpec(
            num_scalar_prefetch=1, grid=(m // bm,),
            in_specs=[pl.BlockSpec((None, bm, n), lambda i, s: (s[0], i, 0)),
                      pl.BlockSpec((3, bm, n), lambda i, s: (0, i, 0))],
            out_specs=pl.BlockSpec((bm, n), lambda i, s: (i, 0))),
        out_shape=jax.ShapeDtypeStruct((m, n), F32), compiler_params=_params(1), name=name)(slot, full, recv)


def sum_slots(parts, name, bm=312):
    _, r, n = parts.shape
    bm = bm if r % bm == 0 else r

    def body(p_ref, s_ref):
        acc = p_ref[0]
        for k in range(1, 8):
            acc = acc + p_ref[k]
        s_ref[...] = acc

    return pl.pallas_call(
        body, grid=(r // bm,), in_specs=[pl.BlockSpec((8, bm, n), lambda i: (0, i, 0))],
        out_specs=pl.BlockSpec((bm, n), lambda i: (i, 0)), out_shape=jax.ShapeDtypeStruct((r, n), F32),
        compiler_params=_params(1), name=name)(parts)


def _adamw_math(w, g, m, v):
    m = ADAM_B1 * m + (1.0 - ADAM_B1) * g
    v = ADAM_B2 * v + (1.0 - ADAM_B2) * (g * g)
    m_hat = m / (1.0 - ADAM_B1 ** ADAM_STEP)
    v_hat = v / (1.0 - ADAM_B2 ** ADAM_STEP)
    delta = -ADAM_LR * (m_hat / (jnp.sqrt(v_hat) + ADAM_EPS) + ADAM_WD * w)
    return delta, m, v


def adamw(w, m, v, grads, layer, prev, name, bm=256, row0=0):
    _, _, n = w.shape
    mm = grads[0].shape[0]
    bm = _row_block(mm, bm)
    first = row0 // bm
    ng = len(grads)

    def body(*refs):
        w_ref, m_ref, v_ref = refs[:3]
        g_refs = refs[3:3 + ng]
        g_out, d_out, m_out, v_out = refs[-4:]
        g = g_refs[0][...]
        for gr in g_refs[1:]:
            g = g + gr[...]
        delta, mn, vn = _adamw_math(w_ref[...], g, m_ref[...], v_ref[...])
        g_out[...] = g
        d_out[...] = delta
        m_out[...] = mn
        v_out[...] = vn

    slab = pl.BlockSpec((None, bm, n), lambda i: (layer, first + i, 0))
    flat = pl.BlockSpec((bm, n), lambda i: (i, 0))
    in_specs = [slab] * 3 + [flat] * ng
    args = [w, m, v, *grads]
    aliases = {}
    if prev is not None:
        in_specs += [ANY] * 4
        aliases = {3 + ng + q: q for q in range(4)}
        args += list(prev)
    return pl.pallas_call(
        body, grid=(mm // bm,), in_specs=in_specs, out_specs=[slab] * 4,
        out_shape=[jax.ShapeDtypeStruct(w.shape, F32)] * 4, input_output_aliases=aliases,
        compiler_params=_params(1), name=name)(*args)


def _pack_rows(parts, rows):
    flat = jnp.concatenate([p.reshape(-1) for p in parts])
    return jnp.pad(flat, (0, rows * 128 - flat.shape[0])).reshape(rows, 128)


def _as_shards(a, rows):
    return a.reshape(N_SHARD, rows, a.shape[-1])


def _local_step(x, pos_col, target, gains, pool_w, pool_scale, gn_gain, conv_w, conv_b, weights, send_grads):
    def gain(l, n, token=None):
        g = gains[l, n].reshape(1, D_MODEL)
        return g if token is None else g + token[0:1, 0:1]

    inv_freq = (ROPE_BASE ** (-jnp.arange(0, RET_QK, 2, dtype=F32) / RET_QK)).reshape(1, RET_QK // 2)
    intra, cross, inner, chunk_decay = _decay_tables()
    dn_rows = D_FF // N_SHARD

    x1 = pool_fwd(x, gain(0, 0), gain(0, 1), pool_w, pool_scale)
    cos, sin = rope_tables(pos_col, inv_freq, x1)
    w_up0, w_dn0 = weights("mlp0", cos)
    w_dn0 = w_dn0.reshape(D_FF, D_MODEL)
    x2, u0, s0, a0, f0, h0 = mlp_fwd(x1, gain(0, 2), gain(0, 3), w_up0, w_dn0, conv_w[0], conv_b[0])
    w_in, w_out = weights("ret", x2)
    w_out = w_out.reshape(RET_HEADS * RET_V, D_MODEL)
    pj, hr = ret_proj(x2, gain(1, 0), w_in, cos, sin)
    o, sprev = ret_core_fwd(pj, intra, cross, inner, chunk_decay)
    x3, yb, out = ret_out_fwd(o, pj, x2, gn_gain, gain(1, 1), w_out)
    w_up1, w_dn1 = weights("mlp1", x3)
    w_dn1 = w_dn1.reshape(D_FF, D_MODEL)
    dx4, u1, s1, a1, f1, h1, loss = mlp_fwd(x3, gain(1, 2), gain(1, 3), w_up1, w_dn1, conv_w[1], conv_b[1], target)

    dx3, du1, df1, dg12, dg13, dcw1, dcb1 = mlp_bwd(
        dx4, f1, x3, u1, s1, gain(1, 2), gain(1, 3), w_up1, w_dn1, conv_w[1])
    dwup1 = grad_matmul(h1, du1, D_MODEL, FF_CHUNK, "grad_w_up_1", interleaved=True)
    dwdn1 = grad_matmul(a1, df1, FF_CHUNK, D_MODEL, "grad_w_down_1")
    tok = send_grads("mlp1", [dwup1, [_as_shards(g, dn_rows) for g in dwdn1]])
    dout, dpj, do, dg11, dgn = ret_out_bwd(dx3, out, o, pj, gn_gain, gain(1, 1, tok), w_out)
    dwout = grad_matmul(yb, dout, 1024, D_MODEL, "grad_w_out")
    dpj = ret_core_bwd(pj, do, sprev, cos, sin, dpj, intra, cross, inner, chunk_decay)
    dwin = grad_matmul(hr, dpj, D_MODEL, RET_IN_SHARD, "grad_w_in")
    tok = send_grads("ret", [dwin, [_as_shards(g, RET_V) for g in dwout]])
    dx2, dg10 = ret_in_bwd(dpj, dx3, x2, gain(1, 0, tok), w_in)
    dx1, du0, df0, dg02, dg03, dcw0, dcb0 = mlp_bwd(
        dx2, f0, x1, u0, s0, gain(0, 2), gain(0, 3), w_up0, w_dn0, conv_w[0])
    dwdn0 = grad_matmul(a0, df0, FF_CHUNK, D_MODEL, "grad_w_down_0")
    tok = send_grads("down0", [[_as_shards(g, dn_rows) for g in dwdn0]])
    half = D_MODEL // 2
    for part, first in (("a", 0), ("b", half)):
        dwup0 = grad_matmul(h0, du0, half, FF_CHUNK, "grad_w_up_0" + part, tk=4096, interleaved=True, after=tok,
                            cols=(first, half))
        tok = send_grads("up0" + part, [dwup0])
    dx0, dg00, dg01, dpscale, dpw = pool_bwd(dx1, x, gain(0, 0, tok), gain(0, 1), pool_w, pool_scale)

    rows = lambda g: jnp.sum(g, axis=0, keepdims=True)
    dgains = jnp.concatenate([dg00, dg01, rows(dg02), rows(dg03), dg10, dg11, rows(dg12), rows(dg13)],
                             axis=0).reshape(2, 4, D_MODEL)
    small = {"gains": dgains, "pool_scale": dpscale, "gn": dgn,
             "conv_w": jnp.sum(jnp.stack([dcw0, dcw1]), axis=3),
             "conv_b": jnp.sum(jnp.stack([dcb0, dcb1]), axis=2, keepdims=True), "pool_w": dpw}
    return loss, dx0, small


def kernel(x, positions, norm_gain, pool_w, pool_scale, ret_w_in, ret_gn_gain, ret_w_out, mlp_w_up, mlp_conv_w, mlp_conv_b, mlp_w_down, loss_target, m_norm_gain, m_pool_w, m_pool_scale, m_ret_w_in, m_ret_gn_gain, m_ret_w_out, m_mlp_w_up, m_mlp_conv_w, m_mlp_conv_b, m_mlp_w_down, v_norm_gain, v_pool_w, v_pool_scale, v_ret_w_in, v_ret_gn_gain, v_ret_w_out, v_mlp_w_up, v_mlp_conv_w, v_mlp_conv_b, v_mlp_w_down):
    t = x.shape[1]
    me = 2 * lax.axis_index("x") + lax.axis_index("y")
    me_slot = jnp.reshape(me, (1,)).astype(jnp.int32)

    small_parts = [norm_gain, ret_gn_gain, mlp_conv_w, pool_w]
    small_sizes = [p.size for p in small_parts]
    small_rows = -(-sum(small_sizes) // (128 * 8)) * 8
    groups = {"small": [_pack_rows(small_parts, small_rows)],
              "mlp0": [mlp_w_up[0].astype(BF16), mlp_w_down[0].astype(BF16)],
              "ret": [ret_w_in[0].astype(BF16), ret_w_out[0].astype(BF16)],
              "mlp1": [mlp_w_up[1].astype(BF16), mlp_w_down[1].astype(BF16)]}
    gathers, token = {}, None
    for group, srcs in groups.items():
        lands = [lax.dynamic_update_index_in_dim(lax.empty((N_SHARD,) + s.shape, s.dtype), s, me, 0) for s in srcs]
        mode = "gather_half" if group == "mlp0" else "gather"
        gathers[group] = (mode, exchange_start(mode, srcs, lands, "gather_start_" + group, after=token))
        token = gathers[group][1][-1]

    def weights(group, after):
        mode, started = gathers[group]
        _, lands = exchange_wait(mode, started, after, "gather_wait_" + group)
        if mode == "gather_half":
            forward = exchange_start("forward_half", [], lands, "forward_start_" + group)
            _, lands = exchange_wait("forward_half", forward, forward[-1], "forward_wait_" + group)
        return lands

    sent, early = {}, {}

    def reduced(group, after, names):
        started, own = sent[group]
        _, recv = exchange_wait("scatter", started, after, "scatter_wait_" + group)
        return [plane_sum(me_slot, f, r, "plane_sum_" + nm)
                for f, r, nm in zip(own, recv, names)]

    def swap_start(planes, name):
        return exchange_start("swap", planes, [lax.empty(p.shape, p.dtype) for p in planes], name)

    def send_grads(group, pairs):
        lands = [lax.empty((3,) + b.shape[1:], BF16) for _, b in pairs]
        sent[group] = (exchange_start("scatter", [b for _, b in pairs], lands, "scatter_start_" + group),
                       [f for f, _ in pairs])
        token = sent[group][0][-1]
        if group == "down0":
            marker = pairs[0][1]
            early["planes"] = (reduced("mlp1", marker, ["w_up_1", "w_down_1"])
                               + reduced("ret", marker, ["w_in", "w_out"]))
            early["swap"] = swap_start(early["planes"], "swap_start_a")
            token = token + early["swap"][-1]
        return token

    (smallg,) = weights("small", token)
    smallg = smallg.reshape(N_SHARD, -1)
    offs = np.cumsum([0] + small_sizes)
    piece = lambda i, shape: smallg[:, offs[i]:offs[i + 1]].reshape((N_SHARD,) + shape)
    gains = piece(0, (2, 4, 256)).transpose(1, 2, 0, 3).reshape(2, 4, D_MODEL)
    gn_full = piece(1, (512,)).reshape(1, RET_HEADS * RET_V)
    cw_full = piece(2, (2, 3, FF_CHUNK)).transpose(1, 0, 2, 3)
    pw_full = piece(3, (4, 64, 256)).transpose(1, 0, 2, 3).reshape(4, 256, 256).astype(BF16)
    cb_full = mlp_conv_b.reshape(2, N_SHARD, 1, FF_CHUNK)

    loss, dx0, small = _local_step(
        x[0], positions.reshape(t, 1).astype(F32), loss_target[0], gains, pw_full, pool_scale, gn_full,
        cw_full, cb_full, weights, send_grads)

    def small_adamw(w, m, v, grads, name):
        w3 = w.reshape(1, -1, w.shape[-1])
        out = adamw(w3, m.reshape(w3.shape), v.reshape(w3.shape), [g.reshape(w3.shape[1:]) for g in grads], 0, None, name)
        return [o.reshape(w.shape) for o in out]

    pw_f = small["pool_w"].reshape(4, N_SHARD, 64, 256).transpose(1, 0, 2, 3).reshape(N_SHARD, 256, 256)
    small_order = ["gains", "pool_scale", "gn", "conv_w", "conv_b"]
    gsmall_sizes = [small[k].size for k in small_order]
    gsmall_rows = -(-sum(gsmall_sizes) // (128 * 8)) * 8
    gpack = _pack_rows([small[k] for k in small_order], gsmall_rows)
    mine = 2 * me + lax.axis_index("c")
    small_started = exchange_start(
        "everyone", [gpack], [lax.dynamic_update_index_in_dim(lax.empty((8,) + gpack.shape, F32), gpack, mine, 0)],
        "small_start")
    send_grads("pool_w", [(pw_f, pw_f.astype(BF16))])

    res = {}
    planes_a, others_a = exchange_wait("swap", early["swap"], small_started[-1], "swap_wait_a")
    res["ret_w_in"] = adamw(ret_w_in, m_ret_w_in, v_ret_w_in, (planes_a[2], others_a[2]), 0, None, "adamw_w_in")
    res["ret_w_out"] = adamw(ret_w_out, m_ret_w_out, v_ret_w_out, (planes_a[3], others_a[3]), 0, None, "adamw_w_out")
    up1 = adamw(mlp_w_up, m_mlp_w_up, v_mlp_w_up, (planes_a[0], others_a[0]), 1, None, "adamw_w_up_1")
    dn1 = adamw(mlp_w_down, m_mlp_w_down, v_mlp_w_down, (planes_a[1], others_a[1]), 1, None, "adamw_w_down_1")

    planes_b = (reduced("up0a", dn1[0], ["w_up_0a"]) + reduced("up0b", dn1[0], ["w_up_0b"])
                + reduced("down0", dn1[0], ["w_down_0"]) + reduced("pool_w", dn1[0], ["pool_w"]))
    swap_b = swap_start(planes_b, "swap_start_b")

    _, (small_recv,) = exchange_wait("everyone", small_started, swap_b[-1], "small_wait")
    gsmall = sum_slots(small_recv, "sum_small").reshape(-1)
    goffs = np.cumsum([0] + gsmall_sizes)
    gpiece = lambda i: gsmall[goffs[i]:goffs[i + 1]].reshape(small[small_order[i]].shape)
    g_gains = lax.dynamic_slice_in_dim(gpiece(0), me * 256, 256, axis=2)
    g_gn = lax.dynamic_slice_in_dim(gpiece(2), me * RET_V, RET_V, axis=1)
    g_cw = lax.dynamic_index_in_dim(gpiece(3), me, 1, keepdims=False)
    res["norm_gain"] = small_adamw(norm_gain, m_norm_gain, v_norm_gain, [g_gains], "adamw_norm_gain")
    res["pool_scale"] = small_adamw(pool_scale, m_pool_scale, v_pool_scale, [gpiece(1)], "adamw_pool_scale")
    res["ret_gn_gain"] = small_adamw(ret_gn_gain, m_ret_gn_gain, v_ret_gn_gain, [g_gn], "adamw_gn_gain")
    res["mlp_conv_w"] = small_adamw(mlp_conv_w, m_mlp_conv_w, v_mlp_conv_w, [g_cw], "adamw_conv_w")
    res["mlp_conv_b"] = small_adamw(mlp_conv_b, m_mlp_conv_b, v_mlp_conv_b, [gpiece(4)], "adamw_conv_b")

    planes_b, others_b = exchange_wait("swap", swap_b, res["mlp_conv_b"][0], "swap_wait_b")
    up0a = adamw(mlp_w_up, m_mlp_w_up, v_mlp_w_up, (planes_b[0], others_b[0]), 0, up1, "adamw_w_up_0a")
    res["mlp_w_up"] = adamw(mlp_w_up, m_mlp_w_up, v_mlp_w_up, (planes_b[1], others_b[1]), 0, up0a, "adamw_w_up_0b",
                            row0=D_MODEL // 2)
    res["mlp_w_down"] = adamw(mlp_w_down, m_mlp_w_down, v_mlp_w_down, (planes_b[2], others_b[2]), 0, dn1,
                              "adamw_w_down_0")
    res["pool_w"] = small_adamw(pool_w, m_pool_w, v_pool_w, (planes_b[3], others_b[3]), "adamw_pool_w")

    order = ["norm_gain", "pool_w", "pool_scale", "ret_w_in", "ret_gn_gain", "ret_w_out", "mlp_w_up", "mlp_conv_w",
             "mlp_conv_b", "mlp_w_down"]
    total_loss = lax.psum(loss[0, 0], ("x", "y", "c"))
    outs = [total_loss, dx0.reshape(x.shape)]
    for q in range(4):
        outs += [res[k][q] for k in order]
    return tuple(outs)
```

```python
import numpy as np
import jax
import jax.numpy as jnp
from jax import lax
from jax.experimental import pallas as pl
from jax.experimental.pallas import tpu as pltpu

F32 = jnp.float32
BF16 = jnp.bfloat16

D_MODEL = 1024
D_FF = 2816
FF_CHUNK = 1408
N_SHARD = 4
POOL_WINDOWS = (2, 4, 8, 16)
POOL_DIM = 256
POOL_HALO = 16
RET_HEADS = 4
RET_QK = 256
RET_V = 512
RET_CHUNK = 256
RET_IN = 6144
RET_IN_SHARD = 1536
ROPE_BASE = 10000.0
EPS = 1e-6
CONV_HALO = 8

ADAM_LR, ADAM_B1, ADAM_B2, ADAM_EPS, ADAM_WD, ADAM_STEP = 0.001, 0.9, 0.999, 1e-08, 0.01, 10

VMEM_LIMIT = 56 * 1024 * 1024
VMEM_LIMIT_MLP_BWD = 62 * 1024 * 1024
MESH = pl.DeviceIdType.MESH
ANY = pl.BlockSpec(memory_space=pl.ANY)


def _params(n_grid=1, limit=VMEM_LIMIT):
    return pltpu.CompilerParams(dimension_semantics=("arbitrary",) * n_grid, vmem_limit_bytes=limit)


def _dot(a, b):
    return jnp.dot(a, b, preferred_element_type=F32)


def _dot_nt(a, b):
    return lax.dot_general(a, b, (((1,), (1,)), ((), ())), preferred_element_type=F32)


def _dot_tn(a, b):
    return lax.dot_general(a, b, (((0,), (0,)), ((), ())), preferred_element_type=F32)


def _rms_fwd(x, gain):
    r = lax.rsqrt(jnp.mean(x * x, axis=-1, keepdims=True) + EPS)
    xh = x * r
    return xh * gain, xh, r


def _rms_bwd(xh, r, gain, dy):
    dxh = dy * gain
    return r * (dxh - xh * jnp.mean(dxh * xh, axis=-1, keepdims=True))


def _colsum(v):
    return jnp.sum(v, axis=0, keepdims=True)


def _full(shape):
    nd = len(shape)
    return pl.BlockSpec(shape, lambda *_: (0,) * nd)


def rope_tables(pos_col, inv_freq, after):
    t = pos_col.shape[0]
    tm = min(t, 1024)

    def body(p_ref, f_ref, after_ref, c_ref, s_ref):
        ang = p_ref[...] * f_ref[...]
        c_ref[...] = jnp.cos(ang)
        s_ref[...] = jnp.sin(ang)

    return pl.pallas_call(
        body, grid=(t // tm,),
        in_specs=[pl.BlockSpec((tm, 1), lambda i: (i, 0)), _full((1, 128)), ANY],
        out_specs=[pl.BlockSpec((tm, 128), lambda i: (i, 0))] * 2,
        out_shape=[jax.ShapeDtypeStruct((t, 128), F32)] * 2,
        compiler_params=_params(1), name="rope_tables")(pos_col, inv_freq, after)


def _window_sums(ext, backward):
    n = ext.shape[0]
    cur, sums = ext, []
    for g, win in enumerate(POOL_WINDOWS):
        if g > 0:
            cur = cur[:, POOL_DIM:]
        half = win // 2
        cur = cur + pltpu.roll(cur, n - half if backward else half, axis=0)
        sums.append(cur[:, 0:POOL_DIM])
    return sums


def _pool_diff(h_halo, h, row0, tm):
    t_idx = row0 + lax.broadcasted_iota(jnp.int32, (tm, 1), 0)
    sums = _window_sums(jnp.concatenate([h_halo, h], axis=0), backward=False)
    parts, inv_counts = [], []
    for g, win in enumerate(POOL_WINDOWS):
        inv = 1.0 / jnp.minimum(t_idx + 1, win).astype(F32)
        parts.append(sums[g][POOL_HALO:, :] * inv - h[:, g * POOL_DIM:(g + 1) * POOL_DIM])
        inv_counts.append(inv)
    return parts, inv_counts


def pool_fwd(x, g_pre, g_post, pool_w, pool_scale, tm=512):
    t, d = x.shape
    nt = t // tm

    def body(x_ref, g0_ref, g1_ref, w_ref, sc_ref, o_ref, hext):
        i = pl.program_id(0)

        @pl.when(i == 0)
        def _():
            hext[...] = jnp.zeros((POOL_HALO, d), F32)

        xv = x_ref[...]
        h, _, _ = _rms_fwd(xv, g0_ref[...])
        parts, _ = _pool_diff(hext[...], h, i * tm, tm)
        hext[...] = h[tm - POOL_HALO:tm, :]
        ys = [_dot(parts[g].astype(BF16), w_ref[g]) for g in range(len(POOL_WINDOWS))]
        y = jnp.concatenate(ys, axis=-1) * sc_ref[...]
        m, _, _ = _rms_fwd(y, g1_ref[...])
        o_ref[...] = xv + m

    row = pl.BlockSpec((tm, d), lambda i: (i, 0))
    return pl.pallas_call(
        body, grid=(nt,),
        in_specs=[row, _full((1, d)), _full((1, d)), _full(pool_w.shape), _full((1, d))],
        out_specs=row, out_shape=jax.ShapeDtypeStruct((t, d), F32),
        scratch_shapes=[pltpu.VMEM((POOL_HALO, d), F32)],
        compiler_params=_params(1), name="pool_fwd")(x, g_pre, g_post, pool_w, pool_scale)


def pool_bwd(dx1, x, g_pre, g_post, pool_w, pool_scale, tm=512):
    t, d = x.shape
    nt = t // tm
    ng = len(POOL_WINDOWS)

    def body(dx1_ref, x_ref, xh_ref, g0_ref, g1_ref, w_ref, sc_ref,
             dx_ref, dg0_ref, dg1_ref, dsc_ref, dw_ref, enext):
        i = pl.program_id(0)
        r = nt - 1 - i

        @pl.when(i == 0)
        def _():
            enext[...] = jnp.zeros((POOL_HALO, d), F32)
            dg0_ref[...] = jnp.zeros_like(dg0_ref)
            dg1_ref[...] = jnp.zeros_like(dg1_ref)
            dsc_ref[...] = jnp.zeros_like(dsc_ref)
            dw_ref[...] = jnp.zeros_like(dw_ref)

        g0 = g0_ref[...]
        g1 = g1_ref[...]
        sc = sc_ref[...]
        xv = x_ref[...]
        h, xh, rx = _rms_fwd(xv, g0)
        h_halo, _, _ = _rms_fwd(xh_ref[...], g0)
        parts, inv_counts = _pool_diff(h_halo * jnp.where(r > 0, 1.0, 0.0), h, r * tm, tm)
        parts_b = [p.astype(BF16) for p in parts]
        ypre = jnp.concatenate([_dot(parts_b[g], w_ref[g]) for g in range(ng)], axis=-1)
        _, yh, ry = _rms_fwd(ypre * sc, g1)
        dm = dx1_ref[...]
        dg1_ref[...] += _colsum(dm * yh)
        dy = _rms_bwd(yh, ry, g1, dm)
        dsc_ref[...] += _colsum(dy * ypre)
        dyp = (dy * sc).astype(BF16)
        ddiffs = []
        for g in range(ng):
            cols = slice(g * POOL_DIM, (g + 1) * POOL_DIM)
            dw_ref[g] += _dot_tn(parts_b[g], dyp[:, cols])
            ddiffs.append(_dot_nt(dyp[:, cols], w_ref[g]))
        e = jnp.concatenate([ddiffs[g] * inv_counts[g] for g in range(ng)], axis=-1)
        sums = _window_sums(jnp.concatenate([e, enext[...]], axis=0), backward=True)
        enext[...] = e[0:POOL_HALO, :]
        dh = jnp.concatenate([sums[g][0:tm, :] - ddiffs[g] for g in range(ng)], axis=-1)
        dg0_ref[...] += _colsum(dh * xh)
        dx_ref[...] = dm + _rms_bwd(xh, rx, g0, dh)

    row = pl.BlockSpec((tm, d), lambda i: (nt - 1 - i, 0))
    halo = pl.BlockSpec((POOL_HALO, d), lambda i: (jnp.maximum((nt - 1 - i) * (tm // POOL_HALO) - 1, 0), 0))
    vec = _full((1, d))
    return pl.pallas_call(
        body, grid=(nt,),
        in_specs=[row, row, halo, vec, vec, _full(pool_w.shape), vec],
        out_specs=[row, vec, vec, vec, _full((ng, POOL_DIM, POOL_DIM))],
        out_shape=[jax.ShapeDtypeStruct((t, d), F32)] + [jax.ShapeDtypeStruct((1, d), F32)] * 3
        + [jax.ShapeDtypeStruct((ng, POOL_DIM, POOL_DIM), F32)],
        scratch_shapes=[pltpu.VMEM((POOL_HALO, d), F32)],
        compiler_params=_params(1), name="pool_bwd")(dx1, x, x, g_pre, g_post, pool_w, pool_scale)


def _conv_taps(cw_ref, j):
    return cw_ref[j, 0:1, :], cw_ref[j, 1:2, :], cw_ref[j, 2:3, :]


def _row_block(m, target=256):
    if m <= target:
        return m
    for b in range(target, 7, -8):
        if m % b == 0:
            return b
    return m


def mlp_fwd(x, g_pre, g_post, w_up, w_down, conv_w, conv_b, target=None, tm=256):
    t, d = x.shape
    nt = t // tm
    h8 = CONV_HALO
    with_loss = target is not None
    n_extra = 1 if with_loss else 0

    def body(x_ref, g2_ref, g3_ref, wup_hbm, wdn_hbm, cw_ref, cb_ref, *rest):
        tgt_ref = rest[0] if with_loss else None
        xo_ref, u_ref, s_ref, a_ref, f_ref, h_ref = rest[n_extra:n_extra + 6]
        loss_ref = rest[n_extra + 6] if with_loss else None
        wup_v, wdn_v, tail, sem = rest[-4:]
        i = pl.program_id(0)

        @pl.when(i == 0)
        def _():
            c1 = pltpu.make_async_copy(wup_hbm, wup_v, sem.at[0])
            c2 = pltpu.make_async_copy(wdn_hbm, wdn_v, sem.at[1])
            c1.start()
            c2.start()
            tail[...] = jnp.zeros_like(tail)
            if with_loss:
                loss_ref[...] = jnp.zeros_like(loss_ref)
            c1.wait()
            c2.wait()

        xv = x_ref[...]
        h, _, _ = _rms_fwd(xv, g2_ref[...])
        hb = h.astype(BF16)
        h_ref[...] = hb
        acc = jnp.zeros((tm, d), F32)
        for k in range(2):
            cs = []
            for s in range(2):
                j, cols = k + 2 * s, slice((2 * k + s) * FF_CHUNK, (2 * k + s + 1) * FF_CHUNK)
                uf = _dot(hb, wup_v[j])
                u_ref[:, cols] = uf.astype(BF16)
                ext = jnp.concatenate([tail[j], uf], axis=0)
                tail[j] = uf[tm - h8:tm, :]
                w0, w1, w2 = _conv_taps(cw_ref, j)
                cs.append(cb_ref[j] + w2 * uf + w1 * pltpu.roll(ext, 1, axis=0)[h8:, :]
                          + w0 * pltpu.roll(ext, 2, axis=0)[h8:, :])
            cg, cv = cs
            sg = jax.nn.sigmoid(cg)
            sil = cg * sg
            ab = (sil * cv).astype(BF16)
            a_ref[:, k * FF_CHUNK:(k + 1) * FF_CHUNK] = ab
            s_ref[:, 2 * k * FF_CHUNK:(2 * k + 1) * FF_CHUNK] = sil.astype(BF16)
            s_ref[:, (2 * k + 1) * FF_CHUNK:(2 * k + 2) * FF_CHUNK] = (cv * (sg * (1.0 + cg * (1.0 - sg)))).astype(BF16)
            acc = acc + _dot(ab, wdn_v[k * FF_CHUNK:(k + 1) * FF_CHUNK, :])
        f_ref[...] = acc
        y, _, _ = _rms_fwd(acc, g3_ref[...])
        if with_loss:
            err = (xv + y) - tgt_ref[...]
            xo_ref[...] = err * (1.0 / d)
            loss_ref[...] += 0.5 * jnp.sum(jnp.mean(err * err, axis=-1, keepdims=True), axis=0, keepdims=True)
        else:
            xo_ref[...] = xv + y

    row = pl.BlockSpec((tm, d), lambda i: (i, 0))
    wide = pl.BlockSpec((tm, 2 * D_FF), lambda i: (i, 0))
    vec = _full((1, d))
    extra = [target] if with_loss else []
    return pl.pallas_call(
        body, grid=(nt,),
        in_specs=[row, vec, vec, ANY, ANY, _full(conv_w.shape), _full(conv_b.shape)] + [row] * n_extra,
        out_specs=[row, wide, wide, pl.BlockSpec((tm, D_FF), lambda i: (i, 0)), row, row] + [_full((1, 1))] * n_extra,
        out_shape=[jax.ShapeDtypeStruct((t, d), F32), jax.ShapeDtypeStruct((t, 2 * D_FF), BF16),
                   jax.ShapeDtypeStruct((t, 2 * D_FF), BF16), jax.ShapeDtypeStruct((t, D_FF), BF16),
                   jax.ShapeDtypeStruct((t, d), F32), jax.ShapeDtypeStruct((t, d), BF16)]
        + [jax.ShapeDtypeStruct((1, 1), F32)] * n_extra,
        scratch_shapes=[pltpu.VMEM(w_up.shape, BF16), pltpu.VMEM(w_down.shape, BF16),
                        pltpu.VMEM((N_SHARD, h8, FF_CHUNK), F32), pltpu.SemaphoreType.DMA((2,))],
        compiler_params=_params(1), name="mlp_fwd_loss" if with_loss else "mlp_fwd")(
            x, g_pre, g_post, w_up, w_down, conv_w, conv_b, *extra)


def _rowsum8(v):
    return jnp.sum(v.reshape(v.shape[0] // 8, 8, v.shape[1]), axis=0)


def mlp_bwd(dxo, f, x, u, sp, g_pre, g_post, w_up, w_down, conv_w, tm=256):
    t, d = x.shape
    nt = t // tm
    h8 = CONV_HALO

    def body(dxo_ref, f_ref, x_ref, u_ref, s_ref, g2_ref, g3_ref, wup_hbm, wdn_hbm, cw_ref,
             dx_ref, du_ref, df_ref, dg2_ref, dg3_ref, dcw_ref, dcb_ref,
             wup_v, wdn_v, carry, sem):
        @pl.when(pl.program_id(0) == 0)
        def _():
            c1 = pltpu.make_async_copy(wup_hbm, wup_v, sem.at[0])
            c2 = pltpu.make_async_copy(wdn_hbm, wdn_v, sem.at[1])
            c1.start()
            c2.start()
            carry[...] = jnp.zeros_like(carry)
            dg2_ref[...] = jnp.zeros_like(dg2_ref)
            dg3_ref[...] = jnp.zeros_like(dg3_ref)
            dcw_ref[...] = jnp.zeros_like(dcw_ref)
            dcb_ref[...] = jnp.zeros_like(dcb_ref)
            c1.wait()
            c2.wait()

        g3 = g3_ref[...]
        dxo = dxo_ref[...]
        _, fh, rf = _rms_fwd(f_ref[...], g3)
        dg3_ref[...] += _rowsum8(dxo * fh)
        dfb = _rms_bwd(fh, rf, g3, dxo).astype(BF16)
        df_ref[...] = dfb
        dh = jnp.zeros((tm, d), F32)
        for k in range(2):
            da = _dot_nt(dfb, wdn_v[k * FF_CHUNK:(k + 1) * FF_CHUNK, :])
            for s in range(2):
                j = k + 2 * s
                cols = slice((2 * k + s) * FF_CHUNK, (2 * k + s + 1) * FF_CHUNK)
                dc = da * s_ref[:, (2 * k + 1 - s) * FF_CHUNK:(2 * k + 2 - s) * FF_CHUNK].astype(F32)
                uf = u_ref[:, cols].astype(F32)
                ext = jnp.concatenate([dc, carry[j]], axis=0)
                carry[j] = dc[0:h8, :]
                dc1 = pltpu.roll(ext, tm + h8 - 1, axis=0)[0:tm, :]
                dc2 = pltpu.roll(ext, tm + h8 - 2, axis=0)[0:tm, :]
                dcb_ref[j] += _rowsum8(dc)
                dcw_ref[j, 2] += _rowsum8(dc * uf)
                dcw_ref[j, 1] += _rowsum8(dc1 * uf)
                dcw_ref[j, 0] += _rowsum8(dc2 * uf)
                dub = (cw_ref[j, 2:3, :] * dc + cw_ref[j, 1:2, :] * dc1 + cw_ref[j, 0:1, :] * dc2).astype(BF16)
                du_ref[:, cols] = dub
                dh = dh + _dot_nt(dub, wup_v[j])
        g2 = g2_ref[...]
        _, xh, rx = _rms_fwd(x_ref[...], g2)
        dg2_ref[...] += _rowsum8(dh * xh)
        dx_ref[...] = dxo + _rms_bwd(xh, rx, g2, dh)

    row = pl.BlockSpec((tm, d), lambda i: (nt - 1 - i, 0))
    wide = pl.BlockSpec((tm, 2 * D_FF), lambda i: (nt - 1 - i, 0))
    vec = _full((1, d))
    acc = _full((8, d))
    dcw_shape, dcb_shape = (N_SHARD, 3, 8, FF_CHUNK), (N_SHARD, 8, FF_CHUNK)
    return pl.pallas_call(
        body, grid=(nt,),
        in_specs=[row, row, row, wide, wide, vec, vec, ANY, ANY, _full(conv_w.shape)],
        out_specs=[row, wide, row, acc, acc, _full(dcw_shape), _full(dcb_shape)],
        out_shape=[jax.ShapeDtypeStruct((t, d), F32), jax.ShapeDtypeStruct((t, 2 * D_FF), BF16),
                   jax.ShapeDtypeStruct((t, d), BF16),
                   jax.ShapeDtypeStruct((8, d), F32), jax.ShapeDtypeStruct((8, d), F32),
                   jax.ShapeDtypeStruct(dcw_shape, F32), jax.ShapeDtypeStruct(dcb_shape, F32)],
        scratch_shapes=[pltpu.VMEM(w_up.shape, BF16), pltpu.VMEM(w_down.shape, BF16),
                        pltpu.VMEM((N_SHARD, h8, FF_CHUNK), F32), pltpu.SemaphoreType.DMA((2,))],
        compiler_params=_params(1, VMEM_LIMIT_MLP_BWD), name="mlp_bwd")(
            dxo, f, x, u, sp, g_pre, g_post, w_up, w_down, conv_w)


def grad_matmul(a, b, bm, bn, name, tk=2048, interleaved=False, after=None, cols=None):
    t = a.shape[0]
    m0, m = (0, a.shape[1]) if cols is None else cols
    n = b.shape[1]
    tk = min(tk, t)
    nk = t // tk
    place = (lambda j: (j % 2) * 2 + j // 2) if interleaved else (lambda j: j)
    extra = [] if after is None else [after]
    first = m0 // bm

    def body(a_ref, b_ref, *rest):
        o_ref, ob_ref = rest[len(extra):]
        kk = pl.program_id(2)

        @pl.when(kk == 0)
        def _():
            o_ref[...] = jnp.zeros_like(o_ref)

        o_ref[...] += _dot_tn(a_ref[...], b_ref[...])

        @pl.when(kk == nk - 1)
        def _():
            ob_ref[...] = o_ref[...].astype(BF16)

    ospec = pl.BlockSpec((None, bm, bn), lambda j, i, kk: (place(j), i, 0))
    return pl.pallas_call(
        body, grid=(n // bn, m // bm, nk),
        in_specs=[pl.BlockSpec((tk, bm), lambda j, i, kk: (kk, first + i)),
                  pl.BlockSpec((tk, bn), lambda j, i, kk: (kk, j))]
        + [ANY] * len(extra),
        out_specs=[ospec, ospec],
        out_shape=[jax.ShapeDtypeStruct((n // bn, m, bn), F32), jax.ShapeDtypeStruct((n // bn, m, bn), BF16)],
        compiler_params=_params(3), name=name)(a, b, *extra)


def _decay_tables():
    log_gamma = jnp.log(1.0 - 2.0 ** (-5.0 - jnp.arange(RET_HEADS, dtype=F32)))
    i = jnp.arange(RET_CHUNK, dtype=F32)
    rel = i[:, None] - i[None, :]
    intra = jnp.where(rel >= 0, jnp.exp(jnp.maximum(rel, 0.0) * log_gamma[:, None, None]), 0.0)
    cross = jnp.exp((i + 1.0) * log_gamma[:, None])[:, :, None]
    inner = jnp.exp((RET_CHUNK - 1.0 - i) * log_gamma[:, None])[:, :, None]
    chunk = [float(np.exp(np.float32(RET_CHUNK) * np.log(np.float32(1.0 - 2.0 ** (-5.0 - h))).astype(np.float32)))
             for h in range(RET_HEADS)]
    return intra, cross, inner, chunk


def ret_proj(x, g_pre, w_in, cos, sin, tm=512):
    t, d = x.shape
    nt = t // tm
    per = RET_IN_SHARD // RET_QK

    def body(x_ref, g_ref, win_hbm, c_ref, s_ref, pj_ref, h_ref, win_v, sem):
        @pl.when(pl.program_id(0) == 0)
        def _():
            cp = pltpu.make_async_copy(win_hbm, win_v, sem)
            cp.start()
            cp.wait()

        h, _, _ = _rms_fwd(x_ref[...], g_ref[...])
        hb = h.astype(BF16)
        h_ref[...] = hb
        c = c_ref[...]
        s = s_ref[...]
        for j in range(N_SHARD):
            pjj = _dot(hb, win_v[j])
            for bb in range(per):
                b = per * j + bb
                blk = pjj[:, bb * RET_QK:(bb + 1) * RET_QK]
                if b < 2 * RET_HEADS:
                    x1, x2 = blk[:, :128], blk[:, 128:]
                    o1 = x1 * c - x2 * s
                    o2 = x2 * c + x1 * s
                    if b < RET_HEADS:
                        o1 = o1 * (RET_QK ** -0.5)
                        o2 = o2 * (RET_QK ** -0.5)
                    pj_ref[:, b * RET_QK:b * RET_QK + 128] = o1.astype(BF16)
                    pj_ref[:, b * RET_QK + 128:(b + 1) * RET_QK] = o2.astype(BF16)
                else:
                    pj_ref[:, b * RET_QK:(b + 1) * RET_QK] = blk.astype(BF16)

    row = pl.BlockSpec((tm, d), lambda i: (i, 0))
    tab = pl.BlockSpec((tm, 128), lambda i: (i, 0))
    return pl.pallas_call(
        body, grid=(nt,),
        in_specs=[row, _full((1, d)), ANY, tab, tab],
        out_specs=[pl.BlockSpec((tm, RET_IN), lambda i: (i, 0)), row],
        out_shape=[jax.ShapeDtypeStruct((t, RET_IN), BF16), jax.ShapeDtypeStruct((t, d), BF16)],
        scratch_shapes=[pltpu.VMEM(w_in.shape, BF16), pltpu.SemaphoreType.DMA],
        compiler_params=_params(1), name="ret_proj")(x, g_pre, w_in, cos, sin)


def ret_core_fwd(pj, intra, cross, inner, chunk_decay):
    t = pj.shape[0]
    nc = t // RET_CHUNK
    c = RET_CHUNK
    qk_all = RET_HEADS * RET_QK
    v_all = RET_HEADS * RET_V

    def body(q_ref, k_ref, v_ref, dm_ref, cr_ref, in_ref, o_ref, sp_ref, state):
        @pl.when(pl.program_id(0) == 0)
        def _():
            state[...] = jnp.zeros_like(state)

        for h in range(RET_HEADS):
            q = q_ref[:, h * RET_QK:(h + 1) * RET_QK]
            k = k_ref[:, h * RET_QK:(h + 1) * RET_QK]
            v = v_ref[:, h * RET_V:(h + 1) * RET_V]
            sb = state[h].astype(BF16)
            sp_ref[h] = sb
            sc = _dot_nt(q, k) * dm_ref[h]
            o_ref[:, h * RET_V:(h + 1) * RET_V] = _dot(sc.astype(BF16), v) + _dot(q, sb) * cr_ref[h]
            kd = (k.astype(F32) * in_ref[h]).astype(BF16)
            state[h] = state[h] * chunk_decay[h] + _dot_tn(kd, v)

    return pl.pallas_call(
        body, grid=(nc,),
        in_specs=[pl.BlockSpec((c, qk_all), lambda n: (n, 0)), pl.BlockSpec((c, qk_all), lambda n: (n, 1)),
                  pl.BlockSpec((c, v_all), lambda n: (n, 1)),
                  _full(intra.shape), _full(cross.shape), _full(inner.shape)],
        out_specs=[pl.BlockSpec((c, v_all), lambda n: (n, 0)),
                   pl.BlockSpec((None, RET_HEADS, RET_QK, RET_V), lambda n: (n, 0, 0, 0))],
        out_shape=[jax.ShapeDtypeStruct((t, v_all), F32),
                   jax.ShapeDtypeStruct((nc, RET_HEADS, RET_QK, RET_V), BF16)],
        scratch_shapes=[pltpu.VMEM((RET_HEADS, RET_QK, RET_V), F32)],
        compiler_params=_params(1), name="ret_core_fwd")(pj, pj, pj, intra, cross, inner)


def _group_norm(o_h):
    mu = jnp.mean(o_h, axis=-1, keepdims=True)
    dev = o_h - mu
    rstd = lax.rsqrt(jnp.mean(dev * dev, axis=-1, keepdims=True) + EPS)
    return dev * rstd, rstd


def ret_out_fwd(o, pj, x, gn_gain, g_post, w_out, tm=512):
    t, d = x.shape
    nt = t // tm
    v_all = RET_HEADS * RET_V

    def body(o_ref, g_ref, x_ref, gn_ref, g1_ref, w_ref, xo_ref, y_ref, out_ref):
        for h in range(RET_HEADS):
            cols = slice(h * RET_V, (h + 1) * RET_V)
            ohat, _ = _group_norm(o_ref[:, cols])
            g = g_ref[:, cols].astype(F32)
            y_ref[:, cols] = (g * jax.nn.sigmoid(g) * (ohat * gn_ref[:, cols])).astype(BF16)
        out = _dot(y_ref[...], w_ref[...])
        out_ref[...] = out
        m, _, _ = _rms_fwd(out, g1_ref[...])
        xo_ref[...] = x_ref[...] + m

    row = pl.BlockSpec((tm, d), lambda i: (i, 0))
    wide = pl.BlockSpec((tm, v_all), lambda i: (i, 0))
    return pl.pallas_call(
        body, grid=(nt,),
        in_specs=[wide, pl.BlockSpec((tm, v_all), lambda i: (i, 2)), row, _full((1, v_all)), _full((1, d)),
                  _full(w_out.shape)],
        out_specs=[row, wide, row],
        out_shape=[jax.ShapeDtypeStruct((t, d), F32), jax.ShapeDtypeStruct((t, v_all), BF16),
                   jax.ShapeDtypeStruct((t, d), F32)],
        compiler_params=_params(1), name="ret_out_fwd")(o, pj, x, gn_gain, g_post, w_out)


def ret_out_bwd(dxo, out, o, pj, gn_gain, g_post, w_out, tm=256):
    t, d = out.shape
    nt = t // tm
    v_all = RET_HEADS * RET_V

    def body(dxo_ref, out_ref, o_ref, g_ref, gn_ref, g1_ref, w_ref,
             dout_ref, dgate_ref, do_ref, dg1_ref, dgn_ref):
        @pl.when(pl.program_id(0) == 0)
        def _():
            dg1_ref[...] = jnp.zeros_like(dg1_ref)
            dgn_ref[...] = jnp.zeros_like(dgn_ref)

        g1 = g1_ref[...]
        dxo = dxo_ref[...]
        _, oh_, r_ = _rms_fwd(out_ref[...], g1)
        dg1_ref[...] += _colsum(dxo * oh_)
        doutb = _rms_bwd(oh_, r_, g1, dxo).astype(BF16)
        dout_ref[...] = doutb
        dy = _dot_nt(doutb, w_ref[...])
        for h in range(RET_HEADS):
            cols = slice(h * RET_V, (h + 1) * RET_V)
            gn = gn_ref[:, cols]
            ohat, rstd = _group_norm(o_ref[:, cols])
            g = g_ref[:, cols].astype(F32)
            sg = jax.nn.sigmoid(g)
            dyh = dy[:, cols]
            dgate_ref[:, cols] = (dyh * (ohat * gn) * (sg * (1.0 + g * (1.0 - sg)))).astype(BF16)
            don = dyh * (g * sg)
            dgn_ref[:, cols] += _colsum(don * ohat)
            dohat = don * gn
            do_ref[:, cols] = (rstd * (dohat - jnp.mean(dohat, axis=-1, keepdims=True)
                                       - ohat * jnp.mean(dohat * ohat, axis=-1, keepdims=True))).astype(BF16)

    row = pl.BlockSpec((tm, d), lambda i: (i, 0))
    wide = pl.BlockSpec((tm, v_all), lambda i: (i, 0))
    gate = pl.BlockSpec((tm, v_all), lambda i: (i, 2))
    return pl.pallas_call(
        body, grid=(nt,),
        in_specs=[row, row, wide, gate, _full((1, v_all)), _full((1, d)), _full(w_out.shape)],
        out_specs=[row, gate, wide, _full((1, d)), _full((1, v_all))],
        out_shape=[jax.ShapeDtypeStruct((t, d), BF16), jax.ShapeDtypeStruct((t, RET_IN), BF16),
                   jax.ShapeDtypeStruct((t, v_all), BF16), jax.ShapeDtypeStruct((1, d), F32),
                   jax.ShapeDtypeStruct((1, v_all), F32)],
        compiler_params=_params(1), name="ret_out_bwd")(dxo, out, o, pj, gn_gain, g_post, w_out)


def ret_core_bwd(pj, do, sprev, cos, sin, dpj, intra, cross, inner, chunk_decay):
    t = pj.shape[0]
    nc = t // RET_CHUNK
    c = RET_CHUNK
    qk_all = RET_HEADS * RET_QK
    v_all = RET_HEADS * RET_V
    scale = RET_QK ** -0.5

    def body(q_ref, k_ref, v_ref, do_ref, sp_ref, c_ref, s_ref, dm_ref, cr_ref, in_ref, dpj_in, dpj_ref, dstate):
        @pl.when(pl.program_id(0) == 0)
        def _():
            dstate[...] = jnp.zeros_like(dstate)

        cs = c_ref[...]
        sn = s_ref[...]
        for h in range(RET_HEADS):
            q = q_ref[:, h * RET_QK:(h + 1) * RET_QK]
            k = k_ref[:, h * RET_QK:(h + 1) * RET_QK]
            v = v_ref[:, h * RET_V:(h + 1) * RET_V]
            doh = do_ref[:, h * RET_V:(h + 1) * RET_V]
            dm = dm_ref[h]
            ab = (_dot_nt(q, k) * dm).astype(BF16)
            dab = (_dot_nt(doh, v) * dm).astype(BF16)
            dsb = dstate[h].astype(BF16)
            kd = (k.astype(F32) * in_ref[h]).astype(BF16)
            dv = _dot_tn(ab, doh) + _dot(kd, dsb)
            dq = _dot(dab, k) + cr_ref[h] * _dot_nt(doh, sp_ref[h])
            dk = _dot_tn(dab, q) + in_ref[h] * _dot_nt(v, dsb)
            qd = (q.astype(F32) * cr_ref[h]).astype(BF16)
            dstate[h] = dstate[h] * chunk_decay[h] + _dot_tn(qd, doh)
            for base, dd, sc in ((h * RET_QK, dq, scale), (qk_all + h * RET_QK, dk, 1.0)):
                d1, d2 = dd[:, :128], dd[:, 128:]
                dpj_ref[:, base:base + 128] = ((d1 * cs + d2 * sn) * sc).astype(BF16)
                dpj_ref[:, base + 128:base + RET_QK] = ((d2 * cs - d1 * sn) * sc).astype(BF16)
            dpj_ref[:, 2 * qk_all + h * RET_V:2 * qk_all + (h + 1) * RET_V] = dv.astype(BF16)

    rev = lambda n: nc - 1 - n
    tab = pl.BlockSpec((c, 128), lambda n: (rev(n), 0))
    return pl.pallas_call(
        body, grid=(nc,),
        in_specs=[pl.BlockSpec((c, qk_all), lambda n: (rev(n), 0)), pl.BlockSpec((c, qk_all), lambda n: (rev(n), 1)),
                  pl.BlockSpec((c, v_all), lambda n: (rev(n), 1)), pl.BlockSpec((c, v_all), lambda n: (rev(n), 0)),
                  pl.BlockSpec((None, RET_HEADS, RET_QK, RET_V), lambda n: (rev(n), 0, 0, 0)),
                  tab, tab, _full(intra.shape), _full(cross.shape), _full(inner.shape), ANY],
        out_specs=pl.BlockSpec((c, 2 * qk_all + v_all), lambda n: (rev(n), 0)),
        out_shape=jax.ShapeDtypeStruct((t, RET_IN), BF16),
        scratch_shapes=[pltpu.VMEM((RET_HEADS, RET_QK, RET_V), F32)],
        input_output_aliases={10: 0},
        compiler_params=_params(1), name="ret_core_bwd")(pj, pj, pj, do, sprev, cos, sin, intra, cross, inner, dpj)


def ret_in_bwd(dpj, dres, x, g_pre, w_in, tm=512):
    t, d = x.shape
    nt = t // tm

    def body(dpj_ref, dres_ref, x_ref, g_ref, win_hbm, dx_ref, dg_ref, win_v, sem):
        @pl.when(pl.program_id(0) == 0)
        def _():
            cp = pltpu.make_async_copy(win_hbm, win_v, sem)
            cp.start()
            dg_ref[...] = jnp.zeros_like(dg_ref)
            cp.wait()

        g = g_ref[...]
        dh = jnp.zeros((tm, d), F32)
        for j in range(N_SHARD):
            dh = dh + _dot_nt(dpj_ref[:, j * RET_IN_SHARD:(j + 1) * RET_IN_SHARD], win_v[j])
        _, xh, rx = _rms_fwd(x_ref[...], g)
        dg_ref[...] += _colsum(dh * xh)
        dx_ref[...] = dres_ref[...] + _rms_bwd(xh, rx, g, dh)

    row = pl.BlockSpec((tm, d), lambda i: (i, 0))
    return pl.pallas_call(
        body, grid=(nt,),
        in_specs=[pl.BlockSpec((tm, RET_IN), lambda i: (i, 0)), row, row, _full((1, d)), ANY],
        out_specs=[row, _full((1, d))],
        out_shape=[jax.ShapeDtypeStruct((t, d), F32), jax.ShapeDtypeStruct((1, d), F32)],
        scratch_shapes=[pltpu.VMEM(w_in.shape, BF16), pltpu.SemaphoreType.DMA],
        compiler_params=_params(1), name="ret_in_bwd")(dpj, dres, x, g_pre, w_in)


_CHIP_FLIPS = ((1, 0), (0, 1), (1, 1))


def _flip(v, b):
    return 1 - v if b else v


_HBM = pl.BlockSpec(memory_space=pltpu.HBM)
_SEM = pl.BlockSpec(memory_space=pltpu.SEMAPHORE)
_EFFECT = pltpu.SideEffectType.DATAFLOW_SIDE_EFFECTING


def _chip_copies(mode, srcs, lands, send_sems, recv_sems):
    x, y, c = lax.axis_index("x"), lax.axis_index("y"), lax.axis_index("c")
    copies = []
    for t in range(len(lands)):
        if mode == "swap":
            copies.append(pltpu.make_async_remote_copy(
                src_ref=srcs[t], dst_ref=lands[t], send_sem=send_sems.at[t], recv_sem=recv_sems.at[t],
                device_id=(x, y, 1 - c), device_id_type=MESH))
            continue
        if mode == "everyone":
            for m in range(1, 8):
                bx, by, bc = (m >> 2) & 1, (m >> 1) & 1, m & 1
                copies.append(pltpu.make_async_remote_copy(
                    src_ref=srcs[t], dst_ref=lands[t].at[4 * x + 2 * y + c], send_sem=send_sems.at[7 * t + m - 1],
                    recv_sem=recv_sems.at[7 * t + m - 1], device_id=(_flip(x, bx), _flip(y, by), _flip(c, bc)),
                    device_id_type=MESH))
            continue
        for k, (bx, by) in enumerate(_CHIP_FLIPS):
            px, py = _flip(x, bx), _flip(y, by)
            target = (px, py, c)
            if mode == "gather":
                src, dst = srcs[t], lands[t].at[2 * x + y]
            elif mode == "gather_half":
                half = pl.ds(c * (srcs[t].shape[0] // 2), srcs[t].shape[0] // 2)
                src, dst = srcs[t].at[half], lands[t].at[2 * x + y, half]
            elif mode == "forward_half":
                half = pl.ds(c * (lands[t].shape[1] // 2), lands[t].shape[1] // 2)
                src = dst = lands[t].at[2 * px + py, half]
                target = (x, y, 1 - c)
            else:
                src, dst = srcs[t].at[2 * px + py], lands[t].at[k]
            copies.append(pltpu.make_async_remote_copy(
                src_ref=src, dst_ref=dst, send_sem=send_sems.at[3 * t + k], recv_sem=recv_sems.at[3 * t + k],
                device_id=target, device_id_type=MESH))
    return copies


def exchange_start(mode, srcs, lands, name, after=None):
    n, ns = len(lands), len(srcs)
    extra = [] if after is None else [after]

    def body(*refs):
        ins, lnd = refs[:ns], refs[ns:ns + n]
        send_sems, recv_sems = refs[ns + n + len(extra)], refs[ns + n + len(extra) + 1]
        token = refs[-1]
        for cp in _chip_copies(mode, ins, lnd, send_sems, recv_sems):
            cp.start()
        token[...] = jnp.zeros(token.shape, token.dtype)

    hbm = lambda a: pltpu.with_memory_space_constraint(a, pltpu.HBM)
    passed = list(srcs) + list(lands)
    n_sem = {"swap": 1, "everyone": 7}.get(mode, 3) * n
    return pl.pallas_call(
        body, name=name,
        out_shape=(pltpu.SemaphoreType.DMA((n_sem,)), pltpu.SemaphoreType.DMA((n_sem,)),
                   *[pltpu.HBM(a.shape, a.dtype) for a in passed], jax.ShapeDtypeStruct((8, 128), F32)),
        in_specs=[_HBM] * (ns + n) + [ANY] * len(extra),
        out_specs=(_SEM, _SEM, *[_HBM] * (ns + n), pl.BlockSpec(memory_space=pltpu.VMEM)),
        input_output_aliases={i: 2 + i for i in range(ns + n)},
        compiler_params=pltpu.CompilerParams(has_side_effects=_EFFECT))(*[hbm(a) for a in passed], *extra)


def exchange_wait(mode, started, after, name):
    send_sems, recv_sems = started[0], started[1]
    passed = list(started[2:-1])
    n = len(passed) if mode == "forward_half" else len(passed) // 2
    ns = len(passed) - n

    def body(*refs):
        ins, lnd = refs[:ns], refs[ns:ns + n]
        for cp in _chip_copies(mode, ins, lnd, refs[ns + n], refs[ns + n + 1]):
            cp.wait_send()
            cp.wait_recv()

    outs = pl.pallas_call(
        body, name=name, out_shape=tuple(pltpu.HBM(a.shape, a.dtype) for a in passed),
        in_specs=[_HBM] * (ns + n) + [_SEM, _SEM, ANY], out_specs=tuple([_HBM] * (ns + n)),
        input_output_aliases={i: i for i in range(ns + n)},
        compiler_params=pltpu.CompilerParams(has_side_effects=_EFFECT))(*passed, send_sems, recv_sems, after)
    return list(outs[:ns]), list(outs[ns:])


def plane_sum(slot, full, recv, name, bm=256):
    _, m, n = full.shape
    bm = _row_block(m, bm)

    def body(slot_ref, o_ref, r_ref, s_ref):
        s_ref[...] = ((o_ref[...] + r_ref[0].astype(F32)) + r_ref[1].astype(F32)) + r_ref[2].astype(F32)

    return pl.pallas_call(
        body,
        grid_spec=pltpu.PrefetchScalarGridSpec(
            num_scalar_prefetch=1, grid=(m // bm,),
            in_specs=[pl.BlockSpec((None, bm, n), lambda i, s: (s[0], i, 0)),
                      pl.BlockSpec((3, bm, n), lambda i, s: (0, i, 0))],
            out_specs=pl.BlockSpec((bm, n), lambda i, s: (i, 0))),
        out_shape=jax.ShapeDtypeStruct((m, n), F32), compiler_params=_params(1), name=name)(slot, full, recv)


def sum_slots(parts, name, bm=312):
    _, r, n = parts.shape
    bm = bm if r % bm == 0 else r

    def body(p_ref, s_ref):
        acc = p_ref[0]
        for k in range(1, 8):
            acc = acc + p_ref[k]
        s_ref[...] = acc

    return pl.pallas_call(
        body, grid=(r // bm,), in_specs=[pl.BlockSpec((8, bm, n), lambda i: (0, i, 0))],
        out_specs=pl.BlockSpec((bm, n), lambda i: (i, 0)), out_shape=jax.ShapeDtypeStruct((r, n), F32),
        compiler_params=_params(1), name=name)(parts)


def _adamw_math(w, g, m, v):
    m = ADAM_B1 * m + (1.0 - ADAM_B1) * g
    v = ADAM_B2 * v + (1.0 - ADAM_B2) * (g * g)
    m_hat = m / (1.0 - ADAM_B1 ** ADAM_STEP)
    v_hat = v / (1.0 - ADAM_B2 ** ADAM_STEP)
    delta = -ADAM_LR * (m_hat / (jnp.sqrt(v_hat) + ADAM_EPS) + ADAM_WD * w)
    return delta, m, v


def adamw(w, m, v, grads, layer, prev, name, bm=256, row0=0):
    _, _, n = w.shape
    mm = grads[0].shape[0]
    bm = _row_block(mm, bm)
    first = row0 // bm
    ng = len(grads)

    def body(*refs):
        w_ref, m_ref, v_ref = refs[:3]
        g_refs = refs[3:3 + ng]
        g_out, d_out, m_out, v_out = refs[-4:]
        g = g_refs[0][...]
        for gr in g_refs[1:]:
            g = g + gr[...]
        delta, mn, vn = _adamw_math(w_ref[...], g, m_ref[...], v_ref[...])
        g_out[...] = g
        d_out[...] = delta
        m_out[...] = mn
        v_out[...] = vn

    slab = pl.BlockSpec((None, bm, n), lambda i: (layer, first + i, 0))
    flat = pl.BlockSpec((bm, n), lambda i: (i, 0))
    in_specs = [slab] * 3 + [flat] * ng
    args = [w, m, v, *grads]
    aliases = {}
    if prev is not None:
        in_specs += [ANY] * 4
        aliases = {3 + ng + q: q for q in range(4)}
        args += list(prev)
    return pl.pallas_call(
        body, grid=(mm // bm,), in_specs=in_specs, out_specs=[slab] * 4,
        out_shape=[jax.ShapeDtypeStruct(w.shape, F32)] * 4, input_output_aliases=aliases,
        compiler_params=_params(1), name=name)(*args)


def _pack_rows(parts, rows):
    flat = jnp.concatenate([p.reshape(-1) for p in parts])
    return jnp.pad(flat, (0, rows * 128 - flat.shape[0])).reshape(rows, 128)


def _as_shards(a, rows):
    return a.reshape(N_SHARD, rows, a.shape[-1])


def _local_step(x, pos_col, target, gains, pool_w, pool_scale, gn_gain, conv_w, conv_b, weights, send_grads):
    def gain(l, n, token=None):
        g = gains[l, n].reshape(1, D_MODEL)
        return g if token is None else g + token[0:1, 0:1]

    inv_freq = (ROPE_BASE ** (-jnp.arange(0, RET_QK, 2, dtype=F32) / RET_QK)).reshape(1, RET_QK // 2)
    intra, cross, inner, chunk_decay = _decay_tables()
    dn_rows = D_FF // N_SHARD

    x1 = pool_fwd(x, gain(0, 0), gain(0, 1), pool_w, pool_scale)
    cos, sin = rope_tables(pos_col, inv_freq, x1)
    w_up0, w_dn0 = weights("mlp0", cos)
    w_dn0 = w_dn0.reshape(D_FF, D_MODEL)
    x2, u0, s0, a0, f0, h0 = mlp_fwd(x1, gain(0, 2), gain(0, 3), w_up0, w_dn0, conv_w[0], conv_b[0])
    w_in, w_out = weights("ret", x2)
    w_out = w_out.reshape(RET_HEADS * RET_V, D_MODEL)
    pj, hr = ret_proj(x2, gain(1, 0), w_in, cos, sin)
    o, sprev = ret_core_fwd(pj, intra, cross, inner, chunk_decay)
    x3, yb, out = ret_out_fwd(o, pj, x2, gn_gain, gain(1, 1), w_out)
    w_up1, w_dn1 = weights("mlp1", x3)
    w_dn1 = w_dn1.reshape(D_FF, D_MODEL)
    dx4, u1, s1, a1, f1, h1, loss = mlp_fwd(x3, gain(1, 2), gain(1, 3), w_up1, w_dn1, conv_w[1], conv_b[1], target)

    dx3, du1, df1, dg12, dg13, dcw1, dcb1 = mlp_bwd(
        dx4, f1, x3, u1, s1, gain(1, 2), gain(1, 3), w_up1, w_dn1, conv_w[1])
    dwup1 = grad_matmul(h1, du1, D_MODEL, FF_CHUNK, "grad_w_up_1", interleaved=True)
    dwdn1 = grad_matmul(a1, df1, FF_CHUNK, D_MODEL, "grad_w_down_1")
    tok = send_grads("mlp1", [dwup1, [_as_shards(g, dn_rows) for g in dwdn1]])
    dout, dpj, do, dg11, dgn = ret_out_bwd(dx3, out, o, pj, gn_gain, gain(1, 1, tok), w_out)
    dwout = grad_matmul(yb, dout, 1024, D_MODEL, "grad_w_out")
    dpj = ret_core_bwd(pj, do, sprev, cos, sin, dpj, intra, cross, inner, chunk_decay)
    dwin = grad_matmul(hr, dpj, D_MODEL, RET_IN_SHARD, "grad_w_in")
    tok = send_grads("ret", [dwin, [_as_shards(g, RET_V) for g in dwout]])
    dx2, dg10 = ret_in_bwd(dpj, dx3, x2, gain(1, 0, tok), w_in)
    dx1, du0, df0, dg02, dg03, dcw0, dcb0 = mlp_bwd(
        dx2, f0, x1, u0, s0, gain(0, 2), gain(0, 3), w_up0, w_dn0, conv_w[0])
    dwdn0 = grad_matmul(a0, df0, FF_CHUNK, D_MODEL, "grad_w_down_0")
    tok = send_grads("down0", [[_as_shards(g, dn_rows) for g in dwdn0]])
    half = D_MODEL // 2
    for part, first in (("a", 0), ("b", half)):
        dwup0 = grad_matmul(h0, du0, half, FF_CHUNK, "grad_w_up_0" + part, tk=4096, interleaved=True, after=tok,
                            cols=(first, half))
        tok = send_grads("up0" + part, [dwup0])
    dx0, dg00, dg01, dpscale, dpw = pool_bwd(dx1, x, gain(0, 0, tok), gain(0, 1), pool_w, pool_scale)

    rows = lambda g: jnp.sum(g, axis=0, keepdims=True)
    dgains = jnp.concatenate([dg00, dg01, rows(dg02), rows(dg03), dg10, dg11, rows(dg12), rows(dg13)],
                             axis=0).reshape(2, 4, D_MODEL)
    small = {"gains": dgains, "pool_scale": dpscale, "gn": dgn,
             "conv_w": jnp.sum(jnp.stack([dcw0, dcw1]), axis=3),
             "conv_b": jnp.sum(jnp.stack([dcb0, dcb1]), axis=2, keepdims=True), "pool_w": dpw}
    return loss, dx0, small


def kernel(x, positions, norm_gain, pool_w, pool_scale, ret_w_in, ret_gn_gain, ret_w_out, mlp_w_up, mlp_conv_w, mlp_conv_b, mlp_w_down, loss_target, m_norm_gain, m_pool_w, m_pool_scale, m_ret_w_in, m_ret_gn_gain, m_ret_w_out, m_mlp_w_up, m_mlp_conv_w, m_mlp_conv_b, m_mlp_w_down, v_norm_gain, v_pool_w, v_pool_scale, v_ret_w_in, v_ret_gn_gain, v_ret_w_out, v_mlp_w_up, v_mlp_conv_w, v_mlp_conv_b, v_mlp_w_down):
    t = x.shape[1]
    me = 2 * lax.axis_index("x") + lax.axis_index("y")
    me_slot = jnp.reshape(me, (1,)).astype(jnp.int32)

    small_parts = [norm_gain, ret_gn_gain, mlp_conv_w, pool_w]
    small_sizes = [p.size for p in small_parts]
    small_rows = -(-sum(small_sizes) // (128 * 8)) * 8
    gathers = {}

    def start_gather(group, srcs, after):
        lands = [lax.dynamic_update_index_in_dim(lax.empty((N_SHARD,) + s.shape, s.dtype), s, me, 0) for s in srcs]
        mode = "gather_half" if group == "mlp0" else "gather"
        gathers[group] = (mode, exchange_start(mode, srcs, lands, "gather_start_" + group, after=after))
        return gathers[group][1][-1]

    token = start_gather("mlp0", [mlp_w_up[0].astype(BF16), mlp_w_down[0].astype(BF16)], None)
    token = start_gather("small", [_pack_rows(small_parts, small_rows)], token)

    def weights(group, after):
        if group == "mlp0":
            tok = start_gather("ret", [ret_w_in[0].astype(BF16), ret_w_out[0].astype(BF16)], after)
            start_gather("mlp1", [mlp_w_up[1].astype(BF16), mlp_w_down[1].astype(BF16)], tok)
        mode, started = gathers[group]
        _, lands = exchange_wait(mode, started, after, "gather_wait_" + group)
        if mode == "gather_half":
            forward = exchange_start("forward_half", [], lands, "forward_start_" + group)
            _, lands = exchange_wait("forward_half", forward, forward[-1], "forward_wait_" + group)
        return lands

    sent, early = {}, {}

    def reduced(group, after, names):
        started, own = sent[group]
        _, recv = exchange_wait("scatter", started, after, "scatter_wait_" + group)
        return [plane_sum(me_slot, f, r, "plane_sum_" + nm)
                for f, r, nm in zip(own, recv, names)]

    def swap_start(planes, name):
        return exchange_start("swap", planes, [lax.empty(p.shape, p.dtype) for p in planes], name)

    def send_grads(group, pairs):
        lands = [lax.empty((3,) + b.shape[1:], BF16) for _, b in pairs]
        sent[group] = (exchange_start("scatter", [b for _, b in pairs], lands, "scatter_start_" + group),
                       [f for f, _ in pairs])
        token = sent[group][0][-1]
        if group == "down0":
            marker = pairs[0][1]
            early["planes"] = (reduced("mlp1", marker, ["w_up_1", "w_down_1"])
                               + reduced("ret", marker, ["w_in", "w_out"]))
            early["swap"] = swap_start(early["planes"], "swap_start_a")
            token = token + early["swap"][-1]
        return token

    (smallg,) = weights("small", token)
    smallg = smallg.reshape(N_SHARD, -1)
    offs = np.cumsum([0] + small_sizes)
    piece = lambda i, shape: smallg[:, offs[i]:offs[i + 1]].reshape((N_SHARD,) + shape)
    gains = piece(0, (2, 4, 256)).transpose(1, 2, 0, 3).reshape(2, 4, D_MODEL)
    gn_full = piece(1, (512,)).reshape(1, RET_HEADS * RET_V)
    cw_full = piece(2, (2, 3, FF_CHUNK)).transpose(1, 0, 2, 3)
    pw_full = piece(3, (4, 64, 256)).transpose(1, 0, 2, 3).reshape(4, 256, 256).astype(BF16)
    cb_full = mlp_conv_b.reshape(2, N_SHARD, 1, FF_CHUNK)

    loss, dx0, small = _local_step(
        x[0], positions.reshape(t, 1).astype(F32), loss_target[0], gains, pw_full, pool_scale, gn_full,
        cw_full, cb_full, weights, send_grads)

    def small_adamw(w, m, v, grads, name):
        w3 = w.reshape(1, -1, w.shape[-1])
        out = adamw(w3, m.reshape(w3.shape), v.reshape(w3.shape), [g.reshape(w3.shape[1:]) for g in grads], 0, None, name)
        return [o.reshape(w.shape) for o in out]

    pw_f = small["pool_w"].reshape(4, N_SHARD, 64, 256).transpose(1, 0, 2, 3).reshape(N_SHARD, 256, 256)
    small_order = ["gains", "pool_scale", "gn", "conv_w", "conv_b"]
    gsmall_sizes = [small[k].size for k in small_order]
    gsmall_rows = -(-sum(gsmall_sizes) // (128 * 8)) * 8
    gpack = _pack_rows([small[k] for k in small_order], gsmall_rows)
    mine = 2 * me + lax.axis_index("c")
    small_started = exchange_start(
        "everyone", [gpack], [lax.dynamic_update_index_in_dim(lax.empty((8,) + gpack.shape, F32), gpack, mine, 0)],
        "small_start")
    send_grads("pool_w", [(pw_f, pw_f.astype(BF16))])

    res = {}
    planes_a, others_a = exchange_wait("swap", early["swap"], small_started[-1], "swap_wait_a")
    res["ret_w_in"] = adamw(ret_w_in, m_ret_w_in, v_ret_w_in, (planes_a[2], others_a[2]), 0, None, "adamw_w_in")
    res["ret_w_out"] = adamw(ret_w_out, m_ret_w_out, v_ret_w_out, (planes_a[3], others_a[3]), 0, None, "adamw_w_out")
    up1 = adamw(mlp_w_up, m_mlp_w_up, v_mlp_w_up, (planes_a[0], others_a[0]), 1, None, "adamw_w_up_1")
    dn1 = adamw(mlp_w_down, m_mlp_w_down, v_mlp_w_down, (planes_a[1], others_a[1]), 1, None, "adamw_w_down_1")

    planes_b = (reduced("up0a", dn1[0], ["w_up_0a"]) + reduced("up0b", dn1[0], ["w_up_0b"])
                + reduced("down0", dn1[0], ["w_down_0"]) + reduced("pool_w", dn1[0], ["pool_w"]))
    swap_b = swap_start(planes_b, "swap_start_b")

    _, (small_recv,) = exchange_wait("everyone", small_started, swap_b[-1], "small_wait")
    gsmall = sum_slots(small_recv, "sum_small").reshape(-1)
    goffs = np.cumsum([0] + gsmall_sizes)
    gpiece = lambda i: gsmall[goffs[i]:goffs[i + 1]].reshape(small[small_order[i]].shape)
    g_gains = lax.dynamic_slice_in_dim(gpiece(0), me * 256, 256, axis=2)
    g_gn = lax.dynamic_slice_in_dim(gpiece(2), me * RET_V, RET_V, axis=1)
    g_cw = lax.dynamic_index_in_dim(gpiece(3), me, 1, keepdims=False)
    res["norm_gain"] = small_adamw(norm_gain, m_norm_gain, v_norm_gain, [g_gains], "adamw_norm_gain")
    res["pool_scale"] = small_adamw(pool_scale, m_pool_scale, v_pool_scale, [gpiece(1)], "adamw_pool_scale")
    res["ret_gn_gain"] = small_adamw(ret_gn_gain, m_ret_gn_gain, v_ret_gn_gain, [g_gn], "adamw_gn_gain")
    res["mlp_conv_w"] = small_adamw(mlp_conv_w, m_mlp_conv_w, v_mlp_conv_w, [g_cw], "adamw_conv_w")
    res["mlp_conv_b"] = small_adamw(mlp_conv_b, m_mlp_conv_b, v_mlp_conv_b, [gpiece(4)], "adamw_conv_b")

    planes_b, others_b = exchange_wait("swap", swap_b, res["mlp_conv_b"][0], "swap_wait_b")
    up0a = adamw(mlp_w_up, m_mlp_w_up, v_mlp_w_up, (planes_b[0], others_b[0]), 0, up1, "adamw_w_up_0a")
    res["mlp_w_up"] = adamw(mlp_w_up, m_mlp_w_up, v_mlp_w_up, (planes_b[1], others_b[1]), 0, up0a, "adamw_w_up_0b",
                            row0=D_MODEL // 2)
    res["mlp_w_down"] = adamw(mlp_w_down, m_mlp_w_down, v_mlp_w_down, (planes_b[2], others_b[2]), 0, dn1,
                              "adamw_w_down_0")
    res["pool_w"] = small_adamw(pool_w, m_pool_w, v_pool_w, (planes_b[3], others_b[3]), "adamw_pool_w")

    order = ["norm_gain", "pool_w", "pool_scale", "ret_w_in", "ret_gn_gain", "ret_w_out", "mlp_w_up", "mlp_conv_w",
             "mlp_conv_b", "mlp_w_down"]
    total_loss = lax.psum(loss[0, 0], ("x", "y", "c"))
    outs = [total_loss, dx0.reshape(x.shape)]
    for q in range(4):
        outs += [res[k][q] for k in order]
    return tuple(outs)
```

```python
import numpy as np
import jax
import jax.numpy as jnp
from jax import lax
from jax.experimental import pallas as pl
from jax.experimental.pallas import tpu as pltpu

F32 = jnp.float32
BF16 = jnp.bfloat16

D_MODEL = 1024
D_FF = 2816
FF_CHUNK = 1408
N_SHARD = 4
POOL_WINDOWS = (2, 4, 8, 16)
POOL_DIM = 256
POOL_HALO = 16
RET_HEADS = 4
RET_QK = 256
RET_V = 512
RET_CHUNK = 256
RET_IN = 6144
RET_IN_SHARD = 1536
ROPE_BASE = 10000.0
EPS = 1e-6
CONV_HALO = 8

ADAM_LR, ADAM_B1, ADAM_B2, ADAM_EPS, ADAM_WD, ADAM_STEP = 0.001, 0.9, 0.999, 1e-08, 0.01, 10

VMEM_LIMIT = 56 * 1024 * 1024
VMEM_LIMIT_MLP_BWD = 62 * 1024 * 1024
MESH = pl.DeviceIdType.MESH
ANY = pl.BlockSpec(memory_space=pl.ANY)


def _params(n_grid=1, limit=VMEM_LIMIT):
    return pltpu.CompilerParams(dimension_semantics=("arbitrary",) * n_grid, vmem_limit_bytes=limit)


def _dot(a, b):
    return jnp.dot(a, b, preferred_element_type=F32)


def _dot_nt(a, b):
    return lax.dot_general(a, b, (((1,), (1,)), ((), ())), preferred_element_type=F32)


def _dot_tn(a, b):
    return lax.dot_general(a, b, (((0,), (0,)), ((), ())), preferred_element_type=F32)


def _rms_fwd(x, gain):
    r = lax.rsqrt(jnp.mean(x * x, axis=-1, keepdims=True) + EPS)
    xh = x * r
    return xh * gain, xh, r


def _rms_bwd(xh, r, gain, dy):
    dxh = dy * gain
    return r * (dxh - xh * jnp.mean(dxh * xh, axis=-1, keepdims=True))


def _colsum(v):
    return jnp.sum(v, axis=0, keepdims=True)


def _full(shape):
    nd = len(shape)
    return pl.BlockSpec(shape, lambda *_: (0,) * nd)


def rope_tables(pos_col, inv_freq, after):
    t = pos_col.shape[0]
    tm = min(t, 1024)

    def body(p_ref, f_ref, after_ref, c_ref, s_ref):
        ang = p_ref[...] * f_ref[...]
        c_ref[...] = jnp.cos(ang)
        s_ref[...] = jnp.sin(ang)

    return pl.pallas_call(
        body, grid=(t // tm,),
        in_specs=[pl.BlockSpec((tm, 1), lambda i: (i, 0)), _full((1, 128)), ANY],
        out_specs=[pl.BlockSpec((tm, 128), lambda i: (i, 0))] * 2,
        out_shape=[jax.ShapeDtypeStruct((t, 128), F32)] * 2,
        compiler_params=_params(1), name="rope_tables")(pos_col, inv_freq, after)


def _window_sums(ext, backward):
    n = ext.shape[0]
    cur, sums = ext, []
    for g, win in enumerate(POOL_WINDOWS):
        if g > 0:
            cur = cur[:, POOL_DIM:]
        half = win // 2
        cur = cur + pltpu.roll(cur, n - half if backward else half, axis=0)
        sums.append(cur[:, 0:POOL_DIM])
    return sums


def _pool_diff(h_halo, h, row0, tm):
    t_idx = row0 + lax.broadcasted_iota(jnp.int32, (tm, 1), 0)
    sums = _window_sums(jnp.concatenate([h_halo, h], axis=0), backward=False)
    parts, inv_counts = [], []
    for g, win in enumerate(POOL_WINDOWS):
        inv = 1.0 / jnp.minimum(t_idx + 1, win).astype(F32)
        parts.append(sums[g][POOL_HALO:, :] * inv - h[:, g * POOL_DIM:(g + 1) * POOL_DIM])
        inv_counts.append(inv)
    return parts, inv_counts


def pool_fwd(x, g_pre, g_post, pool_w, pool_scale, tm=512):
    t, d = x.shape
    nt = t // tm

    def body(x_ref, g0_ref, g1_ref, w_ref, sc_ref, o_ref, hext):
        i = pl.program_id(0)

        @pl.when(i == 0)
        def _():
            hext[...] = jnp.zeros((POOL_HALO, d), F32)

        xv = x_ref[...]
        h, _, _ = _rms_fwd(xv, g0_ref[...])
        parts, _ = _pool_diff(hext[...], h, i * tm, tm)
        hext[...] = h[tm - POOL_HALO:tm, :]
        ys = [_dot(parts[g].astype(BF16), w_ref[g]) for g in range(len(POOL_WINDOWS))]
        y = jnp.concatenate(ys, axis=-1) * sc_ref[...]
        m, _, _ = _rms_fwd(y, g1_ref[...])
        o_ref[...] = xv + m

    row = pl.BlockSpec((tm, d), lambda i: (i, 0))
    return pl.pallas_call(
        body, grid=(nt,),
        in_specs=[row, _full((1, d)), _full((1, d)), _full(pool_w.shape), _full((1, d))],
        out_specs=row, out_shape=jax.ShapeDtypeStruct((t, d), F32),
        scratch_shapes=[pltpu.VMEM((POOL_HALO, d), F32)],
        compiler_params=_params(1), name="pool_fwd")(x, g_pre, g_post, pool_w, pool_scale)


def pool_bwd(dx1, x, g_pre, g_post, pool_w, pool_scale, tm=512):
    t, d = x.shape
    nt = t // tm
    ng = len(POOL_WINDOWS)

    def body(dx1_ref, x_ref, xh_ref, g0_ref, g1_ref, w_ref, sc_ref,
             dx_ref, dg0_ref, dg1_ref, dsc_ref, dw_ref, enext):
        i = pl.program_id(0)
        r = nt - 1 - i

        @pl.when(i == 0)
        def _():
            enext[...] = jnp.zeros((POOL_HALO, d), F32)
            dg0_ref[...] = jnp.zeros_like(dg0_ref)
            dg1_ref[...] = jnp.zeros_like(dg1_ref)
            dsc_ref[...] = jnp.zeros_like(dsc_ref)
            dw_ref[...] = jnp.zeros_like(dw_ref)

        g0 = g0_ref[...]
        g1 = g1_ref[...]
        sc = sc_ref[...]
        xv = x_ref[...]
        h, xh, rx = _rms_fwd(xv, g0)
        h_halo, _, _ = _rms_fwd(xh_ref[...], g0)
        parts, inv_counts = _pool_diff(h_halo * jnp.where(r > 0, 1.0, 0.0), h, r * tm, tm)
        parts_b = [p.astype(BF16) for p in parts]
        ypre = jnp.concatenate([_dot(parts_b[g], w_ref[g]) for g in range(ng)], axis=-1)
        _, yh, ry = _rms_fwd(ypre * sc, g1)
        dm = dx1_ref[...]
        dg1_ref[...] += _colsum(dm * yh)
        dy = _rms_bwd(yh, ry, g1, dm)
        dsc_ref[...] += _colsum(dy * ypre)
        dyp = (dy * sc).astype(BF16)
        ddiffs = []
        for g in range(ng):
            cols = slice(g * POOL_DIM, (g + 1) * POOL_DIM)
            dw_ref[g] += _dot_tn(parts_b[g], dyp[:, cols])
            ddiffs.append(_dot_nt(dyp[:, cols], w_ref[g]))
        e = jnp.concatenate([ddiffs[g] * inv_counts[g] for g in range(ng)], axis=-1)
        sums = _window_sums(jnp.concatenate([e, enext[...]], axis=0), backward=True)
        enext[...] = e[0:POOL_HALO, :]
        dh = jnp.concatenate([sums[g][0:tm, :] - ddiffs[g] for g in range(ng)], axis=-1)
        dg0_ref[...] += _colsum(dh * xh)
        dx_ref[...] = dm + _rms_bwd(xh, rx, g0, dh)

    row = pl.BlockSpec((tm, d), lambda i: (nt - 1 - i, 0))
    halo = pl.BlockSpec((POOL_HALO, d), lambda i: (jnp.maximum((nt - 1 - i) * (tm // POOL_HALO) - 1, 0), 0))
    vec = _full((1, d))
    return pl.pallas_call(
        body, grid=(nt,),
        in_specs=[row, row, halo, vec, vec, _full(pool_w.shape), vec],
        out_specs=[row, vec, vec, vec, _full((ng, POOL_DIM, POOL_DIM))],
        out_shape=[jax.ShapeDtypeStruct((t, d), F32)] + [jax.ShapeDtypeStruct((1, d), F32)] * 3
        + [jax.ShapeDtypeStruct((ng, POOL_DIM, POOL_DIM), F32)],
        scratch_shapes=[pltpu.VMEM((POOL_HALO, d), F32)],
        compiler_params=_params(1), name="pool_bwd")(dx1, x, x, g_pre, g_post, pool_w, pool_scale)


def _conv_taps(cw_ref, j):
    return cw_ref[j, 0:1, :], cw_ref[j, 1:2, :], cw_ref[j, 2:3, :]


def _row_block(m, target=256):
    if m <= target:
        return m
    for b in range(target, 7, -8):
        if m % b == 0:
            return b
    return m


def mlp_fwd(x, g_pre, g_post, w_up, w_down, conv_w, conv_b, target=None, tm=256):
    t, d = x.shape
    nt = t // tm
    h8 = CONV_HALO
    with_loss = target is not None
    n_extra = 1 if with_loss else 0

    def body(x_ref, g2_ref, g3_ref, wup_hbm, wdn_hbm, cw_ref, cb_ref, *rest):
        tgt_ref = rest[0] if with_loss else None
        xo_ref, u_ref, s_ref, a_ref, f_ref, h_ref = rest[n_extra:n_extra + 6]
        loss_ref = rest[n_extra + 6] if with_loss else None
        wup_v, wdn_v, tail, sem = rest[-4:]
        i = pl.program_id(0)

        @pl.when(i == 0)
        def _():
            c1 = pltpu.make_async_copy(wup_hbm, wup_v, sem.at[0])
            c2 = pltpu.make_async_copy(wdn_hbm, wdn_v, sem.at[1])
            c1.start()
            c2.start()
            tail[...] = jnp.zeros_like(tail)
            if with_loss:
                loss_ref[...] = jnp.zeros_like(loss_ref)
            c1.wait()
            c2.wait()

        xv = x_ref[...]
        h, _, _ = _rms_fwd(xv, g2_ref[...])
        hb = h.astype(BF16)
        h_ref[...] = hb
        acc = jnp.zeros((tm, d), F32)
        for k in range(2):
            cs = []
            for s in range(2):
                j, cols = k + 2 * s, slice((2 * k + s) * FF_CHUNK, (2 * k + s + 1) * FF_CHUNK)
                uf = _dot(hb, wup_v[j])
                u_ref[:, cols] = uf.astype(BF16)
                ext = jnp.concatenate([tail[j], uf], axis=0)
                tail[j] = uf[tm - h8:tm, :]
                w0, w1, w2 = _conv_taps(cw_ref, j)
                cs.append(cb_ref[j] + w2 * uf + w1 * pltpu.roll(ext, 1, axis=0)[h8:, :]
                          + w0 * pltpu.roll(ext, 2, axis=0)[h8:, :])
            cg, cv = cs
            sg = jax.nn.sigmoid(cg)
            sil = cg * sg
            ab = (sil * cv).astype(BF16)
            a_ref[:, k * FF_CHUNK:(k + 1) * FF_CHUNK] = ab
            s_ref[:, 2 * k * FF_CHUNK:(2 * k + 1) * FF_CHUNK] = sil.astype(BF16)
            s_ref[:, (2 * k + 1) * FF_CHUNK:(2 * k + 2) * FF_CHUNK] = (cv * (sg * (1.0 + cg * (1.0 - sg)))).astype(BF16)
            acc = acc + _dot(ab, wdn_v[k * FF_CHUNK:(k + 1) * FF_CHUNK, :])
        f_ref[...] = acc
        y, _, _ = _rms_fwd(acc, g3_ref[...])
        if with_loss:
            err = (xv + y) - tgt_ref[...]
            xo_ref[...] = err * (1.0 / d)
            loss_ref[...] += 0.5 * jnp.sum(jnp.mean(err * err, axis=-1, keepdims=True), axis=0, keepdims=True)
        else:
            xo_ref[...] = xv + y

    row = pl.BlockSpec((tm, d), lambda i: (i, 0))
    wide = pl.BlockSpec((tm, 2 * D_FF), lambda i: (i, 0))
    vec = _full((1, d))
    extra = [target] if with_loss else []
    return pl.pallas_call(
        body, grid=(nt,),
        in_specs=[row, vec, vec, ANY, ANY, _full(conv_w.shape), _full(conv_b.shape)] + [row] * n_extra,
        out_specs=[row, wide, wide, pl.BlockSpec((tm, D_FF), lambda i: (i, 0)), row, row] + [_full((1, 1))] * n_extra,
        out_shape=[jax.ShapeDtypeStruct((t, d), F32), jax.ShapeDtypeStruct((t, 2 * D_FF), BF16),
                   jax.ShapeDtypeStruct((t, 2 * D_FF), BF16), jax.ShapeDtypeStruct((t, D_FF), BF16),
                   jax.ShapeDtypeStruct((t, d), F32), jax.ShapeDtypeStruct((t, d), BF16)]
        + [jax.ShapeDtypeStruct((1, 1), F32)] * n_extra,
        scratch_shapes=[pltpu.VMEM(w_up.shape, BF16), pltpu.VMEM(w_down.shape, BF16),
                        pltpu.VMEM((N_SHARD, h8, FF_CHUNK), F32), pltpu.SemaphoreType.DMA((2,))],
        compiler_params=_params(1), name="mlp_fwd_loss" if with_loss else "mlp_fwd")(
            x, g_pre, g_post, w_up, w_down, conv_w, conv_b, *extra)


def _rowsum8(v):
    return jnp.sum(v.reshape(v.shape[0] // 8, 8, v.shape[1]), axis=0)


def mlp_bwd(dxo, f, x, u, sp, g_pre, g_post, w_up, w_down, conv_w, tm=256):
    t, d = x.shape
    nt = t // tm
    h8 = CONV_HALO

    def body(dxo_ref, f_ref, x_ref, u_ref, s_ref, g2_ref, g3_ref, wup_hbm, wdn_hbm, cw_ref,
             dx_ref, du_ref, df_ref, dg2_ref, dg3_ref, dcw_ref, dcb_ref,
             wup_v, wdn_v, carry, sem):
        @pl.when(pl.program_id(0) == 0)
        def _():
            c1 = pltpu.make_async_copy(wup_hbm, wup_v, sem.at[0])
            c2 = pltpu.make_async_copy(wdn_hbm, wdn_v, sem.at[1])
            c1.start()
            c2.start()
            carry[...] = jnp.zeros_like(carry)
            dg2_ref[...] = jnp.zeros_like(dg2_ref)
            dg3_ref[...] = jnp.zeros_like(dg3_ref)
            dcw_ref[...] = jnp.zeros_like(dcw_ref)
            dcb_ref[...] = jnp.zeros_like(dcb_ref)
            c1.wait()
            c2.wait()

        g3 = g3_ref[...]
        dxo = dxo_ref[...]
        _, fh, rf = _rms_fwd(f_ref[...], g3)
        dg3_ref[...] += _rowsum8(dxo * fh)
        dfb = _rms_bwd(fh, rf, g3, dxo).astype(BF16)
        df_ref[...] = dfb
        dh = jnp.zeros((tm, d), F32)
        for k in range(2):
            da = _dot_nt(dfb, wdn_v[k * FF_CHUNK:(k + 1) * FF_CHUNK, :])
            for s in range(2):
                j = k + 2 * s
                cols = slice((2 * k + s) * FF_CHUNK, (2 * k + s + 1) * FF_CHUNK)
                dc = da * s_ref[:, (2 * k + 1 - s) * FF_CHUNK:(2 * k + 2 - s) * FF_CHUNK].astype(F32)
                uf = u_ref[:, cols].astype(F32)
                ext = jnp.concatenate([dc, carry[j]], axis=0)
                carry[j] = dc[0:h8, :]
                dc1 = pltpu.roll(ext, tm + h8 - 1, axis=0)[0:tm, :]
                dc2 = pltpu.roll(ext, tm + h8 - 2, axis=0)[0:tm, :]
                dcb_ref[j] += _rowsum8(dc)
                dcw_ref[j, 2] += _rowsum8(dc * uf)
                dcw_ref[j, 1] += _rowsum8(dc1 * uf)
                dcw_ref[j, 0] += _rowsum8(dc2 * uf)
                dub = (cw_ref[j, 2:3, :] * dc + cw_ref[j, 1:2, :] * dc1 + cw_ref[j, 0:1, :] * dc2).astype(BF16)
                du_ref[:, cols] = dub
                dh = dh + _dot_nt(dub, wup_v[j])
        g2 = g2_ref[...]
        _, xh, rx = _rms_fwd(x_ref[...], g2)
        dg2_ref[...] += _rowsum8(dh * xh)
        dx_ref[...] = dxo + _rms_bwd(xh, rx, g2, dh)

    row = pl.BlockSpec((tm, d), lambda i: (nt - 1 - i, 0))
    wide = pl.BlockSpec((tm, 2 * D_FF), lambda i: (nt - 1 - i, 0))
    vec = _full((1, d))
    acc = _full((8, d))
    dcw_shape, dcb_shape = (N_SHARD, 3, 8, FF_CHUNK), (N_SHARD, 8, FF_CHUNK)
    return pl.pallas_call(
        body, grid=(nt,),
        in_specs=[row, row, row, wide, wide, vec, vec, ANY, ANY, _full(conv_w.shape)],
        out_specs=[row, wide, row, acc, acc, _full(dcw_shape), _full(dcb_shape)],
        out_shape=[jax.ShapeDtypeStruct((t, d), F32), jax.ShapeDtypeStruct((t, 2 * D_FF), BF16),
                   jax.ShapeDtypeStruct((t, d), BF16),
                   jax.ShapeDtypeStruct((8, d), F32), jax.ShapeDtypeStruct((8, d), F32),
                   jax.ShapeDtypeStruct(dcw_shape, F32), jax.ShapeDtypeStruct(dcb_shape, F32)],
        scratch_shapes=[pltpu.VMEM(w_up.shape, BF16), pltpu.VMEM(w_down.shape, BF16),
                        pltpu.VMEM((N_SHARD, h8, FF_CHUNK), F32), pltpu.SemaphoreType.DMA((2,))],
        compiler_params=_params(1, VMEM_LIMIT_MLP_BWD), name="mlp_bwd")(
            dxo, f, x, u, sp, g_pre, g_post, w_up, w_down, conv_w)


def grad_matmul(a, b, bm, bn, name, tk=2048, interleaved=False, after=None, cols=None):
    t = a.shape[0]
    m0, m = (0, a.shape[1]) if cols is None else cols
    n = b.shape[1]
    tk = min(tk, t)
    nk = t // tk
    place = (lambda j: (j % 2) * 2 + j // 2) if interleaved else (lambda j: j)
    extra = [] if after is None else [after]
    first = m0 // bm

    def body(a_ref, b_ref, *rest):
        o_ref, ob_ref = rest[len(extra):]
        kk = pl.program_id(2)

        @pl.when(kk == 0)
        def _():
            o_ref[...] = jnp.zeros_like(o_ref)

        o_ref[...] += _dot_tn(a_ref[...], b_ref[...])

        @pl.when(kk == nk - 1)
        def _():
            ob_ref[...] = o_ref[...].astype(BF16)

    ospec = pl.BlockSpec((None, bm, bn), lambda j, i, kk: (place(j), i, 0))
    return pl.pallas_call(
        body, grid=(n // bn, m // bm, nk),
        in_specs=[pl.BlockSpec((tk, bm), lambda j, i, kk: (kk, first + i)),
                  pl.BlockSpec((tk, bn), lambda j, i, kk: (kk, j))]
        + [ANY] * len(extra),
        out_specs=[ospec, ospec],
        out_shape=[jax.ShapeDtypeStruct((n // bn, m, bn), F32), jax.ShapeDtypeStruct((n // bn, m, bn), BF16)],
        compiler_params=_params(3), name=name)(a, b, *extra)


def _decay_tables():
    log_gamma = jnp.log(1.0 - 2.0 ** (-5.0 - jnp.arange(RET_HEADS, dtype=F32)))
    i = jnp.arange(RET_CHUNK, dtype=F32)
    rel = i[:, None] - i[None, :]
    intra = jnp.where(rel >= 0, jnp.exp(jnp.maximum(rel, 0.0) * log_gamma[:, None, None]), 0.0)
    cross = jnp.exp((i + 1.0) * log_gamma[:, None])[:, :, None]
    inner = jnp.exp((RET_CHUNK - 1.0 - i) * log_gamma[:, None])[:, :, None]
    chunk = [float(np.exp(np.float32(RET_CHUNK) * np.log(np.float32(1.0 - 2.0 ** (-5.0 - h))).astype(np.float32)))
             for h in range(RET_HEADS)]
    return intra, cross, inner, chunk


def ret_proj(x, g_pre, w_in, cos, sin, tm=512):
    t, d = x.shape
    nt = t // tm
    per = RET_IN_SHARD // RET_QK

    def body(x_ref, g_ref, win_hbm, c_ref, s_ref, pj_ref, h_ref, win_v, sem):
        @pl.when(pl.program_id(0) == 0)
        def _():
            cp = pltpu.make_async_copy(win_hbm, win_v, sem)
            cp.start()
            cp.wait()

        h, _, _ = _rms_fwd(x_ref[...], g_ref[...])
        hb = h.astype(BF16)
        h_ref[...] = hb
        c = c_ref[...]
        s = s_ref[...]
        for j in range(N_SHARD):
            pjj = _dot(hb, win_v[j])
            for bb in range(per):
                b = per * j + bb
                blk = pjj[:, bb * RET_QK:(bb + 1) * RET_QK]
                if b < 2 * RET_HEADS:
                    x1, x2 = blk[:, :128], blk[:, 128:]
                    o1 = x1 * c - x2 * s
                    o2 = x2 * c + x1 * s
                    if b < RET_HEADS:
                        o1 = o1 * (RET_QK ** -0.5)
                        o2 = o2 * (RET_QK ** -0.5)
                    pj_ref[:, b * RET_QK:b * RET_QK + 128] = o1.astype(BF16)
                    pj_ref[:, b * RET_QK + 128:(b + 1) * RET_QK] = o2.astype(BF16)
                else:
                    pj_ref[:, b * RET_QK:(b + 1) * RET_QK] = blk.astype(BF16)

    row = pl.BlockSpec((tm, d), lambda i: (i, 0))
    tab = pl.BlockSpec((tm, 128), lambda i: (i, 0))
    return pl.pallas_call(
        body, grid=(nt,),
        in_specs=[row, _full((1, d)), ANY, tab, tab],
        out_specs=[pl.BlockSpec((tm, RET_IN), lambda i: (i, 0)), row],
        out_shape=[jax.ShapeDtypeStruct((t, RET_IN), BF16), jax.ShapeDtypeStruct((t, d), BF16)],
        scratch_shapes=[pltpu.VMEM(w_in.shape, BF16), pltpu.SemaphoreType.DMA],
        compiler_params=_params(1), name="ret_proj")(x, g_pre, w_in, cos, sin)


def ret_core_fwd(pj, intra, cross, inner, chunk_decay):
    t = pj.shape[0]
    nc = t // RET_CHUNK
    c = RET_CHUNK
    qk_all = RET_HEADS * RET_QK
    v_all = RET_HEADS * RET_V

    def body(q_ref, k_ref, v_ref, dm_ref, cr_ref, in_ref, o_ref, sp_ref, state):
        @pl.when(pl.program_id(0) == 0)
        def _():
            state[...] = jnp.zeros_like(state)

        for h in range(RET_HEADS):
            q = q_ref[:, h * RET_QK:(h + 1) * RET_QK]
            k = k_ref[:, h * RET_QK:(h + 1) * RET_QK]
            v = v_ref[:, h * RET_V:(h + 1) * RET_V]
            sb = state[h].astype(BF16)
            sp_ref[h] = sb
            sc = _dot_nt(q, k) * dm_ref[h]
            o_ref[:, h * RET_V:(h + 1) * RET_V] = _dot(sc.astype(BF16), v) + _dot(q, sb) * cr_ref[h]
            kd = (k.astype(F32) * in_ref[h]).astype(BF16)
            state[h] = state[h] * chunk_decay[h] + _dot_tn(kd, v)

    return pl.pallas_call(
        body, grid=(nc,),
        in_specs=[pl.BlockSpec((c, qk_all), lambda n: (n, 0)), pl.BlockSpec((c, qk_all), lambda n: (n, 1)),
                  pl.BlockSpec((c, v_all), lambda n: (n, 1)),
                  _full(intra.shape), _full(cross.shape), _full(inner.shape)],
        out_specs=[pl.BlockSpec((c, v_all), lambda n: (n, 0)),
                   pl.BlockSpec((None, RET_HEADS, RET_QK, RET_V), lambda n: (n, 0, 0, 0))],
        out_shape=[jax.ShapeDtypeStruct((t, v_all), F32),
                   jax.ShapeDtypeStruct((nc, RET_HEADS, RET_QK, RET_V), BF16)],
        scratch_shapes=[pltpu.VMEM((RET_HEADS, RET_QK, RET_V), F32)],
        compiler_params=_params(1), name="ret_core_fwd")(pj, pj, pj, intra, cross, inner)


def _group_norm(o_h):
    mu = jnp.mean(o_h, axis=-1, keepdims=True)
    dev = o_h - mu
    rstd = lax.rsqrt(jnp.mean(dev * dev, axis=-1, keepdims=True) + EPS)
    return dev * rstd, rstd


def ret_out_fwd(o, pj, x, gn_gain, g_post, w_out, tm=512):
    t, d = x.shape
    nt = t // tm
    v_all = RET_HEADS * RET_V

    def body(o_ref, g_ref, x_ref, gn_ref, g1_ref, w_ref, xo_ref, y_ref, out_ref):
        for h in range(RET_HEADS):
            cols = slice(h * RET_V, (h + 1) * RET_V)
            ohat, _ = _group_norm(o_ref[:, cols])
            g = g_ref[:, cols].astype(F32)
            y_ref[:, cols] = (g * jax.nn.sigmoid(g) * (ohat * gn_ref[:, cols])).astype(BF16)
        out = _dot(y_ref[...], w_ref[...])
        out_ref[...] = out
        m, _, _ = _rms_fwd(out, g1_ref[...])
        xo_ref[...] = x_ref[...] + m

    row = pl.BlockSpec((tm, d), lambda i: (i, 0))
    wide = pl.BlockSpec((tm, v_all), lambda i: (i, 0))
    return pl.pallas_call(
        body, grid=(nt,),
        in_specs=[wide, pl.BlockSpec((tm, v_all), lambda i: (i, 2)), row, _full((1, v_all)), _full((1, d)),
                  _full(w_out.shape)],
        out_specs=[row, wide, row],
        out_shape=[jax.ShapeDtypeStruct((t, d), F32), jax.ShapeDtypeStruct((t, v_all), BF16),
                   jax.ShapeDtypeStruct((t, d), F32)],
        compiler_params=_params(1), name="ret_out_fwd")(o, pj, x, gn_gain, g_post, w_out)


def ret_out_bwd(dxo, out, o, pj, gn_gain, g_post, w_out, tm=256):
    t, d = out.shape
    nt = t // tm
    v_all = RET_HEADS * RET_V

    def body(dxo_ref, out_ref, o_ref, g_ref, gn_ref, g1_ref, w_ref,
             dout_ref, dgate_ref, do_ref, dg1_ref, dgn_ref):
        @pl.when(pl.program_id(0) == 0)
        def _():
            dg1_ref[...] = jnp.zeros_like(dg1_ref)
            dgn_ref[...] = jnp.zeros_like(dgn_ref)

        g1 = g1_ref[...]
        dxo = dxo_ref[...]
        _, oh_, r_ = _rms_fwd(out_ref[...], g1)
        dg1_ref[...] += _colsum(dxo * oh_)
        doutb = _rms_bwd(oh_, r_, g1, dxo).astype(BF16)
        dout_ref[...] = doutb
        dy = _dot_nt(doutb, w_ref[...])
        for h in range(RET_HEADS):
            cols = slice(h * RET_V, (h + 1) * RET_V)
            gn = gn_ref[:, cols]
            ohat, rstd = _group_norm(o_ref[:, cols])
            g = g_ref[:, cols].astype(F32)
            sg = jax.nn.sigmoid(g)
            dyh = dy[:, cols]
            dgate_ref[:, cols] = (dyh * (ohat * gn) * (sg * (1.0 + g * (1.0 - sg)))).astype(BF16)
            don = dyh * (g * sg)
            dgn_ref[:, cols] += _colsum(don * ohat)
            dohat = don * gn
            do_ref[:, cols] = (rstd * (dohat - jnp.mean(dohat, axis=-1, keepdims=True)
                                       - ohat * jnp.mean(dohat * ohat, axis=-1, keepdims=True))).astype(BF16)

    row = pl.BlockSpec((tm, d), lambda i: (i, 0))
    wide = pl.BlockSpec((tm, v_all), lambda i: (i, 0))
    gate = pl.BlockSpec((tm, v_all), lambda i: (i, 2))
    return pl.pallas_call(
        body, grid=(nt,),
        in_specs=[row, row, wide, gate, _full((1, v_all)), _full((1, d)), _full(w_out.shape)],
        out_specs=[row, gate, wide, _full((1, d)), _full((1, v_all))],
        out_shape=[jax.ShapeDtypeStruct((t, d), BF16), jax.ShapeDtypeStruct((t, RET_IN), BF16),
                   jax.ShapeDtypeStruct((t, v_all), BF16), jax.ShapeDtypeStruct((1, d), F32),
                   jax.ShapeDtypeStruct((1, v_all), F32)],
        compiler_params=_params(1), name="ret_out_bwd")(dxo, out, o, pj, gn_gain, g_post, w_out)


def ret_core_bwd(pj, do, sprev, cos, sin, dpj, intra, cross, inner, chunk_decay):
    t = pj.shape[0]
    nc = t // RET_CHUNK
    c = RET_CHUNK
    qk_all = RET_HEADS * RET_QK
    v_all = RET_HEADS * RET_V
    scale = RET_QK ** -0.5

    def body(q_ref, k_ref, v_ref, do_ref, sp_ref, c_ref, s_ref, dm_ref, cr_ref, in_ref, dpj_in, dpj_ref, dstate):
        @pl.when(pl.program_id(0) == 0)
        def _():
            dstate[...] = jnp.zeros_like(dstate)

        cs = c_ref[...]
        sn = s_ref[...]
        for h in range(RET_HEADS):
            q = q_ref[:, h * RET_QK:(h + 1) * RET_QK]
            k = k_ref[:, h * RET_QK:(h + 1) * RET_QK]
            v = v_ref[:, h * RET_V:(h + 1) * RET_V]
            doh = do_ref[:, h * RET_V:(h + 1) * RET_V]
            dm = dm_ref[h]
            ab = (_dot_nt(q, k) * dm).astype(BF16)
            dab = (_dot_nt(doh, v) * dm).astype(BF16)
            dsb = dstate[h].astype(BF16)
            kd = (k.astype(F32) * in_ref[h]).astype(BF16)
            dv = _dot_tn(ab, doh) + _dot(kd, dsb)
            dq = _dot(dab, k) + cr_ref[h] * _dot_nt(doh, sp_ref[h])
            dk = _dot_tn(dab, q) + in_ref[h] * _dot_nt(v, dsb)
            qd = (q.astype(F32) * cr_ref[h]).astype(BF16)
            dstate[h] = dstate[h] * chunk_decay[h] + _dot_tn(qd, doh)
            for base, dd, sc in ((h * RET_QK, dq, scale), (qk_all + h * RET_QK, dk, 1.0)):
                d1, d2 = dd[:, :128], dd[:, 128:]
                dpj_ref[:, base:base + 128] = ((d1 * cs + d2 * sn) * sc).astype(BF16)
                dpj_ref[:, base + 128:base + RET_QK] = ((d2 * cs - d1 * sn) * sc).astype(BF16)
            dpj_ref[:, 2 * qk_all + h * RET_V:2 * qk_all + (h + 1) * RET_V] = dv.astype(BF16)

    rev = lambda n: nc - 1 - n
    tab = pl.BlockSpec((c, 128), lambda n: (rev(n), 0))
    return pl.pallas_call(
        body, grid=(nc,),
        in_specs=[pl.BlockSpec((c, qk_all), lambda n: (rev(n), 0)), pl.BlockSpec((c, qk_all), lambda n: (rev(n), 1)),
                  pl.BlockSpec((c, v_all), lambda n: (rev(n), 1)), pl.BlockSpec((c, v_all), lambda n: (rev(n), 0)),
                  pl.BlockSpec((None, RET_HEADS, RET_QK, RET_V), lambda n: (rev(n), 0, 0, 0)),
                  tab, tab, _full(intra.shape), _full(cross.shape), _full(inner.shape), ANY],
        out_specs=pl.BlockSpec((c, 2 * qk_all + v_all), lambda n: (rev(n), 0)),
        out_shape=jax.ShapeDtypeStruct((t, RET_IN), BF16),
        scratch_shapes=[pltpu.VMEM((RET_HEADS, RET_QK, RET_V), F32)],
        input_output_aliases={10: 0},
        compiler_params=_params(1), name="ret_core_bwd")(pj, pj, pj, do, sprev, cos, sin, intra, cross, inner, dpj)


def ret_in_bwd(dpj, dres, x, g_pre, w_in, tm=512):
    t, d = x.shape
    nt = t // tm

    def body(dpj_ref, dres_ref, x_ref, g_ref, win_hbm, dx_ref, dg_ref, win_v, sem):
        @pl.when(pl.program_id(0) == 0)
        def _():
            cp = pltpu.make_async_copy(win_hbm, win_v, sem)
            cp.start()
            dg_ref[...] = jnp.zeros_like(dg_ref)
            cp.wait()

        g = g_ref[...]
        dh = jnp.zeros((tm, d), F32)
        for j in range(N_SHARD):
            dh = dh + _dot_nt(dpj_ref[:, j * RET_IN_SHARD:(j + 1) * RET_IN_SHARD], win_v[j])
        _, xh, rx = _rms_fwd(x_ref[...], g)
        dg_ref[...] += _colsum(dh * xh)
        dx_ref[...] = dres_ref[...] + _rms_bwd(xh, rx, g, dh)

    row = pl.BlockSpec((tm, d), lambda i: (i, 0))
    return pl.pallas_call(
        body, grid=(nt,),
        in_specs=[pl.BlockSpec((tm, RET_IN), lambda i: (i, 0)), row, row, _full((1, d)), ANY],
        out_specs=[row, _full((1, d))],
        out_shape=[jax.ShapeDtypeStruct((t, d), F32), jax.ShapeDtypeStruct((1, d), F32)],
        scratch_shapes=[pltpu.VMEM(w_in.shape, BF16), pltpu.SemaphoreType.DMA],
        compiler_params=_params(1), name="ret_in_bwd")(dpj, dres, x, g_pre, w_in)


_CHIP_FLIPS = ((1, 0), (0, 1), (1, 1))


def _flip(v, b):
    return 1 - v if b else v


_HBM = pl.BlockSpec(memory_space=pltpu.HBM)
_SEM = pl.BlockSpec(memory_space=pltpu.SEMAPHORE)
_EFFECT = pltpu.SideEffectType.DATAFLOW_SIDE_EFFECTING


def _chip_copies(mode, srcs, lands, send_sems, recv_sems):
    x, y, c = lax.axis_index("x"), lax.axis_index("y"), lax.axis_index("c")
    copies = []
    for t in range(len(lands)):
        if mode == "swap":
            copies.append(pltpu.make_async_remote_copy(
                src_ref=srcs[t], dst_ref=lands[t], send_sem=send_sems.at[t], recv_sem=recv_sems.at[t],
                device_id=(x, y, 1 - c), device_id_type=MESH))
            continue
        if mode == "everyone":
            for m in range(1, 8):
                bx, by, bc = (m >> 2) & 1, (m >> 1) & 1, m & 1
                copies.append(pltpu.make_async_remote_copy(
                    src_ref=srcs[t], dst_ref=lands[t].at[4 * x + 2 * y + c], send_sem=send_sems.at[7 * t + m - 1],
                    recv_sem=recv_sems.at[7 * t + m - 1], device_id=(_flip(x, bx), _flip(y, by), _flip(c, bc)),
                    device_id_type=MESH))
            continue
        for k, (bx, by) in enumerate(_CHIP_FLIPS):
            px, py = _flip(x, bx), _flip(y, by)
            target = (px, py, c)
            if mode == "gather":
                src, dst = srcs[t], lands[t].at[2 * x + y]
            elif mode == "gather_half":
                half = pl.ds(c * (srcs[t].shape[0] // 2), srcs[t].shape[0] // 2)
                src, dst = srcs[t].at[half], lands[t].at[2 * x + y, half]
            elif mode == "forward_half":
                half = pl.ds(c * (lands[t].shape[1] // 2), lands[t].shape[1] // 2)
                src = dst = lands[t].at[2 * px + py, half]
                target = (x, y, 1 - c)
            else:
                src, dst = srcs[t].at[2 * px + py], lands[t].at[k]
            copies.append(pltpu.make_async_remote_copy(
                src_ref=src, dst_ref=dst, send_sem=send_sems.at[3 * t + k], recv_sem=recv_sems.at[3 * t + k],
                device_id=target, device_id_type=MESH))
    return copies


def exchange_start(mode, srcs, lands, name, after=None):
    n, ns = len(lands), len(srcs)
    extra = [] if after is None else [after]

    def body(*refs):
        ins, lnd = refs[:ns], refs[ns:ns + n]
        send_sems, recv_sems = refs[ns + n + len(extra)], refs[ns + n + len(extra) + 1]
        token = refs[-1]
        for cp in _chip_copies(mode, ins, lnd, send_sems, recv_sems):
            cp.start()
        token[...] = jnp.zeros(token.shape, token.dtype)

    hbm = lambda a: pltpu.with_memory_space_constraint(a, pltpu.HBM)
    passed = list(srcs) + list(lands)
    n_sem = {"swap": 1, "everyone": 7}.get(mode, 3) * n
    return pl.pallas_call(
        body, name=name,
        out_shape=(pltpu.SemaphoreType.DMA((n_sem,)), pltpu.SemaphoreType.DMA((n_sem,)),
                   *[pltpu.HBM(a.shape, a.dtype) for a in passed], jax.ShapeDtypeStruct((8, 128), F32)),
        in_specs=[_HBM] * (ns + n) + [ANY] * len(extra),
        out_specs=(_SEM, _SEM, *[_HBM] * (ns + n), pl.BlockSpec(memory_space=pltpu.VMEM)),
        input_output_aliases={i: 2 + i for i in range(ns + n)},
        compiler_params=pltpu.CompilerParams(has_side_effects=_EFFECT))(*[hbm(a) for a in passed], *extra)


def exchange_wait(mode, started, after, name):
    send_sems, recv_sems = started[0], started[1]
    passed = list(started[2:-1])
    n = len(passed) if mode == "forward_half" else len(passed) // 2
    ns = len(passed) - n

    def body(*refs):
        ins, lnd = refs[:ns], refs[ns:ns + n]
        for cp in _chip_copies(mode, ins, lnd, refs[ns + n], refs[ns + n + 1]):
            cp.wait_send()
            cp.wait_recv()

    outs = pl.pallas_call(
        body, name=name, out_shape=tuple(pltpu.HBM(a.shape, a.dtype) for a in passed),
        in_specs=[_HBM] * (ns + n) + [_SEM, _SEM, ANY], out_specs=tuple([_HBM] * (ns + n)),
        input_output_aliases={i: i for i in range(ns + n)},
        compiler_params=pltpu.CompilerParams(has_side_effects=_EFFECT))(*passed, send_sems, recv_sems, after)
    return list(outs[:ns]), list(outs[ns:])


def plane_sum(slot, full, recv, name, bm=256):
    _, m, n = full.shape
    bm = _row_block(m, bm)

    def body(slot_ref, o_ref, r_ref, s_ref):
        s_ref[...] = ((o_ref[...] + r_ref[0].astype(F32)) + r_ref[1].astype(F32)) + r_ref[2].astype(F32)

    return pl.pallas_call(
        body,
        grid_spec=pltpu.PrefetchScalarGridSpec(
            num_scalar_prefetch=1, grid=(m // bm,),
            in_specs=[pl.BlockSpec((None, bm, n), lambda i, s: (s[0], i, 0)),
                      pl.BlockSpec((3, bm, n), lambda i, s: (0, i, 0))],
            out_specs=pl.BlockSpec((bm, n), lambda i, s: (i, 0))),
        out_shape=jax.ShapeDtypeStruct((m, n), F32), compiler_params=_params(1), name=name)(slot, full, recv)


def sum_slots(parts, name, bm=312):
    _, r, n = parts.shape
    bm = bm if r % bm == 0 else r

    def body(p_ref, s_ref):
        acc = p_ref[0]
        for k in range(1, 8):
            acc = acc + p_ref[k]
        s_ref[...] = acc

    return pl.pallas_call(
        body, grid=(r // bm,), in_specs=[pl.BlockSpec((8, bm, n), lambda i: (0, i, 0))],
        out_specs=pl.BlockSpec((bm, n), lambda i: (i, 0)), out_shape=jax.ShapeDtypeStruct((r, n), F32),
        compiler_params=_params(1), name=name)(parts)


def _adamw_math(w, g, m, v):
    m = ADAM_B1 * m + (1.0 - ADAM_B1) * g
    v = ADAM_B2 * v + (1.0 - ADAM_B2) * (g * g)
    m_hat = m / (1.0 - ADAM_B1 ** ADAM_STEP)
    v_hat = v / (1.0 - ADAM_B2 ** ADAM_STEP)
    delta = -ADAM_LR * (m_hat / (jnp.sqrt(v_hat) + ADAM_EPS) + ADAM_WD * w)
    return delta, m, v


def adamw(w, m, v, grads, layer, prev, name, bm=256, row0=0):
    _, _, n = w.shape
    mm = grads[0].shape[0]
    bm = _row_block(mm, bm)
    first = row0 // bm
    ng = len(grads)

    def body(*refs):
        w_ref, m_ref, v_ref = refs[:3]
        g_refs = refs[3:3 + ng]
        g_out, d_out, m_out, v_out = refs[-4:]
        g = g_refs[0][...]
        for gr in g_refs[1:]:
            g = g + gr[...]
        delta, mn, vn = _adamw_math(w_ref[...], g, m_ref[...], v_ref[...])
        g_out[...] = g
        d_out[...] = delta
        m_out[...] = mn
        v_out[...] = vn

    slab = pl.BlockSpec((None, bm, n), lambda i: (layer, first + i, 0))
    flat = pl.BlockSpec((bm, n), lambda i: (i, 0))
    in_specs = [slab] * 3 + [flat] * ng
    args = [w, m, v, *grads]
    aliases = {}
    if prev is not None:
        in_specs += [ANY] * 4
        aliases = {3 + ng + q: q for q in range(4)}
        args += list(prev)
    return pl.pallas_call(
        body, grid=(mm // bm,), in_specs=in_specs, out_specs=[slab] * 4,
        out_shape=[jax.ShapeDtypeStruct(w.shape, F32)] * 4, input_output_aliases=aliases,
        compiler_params=_params(1), name=name)(*args)


def _pack_rows(parts, rows):
    flat = jnp.concatenate([p.reshape(-1) for p in parts])
    return jnp.pad(flat, (0, rows * 128 - flat.shape[0])).reshape(rows, 128)


def _as_shards(a, rows):
    return a.reshape(N_SHARD, rows, a.shape[-1])


def _local_step(x, pos_col, target, gains, pool_w, pool_scale, gn_gain, conv_w, conv_b, weights, send_grads):
    def gain(l, n, token=None):
        g = gains[l, n].reshape(1, D_MODEL)
        return g if token is None else g + token[0:1, 0:1]

    inv_freq = (ROPE_BASE ** (-jnp.arange(0, RET_QK, 2, dtype=F32) / RET_QK)).reshape(1, RET_QK // 2)
    intra, cross, inner, chunk_decay = _decay_tables()
    dn_rows = D_FF // N_SHARD

    x1 = pool_fwd(x, gain(0, 0), gain(0, 1), pool_w, pool_scale)
    cos, sin = rope_tables(pos_col, inv_freq, x1)
    w_up0, w_dn0 = weights("mlp0", cos)
    w_dn0 = w_dn0.reshape(D_FF, D_MODEL)
    x2, u0, s0, a0, f0, h0 = mlp_fwd(x1, gain(0, 2), gain(0, 3), w_up0, w_dn0, conv_w[0], conv_b[0])
    w_in, w_out = weights("ret", x2)
    w_out = w_out.reshape(RET_HEADS * RET_V, D_MODEL)
    pj, hr = ret_proj(x2, gain(1, 0), w_in, cos, sin)
    o, sprev = ret_core_fwd(pj, intra, cross, inner, chunk_decay)
    x3, yb, out = ret_out_fwd(o, pj, x2, gn_gain, gain(1, 1), w_out)
    w_up1, w_dn1 = weights("mlp1", x3)
    w_dn1 = w_dn1.reshape(D_FF, D_MODEL)
    dx4, u1, s1, a1, f1, h1, loss = mlp_fwd(x3, gain(1, 2), gain(1, 3), w_up1, w_dn1, conv_w[1], conv_b[1], target)

    dx3, du1, df1, dg12, dg13, dcw1, dcb1 = mlp_bwd(
        dx4, f1, x3, u1, s1, gain(1, 2), gain(1, 3), w_up1, w_dn1, conv_w[1])
    dwup1 = grad_matmul(h1, du1, D_MODEL, FF_CHUNK, "grad_w_up_1", interleaved=True)
    dwdn1 = grad_matmul(a1, df1, FF_CHUNK, D_MODEL, "grad_w_down_1")
    tok = send_grads("mlp1", [dwup1, [_as_shards(g, dn_rows) for g in dwdn1]])
    dout, dpj, do, dg11, dgn = ret_out_bwd(dx3, out, o, pj, gn_gain, gain(1, 1, tok), w_out)
    dwout = grad_matmul(yb, dout, 1024, D_MODEL, "grad_w_out")
    dpj = ret_core_bwd(pj, do, sprev, cos, sin, dpj, intra, cross, inner, chunk_decay)
    dwin = grad_matmul(hr, dpj, D_MODEL, RET_IN_SHARD, "grad_w_in")
    tok = send_grads("ret", [dwin, [_as_shards(g, RET_V) for g in dwout]])
    dx2, dg10 = ret_in_bwd(dpj, dx3, x2, gain(1, 0, tok), w_in)
    dx1, du0, df0, dg02, dg03, dcw0, dcb0 = mlp_bwd(
        dx2, f0, x1, u0, s0, gain(0, 2), gain(0, 3), w_up0, w_dn0, conv_w[0])
    dwdn0 = grad_matmul(a0, df0, FF_CHUNK, D_MODEL, "grad_w_down_0")
    tok = send_grads("down0", [[_as_shards(g, dn_rows) for g in dwdn0]])
    half = D_MODEL // 2
    for part, first in (("a", 0), ("b", half)):
        dwup0 = grad_matmul(h0, du0, half, FF_CHUNK, "grad_w_up_0" + part, tk=4096, interleaved=True, after=tok,
                            cols=(first, half))
        tok = send_grads("up0" + part, [dwup0])
    dx0, dg00, dg01, dpscale, dpw = pool_bwd(dx1, x, gain(0, 0, tok), gain(0, 1), pool_w, pool_scale)

    rows = lambda g: jnp.sum(g, axis=0, keepdims=True)
    dgains = jnp.concatenate([dg00, dg01, rows(dg02), rows(dg03), dg10, dg11, rows(dg12), rows(dg13)],
                             axis=0).reshape(2, 4, D_MODEL)
    small = {"gains": dgains, "pool_scale": dpscale, "gn": dgn,
             "conv_w": jnp.sum(jnp.stack([dcw0, dcw1]), axis=3),
             "conv_b": jnp.sum(jnp.stack([dcb0, dcb1]), axis=2, keepdims=True), "pool_w": dpw}
    return loss, dx0, small


def kernel(x, positions, norm_gain, pool_w, pool_scale, ret_w_in, ret_gn_gain, ret_w_out, mlp_w_up, mlp_conv_w, mlp_conv_b, mlp_w_down, loss_target, m_norm_gain, m_pool_w, m_pool_scale, m_ret_w_in, m_ret_gn_gain, m_ret_w_out, m_mlp_w_up, m_mlp_conv_w, m_mlp_conv_b, m_mlp_w_down, v_norm_gain, v_pool_w, v_pool_scale, v_ret_w_in, v_ret_gn_gain, v_ret_w_out, v_mlp_w_up, v_mlp_conv_w, v_mlp_conv_b, v_mlp_w_down):
    t = x.shape[1]
    me = 2 * lax.axis_index("x") + lax.axis_index("y")
    me_slot = jnp.reshape(me, (1,)).astype(jnp.int32)

    small_parts = [norm_gain, ret_gn_gain, mlp_conv_w, pool_w]
    small_sizes = [p.size for p in small_parts]
    small_rows = -(-sum(small_sizes) // (128 * 8)) * 8
    gathers = {}

    def start_gather(group, srcs, after):
        lands = [lax.dynamic_update_index_in_dim(lax.empty((N_SHARD,) + s.shape, s.dtype), s, me, 0) for s in srcs]
        mode = "gather_half" if group == "mlp0" else "gather"
        gathers[group] = (mode, exchange_start(mode, srcs, lands, "gather_start_" + group, after=after))
        return gathers[group][1][-1]

    token = start_gather("small", [_pack_rows(small_parts, small_rows)], None)
    token = start_gather("mlp0", [mlp_w_up[0].astype(BF16), mlp_w_down[0].astype(BF16)], token)

    def weights(group, after):
        if group == "mlp0":
            tok = start_gather("ret", [ret_w_in[0].astype(BF16), ret_w_out[0].astype(BF16)], after)
            after = start_gather("mlp1", [mlp_w_up[1].astype(BF16), mlp_w_down[1].astype(BF16)], tok)
        mode, started = gathers[group]
        _, lands = exchange_wait(mode, started, after, "gather_wait_" + group)
        if mode == "gather_half":
            forward = exchange_start("forward_half", [], lands, "forward_start_" + group)
            _, lands = exchange_wait("forward_half", forward, forward[-1], "forward_wait_" + group)
        return lands

    sent, early = {}, {}

    def reduced(group, after, names):
        started, own = sent[group]
        _, recv = exchange_wait("scatter", started, after, "scatter_wait_" + group)
        return [plane_sum(me_slot, f, r, "plane_sum_" + nm)
                for f, r, nm in zip(own, recv, names)]

    def swap_start(planes, name):
        return exchange_start("swap", planes, [lax.empty(p.shape, p.dtype) for p in planes], name)

    def send_grads(group, pairs):
        lands = [lax.empty((3,) + b.shape[1:], BF16) for _, b in pairs]
        sent[group] = (exchange_start("scatter", [b for _, b in pairs], lands, "scatter_start_" + group),
                       [f for f, _ in pairs])
        token = sent[group][0][-1]
        if group == "down0":
            marker = pairs[0][1]
            early["planes"] = (reduced("mlp1", marker, ["w_up_1", "w_down_1"])
                               + reduced("ret", marker, ["w_in", "w_out"]))
            early["swap"] = swap_start(early["planes"], "swap_start_a")
            token = token + early["swap"][-1]
        return token

    (smallg,) = weights("small", token)
    smallg = smallg.reshape(N_SHARD, -1)
    offs = np.cumsum([0] + small_sizes)
    piece = lambda i, shape: smallg[:, offs[i]:offs[i + 1]].reshape((N_SHARD,) + shape)
    gains = piece(0, (2, 4, 256)).transpose(1, 2, 0, 3).reshape(2, 4, D_MODEL)
    gn_full = piece(1, (512,)).reshape(1, RET_HEADS * RET_V)
    cw_full = piece(2, (2, 3, FF_CHUNK)).transpose(1, 0, 2, 3)
    pw_full = piece(3, (4, 64, 256)).transpose(1, 0, 2, 3).reshape(4, 256, 256).astype(BF16)
    cb_full = mlp_conv_b.reshape(2, N_SHARD, 1, FF_CHUNK)

    loss, dx0, small = _local_step(
        x[0], positions.reshape(t, 1).astype(F32), loss_target[0], gains, pw_full, pool_scale, gn_full,
        cw_full, cb_full, weights, send_grads)

    def small_adamw(w, m, v, grads, name):
        w3 = w.reshape(1, -1, w.shape[-1])
        out = adamw(w3, m.reshape(w3.shape), v.reshape(w3.shape), [g.reshape(w3.shape[1:]) for g in grads], 0, None, name)
        return [o.reshape(w.shape) for o in out]

    pw_f = small["pool_w"].reshape(4, N_SHARD, 64, 256).transpose(1, 0, 2, 3).reshape(N_SHARD, 256, 256)
    small_order = ["gains", "pool_scale", "gn", "conv_w", "conv_b"]
    gsmall_sizes = [small[k].size for k in small_order]
    gsmall_rows = -(-sum(gsmall_sizes) // (128 * 8)) * 8
    gpack = _pack_rows([small[k] for k in small_order], gsmall_rows)
    mine = 2 * me + lax.axis_index("c")
    small_started = exchange_start(
        "everyone", [gpack], [lax.dynamic_update_index_in_dim(lax.empty((8,) + gpack.shape, F32), gpack, mine, 0)],
        "small_start")
    send_grads("pool_w", [(pw_f, pw_f.astype(BF16))])

    res = {}
    planes_a, others_a = exchange_wait("swap", early["swap"], small_started[-1], "swap_wait_a")
    res["ret_w_in"] = adamw(ret_w_in, m_ret_w_in, v_ret_w_in, (planes_a[2], others_a[2]), 0, None, "adamw_w_in")
    res["ret_w_out"] = adamw(ret_w_out, m_ret_w_out, v_ret_w_out, (planes_a[3], others_a[3]), 0, None, "adamw_w_out")
    up1 = adamw(mlp_w_up, m_mlp_w_up, v_mlp_w_up, (planes_a[0], others_a[0]), 1, None, "adamw_w_up_1")
    dn1 = adamw(mlp_w_down, m_mlp_w_down, v_mlp_w_down, (planes_a[1], others_a[1]), 1, None, "adamw_w_down_1")

    planes_b = (reduced("up0a", dn1[0], ["w_up_0a"]) + reduced("up0b", dn1[0], ["w_up_0b"])
                + reduced("down0", dn1[0], ["w_down_0"]) + reduced("pool_w", dn1[0], ["pool_w"]))
    swap_b = swap_start(planes_b, "swap_start_b")

    _, (small_recv,) = exchange_wait("everyone", small_started, swap_b[-1], "small_wait")
    gsmall = sum_slots(small_recv, "sum_small").reshape(-1)
    goffs = np.cumsum([0] + gsmall_sizes)
    gpiece = lambda i: gsmall[goffs[i]:goffs[i + 1]].reshape(small[small_order[i]].shape)
    g_gains = lax.dynamic_slice_in_dim(gpiece(0), me * 256, 256, axis=2)
    g_gn = lax.dynamic_slice_in_dim(gpiece(2), me * RET_V, RET_V, axis=1)
    g_cw = lax.dynamic_index_in_dim(gpiece(3), me, 1, keepdims=False)
    res["norm_gain"] = small_adamw(norm_gain, m_norm_gain, v_norm_gain, [g_gains], "adamw_norm_gain")
    res["pool_scale"] = small_adamw(pool_scale, m_pool_scale, v_pool_scale, [gpiece(1)], "adamw_pool_scale")
    res["ret_gn_gain"] = small_adamw(ret_gn_gain, m_ret_gn_gain, v_ret_gn_gain, [g_gn], "adamw_gn_gain")
    res["mlp_conv_w"] = small_adamw(mlp_conv_w, m_mlp_conv_w, v_mlp_conv_w, [g_cw], "adamw_conv_w")
    res["mlp_conv_b"] = small_adamw(mlp_conv_b, m_mlp_conv_b, v_mlp_conv_b, [gpiece(4)], "adamw_conv_b")

    planes_b, others_b = exchange_wait("swap", swap_b, res["mlp_conv_b"][0], "swap_wait_b")
    up0a = adamw(mlp_w_up, m_mlp_w_up, v_mlp_w_up, (planes_b[0], others_b[0]), 0, up1, "adamw_w_up_0a")
    res["mlp_w_up"] = adamw(mlp_w_up, m_mlp_w_up, v_mlp_w_up, (planes_b[1], others_b[1]), 0, up0a, "adamw_w_up_0b",
                            row0=D_MODEL // 2)
    res["mlp_w_down"] = adamw(mlp_w_down, m_mlp_w_down, v_mlp_w_down, (planes_b[2], others_b[2]), 0, dn1,
                              "adamw_w_down_0")
    res["pool_w"] = small_adamw(pool_w, m_pool_w, v_pool_w, (planes_b[3], others_b[3]), "adamw_pool_w")

    order = ["norm_gain", "pool_w", "pool_scale", "ret_w_in", "ret_gn_gain", "ret_w_out", "mlp_w_up", "mlp_conv_w",
             "mlp_conv_b", "mlp_w_down"]
    total_loss = lax.psum(loss[0, 0], ("x", "y", "c"))
    outs = [total_loss, dx0.reshape(x.shape)]
    for q in range(4):
        outs += [res[k][q] for k in order]
    return tuple(outs)
```

```python
import numpy as np
import jax
import jax.numpy as jnp
from jax import lax
from jax.experimental import pallas as pl
from jax.experimental.pallas import tpu as pltpu

F32 = jnp.float32
BF16 = jnp.bfloat16

D_MODEL = 1024
D_FF = 2816
FF_CHUNK = 1408
N_SHARD = 4
POOL_WINDOWS = (2, 4, 8, 16)
POOL_DIM = 256
POOL_HALO = 16
RET_HEADS = 4
RET_QK = 256
RET_V = 512
RET_CHUNK = 256
RET_IN = 6144
RET_IN_SHARD = 1536
ROPE_BASE = 10000.0
EPS = 1e-6
CONV_HALO = 8

ADAM_LR, ADAM_B1, ADAM_B2, ADAM_EPS, ADAM_WD, ADAM_STEP = 0.001, 0.9, 0.999, 1e-08, 0.01, 10

VMEM_LIMIT = 56 * 1024 * 1024
VMEM_LIMIT_MLP_BWD = 62 * 1024 * 1024
MESH = pl.DeviceIdType.MESH
ANY = pl.BlockSpec(memory_space=pl.ANY)


def _params(n_grid=1, limit=VMEM_LIMIT):
    return pltpu.CompilerParams(dimension_semantics=("arbitrary",) * n_grid, vmem_limit_bytes=limit)


def _dot(a, b):
    return jnp.dot(a, b, preferred_element_type=F32)


def _dot_nt(a, b):
    return lax.dot_general(a, b, (((1,), (1,)), ((), ())), preferred_element_type=F32)


def _dot_tn(a, b):
    return lax.dot_general(a, b, (((0,), (0,)), ((), ())), preferred_element_type=F32)


def _rms_fwd(x, gain):
    r = lax.rsqrt(jnp.mean(x * x, axis=-1, keepdims=True) + EPS)
    xh = x * r
    return xh * gain, xh, r


def _rms_bwd(xh, r, gain, dy):
    dxh = dy * gain
    return r * (dxh - xh * jnp.mean(dxh * xh, axis=-1, keepdims=True))


def _colsum(v):
    return jnp.sum(v, axis=0, keepdims=True)


def _full(shape):
    nd = len(shape)
    return pl.BlockSpec(shape, lambda *_: (0,) * nd)


def rope_tables(pos_col, inv_freq, after):
    t = pos_col.shape[0]
    tm = min(t, 1024)

    def body(p_ref, f_ref, after_ref, c_ref, s_ref):
        ang = p_ref[...] * f_ref[...]
        c_ref[...] = jnp.cos(ang)
        s_ref[...] = jnp.sin(ang)

    return pl.pallas_call(
        body, grid=(t // tm,),
        in_specs=[pl.BlockSpec((tm, 1), lambda i: (i, 0)), _full((1, 128)), ANY],
        out_specs=[pl.BlockSpec((tm, 128), lambda i: (i, 0))] * 2,
        out_shape=[jax.ShapeDtypeStruct((t, 128), F32)] * 2,
        compiler_params=_params(1), name="rope_tables")(pos_col, inv_freq, after)


def _window_sums(ext, backward):
    n = ext.shape[0]
    cur, sums = ext, []
    for g, win in enumerate(POOL_WINDOWS):
        if g > 0:
            cur = cur[:, POOL_DIM:]
        half = win // 2
        cur = cur + pltpu.roll(cur, n - half if backward else half, axis=0)
        sums.append(cur[:, 0:POOL_DIM])
    return sums


def _pool_diff(h_halo, h, row0, tm):
    t_idx = row0 + lax.broadcasted_iota(jnp.int32, (tm, 1), 0)
    sums = _window_sums(jnp.concatenate([h_halo, h], axis=0), backward=False)
    parts, inv_counts = [], []
    for g, win in enumerate(POOL_WINDOWS):
        inv = 1.0 / jnp.minimum(t_idx + 1, win).astype(F32)
        parts.append(sums[g][POOL_HALO:, :] * inv - h[:, g * POOL_DIM:(g + 1) * POOL_DIM])
        inv_counts.append(inv)
    return parts, inv_counts


def pool_fwd(x, g_pre, g_post, pool_w, pool_scale, tm=512):
    t, d = x.shape
    nt = t // tm

    def body(x_ref, g0_ref, g1_ref, w_ref, sc_ref, o_ref, hext):
        i = pl.program_id(0)

        @pl.when(i == 0)
        def _():
            hext[...] = jnp.zeros((POOL_HALO, d), F32)

        xv = x_ref[...]
        h, _, _ = _rms_fwd(xv, g0_ref[...])
        parts, _ = _pool_diff(hext[...], h, i * tm, tm)
        hext[...] = h[tm - POOL_HALO:tm, :]
        ys = [_dot(parts[g].astype(BF16), w_ref[g]) for g in range(len(POOL_WINDOWS))]
        y = jnp.concatenate(ys, axis=-1) * sc_ref[...]
        m, _, _ = _rms_fwd(y, g1_ref[...])
        o_ref[...] = xv + m

    row = pl.BlockSpec((tm, d), lambda i: (i, 0))
    return pl.pallas_call(
        body, grid=(nt,),
        in_specs=[row, _full((1, d)), _full((1, d)), _full(pool_w.shape), _full((1, d))],
        out_specs=row, out_shape=jax.ShapeDtypeStruct((t, d), F32),
        scratch_shapes=[pltpu.VMEM((POOL_HALO, d), F32)],
        compiler_params=_params(1), name="pool_fwd")(x, g_pre, g_post, pool_w, pool_scale)


def pool_bwd(dx1, x, g_pre, g_post, pool_w, pool_scale, tm=512):
    t, d = x.shape
    nt = t // tm
    ng = len(POOL_WINDOWS)

    def body(dx1_ref, x_ref, xh_ref, g0_ref, g1_ref, w_ref, sc_ref,
             dx_ref, dg0_ref, dg1_ref, dsc_ref, dw_ref, enext):
        i = pl.program_id(0)
        r = nt - 1 - i

        @pl.when(i == 0)
        def _():
            enext[...] = jnp.zeros((POOL_HALO, d), F32)
            dg0_ref[...] = jnp.zeros_like(dg0_ref)
            dg1_ref[...] = jnp.zeros_like(dg1_ref)
            dsc_ref[...] = jnp.zeros_like(dsc_ref)
            dw_ref[...] = jnp.zeros_like(dw_ref)

        g0 = g0_ref[...]
        g1 = g1_ref[...]
        sc = sc_ref[...]
        xv = x_ref[...]
        h, xh, rx = _rms_fwd(xv, g0)
        h_halo, _, _ = _rms_fwd(xh_ref[...], g0)
        parts, inv_counts = _pool_diff(h_halo * jnp.where(r > 0, 1.0, 0.0), h, r * tm, tm)
        parts_b = [p.astype(BF16) for p in parts]
        ypre = jnp.concatenate([_dot(parts_b[g], w_ref[g]) for g in range(ng)], axis=-1)
        _, yh, ry = _rms_fwd(ypre * sc, g1)
        dm = dx1_ref[...]
        dg1_ref[...] += _colsum(dm * yh)
        dy = _rms_bwd(yh, ry, g1, dm)
        dsc_ref[...] += _colsum(dy * ypre)
        dyp = (dy * sc).astype(BF16)
        ddiffs = []
        for g in range(ng):
            cols = slice(g * POOL_DIM, (g + 1) * POOL_DIM)
            dw_ref[g] += _dot_tn(parts_b[g], dyp[:, cols])
            ddiffs.append(_dot_nt(dyp[:, cols], w_ref[g]))
        e = jnp.concatenate([ddiffs[g] * inv_counts[g] for g in range(ng)], axis=-1)
        sums = _window_sums(jnp.concatenate([e, enext[...]], axis=0), backward=True)
        enext[...] = e[0:POOL_HALO, :]
        dh = jnp.concatenate([sums[g][0:tm, :] - ddiffs[g] for g in range(ng)], axis=-1)
        dg0_ref[...] += _colsum(dh * xh)
        dx_ref[...] = dm + _rms_bwd(xh, rx, g0, dh)

    row = pl.BlockSpec((tm, d), lambda i: (nt - 1 - i, 0))
    halo = pl.BlockSpec((POOL_HALO, d), lambda i: (jnp.maximum((nt - 1 - i) * (tm // POOL_HALO) - 1, 0), 0))
    vec = _full((1, d))
    return pl.pallas_call(
        body, grid=(nt,),
        in_specs=[row, row, halo, vec, vec, _full(pool_w.shape), vec],
        out_specs=[row, vec, vec, vec, _full((ng, POOL_DIM, POOL_DIM))],
        out_shape=[jax.ShapeDtypeStruct((t, d), F32)] + [jax.ShapeDtypeStruct((1, d), F32)] * 3
        + [jax.ShapeDtypeStruct((ng, POOL_DIM, POOL_DIM), F32)],
        scratch_shapes=[pltpu.VMEM((POOL_HALO, d), F32)],
        compiler_params=_params(1), name="pool_bwd")(dx1, x, x, g_pre, g_post, pool_w, pool_scale)


def _conv_taps(cw_ref, j):
    return cw_ref[j, 0:1, :], cw_ref[j, 1:2, :], cw_ref[j, 2:3, :]


def _row_block(m, target=256):
    if m <= target:
        return m
    for b in range(target, 7, -8):
        if m % b == 0:
            return b
    return m


def mlp_fwd(x, g_pre, g_post, w_up, w_down, conv_w, conv_b, target=None, tm=256):
    t, d = x.shape
    nt = t // tm
    h8 = CONV_HALO
    with_loss = target is not None
    n_extra = 1 if with_loss else 0

    def body(x_ref, g2_ref, g3_ref, wup_hbm, wdn_hbm, cw_ref, cb_ref, *rest):
        tgt_ref = rest[0] if with_loss else None
        xo_ref, u_ref, s_ref, a_ref, f_ref, h_ref = rest[n_extra:n_extra + 6]
        loss_ref = rest[n_extra + 6] if with_loss else None
        wup_v, wdn_v, tail, sem = rest[-4:]
        i = pl.program_id(0)

        @pl.when(i == 0)
        def _():
            c1 = pltpu.make_async_copy(wup_hbm, wup_v, sem.at[0])
            c2 = pltpu.make_async_copy(wdn_hbm, wdn_v, sem.at[1])
            c1.start()
            c2.start()
            tail[...] = jnp.zeros_like(tail)
            if with_loss:
                loss_ref[...] = jnp.zeros_like(loss_ref)
            c1.wait()
            c2.wait()

        xv = x_ref[...]
        h, _, _ = _rms_fwd(xv, g2_ref[...])
        hb = h.astype(BF16)
        h_ref[...] = hb
        acc = jnp.zeros((tm, d), F32)
        for k in range(2):
            cs = []
            for s in range(2):
                j, cols = k + 2 * s, slice((2 * k + s) * FF_CHUNK, (2 * k + s + 1) * FF_CHUNK)
                uf = _dot(hb, wup_v[j])
                u_ref[:, cols] = uf.astype(BF16)
                ext = jnp.concatenate([tail[j], uf], axis=0)
                tail[j] = uf[tm - h8:tm, :]
                w0, w1, w2 = _conv_taps(cw_ref, j)
                cs.append(cb_ref[j] + w2 * uf + w1 * pltpu.roll(ext, 1, axis=0)[h8:, :]
                          + w0 * pltpu.roll(ext, 2, axis=0)[h8:, :])
            cg, cv = cs
            sg = jax.nn.sigmoid(cg)
            sil = cg * sg
            ab = (sil * cv).astype(BF16)
            a_ref[:, k * FF_CHUNK:(k + 1) * FF_CHUNK] = ab
            s_ref[:, 2 * k * FF_CHUNK:(2 * k + 1) * FF_CHUNK] = sil.astype(BF16)
            s_ref[:, (2 * k + 1) * FF_CHUNK:(2 * k + 2) * FF_CHUNK] = (cv * (sg * (1.0 + cg * (1.0 - sg)))).astype(BF16)
            acc = acc + _dot(ab, wdn_v[k * FF_CHUNK:(k + 1) * FF_CHUNK, :])
        f_ref[...] = acc
        y, _, _ = _rms_fwd(acc, g3_ref[...])
        if with_loss:
            err = (xv + y) - tgt_ref[...]
            xo_ref[...] = err * (1.0 / d)
            loss_ref[...] += 0.5 * jnp.sum(jnp.mean(err * err, axis=-1, keepdims=True), axis=0, keepdims=True)
        else:
            xo_ref[...] = xv + y

    row = pl.BlockSpec((tm, d), lambda i: (i, 0))
    wide = pl.BlockSpec((tm, 2 * D_FF), lambda i: (i, 0))
    vec = _full((1, d))
    extra = [target] if with_loss else []
    return pl.pallas_call(
        body, grid=(nt,),
        in_specs=[row, vec, vec, ANY, ANY, _full(conv_w.shape), _full(conv_b.shape)] + [row] * n_extra,
        out_specs=[row, wide, wide, pl.BlockSpec((tm, D_FF), lambda i: (i, 0)), row, row] + [_full((1, 1))] * n_extra,
        out_shape=[jax.ShapeDtypeStruct((t, d), F32), jax.ShapeDtypeStruct((t, 2 * D_FF), BF16),
                   jax.ShapeDtypeStruct((t, 2 * D_FF), BF16), jax.ShapeDtypeStruct((t, D_FF), BF16),
                   jax.ShapeDtypeStruct((t, d), F32), jax.ShapeDtypeStruct((t, d), BF16)]
        + [jax.ShapeDtypeStruct((1, 1), F32)] * n_extra,
        scratch_shapes=[pltpu.VMEM(w_up.shape, BF16), pltpu.VMEM(w_down.shape, BF16),
                        pltpu.VMEM((N_SHARD, h8, FF_CHUNK), F32), pltpu.SemaphoreType.DMA((2,))],
        compiler_params=_params(1), name="mlp_fwd_loss" if with_loss else "mlp_fwd")(
            x, g_pre, g_post, w_up, w_down, conv_w, conv_b, *extra)


def _rowsum8(v):
    return jnp.sum(v.reshape(v.shape[0] // 8, 8, v.shape[1]), axis=0)


def mlp_bwd(dxo, f, x, u, sp, g_pre, g_post, w_up, w_down, conv_w, tm=256):
    t, d = x.shape
    nt = t // tm
    h8 = CONV_HALO

    def body(dxo_ref, f_ref, x_ref, u_ref, s_ref, g2_ref, g3_ref, wup_hbm, wdn_hbm, cw_ref,
             dx_ref, du_ref, df_ref, dg2_ref, dg3_ref, dcw_ref, dcb_ref,
             wup_v, wdn_v, carry, sem):
        @pl.when(pl.program_id(0) == 0)
        def _():
            c1 = pltpu.make_async_copy(wup_hbm, wup_v, sem.at[0])
            c2 = pltpu.make_async_copy(wdn_hbm, wdn_v, sem.at[1])
            c1.start()
            c2.start()
            carry[...] = jnp.zeros_like(carry)
            dg2_ref[...] = jnp.zeros_like(dg2_ref)
            dg3_ref[...] = jnp.zeros_like(dg3_ref)
            dcw_ref[...] = jnp.zeros_like(dcw_ref)
            dcb_ref[...] = jnp.zeros_like(dcb_ref)
            c1.wait()
            c2.wait()

        g3 = g3_ref[...]
        dxo = dxo_ref[...]
        _, fh, rf = _rms_fwd(f_ref[...], g3)
        dg3_ref[...] += _rowsum8(dxo * fh)
        dfb = _rms_bwd(fh, rf, g3, dxo).astype(BF16)
        df_ref[...] = dfb
        dh = jnp.zeros((tm, d), F32)
        for k in range(2):
            da = _dot_nt(dfb, wdn_v[k * FF_CHUNK:(k + 1) * FF_CHUNK, :])
            for s in range(2):
                j = k + 2 * s
                cols = slice((2 * k + s) * FF_CHUNK, (2 * k + s + 1) * FF_CHUNK)
                dc = da * s_ref[:, (2 * k + 1 - s) * FF_CHUNK:(2 * k + 2 - s) * FF_CHUNK].astype(F32)
                uf = u_ref[:, cols].astype(F32)
                ext = jnp.concatenate([dc, carry[j]], axis=0)
                carry[j] = dc[0:h8, :]
                dc1 = pltpu.roll(ext, tm + h8 - 1, axis=0)[0:tm, :]
                dc2 = pltpu.roll(ext, tm + h8 - 2, axis=0)[0:tm, :]
                dcb_ref[j] += _rowsum8(dc)
                dcw_ref[j, 2] += _rowsum8(dc * uf)
                dcw_ref[j, 1] += _rowsum8(dc1 * uf)
                dcw_ref[j, 0] += _rowsum8(dc2 * uf)
                dub = (cw_ref[j, 2:3, :] * dc + cw_ref[j, 1:2, :] * dc1 + cw_ref[j, 0:1, :] * dc2).astype(BF16)
                du_ref[:, cols] = dub
                dh = dh + _dot_nt(dub, wup_v[j])
        g2 = g2_ref[...]
        _, xh, rx = _rms_fwd(x_ref[...], g2)
        dg2_ref[...] += _rowsum8(dh * xh)
        dx_ref[...] = dxo + _rms_bwd(xh, rx, g2, dh)

    row = pl.BlockSpec((tm, d), lambda i: (nt - 1 - i, 0))
    wide = pl.BlockSpec((tm, 2 * D_FF), lambda i: (nt - 1 - i, 0))
    vec = _full((1, d))
    acc = _full((8, d))
    dcw_shape, dcb_shape = (N_SHARD, 3, 8, FF_CHUNK), (N_SHARD, 8, FF_CHUNK)
    return pl.pallas_call(
        body, grid=(nt,),
        in_specs=[row, row, row, wide, wide, vec, vec, ANY, ANY, _full(conv_w.shape)],
        out_specs=[row, wide, row, acc, acc, _full(dcw_shape), _full(dcb_shape)],
        out_shape=[jax.ShapeDtypeStruct((t, d), F32), jax.ShapeDtypeStruct((t, 2 * D_FF), BF16),
                   jax.ShapeDtypeStruct((t, d), BF16),
                   jax.ShapeDtypeStruct((8, d), F32), jax.ShapeDtypeStruct((8, d), F32),
                   jax.ShapeDtypeStruct(dcw_shape, F32), jax.ShapeDtypeStruct(dcb_shape, F32)],
        scratch_shapes=[pltpu.VMEM(w_up.shape, BF16), pltpu.VMEM(w_down.shape, BF16),
                        pltpu.VMEM((N_SHARD, h8, FF_CHUNK), F32), pltpu.SemaphoreType.DMA((2,))],
        compiler_params=_params(1, VMEM_LIMIT_MLP_BWD), name="mlp_bwd")(
            dxo, f, x, u, sp, g_pre, g_post, w_up, w_down, conv_w)


def grad_matmul(a, b, bm, bn, name, tk=2048, interleaved=False, after=None, cols=None):
    t = a.shape[0]
    m0, m = (0, a.shape[1]) if cols is None else cols
    n = b.shape[1]
    tk = min(tk, t)
    nk = t // tk
    place = (lambda j: (j % 2) * 2 + j // 2) if interleaved else (lambda j: j)
    extra = [] if after is None else [after]
    first = m0 // bm

    def body(a_ref, b_ref, *rest):
        o_ref, ob_ref = rest[len(extra):]
        kk = pl.program_id(2)

        @pl.when(kk == 0)
        def _():
            o_ref[...] = jnp.zeros_like(o_ref)

        o_ref[...] += _dot_tn(a_ref[...], b_ref[...])

        @pl.when(kk == nk - 1)
        def _():
            ob_ref[...] = o_ref[...].astype(BF16)

    ospec = pl.BlockSpec((None, bm, bn), lambda j, i, kk: (place(j), i, 0))
    return pl.pallas_call(
        body, grid=(n // bn, m // bm, nk),
        in_specs=[pl.BlockSpec((tk, bm), lambda j, i, kk: (kk, first + i)),
                  pl.BlockSpec((tk, bn), lambda j, i, kk: (kk, j))]
        + [ANY] * len(extra),
        out_specs=[ospec, ospec],
        out_shape=[jax.ShapeDtypeStruct((n // bn, m, bn), F32), jax.ShapeDtypeStruct((n // bn, m, bn), BF16)],
        compiler_params=_params(3), name=name)(a, b, *extra)


def _decay_tables():
    log_gamma = jnp.log(1.0 - 2.0 ** (-5.0 - jnp.arange(RET_HEADS, dtype=F32)))
    i = jnp.arange(RET_CHUNK, dtype=F32)
    rel = i[:, None] - i[None, :]
    intra = jnp.where(rel >= 0, jnp.exp(jnp.maximum(rel, 0.0) * log_gamma[:, None, None]), 0.0)
    cross = jnp.exp((i + 1.0) * log_gamma[:, None])[:, :, None]
    inner = jnp.exp((RET_CHUNK - 1.0 - i) * log_gamma[:, None])[:, :, None]
    chunk = [float(np.exp(np.float32(RET_CHUNK) * np.log(np.float32(1.0 - 2.0 ** (-5.0 - h))).astype(np.float32)))
             for h in range(RET_HEADS)]
    return intra, cross, inner, chunk


def ret_proj(x, g_pre, w_in, cos, sin, tm=512):
    t, d = x.shape
    nt = t // tm
    per = RET_IN_SHARD // RET_QK

    def body(x_ref, g_ref, win_hbm, c_ref, s_ref, pj_ref, h_ref, win_v, sem):
        @pl.when(pl.program_id(0) == 0)
        def _():
            cp = pltpu.make_async_copy(win_hbm, win_v, sem)
            cp.start()
            cp.wait()

        h, _, _ = _rms_fwd(x_ref[...], g_ref[...])
        hb = h.astype(BF16)
        h_ref[...] = hb
        c = c_ref[...]
        s = s_ref[...]
        for j in range(N_SHARD):
            pjj = _dot(hb, win_v[j])
            for bb in range(per):
                b = per * j + bb
                blk = pjj[:, bb * RET_QK:(bb + 1) * RET_QK]
                if b < 2 * RET_HEADS:
                    x1, x2 = blk[:, :128], blk[:, 128:]
                    o1 = x1 * c - x2 * s
                    o2 = x2 * c + x1 * s
                    if b < RET_HEADS:
                        o1 = o1 * (RET_QK ** -0.5)
                        o2 = o2 * (RET_QK ** -0.5)
                    pj_ref[:, b * RET_QK:b * RET_QK + 128] = o1.astype(BF16)
                    pj_ref[:, b * RET_QK + 128:(b + 1) * RET_QK] = o2.astype(BF16)
                else:
                    pj_ref[:, b * RET_QK:(b + 1) * RET_QK] = blk.astype(BF16)

    row = pl.BlockSpec((tm, d), lambda i: (i, 0))
    tab = pl.BlockSpec((tm, 128), lambda i: (i, 0))
    return pl.pallas_call(
        body, grid=(nt,),
        in_specs=[row, _full((1, d)), ANY, tab, tab],
        out_specs=[pl.BlockSpec((tm, RET_IN), lambda i: (i, 0)), row],
        out_shape=[jax.ShapeDtypeStruct((t, RET_IN), BF16), jax.ShapeDtypeStruct((t, d), BF16)],
        scratch_shapes=[pltpu.VMEM(w_in.shape, BF16), pltpu.SemaphoreType.DMA],
        compiler_params=_params(1), name="ret_proj")(x, g_pre, w_in, cos, sin)


def ret_core_fwd(pj, intra, cross, inner, chunk_decay):
    t = pj.shape[0]
    nc = t // RET_CHUNK
    c = RET_CHUNK
    qk_all = RET_HEADS * RET_QK
    v_all = RET_HEADS * RET_V

    def body(q_ref, k_ref, v_ref, dm_ref, cr_ref, in_ref, o_ref, sp_ref, state):
        @pl.when(pl.program_id(0) == 0)
        def _():
            state[...] = jnp.zeros_like(state)

        for h in range(RET_HEADS):
            q = q_ref[:, h * RET_QK:(h + 1) * RET_QK]
            k = k_ref[:, h * RET_QK:(h + 1) * RET_QK]
            v = v_ref[:, h * RET_V:(h + 1) * RET_V]
            sb = state[h].astype(BF16)
            sp_ref[h] = sb
            sc = _dot_nt(q, k) * dm_ref[h]
            o_ref[:, h * RET_V:(h + 1) * RET_V] = _dot(sc.astype(BF16), v) + _dot(q, sb) * cr_ref[h]
            kd = (k.astype(F32) * in_ref[h]).astype(BF16)
            state[h] = state[h] * chunk_decay[h] + _dot_tn(kd, v)

    return pl.pallas_call(
        body, grid=(nc,),
        in_specs=[pl.BlockSpec((c, qk_all), lambda n: (n, 0)), pl.BlockSpec((c, qk_all), lambda n: (n, 1)),
                  pl.BlockSpec((c, v_all), lambda n: (n, 1)),
                  _full(intra.shape), _full(cross.shape), _full(inner.shape)],
        out_specs=[pl.BlockSpec((c, v_all), lambda n: (n, 0)),
                   pl.BlockSpec((None, RET_HEADS, RET_QK, RET_V), lambda n: (n, 0, 0, 0))],
        out_shape=[jax.ShapeDtypeStruct((t, v_all), F32),
                   jax.ShapeDtypeStruct((nc, RET_HEADS, RET_QK, RET_V), BF16)],
        scratch_shapes=[pltpu.VMEM((RET_HEADS, RET_QK, RET_V), F32)],
        compiler_params=_params(1), name="ret_core_fwd")(pj, pj, pj, intra, cross, inner)


def _group_norm(o_h):
    mu = jnp.mean(o_h, axis=-1, keepdims=True)
    dev = o_h - mu
    rstd = lax.rsqrt(jnp.mean(dev * dev, axis=-1, keepdims=True) + EPS)
    return dev * rstd, rstd


def ret_out_fwd(o, pj, x, gn_gain, g_post, w_out, tm=512):
    t, d = x.shape
    nt = t // tm
    v_all = RET_HEADS * RET_V

    def body(o_ref, g_ref, x_ref, gn_ref, g1_ref, w_ref, xo_ref, y_ref, out_ref):
        out = jnp.zeros((tm, d), F32)
        for h in range(RET_HEADS):
            cols = slice(h * RET_V, (h + 1) * RET_V)
            ohat, _ = _group_norm(o_ref[:, cols])
            g = g_ref[:, cols].astype(F32)
            yb = (g * jax.nn.sigmoid(g) * (ohat * gn_ref[:, cols])).astype(BF16)
            y_ref[:, cols] = yb
            out = out + _dot(yb, w_ref[cols, :])
        out_ref[...] = out
        m, _, _ = _rms_fwd(out, g1_ref[...])
        xo_ref[...] = x_ref[...] + m

    row = pl.BlockSpec((tm, d), lambda i: (i, 0))
    wide = pl.BlockSpec((tm, v_all), lambda i: (i, 0))
    return pl.pallas_call(
        body, grid=(nt,),
        in_specs=[wide, pl.BlockSpec((tm, v_all), lambda i: (i, 2)), row, _full((1, v_all)), _full((1, d)),
                  _full(w_out.shape)],
        out_specs=[row, wide, row],
        out_shape=[jax.ShapeDtypeStruct((t, d), F32), jax.ShapeDtypeStruct((t, v_all), BF16),
                   jax.ShapeDtypeStruct((t, d), F32)],
        compiler_params=_params(1), name="ret_out_fwd")(o, pj, x, gn_gain, g_post, w_out)


def ret_out_bwd(dxo, out, o, pj, gn_gain, g_post, w_out, tm=256):
    t, d = out.shape
    nt = t // tm
    v_all = RET_HEADS * RET_V

    def body(dxo_ref, out_ref, o_ref, g_ref, gn_ref, g1_ref, w_ref,
             dout_ref, dgate_ref, do_ref, dg1_ref, dgn_ref):
        @pl.when(pl.program_id(0) == 0)
        def _():
            dg1_ref[...] = jnp.zeros_like(dg1_ref)
            dgn_ref[...] = jnp.zeros_like(dgn_ref)

        g1 = g1_ref[...]
        dxo = dxo_ref[...]
        _, oh_, r_ = _rms_fwd(out_ref[...], g1)
        dg1_ref[...] += _colsum(dxo * oh_)
        doutb = _rms_bwd(oh_, r_, g1, dxo).astype(BF16)
        dout_ref[...] = doutb
        for h in range(RET_HEADS):
            cols = slice(h * RET_V, (h + 1) * RET_V)
            gn = gn_ref[:, cols]
            ohat, rstd = _group_norm(o_ref[:, cols])
            g = g_ref[:, cols].astype(F32)
            sg = jax.nn.sigmoid(g)
            dyh = _dot_nt(doutb, w_ref[cols, :])
            dgate_ref[:, cols] = (dyh * (ohat * gn) * (sg * (1.0 + g * (1.0 - sg)))).astype(BF16)
            don = dyh * (g * sg)
            dgn_ref[:, cols] += _colsum(don * ohat)
            dohat = don * gn
            do_ref[:, cols] = (rstd * (dohat - jnp.mean(dohat, axis=-1, keepdims=True)
                                       - ohat * jnp.mean(dohat * ohat, axis=-1, keepdims=True))).astype(BF16)

    row = pl.BlockSpec((tm, d), lambda i: (i, 0))
    wide = pl.BlockSpec((tm, v_all), lambda i: (i, 0))
    gate = pl.BlockSpec((tm, v_all), lambda i: (i, 2))
    return pl.pallas_call(
        body, grid=(nt,),
        in_specs=[row, row, wide, gate, _full((1, v_all)), _full((1, d)), _full(w_out.shape)],
        out_specs=[row, gate, wide, _full((1, d)), _full((1, v_all))],
        out_shape=[jax.ShapeDtypeStruct((t, d), BF16), jax.ShapeDtypeStruct((t, RET_IN), BF16),
                   jax.ShapeDtypeStruct((t, v_all), BF16), jax.ShapeDtypeStruct((1, d), F32),
                   jax.ShapeDtypeStruct((1, v_all), F32)],
        compiler_params=_params(1), name="ret_out_bwd")(dxo, out, o, pj, gn_gain, g_post, w_out)


def ret_core_bwd(pj, do, sprev, cos, sin, dpj, intra, cross, inner, chunk_decay):
    t = pj.shape[0]
    nc = t // RET_CHUNK
    c = RET_CHUNK
    qk_all = RET_HEADS * RET_QK
    v_all = RET_HEADS * RET_V
    scale = RET_QK ** -0.5

    def body(q_ref, k_ref, v_ref, do_ref, sp_ref, c_ref, s_ref, dm_ref, cr_ref, in_ref, dpj_in, dpj_ref, dstate):
        @pl.when(pl.program_id(0) == 0)
        def _():
            dstate[...] = jnp.zeros_like(dstate)

        cs = c_ref[...]
        sn = s_ref[...]
        for h in range(RET_HEADS):
            q = q_ref[:, h * RET_QK:(h + 1) * RET_QK]
            k = k_ref[:, h * RET_QK:(h + 1) * RET_QK]
            v = v_ref[:, h * RET_V:(h + 1) * RET_V]
            doh = do_ref[:, h * RET_V:(h + 1) * RET_V]
            dm = dm_ref[h]
            ab = (_dot_nt(q, k) * dm).astype(BF16)
            dab = (_dot_nt(doh, v) * dm).astype(BF16)
            dsb = dstate[h].astype(BF16)
            kd = (k.astype(F32) * in_ref[h]).astype(BF16)
            dv = _dot_tn(ab, doh) + _dot(kd, dsb)
            dq = _dot(dab, k) + cr_ref[h] * _dot_nt(doh, sp_ref[h])
            dk = _dot_tn(dab, q) + in_ref[h] * _dot_nt(v, dsb)
            qd = (q.astype(F32) * cr_ref[h]).astype(BF16)
            dstate[h] = dstate[h] * chunk_decay[h] + _dot_tn(qd, doh)
            for base, dd, sc in ((h * RET_QK, dq, scale), (qk_all + h * RET_QK, dk, 1.0)):
                d1, d2 = dd[:, :128], dd[:, 128:]
                dpj_ref[:, base:base + 128] = ((d1 * cs + d2 * sn) * sc).astype(BF16)
                dpj_ref[:, base + 128:base + RET_QK] = ((d2 * cs - d1 * sn) * sc).astype(BF16)
            dpj_ref[:, 2 * qk_all + h * RET_V:2 * qk_all + (h + 1) * RET_V] = dv.astype(BF16)

    rev = lambda n: nc - 1 - n
    tab = pl.BlockSpec((c, 128), lambda n: (rev(n), 0))
    return pl.pallas_call(
        body, grid=(nc,),
        in_specs=[pl.BlockSpec((c, qk_all), lambda n: (rev(n), 0)), pl.BlockSpec((c, qk_all), lambda n: (rev(n), 1)),
                  pl.BlockSpec((c, v_all), lambda n: (rev(n), 1)), pl.BlockSpec((c, v_all), lambda n: (rev(n), 0)),
                  pl.BlockSpec((None, RET_HEADS, RET_QK, RET_V), lambda n: (rev(n), 0, 0, 0)),
                  tab, tab, _full(intra.shape), _full(cross.shape), _full(inner.shape), ANY],
        out_specs=pl.BlockSpec((c, 2 * qk_all + v_all), lambda n: (rev(n), 0)),
        out_shape=jax.ShapeDtypeStruct((t, RET_IN), BF16),
        scratch_shapes=[pltpu.VMEM((RET_HEADS, RET_QK, RET_V), F32)],
        input_output_aliases={10: 0},
        compiler_params=_params(1), name="ret_core_bwd")(pj, pj, pj, do, sprev, cos, sin, intra, cross, inner, dpj)


def ret_in_bwd(dpj, dres, x, g_pre, w_in, tm=512):
    t, d = x.shape
    nt = t // tm

    def body(dpj_ref, dres_ref, x_ref, g_ref, win_hbm, dx_ref, dg_ref, win_v, sem):
        @pl.when(pl.program_id(0) == 0)
        def _():
            cp = pltpu.make_async_copy(win_hbm, win_v, sem)
            cp.start()
            dg_ref[...] = jnp.zeros_like(dg_ref)
            cp.wait()

        g = g_ref[...]
        dh = jnp.zeros((tm, d), F32)
        for j in range(N_SHARD):
            dh = dh + _dot_nt(dpj_ref[:, j * RET_IN_SHARD:(j + 1) * RET_IN_SHARD], win_v[j])
        _, xh, rx = _rms_fwd(x_ref[...], g)
        dg_ref[...] += _colsum(dh * xh)
        dx_ref[...] = dres_ref[...] + _rms_bwd(xh, rx, g, dh)

    row = pl.BlockSpec((tm, d), lambda i: (i, 0))
    return pl.pallas_call(
        body, grid=(nt,),
        in_specs=[pl.BlockSpec((tm, RET_IN), lambda i: (i, 0)), row, row, _full((1, d)), ANY],
        out_specs=[row, _full((1, d))],
        out_shape=[jax.ShapeDtypeStruct((t, d), F32), jax.ShapeDtypeStruct((1, d), F32)],
        scratch_shapes=[pltpu.VMEM(w_in.shape, BF16), pltpu.SemaphoreType.DMA],
        compiler_params=_params(1), name="ret_in_bwd")(dpj, dres, x, g_pre, w_in)


_CHIP_FLIPS = ((1, 0), (0, 1), (1, 1))


def _flip(v, b):
    return 1 - v if b else v


_HBM = pl.BlockSpec(memory_space=pltpu.HBM)
_SEM = pl.BlockSpec(memory_space=pltpu.SEMAPHORE)
_EFFECT = pltpu.SideEffectType.DATAFLOW_SIDE_EFFECTING


def _chip_copies(mode, srcs, lands, send_sems, recv_sems):
    x, y, c = lax.axis_index("x"), lax.axis_index("y"), lax.axis_index("c")
    copies = []
    for t in range(len(lands)):
        if mode == "swap":
            copies.append(pltpu.make_async_remote_copy(
                src_ref=srcs[t], dst_ref=lands[t], send_sem=send_sems.at[t], recv_sem=recv_sems.at[t],
                device_id=(x, y, 1 - c), device_id_type=MESH))
            continue
        if mode == "everyone":
            for m in range(1, 8):
                bx, by, bc = (m >> 2) & 1, (m >> 1) & 1, m & 1
                copies.append(pltpu.make_async_remote_copy(
                    src_ref=srcs[t], dst_ref=lands[t].at[4 * x + 2 * y + c], send_sem=send_sems.at[7 * t + m - 1],
                    recv_sem=recv_sems.at[7 * t + m - 1], device_id=(_flip(x, bx), _flip(y, by), _flip(c, bc)),
                    device_id_type=MESH))
            continue
        for k, (bx, by) in enumerate(_CHIP_FLIPS):
            px, py = _flip(x, bx), _flip(y, by)
            target = (px, py, c)
            if mode == "gather":
                src, dst = srcs[t], lands[t].at[2 * x + y]
            elif mode == "gather_half":
                half = pl.ds(c * (srcs[t].shape[0] // 2), srcs[t].shape[0] // 2)
                src, dst = srcs[t].at[half], lands[t].at[2 * x + y, half]
            elif mode == "forward_half":
                half = pl.ds(c * (lands[t].shape[1] // 2), lands[t].shape[1] // 2)
                src = dst = lands[t].at[2 * px + py, half]
                target = (x, y, 1 - c)
            else:
                src, dst = srcs[t].at[2 * px + py], lands[t].at[k]
            copies.append(pltpu.make_async_remote_copy(
                src_ref=src, dst_ref=dst, send_sem=send_sems.at[3 * t + k], recv_sem=recv_sems.at[3 * t + k],
                device_id=target, device_id_type=MESH))
    return copies


def exchange_start(mode, srcs, lands, name, after=None):
    n, ns = len(lands), len(srcs)
    extra = [] if after is None else [after]

    def body(*refs):
        ins, lnd = refs[:ns], refs[ns:ns + n]
        send_sems, recv_sems = refs[ns + n + len(extra)], refs[ns + n + len(extra) + 1]
        token = refs[-1]
        for cp in _chip_copies(mode, ins, lnd, send_sems, recv_sems):
            cp.start()
        token[...] = jnp.zeros(token.shape, token.dtype)

    hbm = lambda a: pltpu.with_memory_space_constraint(a, pltpu.HBM)
    passed = list(srcs) + list(lands)
    n_sem = {"swap": 1, "everyone": 7}.get(mode, 3) * n
    return pl.pallas_call(
        body, name=name,
        out_shape=(pltpu.SemaphoreType.DMA((n_sem,)), pltpu.SemaphoreType.DMA((n_sem,)),
                   *[pltpu.HBM(a.shape, a.dtype) for a in passed], jax.ShapeDtypeStruct((8, 128), F32)),
        in_specs=[_HBM] * (ns + n) + [ANY] * len(extra),
        out_specs=(_SEM, _SEM, *[_HBM] * (ns + n), pl.BlockSpec(memory_space=pltpu.VMEM)),
        input_output_aliases={i: 2 + i for i in range(ns + n)},
        compiler_params=pltpu.CompilerParams(has_side_effects=_EFFECT))(*[hbm(a) for a in passed], *extra)


def exchange_wait(mode, started, after, name):
    send_sems, recv_sems = started[0], started[1]
    passed = list(started[2:-1])
    n = len(passed) if mode == "forward_half" else len(passed) // 2
    ns = len(passed) - n

    def body(*refs):
        ins, lnd = refs[:ns], refs[ns:ns + n]
        for cp in _chip_copies(mode, ins, lnd, refs[ns + n], refs[ns + n + 1]):
            cp.wait_send()
            cp.wait_recv()

    outs = pl.pallas_call(
        body, name=name, out_shape=tuple(pltpu.HBM(a.shape, a.dtype) for a in passed),
        in_specs=[_HBM] * (ns + n) + [_SEM, _SEM, ANY], out_specs=tuple([_HBM] * (ns + n)),
        input_output_aliases={i: i for i in range(ns + n)},
        compiler_params=pltpu.CompilerParams(has_side_effects=_EFFECT))(*passed, send_sems, recv_sems, after)
    return list(outs[:ns]), list(outs[ns:])


def plane_sum(slot, full, recv, name, bm=256):
    _, m, n = full.shape
    bm = _row_block(m, bm)

    def body(slot_ref, o_ref, r_ref, s_ref):
        s_ref[...] = ((o_ref[...] + r_ref[0].astype(F32)) + r_ref[1].astype(F32)) + r_ref[2].astype(F32)

    return pl.pallas_call(
        body,
        grid_spec=pltpu.PrefetchScalarGridSpec(
            num_scalar_prefetch=1, grid=(m // bm,),
            in_specs=[pl.BlockSpec((None, bm, n), lambda i, s: (s[0], i, 0)),
                      pl.BlockSpec((3, bm, n), lambda i, s: (0, i, 0))],
            out_specs=pl.BlockSpec((bm, n), lambda i, s: (i, 0))),
        out_shape=jax.ShapeDtypeStruct((m, n), F32), compiler_params=_params(1), name=name)(slot, full, recv)


def sum_slots(parts, name, bm=312):
    _, r, n = parts.shape
    bm = bm if r % bm == 0 else r

    def body(p_ref, s_ref):
        acc = p_ref[0]
        for k in range(1, 8):
            acc = acc + p_ref[k]
        s_ref[...] = acc

    return pl.pallas_call(
        body, grid=(r // bm,), in_specs=[pl.BlockSpec((8, bm, n), lambda i: (0, i, 0))],
        out_specs=pl.BlockSpec((bm, n), lambda i: (i, 0)), out_shape=jax.ShapeDtypeStruct((r, n), F32),
        compiler_params=_params(1), name=name)(parts)


def _adamw_math(w, g, m, v):
    m = ADAM_B1 * m + (1.0 - ADAM_B1) * g
    v = ADAM_B2 * v + (1.0 - ADAM_B2) * (g * g)
    m_hat = m / (1.0 - ADAM_B1 ** ADAM_STEP)
    v_hat = v / (1.0 - ADAM_B2 ** ADAM_STEP)
    delta = -ADAM_LR * (m_hat / (jnp.sqrt(v_hat) + ADAM_EPS) + ADAM_WD * w)
    return delta, m, v


def adamw(w, m, v, grads, layer, prev, name, bm=256, row0=0):
    _, _, n = w.shape
    mm = grads[0].shape[0]
    bm = _row_block(mm, bm)
    first = row0 // bm
    ng = len(grads)

    def body(*refs):
        w_ref, m_ref, v_ref = refs[:3]
        g_refs = refs[3:3 + ng]
        g_out, d_out, m_out, v_out = refs[-4:]
        g = g_refs[0][...]
        for gr in g_refs[1:]:
            g = g + gr[...]
        delta, mn, vn = _adamw_math(w_ref[...], g, m_ref[...], v_ref[...])
        g_out[...] = g
        d_out[...] = delta
        m_out[...] = mn
        v_out[...] = vn

    slab = pl.BlockSpec((None, bm, n), lambda i: (layer, first + i, 0))
    flat = pl.BlockSpec((bm, n), lambda i: (i, 0))
    in_specs = [slab] * 3 + [flat] * ng
    args = [w, m, v, *grads]
    aliases = {}
    if prev is not None:
        in_specs += [ANY] * 4
        aliases = {3 + ng + q: q for q in range(4)}
        args += list(prev)
    return pl.pallas_call(
        body, grid=(mm // bm,), in_specs=in_specs, out_specs=[slab] * 4,
        out_shape=[jax.ShapeDtypeStruct(w.shape, F32)] * 4, input_output_aliases=aliases,
        compiler_params=_params(1), name=name)(*args)


def _pack_rows(parts, rows):
    flat = jnp.concatenate([p.reshape(-1) for p in parts])
    return jnp.pad(flat, (0, rows * 128 - flat.shape[0])).reshape(rows, 128)


def _as_shards(a, rows):
    return a.reshape(N_SHARD, rows, a.shape[-1])


def _local_step(x, pos_col, target, gains, pool_w, pool_scale, gn_gain, conv_w, conv_b, weights, send_grads):
    def gain(l, n, token=None):
        g = gains[l, n].reshape(1, D_MODEL)
        return g if token is None else g + token[0:1, 0:1]

    inv_freq = (ROPE_BASE ** (-jnp.arange(0, RET_QK, 2, dtype=F32) / RET_QK)).reshape(1, RET_QK // 2)
    intra, cross, inner, chunk_decay = _decay_tables()
    dn_rows = D_FF // N_SHARD

    x1 = pool_fwd(x, gain(0, 0), gain(0, 1), pool_w, pool_scale)
    cos, sin = rope_tables(pos_col, inv_freq, x1)
    w_up0, w_dn0 = weights("mlp0", cos)
    w_dn0 = w_dn0.reshape(D_FF, D_MODEL)
    x2, u0, s0, a0, f0, h0 = mlp_fwd(x1, gain(0, 2), gain(0, 3), w_up0, w_dn0, conv_w[0], conv_b[0])
    w_in, w_out = weights("ret", x2)
    w_out = w_out.reshape(RET_HEADS * RET_V, D_MODEL)
    pj, hr = ret_proj(x2, gain(1, 0), w_in, cos, sin)
    o, sprev = ret_core_fwd(pj, intra, cross, inner, chunk_decay)
    x3, yb, out = ret_out_fwd(o, pj, x2, gn_gain, gain(1, 1), w_out)
    w_up1, w_dn1 = weights("mlp1", x3)
    w_dn1 = w_dn1.reshape(D_FF, D_MODEL)
    dx4, u1, s1, a1, f1, h1, loss = mlp_fwd(x3, gain(1, 2), gain(1, 3), w_up1, w_dn1, conv_w[1], conv_b[1], target)

    dx3, du1, df1, dg12, dg13, dcw1, dcb1 = mlp_bwd(
        dx4, f1, x3, u1, s1, gain(1, 2), gain(1, 3), w_up1, w_dn1, conv_w[1])
    dwup1 = grad_matmul(h1, du1, D_MODEL, FF_CHUNK, "grad_w_up_1", interleaved=True)
    dwdn1 = grad_matmul(a1, df1, FF_CHUNK, D_MODEL, "grad_w_down_1")
    tok = send_grads("mlp1", [dwup1, [_as_shards(g, dn_rows) for g in dwdn1]])
    dout, dpj, do, dg11, dgn = ret_out_bwd(dx3, out, o, pj, gn_gain, gain(1, 1, tok), w_out)
    dwout = grad_matmul(yb, dout, 1024, D_MODEL, "grad_w_out")
    dpj = ret_core_bwd(pj, do, sprev, cos, sin, dpj, intra, cross, inner, chunk_decay)
    dwin = grad_matmul(hr, dpj, D_MODEL, RET_IN_SHARD, "grad_w_in")
    tok = send_grads("ret", [dwin, [_as_shards(g, RET_V) for g in dwout]])
    dx2, dg10 = ret_in_bwd(dpj, dx3, x2, gain(1, 0, tok), w_in)
    dx1, du0, df0, dg02, dg03, dcw0, dcb0 = mlp_bwd(
        dx2, f0, x1, u0, s0, gain(0, 2), gain(0, 3), w_up0, w_dn0, conv_w[0])
    dwdn0 = grad_matmul(a0, df0, FF_CHUNK, D_MODEL, "grad_w_down_0")
    tok = send_grads("down0", [[_as_shards(g, dn_rows) for g in dwdn0]])
    half = D_MODEL // 2
    for part, first in (("a", 0), ("b", half)):
        dwup0 = grad_matmul(h0, du0, half, FF_CHUNK, "grad_w_up_0" + part, tk=4096, interleaved=True, after=tok,
                            cols=(first, half))
        tok = send_grads("up0" + part, [dwup0])
    dx0, dg00, dg01, dpscale, dpw = pool_bwd(dx1, x, gain(0, 0, tok), gain(0, 1), pool_w, pool_scale)

    rows = lambda g: jnp.sum(g, axis=0, keepdims=True)
    dgains = jnp.concatenate([dg00, dg01, rows(dg02), rows(dg03), dg10, dg11, rows(dg12), rows(dg13)],
                             axis=0).reshape(2, 4, D_MODEL)
    small = {"gains": dgains, "pool_scale": dpscale, "gn": dgn,
             "conv_w": jnp.sum(jnp.stack([dcw0, dcw1]), axis=3),
             "conv_b": jnp.sum(jnp.stack([dcb0, dcb1]), axis=2, keepdims=True), "pool_w": dpw}
    return loss, dx0, small


def kernel(x, positions, norm_gain, pool_w, pool_scale, ret_w_in, ret_gn_gain, ret_w_out, mlp_w_up, mlp_conv_w, mlp_conv_b, mlp_w_down, loss_target, m_norm_gain, m_pool_w, m_pool_scale, m_ret_w_in, m_ret_gn_gain, m_ret_w_out, m_mlp_w_up, m_mlp_conv_w, m_mlp_conv_b, m_mlp_w_down, v_norm_gain, v_pool_w, v_pool_scale, v_ret_w_in, v_ret_gn_gain, v_ret_w_out, v_mlp_w_up, v_mlp_conv_w, v_mlp_conv_b, v_mlp_w_down):
    t = x.shape[1]
    me = 2 * lax.axis_index("x") + lax.axis_index("y")
    me_slot = jnp.reshape(me, (1,)).astype(jnp.int32)

    small_parts = [norm_gain, ret_gn_gain, mlp_conv_w, pool_w]
    small_sizes = [p.size for p in small_parts]
    small_rows = -(-sum(small_sizes) // (128 * 8)) * 8
    gathers = {}

    def start_gather(group, srcs, after):
        lands = [lax.dynamic_update_index_in_dim(lax.empty((N_SHARD,) + s.shape, s.dtype), s, me, 0) for s in srcs]
        mode = "gather_half" if group == "mlp0" else "gather"
        gathers[group] = (mode, exchange_start(mode, srcs, lands, "gather_start_" + group, after=after))
        return gathers[group][1][-1]

    token = start_gather("small", [_pack_rows(small_parts, small_rows)], None)
    token = start_gather("mlp0", [mlp_w_up[0].astype(BF16), mlp_w_down[0].astype(BF16)], token)

    def weights(group, after):
        if group == "mlp0":
            tok = start_gather("ret", [ret_w_in[0].astype(BF16), ret_w_out[0].astype(BF16)], after)
            after = start_gather("mlp1", [mlp_w_up[1].astype(BF16), mlp_w_down[1].astype(BF16)], tok)
        mode, started = gathers[group]
        _, lands = exchange_wait(mode, started, after, "gather_wait_" + group)
        if mode == "gather_half":
            forward = exchange_start("forward_half", [], lands, "forward_start_" + group)
            _, lands = exchange_wait("forward_half", forward, forward[-1], "forward_wait_" + group)
        return lands

    sent, early = {}, {}

    def reduced(group, after, names):
        started, own = sent[group]
        _, recv = exchange_wait("scatter", started, after, "scatter_wait_" + group)
        return [plane_sum(me_slot, f, r, "plane_sum_" + nm)
                for f, r, nm in zip(own, recv, names)]

    def swap_start(planes, name):
        return exchange_start("swap", planes, [lax.empty(p.shape, p.dtype) for p in planes], name)

    def send_grads(group, pairs):
        lands = [lax.empty((3,) + b.shape[1:], BF16) for _, b in pairs]
        sent[group] = (exchange_start("scatter", [b for _, b in pairs], lands, "scatter_start_" + group),
                       [f for f, _ in pairs])
        token = sent[group][0][-1]
        if group == "down0":
            marker = pairs[0][1]
            early["planes"] = (reduced("mlp1", marker, ["w_up_1", "w_down_1"])
                               + reduced("ret", marker, ["w_in", "w_out"]))
            early["swap"] = swap_start(early["planes"], "swap_start_a")
            token = token + early["swap"][-1]
        return token

    (smallg,) = weights("small", token)
    smallg = smallg.reshape(N_SHARD, -1)
    offs = np.cumsum([0] + small_sizes)
    piece = lambda i, shape: smallg[:, offs[i]:offs[i + 1]].reshape((N_SHARD,) + shape)
    gains = piece(0, (2, 4, 256)).transpose(1, 2, 0, 3).reshape(2, 4, D_MODEL)
    gn_full = piece(1, (512,)).reshape(1, RET_HEADS * RET_V)
    cw_full = piece(2, (2, 3, FF_CHUNK)).transpose(1, 0, 2, 3)
    pw_full = piece(3, (4, 64, 256)).transpose(1, 0, 2, 3).reshape(4, 256, 256).astype(BF16)
    cb_full = mlp_conv_b.reshape(2, N_SHARD, 1, FF_CHUNK)

    loss, dx0, small = _local_step(
        x[0], positions.reshape(t, 1).astype(F32), loss_target[0], gains, pw_full, pool_scale, gn_full,
        cw_full, cb_full, weights, send_grads)

    def small_adamw(w, m, v, grads, name):
        w3 = w.reshape(1, -1, w.shape[-1])
        out = adamw(w3, m.reshape(w3.shape), v.reshape(w3.shape), [g.reshape(w3.shape[1:]) for g in grads], 0, None, name)
        return [o.reshape(w.shape) for o in out]

    pw_f = small["pool_w"].reshape(4, N_SHARD, 64, 256).transpose(1, 0, 2, 3).reshape(N_SHARD, 256, 256)
    small_order = ["gains", "pool_scale", "gn", "conv_w", "conv_b"]
    gsmall_sizes = [small[k].size for k in small_order]
    gsmall_rows = -(-sum(gsmall_sizes) // (128 * 8)) * 8
    gpack = _pack_rows([small[k] for k in small_order], gsmall_rows)
    mine = 2 * me + lax.axis_index("c")
    small_started = exchange_start(
        "everyone", [gpack], [lax.dynamic_update_index_in_dim(lax.empty((8,) + gpack.shape, F32), gpack, mine, 0)],
        "small_start")
    send_grads("pool_w", [(pw_f, pw_f.astype(BF16))])

    res = {}
    planes_a, others_a = exchange_wait("swap", early["swap"], small_started[-1], "swap_wait_a")
    res["ret_w_in"] = adamw(ret_w_in, m_ret_w_in, v_ret_w_in, (planes_a[2], others_a[2]), 0, None, "adamw_w_in")
    res["ret_w_out"] = adamw(ret_w_out, m_ret_w_out, v_ret_w_out, (planes_a[3], others_a[3]), 0, None, "adamw_w_out")
    up1 = adamw(mlp_w_up, m_mlp_w_up, v_mlp_w_up, (planes_a[0], others_a[0]), 1, None, "adamw_w_up_1")
    dn1 = adamw(mlp_w_down, m_mlp_w_down, v_mlp_w_down, (planes_a[1], others_a[1]), 1, None, "adamw_w_down_1")

    planes_b = (reduced("up0a", dn1[0], ["w_up_0a"]) + reduced("up0b", dn1[0], ["w_up_0b"])
                + reduced("down0", dn1[0], ["w_down_0"]) + reduced("pool_w", dn1[0], ["pool_w"]))
    swap_b = swap_start(planes_b, "swap_start_b")

    _, (small_recv,) = exchange_wait("everyone", small_started, swap_b[-1], "small_wait")
    gsmall = sum_slots(small_recv, "sum_small").reshape(-1)
    goffs = np.cumsum([0] + gsmall_sizes)
    gpiece = lambda i: gsmall[goffs[i]:goffs[i + 1]].reshape(small[small_order[i]].shape)
    g_gains = lax.dynamic_slice_in_dim(gpiece(0), me * 256, 256, axis=2)
    g_gn = lax.dynamic_slice_in_dim(gpiece(2), me * RET_V, RET_V, axis=1)
    g_cw = lax.dynamic_index_in_dim(gpiece(3), me, 1, keepdims=False)
    res["norm_gain"] = small_adamw(norm_gain, m_norm_gain, v_norm_gain, [g_gains], "adamw_norm_gain")
    res["pool_scale"] = small_adamw(pool_scale, m_pool_scale, v_pool_scale, [gpiece(1)], "adamw_pool_scale")
    res["ret_gn_gain"] = small_adamw(ret_gn_gain, m_ret_gn_gain, v_ret_gn_gain, [g_gn], "adamw_gn_gain")
    res["mlp_conv_w"] = small_adamw(mlp_conv_w, m_mlp_conv_w, v_mlp_conv_w, [g_cw], "adamw_conv_w")
    res["mlp_conv_b"] = small_adamw(mlp_conv_b, m_mlp_conv_b, v_mlp_conv_b, [gpiece(4)], "adamw_conv_b")

    planes_b, others_b = exchange_wait("swap", swap_b, res["mlp_conv_b"][0], "swap_wait_b")
    up0a = adamw(mlp_w_up, m_mlp_w_up, v_mlp_w_up, (planes_b[0], others_b[0]), 0, up1, "adamw_w_up_0a")
    res["mlp_w_up"] = adamw(mlp_w_up, m_mlp_w_up, v_mlp_w_up, (planes_b[1], others_b[1]), 0, up0a, "adamw_w_up_0b",
                            row0=D_MODEL // 2)
    res["mlp_w_down"] = adamw(mlp_w_down, m_mlp_w_down, v_mlp_w_down, (planes_b[2], others_b[2]), 0, dn1,
                              "adamw_w_down_0")
    res["pool_w"] = small_adamw(pool_w, m_pool_w, v_pool_w, (planes_b[3], others_b[3]), "adamw_pool_w")

    order = ["norm_gain", "pool_w", "pool_scale", "ret_w_in", "ret_gn_gain", "ret_w_out", "mlp_w_up", "mlp_conv_w",
             "mlp_conv_b", "mlp_w_down"]
    total_loss = lax.psum(loss[0, 0], ("x", "y", "c"))
    outs = [total_loss, dx0.reshape(x.shape)]
    for q in range(4):
        outs += [res[k][q] for k in order]
    return tuple(outs)
```

```python
import numpy as np
import jax
import jax.numpy as jnp
from jax import lax
from jax.experimental import pallas as pl
from jax.experimental.pallas import tpu as pltpu

F32 = jnp.float32
BF16 = jnp.bfloat16

D_MODEL = 1024
D_FF = 2816
FF_CHUNK = 1408
N_SHARD = 4
POOL_WINDOWS = (2, 4, 8, 16)
POOL_DIM = 256
POOL_HALO = 16
RET_HEADS = 4
RET_QK = 256
RET_V = 512
RET_CHUNK = 256
RET_IN = 6144
RET_IN_SHARD = 1536
ROPE_BASE = 10000.0
EPS = 1e-6
CONV_HALO = 8

ADAM_LR, ADAM_B1, ADAM_B2, ADAM_EPS, ADAM_WD, ADAM_STEP = 0.001, 0.9, 0.999, 1e-08, 0.01, 10

VMEM_LIMIT = 56 * 1024 * 1024
VMEM_LIMIT_MLP_BWD = 62 * 1024 * 1024
MESH = pl.DeviceIdType.MESH
ANY = pl.BlockSpec(memory_space=pl.ANY)


def _params(n_grid=1, limit=VMEM_LIMIT):
    return pltpu.CompilerParams(dimension_semantics=("arbitrary",) * n_grid, vmem_limit_bytes=limit)


def _dot(a, b):
    return jnp.dot(a, b, preferred_element_type=F32)


def _dot_nt(a, b):
    return lax.dot_general(a, b, (((1,), (1,)), ((), ())), preferred_element_type=F32)


def _dot_tn(a, b):
    return lax.dot_general(a, b, (((0,), (0,)), ((), ())), preferred_element_type=F32)


def _rms_fwd(x, gain):
    r = lax.rsqrt(jnp.mean(x * x, axis=-1, keepdims=True) + EPS)
    xh = x * r
    return xh * gain, xh, r


def _rms_bwd(xh, r, gain, dy):
    dxh = dy * gain
    return r * (dxh - xh * jnp.mean(dxh * xh, axis=-1, keepdims=True))


def _colsum(v):
    return jnp.sum(v, axis=0, keepdims=True)


def _full(shape):
    nd = len(shape)
    return pl.BlockSpec(shape, lambda *_: (0,) * nd)


def rope_tables(pos_col, inv_freq, after):
    t = pos_col.shape[0]
    tm = min(t, 1024)

    def body(p_ref, f_ref, after_ref, c_ref, s_ref):
        ang = p_ref[...] * f_ref[...]
        c_ref[...] = jnp.cos(ang)
        s_ref[...] = jnp.sin(ang)

    return pl.pallas_call(
        body, grid=(t // tm,),
        in_specs=[pl.BlockSpec((tm, 1), lambda i: (i, 0)), _full((1, 128)), ANY],
        out_specs=[pl.BlockSpec((tm, 128), lambda i: (i, 0))] * 2,
        out_shape=[jax.ShapeDtypeStruct((t, 128), F32)] * 2,
        compiler_params=_params(1), name="rope_tables")(pos_col, inv_freq, after)


def _window_sums(ext, backward):
    n = ext.shape[0]
    cur, sums = ext, []
    for g, win in enumerate(POOL_WINDOWS):
        if g > 0:
            cur = cur[:, POOL_DIM:]
        half = win // 2
        cur = cur + pltpu.roll(cur, n - half if backward else half, axis=0)
        sums.append(cur[:, 0:POOL_DIM])
    return sums


def _pool_diff(h_halo, h, row0, tm):
    t_idx = row0 + lax.broadcasted_iota(jnp.int32, (tm, 1), 0)
    sums = _window_sums(jnp.concatenate([h_halo, h], axis=0), backward=False)
    parts, inv_counts = [], []
    for g, win in enumerate(POOL_WINDOWS):
        inv = 1.0 / jnp.minimum(t_idx + 1, win).astype(F32)
        parts.append(sums[g][POOL_HALO:, :] * inv - h[:, g * POOL_DIM:(g + 1) * POOL_DIM])
        inv_counts.append(inv)
    return parts, inv_counts


def pool_fwd(x, g_pre, g_post, pool_w, pool_scale, tm=512):
    t, d = x.shape
    nt = t // tm

    def body(x_ref, g0_ref, g1_ref, w_ref, sc_ref, o_ref, hext):
        i = pl.program_id(0)

        @pl.when(i == 0)
        def _():
            hext[...] = jnp.zeros((POOL_HALO, d), F32)

        xv = x_ref[...]
        h, _, _ = _rms_fwd(xv, g0_ref[...])
        parts, _ = _pool_diff(hext[...], h, i * tm, tm)
        hext[...] = h[tm - POOL_HALO:tm, :]
        ys = [_dot(parts[g].astype(BF16), w_ref[g]) for g in range(len(POOL_WINDOWS))]
        y = jnp.concatenate(ys, axis=-1) * sc_ref[...]
        m, _, _ = _rms_fwd(y, g1_ref[...])
        o_ref[...] = xv + m

    row = pl.BlockSpec((tm, d), lambda i: (i, 0))
    return pl.pallas_call(
        body, grid=(nt,),
        in_specs=[row, _full((1, d)), _full((1, d)), _full(pool_w.shape), _full((1, d))],
        out_specs=row, out_shape=jax.ShapeDtypeStruct((t, d), F32),
        scratch_shapes=[pltpu.VMEM((POOL_HALO, d), F32)],
        compiler_params=_params(1), name="pool_fwd")(x, g_pre, g_post, pool_w, pool_scale)


def pool_bwd(dx1, x, g_pre, g_post, pool_w, pool_scale, tm=512):
    t, d = x.shape
    nt = t // tm
    ng = len(POOL_WINDOWS)

    def body(dx1_ref, x_ref, xh_ref, g0_ref, g1_ref, w_ref, sc_ref,
             dx_ref, dg0_ref, dg1_ref, dsc_ref, dw_ref, enext):
        i = pl.program_id(0)
        r = nt - 1 - i

        @pl.when(i == 0)
        def _():
            enext[...] = jnp.zeros((POOL_HALO, d), F32)
            dg0_ref[...] = jnp.zeros_like(dg0_ref)
            dg1_ref[...] = jnp.zeros_like(dg1_ref)
            dsc_ref[...] = jnp.zeros_like(dsc_ref)
            dw_ref[...] = jnp.zeros_like(dw_ref)

        g0 = g0_ref[...]
        g1 = g1_ref[...]
        sc = sc_ref[...]
        xv = x_ref[...]
        h, xh, rx = _rms_fwd(xv, g0)
        h_halo, _, _ = _rms_fwd(xh_ref[...], g0)
        parts, inv_counts = _pool_diff(h_halo * jnp.where(r > 0, 1.0, 0.0), h, r * tm, tm)
        parts_b = [p.astype(BF16) for p in parts]
        ypre = jnp.concatenate([_dot(parts_b[g], w_ref[g]) for g in range(ng)], axis=-1)
        _, yh, ry = _rms_fwd(ypre * sc, g1)
        dm = dx1_ref[...]
        dg1_ref[...] += _colsum(dm * yh)
        dy = _rms_bwd(yh, ry, g1, dm)
        dsc_ref[...] += _colsum(dy * ypre)
        dyp = (dy * sc).astype(BF16)
        ddiffs = []
        for g in range(ng):
            cols = slice(g * POOL_DIM, (g + 1) * POOL_DIM)
            dw_ref[g] += _dot_tn(parts_b[g], dyp[:, cols])
            ddiffs.append(_dot_nt(dyp[:, cols], w_ref[g]))
        e = jnp.concatenate([ddiffs[g] * inv_counts[g] for g in range(ng)], axis=-1)
        sums = _window_sums(jnp.concatenate([e, enext[...]], axis=0), backward=True)
        enext[...] = e[0:POOL_HALO, :]
        dh = jnp.concatenate([sums[g][0:tm, :] - ddiffs[g] for g in range(ng)], axis=-1)
        dg0_ref[...] += _colsum(dh * xh)
        dx_ref[...] = dm + _rms_bwd(xh, rx, g0, dh)

    row = pl.BlockSpec((tm, d), lambda i: (nt - 1 - i, 0))
    halo = pl.BlockSpec((POOL_HALO, d), lambda i: (jnp.maximum((nt - 1 - i) * (tm // POOL_HALO) - 1, 0), 0))
    vec = _full((1, d))
    return pl.pallas_call(
        body, grid=(nt,),
        in_specs=[row, row, halo, vec, vec, _full(pool_w.shape), vec],
        out_specs=[row, vec, vec, vec, _full((ng, POOL_DIM, POOL_DIM))],
        out_shape=[jax.ShapeDtypeStruct((t, d), F32)] + [jax.ShapeDtypeStruct((1, d), F32)] * 3
        + [jax.ShapeDtypeStruct((ng, POOL_DIM, POOL_DIM), F32)],
        scratch_shapes=[pltpu.VMEM((POOL_HALO, d), F32)],
        compiler_params=_params(1), name="pool_bwd")(dx1, x, x, g_pre, g_post, pool_w, pool_scale)


def _conv_taps(cw_ref, j):
    return cw_ref[j, 0:1, :], cw_ref[j, 1:2, :], cw_ref[j, 2:3, :]


def _row_block(m, target=256):
    if m <= target:
        return m
    for b in range(target, 7, -8):
        if m % b == 0:
            return b
    return m


def mlp_fwd(x, g_pre, g_post, w_up, w_down, conv_w, conv_b, target=None, tm=256):
    t, d = x.shape
    nt = t // tm
    h8 = CONV_HALO
    with_loss = target is not None
    n_extra = 1 if with_loss else 0

    def body(x_ref, g2_ref, g3_ref, wup_hbm, wdn_hbm, cw_ref, cb_ref, *rest):
        tgt_ref = rest[0] if with_loss else None
        xo_ref, u_ref, s_ref, a_ref, f_ref, h_ref = rest[n_extra:n_extra + 6]
        loss_ref = rest[n_extra + 6] if with_loss else None
        wup_v, wdn_v, tail, sem = rest[-4:]
        i = pl.program_id(0)

        @pl.when(i == 0)
        def _():
            c1 = pltpu.make_async_copy(wup_hbm, wup_v, sem.at[0])
            c2 = pltpu.make_async_copy(wdn_hbm, wdn_v, sem.at[1])
            c1.start()
            c2.start()
            tail[...] = jnp.zeros_like(tail)
            if with_loss:
                loss_ref[...] = jnp.zeros_like(loss_ref)
            c1.wait()
            c2.wait()

        xv = x_ref[...]
        h, _, _ = _rms_fwd(xv, g2_ref[...])
        hb = h.astype(BF16)
        h_ref[...] = hb
        acc = jnp.zeros((tm, d), F32)
        for k in range(2):
            cs = []
            for s in range(2):
                j, cols = k + 2 * s, slice((2 * k + s) * FF_CHUNK, (2 * k + s + 1) * FF_CHUNK)
                uf = _dot(hb, wup_v[j])
                u_ref[:, cols] = uf.astype(BF16)
                ext = jnp.concatenate([tail[j], uf], axis=0)
                tail[j] = uf[tm - h8:tm, :]
                w0, w1, w2 = _conv_taps(cw_ref, j)
                cs.append(cb_ref[j] + w2 * uf + w1 * pltpu.roll(ext, 1, axis=0)[h8:, :]
                          + w0 * pltpu.roll(ext, 2, axis=0)[h8:, :])
            cg, cv = cs
            sg = jax.nn.sigmoid(cg)
            sil = cg * sg
            ab = (sil * cv).astype(BF16)
            a_ref[:, k * FF_CHUNK:(k + 1) * FF_CHUNK] = ab
            s_ref[:, 2 * k * FF_CHUNK:(2 * k + 1) * FF_CHUNK] = sil.astype(BF16)
            s_ref[:, (2 * k + 1) * FF_CHUNK:(2 * k + 2) * FF_CHUNK] = (cv * (sg * (1.0 + cg * (1.0 - sg)))).astype(BF16)
            acc = acc + _dot(ab, wdn_v[k * FF_CHUNK:(k + 1) * FF_CHUNK, :])
        f_ref[...] = acc
        y, _, _ = _rms_fwd(acc, g3_ref[...])
        if with_loss:
            err = (xv + y) - tgt_ref[...]
            xo_ref[...] = err * (1.0 / d)
            loss_ref[...] += 0.5 * jnp.sum(jnp.mean(err * err, axis=-1, keepdims=True), axis=0, keepdims=True)
        else:
            xo_ref[...] = xv + y

    row = pl.BlockSpec((tm, d), lambda i: (i, 0))
    wide = pl.BlockSpec((tm, 2 * D_FF), lambda i: (i, 0))
    vec = _full((1, d))
    extra = [target] if with_loss else []
    return pl.pallas_call(
        body, grid=(nt,),
        in_specs=[row, vec, vec, ANY, ANY, _full(conv_w.shape), _full(conv_b.shape)] + [row] * n_extra,
        out_specs=[row, wide, wide, pl.BlockSpec((tm, D_FF), lambda i: (i, 0)), row, row] + [_full((1, 1))] * n_extra,
        out_shape=[jax.ShapeDtypeStruct((t, d), F32), jax.ShapeDtypeStruct((t, 2 * D_FF), BF16),
                   jax.ShapeDtypeStruct((t, 2 * D_FF), BF16), jax.ShapeDtypeStruct((t, D_FF), BF16),
                   jax.ShapeDtypeStruct((t, d), F32), jax.ShapeDtypeStruct((t, d), BF16)]
        + [jax.ShapeDtypeStruct((1, 1), F32)] * n_extra,
        scratch_shapes=[pltpu.VMEM(w_up.shape, BF16), pltpu.VMEM(w_down.shape, BF16),
                        pltpu.VMEM((N_SHARD, h8, FF_CHUNK), F32), pltpu.SemaphoreType.DMA((2,))],
        compiler_params=_params(1), name="mlp_fwd_loss" if with_loss else "mlp_fwd")(
            x, g_pre, g_post, w_up, w_down, conv_w, conv_b, *extra)


def _rowsum8(v):
    return jnp.sum(v.reshape(v.shape[0] // 8, 8, v.shape[1]), axis=0)


def mlp_bwd(dxo, f, x, u, sp, g_pre, g_post, w_up, w_down, conv_w, tm=256):
    t, d = x.shape
    nt = t // tm
    h8 = CONV_HALO

    def body(dxo_ref, f_ref, x_ref, u_ref, s_ref, g2_ref, g3_ref, wup_hbm, wdn_hbm, cw_ref,
             dx_ref, du_ref, df_ref, dg2_ref, dg3_ref, dcw_ref, dcb_ref,
             wup_v, wdn_v, carry, sem):
        @pl.when(pl.program_id(0) == 0)
        def _():
            c1 = pltpu.make_async_copy(wup_hbm, wup_v, sem.at[0])
            c2 = pltpu.make_async_copy(wdn_hbm, wdn_v, sem.at[1])
            c1.start()
            c2.start()
            carry[...] = jnp.zeros_like(carry)
            dg2_ref[...] = jnp.zeros_like(dg2_ref)
            dg3_ref[...] = jnp.zeros_like(dg3_ref)
            dcw_ref[...] = jnp.zeros_like(dcw_ref)
            dcb_ref[...] = jnp.zeros_like(dcb_ref)
            c1.wait()
            c2.wait()

        g3 = g3_ref[...]
        dxo = dxo_ref[...]
        _, fh, rf = _rms_fwd(f_ref[...], g3)
        dg3_ref[...] += _rowsum8(dxo * fh)
        dfb = _rms_bwd(fh, rf, g3, dxo).astype(BF16)
        df_ref[...] = dfb
        dh = jnp.zeros((tm, d), F32)
        for k in range(2):
            da = _dot_nt(dfb, wdn_v[k * FF_CHUNK:(k + 1) * FF_CHUNK, :])
            for s in range(2):
                j = k + 2 * s
                cols = slice((2 * k + s) * FF_CHUNK, (2 * k + s + 1) * FF_CHUNK)
                dc = da * s_ref[:, (2 * k + 1 - s) * FF_CHUNK:(2 * k + 2 - s) * FF_CHUNK].astype(F32)
                uf = u_ref[:, cols].astype(F32)
                ext = jnp.concatenate([dc, carry[j]], axis=0)
                carry[j] = dc[0:h8, :]
                dc1 = pltpu.roll(ext, tm + h8 - 1, axis=0)[0:tm, :]
                dc2 = pltpu.roll(ext, tm + h8 - 2, axis=0)[0:tm, :]
                dcb_ref[j] += _rowsum8(dc)
                dcw_ref[j, 2] += _rowsum8(dc * uf)
                dcw_ref[j, 1] += _rowsum8(dc1 * uf)
                dcw_ref[j, 0] += _rowsum8(dc2 * uf)
                dub = (cw_ref[j, 2:3, :] * dc + cw_ref[j, 1:2, :] * dc1 + cw_ref[j, 0:1, :] * dc2).astype(BF16)
                du_ref[:, cols] = dub
                dh = dh + _dot_nt(dub, wup_v[j])
        g2 = g2_ref[...]
        _, xh, rx = _rms_fwd(x_ref[...], g2)
        dg2_ref[...] += _rowsum8(dh * xh)
        dx_ref[...] = dxo + _rms_bwd(xh, rx, g2, dh)

    row = pl.BlockSpec((tm, d), lambda i: (nt - 1 - i, 0))
    wide = pl.BlockSpec((tm, 2 * D_FF), lambda i: (nt - 1 - i, 0))
    vec = _full((1, d))
    acc = _full((8, d))
    dcw_shape, dcb_shape = (N_SHARD, 3, 8, FF_CHUNK), (N_SHARD, 8, FF_CHUNK)
    return pl.pallas_call(
        body, grid=(nt,),
        in_specs=[row, row, row, wide, wide, vec, vec, ANY, ANY, _full(conv_w.shape)],
        out_specs=[row, wide, row, acc, acc, _full(dcw_shape), _full(dcb_shape)],
        out_shape=[jax.ShapeDtypeStruct((t, d), F32), jax.ShapeDtypeStruct((t, 2 * D_FF), BF16),
                   jax.ShapeDtypeStruct((t, d), BF16),
                   jax.ShapeDtypeStruct((8, d), F32), jax.ShapeDtypeStruct((8, d), F32),
                   jax.ShapeDtypeStruct(dcw_shape, F32), jax.ShapeDtypeStruct(dcb_shape, F32)],
        scratch_shapes=[pltpu.VMEM(w_up.shape, BF16), pltpu.VMEM(w_down.shape, BF16),
                        pltpu.VMEM((N_SHARD, h8, FF_CHUNK), F32), pltpu.SemaphoreType.DMA((2,))],
        compiler_params=_params(1, VMEM_LIMIT_MLP_BWD), name="mlp_bwd")(
            dxo, f, x, u, sp, g_pre, g_post, w_up, w_down, conv_w)


def grad_matmul(a, b, bm, bn, name, tk=2048, interleaved=False, after=None, cols=None, riders=()):
    t = a.shape[0]
    m0, m = (0, a.shape[1]) if cols is None else cols
    n = b.shape[1]
    tk = min(tk, t)
    nk = t // tk
    ni = m // bm
    steps = (n // bn) * ni * nk
    place = (lambda j: (j % 2) * 2 + j // 2) if interleaved else (lambda j: j)
    step = lambda j, i, kk: (j * ni + i) * nk + kk
    extra = [] if after is None else [after]
    first = m0 // bm

    in_specs = [pl.BlockSpec((tk, bm), lambda j, i, kk: (kk, first + i)),
                pl.BlockSpec((tk, bn), lambda j, i, kk: (kk, j))] + [ANY] * len(extra)
    args = [a, b, *extra]
    ospec = pl.BlockSpec((None, bm, bn), lambda j, i, kk: (place(j), i, 0))
    out_specs = [ospec, ospec]
    out_shape = [jax.ShapeDtypeStruct((n // bn, m, bn), F32), jax.ShapeDtypeStruct((n // bn, m, bn), BF16)]
    aliases, counts = {}, []
    for w, mom, var, grads, layer, prev in riders:
        rows, ncol = grads[0].shape
        r = rows // steps
        assert r * steps == rows and r % 8 == 0, (name, rows, steps)
        slab = pl.BlockSpec((None, r, ncol), lambda j, i, kk, layer=layer: (layer, step(j, i, kk), 0))
        flat = pl.BlockSpec((r, ncol), lambda j, i, kk: (step(j, i, kk), 0))
        in_specs += [slab] * 3 + [flat] * len(grads)
        args += [w, mom, var, *grads]
        if prev is not None:
            for q in range(4):
                aliases[len(args) + q] = len(out_shape) + q
            in_specs += [ANY] * 4
            args += list(prev)
        counts.append((3 + len(grads) + (4 if prev is not None else 0), len(grads)))
        out_specs += [slab] * 4
        out_shape += [jax.ShapeDtypeStruct(w.shape, F32)] * 4

    def body(a_ref, b_ref, *rest):
        pos = len(extra)
        rider_ins = []
        for cnt, _ in counts:
            rider_ins.append(rest[pos:pos + cnt])
            pos += cnt
        o_ref, ob_ref = rest[pos:pos + 2]
        rider_outs = [rest[pos + 2 + 4 * q:pos + 6 + 4 * q] for q in range(len(counts))]
        kk = pl.program_id(2)

        @pl.when(kk == 0)
        def _():
            o_ref[...] = jnp.zeros_like(o_ref)

        o_ref[...] += _dot_tn(a_ref[...], b_ref[...])
        for (_, ng), ins, outs in zip(counts, rider_ins, rider_outs):
            g = ins[3][...]
            for gr in ins[4:3 + ng]:
                g = g + gr[...]
            delta, mn, vn = _adamw_math(ins[0][...], g, ins[1][...], ins[2][...])
            outs[0][...] = g
            outs[1][...] = delta
            outs[2][...] = mn
            outs[3][...] = vn

        @pl.when(kk == nk - 1)
        def _():
            ob_ref[...] = o_ref[...].astype(BF16)

    outs = pl.pallas_call(
        body, grid=(n // bn, ni, nk), in_specs=in_specs, out_specs=out_specs, out_shape=out_shape,
        input_output_aliases=aliases, compiler_params=_params(3), name=name)(*args)
    return (outs[0], outs[1]) if not riders else (outs[0], outs[1], [outs[2 + 4 * q:6 + 4 * q] for q in range(len(riders))])


def _decay_tables():
    log_gamma = jnp.log(1.0 - 2.0 ** (-5.0 - jnp.arange(RET_HEADS, dtype=F32)))
    i = jnp.arange(RET_CHUNK, dtype=F32)
    rel = i[:, None] - i[None, :]
    intra = jnp.where(rel >= 0, jnp.exp(jnp.maximum(rel, 0.0) * log_gamma[:, None, None]), 0.0)
    cross = jnp.exp((i + 1.0) * log_gamma[:, None])[:, :, None]
    inner = jnp.exp((RET_CHUNK - 1.0 - i) * log_gamma[:, None])[:, :, None]
    chunk = [float(np.exp(np.float32(RET_CHUNK) * np.log(np.float32(1.0 - 2.0 ** (-5.0 - h))).astype(np.float32)))
             for h in range(RET_HEADS)]
    return intra, cross, inner, chunk


def ret_proj(x, g_pre, w_in, cos, sin, tm=512):
    t, d = x.shape
    nt = t // tm
    per = RET_IN_SHARD // RET_QK

    def body(x_ref, g_ref, win_hbm, c_ref, s_ref, pj_ref, h_ref, win_v, sem):
        @pl.when(pl.program_id(0) == 0)
        def _():
            cp = pltpu.make_async_copy(win_hbm, win_v, sem)
            cp.start()
            cp.wait()

        h, _, _ = _rms_fwd(x_ref[...], g_ref[...])
        hb = h.astype(BF16)
        h_ref[...] = hb
        c = c_ref[...]
        s = s_ref[...]
        for j in range(N_SHARD):
            pjj = _dot(hb, win_v[j])
            for bb in range(per):
                b = per * j + bb
                blk = pjj[:, bb * RET_QK:(bb + 1) * RET_QK]
                if b < 2 * RET_HEADS:
                    x1, x2 = blk[:, :128], blk[:, 128:]
                    o1 = x1 * c - x2 * s
                    o2 = x2 * c + x1 * s
                    if b < RET_HEADS:
                        o1 = o1 * (RET_QK ** -0.5)
                        o2 = o2 * (RET_QK ** -0.5)
                    pj_ref[:, b * RET_QK:b * RET_QK + 128] = o1.astype(BF16)
                    pj_ref[:, b * RET_QK + 128:(b + 1) * RET_QK] = o2.astype(BF16)
                else:
                    pj_ref[:, b * RET_QK:(b + 1) * RET_QK] = blk.astype(BF16)

    row = pl.BlockSpec((tm, d), lambda i: (i, 0))
    tab = pl.BlockSpec((tm, 128), lambda i: (i, 0))
    return pl.pallas_call(
        body, grid=(nt,),
        in_specs=[row, _full((1, d)), ANY, tab, tab],
        out_specs=[pl.BlockSpec((tm, RET_IN), lambda i: (i, 0)), row],
        out_shape=[jax.ShapeDtypeStruct((t, RET_IN), BF16), jax.ShapeDtypeStruct((t, d), BF16)],
        scratch_shapes=[pltpu.VMEM(w_in.shape, BF16), pltpu.SemaphoreType.DMA],
        compiler_params=_params(1), name="ret_proj")(x, g_pre, w_in, cos, sin)


def ret_core_fwd(pj, intra, cross, inner, chunk_decay):
    t = pj.shape[0]
    nc = t // RET_CHUNK
    c = RET_CHUNK
    qk_all = RET_HEADS * RET_QK
    v_all = RET_HEADS * RET_V

    def body(q_ref, k_ref, v_ref, dm_ref, cr_ref, in_ref, o_ref, sp_ref, state):
        @pl.when(pl.program_id(0) == 0)
        def _():
            state[...] = jnp.zeros_like(state)

        for h in range(RET_HEADS):
            q = q_ref[:, h * RET_QK:(h + 1) * RET_QK]
            k = k_ref[:, h * RET_QK:(h + 1) * RET_QK]
            v = v_ref[:, h * RET_V:(h + 1) * RET_V]
            sb = state[h].astype(BF16)
            sp_ref[h] = sb
            sc = _dot_nt(q, k) * dm_ref[h]
            o_ref[:, h * RET_V:(h + 1) * RET_V] = _dot(sc.astype(BF16), v) + _dot(q, sb) * cr_ref[h]
            kd = (k.astype(F32) * in_ref[h]).astype(BF16)
            state[h] = state[h] * chunk_decay[h] + _dot_tn(kd, v)

    return pl.pallas_call(
        body, grid=(nc,),
        in_specs=[pl.BlockSpec((c, qk_all), lambda n: (n, 0)), pl.BlockSpec((c, qk_all), lambda n: (n, 1)),
                  pl.BlockSpec((c, v_all), lambda n: (n, 1)),
                  _full(intra.shape), _full(cross.shape), _full(inner.shape)],
        out_specs=[pl.BlockSpec((c, v_all), lambda n: (n, 0)),
                   pl.BlockSpec((None, RET_HEADS, RET_QK, RET_V), lambda n: (n, 0, 0, 0))],
        out_shape=[jax.ShapeDtypeStruct((t, v_all), F32),
                   jax.ShapeDtypeStruct((nc, RET_HEADS, RET_QK, RET_V), BF16)],
        scratch_shapes=[pltpu.VMEM((RET_HEADS, RET_QK, RET_V), F32)],
        compiler_params=_params(1), name="ret_core_fwd")(pj, pj, pj, intra, cross, inner)


def _group_norm(o_h):
    mu = jnp.mean(o_h, axis=-1, keepdims=True)
    dev = o_h - mu
    rstd = lax.rsqrt(jnp.mean(dev * dev, axis=-1, keepdims=True) + EPS)
    return dev * rstd, rstd


def ret_out_fwd(o, pj, x, gn_gain, g_post, w_out, tm=512):
    t, d = x.shape
    nt = t // tm
    v_all = RET_HEADS * RET_V

    def body(o_ref, g_ref, x_ref, gn_ref, g1_ref, w_ref, xo_ref, y_ref, out_ref):
        out = jnp.zeros((tm, d), F32)
        for h in range(RET_HEADS):
            cols = slice(h * RET_V, (h + 1) * RET_V)
            ohat, _ = _group_norm(o_ref[:, cols])
            g = g_ref[:, cols].astype(F32)
            yb = (g * jax.nn.sigmoid(g) * (ohat * gn_ref[:, cols])).astype(BF16)
            y_ref[:, cols] = yb
            out = out + _dot(yb, w_ref[cols, :])
        out_ref[...] = out
        m, _, _ = _rms_fwd(out, g1_ref[...])
        xo_ref[...] = x_ref[...] + m

    row = pl.BlockSpec((tm, d), lambda i: (i, 0))
    wide = pl.BlockSpec((tm, v_all), lambda i: (i, 0))
    return pl.pallas_call(
        body, grid=(nt,),
        in_specs=[wide, pl.BlockSpec((tm, v_all), lambda i: (i, 2)), row, _full((1, v_all)), _full((1, d)),
                  _full(w_out.shape)],
        out_specs=[row, wide, row],
        out_shape=[jax.ShapeDtypeStruct((t, d), F32), jax.ShapeDtypeStruct((t, v_all), BF16),
                   jax.ShapeDtypeStruct((t, d), F32)],
        compiler_params=_params(1), name="ret_out_fwd")(o, pj, x, gn_gain, g_post, w_out)


def ret_out_bwd(dxo, out, o, pj, gn_gain, g_post, w_out, tm=256):
    t, d = out.shape
    nt = t // tm
    v_all = RET_HEADS * RET_V

    def body(dxo_ref, out_ref, o_ref, g_ref, gn_ref, g1_ref, w_ref,
             dout_ref, dgate_ref, do_ref, dg1_ref, dgn_ref):
        @pl.when(pl.program_id(0) == 0)
        def _():
            dg1_ref[...] = jnp.zeros_like(dg1_ref)
            dgn_ref[...] = jnp.zeros_like(dgn_ref)

        g1 = g1_ref[...]
        dxo = dxo_ref[...]
        _, oh_, r_ = _rms_fwd(out_ref[...], g1)
        dg1_ref[...] += _colsum(dxo * oh_)
        doutb = _rms_bwd(oh_, r_, g1, dxo).astype(BF16)
        dout_ref[...] = doutb
        for h in range(RET_HEADS):
            cols = slice(h * RET_V, (h + 1) * RET_V)
            gn = gn_ref[:, cols]
            ohat, rstd = _group_norm(o_ref[:, cols])
            g = g_ref[:, cols].astype(F32)
            sg = jax.nn.sigmoid(g)
            dyh = _dot_nt(doutb, w_ref[cols, :])
            dgate_ref[:, cols] = (dyh * (ohat * gn) * (sg * (1.0 + g * (1.0 - sg)))).astype(BF16)
            don = dyh * (g * sg)
            dgn_ref[:, cols] += _colsum(don * ohat)
            dohat = don * gn
            do_ref[:, cols] = (rstd * (dohat - jnp.mean(dohat, axis=-1, keepdims=True)
                                       - ohat * jnp.mean(dohat * ohat, axis=-1, keepdims=True))).astype(BF16)

    row = pl.BlockSpec((tm, d), lambda i: (i, 0))
    wide = pl.BlockSpec((tm, v_all), lambda i: (i, 0))
    gate = pl.BlockSpec((tm, v_all), lambda i: (i, 2))
    return pl.pallas_call(
        body, grid=(nt,),
        in_specs=[row, row, wide, gate, _full((1, v_all)), _full((1, d)), _full(w_out.shape)],
        out_specs=[row, gate, wide, _full((1, d)), _full((1, v_all))],
        out_shape=[jax.ShapeDtypeStruct((t, d), BF16), jax.ShapeDtypeStruct((t, RET_IN), BF16),
                   jax.ShapeDtypeStruct((t, v_all), BF16), jax.ShapeDtypeStruct((1, d), F32),
                   jax.ShapeDtypeStruct((1, v_all), F32)],
        compiler_params=_params(1), name="ret_out_bwd")(dxo, out, o, pj, gn_gain, g_post, w_out)


def ret_core_bwd(pj, do, sprev, cos, sin, dpj, intra, cross, inner, chunk_decay):
    t = pj.shape[0]
    nc = t // RET_CHUNK
    c = RET_CHUNK
    qk_all = RET_HEADS * RET_QK
    v_all = RET_HEADS * RET_V
    scale = RET_QK ** -0.5

    def body(q_ref, k_ref, v_ref, do_ref, sp_ref, c_ref, s_ref, dm_ref, cr_ref, in_ref, dpj_in, dpj_ref, dstate):
        @pl.when(pl.program_id(0) == 0)
        def _():
            dstate[...] = jnp.zeros_like(dstate)

        cs = c_ref[...]
        sn = s_ref[...]
        for h in range(RET_HEADS):
            q = q_ref[:, h * RET_QK:(h + 1) * RET_QK]
            k = k_ref[:, h * RET_QK:(h + 1) * RET_QK]
            v = v_ref[:, h * RET_V:(h + 1) * RET_V]
            doh = do_ref[:, h * RET_V:(h + 1) * RET_V]
            dm = dm_ref[h]
            ab = (_dot_nt(q, k) * dm).astype(BF16)
            dab = (_dot_nt(doh, v) * dm).astype(BF16)
            dsb = dstate[h].astype(BF16)
            kd = (k.astype(F32) * in_ref[h]).astype(BF16)
            dv = _dot_tn(ab, doh) + _dot(kd, dsb)
            dq = _dot(dab, k) + cr_ref[h] * _dot_nt(doh, sp_ref[h])
            dk = _dot_tn(dab, q) + in_ref[h] * _dot_nt(v, dsb)
            qd = (q.astype(F32) * cr_ref[h]).astype(BF16)
            dstate[h] = dstate[h] * chunk_decay[h] + _dot_tn(qd, doh)
            for base, dd, sc in ((h * RET_QK, dq, scale), (qk_all + h * RET_QK, dk, 1.0)):
                d1, d2 = dd[:, :128], dd[:, 128:]
                dpj_ref[:, base:base + 128] = ((d1 * cs + d2 * sn) * sc).astype(BF16)
                dpj_ref[:, base + 128:base + RET_QK] = ((d2 * cs - d1 * sn) * sc).astype(BF16)
            dpj_ref[:, 2 * qk_all + h * RET_V:2 * qk_all + (h + 1) * RET_V] = dv.astype(BF16)

    rev = lambda n: nc - 1 - n
    tab = pl.BlockSpec((c, 128), lambda n: (rev(n), 0))
    return pl.pallas_call(
        body, grid=(nc,),
        in_specs=[pl.BlockSpec((c, qk_all), lambda n: (rev(n), 0)), pl.BlockSpec((c, qk_all), lambda n: (rev(n), 1)),
                  pl.BlockSpec((c, v_all), lambda n: (rev(n), 1)), pl.BlockSpec((c, v_all), lambda n: (rev(n), 0)),
                  pl.BlockSpec((None, RET_HEADS, RET_QK, RET_V), lambda n: (rev(n), 0, 0, 0)),
                  tab, tab, _full(intra.shape), _full(cross.shape), _full(inner.shape), ANY],
        out_specs=pl.BlockSpec((c, 2 * qk_all + v_all), lambda n: (rev(n), 0)),
        out_shape=jax.ShapeDtypeStruct((t, RET_IN), BF16),
        scratch_shapes=[pltpu.VMEM((RET_HEADS, RET_QK, RET_V), F32)],
        input_output_aliases={10: 0},
        compiler_params=_params(1), name="ret_core_bwd")(pj, pj, pj, do, sprev, cos, sin, intra, cross, inner, dpj)


def ret_in_bwd(dpj, dres, x, g_pre, w_in, tm=512):
    t, d = x.shape
    nt = t // tm

    def body(dpj_ref, dres_ref, x_ref, g_ref, win_hbm, dx_ref, dg_ref, win_v, sem):
        @pl.when(pl.program_id(0) == 0)
        def _():
            cp = pltpu.make_async_copy(win_hbm, win_v, sem)
            cp.start()
            dg_ref[...] = jnp.zeros_like(dg_ref)
            cp.wait()

        g = g_ref[...]
        dh = jnp.zeros((tm, d), F32)
        for j in range(N_SHARD):
            dh = dh + _dot_nt(dpj_ref[:, j * RET_IN_SHARD:(j + 1) * RET_IN_SHARD], win_v[j])
        _, xh, rx = _rms_fwd(x_ref[...], g)
        dg_ref[...] += _colsum(dh * xh)
        dx_ref[...] = dres_ref[...] + _rms_bwd(xh, rx, g, dh)

    row = pl.BlockSpec((tm, d), lambda i: (i, 0))
    return pl.pallas_call(
        body, grid=(nt,),
        in_specs=[pl.BlockSpec((tm, RET_IN), lambda i: (i, 0)), row, row, _full((1, d)), ANY],
        out_specs=[row, _full((1, d))],
        out_shape=[jax.ShapeDtypeStruct((t, d), F32), jax.ShapeDtypeStruct((1, d), F32)],
        scratch_shapes=[pltpu.VMEM(w_in.shape, BF16), pltpu.SemaphoreType.DMA],
        compiler_params=_params(1), name="ret_in_bwd")(dpj, dres, x, g_pre, w_in)


_CHIP_FLIPS = ((1, 0), (0, 1), (1, 1))


def _flip(v, b):
    return 1 - v if b else v


_HBM = pl.BlockSpec(memory_space=pltpu.HBM)
_SEM = pl.BlockSpec(memory_space=pltpu.SEMAPHORE)
_EFFECT = pltpu.SideEffectType.DATAFLOW_SIDE_EFFECTING


def _chip_copies(mode, srcs, lands, send_sems, recv_sems):
    x, y, c = lax.axis_index("x"), lax.axis_index("y"), lax.axis_index("c")
    copies = []
    for t in range(len(lands)):
        if mode == "swap":
            copies.append(pltpu.make_async_remote_copy(
                src_ref=srcs[t], dst_ref=lands[t], send_sem=send_sems.at[t], recv_sem=recv_sems.at[t],
                device_id=(x, y, 1 - c), device_id_type=MESH))
            continue
        if mode == "everyone":
            for m in range(1, 8):
                bx, by, bc = (m >> 2) & 1, (m >> 1) & 1, m & 1
                copies.append(pltpu.make_async_remote_copy(
                    src_ref=srcs[t], dst_ref=lands[t].at[4 * x + 2 * y + c], send_sem=send_sems.at[7 * t + m - 1],
                    recv_sem=recv_sems.at[7 * t + m - 1], device_id=(_flip(x, bx), _flip(y, by), _flip(c, bc)),
                    device_id_type=MESH))
            continue
        for k, (bx, by) in enumerate(_CHIP_FLIPS):
            px, py = _flip(x, bx), _flip(y, by)
            target = (px, py, c)
            if mode == "gather":
                src, dst = srcs[t], lands[t].at[2 * x + y]
            elif mode == "gather_half":
                half = pl.ds(c * (srcs[t].shape[0] // 2), srcs[t].shape[0] // 2)
                src, dst = srcs[t].at[half], lands[t].at[2 * x + y, half]
            elif mode == "forward_half":
                half = pl.ds(c * (lands[t].shape[1] // 2), lands[t].shape[1] // 2)
                src = dst = lands[t].at[2 * px + py, half]
                target = (x, y, 1 - c)
            else:
                src, dst = srcs[t].at[2 * px + py], lands[t].at[k]
            copies.append(pltpu.make_async_remote_copy(
                src_ref=src, dst_ref=dst, send_sem=send_sems.at[3 * t + k], recv_sem=recv_sems.at[3 * t + k],
                device_id=target, device_id_type=MESH))
    return copies


def exchange_start(mode, srcs, lands, name, after=None):
    n, ns = len(lands), len(srcs)
    extra = [] if after is None else [after]

    def body(*refs):
        ins, lnd = refs[:ns], refs[ns:ns + n]
        send_sems, recv_sems = refs[ns + n + len(extra)], refs[ns + n + len(extra) + 1]
        token = refs[-1]
        for cp in _chip_copies(mode, ins, lnd, send_sems, recv_sems):
            cp.start()
        token[...] = jnp.zeros(token.shape, token.dtype)

    hbm = lambda a: pltpu.with_memory_space_constraint(a, pltpu.HBM)
    passed = list(srcs) + list(lands)
    n_sem = {"swap": 1, "everyone": 7}.get(mode, 3) * n
    return pl.pallas_call(
        body, name=name,
        out_shape=(pltpu.SemaphoreType.DMA((n_sem,)), pltpu.SemaphoreType.DMA((n_sem,)),
                   *[pltpu.HBM(a.shape, a.dtype) for a in passed], jax.ShapeDtypeStruct((8, 128), F32)),
        in_specs=[_HBM] * (ns + n) + [ANY] * len(extra),
        out_specs=(_SEM, _SEM, *[_HBM] * (ns + n), pl.BlockSpec(memory_space=pltpu.VMEM)),
        input_output_aliases={i: 2 + i for i in range(ns + n)},
        compiler_params=pltpu.CompilerParams(has_side_effects=_EFFECT))(*[hbm(a) for a in passed], *extra)


def exchange_wait(mode, started, after, name):
    send_sems, recv_sems = started[0], started[1]
    passed = list(started[2:-1])
    n = len(passed) if mode == "forward_half" else len(passed) // 2
    ns = len(passed) - n

    def body(*refs):
        ins, lnd = refs[:ns], refs[ns:ns + n]
        for cp in _chip_copies(mode, ins, lnd, refs[ns + n], refs[ns + n + 1]):
            cp.wait_send()
            cp.wait_recv()

    outs = pl.pallas_call(
        body, name=name, out_shape=tuple(pltpu.HBM(a.shape, a.dtype) for a in passed),
        in_specs=[_HBM] * (ns + n) + [_SEM, _SEM, ANY], out_specs=tuple([_HBM] * (ns + n)),
        input_output_aliases={i: i for i in range(ns + n)},
        compiler_params=pltpu.CompilerParams(has_side_effects=_EFFECT))(*passed, send_sems, recv_sems, after)
    return list(outs[:ns]), list(outs[ns:])


def plane_sum(slot, full, recv, name, bm=256):
    _, m, n = full.shape
    bm = _row_block(m, bm)

    def body(slot_ref, o_ref, r_ref, s_ref):
        s_ref[...] = ((o_ref[...] + r_ref[0].astype(F32)) + r_ref[1].astype(F32)) + r_ref[2].astype(F32)

    return pl.pallas_call(
        body,
        grid_spec=pltpu.PrefetchScalarGridSpec(
            num_scalar_prefetch=1, grid=(m // bm,),
            in_specs=[pl.BlockSpec((None, bm, n), lambda i, s: (s[0], i, 0)),
                      pl.BlockSpec((3, bm, n), lambda i, s: (0, i, 0))],
            out_specs=pl.BlockSpec((bm, n), lambda i, s: (i, 0))),
        out_shape=jax.ShapeDtypeStruct((m, n), F32), compiler_params=_params(1), name=name)(slot, full, recv)


def sum_slots(parts, name, bm=312):
    _, r, n = parts.shape
    bm = bm if r % bm == 0 else r

    def body(p_ref, s_ref):
        acc = p_ref[0]
        for k in range(1, 8):
            acc = acc + p_ref[k]
        s_ref[...] = acc

    return pl.pallas_call(
        body, grid=(r // bm,), in_specs=[pl.BlockSpec((8, bm, n), lambda i: (0, i, 0))],
        out_specs=pl.BlockSpec((bm, n), lambda i: (i, 0)), out_shape=jax.ShapeDtypeStruct((r, n), F32),
        compiler_params=_params(1), name=name)(parts)


def _adamw_math(w, g, m, v):
    m = ADAM_B1 * m + (1.0 - ADAM_B1) * g
    v = ADAM_B2 * v + (1.0 - ADAM_B2) * (g * g)
    m_hat = m / (1.0 - ADAM_B1 ** ADAM_STEP)
    v_hat = v / (1.0 - ADAM_B2 ** ADAM_STEP)
    delta = -ADAM_LR * (m_hat / (jnp.sqrt(v_hat) + ADAM_EPS) + ADAM_WD * w)
    return delta, m, v


def adamw(w, m, v, grads, layer, prev, name, bm=256, row0=0):
    _, _, n = w.shape
    mm = grads[0].shape[0]
    bm = _row_block(mm, bm)
    first = row0 // bm
    ng = len(grads)

    def body(*refs):
        w_ref, m_ref, v_ref = refs[:3]
        g_refs = refs[3:3 + ng]
        g_out, d_out, m_out, v_out = refs[-4:]
        g = g_refs[0][...]
        for gr in g_refs[1:]:
            g = g + gr[...]
        delta, mn, vn = _adamw_math(w_ref[...], g, m_ref[...], v_ref[...])
        g_out[...] = g
        d_out[...] = delta
        m_out[...] = mn
        v_out[...] = vn

    slab = pl.BlockSpec((None, bm, n), lambda i: (layer, first + i, 0))
    flat = pl.BlockSpec((bm, n), lambda i: (i, 0))
    in_specs = [slab] * 3 + [flat] * ng
    args = [w, m, v, *grads]
    aliases = {}
    if prev is not None:
        in_specs += [ANY] * 4
        aliases = {3 + ng + q: q for q in range(4)}
        args += list(prev)
    return pl.pallas_call(
        body, grid=(mm // bm,), in_specs=in_specs, out_specs=[slab] * 4,
        out_shape=[jax.ShapeDtypeStruct(w.shape, F32)] * 4, input_output_aliases=aliases,
        compiler_params=_params(1), name=name)(*args)


def _pack_rows(parts, rows):
    flat = jnp.concatenate([p.reshape(-1) for p in parts])
    return jnp.pad(flat, (0, rows * 128 - flat.shape[0])).reshape(rows, 128)


def _as_shards(a, rows):
    return a.reshape(N_SHARD, rows, a.shape[-1])


def _local_step(x, pos_col, target, gains, pool_w, pool_scale, gn_gain, conv_w, conv_b, weights, send_grads,
                riders=None):
    def gain(l, n, token=None):
        g = gains[l, n].reshape(1, D_MODEL)
        return g if token is None else g + token[0:1, 0:1]

    inv_freq = (ROPE_BASE ** (-jnp.arange(0, RET_QK, 2, dtype=F32) / RET_QK)).reshape(1, RET_QK // 2)
    intra, cross, inner, chunk_decay = _decay_tables()
    dn_rows = D_FF // N_SHARD

    x1 = pool_fwd(x, gain(0, 0), gain(0, 1), pool_w, pool_scale)
    cos, sin = rope_tables(pos_col, inv_freq, x1)
    w_up0, w_dn0 = weights("mlp0", cos)
    w_dn0 = w_dn0.reshape(D_FF, D_MODEL)
    x2, u0, s0, a0, f0, h0 = mlp_fwd(x1, gain(0, 2), gain(0, 3), w_up0, w_dn0, conv_w[0], conv_b[0])
    w_in, w_out = weights("ret", x2)
    w_out = w_out.reshape(RET_HEADS * RET_V, D_MODEL)
    pj, hr = ret_proj(x2, gain(1, 0), w_in, cos, sin)
    o, sprev = ret_core_fwd(pj, intra, cross, inner, chunk_decay)
    x3, yb, out = ret_out_fwd(o, pj, x2, gn_gain, gain(1, 1), w_out)
    w_up1, w_dn1 = weights("mlp1", x3)
    w_dn1 = w_dn1.reshape(D_FF, D_MODEL)
    dx4, u1, s1, a1, f1, h1, loss = mlp_fwd(x3, gain(1, 2), gain(1, 3), w_up1, w_dn1, conv_w[1], conv_b[1], target)

    dx3, du1, df1, dg12, dg13, dcw1, dcb1 = mlp_bwd(
        dx4, f1, x3, u1, s1, gain(1, 2), gain(1, 3), w_up1, w_dn1, conv_w[1])
    dwup1 = grad_matmul(h1, du1, D_MODEL, FF_CHUNK, "grad_w_up_1", interleaved=True)
    dwdn1 = grad_matmul(a1, df1, FF_CHUNK, D_MODEL, "grad_w_down_1")
    tok = send_grads("mlp1", [dwup1, [_as_shards(g, dn_rows) for g in dwdn1]])
    dout, dpj, do, dg11, dgn = ret_out_bwd(dx3, out, o, pj, gn_gain, gain(1, 1, tok), w_out)
    dwout = grad_matmul(yb, dout, 1024, D_MODEL, "grad_w_out")
    dpj = ret_core_bwd(pj, do, sprev, cos, sin, dpj, intra, cross, inner, chunk_decay)
    dwin = grad_matmul(hr, dpj, D_MODEL, RET_IN_SHARD, "grad_w_in")
    tok = send_grads("ret", [dwin, [_as_shards(g, RET_V) for g in dwout]])
    dx2, dg10 = ret_in_bwd(dpj, dx3, x2, gain(1, 0, tok), w_in)
    dx1, du0, df0, dg02, dg03, dcw0, dcb0 = mlp_bwd(
        dx2, f0, x1, u0, s0, gain(0, 2), gain(0, 3), w_up0, w_dn0, conv_w[0])
    def late_grad(name, marker, *args, **kwargs):
        specs = [] if riders is None else riders.specs(name, marker)
        out = grad_matmul(*args, name, riders=specs, **kwargs)
        if specs:
            riders.done(name, out[2])
        return out[:2]

    dwdn0 = late_grad("grad_w_down_0", df0, a0, df0, FF_CHUNK, D_MODEL)
    tok = send_grads("down0", [[_as_shards(g, dn_rows) for g in dwdn0]])
    half = D_MODEL // 2
    for part, first, tk in (("a", 0, 4096), ("b", half, 2048)):
        dwup0 = late_grad("grad_w_up_0" + part, tok, h0, du0, half, FF_CHUNK, tk=tk, interleaved=True, after=tok,
                          cols=(first, half))
        tok = send_grads("up0" + part, [dwup0])
    dx0, dg00, dg01, dpscale, dpw = pool_bwd(dx1, x, gain(0, 0, tok), gain(0, 1), pool_w, pool_scale)

    rows = lambda g: jnp.sum(g, axis=0, keepdims=True)
    dgains = jnp.concatenate([dg00, dg01, rows(dg02), rows(dg03), dg10, dg11, rows(dg12), rows(dg13)],
                             axis=0).reshape(2, 4, D_MODEL)
    small = {"gains": dgains, "pool_scale": dpscale, "gn": dgn,
             "conv_w": jnp.sum(jnp.stack([dcw0, dcw1]), axis=3),
             "conv_b": jnp.sum(jnp.stack([dcb0, dcb1]), axis=2, keepdims=True), "pool_w": dpw}
    return loss, dx0, small


def kernel(x, positions, norm_gain, pool_w, pool_scale, ret_w_in, ret_gn_gain, ret_w_out, mlp_w_up, mlp_conv_w, mlp_conv_b, mlp_w_down, loss_target, m_norm_gain, m_pool_w, m_pool_scale, m_ret_w_in, m_ret_gn_gain, m_ret_w_out, m_mlp_w_up, m_mlp_conv_w, m_mlp_conv_b, m_mlp_w_down, v_norm_gain, v_pool_w, v_pool_scale, v_ret_w_in, v_ret_gn_gain, v_ret_w_out, v_mlp_w_up, v_mlp_conv_w, v_mlp_conv_b, v_mlp_w_down):
    t = x.shape[1]
    me = 2 * lax.axis_index("x") + lax.axis_index("y")
    me_slot = jnp.reshape(me, (1,)).astype(jnp.int32)

    small_parts = [norm_gain, ret_gn_gain, mlp_conv_w, pool_w]
    small_sizes = [p.size for p in small_parts]
    small_rows = -(-sum(small_sizes) // (128 * 8)) * 8
    gathers = {}

    def start_gather(group, srcs, after):
        lands = [lax.dynamic_update_index_in_dim(lax.empty((N_SHARD,) + s.shape, s.dtype), s, me, 0) for s in srcs]
        mode = "gather_half" if group == "mlp0" else "gather"
        gathers[group] = (mode, exchange_start(mode, srcs, lands, "gather_start_" + group, after=after))
        return gathers[group][1][-1]

    token = start_gather("small", [_pack_rows(small_parts, small_rows)], None)
    token = start_gather("mlp0", [mlp_w_up[0].astype(BF16), mlp_w_down[0].astype(BF16)], token)

    def weights(group, after):
        if group == "mlp0":
            tok = start_gather("ret", [ret_w_in[0].astype(BF16), ret_w_out[0].astype(BF16)], after)
            after = start_gather("mlp1", [mlp_w_up[1].astype(BF16), mlp_w_down[1].astype(BF16)], tok)
        mode, started = gathers[group]
        _, lands = exchange_wait(mode, started, after, "gather_wait_" + group)
        if mode == "gather_half":
            forward = exchange_start("forward_half", [], lands, "forward_start_" + group)
            _, lands = exchange_wait("forward_half", forward, forward[-1], "forward_wait_" + group)
        return lands

    sent, early = {}, {}

    def reduced(group, after, names):
        started, own = sent[group]
        _, recv = exchange_wait("scatter", started, after, "scatter_wait_" + group)
        return [plane_sum(me_slot, f, r, "plane_sum_" + nm)
                for f, r, nm in zip(own, recv, names)]

    def swap_start(planes, name):
        return exchange_start("swap", planes, [lax.empty(p.shape, p.dtype) for p in planes], name)

    def send_grads(group, pairs):
        lands = [lax.empty((3,) + b.shape[1:], BF16) for _, b in pairs]
        sent[group] = (exchange_start("scatter", [b for _, b in pairs], lands, "scatter_start_" + group),
                       [f for f, _ in pairs])
        token = sent[group][0][-1]
        done = {"ret": ("mlp1", ["w_up_1", "w_down_1"]), "down0": ("ret", ["w_in", "w_out"])}.get(group)
        if done is not None:
            early[done[0]] = swap_start(reduced(done[0], pairs[0][1], done[1]), "swap_start_" + done[0])
            token = token + early[done[0]][-1]
        return token

    class _Riders:
        def __init__(self):
            self.results = {}

        def specs(self, name, marker):
            if name == "grad_w_down_0":
                early["mlp1_sums"] = exchange_wait("swap", early["mlp1"], marker, "swap_wait_mlp1")
                (p_up, _), (q_up, _) = early["mlp1_sums"]
                return [(mlp_w_up, m_mlp_w_up, v_mlp_w_up, (p_up, q_up), 1, None)]
            if name == "grad_w_up_0a":
                (_, p_dn), (_, q_dn) = early["mlp1_sums"]
                return [(mlp_w_down, m_mlp_w_down, v_mlp_w_down, (p_dn, q_dn), 1, None)]
            (p_in, p_out), (q_in, q_out) = exchange_wait("swap", early["ret"], marker, "swap_wait_ret")
            return [(ret_w_in, m_ret_w_in, v_ret_w_in, (p_in, q_in), 0, None),
                    (ret_w_out, m_ret_w_out, v_ret_w_out, (p_out, q_out), 0, None)]

        def done(self, name, results):
            self.results[name] = results

    riders = _Riders()

    (smallg,) = weights("small", token)
    smallg = smallg.reshape(N_SHARD, -1)
    offs = np.cumsum([0] + small_sizes)
    piece = lambda i, shape: smallg[:, offs[i]:offs[i + 1]].reshape((N_SHARD,) + shape)
    gains = piece(0, (2, 4, 256)).transpose(1, 2, 0, 3).reshape(2, 4, D_MODEL)
    gn_full = piece(1, (512,)).reshape(1, RET_HEADS * RET_V)
    cw_full = piece(2, (2, 3, FF_CHUNK)).transpose(1, 0, 2, 3)
    pw_full = piece(3, (4, 64, 256)).transpose(1, 0, 2, 3).reshape(4, 256, 256).astype(BF16)
    cb_full = mlp_conv_b.reshape(2, N_SHARD, 1, FF_CHUNK)

    loss, dx0, small = _local_step(
        x[0], positions.reshape(t, 1).astype(F32), loss_target[0], gains, pw_full, pool_scale, gn_full,
        cw_full, cb_full, weights, send_grads, riders)

    def small_adamw(w, m, v, grads, name):
        w3 = w.reshape(1, -1, w.shape[-1])
        out = adamw(w3, m.reshape(w3.shape), v.reshape(w3.shape), [g.reshape(w3.shape[1:]) for g in grads], 0, None, name)
        return [o.reshape(w.shape) for o in out]

    pw_f = small["pool_w"].reshape(4, N_SHARD, 64, 256).transpose(1, 0, 2, 3).reshape(N_SHARD, 256, 256)
    small_order = ["gains", "pool_scale", "gn", "conv_w", "conv_b"]
    gsmall_sizes = [small[k].size for k in small_order]
    gsmall_rows = -(-sum(gsmall_sizes) // (128 * 8)) * 8
    gpack = _pack_rows([small[k] for k in small_order], gsmall_rows)
    mine = 2 * me + lax.axis_index("c")
    small_started = exchange_start(
        "everyone", [gpack], [lax.dynamic_update_index_in_dim(lax.empty((8,) + gpack.shape, F32), gpack, mine, 0)],
        "small_start")
    last_token = send_grads("pool_w", [(pw_f, pw_f.astype(BF16))]) + small_started[-1]

    res = {}
    (up1,) = riders.results["grad_w_down_0"]
    (dn1,) = riders.results["grad_w_up_0a"]
    res["ret_w_in"], res["ret_w_out"] = riders.results["grad_w_up_0b"]

    planes_b = (reduced("up0a", last_token, ["w_up_0a"]) + reduced("up0b", last_token, ["w_up_0b"])
                + reduced("down0", last_token, ["w_down_0"]) + reduced("pool_w", last_token, ["pool_w"]))
    swap_b = swap_start(planes_b, "swap_start_b")

    _, (small_recv,) = exchange_wait("everyone", small_started, swap_b[-1], "small_wait")
    gsmall = sum_slots(small_recv, "sum_small").reshape(-1)
    goffs = np.cumsum([0] + gsmall_sizes)
    gpiece = lambda i: gsmall[goffs[i]:goffs[i + 1]].reshape(small[small_order[i]].shape)
    g_gains = lax.dynamic_slice_in_dim(gpiece(0), me * 256, 256, axis=2)
    g_gn = lax.dynamic_slice_in_dim(gpiece(2), me * RET_V, RET_V, axis=1)
    g_cw = lax.dynamic_index_in_dim(gpiece(3), me, 1, keepdims=False)
    res["norm_gain"] = small_adamw(norm_gain, m_norm_gain, v_norm_gain, [g_gains], "adamw_norm_gain")
    res["pool_scale"] = small_adamw(pool_scale, m_pool_scale, v_pool_scale, [gpiece(1)], "adamw_pool_scale")
    res["ret_gn_gain"] = small_adamw(ret_gn_gain, m_ret_gn_gain, v_ret_gn_gain, [g_gn], "adamw_gn_gain")
    res["mlp_conv_w"] = small_adamw(mlp_conv_w, m_mlp_conv_w, v_mlp_conv_w, [g_cw], "adamw_conv_w")
    res["mlp_conv_b"] = small_adamw(mlp_conv_b, m_mlp_conv_b, v_mlp_conv_b, [gpiece(4)], "adamw_conv_b")

    planes_b, others_b = exchange_wait("swap", swap_b, res["mlp_conv_b"][0], "swap_wait_b")
    up0a = adamw(mlp_w_up, m_mlp_w_up, v_mlp_w_up, (planes_b[0], others_b[0]), 0, up1, "adamw_w_up_0a")
    res["mlp_w_up"] = adamw(mlp_w_up, m_mlp_w_up, v_mlp_w_up, (planes_b[1], others_b[1]), 0, up0a, "adamw_w_up_0b",
                            row0=D_MODEL // 2)
    res["mlp_w_down"] = adamw(mlp_w_down, m_mlp_w_down, v_mlp_w_down, (planes_b[2], others_b[2]), 0, dn1,
                              "adamw_w_down_0")
    res["pool_w"] = small_adamw(pool_w, m_pool_w, v_pool_w, (planes_b[3], others_b[3]), "adamw_pool_w")

    order = ["norm_gain", "pool_w", "pool_scale", "ret_w_in", "ret_gn_gain", "ret_w_out", "mlp_w_up", "mlp_conv_w",
             "mlp_conv_b", "mlp_w_down"]
    total_loss = lax.psum(loss[0, 0], ("x", "y", "c"))
    outs = [total_loss, dx0.reshape(x.shape)]
    for q in range(4):
        outs += [res[k][q] for k in order]
    return tuple(outs)
```

```python
import numpy as np
import jax
import jax.numpy as jnp
from jax import lax
from jax.experimental import pallas as pl
from jax.experimental.pallas import tpu as pltpu

F32 = jnp.float32
BF16 = jnp.bfloat16

D_MODEL = 1024
D_FF = 2816
FF_CHUNK = 1408
N_SHARD = 4
POOL_WINDOWS = (2, 4, 8, 16)
POOL_DIM = 256
POOL_HALO = 16
RET_HEADS = 4
RET_QK = 256
RET_V = 512
RET_CHUNK = 256
RET_IN = 6144
RET_IN_SHARD = 1536
ROPE_BASE = 10000.0
EPS = 1e-6
CONV_HALO = 8

ADAM_LR, ADAM_B1, ADAM_B2, ADAM_EPS, ADAM_WD, ADAM_STEP = 0.001, 0.9, 0.999, 1e-08, 0.01, 10

VMEM_LIMIT = 56 * 1024 * 1024
VMEM_LIMIT_MLP_BWD = 62 * 1024 * 1024
MESH = pl.DeviceIdType.MESH
ANY = pl.BlockSpec(memory_space=pl.ANY)


def _params(n_grid=1, limit=VMEM_LIMIT):
    return pltpu.CompilerParams(dimension_semantics=("arbitrary",) * n_grid, vmem_limit_bytes=limit)


def _dot(a, b):
    return jnp.dot(a, b, preferred_element_type=F32)


def _dot_nt(a, b):
    return lax.dot_general(a, b, (((1,), (1,)), ((), ())), preferred_element_type=F32)


def _dot_tn(a, b):
    return lax.dot_general(a, b, (((0,), (0,)), ((), ())), preferred_element_type=F32)


def _rms_fwd(x, gain):
    r = lax.rsqrt(jnp.mean(x * x, axis=-1, keepdims=True) + EPS)
    xh = x * r
    return xh * gain, xh, r


def _rms_bwd(xh, r, gain, dy):
    dxh = dy * gain
    return r * (dxh - xh * jnp.mean(dxh * xh, axis=-1, keepdims=True))


def _colsum(v):
    return jnp.sum(v, axis=0, keepdims=True)


def _full(shape):
    nd = len(shape)
    return pl.BlockSpec(shape, lambda *_: (0,) * nd)


def rope_tables(pos_col, inv_freq, after):
    t = pos_col.shape[0]
    tm = min(t, 1024)

    def body(p_ref, f_ref, after_ref, c_ref, s_ref):
        ang = p_ref[...] * f_ref[...]
        c_ref[...] = jnp.cos(ang)
        s_ref[...] = jnp.sin(ang)

    return pl.pallas_call(
        body, grid=(t // tm,),
        in_specs=[pl.BlockSpec((tm, 1), lambda i: (i, 0)), _full((1, 128)), ANY],
        out_specs=[pl.BlockSpec((tm, 128), lambda i: (i, 0))] * 2,
        out_shape=[jax.ShapeDtypeStruct((t, 128), F32)] * 2,
        compiler_params=_params(1), name="rope_tables")(pos_col, inv_freq, after)


def _window_sums(ext, backward):
    n = ext.shape[0]
    cur, sums = ext, []
    for g, win in enumerate(POOL_WINDOWS):
        if g > 0:
            cur = cur[:, POOL_DIM:]
        half = win // 2
        cur = cur + pltpu.roll(cur, n - half if backward else half, axis=0)
        sums.append(cur[:, 0:POOL_DIM])
    return sums


def _pool_diff(h_halo, h, row0, tm):
    t_idx = row0 + lax.broadcasted_iota(jnp.int32, (tm, 1), 0)
    sums = _window_sums(jnp.concatenate([h_halo, h], axis=0), backward=False)
    parts, inv_counts = [], []
    for g, win in enumerate(POOL_WINDOWS):
        inv = 1.0 / jnp.minimum(t_idx + 1, win).astype(F32)
        parts.append(sums[g][POOL_HALO:, :] * inv - h[:, g * POOL_DIM:(g + 1) * POOL_DIM])
        inv_counts.append(inv)
    return parts, inv_counts


def pool_fwd(x, g_pre, g_post, pool_w, pool_scale, tm=512):
    t, d = x.shape
    nt = t // tm

    def body(x_ref, g0_ref, g1_ref, w_ref, sc_ref, o_ref, hext):
        i = pl.program_id(0)

        @pl.when(i == 0)
        def _():
            hext[...] = jnp.zeros((POOL_HALO, d), F32)

        xv = x_ref[...]
        h, _, _ = _rms_fwd(xv, g0_ref[...])
        parts, _ = _pool_diff(hext[...], h, i * tm, tm)
        hext[...] = h[tm - POOL_HALO:tm, :]
        ys = [_dot(parts[g].astype(BF16), w_ref[g]) for g in range(len(POOL_WINDOWS))]
        y = jnp.concatenate(ys, axis=-1) * sc_ref[...]
        m, _, _ = _rms_fwd(y, g1_ref[...])
        o_ref[...] = xv + m

    row = pl.BlockSpec((tm, d), lambda i: (i, 0))
    return pl.pallas_call(
        body, grid=(nt,),
        in_specs=[row, _full((1, d)), _full((1, d)), _full(pool_w.shape), _full((1, d))],
        out_specs=row, out_shape=jax.ShapeDtypeStruct((t, d), F32),
        scratch_shapes=[pltpu.VMEM((POOL_HALO, d), F32)],
        compiler_params=_params(1), name="pool_fwd")(x, g_pre, g_post, pool_w, pool_scale)


def pool_bwd(dx1, x, g_pre, g_post, pool_w, pool_scale, tm=512):
    t, d = x.shape
    nt = t // tm
    ng = len(POOL_WINDOWS)

    def body(dx1_ref, x_ref, xh_ref, g0_ref, g1_ref, w_ref, sc_ref,
             dx_ref, dg0_ref, dg1_ref, dsc_ref, dw_ref, enext):
        i = pl.program_id(0)
        r = nt - 1 - i

        @pl.when(i == 0)
        def _():
            enext[...] = jnp.zeros((POOL_HALO, d), F32)
            dg0_ref[...] = jnp.zeros_like(dg0_ref)
            dg1_ref[...] = jnp.zeros_like(dg1_ref)
            dsc_ref[...] = jnp.zeros_like(dsc_ref)
            dw_ref[...] = jnp.zeros_like(dw_ref)

        g0 = g0_ref[...]
        g1 = g1_ref[...]
        sc = sc_ref[...]
        xv = x_ref[...]
        h, xh, rx = _rms_fwd(xv, g0)
        h_halo, _, _ = _rms_fwd(xh_ref[...], g0)
        parts, inv_counts = _pool_diff(h_halo * jnp.where(r > 0, 1.0, 0.0), h, r * tm, tm)
        parts_b = [p.astype(BF16) for p in parts]
        ypre = jnp.concatenate([_dot(parts_b[g], w_ref[g]) for g in range(ng)], axis=-1)
        _, yh, ry = _rms_fwd(ypre * sc, g1)
        dm = dx1_ref[...]
        dg1_ref[...] += _colsum(dm * yh)
        dy = _rms_bwd(yh, ry, g1, dm)
        dsc_ref[...] += _colsum(dy * ypre)
        dyp = (dy * sc).astype(BF16)
        ddiffs = []
        for g in range(ng):
            cols = slice(g * POOL_DIM, (g + 1) * POOL_DIM)
            dw_ref[g] += _dot_tn(parts_b[g], dyp[:, cols])
            ddiffs.append(_dot_nt(dyp[:, cols], w_ref[g]))
        e = jnp.concatenate([ddiffs[g] * inv_counts[g] for g in range(ng)], axis=-1)
        sums = _window_sums(jnp.concatenate([e, enext[...]], axis=0), backward=True)
        enext[...] = e[0:POOL_HALO, :]
        dh = jnp.concatenate([sums[g][0:tm, :] - ddiffs[g] for g in range(ng)], axis=-1)
        dg0_ref[...] += _colsum(dh * xh)
        dx_ref[...] = dm + _rms_bwd(xh, rx, g0, dh)

    row = pl.BlockSpec((tm, d), lambda i: (nt - 1 - i, 0))
    halo = pl.BlockSpec((POOL_HALO, d), lambda i: (jnp.maximum((nt - 1 - i) * (tm // POOL_HALO) - 1, 0), 0))
    vec = _full((1, d))
    return pl.pallas_call(
        body, grid=(nt,),
        in_specs=[row, row, halo, vec, vec, _full(pool_w.shape), vec],
        out_specs=[row, vec, vec, vec, _full((ng, POOL_DIM, POOL_DIM))],
        out_shape=[jax.ShapeDtypeStruct((t, d), F32)] + [jax.ShapeDtypeStruct((1, d), F32)] * 3
        + [jax.ShapeDtypeStruct((ng, POOL_DIM, POOL_DIM), F32)],
        scratch_shapes=[pltpu.VMEM((POOL_HALO, d), F32)],
        compiler_params=_params(1), name="pool_bwd")(dx1, x, x, g_pre, g_post, pool_w, pool_scale)


def _conv_taps(cw_ref, j):
    return cw_ref[j, 0:1, :], cw_ref[j, 1:2, :], cw_ref[j, 2:3, :]


def _row_block(m, target=256):
    if m <= target:
        return m
    for b in range(target, 7, -8):
        if m % b == 0:
            return b
    return m


def mlp_fwd(x, g_pre, g_post, w_up, w_down, conv_w, conv_b, target=None, tm=256):
    t, d = x.shape
    nt = t // tm
    h8 = CONV_HALO
    with_loss = target is not None
    n_extra = 1 if with_loss else 0

    def body(x_ref, g2_ref, g3_ref, wup_hbm, wdn_hbm, cw_ref, cb_ref, *rest):
        tgt_ref = rest[0] if with_loss else None
        xo_ref, u_ref, s_ref, a_ref, f_ref, h_ref = rest[n_extra:n_extra + 6]
        loss_ref = rest[n_extra + 6] if with_loss else None
        wup_v, wdn_v, tail, sem = rest[-4:]
        i = pl.program_id(0)

        @pl.when(i == 0)
        def _():
            c1 = pltpu.make_async_copy(wup_hbm, wup_v, sem.at[0])
            c2 = pltpu.make_async_copy(wdn_hbm, wdn_v, sem.at[1])
            c1.start()
            c2.start()
            tail[...] = jnp.zeros_like(tail)
            if with_loss:
                loss_ref[...] = jnp.zeros_like(loss_ref)
            c1.wait()
            c2.wait()

        xv = x_ref[...]
        h, _, _ = _rms_fwd(xv, g2_ref[...])
        hb = h.astype(BF16)
        h_ref[...] = hb
        acc = jnp.zeros((tm, d), F32)
        for k in range(2):
            cs = []
            for s in range(2):
                j, cols = k + 2 * s, slice((2 * k + s) * FF_CHUNK, (2 * k + s + 1) * FF_CHUNK)
                uf = _dot(hb, wup_v[j])
                u_ref[:, cols] = uf.astype(BF16)
                ext = jnp.concatenate([tail[j], uf], axis=0)
                tail[j] = uf[tm - h8:tm, :]
                w0, w1, w2 = _conv_taps(cw_ref, j)
                cs.append(cb_ref[j] + w2 * uf + w1 * pltpu.roll(ext, 1, axis=0)[h8:, :]
                          + w0 * pltpu.roll(ext, 2, axis=0)[h8:, :])
            cg, cv = cs
            sg = jax.nn.sigmoid(cg)
            sil = cg * sg
            ab = (sil * cv).astype(BF16)
            a_ref[:, k * FF_CHUNK:(k + 1) * FF_CHUNK] = ab
            s_ref[:, 2 * k * FF_CHUNK:(2 * k + 1) * FF_CHUNK] = sil.astype(BF16)
            s_ref[:, (2 * k + 1) * FF_CHUNK:(2 * k + 2) * FF_CHUNK] = (cv * (sg + sil * (1.0 - sg))).astype(BF16)
            acc = acc + _dot(ab, wdn_v[k * FF_CHUNK:(k + 1) * FF_CHUNK, :])
        f_ref[...] = acc
        y, _, _ = _rms_fwd(acc, g3_ref[...])
        if with_loss:
            err = (xv + y) - tgt_ref[...]
            xo_ref[...] = err * (1.0 / d)
            loss_ref[...] += 0.5 * jnp.sum(jnp.mean(err * err, axis=-1, keepdims=True), axis=0, keepdims=True)
        else:
            xo_ref[...] = xv + y

    row = pl.BlockSpec((tm, d), lambda i: (i, 0))
    wide = pl.BlockSpec((tm, 2 * D_FF), lambda i: (i, 0))
    vec = _full((1, d))
    extra = [target] if with_loss else []
    return pl.pallas_call(
        body, grid=(nt,),
        in_specs=[row, vec, vec, ANY, ANY, _full(conv_w.shape), _full(conv_b.shape)] + [row] * n_extra,
        out_specs=[row, wide, wide, pl.BlockSpec((tm, D_FF), lambda i: (i, 0)), row, row] + [_full((1, 1))] * n_extra,
        out_shape=[jax.ShapeDtypeStruct((t, d), F32), jax.ShapeDtypeStruct((t, 2 * D_FF), BF16),
                   jax.ShapeDtypeStruct((t, 2 * D_FF), BF16), jax.ShapeDtypeStruct((t, D_FF), BF16),
                   jax.ShapeDtypeStruct((t, d), F32), jax.ShapeDtypeStruct((t, d), BF16)]
        + [jax.ShapeDtypeStruct((1, 1), F32)] * n_extra,
        scratch_shapes=[pltpu.VMEM(w_up.shape, BF16), pltpu.VMEM(w_down.shape, BF16),
                        pltpu.VMEM((N_SHARD, h8, FF_CHUNK), F32), pltpu.SemaphoreType.DMA((2,))],
        compiler_params=_params(1), name="mlp_fwd_loss" if with_loss else "mlp_fwd")(
            x, g_pre, g_post, w_up, w_down, conv_w, conv_b, *extra)


def _rowsum8(v):
    return jnp.sum(v.reshape(v.shape[0] // 8, 8, v.shape[1]), axis=0)


def mlp_bwd(dxo, f, x, u, sp, g_pre, g_post, w_up, w_down, conv_w, tm=256):
    t, d = x.shape
    nt = t // tm
    h8 = CONV_HALO

    def body(dxo_ref, f_ref, x_ref, u_ref, s_ref, g2_ref, g3_ref, wup_hbm, wdn_hbm, cw_ref,
             dx_ref, du_ref, df_ref, dg2_ref, dg3_ref, dcw_ref, dcb_ref,
             wup_v, wdn_v, carry, sem):
        @pl.when(pl.program_id(0) == 0)
        def _():
            c1 = pltpu.make_async_copy(wup_hbm, wup_v, sem.at[0])
            c2 = pltpu.make_async_copy(wdn_hbm, wdn_v, sem.at[1])
            c1.start()
            c2.start()
            carry[...] = jnp.zeros_like(carry)
            dg2_ref[...] = jnp.zeros_like(dg2_ref)
            dg3_ref[...] = jnp.zeros_like(dg3_ref)
            dcw_ref[...] = jnp.zeros_like(dcw_ref)
            dcb_ref[...] = jnp.zeros_like(dcb_ref)
            c1.wait()
            c2.wait()

        g3 = g3_ref[...]
        dxo = dxo_ref[...]
        _, fh, rf = _rms_fwd(f_ref[...], g3)
        dg3_ref[...] += _rowsum8(dxo * fh)
        dfb = _rms_bwd(fh, rf, g3, dxo).astype(BF16)
        df_ref[...] = dfb
        dh = jnp.zeros((tm, d), F32)
        for k in range(2):
            da = _dot_nt(dfb, wdn_v[k * FF_CHUNK:(k + 1) * FF_CHUNK, :])
            for s in range(2):
                j = k + 2 * s
                cols = slice((2 * k + s) * FF_CHUNK, (2 * k + s + 1) * FF_CHUNK)
                dc = da * s_ref[:, (2 * k + 1 - s) * FF_CHUNK:(2 * k + 2 - s) * FF_CHUNK].astype(F32)
                uf = u_ref[:, cols].astype(F32)
                ext = jnp.concatenate([dc, carry[j]], axis=0)
                carry[j] = dc[0:h8, :]
                dc1 = pltpu.roll(ext, tm + h8 - 1, axis=0)[0:tm, :]
                dc2 = pltpu.roll(ext, tm + h8 - 2, axis=0)[0:tm, :]
                dcb_ref[j] += _rowsum8(dc)
                dcw_ref[j, 2] += _rowsum8(dc * uf)
                dcw_ref[j, 1] += _rowsum8(dc1 * uf)
                dcw_ref[j, 0] += _rowsum8(dc2 * uf)
                dub = (cw_ref[j, 2:3, :] * dc + cw_ref[j, 1:2, :] * dc1 + cw_ref[j, 0:1, :] * dc2).astype(BF16)
                du_ref[:, cols] = dub
                dh = dh + _dot_nt(dub, wup_v[j])
        g2 = g2_ref[...]
        _, xh, rx = _rms_fwd(x_ref[...], g2)
        dg2_ref[...] += _rowsum8(dh * xh)
        dx_ref[...] = dxo + _rms_bwd(xh, rx, g2, dh)

    row = pl.BlockSpec((tm, d), lambda i: (nt - 1 - i, 0))
    wide = pl.BlockSpec((tm, 2 * D_FF), lambda i: (nt - 1 - i, 0))
    vec = _full((1, d))
    acc = _full((8, d))
    dcw_shape, dcb_shape = (N_SHARD, 3, 8, FF_CHUNK), (N_SHARD, 8, FF_CHUNK)
    return pl.pallas_call(
        body, grid=(nt,),
        in_specs=[row, row, row, wide, wide, vec, vec, ANY, ANY, _full(conv_w.shape)],
        out_specs=[row, wide, row, acc, acc, _full(dcw_shape), _full(dcb_shape)],
        out_shape=[jax.ShapeDtypeStruct((t, d), F32), jax.ShapeDtypeStruct((t, 2 * D_FF), BF16),
                   jax.ShapeDtypeStruct((t, d), BF16),
                   jax.ShapeDtypeStruct((8, d), F32), jax.ShapeDtypeStruct((8, d), F32),
                   jax.ShapeDtypeStruct(dcw_shape, F32), jax.ShapeDtypeStruct(dcb_shape, F32)],
        scratch_shapes=[pltpu.VMEM(w_up.shape, BF16), pltpu.VMEM(w_down.shape, BF16),
                        pltpu.VMEM((N_SHARD, h8, FF_CHUNK), F32), pltpu.SemaphoreType.DMA((2,))],
        compiler_params=_params(1, VMEM_LIMIT_MLP_BWD), name="mlp_bwd")(
            dxo, f, x, u, sp, g_pre, g_post, w_up, w_down, conv_w)


def grad_matmul(a, b, bm, bn, name, tk=2048, interleaved=False, after=None, cols=None):
    t = a.shape[0]
    m0, m = (0, a.shape[1]) if cols is None else cols
    n = b.shape[1]
    tk = min(tk, t)
    nk = t // tk
    place = (lambda j: (j % 2) * 2 + j // 2) if interleaved else (lambda j: j)
    extra = [] if after is None else [after]
    first = m0 // bm

    def body(a_ref, b_ref, *rest):
        o_ref, ob_ref = rest[len(extra):]
        kk = pl.program_id(2)

        @pl.when(kk == 0)
        def _():
            o_ref[...] = jnp.zeros_like(o_ref)

        o_ref[...] += _dot_tn(a_ref[...], b_ref[...])

        @pl.when(kk == nk - 1)
        def _():
            ob_ref[...] = o_ref[...].astype(BF16)

    ospec = pl.BlockSpec((None, bm, bn), lambda j, i, kk: (place(j), i, 0))
    return pl.pallas_call(
        body, grid=(n // bn, m // bm, nk),
        in_specs=[pl.BlockSpec((tk, bm), lambda j, i, kk: (kk, first + i)),
                  pl.BlockSpec((tk, bn), lambda j, i, kk: (kk, j))]
        + [ANY] * len(extra),
        out_specs=[ospec, ospec],
        out_shape=[jax.ShapeDtypeStruct((n // bn, m, bn), F32), jax.ShapeDtypeStruct((n // bn, m, bn), BF16)],
        compiler_params=_params(3), name=name)(a, b, *extra)


def _decay_tables():
    log_gamma = jnp.log(1.0 - 2.0 ** (-5.0 - jnp.arange(RET_HEADS, dtype=F32)))
    i = jnp.arange(RET_CHUNK, dtype=F32)
    rel = i[:, None] - i[None, :]
    intra = jnp.where(rel >= 0, jnp.exp(jnp.maximum(rel, 0.0) * log_gamma[:, None, None]), 0.0)
    cross = jnp.exp((i + 1.0) * log_gamma[:, None])[:, :, None]
    inner = jnp.exp((RET_CHUNK - 1.0 - i) * log_gamma[:, None])[:, :, None]
    chunk = [float(np.exp(np.float32(RET_CHUNK) * np.log(np.float32(1.0 - 2.0 ** (-5.0 - h))).astype(np.float32)))
             for h in range(RET_HEADS)]
    return intra, cross, inner, chunk


def ret_proj(x, g_pre, w_in, cos, sin, tm=512):
    t, d = x.shape
    nt = t // tm
    per = RET_IN_SHARD // RET_QK

    def body(x_ref, g_ref, win_hbm, c_ref, s_ref, pj_ref, h_ref, win_v, sem):
        @pl.when(pl.program_id(0) == 0)
        def _():
            cp = pltpu.make_async_copy(win_hbm, win_v, sem)
            cp.start()
            cp.wait()

        h, _, _ = _rms_fwd(x_ref[...], g_ref[...])
        hb = h.astype(BF16)
        h_ref[...] = hb
        c = c_ref[...]
        s = s_ref[...]
        for j in range(N_SHARD):
            pjj = _dot(hb, win_v[j])
            for bb in range(per):
                b = per * j + bb
                blk = pjj[:, bb * RET_QK:(bb + 1) * RET_QK]
                if b < 2 * RET_HEADS:
                    x1, x2 = blk[:, :128], blk[:, 128:]
                    o1 = x1 * c - x2 * s
                    o2 = x2 * c + x1 * s
                    if b < RET_HEADS:
                        o1 = o1 * (RET_QK ** -0.5)
                        o2 = o2 * (RET_QK ** -0.5)
                    pj_ref[:, b * RET_QK:b * RET_QK + 128] = o1.astype(BF16)
                    pj_ref[:, b * RET_QK + 128:(b + 1) * RET_QK] = o2.astype(BF16)
                else:
                    pj_ref[:, b * RET_QK:(b + 1) * RET_QK] = blk.astype(BF16)

    row = pl.BlockSpec((tm, d), lambda i: (i, 0))
    tab = pl.BlockSpec((tm, 128), lambda i: (i, 0))
    return pl.pallas_call(
        body, grid=(nt,),
        in_specs=[row, _full((1, d)), ANY, tab, tab],
        out_specs=[pl.BlockSpec((tm, RET_IN), lambda i: (i, 0)), row],
        out_shape=[jax.ShapeDtypeStruct((t, RET_IN), BF16), jax.ShapeDtypeStruct((t, d), BF16)],
        scratch_shapes=[pltpu.VMEM(w_in.shape, BF16), pltpu.SemaphoreType.DMA],
        compiler_params=_params(1), name="ret_proj")(x, g_pre, w_in, cos, sin)


def ret_core_fwd(pj, intra, cross, inner, chunk_decay):
    t = pj.shape[0]
    nc = t // RET_CHUNK
    c = RET_CHUNK
    qk_all = RET_HEADS * RET_QK
    v_all = RET_HEADS * RET_V

    def body(q_ref, k_ref, v_ref, dm_ref, cr_ref, in_ref, o_ref, sp_ref, state):
        @pl.when(pl.program_id(0) == 0)
        def _():
            state[...] = jnp.zeros_like(state)

        for h in range(RET_HEADS):
            q = q_ref[:, h * RET_QK:(h + 1) * RET_QK]
            k = k_ref[:, h * RET_QK:(h + 1) * RET_QK]
            v = v_ref[:, h * RET_V:(h + 1) * RET_V]
            sb = state[h].astype(BF16)
            sp_ref[h] = sb
            sc = _dot_nt(q, k) * dm_ref[h]
            o_ref[:, h * RET_V:(h + 1) * RET_V] = _dot(sc.astype(BF16), v) + _dot(q, sb) * cr_ref[h]
            kd = (k.astype(F32) * in_ref[h]).astype(BF16)
            state[h] = state[h] * chunk_decay[h] + _dot_tn(kd, v)

    return pl.pallas_call(
        body, grid=(nc,),
        in_specs=[pl.BlockSpec((c, qk_all), lambda n: (n, 0)), pl.BlockSpec((c, qk_all), lambda n: (n, 1)),
                  pl.BlockSpec((c, v_all), lambda n: (n, 1)),
                  _full(intra.shape), _full(cross.shape), _full(inner.shape)],
        out_specs=[pl.BlockSpec((c, v_all), lambda n: (n, 0)),
                   pl.BlockSpec((None, RET_HEADS, RET_QK, RET_V), lambda n: (n, 0, 0, 0))],
        out_shape=[jax.ShapeDtypeStruct((t, v_all), F32),
                   jax.ShapeDtypeStruct((nc, RET_HEADS, RET_QK, RET_V), BF16)],
        scratch_shapes=[pltpu.VMEM((RET_HEADS, RET_QK, RET_V), F32)],
        compiler_params=_params(1), name="ret_core_fwd")(pj, pj, pj, intra, cross, inner)


def _group_norm(o_h):
    mu = jnp.mean(o_h, axis=-1, keepdims=True)
    dev = o_h - mu
    rstd = lax.rsqrt(jnp.mean(dev * dev, axis=-1, keepdims=True) + EPS)
    return dev * rstd, rstd


def ret_out_fwd(o, pj, x, gn_gain, g_post, w_out, tm=512):
    t, d = x.shape
    nt = t // tm
    v_all = RET_HEADS * RET_V

    def body(o_ref, g_ref, x_ref, gn_ref, g1_ref, w_ref, xo_ref, y_ref, out_ref):
        out = jnp.zeros((tm, d), F32)
        for h in range(RET_HEADS):
            cols = slice(h * RET_V, (h + 1) * RET_V)
            ohat, _ = _group_norm(o_ref[:, cols])
            g = g_ref[:, cols].astype(F32)
            yb = (g * jax.nn.sigmoid(g) * (ohat * gn_ref[:, cols])).astype(BF16)
            y_ref[:, cols] = yb
            out = out + _dot(yb, w_ref[cols, :])
        out_ref[...] = out
        m, _, _ = _rms_fwd(out, g1_ref[...])
        xo_ref[...] = x_ref[...] + m

    row = pl.BlockSpec((tm, d), lambda i: (i, 0))
    wide = pl.BlockSpec((tm, v_all), lambda i: (i, 0))
    return pl.pallas_call(
        body, grid=(nt,),
        in_specs=[wide, pl.BlockSpec((tm, v_all), lambda i: (i, 2)), row, _full((1, v_all)), _full((1, d)),
                  _full(w_out.shape)],
        out_specs=[row, wide, row],
        out_shape=[jax.ShapeDtypeStruct((t, d), F32), jax.ShapeDtypeStruct((t, v_all), BF16),
                   jax.ShapeDtypeStruct((t, d), F32)],
        compiler_params=_params(1), name="ret_out_fwd")(o, pj, x, gn_gain, g_post, w_out)


def ret_out_bwd(dxo, out, o, pj, gn_gain, g_post, w_out, tm=512):
    t, d = out.shape
    nt = t // tm
    v_all = RET_HEADS * RET_V

    def body(dxo_ref, out_ref, o_ref, g_ref, gn_ref, g1_ref, w_ref,
             dout_ref, dgate_ref, do_ref, dg1_ref, dgn_ref):
        @pl.when(pl.program_id(0) == 0)
        def _():
            dg1_ref[...] = jnp.zeros_like(dg1_ref)
            dgn_ref[...] = jnp.zeros_like(dgn_ref)

        g1 = g1_ref[...]
        dxo = dxo_ref[...]
        _, oh_, r_ = _rms_fwd(out_ref[...], g1)
        dg1_ref[...] += _colsum(dxo * oh_)
        doutb = _rms_bwd(oh_, r_, g1, dxo).astype(BF16)
        dout_ref[...] = doutb
        for h in range(RET_HEADS):
            cols = slice(h * RET_V, (h + 1) * RET_V)
            gn = gn_ref[:, cols]
            ohat, rstd = _group_norm(o_ref[:, cols])
            g = g_ref[:, cols].astype(F32)
            sg = jax.nn.sigmoid(g)
            dyh = _dot_nt(doutb, w_ref[cols, :])
            sil = g * sg
            dgate_ref[:, cols] = (dyh * (ohat * gn) * (sg + sil * (1.0 - sg))).astype(BF16)
            don = dyh * sil
            dgn_ref[:, cols] += _colsum(don * ohat)
            dohat = don * gn
            do_ref[:, cols] = (rstd * (dohat - jnp.mean(dohat, axis=-1, keepdims=True)
                                       - ohat * jnp.mean(dohat * ohat, axis=-1, keepdims=True))).astype(BF16)

    row = pl.BlockSpec((tm, d), lambda i: (i, 0))
    wide = pl.BlockSpec((tm, v_all), lambda i: (i, 0))
    gate = pl.BlockSpec((tm, v_all), lambda i: (i, 2))
    return pl.pallas_call(
        body, grid=(nt,),
        in_specs=[row, row, wide, gate, _full((1, v_all)), _full((1, d)), _full(w_out.shape)],
        out_specs=[row, gate, wide, _full((1, d)), _full((1, v_all))],
        out_shape=[jax.ShapeDtypeStruct((t, d), BF16), jax.ShapeDtypeStruct((t, RET_IN), BF16),
                   jax.ShapeDtypeStruct((t, v_all), BF16), jax.ShapeDtypeStruct((1, d), F32),
                   jax.ShapeDtypeStruct((1, v_all), F32)],
        compiler_params=_params(1), name="ret_out_bwd")(dxo, out, o, pj, gn_gain, g_post, w_out)


def ret_core_bwd(pj, do, sprev, cos, sin, dpj, intra, cross, inner, chunk_decay):
    t = pj.shape[0]
    nc = t // RET_CHUNK
    c = RET_CHUNK
    qk_all = RET_HEADS * RET_QK
    v_all = RET_HEADS * RET_V
    scale = RET_QK ** -0.5

    def body(q_ref, k_ref, v_ref, do_ref, sp_ref, c_ref, s_ref, dm_ref, cr_ref, in_ref, dpj_in, dpj_ref, dstate):
        @pl.when(pl.program_id(0) == 0)
        def _():
            dstate[...] = jnp.zeros_like(dstate)

        cs = c_ref[...]
        sn = s_ref[...]
        for h in range(RET_HEADS):
            q = q_ref[:, h * RET_QK:(h + 1) * RET_QK]
            k = k_ref[:, h * RET_QK:(h + 1) * RET_QK]
            v = v_ref[:, h * RET_V:(h + 1) * RET_V]
            doh = do_ref[:, h * RET_V:(h + 1) * RET_V]
            dm = dm_ref[h]
            ab = (_dot_nt(q, k) * dm).astype(BF16)
            dab = (_dot_nt(doh, v) * dm).astype(BF16)
            dsb = dstate[h].astype(BF16)
            kd = (k.astype(F32) * in_ref[h]).astype(BF16)
            dv = _dot_tn(ab, doh) + _dot(kd, dsb)
            dq = _dot(dab, k) + cr_ref[h] * _dot_nt(doh, sp_ref[h])
            dk = _dot_tn(dab, q) + in_ref[h] * _dot_nt(v, dsb)
            qd = (q.astype(F32) * cr_ref[h]).astype(BF16)
            dstate[h] = dstate[h] * chunk_decay[h] + _dot_tn(qd, doh)
            for base, dd, sc in ((h * RET_QK, dq, scale), (qk_all + h * RET_QK, dk, 1.0)):
                d1, d2 = dd[:, :128], dd[:, 128:]
                dpj_ref[:, base:base + 128] = ((d1 * cs + d2 * sn) * sc).astype(BF16)
                dpj_ref[:, base + 128:base + RET_QK] = ((d2 * cs - d1 * sn) * sc).astype(BF16)
            dpj_ref[:, 2 * qk_all + h * RET_V:2 * qk_all + (h + 1) * RET_V] = dv.astype(BF16)

    rev = lambda n: nc - 1 - n
    tab = pl.BlockSpec((c, 128), lambda n: (rev(n), 0))
    return pl.pallas_call(
        body, grid=(nc,),
        in_specs=[pl.BlockSpec((c, qk_all), lambda n: (rev(n), 0)), pl.BlockSpec((c, qk_all), lambda n: (rev(n), 1)),
                  pl.BlockSpec((c, v_all), lambda n: (rev(n), 1)), pl.BlockSpec((c, v_all), lambda n: (rev(n), 0)),
                  pl.BlockSpec((None, RET_HEADS, RET_QK, RET_V), lambda n: (rev(n), 0, 0, 0)),
                  tab, tab, _full(intra.shape), _full(cross.shape), _full(inner.shape), ANY],
        out_specs=pl.BlockSpec((c, 2 * qk_all + v_all), lambda n: (rev(n), 0)),
        out_shape=jax.ShapeDtypeStruct((t, RET_IN), BF16),
        scratch_shapes=[pltpu.VMEM((RET_HEADS, RET_QK, RET_V), F32)],
        input_output_aliases={10: 0},
        compiler_params=_params(1), name="ret_core_bwd")(pj, pj, pj, do, sprev, cos, sin, intra, cross, inner, dpj)


def ret_in_bwd(dpj, dres, x, g_pre, w_in, tm=512):
    t, d = x.shape
    nt = t // tm

    def body(dpj_ref, dres_ref, x_ref, g_ref, win_hbm, dx_ref, dg_ref, win_v, sem):
        @pl.when(pl.program_id(0) == 0)
        def _():
            cp = pltpu.make_async_copy(win_hbm, win_v, sem)
            cp.start()
            dg_ref[...] = jnp.zeros_like(dg_ref)
            cp.wait()

        g = g_ref[...]
        dh = jnp.zeros((tm, d), F32)
        for j in range(N_SHARD):
            dh = dh + _dot_nt(dpj_ref[:, j * RET_IN_SHARD:(j + 1) * RET_IN_SHARD], win_v[j])
        _, xh, rx = _rms_fwd(x_ref[...], g)
        dg_ref[...] += _colsum(dh * xh)
        dx_ref[...] = dres_ref[...] + _rms_bwd(xh, rx, g, dh)

    row = pl.BlockSpec((tm, d), lambda i: (i, 0))
    return pl.pallas_call(
        body, grid=(nt,),
        in_specs=[pl.BlockSpec((tm, RET_IN), lambda i: (i, 0)), row, row, _full((1, d)), ANY],
        out_specs=[row, _full((1, d))],
        out_shape=[jax.ShapeDtypeStruct((t, d), F32), jax.ShapeDtypeStruct((1, d), F32)],
        scratch_shapes=[pltpu.VMEM(w_in.shape, BF16), pltpu.SemaphoreType.DMA],
        compiler_params=_params(1), name="ret_in_bwd")(dpj, dres, x, g_pre, w_in)


_CHIP_FLIPS = ((1, 0), (0, 1), (1, 1))


def _flip(v, b):
    return 1 - v if b else v


_HBM = pl.BlockSpec(memory_space=pltpu.HBM)
_SEM = pl.BlockSpec(memory_space=pltpu.SEMAPHORE)
_EFFECT = pltpu.SideEffectType.DATAFLOW_SIDE_EFFECTING


def _chip_copies(mode, srcs, lands, send_sems, recv_sems):
    x, y, c = lax.axis_index("x"), lax.axis_index("y"), lax.axis_index("c")
    copies = []
    for t in range(len(lands)):
        if mode == "swap":
            copies.append(pltpu.make_async_remote_copy(
                src_ref=srcs[t], dst_ref=lands[t], send_sem=send_sems.at[t], recv_sem=recv_sems.at[t],
                device_id=(x, y, 1 - c), device_id_type=MESH))
            continue
        if mode == "everyone":
            for m in range(1, 8):
                bx, by, bc = (m >> 2) & 1, (m >> 1) & 1, m & 1
                copies.append(pltpu.make_async_remote_copy(
                    src_ref=srcs[t], dst_ref=lands[t].at[4 * x + 2 * y + c], send_sem=send_sems.at[7 * t + m - 1],
                    recv_sem=recv_sems.at[7 * t + m - 1], device_id=(_flip(x, bx), _flip(y, by), _flip(c, bc)),
                    device_id_type=MESH))
            continue
        for k, (bx, by) in enumerate(_CHIP_FLIPS):
            px, py = _flip(x, bx), _flip(y, by)
            target = (px, py, c)
            if mode == "gather":
                src, dst = srcs[t], lands[t].at[2 * x + y]
            elif mode == "gather_half":
                half = pl.ds(c * (srcs[t].shape[0] // 2), srcs[t].shape[0] // 2)
                src, dst = srcs[t].at[half], lands[t].at[2 * x + y, half]
            elif mode == "forward_half":
                half = pl.ds(c * (lands[t].shape[1] // 2), lands[t].shape[1] // 2)
                src = dst = lands[t].at[2 * px + py, half]
                target = (x, y, 1 - c)
            else:
                src, dst = srcs[t].at[2 * px + py], lands[t].at[k]
            copies.append(pltpu.make_async_remote_copy(
                src_ref=src, dst_ref=dst, send_sem=send_sems.at[3 * t + k], recv_sem=recv_sems.at[3 * t + k],
                device_id=target, device_id_type=MESH))
    return copies


def exchange_start(mode, srcs, lands, name, after=None):
    n, ns = len(lands), len(srcs)
    extra = [] if after is None else [after]

    def body(*refs):
        ins, lnd = refs[:ns], refs[ns:ns + n]
        send_sems, recv_sems = refs[ns + n + len(extra)], refs[ns + n + len(extra) + 1]
        token = refs[-1]
        for cp in _chip_copies(mode, ins, lnd, send_sems, recv_sems):
            cp.start()
        token[...] = jnp.zeros(token.shape, token.dtype)

    hbm = lambda a: pltpu.with_memory_space_constraint(a, pltpu.HBM)
    passed = list(srcs) + list(lands)
    n_sem = {"swap": 1, "everyone": 7}.get(mode, 3) * n
    return pl.pallas_call(
        body, name=name,
        out_shape=(pltpu.SemaphoreType.DMA((n_sem,)), pltpu.SemaphoreType.DMA((n_sem,)),
                   *[pltpu.HBM(a.shape, a.dtype) for a in passed], jax.ShapeDtypeStruct((8, 128), F32)),
        in_specs=[_HBM] * (ns + n) + [ANY] * len(extra),
        out_specs=(_SEM, _SEM, *[_HBM] * (ns + n), pl.BlockSpec(memory_space=pltpu.VMEM)),
        input_output_aliases={i: 2 + i for i in range(ns + n)},
        compiler_params=pltpu.CompilerParams(has_side_effects=_EFFECT))(*[hbm(a) for a in passed], *extra)


def exchange_wait(mode, started, after, name):
    send_sems, recv_sems = started[0], started[1]
    passed = list(started[2:-1])
    n = len(passed) if mode == "forward_half" else len(passed) // 2
    ns = len(passed) - n

    def body(*refs):
        ins, lnd = refs[:ns], refs[ns:ns + n]
        for cp in _chip_copies(mode, ins, lnd, refs[ns + n], refs[ns + n + 1]):
            cp.wait_send()
            cp.wait_recv()

    outs = pl.pallas_call(
        body, name=name, out_shape=tuple(pltpu.HBM(a.shape, a.dtype) for a in passed),
        in_specs=[_HBM] * (ns + n) + [_SEM, _SEM, ANY], out_specs=tuple([_HBM] * (ns + n)),
        input_output_aliases={i: i for i in range(ns + n)},
        compiler_params=pltpu.CompilerParams(has_side_effects=_EFFECT))(*passed, send_sems, recv_sems, after)
    return list(outs[:ns]), list(outs[ns:])


def plane_sum(slot, full, recv, name, bm=256):
    _, m, n = full.shape
    bm = _row_block(m, bm)

    def body(slot_ref, o_ref, r_ref, s_ref):
        s_ref[...] = ((o_ref[...] + r_ref[0].astype(F32)) + r_ref[1].astype(F32)) + r_ref[2].astype(F32)

    return pl.pallas_call(
        body,
        grid_spec=pltpu.PrefetchScalarGridSpec(
            num_scalar_prefetch=1, grid=(m // bm,),
            in_specs=[pl.BlockSpec((None, bm, n), lambda i, s: (s[0], i, 0)),
                      pl.BlockSpec((3, bm, n), lambda i, s: (0, i, 0))],
            out_specs=pl.BlockSpec((bm, n), lambda i, s: (i, 0))),
        out_shape=jax.ShapeDtypeStruct((m, n), F32), compiler_params=_params(1), name=name)(slot, full, recv)


def sum_slots(parts, name, bm=312):
    _, r, n = parts.shape
    bm = bm if r % bm == 0 else r

    def body(p_ref, s_ref):
        acc = p_ref[0]
        for k in range(1, 8):
            acc = acc + p_ref[k]
        s_ref[...] = acc

    return pl.pallas_call(
        body, grid=(r // bm,), in_specs=[pl.BlockSpec((8, bm, n), lambda i: (0, i, 0))],
        out_specs=pl.BlockSpec((bm, n), lambda i: (i, 0)), out_shape=jax.ShapeDtypeStruct((r, n), F32),
        compiler_params=_params(1), name=name)(parts)


def _adamw_math(w, g, m, v):
    m = ADAM_B1 * m + (1.0 - ADAM_B1) * g
    v = ADAM_B2 * v + (1.0 - ADAM_B2) * (g * g)
    m_hat = m / (1.0 - ADAM_B1 ** ADAM_STEP)
    v_hat = v / (1.0 - ADAM_B2 ** ADAM_STEP)
    delta = -ADAM_LR * (m_hat / (jnp.sqrt(v_hat) + ADAM_EPS) + ADAM_WD * w)
    return delta, m, v


def adamw(w, m, v, grads, layer, prev, name, bm=256, row0=0):
    _, _, n = w.shape
    mm = grads[0].shape[0]
    bm = _row_block(mm, bm)
    first = row0 // bm
    ng = len(grads)

    def body(*refs):
        w_ref, m_ref, v_ref = refs[:3]
        g_refs = refs[3:3 + ng]
        g_out, d_out, m_out, v_out = refs[-4:]
        g = g_refs[0][...]
        for gr in g_refs[1:]:
            g = g + gr[...]
        delta, mn, vn = _adamw_math(w_ref[...], g, m_ref[...], v_ref[...])
        g_out[...] = g
        d_out[...] = delta
        m_out[...] = mn
        v_out[...] = vn

    slab = pl.BlockSpec((None, bm, n), lambda i: (layer, first + i, 0))
    flat = pl.BlockSpec((bm, n), lambda i: (i, 0))
    in_specs = [slab] * 3 + [flat] * ng
    args = [w, m, v, *grads]
    aliases = {}
    if prev is not None:
        in_specs += [ANY] * 4
        aliases = {3 + ng + q: q for q in range(4)}
        args += list(prev)
    return pl.pallas_call(
        body, grid=(mm // bm,), in_specs=in_specs, out_specs=[slab] * 4,
        out_shape=[jax.ShapeDtypeStruct(w.shape, F32)] * 4, input_output_aliases=aliases,
        compiler_params=_params(1), name=name)(*args)


def _pack_rows(parts, rows):
    flat = jnp.concatenate([p.reshape(-1) for p in parts])
    return jnp.pad(flat, (0, rows * 128 - flat.shape[0])).reshape(rows, 128)


def _as_shards(a, rows):
    return a.reshape(N_SHARD, rows, a.shape[-1])


def _local_step(x, pos_col, target, gains, pool_w, pool_scale, gn_gain, conv_w, conv_b, weights, send_grads):
    def gain(l, n, token=None):
        g = gains[l, n].reshape(1, D_MODEL)
        return g if token is None else g + token[0:1, 0:1]

    inv_freq = (ROPE_BASE ** (-jnp.arange(0, RET_QK, 2, dtype=F32) / RET_QK)).reshape(1, RET_QK // 2)
    intra, cross, inner, chunk_decay = _decay_tables()
    dn_rows = D_FF // N_SHARD

    x1 = pool_fwd(x, gain(0, 0), gain(0, 1), pool_w, pool_scale)
    cos, sin = rope_tables(pos_col, inv_freq, x1)
    w_up0, w_dn0 = weights("mlp0", cos)
    w_dn0 = w_dn0.reshape(D_FF, D_MODEL)
    x2, u0, s0, a0, f0, h0 = mlp_fwd(x1, gain(0, 2), gain(0, 3), w_up0, w_dn0, conv_w[0], conv_b[0])
    w_in, w_out = weights("ret", x2)
    w_out = w_out.reshape(RET_HEADS * RET_V, D_MODEL)
    pj, hr = ret_proj(x2, gain(1, 0), w_in, cos, sin)
    o, sprev = ret_core_fwd(pj, intra, cross, inner, chunk_decay)
    x3, yb, out = ret_out_fwd(o, pj, x2, gn_gain, gain(1, 1), w_out)
    w_up1, w_dn1 = weights("mlp1", x3)
    w_dn1 = w_dn1.reshape(D_FF, D_MODEL)
    dx4, u1, s1, a1, f1, h1, loss = mlp_fwd(x3, gain(1, 2), gain(1, 3), w_up1, w_dn1, conv_w[1], conv_b[1], target)

    dx3, du1, df1, dg12, dg13, dcw1, dcb1 = mlp_bwd(
        dx4, f1, x3, u1, s1, gain(1, 2), gain(1, 3), w_up1, w_dn1, conv_w[1])
    dwup1 = grad_matmul(h1, du1, D_MODEL, FF_CHUNK, "grad_w_up_1", interleaved=True)
    dwdn1 = grad_matmul(a1, df1, FF_CHUNK, D_MODEL, "grad_w_down_1")
    tok = send_grads("mlp1", [dwup1, [_as_shards(g, dn_rows) for g in dwdn1]])
    dout, dpj, do, dg11, dgn = ret_out_bwd(dx3, out, o, pj, gn_gain, gain(1, 1, tok), w_out)
    dwout = grad_matmul(yb, dout, 1024, D_MODEL, "grad_w_out")
    dpj = ret_core_bwd(pj, do, sprev, cos, sin, dpj, intra, cross, inner, chunk_decay)
    dwin = grad_matmul(hr, dpj, D_MODEL, RET_IN_SHARD, "grad_w_in")
    tok = send_grads("ret", [dwin, [_as_shards(g, RET_V) for g in dwout]])
    dx2, dg10 = ret_in_bwd(dpj, dx3, x2, gain(1, 0, tok), w_in)
    dx1, du0, df0, dg02, dg03, dcw0, dcb0 = mlp_bwd(
        dx2, f0, x1, u0, s0, gain(0, 2), gain(0, 3), w_up0, w_dn0, conv_w[0])
    dwdn0 = grad_matmul(a0, df0, FF_CHUNK, D_MODEL, "grad_w_down_0")
    tok = send_grads("down0", [[_as_shards(g, dn_rows) for g in dwdn0]])
    half = D_MODEL // 2
    for part, first in (("a", 0), ("b", half)):
        dwup0 = grad_matmul(h0, du0, half, FF_CHUNK, "grad_w_up_0" + part, tk=4096, interleaved=True, after=tok,
                            cols=(first, half))
        tok = send_grads("up0" + part, [dwup0])
    dx0, dg00, dg01, dpscale, dpw = pool_bwd(dx1, x, gain(0, 0, tok), gain(0, 1), pool_w, pool_scale)

    rows = lambda g: jnp.sum(g, axis=0, keepdims=True)
    dgains = jnp.concatenate([dg00, dg01, rows(dg02), rows(dg03), dg10, dg11, rows(dg12), rows(dg13)],
                             axis=0).reshape(2, 4, D_MODEL)
    small = {"gains": dgains, "pool_scale": dpscale, "gn": dgn,
             "conv_w": jnp.sum(jnp.stack([dcw0, dcw1]), axis=3),
             "conv_b": jnp.sum(jnp.stack([dcb0, dcb1]), axis=2, keepdims=True), "pool_w": dpw}
    return loss, dx0, small


def kernel(x, positions, norm_gain, pool_w, pool_scale, ret_w_in, ret_gn_gain, ret_w_out, mlp_w_up, mlp_conv_w, mlp_conv_b, mlp_w_down, loss_target, m_norm_gain, m_pool_w, m_pool_scale, m_ret_w_in, m_ret_gn_gain, m_ret_w_out, m_mlp_w_up, m_mlp_conv_w, m_mlp_conv_b, m_mlp_w_down, v_norm_gain, v_pool_w, v_pool_scale, v_ret_w_in, v_ret_gn_gain, v_ret_w_out, v_mlp_w_up, v_mlp_conv_w, v_mlp_conv_b, v_mlp_w_down):
    t = x.shape[1]
    me = 2 * lax.axis_index("x") + lax.axis_index("y")
    me_slot = jnp.reshape(me, (1,)).astype(jnp.int32)

    small_parts = [norm_gain, ret_gn_gain, mlp_conv_w, pool_w]
    small_sizes = [p.size for p in small_parts]
    small_rows = -(-sum(small_sizes) // (128 * 8)) * 8
    gathers = {}

    def start_gather(group, srcs, after):
        lands = [lax.dynamic_update_index_in_dim(lax.empty((N_SHARD,) + s.shape, s.dtype), s, me, 0) for s in srcs]
        mode = "gather_half" if group == "mlp0" else "gather"
        gathers[group] = (mode, exchange_start(mode, srcs, lands, "gather_start_" + group, after=after))
        return gathers[group][1][-1]

    token = start_gather("small", [_pack_rows(small_parts, small_rows)], None)
    token = start_gather("mlp0", [mlp_w_up[0].astype(BF16), mlp_w_down[0].astype(BF16)], token)

    def weights(group, after):
        if group == "mlp0":
            tok = start_gather("ret", [ret_w_in[0].astype(BF16), ret_w_out[0].astype(BF16)], after)
            after = start_gather("mlp1", [mlp_w_up[1].astype(BF16), mlp_w_down[1].astype(BF16)], tok)
        mode, started = gathers[group]
        _, lands = exchange_wait(mode, started, after, "gather_wait_" + group)
        if mode == "gather_half":
            forward = exchange_start("forward_half", [], lands, "forward_start_" + group)
            _, lands = exchange_wait("forward_half", forward, forward[-1], "forward_wait_" + group)
        return lands

    sent, early = {}, {}

    def reduced(group, after, names):
        started, own = sent[group]
        _, recv = exchange_wait("scatter", started, after, "scatter_wait_" + group)
        return [plane_sum(me_slot, f, r, "plane_sum_" + nm)
                for f, r, nm in zip(own, recv, names)]

    def swap_start(planes, name):
        return exchange_start("swap", planes, [lax.empty(p.shape, p.dtype) for p in planes], name)

    def send_grads(group, pairs):
        lands = [lax.empty((3,) + b.shape[1:], BF16) for _, b in pairs]
        sent[group] = (exchange_start("scatter", [b for _, b in pairs], lands, "scatter_start_" + group),
                       [f for f, _ in pairs])
        token = sent[group][0][-1]
        if group == "down0":
            marker = pairs[0][1]
            early["planes"] = (reduced("mlp1", marker, ["w_up_1", "w_down_1"])
                               + reduced("ret", marker, ["w_in", "w_out"]))
            early["swap"] = swap_start(early["planes"], "swap_start_a")
            token = token + early["swap"][-1]
        return token

    (smallg,) = weights("small", token)
    smallg = smallg.reshape(N_SHARD, -1)
    offs = np.cumsum([0] + small_sizes)
    piece = lambda i, shape: smallg[:, offs[i]:offs[i + 1]].reshape((N_SHARD,) + shape)
    gains = piece(0, (2, 4, 256)).transpose(1, 2, 0, 3).reshape(2, 4, D_MODEL)
    gn_full = piece(1, (512,)).reshape(1, RET_HEADS * RET_V)
    cw_full = piece(2, (2, 3, FF_CHUNK)).transpose(1, 0, 2, 3)
    pw_full = piece(3, (4, 64, 256)).transpose(1, 0, 2, 3).reshape(4, 256, 256).astype(BF16)
    cb_full = mlp_conv_b.reshape(2, N_SHARD, 1, FF_CHUNK)

    loss, dx0, small = _local_step(
        x[0], positions.reshape(t, 1).astype(F32), loss_target[0], gains, pw_full, pool_scale, gn_full,
        cw_full, cb_full, weights, send_grads)

    def small_adamw(w, m, v, grads, name):
        w3 = w.reshape(1, -1, w.shape[-1])
        out = adamw(w3, m.reshape(w3.shape), v.reshape(w3.shape), [g.reshape(w3.shape[1:]) for g in grads], 0, None, name)
        return [o.reshape(w.shape) for o in out]

    pw_f = small["pool_w"].reshape(4, N_SHARD, 64, 256).transpose(1, 0, 2, 3).reshape(N_SHARD, 256, 256)
    small_order = ["gains", "pool_scale", "gn", "conv_w", "conv_b"]
    gsmall_sizes = [small[k].size for k in small_order]
    gsmall_rows = -(-sum(gsmall_sizes) // (128 * 8)) * 8
    gpack = _pack_rows([small[k] for k in small_order], gsmall_rows)
    mine = 2 * me + lax.axis_index("c")
    small_started = exchange_start(
        "everyone", [gpack], [lax.dynamic_update_index_in_dim(lax.empty((8,) + gpack.shape, F32), gpack, mine, 0)],
        "small_start")
    send_grads("pool_w", [(pw_f, pw_f.astype(BF16))])

    res = {}
    planes_a, others_a = exchange_wait("swap", early["swap"], small_started[-1], "swap_wait_a")
    res["ret_w_in"] = adamw(ret_w_in, m_ret_w_in, v_ret_w_in, (planes_a[2], others_a[2]), 0, None, "adamw_w_in")
    res["ret_w_out"] = adamw(ret_w_out, m_ret_w_out, v_ret_w_out, (planes_a[3], others_a[3]), 0, None, "adamw_w_out")
    up1 = adamw(mlp_w_up, m_mlp_w_up, v_mlp_w_up, (planes_a[0], others_a[0]), 1, None, "adamw_w_up_1")
    dn1 = adamw(mlp_w_down, m_mlp_w_down, v_mlp_w_down, (planes_a[1], others_a[1]), 1, None, "adamw_w_down_1")

    planes_b = (reduced("up0a", dn1[0], ["w_up_0a"]) + reduced("up0b", dn1[0], ["w_up_0b"])
                + reduced("down0", dn1[0], ["w_down_0"]) + reduced("pool_w", dn1[0], ["pool_w"]))
    swap_b = swap_start(planes_b, "swap_start_b")

    _, (small_recv,) = exchange_wait("everyone", small_started, swap_b[-1], "small_wait")
    gsmall = sum_slots(small_recv, "sum_small").reshape(-1)
    goffs = np.cumsum([0] + gsmall_sizes)
    gpiece = lambda i: gsmall[goffs[i]:goffs[i + 1]].reshape(small[small_order[i]].shape)
    g_gains = lax.dynamic_slice_in_dim(gpiece(0), me * 256, 256, axis=2)
    g_gn = lax.dynamic_slice_in_dim(gpiece(2), me * RET_V, RET_V, axis=1)
    g_cw = lax.dynamic_index_in_dim(gpiece(3), me, 1, keepdims=False)
    res["norm_gain"] = small_adamw(norm_gain, m_norm_gain, v_norm_gain, [g_gains], "adamw_norm_gain")
    res["pool_scale"] = small_adamw(pool_scale, m_pool_scale, v_pool_scale, [gpiece(1)], "adamw_pool_scale")
    res["ret_gn_gain"] = small_adamw(ret_gn_gain, m_ret_gn_gain, v_ret_gn_gain, [g_gn], "adamw_gn_gain")
    res["mlp_conv_w"] = small_adamw(mlp_conv_w, m_mlp_conv_w, v_mlp_conv_w, [g_cw], "adamw_conv_w")
    res["mlp_conv_b"] = small_adamw(mlp_conv_b, m_mlp_conv_b, v_mlp_conv_b, [gpiece(4)], "adamw_conv_b")

    planes_b, others_b = exchange_wait("swap", swap_b, res["mlp_conv_b"][0], "swap_wait_b")
    up0a = adamw(mlp_w_up, m_mlp_w_up, v_mlp_w_up, (planes_b[0], others_b[0]), 0, up1, "adamw_w_up_0a")
    res["mlp_w_up"] = adamw(mlp_w_up, m_mlp_w_up, v_mlp_w_up, (planes_b[1], others_b[1]), 0, up0a, "adamw_w_up_0b",
                            row0=D_MODEL // 2)
    res["mlp_w_down"] = adamw(mlp_w_down, m_mlp_w_down, v_mlp_w_down, (planes_b[2], others_b[2]), 0, dn1,
                              "adamw_w_down_0")
    res["pool_w"] = small_adamw(pool_w, m_pool_w, v_pool_w, (planes_b[3], others_b[3]), "adamw_pool_w")

    order = ["norm_gain", "pool_w", "pool_scale", "ret_w_in", "ret_gn_gain", "ret_w_out", "mlp_w_up", "mlp_conv_w",
             "mlp_conv_b", "mlp_w_down"]
    total_loss = lax.psum(loss[0, 0], ("x", "y", "c"))
    outs = [total_loss, dx0.reshape(x.shape)]
    for q in range(4):
        outs += [res[k][q] for k in order]
    return tuple(outs)
```

```python
import numpy as np
import jax
import jax.numpy as jnp
from jax import lax
from jax.experimental import pallas as pl
from jax.experimental.pallas import tpu as pltpu

F32 = jnp.float32
BF16 = jnp.bfloat16

D_MODEL = 1024
D_FF = 2816
FF_CHUNK = 1408
N_SHARD = 4
POOL_WINDOWS = (2, 4, 8, 16)
POOL_DIM = 256
POOL_HALO = 16
RET_HEADS = 4
RET_QK = 256
RET_V = 512
RET_CHUNK = 256
RET_IN = 6144
RET_IN_SHARD = 1536
ROPE_BASE = 10000.0
EPS = 1e-6
CONV_HALO = 8

ADAM_LR, ADAM_B1, ADAM_B2, ADAM_EPS, ADAM_WD, ADAM_STEP = 0.001, 0.9, 0.999, 1e-08, 0.01, 10

VMEM_LIMIT = 56 * 1024 * 1024
VMEM_LIMIT_MLP_BWD = 62 * 1024 * 1024
MESH = pl.DeviceIdType.MESH
ANY = pl.BlockSpec(memory_space=pl.ANY)


def _params(n_grid=1, limit=VMEM_LIMIT):
    return pltpu.CompilerParams(dimension_semantics=("arbitrary",) * n_grid, vmem_limit_bytes=limit)


def _dot(a, b):
    return jnp.dot(a, b, preferred_element_type=F32)


def _dot_nt(a, b):
    return lax.dot_general(a, b, (((1,), (1,)), ((), ())), preferred_element_type=F32)


def _dot_tn(a, b):
    return lax.dot_general(a, b, (((0,), (0,)), ((), ())), preferred_element_type=F32)


def _rms_fwd(x, gain):
    r = lax.rsqrt(jnp.mean(x * x, axis=-1, keepdims=True) + EPS)
    xh = x * r
    return xh * gain, xh, r


def _rms_bwd(xh, r, gain, dy):
    dxh = dy * gain
    return r * (dxh - xh * jnp.mean(dxh * xh, axis=-1, keepdims=True))


def _colsum(v):
    return jnp.sum(v, axis=0, keepdims=True)


def _full(shape):
    nd = len(shape)
    return pl.BlockSpec(shape, lambda *_: (0,) * nd)


def rope_tables(pos_col, inv_freq, after):
    t = pos_col.shape[0]
    tm = min(t, 1024)

    def body(p_ref, f_ref, after_ref, c_ref, s_ref):
        ang = p_ref[...] * f_ref[...]
        c_ref[...] = jnp.cos(ang)
        s_ref[...] = jnp.sin(ang)

    return pl.pallas_call(
        body, grid=(t // tm,),
        in_specs=[pl.BlockSpec((tm, 1), lambda i: (i, 0)), _full((1, 128)), ANY],
        out_specs=[pl.BlockSpec((tm, 128), lambda i: (i, 0))] * 2,
        out_shape=[jax.ShapeDtypeStruct((t, 128), F32)] * 2,
        compiler_params=_params(1), name="rope_tables")(pos_col, inv_freq, after)


def _window_sums(ext, backward):
    n = ext.shape[0]
    cur, sums = ext, []
    for g, win in enumerate(POOL_WINDOWS):
        if g > 0:
            cur = cur[:, POOL_DIM:]
        half = win // 2
        cur = cur + pltpu.roll(cur, n - half if backward else half, axis=0)
        sums.append(cur[:, 0:POOL_DIM])
    return sums


def _pool_diff(h_halo, h, row0, tm):
    t_idx = row0 + lax.broadcasted_iota(jnp.int32, (tm, 1), 0)
    sums = _window_sums(jnp.concatenate([h_halo, h], axis=0), backward=False)
    parts, inv_counts = [], []
    for g, win in enumerate(POOL_WINDOWS):
        inv = 1.0 / jnp.minimum(t_idx + 1, win).astype(F32)
        parts.append(sums[g][POOL_HALO:, :] * inv - h[:, g * POOL_DIM:(g + 1) * POOL_DIM])
        inv_counts.append(inv)
    return parts, inv_counts


def pool_fwd(x, g_pre, g_post, pool_w, pool_scale, tm=512):
    t, d = x.shape
    nt = t // tm

    def body(x_ref, g0_ref, g1_ref, w_ref, sc_ref, o_ref, hext):
        i = pl.program_id(0)

        @pl.when(i == 0)
        def _():
            hext[...] = jnp.zeros((POOL_HALO, d), F32)

        xv = x_ref[...]
        h, _, _ = _rms_fwd(xv, g0_ref[...])
        parts, _ = _pool_diff(hext[...], h, i * tm, tm)
        hext[...] = h[tm - POOL_HALO:tm, :]
        ys = [_dot(parts[g].astype(BF16), w_ref[g]) for g in range(len(POOL_WINDOWS))]
        y = jnp.concatenate(ys, axis=-1) * sc_ref[...]
        m, _, _ = _rms_fwd(y, g1_ref[...])
        o_ref[...] = xv + m

    row = pl.BlockSpec((tm, d), lambda i: (i, 0))
    return pl.pallas_call(
        body, grid=(nt,),
        in_specs=[row, _full((1, d)), _full((1, d)), _full(pool_w.shape), _full((1, d))],
        out_specs=row, out_shape=jax.ShapeDtypeStruct((t, d), F32),
        scratch_shapes=[pltpu.VMEM((POOL_HALO, d), F32)],
        compiler_params=_params(1), name="pool_fwd")(x, g_pre, g_post, pool_w, pool_scale)


def pool_bwd(dx1, x, g_pre, g_post, pool_w, pool_scale, tm=512):
    t, d = x.shape
    nt = t // tm
    ng = len(POOL_WINDOWS)

    def body(dx1_ref, x_ref, xh_ref, g0_ref, g1_ref, w_ref, sc_ref,
             dx_ref, dg0_ref, dg1_ref, dsc_ref, dw_ref, enext):
        i = pl.program_id(0)
        r = nt - 1 - i

        @pl.when(i == 0)
        def _():
            enext[...] = jnp.zeros((POOL_HALO, d), F32)
            dg0_ref[...] = jnp.zeros_like(dg0_ref)
            dg1_ref[...] = jnp.zeros_like(dg1_ref)
            dsc_ref[...] = jnp.zeros_like(dsc_ref)
            dw_ref[...] = jnp.zeros_like(dw_ref)

        g0 = g0_ref[...]
        g1 = g1_ref[...]
        sc = sc_ref[...]
        xv = x_ref[...]
        h, xh, rx = _rms_fwd(xv, g0)
        h_halo, _, _ = _rms_fwd(xh_ref[...], g0)
        parts, inv_counts = _pool_diff(h_halo * jnp.where(r > 0, 1.0, 0.0), h, r * tm, tm)
        parts_b = [p.astype(BF16) for p in parts]
        ypre = jnp.concatenate([_dot(parts_b[g], w_ref[g]) for g in range(ng)], axis=-1)
        _, yh, ry = _rms_fwd(ypre * sc, g1)
        dm = dx1_ref[...]
        dg1_ref[...] += _colsum(dm * yh)
        dy = _rms_bwd(yh, ry, g1, dm)
        dsc_ref[...] += _colsum(dy * ypre)
        dyp = (dy * sc).astype(BF16)
        ddiffs = []
        for g in range(ng):
            cols = slice(g * POOL_DIM, (g + 1) * POOL_DIM)
            dw_ref[g] += _dot_tn(parts_b[g], dyp[:, cols])
            ddiffs.append(_dot_nt(dyp[:, cols], w_ref[g]))
        e = jnp.concatenate([ddiffs[g] * inv_counts[g] for g in range(ng)], axis=-1)
        sums = _window_sums(jnp.concatenate([e, enext[...]], axis=0), backward=True)
        enext[...] = e[0:POOL_HALO, :]
        dh = jnp.concatenate([sums[g][0:tm, :] - ddiffs[g] for g in range(ng)], axis=-1)
        dg0_ref[...] += _colsum(dh * xh)
        dx_ref[...] = dm + _rms_bwd(xh, rx, g0, dh)

    row = pl.BlockSpec((tm, d), lambda i: (nt - 1 - i, 0))
    halo = pl.BlockSpec((POOL_HALO, d), lambda i: (jnp.maximum((nt - 1 - i) * (tm // POOL_HALO) - 1, 0), 0))
    vec = _full((1, d))
    return pl.pallas_call(
        body, grid=(nt,),
        in_specs=[row, row, halo, vec, vec, _full(pool_w.shape), vec],
        out_specs=[row, vec, vec, vec, _full((ng, POOL_DIM, POOL_DIM))],
        out_shape=[jax.ShapeDtypeStruct((t, d), F32)] + [jax.ShapeDtypeStruct((1, d), F32)] * 3
        + [jax.ShapeDtypeStruct((ng, POOL_DIM, POOL_DIM), F32)],
        scratch_shapes=[pltpu.VMEM((POOL_HALO, d), F32)],
        compiler_params=_params(1), name="pool_bwd")(dx1, x, x, g_pre, g_post, pool_w, pool_scale)


def _conv_taps(cw_ref, j):
    return cw_ref[j, 0:1, :], cw_ref[j, 1:2, :], cw_ref[j, 2:3, :]


def _row_block(m, target=256):
    if m <= target:
        return m
    for b in range(target, 7, -8):
        if m % b == 0:
            return b
    return m


def mlp_fwd(x, g_pre, g_post, w_up, w_down, conv_w, conv_b, target=None, tm=256):
    t, d = x.shape
    nt = t // tm
    h8 = CONV_HALO
    with_loss = target is not None
    n_extra = 1 if with_loss else 0

    def body(x_ref, g2_ref, g3_ref, wup_hbm, wdn_hbm, cw_ref, cb_ref, *rest):
        tgt_ref = rest[0] if with_loss else None
        xo_ref, u_ref, s_ref, a_ref, f_ref, h_ref = rest[n_extra:n_extra + 6]
        loss_ref = rest[n_extra + 6] if with_loss else None
        wup_v, wdn_v, tail, sem = rest[-4:]
        i = pl.program_id(0)

        @pl.when(i == 0)
        def _():
            c1 = pltpu.make_async_copy(wup_hbm, wup_v, sem.at[0])
            c2 = pltpu.make_async_copy(wdn_hbm, wdn_v, sem.at[1])
            c1.start()
            c2.start()
            tail[...] = jnp.zeros_like(tail)
            if with_loss:
                loss_ref[...] = jnp.zeros_like(loss_ref)
            c1.wait()
            c2.wait()

        xv = x_ref[...]
        h, _, _ = _rms_fwd(xv, g2_ref[...])
        hb = h.astype(BF16)
        h_ref[...] = hb
        acc = jnp.zeros((tm, d), F32)
        for k in range(2):
            cs = []
            for s in range(2):
                j, cols = k + 2 * s, slice((2 * k + s) * FF_CHUNK, (2 * k + s + 1) * FF_CHUNK)
                uf = _dot(hb, wup_v[j])
                u_ref[:, cols] = uf.astype(BF16)
                ext = jnp.concatenate([tail[j], uf], axis=0)
                tail[j] = uf[tm - h8:tm, :]
                w0, w1, w2 = _conv_taps(cw_ref, j)
                cs.append(cb_ref[j] + w2 * uf + w1 * pltpu.roll(ext, 1, axis=0)[h8:, :]
                          + w0 * pltpu.roll(ext, 2, axis=0)[h8:, :])
            cg, cv = cs
            sg = jax.nn.sigmoid(cg)
            sil = cg * sg
            ab = (sil * cv).astype(BF16)
            a_ref[:, k * FF_CHUNK:(k + 1) * FF_CHUNK] = ab
            s_ref[:, 2 * k * FF_CHUNK:(2 * k + 1) * FF_CHUNK] = sil.astype(BF16)
            s_ref[:, (2 * k + 1) * FF_CHUNK:(2 * k + 2) * FF_CHUNK] = (cv * (sg + sil * (1.0 - sg))).astype(BF16)
            acc = acc + _dot(ab, wdn_v[k * FF_CHUNK:(k + 1) * FF_CHUNK, :])
        f_ref[...] = acc
        y, _, _ = _rms_fwd(acc, g3_ref[...])
        if with_loss:
            err = (xv + y) - tgt_ref[...]
            xo_ref[...] = err * (1.0 / d)
            loss_ref[...] += 0.5 * jnp.sum(jnp.mean(err * err, axis=-1, keepdims=True), axis=0, keepdims=True)
        else:
            xo_ref[...] = xv + y

    row = pl.BlockSpec((tm, d), lambda i: (i, 0))
    wide = pl.BlockSpec((tm, 2 * D_FF), lambda i: (i, 0))
    vec = _full((1, d))
    extra = [target] if with_loss else []
    return pl.pallas_call(
        body, grid=(nt,),
        in_specs=[row, vec, vec, ANY, ANY, _full(conv_w.shape), _full(conv_b.shape)] + [row] * n_extra,
        out_specs=[row, wide, wide, pl.BlockSpec((tm, D_FF), lambda i: (i, 0)), row, row] + [_full((1, 1))] * n_extra,
        out_shape=[jax.ShapeDtypeStruct((t, d), F32), jax.ShapeDtypeStruct((t, 2 * D_FF), BF16),
                   jax.ShapeDtypeStruct((t, 2 * D_FF), BF16), jax.ShapeDtypeStruct((t, D_FF), BF16),
                   jax.ShapeDtypeStruct((t, d), F32), jax.ShapeDtypeStruct((t, d), BF16)]
        + [jax.ShapeDtypeStruct((1, 1), F32)] * n_extra,
        scratch_shapes=[pltpu.VMEM(w_up.shape, BF16), pltpu.VMEM(w_down.shape, BF16),
                        pltpu.VMEM((N_SHARD, h8, FF_CHUNK), F32), pltpu.SemaphoreType.DMA((2,))],
        compiler_params=_params(1), name="mlp_fwd_loss" if with_loss else "mlp_fwd")(
            x, g_pre, g_post, w_up, w_down, conv_w, conv_b, *extra)


def _rowsum8(v):
    return jnp.sum(v.reshape(v.shape[0] // 8, 8, v.shape[1]), axis=0)


def mlp_bwd(dxo, f, x, u, sp, g_pre, g_post, w_up, w_down, conv_w, tm=256):
    t, d = x.shape
    nt = t // tm
    h8 = CONV_HALO

    def body(dxo_ref, f_ref, x_ref, u_ref, s_ref, g2_ref, g3_ref, wup_hbm, wdn_hbm, cw_ref,
             dx_ref, du_ref, df_ref, dg2_ref, dg3_ref, dcw_ref, dcb_ref,
             wup_v, wdn_v, carry, sem):
        @pl.when(pl.program_id(0) == 0)
        def _():
            c1 = pltpu.make_async_copy(wup_hbm, wup_v, sem.at[0])
            c2 = pltpu.make_async_copy(wdn_hbm, wdn_v, sem.at[1])
            c1.start()
            c2.start()
            carry[...] = jnp.zeros_like(carry)
            dg2_ref[...] = jnp.zeros_like(dg2_ref)
            dg3_ref[...] = jnp.zeros_like(dg3_ref)
            dcw_ref[...] = jnp.zeros_like(dcw_ref)
            dcb_ref[...] = jnp.zeros_like(dcb_ref)
            c1.wait()
            c2.wait()

        g3 = g3_ref[...]
        dxo = dxo_ref[...]
        _, fh, rf = _rms_fwd(f_ref[...], g3)
        dg3_ref[...] += _rowsum8(dxo * fh)
        dfb = _rms_bwd(fh, rf, g3, dxo).astype(BF16)
        df_ref[...] = dfb
        dh = jnp.zeros((tm, d), F32)
        for k in range(2):
            da = _dot_nt(dfb, wdn_v[k * FF_CHUNK:(k + 1) * FF_CHUNK, :])
            for s in range(2):
                j = k + 2 * s
                cols = slice((2 * k + s) * FF_CHUNK, (2 * k + s + 1) * FF_CHUNK)
                dc = da * s_ref[:, (2 * k + 1 - s) * FF_CHUNK:(2 * k + 2 - s) * FF_CHUNK].astype(F32)
                uf = u_ref[:, cols].astype(F32)
                ext = jnp.concatenate([dc, carry[j]], axis=0)
                carry[j] = dc[0:h8, :]
                dc1 = pltpu.roll(ext, tm + h8 - 1, axis=0)[0:tm, :]
                dc2 = pltpu.roll(ext, tm + h8 - 2, axis=0)[0:tm, :]
                dcb_ref[j] += _rowsum8(dc)
                dcw_ref[j, 2] += _rowsum8(dc * uf)
                dcw_ref[j, 1] += _rowsum8(dc1 * uf)
                dcw_ref[j, 0] += _rowsum8(dc2 * uf)
                dub = (cw_ref[j, 2:3, :] * dc + cw_ref[j, 1:2, :] * dc1 + cw_ref[j, 0:1, :] * dc2).astype(BF16)
                du_ref[:, cols] = dub
                dh = dh + _dot_nt(dub, wup_v[j])
        g2 = g2_ref[...]
        _, xh, rx = _rms_fwd(x_ref[...], g2)
        dg2_ref[...] += _rowsum8(dh * xh)
        dx_ref[...] = dxo + _rms_bwd(xh, rx, g2, dh)

    row = pl.BlockSpec((tm, d), lambda i: (nt - 1 - i, 0))
    wide = pl.BlockSpec((tm, 2 * D_FF), lambda i: (nt - 1 - i, 0))
    vec = _full((1, d))
    acc = _full((8, d))
    dcw_shape, dcb_shape = (N_SHARD, 3, 8, FF_CHUNK), (N_SHARD, 8, FF_CHUNK)
    return pl.pallas_call(
        body, grid=(nt,),
        in_specs=[row, row, row, wide, wide, vec, vec, ANY, ANY, _full(conv_w.shape)],
        out_specs=[row, wide, row, acc, acc, _full(dcw_shape), _full(dcb_shape)],
        out_shape=[jax.ShapeDtypeStruct((t, d), F32), jax.ShapeDtypeStruct((t, 2 * D_FF), BF16),
                   jax.ShapeDtypeStruct((t, d), BF16),
                   jax.ShapeDtypeStruct((8, d), F32), jax.ShapeDtypeStruct((8, d), F32),
                   jax.ShapeDtypeStruct(dcw_shape, F32), jax.ShapeDtypeStruct(dcb_shape, F32)],
        scratch_shapes=[pltpu.VMEM(w_up.shape, BF16), pltpu.VMEM(w_down.shape, BF16),
                        pltpu.VMEM((N_SHARD, h8, FF_CHUNK), F32), pltpu.SemaphoreType.DMA((2,))],
        compiler_params=_params(1, VMEM_LIMIT_MLP_BWD), name="mlp_bwd")(
            dxo, f, x, u, sp, g_pre, g_post, w_up, w_down, conv_w)


def grad_matmul(a, b, bm, bn, name, tk=2048, interleaved=False, after=None, cols=None):
    t = a.shape[0]
    m0, m = (0, a.shape[1]) if cols is None else cols
    n = b.shape[1]
    tk = min(tk, t)
    nk = t // tk
    place = (lambda j: (j % 2) * 2 + j // 2) if interleaved else (lambda j: j)
    extra = [] if after is None else [after]
    first = m0 // bm

    def body(a_ref, b_ref, *rest):
        o_ref, ob_ref = rest[len(extra):]
        kk = pl.program_id(2)

        @pl.when(kk == 0)
        def _():
            o_ref[...] = jnp.zeros_like(o_ref)

        o_ref[...] += _dot_tn(a_ref[...], b_ref[...])

        @pl.when(kk == nk - 1)
        def _():
            ob_ref[...] = o_ref[...].astype(BF16)

    ospec = pl.BlockSpec((None, bm, bn), lambda j, i, kk: (place(j), i, 0))
    return pl.pallas_call(
        body, grid=(n // bn, m // bm, nk),
        in_specs=[pl.BlockSpec((tk, bm), lambda j, i, kk: (kk, first + i)),
                  pl.BlockSpec((tk, bn), lambda j, i, kk: (kk, j))]
        + [ANY] * len(extra),
        out_specs=[ospec, ospec],
        out_shape=[jax.ShapeDtypeStruct((n // bn, m, bn), F32), jax.ShapeDtypeStruct((n // bn, m, bn), BF16)],
        compiler_params=_params(3), name=name)(a, b, *extra)


def _decay_tables():
    log_gamma = jnp.log(1.0 - 2.0 ** (-5.0 - jnp.arange(RET_HEADS, dtype=F32)))
    i = jnp.arange(RET_CHUNK, dtype=F32)
    rel = i[:, None] - i[None, :]
    intra = jnp.where(rel >= 0, jnp.exp(jnp.maximum(rel, 0.0) * log_gamma[:, None, None]), 0.0)
    cross = jnp.exp((i + 1.0) * log_gamma[:, None])[:, :, None]
    inner = jnp.exp((RET_CHUNK - 1.0 - i) * log_gamma[:, None])[:, :, None]
    chunk = [float(np.exp(np.float32(RET_CHUNK) * np.log(np.float32(1.0 - 2.0 ** (-5.0 - h))).astype(np.float32)))
             for h in range(RET_HEADS)]
    return intra, cross, inner, chunk


def ret_proj(x, g_pre, w_in, cos, sin, tm=512):
    t, d = x.shape
    nt = t // tm
    per = RET_IN_SHARD // RET_QK

    def body(x_ref, g_ref, win_hbm, c_ref, s_ref, pj_ref, h_ref, win_v, sem):
        @pl.when(pl.program_id(0) == 0)
        def _():
            cp = pltpu.make_async_copy(win_hbm, win_v, sem)
            cp.start()
            cp.wait()

        h, _, _ = _rms_fwd(x_ref[...], g_ref[...])
        hb = h.astype(BF16)
        h_ref[...] = hb
        c = c_ref[...]
        s = s_ref[...]
        for j in range(N_SHARD):
            pjj = _dot(hb, win_v[j])
            for bb in range(per):
                b = per * j + bb
                blk = pjj[:, bb * RET_QK:(bb + 1) * RET_QK]
                if b < 2 * RET_HEADS:
                    x1, x2 = blk[:, :128], blk[:, 128:]
                    o1 = x1 * c - x2 * s
                    o2 = x2 * c + x1 * s
                    if b < RET_HEADS:
                        o1 = o1 * (RET_QK ** -0.5)
                        o2 = o2 * (RET_QK ** -0.5)
                    pj_ref[:, b * RET_QK:b * RET_QK + 128] = o1.astype(BF16)
                    pj_ref[:, b * RET_QK + 128:(b + 1) * RET_QK] = o2.astype(BF16)
                else:
                    pj_ref[:, b * RET_QK:(b + 1) * RET_QK] = blk.astype(BF16)

    row = pl.BlockSpec((tm, d), lambda i: (i, 0))
    tab = pl.BlockSpec((tm, 128), lambda i: (i, 0))
    return pl.pallas_call(
        body, grid=(nt,),
        in_specs=[row, _full((1, d)), ANY, tab, tab],
        out_specs=[pl.BlockSpec((tm, RET_IN), lambda i: (i, 0)), row],
        out_shape=[jax.ShapeDtypeStruct((t, RET_IN), BF16), jax.ShapeDtypeStruct((t, d), BF16)],
        scratch_shapes=[pltpu.VMEM(w_in.shape, BF16), pltpu.SemaphoreType.DMA],
        compiler_params=_params(1), name="ret_proj")(x, g_pre, w_in, cos, sin)


def ret_core_fwd(pj, intra, cross, inner, chunk_decay):
    t = pj.shape[0]
    nc = t // RET_CHUNK
    c = RET_CHUNK
    qk_all = RET_HEADS * RET_QK
    v_all = RET_HEADS * RET_V

    def body(q_ref, k_ref, v_ref, dm_ref, cr_ref, in_ref, o_ref, sp_ref, state):
        @pl.when(pl.program_id(0) == 0)
        def _():
            state[...] = jnp.zeros_like(state)

        for h in range(RET_HEADS):
            q = q_ref[:, h * RET_QK:(h + 1) * RET_QK]
            k = k_ref[:, h * RET_QK:(h + 1) * RET_QK]
            v = v_ref[:, h * RET_V:(h + 1) * RET_V]
            sb = state[h].astype(BF16)
            sp_ref[h] = sb
            sc = _dot_nt(q, k) * dm_ref[h]
            o_ref[:, h * RET_V:(h + 1) * RET_V] = (_dot(sc.astype(BF16), v) + _dot(q, sb) * cr_ref[h]).astype(BF16)
            kd = (k.astype(F32) * in_ref[h]).astype(BF16)
            state[h] = state[h] * chunk_decay[h] + _dot_tn(kd, v)

    return pl.pallas_call(
        body, grid=(nc,),
        in_specs=[pl.BlockSpec((c, qk_all), lambda n: (n, 0)), pl.BlockSpec((c, qk_all), lambda n: (n, 1)),
                  pl.BlockSpec((c, v_all), lambda n: (n, 1)),
                  _full(intra.shape), _full(cross.shape), _full(inner.shape)],
        out_specs=[pl.BlockSpec((c, v_all), lambda n: (n, 0)),
                   pl.BlockSpec((None, RET_HEADS, RET_QK, RET_V), lambda n: (n, 0, 0, 0))],
        out_shape=[jax.ShapeDtypeStruct((t, v_all), BF16),
                   jax.ShapeDtypeStruct((nc, RET_HEADS, RET_QK, RET_V), BF16)],
        scratch_shapes=[pltpu.VMEM((RET_HEADS, RET_QK, RET_V), F32)],
        compiler_params=_params(1), name="ret_core_fwd")(pj, pj, pj, intra, cross, inner)


def _group_norm(o_h):
    mu = jnp.mean(o_h, axis=-1, keepdims=True)
    dev = o_h - mu
    rstd = lax.rsqrt(jnp.mean(dev * dev, axis=-1, keepdims=True) + EPS)
    return dev * rstd, rstd


def ret_out_fwd(o, pj, x, gn_gain, g_post, w_out, tm=512):
    t, d = x.shape
    nt = t // tm
    v_all = RET_HEADS * RET_V

    def body(o_ref, g_ref, x_ref, gn_ref, g1_ref, w_ref, xo_ref, y_ref, out_ref):
        out = jnp.zeros((tm, d), F32)
        for h in range(RET_HEADS):
            cols = slice(h * RET_V, (h + 1) * RET_V)
            ohat, _ = _group_norm(o_ref[:, cols].astype(F32))
            g = g_ref[:, cols].astype(F32)
            yb = (g * jax.nn.sigmoid(g) * (ohat * gn_ref[:, cols])).astype(BF16)
            y_ref[:, cols] = yb
            out = out + _dot(yb, w_ref[cols, :])
        out_ref[...] = out
        m, _, _ = _rms_fwd(out, g1_ref[...])
        xo_ref[...] = x_ref[...] + m

    row = pl.BlockSpec((tm, d), lambda i: (i, 0))
    wide = pl.BlockSpec((tm, v_all), lambda i: (i, 0))
    return pl.pallas_call(
        body, grid=(nt,),
        in_specs=[wide, pl.BlockSpec((tm, v_all), lambda i: (i, 2)), row, _full((1, v_all)), _full((1, d)),
                  _full(w_out.shape)],
        out_specs=[row, wide, row],
        out_shape=[jax.ShapeDtypeStruct((t, d), F32), jax.ShapeDtypeStruct((t, v_all), BF16),
                   jax.ShapeDtypeStruct((t, d), F32)],
        compiler_params=_params(1), name="ret_out_fwd")(o, pj, x, gn_gain, g_post, w_out)


def ret_out_bwd(dxo, out, o, pj, gn_gain, g_post, w_out, tm=512):
    t, d = out.shape
    nt = t // tm
    v_all = RET_HEADS * RET_V

    def body(dxo_ref, out_ref, o_ref, g_ref, gn_ref, g1_ref, w_ref,
             dout_ref, dgate_ref, do_ref, dg1_ref, dgn_ref):
        @pl.when(pl.program_id(0) == 0)
        def _():
            dg1_ref[...] = jnp.zeros_like(dg1_ref)
            dgn_ref[...] = jnp.zeros_like(dgn_ref)

        g1 = g1_ref[...]
        dxo = dxo_ref[...]
        _, oh_, r_ = _rms_fwd(out_ref[...], g1)
        dg1_ref[...] += _colsum(dxo * oh_)
        doutb = _rms_bwd(oh_, r_, g1, dxo).astype(BF16)
        dout_ref[...] = doutb
        for h in range(RET_HEADS):
            cols = slice(h * RET_V, (h + 1) * RET_V)
            gn = gn_ref[:, cols]
            ohat, rstd = _group_norm(o_ref[:, cols].astype(F32))
            g = g_ref[:, cols].astype(F32)
            sg = jax.nn.sigmoid(g)
            dyh = _dot_nt(doutb, w_ref[cols, :])
            sil = g * sg
            dgate_ref[:, cols] = (dyh * (ohat * gn) * (sg + sil * (1.0 - sg))).astype(BF16)
            don = dyh * sil
            dgn_ref[:, cols] += _colsum(don * ohat)
            dohat = don * gn
            do_ref[:, cols] = (rstd * (dohat - jnp.mean(dohat, axis=-1, keepdims=True)
                                       - ohat * jnp.mean(dohat * ohat, axis=-1, keepdims=True))).astype(BF16)

    row = pl.BlockSpec((tm, d), lambda i: (i, 0))
    wide = pl.BlockSpec((tm, v_all), lambda i: (i, 0))
    gate = pl.BlockSpec((tm, v_all), lambda i: (i, 2))
    return pl.pallas_call(
        body, grid=(nt,),
        in_specs=[row, row, wide, gate, _full((1, v_all)), _full((1, d)), _full(w_out.shape)],
        out_specs=[row, gate, wide, _full((1, d)), _full((1, v_all))],
        out_shape=[jax.ShapeDtypeStruct((t, d), BF16), jax.ShapeDtypeStruct((t, RET_IN), BF16),
                   jax.ShapeDtypeStruct((t, v_all), BF16), jax.ShapeDtypeStruct((1, d), F32),
                   jax.ShapeDtypeStruct((1, v_all), F32)],
        compiler_params=_params(1), name="ret_out_bwd")(dxo, out, o, pj, gn_gain, g_post, w_out)


def ret_core_bwd(pj, do, sprev, cos, sin, dpj, intra, cross, inner, chunk_decay):
    t = pj.shape[0]
    nc = t // RET_CHUNK
    c = RET_CHUNK
    qk_all = RET_HEADS * RET_QK
    v_all = RET_HEADS * RET_V
    scale = RET_QK ** -0.5

    def body(q_ref, k_ref, v_ref, do_ref, sp_ref, c_ref, s_ref, dm_ref, cr_ref, in_ref, dpj_in, dpj_ref, dstate):
        @pl.when(pl.program_id(0) == 0)
        def _():
            dstate[...] = jnp.zeros_like(dstate)

        cs = c_ref[...]
        sn = s_ref[...]
        for h in range(RET_HEADS):
            q = q_ref[:, h * RET_QK:(h + 1) * RET_QK]
            k = k_ref[:, h * RET_QK:(h + 1) * RET_QK]
            v = v_ref[:, h * RET_V:(h + 1) * RET_V]
            doh = do_ref[:, h * RET_V:(h + 1) * RET_V]
            dm = dm_ref[h]
            ab = (_dot_nt(q, k) * dm).astype(BF16)
            dab = (_dot_nt(doh, v) * dm).astype(BF16)
            dsb = dstate[h].astype(BF16)
            kd = (k.astype(F32) * in_ref[h]).astype(BF16)
            dv = _dot_tn(ab, doh) + _dot(kd, dsb)
            dq = _dot(dab, k) + cr_ref[h] * _dot_nt(doh, sp_ref[h])
            dk = _dot_tn(dab, q) + in_ref[h] * _dot_nt(v, dsb)
            qd = (q.astype(F32) * cr_ref[h]).astype(BF16)
            dstate[h] = dstate[h] * chunk_decay[h] + _dot_tn(qd, doh)
            for base, dd, sc in ((h * RET_QK, dq, scale), (qk_all + h * RET_QK, dk, 1.0)):
                d1, d2 = dd[:, :128], dd[:, 128:]
                dpj_ref[:, base:base + 128] = ((d1 * cs + d2 * sn) * sc).astype(BF16)
                dpj_ref[:, base + 128:base + RET_QK] = ((d2 * cs - d1 * sn) * sc).astype(BF16)
            dpj_ref[:, 2 * qk_all + h * RET_V:2 * qk_all + (h + 1) * RET_V] = dv.astype(BF16)

    rev = lambda n: nc - 1 - n
    tab = pl.BlockSpec((c, 128), lambda n: (rev(n), 0))
    return pl.pallas_call(
        body, grid=(nc,),
        in_specs=[pl.BlockSpec((c, qk_all), lambda n: (rev(n), 0)), pl.BlockSpec((c, qk_all), lambda n: (rev(n), 1)),
                  pl.BlockSpec((c, v_all), lambda n: (rev(n), 1)), pl.BlockSpec((c, v_all), lambda n: (rev(n), 0)),
                  pl.BlockSpec((None, RET_HEADS, RET_QK, RET_V), lambda n: (rev(n), 0, 0, 0)),
                  tab, tab, _full(intra.shape), _full(cross.shape), _full(inner.shape), ANY],
        out_specs=pl.BlockSpec((c, 2 * qk_all + v_all), lambda n: (rev(n), 0)),
        out_shape=jax.ShapeDtypeStruct((t, RET_IN), BF16),
        scratch_shapes=[pltpu.VMEM((RET_HEADS, RET_QK, RET_V), F32)],
        input_output_aliases={10: 0},
        compiler_params=_params(1), name="ret_core_bwd")(pj, pj, pj, do, sprev, cos, sin, intra, cross, inner, dpj)


def ret_in_bwd(dpj, dres, x, g_pre, w_in, tm=512):
    t, d = x.shape
    nt = t // tm

    def body(dpj_ref, dres_ref, x_ref, g_ref, win_hbm, dx_ref, dg_ref, win_v, sem):
        @pl.when(pl.program_id(0) == 0)
        def _():
            cp = pltpu.make_async_copy(win_hbm, win_v, sem)
            cp.start()
            dg_ref[...] = jnp.zeros_like(dg_ref)
            cp.wait()

        g = g_ref[...]
        dh = jnp.zeros((tm, d), F32)
        for j in range(N_SHARD):
            dh = dh + _dot_nt(dpj_ref[:, j * RET_IN_SHARD:(j + 1) * RET_IN_SHARD], win_v[j])
        _, xh, rx = _rms_fwd(x_ref[...], g)
        dg_ref[...] += _colsum(dh * xh)
        dx_ref[...] = dres_ref[...] + _rms_bwd(xh, rx, g, dh)

    row = pl.BlockSpec((tm, d), lambda i: (i, 0))
    return pl.pallas_call(
        body, grid=(nt,),
        in_specs=[pl.BlockSpec((tm, RET_IN), lambda i: (i, 0)), row, row, _full((1, d)), ANY],
        out_specs=[row, _full((1, d))],
        out_shape=[jax.ShapeDtypeStruct((t, d), F32), jax.ShapeDtypeStruct((1, d), F32)],
        scratch_shapes=[pltpu.VMEM(w_in.shape, BF16), pltpu.SemaphoreType.DMA],
        compiler_params=_params(1), name="ret_in_bwd")(dpj, dres, x, g_pre, w_in)


_CHIP_FLIPS = ((1, 0), (0, 1), (1, 1))


def _flip(v, b):
    return 1 - v if b else v


_HBM = pl.BlockSpec(memory_space=pltpu.HBM)
_SEM = pl.BlockSpec(memory_space=pltpu.SEMAPHORE)
_EFFECT = pltpu.SideEffectType.DATAFLOW_SIDE_EFFECTING


def _chip_copies(mode, srcs, lands, send_sems, recv_sems):
    x, y, c = lax.axis_index("x"), lax.axis_index("y"), lax.axis_index("c")
    copies = []
    for t in range(len(lands)):
        if mode == "swap":
            copies.append(pltpu.make_async_remote_copy(
                src_ref=srcs[t], dst_ref=lands[t], send_sem=send_sems.at[t], recv_sem=recv_sems.at[t],
                device_id=(x, y, 1 - c), device_id_type=MESH))
            continue
        if mode == "everyone":
            for m in range(1, 8):
                bx, by, bc = (m >> 2) & 1, (m >> 1) & 1, m & 1
                copies.append(pltpu.make_async_remote_copy(
                    src_ref=srcs[t], dst_ref=lands[t].at[4 * x + 2 * y + c], send_sem=send_sems.at[7 * t + m - 1],
                    recv_sem=recv_sems.at[7 * t + m - 1], device_id=(_flip(x, bx), _flip(y, by), _flip(c, bc)),
                    device_id_type=MESH))
            continue
        for k, (bx, by) in enumerate(_CHIP_FLIPS):
            px, py = _flip(x, bx), _flip(y, by)
            target = (px, py, c)
            if mode == "gather":
                src, dst = srcs[t], lands[t].at[2 * x + y]
            elif mode == "gather_half":
                half = pl.ds(c * (srcs[t].shape[0] // 2), srcs[t].shape[0] // 2)
                src, dst = srcs[t].at[half], lands[t].at[2 * x + y, half]
            elif mode == "forward_half":
                half = pl.ds(c * (lands[t].shape[1] // 2), lands[t].shape[1] // 2)
                src = dst = lands[t].at[2 * px + py, half]
                target = (x, y, 1 - c)
            else:
                src, dst = srcs[t].at[2 * px + py], lands[t].at[k]
            copies.append(pltpu.make_async_remote_copy(
                src_ref=src, dst_ref=dst, send_sem=send_sems.at[3 * t + k], recv_sem=recv_sems.at[3 * t + k],
                device_id=target, device_id_type=MESH))
    return copies


def exchange_start(mode, srcs, lands, name, after=None):
    n, ns = len(lands), len(srcs)
    extra = [] if after is None else [after]

    def body(*refs):
        ins, lnd = refs[:ns], refs[ns:ns + n]
        send_sems, recv_sems = refs[ns + n + len(extra)], refs[ns + n + len(extra) + 1]
        token = refs[-1]
        for cp in _chip_copies(mode, ins, lnd, send_sems, recv_sems):
            cp.start()
        token[...] = jnp.zeros(token.shape, token.dtype)

    hbm = lambda a: pltpu.with_memory_space_constraint(a, pltpu.HBM)
    passed = list(srcs) + list(lands)
    n_sem = {"swap": 1, "everyone": 7}.get(mode, 3) * n
    return pl.pallas_call(
        body, name=name,
        out_shape=(pltpu.SemaphoreType.DMA((n_sem,)), pltpu.SemaphoreType.DMA((n_sem,)),
                   *[pltpu.HBM(a.shape, a.dtype) for a in passed], jax.ShapeDtypeStruct((8, 128), F32)),
        in_specs=[_HBM] * (ns + n) + [ANY] * len(extra),
        out_specs=(_SEM, _SEM, *[_HBM] * (ns + n), pl.BlockSpec(memory_space=pltpu.VMEM)),
        input_output_aliases={i: 2 + i for i in range(ns + n)},
        compiler_params=pltpu.CompilerParams(has_side_effects=_EFFECT))(*[hbm(a) for a in passed], *extra)


def exchange_wait(mode, started, after, name):
    send_sems, recv_sems = started[0], started[1]
    passed = list(started[2:-1])
    n = len(passed) if mode == "forward_half" else len(passed) // 2
    ns = len(passed) - n

    def body(*refs):
        ins, lnd = refs[:ns], refs[ns:ns + n]
        for cp in _chip_copies(mode, ins, lnd, refs[ns + n], refs[ns + n + 1]):
            cp.wait_send()
            cp.wait_recv()

    outs = pl.pallas_call(
        body, name=name, out_shape=tuple(pltpu.HBM(a.shape, a.dtype) for a in passed),
        in_specs=[_HBM] * (ns + n) + [_SEM, _SEM, ANY], out_specs=tuple([_HBM] * (ns + n)),
        input_output_aliases={i: i for i in range(ns + n)},
        compiler_params=pltpu.CompilerParams(has_side_effects=_EFFECT))(*passed, send_sems, recv_sems, after)
    return list(outs[:ns]), list(outs[ns:])


def plane_sum(slot, full, recv, name, bm=256):
    _, m, n = full.shape
    bm = _row_block(m, bm)

    def body(slot_ref, o_ref, r_ref, s_ref):
        s_ref[...] = ((o_ref[...] + r_ref[0].astype(F32)) + r_ref[1].astype(F32)) + r_ref[2].astype(F32)

    return pl.pallas_call(
        body,
        grid_spec=pltpu.PrefetchScalarGridSpec(
            num_scalar_prefetch=1, grid=(m // bm,),
            in_specs=[pl.BlockSpec((None, bm, n), lambda i, s: (s[0], i, 0)),
                      pl.BlockSpec((3, bm, n), lambda i, s: (0, i, 0))],
            out_specs=pl.BlockSpec((bm, n), lambda i, s: (i, 0))),
        out_shape=jax.ShapeDtypeStruct((m, n), F32), compiler_params=_params(1), name=name)(slot, full, recv)


def sum_slots(parts, name, bm=312):
    _, r, n = parts.shape
    bm = bm if r % bm == 0 else r

    def body(p_ref, s_ref):
        acc = p_ref[0]
        for k in range(1, 8):
            acc = acc + p_ref[k]
        s_ref[...] = acc

    return pl.pallas_call(
        body, grid=(r // bm,), in_specs=[pl.BlockSpec((8, bm, n), lambda i: (0, i, 0))],
        out_specs=pl.BlockSpec((bm, n), lambda i: (i, 0)), out_shape=jax.ShapeDtypeStruct((r, n), F32),
        compiler_params=_params(1), name=name)(parts)


def _adamw_math(w, g, m, v):
    m = ADAM_B1 * m + (1.0 - ADAM_B1) * g
    v = ADAM_B2 * v + (1.0 - ADAM_B2) * (g * g)
    m_hat = m / (1.0 - ADAM_B1 ** ADAM_STEP)
    v_hat = v / (1.0 - ADAM_B2 ** ADAM_STEP)
    delta = -ADAM_LR * (m_hat / (jnp.sqrt(v_hat) + ADAM_EPS) + ADAM_WD * w)
    return delta, m, v


def adamw(w, m, v, grads, layer, prev, name, bm=256, row0=0):
    _, _, n = w.shape
    mm = grads[0].shape[0]
    bm = _row_block(mm, bm)
    first = row0 // bm
    ng = len(grads)

    def body(*refs):
        w_ref, m_ref, v_ref = refs[:3]
        g_refs = refs[3:3 + ng]
        g_out, d_out, m_out, v_out = refs[-4:]
        g = g_refs[0][...]
        for gr in g_refs[1:]:
            g = g + gr[...]
        delta, mn, vn = _adamw_math(w_ref[...], g, m_ref[...], v_ref[...])
        g_out[...] = g
        d_out[...] = delta
        m_out[...] = mn
        v_out[...] = vn

    slab = pl.BlockSpec((None, bm, n), lambda i: (layer, first + i, 0))
    flat = pl.BlockSpec((bm, n), lambda i: (i, 0))
    in_specs = [slab] * 3 + [flat] * ng
    args = [w, m, v, *grads]
    aliases = {}
    if prev is not None:
        in_specs += [ANY] * 4
        aliases = {3 + ng + q: q for q in range(4)}
        args += list(prev)
    return pl.pallas_call(
        body, grid=(mm // bm,), in_specs=in_specs, out_specs=[slab] * 4,
        out_shape=[jax.ShapeDtypeStruct(w.shape, F32)] * 4, input_output_aliases=aliases,
        compiler_params=_params(1), name=name)(*args)


def _pack_rows(parts, rows):
    flat = jnp.concatenate([p.reshape(-1) for p in parts])
    return jnp.pad(flat, (0, rows * 128 - flat.shape[0])).reshape(rows, 128)


def _as_shards(a, rows):
    return a.reshape(N_SHARD, rows, a.shape[-1])


def _local_step(x, pos_col, target, gains, pool_w, pool_scale, gn_gain, conv_w, conv_b, weights, send_grads):
    def gain(l, n, token=None):
        g = gains[l, n].reshape(1, D_MODEL)
        return g if token is None else g + token[0:1, 0:1]

    inv_freq = (ROPE_BASE ** (-jnp.arange(0, RET_QK, 2, dtype=F32) / RET_QK)).reshape(1, RET_QK // 2)
    intra, cross, inner, chunk_decay = _decay_tables()
    dn_rows = D_FF // N_SHARD

    x1 = pool_fwd(x, gain(0, 0), gain(0, 1), pool_w, pool_scale)
    cos, sin = rope_tables(pos_col, inv_freq, x1)
    w_up0, w_dn0 = weights("mlp0", cos)
    w_dn0 = w_dn0.reshape(D_FF, D_MODEL)
    x2, u0, s0, a0, f0, h0 = mlp_fwd(x1, gain(0, 2), gain(0, 3), w_up0, w_dn0, conv_w[0], conv_b[0])
    w_in, w_out = weights("ret", x2)
    w_out = w_out.reshape(RET_HEADS * RET_V, D_MODEL)
    pj, hr = ret_proj(x2, gain(1, 0), w_in, cos, sin)
    o, sprev = ret_core_fwd(pj, intra, cross, inner, chunk_decay)
    x3, yb, out = ret_out_fwd(o, pj, x2, gn_gain, gain(1, 1), w_out)
    w_up1, w_dn1 = weights("mlp1", x3)
    w_dn1 = w_dn1.reshape(D_FF, D_MODEL)
    dx4, u1, s1, a1, f1, h1, loss = mlp_fwd(x3, gain(1, 2), gain(1, 3), w_up1, w_dn1, conv_w[1], conv_b[1], target)

    dx3, du1, df1, dg12, dg13, dcw1, dcb1 = mlp_bwd(
        dx4, f1, x3, u1, s1, gain(1, 2), gain(1, 3), w_up1, w_dn1, conv_w[1])
    dwup1 = grad_matmul(h1, du1, D_MODEL, FF_CHUNK, "grad_w_up_1", interleaved=True)
    dwdn1 = grad_matmul(a1, df1, FF_CHUNK, D_MODEL, "grad_w_down_1")
    tok = send_grads("mlp1", [dwup1, [_as_shards(g, dn_rows) for g in dwdn1]])
    dout, dpj, do, dg11, dgn = ret_out_bwd(dx3, out, o, pj, gn_gain, gain(1, 1, tok), w_out)
    dwout = grad_matmul(yb, dout, 1024, D_MODEL, "grad_w_out")
    dpj = ret_core_bwd(pj, do, sprev, cos, sin, dpj, intra, cross, inner, chunk_decay)
    dwin = grad_matmul(hr, dpj, D_MODEL, RET_IN_SHARD, "grad_w_in")
    tok = send_grads("ret", [dwin, [_as_shards(g, RET_V) for g in dwout]])
    dx2, dg10 = ret_in_bwd(dpj, dx3, x2, gain(1, 0, tok), w_in)
    dx1, du0, df0, dg02, dg03, dcw0, dcb0 = mlp_bwd(
        dx2, f0, x1, u0, s0, gain(0, 2), gain(0, 3), w_up0, w_dn0, conv_w[0])
    dwdn0 = grad_matmul(a0, df0, FF_CHUNK, D_MODEL, "grad_w_down_0")
    tok = send_grads("down0", [[_as_shards(g, dn_rows) for g in dwdn0]])
    half = D_MODEL // 2
    for part, first in (("a", 0), ("b", half)):
        dwup0 = grad_matmul(h0, du0, half, FF_CHUNK, "grad_w_up_0" + part, tk=4096, interleaved=True, after=tok,
                            cols=(first, half))
        tok = send_grads("up0" + part, [dwup0])
    dx0, dg00, dg01, dpscale, dpw = pool_bwd(dx1, x, gain(0, 0, tok), gain(0, 1), pool_w, pool_scale)

    rows = lambda g: jnp.sum(g, axis=0, keepdims=True)
    dgains = jnp.concatenate([dg00, dg01, rows(dg02), rows(dg03), dg10, dg11, rows(dg12), rows(dg13)],
                             axis=0).reshape(2, 4, D_MODEL)
    small = {"gains": dgains, "pool_scale": dpscale, "gn": dgn,
             "conv_w": jnp.sum(jnp.stack([dcw0, dcw1]), axis=3),
             "conv_b": jnp.sum(jnp.stack([dcb0, dcb1]), axis=2, keepdims=True), "pool_w": dpw}
    return loss, dx0, small


def kernel(x, positions, norm_gain, pool_w, pool_scale, ret_w_in, ret_gn_gain, ret_w_out, mlp_w_up, mlp_conv_w, mlp_conv_b, mlp_w_down, loss_target, m_norm_gain, m_pool_w, m_pool_scale, m_ret_w_in, m_ret_gn_gain, m_ret_w_out, m_mlp_w_up, m_mlp_conv_w, m_mlp_conv_b, m_mlp_w_down, v_norm_gain, v_pool_w, v_pool_scale, v_ret_w_in, v_ret_gn_gain, v_ret_w_out, v_mlp_w_up, v_mlp_conv_w, v_mlp_conv_b, v_mlp_w_down):
    t = x.shape[1]
    me = 2 * lax.axis_index("x") + lax.axis_index("y")
    me_slot = jnp.reshape(me, (1,)).astype(jnp.int32)

    small_parts = [norm_gain, ret_gn_gain, mlp_conv_w, pool_w]
    small_sizes = [p.size for p in small_parts]
    small_rows = -(-sum(small_sizes) // (128 * 8)) * 8
    gathers = {}

    def start_gather(group, srcs, after):
        lands = [lax.dynamic_update_index_in_dim(lax.empty((N_SHARD,) + s.shape, s.dtype), s, me, 0) for s in srcs]
        mode = "gather_half" if group == "mlp0" else "gather"
        gathers[group] = (mode, exchange_start(mode, srcs, lands, "gather_start_" + group, after=after))
        return gathers[group][1][-1]

    token = start_gather("small", [_pack_rows(small_parts, small_rows)], None)
    token = start_gather("mlp0", [mlp_w_up[0].astype(BF16), mlp_w_down[0].astype(BF16)], token)

    def weights(group, after):
        if group == "mlp0":
            tok = start_gather("ret", [ret_w_in[0].astype(BF16), ret_w_out[0].astype(BF16)], after)
            after = start_gather("mlp1", [mlp_w_up[1].astype(BF16), mlp_w_down[1].astype(BF16)], tok)
        mode, started = gathers[group]
        _, lands = exchange_wait(mode, started, after, "gather_wait_" + group)
        if mode == "gather_half":
            forward = exchange_start("forward_half", [], lands, "forward_start_" + group)
            _, lands = exchange_wait("forward_half", forward, forward[-1], "forward_wait_" + group)
        return lands

    sent, early = {}, {}

    def reduced(group, after, names):
        started, own = sent[group]
        _, recv = exchange_wait("scatter", started, after, "scatter_wait_" + group)
        return [plane_sum(me_slot, f, r, "plane_sum_" + nm)
                for f, r, nm in zip(own, recv, names)]

    def swap_start(planes, name):
        return exchange_start("swap", planes, [lax.empty(p.shape, p.dtype) for p in planes], name)

    def send_grads(group, pairs):
        lands = [lax.empty((3,) + b.shape[1:], BF16) for _, b in pairs]
        sent[group] = (exchange_start("scatter", [b for _, b in pairs], lands, "scatter_start_" + group),
                       [f for f, _ in pairs])
        token = sent[group][0][-1]
        if group == "down0":
            marker = pairs[0][1]
            early["planes"] = (reduced("mlp1", marker, ["w_up_1", "w_down_1"])
                               + reduced("ret", marker, ["w_in", "w_out"]))
            early["swap"] = swap_start(early["planes"], "swap_start_a")
            token = token + early["swap"][-1]
        return token

    (smallg,) = weights("small", token)
    smallg = smallg.reshape(N_SHARD, -1)
    offs = np.cumsum([0] + small_sizes)
    piece = lambda i, shape: smallg[:, offs[i]:offs[i + 1]].reshape((N_SHARD,) + shape)
    gains = piece(0, (2, 4, 256)).transpose(1, 2, 0, 3).reshape(2, 4, D_MODEL)
    gn_full = piece(1, (512,)).reshape(1, RET_HEADS * RET_V)
    cw_full = piece(2, (2, 3, FF_CHUNK)).transpose(1, 0, 2, 3)
    pw_full = piece(3, (4, 64, 256)).transpose(1, 0, 2, 3).reshape(4, 256, 256).astype(BF16)
    cb_full = mlp_conv_b.reshape(2, N_SHARD, 1, FF_CHUNK)

    loss, dx0, small = _local_step(
        x[0], positions.reshape(t, 1).astype(F32), loss_target[0], gains, pw_full, pool_scale, gn_full,
        cw_full, cb_full, weights, send_grads)

    def small_adamw(w, m, v, grads, name):
        w3 = w.reshape(1, -1, w.shape[-1])
        out = adamw(w3, m.reshape(w3.shape), v.reshape(w3.shape), [g.reshape(w3.shape[1:]) for g in grads], 0, None, name)
        return [o.reshape(w.shape) for o in out]

    pw_f = small["pool_w"].reshape(4, N_SHARD, 64, 256).transpose(1, 0, 2, 3).reshape(N_SHARD, 256, 256)
    small_order = ["gains", "pool_scale", "gn", "conv_w", "conv_b"]
    gsmall_sizes = [small[k].size for k in small_order]
    gsmall_rows = -(-sum(gsmall_sizes) // (128 * 8)) * 8
    gpack = _pack_rows([small[k] for k in small_order], gsmall_rows)
    mine = 2 * me + lax.axis_index("c")
    small_started = exchange_start(
        "everyone", [gpack], [lax.dynamic_update_index_in_dim(lax.empty((8,) + gpack.shape, F32), gpack, mine, 0)],
        "small_start")
    send_grads("pool_w", [(pw_f, pw_f.astype(BF16))])

    res = {}
    planes_a, others_a = exchange_wait("swap", early["swap"], small_started[-1], "swap_wait_a")
    res["ret_w_in"] = adamw(ret_w_in, m_ret_w_in, v_ret_w_in, (planes_a[2], others_a[2]), 0, None, "adamw_w_in")
    res["ret_w_out"] = adamw(ret_w_out, m_ret_w_out, v_ret_w_out, (planes_a[3], others_a[3]), 0, None, "adamw_w_out")
    up1 = adamw(mlp_w_up, m_mlp_w_up, v_mlp_w_up, (planes_a[0], others_a[0]), 1, None, "adamw_w_up_1")
    dn1 = adamw(mlp_w_down, m_mlp_w_down, v_mlp_w_down, (planes_a[1], others_a[1]), 1, None, "adamw_w_down_1")

    planes_b = (reduced("up0a", dn1[0], ["w_up_0a"]) + reduced("up0b", dn1[0], ["w_up_0b"])
                + reduced("down0", dn1[0], ["w_down_0"]) + reduced("pool_w", dn1[0], ["pool_w"]))
    swap_b = swap_start(planes_b, "swap_start_b")

    _, (small_recv,) = exchange_wait("everyone", small_started, swap_b[-1], "small_wait")
    gsmall = sum_slots(small_recv, "sum_small").reshape(-1)
    goffs = np.cumsum([0] + gsmall_sizes)
    gpiece = lambda i: gsmall[goffs[i]:goffs[i + 1]].reshape(small[small_order[i]].shape)
    g_gains = lax.dynamic_slice_in_dim(gpiece(0), me * 256, 256, axis=2)
    g_gn = lax.dynamic_slice_in_dim(gpiece(2), me * RET_V, RET_V, axis=1)
    g_cw = lax.dynamic_index_in_dim(gpiece(3), me, 1, keepdims=False)
    res["norm_gain"] = small_adamw(norm_gain, m_norm_gain, v_norm_gain, [g_gains], "adamw_norm_gain")
    res["pool_scale"] = small_adamw(pool_scale, m_pool_scale, v_pool_scale, [gpiece(1)], "adamw_pool_scale")
    res["ret_gn_gain"] = small_adamw(ret_gn_gain, m_ret_gn_gain, v_ret_gn_gain, [g_gn], "adamw_gn_gain")
    res["mlp_conv_w"] = small_adamw(mlp_conv_w, m_mlp_conv_w, v_mlp_conv_w, [g_cw], "adamw_conv_w")
    res["mlp_conv_b"] = small_adamw(mlp_conv_b, m_mlp_conv_b, v_mlp_conv_b, [gpiece(4)], "adamw_conv_b")

    planes_b, others_b = exchange_wait("swap", swap_b, res["mlp_conv_b"][0], "swap_wait_b")
    up0a = adamw(mlp_w_up, m_mlp_w_up, v_mlp_w_up, (planes_b[0], others_b[0]), 0, up1, "adamw_w_up_0a")
    res["mlp_w_up"] = adamw(mlp_w_up, m_mlp_w_up, v_mlp_w_up, (planes_b[1], others_b[1]), 0, up0a, "adamw_w_up_0b",
                            row0=D_MODEL // 2)
    res["mlp_w_down"] = adamw(mlp_w_down, m_mlp_w_down, v_mlp_w_down, (planes_b[2], others_b[2]), 0, dn1,
                              "adamw_w_down_0")
    res["pool_w"] = small_adamw(pool_w, m_pool_w, v_pool_w, (planes_b[3], others_b[3]), "adamw_pool_w")

    order = ["norm_gain", "pool_w", "pool_scale", "ret_w_in", "ret_gn_gain", "ret_w_out", "mlp_w_up", "mlp_conv_w",
             "mlp_conv_b", "mlp_w_down"]
    total_loss = lax.psum(loss[0, 0], ("x", "y", "c"))
    outs = [total_loss, dx0.reshape(x.shape)]
    for q in range(4):
        outs += [res[k][q] for k in order]
    return tuple(outs)
```

```python
import numpy as np
import jax
import jax.numpy as jnp
from jax import lax
from jax.experimental import pallas as pl
from jax.experimental.pallas import tpu as pltpu

F32 = jnp.float32
BF16 = jnp.bfloat16

D_MODEL = 1024
D_FF = 2816
FF_CHUNK = 1408
N_SHARD = 4
POOL_WINDOWS = (2, 4, 8, 16)
POOL_DIM = 256
POOL_HALO = 16
RET_HEADS = 4
RET_QK = 256
RET_V = 512
RET_CHUNK = 256
RET_IN = 6144
RET_IN_SHARD = 1536
ROPE_BASE = 10000.0
EPS = 1e-6
CONV_HALO = 8

ADAM_LR, ADAM_B1, ADAM_B2, ADAM_EPS, ADAM_WD, ADAM_STEP = 0.001, 0.9, 0.999, 1e-08, 0.01, 10

VMEM_LIMIT = 56 * 1024 * 1024
VMEM_LIMIT_MLP_BWD = 62 * 1024 * 1024
MESH = pl.DeviceIdType.MESH
ANY = pl.BlockSpec(memory_space=pl.ANY)


def _params(n_grid=1, limit=VMEM_LIMIT):
    return pltpu.CompilerParams(dimension_semantics=("arbitrary",) * n_grid, vmem_limit_bytes=limit)


def _dot(a, b):
    return jnp.dot(a, b, preferred_element_type=F32)


def _dot_nt(a, b):
    return lax.dot_general(a, b, (((1,), (1,)), ((), ())), preferred_element_type=F32)


def _dot_tn(a, b):
    return lax.dot_general(a, b, (((0,), (0,)), ((), ())), preferred_element_type=F32)


def _rms_fwd(x, gain):
    r = lax.rsqrt(jnp.mean(x * x, axis=-1, keepdims=True) + EPS)
    xh = x * r
    return xh * gain, xh, r


def _rms_bwd(xh, r, gain, dy):
    dxh = dy * gain
    return r * (dxh - xh * jnp.mean(dxh * xh, axis=-1, keepdims=True))


def _colsum(v):
    return jnp.sum(v, axis=0, keepdims=True)


def _full(shape):
    nd = len(shape)
    return pl.BlockSpec(shape, lambda *_: (0,) * nd)


def rope_tables(pos_col, inv_freq, after):
    t = pos_col.shape[0]
    tm = min(t, 1024)

    def body(p_ref, f_ref, after_ref, c_ref, s_ref):
        ang = p_ref[...] * f_ref[...]
        c_ref[...] = jnp.cos(ang)
        s_ref[...] = jnp.sin(ang)

    return pl.pallas_call(
        body, grid=(t // tm,),
        in_specs=[pl.BlockSpec((tm, 1), lambda i: (i, 0)), _full((1, 128)), ANY],
        out_specs=[pl.BlockSpec((tm, 128), lambda i: (i, 0))] * 2,
        out_shape=[jax.ShapeDtypeStruct((t, 128), F32)] * 2,
        compiler_params=_params(1), name="rope_tables")(pos_col, inv_freq, after)


def _window_sums(ext, backward):
    n = ext.shape[0]
    cur, sums = ext, []
    for g, win in enumerate(POOL_WINDOWS):
        if g > 0:
            cur = cur[:, POOL_DIM:]
        half = win // 2
        cur = cur + pltpu.roll(cur, n - half if backward else half, axis=0)
        sums.append(cur[:, 0:POOL_DIM])
    return sums


def _pool_diff(h_halo, h, row0, tm):
    t_idx = row0 + lax.broadcasted_iota(jnp.int32, (tm, 1), 0)
    sums = _window_sums(jnp.concatenate([h_halo, h], axis=0), backward=False)
    parts, inv_counts = [], []
    for g, win in enumerate(POOL_WINDOWS):
        inv = 1.0 / jnp.minimum(t_idx + 1, win).astype(F32)
        parts.append(sums[g][POOL_HALO:, :] * inv - h[:, g * POOL_DIM:(g + 1) * POOL_DIM])
        inv_counts.append(inv)
    return parts, inv_counts


def pool_fwd(x, g_pre, g_post, pool_w, pool_scale, tm=512):
    t, d = x.shape
    nt = t // tm

    def body(x_ref, g0_ref, g1_ref, w_ref, sc_ref, o_ref, hext):
        i = pl.program_id(0)

        @pl.when(i == 0)
        def _():
            hext[...] = jnp.zeros((POOL_HALO, d), F32)

        xv = x_ref[...]
        h, _, _ = _rms_fwd(xv, g0_ref[...])
        parts, _ = _pool_diff(hext[...], h, i * tm, tm)
        hext[...] = h[tm - POOL_HALO:tm, :]
        ys = [_dot(parts[g].astype(BF16), w_ref[g]) for g in range(len(POOL_WINDOWS))]
        y = jnp.concatenate(ys, axis=-1) * sc_ref[...]
        m, _, _ = _rms_fwd(y, g1_ref[...])
        o_ref[...] = xv + m

    row = pl.BlockSpec((tm, d), lambda i: (i, 0))
    return pl.pallas_call(
        body, grid=(nt,),
        in_specs=[row, _full((1, d)), _full((1, d)), _full(pool_w.shape), _full((1, d))],
        out_specs=row, out_shape=jax.ShapeDtypeStruct((t, d), F32),
        scratch_shapes=[pltpu.VMEM((POOL_HALO, d), F32)],
        compiler_params=_params(1), name="pool_fwd")(x, g_pre, g_post, pool_w, pool_scale)


def pool_bwd(dx1, x, g_pre, g_post, pool_w, pool_scale, tm=512):
    t, d = x.shape
    nt = t // tm
    ng = len(POOL_WINDOWS)

    def body(dx1_ref, x_ref, xh_ref, g0_ref, g1_ref, w_ref, sc_ref,
             dx_ref, dg0_ref, dg1_ref, dsc_ref, dw_ref, enext):
        i = pl.program_id(0)
        r = nt - 1 - i

        @pl.when(i == 0)
        def _():
            enext[...] = jnp.zeros((POOL_HALO, d), F32)
            dg0_ref[...] = jnp.zeros_like(dg0_ref)
            dg1_ref[...] = jnp.zeros_like(dg1_ref)
            dsc_ref[...] = jnp.zeros_like(dsc_ref)
            dw_ref[...] = jnp.zeros_like(dw_ref)

        g0 = g0_ref[...]
        g1 = g1_ref[...]
        sc = sc_ref[...]
        xv = x_ref[...]
        h, xh, rx = _rms_fwd(xv, g0)
        h_halo, _, _ = _rms_fwd(xh_ref[...], g0)
        parts, inv_counts = _pool_diff(h_halo * jnp.where(r > 0, 1.0, 0.0), h, r * tm, tm)
        parts_b = [p.astype(BF16) for p in parts]
        ypre = jnp.concatenate([_dot(parts_b[g], w_ref[g]) for g in range(ng)], axis=-1)
        _, yh, ry = _rms_fwd(ypre * sc, g1)
        dm = dx1_ref[...]
        dg1_ref[...] += _colsum(dm * yh)
        dy = _rms_bwd(yh, ry, g1, dm)
        dsc_ref[...] += _colsum(dy * ypre)
        dyp = (dy * sc).astype(BF16)
        ddiffs = []
        for g in range(ng):
            cols = slice(g * POOL_DIM, (g + 1) * POOL_DIM)
            dw_ref[g] += _dot_tn(parts_b[g], dyp[:, cols])
            ddiffs.append(_dot_nt(dyp[:, cols], w_ref[g]))
        e = jnp.concatenate([ddiffs[g] * inv_counts[g] for g in range(ng)], axis=-1)
        sums = _window_sums(jnp.concatenate([e, enext[...]], axis=0), backward=True)
        enext[...] = e[0:POOL_HALO, :]
        dh = jnp.concatenate([sums[g][0:tm, :] - ddiffs[g] for g in range(ng)], axis=-1)
        dg0_ref[...] += _colsum(dh * xh)
        dx_ref[...] = dm + _rms_bwd(xh, rx, g0, dh)

    row = pl.BlockSpec((tm, d), lambda i: (nt - 1 - i, 0))
    halo = pl.BlockSpec((POOL_HALO, d), lambda i: (jnp.maximum((nt - 1 - i) * (tm // POOL_HALO) - 1, 0), 0))
    vec = _full((1, d))
    return pl.pallas_call(
        body, grid=(nt,),
        in_specs=[row, row, halo, vec, vec, _full(pool_w.shape), vec],
        out_specs=[row, vec, vec, vec, _full((ng, POOL_DIM, POOL_DIM))],
        out_shape=[jax.ShapeDtypeStruct((t, d), F32)] + [jax.ShapeDtypeStruct((1, d), F32)] * 3
        + [jax.ShapeDtypeStruct((ng, POOL_DIM, POOL_DIM), F32)],
        scratch_shapes=[pltpu.VMEM((POOL_HALO, d), F32)],
        compiler_params=_params(1), name="pool_bwd")(dx1, x, x, g_pre, g_post, pool_w, pool_scale)


def _conv_taps(cw_ref, j):
    return cw_ref[j, 0:1, :], cw_ref[j, 1:2, :], cw_ref[j, 2:3, :]


def _row_block(m, target=256):
    if m <= target:
        return m
    for b in range(target, 7, -8):
        if m % b == 0:
            return b
    return m


def mlp_fwd(x, g_pre, g_post, w_up, w_down, conv_w, conv_b, target=None, tm=256):
    t, d = x.shape
    nt = t // tm
    h8 = CONV_HALO
    with_loss = target is not None
    n_extra = 1 if with_loss else 0

    def body(x_ref, g2_ref, g3_ref, wup_hbm, wdn_hbm, cw_ref, cb_ref, *rest):
        tgt_ref = rest[0] if with_loss else None
        xo_ref, u_ref, s_ref, a_ref, f_ref, h_ref = rest[n_extra:n_extra + 6]
        loss_ref = rest[n_extra + 6] if with_loss else None
        wup_v, wdn_v, tail, sem = rest[-4:]
        i = pl.program_id(0)

        @pl.when(i == 0)
        def _():
            c1 = pltpu.make_async_copy(wup_hbm, wup_v, sem.at[0])
            c2 = pltpu.make_async_copy(wdn_hbm, wdn_v, sem.at[1])
            c1.start()
            c2.start()
            tail[...] = jnp.zeros_like(tail)
            if with_loss:
                loss_ref[...] = jnp.zeros_like(loss_ref)
            c1.wait()
            c2.wait()

        xv = x_ref[...]
        h, _, _ = _rms_fwd(xv, g2_ref[...])
        hb = h.astype(BF16)
        h_ref[...] = hb
        acc = jnp.zeros((tm, d), F32)
        for k in range(2):
            cs = []
            for s in range(2):
                j, cols = k + 2 * s, slice((2 * k + s) * FF_CHUNK, (2 * k + s + 1) * FF_CHUNK)
                uf = _dot(hb, wup_v[j])
                u_ref[:, cols] = uf.astype(BF16)
                ext = jnp.concatenate([tail[j], uf], axis=0)
                tail[j] = uf[tm - h8:tm, :]
                w0, w1, w2 = _conv_taps(cw_ref, j)
                cs.append(cb_ref[j] + w2 * uf + w1 * pltpu.roll(ext, 1, axis=0)[h8:, :]
                          + w0 * pltpu.roll(ext, 2, axis=0)[h8:, :])
            cg, cv = cs
            sg = jax.nn.sigmoid(cg)
            sil = cg * sg
            ab = (sil * cv).astype(BF16)
            a_ref[:, k * FF_CHUNK:(k + 1) * FF_CHUNK] = ab
            s_ref[:, 2 * k * FF_CHUNK:(2 * k + 1) * FF_CHUNK] = sil.astype(BF16)
            s_ref[:, (2 * k + 1) * FF_CHUNK:(2 * k + 2) * FF_CHUNK] = (cv * (sg + sil * (1.0 - sg))).astype(BF16)
            acc = acc + _dot(ab, wdn_v[k * FF_CHUNK:(k + 1) * FF_CHUNK, :])
        f_ref[...] = acc.astype(BF16)
        y, _, _ = _rms_fwd(acc, g3_ref[...])
        if with_loss:
            err = (xv + y) - tgt_ref[...]
            xo_ref[...] = err * (1.0 / d)
            loss_ref[...] += 0.5 * jnp.sum(jnp.mean(err * err, axis=-1, keepdims=True), axis=0, keepdims=True)
        else:
            xo_ref[...] = xv + y

    row = pl.BlockSpec((tm, d), lambda i: (i, 0))
    wide = pl.BlockSpec((tm, 2 * D_FF), lambda i: (i, 0))
    vec = _full((1, d))
    extra = [target] if with_loss else []
    return pl.pallas_call(
        body, grid=(nt,),
        in_specs=[row, vec, vec, ANY, ANY, _full(conv_w.shape), _full(conv_b.shape)] + [row] * n_extra,
        out_specs=[row, wide, wide, pl.BlockSpec((tm, D_FF), lambda i: (i, 0)), row, row] + [_full((1, 1))] * n_extra,
        out_shape=[jax.ShapeDtypeStruct((t, d), F32), jax.ShapeDtypeStruct((t, 2 * D_FF), BF16),
                   jax.ShapeDtypeStruct((t, 2 * D_FF), BF16), jax.ShapeDtypeStruct((t, D_FF), BF16),
                   jax.ShapeDtypeStruct((t, d), BF16), jax.ShapeDtypeStruct((t, d), BF16)]
        + [jax.ShapeDtypeStruct((1, 1), F32)] * n_extra,
        scratch_shapes=[pltpu.VMEM(w_up.shape, BF16), pltpu.VMEM(w_down.shape, BF16),
                        pltpu.VMEM((N_SHARD, h8, FF_CHUNK), F32), pltpu.SemaphoreType.DMA((2,))],
        compiler_params=_params(1), name="mlp_fwd_loss" if with_loss else "mlp_fwd")(
            x, g_pre, g_post, w_up, w_down, conv_w, conv_b, *extra)


def _rowsum8(v):
    return jnp.sum(v.reshape(v.shape[0] // 8, 8, v.shape[1]), axis=0)


def mlp_bwd(dxo, f, x, u, sp, g_pre, g_post, w_up, w_down, conv_w, tm=256):
    t, d = x.shape
    nt = t // tm
    h8 = CONV_HALO

    def body(dxo_ref, f_ref, x_ref, u_ref, s_ref, g2_ref, g3_ref, wup_hbm, wdn_hbm, cw_ref,
             dx_ref, du_ref, df_ref, dg2_ref, dg3_ref, dcw_ref, dcb_ref,
             wup_v, wdn_v, carry, sem):
        @pl.when(pl.program_id(0) == 0)
        def _():
            c1 = pltpu.make_async_copy(wup_hbm, wup_v, sem.at[0])
            c2 = pltpu.make_async_copy(wdn_hbm, wdn_v, sem.at[1])
            c1.start()
            c2.start()
            carry[...] = jnp.zeros_like(carry)
            dg2_ref[...] = jnp.zeros_like(dg2_ref)
            dg3_ref[...] = jnp.zeros_like(dg3_ref)
            dcw_ref[...] = jnp.zeros_like(dcw_ref)
            dcb_ref[...] = jnp.zeros_like(dcb_ref)
            c1.wait()
            c2.wait()

        g3 = g3_ref[...]
        dxo = dxo_ref[...]
        _, fh, rf = _rms_fwd(f_ref[...].astype(F32), g3)
        dg3_ref[...] += _rowsum8(dxo * fh)
        dfb = _rms_bwd(fh, rf, g3, dxo).astype(BF16)
        df_ref[...] = dfb
        dh = jnp.zeros((tm, d), F32)
        for k in range(2):
            da = _dot_nt(dfb, wdn_v[k * FF_CHUNK:(k + 1) * FF_CHUNK, :])
            for s in range(2):
                j = k + 2 * s
                cols = slice((2 * k + s) * FF_CHUNK, (2 * k + s + 1) * FF_CHUNK)
                dc = da * s_ref[:, (2 * k + 1 - s) * FF_CHUNK:(2 * k + 2 - s) * FF_CHUNK].astype(F32)
                uf = u_ref[:, cols].astype(F32)
                ext = jnp.concatenate([dc, carry[j]], axis=0)
                carry[j] = dc[0:h8, :]
                dc1 = pltpu.roll(ext, tm + h8 - 1, axis=0)[0:tm, :]
                dc2 = pltpu.roll(ext, tm + h8 - 2, axis=0)[0:tm, :]
                dcb_ref[j] += _rowsum8(dc)
                dcw_ref[j, 2] += _rowsum8(dc * uf)
                dcw_ref[j, 1] += _rowsum8(dc1 * uf)
                dcw_ref[j, 0] += _rowsum8(dc2 * uf)
                dub = (cw_ref[j, 2:3, :] * dc + cw_ref[j, 1:2, :] * dc1 + cw_ref[j, 0:1, :] * dc2).astype(BF16)
                du_ref[:, cols] = dub
                dh = dh + _dot_nt(dub, wup_v[j])
        g2 = g2_ref[...]
        _, xh, rx = _rms_fwd(x_ref[...], g2)
        dg2_ref[...] += _rowsum8(dh * xh)
        dx_ref[...] = dxo + _rms_bwd(xh, rx, g2, dh)

    row = pl.BlockSpec((tm, d), lambda i: (nt - 1 - i, 0))
    wide = pl.BlockSpec((tm, 2 * D_FF), lambda i: (nt - 1 - i, 0))
    vec = _full((1, d))
    acc = _full((8, d))
    dcw_shape, dcb_shape = (N_SHARD, 3, 8, FF_CHUNK), (N_SHARD, 8, FF_CHUNK)
    return pl.pallas_call(
        body, grid=(nt,),
        in_specs=[row, row, row, wide, wide, vec, vec, ANY, ANY, _full(conv_w.shape)],
        out_specs=[row, wide, row, acc, acc, _full(dcw_shape), _full(dcb_shape)],
        out_shape=[jax.ShapeDtypeStruct((t, d), F32), jax.ShapeDtypeStruct((t, 2 * D_FF), BF16),
                   jax.ShapeDtypeStruct((t, d), BF16),
                   jax.ShapeDtypeStruct((8, d), F32), jax.ShapeDtypeStruct((8, d), F32),
                   jax.ShapeDtypeStruct(dcw_shape, F32), jax.ShapeDtypeStruct(dcb_shape, F32)],
        scratch_shapes=[pltpu.VMEM(w_up.shape, BF16), pltpu.VMEM(w_down.shape, BF16),
                        pltpu.VMEM((N_SHARD, h8, FF_CHUNK), F32), pltpu.SemaphoreType.DMA((2,))],
        compiler_params=_params(1, VMEM_LIMIT_MLP_BWD), name="mlp_bwd")(
            dxo, f, x, u, sp, g_pre, g_post, w_up, w_down, conv_w)


def grad_matmul(a, b, bm, bn, name, tk=2048, interleaved=False, after=None, cols=None):
    t = a.shape[0]
    m0, m = (0, a.shape[1]) if cols is None else cols
    n = b.shape[1]
    tk = min(tk, t)
    nk = t // tk
    place = (lambda j: (j % 2) * 2 + j // 2) if interleaved else (lambda j: j)
    extra = [] if after is None else [after]
    first = m0 // bm

    def body(a_ref, b_ref, *rest):
        o_ref, ob_ref = rest[len(extra):]
        kk = pl.program_id(2)

        @pl.when(kk == 0)
        def _():
            o_ref[...] = jnp.zeros_like(o_ref)

        o_ref[...] += _dot_tn(a_ref[...], b_ref[...])

        @pl.when(kk == nk - 1)
        def _():
            ob_ref[...] = o_ref[...].astype(BF16)

    ospec = pl.BlockSpec((None, bm, bn), lambda j, i, kk: (place(j), i, 0))
    return pl.pallas_call(
        body, grid=(n // bn, m // bm, nk),
        in_specs=[pl.BlockSpec((tk, bm), lambda j, i, kk: (kk, first + i)),
                  pl.BlockSpec((tk, bn), lambda j, i, kk: (kk, j))]
        + [ANY] * len(extra),
        out_specs=[ospec, ospec],
        out_shape=[jax.ShapeDtypeStruct((n // bn, m, bn), F32), jax.ShapeDtypeStruct((n // bn, m, bn), BF16)],
        compiler_params=_params(3), name=name)(a, b, *extra)


def _decay_tables():
    log_gamma = jnp.log(1.0 - 2.0 ** (-5.0 - jnp.arange(RET_HEADS, dtype=F32)))
    i = jnp.arange(RET_CHUNK, dtype=F32)
    rel = i[:, None] - i[None, :]
    intra = jnp.where(rel >= 0, jnp.exp(jnp.maximum(rel, 0.0) * log_gamma[:, None, None]), 0.0)
    cross = jnp.exp((i + 1.0) * log_gamma[:, None])[:, :, None]
    inner = jnp.exp((RET_CHUNK - 1.0 - i) * log_gamma[:, None])[:, :, None]
    chunk = [float(np.exp(np.float32(RET_CHUNK) * np.log(np.float32(1.0 - 2.0 ** (-5.0 - h))).astype(np.float32)))
             for h in range(RET_HEADS)]
    return intra, cross, inner, chunk


def ret_proj(x, g_pre, w_in, cos, sin, tm=512):
    t, d = x.shape
    nt = t // tm
    per = RET_IN_SHARD // RET_QK

    def body(x_ref, g_ref, win_hbm, c_ref, s_ref, pj_ref, h_ref, win_v, sem):
        @pl.when(pl.program_id(0) == 0)
        def _():
            cp = pltpu.make_async_copy(win_hbm, win_v, sem)
            cp.start()
            cp.wait()

        h, _, _ = _rms_fwd(x_ref[...], g_ref[...])
        hb = h.astype(BF16)
        h_ref[...] = hb
        c = c_ref[...]
        s = s_ref[...]
        for j in range(N_SHARD):
            pjj = _dot(hb, win_v[j])
            for bb in range(per):
                b = per * j + bb
                blk = pjj[:, bb * RET_QK:(bb + 1) * RET_QK]
                if b < 2 * RET_HEADS:
                    x1, x2 = blk[:, :128], blk[:, 128:]
                    o1 = x1 * c - x2 * s
                    o2 = x2 * c + x1 * s
                    if b < RET_HEADS:
                        o1 = o1 * (RET_QK ** -0.5)
                        o2 = o2 * (RET_QK ** -0.5)
                    pj_ref[:, b * RET_QK:b * RET_QK + 128] = o1.astype(BF16)
                    pj_ref[:, b * RET_QK + 128:(b + 1) * RET_QK] = o2.astype(BF16)
                else:
                    pj_ref[:, b * RET_QK:(b + 1) * RET_QK] = blk.astype(BF16)

    row = pl.BlockSpec((tm, d), lambda i: (i, 0))
    tab = pl.BlockSpec((tm, 128), lambda i: (i, 0))
    return pl.pallas_call(
        body, grid=(nt,),
        in_specs=[row, _full((1, d)), ANY, tab, tab],
        out_specs=[pl.BlockSpec((tm, RET_IN), lambda i: (i, 0)), row],
        out_shape=[jax.ShapeDtypeStruct((t, RET_IN), BF16), jax.ShapeDtypeStruct((t, d), BF16)],
        scratch_shapes=[pltpu.VMEM(w_in.shape, BF16), pltpu.SemaphoreType.DMA],
        compiler_params=_params(1), name="ret_proj")(x, g_pre, w_in, cos, sin)


def ret_core_fwd(pj, intra, cross, inner, chunk_decay):
    t = pj.shape[0]
    nc = t // RET_CHUNK
    c = RET_CHUNK
    qk_all = RET_HEADS * RET_QK
    v_all = RET_HEADS * RET_V

    def body(q_ref, k_ref, v_ref, dm_ref, cr_ref, in_ref, o_ref, sp_ref, state):
        @pl.when(pl.program_id(0) == 0)
        def _():
            state[...] = jnp.zeros_like(state)

        for h in range(RET_HEADS):
            q = q_ref[:, h * RET_QK:(h + 1) * RET_QK]
            k = k_ref[:, h * RET_QK:(h + 1) * RET_QK]
            v = v_ref[:, h * RET_V:(h + 1) * RET_V]
            sb = state[h].astype(BF16)
            sp_ref[h] = sb
            sc = _dot_nt(q, k) * dm_ref[h]
            o_ref[:, h * RET_V:(h + 1) * RET_V] = (_dot(sc.astype(BF16), v) + _dot(q, sb) * cr_ref[h]).astype(BF16)
            kd = (k.astype(F32) * in_ref[h]).astype(BF16)
            state[h] = state[h] * chunk_decay[h] + _dot_tn(kd, v)

    return pl.pallas_call(
        body, grid=(nc,),
        in_specs=[pl.BlockSpec((c, qk_all), lambda n: (n, 0)), pl.BlockSpec((c, qk_all), lambda n: (n, 1)),
                  pl.BlockSpec((c, v_all), lambda n: (n, 1)),
                  _full(intra.shape), _full(cross.shape), _full(inner.shape)],
        out_specs=[pl.BlockSpec((c, v_all), lambda n: (n, 0)),
                   pl.BlockSpec((None, RET_HEADS, RET_QK, RET_V), lambda n: (n, 0, 0, 0))],
        out_shape=[jax.ShapeDtypeStruct((t, v_all), BF16),
                   jax.ShapeDtypeStruct((nc, RET_HEADS, RET_QK, RET_V), BF16)],
        scratch_shapes=[pltpu.VMEM((RET_HEADS, RET_QK, RET_V), F32)],
        compiler_params=_params(1), name="ret_core_fwd")(pj, pj, pj, intra, cross, inner)


def _group_norm(o_h):
    mu = jnp.mean(o_h, axis=-1, keepdims=True)
    dev = o_h - mu
    rstd = lax.rsqrt(jnp.mean(dev * dev, axis=-1, keepdims=True) + EPS)
    return dev * rstd, rstd


def ret_out_fwd(o, pj, x, gn_gain, g_post, w_out, tm=512):
    t, d = x.shape
    nt = t // tm
    v_all = RET_HEADS * RET_V

    def body(o_ref, g_ref, x_ref, gn_ref, g1_ref, w_ref, xo_ref, y_ref, out_ref):
        out = jnp.zeros((tm, d), F32)
        for h in range(RET_HEADS):
            cols = slice(h * RET_V, (h + 1) * RET_V)
            ohat, _ = _group_norm(o_ref[:, cols].astype(F32))
            g = g_ref[:, cols].astype(F32)
            yb = (g * jax.nn.sigmoid(g) * (ohat * gn_ref[:, cols])).astype(BF16)
            y_ref[:, cols] = yb
            out = out + _dot(yb, w_ref[cols, :])
        out_ref[...] = out.astype(BF16)
        m, _, _ = _rms_fwd(out, g1_ref[...])
        xo_ref[...] = x_ref[...] + m

    row = pl.BlockSpec((tm, d), lambda i: (i, 0))
    wide = pl.BlockSpec((tm, v_all), lambda i: (i, 0))
    return pl.pallas_call(
        body, grid=(nt,),
        in_specs=[wide, pl.BlockSpec((tm, v_all), lambda i: (i, 2)), row, _full((1, v_all)), _full((1, d)),
                  _full(w_out.shape)],
        out_specs=[row, wide, row],
        out_shape=[jax.ShapeDtypeStruct((t, d), F32), jax.ShapeDtypeStruct((t, v_all), BF16),
                   jax.ShapeDtypeStruct((t, d), BF16)],
        compiler_params=_params(1), name="ret_out_fwd")(o, pj, x, gn_gain, g_post, w_out)


def ret_out_bwd(dxo, out, o, pj, gn_gain, g_post, w_out, tm=512):
    t, d = out.shape
    nt = t // tm
    v_all = RET_HEADS * RET_V

    def body(dxo_ref, out_ref, o_ref, g_ref, gn_ref, g1_ref, w_ref,
             dout_ref, dgate_ref, do_ref, dg1_ref, dgn_ref):
        @pl.when(pl.program_id(0) == 0)
        def _():
            dg1_ref[...] = jnp.zeros_like(dg1_ref)
            dgn_ref[...] = jnp.zeros_like(dgn_ref)

        g1 = g1_ref[...]
        dxo = dxo_ref[...]
        _, oh_, r_ = _rms_fwd(out_ref[...].astype(F32), g1)
        dg1_ref[...] += _colsum(dxo * oh_)
        doutb = _rms_bwd(oh_, r_, g1, dxo).astype(BF16)
        dout_ref[...] = doutb
        for h in range(RET_HEADS):
            cols = slice(h * RET_V, (h + 1) * RET_V)
            gn = gn_ref[:, cols]
            ohat, rstd = _group_norm(o_ref[:, cols].astype(F32))
            g = g_ref[:, cols].astype(F32)
            sg = jax.nn.sigmoid(g)
            dyh = _dot_nt(doutb, w_ref[cols, :])
            sil = g * sg
            dgate_ref[:, cols] = (dyh * (ohat * gn) * (sg + sil * (1.0 - sg))).astype(BF16)
            don = dyh * sil
            dgn_ref[:, cols] += _colsum(don * ohat)
            dohat = don * gn
            do_ref[:, cols] = (rstd * (dohat - jnp.mean(dohat, axis=-1, keepdims=True)
                                       - ohat * jnp.mean(dohat * ohat, axis=-1, keepdims=True))).astype(BF16)

    row = pl.BlockSpec((tm, d), lambda i: (i, 0))
    wide = pl.BlockSpec((tm, v_all), lambda i: (i, 0))
    gate = pl.BlockSpec((tm, v_all), lambda i: (i, 2))
    return pl.pallas_call(
        body, grid=(nt,),
        in_specs=[row, row, wide, gate, _full((1, v_all)), _full((1, d)), _full(w_out.shape)],
        out_specs=[row, gate, wide, _full((1, d)), _full((1, v_all))],
        out_shape=[jax.ShapeDtypeStruct((t, d), BF16), jax.ShapeDtypeStruct((t, RET_IN), BF16),
                   jax.ShapeDtypeStruct((t, v_all), BF16), jax.ShapeDtypeStruct((1, d), F32),
                   jax.ShapeDtypeStruct((1, v_all), F32)],
        compiler_params=_params(1), name="ret_out_bwd")(dxo, out, o, pj, gn_gain, g_post, w_out)


def ret_core_bwd(pj, do, sprev, cos, sin, dpj, intra, cross, inner, chunk_decay):
    t = pj.shape[0]
    nc = t // RET_CHUNK
    c = RET_CHUNK
    qk_all = RET_HEADS * RET_QK
    v_all = RET_HEADS * RET_V
    scale = RET_QK ** -0.5

    def body(q_ref, k_ref, v_ref, do_ref, sp_ref, c_ref, s_ref, dm_ref, cr_ref, in_ref, dpj_in, dpj_ref, dstate):
        @pl.when(pl.program_id(0) == 0)
        def _():
            dstate[...] = jnp.zeros_like(dstate)

        cs = c_ref[...]
        sn = s_ref[...]
        for h in range(RET_HEADS):
            q = q_ref[:, h * RET_QK:(h + 1) * RET_QK]
            k = k_ref[:, h * RET_QK:(h + 1) * RET_QK]
            v = v_ref[:, h * RET_V:(h + 1) * RET_V]
            doh = do_ref[:, h * RET_V:(h + 1) * RET_V]
            dm = dm_ref[h]
            ab = (_dot_nt(q, k) * dm).astype(BF16)
            dab = (_dot_nt(doh, v) * dm).astype(BF16)
            dsb = dstate[h].astype(BF16)
            kd = (k.astype(F32) * in_ref[h]).astype(BF16)
            dv = _dot_tn(ab, doh) + _dot(kd, dsb)
            dq = _dot(dab, k) + cr_ref[h] * _dot_nt(doh, sp_ref[h])
            dk = _dot_tn(dab, q) + in_ref[h] * _dot_nt(v, dsb)
            qd = (q.astype(F32) * cr_ref[h]).astype(BF16)
            dstate[h] = dstate[h] * chunk_decay[h] + _dot_tn(qd, doh)
            for base, dd, sc in ((h * RET_QK, dq, scale), (qk_all + h * RET_QK, dk, 1.0)):
                d1, d2 = dd[:, :128], dd[:, 128:]
                dpj_ref[:, base:base + 128] = ((d1 * cs + d2 * sn) * sc).astype(BF16)
                dpj_ref[:, base + 128:base + RET_QK] = ((d2 * cs - d1 * sn) * sc).astype(BF16)
            dpj_ref[:, 2 * qk_all + h * RET_V:2 * qk_all + (h + 1) * RET_V] = dv.astype(BF16)

    rev = lambda n: nc - 1 - n
    tab = pl.BlockSpec((c, 128), lambda n: (rev(n), 0))
    return pl.pallas_call(
        body, grid=(nc,),
        in_specs=[pl.BlockSpec((c, qk_all), lambda n: (rev(n), 0)), pl.BlockSpec((c, qk_all), lambda n: (rev(n), 1)),
                  pl.BlockSpec((c, v_all), lambda n: (rev(n), 1)), pl.BlockSpec((c, v_all), lambda n: (rev(n), 0)),
                  pl.BlockSpec((None, RET_HEADS, RET_QK, RET_V), lambda n: (rev(n), 0, 0, 0)),
                  tab, tab, _full(intra.shape), _full(cross.shape), _full(inner.shape), ANY],
        out_specs=pl.BlockSpec((c, 2 * qk_all + v_all), lambda n: (rev(n), 0)),
        out_shape=jax.ShapeDtypeStruct((t, RET_IN), BF16),
        scratch_shapes=[pltpu.VMEM((RET_HEADS, RET_QK, RET_V), F32)],
        input_output_aliases={10: 0},
        compiler_params=_params(1), name="ret_core_bwd")(pj, pj, pj, do, sprev, cos, sin, intra, cross, inner, dpj)


def ret_in_bwd(dpj, dres, x, g_pre, w_in, tm=512):
    t, d = x.shape
    nt = t // tm

    def body(dpj_ref, dres_ref, x_ref, g_ref, win_hbm, dx_ref, dg_ref, win_v, sem):
        @pl.when(pl.program_id(0) == 0)
        def _():
            cp = pltpu.make_async_copy(win_hbm, win_v, sem)
            cp.start()
            dg_ref[...] = jnp.zeros_like(dg_ref)
            cp.wait()

        g = g_ref[...]
        dh = jnp.zeros((tm, d), F32)
        for j in range(N_SHARD):
            dh = dh + _dot_nt(dpj_ref[:, j * RET_IN_SHARD:(j + 1) * RET_IN_SHARD], win_v[j])
        _, xh, rx = _rms_fwd(x_ref[...], g)
        dg_ref[...] += _colsum(dh * xh)
        dx_ref[...] = dres_ref[...] + _rms_bwd(xh, rx, g, dh)

    row = pl.BlockSpec((tm, d), lambda i: (i, 0))
    return pl.pallas_call(
        body, grid=(nt,),
        in_specs=[pl.BlockSpec((tm, RET_IN), lambda i: (i, 0)), row, row, _full((1, d)), ANY],
        out_specs=[row, _full((1, d))],
        out_shape=[jax.ShapeDtypeStruct((t, d), F32), jax.ShapeDtypeStruct((1, d), F32)],
        scratch_shapes=[pltpu.VMEM(w_in.shape, BF16), pltpu.SemaphoreType.DMA],
        compiler_params=_params(1), name="ret_in_bwd")(dpj, dres, x, g_pre, w_in)


_CHIP_FLIPS = ((1, 0), (0, 1), (1, 1))


def _flip(v, b):
    return 1 - v if b else v


_HBM = pl.BlockSpec(memory_space=pltpu.HBM)
_SEM = pl.BlockSpec(memory_space=pltpu.SEMAPHORE)
_EFFECT = pltpu.SideEffectType.DATAFLOW_SIDE_EFFECTING


def _chip_copies(mode, srcs, lands, send_sems, recv_sems):
    x, y, c = lax.axis_index("x"), lax.axis_index("y"), lax.axis_index("c")
    copies = []
    for t in range(len(lands)):
        if mode == "swap":
            copies.append(pltpu.make_async_remote_copy(
                src_ref=srcs[t], dst_ref=lands[t], send_sem=send_sems.at[t], recv_sem=recv_sems.at[t],
                device_id=(x, y, 1 - c), device_id_type=MESH))
            continue
        if mode == "everyone":
            for m in range(1, 8):
                bx, by, bc = (m >> 2) & 1, (m >> 1) & 1, m & 1
                copies.append(pltpu.make_async_remote_copy(
                    src_ref=srcs[t], dst_ref=lands[t].at[4 * x + 2 * y + c], send_sem=send_sems.at[7 * t + m - 1],
                    recv_sem=recv_sems.at[7 * t + m - 1], device_id=(_flip(x, bx), _flip(y, by), _flip(c, bc)),
                    device_id_type=MESH))
            continue
        for k, (bx, by) in enumerate(_CHIP_FLIPS):
            px, py = _flip(x, bx), _flip(y, by)
            target = (px, py, c)
            if mode == "gather":
                src, dst = srcs[t], lands[t].at[2 * x + y]
            elif mode == "gather_half":
                half = pl.ds(c * (srcs[t].shape[0] // 2), srcs[t].shape[0] // 2)
                src, dst = srcs[t].at[half], lands[t].at[2 * x + y, half]
            elif mode == "forward_half":
                half = pl.ds(c * (lands[t].shape[1] // 2), lands[t].shape[1] // 2)
                src = dst = lands[t].at[2 * px + py, half]
                target = (x, y, 1 - c)
            else:
                src, dst = srcs[t].at[2 * px + py], lands[t].at[k]
            copies.append(pltpu.make_async_remote_copy(
                src_ref=src, dst_ref=dst, send_sem=send_sems.at[3 * t + k], recv_sem=recv_sems.at[3 * t + k],
                device_id=target, device_id_type=MESH))
    return copies


def exchange_start(mode, srcs, lands, name, after=None):
    n, ns = len(lands), len(srcs)
    extra = [] if after is None else [after]

    def body(*refs):
        ins, lnd = refs[:ns], refs[ns:ns + n]
        send_sems, recv_sems = refs[ns + n + len(extra)], refs[ns + n + len(extra) + 1]
        token = refs[-1]
        for cp in _chip_copies(mode, ins, lnd, send_sems, recv_sems):
            cp.start()
        token[...] = jnp.zeros(token.shape, token.dtype)

    hbm = lambda a: pltpu.with_memory_space_constraint(a, pltpu.HBM)
    passed = list(srcs) + list(lands)
    n_sem = {"swap": 1, "everyone": 7}.get(mode, 3) * n
    return pl.pallas_call(
        body, name=name,
        out_shape=(pltpu.SemaphoreType.DMA((n_sem,)), pltpu.SemaphoreType.DMA((n_sem,)),
                   *[pltpu.HBM(a.shape, a.dtype) for a in passed], jax.ShapeDtypeStruct((8, 128), F32)),
        in_specs=[_HBM] * (ns + n) + [ANY] * len(extra),
        out_specs=(_SEM, _SEM, *[_HBM] * (ns + n), pl.BlockSpec(memory_space=pltpu.VMEM)),
        input_output_aliases={i: 2 + i for i in range(ns + n)},
        compiler_params=pltpu.CompilerParams(has_side_effects=_EFFECT))(*[hbm(a) for a in passed], *extra)


def exchange_wait(mode, started, after, name):
    send_sems, recv_sems = started[0], started[1]
    passed = list(started[2:-1])
    n = len(passed) if mode == "forward_half" else len(passed) // 2
    ns = len(passed) - n

    def body(*refs):
        ins, lnd = refs[:ns], refs[ns:ns + n]
        for cp in _chip_copies(mode, ins, lnd, refs[ns + n], refs[ns + n + 1]):
            cp.wait_send()
            cp.wait_recv()

    outs = pl.pallas_call(
        body, name=name, out_shape=tuple(pltpu.HBM(a.shape, a.dtype) for a in passed),
        in_specs=[_HBM] * (ns + n) + [_SEM, _SEM, ANY], out_specs=tuple([_HBM] * (ns + n)),
        input_output_aliases={i: i for i in range(ns + n)},
        compiler_params=pltpu.CompilerParams(has_side_effects=_EFFECT))(*passed, send_sems, recv_sems, after)
    return list(outs[:ns]), list(outs[ns:])


def plane_sum(slot, full, recv, name, bm=256):
    _, m, n = full.shape
    bm = _row_block(m, bm)

    def body(slot_ref, o_ref, r_ref, s_ref):
        s_ref[...] = ((o_ref[...] + r_ref[0].astype(F32)) + r_ref[1].astype(F32)) + r_ref[2].astype(F32)

    return pl.pallas_call(
        body,
        grid_spec=pltpu.PrefetchScalarGridSpec(
            num_scalar_prefetch=1, grid=(m // bm,),
            in_specs=[pl.BlockSpec((None, bm, n), lambda i, s: (s[0], i, 0)),
                      pl.BlockSpec((3, bm, n), lambda i, s: (0, i, 0))],
            out_specs=pl.BlockSpec((bm, n), lambda i, s: (i, 0))),
        out_shape=jax.ShapeDtypeStruct((m, n), F32), compiler_params=_params(1), name=name)(slot, full, recv)


def sum_slots(parts, name, bm=312):
    _, r, n = parts.shape
    bm = bm if r % bm == 0 else r

    def body(p_ref, s_ref):
        acc = p_ref[0]
        for k in range(1, 8):
            acc = acc + p_ref[k]
        s_ref[...] = acc

    return pl.pallas_call(
        body, grid=(r // bm,), in_specs=[pl.BlockSpec((8, bm, n), lambda i: (0, i, 0))],
        out_specs=pl.BlockSpec((bm, n), lambda i: (i, 0)), out_shape=jax.ShapeDtypeStruct((r, n), F32),
        compiler_params=_params(1), name=name)(parts)


def _adamw_math(w, g, m, v):
    m = ADAM_B1 * m + (1.0 - ADAM_B1) * g
    v = ADAM_B2 * v + (1.0 - ADAM_B2) * (g * g)
    m_hat = m / (1.0 - ADAM_B1 ** ADAM_STEP)
    v_hat = v / (1.0 - ADAM_B2 ** ADAM_STEP)
    delta = -ADAM_LR * (m_hat / (jnp.sqrt(v_hat) + ADAM_EPS) + ADAM_WD * w)
    return delta, m, v


def adamw(w, m, v, grads, layer, prev, name, bm=256, row0=0):
    _, _, n = w.shape
    mm = grads[0].shape[0]
    bm = _row_block(mm, bm)
    first = row0 // bm
    ng = len(grads)

    def body(*refs):
        w_ref, m_ref, v_ref = refs[:3]
        g_refs = refs[3:3 + ng]
        g_out, d_out, m_out, v_out = refs[-4:]
        g = g_refs[0][...]
        for gr in g_refs[1:]:
            g = g + gr[...]
        delta, mn, vn = _adamw_math(w_ref[...], g, m_ref[...], v_ref[...])
        g_out[...] = g
        d_out[...] = delta
        m_out[...] = mn
        v_out[...] = vn

    slab = pl.BlockSpec((None, bm, n), lambda i: (layer, first + i, 0))
    flat = pl.BlockSpec((bm, n), lambda i: (i, 0))
    in_specs = [slab] * 3 + [flat] * ng
    args = [w, m, v, *grads]
    aliases = {}
    if prev is not None:
        in_specs += [ANY] * 4
        aliases = {3 + ng + q: q for q in range(4)}
        args += list(prev)
    return pl.pallas_call(
        body, grid=(mm // bm,), in_specs=in_specs, out_specs=[slab] * 4,
        out_shape=[jax.ShapeDtypeStruct(w.shape, F32)] * 4, input_output_aliases=aliases,
        compiler_params=_params(1), name=name)(*args)


def _pack_rows(parts, rows):
    flat = jnp.concatenate([p.reshape(-1) for p in parts])
    return jnp.pad(flat, (0, rows * 128 - flat.shape[0])).reshape(rows, 128)


def _as_shards(a, rows):
    return a.reshape(N_SHARD, rows, a.shape[-1])


def _local_step(x, pos_col, target, gains, pool_w, pool_scale, gn_gain, conv_w, conv_b, weights, send_grads):
    def gain(l, n, token=None):
        g = gains[l, n].reshape(1, D_MODEL)
        return g if token is None else g + token[0:1, 0:1]

    inv_freq = (ROPE_BASE ** (-jnp.arange(0, RET_QK, 2, dtype=F32) / RET_QK)).reshape(1, RET_QK // 2)
    intra, cross, inner, chunk_decay = _decay_tables()
    dn_rows = D_FF // N_SHARD

    x1 = pool_fwd(x, gain(0, 0), gain(0, 1), pool_w, pool_scale)
    cos, sin = rope_tables(pos_col, inv_freq, x1)
    w_up0, w_dn0 = weights("mlp0", cos)
    w_dn0 = w_dn0.reshape(D_FF, D_MODEL)
    x2, u0, s0, a0, f0, h0 = mlp_fwd(x1, gain(0, 2), gain(0, 3), w_up0, w_dn0, conv_w[0], conv_b[0])
    w_in, w_out = weights("ret", x2)
    w_out = w_out.reshape(RET_HEADS * RET_V, D_MODEL)
    pj, hr = ret_proj(x2, gain(1, 0), w_in, cos, sin)
    o, sprev = ret_core_fwd(pj, intra, cross, inner, chunk_decay)
    x3, yb, out = ret_out_fwd(o, pj, x2, gn_gain, gain(1, 1), w_out)
    w_up1, w_dn1 = weights("mlp1", x3)
    w_dn1 = w_dn1.reshape(D_FF, D_MODEL)
    dx4, u1, s1, a1, f1, h1, loss = mlp_fwd(x3, gain(1, 2), gain(1, 3), w_up1, w_dn1, conv_w[1], conv_b[1], target)

    dx3, du1, df1, dg12, dg13, dcw1, dcb1 = mlp_bwd(
        dx4, f1, x3, u1, s1, gain(1, 2), gain(1, 3), w_up1, w_dn1, conv_w[1])
    dwup1 = grad_matmul(h1, du1, D_MODEL, FF_CHUNK, "grad_w_up_1", interleaved=True)
    dwdn1 = grad_matmul(a1, df1, FF_CHUNK, D_MODEL, "grad_w_down_1")
    tok = send_grads("mlp1", [dwup1, [_as_shards(g, dn_rows) for g in dwdn1]])
    dout, dpj, do, dg11, dgn = ret_out_bwd(dx3, out, o, pj, gn_gain, gain(1, 1, tok), w_out)
    dwout = grad_matmul(yb, dout, 1024, D_MODEL, "grad_w_out")
    dpj = ret_core_bwd(pj, do, sprev, cos, sin, dpj, intra, cross, inner, chunk_decay)
    dwin = grad_matmul(hr, dpj, D_MODEL, RET_IN_SHARD, "grad_w_in")
    tok = send_grads("ret", [dwin, [_as_shards(g, RET_V) for g in dwout]])
    dx2, dg10 = ret_in_bwd(dpj, dx3, x2, gain(1, 0, tok), w_in)
    dx1, du0, df0, dg02, dg03, dcw0, dcb0 = mlp_bwd(
        dx2, f0, x1, u0, s0, gain(0, 2), gain(0, 3), w_up0, w_dn0, conv_w[0])
    dwdn0 = grad_matmul(a0, df0, FF_CHUNK, D_MODEL, "grad_w_down_0")
    tok = send_grads("down0", [[_as_shards(g, dn_rows) for g in dwdn0]])
    half = D_MODEL // 2
    for part, first in (("a", 0), ("b", half)):
        dwup0 = grad_matmul(h0, du0, half, FF_CHUNK, "grad_w_up_0" + part, tk=4096, interleaved=True, after=tok,
                            cols=(first, half))
        tok = send_grads("up0" + part, [dwup0])
    dx0, dg00, dg01, dpscale, dpw = pool_bwd(dx1, x, gain(0, 0, tok), gain(0, 1), pool_w, pool_scale)

    rows = lambda g: jnp.sum(g, axis=0, keepdims=True)
    dgains = jnp.concatenate([dg00, dg01, rows(dg02), rows(dg03), dg10, dg11, rows(dg12), rows(dg13)],
                             axis=0).reshape(2, 4, D_MODEL)
    small = {"gains": dgains, "pool_scale": dpscale, "gn": dgn,
             "conv_w": jnp.sum(jnp.stack([dcw0, dcw1]), axis=3),
             "conv_b": jnp.sum(jnp.stack([dcb0, dcb1]), axis=2, keepdims=True), "pool_w": dpw}
    return loss, dx0, small


def kernel(x, positions, norm_gain, pool_w, pool_scale, ret_w_in, ret_gn_gain, ret_w_out, mlp_w_up, mlp_conv_w, mlp_conv_b, mlp_w_down, loss_target, m_norm_gain, m_pool_w, m_pool_scale, m_ret_w_in, m_ret_gn_gain, m_ret_w_out, m_mlp_w_up, m_mlp_conv_w, m_mlp_conv_b, m_mlp_w_down, v_norm_gain, v_pool_w, v_pool_scale, v_ret_w_in, v_ret_gn_gain, v_ret_w_out, v_mlp_w_up, v_mlp_conv_w, v_mlp_conv_b, v_mlp_w_down):
    t = x.shape[1]
    me = 2 * lax.axis_index("x") + lax.axis_index("y")
    me_slot = jnp.reshape(me, (1,)).astype(jnp.int32)

    small_parts = [norm_gain, ret_gn_gain, mlp_conv_w, pool_w]
    small_sizes = [p.size for p in small_parts]
    small_rows = -(-sum(small_sizes) // (128 * 8)) * 8
    gathers = {}

    def start_gather(group, srcs, after):
        lands = [lax.dynamic_update_index_in_dim(lax.empty((N_SHARD,) + s.shape, s.dtype), s, me, 0) for s in srcs]
        mode = "gather_half" if group == "mlp0" else "gather"
        gathers[group] = (mode, exchange_start(mode, srcs, lands, "gather_start_" + group, after=after))
        return gathers[group][1][-1]

    token = start_gather("small", [_pack_rows(small_parts, small_rows)], None)
    token = start_gather("mlp0", [mlp_w_up[0].astype(BF16), mlp_w_down[0].astype(BF16)], token)

    def weights(group, after):
        if group == "mlp0":
            tok = start_gather("ret", [ret_w_in[0].astype(BF16), ret_w_out[0].astype(BF16)], after)
            after = start_gather("mlp1", [mlp_w_up[1].astype(BF16), mlp_w_down[1].astype(BF16)], tok)
        mode, started = gathers[group]
        _, lands = exchange_wait(mode, started, after, "gather_wait_" + group)
        if mode == "gather_half":
            forward = exchange_start("forward_half", [], lands, "forward_start_" + group)
            _, lands = exchange_wait("forward_half", forward, forward[-1], "forward_wait_" + group)
        return lands

    sent, early = {}, {}

    def reduced(group, after, names):
        started, own = sent[group]
        _, recv = exchange_wait("scatter", started, after, "scatter_wait_" + group)
        return [plane_sum(me_slot, f, r, "plane_sum_" + nm)
                for f, r, nm in zip(own, recv, names)]

    def swap_start(planes, name):
        return exchange_start("swap", planes, [lax.empty(p.shape, p.dtype) for p in planes], name)

    def send_grads(group, pairs):
        lands = [lax.empty((3,) + b.shape[1:], BF16) for _, b in pairs]
        sent[group] = (exchange_start("scatter", [b for _, b in pairs], lands, "scatter_start_" + group),
                       [f for f, _ in pairs])
        token = sent[group][0][-1]
        if group == "down0":
            marker = pairs[0][1]
            early["planes"] = (reduced("mlp1", marker, ["w_up_1", "w_down_1"])
                               + reduced("ret", marker, ["w_in", "w_out"]))
            early["swap"] = swap_start(early["planes"], "swap_start_a")
            token = token + early["swap"][-1]
        return token

    (smallg,) = weights("small", token)
    smallg = smallg.reshape(N_SHARD, -1)
    offs = np.cumsum([0] + small_sizes)
    piece = lambda i, shape: smallg[:, offs[i]:offs[i + 1]].reshape((N_SHARD,) + shape)
    gains = piece(0, (2, 4, 256)).transpose(1, 2, 0, 3).reshape(2, 4, D_MODEL)
    gn_full = piece(1, (512,)).reshape(1, RET_HEADS * RET_V)
    cw_full = piece(2, (2, 3, FF_CHUNK)).transpose(1, 0, 2, 3)
    pw_full = piece(3, (4, 64, 256)).transpose(1, 0, 2, 3).reshape(4, 256, 256).astype(BF16)
    cb_full = mlp_conv_b.reshape(2, N_SHARD, 1, FF_CHUNK)

    loss, dx0, small = _local_step(
        x[0], positions.reshape(t, 1).astype(F32), loss_target[0], gains, pw_full, pool_scale, gn_full,
        cw_full, cb_full, weights, send_grads)

    def small_adamw(w, m, v, grads, name):
        w3 = w.reshape(1, -1, w.shape[-1])
        out = adamw(w3, m.reshape(w3.shape), v.reshape(w3.shape), [g.reshape(w3.shape[1:]) for g in grads], 0, None, name)
        return [o.reshape(w.shape) for o in out]

    pw_f = small["pool_w"].reshape(4, N_SHARD, 64, 256).transpose(1, 0, 2, 3).reshape(N_SHARD, 256, 256)
    small_order = ["gains", "pool_scale", "gn", "conv_w", "conv_b"]
    gsmall_sizes = [small[k].size for k in small_order]
    gsmall_rows = -(-sum(gsmall_sizes) // (128 * 8)) * 8
    gpack = _pack_rows([small[k] for k in small_order], gsmall_rows)
    mine = 2 * me + lax.axis_index("c")
    small_started = exchange_start(
        "everyone", [gpack], [lax.dynamic_update_index_in_dim(lax.empty((8,) + gpack.shape, F32), gpack, mine, 0)],
        "small_start")
    send_grads("pool_w", [(pw_f, pw_f.astype(BF16))])

    res = {}
    planes_a, others_a = exchange_wait("swap", early["swap"], small_started[-1], "swap_wait_a")
    res["ret_w_in"] = adamw(ret_w_in, m_ret_w_in, v_ret_w_in, (planes_a[2], others_a[2]), 0, None, "adamw_w_in")
    res["ret_w_out"] = adamw(ret_w_out, m_ret_w_out, v_ret_w_out, (planes_a[3], others_a[3]), 0, None, "adamw_w_out")
    up1 = adamw(mlp_w_up, m_mlp_w_up, v_mlp_w_up, (planes_a[0], others_a[0]), 1, None, "adamw_w_up_1")
    dn1 = adamw(mlp_w_down, m_mlp_w_down, v_mlp_w_down, (planes_a[1], others_a[1]), 1, None, "adamw_w_down_1")

    planes_b = (reduced("up0a", dn1[0], ["w_up_0a"]) + reduced("up0b", dn1[0], ["w_up_0b"])
                + reduced("down0", dn1[0], ["w_down_0"]) + reduced("pool_w", dn1[0], ["pool_w"]))
    swap_b = swap_start(planes_b, "swap_start_b")

    _, (small_recv,) = exchange_wait("everyone", small_started, swap_b[-1], "small_wait")
    gsmall = sum_slots(small_recv, "sum_small").reshape(-1)
    goffs = np.cumsum([0] + gsmall_sizes)
    gpiece = lambda i: gsmall[goffs[i]:goffs[i + 1]].reshape(small[small_order[i]].shape)
    g_gains = lax.dynamic_slice_in_dim(gpiece(0), me * 256, 256, axis=2)
    g_gn = lax.dynamic_slice_in_dim(gpiece(2), me * RET_V, RET_V, axis=1)
    g_cw = lax.dynamic_index_in_dim(gpiece(3), me, 1, keepdims=False)
    res["norm_gain"] = small_adamw(norm_gain, m_norm_gain, v_norm_gain, [g_gains], "adamw_norm_gain")
    res["pool_scale"] = small_adamw(pool_scale, m_pool_scale, v_pool_scale, [gpiece(1)], "adamw_pool_scale")
    res["ret_gn_gain"] = small_adamw(ret_gn_gain, m_ret_gn_gain, v_ret_gn_gain, [g_gn], "adamw_gn_gain")
    res["mlp_conv_w"] = small_adamw(mlp_conv_w, m_mlp_conv_w, v_mlp_conv_w, [g_cw], "adamw_conv_w")
    res["mlp_conv_b"] = small_adamw(mlp_conv_b, m_mlp_conv_b, v_mlp_conv_b, [gpiece(4)], "adamw_conv_b")

    planes_b, others_b = exchange_wait("swap", swap_b, res["mlp_conv_b"][0], "swap_wait_b")
    up0a = adamw(mlp_w_up, m_mlp_w_up, v_mlp_w_up, (planes_b[0], others_b[0]), 0, up1, "adamw_w_up_0a")
    res["mlp_w_up"] = adamw(mlp_w_up, m_mlp_w_up, v_mlp_w_up, (planes_b[1], others_b[1]), 0, up0a, "adamw_w_up_0b",
                            row0=D_MODEL // 2)
    res["mlp_w_down"] = adamw(mlp_w_down, m_mlp_w_down, v_mlp_w_down, (planes_b[2], others_b[2]), 0, dn1,
                              "adamw_w_down_0")
    res["pool_w"] = small_adamw(pool_w, m_pool_w, v_pool_w, (planes_b[3], others_b[3]), "adamw_pool_w")

    order = ["norm_gain", "pool_w", "pool_scale", "ret_w_in", "ret_gn_gain", "ret_w_out", "mlp_w_up", "mlp_conv_w",
             "mlp_conv_b", "mlp_w_down"]
    total_loss = lax.psum(loss[0, 0], ("x", "y", "c"))
    outs = [total_loss, dx0.reshape(x.shape)]
    for q in range(4):
        outs += [res[k][q] for k in order]
    return tuple(outs)
```

```python
import numpy as np
import jax
import jax.numpy as jnp
from jax import lax
from jax.experimental import pallas as pl
from jax.experimental.pallas import tpu as pltpu

F32 = jnp.float32
BF16 = jnp.bfloat16

D_MODEL = 1024
D_FF = 2816
FF_CHUNK = 1408
N_SHARD = 4
POOL_WINDOWS = (2, 4, 8, 16)
POOL_DIM = 256
POOL_HALO = 16
RET_HEADS = 4
RET_QK = 256
RET_V = 512
RET_CHUNK = 256
RET_STEP_CHUNKS = 2
RET_IN = 6144
RET_IN_SHARD = 1536
ROPE_BASE = 10000.0
EPS = 1e-6
CONV_HALO = 8

ADAM_LR, ADAM_B1, ADAM_B2, ADAM_EPS, ADAM_WD, ADAM_STEP = 0.001, 0.9, 0.999, 1e-08, 0.01, 10

VMEM_LIMIT = 56 * 1024 * 1024
VMEM_LIMIT_MLP_BWD = 62 * 1024 * 1024
MESH = pl.DeviceIdType.MESH
ANY = pl.BlockSpec(memory_space=pl.ANY)


def _params(n_grid=1, limit=VMEM_LIMIT):
    return pltpu.CompilerParams(dimension_semantics=("arbitrary",) * n_grid, vmem_limit_bytes=limit)


def _dot(a, b):
    return jnp.dot(a, b, preferred_element_type=F32)


def _dot_nt(a, b):
    return lax.dot_general(a, b, (((1,), (1,)), ((), ())), preferred_element_type=F32)


def _dot_tn(a, b):
    return lax.dot_general(a, b, (((0,), (0,)), ((), ())), preferred_element_type=F32)


def _rms_fwd(x, gain):
    r = lax.rsqrt(jnp.mean(x * x, axis=-1, keepdims=True) + EPS)
    xh = x * r
    return xh * gain, xh, r


def _rms_bwd(xh, r, gain, dy):
    dxh = dy * gain
    return r * (dxh - xh * jnp.mean(dxh * xh, axis=-1, keepdims=True))


def _colsum(v):
    return jnp.sum(v, axis=0, keepdims=True)


def _full(shape):
    nd = len(shape)
    return pl.BlockSpec(shape, lambda *_: (0,) * nd)


def rope_tables(pos_col, inv_freq, after):
    t = pos_col.shape[0]
    tm = min(t, 1024)

    def body(p_ref, f_ref, after_ref, c_ref, s_ref):
        ang = p_ref[...] * f_ref[...]
        c_ref[...] = jnp.cos(ang)
        s_ref[...] = jnp.sin(ang)

    return pl.pallas_call(
        body, grid=(t // tm,),
        in_specs=[pl.BlockSpec((tm, 1), lambda i: (i, 0)), _full((1, 128)), ANY],
        out_specs=[pl.BlockSpec((tm, 128), lambda i: (i, 0))] * 2,
        out_shape=[jax.ShapeDtypeStruct((t, 128), F32)] * 2,
        compiler_params=_params(1), name="rope_tables")(pos_col, inv_freq, after)


def _window_sums(ext, backward):
    n = ext.shape[0]
    cur, sums = ext, []
    for g, win in enumerate(POOL_WINDOWS):
        if g > 0:
            cur = cur[:, POOL_DIM:]
        half = win // 2
        cur = cur + pltpu.roll(cur, n - half if backward else half, axis=0)
        sums.append(cur[:, 0:POOL_DIM])
    return sums


def _pool_diff(h_halo, h, row0, tm):
    t_idx = row0 + lax.broadcasted_iota(jnp.int32, (tm, 1), 0)
    sums = _window_sums(jnp.concatenate([h_halo, h], axis=0), backward=False)
    parts, inv_counts = [], []
    for g, win in enumerate(POOL_WINDOWS):
        inv = 1.0 / jnp.minimum(t_idx + 1, win).astype(F32)
        parts.append(sums[g][POOL_HALO:, :] * inv - h[:, g * POOL_DIM:(g + 1) * POOL_DIM])
        inv_counts.append(inv)
    return parts, inv_counts


def pool_fwd(x, g_pre, g_post, pool_w, pool_scale, tm=512):
    t, d = x.shape
    nt = t // tm

    def body(x_ref, g0_ref, g1_ref, w_ref, sc_ref, o_ref, hext):
        i = pl.program_id(0)

        @pl.when(i == 0)
        def _():
            hext[...] = jnp.zeros((POOL_HALO, d), F32)

        xv = x_ref[...]
        h, _, _ = _rms_fwd(xv, g0_ref[...])
        parts, _ = _pool_diff(hext[...], h, i * tm, tm)
        hext[...] = h[tm - POOL_HALO:tm, :]
        ys = [_dot(parts[g].astype(BF16), w_ref[g]) for g in range(len(POOL_WINDOWS))]
        y = jnp.concatenate(ys, axis=-1) * sc_ref[...]
        m, _, _ = _rms_fwd(y, g1_ref[...])
        o_ref[...] = xv + m

    row = pl.BlockSpec((tm, d), lambda i: (i, 0))
    return pl.pallas_call(
        body, grid=(nt,),
        in_specs=[row, _full((1, d)), _full((1, d)), _full(pool_w.shape), _full((1, d))],
        out_specs=row, out_shape=jax.ShapeDtypeStruct((t, d), F32),
        scratch_shapes=[pltpu.VMEM((POOL_HALO, d), F32)],
        compiler_params=_params(1), name="pool_fwd")(x, g_pre, g_post, pool_w, pool_scale)


def pool_bwd(dx1, x, g_pre, g_post, pool_w, pool_scale, tm=512):
    t, d = x.shape
    nt = t // tm
    ng = len(POOL_WINDOWS)

    def body(dx1_ref, x_ref, xh_ref, g0_ref, g1_ref, w_ref, sc_ref,
             dx_ref, dg0_ref, dg1_ref, dsc_ref, dw_ref, enext):
        i = pl.program_id(0)
        r = nt - 1 - i

        @pl.when(i == 0)
        def _():
            enext[...] = jnp.zeros((POOL_HALO, d), F32)
            dg0_ref[...] = jnp.zeros_like(dg0_ref)
            dg1_ref[...] = jnp.zeros_like(dg1_ref)
            dsc_ref[...] = jnp.zeros_like(dsc_ref)
            dw_ref[...] = jnp.zeros_like(dw_ref)

        g0 = g0_ref[...]
        g1 = g1_ref[...]
        sc = sc_ref[...]
        xv = x_ref[...]
        h, xh, rx = _rms_fwd(xv, g0)
        h_halo, _, _ = _rms_fwd(xh_ref[...], g0)
        parts, inv_counts = _pool_diff(h_halo * jnp.where(r > 0, 1.0, 0.0), h, r * tm, tm)
        parts_b = [p.astype(BF16) for p in parts]
        ypre = jnp.concatenate([_dot(parts_b[g], w_ref[g]) for g in range(ng)], axis=-1)
        _, yh, ry = _rms_fwd(ypre * sc, g1)
        dm = dx1_ref[...]
        dg1_ref[...] += _colsum(dm * yh)
        dy = _rms_bwd(yh, ry, g1, dm)
        dsc_ref[...] += _colsum(dy * ypre)
        dyp = (dy * sc).astype(BF16)
        ddiffs = []
        for g in range(ng):
            cols = slice(g * POOL_DIM, (g + 1) * POOL_DIM)
            dw_ref[g] += _dot_tn(parts_b[g], dyp[:, cols])
            ddiffs.append(_dot_nt(dyp[:, cols], w_ref[g]))
        e = jnp.concatenate([ddiffs[g] * inv_counts[g] for g in range(ng)], axis=-1)
        sums = _window_sums(jnp.concatenate([e, enext[...]], axis=0), backward=True)
        enext[...] = e[0:POOL_HALO, :]
        dh = jnp.concatenate([sums[g][0:tm, :] - ddiffs[g] for g in range(ng)], axis=-1)
        dg0_ref[...] += _colsum(dh * xh)
        dx_ref[...] = dm + _rms_bwd(xh, rx, g0, dh)

    row = pl.BlockSpec((tm, d), lambda i: (nt - 1 - i, 0))
    halo = pl.BlockSpec((POOL_HALO, d), lambda i: (jnp.maximum((nt - 1 - i) * (tm // POOL_HALO) - 1, 0), 0))
    vec = _full((1, d))
    return pl.pallas_call(
        body, grid=(nt,),
        in_specs=[row, row, halo, vec, vec, _full(pool_w.shape), vec],
        out_specs=[row, vec, vec, vec, _full((ng, POOL_DIM, POOL_DIM))],
        out_shape=[jax.ShapeDtypeStruct((t, d), F32)] + [jax.ShapeDtypeStruct((1, d), F32)] * 3
        + [jax.ShapeDtypeStruct((ng, POOL_DIM, POOL_DIM), F32)],
        scratch_shapes=[pltpu.VMEM((POOL_HALO, d), F32)],
        compiler_params=_params(1), name="pool_bwd")(dx1, x, x, g_pre, g_post, pool_w, pool_scale)


def _conv_taps(cw_ref, j):
    return cw_ref[j, 0:1, :], cw_ref[j, 1:2, :], cw_ref[j, 2:3, :]


def _row_block(m, target=256):
    if m <= target:
        return m
    for b in range(target, 7, -8):
        if m % b == 0:
            return b
    return m


def mlp_fwd(x, g_pre, g_post, w_up, w_down, conv_w, conv_b, target=None, tm=256):
    t, d = x.shape
    nt = t // tm
    h8 = CONV_HALO
    with_loss = target is not None
    n_extra = 1 if with_loss else 0

    def body(x_ref, g2_ref, g3_ref, wup_hbm, wdn_hbm, cw_ref, cb_ref, *rest):
        tgt_ref = rest[0] if with_loss else None
        xo_ref, u_ref, s_ref, a_ref, f_ref, h_ref = rest[n_extra:n_extra + 6]
        loss_ref = rest[n_extra + 6] if with_loss else None
        wup_v, wdn_v, tail, sem = rest[-4:]
        i = pl.program_id(0)

        @pl.when(i == 0)
        def _():
            c1 = pltpu.make_async_copy(wup_hbm, wup_v, sem.at[0])
            c2 = pltpu.make_async_copy(wdn_hbm, wdn_v, sem.at[1])
            c1.start()
            c2.start()
            tail[...] = jnp.zeros_like(tail)
            if with_loss:
                loss_ref[...] = jnp.zeros_like(loss_ref)
            c1.wait()
            c2.wait()

        xv = x_ref[...]
        h, _, _ = _rms_fwd(xv, g2_ref[...])
        hb = h.astype(BF16)
        h_ref[...] = hb
        acc = jnp.zeros((tm, d), F32)
        for k in range(2):
            cs = []
            for s in range(2):
                j, cols = k + 2 * s, slice((2 * k + s) * FF_CHUNK, (2 * k + s + 1) * FF_CHUNK)
                uf = _dot(hb, wup_v[j])
                u_ref[:, cols] = uf.astype(BF16)
                ext = jnp.concatenate([tail[j], uf], axis=0)
                tail[j] = uf[tm - h8:tm, :]
                w0, w1, w2 = _conv_taps(cw_ref, j)
                cs.append(cb_ref[j] + w2 * uf + w1 * pltpu.roll(ext, 1, axis=0)[h8:, :]
                          + w0 * pltpu.roll(ext, 2, axis=0)[h8:, :])
            cg, cv = cs
            sg = jax.nn.sigmoid(cg)
            sil = cg * sg
            ab = (sil * cv).astype(BF16)
            a_ref[:, k * FF_CHUNK:(k + 1) * FF_CHUNK] = ab
            s_ref[:, 2 * k * FF_CHUNK:(2 * k + 1) * FF_CHUNK] = sil.astype(BF16)
            s_ref[:, (2 * k + 1) * FF_CHUNK:(2 * k + 2) * FF_CHUNK] = (cv * (sg + sil * (1.0 - sg))).astype(BF16)
            acc = acc + _dot(ab, wdn_v[k * FF_CHUNK:(k + 1) * FF_CHUNK, :])
        f_ref[...] = acc
        y, _, _ = _rms_fwd(acc, g3_ref[...])
        if with_loss:
            err = (xv + y) - tgt_ref[...]
            xo_ref[...] = err * (1.0 / d)
            loss_ref[...] += 0.5 * jnp.sum(jnp.mean(err * err, axis=-1, keepdims=True), axis=0, keepdims=True)
        else:
            xo_ref[...] = xv + y

    row = pl.BlockSpec((tm, d), lambda i: (i, 0))
    wide = pl.BlockSpec((tm, 2 * D_FF), lambda i: (i, 0))
    vec = _full((1, d))
    extra = [target] if with_loss else []
    return pl.pallas_call(
        body, grid=(nt,),
        in_specs=[row, vec, vec, ANY, ANY, _full(conv_w.shape), _full(conv_b.shape)] + [row] * n_extra,
        out_specs=[row, wide, wide, pl.BlockSpec((tm, D_FF), lambda i: (i, 0)), row, row] + [_full((1, 1))] * n_extra,
        out_shape=[jax.ShapeDtypeStruct((t, d), F32), jax.ShapeDtypeStruct((t, 2 * D_FF), BF16),
                   jax.ShapeDtypeStruct((t, 2 * D_FF), BF16), jax.ShapeDtypeStruct((t, D_FF), BF16),
                   jax.ShapeDtypeStruct((t, d), F32), jax.ShapeDtypeStruct((t, d), BF16)]
        + [jax.ShapeDtypeStruct((1, 1), F32)] * n_extra,
        scratch_shapes=[pltpu.VMEM(w_up.shape, BF16), pltpu.VMEM(w_down.shape, BF16),
                        pltpu.VMEM((N_SHARD, h8, FF_CHUNK), F32), pltpu.SemaphoreType.DMA((2,))],
        compiler_params=_params(1), name="mlp_fwd_loss" if with_loss else "mlp_fwd")(
            x, g_pre, g_post, w_up, w_down, conv_w, conv_b, *extra)


def _rowsum8(v):
    return jnp.sum(v.reshape(v.shape[0] // 8, 8, v.shape[1]), axis=0)


def mlp_bwd(dxo, f, x, u, sp, g_pre, g_post, w_up, w_down, conv_w, tm=256):
    t, d = x.shape
    nt = t // tm
    h8 = CONV_HALO

    def body(dxo_ref, f_ref, x_ref, u_ref, s_ref, g2_ref, g3_ref, wup_hbm, wdn_hbm, cw_ref,
             dx_ref, du_ref, df_ref, dg2_ref, dg3_ref, dcw_ref, dcb_ref,
             wup_v, wdn_v, carry, sem):
        @pl.when(pl.program_id(0) == 0)
        def _():
            c1 = pltpu.make_async_copy(wup_hbm, wup_v, sem.at[0])
            c2 = pltpu.make_async_copy(wdn_hbm, wdn_v, sem.at[1])
            c1.start()
            c2.start()
            carry[...] = jnp.zeros_like(carry)
            dg2_ref[...] = jnp.zeros_like(dg2_ref)
            dg3_ref[...] = jnp.zeros_like(dg3_ref)
            dcw_ref[...] = jnp.zeros_like(dcw_ref)
            dcb_ref[...] = jnp.zeros_like(dcb_ref)
            c1.wait()
            c2.wait()

        g3 = g3_ref[...]
        dxo = dxo_ref[...]
        _, fh, rf = _rms_fwd(f_ref[...], g3)
        dg3_ref[...] += _rowsum8(dxo * fh)
        dfb = _rms_bwd(fh, rf, g3, dxo).astype(BF16)
        df_ref[...] = dfb
        dh = jnp.zeros((tm, d), F32)
        for k in range(2):
            da = _dot_nt(dfb, wdn_v[k * FF_CHUNK:(k + 1) * FF_CHUNK, :])
            for s in range(2):
                j = k + 2 * s
                cols = slice((2 * k + s) * FF_CHUNK, (2 * k + s + 1) * FF_CHUNK)
                dc = da * s_ref[:, (2 * k + 1 - s) * FF_CHUNK:(2 * k + 2 - s) * FF_CHUNK].astype(F32)
                uf = u_ref[:, cols].astype(F32)
                ext = jnp.concatenate([dc, carry[j]], axis=0)
                carry[j] = dc[0:h8, :]
                dc1 = pltpu.roll(ext, tm + h8 - 1, axis=0)[0:tm, :]
                dc2 = pltpu.roll(ext, tm + h8 - 2, axis=0)[0:tm, :]
                dcb_ref[j] += _rowsum8(dc)
                dcw_ref[j, 2] += _rowsum8(dc * uf)
                dcw_ref[j, 1] += _rowsum8(dc1 * uf)
                dcw_ref[j, 0] += _rowsum8(dc2 * uf)
                dub = (cw_ref[j, 2:3, :] * dc + cw_ref[j, 1:2, :] * dc1 + cw_ref[j, 0:1, :] * dc2).astype(BF16)
                du_ref[:, cols] = dub
                dh = dh + _dot_nt(dub, wup_v[j])
        g2 = g2_ref[...]
        _, xh, rx = _rms_fwd(x_ref[...], g2)
        dg2_ref[...] += _rowsum8(dh * xh)
        dx_ref[...] = dxo + _rms_bwd(xh, rx, g2, dh)

    row = pl.BlockSpec((tm, d), lambda i: (nt - 1 - i, 0))
    wide = pl.BlockSpec((tm, 2 * D_FF), lambda i: (nt - 1 - i, 0))
    vec = _full((1, d))
    acc = _full((8, d))
    dcw_shape, dcb_shape = (N_SHARD, 3, 8, FF_CHUNK), (N_SHARD, 8, FF_CHUNK)
    return pl.pallas_call(
        body, grid=(nt,),
        in_specs=[row, row, row, wide, wide, vec, vec, ANY, ANY, _full(conv_w.shape)],
        out_specs=[row, wide, row, acc, acc, _full(dcw_shape), _full(dcb_shape)],
        out_shape=[jax.ShapeDtypeStruct((t, d), F32), jax.ShapeDtypeStruct((t, 2 * D_FF), BF16),
                   jax.ShapeDtypeStruct((t, d), BF16),
                   jax.ShapeDtypeStruct((8, d), F32), jax.ShapeDtypeStruct((8, d), F32),
                   jax.ShapeDtypeStruct(dcw_shape, F32), jax.ShapeDtypeStruct(dcb_shape, F32)],
        scratch_shapes=[pltpu.VMEM(w_up.shape, BF16), pltpu.VMEM(w_down.shape, BF16),
                        pltpu.VMEM((N_SHARD, h8, FF_CHUNK), F32), pltpu.SemaphoreType.DMA((2,))],
        compiler_params=_params(1, VMEM_LIMIT_MLP_BWD), name="mlp_bwd")(
            dxo, f, x, u, sp, g_pre, g_post, w_up, w_down, conv_w)


def grad_matmul(a, b, bm, bn, name, tk=2048, interleaved=False, after=None, cols=None):
    t = a.shape[0]
    m0, m = (0, a.shape[1]) if cols is None else cols
    n = b.shape[1]
    tk = min(tk, t)
    nk = t // tk
    place = (lambda j: (j % 2) * 2 + j // 2) if interleaved else (lambda j: j)
    extra = [] if after is None else [after]
    first = m0 // bm

    def body(a_ref, b_ref, *rest):
        o_ref, ob_ref = rest[len(extra):]
        kk = pl.program_id(2)

        @pl.when(kk == 0)
        def _():
            o_ref[...] = jnp.zeros_like(o_ref)

        o_ref[...] += _dot_tn(a_ref[...], b_ref[...])

        @pl.when(kk == nk - 1)
        def _():
            ob_ref[...] = o_ref[...].astype(BF16)

    ospec = pl.BlockSpec((None, bm, bn), lambda j, i, kk: (place(j), i, 0))
    return pl.pallas_call(
        body, grid=(n // bn, m // bm, nk),
        in_specs=[pl.BlockSpec((tk, bm), lambda j, i, kk: (kk, first + i)),
                  pl.BlockSpec((tk, bn), lambda j, i, kk: (kk, j))]
        + [ANY] * len(extra),
        out_specs=[ospec, ospec],
        out_shape=[jax.ShapeDtypeStruct((n // bn, m, bn), F32), jax.ShapeDtypeStruct((n // bn, m, bn), BF16)],
        compiler_params=_params(3), name=name)(a, b, *extra)


def _decay_tables():
    log_gamma = jnp.log(1.0 - 2.0 ** (-5.0 - jnp.arange(RET_HEADS, dtype=F32)))
    i = jnp.arange(RET_CHUNK, dtype=F32)
    rel = i[:, None] - i[None, :]
    intra = jnp.where(rel >= 0, jnp.exp(jnp.maximum(rel, 0.0) * log_gamma[:, None, None]), 0.0)
    cross = jnp.exp((i + 1.0) * log_gamma[:, None])[:, :, None]
    inner = jnp.exp((RET_CHUNK - 1.0 - i) * log_gamma[:, None])[:, :, None]
    chunk = [float(np.exp(np.float32(RET_CHUNK) * np.log(np.float32(1.0 - 2.0 ** (-5.0 - h))).astype(np.float32)))
             for h in range(RET_HEADS)]
    return intra, cross, inner, chunk


def ret_proj(x, g_pre, w_in, cos, sin, tm=512):
    t, d = x.shape
    nt = t // tm
    per = RET_IN_SHARD // RET_QK

    def body(x_ref, g_ref, win_hbm, c_ref, s_ref, pj_ref, h_ref, win_v, sem):
        @pl.when(pl.program_id(0) == 0)
        def _():
            cp = pltpu.make_async_copy(win_hbm, win_v, sem)
            cp.start()
            cp.wait()

        h, _, _ = _rms_fwd(x_ref[...], g_ref[...])
        hb = h.astype(BF16)
        h_ref[...] = hb
        c = c_ref[...]
        s = s_ref[...]
        for j in range(N_SHARD):
            pjj = _dot(hb, win_v[j])
            for bb in range(per):
                b = per * j + bb
                blk = pjj[:, bb * RET_QK:(bb + 1) * RET_QK]
                if b < 2 * RET_HEADS:
                    x1, x2 = blk[:, :128], blk[:, 128:]
                    o1 = x1 * c - x2 * s
                    o2 = x2 * c + x1 * s
                    if b < RET_HEADS:
                        o1 = o1 * (RET_QK ** -0.5)
                        o2 = o2 * (RET_QK ** -0.5)
                    pj_ref[:, b * RET_QK:b * RET_QK + 128] = o1.astype(BF16)
                    pj_ref[:, b * RET_QK + 128:(b + 1) * RET_QK] = o2.astype(BF16)
                else:
                    pj_ref[:, b * RET_QK:(b + 1) * RET_QK] = blk.astype(BF16)

    row = pl.BlockSpec((tm, d), lambda i: (i, 0))
    tab = pl.BlockSpec((tm, 128), lambda i: (i, 0))
    return pl.pallas_call(
        body, grid=(nt,),
        in_specs=[row, _full((1, d)), ANY, tab, tab],
        out_specs=[pl.BlockSpec((tm, RET_IN), lambda i: (i, 0)), row],
        out_shape=[jax.ShapeDtypeStruct((t, RET_IN), BF16), jax.ShapeDtypeStruct((t, d), BF16)],
        scratch_shapes=[pltpu.VMEM(w_in.shape, BF16), pltpu.SemaphoreType.DMA],
        compiler_params=_params(1), name="ret_proj")(x, g_pre, w_in, cos, sin)


def ret_core_fwd(pj, intra, cross, inner, chunk_decay):
    t = pj.shape[0]
    nc = t // RET_CHUNK
    c = RET_CHUNK
    per = RET_STEP_CHUNKS
    qk_all = RET_HEADS * RET_QK
    v_all = RET_HEADS * RET_V

    def body(q_ref, k_ref, v_ref, dm_ref, cr_ref, in_ref, o_ref, sp_ref, state):
        @pl.when(pl.program_id(0) == 0)
        def _():
            state[...] = jnp.zeros_like(state)

        for h in range(RET_HEADS):
            for cc in range(per):
                rows = slice(cc * c, (cc + 1) * c)
                q = q_ref[rows, h * RET_QK:(h + 1) * RET_QK]
                k = k_ref[rows, h * RET_QK:(h + 1) * RET_QK]
                v = v_ref[rows, h * RET_V:(h + 1) * RET_V]
                sb = state[h].astype(BF16)
                sp_ref[cc, h] = sb
                sc = _dot_nt(q, k) * dm_ref[h]
                o_ref[rows, h * RET_V:(h + 1) * RET_V] = (_dot(sc.astype(BF16), v)
                                                          + _dot(q, sb) * cr_ref[h]).astype(BF16)
                kd = (k.astype(F32) * in_ref[h]).astype(BF16)
                state[h] = state[h] * chunk_decay[h] + _dot_tn(kd, v)

    return pl.pallas_call(
        body, grid=(nc // per,),
        in_specs=[pl.BlockSpec((per * c, qk_all), lambda n: (n, 0)), pl.BlockSpec((per * c, qk_all), lambda n: (n, 1)),
                  pl.BlockSpec((per * c, v_all), lambda n: (n, 1)),
                  _full(intra.shape), _full(cross.shape), _full(inner.shape)],
        out_specs=[pl.BlockSpec((per * c, v_all), lambda n: (n, 0)),
                   pl.BlockSpec((per, RET_HEADS, RET_QK, RET_V), lambda n: (n, 0, 0, 0))],
        out_shape=[jax.ShapeDtypeStruct((t, v_all), BF16),
                   jax.ShapeDtypeStruct((nc, RET_HEADS, RET_QK, RET_V), BF16)],
        scratch_shapes=[pltpu.VMEM((RET_HEADS, RET_QK, RET_V), F32)],
        compiler_params=_params(1), name="ret_core_fwd")(pj, pj, pj, intra, cross, inner)


def _group_norm(o_h):
    mu = jnp.mean(o_h, axis=-1, keepdims=True)
    dev = o_h - mu
    rstd = lax.rsqrt(jnp.mean(dev * dev, axis=-1, keepdims=True) + EPS)
    return dev * rstd, rstd


def ret_out_fwd(o, pj, x, gn_gain, g_post, w_out, tm=512):
    t, d = x.shape
    nt = t // tm
    v_all = RET_HEADS * RET_V

    def body(o_ref, g_ref, x_ref, gn_ref, g1_ref, w_ref, xo_ref, y_ref, out_ref):
        out = jnp.zeros((tm, d), F32)
        for h in range(RET_HEADS):
            cols = slice(h * RET_V, (h + 1) * RET_V)
            ohat, _ = _group_norm(o_ref[:, cols].astype(F32))
            g = g_ref[:, cols].astype(F32)
            yb = (g * jax.nn.sigmoid(g) * (ohat * gn_ref[:, cols])).astype(BF16)
            y_ref[:, cols] = yb
            out = out + _dot(yb, w_ref[cols, :])
        out_ref[...] = out
        m, _, _ = _rms_fwd(out, g1_ref[...])
        xo_ref[...] = x_ref[...] + m

    row = pl.BlockSpec((tm, d), lambda i: (i, 0))
    wide = pl.BlockSpec((tm, v_all), lambda i: (i, 0))
    return pl.pallas_call(
        body, grid=(nt,),
        in_specs=[wide, pl.BlockSpec((tm, v_all), lambda i: (i, 2)), row, _full((1, v_all)), _full((1, d)),
                  _full(w_out.shape)],
        out_specs=[row, wide, row],
        out_shape=[jax.ShapeDtypeStruct((t, d), F32), jax.ShapeDtypeStruct((t, v_all), BF16),
                   jax.ShapeDtypeStruct((t, d), F32)],
        compiler_params=_params(1), name="ret_out_fwd")(o, pj, x, gn_gain, g_post, w_out)


def ret_out_bwd(dxo, out, o, pj, gn_gain, g_post, w_out, tm=512):
    t, d = out.shape
    nt = t // tm
    v_all = RET_HEADS * RET_V

    def body(dxo_ref, out_ref, o_ref, g_ref, gn_ref, g1_ref, w_ref,
             dout_ref, dgate_ref, do_ref, dg1_ref, dgn_ref):
        @pl.when(pl.program_id(0) == 0)
        def _():
            dg1_ref[...] = jnp.zeros_like(dg1_ref)
            dgn_ref[...] = jnp.zeros_like(dgn_ref)

        g1 = g1_ref[...]
        dxo = dxo_ref[...]
        _, oh_, r_ = _rms_fwd(out_ref[...], g1)
        dg1_ref[...] += _colsum(dxo * oh_)
        doutb = _rms_bwd(oh_, r_, g1, dxo).astype(BF16)
        dout_ref[...] = doutb
        for h in range(RET_HEADS):
            cols = slice(h * RET_V, (h + 1) * RET_V)
            gn = gn_ref[:, cols]
            ohat, rstd = _group_norm(o_ref[:, cols].astype(F32))
            g = g_ref[:, cols].astype(F32)
            sg = jax.nn.sigmoid(g)
            dyh = _dot_nt(doutb, w_ref[cols, :])
            sil = g * sg
            dgate_ref[:, cols] = (dyh * (ohat * gn) * (sg + sil * (1.0 - sg))).astype(BF16)
            don = dyh * sil
            dgn_ref[:, cols] += _colsum(don * ohat)
            dohat = don * gn
            do_ref[:, cols] = (rstd * (dohat - jnp.mean(dohat, axis=-1, keepdims=True)
                                       - ohat * jnp.mean(dohat * ohat, axis=-1, keepdims=True))).astype(BF16)

    row = pl.BlockSpec((tm, d), lambda i: (i, 0))
    wide = pl.BlockSpec((tm, v_all), lambda i: (i, 0))
    gate = pl.BlockSpec((tm, v_all), lambda i: (i, 2))
    return pl.pallas_call(
        body, grid=(nt,),
        in_specs=[row, row, wide, gate, _full((1, v_all)), _full((1, d)), _full(w_out.shape)],
        out_specs=[row, gate, wide, _full((1, d)), _full((1, v_all))],
        out_shape=[jax.ShapeDtypeStruct((t, d), BF16), jax.ShapeDtypeStruct((t, RET_IN), BF16),
                   jax.ShapeDtypeStruct((t, v_all), BF16), jax.ShapeDtypeStruct((1, d), F32),
                   jax.ShapeDtypeStruct((1, v_all), F32)],
        compiler_params=_params(1), name="ret_out_bwd")(dxo, out, o, pj, gn_gain, g_post, w_out)


def ret_core_bwd(pj, do, sprev, cos, sin, dpj, intra, cross, inner, chunk_decay):
    t = pj.shape[0]
    nc = t // RET_CHUNK
    c = RET_CHUNK
    per = RET_STEP_CHUNKS
    qk_all = RET_HEADS * RET_QK
    v_all = RET_HEADS * RET_V
    scale = RET_QK ** -0.5

    def body(q_ref, k_ref, v_ref, do_ref, sp_ref, c_ref, s_ref, dm_ref, cr_ref, in_ref, dpj_in, dpj_ref, dstate):
        @pl.when(pl.program_id(0) == 0)
        def _():
            dstate[...] = jnp.zeros_like(dstate)

        for h in range(RET_HEADS):
            for cc in reversed(range(per)):
                rows = slice(cc * c, (cc + 1) * c)
                cs = c_ref[rows, :]
                sn = s_ref[rows, :]
                q = q_ref[rows, h * RET_QK:(h + 1) * RET_QK]
                k = k_ref[rows, h * RET_QK:(h + 1) * RET_QK]
                v = v_ref[rows, h * RET_V:(h + 1) * RET_V]
                doh = do_ref[rows, h * RET_V:(h + 1) * RET_V]
                dm = dm_ref[h]
                ab = (_dot_nt(q, k) * dm).astype(BF16)
                dab = (_dot_nt(doh, v) * dm).astype(BF16)
                dsb = dstate[h].astype(BF16)
                kd = (k.astype(F32) * in_ref[h]).astype(BF16)
                dv = _dot_tn(ab, doh) + _dot(kd, dsb)
                dq = _dot(dab, k) + cr_ref[h] * _dot_nt(doh, sp_ref[cc, h])
                dk = _dot_tn(dab, q) + in_ref[h] * _dot_nt(v, dsb)
                qd = (q.astype(F32) * cr_ref[h]).astype(BF16)
                dstate[h] = dstate[h] * chunk_decay[h] + _dot_tn(qd, doh)
                for base, dd, sc in ((h * RET_QK, dq, scale), (qk_all + h * RET_QK, dk, 1.0)):
                    d1, d2 = dd[:, :128], dd[:, 128:]
                    dpj_ref[rows, base:base + 128] = ((d1 * cs + d2 * sn) * sc).astype(BF16)
                    dpj_ref[rows, base + 128:base + RET_QK] = ((d2 * cs - d1 * sn) * sc).astype(BF16)
                dpj_ref[rows, 2 * qk_all + h * RET_V:2 * qk_all + (h + 1) * RET_V] = dv.astype(BF16)

    rev = lambda n: nc // per - 1 - n
    tab = pl.BlockSpec((per * c, 128), lambda n: (rev(n), 0))
    return pl.pallas_call(
        body, grid=(nc // per,),
        in_specs=[pl.BlockSpec((per * c, qk_all), lambda n: (rev(n), 0)),
                  pl.BlockSpec((per * c, qk_all), lambda n: (rev(n), 1)),
                  pl.BlockSpec((per * c, v_all), lambda n: (rev(n), 1)),
                  pl.BlockSpec((per * c, v_all), lambda n: (rev(n), 0)),
                  pl.BlockSpec((per, RET_HEADS, RET_QK, RET_V), lambda n: (rev(n), 0, 0, 0)),
                  tab, tab, _full(intra.shape), _full(cross.shape), _full(inner.shape), ANY],
        out_specs=pl.BlockSpec((per * c, 2 * qk_all + v_all), lambda n: (rev(n), 0)),
        out_shape=jax.ShapeDtypeStruct((t, RET_IN), BF16),
        scratch_shapes=[pltpu.VMEM((RET_HEADS, RET_QK, RET_V), F32)],
        input_output_aliases={10: 0},
        compiler_params=_params(1), name="ret_core_bwd")(pj, pj, pj, do, sprev, cos, sin, intra, cross, inner, dpj)


def ret_in_bwd(dpj, dres, x, g_pre, w_in, tm=512):
    t, d = x.shape
    nt = t // tm

    def body(dpj_ref, dres_ref, x_ref, g_ref, win_hbm, dx_ref, dg_ref, win_v, sem):
        @pl.when(pl.program_id(0) == 0)
        def _():
            cp = pltpu.make_async_copy(win_hbm, win_v, sem)
            cp.start()
            dg_ref[...] = jnp.zeros_like(dg_ref)
            cp.wait()

        g = g_ref[...]
        dh = jnp.zeros((tm, d), F32)
        for j in range(N_SHARD):
            dh = dh + _dot_nt(dpj_ref[:, j * RET_IN_SHARD:(j + 1) * RET_IN_SHARD], win_v[j])
        _, xh, rx = _rms_fwd(x_ref[...], g)
        dg_ref[...] += _colsum(dh * xh)
        dx_ref[...] = dres_ref[...] + _rms_bwd(xh, rx, g, dh)

    row = pl.BlockSpec((tm, d), lambda i: (i, 0))
    return pl.pallas_call(
        body, grid=(nt,),
        in_specs=[pl.BlockSpec((tm, RET_IN), lambda i: (i, 0)), row, row, _full((1, d)), ANY],
        out_specs=[row, _full((1, d))],
        out_shape=[jax.ShapeDtypeStruct((t, d), F32), jax.ShapeDtypeStruct((1, d), F32)],
        scratch_shapes=[pltpu.VMEM(w_in.shape, BF16), pltpu.SemaphoreType.DMA],
        compiler_params=_params(1), name="ret_in_bwd")(dpj, dres, x, g_pre, w_in)


_CHIP_FLIPS = ((1, 0), (0, 1), (1, 1))


def _flip(v, b):
    return 1 - v if b else v


_HBM = pl.BlockSpec(memory_space=pltpu.HBM)
_SEM = pl.BlockSpec(memory_space=pltpu.SEMAPHORE)
_EFFECT = pltpu.SideEffectType.DATAFLOW_SIDE_EFFECTING


def _chip_copies(mode, srcs, lands, send_sems, recv_sems):
    x, y, c = lax.axis_index("x"), lax.axis_index("y"), lax.axis_index("c")
    copies = []
    for t in range(len(lands)):
        if mode == "swap":
            copies.append(pltpu.make_async_remote_copy(
                src_ref=srcs[t], dst_ref=lands[t], send_sem=send_sems.at[t], recv_sem=recv_sems.at[t],
                device_id=(x, y, 1 - c), device_id_type=MESH))
            continue
        if mode == "everyone":
            for m in range(1, 8):
                bx, by, bc = (m >> 2) & 1, (m >> 1) & 1, m & 1
                copies.append(pltpu.make_async_remote_copy(
                    src_ref=srcs[t], dst_ref=lands[t].at[4 * x + 2 * y + c], send_sem=send_sems.at[7 * t + m - 1],
                    recv_sem=recv_sems.at[7 * t + m - 1], device_id=(_flip(x, bx), _flip(y, by), _flip(c, bc)),
                    device_id_type=MESH))
            continue
        for k, (bx, by) in enumerate(_CHIP_FLIPS):
            px, py = _flip(x, bx), _flip(y, by)
            target = (px, py, c)
            if mode == "gather":
                src, dst = srcs[t], lands[t].at[2 * x + y]
            elif mode == "gather_half":
                half = pl.ds(c * (srcs[t].shape[0] // 2), srcs[t].shape[0] // 2)
                src, dst = srcs[t].at[half], lands[t].at[2 * x + y, half]
            elif mode == "forward_half":
                half = pl.ds(c * (lands[t].shape[1] // 2), lands[t].shape[1] // 2)
                src = dst = lands[t].at[2 * px + py, half]
                target = (x, y, 1 - c)
            else:
                src, dst = srcs[t].at[2 * px + py], lands[t].at[k]
            copies.append(pltpu.make_async_remote_copy(
                src_ref=src, dst_ref=dst, send_sem=send_sems.at[3 * t + k], recv_sem=recv_sems.at[3 * t + k],
                device_id=target, device_id_type=MESH))
    return copies


def exchange_start(mode, srcs, lands, name, after=None):
    n, ns = len(lands), len(srcs)
    extra = [] if after is None else [after]

    def body(*refs):
        ins, lnd = refs[:ns], refs[ns:ns + n]
        send_sems, recv_sems = refs[ns + n + len(extra)], refs[ns + n + len(extra) + 1]
        token = refs[-1]
        for cp in _chip_copies(mode, ins, lnd, send_sems, recv_sems):
            cp.start()
        token[...] = jnp.zeros(token.shape, token.dtype)

    hbm = lambda a: pltpu.with_memory_space_constraint(a, pltpu.HBM)
    passed = list(srcs) + list(lands)
    n_sem = {"swap": 1, "everyone": 7}.get(mode, 3) * n
    return pl.pallas_call(
        body, name=name,
        out_shape=(pltpu.SemaphoreType.DMA((n_sem,)), pltpu.SemaphoreType.DMA((n_sem,)),
                   *[pltpu.HBM(a.shape, a.dtype) for a in passed], jax.ShapeDtypeStruct((8, 128), F32)),
        in_specs=[_HBM] * (ns + n) + [ANY] * len(extra),
        out_specs=(_SEM, _SEM, *[_HBM] * (ns + n), pl.BlockSpec(memory_space=pltpu.VMEM)),
        input_output_aliases={i: 2 + i for i in range(ns + n)},
        compiler_params=pltpu.CompilerParams(has_side_effects=_EFFECT))(*[hbm(a) for a in passed], *extra)


def exchange_wait(mode, started, after, name):
    send_sems, recv_sems = started[0], started[1]
    passed = list(started[2:-1])
    n = len(passed) if mode == "forward_half" else len(passed) // 2
    ns = len(passed) - n

    def body(*refs):
        ins, lnd = refs[:ns], refs[ns:ns + n]
        for cp in _chip_copies(mode, ins, lnd, refs[ns + n], refs[ns + n + 1]):
            cp.wait_send()
            cp.wait_recv()

    outs = pl.pallas_call(
        body, name=name, out_shape=tuple(pltpu.HBM(a.shape, a.dtype) for a in passed),
        in_specs=[_HBM] * (ns + n) + [_SEM, _SEM, ANY], out_specs=tuple([_HBM] * (ns + n)),
        input_output_aliases={i: i for i in range(ns + n)},
        compiler_params=pltpu.CompilerParams(has_side_effects=_EFFECT))(*passed, send_sems, recv_sems, after)
    return list(outs[:ns]), list(outs[ns:])


def plane_sum(slot, full, recv, name, bm=256):
    _, m, n = full.shape
    bm = _row_block(m, bm)

    def body(slot_ref, o_ref, r_ref, s_ref):
        s_ref[...] = ((o_ref[...] + r_ref[0].astype(F32)) + r_ref[1].astype(F32)) + r_ref[2].astype(F32)

    return pl.pallas_call(
        body,
        grid_spec=pltpu.PrefetchScalarGridSpec(
            num_scalar_prefetch=1, grid=(m // bm,),
            in_specs=[pl.BlockSpec((None, bm, n), lambda i, s: (s[0], i, 0)),
                      pl.BlockSpec((3, bm, n), lambda i, s: (0, i, 0))],
            out_specs=pl.BlockSpec((bm, n), lambda i, s: (i, 0))),
        out_shape=jax.ShapeDtypeStruct((m, n), F32), compiler_params=_params(1), name=name)(slot, full, recv)


def sum_slots(parts, name, bm=312):
    _, r, n = parts.shape
    bm = bm if r % bm == 0 else r

    def body(p_ref, s_ref):
        acc = p_ref[0]
        for k in range(1, 8):
            acc = acc + p_ref[k]
        s_ref[...] = acc

    return pl.pallas_call(
        body, grid=(r // bm,), in_specs=[pl.BlockSpec((8, bm, n), lambda i: (0, i, 0))],
        out_specs=pl.BlockSpec((bm, n), lambda i: (i, 0)), out_shape=jax.ShapeDtypeStruct((r, n), F32),
        compiler_params=_params(1), name=name)(parts)


def _adamw_math(w, g, m, v):
    m = ADAM_B1 * m + (1.0 - ADAM_B1) * g
    v = ADAM_B2 * v + (1.0 - ADAM_B2) * (g * g)
    m_hat = m / (1.0 - ADAM_B1 ** ADAM_STEP)
    v_hat = v / (1.0 - ADAM_B2 ** ADAM_STEP)
    delta = -ADAM_LR * (m_hat / (jnp.sqrt(v_hat) + ADAM_EPS) + ADAM_WD * w)
    return delta, m, v


def adamw(w, m, v, grads, layer, prev, name, bm=256, row0=0):
    _, _, n = w.shape
    mm = grads[0].shape[0]
    bm = _row_block(mm, bm)
    first = row0 // bm
    ng = len(grads)

    def body(*refs):
        w_ref, m_ref, v_ref = refs[:3]
        g_refs = refs[3:3 + ng]
        g_out, d_out, m_out, v_out = refs[-4:]
        g = g_refs[0][...]
        for gr in g_refs[1:]:
            g = g + gr[...]
        delta, mn, vn = _adamw_math(w_ref[...], g, m_ref[...], v_ref[...])
        g_out[...] = g
        d_out[...] = delta
        m_out[...] = mn
        v_out[...] = vn

    slab = pl.BlockSpec((None, bm, n), lambda i: (layer, first + i, 0))
    flat = pl.BlockSpec((bm, n), lambda i: (i, 0))
    in_specs = [slab] * 3 + [flat] * ng
    args = [w, m, v, *grads]
    aliases = {}
    if prev is not None:
        in_specs += [ANY] * 4
        aliases = {3 + ng + q: q for q in range(4)}
        args += list(prev)
    return pl.pallas_call(
        body, grid=(mm // bm,), in_specs=in_specs, out_specs=[slab] * 4,
        out_shape=[jax.ShapeDtypeStruct(w.shape, F32)] * 4, input_output_aliases=aliases,
        compiler_params=_params(1), name=name)(*args)


def _pack_rows(parts, rows):
    flat = jnp.concatenate([p.reshape(-1) for p in parts])
    return jnp.pad(flat, (0, rows * 128 - flat.shape[0])).reshape(rows, 128)


def _as_shards(a, rows):
    return a.reshape(N_SHARD, rows, a.shape[-1])


def _local_step(x, pos_col, target, gains, pool_w, pool_scale, gn_gain, conv_w, conv_b, weights, send_grads):
    def gain(l, n, token=None):
        g = gains[l, n].reshape(1, D_MODEL)
        return g if token is None else g + token[0:1, 0:1]

    inv_freq = (ROPE_BASE ** (-jnp.arange(0, RET_QK, 2, dtype=F32) / RET_QK)).reshape(1, RET_QK // 2)
    intra, cross, inner, chunk_decay = _decay_tables()
    dn_rows = D_FF // N_SHARD

    x1 = pool_fwd(x, gain(0, 0), gain(0, 1), pool_w, pool_scale)
    cos, sin = rope_tables(pos_col, inv_freq, x1)
    w_up0, w_dn0 = weights("mlp0", cos)
    w_dn0 = w_dn0.reshape(D_FF, D_MODEL)
    x2, u0, s0, a0, f0, h0 = mlp_fwd(x1, gain(0, 2), gain(0, 3), w_up0, w_dn0, conv_w[0], conv_b[0])
    w_in, w_out = weights("ret", x2)
    w_out = w_out.reshape(RET_HEADS * RET_V, D_MODEL)
    pj, hr = ret_proj(x2, gain(1, 0), w_in, cos, sin)
    o, sprev = ret_core_fwd(pj, intra, cross, inner, chunk_decay)
    x3, yb, out = ret_out_fwd(o, pj, x2, gn_gain, gain(1, 1), w_out)
    w_up1, w_dn1 = weights("mlp1", x3)
    w_dn1 = w_dn1.reshape(D_FF, D_MODEL)
    dx4, u1, s1, a1, f1, h1, loss = mlp_fwd(x3, gain(1, 2), gain(1, 3), w_up1, w_dn1, conv_w[1], conv_b[1], target)

    dx3, du1, df1, dg12, dg13, dcw1, dcb1 = mlp_bwd(
        dx4, f1, x3, u1, s1, gain(1, 2), gain(1, 3), w_up1, w_dn1, conv_w[1])
    dwup1 = grad_matmul(h1, du1, D_MODEL, FF_CHUNK, "grad_w_up_1", interleaved=True)
    dwdn1 = grad_matmul(a1, df1, FF_CHUNK, D_MODEL, "grad_w_down_1")
    tok = send_grads("mlp1", [dwup1, [_as_shards(g, dn_rows) for g in dwdn1]])
    dout, dpj, do, dg11, dgn = ret_out_bwd(dx3, out, o, pj, gn_gain, gain(1, 1, tok), w_out)
    dwout = grad_matmul(yb, dout, 1024, D_MODEL, "grad_w_out")
    dpj = ret_core_bwd(pj, do, sprev, cos, sin, dpj, intra, cross, inner, chunk_decay)
    dwin = grad_matmul(hr, dpj, D_MODEL, RET_IN_SHARD, "grad_w_in")
    tok = send_grads("ret", [dwin, [_as_shards(g, RET_V) for g in dwout]])
    dx2, dg10 = ret_in_bwd(dpj, dx3, x2, gain(1, 0, tok), w_in)
    dx1, du0, df0, dg02, dg03, dcw0, dcb0 = mlp_bwd(
        dx2, f0, x1, u0, s0, gain(0, 2), gain(0, 3), w_up0, w_dn0, conv_w[0])
    dwdn0 = grad_matmul(a0, df0, FF_CHUNK, D_MODEL, "grad_w_down_0")
    tok = send_grads("down0", [[_as_shards(g, dn_rows) for g in dwdn0]])
    half = D_MODEL // 2
    for part, first in (("a", 0), ("b", half)):
        dwup0 = grad_matmul(h0, du0, half, FF_CHUNK, "grad_w_up_0" + part, tk=4096, interleaved=True, after=tok,
                            cols=(first, half))
        tok = send_grads("up0" + part, [dwup0])
    dx0, dg00, dg01, dpscale, dpw = pool_bwd(dx1, x, gain(0, 0, tok), gain(0, 1), pool_w, pool_scale)

    rows = lambda g: jnp.sum(g, axis=0, keepdims=True)
    dgains = jnp.concatenate([dg00, dg01, rows(dg02), rows(dg03), dg10, dg11, rows(dg12), rows(dg13)],
                             axis=0).reshape(2, 4, D_MODEL)
    small = {"gains": dgains, "pool_scale": dpscale, "gn": dgn,
             "conv_w": jnp.sum(jnp.stack([dcw0, dcw1]), axis=3),
             "conv_b": jnp.sum(jnp.stack([dcb0, dcb1]), axis=2, keepdims=True), "pool_w": dpw}
    return loss, dx0, small


def kernel(x, positions, norm_gain, pool_w, pool_scale, ret_w_in, ret_gn_gain, ret_w_out, mlp_w_up, mlp_conv_w, mlp_conv_b, mlp_w_down, loss_target, m_norm_gain, m_pool_w, m_pool_scale, m_ret_w_in, m_ret_gn_gain, m_ret_w_out, m_mlp_w_up, m_mlp_conv_w, m_mlp_conv_b, m_mlp_w_down, v_norm_gain, v_pool_w, v_pool_scale, v_ret_w_in, v_ret_gn_gain, v_ret_w_out, v_mlp_w_up, v_mlp_conv_w, v_mlp_conv_b, v_mlp_w_down):
    t = x.shape[1]
    me = 2 * lax.axis_index("x") + lax.axis_index("y")
    me_slot = jnp.reshape(me, (1,)).astype(jnp.int32)

    small_parts = [norm_gain, ret_gn_gain, mlp_conv_w, pool_w]
    small_sizes = [p.size for p in small_parts]
    small_rows = -(-sum(small_sizes) // (128 * 8)) * 8
    gathers = {}

    def start_gather(group, srcs, after):
        lands = [lax.dynamic_update_index_in_dim(lax.empty((N_SHARD,) + s.shape, s.dtype), s, me, 0) for s in srcs]
        mode = "gather_half" if group == "mlp0" else "gather"
        gathers[group] = (mode, exchange_start(mode, srcs, lands, "gather_start_" + group, after=after))
        return gathers[group][1][-1]

    token = start_gather("small", [_pack_rows(small_parts, small_rows)], None)
    token = start_gather("mlp0", [mlp_w_up[0].astype(BF16), mlp_w_down[0].astype(BF16)], token)

    def weights(group, after):
        if group == "mlp0":
            tok = start_gather("ret", [ret_w_in[0].astype(BF16), ret_w_out[0].astype(BF16)], after)
            after = start_gather("mlp1", [mlp_w_up[1].astype(BF16), mlp_w_down[1].astype(BF16)], tok)
        mode, started = gathers[group]
        _, lands = exchange_wait(mode, started, after, "gather_wait_" + group)
        if mode == "gather_half":
            forward = exchange_start("forward_half", [], lands, "forward_start_" + group)
            _, lands = exchange_wait("forward_half", forward, forward[-1], "forward_wait_" + group)
        return lands

    sent, early = {}, {}

    def reduced(group, after, names):
        started, own = sent[group]
        _, recv = exchange_wait("scatter", started, after, "scatter_wait_" + group)
        return [plane_sum(me_slot, f, r, "plane_sum_" + nm)
                for f, r, nm in zip(own, recv, names)]

    def swap_start(planes, name):
        return exchange_start("swap", planes, [lax.empty(p.shape, p.dtype) for p in planes], name)

    def send_grads(group, pairs):
        lands = [lax.empty((3,) + b.shape[1:], BF16) for _, b in pairs]
        sent[group] = (exchange_start("scatter", [b for _, b in pairs], lands, "scatter_start_" + group),
                       [f for f, _ in pairs])
        token = sent[group][0][-1]
        if group == "down0":
            marker = pairs[0][1]
            early["planes"] = (reduced("mlp1", marker, ["w_up_1", "w_down_1"])
                               + reduced("ret", marker, ["w_in", "w_out"]))
            early["swap"] = swap_start(early["planes"], "swap_start_a")
            token = token + early["swap"][-1]
        return token

    (smallg,) = weights("small", token)
    smallg = smallg.reshape(N_SHARD, -1)
    offs = np.cumsum([0] + small_sizes)
    piece = lambda i, shape: smallg[:, offs[i]:offs[i + 1]].reshape((N_SHARD,) + shape)
    gains = piece(0, (2, 4, 256)).transpose(1, 2, 0, 3).reshape(2, 4, D_MODEL)
    gn_full = piece(1, (512,)).reshape(1, RET_HEADS * RET_V)
    cw_full = piece(2, (2, 3, FF_CHUNK)).transpose(1, 0, 2, 3)
    pw_full = piece(3, (4, 64, 256)).transpose(1, 0, 2, 3).reshape(4, 256, 256).astype(BF16)
    cb_full = mlp_conv_b.reshape(2, N_SHARD, 1, FF_CHUNK)

    loss, dx0, small = _local_step(
        x[0], positions.reshape(t, 1).astype(F32), loss_target[0], gains, pw_full, pool_scale, gn_full,
        cw_full, cb_full, weights, send_grads)

    def small_adamw(w, m, v, grads, name):
        w3 = w.reshape(1, -1, w.shape[-1])
        out = adamw(w3, m.reshape(w3.shape), v.reshape(w3.shape), [g.reshape(w3.shape[1:]) for g in grads], 0, None, name)
        return [o.reshape(w.shape) for o in out]

    pw_f = small["pool_w"].reshape(4, N_SHARD, 64, 256).transpose(1, 0, 2, 3).reshape(N_SHARD, 256, 256)
    small = dict(small, loss=loss)
    small_order = ["gains", "pool_scale", "gn", "conv_w", "conv_b", "loss"]
    gsmall_sizes = [small[k].size for k in small_order]
    gsmall_rows = -(-sum(gsmall_sizes) // (128 * 8)) * 8
    gpack = _pack_rows([small[k] for k in small_order], gsmall_rows)
    mine = 2 * me + lax.axis_index("c")
    small_started = exchange_start(
        "everyone", [gpack], [lax.dynamic_update_index_in_dim(lax.empty((8,) + gpack.shape, F32), gpack, mine, 0)],
        "small_start")
    send_grads("pool_w", [(pw_f, pw_f.astype(BF16))])

    res = {}
    planes_a, others_a = exchange_wait("swap", early["swap"], small_started[-1], "swap_wait_a")
    res["ret_w_in"] = adamw(ret_w_in, m_ret_w_in, v_ret_w_in, (planes_a[2], others_a[2]), 0, None, "adamw_w_in")
    res["ret_w_out"] = adamw(ret_w_out, m_ret_w_out, v_ret_w_out, (planes_a[3], others_a[3]), 0, None, "adamw_w_out")
    up1 = adamw(mlp_w_up, m_mlp_w_up, v_mlp_w_up, (planes_a[0], others_a[0]), 1, None, "adamw_w_up_1")
    dn1 = adamw(mlp_w_down, m_mlp_w_down, v_mlp_w_down, (planes_a[1], others_a[1]), 1, None, "adamw_w_down_1")

    planes_b = (reduced("up0a", dn1[0], ["w_up_0a"]) + reduced("up0b", dn1[0], ["w_up_0b"])
                + reduced("down0", dn1[0], ["w_down_0"]) + reduced("pool_w", dn1[0], ["pool_w"]))
    swap_b = swap_start(planes_b, "swap_start_b")

    _, (small_recv,) = exchange_wait("everyone", small_started, swap_b[-1], "small_wait")
    gsmall = sum_slots(small_recv, "sum_small").reshape(-1)
    goffs = np.cumsum([0] + gsmall_sizes)
    gpiece = lambda i: gsmall[goffs[i]:goffs[i + 1]].reshape(small[small_order[i]].shape)
    g_gains = lax.dynamic_slice_in_dim(gpiece(0), me * 256, 256, axis=2)
    g_gn = lax.dynamic_slice_in_dim(gpiece(2), me * RET_V, RET_V, axis=1)
    g_cw = lax.dynamic_index_in_dim(gpiece(3), me, 1, keepdims=False)
    res["norm_gain"] = small_adamw(norm_gain, m_norm_gain, v_norm_gain, [g_gains], "adamw_norm_gain")
    res["pool_scale"] = small_adamw(pool_scale, m_pool_scale, v_pool_scale, [gpiece(1)], "adamw_pool_scale")
    res["ret_gn_gain"] = small_adamw(ret_gn_gain, m_ret_gn_gain, v_ret_gn_gain, [g_gn], "adamw_gn_gain")
    res["mlp_conv_w"] = small_adamw(mlp_conv_w, m_mlp_conv_w, v_mlp_conv_w, [g_cw], "adamw_conv_w")
    res["mlp_conv_b"] = small_adamw(mlp_conv_b, m_mlp_conv_b, v_mlp_conv_b, [gpiece(4)], "adamw_conv_b")

    planes_b, others_b = exchange_wait("swap", swap_b, res["mlp_conv_b"][0], "swap_wait_b")
    up0a = adamw(mlp_w_up, m_mlp_w_up, v_mlp_w_up, (planes_b[0], others_b[0]), 0, up1, "adamw_w_up_0a")
    res["mlp_w_up"] = adamw(mlp_w_up, m_mlp_w_up, v_mlp_w_up, (planes_b[1], others_b[1]), 0, up0a, "adamw_w_up_0b",
                            row0=D_MODEL // 2)
    res["mlp_w_down"] = adamw(mlp_w_down, m_mlp_w_down, v_mlp_w_down, (planes_b[2], others_b[2]), 0, dn1,
                              "adamw_w_down_0")
    res["pool_w"] = small_adamw(pool_w, m_pool_w, v_pool_w, (planes_b[3], others_b[3]), "adamw_pool_w")

    order = ["norm_gain", "pool_w", "pool_scale", "ret_w_in", "ret_gn_gain", "ret_w_out", "mlp_w_up", "mlp_conv_w",
             "mlp_conv_b", "mlp_w_down"]
    outs = [gpiece(5)[0, 0], dx0.reshape(x.shape)]
    for q in range(4):
        outs += [res[k][q] for k in order]
    return tuple(outs)
```

```python
import numpy as np
import jax
import jax.numpy as jnp
from jax import lax
from jax.experimental import pallas as pl
from jax.experimental.pallas import tpu as pltpu

F32 = jnp.float32
BF16 = jnp.bfloat16

D_MODEL = 1024
D_FF = 2816
FF_CHUNK = 1408
N_SHARD = 4
POOL_WINDOWS = (2, 4, 8, 16)
POOL_DIM = 256
POOL_HALO = 16
RET_HEADS = 4
RET_QK = 256
RET_V = 512
RET_CHUNK = 256
RET_STEP_CHUNKS = 4
RET_IN = 6144
RET_IN_SHARD = 1536
ROPE_BASE = 10000.0
EPS = 1e-6
CONV_HALO = 8

ADAM_LR, ADAM_B1, ADAM_B2, ADAM_EPS, ADAM_WD, ADAM_STEP = 0.001, 0.9, 0.999, 1e-08, 0.01, 10

VMEM_LIMIT = 56 * 1024 * 1024
VMEM_LIMIT_MLP_BWD = 62 * 1024 * 1024
MESH = pl.DeviceIdType.MESH
ANY = pl.BlockSpec(memory_space=pl.ANY)


def _params(n_grid=1, limit=VMEM_LIMIT):
    return pltpu.CompilerParams(dimension_semantics=("arbitrary",) * n_grid, vmem_limit_bytes=limit)


def _dot(a, b):
    return jnp.dot(a, b, preferred_element_type=F32)


def _dot_nt(a, b):
    return lax.dot_general(a, b, (((1,), (1,)), ((), ())), preferred_element_type=F32)


def _dot_tn(a, b):
    return lax.dot_general(a, b, (((0,), (0,)), ((), ())), preferred_element_type=F32)


def _rms_fwd(x, gain):
    r = lax.rsqrt(jnp.mean(x * x, axis=-1, keepdims=True) + EPS)
    xh = x * r
    return xh * gain, xh, r


def _rms_bwd(xh, r, gain, dy):
    dxh = dy * gain
    return r * (dxh - xh * jnp.mean(dxh * xh, axis=-1, keepdims=True))


def _colsum(v):
    return jnp.sum(v, axis=0, keepdims=True)


def _full(shape):
    nd = len(shape)
    return pl.BlockSpec(shape, lambda *_: (0,) * nd)


def rope_tables(pos_col, inv_freq, after):
    t = pos_col.shape[0]
    tm = min(t, 1024)

    def body(p_ref, f_ref, after_ref, c_ref, s_ref):
        ang = p_ref[...] * f_ref[...]
        c_ref[...] = jnp.cos(ang)
        s_ref[...] = jnp.sin(ang)

    return pl.pallas_call(
        body, grid=(t // tm,),
        in_specs=[pl.BlockSpec((tm, 1), lambda i: (i, 0)), _full((1, 128)), ANY],
        out_specs=[pl.BlockSpec((tm, 128), lambda i: (i, 0))] * 2,
        out_shape=[jax.ShapeDtypeStruct((t, 128), F32)] * 2,
        compiler_params=_params(1), name="rope_tables")(pos_col, inv_freq, after)


def _window_sums(ext, backward):
    n = ext.shape[0]
    cur, sums = ext, []
    for g, win in enumerate(POOL_WINDOWS):
        if g > 0:
            cur = cur[:, POOL_DIM:]
        half = win // 2
        cur = cur + pltpu.roll(cur, n - half if backward else half, axis=0)
        sums.append(cur[:, 0:POOL_DIM])
    return sums


def _pool_diff(h_halo, h, row0, tm):
    t_idx = row0 + lax.broadcasted_iota(jnp.int32, (tm, 1), 0)
    sums = _window_sums(jnp.concatenate([h_halo, h], axis=0), backward=False)
    parts, inv_counts = [], []
    for g, win in enumerate(POOL_WINDOWS):
        inv = 1.0 / jnp.minimum(t_idx + 1, win).astype(F32)
        parts.append(sums[g][POOL_HALO:, :] * inv - h[:, g * POOL_DIM:(g + 1) * POOL_DIM])
        inv_counts.append(inv)
    return parts, inv_counts


def pool_fwd(x, g_pre, g_post, pool_w, pool_scale, tm=1024):
    t, d = x.shape
    tm = min(tm, t)
    nt = t // tm

    def body(x_ref, g0_ref, g1_ref, w_ref, sc_ref, o_ref, hext):
        i = pl.program_id(0)

        @pl.when(i == 0)
        def _():
            hext[...] = jnp.zeros((POOL_HALO, d), F32)

        xv = x_ref[...]
        h, _, _ = _rms_fwd(xv, g0_ref[...])
        parts, _ = _pool_diff(hext[...], h, i * tm, tm)
        hext[...] = h[tm - POOL_HALO:tm, :]
        ys = [_dot(parts[g].astype(BF16), w_ref[g]) for g in range(len(POOL_WINDOWS))]
        y = jnp.concatenate(ys, axis=-1) * sc_ref[...]
        m, _, _ = _rms_fwd(y, g1_ref[...])
        o_ref[...] = xv + m

    row = pl.BlockSpec((tm, d), lambda i: (i, 0))
    return pl.pallas_call(
        body, grid=(nt,),
        in_specs=[row, _full((1, d)), _full((1, d)), _full(pool_w.shape), _full((1, d))],
        out_specs=row, out_shape=jax.ShapeDtypeStruct((t, d), F32),
        scratch_shapes=[pltpu.VMEM((POOL_HALO, d), F32)],
        compiler_params=_params(1), name="pool_fwd")(x, g_pre, g_post, pool_w, pool_scale)


def pool_bwd(dx1, x, g_pre, g_post, pool_w, pool_scale, tm=512):
    t, d = x.shape
    nt = t // tm
    ng = len(POOL_WINDOWS)

    def body(dx1_ref, x_ref, xh_ref, g0_ref, g1_ref, w_ref, sc_ref,
             dx_ref, dg0_ref, dg1_ref, dsc_ref, dw_ref, enext):
        i = pl.program_id(0)
        r = nt - 1 - i

        @pl.when(i == 0)
        def _():
            enext[...] = jnp.zeros((POOL_HALO, d), F32)
            dg0_ref[...] = jnp.zeros_like(dg0_ref)
            dg1_ref[...] = jnp.zeros_like(dg1_ref)
            dsc_ref[...] = jnp.zeros_like(dsc_ref)
            dw_ref[...] = jnp.zeros_like(dw_ref)

        g0 = g0_ref[...]
        g1 = g1_ref[...]
        sc = sc_ref[...]
        xv = x_ref[...]
        h, xh, rx = _rms_fwd(xv, g0)
        h_halo, _, _ = _rms_fwd(xh_ref[...], g0)
        parts, inv_counts = _pool_diff(h_halo * jnp.where(r > 0, 1.0, 0.0), h, r * tm, tm)
        parts_b = [p.astype(BF16) for p in parts]
        ypre = jnp.concatenate([_dot(parts_b[g], w_ref[g]) for g in range(ng)], axis=-1)
        _, yh, ry = _rms_fwd(ypre * sc, g1)
        dm = dx1_ref[...]
        dg1_ref[...] += _colsum(dm * yh)
        dy = _rms_bwd(yh, ry, g1, dm)
        dsc_ref[...] += _colsum(dy * ypre)
        dyp = (dy * sc).astype(BF16)
        ddiffs = []
        for g in range(ng):
            cols = slice(g * POOL_DIM, (g + 1) * POOL_DIM)
            dw_ref[g] += _dot_tn(parts_b[g], dyp[:, cols])
            ddiffs.append(_dot_nt(dyp[:, cols], w_ref[g]))
        e = jnp.concatenate([ddiffs[g] * inv_counts[g] for g in range(ng)], axis=-1)
        sums = _window_sums(jnp.concatenate([e, enext[...]], axis=0), backward=True)
        enext[...] = e[0:POOL_HALO, :]
        dh = jnp.concatenate([sums[g][0:tm, :] - ddiffs[g] for g in range(ng)], axis=-1)
        dg0_ref[...] += _colsum(dh * xh)
        dx_ref[...] = dm + _rms_bwd(xh, rx, g0, dh)

    row = pl.BlockSpec((tm, d), lambda i: (nt - 1 - i, 0))
    halo = pl.BlockSpec((POOL_HALO, d), lambda i: (jnp.maximum((nt - 1 - i) * (tm // POOL_HALO) - 1, 0), 0))
    vec = _full((1, d))
    return pl.pallas_call(
        body, grid=(nt,),
        in_specs=[row, row, halo, vec, vec, _full(pool_w.shape), vec],
        out_specs=[row, vec, vec, vec, _full((ng, POOL_DIM, POOL_DIM))],
        out_shape=[jax.ShapeDtypeStruct((t, d), F32)] + [jax.ShapeDtypeStruct((1, d), F32)] * 3
        + [jax.ShapeDtypeStruct((ng, POOL_DIM, POOL_DIM), F32)],
        scratch_shapes=[pltpu.VMEM((POOL_HALO, d), F32)],
        compiler_params=_params(1), name="pool_bwd")(dx1, x, x, g_pre, g_post, pool_w, pool_scale)


def _conv_taps(cw_ref, j):
    return cw_ref[j, 0:1, :], cw_ref[j, 1:2, :], cw_ref[j, 2:3, :]


def _row_block(m, target=256):
    if m <= target:
        return m
    for b in range(target, 7, -8):
        if m % b == 0:
            return b
    return m


def mlp_fwd(x, g_pre, g_post, w_up, w_down, conv_w, conv_b, target=None, tm=256):
    t, d = x.shape
    nt = t // tm
    h8 = CONV_HALO
    with_loss = target is not None
    n_extra = 1 if with_loss else 0

    def body(x_ref, g2_ref, g3_ref, wup_hbm, wdn_hbm, cw_ref, cb_ref, *rest):
        tgt_ref = rest[0] if with_loss else None
        xo_ref, u_ref, s_ref, a_ref, f_ref, h_ref = rest[n_extra:n_extra + 6]
        loss_ref = rest[n_extra + 6] if with_loss else None
        wup_v, wdn_v, tail, sem = rest[-4:]
        i = pl.program_id(0)

        @pl.when(i == 0)
        def _():
            c1 = pltpu.make_async_copy(wup_hbm, wup_v, sem.at[0])
            c2 = pltpu.make_async_copy(wdn_hbm, wdn_v, sem.at[1])
            c1.start()
            c2.start()
            tail[...] = jnp.zeros_like(tail)
            if with_loss:
                loss_ref[...] = jnp.zeros_like(loss_ref)
            c1.wait()
            c2.wait()

        xv = x_ref[...]
        h, _, _ = _rms_fwd(xv, g2_ref[...])
        hb = h.astype(BF16)
        h_ref[...] = hb
        acc = jnp.zeros((tm, d), F32)
        for k in range(2):
            cs = []
            for s in range(2):
                j, cols = k + 2 * s, slice((2 * k + s) * FF_CHUNK, (2 * k + s + 1) * FF_CHUNK)
                uf = _dot(hb, wup_v[j])
                u_ref[:, cols] = uf.astype(BF16)
                ext = jnp.concatenate([tail[j], uf], axis=0)
                tail[j] = uf[tm - h8:tm, :]
                w0, w1, w2 = _conv_taps(cw_ref, j)
                cs.append(cb_ref[j] + w2 * uf + w1 * pltpu.roll(ext, 1, axis=0)[h8:, :]
                          + w0 * pltpu.roll(ext, 2, axis=0)[h8:, :])
            cg, cv = cs
            sg = jax.nn.sigmoid(cg)
            sil = cg * sg
            ab = (sil * cv).astype(BF16)
            a_ref[:, k * FF_CHUNK:(k + 1) * FF_CHUNK] = ab
            s_ref[:, 2 * k * FF_CHUNK:(2 * k + 1) * FF_CHUNK] = sil.astype(BF16)
            s_ref[:, (2 * k + 1) * FF_CHUNK:(2 * k + 2) * FF_CHUNK] = (cv * (sg + sil * (1.0 - sg))).astype(BF16)
            acc = acc + _dot(ab, wdn_v[k * FF_CHUNK:(k + 1) * FF_CHUNK, :])
        f_ref[...] = acc
        y, _, _ = _rms_fwd(acc, g3_ref[...])
        if with_loss:
            err = (xv + y) - tgt_ref[...]
            xo_ref[...] = err * (1.0 / d)
            loss_ref[...] += 0.5 * jnp.sum(jnp.mean(err * err, axis=-1, keepdims=True), axis=0, keepdims=True)
        else:
            xo_ref[...] = xv + y

    row = pl.BlockSpec((tm, d), lambda i: (i, 0))
    wide = pl.BlockSpec((tm, 2 * D_FF), lambda i: (i, 0))
    vec = _full((1, d))
    extra = [target] if with_loss else []
    return pl.pallas_call(
        body, grid=(nt,),
        in_specs=[row, vec, vec, ANY, ANY, _full(conv_w.shape), _full(conv_b.shape)] + [row] * n_extra,
        out_specs=[row, wide, wide, pl.BlockSpec((tm, D_FF), lambda i: (i, 0)), row, row] + [_full((1, 1))] * n_extra,
        out_shape=[jax.ShapeDtypeStruct((t, d), F32), jax.ShapeDtypeStruct((t, 2 * D_FF), BF16),
                   jax.ShapeDtypeStruct((t, 2 * D_FF), BF16), jax.ShapeDtypeStruct((t, D_FF), BF16),
                   jax.ShapeDtypeStruct((t, d), F32), jax.ShapeDtypeStruct((t, d), BF16)]
        + [jax.ShapeDtypeStruct((1, 1), F32)] * n_extra,
        scratch_shapes=[pltpu.VMEM(w_up.shape, BF16), pltpu.VMEM(w_down.shape, BF16),
                        pltpu.VMEM((N_SHARD, h8, FF_CHUNK), F32), pltpu.SemaphoreType.DMA((2,))],
        compiler_params=_params(1), name="mlp_fwd_loss" if with_loss else "mlp_fwd")(
            x, g_pre, g_post, w_up, w_down, conv_w, conv_b, *extra)


def _rowsum8(v):
    return jnp.sum(v.reshape(v.shape[0] // 8, 8, v.shape[1]), axis=0)


def mlp_bwd(dxo, f, x, u, sp, g_pre, g_post, w_up, w_down, conv_w, tm=256):
    t, d = x.shape
    nt = t // tm
    h8 = CONV_HALO

    def body(dxo_ref, f_ref, x_ref, u_ref, s_ref, g2_ref, g3_ref, wup_hbm, wdn_hbm, cw_ref,
             dx_ref, du_ref, df_ref, dg2_ref, dg3_ref, dcw_ref, dcb_ref,
             wup_v, wdn_v, carry, sem):
        @pl.when(pl.program_id(0) == 0)
        def _():
            c1 = pltpu.make_async_copy(wup_hbm, wup_v, sem.at[0])
            c2 = pltpu.make_async_copy(wdn_hbm, wdn_v, sem.at[1])
            c1.start()
            c2.start()
            carry[...] = jnp.zeros_like(carry)
            dg2_ref[...] = jnp.zeros_like(dg2_ref)
            dg3_ref[...] = jnp.zeros_like(dg3_ref)
            dcw_ref[...] = jnp.zeros_like(dcw_ref)
            dcb_ref[...] = jnp.zeros_like(dcb_ref)
            c1.wait()
            c2.wait()

        g3 = g3_ref[...]
        dxo = dxo_ref[...]
        _, fh, rf = _rms_fwd(f_ref[...], g3)
        dg3_ref[...] += _rowsum8(dxo * fh)
        dfb = _rms_bwd(fh, rf, g3, dxo).astype(BF16)
        df_ref[...] = dfb
        dh = jnp.zeros((tm, d), F32)
        for k in range(2):
            da = _dot_nt(dfb, wdn_v[k * FF_CHUNK:(k + 1) * FF_CHUNK, :])
            for s in range(2):
                j = k + 2 * s
                cols = slice((2 * k + s) * FF_CHUNK, (2 * k + s + 1) * FF_CHUNK)
                dc = da * s_ref[:, (2 * k + 1 - s) * FF_CHUNK:(2 * k + 2 - s) * FF_CHUNK].astype(F32)
                uf = u_ref[:, cols].astype(F32)
                ext = jnp.concatenate([dc, carry[j]], axis=0)
                carry[j] = dc[0:h8, :]
                dc1 = pltpu.roll(ext, tm + h8 - 1, axis=0)[0:tm, :]
                dc2 = pltpu.roll(ext, tm + h8 - 2, axis=0)[0:tm, :]
                dcb_ref[j] += _rowsum8(dc)
                dcw_ref[j, 2] += _rowsum8(dc * uf)
                dcw_ref[j, 1] += _rowsum8(dc1 * uf)
                dcw_ref[j, 0] += _rowsum8(dc2 * uf)
                dub = (cw_ref[j, 2:3, :] * dc + cw_ref[j, 1:2, :] * dc1 + cw_ref[j, 0:1, :] * dc2).astype(BF16)
                du_ref[:, cols] = dub
                dh = dh + _dot_nt(dub, wup_v[j])
        g2 = g2_ref[...]
        _, xh, rx = _rms_fwd(x_ref[...], g2)
        dg2_ref[...] += _rowsum8(dh * xh)
        dx_ref[...] = dxo + _rms_bwd(xh, rx, g2, dh)

    row = pl.BlockSpec((tm, d), lambda i: (nt - 1 - i, 0))
    wide = pl.BlockSpec((tm, 2 * D_FF), lambda i: (nt - 1 - i, 0))
    vec = _full((1, d))
    acc = _full((8, d))
    dcw_shape, dcb_shape = (N_SHARD, 3, 8, FF_CHUNK), (N_SHARD, 8, FF_CHUNK)
    return pl.pallas_call(
        body, grid=(nt,),
        in_specs=[row, row, row, wide, wide, vec, vec, ANY, ANY, _full(conv_w.shape)],
        out_specs=[row, wide, row, acc, acc, _full(dcw_shape), _full(dcb_shape)],
        out_shape=[jax.ShapeDtypeStruct((t, d), F32), jax.ShapeDtypeStruct((t, 2 * D_FF), BF16),
                   jax.ShapeDtypeStruct((t, d), BF16),
                   jax.ShapeDtypeStruct((8, d), F32), jax.ShapeDtypeStruct((8, d), F32),
                   jax.ShapeDtypeStruct(dcw_shape, F32), jax.ShapeDtypeStruct(dcb_shape, F32)],
        scratch_shapes=[pltpu.VMEM(w_up.shape, BF16), pltpu.VMEM(w_down.shape, BF16),
                        pltpu.VMEM((N_SHARD, h8, FF_CHUNK), F32), pltpu.SemaphoreType.DMA((2,))],
        compiler_params=_params(1, VMEM_LIMIT_MLP_BWD), name="mlp_bwd")(
            dxo, f, x, u, sp, g_pre, g_post, w_up, w_down, conv_w)


def grad_matmul(a, b, bm, bn, name, tk=2048, interleaved=False, after=None, cols=None):
    t = a.shape[0]
    m0, m = (0, a.shape[1]) if cols is None else cols
    n = b.shape[1]
    tk = min(tk, t)
    nk = t // tk
    place = (lambda j: (j % 2) * 2 + j // 2) if interleaved else (lambda j: j)
    extra = [] if after is None else [after]
    first = m0 // bm

    def body(a_ref, b_ref, *rest):
        o_ref, ob_ref = rest[len(extra):]
        kk = pl.program_id(2)

        @pl.when(kk == 0)
        def _():
            o_ref[...] = jnp.zeros_like(o_ref)

        o_ref[...] += _dot_tn(a_ref[...], b_ref[...])

        @pl.when(kk == nk - 1)
        def _():
            ob_ref[...] = o_ref[...].astype(BF16)

    ospec = pl.BlockSpec((None, bm, bn), lambda j, i, kk: (place(j), i, 0))
    return pl.pallas_call(
        body, grid=(n // bn, m // bm, nk),
        in_specs=[pl.BlockSpec((tk, bm), lambda j, i, kk: (kk, first + i)),
                  pl.BlockSpec((tk, bn), lambda j, i, kk: (kk, j))]
        + [ANY] * len(extra),
        out_specs=[ospec, ospec],
        out_shape=[jax.ShapeDtypeStruct((n // bn, m, bn), F32), jax.ShapeDtypeStruct((n // bn, m, bn), BF16)],
        compiler_params=_params(3), name=name)(a, b, *extra)


def _decay_tables():
    log_gamma = jnp.log(1.0 - 2.0 ** (-5.0 - jnp.arange(RET_HEADS, dtype=F32)))
    i = jnp.arange(RET_CHUNK, dtype=F32)
    rel = i[:, None] - i[None, :]
    intra = jnp.where(rel >= 0, jnp.exp(jnp.maximum(rel, 0.0) * log_gamma[:, None, None]), 0.0)
    cross = jnp.exp((i + 1.0) * log_gamma[:, None])[:, :, None]
    inner = jnp.exp((RET_CHUNK - 1.0 - i) * log_gamma[:, None])[:, :, None]
    chunk = [float(np.exp(np.float32(RET_CHUNK) * np.log(np.float32(1.0 - 2.0 ** (-5.0 - h))).astype(np.float32)))
             for h in range(RET_HEADS)]
    return intra, cross, inner, chunk


def ret_proj(x, g_pre, w_in, cos, sin, tm=512):
    t, d = x.shape
    nt = t // tm
    per = RET_IN_SHARD // RET_QK

    def body(x_ref, g_ref, win_hbm, c_ref, s_ref, pj_ref, h_ref, win_v, sem):
        @pl.when(pl.program_id(0) == 0)
        def _():
            cp = pltpu.make_async_copy(win_hbm, win_v, sem)
            cp.start()
            cp.wait()

        h, _, _ = _rms_fwd(x_ref[...], g_ref[...])
        hb = h.astype(BF16)
        h_ref[...] = hb
        c = c_ref[...]
        s = s_ref[...]
        for j in range(N_SHARD):
            pjj = _dot(hb, win_v[j])
            for bb in range(per):
                b = per * j + bb
                blk = pjj[:, bb * RET_QK:(bb + 1) * RET_QK]
                if b < 2 * RET_HEADS:
                    x1, x2 = blk[:, :128], blk[:, 128:]
                    o1 = x1 * c - x2 * s
                    o2 = x2 * c + x1 * s
                    if b < RET_HEADS:
                        o1 = o1 * (RET_QK ** -0.5)
                        o2 = o2 * (RET_QK ** -0.5)
                    pj_ref[:, b * RET_QK:b * RET_QK + 128] = o1.astype(BF16)
                    pj_ref[:, b * RET_QK + 128:(b + 1) * RET_QK] = o2.astype(BF16)
                else:
                    pj_ref[:, b * RET_QK:(b + 1) * RET_QK] = blk.astype(BF16)

    row = pl.BlockSpec((tm, d), lambda i: (i, 0))
    tab = pl.BlockSpec((tm, 128), lambda i: (i, 0))
    return pl.pallas_call(
        body, grid=(nt,),
        in_specs=[row, _full((1, d)), ANY, tab, tab],
        out_specs=[pl.BlockSpec((tm, RET_IN), lambda i: (i, 0)), row],
        out_shape=[jax.ShapeDtypeStruct((t, RET_IN), BF16), jax.ShapeDtypeStruct((t, d), BF16)],
        scratch_shapes=[pltpu.VMEM(w_in.shape, BF16), pltpu.SemaphoreType.DMA],
        compiler_params=_params(1), name="ret_proj")(x, g_pre, w_in, cos, sin)


def ret_core_fwd(pj, intra, cross, inner, chunk_decay):
    t = pj.shape[0]
    nc = t // RET_CHUNK
    c = RET_CHUNK
    per = min(RET_STEP_CHUNKS, nc)
    qk_all = RET_HEADS * RET_QK
    v_all = RET_HEADS * RET_V

    def body(q_ref, k_ref, v_ref, dm_ref, cr_ref, in_ref, o_ref, sp_ref, state):
        @pl.when(pl.program_id(0) == 0)
        def _():
            state[...] = jnp.zeros_like(state)

        for h in range(RET_HEADS):
            for cc in range(per):
                rows = slice(cc * c, (cc + 1) * c)
                q = q_ref[rows, h * RET_QK:(h + 1) * RET_QK]
                k = k_ref[rows, h * RET_QK:(h + 1) * RET_QK]
                v = v_ref[rows, h * RET_V:(h + 1) * RET_V]
                sb = state[h].astype(BF16)
                sp_ref[cc, h] = sb
                sc = _dot_nt(q, k) * dm_ref[h]
                o_ref[rows, h * RET_V:(h + 1) * RET_V] = (_dot(sc.astype(BF16), v)
                                                          + _dot(q, sb) * cr_ref[h]).astype(BF16)
                kd = (k.astype(F32) * in_ref[h]).astype(BF16)
                state[h] = state[h] * chunk_decay[h] + _dot_tn(kd, v)

    return pl.pallas_call(
        body, grid=(nc // per,),
        in_specs=[pl.BlockSpec((per * c, qk_all), lambda n: (n, 0)), pl.BlockSpec((per * c, qk_all), lambda n: (n, 1)),
                  pl.BlockSpec((per * c, v_all), lambda n: (n, 1)),
                  _full(intra.shape), _full(cross.shape), _full(inner.shape)],
        out_specs=[pl.BlockSpec((per * c, v_all), lambda n: (n, 0)),
                   pl.BlockSpec((per, RET_HEADS, RET_QK, RET_V), lambda n: (n, 0, 0, 0))],
        out_shape=[jax.ShapeDtypeStruct((t, v_all), BF16),
                   jax.ShapeDtypeStruct((nc, RET_HEADS, RET_QK, RET_V), BF16)],
        scratch_shapes=[pltpu.VMEM((RET_HEADS, RET_QK, RET_V), F32)],
        compiler_params=_params(1), name="ret_core_fwd")(pj, pj, pj, intra, cross, inner)


def _group_norm(o_h):
    mu = jnp.mean(o_h, axis=-1, keepdims=True)
    dev = o_h - mu
    rstd = lax.rsqrt(jnp.mean(dev * dev, axis=-1, keepdims=True) + EPS)
    return dev * rstd, rstd


def ret_out_fwd(o, pj, x, gn_gain, g_post, w_out, tm=512):
    t, d = x.shape
    nt = t // tm
    v_all = RET_HEADS * RET_V

    def body(o_ref, g_ref, x_ref, gn_ref, g1_ref, w_ref, xo_ref, y_ref, out_ref):
        out = jnp.zeros((tm, d), F32)
        for h in range(RET_HEADS):
            cols = slice(h * RET_V, (h + 1) * RET_V)
            ohat, _ = _group_norm(o_ref[:, cols].astype(F32))
            g = g_ref[:, cols].astype(F32)
            yb = (g * jax.nn.sigmoid(g) * (ohat * gn_ref[:, cols])).astype(BF16)
            y_ref[:, cols] = yb
            out = out + _dot(yb, w_ref[cols, :])
        out_ref[...] = out
        m, _, _ = _rms_fwd(out, g1_ref[...])
        xo_ref[...] = x_ref[...] + m

    row = pl.BlockSpec((tm, d), lambda i: (i, 0))
    wide = pl.BlockSpec((tm, v_all), lambda i: (i, 0))
    return pl.pallas_call(
        body, grid=(nt,),
        in_specs=[wide, pl.BlockSpec((tm, v_all), lambda i: (i, 2)), row, _full((1, v_all)), _full((1, d)),
                  _full(w_out.shape)],
        out_specs=[row, wide, row],
        out_shape=[jax.ShapeDtypeStruct((t, d), F32), jax.ShapeDtypeStruct((t, v_all), BF16),
                   jax.ShapeDtypeStruct((t, d), F32)],
        compiler_params=_params(1), name="ret_out_fwd")(o, pj, x, gn_gain, g_post, w_out)


def ret_out_bwd(dxo, out, o, pj, gn_gain, g_post, w_out, tm=512):
    t, d = out.shape
    nt = t // tm
    v_all = RET_HEADS * RET_V

    def body(dxo_ref, out_ref, o_ref, g_ref, gn_ref, g1_ref, w_ref,
             dout_ref, dgate_ref, do_ref, dg1_ref, dgn_ref):
        @pl.when(pl.program_id(0) == 0)
        def _():
            dg1_ref[...] = jnp.zeros_like(dg1_ref)
            dgn_ref[...] = jnp.zeros_like(dgn_ref)

        g1 = g1_ref[...]
        dxo = dxo_ref[...]
        _, oh_, r_ = _rms_fwd(out_ref[...], g1)
        dg1_ref[...] += _colsum(dxo * oh_)
        doutb = _rms_bwd(oh_, r_, g1, dxo).astype(BF16)
        dout_ref[...] = doutb
        for h in range(RET_HEADS):
            cols = slice(h * RET_V, (h + 1) * RET_V)
            gn = gn_ref[:, cols]
            ohat, rstd = _group_norm(o_ref[:, cols].astype(F32))
            g = g_ref[:, cols].astype(F32)
            sg = jax.nn.sigmoid(g)
            dyh = _dot_nt(doutb, w_ref[cols, :])
            sil = g * sg
            dgate_ref[:, cols] = (dyh * (ohat * gn) * (sg + sil * (1.0 - sg))).astype(BF16)
            don = dyh * sil
            dgn_ref[:, cols] += _colsum(don * ohat)
            dohat = don * gn
            do_ref[:, cols] = (rstd * (dohat - jnp.mean(dohat, axis=-1, keepdims=True)
                                       - ohat * jnp.mean(dohat * ohat, axis=-1, keepdims=True))).astype(BF16)

    row = pl.BlockSpec((tm, d), lambda i: (i, 0))
    wide = pl.BlockSpec((tm, v_all), lambda i: (i, 0))
    gate = pl.BlockSpec((tm, v_all), lambda i: (i, 2))
    return pl.pallas_call(
        body, grid=(nt,),
        in_specs=[row, row, wide, gate, _full((1, v_all)), _full((1, d)), _full(w_out.shape)],
        out_specs=[row, gate, wide, _full((1, d)), _full((1, v_all))],
        out_shape=[jax.ShapeDtypeStruct((t, d), BF16), jax.ShapeDtypeStruct((t, RET_IN), BF16),
                   jax.ShapeDtypeStruct((t, v_all), BF16), jax.ShapeDtypeStruct((1, d), F32),
                   jax.ShapeDtypeStruct((1, v_all), F32)],
        compiler_params=_params(1), name="ret_out_bwd")(dxo, out, o, pj, gn_gain, g_post, w_out)


def ret_core_bwd(pj, do, sprev, cos, sin, dpj, intra, cross, inner, chunk_decay):
    t = pj.shape[0]
    nc = t // RET_CHUNK
    c = RET_CHUNK
    per = min(RET_STEP_CHUNKS, nc)
    qk_all = RET_HEADS * RET_QK
    v_all = RET_HEADS * RET_V
    scale = RET_QK ** -0.5

    def body(q_ref, k_ref, v_ref, do_ref, sp_ref, c_ref, s_ref, dm_ref, cr_ref, in_ref, dpj_in, dpj_ref, dstate):
        @pl.when(pl.program_id(0) == 0)
        def _():
            dstate[...] = jnp.zeros_like(dstate)

        for h in range(RET_HEADS):
            for cc in reversed(range(per)):
                rows = slice(cc * c, (cc + 1) * c)
                cs = c_ref[rows, :]
                sn = s_ref[rows, :]
                q = q_ref[rows, h * RET_QK:(h + 1) * RET_QK]
                k = k_ref[rows, h * RET_QK:(h + 1) * RET_QK]
                v = v_ref[rows, h * RET_V:(h + 1) * RET_V]
                doh = do_ref[rows, h * RET_V:(h + 1) * RET_V]
                dm = dm_ref[h]
                ab = (_dot_nt(q, k) * dm).astype(BF16)
                dab = (_dot_nt(doh, v) * dm).astype(BF16)
                dsb = dstate[h].astype(BF16)
                kd = (k.astype(F32) * in_ref[h]).astype(BF16)
                dv = _dot_tn(ab, doh) + _dot(kd, dsb)
                dq = _dot(dab, k) + cr_ref[h] * _dot_nt(doh, sp_ref[cc, h])
                dk = _dot_tn(dab, q) + in_ref[h] * _dot_nt(v, dsb)
                qd = (q.astype(F32) * cr_ref[h]).astype(BF16)
                dstate[h] = dstate[h] * chunk_decay[h] + _dot_tn(qd, doh)
                for base, dd, sc in ((h * RET_QK, dq, scale), (qk_all + h * RET_QK, dk, 1.0)):
                    d1, d2 = dd[:, :128], dd[:, 128:]
                    dpj_ref[rows, base:base + 128] = ((d1 * cs + d2 * sn) * sc).astype(BF16)
                    dpj_ref[rows, base + 128:base + RET_QK] = ((d2 * cs - d1 * sn) * sc).astype(BF16)
                dpj_ref[rows, 2 * qk_all + h * RET_V:2 * qk_all + (h + 1) * RET_V] = dv.astype(BF16)

    rev = lambda n: nc // per - 1 - n
    tab = pl.BlockSpec((per * c, 128), lambda n: (rev(n), 0))
    return pl.pallas_call(
        body, grid=(nc // per,),
        in_specs=[pl.BlockSpec((per * c, qk_all), lambda n: (rev(n), 0)),
                  pl.BlockSpec((per * c, qk_all), lambda n: (rev(n), 1)),
                  pl.BlockSpec((per * c, v_all), lambda n: (rev(n), 1)),
                  pl.BlockSpec((per * c, v_all), lambda n: (rev(n), 0)),
                  pl.BlockSpec((per, RET_HEADS, RET_QK, RET_V), lambda n: (rev(n), 0, 0, 0)),
                  tab, tab, _full(intra.shape), _full(cross.shape), _full(inner.shape), ANY],
        out_specs=pl.BlockSpec((per * c, 2 * qk_all + v_all), lambda n: (rev(n), 0)),
        out_shape=jax.ShapeDtypeStruct((t, RET_IN), BF16),
        scratch_shapes=[pltpu.VMEM((RET_HEADS, RET_QK, RET_V), F32)],
        input_output_aliases={10: 0},
        compiler_params=_params(1), name="ret_core_bwd")(pj, pj, pj, do, sprev, cos, sin, intra, cross, inner, dpj)


def ret_in_bwd(dpj, dres, x, g_pre, w_in, tm=512):
    t, d = x.shape
    nt = t // tm

    def body(dpj_ref, dres_ref, x_ref, g_ref, win_hbm, dx_ref, dg_ref, win_v, sem):
        @pl.when(pl.program_id(0) == 0)
        def _():
            cp = pltpu.make_async_copy(win_hbm, win_v, sem)
            cp.start()
            dg_ref[...] = jnp.zeros_like(dg_ref)
            cp.wait()

        g = g_ref[...]
        dh = jnp.zeros((tm, d), F32)
        for j in range(N_SHARD):
            dh = dh + _dot_nt(dpj_ref[:, j * RET_IN_SHARD:(j + 1) * RET_IN_SHARD], win_v[j])
        _, xh, rx = _rms_fwd(x_ref[...], g)
        dg_ref[...] += _colsum(dh * xh)
        dx_ref[...] = dres_ref[...] + _rms_bwd(xh, rx, g, dh)

    row = pl.BlockSpec((tm, d), lambda i: (i, 0))
    return pl.pallas_call(
        body, grid=(nt,),
        in_specs=[pl.BlockSpec((tm, RET_IN), lambda i: (i, 0)), row, row, _full((1, d)), ANY],
        out_specs=[row, _full((1, d))],
        out_shape=[jax.ShapeDtypeStruct((t, d), F32), jax.ShapeDtypeStruct((1, d), F32)],
        scratch_shapes=[pltpu.VMEM(w_in.shape, BF16), pltpu.SemaphoreType.DMA],
        compiler_params=_params(1), name="ret_in_bwd")(dpj, dres, x, g_pre, w_in)


_CHIP_FLIPS = ((1, 0), (0, 1), (1, 1))


def _flip(v, b):
    return 1 - v if b else v


_HBM = pl.BlockSpec(memory_space=pltpu.HBM)
_SEM = pl.BlockSpec(memory_space=pltpu.SEMAPHORE)
_EFFECT = pltpu.SideEffectType.DATAFLOW_SIDE_EFFECTING


def _chip_copies(mode, srcs, lands, send_sems, recv_sems):
    x, y, c = lax.axis_index("x"), lax.axis_index("y"), lax.axis_index("c")
    copies = []
    for t in range(len(lands)):
        if mode == "swap":
            copies.append(pltpu.make_async_remote_copy(
                src_ref=srcs[t], dst_ref=lands[t], send_sem=send_sems.at[t], recv_sem=recv_sems.at[t],
                device_id=(x, y, 1 - c), device_id_type=MESH))
            continue
        if mode == "everyone":
            for m in range(1, 8):
                bx, by, bc = (m >> 2) & 1, (m >> 1) & 1, m & 1
                copies.append(pltpu.make_async_remote_copy(
                    src_ref=srcs[t], dst_ref=lands[t].at[4 * x + 2 * y + c], send_sem=send_sems.at[7 * t + m - 1],
                    recv_sem=recv_sems.at[7 * t + m - 1], device_id=(_flip(x, bx), _flip(y, by), _flip(c, bc)),
                    device_id_type=MESH))
            continue
        for k, (bx, by) in enumerate(_CHIP_FLIPS):
            px, py = _flip(x, bx), _flip(y, by)
            target = (px, py, c)
            if mode == "gather":
                src, dst = srcs[t], lands[t].at[2 * x + y]
            elif mode == "gather_half":
                half = pl.ds(c * (srcs[t].shape[0] // 2), srcs[t].shape[0] // 2)
                src, dst = srcs[t].at[half], lands[t].at[2 * x + y, half]
            elif mode == "forward_half":
                half = pl.ds(c * (lands[t].shape[1] // 2), lands[t].shape[1] // 2)
                src = dst = lands[t].at[2 * px + py, half]
                target = (x, y, 1 - c)
            else:
                src, dst = srcs[t].at[2 * px + py], lands[t].at[k]
            copies.append(pltpu.make_async_remote_copy(
                src_ref=src, dst_ref=dst, send_sem=send_sems.at[3 * t + k], recv_sem=recv_sems.at[3 * t + k],
                device_id=target, device_id_type=MESH))
    return copies


def exchange_start(mode, srcs, lands, name, after=None):
    n, ns = len(lands), len(srcs)
    extra = [] if after is None else [after]

    def body(*refs):
        ins, lnd = refs[:ns], refs[ns:ns + n]
        send_sems, recv_sems = refs[ns + n + len(extra)], refs[ns + n + len(extra) + 1]
        token = refs[-1]
        for cp in _chip_copies(mode, ins, lnd, send_sems, recv_sems):
            cp.start()
        token[...] = jnp.zeros(token.shape, token.dtype)

    hbm = lambda a: pltpu.with_memory_space_constraint(a, pltpu.HBM)
    passed = list(srcs) + list(lands)
    n_sem = {"swap": 1, "everyone": 7}.get(mode, 3) * n
    return pl.pallas_call(
        body, name=name,
        out_shape=(pltpu.SemaphoreType.DMA((n_sem,)), pltpu.SemaphoreType.DMA((n_sem,)),
                   *[pltpu.HBM(a.shape, a.dtype) for a in passed], jax.ShapeDtypeStruct((8, 128), F32)),
        in_specs=[_HBM] * (ns + n) + [ANY] * len(extra),
        out_specs=(_SEM, _SEM, *[_HBM] * (ns + n), pl.BlockSpec(memory_space=pltpu.VMEM)),
        input_output_aliases={i: 2 + i for i in range(ns + n)},
        compiler_params=pltpu.CompilerParams(has_side_effects=_EFFECT))(*[hbm(a) for a in passed], *extra)


def exchange_wait(mode, started, after, name):
    send_sems, recv_sems = started[0], started[1]
    passed = list(started[2:-1])
    n = len(passed) if mode == "forward_half" else len(passed) // 2
    ns = len(passed) - n

    def body(*refs):
        ins, lnd = refs[:ns], refs[ns:ns + n]
        for cp in _chip_copies(mode, ins, lnd, refs[ns + n], refs[ns + n + 1]):
            cp.wait_send()
            cp.wait_recv()

    outs = pl.pallas_call(
        body, name=name, out_shape=tuple(pltpu.HBM(a.shape, a.dtype) for a in passed),
        in_specs=[_HBM] * (ns + n) + [_SEM, _SEM, ANY], out_specs=tuple([_HBM] * (ns + n)),
        input_output_aliases={i: i for i in range(ns + n)},
        compiler_params=pltpu.CompilerParams(has_side_effects=_EFFECT))(*passed, send_sems, recv_sems, after)
    return list(outs[:ns]), list(outs[ns:])


def plane_sum(slot, full, recv, name, bm=256):
    _, m, n = full.shape
    bm = _row_block(m, bm)

    def body(slot_ref, o_ref, r_ref, s_ref):
        s_ref[...] = ((o_ref[...] + r_ref[0].astype(F32)) + r_ref[1].astype(F32)) + r_ref[2].astype(F32)

    return pl.pallas_call(
        body,
        grid_spec=pltpu.PrefetchScalarGridSpec(
            num_scalar_prefetch=1, grid=(m // bm,),
            in_specs=[pl.BlockSpec((None, bm, n), lambda i, s: (s[0], i, 0)),
                      pl.BlockSpec((3, bm, n), lambda i, s: (0, i, 0))],
            out_specs=pl.BlockSpec((bm, n), lambda i, s: (i, 0))),
        out_shape=jax.ShapeDtypeStruct((m, n), F32), compiler_params=_params(1), name=name)(slot, full, recv)


def sum_slots(parts, name, bm=312):
    _, r, n = parts.shape
    bm = bm if r % bm == 0 else r

    def body(p_ref, s_ref):
        acc = p_ref[0]
        for k in range(1, 8):
            acc = acc + p_ref[k]
        s_ref[...] = acc

    return pl.pallas_call(
        body, grid=(r // bm,), in_specs=[pl.BlockSpec((8, bm, n), lambda i: (0, i, 0))],
        out_specs=pl.BlockSpec((bm, n), lambda i: (i, 0)), out_shape=jax.ShapeDtypeStruct((r, n), F32),
        compiler_params=_params(1), name=name)(parts)


def _adamw_math(w, g, m, v):
    m = ADAM_B1 * m + (1.0 - ADAM_B1) * g
    v = ADAM_B2 * v + (1.0 - ADAM_B2) * (g * g)
    m_hat = m / (1.0 - ADAM_B1 ** ADAM_STEP)
    v_hat = v / (1.0 - ADAM_B2 ** ADAM_STEP)
    delta = -ADAM_LR * (m_hat / (jnp.sqrt(v_hat) + ADAM_EPS) + ADAM_WD * w)
    return delta, m, v


def adamw(w, m, v, grads, layer, prev, name, bm=256, row0=0):
    _, _, n = w.shape
    mm = grads[0].shape[0]
    bm = _row_block(mm, bm)
    first = row0 // bm
    ng = len(grads)

    def body(*refs):
        w_ref, m_ref, v_ref = refs[:3]
        g_refs = refs[3:3 + ng]
        g_out, d_out, m_out, v_out = refs[-4:]
        g = g_refs[0][...]
        for gr in g_refs[1:]:
            g = g + gr[...]
        delta, mn, vn = _adamw_math(w_ref[...], g, m_ref[...], v_ref[...])
        g_out[...] = g
        d_out[...] = delta
        m_out[...] = mn
        v_out[...] = vn

    slab = pl.BlockSpec((None, bm, n), lambda i: (layer, first + i, 0))
    flat = pl.BlockSpec((bm, n), lambda i: (i, 0))
    in_specs = [slab] * 3 + [flat] * ng
    args = [w, m, v, *grads]
    aliases = {}
    if prev is not None:
        in_specs += [ANY] * 4
        aliases = {3 + ng + q: q for q in range(4)}
        args += list(prev)
    return pl.pallas_call(
        body, grid=(mm // bm,), in_specs=in_specs, out_specs=[slab] * 4,
        out_shape=[jax.ShapeDtypeStruct(w.shape, F32)] * 4, input_output_aliases=aliases,
        compiler_params=_params(1), name=name)(*args)


def _pack_rows(parts, rows):
    flat = jnp.concatenate([p.reshape(-1) for p in parts])
    return jnp.pad(flat, (0, rows * 128 - flat.shape[0])).reshape(rows, 128)


def _as_shards(a, rows):
    return a.reshape(N_SHARD, rows, a.shape[-1])


def _local_step(x, pos_col, target, gains, pool_w, pool_scale, gn_gain, conv_w, conv_b, weights, send_grads):
    def gain(l, n, token=None):
        g = gains[l, n].reshape(1, D_MODEL)
        return g if token is None else g + token[0:1, 0:1]

    inv_freq = (ROPE_BASE ** (-jnp.arange(0, RET_QK, 2, dtype=F32) / RET_QK)).reshape(1, RET_QK // 2)
    intra, cross, inner, chunk_decay = _decay_tables()
    dn_rows = D_FF // N_SHARD

    x1 = pool_fwd(x, gain(0, 0), gain(0, 1), pool_w, pool_scale)
    cos, sin = rope_tables(pos_col, inv_freq, x1)
    w_up0, w_dn0 = weights("mlp0", cos)
    w_dn0 = w_dn0.reshape(D_FF, D_MODEL)
    x2, u0, s0, a0, f0, h0 = mlp_fwd(x1, gain(0, 2), gain(0, 3), w_up0, w_dn0, conv_w[0], conv_b[0])
    w_in, w_out = weights("ret", x2)
    w_out = w_out.reshape(RET_HEADS * RET_V, D_MODEL)
    pj, hr = ret_proj(x2, gain(1, 0), w_in, cos, sin)
    o, sprev = ret_core_fwd(pj, intra, cross, inner, chunk_decay)
    x3, yb, out = ret_out_fwd(o, pj, x2, gn_gain, gain(1, 1), w_out)
    w_up1, w_dn1 = weights("mlp1", x3)
    w_dn1 = w_dn1.reshape(D_FF, D_MODEL)
    dx4, u1, s1, a1, f1, h1, loss = mlp_fwd(x3, gain(1, 2), gain(1, 3), w_up1, w_dn1, conv_w[1], conv_b[1], target)

    dx3, du1, df1, dg12, dg13, dcw1, dcb1 = mlp_bwd(
        dx4, f1, x3, u1, s1, gain(1, 2), gain(1, 3), w_up1, w_dn1, conv_w[1])
    dwup1 = grad_matmul(h1, du1, D_MODEL, FF_CHUNK, "grad_w_up_1", interleaved=True)
    dwdn1 = grad_matmul(a1, df1, FF_CHUNK, D_MODEL, "grad_w_down_1")
    tok = send_grads("mlp1", [dwup1, [_as_shards(g, dn_rows) for g in dwdn1]])
    dout, dpj, do, dg11, dgn = ret_out_bwd(dx3, out, o, pj, gn_gain, gain(1, 1, tok), w_out)
    dwout = grad_matmul(yb, dout, 1024, D_MODEL, "grad_w_out")
    dpj = ret_core_bwd(pj, do, sprev, cos, sin, dpj, intra, cross, inner, chunk_decay)
    dwin = grad_matmul(hr, dpj, D_MODEL, RET_IN_SHARD, "grad_w_in")
    tok = send_grads("ret", [dwin, [_as_shards(g, RET_V) for g in dwout]])
    dx2, dg10 = ret_in_bwd(dpj, dx3, x2, gain(1, 0, tok), w_in)
    dx1, du0, df0, dg02, dg03, dcw0, dcb0 = mlp_bwd(
        dx2, f0, x1, u0, s0, gain(0, 2), gain(0, 3), w_up0, w_dn0, conv_w[0])
    dwdn0 = grad_matmul(a0, df0, FF_CHUNK, D_MODEL, "grad_w_down_0")
    tok = send_grads("down0", [[_as_shards(g, dn_rows) for g in dwdn0]])
    half = D_MODEL // 2
    for part, first in (("a", 0), ("b", half)):
        dwup0 = grad_matmul(h0, du0, half, FF_CHUNK, "grad_w_up_0" + part, tk=4096, interleaved=True, after=tok,
                            cols=(first, half))
        tok = send_grads("up0" + part, [dwup0])
    dx0, dg00, dg01, dpscale, dpw = pool_bwd(dx1, x, gain(0, 0, tok), gain(0, 1), pool_w, pool_scale)

    rows = lambda g: jnp.sum(g, axis=0, keepdims=True)
    dgains = jnp.concatenate([dg00, dg01, rows(dg02), rows(dg03), dg10, dg11, rows(dg12), rows(dg13)],
                             axis=0).reshape(2, 4, D_MODEL)
    small = {"gains": dgains, "pool_scale": dpscale, "gn": dgn,
             "conv_w": jnp.sum(jnp.stack([dcw0, dcw1]), axis=3),
             "conv_b": jnp.sum(jnp.stack([dcb0, dcb1]), axis=2, keepdims=True), "pool_w": dpw}
    return loss, dx0, small


def kernel(x, positions, norm_gain, pool_w, pool_scale, ret_w_in, ret_gn_gain, ret_w_out, mlp_w_up, mlp_conv_w, mlp_conv_b, mlp_w_down, loss_target, m_norm_gain, m_pool_w, m_pool_scale, m_ret_w_in, m_ret_gn_gain, m_ret_w_out, m_mlp_w_up, m_mlp_conv_w, m_mlp_conv_b, m_mlp_w_down, v_norm_gain, v_pool_w, v_pool_scale, v_ret_w_in, v_ret_gn_gain, v_ret_w_out, v_mlp_w_up, v_mlp_conv_w, v_mlp_conv_b, v_mlp_w_down):
    t = x.shape[1]
    me = 2 * lax.axis_index("x") + lax.axis_index("y")
    me_slot = jnp.reshape(me, (1,)).astype(jnp.int32)

    small_parts = [norm_gain, ret_gn_gain, mlp_conv_w, pool_w]
    small_sizes = [p.size for p in small_parts]
    small_rows = -(-sum(small_sizes) // (128 * 8)) * 8
    gathers = {}

    def start_gather(group, srcs, after):
        lands = [lax.dynamic_update_index_in_dim(lax.empty((N_SHARD,) + s.shape, s.dtype), s, me, 0) for s in srcs]
        mode = "gather_half" if group == "mlp0" else "gather"
        gathers[group] = (mode, exchange_start(mode, srcs, lands, "gather_start_" + group, after=after))
        return gathers[group][1][-1]

    token = start_gather("small", [_pack_rows(small_parts, small_rows)], None)
    token = start_gather("mlp0", [mlp_w_up[0].astype(BF16), mlp_w_down[0].astype(BF16)], token)

    def weights(group, after):
        if group == "mlp0":
            tok = start_gather("ret", [ret_w_in[0].astype(BF16), ret_w_out[0].astype(BF16)], after)
            after = start_gather("mlp1", [mlp_w_up[1].astype(BF16), mlp_w_down[1].astype(BF16)], tok)
        mode, started = gathers[group]
        _, lands = exchange_wait(mode, started, after, "gather_wait_" + group)
        if mode == "gather_half":
            forward = exchange_start("forward_half", [], lands, "forward_start_" + group)
            _, lands = exchange_wait("forward_half", forward, forward[-1], "forward_wait_" + group)
        return lands

    sent, early = {}, {}

    def reduced(group, after, names):
        started, own = sent[group]
        _, recv = exchange_wait("scatter", started, after, "scatter_wait_" + group)
        return [plane_sum(me_slot, f, r, "plane_sum_" + nm)
                for f, r, nm in zip(own, recv, names)]

    def swap_start(planes, name):
        return exchange_start("swap", planes, [lax.empty(p.shape, p.dtype) for p in planes], name)

    def send_grads(group, pairs):
        lands = [lax.empty((3,) + b.shape[1:], BF16) for _, b in pairs]
        sent[group] = (exchange_start("scatter", [b for _, b in pairs], lands, "scatter_start_" + group),
                       [f for f, _ in pairs])
        token = sent[group][0][-1]
        if group == "down0":
            marker = pairs[0][1]
            early["planes"] = (reduced("mlp1", marker, ["w_up_1", "w_down_1"])
                               + reduced("ret", marker, ["w_in", "w_out"]))
            early["swap"] = swap_start(early["planes"], "swap_start_a")
            token = token + early["swap"][-1]
        return token

    (smallg,) = weights("small", token)
    smallg = smallg.reshape(N_SHARD, -1)
    offs = np.cumsum([0] + small_sizes)
    piece = lambda i, shape: smallg[:, offs[i]:offs[i + 1]].reshape((N_SHARD,) + shape)
    gains = piece(0, (2, 4, 256)).transpose(1, 2, 0, 3).reshape(2, 4, D_MODEL)
    gn_full = piece(1, (512,)).reshape(1, RET_HEADS * RET_V)
    cw_full = piece(2, (2, 3, FF_CHUNK)).transpose(1, 0, 2, 3)
    pw_full = piece(3, (4, 64, 256)).transpose(1, 0, 2, 3).reshape(4, 256, 256).astype(BF16)
    cb_full = mlp_conv_b.reshape(2, N_SHARD, 1, FF_CHUNK)

    loss, dx0, small = _local_step(
        x[0], positions.reshape(t, 1).astype(F32), loss_target[0], gains, pw_full, pool_scale, gn_full,
        cw_full, cb_full, weights, send_grads)

    def small_adamw(w, m, v, grads, name):
        w3 = w.reshape(1, -1, w.shape[-1])
        out = adamw(w3, m.reshape(w3.shape), v.reshape(w3.shape), [g.reshape(w3.shape[1:]) for g in grads], 0, None, name)
        return [o.reshape(w.shape) for o in out]

    pw_f = small["pool_w"].reshape(4, N_SHARD, 64, 256).transpose(1, 0, 2, 3).reshape(N_SHARD, 256, 256)
    small = dict(small, loss=loss)
    small_order = ["gains", "pool_scale", "gn", "conv_w", "conv_b", "loss"]
    gsmall_sizes = [small[k].size for k in small_order]
    gsmall_rows = -(-sum(gsmall_sizes) // (128 * 8)) * 8
    gpack = _pack_rows([small[k] for k in small_order], gsmall_rows)
    mine = 2 * me + lax.axis_index("c")
    small_started = exchange_start(
        "everyone", [gpack], [lax.dynamic_update_index_in_dim(lax.empty((8,) + gpack.shape, F32), gpack, mine, 0)],
        "small_start")
    send_grads("pool_w", [(pw_f, pw_f.astype(BF16))])

    res = {}
    planes_a, others_a = exchange_wait("swap", early["swap"], small_started[-1], "swap_wait_a")
    res["ret_w_in"] = adamw(ret_w_in, m_ret_w_in, v_ret_w_in, (planes_a[2], others_a[2]), 0, None, "adamw_w_in")
    res["ret_w_out"] = adamw(ret_w_out, m_ret_w_out, v_ret_w_out, (planes_a[3], others_a[3]), 0, None, "adamw_w_out")
    up1 = adamw(mlp_w_up, m_mlp_w_up, v_mlp_w_up, (planes_a[0], others_a[0]), 1, None, "adamw_w_up_1")
    dn1 = adamw(mlp_w_down, m_mlp_w_down, v_mlp_w_down, (planes_a[1], others_a[1]), 1, None, "adamw_w_down_1")

    planes_b = (reduced("up0a", dn1[0], ["w_up_0a"]) + reduced("up0b", dn1[0], ["w_up_0b"])
                + reduced("down0", dn1[0], ["w_down_0"]) + reduced("pool_w", dn1[0], ["pool_w"]))
    swap_b = swap_start(planes_b, "swap_start_b")

    _, (small_recv,) = exchange_wait("everyone", small_started, swap_b[-1], "small_wait")
    gsmall = sum_slots(small_recv, "sum_small").reshape(-1)
    goffs = np.cumsum([0] + gsmall_sizes)
    gpiece = lambda i: gsmall[goffs[i]:goffs[i + 1]].reshape(small[small_order[i]].shape)
    g_gains = lax.dynamic_slice_in_dim(gpiece(0), me * 256, 256, axis=2)
    g_gn = lax.dynamic_slice_in_dim(gpiece(2), me * RET_V, RET_V, axis=1)
    g_cw = lax.dynamic_index_in_dim(gpiece(3), me, 1, keepdims=False)
    res["norm_gain"] = small_adamw(norm_gain, m_norm_gain, v_norm_gain, [g_gains], "adamw_norm_gain")
    res["pool_scale"] = small_adamw(pool_scale, m_pool_scale, v_pool_scale, [gpiece(1)], "adamw_pool_scale")
    res["ret_gn_gain"] = small_adamw(ret_gn_gain, m_ret_gn_gain, v_ret_gn_gain, [g_gn], "adamw_gn_gain")
    res["mlp_conv_w"] = small_adamw(mlp_conv_w, m_mlp_conv_w, v_mlp_conv_w, [g_cw], "adamw_conv_w")
    res["mlp_conv_b"] = small_adamw(mlp_conv_b, m_mlp_conv_b, v_mlp_conv_b, [gpiece(4)], "adamw_conv_b")

    planes_b, others_b = exchange_wait("swap", swap_b, res["mlp_conv_b"][0], "swap_wait_b")
    up0a = adamw(mlp_w_up, m_mlp_w_up, v_mlp_w_up, (planes_b[0], others_b[0]), 0, up1, "adamw_w_up_0a")
    res["mlp_w_up"] = adamw(mlp_w_up, m_mlp_w_up, v_mlp_w_up, (planes_b[1], others_b[1]), 0, up0a, "adamw_w_up_0b",
                            row0=D_MODEL // 2)
    res["mlp_w_down"] = adamw(mlp_w_down, m_mlp_w_down, v_mlp_w_down, (planes_b[2], others_b[2]), 0, dn1,
                              "adamw_w_down_0")
    res["pool_w"] = small_adamw(pool_w, m_pool_w, v_pool_w, (planes_b[3], others_b[3]), "adamw_pool_w")

    order = ["norm_gain", "pool_w", "pool_scale", "ret_w_in", "ret_gn_gain", "ret_w_out", "mlp_w_up", "mlp_conv_w",
             "mlp_conv_b", "mlp_w_down"]
    outs = [gpiece(5)[0, 0], dx0.reshape(x.shape)]
    for q in range(4):
        outs += [res[k][q] for k in order]
    return tuple(outs)
```

```python
import numpy as np
import jax
import jax.numpy as jnp
from jax import lax
from jax.experimental import pallas as pl
from jax.experimental.pallas import tpu as pltpu

F32 = jnp.float32
BF16 = jnp.bfloat16

D_MODEL = 1024
D_FF = 2816
FF_CHUNK = 1408
N_SHARD = 4
POOL_WINDOWS = (2, 4, 8, 16)
POOL_DIM = 256
POOL_HALO = 16
RET_HEADS = 4
RET_QK = 256
RET_V = 512
RET_CHUNK = 256
RET_STEP_CHUNKS = 2
RET_IN = 6144
RET_IN_SHARD = 1536
ROPE_BASE = 10000.0
EPS = 1e-6
CONV_HALO = 8

ADAM_LR, ADAM_B1, ADAM_B2, ADAM_EPS, ADAM_WD, ADAM_STEP = 0.001, 0.9, 0.999, 1e-08, 0.01, 10

VMEM_LIMIT = 56 * 1024 * 1024
VMEM_LIMIT_MLP_BWD = 62 * 1024 * 1024
MESH = pl.DeviceIdType.MESH
ANY = pl.BlockSpec(memory_space=pl.ANY)


def _params(n_grid=1, limit=VMEM_LIMIT):
    return pltpu.CompilerParams(dimension_semantics=("arbitrary",) * n_grid, vmem_limit_bytes=limit)


def _dot(a, b):
    return jnp.dot(a, b, preferred_element_type=F32)


def _dot_nt(a, b):
    return lax.dot_general(a, b, (((1,), (1,)), ((), ())), preferred_element_type=F32)


def _dot_tn(a, b):
    return lax.dot_general(a, b, (((0,), (0,)), ((), ())), preferred_element_type=F32)


def _rms_fwd(x, gain):
    r = lax.rsqrt(jnp.mean(x * x, axis=-1, keepdims=True) + EPS)
    xh = x * r
    return xh * gain, xh, r


def _rms_bwd(xh, r, gain, dy):
    dxh = dy * gain
    return r * (dxh - xh * jnp.mean(dxh * xh, axis=-1, keepdims=True))


def _colsum(v):
    return jnp.sum(v, axis=0, keepdims=True)


def _full(shape):
    nd = len(shape)
    return pl.BlockSpec(shape, lambda *_: (0,) * nd)


def rope_tables(pos_col, inv_freq, after):
    t = pos_col.shape[0]
    tm = min(t, 1024)

    def body(p_ref, f_ref, after_ref, c_ref, s_ref):
        ang = p_ref[...] * f_ref[...]
        c_ref[...] = jnp.cos(ang)
        s_ref[...] = jnp.sin(ang)

    return pl.pallas_call(
        body, grid=(t // tm,),
        in_specs=[pl.BlockSpec((tm, 1), lambda i: (i, 0)), _full((1, 128)), ANY],
        out_specs=[pl.BlockSpec((tm, 128), lambda i: (i, 0))] * 2,
        out_shape=[jax.ShapeDtypeStruct((t, 128), F32)] * 2,
        compiler_params=_params(1), name="rope_tables")(pos_col, inv_freq, after)


def _window_sums(ext, backward):
    n = ext.shape[0]
    cur, sums = ext, []
    for g, win in enumerate(POOL_WINDOWS):
        if g > 0:
            cur = cur[:, POOL_DIM:]
        half = win // 2
        cur = cur + pltpu.roll(cur, n - half if backward else half, axis=0)
        sums.append(cur[:, 0:POOL_DIM])
    return sums


def _pool_diff(h_halo, h, row0, tm):
    t_idx = row0 + lax.broadcasted_iota(jnp.int32, (tm, 1), 0)
    sums = _window_sums(jnp.concatenate([h_halo, h], axis=0), backward=False)
    parts, inv_counts = [], []
    for g, win in enumerate(POOL_WINDOWS):
        inv = 1.0 / jnp.minimum(t_idx + 1, win).astype(F32)
        parts.append(sums[g][POOL_HALO:, :] * inv - h[:, g * POOL_DIM:(g + 1) * POOL_DIM])
        inv_counts.append(inv)
    return parts, inv_counts


def pool_fwd(x, g_pre, g_post, pool_w, pool_scale, tm=512):
    t, d = x.shape
    nt = t // tm

    def body(x_ref, g0_ref, g1_ref, w_ref, sc_ref, o_ref, hext):
        i = pl.program_id(0)

        @pl.when(i == 0)
        def _():
            hext[...] = jnp.zeros((POOL_HALO, d), F32)

        xv = x_ref[...]
        h, _, _ = _rms_fwd(xv, g0_ref[...])
        parts, _ = _pool_diff(hext[...], h, i * tm, tm)
        hext[...] = h[tm - POOL_HALO:tm, :]
        ys = [_dot(parts[g].astype(BF16), w_ref[g]) for g in range(len(POOL_WINDOWS))]
        y = jnp.concatenate(ys, axis=-1) * sc_ref[...]
        m, _, _ = _rms_fwd(y, g1_ref[...])
        o_ref[...] = xv + m

    row = pl.BlockSpec((tm, d), lambda i: (i, 0))
    return pl.pallas_call(
        body, grid=(nt,),
        in_specs=[row, _full((1, d)), _full((1, d)), _full(pool_w.shape), _full((1, d))],
        out_specs=row, out_shape=jax.ShapeDtypeStruct((t, d), F32),
        scratch_shapes=[pltpu.VMEM((POOL_HALO, d), F32)],
        compiler_params=_params(1), name="pool_fwd")(x, g_pre, g_post, pool_w, pool_scale)


def pool_bwd(dx1, x, g_pre, g_post, pool_w, pool_scale, tm=512):
    t, d = x.shape
    nt = t // tm
    ng = len(POOL_WINDOWS)

    def body(dx1_ref, x_ref, xh_ref, g0_ref, g1_ref, w_ref, sc_ref,
             dx_ref, dg0_ref, dg1_ref, dsc_ref, dw_ref, enext):
        i = pl.program_id(0)
        r = nt - 1 - i

        @pl.when(i == 0)
        def _():
            enext[...] = jnp.zeros((POOL_HALO, d), F32)
            dg0_ref[...] = jnp.zeros_like(dg0_ref)
            dg1_ref[...] = jnp.zeros_like(dg1_ref)
            dsc_ref[...] = jnp.zeros_like(dsc_ref)
            dw_ref[...] = jnp.zeros_like(dw_ref)

        g0 = g0_ref[...]
        g1 = g1_ref[...]
        sc = sc_ref[...]
        xv = x_ref[...]
        h, xh, rx = _rms_fwd(xv, g0)
        h_halo, _, _ = _rms_fwd(xh_ref[...], g0)
        parts, inv_counts = _pool_diff(h_halo * jnp.where(r > 0, 1.0, 0.0), h, r * tm, tm)
        parts_b = [p.astype(BF16) for p in parts]
        ypre = jnp.concatenate([_dot(parts_b[g], w_ref[g]) for g in range(ng)], axis=-1)
        _, yh, ry = _rms_fwd(ypre * sc, g1)
        dm = dx1_ref[...]
        dg1_ref[...] += _colsum(dm * yh)
        dy = _rms_bwd(yh, ry, g1, dm)
        dsc_ref[...] += _colsum(dy * ypre)
        dyp = (dy * sc).astype(BF16)
        ddiffs = []
        for g in range(ng):
            cols = slice(g * POOL_DIM, (g + 1) * POOL_DIM)
            dw_ref[g] += _dot_tn(parts_b[g], dyp[:, cols])
            ddiffs.append(_dot_nt(dyp[:, cols], w_ref[g]))
        e = jnp.concatenate([ddiffs[g] * inv_counts[g] for g in range(ng)], axis=-1)
        sums = _window_sums(jnp.concatenate([e, enext[...]], axis=0), backward=True)
        enext[...] = e[0:POOL_HALO, :]
        dh = jnp.concatenate([sums[g][0:tm, :] - ddiffs[g] for g in range(ng)], axis=-1)
        dg0_ref[...] += _colsum(dh * xh)
        dx_ref[...] = dm + _rms_bwd(xh, rx, g0, dh)

    row = pl.BlockSpec((tm, d), lambda i: (nt - 1 - i, 0))
    halo = pl.BlockSpec((POOL_HALO, d), lambda i: (jnp.maximum((nt - 1 - i) * (tm // POOL_HALO) - 1, 0), 0))
    vec = _full((1, d))
    return pl.pallas_call(
        body, grid=(nt,),
        in_specs=[row, row, halo, vec, vec, _full(pool_w.shape), vec],
        out_specs=[row, vec, vec, vec, _full((ng, POOL_DIM, POOL_DIM))],
        out_shape=[jax.ShapeDtypeStruct((t, d), F32)] + [jax.ShapeDtypeStruct((1, d), F32)] * 3
        + [jax.ShapeDtypeStruct((ng, POOL_DIM, POOL_DIM), F32)],
        scratch_shapes=[pltpu.VMEM((POOL_HALO, d), F32)],
        compiler_params=_params(1), name="pool_bwd")(dx1, x, x, g_pre, g_post, pool_w, pool_scale)


def _conv_taps(cw_ref, j):
    return cw_ref[j, 0:1, :], cw_ref[j, 1:2, :], cw_ref[j, 2:3, :]


def _row_block(m, target=256):
    if m <= target:
        return m
    for b in range(target, 7, -8):
        if m % b == 0:
            return b
    return m


def mlp_fwd(x, g_pre, g_post, w_up, w_down, conv_w, conv_b, target=None, tm=256):
    t, d = x.shape
    nt = t // tm
    h8 = CONV_HALO
    with_loss = target is not None
    n_extra = 1 if with_loss else 0

    def body(x_ref, g2_ref, g3_ref, wup_hbm, wdn_hbm, cw_ref, cb_ref, *rest):
        tgt_ref = rest[0] if with_loss else None
        xo_ref, u_ref, s_ref, a_ref, f_ref, h_ref = rest[n_extra:n_extra + 6]
        loss_ref = rest[n_extra + 6] if with_loss else None
        wup_v, wdn_v, tail, sem = rest[-4:]
        i = pl.program_id(0)

        @pl.when(i == 0)
        def _():
            c1 = pltpu.make_async_copy(wup_hbm, wup_v, sem.at[0])
            c2 = pltpu.make_async_copy(wdn_hbm, wdn_v, sem.at[1])
            c1.start()
            c2.start()
            tail[...] = jnp.zeros_like(tail)
            if with_loss:
                loss_ref[...] = jnp.zeros_like(loss_ref)
            c1.wait()
            c2.wait()

        xv = x_ref[...]
        h, _, _ = _rms_fwd(xv, g2_ref[...])
        hb = h.astype(BF16)
        h_ref[...] = hb
        acc = jnp.zeros((tm, d), F32)
        for k in range(2):
            cs = []
            for s in range(2):
                j, cols = k + 2 * s, slice((2 * k + s) * FF_CHUNK, (2 * k + s + 1) * FF_CHUNK)
                uf = _dot(hb, wup_v[j])
                u_ref[:, cols] = uf.astype(BF16)
                ext = jnp.concatenate([tail[j], uf], axis=0)
                tail[j] = uf[tm - h8:tm, :]
                w0, w1, w2 = _conv_taps(cw_ref, j)
                cs.append(cb_ref[j] + w2 * uf + w1 * pltpu.roll(ext, 1, axis=0)[h8:, :]
                          + w0 * pltpu.roll(ext, 2, axis=0)[h8:, :])
            cg, cv = cs
            sg = jax.nn.sigmoid(cg)
            sil = cg * sg
            ab = (sil * cv).astype(BF16)
            a_ref[:, k * FF_CHUNK:(k + 1) * FF_CHUNK] = ab
            s_ref[:, 2 * k * FF_CHUNK:(2 * k + 1) * FF_CHUNK] = sil.astype(BF16)
            s_ref[:, (2 * k + 1) * FF_CHUNK:(2 * k + 2) * FF_CHUNK] = (cv * (sg + sil * (1.0 - sg))).astype(BF16)
            acc = acc + _dot(ab, wdn_v[k * FF_CHUNK:(k + 1) * FF_CHUNK, :])
        f_ref[...] = acc
        y, _, _ = _rms_fwd(acc, g3_ref[...])
        if with_loss:
            err = (xv + y) - tgt_ref[...]
            xo_ref[...] = err * (1.0 / d)
            loss_ref[...] += 0.5 * jnp.sum(jnp.mean(err * err, axis=-1, keepdims=True), axis=0, keepdims=True)
        else:
            xo_ref[...] = xv + y

    row = pl.BlockSpec((tm, d), lambda i: (i, 0))
    wide = pl.BlockSpec((tm, 2 * D_FF), lambda i: (i, 0))
    vec = _full((1, d))
    extra = [target] if with_loss else []
    return pl.pallas_call(
        body, grid=(nt,),
        in_specs=[row, vec, vec, ANY, ANY, _full(conv_w.shape), _full(conv_b.shape)] + [row] * n_extra,
        out_specs=[row, wide, wide, pl.BlockSpec((tm, D_FF), lambda i: (i, 0)), row, row] + [_full((1, 1))] * n_extra,
        out_shape=[jax.ShapeDtypeStruct((t, d), F32), jax.ShapeDtypeStruct((t, 2 * D_FF), BF16),
                   jax.ShapeDtypeStruct((t, 2 * D_FF), BF16), jax.ShapeDtypeStruct((t, D_FF), BF16),
                   jax.ShapeDtypeStruct((t, d), F32), jax.ShapeDtypeStruct((t, d), BF16)]
        + [jax.ShapeDtypeStruct((1, 1), F32)] * n_extra,
        scratch_shapes=[pltpu.VMEM(w_up.shape, BF16), pltpu.VMEM(w_down.shape, BF16),
                        pltpu.VMEM((N_SHARD, h8, FF_CHUNK), F32), pltpu.SemaphoreType.DMA((2,))],
        compiler_params=_params(1), name="mlp_fwd_loss" if with_loss else "mlp_fwd")(
            x, g_pre, g_post, w_up, w_down, conv_w, conv_b, *extra)


def _rowsum8(v):
    return jnp.sum(v.reshape(v.shape[0] // 8, 8, v.shape[1]), axis=0)


def mlp_bwd(dxo, f, x, u, sp, g_pre, g_post, w_up, w_down, conv_w, tm=256):
    t, d = x.shape
    nt = t // tm
    h8 = CONV_HALO

    def body(dxo_ref, f_ref, x_ref, u_ref, s_ref, g2_ref, g3_ref, wup_hbm, wdn_hbm, cw_ref,
             dx_ref, du_ref, df_ref, dg2_ref, dg3_ref, dcw_ref, dcb_ref,
             wup_v, wdn_v, carry, sem):
        @pl.when(pl.program_id(0) == 0)
        def _():
            c1 = pltpu.make_async_copy(wup_hbm, wup_v, sem.at[0])
            c2 = pltpu.make_async_copy(wdn_hbm, wdn_v, sem.at[1])
            c1.start()
            c2.start()
            carry[...] = jnp.zeros_like(carry)
            dg2_ref[...] = jnp.zeros_like(dg2_ref)
            dg3_ref[...] = jnp.zeros_like(dg3_ref)
            dcw_ref[...] = jnp.zeros_like(dcw_ref)
            dcb_ref[...] = jnp.zeros_like(dcb_ref)
            c1.wait()
            c2.wait()

        g3 = g3_ref[...]
        dxo = dxo_ref[...]
        _, fh, rf = _rms_fwd(f_ref[...], g3)
        dg3_ref[...] += _rowsum8(dxo * fh)
        dfb = _rms_bwd(fh, rf, g3, dxo).astype(BF16)
        df_ref[...] = dfb
        dh = jnp.zeros((tm, d), F32)
        for k in range(2):
            da = _dot_nt(dfb, wdn_v[k * FF_CHUNK:(k + 1) * FF_CHUNK, :])
            for s in range(2):
                j = k + 2 * s
                cols = slice((2 * k + s) * FF_CHUNK, (2 * k + s + 1) * FF_CHUNK)
                dc = da * s_ref[:, (2 * k + 1 - s) * FF_CHUNK:(2 * k + 2 - s) * FF_CHUNK].astype(F32)
                uf = u_ref[:, cols].astype(F32)
                ext = jnp.concatenate([dc, carry[j]], axis=0)
                carry[j] = dc[0:h8, :]
                dc1 = pltpu.roll(ext, tm + h8 - 1, axis=0)[0:tm, :]
                dc2 = pltpu.roll(ext, tm + h8 - 2, axis=0)[0:tm, :]
                dcb_ref[j] += _rowsum8(dc)
                dcw_ref[j, 2] += _rowsum8(dc * uf)
                dcw_ref[j, 1] += _rowsum8(dc1 * uf)
                dcw_ref[j, 0] += _rowsum8(dc2 * uf)
                dub = (cw_ref[j, 2:3, :] * dc + cw_ref[j, 1:2, :] * dc1 + cw_ref[j, 0:1, :] * dc2).astype(BF16)
                du_ref[:, cols] = dub
                dh = dh + _dot_nt(dub, wup_v[j])
        g2 = g2_ref[...]
        _, xh, rx = _rms_fwd(x_ref[...], g2)
        dg2_ref[...] += _rowsum8(dh * xh)
        dx_ref[...] = dxo + _rms_bwd(xh, rx, g2, dh)

    row = pl.BlockSpec((tm, d), lambda i: (nt - 1 - i, 0))
    wide = pl.BlockSpec((tm, 2 * D_FF), lambda i: (nt - 1 - i, 0))
    vec = _full((1, d))
    acc = _full((8, d))
    dcw_shape, dcb_shape = (N_SHARD, 3, 8, FF_CHUNK), (N_SHARD, 8, FF_CHUNK)
    return pl.pallas_call(
        body, grid=(nt,),
        in_specs=[row, row, row, wide, wide, vec, vec, ANY, ANY, _full(conv_w.shape)],
        out_specs=[row, wide, row, acc, acc, _full(dcw_shape), _full(dcb_shape)],
        out_shape=[jax.ShapeDtypeStruct((t, d), F32), jax.ShapeDtypeStruct((t, 2 * D_FF), BF16),
                   jax.ShapeDtypeStruct((t, d), BF16),
                   jax.ShapeDtypeStruct((8, d), F32), jax.ShapeDtypeStruct((8, d), F32),
                   jax.ShapeDtypeStruct(dcw_shape, F32), jax.ShapeDtypeStruct(dcb_shape, F32)],
        scratch_shapes=[pltpu.VMEM(w_up.shape, BF16), pltpu.VMEM(w_down.shape, BF16),
                        pltpu.VMEM((N_SHARD, h8, FF_CHUNK), F32), pltpu.SemaphoreType.DMA((2,))],
        compiler_params=_params(1, VMEM_LIMIT_MLP_BWD), name="mlp_bwd")(
            dxo, f, x, u, sp, g_pre, g_post, w_up, w_down, conv_w)


def grad_matmul(a, b, bm, bn, name, tk=2048, interleaved=False, after=None, cols=None):
    t = a.shape[0]
    m0, m = (0, a.shape[1]) if cols is None else cols
    n = b.shape[1]
    tk = min(tk, t)
    nk = t // tk
    place = (lambda j: (j % 2) * 2 + j // 2) if interleaved else (lambda j: j)
    extra = [] if after is None else [after]
    first = m0 // bm

    def body(a_ref, b_ref, *rest):
        o_ref, ob_ref = rest[len(extra):]
        kk = pl.program_id(2)

        @pl.when(kk == 0)
        def _():
            o_ref[...] = jnp.zeros_like(o_ref)

        o_ref[...] += _dot_tn(a_ref[...], b_ref[...])

        @pl.when(kk == nk - 1)
        def _():
            ob_ref[...] = o_ref[...].astype(BF16)

    ospec = pl.BlockSpec((None, bm, bn), lambda j, i, kk: (place(j), i, 0))
    return pl.pallas_call(
        body, grid=(n // bn, m // bm, nk),
        in_specs=[pl.BlockSpec((tk, bm), lambda j, i, kk: (kk, first + i)),
                  pl.BlockSpec((tk, bn), lambda j, i, kk: (kk, j))]
        + [ANY] * len(extra),
        out_specs=[ospec, ospec],
        out_shape=[jax.ShapeDtypeStruct((n // bn, m, bn), F32), jax.ShapeDtypeStruct((n // bn, m, bn), BF16)],
        compiler_params=_params(3), name=name)(a, b, *extra)


def _decay_tables():
    log_gamma = jnp.log(1.0 - 2.0 ** (-5.0 - jnp.arange(RET_HEADS, dtype=F32)))
    i = jnp.arange(RET_CHUNK, dtype=F32)
    rel = i[:, None] - i[None, :]
    intra = jnp.where(rel >= 0, jnp.exp(jnp.maximum(rel, 0.0) * log_gamma[:, None, None]), 0.0)
    cross = jnp.exp((i + 1.0) * log_gamma[:, None])[:, :, None]
    inner = jnp.exp((RET_CHUNK - 1.0 - i) * log_gamma[:, None])[:, :, None]
    chunk = [float(np.exp(np.float32(RET_CHUNK) * np.log(np.float32(1.0 - 2.0 ** (-5.0 - h))).astype(np.float32)))
             for h in range(RET_HEADS)]
    return intra, cross, inner, chunk


def ret_proj(x, g_pre, w_in, cos, sin, tm=512):
    t, d = x.shape
    nt = t // tm
    per = RET_IN_SHARD // RET_QK

    def body(x_ref, g_ref, win_hbm, c_ref, s_ref, pj_ref, h_ref, win_v, sem):
        @pl.when(pl.program_id(0) == 0)
        def _():
            cp = pltpu.make_async_copy(win_hbm, win_v, sem)
            cp.start()
            cp.wait()

        h, _, _ = _rms_fwd(x_ref[...], g_ref[...])
        hb = h.astype(BF16)
        h_ref[...] = hb
        c = c_ref[...]
        s = s_ref[...]
        for j in range(N_SHARD):
            pjj = _dot(hb, win_v[j])
            for bb in range(per):
                b = per * j + bb
                blk = pjj[:, bb * RET_QK:(bb + 1) * RET_QK]
                if b < 2 * RET_HEADS:
                    x1, x2 = blk[:, :128], blk[:, 128:]
                    o1 = x1 * c - x2 * s
                    o2 = x2 * c + x1 * s
                    if b < RET_HEADS:
                        o1 = o1 * (RET_QK ** -0.5)
                        o2 = o2 * (RET_QK ** -0.5)
                    pj_ref[:, b * RET_QK:b * RET_QK + 128] = o1.astype(BF16)
                    pj_ref[:, b * RET_QK + 128:(b + 1) * RET_QK] = o2.astype(BF16)
                else:
                    pj_ref[:, b * RET_QK:(b + 1) * RET_QK] = blk.astype(BF16)

    row = pl.BlockSpec((tm, d), lambda i: (i, 0))
    tab = pl.BlockSpec((tm, 128), lambda i: (i, 0))
    return pl.pallas_call(
        body, grid=(nt,),
        in_specs=[row, _full((1, d)), ANY, tab, tab],
        out_specs=[pl.BlockSpec((tm, RET_IN), lambda i: (i, 0)), row],
        out_shape=[jax.ShapeDtypeStruct((t, RET_IN), BF16), jax.ShapeDtypeStruct((t, d), BF16)],
        scratch_shapes=[pltpu.VMEM(w_in.shape, BF16), pltpu.SemaphoreType.DMA],
        compiler_params=_params(1), name="ret_proj")(x, g_pre, w_in, cos, sin)


def ret_core_fwd(pj, intra, cross, inner, chunk_decay):
    t = pj.shape[0]
    nc = t // RET_CHUNK
    c = RET_CHUNK
    per = RET_STEP_CHUNKS
    qk_all = RET_HEADS * RET_QK
    v_all = RET_HEADS * RET_V

    def body(q_ref, k_ref, v_ref, dm_ref, cr_ref, in_ref, o_ref, sp_ref, state):
        @pl.when(pl.program_id(0) == 0)
        def _():
            state[...] = jnp.zeros_like(state)

        for h in range(RET_HEADS):
            for cc in range(per):
                rows = slice(cc * c, (cc + 1) * c)
                q = q_ref[rows, h * RET_QK:(h + 1) * RET_QK]
                k = k_ref[rows, h * RET_QK:(h + 1) * RET_QK]
                v = v_ref[rows, h * RET_V:(h + 1) * RET_V]
                sb = state[h].astype(BF16)
                sp_ref[cc, h] = sb
                sc = _dot_nt(q, k) * dm_ref[h]
                o_ref[rows, h * RET_V:(h + 1) * RET_V] = (_dot(sc.astype(BF16), v)
                                                          + _dot(q, sb) * cr_ref[h]).astype(BF16)
                kd = (k.astype(F32) * in_ref[h]).astype(BF16)
                state[h] = state[h] * chunk_decay[h] + _dot_tn(kd, v)

    return pl.pallas_call(
        body, grid=(nc // per,),
        in_specs=[pl.BlockSpec((per * c, qk_all), lambda n: (n, 0)), pl.BlockSpec((per * c, qk_all), lambda n: (n, 1)),
                  pl.BlockSpec((per * c, v_all), lambda n: (n, 1)),
                  _full(intra.shape), _full(cross.shape), _full(inner.shape)],
        out_specs=[pl.BlockSpec((per * c, v_all), lambda n: (n, 0)),
                   pl.BlockSpec((per, RET_HEADS, RET_QK, RET_V), lambda n: (n, 0, 0, 0))],
        out_shape=[jax.ShapeDtypeStruct((t, v_all), BF16),
                   jax.ShapeDtypeStruct((nc, RET_HEADS, RET_QK, RET_V), BF16)],
        scratch_shapes=[pltpu.VMEM((RET_HEADS, RET_QK, RET_V), F32)],
        compiler_params=_params(1), name="ret_core_fwd")(pj, pj, pj, intra, cross, inner)


def _group_norm(o_h):
    mu = jnp.mean(o_h, axis=-1, keepdims=True)
    dev = o_h - mu
    rstd = lax.rsqrt(jnp.mean(dev * dev, axis=-1, keepdims=True) + EPS)
    return dev * rstd, rstd


def ret_out_fwd(o, pj, x, gn_gain, g_post, w_out, tm=512):
    t, d = x.shape
    nt = t // tm
    v_all = RET_HEADS * RET_V

    def body(o_ref, g_ref, x_ref, gn_ref, g1_ref, w_ref, xo_ref, y_ref, out_ref):
        out = jnp.zeros((tm, d), F32)
        for h in range(RET_HEADS):
            cols = slice(h * RET_V, (h + 1) * RET_V)
            ohat, _ = _group_norm(o_ref[:, cols].astype(F32))
            g = g_ref[:, cols].astype(F32)
            yb = (g * jax.nn.sigmoid(g) * (ohat * gn_ref[:, cols])).astype(BF16)
            y_ref[:, cols] = yb
            out = out + _dot(yb, w_ref[cols, :])
        out_ref[...] = out
        m, _, _ = _rms_fwd(out, g1_ref[...])
        xo_ref[...] = x_ref[...] + m

    row = pl.BlockSpec((tm, d), lambda i: (i, 0))
    wide = pl.BlockSpec((tm, v_all), lambda i: (i, 0))
    return pl.pallas_call(
        body, grid=(nt,),
        in_specs=[wide, pl.BlockSpec((tm, v_all), lambda i: (i, 2)), row, _full((1, v_all)), _full((1, d)),
                  _full(w_out.shape)],
        out_specs=[row, wide, row],
        out_shape=[jax.ShapeDtypeStruct((t, d), F32), jax.ShapeDtypeStruct((t, v_all), BF16),
                   jax.ShapeDtypeStruct((t, d), F32)],
        compiler_params=_params(1), name="ret_out_fwd")(o, pj, x, gn_gain, g_post, w_out)


def ret_out_bwd(dxo, out, o, pj, gn_gain, g_post, w_out, tm=512):
    t, d = out.shape
    nt = t // tm
    v_all = RET_HEADS * RET_V

    def body(dxo_ref, out_ref, o_ref, g_ref, gn_ref, g1_ref, w_ref,
             dout_ref, dgate_ref, do_ref, dg1_ref, dgn_ref):
        @pl.when(pl.program_id(0) == 0)
        def _():
            dg1_ref[...] = jnp.zeros_like(dg1_ref)
            dgn_ref[...] = jnp.zeros_like(dgn_ref)

        g1 = g1_ref[...]
        dxo = dxo_ref[...]
        _, oh_, r_ = _rms_fwd(out_ref[...], g1)
        dg1_ref[...] += _colsum(dxo * oh_)
        doutb = _rms_bwd(oh_, r_, g1, dxo).astype(BF16)
        dout_ref[...] = doutb
        for h in range(RET_HEADS):
            cols = slice(h * RET_V, (h + 1) * RET_V)
            gn = gn_ref[:, cols]
            ohat, rstd = _group_norm(o_ref[:, cols].astype(F32))
            g = g_ref[:, cols].astype(F32)
            sg = jax.nn.sigmoid(g)
            dyh = _dot_nt(doutb, w_ref[cols, :])
            sil = g * sg
            dgate_ref[:, cols] = (dyh * (ohat * gn) * (sg + sil * (1.0 - sg))).astype(BF16)
            don = dyh * sil
            dgn_ref[:, cols] += _colsum(don * ohat)
            dohat = don * gn
            do_ref[:, cols] = (rstd * (dohat - jnp.mean(dohat, axis=-1, keepdims=True)
                                       - ohat * jnp.mean(dohat * ohat, axis=-1, keepdims=True))).astype(BF16)

    row = pl.BlockSpec((tm, d), lambda i: (i, 0))
    wide = pl.BlockSpec((tm, v_all), lambda i: (i, 0))
    gate = pl.BlockSpec((tm, v_all), lambda i: (i, 2))
    return pl.pallas_call(
        body, grid=(nt,),
        in_specs=[row, row, wide, gate, _full((1, v_all)), _full((1, d)), _full(w_out.shape)],
        out_specs=[row, gate, wide, _full((1, d)), _full((1, v_all))],
        out_shape=[jax.ShapeDtypeStruct((t, d), BF16), jax.ShapeDtypeStruct((t, RET_IN), BF16),
                   jax.ShapeDtypeStruct((t, v_all), BF16), jax.ShapeDtypeStruct((1, d), F32),
                   jax.ShapeDtypeStruct((1, v_all), F32)],
        compiler_params=_params(1), name="ret_out_bwd")(dxo, out, o, pj, gn_gain, g_post, w_out)


def ret_core_bwd(pj, do, sprev, cos, sin, dpj, intra, cross, inner, chunk_decay):
    t = pj.shape[0]
    nc = t // RET_CHUNK
    c = RET_CHUNK
    per = RET_STEP_CHUNKS
    qk_all = RET_HEADS * RET_QK
    v_all = RET_HEADS * RET_V
    scale = RET_QK ** -0.5

    def body(q_ref, k_ref, v_ref, do_ref, sp_ref, c_ref, s_ref, dm_ref, cr_ref, in_ref, dpj_in, dpj_ref, dstate):
        @pl.when(pl.program_id(0) == 0)
        def _():
            dstate[...] = jnp.zeros_like(dstate)

        for h in range(RET_HEADS):
            for cc in reversed(range(per)):
                rows = slice(cc * c, (cc + 1) * c)
                cs = c_ref[rows, :]
                sn = s_ref[rows, :]
                q = q_ref[rows, h * RET_QK:(h + 1) * RET_QK]
                k = k_ref[rows, h * RET_QK:(h + 1) * RET_QK]
                v = v_ref[rows, h * RET_V:(h + 1) * RET_V]
                doh = do_ref[rows, h * RET_V:(h + 1) * RET_V]
                dm = dm_ref[h]
                ab = (_dot_nt(q, k) * dm).astype(BF16)
                dab = (_dot_nt(doh, v) * dm).astype(BF16)
                dsb = dstate[h].astype(BF16)
                kd = (k.astype(F32) * in_ref[h]).astype(BF16)
                dv = _dot_tn(ab, doh) + _dot(kd, dsb)
                dq = _dot(dab, k) + cr_ref[h] * _dot_nt(doh, sp_ref[cc, h])
                dk = _dot_tn(dab, q) + in_ref[h] * _dot_nt(v, dsb)
                qd = (q.astype(F32) * cr_ref[h]).astype(BF16)
                dstate[h] = dstate[h] * chunk_decay[h] + _dot_tn(qd, doh)
                for base, dd, sc in ((h * RET_QK, dq, scale), (qk_all + h * RET_QK, dk, 1.0)):
                    d1, d2 = dd[:, :128], dd[:, 128:]
                    dpj_ref[rows, base:base + 128] = ((d1 * cs + d2 * sn) * sc).astype(BF16)
                    dpj_ref[rows, base + 128:base + RET_QK] = ((d2 * cs - d1 * sn) * sc).astype(BF16)
                dpj_ref[rows, 2 * qk_all + h * RET_V:2 * qk_all + (h + 1) * RET_V] = dv.astype(BF16)

    rev = lambda n: nc // per - 1 - n
    tab = pl.BlockSpec((per * c, 128), lambda n: (rev(n), 0))
    return pl.pallas_call(
        body, grid=(nc // per,),
        in_specs=[pl.BlockSpec((per * c, qk_all), lambda n: (rev(n), 0)),
                  pl.BlockSpec((per * c, qk_all), lambda n: (rev(n), 1)),
                  pl.BlockSpec((per * c, v_all), lambda n: (rev(n), 1)),
                  pl.BlockSpec((per * c, v_all), lambda n: (rev(n), 0)),
                  pl.BlockSpec((per, RET_HEADS, RET_QK, RET_V), lambda n: (rev(n), 0, 0, 0)),
                  tab, tab, _full(intra.shape), _full(cross.shape), _full(inner.shape), ANY],
        out_specs=pl.BlockSpec((per * c, 2 * qk_all + v_all), lambda n: (rev(n), 0)),
        out_shape=jax.ShapeDtypeStruct((t, RET_IN), BF16),
        scratch_shapes=[pltpu.VMEM((RET_HEADS, RET_QK, RET_V), F32)],
        input_output_aliases={10: 0},
        compiler_params=_params(1), name="ret_core_bwd")(pj, pj, pj, do, sprev, cos, sin, intra, cross, inner, dpj)


def ret_in_bwd(dpj, dres, x, g_pre, w_in, tm=512):
    t, d = x.shape
    nt = t // tm

    def body(dpj_ref, dres_ref, x_ref, g_ref, win_hbm, dx_ref, dg_ref, win_v, sem):
        @pl.when(pl.program_id(0) == 0)
        def _():
            cp = pltpu.make_async_copy(win_hbm, win_v, sem)
            cp.start()
            dg_ref[...] = jnp.zeros_like(dg_ref)
            cp.wait()

        g = g_ref[...]
        dh = jnp.zeros((tm, d), F32)
        for j in range(N_SHARD):
            dh = dh + _dot_nt(dpj_ref[:, j * RET_IN_SHARD:(j + 1) * RET_IN_SHARD], win_v[j])
        _, xh, rx = _rms_fwd(x_ref[...], g)
        dg_ref[...] += _colsum(dh * xh)
        dx_ref[...] = dres_ref[...] + _rms_bwd(xh, rx, g, dh)

    row = pl.BlockSpec((tm, d), lambda i: (i, 0))
    return pl.pallas_call(
        body, grid=(nt,),
        in_specs=[pl.BlockSpec((tm, RET_IN), lambda i: (i, 0)), row, row, _full((1, d)), ANY],
        out_specs=[row, _full((1, d))],
        out_shape=[jax.ShapeDtypeStruct((t, d), F32), jax.ShapeDtypeStruct((1, d), F32)],
        scratch_shapes=[pltpu.VMEM(w_in.shape, BF16), pltpu.SemaphoreType.DMA],
        compiler_params=_params(1), name="ret_in_bwd")(dpj, dres, x, g_pre, w_in)


_CHIP_FLIPS = ((1, 0), (0, 1), (1, 1))


def _flip(v, b):
    return 1 - v if b else v


_HBM = pl.BlockSpec(memory_space=pltpu.HBM)
_SEM = pl.BlockSpec(memory_space=pltpu.SEMAPHORE)
_EFFECT = pltpu.SideEffectType.DATAFLOW_SIDE_EFFECTING


def _chip_copies(mode, srcs, lands, send_sems, recv_sems):
    x, y, c = lax.axis_index("x"), lax.axis_index("y"), lax.axis_index("c")
    copies = []
    for t in range(len(lands)):
        if mode == "swap":
            copies.append(pltpu.make_async_remote_copy(
                src_ref=srcs[t], dst_ref=lands[t], send_sem=send_sems.at[t], recv_sem=recv_sems.at[t],
                device_id=(x, y, 1 - c), device_id_type=MESH))
            continue
        if mode == "everyone":
            for m in range(1, 8):
                bx, by, bc = (m >> 2) & 1, (m >> 1) & 1, m & 1
                copies.append(pltpu.make_async_remote_copy(
                    src_ref=srcs[t], dst_ref=lands[t].at[4 * x + 2 * y + c], send_sem=send_sems.at[7 * t + m - 1],
                    recv_sem=recv_sems.at[7 * t + m - 1], device_id=(_flip(x, bx), _flip(y, by), _flip(c, bc)),
                    device_id_type=MESH))
            continue
        for k, (bx, by) in enumerate(_CHIP_FLIPS):
            px, py = _flip(x, bx), _flip(y, by)
            target = (px, py, c)
            if mode == "gather":
                src, dst = srcs[t], lands[t].at[2 * x + y]
            elif mode == "gather_half":
                half = pl.ds(c * (srcs[t].shape[0] // 2), srcs[t].shape[0] // 2)
                src, dst = srcs[t].at[half], lands[t].at[2 * x + y, half]
            elif mode == "forward_half":
                half = pl.ds(c * (lands[t].shape[1] // 2), lands[t].shape[1] // 2)
                src = dst = lands[t].at[2 * px + py, half]
                target = (x, y, 1 - c)
            else:
                src, dst = srcs[t].at[2 * px + py], lands[t].at[k]
            copies.append(pltpu.make_async_remote_copy(
                src_ref=src, dst_ref=dst, send_sem=send_sems.at[3 * t + k], recv_sem=recv_sems.at[3 * t + k],
                device_id=target, device_id_type=MESH))
    return copies


def exchange_start(mode, srcs, lands, name, after=None):
    n, ns = len(lands), len(srcs)
    extra = [] if after is None else [after]

    def body(*refs):
        ins, lnd = refs[:ns], refs[ns:ns + n]
        send_sems, recv_sems = refs[ns + n + len(extra)], refs[ns + n + len(extra) + 1]
        token = refs[-1]
        for cp in _chip_copies(mode, ins, lnd, send_sems, recv_sems):
            cp.start()
        token[...] = jnp.zeros(token.shape, token.dtype)

    hbm = lambda a: pltpu.with_memory_space_constraint(a, pltpu.HBM)
    passed = list(srcs) + list(lands)
    n_sem = {"swap": 1, "everyone": 7}.get(mode, 3) * n
    return pl.pallas_call(
        body, name=name,
        out_shape=(pltpu.SemaphoreType.DMA((n_sem,)), pltpu.SemaphoreType.DMA((n_sem,)),
                   *[pltpu.HBM(a.shape, a.dtype) for a in passed], jax.ShapeDtypeStruct((8, 128), F32)),
        in_specs=[_HBM] * (ns + n) + [ANY] * len(extra),
        out_specs=(_SEM, _SEM, *[_HBM] * (ns + n), pl.BlockSpec(memory_space=pltpu.VMEM)),
        input_output_aliases={i: 2 + i for i in range(ns + n)},
        compiler_params=pltpu.CompilerParams(has_side_effects=_EFFECT))(*[hbm(a) for a in passed], *extra)


def exchange_wait(mode, started, after, name):
    send_sems, recv_sems = started[0], started[1]
    passed = list(started[2:-1])
    n = len(passed) if mode == "forward_half" else len(passed) // 2
    ns = len(passed) - n
    after = list(after) if isinstance(after, (list, tuple)) else [after]

    def body(*refs):
        ins, lnd = refs[:ns], refs[ns:ns + n]
        for cp in _chip_copies(mode, ins, lnd, refs[ns + n], refs[ns + n + 1]):
            cp.wait_send()
            cp.wait_recv()

    outs = pl.pallas_call(
        body, name=name, out_shape=tuple(pltpu.HBM(a.shape, a.dtype) for a in passed),
        in_specs=[_HBM] * (ns + n) + [_SEM, _SEM] + [ANY] * len(after), out_specs=tuple([_HBM] * (ns + n)),
        input_output_aliases={i: i for i in range(ns + n)},
        compiler_params=pltpu.CompilerParams(has_side_effects=_EFFECT))(*passed, send_sems, recv_sems, *after)
    return list(outs[:ns]), list(outs[ns:])


def plane_sum(slot, full, recv, name, bm=256):
    _, m, n = full.shape
    bm = _row_block(m, bm)

    def body(slot_ref, o_ref, r_ref, s_ref):
        s_ref[...] = ((o_ref[...] + r_ref[0].astype(F32)) + r_ref[1].astype(F32)) + r_ref[2].astype(F32)

    return pl.pallas_call(
        body,
        grid_spec=pltpu.PrefetchScalarGridSpec(
            num_scalar_prefetch=1, grid=(m // bm,),
            in_specs=[pl.BlockSpec((None, bm, n), lambda i, s: (s[0], i, 0)),
                      pl.BlockSpec((3, bm, n), lambda i, s: (0, i, 0))],
            out_specs=pl.BlockSpec((bm, n), lambda i, s: (i, 0))),
        out_shape=jax.ShapeDtypeStruct((m, n), F32), compiler_params=_params(1), name=name)(slot, full, recv)


def sum_slots(parts, name, bm=312):
    _, r, n = parts.shape
    bm = bm if r % bm == 0 else r

    def body(p_ref, s_ref):
        acc = p_ref[0]
        for k in range(1, 8):
            acc = acc + p_ref[k]
        s_ref[...] = acc

    return pl.pallas_call(
        body, grid=(r // bm,), in_specs=[pl.BlockSpec((8, bm, n), lambda i: (0, i, 0))],
        out_specs=pl.BlockSpec((bm, n), lambda i: (i, 0)), out_shape=jax.ShapeDtypeStruct((r, n), F32),
        compiler_params=_params(1), name=name)(parts)


def _adamw_math(w, g, m, v):
    m = ADAM_B1 * m + (1.0 - ADAM_B1) * g
    v = ADAM_B2 * v + (1.0 - ADAM_B2) * (g * g)
    m_hat = m / (1.0 - ADAM_B1 ** ADAM_STEP)
    v_hat = v / (1.0 - ADAM_B2 ** ADAM_STEP)
    delta = -ADAM_LR * (m_hat / (jnp.sqrt(v_hat) + ADAM_EPS) + ADAM_WD * w)
    return delta, m, v


def adamw(w, m, v, grads, layer, prev, name, bm=256, row0=0):
    _, _, n = w.shape
    mm = grads[0].shape[0]
    bm = _row_block(mm, bm)
    first = row0 // bm
    ng = len(grads)

    def body(*refs):
        w_ref, m_ref, v_ref = refs[:3]
        g_refs = refs[3:3 + ng]
        g_out, d_out, m_out, v_out = refs[-4:]
        g = g_refs[0][...]
        for gr in g_refs[1:]:
            g = g + gr[...]
        delta, mn, vn = _adamw_math(w_ref[...], g, m_ref[...], v_ref[...])
        g_out[...] = g
        d_out[...] = delta
        m_out[...] = mn
        v_out[...] = vn

    slab = pl.BlockSpec((None, bm, n), lambda i: (layer, first + i, 0))
    flat = pl.BlockSpec((bm, n), lambda i: (i, 0))
    in_specs = [slab] * 3 + [flat] * ng
    args = [w, m, v, *grads]
    aliases = {}
    if prev is not None:
        in_specs += [ANY] * 4
        aliases = {3 + ng + q: q for q in range(4)}
        args += list(prev)
    return pl.pallas_call(
        body, grid=(mm // bm,), in_specs=in_specs, out_specs=[slab] * 4,
        out_shape=[jax.ShapeDtypeStruct(w.shape, F32)] * 4, input_output_aliases=aliases,
        compiler_params=_params(1), name=name)(*args)


def _pack_rows(parts, rows):
    flat = jnp.concatenate([p.reshape(-1) for p in parts])
    return jnp.pad(flat, (0, rows * 128 - flat.shape[0])).reshape(rows, 128)


def _as_shards(a, rows):
    return a.reshape(N_SHARD, rows, a.shape[-1])


def _local_step(x, pos_col, target, gains, pool_w, pool_scale, gn_gain, conv_w, conv_b, weights, send_grads):
    def gain(l, n, token=None):
        g = gains[l, n].reshape(1, D_MODEL)
        return g if token is None else g + token[0:1, 0:1]

    inv_freq = (ROPE_BASE ** (-jnp.arange(0, RET_QK, 2, dtype=F32) / RET_QK)).reshape(1, RET_QK // 2)
    intra, cross, inner, chunk_decay = _decay_tables()
    dn_rows = D_FF // N_SHARD

    x1 = pool_fwd(x, gain(0, 0), gain(0, 1), pool_w, pool_scale)
    cos, sin = rope_tables(pos_col, inv_freq, x1)
    w_up0, w_dn0 = weights("mlp0", cos)
    w_dn0 = w_dn0.reshape(D_FF, D_MODEL)
    x2, u0, s0, a0, f0, h0 = mlp_fwd(x1, gain(0, 2), gain(0, 3), w_up0, w_dn0, conv_w[0], conv_b[0])
    w_in, w_out = weights("ret", x2)
    w_out = w_out.reshape(RET_HEADS * RET_V, D_MODEL)
    pj, hr = ret_proj(x2, gain(1, 0), w_in, cos, sin)
    o, sprev = ret_core_fwd(pj, intra, cross, inner, chunk_decay)
    x3, yb, out = ret_out_fwd(o, pj, x2, gn_gain, gain(1, 1), w_out)
    w_up1, w_dn1 = weights("mlp1", x3)
    w_dn1 = w_dn1.reshape(D_FF, D_MODEL)
    dx4, u1, s1, a1, f1, h1, loss = mlp_fwd(x3, gain(1, 2), gain(1, 3), w_up1, w_dn1, conv_w[1], conv_b[1], target)

    dx3, du1, df1, dg12, dg13, dcw1, dcb1 = mlp_bwd(
        dx4, f1, x3, u1, s1, gain(1, 2), gain(1, 3), w_up1, w_dn1, conv_w[1])
    dwup1 = grad_matmul(h1, du1, D_MODEL, FF_CHUNK, "grad_w_up_1", interleaved=True)
    dwdn1 = grad_matmul(a1, df1, FF_CHUNK, D_MODEL, "grad_w_down_1")
    tok = send_grads("mlp1", [dwup1, [_as_shards(g, dn_rows) for g in dwdn1]])
    dout, dpj, do, dg11, dgn = ret_out_bwd(dx3, out, o, pj, gn_gain, gain(1, 1, tok), w_out)
    dwout = grad_matmul(yb, dout, 1024, D_MODEL, "grad_w_out")
    dpj = ret_core_bwd(pj, do, sprev, cos, sin, dpj, intra, cross, inner, chunk_decay)
    dwin = grad_matmul(hr, dpj, D_MODEL, RET_IN_SHARD, "grad_w_in")
    tok = send_grads("ret", [dwin, [_as_shards(g, RET_V) for g in dwout]])
    dx2, dg10 = ret_in_bwd(dpj, dx3, x2, gain(1, 0, tok), w_in)
    dx1, du0, df0, dg02, dg03, dcw0, dcb0 = mlp_bwd(
        dx2, f0, x1, u0, s0, gain(0, 2), gain(0, 3), w_up0, w_dn0, conv_w[0])
    dwdn0 = grad_matmul(a0, df0, FF_CHUNK, D_MODEL, "grad_w_down_0")
    tok = send_grads("down0", [[_as_shards(g, dn_rows) for g in dwdn0]])
    half = D_MODEL // 2
    for part, first in (("a", 0), ("b", half)):
        dwup0 = grad_matmul(h0, du0, half, FF_CHUNK, "grad_w_up_0" + part, tk=4096, interleaved=True, after=tok,
                            cols=(first, half))
        tok = send_grads("up0" + part, [dwup0])
    dx0, dg00, dg01, dpscale, dpw = pool_bwd(dx1, x, gain(0, 0, tok), gain(0, 1), pool_w, pool_scale)

    rows = lambda g: jnp.sum(g, axis=0, keepdims=True)
    dgains = jnp.concatenate([dg00, dg01, rows(dg02), rows(dg03), dg10, dg11, rows(dg12), rows(dg13)],
                             axis=0).reshape(2, 4, D_MODEL)
    small = {"gains": dgains, "pool_scale": dpscale, "gn": dgn,
             "conv_w": jnp.sum(jnp.stack([dcw0, dcw1]), axis=3),
             "conv_b": jnp.sum(jnp.stack([dcb0, dcb1]), axis=2, keepdims=True), "pool_w": dpw}
    return loss, dx0, small


def kernel(x, positions, norm_gain, pool_w, pool_scale, ret_w_in, ret_gn_gain, ret_w_out, mlp_w_up, mlp_conv_w, mlp_conv_b, mlp_w_down, loss_target, m_norm_gain, m_pool_w, m_pool_scale, m_ret_w_in, m_ret_gn_gain, m_ret_w_out, m_mlp_w_up, m_mlp_conv_w, m_mlp_conv_b, m_mlp_w_down, v_norm_gain, v_pool_w, v_pool_scale, v_ret_w_in, v_ret_gn_gain, v_ret_w_out, v_mlp_w_up, v_mlp_conv_w, v_mlp_conv_b, v_mlp_w_down):
    t = x.shape[1]
    me = 2 * lax.axis_index("x") + lax.axis_index("y")
    me_slot = jnp.reshape(me, (1,)).astype(jnp.int32)

    small_parts = [norm_gain, ret_gn_gain, mlp_conv_w, pool_w]
    small_sizes = [p.size for p in small_parts]
    small_rows = -(-sum(small_sizes) // (128 * 8)) * 8
    gathers = {}

    def start_gather(group, srcs, after):
        lands = [lax.dynamic_update_index_in_dim(lax.empty((N_SHARD,) + s.shape, s.dtype), s, me, 0) for s in srcs]
        mode = "gather_half" if group == "mlp0" else "gather"
        gathers[group] = (mode, exchange_start(mode, srcs, lands, "gather_start_" + group, after=after))
        return gathers[group][1][-1]

    token = start_gather("small", [_pack_rows(small_parts, small_rows)], None)
    token = start_gather("mlp0", [mlp_w_up[0].astype(BF16), mlp_w_down[0].astype(BF16)], token)

    def weights(group, after):
        if group == "mlp0":
            tok = start_gather("ret", [ret_w_in[0].astype(BF16), ret_w_out[0].astype(BF16)], after)
            after = start_gather("mlp1", [mlp_w_up[1].astype(BF16), mlp_w_down[1].astype(BF16)], tok)
        mode, started = gathers[group]
        _, lands = exchange_wait(mode, started, after, "gather_wait_" + group)
        if mode == "gather_half":
            forward = exchange_start("forward_half", [], lands, "forward_start_" + group)
            _, lands = exchange_wait("forward_half", forward, forward[-1], "forward_wait_" + group)
        return lands

    sent, early = {}, {}

    def reduced(group, after, names):
        started, own = sent[group]
        _, recv = exchange_wait("scatter", started, after, "scatter_wait_" + group)
        return [plane_sum(me_slot, f, r, "plane_sum_" + nm)
                for f, r, nm in zip(own, recv, names)]

    def swap_start(planes, name):
        return exchange_start("swap", planes, [lax.empty(p.shape, p.dtype) for p in planes], name)

    def send_grads(group, pairs):
        lands = [lax.empty((3,) + b.shape[1:], BF16) for _, b in pairs]
        sent[group] = (exchange_start("scatter", [b for _, b in pairs], lands, "scatter_start_" + group),
                       [f for f, _ in pairs])
        token = sent[group][0][-1]
        if group == "down0":
            marker = pairs[0][1]
            early["planes"] = (reduced("mlp1", marker, ["w_up_1", "w_down_1"])
                               + reduced("ret", marker, ["w_in", "w_out"]))
            early["swap"] = swap_start(early["planes"], "swap_start_a")
            token = token + early["swap"][-1]
        return token

    (smallg,) = weights("small", token)
    smallg = smallg.reshape(N_SHARD, -1)
    offs = np.cumsum([0] + small_sizes)
    piece = lambda i, shape: smallg[:, offs[i]:offs[i + 1]].reshape((N_SHARD,) + shape)
    gains = piece(0, (2, 4, 256)).transpose(1, 2, 0, 3).reshape(2, 4, D_MODEL)
    gn_full = piece(1, (512,)).reshape(1, RET_HEADS * RET_V)
    cw_full = piece(2, (2, 3, FF_CHUNK)).transpose(1, 0, 2, 3)
    pw_full = piece(3, (4, 64, 256)).transpose(1, 0, 2, 3).reshape(4, 256, 256).astype(BF16)
    cb_full = mlp_conv_b.reshape(2, N_SHARD, 1, FF_CHUNK)

    loss, dx0, small = _local_step(
        x[0], positions.reshape(t, 1).astype(F32), loss_target[0], gains, pw_full, pool_scale, gn_full,
        cw_full, cb_full, weights, send_grads)

    def small_adamw(w, m, v, grads, name):
        w3 = w.reshape(1, -1, w.shape[-1])
        out = adamw(w3, m.reshape(w3.shape), v.reshape(w3.shape), [g.reshape(w3.shape[1:]) for g in grads], 0, None, name)
        return [o.reshape(w.shape) for o in out]

    pw_f = small["pool_w"].reshape(4, N_SHARD, 64, 256).transpose(1, 0, 2, 3).reshape(N_SHARD, 256, 256)
    small = dict(small, loss=loss)
    small_order = ["gains", "pool_scale", "gn", "conv_w", "conv_b", "loss"]
    gsmall_sizes = [small[k].size for k in small_order]
    gsmall_rows = -(-sum(gsmall_sizes) // (128 * 8)) * 8
    gpack = _pack_rows([small[k] for k in small_order], gsmall_rows)
    mine = 2 * me + lax.axis_index("c")
    small_started = exchange_start(
        "everyone", [gpack], [lax.dynamic_update_index_in_dim(lax.empty((8,) + gpack.shape, F32), gpack, mine, 0)],
        "small_start")
    send_grads("pool_w", [(pw_f, pw_f.astype(BF16))])

    res = {}
    planes_a, others_a = exchange_wait("swap", early["swap"], small_started[-1], "swap_wait_a")
    res["ret_w_in"] = adamw(ret_w_in, m_ret_w_in, v_ret_w_in, (planes_a[2], others_a[2]), 0, None, "adamw_w_in")
    res["ret_w_out"] = adamw(ret_w_out, m_ret_w_out, v_ret_w_out, (planes_a[3], others_a[3]), 0, None, "adamw_w_out")
    up1 = adamw(mlp_w_up, m_mlp_w_up, v_mlp_w_up, (planes_a[0], others_a[0]), 1, None, "adamw_w_up_1")
    dn1 = adamw(mlp_w_down, m_mlp_w_down, v_mlp_w_down, (planes_a[1], others_a[1]), 1, None, "adamw_w_down_1")

    done_a = [res["ret_w_in"][0], res["ret_w_out"][0], up1[0], dn1[0]]
    planes_b = (reduced("up0a", done_a, ["w_up_0a"]) + reduced("up0b", done_a, ["w_up_0b"])
                + reduced("down0", done_a, ["w_down_0"]) + reduced("pool_w", done_a, ["pool_w"]))
    swap_b = swap_start(planes_b, "swap_start_b")

    _, (small_recv,) = exchange_wait("everyone", small_started, swap_b[-1], "small_wait")
    gsmall = sum_slots(small_recv, "sum_small").reshape(-1)
    goffs = np.cumsum([0] + gsmall_sizes)
    gpiece = lambda i: gsmall[goffs[i]:goffs[i + 1]].reshape(small[small_order[i]].shape)
    g_gains = lax.dynamic_slice_in_dim(gpiece(0), me * 256, 256, axis=2)
    g_gn = lax.dynamic_slice_in_dim(gpiece(2), me * RET_V, RET_V, axis=1)
    g_cw = lax.dynamic_index_in_dim(gpiece(3), me, 1, keepdims=False)
    res["norm_gain"] = small_adamw(norm_gain, m_norm_gain, v_norm_gain, [g_gains], "adamw_norm_gain")
    res["pool_scale"] = small_adamw(pool_scale, m_pool_scale, v_pool_scale, [gpiece(1)], "adamw_pool_scale")
    res["ret_gn_gain"] = small_adamw(ret_gn_gain, m_ret_gn_gain, v_ret_gn_gain, [g_gn], "adamw_gn_gain")
    res["mlp_conv_w"] = small_adamw(mlp_conv_w, m_mlp_conv_w, v_mlp_conv_w, [g_cw], "adamw_conv_w")
    res["mlp_conv_b"] = small_adamw(mlp_conv_b, m_mlp_conv_b, v_mlp_conv_b, [gpiece(4)], "adamw_conv_b")

    planes_b, others_b = exchange_wait("swap", swap_b, res["mlp_conv_b"][0], "swap_wait_b")
    up0a = adamw(mlp_w_up, m_mlp_w_up, v_mlp_w_up, (planes_b[0], others_b[0]), 0, up1, "adamw_w_up_0a")
    res["mlp_w_up"] = adamw(mlp_w_up, m_mlp_w_up, v_mlp_w_up, (planes_b[1], others_b[1]), 0, up0a, "adamw_w_up_0b",
                            row0=D_MODEL // 2)
    res["mlp_w_down"] = adamw(mlp_w_down, m_mlp_w_down, v_mlp_w_down, (planes_b[2], others_b[2]), 0, dn1,
                              "adamw_w_down_0")
    res["pool_w"] = small_adamw(pool_w, m_pool_w, v_pool_w, (planes_b[3], others_b[3]), "adamw_pool_w")

    order = ["norm_gain", "pool_w", "pool_scale", "ret_w_in", "ret_gn_gain", "ret_w_out", "mlp_w_up", "mlp_conv_w",
             "mlp_conv_b", "mlp_w_down"]
    outs = [gpiece(5)[0, 0], dx0.reshape(x.shape)]
    for q in range(4):
        outs += [res[k][q] for k in order]
    return tuple(outs)
```

```python
import numpy as np
import jax
import jax.numpy as jnp
from jax import lax
from jax.experimental import pallas as pl
from jax.experimental.pallas import tpu as pltpu

F32 = jnp.float32
BF16 = jnp.bfloat16

D_MODEL = 1024
D_FF = 2816
FF_CHUNK = 1408
N_SHARD = 4
POOL_WINDOWS = (2, 4, 8, 16)
POOL_DIM = 256
POOL_HALO = 16
RET_HEADS = 4
RET_QK = 256
RET_V = 512
RET_CHUNK = 256
RET_STEP_CHUNKS = 2
RET_IN = 6144
RET_IN_SHARD = 1536
ROPE_BASE = 10000.0
EPS = 1e-6
CONV_HALO = 8

ADAM_LR, ADAM_B1, ADAM_B2, ADAM_EPS, ADAM_WD, ADAM_STEP = 0.001, 0.9, 0.999, 1e-08, 0.01, 10

VMEM_LIMIT = 56 * 1024 * 1024
VMEM_LIMIT_MLP_BWD = 62 * 1024 * 1024
MESH = pl.DeviceIdType.MESH
ANY = pl.BlockSpec(memory_space=pl.ANY)


def _params(n_grid=1, limit=VMEM_LIMIT):
    return pltpu.CompilerParams(dimension_semantics=("arbitrary",) * n_grid, vmem_limit_bytes=limit)


def _dot(a, b):
    return jnp.dot(a, b, preferred_element_type=F32)


def _dot_nt(a, b):
    return lax.dot_general(a, b, (((1,), (1,)), ((), ())), preferred_element_type=F32)


def _dot_tn(a, b):
    return lax.dot_general(a, b, (((0,), (0,)), ((), ())), preferred_element_type=F32)


def _rms_fwd(x, gain):
    r = lax.rsqrt(jnp.mean(x * x, axis=-1, keepdims=True) + EPS)
    xh = x * r
    return xh * gain, xh, r


def _rms_bwd(xh, r, gain, dy):
    dxh = dy * gain
    return r * (dxh - xh * jnp.mean(dxh * xh, axis=-1, keepdims=True))


def _colsum(v):
    return jnp.sum(v, axis=0, keepdims=True)


def _full(shape):
    nd = len(shape)
    return pl.BlockSpec(shape, lambda *_: (0,) * nd)


def rope_tables(pos_col, inv_freq, after):
    t = pos_col.shape[0]
    tm = min(t, 1024)

    def body(p_ref, f_ref, after_ref, c_ref, s_ref):
        ang = p_ref[...] * f_ref[...]
        c_ref[...] = jnp.cos(ang)
        s_ref[...] = jnp.sin(ang)

    return pl.pallas_call(
        body, grid=(t // tm,),
        in_specs=[pl.BlockSpec((tm, 1), lambda i: (i, 0)), _full((1, 128)), ANY],
        out_specs=[pl.BlockSpec((tm, 128), lambda i: (i, 0))] * 2,
        out_shape=[jax.ShapeDtypeStruct((t, 128), F32)] * 2,
        compiler_params=_params(1), name="rope_tables")(pos_col, inv_freq, after)


def _window_sums(ext, backward):
    n = ext.shape[0]
    cur, sums = ext, []
    for g, win in enumerate(POOL_WINDOWS):
        if g > 0:
            cur = cur[:, POOL_DIM:]
        half = win // 2
        cur = cur + pltpu.roll(cur, n - half if backward else half, axis=0)
        sums.append(cur[:, 0:POOL_DIM])
    return sums


def _pool_diff(h_halo, h, row0, tm):
    t_idx = row0 + lax.broadcasted_iota(jnp.int32, (tm, 1), 0)
    sums = _window_sums(jnp.concatenate([h_halo, h], axis=0), backward=False)
    parts, inv_counts = [], []
    for g, win in enumerate(POOL_WINDOWS):
        inv = 1.0 / jnp.minimum(t_idx + 1, win).astype(F32)
        parts.append(sums[g][POOL_HALO:, :] * inv - h[:, g * POOL_DIM:(g + 1) * POOL_DIM])
        inv_counts.append(inv)
    return parts, inv_counts


def pool_fwd(x, g_pre, g_post, pool_w, pool_scale, tm=512):
    t, d = x.shape
    nt = t // tm

    def body(x_ref, g0_ref, g1_ref, w_ref, sc_ref, o_ref, hext):
        i = pl.program_id(0)

        @pl.when(i == 0)
        def _():
            hext[...] = jnp.zeros((POOL_HALO, d), F32)

        xv = x_ref[...]
        h, _, _ = _rms_fwd(xv, g0_ref[...])
        parts, _ = _pool_diff(hext[...], h, i * tm, tm)
        hext[...] = h[tm - POOL_HALO:tm, :]
        ys = [_dot(parts[g].astype(BF16), w_ref[g]) for g in range(len(POOL_WINDOWS))]
        y = jnp.concatenate(ys, axis=-1) * sc_ref[...]
        m, _, _ = _rms_fwd(y, g1_ref[...])
        o_ref[...] = xv + m

    row = pl.BlockSpec((tm, d), lambda i: (i, 0))
    return pl.pallas_call(
        body, grid=(nt,),
        in_specs=[row, _full((1, d)), _full((1, d)), _full(pool_w.shape), _full((1, d))],
        out_specs=row, out_shape=jax.ShapeDtypeStruct((t, d), F32),
        scratch_shapes=[pltpu.VMEM((POOL_HALO, d), F32)],
        compiler_params=_params(1), name="pool_fwd")(x, g_pre, g_post, pool_w, pool_scale)


def pool_bwd(dx1, x, g_pre, g_post, pool_w, pool_scale, tm=512):
    t, d = x.shape
    nt = t // tm
    ng = len(POOL_WINDOWS)

    def body(dx1_ref, x_ref, xh_ref, g0_ref, g1_ref, w_ref, sc_ref,
             dx_ref, dg0_ref, dg1_ref, dsc_ref, dw_ref, enext):
        i = pl.program_id(0)
        r = nt - 1 - i

        @pl.when(i == 0)
        def _():
            enext[...] = jnp.zeros((POOL_HALO, d), F32)
            dg0_ref[...] = jnp.zeros_like(dg0_ref)
            dg1_ref[...] = jnp.zeros_like(dg1_ref)
            dsc_ref[...] = jnp.zeros_like(dsc_ref)
            dw_ref[...] = jnp.zeros_like(dw_ref)

        g0 = g0_ref[...]
        g1 = g1_ref[...]
        sc = sc_ref[...]
        xv = x_ref[...]
        h, xh, rx = _rms_fwd(xv, g0)
        h_halo, _, _ = _rms_fwd(xh_ref[...], g0)
        parts, inv_counts = _pool_diff(h_halo * jnp.where(r > 0, 1.0, 0.0), h, r * tm, tm)
        parts_b = [p.astype(BF16) for p in parts]
        ypre = jnp.concatenate([_dot(parts_b[g], w_ref[g]) for g in range(ng)], axis=-1)
        _, yh, ry = _rms_fwd(ypre * sc, g1)
        dm = dx1_ref[...]
        dg1_ref[...] += _colsum(dm * yh)
        dy = _rms_bwd(yh, ry, g1, dm)
        dsc_ref[...] += _colsum(dy * ypre)
        dyp = (dy * sc).astype(BF16)
        ddiffs = []
        for g in range(ng):
            cols = slice(g * POOL_DIM, (g + 1) * POOL_DIM)
            dw_ref[g] += _dot_tn(parts_b[g], dyp[:, cols])
            ddiffs.append(_dot_nt(dyp[:, cols], w_ref[g]))
        e = jnp.concatenate([ddiffs[g] * inv_counts[g] for g in range(ng)], axis=-1)
        sums = _window_sums(jnp.concatenate([e, enext[...]], axis=0), backward=True)
        enext[...] = e[0:POOL_HALO, :]
        dh = jnp.concatenate([sums[g][0:tm, :] - ddiffs[g] for g in range(ng)], axis=-1)
        dg0_ref[...] += _colsum(dh * xh)
        dx_ref[...] = dm + _rms_bwd(xh, rx, g0, dh)

    row = pl.BlockSpec((tm, d), lambda i: (nt - 1 - i, 0))
    halo = pl.BlockSpec((POOL_HALO, d), lambda i: (jnp.maximum((nt - 1 - i) * (tm // POOL_HALO) - 1, 0), 0))
    vec = _full((1, d))
    return pl.pallas_call(
        body, grid=(nt,),
        in_specs=[row, row, halo, vec, vec, _full(pool_w.shape), vec],
        out_specs=[row, vec, vec, vec, _full((ng, POOL_DIM, POOL_DIM))],
        out_shape=[jax.ShapeDtypeStruct((t, d), F32)] + [jax.ShapeDtypeStruct((1, d), F32)] * 3
        + [jax.ShapeDtypeStruct((ng, POOL_DIM, POOL_DIM), F32)],
        scratch_shapes=[pltpu.VMEM((POOL_HALO, d), F32)],
        compiler_params=_params(1), name="pool_bwd")(dx1, x, x, g_pre, g_post, pool_w, pool_scale)


def _conv_taps(cw_ref, j):
    return cw_ref[j, 0:1, :], cw_ref[j, 1:2, :], cw_ref[j, 2:3, :]


def _quarter_copies(pairs, sem):
    copies = []
    for n, (src, dst) in enumerate(pairs):
        rows = src.shape[0] // N_SHARD
        for j in range(N_SHARD):
            part = pl.ds(j * rows, rows)
            copies.append(pltpu.make_async_copy(src.at[part], dst.at[part], sem.at[N_SHARD * n + j]))
    return copies


def _row_block(m, target=256):
    if m <= target:
        return m
    for b in range(target, 7, -8):
        if m % b == 0:
            return b
    return m


def mlp_fwd(x, g_pre, g_post, w_up, w_down, conv_w, conv_b, target=None, tm=256):
    t, d = x.shape
    nt = t // tm
    h8 = CONV_HALO
    with_loss = target is not None
    n_extra = 1 if with_loss else 0

    def body(x_ref, g2_ref, g3_ref, wup_hbm, wdn_hbm, cw_ref, cb_ref, *rest):
        tgt_ref = rest[0] if with_loss else None
        xo_ref, u_ref, s_ref, a_ref, f_ref, h_ref = rest[n_extra:n_extra + 6]
        loss_ref = rest[n_extra + 6] if with_loss else None
        wup_v, wdn_v, tail, sem = rest[-4:]
        i = pl.program_id(0)

        @pl.when(i == 0)
        def _():
            copies = _quarter_copies([(wup_hbm, wup_v), (wdn_hbm, wdn_v)], sem)
            for c in copies:
                c.start()
            tail[...] = jnp.zeros_like(tail)
            if with_loss:
                loss_ref[...] = jnp.zeros_like(loss_ref)
            for c in copies:
                c.wait()

        xv = x_ref[...]
        h, _, _ = _rms_fwd(xv, g2_ref[...])
        hb = h.astype(BF16)
        h_ref[...] = hb
        acc = jnp.zeros((tm, d), F32)
        for k in range(2):
            cs = []
            for s in range(2):
                j, cols = k + 2 * s, slice((2 * k + s) * FF_CHUNK, (2 * k + s + 1) * FF_CHUNK)
                uf = _dot(hb, wup_v[j])
                u_ref[:, cols] = uf.astype(BF16)
                ext = jnp.concatenate([tail[j], uf], axis=0)
                tail[j] = uf[tm - h8:tm, :]
                w0, w1, w2 = _conv_taps(cw_ref, j)
                cs.append(cb_ref[j] + w2 * uf + w1 * pltpu.roll(ext, 1, axis=0)[h8:, :]
                          + w0 * pltpu.roll(ext, 2, axis=0)[h8:, :])
            cg, cv = cs
            sg = jax.nn.sigmoid(cg)
            sil = cg * sg
            ab = (sil * cv).astype(BF16)
            a_ref[:, k * FF_CHUNK:(k + 1) * FF_CHUNK] = ab
            s_ref[:, 2 * k * FF_CHUNK:(2 * k + 1) * FF_CHUNK] = sil.astype(BF16)
            s_ref[:, (2 * k + 1) * FF_CHUNK:(2 * k + 2) * FF_CHUNK] = (cv * (sg + sil * (1.0 - sg))).astype(BF16)
            acc = acc + _dot(ab, wdn_v[k * FF_CHUNK:(k + 1) * FF_CHUNK, :])
        f_ref[...] = acc
        y, _, _ = _rms_fwd(acc, g3_ref[...])
        if with_loss:
            err = (xv + y) - tgt_ref[...]
            xo_ref[...] = err * (1.0 / d)
            loss_ref[...] += 0.5 * jnp.sum(jnp.mean(err * err, axis=-1, keepdims=True), axis=0, keepdims=True)
        else:
            xo_ref[...] = xv + y

    row = pl.BlockSpec((tm, d), lambda i: (i, 0))
    wide = pl.BlockSpec((tm, 2 * D_FF), lambda i: (i, 0))
    vec = _full((1, d))
    extra = [target] if with_loss else []
    return pl.pallas_call(
        body, grid=(nt,),
        in_specs=[row, vec, vec, ANY, ANY, _full(conv_w.shape), _full(conv_b.shape)] + [row] * n_extra,
        out_specs=[row, wide, wide, pl.BlockSpec((tm, D_FF), lambda i: (i, 0)), row, row] + [_full((1, 1))] * n_extra,
        out_shape=[jax.ShapeDtypeStruct((t, d), F32), jax.ShapeDtypeStruct((t, 2 * D_FF), BF16),
                   jax.ShapeDtypeStruct((t, 2 * D_FF), BF16), jax.ShapeDtypeStruct((t, D_FF), BF16),
                   jax.ShapeDtypeStruct((t, d), F32), jax.ShapeDtypeStruct((t, d), BF16)]
        + [jax.ShapeDtypeStruct((1, 1), F32)] * n_extra,
        scratch_shapes=[pltpu.VMEM(w_up.shape, BF16), pltpu.VMEM(w_down.shape, BF16),
                        pltpu.VMEM((N_SHARD, h8, FF_CHUNK), F32), pltpu.SemaphoreType.DMA((2 * N_SHARD,))],
        compiler_params=_params(1), name="mlp_fwd_loss" if with_loss else "mlp_fwd")(
            x, g_pre, g_post, w_up, w_down, conv_w, conv_b, *extra)


def _rowsum8(v):
    return jnp.sum(v.reshape(v.shape[0] // 8, 8, v.shape[1]), axis=0)


def mlp_bwd(dxo, f, x, u, sp, g_pre, g_post, w_up, w_down, conv_w, tm=256):
    t, d = x.shape
    nt = t // tm
    h8 = CONV_HALO

    def body(dxo_ref, f_ref, x_ref, u_ref, s_ref, g2_ref, g3_ref, wup_hbm, wdn_hbm, cw_ref,
             dx_ref, du_ref, df_ref, dg2_ref, dg3_ref, dcw_ref, dcb_ref,
             wup_v, wdn_v, carry, sem):
        @pl.when(pl.program_id(0) == 0)
        def _():
            copies = _quarter_copies([(wup_hbm, wup_v), (wdn_hbm, wdn_v)], sem)
            for c in copies:
                c.start()
            carry[...] = jnp.zeros_like(carry)
            dg2_ref[...] = jnp.zeros_like(dg2_ref)
            dg3_ref[...] = jnp.zeros_like(dg3_ref)
            dcw_ref[...] = jnp.zeros_like(dcw_ref)
            dcb_ref[...] = jnp.zeros_like(dcb_ref)
            for c in copies:
                c.wait()

        g3 = g3_ref[...]
        dxo = dxo_ref[...]
        _, fh, rf = _rms_fwd(f_ref[...], g3)
        dg3_ref[...] += _rowsum8(dxo * fh)
        dfb = _rms_bwd(fh, rf, g3, dxo).astype(BF16)
        df_ref[...] = dfb
        dh = jnp.zeros((tm, d), F32)
        for k in range(2):
            da = _dot_nt(dfb, wdn_v[k * FF_CHUNK:(k + 1) * FF_CHUNK, :])
            for s in range(2):
                j = k + 2 * s
                cols = slice((2 * k + s) * FF_CHUNK, (2 * k + s + 1) * FF_CHUNK)
                dc = da * s_ref[:, (2 * k + 1 - s) * FF_CHUNK:(2 * k + 2 - s) * FF_CHUNK].astype(F32)
                uf = u_ref[:, cols].astype(F32)
                ext = jnp.concatenate([dc, carry[j]], axis=0)
                carry[j] = dc[0:h8, :]
                dc1 = pltpu.roll(ext, tm + h8 - 1, axis=0)[0:tm, :]
                dc2 = pltpu.roll(ext, tm + h8 - 2, axis=0)[0:tm, :]
                dcb_ref[j] += _rowsum8(dc)
                dcw_ref[j, 2] += _rowsum8(dc * uf)
                dcw_ref[j, 1] += _rowsum8(dc1 * uf)
                dcw_ref[j, 0] += _rowsum8(dc2 * uf)
                dub = (cw_ref[j, 2:3, :] * dc + cw_ref[j, 1:2, :] * dc1 + cw_ref[j, 0:1, :] * dc2).astype(BF16)
                du_ref[:, cols] = dub
                dh = dh + _dot_nt(dub, wup_v[j])
        g2 = g2_ref[...]
        _, xh, rx = _rms_fwd(x_ref[...], g2)
        dg2_ref[...] += _rowsum8(dh * xh)
        dx_ref[...] = dxo + _rms_bwd(xh, rx, g2, dh)

    row = pl.BlockSpec((tm, d), lambda i: (nt - 1 - i, 0))
    wide = pl.BlockSpec((tm, 2 * D_FF), lambda i: (nt - 1 - i, 0))
    vec = _full((1, d))
    acc = _full((8, d))
    dcw_shape, dcb_shape = (N_SHARD, 3, 8, FF_CHUNK), (N_SHARD, 8, FF_CHUNK)
    return pl.pallas_call(
        body, grid=(nt,),
        in_specs=[row, row, row, wide, wide, vec, vec, ANY, ANY, _full(conv_w.shape)],
        out_specs=[row, wide, row, acc, acc, _full(dcw_shape), _full(dcb_shape)],
        out_shape=[jax.ShapeDtypeStruct((t, d), F32), jax.ShapeDtypeStruct((t, 2 * D_FF), BF16),
                   jax.ShapeDtypeStruct((t, d), BF16),
                   jax.ShapeDtypeStruct((8, d), F32), jax.ShapeDtypeStruct((8, d), F32),
                   jax.ShapeDtypeStruct(dcw_shape, F32), jax.ShapeDtypeStruct(dcb_shape, F32)],
        scratch_shapes=[pltpu.VMEM(w_up.shape, BF16), pltpu.VMEM(w_down.shape, BF16),
                        pltpu.VMEM((N_SHARD, h8, FF_CHUNK), F32), pltpu.SemaphoreType.DMA((2 * N_SHARD,))],
        compiler_params=_params(1, VMEM_LIMIT_MLP_BWD), name="mlp_bwd")(
            dxo, f, x, u, sp, g_pre, g_post, w_up, w_down, conv_w)


def grad_matmul(a, b, bm, bn, name, tk=2048, interleaved=False, after=None, cols=None):
    t = a.shape[0]
    m0, m = (0, a.shape[1]) if cols is None else cols
    n = b.shape[1]
    tk = min(tk, t)
    nk = t // tk
    place = (lambda j: (j % 2) * 2 + j // 2) if interleaved else (lambda j: j)
    extra = [] if after is None else [after]
    first = m0 // bm

    def body(a_ref, b_ref, *rest):
        o_ref, ob_ref = rest[len(extra):]
        kk = pl.program_id(2)

        @pl.when(kk == 0)
        def _():
            o_ref[...] = jnp.zeros_like(o_ref)

        o_ref[...] += _dot_tn(a_ref[...], b_ref[...])

        @pl.when(kk == nk - 1)
        def _():
            ob_ref[...] = o_ref[...].astype(BF16)

    ospec = pl.BlockSpec((None, bm, bn), lambda j, i, kk: (place(j), i, 0))
    return pl.pallas_call(
        body, grid=(n // bn, m // bm, nk),
        in_specs=[pl.BlockSpec((tk, bm), lambda j, i, kk: (kk, first + i)),
                  pl.BlockSpec((tk, bn), lambda j, i, kk: (kk, j))]
        + [ANY] * len(extra),
        out_specs=[ospec, ospec],
        out_shape=[jax.ShapeDtypeStruct((n // bn, m, bn), F32), jax.ShapeDtypeStruct((n // bn, m, bn), BF16)],
        compiler_params=_params(3), name=name)(a, b, *extra)


def _decay_tables():
    log_gamma = jnp.log(1.0 - 2.0 ** (-5.0 - jnp.arange(RET_HEADS, dtype=F32)))
    i = jnp.arange(RET_CHUNK, dtype=F32)
    rel = i[:, None] - i[None, :]
    intra = jnp.where(rel >= 0, jnp.exp(jnp.maximum(rel, 0.0) * log_gamma[:, None, None]), 0.0)
    cross = jnp.exp((i + 1.0) * log_gamma[:, None])[:, :, None]
    inner = jnp.exp((RET_CHUNK - 1.0 - i) * log_gamma[:, None])[:, :, None]
    chunk = [float(np.exp(np.float32(RET_CHUNK) * np.log(np.float32(1.0 - 2.0 ** (-5.0 - h))).astype(np.float32)))
             for h in range(RET_HEADS)]
    return intra, cross, inner, chunk


def ret_proj(x, g_pre, w_in, cos, sin, tm=512):
    t, d = x.shape
    nt = t // tm
    per = RET_IN_SHARD // RET_QK

    def body(x_ref, g_ref, win_hbm, c_ref, s_ref, pj_ref, h_ref, win_v, sem):
        @pl.when(pl.program_id(0) == 0)
        def _():
            copies = _quarter_copies([(win_hbm, win_v)], sem)
            for c in copies:
                c.start()
            for c in copies:
                c.wait()

        h, _, _ = _rms_fwd(x_ref[...], g_ref[...])
        hb = h.astype(BF16)
        h_ref[...] = hb
        c = c_ref[...]
        s = s_ref[...]
        for j in range(N_SHARD):
            pjj = _dot(hb, win_v[j])
            for bb in range(per):
                b = per * j + bb
                blk = pjj[:, bb * RET_QK:(bb + 1) * RET_QK]
                if b < 2 * RET_HEADS:
                    x1, x2 = blk[:, :128], blk[:, 128:]
                    o1 = x1 * c - x2 * s
                    o2 = x2 * c + x1 * s
                    if b < RET_HEADS:
                        o1 = o1 * (RET_QK ** -0.5)
                        o2 = o2 * (RET_QK ** -0.5)
                    pj_ref[:, b * RET_QK:b * RET_QK + 128] = o1.astype(BF16)
                    pj_ref[:, b * RET_QK + 128:(b + 1) * RET_QK] = o2.astype(BF16)
                else:
                    pj_ref[:, b * RET_QK:(b + 1) * RET_QK] = blk.astype(BF16)

    row = pl.BlockSpec((tm, d), lambda i: (i, 0))
    tab = pl.BlockSpec((tm, 128), lambda i: (i, 0))
    return pl.pallas_call(
        body, grid=(nt,),
        in_specs=[row, _full((1, d)), ANY, tab, tab],
        out_specs=[pl.BlockSpec((tm, RET_IN), lambda i: (i, 0)), row],
        out_shape=[jax.ShapeDtypeStruct((t, RET_IN), BF16), jax.ShapeDtypeStruct((t, d), BF16)],
        scratch_shapes=[pltpu.VMEM(w_in.shape, BF16), pltpu.SemaphoreType.DMA((N_SHARD,))],
        compiler_params=_params(1), name="ret_proj")(x, g_pre, w_in, cos, sin)


def ret_core_fwd(pj, intra, cross, inner, chunk_decay):
    t = pj.shape[0]
    nc = t // RET_CHUNK
    c = RET_CHUNK
    per = RET_STEP_CHUNKS
    qk_all = RET_HEADS * RET_QK
    v_all = RET_HEADS * RET_V

    def body(q_ref, k_ref, v_ref, dm_ref, cr_ref, in_ref, o_ref, sp_ref, state):
        @pl.when(pl.program_id(0) == 0)
        def _():
            state[...] = jnp.zeros_like(state)

        for h in range(RET_HEADS):
            for cc in range(per):
                rows = slice(cc * c, (cc + 1) * c)
                q = q_ref[rows, h * RET_QK:(h + 1) * RET_QK]
                k = k_ref[rows, h * RET_QK:(h + 1) * RET_QK]
                v = v_ref[rows, h * RET_V:(h + 1) * RET_V]
                sb = state[h].astype(BF16)
                sp_ref[cc, h] = sb
                sc = _dot_nt(q, k) * dm_ref[h]
                o_ref[rows, h * RET_V:(h + 1) * RET_V] = (_dot(sc.astype(BF16), v)
                                                          + _dot(q, sb) * cr_ref[h]).astype(BF16)
                kd = (k.astype(F32) * in_ref[h]).astype(BF16)
                state[h] = state[h] * chunk_decay[h] + _dot_tn(kd, v)

    return pl.pallas_call(
        body, grid=(nc // per,),
        in_specs=[pl.BlockSpec((per * c, qk_all), lambda n: (n, 0)), pl.BlockSpec((per * c, qk_all), lambda n: (n, 1)),
                  pl.BlockSpec((per * c, v_all), lambda n: (n, 1)),
                  _full(intra.shape), _full(cross.shape), _full(inner.shape)],
        out_specs=[pl.BlockSpec((per * c, v_all), lambda n: (n, 0)),
                   pl.BlockSpec((per, RET_HEADS, RET_QK, RET_V), lambda n: (n, 0, 0, 0))],
        out_shape=[jax.ShapeDtypeStruct((t, v_all), BF16),
                   jax.ShapeDtypeStruct((nc, RET_HEADS, RET_QK, RET_V), BF16)],
        scratch_shapes=[pltpu.VMEM((RET_HEADS, RET_QK, RET_V), F32)],
        compiler_params=_params(1), name="ret_core_fwd")(pj, pj, pj, intra, cross, inner)


def _group_norm(o_h):
    mu = jnp.mean(o_h, axis=-1, keepdims=True)
    dev = o_h - mu
    rstd = lax.rsqrt(jnp.mean(dev * dev, axis=-1, keepdims=True) + EPS)
    return dev * rstd, rstd


def ret_out_fwd(o, pj, x, gn_gain, g_post, w_out, tm=512):
    t, d = x.shape
    nt = t // tm
    v_all = RET_HEADS * RET_V

    def body(o_ref, g_ref, x_ref, gn_ref, g1_ref, w_ref, xo_ref, y_ref, out_ref):
        out = jnp.zeros((tm, d), F32)
        for h in range(RET_HEADS):
            cols = slice(h * RET_V, (h + 1) * RET_V)
            ohat, _ = _group_norm(o_ref[:, cols].astype(F32))
            g = g_ref[:, cols].astype(F32)
            yb = (g * jax.nn.sigmoid(g) * (ohat * gn_ref[:, cols])).astype(BF16)
            y_ref[:, cols] = yb
            out = out + _dot(yb, w_ref[cols, :])
        out_ref[...] = out
        m, _, _ = _rms_fwd(out, g1_ref[...])
        xo_ref[...] = x_ref[...] + m

    row = pl.BlockSpec((tm, d), lambda i: (i, 0))
    wide = pl.BlockSpec((tm, v_all), lambda i: (i, 0))
    return pl.pallas_call(
        body, grid=(nt,),
        in_specs=[wide, pl.BlockSpec((tm, v_all), lambda i: (i, 2)), row, _full((1, v_all)), _full((1, d)),
                  _full(w_out.shape)],
        out_specs=[row, wide, row],
        out_shape=[jax.ShapeDtypeStruct((t, d), F32), jax.ShapeDtypeStruct((t, v_all), BF16),
                   jax.ShapeDtypeStruct((t, d), F32)],
        compiler_params=_params(1), name="ret_out_fwd")(o, pj, x, gn_gain, g_post, w_out)


def ret_out_bwd(dxo, out, o, pj, gn_gain, g_post, w_out, tm=512):
    t, d = out.shape
    nt = t // tm
    v_all = RET_HEADS * RET_V

    def body(dxo_ref, out_ref, o_ref, g_ref, gn_ref, g1_ref, w_ref,
             dout_ref, dgate_ref, do_ref, dg1_ref, dgn_ref):
        @pl.when(pl.program_id(0) == 0)
        def _():
            dg1_ref[...] = jnp.zeros_like(dg1_ref)
            dgn_ref[...] = jnp.zeros_like(dgn_ref)

        g1 = g1_ref[...]
        dxo = dxo_ref[...]
        _, oh_, r_ = _rms_fwd(out_ref[...], g1)
        dg1_ref[...] += _colsum(dxo * oh_)
        doutb = _rms_bwd(oh_, r_, g1, dxo).astype(BF16)
        dout_ref[...] = doutb
        for h in range(RET_HEADS):
            cols = slice(h * RET_V, (h + 1) * RET_V)
            gn = gn_ref[:, cols]
            ohat, rstd = _group_norm(o_ref[:, cols].astype(F32))
            g = g_ref[:, cols].astype(F32)
            sg = jax.nn.sigmoid(g)
            dyh = _dot_nt(doutb, w_ref[cols, :])
            sil = g * sg
            dgate_ref[:, cols] = (dyh * (ohat * gn) * (sg + sil * (1.0 - sg))).astype(BF16)
            don = dyh * sil
            dgn_ref[:, cols] += _colsum(don * ohat)
            dohat = don * gn
            do_ref[:, cols] = (rstd * (dohat - jnp.mean(dohat, axis=-1, keepdims=True)
                                       - ohat * jnp.mean(dohat * ohat, axis=-1, keepdims=True))).astype(BF16)

    row = pl.BlockSpec((tm, d), lambda i: (i, 0))
    wide = pl.BlockSpec((tm, v_all), lambda i: (i, 0))
    gate = pl.BlockSpec((tm, v_all), lambda i: (i, 2))
    return pl.pallas_call(
        body, grid=(nt,),
        in_specs=[row, row, wide, gate, _full((1, v_all)), _full((1, d)), _full(w_out.shape)],
        out_specs=[row, gate, wide, _full((1, d)), _full((1, v_all))],
        out_shape=[jax.ShapeDtypeStruct((t, d), BF16), jax.ShapeDtypeStruct((t, RET_IN), BF16),
                   jax.ShapeDtypeStruct((t, v_all), BF16), jax.ShapeDtypeStruct((1, d), F32),
                   jax.ShapeDtypeStruct((1, v_all), F32)],
        compiler_params=_params(1), name="ret_out_bwd")(dxo, out, o, pj, gn_gain, g_post, w_out)


def ret_core_bwd(pj, do, sprev, cos, sin, dpj, intra, cross, inner, chunk_decay):
    t = pj.shape[0]
    nc = t // RET_CHUNK
    c = RET_CHUNK
    per = RET_STEP_CHUNKS
    qk_all = RET_HEADS * RET_QK
    v_all = RET_HEADS * RET_V
    scale = RET_QK ** -0.5

    def body(q_ref, k_ref, v_ref, do_ref, sp_ref, c_ref, s_ref, dm_ref, cr_ref, in_ref, dpj_in, dpj_ref, dstate):
        @pl.when(pl.program_id(0) == 0)
        def _():
            dstate[...] = jnp.zeros_like(dstate)

        for h in range(RET_HEADS):
            for cc in reversed(range(per)):
                rows = slice(cc * c, (cc + 1) * c)
                cs = c_ref[rows, :]
                sn = s_ref[rows, :]
                q = q_ref[rows, h * RET_QK:(h + 1) * RET_QK]
                k = k_ref[rows, h * RET_QK:(h + 1) * RET_QK]
                v = v_ref[rows, h * RET_V:(h + 1) * RET_V]
                doh = do_ref[rows, h * RET_V:(h + 1) * RET_V]
                dm = dm_ref[h]
                ab = (_dot_nt(q, k) * dm).astype(BF16)
                dab = (_dot_nt(doh, v) * dm).astype(BF16)
                dsb = dstate[h].astype(BF16)
                kd = (k.astype(F32) * in_ref[h]).astype(BF16)
                dv = _dot_tn(ab, doh) + _dot(kd, dsb)
                dq = _dot(dab, k) + cr_ref[h] * _dot_nt(doh, sp_ref[cc, h])
                dk = _dot_tn(dab, q) + in_ref[h] * _dot_nt(v, dsb)
                qd = (q.astype(F32) * cr_ref[h]).astype(BF16)
                dstate[h] = dstate[h] * chunk_decay[h] + _dot_tn(qd, doh)
                for base, dd, sc in ((h * RET_QK, dq, scale), (qk_all + h * RET_QK, dk, 1.0)):
                    d1, d2 = dd[:, :128], dd[:, 128:]
                    dpj_ref[rows, base:base + 128] = ((d1 * cs + d2 * sn) * sc).astype(BF16)
                    dpj_ref[rows, base + 128:base + RET_QK] = ((d2 * cs - d1 * sn) * sc).astype(BF16)
                dpj_ref[rows, 2 * qk_all + h * RET_V:2 * qk_all + (h + 1) * RET_V] = dv.astype(BF16)

    rev = lambda n: nc // per - 1 - n
    tab = pl.BlockSpec((per * c, 128), lambda n: (rev(n), 0))
    return pl.pallas_call(
        body, grid=(nc // per,),
        in_specs=[pl.BlockSpec((per * c, qk_all), lambda n: (rev(n), 0)),
                  pl.BlockSpec((per * c, qk_all), lambda n: (rev(n), 1)),
                  pl.BlockSpec((per * c, v_all), lambda n: (rev(n), 1)),
                  pl.BlockSpec((per * c, v_all), lambda n: (rev(n), 0)),
                  pl.BlockSpec((per, RET_HEADS, RET_QK, RET_V), lambda n: (rev(n), 0, 0, 0)),
                  tab, tab, _full(intra.shape), _full(cross.shape), _full(inner.shape), ANY],
        out_specs=pl.BlockSpec((per * c, 2 * qk_all + v_all), lambda n: (rev(n), 0)),
        out_shape=jax.ShapeDtypeStruct((t, RET_IN), BF16),
        scratch_shapes=[pltpu.VMEM((RET_HEADS, RET_QK, RET_V), F32)],
        input_output_aliases={10: 0},
        compiler_params=_params(1), name="ret_core_bwd")(pj, pj, pj, do, sprev, cos, sin, intra, cross, inner, dpj)


def ret_in_bwd(dpj, dres, x, g_pre, w_in, tm=512):
    t, d = x.shape
    nt = t // tm

    def body(dpj_ref, dres_ref, x_ref, g_ref, win_hbm, dx_ref, dg_ref, win_v, sem):
        @pl.when(pl.program_id(0) == 0)
        def _():
            copies = _quarter_copies([(win_hbm, win_v)], sem)
            for c in copies:
                c.start()
            dg_ref[...] = jnp.zeros_like(dg_ref)
            for c in copies:
                c.wait()

        g = g_ref[...]
        dh = jnp.zeros((tm, d), F32)
        for j in range(N_SHARD):
            dh = dh + _dot_nt(dpj_ref[:, j * RET_IN_SHARD:(j + 1) * RET_IN_SHARD], win_v[j])
        _, xh, rx = _rms_fwd(x_ref[...], g)
        dg_ref[...] += _colsum(dh * xh)
        dx_ref[...] = dres_ref[...] + _rms_bwd(xh, rx, g, dh)

    row = pl.BlockSpec((tm, d), lambda i: (i, 0))
    return pl.pallas_call(
        body, grid=(nt,),
        in_specs=[pl.BlockSpec((tm, RET_IN), lambda i: (i, 0)), row, row, _full((1, d)), ANY],
        out_specs=[row, _full((1, d))],
        out_shape=[jax.ShapeDtypeStruct((t, d), F32), jax.ShapeDtypeStruct((1, d), F32)],
        scratch_shapes=[pltpu.VMEM(w_in.shape, BF16), pltpu.SemaphoreType.DMA((N_SHARD,))],
        compiler_params=_params(1), name="ret_in_bwd")(dpj, dres, x, g_pre, w_in)


_CHIP_FLIPS = ((1, 0), (0, 1), (1, 1))


def _flip(v, b):
    return 1 - v if b else v


_HBM = pl.BlockSpec(memory_space=pltpu.HBM)
_SEM = pl.BlockSpec(memory_space=pltpu.SEMAPHORE)
_EFFECT = pltpu.SideEffectType.DATAFLOW_SIDE_EFFECTING


def _chip_copies(mode, srcs, lands, send_sems, recv_sems):
    x, y, c = lax.axis_index("x"), lax.axis_index("y"), lax.axis_index("c")
    copies = []
    for t in range(len(lands)):
        if mode == "swap":
            copies.append(pltpu.make_async_remote_copy(
                src_ref=srcs[t], dst_ref=lands[t], send_sem=send_sems.at[t], recv_sem=recv_sems.at[t],
                device_id=(x, y, 1 - c), device_id_type=MESH))
            continue
        if mode == "everyone":
            for m in range(1, 8):
                bx, by, bc = (m >> 2) & 1, (m >> 1) & 1, m & 1
                copies.append(pltpu.make_async_remote_copy(
                    src_ref=srcs[t], dst_ref=lands[t].at[4 * x + 2 * y + c], send_sem=send_sems.at[7 * t + m - 1],
                    recv_sem=recv_sems.at[7 * t + m - 1], device_id=(_flip(x, bx), _flip(y, by), _flip(c, bc)),
                    device_id_type=MESH))
            continue
        for k, (bx, by) in enumerate(_CHIP_FLIPS):
            px, py = _flip(x, bx), _flip(y, by)
            target = (px, py, c)
            if mode == "gather":
                src, dst = srcs[t], lands[t].at[2 * x + y]
            elif mode == "gather_half":
                half = pl.ds(c * (srcs[t].shape[0] // 2), srcs[t].shape[0] // 2)
                src, dst = srcs[t].at[half], lands[t].at[2 * x + y, half]
            elif mode == "forward_half":
                half = pl.ds(c * (lands[t].shape[1] // 2), lands[t].shape[1] // 2)
                src = dst = lands[t].at[2 * px + py, half]
                target = (x, y, 1 - c)
            else:
                src, dst = srcs[t].at[2 * px + py], lands[t].at[k]
            copies.append(pltpu.make_async_remote_copy(
                src_ref=src, dst_ref=dst, send_sem=send_sems.at[3 * t + k], recv_sem=recv_sems.at[3 * t + k],
                device_id=target, device_id_type=MESH))
    return copies


def exchange_start(mode, srcs, lands, name, after=None):
    n, ns = len(lands), len(srcs)
    extra = [] if after is None else [after]

    def body(*refs):
        ins, lnd = refs[:ns], refs[ns:ns + n]
        send_sems, recv_sems = refs[ns + n + len(extra)], refs[ns + n + len(extra) + 1]
        token = refs[-1]
        for cp in _chip_copies(mode, ins, lnd, send_sems, recv_sems):
            cp.start()
        token[...] = jnp.zeros(token.shape, token.dtype)

    hbm = lambda a: pltpu.with_memory_space_constraint(a, pltpu.HBM)
    passed = list(srcs) + list(lands)
    n_sem = {"swap": 1, "everyone": 7}.get(mode, 3) * n
    return pl.pallas_call(
        body, name=name,
        out_shape=(pltpu.SemaphoreType.DMA((n_sem,)), pltpu.SemaphoreType.DMA((n_sem,)),
                   *[pltpu.HBM(a.shape, a.dtype) for a in passed], jax.ShapeDtypeStruct((8, 128), F32)),
        in_specs=[_HBM] * (ns + n) + [ANY] * len(extra),
        out_specs=(_SEM, _SEM, *[_HBM] * (ns + n), pl.BlockSpec(memory_space=pltpu.VMEM)),
        input_output_aliases={i: 2 + i for i in range(ns + n)},
        compiler_params=pltpu.CompilerParams(has_side_effects=_EFFECT))(*[hbm(a) for a in passed], *extra)


def exchange_wait(mode, started, after, name):
    send_sems, recv_sems = started[0], started[1]
    passed = list(started[2:-1])
    n = len(passed) if mode == "forward_half" else len(passed) // 2
    ns = len(passed) - n
    after = list(after) if isinstance(after, (list, tuple)) else [after]

    def body(*refs):
        ins, lnd = refs[:ns], refs[ns:ns + n]
        for cp in _chip_copies(mode, ins, lnd, refs[ns + n], refs[ns + n + 1]):
            cp.wait_send()
            cp.wait_recv()

    outs = pl.pallas_call(
        body, name=name, out_shape=tuple(pltpu.HBM(a.shape, a.dtype) for a in passed),
        in_specs=[_HBM] * (ns + n) + [_SEM, _SEM] + [ANY] * len(after), out_specs=tuple([_HBM] * (ns + n)),
        input_output_aliases={i: i for i in range(ns + n)},
        compiler_params=pltpu.CompilerParams(has_side_effects=_EFFECT))(*passed, send_sems, recv_sems, *after)
    return list(outs[:ns]), list(outs[ns:])


def plane_sum(slot, full, recv, name, bm=256):
    _, m, n = full.shape
    bm = _row_block(m, bm)

    def body(slot_ref, o_ref, r_ref, s_ref):
        s_ref[...] = ((o_ref[...] + r_ref[0].astype(F32)) + r_ref[1].astype(F32)) + r_ref[2].astype(F32)

    return pl.pallas_call(
        body,
        grid_spec=pltpu.PrefetchScalarGridSpec(
            num_scalar_prefetch=1, grid=(m // bm,),
            in_specs=[pl.BlockSpec((None, bm, n), lambda i, s: (s[0], i, 0)),
                      pl.BlockSpec((3, bm, n), lambda i, s: (0, i, 0))],
            out_specs=pl.BlockSpec((bm, n), lambda i, s: (i, 0))),
        out_shape=jax.ShapeDtypeStruct((m, n), F32), compiler_params=_params(1), name=name)(slot, full, recv)


def sum_slots(parts, name, bm=312):
    _, r, n = parts.shape
    bm = bm if r % bm == 0 else r

    def body(p_ref, s_ref):
        acc = p_ref[0]
        for k in range(1, 8):
            acc = acc + p_ref[k]
        s_ref[...] = acc

    return pl.pallas_call(
        body, grid=(r // bm,), in_specs=[pl.BlockSpec((8, bm, n), lambda i: (0, i, 0))],
        out_specs=pl.BlockSpec((bm, n), lambda i: (i, 0)), out_shape=jax.ShapeDtypeStruct((r, n), F32),
        compiler_params=_params(1), name=name)(parts)


def _adamw_math(w, g, m, v):
    m = ADAM_B1 * m + (1.0 - ADAM_B1) * g
    v = ADAM_B2 * v + (1.0 - ADAM_B2) * (g * g)
    m_hat = m / (1.0 - ADAM_B1 ** ADAM_STEP)
    v_hat = v / (1.0 - ADAM_B2 ** ADAM_STEP)
    delta = -ADAM_LR * (m_hat / (jnp.sqrt(v_hat) + ADAM_EPS) + ADAM_WD * w)
    return delta, m, v


def adamw(w, m, v, grads, layer, prev, name, bm=256, row0=0):
    _, _, n = w.shape
    mm = grads[0].shape[0]
    bm = _row_block(mm, bm)
    first = row0 // bm
    ng = len(grads)

    def body(*refs):
        w_ref, m_ref, v_ref = refs[:3]
        g_refs = refs[3:3 + ng]
        g_out, d_out, m_out, v_out = refs[-4:]
        g = g_refs[0][...]
        for gr in g_refs[1:]:
            g = g + gr[...]
        delta, mn, vn = _adamw_math(w_ref[...], g, m_ref[...], v_ref[...])
        g_out[...] = g
        d_out[...] = delta
        m_out[...] = mn
        v_out[...] = vn

    slab = pl.BlockSpec((None, bm, n), lambda i: (layer, first + i, 0))
    flat = pl.BlockSpec((bm, n), lambda i: (i, 0))
    in_specs = [slab] * 3 + [flat] * ng
    args = [w, m, v, *grads]
    aliases = {}
    if prev is not None:
        in_specs += [ANY] * 4
        aliases = {3 + ng + q: q for q in range(4)}
        args += list(prev)
    return pl.pallas_call(
        body, grid=(mm // bm,), in_specs=in_specs, out_specs=[slab] * 4,
        out_shape=[jax.ShapeDtypeStruct(w.shape, F32)] * 4, input_output_aliases=aliases,
        compiler_params=_params(1), name=name)(*args)


def _pack_rows(parts, rows):
    flat = jnp.concatenate([p.reshape(-1) for p in parts])
    return jnp.pad(flat, (0, rows * 128 - flat.shape[0])).reshape(rows, 128)


def _as_shards(a, rows):
    return a.reshape(N_SHARD, rows, a.shape[-1])


def _local_step(x, pos_col, target, gains, pool_w, pool_scale, gn_gain, conv_w, conv_b, weights, send_grads):
    def gain(l, n, token=None):
        g = gains[l, n].reshape(1, D_MODEL)
        return g if token is None else g + token[0:1, 0:1]

    inv_freq = (ROPE_BASE ** (-jnp.arange(0, RET_QK, 2, dtype=F32) / RET_QK)).reshape(1, RET_QK // 2)
    intra, cross, inner, chunk_decay = _decay_tables()
    dn_rows = D_FF // N_SHARD

    x1 = pool_fwd(x, gain(0, 0), gain(0, 1), pool_w, pool_scale)
    cos, sin = rope_tables(pos_col, inv_freq, x1)
    w_up0, w_dn0 = weights("mlp0", cos)
    w_dn0 = w_dn0.reshape(D_FF, D_MODEL)
    x2, u0, s0, a0, f0, h0 = mlp_fwd(x1, gain(0, 2), gain(0, 3), w_up0, w_dn0, conv_w[0], conv_b[0])
    w_in, w_out = weights("ret", x2)
    w_out = w_out.reshape(RET_HEADS * RET_V, D_MODEL)
    pj, hr = ret_proj(x2, gain(1, 0), w_in, cos, sin)
    o, sprev = ret_core_fwd(pj, intra, cross, inner, chunk_decay)
    x3, yb, out = ret_out_fwd(o, pj, x2, gn_gain, gain(1, 1), w_out)
    w_up1, w_dn1 = weights("mlp1", x3)
    w_dn1 = w_dn1.reshape(D_FF, D_MODEL)
    dx4, u1, s1, a1, f1, h1, loss = mlp_fwd(x3, gain(1, 2), gain(1, 3), w_up1, w_dn1, conv_w[1], conv_b[1], target)

    dx3, du1, df1, dg12, dg13, dcw1, dcb1 = mlp_bwd(
        dx4, f1, x3, u1, s1, gain(1, 2), gain(1, 3), w_up1, w_dn1, conv_w[1])
    dwup1 = grad_matmul(h1, du1, D_MODEL, FF_CHUNK, "grad_w_up_1", interleaved=True)
    dwdn1 = grad_matmul(a1, df1, FF_CHUNK, D_MODEL, "grad_w_down_1")
    tok = send_grads("mlp1", [dwup1, [_as_shards(g, dn_rows) for g in dwdn1]])
    dout, dpj, do, dg11, dgn = ret_out_bwd(dx3, out, o, pj, gn_gain, gain(1, 1, tok), w_out)
    dwout = grad_matmul(yb, dout, 1024, D_MODEL, "grad_w_out")
    dpj = ret_core_bwd(pj, do, sprev, cos, sin, dpj, intra, cross, inner, chunk_decay)
    dwin = grad_matmul(hr, dpj, D_MODEL, RET_IN_SHARD, "grad_w_in")
    tok = send_grads("ret", [dwin, [_as_shards(g, RET_V) for g in dwout]])
    dx2, dg10 = ret_in_bwd(dpj, dx3, x2, gain(1, 0, tok), w_in)
    dx1, du0, df0, dg02, dg03, dcw0, dcb0 = mlp_bwd(
        dx2, f0, x1, u0, s0, gain(0, 2), gain(0, 3), w_up0, w_dn0, conv_w[0])
    dwdn0 = grad_matmul(a0, df0, FF_CHUNK, D_MODEL, "grad_w_down_0")
    tok = send_grads("down0", [[_as_shards(g, dn_rows) for g in dwdn0]])
    half = D_MODEL // 2
    for part, first in (("a", 0), ("b", half)):
        dwup0 = grad_matmul(h0, du0, half, FF_CHUNK, "grad_w_up_0" + part, tk=4096, interleaved=True, after=tok,
                            cols=(first, half))
        tok = send_grads("up0" + part, [dwup0])
    dx0, dg00, dg01, dpscale, dpw = pool_bwd(dx1, x, gain(0, 0, tok), gain(0, 1), pool_w, pool_scale)

    rows = lambda g: jnp.sum(g, axis=0, keepdims=True)
    dgains = jnp.concatenate([dg00, dg01, rows(dg02), rows(dg03), dg10, dg11, rows(dg12), rows(dg13)],
                             axis=0).reshape(2, 4, D_MODEL)
    small = {"gains": dgains, "pool_scale": dpscale, "gn": dgn,
             "conv_w": jnp.sum(jnp.stack([dcw0, dcw1]), axis=3),
             "conv_b": jnp.sum(jnp.stack([dcb0, dcb1]), axis=2, keepdims=True), "pool_w": dpw}
    return loss, dx0, small


def kernel(x, positions, norm_gain, pool_w, pool_scale, ret_w_in, ret_gn_gain, ret_w_out, mlp_w_up, mlp_conv_w, mlp_conv_b, mlp_w_down, loss_target, m_norm_gain, m_pool_w, m_pool_scale, m_ret_w_in, m_ret_gn_gain, m_ret_w_out, m_mlp_w_up, m_mlp_conv_w, m_mlp_conv_b, m_mlp_w_down, v_norm_gain, v_pool_w, v_pool_scale, v_ret_w_in, v_ret_gn_gain, v_ret_w_out, v_mlp_w_up, v_mlp_conv_w, v_mlp_conv_b, v_mlp_w_down):
    t = x.shape[1]
    me = 2 * lax.axis_index("x") + lax.axis_index("y")
    me_slot = jnp.reshape(me, (1,)).astype(jnp.int32)

    small_parts = [norm_gain, ret_gn_gain, mlp_conv_w, pool_w]
    small_sizes = [p.size for p in small_parts]
    small_rows = -(-sum(small_sizes) // (128 * 8)) * 8
    gathers = {}

    def start_gather(group, srcs, after):
        lands = [lax.dynamic_update_index_in_dim(lax.empty((N_SHARD,) + s.shape, s.dtype), s, me, 0) for s in srcs]
        mode = "gather_half" if group == "mlp0" else "gather"
        gathers[group] = (mode, exchange_start(mode, srcs, lands, "gather_start_" + group, after=after))
        return gathers[group][1][-1]

    token = start_gather("small", [_pack_rows(small_parts, small_rows)], None)
    token = start_gather("mlp0", [mlp_w_up[0].astype(BF16), mlp_w_down[0].astype(BF16)], token)

    def weights(group, after):
        if group == "mlp0":
            tok = start_gather("ret", [ret_w_in[0].astype(BF16), ret_w_out[0].astype(BF16)], after)
            after = start_gather("mlp1", [mlp_w_up[1].astype(BF16), mlp_w_down[1].astype(BF16)], tok)
        mode, started = gathers[group]
        _, lands = exchange_wait(mode, started, after, "gather_wait_" + group)
        if mode == "gather_half":
            forward = exchange_start("forward_half", [], lands, "forward_start_" + group)
            _, lands = exchange_wait("forward_half", forward, forward[-1], "forward_wait_" + group)
        return lands

    sent, early = {}, {}

    def reduced(group, after, names):
        started, own = sent[group]
        _, recv = exchange_wait("scatter", started, after, "scatter_wait_" + group)
        return [plane_sum(me_slot, f, r, "plane_sum_" + nm)
                for f, r, nm in zip(own, recv, names)]

    def swap_start(planes, name):
        return exchange_start("swap", planes, [lax.empty(p.shape, p.dtype) for p in planes], name)

    def send_grads(group, pairs):
        lands = [lax.empty((3,) + b.shape[1:], BF16) for _, b in pairs]
        sent[group] = (exchange_start("scatter", [b for _, b in pairs], lands, "scatter_start_" + group),
                       [f for f, _ in pairs])
        token = sent[group][0][-1]
        if group == "down0":
            marker = pairs[0][1]
            early["planes"] = (reduced("mlp1", marker, ["w_up_1", "w_down_1"])
                               + reduced("ret", marker, ["w_in", "w_out"]))
            early["swap"] = swap_start(early["planes"], "swap_start_a")
            token = token + early["swap"][-1]
        return token

    (smallg,) = weights("small", token)
    smallg = smallg.reshape(N_SHARD, -1)
    offs = np.cumsum([0] + small_sizes)
    piece = lambda i, shape: smallg[:, offs[i]:offs[i + 1]].reshape((N_SHARD,) + shape)
    gains = piece(0, (2, 4, 256)).transpose(1, 2, 0, 3).reshape(2, 4, D_MODEL)
    gn_full = piece(1, (512,)).reshape(1, RET_HEADS * RET_V)
    cw_full = piece(2, (2, 3, FF_CHUNK)).transpose(1, 0, 2, 3)
    pw_full = piece(3, (4, 64, 256)).transpose(1, 0, 2, 3).reshape(4, 256, 256).astype(BF16)
    cb_full = mlp_conv_b.reshape(2, N_SHARD, 1, FF_CHUNK)

    loss, dx0, small = _local_step(
        x[0], positions.reshape(t, 1).astype(F32), loss_target[0], gains, pw_full, pool_scale, gn_full,
        cw_full, cb_full, weights, send_grads)

    def small_adamw(w, m, v, grads, name):
        w3 = w.reshape(1, -1, w.shape[-1])
        out = adamw(w3, m.reshape(w3.shape), v.reshape(w3.shape), [g.reshape(w3.shape[1:]) for g in grads], 0, None, name)
        return [o.reshape(w.shape) for o in out]

    pw_f = small["pool_w"].reshape(4, N_SHARD, 64, 256).transpose(1, 0, 2, 3).reshape(N_SHARD, 256, 256)
    small = dict(small, loss=loss)
    small_order = ["gains", "pool_scale", "gn", "conv_w", "conv_b", "loss"]
    gsmall_sizes = [small[k].size for k in small_order]
    gsmall_rows = -(-sum(gsmall_sizes) // (128 * 8)) * 8
    gpack = _pack_rows([small[k] for k in small_order], gsmall_rows)
    mine = 2 * me + lax.axis_index("c")
    small_started = exchange_start(
        "everyone", [gpack], [lax.dynamic_update_index_in_dim(lax.empty((8,) + gpack.shape, F32), gpack, mine, 0)],
        "small_start")
    send_grads("pool_w", [(pw_f, pw_f.astype(BF16))])

    res = {}
    planes_a, others_a = exchange_wait("swap", early["swap"], small_started[-1], "swap_wait_a")
    res["ret_w_in"] = adamw(ret_w_in, m_ret_w_in, v_ret_w_in, (planes_a[2], others_a[2]), 0, None, "adamw_w_in")
    res["ret_w_out"] = adamw(ret_w_out, m_ret_w_out, v_ret_w_out, (planes_a[3], others_a[3]), 0, None, "adamw_w_out")
    up1 = adamw(mlp_w_up, m_mlp_w_up, v_mlp_w_up, (planes_a[0], others_a[0]), 1, None, "adamw_w_up_1")
    dn1 = adamw(mlp_w_down, m_mlp_w_down, v_mlp_w_down, (planes_a[1], others_a[1]), 1, None, "adamw_w_down_1")

    done_a = [res["ret_w_in"][0], res["ret_w_out"][0], up1[0], dn1[0]]
    planes_b = (reduced("up0a", done_a, ["w_up_0a"]) + reduced("up0b", done_a, ["w_up_0b"])
                + reduced("down0", done_a, ["w_down_0"]) + reduced("pool_w", done_a, ["pool_w"]))
    swap_b = swap_start(planes_b, "swap_start_b")

    _, (small_recv,) = exchange_wait("everyone", small_started, swap_b[-1], "small_wait")
    gsmall = sum_slots(small_recv, "sum_small").reshape(-1)
    goffs = np.cumsum([0] + gsmall_sizes)
    gpiece = lambda i: gsmall[goffs[i]:goffs[i + 1]].reshape(small[small_order[i]].shape)
    g_gains = lax.dynamic_slice_in_dim(gpiece(0), me * 256, 256, axis=2)
    g_gn = lax.dynamic_slice_in_dim(gpiece(2), me * RET_V, RET_V, axis=1)
    g_cw = lax.dynamic_index_in_dim(gpiece(3), me, 1, keepdims=False)
    res["norm_gain"] = small_adamw(norm_gain, m_norm_gain, v_norm_gain, [g_gains], "adamw_norm_gain")
    res["pool_scale"] = small_adamw(pool_scale, m_pool_scale, v_pool_scale, [gpiece(1)], "adamw_pool_scale")
    res["ret_gn_gain"] = small_adamw(ret_gn_gain, m_ret_gn_gain, v_ret_gn_gain, [g_gn], "adamw_gn_gain")
    res["mlp_conv_w"] = small_adamw(mlp_conv_w, m_mlp_conv_w, v_mlp_conv_w, [g_cw], "adamw_conv_w")
    res["mlp_conv_b"] = small_adamw(mlp_conv_b, m_mlp_conv_b, v_mlp_conv_b, [gpiece(4)], "adamw_conv_b")

    planes_b, others_b = exchange_wait("swap", swap_b, res["mlp_conv_b"][0], "swap_wait_b")
    up0a = adamw(mlp_w_up, m_mlp_w_up, v_mlp_w_up, (planes_b[0], others_b[0]), 0, up1, "adamw_w_up_0a")
    res["mlp_w_up"] = adamw(mlp_w_up, m_mlp_w_up, v_mlp_w_up, (planes_b[1], others_b[1]), 0, up0a, "adamw_w_up_0b",
                            row0=D_MODEL // 2)
    res["mlp_w_down"] = adamw(mlp_w_down, m_mlp_w_down, v_mlp_w_down, (planes_b[2], others_b[2]), 0, dn1,
                              "adamw_w_down_0")
    res["pool_w"] = small_adamw(pool_w, m_pool_w, v_pool_w, (planes_b[3], others_b[3]), "adamw_pool_w")

    order = ["norm_gain", "pool_w", "pool_scale", "ret_w_in", "ret_gn_gain", "ret_w_out", "mlp_w_up", "mlp_conv_w",
             "mlp_conv_b", "mlp_w_down"]
    outs = [gpiece(5)[0, 0], dx0.reshape(x.shape)]
    for q in range(4):
        outs += [res[k][q] for k in order]
    return tuple(outs)
```

```python
import numpy as np
import jax
import jax.numpy as jnp
from jax import lax
from jax.experimental import pallas as pl
from jax.experimental.pallas import tpu as pltpu

F32 = jnp.float32
BF16 = jnp.bfloat16

D_MODEL = 1024
D_FF = 2816
FF_CHUNK = 1408
N_SHARD = 4
POOL_WINDOWS = (2, 4, 8, 16)
POOL_DIM = 256
POOL_HALO = 16
RET_HEADS = 4
RET_QK = 256
RET_V = 512
RET_CHUNK = 256
RET_STEP_CHUNKS = 2
RET_IN = 6144
RET_IN_SHARD = 1536
ROPE_BASE = 10000.0
EPS = 1e-6
CONV_HALO = 8

ADAM_LR, ADAM_B1, ADAM_B2, ADAM_EPS, ADAM_WD, ADAM_STEP = 0.001, 0.9, 0.999, 1e-08, 0.01, 10

VMEM_LIMIT = 56 * 1024 * 1024
VMEM_LIMIT_MLP_BWD = 62 * 1024 * 1024
MESH = pl.DeviceIdType.MESH
ANY = pl.BlockSpec(memory_space=pl.ANY)


def _params(n_grid=1, limit=VMEM_LIMIT):
    return pltpu.CompilerParams(dimension_semantics=("arbitrary",) * n_grid, vmem_limit_bytes=limit)


def _dot(a, b):
    return jnp.dot(a, b, preferred_element_type=F32)


def _dot_nt(a, b):
    return lax.dot_general(a, b, (((1,), (1,)), ((), ())), preferred_element_type=F32)


def _dot_tn(a, b):
    return lax.dot_general(a, b, (((0,), (0,)), ((), ())), preferred_element_type=F32)


def _rms_fwd(x, gain):
    r = lax.rsqrt(jnp.mean(x * x, axis=-1, keepdims=True) + EPS)
    xh = x * r
    return xh * gain, xh, r


def _rms_bwd(xh, r, gain, dy):
    dxh = dy * gain
    return r * (dxh - xh * jnp.mean(dxh * xh, axis=-1, keepdims=True))


def _colsum(v):
    return jnp.sum(v, axis=0, keepdims=True)


def _full(shape):
    nd = len(shape)
    return pl.BlockSpec(shape, lambda *_: (0,) * nd)


def rope_tables(pos_col, inv_freq, after):
    t = pos_col.shape[0]
    tm = min(t, 1024)

    def body(p_ref, f_ref, after_ref, c_ref, s_ref):
        ang = p_ref[...] * f_ref[...]
        c_ref[...] = jnp.cos(ang)
        s_ref[...] = jnp.sin(ang)

    return pl.pallas_call(
        body, grid=(t // tm,),
        in_specs=[pl.BlockSpec((tm, 1), lambda i: (i, 0)), _full((1, 128)), ANY],
        out_specs=[pl.BlockSpec((tm, 128), lambda i: (i, 0))] * 2,
        out_shape=[jax.ShapeDtypeStruct((t, 128), F32)] * 2,
        compiler_params=_params(1), name="rope_tables")(pos_col, inv_freq, after)


def _window_sums(ext, backward):
    n = ext.shape[0]
    cur, sums = ext, []
    for g, win in enumerate(POOL_WINDOWS):
        if g > 0:
            cur = cur[:, POOL_DIM:]
        half = win // 2
        cur = cur + pltpu.roll(cur, n - half if backward else half, axis=0)
        sums.append(cur[:, 0:POOL_DIM])
    return sums


def _pool_diff(h_halo, h, row0, tm):
    t_idx = row0 + lax.broadcasted_iota(jnp.int32, (tm, 1), 0)
    sums = _window_sums(jnp.concatenate([h_halo, h], axis=0), backward=False)
    parts, inv_counts = [], []
    for g, win in enumerate(POOL_WINDOWS):
        inv = 1.0 / jnp.minimum(t_idx + 1, win).astype(F32)
        parts.append(sums[g][POOL_HALO:, :] * inv - h[:, g * POOL_DIM:(g + 1) * POOL_DIM])
        inv_counts.append(inv)
    return parts, inv_counts


def pool_fwd(x, g_pre, g_post, pool_w, pool_scale, tm=512):
    t, d = x.shape
    nt = t // tm

    def body(x_ref, g0_ref, g1_ref, w_ref, sc_ref, o_ref, hext):
        i = pl.program_id(0)

        @pl.when(i == 0)
        def _():
            hext[...] = jnp.zeros((POOL_HALO, d), F32)

        xv = x_ref[...]
        h, _, _ = _rms_fwd(xv, g0_ref[...])
        parts, _ = _pool_diff(hext[...], h, i * tm, tm)
        hext[...] = h[tm - POOL_HALO:tm, :]
        ys = [_dot(parts[g].astype(BF16), w_ref[g]) for g in range(len(POOL_WINDOWS))]
        y = jnp.concatenate(ys, axis=-1) * sc_ref[...]
        m, _, _ = _rms_fwd(y, g1_ref[...])
        o_ref[...] = xv + m

    row = pl.BlockSpec((tm, d), lambda i: (i, 0))
    return pl.pallas_call(
        body, grid=(nt,),
        in_specs=[row, _full((1, d)), _full((1, d)), _full(pool_w.shape), _full((1, d))],
        out_specs=row, out_shape=jax.ShapeDtypeStruct((t, d), F32),
        scratch_shapes=[pltpu.VMEM((POOL_HALO, d), F32)],
        compiler_params=_params(1), name="pool_fwd")(x, g_pre, g_post, pool_w, pool_scale)


def pool_bwd(dx1, x, g_pre, g_post, pool_w, pool_scale, tm=512):
    t, d = x.shape
    nt = t // tm
    ng = len(POOL_WINDOWS)

    def body(dx1_ref, x_ref, xh_ref, g0_ref, g1_ref, w_ref, sc_ref,
             dx_ref, dg0_ref, dg1_ref, dsc_ref, dw_ref, enext):
        i = pl.program_id(0)
        r = nt - 1 - i

        @pl.when(i == 0)
        def _():
            enext[...] = jnp.zeros((POOL_HALO, d), F32)
            dg0_ref[...] = jnp.zeros_like(dg0_ref)
            dg1_ref[...] = jnp.zeros_like(dg1_ref)
            dsc_ref[...] = jnp.zeros_like(dsc_ref)
            dw_ref[...] = jnp.zeros_like(dw_ref)

        g0 = g0_ref[...]
        g1 = g1_ref[...]
        sc = sc_ref[...]
        xv = x_ref[...]
        h, xh, rx = _rms_fwd(xv, g0)
        h_halo, _, _ = _rms_fwd(xh_ref[...], g0)
        parts, inv_counts = _pool_diff(h_halo * jnp.where(r > 0, 1.0, 0.0), h, r * tm, tm)
        parts_b = [p.astype(BF16) for p in parts]
        ypre = jnp.concatenate([_dot(parts_b[g], w_ref[g]) for g in range(ng)], axis=-1)
        _, yh, ry = _rms_fwd(ypre * sc, g1)
        dm = dx1_ref[...]
        dg1_ref[...] += _colsum(dm * yh)
        dy = _rms_bwd(yh, ry, g1, dm)
        dsc_ref[...] += _colsum(dy * ypre)
        dyp = (dy * sc).astype(BF16)
        ddiffs = []
        for g in range(ng):
            cols = slice(g * POOL_DIM, (g + 1) * POOL_DIM)
            dw_ref[g] += _dot_tn(parts_b[g], dyp[:, cols])
            ddiffs.append(_dot_nt(dyp[:, cols], w_ref[g]))
        e = jnp.concatenate([ddiffs[g] * inv_counts[g] for g in range(ng)], axis=-1)
        sums = _window_sums(jnp.concatenate([e, enext[...]], axis=0), backward=True)
        enext[...] = e[0:POOL_HALO, :]
        dh = jnp.concatenate([sums[g][0:tm, :] - ddiffs[g] for g in range(ng)], axis=-1)
        dg0_ref[...] += _colsum(dh * xh)
        dx_ref[...] = dm + _rms_bwd(xh, rx, g0, dh)

    row = pl.BlockSpec((tm, d), lambda i: (nt - 1 - i, 0))
    halo = pl.BlockSpec((POOL_HALO, d), lambda i: (jnp.maximum((nt - 1 - i) * (tm // POOL_HALO) - 1, 0), 0))
    vec = _full((1, d))
    return pl.pallas_call(
        body, grid=(nt,),
        in_specs=[row, row, halo, vec, vec, _full(pool_w.shape), vec],
        out_specs=[row, vec, vec, vec, _full((ng, POOL_DIM, POOL_DIM))],
        out_shape=[jax.ShapeDtypeStruct((t, d), F32)] + [jax.ShapeDtypeStruct((1, d), F32)] * 3
        + [jax.ShapeDtypeStruct((ng, POOL_DIM, POOL_DIM), F32)],
        scratch_shapes=[pltpu.VMEM((POOL_HALO, d), F32)],
        compiler_params=_params(1), name="pool_bwd")(dx1, x, x, g_pre, g_post, pool_w, pool_scale)


def _conv_taps(cw_ref, j):
    return cw_ref[j, 0:1, :], cw_ref[j, 1:2, :], cw_ref[j, 2:3, :]


def _row_block(m, target=256):
    if m <= target:
        return m
    for b in range(target, 7, -8):
        if m % b == 0:
            return b
    return m


def mlp_fwd(x, g_pre, g_post, w_up, w_down, conv_w, conv_b, target=None, tm=256):
    t, d = x.shape
    nt = t // tm
    h8 = CONV_HALO
    with_loss = target is not None
    n_extra = 1 if with_loss else 0

    def body(x_ref, g2_ref, g3_ref, wup_hbm, wdn_hbm, cw_ref, cb_ref, *rest):
        tgt_ref = rest[0] if with_loss else None
        xo_ref, u_ref, s_ref, a_ref, f_ref, h_ref = rest[n_extra:n_extra + 6]
        loss_ref = rest[n_extra + 6] if with_loss else None
        wup_v, wdn_v, tail, sem = rest[-4:]
        i = pl.program_id(0)

        @pl.when(i == 0)
        def _():
            c1 = pltpu.make_async_copy(wup_hbm, wup_v, sem.at[0])
            c2 = pltpu.make_async_copy(wdn_hbm, wdn_v, sem.at[1])
            c1.start()
            c2.start()
            tail[...] = jnp.zeros_like(tail)
            if with_loss:
                loss_ref[...] = jnp.zeros_like(loss_ref)
            c1.wait()
            c2.wait()

        xv = x_ref[...]
        h, _, _ = _rms_fwd(xv, g2_ref[...])
        hb = h.astype(BF16)
        h_ref[...] = hb
        acc = jnp.zeros((tm, d), F32)
        for k in range(2):
            cs = []
            for s in range(2):
                j, cols = k + 2 * s, slice((2 * k + s) * FF_CHUNK, (2 * k + s + 1) * FF_CHUNK)
                uf = _dot(hb, wup_v[j])
                u_ref[:, cols] = uf.astype(BF16)
                ext = jnp.concatenate([tail[j], uf], axis=0)
                tail[j] = uf[tm - h8:tm, :]
                w0, w1, w2 = _conv_taps(cw_ref, j)
                cs.append(cb_ref[j] + w2 * uf + w1 * pltpu.roll(ext, 1, axis=0)[h8:, :]
                          + w0 * pltpu.roll(ext, 2, axis=0)[h8:, :])
            cg, cv = cs
            sg = jax.nn.sigmoid(cg)
            sil = cg * sg
            ab = (sil * cv).astype(BF16)
            a_ref[:, k * FF_CHUNK:(k + 1) * FF_CHUNK] = ab
            s_ref[:, 2 * k * FF_CHUNK:(2 * k + 1) * FF_CHUNK] = sil.astype(BF16)
            s_ref[:, (2 * k + 1) * FF_CHUNK:(2 * k + 2) * FF_CHUNK] = (cv * (sg + sil * (1.0 - sg))).astype(BF16)
            acc = acc + _dot(ab, wdn_v[k * FF_CHUNK:(k + 1) * FF_CHUNK, :])
        f_ref[...] = acc
        y, _, _ = _rms_fwd(acc, g3_ref[...])
        if with_loss:
            err = (xv + y) - tgt_ref[...]
            xo_ref[...] = err * (1.0 / d)
            loss_ref[...] += 0.5 * jnp.sum(jnp.mean(err * err, axis=-1, keepdims=True), axis=0, keepdims=True)
        else:
            xo_ref[...] = xv + y

    row = pl.BlockSpec((tm, d), lambda i: (i, 0))
    wide = pl.BlockSpec((tm, 2 * D_FF), lambda i: (i, 0))
    vec = _full((1, d))
    extra = [target] if with_loss else []
    return pl.pallas_call(
        body, grid=(nt,),
        in_specs=[row, vec, vec, ANY, ANY, _full(conv_w.shape), _full(conv_b.shape)] + [row] * n_extra,
        out_specs=[row, wide, wide, pl.BlockSpec((tm, D_FF), lambda i: (i, 0)), row, row] + [_full((1, 1))] * n_extra,
        out_shape=[jax.ShapeDtypeStruct((t, d), F32), jax.ShapeDtypeStruct((t, 2 * D_FF), BF16),
                   jax.ShapeDtypeStruct((t, 2 * D_FF), BF16), jax.ShapeDtypeStruct((t, D_FF), BF16),
                   jax.ShapeDtypeStruct((t, d), F32), jax.ShapeDtypeStruct((t, d), BF16)]
        + [jax.ShapeDtypeStruct((1, 1), F32)] * n_extra,
        scratch_shapes=[pltpu.VMEM(w_up.shape, BF16), pltpu.VMEM(w_down.shape, BF16),
                        pltpu.VMEM((N_SHARD, h8, FF_CHUNK), F32), pltpu.SemaphoreType.DMA((2,))],
        compiler_params=_params(1), name="mlp_fwd_loss" if with_loss else "mlp_fwd")(
            x, g_pre, g_post, w_up, w_down, conv_w, conv_b, *extra)


def _rowsum8(v):
    return jnp.sum(v.reshape(v.shape[0] // 8, 8, v.shape[1]), axis=0)


def mlp_bwd(dxo, f, x, u, sp, g_pre, g_post, w_up, w_down, conv_w, tm=256):
    t, d = x.shape
    nt = t // tm
    h8 = CONV_HALO

    def body(dxo_ref, f_ref, x_ref, u_ref, s_ref, g2_ref, g3_ref, wup_hbm, wdn_hbm, cw_ref,
             dx_ref, du_ref, df_ref, dg2_ref, dg3_ref, dcw_ref, dcb_ref,
             wup_v, wdn_v, carry, sem):
        @pl.when(pl.program_id(0) == 0)
        def _():
            c1 = pltpu.make_async_copy(wup_hbm, wup_v, sem.at[0])
            c2 = pltpu.make_async_copy(wdn_hbm, wdn_v, sem.at[1])
            c1.start()
            c2.start()
            carry[...] = jnp.zeros_like(carry)
            dg2_ref[...] = jnp.zeros_like(dg2_ref)
            dg3_ref[...] = jnp.zeros_like(dg3_ref)
            dcw_ref[...] = jnp.zeros_like(dcw_ref)
            dcb_ref[...] = jnp.zeros_like(dcb_ref)
            c1.wait()
            c2.wait()

        g3 = g3_ref[...]
        dxo = dxo_ref[...]
        _, fh, rf = _rms_fwd(f_ref[...], g3)
        dg3_ref[...] += _rowsum8(dxo * fh)
        dfb = _rms_bwd(fh, rf, g3, dxo).astype(BF16)
        df_ref[...] = dfb
        dh = jnp.zeros((tm, d), F32)
        for k in range(2):
            da = _dot_nt(dfb, wdn_v[k * FF_CHUNK:(k + 1) * FF_CHUNK, :])
            for s in range(2):
                j = k + 2 * s
                cols = slice((2 * k + s) * FF_CHUNK, (2 * k + s + 1) * FF_CHUNK)
                dc = da * s_ref[:, (2 * k + 1 - s) * FF_CHUNK:(2 * k + 2 - s) * FF_CHUNK].astype(F32)
                uf = u_ref[:, cols].astype(F32)
                ext = jnp.concatenate([dc, carry[j]], axis=0)
                carry[j] = dc[0:h8, :]
                dc1 = pltpu.roll(ext, tm + h8 - 1, axis=0)[0:tm, :]
                dc2 = pltpu.roll(ext, tm + h8 - 2, axis=0)[0:tm, :]
                dcb_ref[j] += _rowsum8(dc)
                dcw_ref[j, 2] += _rowsum8(dc * uf)
                dcw_ref[j, 1] += _rowsum8(dc1 * uf)
                dcw_ref[j, 0] += _rowsum8(dc2 * uf)
                dub = (cw_ref[j, 2:3, :] * dc + cw_ref[j, 1:2, :] * dc1 + cw_ref[j, 0:1, :] * dc2).astype(BF16)
                du_ref[:, cols] = dub
                dh = dh + _dot_nt(dub, wup_v[j])
        g2 = g2_ref[...]
        _, xh, rx = _rms_fwd(x_ref[...], g2)
        dg2_ref[...] += _rowsum8(dh * xh)
        dx_ref[...] = dxo + _rms_bwd(xh, rx, g2, dh)

    row = pl.BlockSpec((tm, d), lambda i: (nt - 1 - i, 0))
    wide = pl.BlockSpec((tm, 2 * D_FF), lambda i: (nt - 1 - i, 0))
    vec = _full((1, d))
    acc = _full((8, d))
    dcw_shape, dcb_shape = (N_SHARD, 3, 8, FF_CHUNK), (N_SHARD, 8, FF_CHUNK)
    return pl.pallas_call(
        body, grid=(nt,),
        in_specs=[row, row, row, wide, wide, vec, vec, ANY, ANY, _full(conv_w.shape)],
        out_specs=[row, wide, row, acc, acc, _full(dcw_shape), _full(dcb_shape)],
        out_shape=[jax.ShapeDtypeStruct((t, d), F32), jax.ShapeDtypeStruct((t, 2 * D_FF), BF16),
                   jax.ShapeDtypeStruct((t, d), BF16),
                   jax.ShapeDtypeStruct((8, d), F32), jax.ShapeDtypeStruct((8, d), F32),
                   jax.ShapeDtypeStruct(dcw_shape, F32), jax.ShapeDtypeStruct(dcb_shape, F32)],
        scratch_shapes=[pltpu.VMEM(w_up.shape, BF16), pltpu.VMEM(w_down.shape, BF16),
                        pltpu.VMEM((N_SHARD, h8, FF_CHUNK), F32), pltpu.SemaphoreType.DMA((2,))],
        compiler_params=_params(1, VMEM_LIMIT_MLP_BWD), name="mlp_bwd")(
            dxo, f, x, u, sp, g_pre, g_post, w_up, w_down, conv_w)


def grad_matmul(a, b, bm, bn, name, tk=2048, interleaved=False, after=None):
    t, m = a.shape
    n = b.shape[1]
    tk = min(tk, t)
    nk = t // tk
    place = (lambda j: (j % 2) * 2 + j // 2) if interleaved else (lambda j: j)
    extra = [] if after is None else [after]

    def body(a_ref, b_ref, *rest):
        o_ref, ob_ref = rest[len(extra):]
        kk = pl.program_id(2)

        @pl.when(kk == 0)
        def _():
            o_ref[...] = jnp.zeros_like(o_ref)

        o_ref[...] += _dot_tn(a_ref[...], b_ref[...])

        @pl.when(kk == nk - 1)
        def _():
            ob_ref[...] = o_ref[...].astype(BF16)

    ospec = pl.BlockSpec((None, bm, bn), lambda j, i, kk: (place(j), i, 0))
    return pl.pallas_call(
        body, grid=(n // bn, m // bm, nk),
        in_specs=[pl.BlockSpec((tk, bm), lambda j, i, kk: (kk, i)),
                  pl.BlockSpec((tk, bn), lambda j, i, kk: (kk, j))]
        + [ANY] * len(extra),
        out_specs=[ospec, ospec],
        out_shape=[jax.ShapeDtypeStruct((n // bn, m, bn), F32), jax.ShapeDtypeStruct((n // bn, m, bn), BF16)],
        compiler_params=_params(3), name=name)(a, b, *extra)


def _decay_tables():
    log_gamma = jnp.log(1.0 - 2.0 ** (-5.0 - jnp.arange(RET_HEADS, dtype=F32)))
    i = jnp.arange(RET_CHUNK, dtype=F32)
    rel = i[:, None] - i[None, :]
    intra = jnp.where(rel >= 0, jnp.exp(jnp.maximum(rel, 0.0) * log_gamma[:, None, None]), 0.0)
    cross = jnp.exp((i + 1.0) * log_gamma[:, None])[:, :, None]
    inner = jnp.exp((RET_CHUNK - 1.0 - i) * log_gamma[:, None])[:, :, None]
    chunk = [float(np.exp(np.float32(RET_CHUNK) * np.log(np.float32(1.0 - 2.0 ** (-5.0 - h))).astype(np.float32)))
             for h in range(RET_HEADS)]
    return intra, cross, inner, chunk


def ret_proj(x, g_pre, w_in, cos, sin, tm=512):
    t, d = x.shape
    nt = t // tm
    per = RET_IN_SHARD // RET_QK

    def body(x_ref, g_ref, win_hbm, c_ref, s_ref, pj_ref, h_ref, win_v, sem):
        @pl.when(pl.program_id(0) == 0)
        def _():
            cp = pltpu.make_async_copy(win_hbm, win_v, sem)
            cp.start()
            cp.wait()

        h, _, _ = _rms_fwd(x_ref[...], g_ref[...])
        hb = h.astype(BF16)
        h_ref[...] = hb
        c = c_ref[...]
        s = s_ref[...]
        for j in range(N_SHARD):
            pjj = _dot(hb, win_v[j])
            for bb in range(per):
                b = per * j + bb
                blk = pjj[:, bb * RET_QK:(bb + 1) * RET_QK]
                if b < 2 * RET_HEADS:
                    x1, x2 = blk[:, :128], blk[:, 128:]
                    o1 = x1 * c - x2 * s
                    o2 = x2 * c + x1 * s
                    if b < RET_HEADS:
                        o1 = o1 * (RET_QK ** -0.5)
                        o2 = o2 * (RET_QK ** -0.5)
                    pj_ref[:, b * RET_QK:b * RET_QK + 128] = o1.astype(BF16)
                    pj_ref[:, b * RET_QK + 128:(b + 1) * RET_QK] = o2.astype(BF16)
                else:
                    pj_ref[:, b * RET_QK:(b + 1) * RET_QK] = blk.astype(BF16)

    row = pl.BlockSpec((tm, d), lambda i: (i, 0))
    tab = pl.BlockSpec((tm, 128), lambda i: (i, 0))
    return pl.pallas_call(
        body, grid=(nt,),
        in_specs=[row, _full((1, d)), ANY, tab, tab],
        out_specs=[pl.BlockSpec((tm, RET_IN), lambda i: (i, 0)), row],
        out_shape=[jax.ShapeDtypeStruct((t, RET_IN), BF16), jax.ShapeDtypeStruct((t, d), BF16)],
        scratch_shapes=[pltpu.VMEM(w_in.shape, BF16), pltpu.SemaphoreType.DMA],
        compiler_params=_params(1), name="ret_proj")(x, g_pre, w_in, cos, sin)


def ret_core_fwd(pj, intra, cross, inner, chunk_decay):
    t = pj.shape[0]
    nc = t // RET_CHUNK
    c = RET_CHUNK
    per = RET_STEP_CHUNKS
    qk_all = RET_HEADS * RET_QK
    v_all = RET_HEADS * RET_V

    def body(q_ref, k_ref, v_ref, dm_ref, cr_ref, in_ref, o_ref, sp_ref, state):
        @pl.when(pl.program_id(0) == 0)
        def _():
            state[...] = jnp.zeros_like(state)

        for h in range(RET_HEADS):
            for cc in range(per):
                rows = slice(cc * c, (cc + 1) * c)
                q = q_ref[rows, h * RET_QK:(h + 1) * RET_QK]
                k = k_ref[rows, h * RET_QK:(h + 1) * RET_QK]
                v = v_ref[rows, h * RET_V:(h + 1) * RET_V]
                sb = state[h].astype(BF16)
                sp_ref[cc, h] = sb
                sc = _dot_nt(q, k) * dm_ref[h]
                o_ref[rows, h * RET_V:(h + 1) * RET_V] = (_dot(sc.astype(BF16), v)
                                                          + _dot(q, sb) * cr_ref[h]).astype(BF16)
                kd = (k.astype(F32) * in_ref[h]).astype(BF16)
                state[h] = state[h] * chunk_decay[h] + _dot_tn(kd, v)

    return pl.pallas_call(
        body, grid=(nc // per,),
        in_specs=[pl.BlockSpec((per * c, qk_all), lambda n: (n, 0)), pl.BlockSpec((per * c, qk_all), lambda n: (n, 1)),
                  pl.BlockSpec((per * c, v_all), lambda n: (n, 1)),
                  _full(intra.shape), _full(cross.shape), _full(inner.shape)],
        out_specs=[pl.BlockSpec((per * c, v_all), lambda n: (n, 0)),
                   pl.BlockSpec((per, RET_HEADS, RET_QK, RET_V), lambda n: (n, 0, 0, 0))],
        out_shape=[jax.ShapeDtypeStruct((t, v_all), BF16),
                   jax.ShapeDtypeStruct((nc, RET_HEADS, RET_QK, RET_V), BF16)],
        scratch_shapes=[pltpu.VMEM((RET_HEADS, RET_QK, RET_V), F32)],
        compiler_params=_params(1), name="ret_core_fwd")(pj, pj, pj, intra, cross, inner)


def _group_norm(o_h):
    mu = jnp.mean(o_h, axis=-1, keepdims=True)
    dev = o_h - mu
    rstd = lax.rsqrt(jnp.mean(dev * dev, axis=-1, keepdims=True) + EPS)
    return dev * rstd, rstd


def ret_out_fwd(o, pj, x, gn_gain, g_post, w_out, tm=512):
    t, d = x.shape
    nt = t // tm
    v_all = RET_HEADS * RET_V

    def body(o_ref, g_ref, x_ref, gn_ref, g1_ref, w_ref, xo_ref, y_ref, out_ref):
        out = jnp.zeros((tm, d), F32)
        for h in range(RET_HEADS):
            cols = slice(h * RET_V, (h + 1) * RET_V)
            ohat, _ = _group_norm(o_ref[:, cols].astype(F32))
            g = g_ref[:, cols].astype(F32)
            yb = (g * jax.nn.sigmoid(g) * (ohat * gn_ref[:, cols])).astype(BF16)
            y_ref[:, cols] = yb
            out = out + _dot(yb, w_ref[cols, :])
        out_ref[...] = out
        m, _, _ = _rms_fwd(out, g1_ref[...])
        xo_ref[...] = x_ref[...] + m

    row = pl.BlockSpec((tm, d), lambda i: (i, 0))
    wide = pl.BlockSpec((tm, v_all), lambda i: (i, 0))
    return pl.pallas_call(
        body, grid=(nt,),
        in_specs=[wide, pl.BlockSpec((tm, v_all), lambda i: (i, 2)), row, _full((1, v_all)), _full((1, d)),
                  _full(w_out.shape)],
        out_specs=[row, wide, row],
        out_shape=[jax.ShapeDtypeStruct((t, d), F32), jax.ShapeDtypeStruct((t, v_all), BF16),
                   jax.ShapeDtypeStruct((t, d), F32)],
        compiler_params=_params(1), name="ret_out_fwd")(o, pj, x, gn_gain, g_post, w_out)


def ret_out_bwd(dxo, out, o, pj, gn_gain, g_post, w_out, tm=512):
    t, d = out.shape
    nt = t // tm
    v_all = RET_HEADS * RET_V

    def body(dxo_ref, out_ref, o_ref, g_ref, gn_ref, g1_ref, w_ref,
             dout_ref, dgate_ref, do_ref, dg1_ref, dgn_ref):
        @pl.when(pl.program_id(0) == 0)
        def _():
            dg1_ref[...] = jnp.zeros_like(dg1_ref)
            dgn_ref[...] = jnp.zeros_like(dgn_ref)

        g1 = g1_ref[...]
        dxo = dxo_ref[...]
        _, oh_, r_ = _rms_fwd(out_ref[...], g1)
        dg1_ref[...] += _colsum(dxo * oh_)
        doutb = _rms_bwd(oh_, r_, g1, dxo).astype(BF16)
        dout_ref[...] = doutb
        for h in range(RET_HEADS):
            cols = slice(h * RET_V, (h + 1) * RET_V)
            gn = gn_ref[:, cols]
            ohat, rstd = _group_norm(o_ref[:, cols].astype(F32))
            g = g_ref[:, cols].astype(F32)
            sg = jax.nn.sigmoid(g)
            dyh = _dot_nt(doutb, w_ref[cols, :])
            sil = g * sg
            dgate_ref[:, cols] = (dyh * (ohat * gn) * (sg + sil * (1.0 - sg))).astype(BF16)
            don = dyh * sil
            dgn_ref[:, cols] += _colsum(don * ohat)
            dohat = don * gn
            do_ref[:, cols] = (rstd * (dohat - jnp.mean(dohat, axis=-1, keepdims=True)
                                       - ohat * jnp.mean(dohat * ohat, axis=-1, keepdims=True))).astype(BF16)

    row = pl.BlockSpec((tm, d), lambda i: (i, 0))
    wide = pl.BlockSpec((tm, v_all), lambda i: (i, 0))
    gate = pl.BlockSpec((tm, v_all), lambda i: (i, 2))
    return pl.pallas_call(
        body, grid=(nt,),
        in_specs=[row, row, wide, gate, _full((1, v_all)), _full((1, d)), _full(w_out.shape)],
        out_specs=[row, gate, wide, _full((1, d)), _full((1, v_all))],
        out_shape=[jax.ShapeDtypeStruct((t, d), BF16), jax.ShapeDtypeStruct((t, RET_IN), BF16),
                   jax.ShapeDtypeStruct((t, v_all), BF16), jax.ShapeDtypeStruct((1, d), F32),
                   jax.ShapeDtypeStruct((1, v_all), F32)],
        compiler_params=_params(1), name="ret_out_bwd")(dxo, out, o, pj, gn_gain, g_post, w_out)


def ret_core_bwd(pj, do, sprev, cos, sin, dpj, intra, cross, inner, chunk_decay):
    t = pj.shape[0]
    nc = t // RET_CHUNK
    c = RET_CHUNK
    per = RET_STEP_CHUNKS
    qk_all = RET_HEADS * RET_QK
    v_all = RET_HEADS * RET_V
    scale = RET_QK ** -0.5

    def body(q_ref, k_ref, v_ref, do_ref, sp_ref, c_ref, s_ref, dm_ref, cr_ref, in_ref, dpj_in, dpj_ref, dstate):
        @pl.when(pl.program_id(0) == 0)
        def _():
            dstate[...] = jnp.zeros_like(dstate)

        for h in range(RET_HEADS):
            for cc in reversed(range(per)):
                rows = slice(cc * c, (cc + 1) * c)
                cs = c_ref[rows, :]
                sn = s_ref[rows, :]
                q = q_ref[rows, h * RET_QK:(h + 1) * RET_QK]
                k = k_ref[rows, h * RET_QK:(h + 1) * RET_QK]
                v = v_ref[rows, h * RET_V:(h + 1) * RET_V]
                doh = do_ref[rows, h * RET_V:(h + 1) * RET_V]
                dm = dm_ref[h]
                ab = (_dot_nt(q, k) * dm).astype(BF16)
                dab = (_dot_nt(doh, v) * dm).astype(BF16)
                dsb = dstate[h].astype(BF16)
                kd = (k.astype(F32) * in_ref[h]).astype(BF16)
                dv = _dot_tn(ab, doh) + _dot(kd, dsb)
                dq = _dot(dab, k) + cr_ref[h] * _dot_nt(doh, sp_ref[cc, h])
                dk = _dot_tn(dab, q) + in_ref[h] * _dot_nt(v, dsb)
                qd = (q.astype(F32) * cr_ref[h]).astype(BF16)
                dstate[h] = dstate[h] * chunk_decay[h] + _dot_tn(qd, doh)
                for base, dd, sc in ((h * RET_QK, dq, scale), (qk_all + h * RET_QK, dk, 1.0)):
                    d1, d2 = dd[:, :128], dd[:, 128:]
                    dpj_ref[rows, base:base + 128] = ((d1 * cs + d2 * sn) * sc).astype(BF16)
                    dpj_ref[rows, base + 128:base + RET_QK] = ((d2 * cs - d1 * sn) * sc).astype(BF16)
                dpj_ref[rows, 2 * qk_all + h * RET_V:2 * qk_all + (h + 1) * RET_V] = dv.astype(BF16)

    rev = lambda n: nc // per - 1 - n
    tab = pl.BlockSpec((per * c, 128), lambda n: (rev(n), 0))
    return pl.pallas_call(
        body, grid=(nc // per,),
        in_specs=[pl.BlockSpec((per * c, qk_all), lambda n: (rev(n), 0)),
                  pl.BlockSpec((per * c, qk_all), lambda n: (rev(n), 1)),
                  pl.BlockSpec((per * c, v_all), lambda n: (rev(n), 1)),
                  pl.BlockSpec((per * c, v_all), lambda n: (rev(n), 0)),
                  pl.BlockSpec((per, RET_HEADS, RET_QK, RET_V), lambda n: (rev(n), 0, 0, 0)),
                  tab, tab, _full(intra.shape), _full(cross.shape), _full(inner.shape), ANY],
        out_specs=pl.BlockSpec((per * c, 2 * qk_all + v_all), lambda n: (rev(n), 0)),
        out_shape=jax.ShapeDtypeStruct((t, RET_IN), BF16),
        scratch_shapes=[pltpu.VMEM((RET_HEADS, RET_QK, RET_V), F32)],
        input_output_aliases={10: 0},
        compiler_params=_params(1), name="ret_core_bwd")(pj, pj, pj, do, sprev, cos, sin, intra, cross, inner, dpj)


def ret_in_bwd(dpj, dres, x, g_pre, w_in, tm=512):
    t, d = x.shape
    nt = t // tm

    def body(dpj_ref, dres_ref, x_ref, g_ref, win_hbm, dx_ref, dg_ref, win_v, sem):
        @pl.when(pl.program_id(0) == 0)
        def _():
            cp = pltpu.make_async_copy(win_hbm, win_v, sem)
            cp.start()
            dg_ref[...] = jnp.zeros_like(dg_ref)
            cp.wait()

        g = g_ref[...]
        dh = jnp.zeros((tm, d), F32)
        for j in range(N_SHARD):
            dh = dh + _dot_nt(dpj_ref[:, j * RET_IN_SHARD:(j + 1) * RET_IN_SHARD], win_v[j])
        _, xh, rx = _rms_fwd(x_ref[...], g)
        dg_ref[...] += _colsum(dh * xh)
        dx_ref[...] = dres_ref[...] + _rms_bwd(xh, rx, g, dh)

    row = pl.BlockSpec((tm, d), lambda i: (i, 0))
    return pl.pallas_call(
        body, grid=(nt,),
        in_specs=[pl.BlockSpec((tm, RET_IN), lambda i: (i, 0)), row, row, _full((1, d)), ANY],
        out_specs=[row, _full((1, d))],
        out_shape=[jax.ShapeDtypeStruct((t, d), F32), jax.ShapeDtypeStruct((1, d), F32)],
        scratch_shapes=[pltpu.VMEM(w_in.shape, BF16), pltpu.SemaphoreType.DMA],
        compiler_params=_params(1), name="ret_in_bwd")(dpj, dres, x, g_pre, w_in)


_CHIP_FLIPS = ((1, 0), (0, 1), (1, 1))


def _flip(v, b):
    return 1 - v if b else v


_HBM = pl.BlockSpec(memory_space=pltpu.HBM)
_SEM = pl.BlockSpec(memory_space=pltpu.SEMAPHORE)
_EFFECT = pltpu.SideEffectType.DATAFLOW_SIDE_EFFECTING


def _chip_copies(mode, srcs, lands, send_sems, recv_sems):
    x, y, c = lax.axis_index("x"), lax.axis_index("y"), lax.axis_index("c")
    copies = []
    for t in range(len(lands)):
        if mode == "swap":
            copies.append(pltpu.make_async_remote_copy(
                src_ref=srcs[t], dst_ref=lands[t], send_sem=send_sems.at[t], recv_sem=recv_sems.at[t],
                device_id=(x, y, 1 - c), device_id_type=MESH))
            continue
        if mode == "everyone":
            for m in range(1, 8):
                bx, by, bc = (m >> 2) & 1, (m >> 1) & 1, m & 1
                copies.append(pltpu.make_async_remote_copy(
                    src_ref=srcs[t], dst_ref=lands[t].at[4 * x + 2 * y + c], send_sem=send_sems.at[7 * t + m - 1],
                    recv_sem=recv_sems.at[7 * t + m - 1], device_id=(_flip(x, bx), _flip(y, by), _flip(c, bc)),
                    device_id_type=MESH))
            continue
        for k, (bx, by) in enumerate(_CHIP_FLIPS):
            px, py = _flip(x, bx), _flip(y, by)
            target = (px, py, c)
            if mode == "gather":
                src, dst = srcs[t], lands[t].at[2 * x + y]
            elif mode == "gather_half":
                half = pl.ds(c * (srcs[t].shape[0] // 2), srcs[t].shape[0] // 2)
                src, dst = srcs[t].at[half], lands[t].at[2 * x + y, half]
            elif mode == "forward_half":
                half = pl.ds(c * (lands[t].shape[1] // 2), lands[t].shape[1] // 2)
                src = dst = lands[t].at[2 * px + py, half]
                target = (x, y, 1 - c)
            else:
                src, dst = srcs[t].at[2 * px + py], lands[t].at[k]
            copies.append(pltpu.make_async_remote_copy(
                src_ref=src, dst_ref=dst, send_sem=send_sems.at[3 * t + k], recv_sem=recv_sems.at[3 * t + k],
                device_id=target, device_id_type=MESH))
    return copies


def exchange_start(mode, srcs, lands, name, after=None):
    n, ns = len(lands), len(srcs)
    extra = [] if after is None else [after]

    def body(*refs):
        ins, lnd = refs[:ns], refs[ns:ns + n]
        send_sems, recv_sems = refs[ns + n + len(extra)], refs[ns + n + len(extra) + 1]
        token = refs[-1]
        for cp in _chip_copies(mode, ins, lnd, send_sems, recv_sems):
            cp.start()
        token[...] = jnp.zeros(token.shape, token.dtype)

    hbm = lambda a: pltpu.with_memory_space_constraint(a, pltpu.HBM)
    passed = list(srcs) + list(lands)
    n_sem = {"swap": 1, "everyone": 7}.get(mode, 3) * n
    return pl.pallas_call(
        body, name=name,
        out_shape=(pltpu.SemaphoreType.DMA((n_sem,)), pltpu.SemaphoreType.DMA((n_sem,)),
                   *[pltpu.HBM(a.shape, a.dtype) for a in passed], jax.ShapeDtypeStruct((8, 128), F32)),
        in_specs=[_HBM] * (ns + n) + [ANY] * len(extra),
        out_specs=(_SEM, _SEM, *[_HBM] * (ns + n), pl.BlockSpec(memory_space=pltpu.VMEM)),
        input_output_aliases={i: 2 + i for i in range(ns + n)},
        compiler_params=pltpu.CompilerParams(has_side_effects=_EFFECT))(*[hbm(a) for a in passed], *extra)


def exchange_wait(mode, started, after, name):
    send_sems, recv_sems = started[0], started[1]
    passed = list(started[2:-1])
    n = len(passed) if mode == "forward_half" else len(passed) // 2
    ns = len(passed) - n
    after = list(after) if isinstance(after, (list, tuple)) else [after]

    def body(*refs):
        ins, lnd = refs[:ns], refs[ns:ns + n]
        for cp in _chip_copies(mode, ins, lnd, refs[ns + n], refs[ns + n + 1]):
            cp.wait_send()
            cp.wait_recv()

    outs = pl.pallas_call(
        body, name=name, out_shape=tuple(pltpu.HBM(a.shape, a.dtype) for a in passed),
        in_specs=[_HBM] * (ns + n) + [_SEM, _SEM] + [ANY] * len(after), out_specs=tuple([_HBM] * (ns + n)),
        input_output_aliases={i: i for i in range(ns + n)},
        compiler_params=pltpu.CompilerParams(has_side_effects=_EFFECT))(*passed, send_sems, recv_sems, *after)
    return list(outs[:ns]), list(outs[ns:])


def plane_sum(slot, full, recv, name, bm=256):
    _, m, n = full.shape
    bm = _row_block(m, bm)

    def body(slot_ref, o_ref, r_ref, s_ref):
        s_ref[...] = ((o_ref[...] + r_ref[0].astype(F32)) + r_ref[1].astype(F32)) + r_ref[2].astype(F32)

    return pl.pallas_call(
        body,
        grid_spec=pltpu.PrefetchScalarGridSpec(
            num_scalar_prefetch=1, grid=(m // bm,),
            in_specs=[pl.BlockSpec((None, bm, n), lambda i, s: (s[0], i, 0)),
                      pl.BlockSpec((3, bm, n), lambda i, s: (0, i, 0))],
            out_specs=pl.BlockSpec((bm, n), lambda i, s: (i, 0))),
        out_shape=jax.ShapeDtypeStruct((m, n), F32), compiler_params=_params(1), name=name)(slot, full, recv)


def sum_slots(parts, name, bm=312):
    _, r, n = parts.shape
    bm = bm if r % bm == 0 else r

    def body(p_ref, s_ref):
        acc = p_ref[0]
        for k in range(1, 8):
            acc = acc + p_ref[k]
        s_ref[...] = acc

    return pl.pallas_call(
        body, grid=(r // bm,), in_specs=[pl.BlockSpec((8, bm, n), lambda i: (0, i, 0))],
        out_specs=pl.BlockSpec((bm, n), lambda i: (i, 0)), out_shape=jax.ShapeDtypeStruct((r, n), F32),
        compiler_params=_params(1), name=name)(parts)


def _adamw_math(w, g, m, v):
    m = ADAM_B1 * m + (1.0 - ADAM_B1) * g
    v = ADAM_B2 * v + (1.0 - ADAM_B2) * (g * g)
    m_hat = m / (1.0 - ADAM_B1 ** ADAM_STEP)
    v_hat = v / (1.0 - ADAM_B2 ** ADAM_STEP)
    delta = -ADAM_LR * (m_hat / (jnp.sqrt(v_hat) + ADAM_EPS) + ADAM_WD * w)
    return delta, m, v


def adamw(w, m, v, grads, layer, prev, name, bm=256):
    _, _, n = w.shape
    mm = grads[0].shape[0]
    bm = _row_block(mm, bm)
    ng = len(grads)

    def body(*refs):
        w_ref, m_ref, v_ref = refs[:3]
        g_refs = refs[3:3 + ng]
        g_out, d_out, m_out, v_out = refs[-4:]
        g = g_refs[0][...]
        for gr in g_refs[1:]:
            g = g + gr[...]
        delta, mn, vn = _adamw_math(w_ref[...], g, m_ref[...], v_ref[...])
        g_out[...] = g
        d_out[...] = delta
        m_out[...] = mn
        v_out[...] = vn

    slab = pl.BlockSpec((None, bm, n), lambda i: (layer, i, 0))
    flat = pl.BlockSpec((bm, n), lambda i: (i, 0))
    in_specs = [slab] * 3 + [flat] * ng
    args = [w, m, v, *grads]
    aliases = {}
    if prev is not None:
        in_specs += [ANY] * 4
        aliases = {3 + ng + q: q for q in range(4)}
        args += list(prev)
    return pl.pallas_call(
        body, grid=(mm // bm,), in_specs=in_specs, out_specs=[slab] * 4,
        out_shape=[jax.ShapeDtypeStruct(w.shape, F32)] * 4, input_output_aliases=aliases,
        compiler_params=_params(1), name=name)(*args)


def _pack_rows(parts, rows):
    flat = jnp.concatenate([p.reshape(-1) for p in parts])
    return jnp.pad(flat, (0, rows * 128 - flat.shape[0])).reshape(rows, 128)


def _as_shards(a, rows):
    return a.reshape(N_SHARD, rows, a.shape[-1])


def _local_step(x, pos_col, target, gains, pool_w, pool_scale, gn_gain, conv_w, conv_b, weights, send_grads):
    def gain(l, n, token=None):
        g = gains[l, n].reshape(1, D_MODEL)
        return g if token is None else g + token[0:1, 0:1]

    inv_freq = (ROPE_BASE ** (-jnp.arange(0, RET_QK, 2, dtype=F32) / RET_QK)).reshape(1, RET_QK // 2)
    intra, cross, inner, chunk_decay = _decay_tables()
    dn_rows = D_FF // N_SHARD

    x1 = pool_fwd(x, gain(0, 0), gain(0, 1), pool_w, pool_scale)
    cos, sin = rope_tables(pos_col, inv_freq, x1)
    w_up0, w_dn0 = weights("mlp0", cos)
    w_dn0 = w_dn0.reshape(D_FF, D_MODEL)
    x2, u0, s0, a0, f0, h0 = mlp_fwd(x1, gain(0, 2), gain(0, 3), w_up0, w_dn0, conv_w[0], conv_b[0])
    w_in, w_out = weights("ret", x2)
    w_out = w_out.reshape(RET_HEADS * RET_V, D_MODEL)
    pj, hr = ret_proj(x2, gain(1, 0), w_in, cos, sin)
    o, sprev = ret_core_fwd(pj, intra, cross, inner, chunk_decay)
    x3, yb, out = ret_out_fwd(o, pj, x2, gn_gain, gain(1, 1), w_out)
    w_up1, w_dn1 = weights("mlp1", x3)
    w_dn1 = w_dn1.reshape(D_FF, D_MODEL)
    dx4, u1, s1, a1, f1, h1, loss = mlp_fwd(x3, gain(1, 2), gain(1, 3), w_up1, w_dn1, conv_w[1], conv_b[1], target)

    dx3, du1, df1, dg12, dg13, dcw1, dcb1 = mlp_bwd(
        dx4, f1, x3, u1, s1, gain(1, 2), gain(1, 3), w_up1, w_dn1, conv_w[1])
    dwup1 = grad_matmul(h1, du1, D_MODEL, FF_CHUNK, "grad_w_up_1", interleaved=True)
    dwdn1 = grad_matmul(a1, df1, FF_CHUNK, D_MODEL, "grad_w_down_1")
    tok = send_grads("mlp1", [dwup1, [_as_shards(g, dn_rows) for g in dwdn1]])
    dout, dpj, do, dg11, dgn = ret_out_bwd(dx3, out, o, pj, gn_gain, gain(1, 1, tok), w_out)
    dwout = grad_matmul(yb, dout, 1024, D_MODEL, "grad_w_out")
    dpj = ret_core_bwd(pj, do, sprev, cos, sin, dpj, intra, cross, inner, chunk_decay)
    dwin = grad_matmul(hr, dpj, D_MODEL, RET_IN_SHARD, "grad_w_in")
    tok = send_grads("ret", [dwin, [_as_shards(g, RET_V) for g in dwout]])
    dx2, dg10 = ret_in_bwd(dpj, dx3, x2, gain(1, 0, tok), w_in)
    dx1, du0, df0, dg02, dg03, dcw0, dcb0 = mlp_bwd(
        dx2, f0, x1, u0, s0, gain(0, 2), gain(0, 3), w_up0, w_dn0, conv_w[0])
    dwdn0 = grad_matmul(a0, df0, FF_CHUNK, D_MODEL, "grad_w_down_0")
    tok = send_grads("down0", [[_as_shards(g, dn_rows) for g in dwdn0]])
    dwup0 = grad_matmul(h0, du0, D_MODEL, FF_CHUNK, "grad_w_up_0", interleaved=True, after=tok)
    tok = send_grads("up0", [dwup0])
    dx0, dg00, dg01, dpscale, dpw = pool_bwd(dx1, x, gain(0, 0, tok), gain(0, 1), pool_w, pool_scale)

    rows = lambda g: jnp.sum(g, axis=0, keepdims=True)
    dgains = jnp.concatenate([dg00, dg01, rows(dg02), rows(dg03), dg10, dg11, rows(dg12), rows(dg13)],
                             axis=0).reshape(2, 4, D_MODEL)
    small = {"gains": dgains, "pool_scale": dpscale, "gn": dgn,
             "conv_w": jnp.sum(jnp.stack([dcw0, dcw1]), axis=3),
             "conv_b": jnp.sum(jnp.stack([dcb0, dcb1]), axis=2, keepdims=True), "pool_w": dpw}
    return loss, dx0, small


def kernel(x, positions, norm_gain, pool_w, pool_scale, ret_w_in, ret_gn_gain, ret_w_out, mlp_w_up, mlp_conv_w, mlp_conv_b, mlp_w_down, loss_target, m_norm_gain, m_pool_w, m_pool_scale, m_ret_w_in, m_ret_gn_gain, m_ret_w_out, m_mlp_w_up, m_mlp_conv_w, m_mlp_conv_b, m_mlp_w_down, v_norm_gain, v_pool_w, v_pool_scale, v_ret_w_in, v_ret_gn_gain, v_ret_w_out, v_mlp_w_up, v_mlp_conv_w, v_mlp_conv_b, v_mlp_w_down):
    t = x.shape[1]
    me = 2 * lax.axis_index("x") + lax.axis_index("y")
    me_slot = jnp.reshape(me, (1,)).astype(jnp.int32)

    small_parts = [norm_gain, ret_gn_gain, mlp_conv_w, pool_w]
    small_sizes = [p.size for p in small_parts]
    small_rows = -(-sum(small_sizes) // (128 * 8)) * 8
    gathers = {}

    def start_gather(group, srcs, after):
        lands = [lax.dynamic_update_index_in_dim(lax.empty((N_SHARD,) + s.shape, s.dtype), s, me, 0) for s in srcs]
        mode = "gather_half" if group == "mlp0" else "gather"
        gathers[group] = (mode, exchange_start(mode, srcs, lands, "gather_start_" + group, after=after))
        return gathers[group][1][-1]

    token = start_gather("small", [_pack_rows(small_parts, small_rows)], None)
    token = start_gather("mlp0", [mlp_w_up[0].astype(BF16), mlp_w_down[0].astype(BF16)], token)

    def weights(group, after):
        if group == "mlp0":
            tok = start_gather("ret", [ret_w_in[0].astype(BF16), ret_w_out[0].astype(BF16)], after)
            after = start_gather("mlp1", [mlp_w_up[1].astype(BF16), mlp_w_down[1].astype(BF16)], tok)
        mode, started = gathers[group]
        _, lands = exchange_wait(mode, started, after, "gather_wait_" + group)
        if mode == "gather_half":
            forward = exchange_start("forward_half", [], lands, "forward_start_" + group)
            _, lands = exchange_wait("forward_half", forward, forward[-1], "forward_wait_" + group)
        return lands

    sent, early = {}, {}

    def reduced(group, after, names):
        started, own = sent[group]
        _, recv = exchange_wait("scatter", started, after, "scatter_wait_" + group)
        return [plane_sum(me_slot, f, r, "plane_sum_" + nm)
                for f, r, nm in zip(own, recv, names)]

    def swap_start(planes, name):
        return exchange_start("swap", planes, [lax.empty(p.shape, p.dtype) for p in planes], name)

    def send_grads(group, pairs):
        lands = [lax.empty((3,) + b.shape[1:], BF16) for _, b in pairs]
        sent[group] = (exchange_start("scatter", [b for _, b in pairs], lands, "scatter_start_" + group),
                       [f for f, _ in pairs])
        token = sent[group][0][-1]
        if group == "down0":
            marker = pairs[0][1]
            early["planes"] = (reduced("mlp1", marker, ["w_up_1", "w_down_1"])
                               + reduced("ret", marker, ["w_in", "w_out"]))
            early["swap"] = swap_start(early["planes"], "swap_start_a")
            token = token + early["swap"][-1]
        return token

    (smallg,) = weights("small", token)
    smallg = smallg.reshape(N_SHARD, -1)
    offs = np.cumsum([0] + small_sizes)
    piece = lambda i, shape: smallg[:, offs[i]:offs[i + 1]].reshape((N_SHARD,) + shape)
    gains = piece(0, (2, 4, 256)).transpose(1, 2, 0, 3).reshape(2, 4, D_MODEL)
    gn_full = piece(1, (512,)).reshape(1, RET_HEADS * RET_V)
    cw_full = piece(2, (2, 3, FF_CHUNK)).transpose(1, 0, 2, 3)
    pw_full = piece(3, (4, 64, 256)).transpose(1, 0, 2, 3).reshape(4, 256, 256).astype(BF16)
    cb_full = mlp_conv_b.reshape(2, N_SHARD, 1, FF_CHUNK)

    loss, dx0, small = _local_step(
        x[0], positions.reshape(t, 1).astype(F32), loss_target[0], gains, pw_full, pool_scale, gn_full,
        cw_full, cb_full, weights, send_grads)

    def small_adamw(w, m, v, grads, name):
        w3 = w.reshape(1, -1, w.shape[-1])
        out = adamw(w3, m.reshape(w3.shape), v.reshape(w3.shape), [g.reshape(w3.shape[1:]) for g in grads], 0, None, name)
        return [o.reshape(w.shape) for o in out]

    pw_f = small["pool_w"].reshape(4, N_SHARD, 64, 256).transpose(1, 0, 2, 3).reshape(N_SHARD, 256, 256)
    small = dict(small, loss=loss)
    small_order = ["gains", "pool_scale", "gn", "conv_w", "conv_b", "loss"]
    gsmall_sizes = [small[k].size for k in small_order]
    gsmall_rows = -(-sum(gsmall_sizes) // (128 * 8)) * 8
    gpack = _pack_rows([small[k] for k in small_order], gsmall_rows)
    mine = 2 * me + lax.axis_index("c")
    small_started = exchange_start(
        "everyone", [gpack], [lax.dynamic_update_index_in_dim(lax.empty((8,) + gpack.shape, F32), gpack, mine, 0)],
        "small_start")
    send_grads("pool_w", [(pw_f, pw_f.astype(BF16))])

    res = {}
    planes_a, others_a = exchange_wait("swap", early["swap"], small_started[-1], "swap_wait_a")
    res["ret_w_in"] = adamw(ret_w_in, m_ret_w_in, v_ret_w_in, (planes_a[2], others_a[2]), 0, None, "adamw_w_in")
    res["ret_w_out"] = adamw(ret_w_out, m_ret_w_out, v_ret_w_out, (planes_a[3], others_a[3]), 0, None, "adamw_w_out")
    up1 = adamw(mlp_w_up, m_mlp_w_up, v_mlp_w_up, (planes_a[0], others_a[0]), 1, None, "adamw_w_up_1")
    dn1 = adamw(mlp_w_down, m_mlp_w_down, v_mlp_w_down, (planes_a[1], others_a[1]), 1, None, "adamw_w_down_1")

    done_a = [res["ret_w_in"][0], res["ret_w_out"][0], up1[0], dn1[0]]
    planes_b = (reduced("up0", done_a, ["w_up_0"]) + reduced("down0", done_a, ["w_down_0"])
                + reduced("pool_w", done_a, ["pool_w"]))
    swap_b = swap_start(planes_b, "swap_start_b")

    _, (small_recv,) = exchange_wait("everyone", small_started, swap_b[-1], "small_wait")
    gsmall = sum_slots(small_recv, "sum_small").reshape(-1)
    goffs = np.cumsum([0] + gsmall_sizes)
    gpiece = lambda i: gsmall[goffs[i]:goffs[i + 1]].reshape(small[small_order[i]].shape)
    g_gains = lax.dynamic_slice_in_dim(gpiece(0), me * 256, 256, axis=2)
    g_gn = lax.dynamic_slice_in_dim(gpiece(2), me * RET_V, RET_V, axis=1)
    g_cw = lax.dynamic_index_in_dim(gpiece(3), me, 1, keepdims=False)
    res["norm_gain"] = small_adamw(norm_gain, m_norm_gain, v_norm_gain, [g_gains], "adamw_norm_gain")
    res["pool_scale"] = small_adamw(pool_scale, m_pool_scale, v_pool_scale, [gpiece(1)], "adamw_pool_scale")
    res["ret_gn_gain"] = small_adamw(ret_gn_gain, m_ret_gn_gain, v_ret_gn_gain, [g_gn], "adamw_gn_gain")
    res["mlp_conv_w"] = small_adamw(mlp_conv_w, m_mlp_conv_w, v_mlp_conv_w, [g_cw], "adamw_conv_w")
    res["mlp_conv_b"] = small_adamw(mlp_conv_b, m_mlp_conv_b, v_mlp_conv_b, [gpiece(4)], "adamw_conv_b")

    planes_b, others_b = exchange_wait("swap", swap_b, res["mlp_conv_b"][0], "swap_wait_b")
    res["mlp_w_up"] = adamw(mlp_w_up, m_mlp_w_up, v_mlp_w_up, (planes_b[0], others_b[0]), 0, up1, "adamw_w_up_0")
    res["mlp_w_down"] = adamw(mlp_w_down, m_mlp_w_down, v_mlp_w_down, (planes_b[1], others_b[1]), 0, dn1,
                              "adamw_w_down_0")
    res["pool_w"] = small_adamw(pool_w, m_pool_w, v_pool_w, (planes_b[2], others_b[2]), "adamw_pool_w")

    order = ["norm_gain", "pool_w", "pool_scale", "ret_w_in", "ret_gn_gain", "ret_w_out", "mlp_w_up", "mlp_conv_w",
             "mlp_conv_b", "mlp_w_down"]
    outs = [gpiece(5)[0, 0], dx0.reshape(x.shape)]
    for q in range(4):
        outs += [res[k][q] for k in order]
    return tuple(outs)
```

```python
import numpy as np
import jax
import jax.numpy as jnp
from jax import lax
from jax.experimental import pallas as pl
from jax.experimental.pallas import tpu as pltpu

F32 = jnp.float32
BF16 = jnp.bfloat16

D_MODEL = 1024
D_FF = 2816
FF_CHUNK = 1408
N_SHARD = 4
POOL_WINDOWS = (2, 4, 8, 16)
POOL_DIM = 256
POOL_HALO = 16
RET_HEADS = 4
RET_QK = 256
RET_V = 512
RET_CHUNK = 256
RET_STEP_CHUNKS = 2
RET_IN = 6144
RET_IN_SHARD = 1536
ROPE_BASE = 10000.0
EPS = 1e-6
CONV_HALO = 8

ADAM_LR, ADAM_B1, ADAM_B2, ADAM_EPS, ADAM_WD, ADAM_STEP = 0.001, 0.9, 0.999, 1e-08, 0.01, 10

VMEM_LIMIT = 56 * 1024 * 1024
VMEM_LIMIT_MLP_BWD = 62 * 1024 * 1024
MESH = pl.DeviceIdType.MESH
ANY = pl.BlockSpec(memory_space=pl.ANY)


def _params(n_grid=1, limit=VMEM_LIMIT):
    return pltpu.CompilerParams(dimension_semantics=("arbitrary",) * n_grid, vmem_limit_bytes=limit)


def _dot(a, b):
    return jnp.dot(a, b, preferred_element_type=F32)


def _dot_nt(a, b):
    return lax.dot_general(a, b, (((1,), (1,)), ((), ())), preferred_element_type=F32)


def _dot_tn(a, b):
    return lax.dot_general(a, b, (((0,), (0,)), ((), ())), preferred_element_type=F32)


def _rms_fwd(x, gain):
    r = lax.rsqrt(jnp.mean(x * x, axis=-1, keepdims=True) + EPS)
    xh = x * r
    return xh * gain, xh, r


def _rms_bwd(xh, r, gain, dy):
    dxh = dy * gain
    return r * (dxh - xh * jnp.mean(dxh * xh, axis=-1, keepdims=True))


def _colsum(v):
    return jnp.sum(v, axis=0, keepdims=True)


def _full(shape):
    nd = len(shape)
    return pl.BlockSpec(shape, lambda *_: (0,) * nd)


def rope_tables(pos_col, inv_freq, after):
    t = pos_col.shape[0]
    tm = min(t, 1024)

    def body(p_ref, f_ref, after_ref, c_ref, s_ref):
        ang = p_ref[...] * f_ref[...]
        c_ref[...] = jnp.cos(ang)
        s_ref[...] = jnp.sin(ang)

    return pl.pallas_call(
        body, grid=(t // tm,),
        in_specs=[pl.BlockSpec((tm, 1), lambda i: (i, 0)), _full((1, 128)), ANY],
        out_specs=[pl.BlockSpec((tm, 128), lambda i: (i, 0))] * 2,
        out_shape=[jax.ShapeDtypeStruct((t, 128), F32)] * 2,
        compiler_params=_params(1), name="rope_tables")(pos_col, inv_freq, after)


def _window_sums(ext, backward):
    n = ext.shape[0]
    cur, sums = ext, []
    for g, win in enumerate(POOL_WINDOWS):
        if g > 0:
            cur = cur[:, POOL_DIM:]
        half = win // 2
        cur = cur + pltpu.roll(cur, n - half if backward else half, axis=0)
        sums.append(cur[:, 0:POOL_DIM])
    return sums


def _pool_diff(h_halo, h, row0, tm):
    t_idx = row0 + lax.broadcasted_iota(jnp.int32, (tm, 1), 0)
    sums = _window_sums(jnp.concatenate([h_halo, h], axis=0), backward=False)
    parts, inv_counts = [], []
    for g, win in enumerate(POOL_WINDOWS):
        inv = 1.0 / jnp.minimum(t_idx + 1, win).astype(F32)
        parts.append(sums[g][POOL_HALO:, :] * inv - h[:, g * POOL_DIM:(g + 1) * POOL_DIM])
        inv_counts.append(inv)
    return parts, inv_counts


def pool_fwd(x, g_pre, g_post, pool_w, pool_scale, tm=512):
    t, d = x.shape
    nt = t // tm

    def body(x_ref, g0_ref, g1_ref, w_ref, sc_ref, o_ref, hext):
        i = pl.program_id(0)

        @pl.when(i == 0)
        def _():
            hext[...] = jnp.zeros((POOL_HALO, d), F32)

        xv = x_ref[...]
        h, _, _ = _rms_fwd(xv, g0_ref[...])
        parts, _ = _pool_diff(hext[...], h, i * tm, tm)
        hext[...] = h[tm - POOL_HALO:tm, :]
        ys = [_dot(parts[g].astype(BF16), w_ref[g]) for g in range(len(POOL_WINDOWS))]
        y = jnp.concatenate(ys, axis=-1) * sc_ref[...]
        m, _, _ = _rms_fwd(y, g1_ref[...])
        o_ref[...] = xv + m

    row = pl.BlockSpec((tm, d), lambda i: (i, 0))
    return pl.pallas_call(
        body, grid=(nt,),
        in_specs=[row, _full((1, d)), _full((1, d)), _full(pool_w.shape), _full((1, d))],
        out_specs=row, out_shape=jax.ShapeDtypeStruct((t, d), F32),
        scratch_shapes=[pltpu.VMEM((POOL_HALO, d), F32)],
        compiler_params=_params(1), name="pool_fwd")(x, g_pre, g_post, pool_w, pool_scale)


def pool_bwd(dx1, x, g_pre, g_post, pool_w, pool_scale, tm=512):
    t, d = x.shape
    nt = t // tm
    ng = len(POOL_WINDOWS)

    def body(dx1_ref, x_ref, xh_ref, g0_ref, g1_ref, w_ref, sc_ref,
             dx_ref, dg0_ref, dg1_ref, dsc_ref, dw_ref, enext):
        i = pl.program_id(0)
        r = nt - 1 - i

        @pl.when(i == 0)
        def _():
            enext[...] = jnp.zeros((POOL_HALO, d), F32)
            dg0_ref[...] = jnp.zeros_like(dg0_ref)
            dg1_ref[...] = jnp.zeros_like(dg1_ref)
            dsc_ref[...] = jnp.zeros_like(dsc_ref)
            dw_ref[...] = jnp.zeros_like(dw_ref)

        g0 = g0_ref[...]
        g1 = g1_ref[...]
        sc = sc_ref[...]
        xv = x_ref[...]
        h, xh, rx = _rms_fwd(xv, g0)
        h_halo, _, _ = _rms_fwd(xh_ref[...], g0)
        parts, inv_counts = _pool_diff(h_halo * jnp.where(r > 0, 1.0, 0.0), h, r * tm, tm)
        parts_b = [p.astype(BF16) for p in parts]
        ypre = jnp.concatenate([_dot(parts_b[g], w_ref[g]) for g in range(ng)], axis=-1)
        _, yh, ry = _rms_fwd(ypre * sc, g1)
        dm = dx1_ref[...]
        dg1_ref[...] += _colsum(dm * yh)
        dy = _rms_bwd(yh, ry, g1, dm)
        dsc_ref[...] += _colsum(dy * ypre)
        dyp = (dy * sc).astype(BF16)
        ddiffs = []
        for g in range(ng):
            cols = slice(g * POOL_DIM, (g + 1) * POOL_DIM)
            dw_ref[g] += _dot_tn(parts_b[g], dyp[:, cols])
            ddiffs.append(_dot_nt(dyp[:, cols], w_ref[g]))
        e = jnp.concatenate([ddiffs[g] * inv_counts[g] for g in range(ng)], axis=-1)
        sums = _window_sums(jnp.concatenate([e, enext[...]], axis=0), backward=True)
        enext[...] = e[0:POOL_HALO, :]
        dh = jnp.concatenate([sums[g][0:tm, :] - ddiffs[g] for g in range(ng)], axis=-1)
        dg0_ref[...] += _colsum(dh * xh)
        dx_ref[...] = dm + _rms_bwd(xh, rx, g0, dh)

    row = pl.BlockSpec((tm, d), lambda i: (nt - 1 - i, 0))
    halo = pl.BlockSpec((POOL_HALO, d), lambda i: (jnp.maximum((nt - 1 - i) * (tm // POOL_HALO) - 1, 0), 0))
    vec = _full((1, d))
    return pl.pallas_call(
        body, grid=(nt,),
        in_specs=[row, row, halo, vec, vec, _full(pool_w.shape), vec],
        out_specs=[row, vec, vec, vec, _full((ng, POOL_DIM, POOL_DIM))],
        out_shape=[jax.ShapeDtypeStruct((t, d), F32)] + [jax.ShapeDtypeStruct((1, d), F32)] * 3
        + [jax.ShapeDtypeStruct((ng, POOL_DIM, POOL_DIM), F32)],
        scratch_shapes=[pltpu.VMEM((POOL_HALO, d), F32)],
        compiler_params=_params(1), name="pool_bwd")(dx1, x, x, g_pre, g_post, pool_w, pool_scale)


def _conv_taps(cw_ref, j):
    return cw_ref[j, 0:1, :], cw_ref[j, 1:2, :], cw_ref[j, 2:3, :]


def _row_block(m, target=256):
    if m <= target:
        return m
    for b in range(target, 7, -8):
        if m % b == 0:
            return b
    return m


def mlp_fwd(x, g_pre, g_post, w_up, w_down, conv_w, conv_b, target=None, tm=256):
    t, d = x.shape
    nt = t // tm
    h8 = CONV_HALO
    with_loss = target is not None
    n_extra = 1 if with_loss else 0

    def body(x_ref, g2_ref, g3_ref, wup_hbm, wdn_hbm, cw_ref, cb_ref, *rest):
        tgt_ref = rest[0] if with_loss else None
        xo_ref, u_ref, s_ref, a_ref, f_ref, h_ref = rest[n_extra:n_extra + 6]
        loss_ref = rest[n_extra + 6] if with_loss else None
        wup_v, wdn_v, tail, sem = rest[-4:]
        i = pl.program_id(0)

        @pl.when(i == 0)
        def _():
            c1 = pltpu.make_async_copy(wup_hbm, wup_v, sem.at[0])
            c2 = pltpu.make_async_copy(wdn_hbm, wdn_v, sem.at[1])
            c1.start()
            c2.start()
            tail[...] = jnp.zeros_like(tail)
            if with_loss:
                loss_ref[...] = jnp.zeros_like(loss_ref)
            c1.wait()
            c2.wait()

        xv = x_ref[...]
        h, _, _ = _rms_fwd(xv, g2_ref[...])
        hb = h.astype(BF16)
        h_ref[...] = hb
        acc = jnp.zeros((tm, d), F32)
        for k in range(2):
            cs = []
            for s in range(2):
                j, cols = k + 2 * s, slice((2 * k + s) * FF_CHUNK, (2 * k + s + 1) * FF_CHUNK)
                uf = _dot(hb, wup_v[j])
                u_ref[:, cols] = uf.astype(BF16)
                ext = jnp.concatenate([tail[j], uf], axis=0)
                tail[j] = uf[tm - h8:tm, :]
                w0, w1, w2 = _conv_taps(cw_ref, j)
                cs.append(cb_ref[j] + w2 * uf + w1 * pltpu.roll(ext, 1, axis=0)[h8:, :]
                          + w0 * pltpu.roll(ext, 2, axis=0)[h8:, :])
            cg, cv = cs
            sg = jax.nn.sigmoid(cg)
            sil = cg * sg
            ab = (sil * cv).astype(BF16)
            a_ref[:, k * FF_CHUNK:(k + 1) * FF_CHUNK] = ab
            s_ref[:, 2 * k * FF_CHUNK:(2 * k + 1) * FF_CHUNK] = sil.astype(BF16)
            s_ref[:, (2 * k + 1) * FF_CHUNK:(2 * k + 2) * FF_CHUNK] = (cv * (sg + sil * (1.0 - sg))).astype(BF16)
            acc = acc + _dot(ab, wdn_v[k * FF_CHUNK:(k + 1) * FF_CHUNK, :])
        f_ref[...] = acc
        y, _, _ = _rms_fwd(acc, g3_ref[...])
        if with_loss:
            err = (xv + y) - tgt_ref[...]
            xo_ref[...] = err * (1.0 / d)
            loss_ref[...] += 0.5 * jnp.sum(jnp.mean(err * err, axis=-1, keepdims=True), axis=0, keepdims=True)
        else:
            xo_ref[...] = xv + y

    row = pl.BlockSpec((tm, d), lambda i: (i, 0))
    wide = pl.BlockSpec((tm, 2 * D_FF), lambda i: (i, 0))
    vec = _full((1, d))
    extra = [target] if with_loss else []
    return pl.pallas_call(
        body, grid=(nt,),
        in_specs=[row, vec, vec, ANY, ANY, _full(conv_w.shape), _full(conv_b.shape)] + [row] * n_extra,
        out_specs=[row, wide, wide, pl.BlockSpec((tm, D_FF), lambda i: (i, 0)), row, row] + [_full((1, 1))] * n_extra,
        out_shape=[jax.ShapeDtypeStruct((t, d), F32), jax.ShapeDtypeStruct((t, 2 * D_FF), BF16),
                   jax.ShapeDtypeStruct((t, 2 * D_FF), BF16), jax.ShapeDtypeStruct((t, D_FF), BF16),
                   jax.ShapeDtypeStruct((t, d), F32), jax.ShapeDtypeStruct((t, d), BF16)]
        + [jax.ShapeDtypeStruct((1, 1), F32)] * n_extra,
        scratch_shapes=[pltpu.VMEM(w_up.shape, BF16), pltpu.VMEM(w_down.shape, BF16),
                        pltpu.VMEM((N_SHARD, h8, FF_CHUNK), F32), pltpu.SemaphoreType.DMA((2,))],
        compiler_params=_params(1), name="mlp_fwd_loss" if with_loss else "mlp_fwd")(
            x, g_pre, g_post, w_up, w_down, conv_w, conv_b, *extra)


def _rowsum8(v):
    return jnp.sum(v.reshape(v.shape[0] // 8, 8, v.shape[1]), axis=0)


def mlp_bwd(dxo, f, x, u, sp, g_pre, g_post, w_up, w_down, conv_w, tm=256):
    t, d = x.shape
    nt = t // tm
    h8 = CONV_HALO

    def body(dxo_ref, f_ref, x_ref, u_ref, s_ref, g2_ref, g3_ref, wup_hbm, wdn_hbm, cw_ref,
             dx_ref, du_ref, df_ref, dg2_ref, dg3_ref, dcw_ref, dcb_ref,
             wup_v, wdn_v, carry, sem):
        @pl.when(pl.program_id(0) == 0)
        def _():
            c1 = pltpu.make_async_copy(wup_hbm, wup_v, sem.at[0])
            c2 = pltpu.make_async_copy(wdn_hbm, wdn_v, sem.at[1])
            c1.start()
            c2.start()
            carry[...] = jnp.zeros_like(carry)
            dg2_ref[...] = jnp.zeros_like(dg2_ref)
            dg3_ref[...] = jnp.zeros_like(dg3_ref)
            dcw_ref[...] = jnp.zeros_like(dcw_ref)
            dcb_ref[...] = jnp.zeros_like(dcb_ref)
            c1.wait()
            c2.wait()

        g3 = g3_ref[...]
        dxo = dxo_ref[...]
        _, fh, rf = _rms_fwd(f_ref[...], g3)
        dg3_ref[...] += _rowsum8(dxo * fh)
        dfb = _rms_bwd(fh, rf, g3, dxo).astype(BF16)
        df_ref[...] = dfb
        dh = jnp.zeros((tm, d), F32)
        for k in range(2):
            da = _dot_nt(dfb, wdn_v[k * FF_CHUNK:(k + 1) * FF_CHUNK, :])
            for s in range(2):
                j = k + 2 * s
                cols = slice((2 * k + s) * FF_CHUNK, (2 * k + s + 1) * FF_CHUNK)
                dc = da * s_ref[:, (2 * k + 1 - s) * FF_CHUNK:(2 * k + 2 - s) * FF_CHUNK].astype(F32)
                uf = u_ref[:, cols].astype(F32)
                ext = jnp.concatenate([dc, carry[j]], axis=0)
                carry[j] = dc[0:h8, :]
                dc1 = pltpu.roll(ext, tm + h8 - 1, axis=0)[0:tm, :]
                dc2 = pltpu.roll(ext, tm + h8 - 2, axis=0)[0:tm, :]
                dcb_ref[j] += _rowsum8(dc)
                dcw_ref[j, 2] += _rowsum8(dc * uf)
                dcw_ref[j, 1] += _rowsum8(dc1 * uf)
                dcw_ref[j, 0] += _rowsum8(dc2 * uf)
                dub = (cw_ref[j, 2:3, :] * dc + cw_ref[j, 1:2, :] * dc1 + cw_ref[j, 0:1, :] * dc2).astype(BF16)
                du_ref[:, cols] = dub
                dh = dh + _dot_nt(dub, wup_v[j])
        g2 = g2_ref[...]
        _, xh, rx = _rms_fwd(x_ref[...], g2)
        dg2_ref[...] += _rowsum8(dh * xh)
        dx_ref[...] = dxo + _rms_bwd(xh, rx, g2, dh)

    row = pl.BlockSpec((tm, d), lambda i: (nt - 1 - i, 0))
    wide = pl.BlockSpec((tm, 2 * D_FF), lambda i: (nt - 1 - i, 0))
    vec = _full((1, d))
    acc = _full((8, d))
    dcw_shape, dcb_shape = (N_SHARD, 3, 8, FF_CHUNK), (N_SHARD, 8, FF_CHUNK)
    return pl.pallas_call(
        body, grid=(nt,),
        in_specs=[row, row, row, wide, wide, vec, vec, ANY, ANY, _full(conv_w.shape)],
        out_specs=[row, wide, row, acc, acc, _full(dcw_shape), _full(dcb_shape)],
        out_shape=[jax.ShapeDtypeStruct((t, d), F32), jax.ShapeDtypeStruct((t, 2 * D_FF), BF16),
                   jax.ShapeDtypeStruct((t, d), BF16),
                   jax.ShapeDtypeStruct((8, d), F32), jax.ShapeDtypeStruct((8, d), F32),
                   jax.ShapeDtypeStruct(dcw_shape, F32), jax.ShapeDtypeStruct(dcb_shape, F32)],
        scratch_shapes=[pltpu.VMEM(w_up.shape, BF16), pltpu.VMEM(w_down.shape, BF16),
                        pltpu.VMEM((N_SHARD, h8, FF_CHUNK), F32), pltpu.SemaphoreType.DMA((2,))],
        compiler_params=_params(1, VMEM_LIMIT_MLP_BWD), name="mlp_bwd")(
            dxo, f, x, u, sp, g_pre, g_post, w_up, w_down, conv_w)


def grad_matmul(a, b, bm, bn, name, tk=2048, interleaved=False, after=None):
    t, m = a.shape
    n = b.shape[1]
    tk = min(tk, t)
    nk = t // tk
    place = (lambda j: (j % 2) * 2 + j // 2) if interleaved else (lambda j: j)
    extra = [] if after is None else [after]

    def body(a_ref, b_ref, *rest):
        o_ref, ob_ref = rest[len(extra):]
        kk = pl.program_id(2)

        @pl.when(kk == 0)
        def _():
            o_ref[...] = jnp.zeros_like(o_ref)

        o_ref[...] += _dot_tn(a_ref[...], b_ref[...])

        @pl.when(kk == nk - 1)
        def _():
            ob_ref[...] = o_ref[...].astype(BF16)

    ospec = pl.BlockSpec((None, bm, bn), lambda j, i, kk: (place(j), i, 0))
    return pl.pallas_call(
        body, grid=(n // bn, m // bm, nk),
        in_specs=[pl.BlockSpec((tk, bm), lambda j, i, kk: (kk, i)),
                  pl.BlockSpec((tk, bn), lambda j, i, kk: (kk, j))]
        + [ANY] * len(extra),
        out_specs=[ospec, ospec],
        out_shape=[jax.ShapeDtypeStruct((n // bn, m, bn), F32), jax.ShapeDtypeStruct((n // bn, m, bn), BF16)],
        compiler_params=_params(3), name=name)(a, b, *extra)


def _decay_tables():
    log_gamma = jnp.log(1.0 - 2.0 ** (-5.0 - jnp.arange(RET_HEADS, dtype=F32)))
    i = jnp.arange(RET_CHUNK, dtype=F32)
    rel = i[:, None] - i[None, :]
    intra = jnp.where(rel >= 0, jnp.exp(jnp.maximum(rel, 0.0) * log_gamma[:, None, None]), 0.0)
    cross = jnp.exp((i + 1.0) * log_gamma[:, None])[:, :, None]
    inner = jnp.exp((RET_CHUNK - 1.0 - i) * log_gamma[:, None])[:, :, None]
    chunk = [float(np.exp(np.float32(RET_CHUNK) * np.log(np.float32(1.0 - 2.0 ** (-5.0 - h))).astype(np.float32)))
             for h in range(RET_HEADS)]
    return intra, cross, inner, chunk


def ret_proj(x, g_pre, w_in, cos, sin, tm=512):
    t, d = x.shape
    nt = t // tm
    per = RET_IN_SHARD // RET_QK

    def body(x_ref, g_ref, win_hbm, c_ref, s_ref, pj_ref, h_ref, win_v, sem):
        @pl.when(pl.program_id(0) == 0)
        def _():
            cp = pltpu.make_async_copy(win_hbm, win_v, sem)
            cp.start()
            cp.wait()

        h, _, _ = _rms_fwd(x_ref[...], g_ref[...])
        hb = h.astype(BF16)
        h_ref[...] = hb
        c = c_ref[...]
        s = s_ref[...]
        for j in range(N_SHARD):
            pjj = _dot(hb, win_v[j])
            for bb in range(per):
                b = per * j + bb
                blk = pjj[:, bb * RET_QK:(bb + 1) * RET_QK]
                if b < 2 * RET_HEADS:
                    x1, x2 = blk[:, :128], blk[:, 128:]
                    o1 = x1 * c - x2 * s
                    o2 = x2 * c + x1 * s
                    if b < RET_HEADS:
                        o1 = o1 * (RET_QK ** -0.5)
                        o2 = o2 * (RET_QK ** -0.5)
                    pj_ref[:, b * RET_QK:b * RET_QK + 128] = o1.astype(BF16)
                    pj_ref[:, b * RET_QK + 128:(b + 1) * RET_QK] = o2.astype(BF16)
                else:
                    pj_ref[:, b * RET_QK:(b + 1) * RET_QK] = blk.astype(BF16)

    row = pl.BlockSpec((tm, d), lambda i: (i, 0))
    tab = pl.BlockSpec((tm, 128), lambda i: (i, 0))
    return pl.pallas_call(
        body, grid=(nt,),
        in_specs=[row, _full((1, d)), ANY, tab, tab],
        out_specs=[pl.BlockSpec((tm, RET_IN), lambda i: (i, 0)), row],
        out_shape=[jax.ShapeDtypeStruct((t, RET_IN), BF16), jax.ShapeDtypeStruct((t, d), BF16)],
        scratch_shapes=[pltpu.VMEM(w_in.shape, BF16), pltpu.SemaphoreType.DMA],
        compiler_params=_params(1), name="ret_proj")(x, g_pre, w_in, cos, sin)


def ret_core_fwd(pj, intra, cross, inner, chunk_decay):
    t = pj.shape[0]
    nc = t // RET_CHUNK
    c = RET_CHUNK
    per = RET_STEP_CHUNKS
    qk_all = RET_HEADS * RET_QK
    v_all = RET_HEADS * RET_V

    def body(q_ref, k_ref, v_ref, dm_ref, cr_ref, in_ref, o_ref, sp_ref, state):
        @pl.when(pl.program_id(0) == 0)
        def _():
            state[...] = jnp.zeros_like(state)

        for h in range(RET_HEADS):
            for cc in range(per):
                rows = slice(cc * c, (cc + 1) * c)
                q = q_ref[rows, h * RET_QK:(h + 1) * RET_QK]
                k = k_ref[rows, h * RET_QK:(h + 1) * RET_QK]
                v = v_ref[rows, h * RET_V:(h + 1) * RET_V]
                sb = state[h].astype(BF16)
                sp_ref[cc, h] = sb
                sc = _dot_nt(q, k) * dm_ref[h]
                o_ref[rows, h * RET_V:(h + 1) * RET_V] = (_dot(sc.astype(BF16), v)
                                                          + _dot(q, sb) * cr_ref[h]).astype(BF16)
                kd = (k.astype(F32) * in_ref[h]).astype(BF16)
                state[h] = state[h] * chunk_decay[h] + _dot_tn(kd, v)

    return pl.pallas_call(
        body, grid=(nc // per,),
        in_specs=[pl.BlockSpec((per * c, qk_all), lambda n: (n, 0)), pl.BlockSpec((per * c, qk_all), lambda n: (n, 1)),
                  pl.BlockSpec((per * c, v_all), lambda n: (n, 1)),
                  _full(intra.shape), _full(cross.shape), _full(inner.shape)],
        out_specs=[pl.BlockSpec((per * c, v_all), lambda n: (n, 0)),
                   pl.BlockSpec((per, RET_HEADS, RET_QK, RET_V), lambda n: (n, 0, 0, 0))],
        out_shape=[jax.ShapeDtypeStruct((t, v_all), BF16),
                   jax.ShapeDtypeStruct((nc, RET_HEADS, RET_QK, RET_V), BF16)],
        scratch_shapes=[pltpu.VMEM((RET_HEADS, RET_QK, RET_V), F32)],
        compiler_params=_params(1), name="ret_core_fwd")(pj, pj, pj, intra, cross, inner)


def _group_norm(o_h):
    mu = jnp.mean(o_h, axis=-1, keepdims=True)
    dev = o_h - mu
    rstd = lax.rsqrt(jnp.mean(dev * dev, axis=-1, keepdims=True) + EPS)
    return dev * rstd, rstd


def ret_out_fwd(o, pj, x, gn_gain, g_post, w_out, tm=512):
    t, d = x.shape
    nt = t // tm
    v_all = RET_HEADS * RET_V

    def body(o_ref, g_ref, x_ref, gn_ref, g1_ref, w_ref, xo_ref, y_ref, out_ref):
        out = jnp.zeros((tm, d), F32)
        for h in range(RET_HEADS):
            cols = slice(h * RET_V, (h + 1) * RET_V)
            ohat, _ = _group_norm(o_ref[:, cols].astype(F32))
            g = g_ref[:, cols].astype(F32)
            yb = (g * jax.nn.sigmoid(g) * (ohat * gn_ref[:, cols])).astype(BF16)
            y_ref[:, cols] = yb
            out = out + _dot(yb, w_ref[cols, :])
        out_ref[...] = out
        m, _, _ = _rms_fwd(out, g1_ref[...])
        xo_ref[...] = x_ref[...] + m

    row = pl.BlockSpec((tm, d), lambda i: (i, 0))
    wide = pl.BlockSpec((tm, v_all), lambda i: (i, 0))
    return pl.pallas_call(
        body, grid=(nt,),
        in_specs=[wide, pl.BlockSpec((tm, v_all), lambda i: (i, 2)), row, _full((1, v_all)), _full((1, d)),
                  _full(w_out.shape)],
        out_specs=[row, wide, row],
        out_shape=[jax.ShapeDtypeStruct((t, d), F32), jax.ShapeDtypeStruct((t, v_all), BF16),
                   jax.ShapeDtypeStruct((t, d), F32)],
        compiler_params=_params(1), name="ret_out_fwd")(o, pj, x, gn_gain, g_post, w_out)


def ret_out_bwd(dxo, out, o, pj, gn_gain, g_post, w_out, tm=512):
    t, d = out.shape
    nt = t // tm
    v_all = RET_HEADS * RET_V

    def body(dxo_ref, out_ref, o_ref, g_ref, gn_ref, g1_ref, w_ref,
             dout_ref, dgate_ref, do_ref, dg1_ref, dgn_ref):
        @pl.when(pl.program_id(0) == 0)
        def _():
            dg1_ref[...] = jnp.zeros_like(dg1_ref)
            dgn_ref[...] = jnp.zeros_like(dgn_ref)

        g1 = g1_ref[...]
        dxo = dxo_ref[...]
        _, oh_, r_ = _rms_fwd(out_ref[...], g1)
        dg1_ref[...] += _colsum(dxo * oh_)
        doutb = _rms_bwd(oh_, r_, g1, dxo).astype(BF16)
        dout_ref[...] = doutb
        for h in range(RET_HEADS):
            cols = slice(h * RET_V, (h + 1) * RET_V)
            gn = gn_ref[:, cols]
            ohat, rstd = _group_norm(o_ref[:, cols].astype(F32))
            g = g_ref[:, cols].astype(F32)
            sg = jax.nn.sigmoid(g)
            dyh = _dot_nt(doutb, w_ref[cols, :])
            sil = g * sg
            dgate_ref[:, cols] = (dyh * (ohat * gn) * (sg + sil * (1.0 - sg))).astype(BF16)
            don = dyh * sil
            dgn_ref[:, cols] += _colsum(don * ohat)
            dohat = don * gn
            do_ref[:, cols] = (rstd * (dohat - jnp.mean(dohat, axis=-1, keepdims=True)
                                       - ohat * jnp.mean(dohat * ohat, axis=-1, keepdims=True))).astype(BF16)

    row = pl.BlockSpec((tm, d), lambda i: (i, 0))
    wide = pl.BlockSpec((tm, v_all), lambda i: (i, 0))
    gate = pl.BlockSpec((tm, v_all), lambda i: (i, 2))
    return pl.pallas_call(
        body, grid=(nt,),
        in_specs=[row, row, wide, gate, _full((1, v_all)), _full((1, d)), _full(w_out.shape)],
        out_specs=[row, gate, wide, _full((1, d)), _full((1, v_all))],
        out_shape=[jax.ShapeDtypeStruct((t, d), BF16), jax.ShapeDtypeStruct((t, RET_IN), BF16),
                   jax.ShapeDtypeStruct((t, v_all), BF16), jax.ShapeDtypeStruct((1, d), F32),
                   jax.ShapeDtypeStruct((1, v_all), F32)],
        compiler_params=_params(1), name="ret_out_bwd")(dxo, out, o, pj, gn_gain, g_post, w_out)


def ret_core_bwd(pj, do, sprev, cos, sin, dpj, intra, cross, inner, chunk_decay):
    t = pj.shape[0]
    nc = t // RET_CHUNK
    c = RET_CHUNK
    per = RET_STEP_CHUNKS
    qk_all = RET_HEADS * RET_QK
    v_all = RET_HEADS * RET_V
    scale = RET_QK ** -0.5

    def body(q_ref, k_ref, v_ref, do_ref, sp_ref, c_ref, s_ref, dm_ref, cr_ref, in_ref, dpj_in, dpj_ref, dstate):
        @pl.when(pl.program_id(0) == 0)
        def _():
            dstate[...] = jnp.zeros_like(dstate)

        for h in range(RET_HEADS):
            for cc in reversed(range(per)):
                rows = slice(cc * c, (cc + 1) * c)
                cs = c_ref[rows, :]
                sn = s_ref[rows, :]
                q = q_ref[rows, h * RET_QK:(h + 1) * RET_QK]
                k = k_ref[rows, h * RET_QK:(h + 1) * RET_QK]
                v = v_ref[rows, h * RET_V:(h + 1) * RET_V]
                doh = do_ref[rows, h * RET_V:(h + 1) * RET_V]
                dm = dm_ref[h]
                ab = (_dot_nt(q, k) * dm).astype(BF16)
                dab = (_dot_nt(doh, v) * dm).astype(BF16)
                dsb = dstate[h].astype(BF16)
                kd = (k.astype(F32) * in_ref[h]).astype(BF16)
                dv = _dot_tn(ab, doh) + _dot(kd, dsb)
                dq = _dot(dab, k) + cr_ref[h] * _dot_nt(doh, sp_ref[cc, h])
                dk = _dot_tn(dab, q) + in_ref[h] * _dot_nt(v, dsb)
                qd = (q.astype(F32) * cr_ref[h]).astype(BF16)
                dstate[h] = dstate[h] * chunk_decay[h] + _dot_tn(qd, doh)
                for base, dd, sc in ((h * RET_QK, dq, scale), (qk_all + h * RET_QK, dk, 1.0)):
                    d1, d2 = dd[:, :128], dd[:, 128:]
                    dpj_ref[rows, base:base + 128] = ((d1 * cs + d2 * sn) * sc).astype(BF16)
                    dpj_ref[rows, base + 128:base + RET_QK] = ((d2 * cs - d1 * sn) * sc).astype(BF16)
                dpj_ref[rows, 2 * qk_all + h * RET_V:2 * qk_all + (h + 1) * RET_V] = dv.astype(BF16)

    rev = lambda n: nc // per - 1 - n
    tab = pl.BlockSpec((per * c, 128), lambda n: (rev(n), 0))
    return pl.pallas_call(
        body, grid=(nc // per,),
        in_specs=[pl.BlockSpec((per * c, qk_all), lambda n: (rev(n), 0)),
                  pl.BlockSpec((per * c, qk_all), lambda n: (rev(n), 1)),
                  pl.BlockSpec((per * c, v_all), lambda n: (rev(n), 1)),
                  pl.BlockSpec((per * c, v_all), lambda n: (rev(n), 0)),
                  pl.BlockSpec((per, RET_HEADS, RET_QK, RET_V), lambda n: (rev(n), 0, 0, 0)),
                  tab, tab, _full(intra.shape), _full(cross.shape), _full(inner.shape), ANY],
        out_specs=pl.BlockSpec((per * c, 2 * qk_all + v_all), lambda n: (rev(n), 0)),
        out_shape=jax.ShapeDtypeStruct((t, RET_IN), BF16),
        scratch_shapes=[pltpu.VMEM((RET_HEADS, RET_QK, RET_V), F32)],
        input_output_aliases={10: 0},
        compiler_params=_params(1), name="ret_core_bwd")(pj, pj, pj, do, sprev, cos, sin, intra, cross, inner, dpj)


def ret_in_bwd(dpj, dres, x, g_pre, w_in, tm=512):
    t, d = x.shape
    nt = t // tm

    def body(dpj_ref, dres_ref, x_ref, g_ref, win_hbm, dx_ref, dg_ref, win_v, sem):
        @pl.when(pl.program_id(0) == 0)
        def _():
            cp = pltpu.make_async_copy(win_hbm, win_v, sem)
            cp.start()
            dg_ref[...] = jnp.zeros_like(dg_ref)
            cp.wait()

        g = g_ref[...]
        dh = jnp.zeros((tm, d), F32)
        for j in range(N_SHARD):
            dh = dh + _dot_nt(dpj_ref[:, j * RET_IN_SHARD:(j + 1) * RET_IN_SHARD], win_v[j])
        _, xh, rx = _rms_fwd(x_ref[...], g)
        dg_ref[...] += _colsum(dh * xh)
        dx_ref[...] = dres_ref[...] + _rms_bwd(xh, rx, g, dh)

    row = pl.BlockSpec((tm, d), lambda i: (i, 0))
    return pl.pallas_call(
        body, grid=(nt,),
        in_specs=[pl.BlockSpec((tm, RET_IN), lambda i: (i, 0)), row, row, _full((1, d)), ANY],
        out_specs=[row, _full((1, d))],
        out_shape=[jax.ShapeDtypeStruct((t, d), F32), jax.ShapeDtypeStruct((1, d), F32)],
        scratch_shapes=[pltpu.VMEM(w_in.shape, BF16), pltpu.SemaphoreType.DMA],
        compiler_params=_params(1), name="ret_in_bwd")(dpj, dres, x, g_pre, w_in)


_CHIP_FLIPS = ((1, 0), (0, 1), (1, 1))


def _flip(v, b):
    return 1 - v if b else v


_HBM = pl.BlockSpec(memory_space=pltpu.HBM)
_SEM = pl.BlockSpec(memory_space=pltpu.SEMAPHORE)
_EFFECT = pltpu.SideEffectType.DATAFLOW_SIDE_EFFECTING


def _chip_copies(mode, srcs, lands, send_sems, recv_sems):
    x, y, c = lax.axis_index("x"), lax.axis_index("y"), lax.axis_index("c")
    copies = []
    for t in range(len(lands)):
        if mode == "swap":
            copies.append(pltpu.make_async_remote_copy(
                src_ref=srcs[t], dst_ref=lands[t], send_sem=send_sems.at[t], recv_sem=recv_sems.at[t],
                device_id=(x, y, 1 - c), device_id_type=MESH))
            continue
        if mode == "everyone":
            for m in range(1, 8):
                bx, by, bc = (m >> 2) & 1, (m >> 1) & 1, m & 1
                copies.append(pltpu.make_async_remote_copy(
                    src_ref=srcs[t], dst_ref=lands[t].at[4 * x + 2 * y + c], send_sem=send_sems.at[7 * t + m - 1],
                    recv_sem=recv_sems.at[7 * t + m - 1], device_id=(_flip(x, bx), _flip(y, by), _flip(c, bc)),
                    device_id_type=MESH))
            continue
        for k, (bx, by) in enumerate(_CHIP_FLIPS):
            px, py = _flip(x, bx), _flip(y, by)
            target = (px, py, c)
            if mode == "gather":
                src, dst = srcs[t], lands[t].at[2 * x + y]
            elif mode == "gather_half":
                half = pl.ds(c * (srcs[t].shape[0] // 2), srcs[t].shape[0] // 2)
                src, dst = srcs[t].at[half], lands[t].at[2 * x + y, half]
            elif mode == "forward_half":
                half = pl.ds(c * (lands[t].shape[1] // 2), lands[t].shape[1] // 2)
                src = dst = lands[t].at[2 * px + py, half]
                target = (x, y, 1 - c)
            else:
                src, dst = srcs[t].at[2 * px + py], lands[t].at[k]
            copies.append(pltpu.make_async_remote_copy(
                src_ref=src, dst_ref=dst, send_sem=send_sems.at[3 * t + k], recv_sem=recv_sems.at[3 * t + k],
                device_id=target, device_id_type=MESH))
    return copies


def exchange_start(mode, srcs, lands, name, after=None):
    n, ns = len(lands), len(srcs)
    extra = [] if after is None else [after]

    def body(*refs):
        ins, lnd = refs[:ns], refs[ns:ns + n]
        send_sems, recv_sems = refs[ns + n + len(extra)], refs[ns + n + len(extra) + 1]
        token = refs[-1]
        for cp in _chip_copies(mode, ins, lnd, send_sems, recv_sems):
            cp.start()
        token[...] = jnp.zeros(token.shape, token.dtype)

    hbm = lambda a: pltpu.with_memory_space_constraint(a, pltpu.HBM)
    passed = list(srcs) + list(lands)
    n_sem = {"swap": 1, "everyone": 7}.get(mode, 3) * n
    return pl.pallas_call(
        body, name=name,
        out_shape=(pltpu.SemaphoreType.DMA((n_sem,)), pltpu.SemaphoreType.DMA((n_sem,)),
                   *[pltpu.HBM(a.shape, a.dtype) for a in passed], jax.ShapeDtypeStruct((8, 128), F32)),
        in_specs=[_HBM] * (ns + n) + [ANY] * len(extra),
        out_specs=(_SEM, _SEM, *[_HBM] * (ns + n), pl.BlockSpec(memory_space=pltpu.VMEM)),
        input_output_aliases={i: 2 + i for i in range(ns + n)},
        compiler_params=pltpu.CompilerParams(has_side_effects=_EFFECT))(*[hbm(a) for a in passed], *extra)


def exchange_wait(mode, started, after, name):
    send_sems, recv_sems = started[0], started[1]
    passed = list(started[2:-1])
    n = len(passed) if mode == "forward_half" else len(passed) // 2
    ns = len(passed) - n
    after = list(after) if isinstance(after, (list, tuple)) else [after]

    def body(*refs):
        ins, lnd = refs[:ns], refs[ns:ns + n]
        for cp in _chip_copies(mode, ins, lnd, refs[ns + n], refs[ns + n + 1]):
            cp.wait_send()
            cp.wait_recv()

    outs = pl.pallas_call(
        body, name=name, out_shape=tuple(pltpu.HBM(a.shape, a.dtype) for a in passed),
        in_specs=[_HBM] * (ns + n) + [_SEM, _SEM] + [ANY] * len(after), out_specs=tuple([_HBM] * (ns + n)),
        input_output_aliases={i: i for i in range(ns + n)},
        compiler_params=pltpu.CompilerParams(has_side_effects=_EFFECT))(*passed, send_sems, recv_sems, *after)
    return list(outs[:ns]), list(outs[ns:])


def plane_sum(slot, full, recv, name, bm=256):
    _, m, n = full.shape
    bm = _row_block(m, bm)

    def body(slot_ref, o_ref, r_ref, s_ref):
        s_ref[...] = ((o_ref[...] + r_ref[0].astype(F32)) + r_ref[1].astype(F32)) + r_ref[2].astype(F32)

    return pl.pallas_call(
        body,
        grid_spec=pltpu.PrefetchScalarGridSpec(
            num_scalar_prefetch=1, grid=(m // bm,),
            in_specs=[pl.BlockSpec((None, bm, n), lambda i, s: (s[0], i, 0)),
                      pl.BlockSpec((3, bm, n), lambda i, s: (0, i, 0))],
            out_specs=pl.BlockSpec((bm, n), lambda i, s: (i, 0))),
        out_shape=jax.ShapeDtypeStruct((m, n), F32), compiler_params=_params(1), name=name)(slot, full, recv)


def sum_slots(parts, name, bm=312):
    _, r, n = parts.shape
    bm = bm if r % bm == 0 else r

    def body(p_ref, s_ref):
        acc = p_ref[0]
        for k in range(1, 8):
            acc = acc + p_ref[k]
        s_ref[...] = acc

    return pl.pallas_call(
        body, grid=(r // bm,), in_specs=[pl.BlockSpec((8, bm, n), lambda i: (0, i, 0))],
        out_specs=pl.BlockSpec((bm, n), lambda i: (i, 0)), out_shape=jax.ShapeDtypeStruct((r, n), F32),
        compiler_params=_params(1), name=name)(parts)


def _adamw_math(w, g, m, v):
    m = ADAM_B1 * m + (1.0 - ADAM_B1) * g
    v = ADAM_B2 * v + (1.0 - ADAM_B2) * (g * g)
    m_hat = m / (1.0 - ADAM_B1 ** ADAM_STEP)
    v_hat = v / (1.0 - ADAM_B2 ** ADAM_STEP)
    delta = -ADAM_LR * (m_hat / (jnp.sqrt(v_hat) + ADAM_EPS) + ADAM_WD * w)
    return delta, m, v


def adamw(w, m, v, grads, layer, prev, name, bm=256):
    _, _, n = w.shape
    mm = grads[0].shape[0]
    bm = _row_block(mm, bm)
    ng = len(grads)

    def body(*refs):
        w_ref, m_ref, v_ref = refs[:3]
        g_refs = refs[3:3 + ng]
        g_out, d_out, m_out, v_out = refs[-4:]
        g = g_refs[0][...]
        for gr in g_refs[1:]:
            g = g + gr[...]
        delta, mn, vn = _adamw_math(w_ref[...], g, m_ref[...], v_ref[...])
        g_out[...] = g
        d_out[...] = delta
        m_out[...] = mn
        v_out[...] = vn

    slab = pl.BlockSpec((None, bm, n), lambda i: (layer, i, 0))
    flat = pl.BlockSpec((bm, n), lambda i: (i, 0))
    in_specs = [slab] * 3 + [flat] * ng
    args = [w, m, v, *grads]
    aliases = {}
    if prev is not None:
        in_specs += [ANY] * 4
        aliases = {3 + ng + q: q for q in range(4)}
        args += list(prev)
    return pl.pallas_call(
        body, grid=(mm // bm,), in_specs=in_specs, out_specs=[slab] * 4,
        out_shape=[jax.ShapeDtypeStruct(w.shape, F32)] * 4, input_output_aliases=aliases,
        compiler_params=_params(1), name=name)(*args)


def _pack_rows(parts, rows):
    flat = jnp.concatenate([p.reshape(-1) for p in parts])
    return jnp.pad(flat, (0, rows * 128 - flat.shape[0])).reshape(rows, 128)


def _as_shards(a, rows):
    return a.reshape(N_SHARD, rows, a.shape[-1])


def _local_step(x, pos_col, target, gains, pool_w, pool_scale, gn_gain, conv_w, conv_b, weights, send_grads):
    def gain(l, n, token=None):
        g = gains[l, n].reshape(1, D_MODEL)
        return g if token is None else g + token[0:1, 0:1]

    inv_freq = (ROPE_BASE ** (-jnp.arange(0, RET_QK, 2, dtype=F32) / RET_QK)).reshape(1, RET_QK // 2)
    intra, cross, inner, chunk_decay = _decay_tables()
    dn_rows = D_FF // N_SHARD

    x1 = pool_fwd(x, gain(0, 0), gain(0, 1), pool_w, pool_scale)
    cos, sin = rope_tables(pos_col, inv_freq, x1)
    w_up0, w_dn0 = weights("mlp0", cos)
    w_dn0 = w_dn0.reshape(D_FF, D_MODEL)
    x2, u0, s0, a0, f0, h0 = mlp_fwd(x1, gain(0, 2), gain(0, 3), w_up0, w_dn0, conv_w[0], conv_b[0])
    w_in, w_out = weights("ret", x2)
    w_out = w_out.reshape(RET_HEADS * RET_V, D_MODEL)
    pj, hr = ret_proj(x2, gain(1, 0), w_in, cos, sin)
    o, sprev = ret_core_fwd(pj, intra, cross, inner, chunk_decay)
    x3, yb, out = ret_out_fwd(o, pj, x2, gn_gain, gain(1, 1), w_out)
    w_up1, w_dn1 = weights("mlp1", x3)
    w_dn1 = w_dn1.reshape(D_FF, D_MODEL)
    dx4, u1, s1, a1, f1, h1, loss = mlp_fwd(x3, gain(1, 2), gain(1, 3), w_up1, w_dn1, conv_w[1], conv_b[1], target)

    dx3, du1, df1, dg12, dg13, dcw1, dcb1 = mlp_bwd(
        dx4, f1, x3, u1, s1, gain(1, 2), gain(1, 3), w_up1, w_dn1, conv_w[1])
    dwup1 = grad_matmul(h1, du1, D_MODEL, FF_CHUNK, "grad_w_up_1", interleaved=True)
    dwdn1 = grad_matmul(a1, df1, FF_CHUNK, D_MODEL, "grad_w_down_1")
    tok = send_grads("mlp1", [dwup1, [_as_shards(g, dn_rows) for g in dwdn1]])
    dout, dpj, do, dg11, dgn = ret_out_bwd(dx3, out, o, pj, gn_gain, gain(1, 1, tok), w_out)
    dwout = grad_matmul(yb, dout, 1024, D_MODEL, "grad_w_out")
    dpj = ret_core_bwd(pj, do, sprev, cos, sin, dpj, intra, cross, inner, chunk_decay)
    dwin = grad_matmul(hr, dpj, D_MODEL, RET_IN_SHARD, "grad_w_in")
    tok = send_grads("ret", [dwin, [_as_shards(g, RET_V) for g in dwout]])
    dx2, dg10 = ret_in_bwd(dpj, dx3, x2, gain(1, 0, tok), w_in)
    dx1, du0, df0, dg02, dg03, dcw0, dcb0 = mlp_bwd(
        dx2, f0, x1, u0, s0, gain(0, 2), gain(0, 3), w_up0, w_dn0, conv_w[0])
    dwdn0 = grad_matmul(a0, df0, FF_CHUNK, D_MODEL, "grad_w_down_0")
    tok = send_grads("down0", [[_as_shards(g, dn_rows) for g in dwdn0]])
    dwup0 = grad_matmul(h0, du0, D_MODEL, FF_CHUNK, "grad_w_up_0", interleaved=True, after=tok)
    tok = send_grads("up0", [dwup0])
    dx0, dg00, dg01, dpscale, dpw = pool_bwd(dx1, x, gain(0, 0, tok), gain(0, 1), pool_w, pool_scale)

    rows = lambda g: jnp.sum(g, axis=0, keepdims=True)
    dgains = jnp.concatenate([dg00, dg01, rows(dg02), rows(dg03), dg10, dg11, rows(dg12), rows(dg13)],
                             axis=0).reshape(2, 4, D_MODEL)
    small = {"gains": dgains, "pool_scale": dpscale, "gn": dgn,
             "conv_w": jnp.sum(jnp.stack([dcw0, dcw1]), axis=3),
             "conv_b": jnp.sum(jnp.stack([dcb0, dcb1]), axis=2, keepdims=True), "pool_w": dpw}
    return loss, dx0, small


def kernel(x, positions, norm_gain, pool_w, pool_scale, ret_w_in, ret_gn_gain, ret_w_out, mlp_w_up, mlp_conv_w, mlp_conv_b, mlp_w_down, loss_target, m_norm_gain, m_pool_w, m_pool_scale, m_ret_w_in, m_ret_gn_gain, m_ret_w_out, m_mlp_w_up, m_mlp_conv_w, m_mlp_conv_b, m_mlp_w_down, v_norm_gain, v_pool_w, v_pool_scale, v_ret_w_in, v_ret_gn_gain, v_ret_w_out, v_mlp_w_up, v_mlp_conv_w, v_mlp_conv_b, v_mlp_w_down):
    t = x.shape[1]
    me = 2 * lax.axis_index("x") + lax.axis_index("y")
    me_slot = jnp.reshape(me, (1,)).astype(jnp.int32)

    small_parts = [norm_gain, ret_gn_gain, mlp_conv_w, pool_w]
    small_sizes = [p.size for p in small_parts]
    small_rows = -(-sum(small_sizes) // (128 * 8)) * 8
    gathers = {}

    def start_gather(group, srcs, after):
        lands = [lax.dynamic_update_index_in_dim(lax.empty((N_SHARD,) + s.shape, s.dtype), s, me, 0) for s in srcs]
        mode = "gather_half" if group == "mlp0" else "gather"
        gathers[group] = (mode, exchange_start(mode, srcs, lands, "gather_start_" + group, after=after))
        return gathers[group][1][-1]

    token = start_gather("small", [_pack_rows(small_parts, small_rows)], None)
    token = start_gather("mlp0", [mlp_w_up[0].astype(BF16), mlp_w_down[0].astype(BF16)], token)

    def weights(group, after):
        if group == "mlp0":
            tok = start_gather("ret", [ret_w_in[0].astype(BF16), ret_w_out[0].astype(BF16)], after)
            after = start_gather("mlp1", [mlp_w_up[1].astype(BF16), mlp_w_down[1].astype(BF16)], tok)
        mode, started = gathers[group]
        _, lands = exchange_wait(mode, started, after, "gather_wait_" + group)
        if mode == "gather_half":
            forward = exchange_start("forward_half", [], lands, "forward_start_" + group)
            _, lands = exchange_wait("forward_half", forward, forward[-1], "forward_wait_" + group)
        return lands

    sent, early = {}, {}

    def reduced(group, after, names):
        started, own = sent[group]
        _, recv = exchange_wait("scatter", started, after, "scatter_wait_" + group)
        return [plane_sum(me_slot, f, r, "plane_sum_" + nm)
                for f, r, nm in zip(own, recv, names)]

    def swap_start(planes, name):
        return exchange_start("swap", planes, [lax.empty(p.shape, p.dtype) for p in planes], name)

    def send_grads(group, pairs):
        lands = [lax.empty((3,) + b.shape[1:], BF16) for _, b in pairs]
        sent[group] = (exchange_start("scatter", [b for _, b in pairs], lands, "scatter_start_" + group),
                       [f for f, _ in pairs])
        token = sent[group][0][-1]
        if group == "up0":
            early["planes"] = (reduced("mlp1", token, ["w_up_1", "w_down_1"])
                               + reduced("ret", token, ["w_in", "w_out"]))
            early["swap"] = swap_start(early["planes"], "swap_start_a")
            token = token + early["swap"][-1]
        return token

    (smallg,) = weights("small", token)
    smallg = smallg.reshape(N_SHARD, -1)
    offs = np.cumsum([0] + small_sizes)
    piece = lambda i, shape: smallg[:, offs[i]:offs[i + 1]].reshape((N_SHARD,) + shape)
    gains = piece(0, (2, 4, 256)).transpose(1, 2, 0, 3).reshape(2, 4, D_MODEL)
    gn_full = piece(1, (512,)).reshape(1, RET_HEADS * RET_V)
    cw_full = piece(2, (2, 3, FF_CHUNK)).transpose(1, 0, 2, 3)
    pw_full = piece(3, (4, 64, 256)).transpose(1, 0, 2, 3).reshape(4, 256, 256).astype(BF16)
    cb_full = mlp_conv_b.reshape(2, N_SHARD, 1, FF_CHUNK)

    loss, dx0, small = _local_step(
        x[0], positions.reshape(t, 1).astype(F32), loss_target[0], gains, pw_full, pool_scale, gn_full,
        cw_full, cb_full, weights, send_grads)

    def small_adamw(w, m, v, grads, name):
        w3 = w.reshape(1, -1, w.shape[-1])
        out = adamw(w3, m.reshape(w3.shape), v.reshape(w3.shape), [g.reshape(w3.shape[1:]) for g in grads], 0, None, name)
        return [o.reshape(w.shape) for o in out]

    pw_f = small["pool_w"].reshape(4, N_SHARD, 64, 256).transpose(1, 0, 2, 3).reshape(N_SHARD, 256, 256)
    small = dict(small, loss=loss)
    small_order = ["gains", "pool_scale", "gn", "conv_w", "conv_b", "loss"]
    gsmall_sizes = [small[k].size for k in small_order]
    gsmall_rows = -(-sum(gsmall_sizes) // (128 * 8)) * 8
    gpack = _pack_rows([small[k] for k in small_order], gsmall_rows)
    mine = 2 * me + lax.axis_index("c")
    small_started = exchange_start(
        "everyone", [gpack], [lax.dynamic_update_index_in_dim(lax.empty((8,) + gpack.shape, F32), gpack, mine, 0)],
        "small_start")
    pool_token = send_grads("pool_w", [(pw_f, pw_f.astype(BF16))])

    res = {}
    planes_a, others_a = exchange_wait("swap", early["swap"], [small_started[-1], pool_token], "swap_wait_a")
    res["ret_w_in"] = adamw(ret_w_in, m_ret_w_in, v_ret_w_in, (planes_a[2], others_a[2]), 0, None, "adamw_w_in")
    res["ret_w_out"] = adamw(ret_w_out, m_ret_w_out, v_ret_w_out, (planes_a[3], others_a[3]), 0, None, "adamw_w_out")
    up1 = adamw(mlp_w_up, m_mlp_w_up, v_mlp_w_up, (planes_a[0], others_a[0]), 1, None, "adamw_w_up_1")
    dn1 = adamw(mlp_w_down, m_mlp_w_down, v_mlp_w_down, (planes_a[1], others_a[1]), 1, None, "adamw_w_down_1")

    done_a = [res["ret_w_in"][0], res["ret_w_out"][0], up1[0], dn1[0]]
    planes_b = (reduced("up0", done_a, ["w_up_0"]) + reduced("down0", done_a, ["w_down_0"])
                + reduced("pool_w", done_a, ["pool_w"]))
    swap_b = swap_start(planes_b, "swap_start_b")

    _, (small_recv,) = exchange_wait("everyone", small_started, swap_b[-1], "small_wait")
    gsmall = sum_slots(small_recv, "sum_small").reshape(-1)
    goffs = np.cumsum([0] + gsmall_sizes)
    gpiece = lambda i: gsmall[goffs[i]:goffs[i + 1]].reshape(small[small_order[i]].shape)
    g_gains = lax.dynamic_slice_in_dim(gpiece(0), me * 256, 256, axis=2)
    g_gn = lax.dynamic_slice_in_dim(gpiece(2), me * RET_V, RET_V, axis=1)
    g_cw = lax.dynamic_index_in_dim(gpiece(3), me, 1, keepdims=False)
    res["norm_gain"] = small_adamw(norm_gain, m_norm_gain, v_norm_gain, [g_gains], "adamw_norm_gain")
    res["pool_scale"] = small_adamw(pool_scale, m_pool_scale, v_pool_scale, [gpiece(1)], "adamw_pool_scale")
    res["ret_gn_gain"] = small_adamw(ret_gn_gain, m_ret_gn_gain, v_ret_gn_gain, [g_gn], "adamw_gn_gain")
    res["mlp_conv_w"] = small_adamw(mlp_conv_w, m_mlp_conv_w, v_mlp_conv_w, [g_cw], "adamw_conv_w")
    res["mlp_conv_b"] = small_adamw(mlp_conv_b, m_mlp_conv_b, v_mlp_conv_b, [gpiece(4)], "adamw_conv_b")

    small_done = [res[k][0] for k in ("norm_gain", "pool_scale", "ret_gn_gain", "mlp_conv_w", "mlp_conv_b")]
    planes_b, others_b = exchange_wait("swap", swap_b, small_done, "swap_wait_b")
    res["mlp_w_up"] = adamw(mlp_w_up, m_mlp_w_up, v_mlp_w_up, (planes_b[0], others_b[0]), 0, up1, "adamw_w_up_0")
    res["mlp_w_down"] = adamw(mlp_w_down, m_mlp_w_down, v_mlp_w_down, (planes_b[1], others_b[1]), 0, dn1,
                              "adamw_w_down_0")
    res["pool_w"] = small_adamw(pool_w, m_pool_w, v_pool_w, (planes_b[2], others_b[2]), "adamw_pool_w")

    order = ["norm_gain", "pool_w", "pool_scale", "ret_w_in", "ret_gn_gain", "ret_w_out", "mlp_w_up", "mlp_conv_w",
             "mlp_conv_b", "mlp_w_down"]
    outs = [gpiece(5)[0, 0], dx0.reshape(x.shape)]
    for q in range(4):
        outs += [res[k][q] for k in order]
    return tuple(outs)
```

```python
import numpy as np
import jax
import jax.numpy as jnp
from jax import lax
from jax.experimental import pallas as pl
from jax.experimental.pallas import tpu as pltpu

F32 = jnp.float32
BF16 = jnp.bfloat16

D_MODEL = 1024
D_FF = 2816
FF_CHUNK = 1408
N_SHARD = 4
POOL_WINDOWS = (2, 4, 8, 16)
POOL_DIM = 256
POOL_HALO = 16
RET_HEADS = 4
RET_QK = 256
RET_V = 512
RET_CHUNK = 256
RET_STEP_CHUNKS = 2
RET_IN = 6144
RET_IN_SHARD = 1536
ROPE_BASE = 10000.0
EPS = 1e-6
CONV_HALO = 8

ADAM_LR, ADAM_B1, ADAM_B2, ADAM_EPS, ADAM_WD, ADAM_STEP = 0.001, 0.9, 0.999, 1e-08, 0.01, 10

VMEM_LIMIT = 56 * 1024 * 1024
VMEM_LIMIT_MLP_BWD = 62 * 1024 * 1024
MESH = pl.DeviceIdType.MESH
ANY = pl.BlockSpec(memory_space=pl.ANY)


def _params(n_grid=1, limit=VMEM_LIMIT):
    return pltpu.CompilerParams(dimension_semantics=("arbitrary",) * n_grid, vmem_limit_bytes=limit)


def _dot(a, b):
    return jnp.dot(a, b, preferred_element_type=F32)


def _dot_nt(a, b):
    return lax.dot_general(a, b, (((1,), (1,)), ((), ())), preferred_element_type=F32)


def _dot_tn(a, b):
    return lax.dot_general(a, b, (((0,), (0,)), ((), ())), preferred_element_type=F32)


def _rms_fwd(x, gain):
    r = lax.rsqrt(jnp.mean(x * x, axis=-1, keepdims=True) + EPS)
    xh = x * r
    return xh * gain, xh, r


def _rms_bwd(xh, r, gain, dy):
    dxh = dy * gain
    return r * (dxh - xh * jnp.mean(dxh * xh, axis=-1, keepdims=True))


def _colsum(v):
    return jnp.sum(v, axis=0, keepdims=True)


def _full(shape):
    nd = len(shape)
    return pl.BlockSpec(shape, lambda *_: (0,) * nd)


def rope_tables(pos_col, inv_freq, after):
    t = pos_col.shape[0]
    tm = min(t, 1024)

    def body(p_ref, f_ref, after_ref, c_ref, s_ref):
        ang = p_ref[...] * f_ref[...]
        c_ref[...] = jnp.cos(ang)
        s_ref[...] = jnp.sin(ang)

    return pl.pallas_call(
        body, grid=(t // tm,),
        in_specs=[pl.BlockSpec((tm, 1), lambda i: (i, 0)), _full((1, 128)), ANY],
        out_specs=[pl.BlockSpec((tm, 128), lambda i: (i, 0))] * 2,
        out_shape=[jax.ShapeDtypeStruct((t, 128), F32)] * 2,
        compiler_params=_params(1), name="rope_tables")(pos_col, inv_freq, after)


def _window_sums(ext, backward):
    n = ext.shape[0]
    cur, sums = ext, []
    for g, win in enumerate(POOL_WINDOWS):
        if g > 0:
            cur = cur[:, POOL_DIM:]
        half = win // 2
        cur = cur + pltpu.roll(cur, n - half if backward else half, axis=0)
        sums.append(cur[:, 0:POOL_DIM])
    return sums


def _pool_diff(h_halo, h, row0, tm):
    t_idx = row0 + lax.broadcasted_iota(jnp.int32, (tm, 1), 0)
    sums = _window_sums(jnp.concatenate([h_halo, h], axis=0), backward=False)
    parts, inv_counts = [], []
    for g, win in enumerate(POOL_WINDOWS):
        inv = 1.0 / jnp.minimum(t_idx + 1, win).astype(F32)
        parts.append(sums[g][POOL_HALO:, :] * inv - h[:, g * POOL_DIM:(g + 1) * POOL_DIM])
        inv_counts.append(inv)
    return parts, inv_counts


def pool_fwd(x, g_pre, g_post, pool_w, pool_scale, tm=512):
    t, d = x.shape
    nt = t // tm

    def body(x_ref, g0_ref, g1_ref, w_ref, sc_ref, o_ref, hext):
        i = pl.program_id(0)

        @pl.when(i == 0)
        def _():
            hext[...] = jnp.zeros((POOL_HALO, d), F32)

        xv = x_ref[...]
        h, _, _ = _rms_fwd(xv, g0_ref[...])
        parts, _ = _pool_diff(hext[...], h, i * tm, tm)
        hext[...] = h[tm - POOL_HALO:tm, :]
        ys = [_dot(parts[g].astype(BF16), w_ref[g]) for g in range(len(POOL_WINDOWS))]
        y = jnp.concatenate(ys, axis=-1) * sc_ref[...]
        m, _, _ = _rms_fwd(y, g1_ref[...])
        o_ref[...] = xv + m

    row = pl.BlockSpec((tm, d), lambda i: (i, 0))
    return pl.pallas_call(
        body, grid=(nt,),
        in_specs=[row, _full((1, d)), _full((1, d)), _full(pool_w.shape), _full((1, d))],
        out_specs=row, out_shape=jax.ShapeDtypeStruct((t, d), F32),
        scratch_shapes=[pltpu.VMEM((POOL_HALO, d), F32)],
        compiler_params=_params(1), name="pool_fwd")(x, g_pre, g_post, pool_w, pool_scale)


def pool_bwd(dx1, x, g_pre, g_post, pool_w, pool_scale, after, tm=512):
    t, d = x.shape
    nt = t // tm
    ng = len(POOL_WINDOWS)

    def body(dx1_ref, x_ref, xh_ref, g0_ref, g1_ref, w_ref, sc_ref, after_ref,
             dx_ref, dg0_ref, dg1_ref, dsc_ref, dw_ref, enext):
        i = pl.program_id(0)
        r = nt - 1 - i

        @pl.when(i == 0)
        def _():
            enext[...] = jnp.zeros((POOL_HALO, d), F32)
            dg0_ref[...] = jnp.zeros_like(dg0_ref)
            dg1_ref[...] = jnp.zeros_like(dg1_ref)
            dsc_ref[...] = jnp.zeros_like(dsc_ref)
            dw_ref[...] = jnp.zeros_like(dw_ref)

        g0 = g0_ref[...]
        g1 = g1_ref[...]
        sc = sc_ref[...]
        xv = x_ref[...]
        h, xh, rx = _rms_fwd(xv, g0)
        h_halo, _, _ = _rms_fwd(xh_ref[...], g0)
        parts, inv_counts = _pool_diff(h_halo * jnp.where(r > 0, 1.0, 0.0), h, r * tm, tm)
        parts_b = [p.astype(BF16) for p in parts]
        ypre = jnp.concatenate([_dot(parts_b[g], w_ref[g]) for g in range(ng)], axis=-1)
        _, yh, ry = _rms_fwd(ypre * sc, g1)
        dm = dx1_ref[...]
        dg1_ref[...] += _colsum(dm * yh)
        dy = _rms_bwd(yh, ry, g1, dm)
        dsc_ref[...] += _colsum(dy * ypre)
        dyp = (dy * sc).astype(BF16)
        ddiffs = []
        for g in range(ng):
            cols = slice(g * POOL_DIM, (g + 1) * POOL_DIM)
            dw_ref[g] += _dot_tn(parts_b[g], dyp[:, cols])
            ddiffs.append(_dot_nt(dyp[:, cols], w_ref[g]))
        e = jnp.concatenate([ddiffs[g] * inv_counts[g] for g in range(ng)], axis=-1)
        sums = _window_sums(jnp.concatenate([e, enext[...]], axis=0), backward=True)
        enext[...] = e[0:POOL_HALO, :]
        dh = jnp.concatenate([sums[g][0:tm, :] - ddiffs[g] for g in range(ng)], axis=-1)
        dg0_ref[...] += _colsum(dh * xh)
        dx_ref[...] = dm + _rms_bwd(xh, rx, g0, dh)

    row = pl.BlockSpec((tm, d), lambda i: (nt - 1 - i, 0))
    halo = pl.BlockSpec((POOL_HALO, d), lambda i: (jnp.maximum((nt - 1 - i) * (tm // POOL_HALO) - 1, 0), 0))
    vec = _full((1, d))
    return pl.pallas_call(
        body, grid=(nt,),
        in_specs=[row, row, halo, vec, vec, _full(pool_w.shape), vec, ANY],
        out_specs=[row, vec, vec, vec, _full((ng, POOL_DIM, POOL_DIM))],
        out_shape=[jax.ShapeDtypeStruct((t, d), F32)] + [jax.ShapeDtypeStruct((1, d), F32)] * 3
        + [jax.ShapeDtypeStruct((ng, POOL_DIM, POOL_DIM), F32)],
        scratch_shapes=[pltpu.VMEM((POOL_HALO, d), F32)],
        compiler_params=_params(1), name="pool_bwd")(dx1, x, x, g_pre, g_post, pool_w, pool_scale, after)


def _conv_taps(cw_ref, j):
    return cw_ref[j, 0:1, :], cw_ref[j, 1:2, :], cw_ref[j, 2:3, :]


def _row_block(m, target=256):
    if m <= target:
        return m
    for b in range(target, 7, -8):
        if m % b == 0:
            return b
    return m


def mlp_fwd(x, g_pre, g_post, w_up, w_down, conv_w, conv_b, target=None, tm=256):
    t, d = x.shape
    nt = t // tm
    h8 = CONV_HALO
    with_loss = target is not None
    n_extra = 1 if with_loss else 0

    def body(x_ref, g2_ref, g3_ref, wup_hbm, wdn_hbm, cw_ref, cb_ref, *rest):
        tgt_ref = rest[0] if with_loss else None
        xo_ref, u_ref, s_ref, a_ref, f_ref, h_ref = rest[n_extra:n_extra + 6]
        loss_ref = rest[n_extra + 6] if with_loss else None
        wup_v, wdn_v, tail, sem = rest[-4:]
        i = pl.program_id(0)

        @pl.when(i == 0)
        def _():
            c1 = pltpu.make_async_copy(wup_hbm, wup_v, sem.at[0])
            c2 = pltpu.make_async_copy(wdn_hbm, wdn_v, sem.at[1])
            c1.start()
            c2.start()
            tail[...] = jnp.zeros_like(tail)
            if with_loss:
                loss_ref[...] = jnp.zeros_like(loss_ref)
            c1.wait()
            c2.wait()

        xv = x_ref[...]
        h, _, _ = _rms_fwd(xv, g2_ref[...])
        hb = h.astype(BF16)
        h_ref[...] = hb
        acc = jnp.zeros((tm, d), F32)
        for k in range(2):
            cs = []
            for s in range(2):
                j, cols = k + 2 * s, slice((2 * k + s) * FF_CHUNK, (2 * k + s + 1) * FF_CHUNK)
                uf = _dot(hb, wup_v[j])
                u_ref[:, cols] = uf.astype(BF16)
                ext = jnp.concatenate([tail[j], uf], axis=0)
                tail[j] = uf[tm - h8:tm, :]
                w0, w1, w2 = _conv_taps(cw_ref, j)
                cs.append(cb_ref[j] + w2 * uf + w1 * pltpu.roll(ext, 1, axis=0)[h8:, :]
                          + w0 * pltpu.roll(ext, 2, axis=0)[h8:, :])
            cg, cv = cs
            sg = jax.nn.sigmoid(cg)
            sil = cg * sg
            ab = (sil * cv).astype(BF16)
            a_ref[:, k * FF_CHUNK:(k + 1) * FF_CHUNK] = ab
            s_ref[:, 2 * k * FF_CHUNK:(2 * k + 1) * FF_CHUNK] = sil.astype(BF16)
            s_ref[:, (2 * k + 1) * FF_CHUNK:(2 * k + 2) * FF_CHUNK] = (cv * (sg + sil * (1.0 - sg))).astype(BF16)
            acc = acc + _dot(ab, wdn_v[k * FF_CHUNK:(k + 1) * FF_CHUNK, :])
        f_ref[...] = acc
        y, _, _ = _rms_fwd(acc, g3_ref[...])
        if with_loss:
            err = (xv + y) - tgt_ref[...]
            xo_ref[...] = err * (1.0 / d)
            loss_ref[...] += 0.5 * jnp.sum(jnp.mean(err * err, axis=-1, keepdims=True), axis=0, keepdims=True)
        else:
            xo_ref[...] = xv + y

    row = pl.BlockSpec((tm, d), lambda i: (i, 0))
    wide = pl.BlockSpec((tm, 2 * D_FF), lambda i: (i, 0))
    vec = _full((1, d))
    extra = [target] if with_loss else []
    return pl.pallas_call(
        body, grid=(nt,),
        in_specs=[row, vec, vec, ANY, ANY, _full(conv_w.shape), _full(conv_b.shape)] + [row] * n_extra,
        out_specs=[row, wide, wide, pl.BlockSpec((tm, D_FF), lambda i: (i, 0)), row, row] + [_full((1, 1))] * n_extra,
        out_shape=[jax.ShapeDtypeStruct((t, d), F32), jax.ShapeDtypeStruct((t, 2 * D_FF), BF16),
                   jax.ShapeDtypeStruct((t, 2 * D_FF), BF16), jax.ShapeDtypeStruct((t, D_FF), BF16),
                   jax.ShapeDtypeStruct((t, d), F32), jax.ShapeDtypeStruct((t, d), BF16)]
        + [jax.ShapeDtypeStruct((1, 1), F32)] * n_extra,
        scratch_shapes=[pltpu.VMEM(w_up.shape, BF16), pltpu.VMEM(w_down.shape, BF16),
                        pltpu.VMEM((N_SHARD, h8, FF_CHUNK), F32), pltpu.SemaphoreType.DMA((2,))],
        compiler_params=_params(1), name="mlp_fwd_loss" if with_loss else "mlp_fwd")(
            x, g_pre, g_post, w_up, w_down, conv_w, conv_b, *extra)


def _rowsum8(v):
    return jnp.sum(v.reshape(v.shape[0] // 8, 8, v.shape[1]), axis=0)


def mlp_bwd(dxo, f, x, u, sp, g_pre, g_post, w_up, w_down, conv_w, tm=256):
    t, d = x.shape
    nt = t // tm
    h8 = CONV_HALO

    def body(dxo_ref, f_ref, x_ref, u_ref, s_ref, g2_ref, g3_ref, wup_hbm, wdn_hbm, cw_ref,
             dx_ref, du_ref, df_ref, dg2_ref, dg3_ref, dcw_ref, dcb_ref,
             wup_v, wdn_v, carry, sem):
        @pl.when(pl.program_id(0) == 0)
        def _():
            c1 = pltpu.make_async_copy(wup_hbm, wup_v, sem.at[0])
            c2 = pltpu.make_async_copy(wdn_hbm, wdn_v, sem.at[1])
            c1.start()
            c2.start()
            carry[...] = jnp.zeros_like(carry)
            dg2_ref[...] = jnp.zeros_like(dg2_ref)
            dg3_ref[...] = jnp.zeros_like(dg3_ref)
            dcw_ref[...] = jnp.zeros_like(dcw_ref)
            dcb_ref[...] = jnp.zeros_like(dcb_ref)
            c1.wait()
            c2.wait()

        g3 = g3_ref[...]
        dxo = dxo_ref[...]
        _, fh, rf = _rms_fwd(f_ref[...], g3)
        dg3_ref[...] += _rowsum8(dxo * fh)
        dfb = _rms_bwd(fh, rf, g3, dxo).astype(BF16)
        df_ref[...] = dfb
        dh = jnp.zeros((tm, d), F32)
        for k in range(2):
            da = _dot_nt(dfb, wdn_v[k * FF_CHUNK:(k + 1) * FF_CHUNK, :])
            for s in range(2):
                j = k + 2 * s
                cols = slice((2 * k + s) * FF_CHUNK, (2 * k + s + 1) * FF_CHUNK)
                dc = da * s_ref[:, (2 * k + 1 - s) * FF_CHUNK:(2 * k + 2 - s) * FF_CHUNK].astype(F32)
                uf = u_ref[:, cols].astype(F32)
                ext = jnp.concatenate([dc, carry[j]], axis=0)
                carry[j] = dc[0:h8, :]
                dc1 = pltpu.roll(ext, tm + h8 - 1, axis=0)[0:tm, :]
                dc2 = pltpu.roll(ext, tm + h8 - 2, axis=0)[0:tm, :]
                dcb_ref[j] += _rowsum8(dc)
                dcw_ref[j, 2] += _rowsum8(dc * uf)
                dcw_ref[j, 1] += _rowsum8(dc1 * uf)
                dcw_ref[j, 0] += _rowsum8(dc2 * uf)
                dub = (cw_ref[j, 2:3, :] * dc + cw_ref[j, 1:2, :] * dc1 + cw_ref[j, 0:1, :] * dc2).astype(BF16)
                du_ref[:, cols] = dub
                dh = dh + _dot_nt(dub, wup_v[j])
        g2 = g2_ref[...]
        _, xh, rx = _rms_fwd(x_ref[...], g2)
        dg2_ref[...] += _rowsum8(dh * xh)
        dx_ref[...] = dxo + _rms_bwd(xh, rx, g2, dh)

    row = pl.BlockSpec((tm, d), lambda i: (nt - 1 - i, 0))
    wide = pl.BlockSpec((tm, 2 * D_FF), lambda i: (nt - 1 - i, 0))
    vec = _full((1, d))
    acc = _full((8, d))
    dcw_shape, dcb_shape = (N_SHARD, 3, 8, FF_CHUNK), (N_SHARD, 8, FF_CHUNK)
    return pl.pallas_call(
        body, grid=(nt,),
        in_specs=[row, row, row, wide, wide, vec, vec, ANY, ANY, _full(conv_w.shape)],
        out_specs=[row, wide, row, acc, acc, _full(dcw_shape), _full(dcb_shape)],
        out_shape=[jax.ShapeDtypeStruct((t, d), F32), jax.ShapeDtypeStruct((t, 2 * D_FF), BF16),
                   jax.ShapeDtypeStruct((t, d), BF16),
                   jax.ShapeDtypeStruct((8, d), F32), jax.ShapeDtypeStruct((8, d), F32),
                   jax.ShapeDtypeStruct(dcw_shape, F32), jax.ShapeDtypeStruct(dcb_shape, F32)],
        scratch_shapes=[pltpu.VMEM(w_up.shape, BF16), pltpu.VMEM(w_down.shape, BF16),
                        pltpu.VMEM((N_SHARD, h8, FF_CHUNK), F32), pltpu.SemaphoreType.DMA((2,))],
        compiler_params=_params(1, VMEM_LIMIT_MLP_BWD), name="mlp_bwd")(
            dxo, f, x, u, sp, g_pre, g_post, w_up, w_down, conv_w)


def grad_matmul(a, b, bm, bn, name, tk=2048, interleaved=False, after=None):
    t, m = a.shape
    n = b.shape[1]
    tk = min(tk, t)
    nk = t // tk
    place = (lambda j: (j % 2) * 2 + j // 2) if interleaved else (lambda j: j)
    extra = [] if after is None else [after]

    def body(a_ref, b_ref, *rest):
        o_ref, ob_ref = rest[len(extra):]
        kk = pl.program_id(2)

        @pl.when(kk == 0)
        def _():
            o_ref[...] = jnp.zeros_like(o_ref)

        o_ref[...] += _dot_tn(a_ref[...], b_ref[...])

        @pl.when(kk == nk - 1)
        def _():
            ob_ref[...] = o_ref[...].astype(BF16)

    ospec = pl.BlockSpec((None, bm, bn), lambda j, i, kk: (place(j), i, 0))
    return pl.pallas_call(
        body, grid=(n // bn, m // bm, nk),
        in_specs=[pl.BlockSpec((tk, bm), lambda j, i, kk: (kk, i)),
                  pl.BlockSpec((tk, bn), lambda j, i, kk: (kk, j))]
        + [ANY] * len(extra),
        out_specs=[ospec, ospec],
        out_shape=[jax.ShapeDtypeStruct((n // bn, m, bn), F32), jax.ShapeDtypeStruct((n // bn, m, bn), BF16)],
        compiler_params=_params(3), name=name)(a, b, *extra)


def _decay_tables():
    log_gamma = jnp.log(1.0 - 2.0 ** (-5.0 - jnp.arange(RET_HEADS, dtype=F32)))
    i = jnp.arange(RET_CHUNK, dtype=F32)
    rel = i[:, None] - i[None, :]
    intra = jnp.where(rel >= 0, jnp.exp(jnp.maximum(rel, 0.0) * log_gamma[:, None, None]), 0.0)
    cross = jnp.exp((i + 1.0) * log_gamma[:, None])[:, :, None]
    inner = jnp.exp((RET_CHUNK - 1.0 - i) * log_gamma[:, None])[:, :, None]
    chunk = [float(np.exp(np.float32(RET_CHUNK) * np.log(np.float32(1.0 - 2.0 ** (-5.0 - h))).astype(np.float32)))
             for h in range(RET_HEADS)]
    return intra, cross, inner, chunk


def ret_proj(x, g_pre, w_in, cos, sin, tm=512):
    t, d = x.shape
    nt = t // tm
    per = RET_IN_SHARD // RET_QK

    def body(x_ref, g_ref, win_hbm, c_ref, s_ref, pj_ref, h_ref, win_v, sem):
        @pl.when(pl.program_id(0) == 0)
        def _():
            cp = pltpu.make_async_copy(win_hbm, win_v, sem)
            cp.start()
            cp.wait()

        h, _, _ = _rms_fwd(x_ref[...], g_ref[...])
        hb = h.astype(BF16)
        h_ref[...] = hb
        c = c_ref[...]
        s = s_ref[...]
        for j in range(N_SHARD):
            pjj = _dot(hb, win_v[j])
            for bb in range(per):
                b = per * j + bb
                blk = pjj[:, bb * RET_QK:(bb + 1) * RET_QK]
                if b < 2 * RET_HEADS:
                    x1, x2 = blk[:, :128], blk[:, 128:]
                    o1 = x1 * c - x2 * s
                    o2 = x2 * c + x1 * s
                    if b < RET_HEADS:
                        o1 = o1 * (RET_QK ** -0.5)
                        o2 = o2 * (RET_QK ** -0.5)
                    pj_ref[:, b * RET_QK:b * RET_QK + 128] = o1.astype(BF16)
                    pj_ref[:, b * RET_QK + 128:(b + 1) * RET_QK] = o2.astype(BF16)
                else:
                    pj_ref[:, b * RET_QK:(b + 1) * RET_QK] = blk.astype(BF16)

    row = pl.BlockSpec((tm, d), lambda i: (i, 0))
    tab = pl.BlockSpec((tm, 128), lambda i: (i, 0))
    return pl.pallas_call(
        body, grid=(nt,),
        in_specs=[row, _full((1, d)), ANY, tab, tab],
        out_specs=[pl.BlockSpec((tm, RET_IN), lambda i: (i, 0)), row],
        out_shape=[jax.ShapeDtypeStruct((t, RET_IN), BF16), jax.ShapeDtypeStruct((t, d), BF16)],
        scratch_shapes=[pltpu.VMEM(w_in.shape, BF16), pltpu.SemaphoreType.DMA],
        compiler_params=_params(1), name="ret_proj")(x, g_pre, w_in, cos, sin)


def ret_core_fwd(pj, intra, cross, inner, chunk_decay):
    t = pj.shape[0]
    nc = t // RET_CHUNK
    c = RET_CHUNK
    per = RET_STEP_CHUNKS
    qk_all = RET_HEADS * RET_QK
    v_all = RET_HEADS * RET_V

    def body(q_ref, k_ref, v_ref, dm_ref, cr_ref, in_ref, o_ref, sp_ref, state):
        @pl.when(pl.program_id(0) == 0)
        def _():
            state[...] = jnp.zeros_like(state)

        for h in range(RET_HEADS):
            for cc in range(per):
                rows = slice(cc * c, (cc + 1) * c)
                q = q_ref[rows, h * RET_QK:(h + 1) * RET_QK]
                k = k_ref[rows, h * RET_QK:(h + 1) * RET_QK]
                v = v_ref[rows, h * RET_V:(h + 1) * RET_V]
                sb = state[h].astype(BF16)
                sp_ref[cc, h] = sb
                sc = _dot_nt(q, k) * dm_ref[h]
                o_ref[rows, h * RET_V:(h + 1) * RET_V] = (_dot(sc.astype(BF16), v)
                                                          + _dot(q, sb) * cr_ref[h]).astype(BF16)
                kd = (k.astype(F32) * in_ref[h]).astype(BF16)
                state[h] = state[h] * chunk_decay[h] + _dot_tn(kd, v)

    return pl.pallas_call(
        body, grid=(nc // per,),
        in_specs=[pl.BlockSpec((per * c, qk_all), lambda n: (n, 0)), pl.BlockSpec((per * c, qk_all), lambda n: (n, 1)),
                  pl.BlockSpec((per * c, v_all), lambda n: (n, 1)),
                  _full(intra.shape), _full(cross.shape), _full(inner.shape)],
        out_specs=[pl.BlockSpec((per * c, v_all), lambda n: (n, 0)),
                   pl.BlockSpec((per, RET_HEADS, RET_QK, RET_V), lambda n: (n, 0, 0, 0))],
        out_shape=[jax.ShapeDtypeStruct((t, v_all), BF16),
                   jax.ShapeDtypeStruct((nc, RET_HEADS, RET_QK, RET_V), BF16)],
        scratch_shapes=[pltpu.VMEM((RET_HEADS, RET_QK, RET_V), F32)],
        compiler_params=_params(1), name="ret_core_fwd")(pj, pj, pj, intra, cross, inner)


def _group_norm(o_h):
    mu = jnp.mean(o_h, axis=-1, keepdims=True)
    dev = o_h - mu
    rstd = lax.rsqrt(jnp.mean(dev * dev, axis=-1, keepdims=True) + EPS)
    return dev * rstd, rstd


def ret_out_fwd(o, pj, x, gn_gain, g_post, w_out, tm=512):
    t, d = x.shape
    nt = t // tm
    v_all = RET_HEADS * RET_V

    def body(o_ref, g_ref, x_ref, gn_ref, g1_ref, w_ref, xo_ref, y_ref, out_ref):
        out = jnp.zeros((tm, d), F32)
        for h in range(RET_HEADS):
            cols = slice(h * RET_V, (h + 1) * RET_V)
            ohat, _ = _group_norm(o_ref[:, cols].astype(F32))
            g = g_ref[:, cols].astype(F32)
            yb = (g * jax.nn.sigmoid(g) * (ohat * gn_ref[:, cols])).astype(BF16)
            y_ref[:, cols] = yb
            out = out + _dot(yb, w_ref[cols, :])
        out_ref[...] = out
        m, _, _ = _rms_fwd(out, g1_ref[...])
        xo_ref[...] = x_ref[...] + m

    row = pl.BlockSpec((tm, d), lambda i: (i, 0))
    wide = pl.BlockSpec((tm, v_all), lambda i: (i, 0))
    return pl.pallas_call(
        body, grid=(nt,),
        in_specs=[wide, pl.BlockSpec((tm, v_all), lambda i: (i, 2)), row, _full((1, v_all)), _full((1, d)),
                  _full(w_out.shape)],
        out_specs=[row, wide, row],
        out_shape=[jax.ShapeDtypeStruct((t, d), F32), jax.ShapeDtypeStruct((t, v_all), BF16),
                   jax.ShapeDtypeStruct((t, d), F32)],
        compiler_params=_params(1), name="ret_out_fwd")(o, pj, x, gn_gain, g_post, w_out)


def ret_out_bwd(dxo, out, o, pj, gn_gain, g_post, w_out, after, tm=512):
    t, d = out.shape
    nt = t // tm
    v_all = RET_HEADS * RET_V

    def body(dxo_ref, out_ref, o_ref, g_ref, gn_ref, g1_ref, w_ref, after_ref,
             dout_ref, dgate_ref, do_ref, dg1_ref, dgn_ref):
        @pl.when(pl.program_id(0) == 0)
        def _():
            dg1_ref[...] = jnp.zeros_like(dg1_ref)
            dgn_ref[...] = jnp.zeros_like(dgn_ref)

        g1 = g1_ref[...]
        dxo = dxo_ref[...]
        _, oh_, r_ = _rms_fwd(out_ref[...], g1)
        dg1_ref[...] += _colsum(dxo * oh_)
        doutb = _rms_bwd(oh_, r_, g1, dxo).astype(BF16)
        dout_ref[...] = doutb
        for h in range(RET_HEADS):
            cols = slice(h * RET_V, (h + 1) * RET_V)
            gn = gn_ref[:, cols]
            ohat, rstd = _group_norm(o_ref[:, cols].astype(F32))
            g = g_ref[:, cols].astype(F32)
            sg = jax.nn.sigmoid(g)
            dyh = _dot_nt(doutb, w_ref[cols, :])
            sil = g * sg
            dgate_ref[:, cols] = (dyh * (ohat * gn) * (sg + sil * (1.0 - sg))).astype(BF16)
            don = dyh * sil
            dgn_ref[:, cols] += _colsum(don * ohat)
            dohat = don * gn
            do_ref[:, cols] = (rstd * (dohat - jnp.mean(dohat, axis=-1, keepdims=True)
                                       - ohat * jnp.mean(dohat * ohat, axis=-1, keepdims=True))).astype(BF16)

    row = pl.BlockSpec((tm, d), lambda i: (i, 0))
    wide = pl.BlockSpec((tm, v_all), lambda i: (i, 0))
    gate = pl.BlockSpec((tm, v_all), lambda i: (i, 2))
    return pl.pallas_call(
        body, grid=(nt,),
        in_specs=[row, row, wide, gate, _full((1, v_all)), _full((1, d)), _full(w_out.shape), ANY],
        out_specs=[row, gate, wide, _full((1, d)), _full((1, v_all))],
        out_shape=[jax.ShapeDtypeStruct((t, d), BF16), jax.ShapeDtypeStruct((t, RET_IN), BF16),
                   jax.ShapeDtypeStruct((t, v_all), BF16), jax.ShapeDtypeStruct((1, d), F32),
                   jax.ShapeDtypeStruct((1, v_all), F32)],
        compiler_params=_params(1), name="ret_out_bwd")(dxo, out, o, pj, gn_gain, g_post, w_out, after)


def ret_core_bwd(pj, do, sprev, cos, sin, dpj, intra, cross, inner, chunk_decay):
    t = pj.shape[0]
    nc = t // RET_CHUNK
    c = RET_CHUNK
    per = RET_STEP_CHUNKS
    qk_all = RET_HEADS * RET_QK
    v_all = RET_HEADS * RET_V
    scale = RET_QK ** -0.5

    def body(q_ref, k_ref, v_ref, do_ref, sp_ref, c_ref, s_ref, dm_ref, cr_ref, in_ref, dpj_in, dpj_ref, dstate):
        @pl.when(pl.program_id(0) == 0)
        def _():
            dstate[...] = jnp.zeros_like(dstate)

        for h in range(RET_HEADS):
            for cc in reversed(range(per)):
                rows = slice(cc * c, (cc + 1) * c)
                cs = c_ref[rows, :]
                sn = s_ref[rows, :]
                q = q_ref[rows, h * RET_QK:(h + 1) * RET_QK]
                k = k_ref[rows, h * RET_QK:(h + 1) * RET_QK]
                v = v_ref[rows, h * RET_V:(h + 1) * RET_V]
                doh = do_ref[rows, h * RET_V:(h + 1) * RET_V]
                dm = dm_ref[h]
                ab = (_dot_nt(q, k) * dm).astype(BF16)
                dab = (_dot_nt(doh, v) * dm).astype(BF16)
                dsb = dstate[h].astype(BF16)
                kd = (k.astype(F32) * in_ref[h]).astype(BF16)
                dv = _dot_tn(ab, doh) + _dot(kd, dsb)
                dq = _dot(dab, k) + cr_ref[h] * _dot_nt(doh, sp_ref[cc, h])
                dk = _dot_tn(dab, q) + in_ref[h] * _dot_nt(v, dsb)
                qd = (q.astype(F32) * cr_ref[h]).astype(BF16)
                dstate[h] = dstate[h] * chunk_decay[h] + _dot_tn(qd, doh)
                for base, dd, sc in ((h * RET_QK, dq, scale), (qk_all + h * RET_QK, dk, 1.0)):
                    d1, d2 = dd[:, :128], dd[:, 128:]
                    dpj_ref[rows, base:base + 128] = ((d1 * cs + d2 * sn) * sc).astype(BF16)
                    dpj_ref[rows, base + 128:base + RET_QK] = ((d2 * cs - d1 * sn) * sc).astype(BF16)
                dpj_ref[rows, 2 * qk_all + h * RET_V:2 * qk_all + (h + 1) * RET_V] = dv.astype(BF16)

    rev = lambda n: nc // per - 1 - n
    tab = pl.BlockSpec((per * c, 128), lambda n: (rev(n), 0))
    return pl.pallas_call(
        body, grid=(nc // per,),
        in_specs=[pl.BlockSpec((per * c, qk_all), lambda n: (rev(n), 0)),
                  pl.BlockSpec((per * c, qk_all), lambda n: (rev(n), 1)),
                  pl.BlockSpec((per * c, v_all), lambda n: (rev(n), 1)),
                  pl.BlockSpec((per * c, v_all), lambda n: (rev(n), 0)),
                  pl.BlockSpec((per, RET_HEADS, RET_QK, RET_V), lambda n: (rev(n), 0, 0, 0)),
                  tab, tab, _full(intra.shape), _full(cross.shape), _full(inner.shape), ANY],
        out_specs=pl.BlockSpec((per * c, 2 * qk_all + v_all), lambda n: (rev(n), 0)),
        out_shape=jax.ShapeDtypeStruct((t, RET_IN), BF16),
        scratch_shapes=[pltpu.VMEM((RET_HEADS, RET_QK, RET_V), F32)],
        input_output_aliases={10: 0},
        compiler_params=_params(1), name="ret_core_bwd")(pj, pj, pj, do, sprev, cos, sin, intra, cross, inner, dpj)


def ret_in_bwd(dpj, dres, x, g_pre, w_in, after, tm=512):
    t, d = x.shape
    nt = t // tm

    def body(dpj_ref, dres_ref, x_ref, g_ref, win_hbm, after_ref, dx_ref, dg_ref, win_v, sem):
        @pl.when(pl.program_id(0) == 0)
        def _():
            cp = pltpu.make_async_copy(win_hbm, win_v, sem)
            cp.start()
            dg_ref[...] = jnp.zeros_like(dg_ref)
            cp.wait()

        g = g_ref[...]
        dh = jnp.zeros((tm, d), F32)
        for j in range(N_SHARD):
            dh = dh + _dot_nt(dpj_ref[:, j * RET_IN_SHARD:(j + 1) * RET_IN_SHARD], win_v[j])
        _, xh, rx = _rms_fwd(x_ref[...], g)
        dg_ref[...] += _colsum(dh * xh)
        dx_ref[...] = dres_ref[...] + _rms_bwd(xh, rx, g, dh)

    row = pl.BlockSpec((tm, d), lambda i: (i, 0))
    return pl.pallas_call(
        body, grid=(nt,),
        in_specs=[pl.BlockSpec((tm, RET_IN), lambda i: (i, 0)), row, row, _full((1, d)), ANY, ANY],
        out_specs=[row, _full((1, d))],
        out_shape=[jax.ShapeDtypeStruct((t, d), F32), jax.ShapeDtypeStruct((1, d), F32)],
        scratch_shapes=[pltpu.VMEM(w_in.shape, BF16), pltpu.SemaphoreType.DMA],
        compiler_params=_params(1), name="ret_in_bwd")(dpj, dres, x, g_pre, w_in, after)


_CHIP_FLIPS = ((1, 0), (0, 1), (1, 1))


def _flip(v, b):
    return 1 - v if b else v


_HBM = pl.BlockSpec(memory_space=pltpu.HBM)
_SEM = pl.BlockSpec(memory_space=pltpu.SEMAPHORE)
_EFFECT = pltpu.SideEffectType.DATAFLOW_SIDE_EFFECTING


def _chip_copies(mode, srcs, lands, send_sems, recv_sems):
    x, y, c = lax.axis_index("x"), lax.axis_index("y"), lax.axis_index("c")
    copies = []
    for t in range(len(lands)):
        if mode == "swap":
            copies.append(pltpu.make_async_remote_copy(
                src_ref=srcs[t], dst_ref=lands[t], send_sem=send_sems.at[t], recv_sem=recv_sems.at[t],
                device_id=(x, y, 1 - c), device_id_type=MESH))
            continue
        if mode == "everyone":
            for m in range(1, 8):
                bx, by, bc = (m >> 2) & 1, (m >> 1) & 1, m & 1
                copies.append(pltpu.make_async_remote_copy(
                    src_ref=srcs[t], dst_ref=lands[t].at[4 * x + 2 * y + c], send_sem=send_sems.at[7 * t + m - 1],
                    recv_sem=recv_sems.at[7 * t + m - 1], device_id=(_flip(x, bx), _flip(y, by), _flip(c, bc)),
                    device_id_type=MESH))
            continue
        for k, (bx, by) in enumerate(_CHIP_FLIPS):
            px, py = _flip(x, bx), _flip(y, by)
            target = (px, py, c)
            if mode == "gather":
                src, dst = srcs[t], lands[t].at[2 * x + y]
            elif mode == "gather_half":
                half = pl.ds(c * (srcs[t].shape[0] // 2), srcs[t].shape[0] // 2)
                src, dst = srcs[t].at[half], lands[t].at[2 * x + y, half]
            elif mode == "forward_half":
                half = pl.ds(c * (lands[t].shape[1] // 2), lands[t].shape[1] // 2)
                src = dst = lands[t].at[2 * px + py, half]
                target = (x, y, 1 - c)
            else:
                src, dst = srcs[t].at[2 * px + py], lands[t].at[k]
            copies.append(pltpu.make_async_remote_copy(
                src_ref=src, dst_ref=dst, send_sem=send_sems.at[3 * t + k], recv_sem=recv_sems.at[3 * t + k],
                device_id=target, device_id_type=MESH))
    return copies


def exchange_start(mode, srcs, lands, name, after=None):
    n, ns = len(lands), len(srcs)
    extra = [] if after is None else [after]

    def body(*refs):
        ins, lnd = refs[:ns], refs[ns:ns + n]
        send_sems, recv_sems = refs[ns + n + len(extra)], refs[ns + n + len(extra) + 1]
        token = refs[-1]
        for cp in _chip_copies(mode, ins, lnd, send_sems, recv_sems):
            cp.start()
        token[...] = jnp.zeros(token.shape, token.dtype)

    hbm = lambda a: pltpu.with_memory_space_constraint(a, pltpu.HBM)
    passed = list(srcs) + list(lands)
    n_sem = {"swap": 1, "everyone": 7}.get(mode, 3) * n
    return pl.pallas_call(
        body, name=name,
        out_shape=(pltpu.SemaphoreType.DMA((n_sem,)), pltpu.SemaphoreType.DMA((n_sem,)),
                   *[pltpu.HBM(a.shape, a.dtype) for a in passed], jax.ShapeDtypeStruct((8, 128), F32)),
        in_specs=[_HBM] * (ns + n) + [ANY] * len(extra),
        out_specs=(_SEM, _SEM, *[_HBM] * (ns + n), pl.BlockSpec(memory_space=pltpu.VMEM)),
        input_output_aliases={i: 2 + i for i in range(ns + n)},
        compiler_params=pltpu.CompilerParams(has_side_effects=_EFFECT))(*[hbm(a) for a in passed], *extra)


def exchange_wait(mode, started, after, name):
    send_sems, recv_sems = started[0], started[1]
    passed = list(started[2:-1])
    n = len(passed) if mode == "forward_half" else len(passed) // 2
    ns = len(passed) - n
    after = list(after) if isinstance(after, (list, tuple)) else [after]

    def body(*refs):
        ins, lnd = refs[:ns], refs[ns:ns + n]
        for cp in _chip_copies(mode, ins, lnd, refs[ns + n], refs[ns + n + 1]):
            cp.wait_send()
            cp.wait_recv()

    outs = pl.pallas_call(
        body, name=name, out_shape=tuple(pltpu.HBM(a.shape, a.dtype) for a in passed),
        in_specs=[_HBM] * (ns + n) + [_SEM, _SEM] + [ANY] * len(after), out_specs=tuple([_HBM] * (ns + n)),
        input_output_aliases={i: i for i in range(ns + n)},
        compiler_params=pltpu.CompilerParams(has_side_effects=_EFFECT))(*passed, send_sems, recv_sems, *after)
    return list(outs[:ns]), list(outs[ns:])


def plane_sum(slot, full, recv, name, bm=256):
    _, m, n = full.shape
    bm = _row_block(m, bm)

    def body(slot_ref, o_ref, r_ref, s_ref):
        s_ref[...] = ((o_ref[...] + r_ref[0].astype(F32)) + r_ref[1].astype(F32)) + r_ref[2].astype(F32)

    return pl.pallas_call(
        body,
        grid_spec=pltpu.PrefetchScalarGridSpec(
            num_scalar_prefetch=1, grid=(m // bm,),
            in_specs=[pl.BlockSpec((None, bm, n), lambda i, s: (s[0], i, 0)),
                      pl.BlockSpec((3, bm, n), lambda i, s: (0, i, 0))],
            out_specs=pl.BlockSpec((bm, n), lambda i, s: (i, 0))),
        out_shape=jax.ShapeDtypeStruct((m, n), F32), compiler_params=_params(1), name=name)(slot, full, recv)


def cast_shard(slot, w, name, bm=256):
    m, n = w.shape
    bm = _row_block(m, bm)

    def body(slot_ref, w_ref, s_ref, l_ref):
        wb = w_ref[...].astype(BF16)
        s_ref[...] = wb
        l_ref[...] = wb

    return pl.pallas_call(
        body,
        grid_spec=pltpu.PrefetchScalarGridSpec(
            num_scalar_prefetch=1, grid=(m // bm,),
            in_specs=[pl.BlockSpec((bm, n), lambda i, s: (i, 0))],
            out_specs=[pl.BlockSpec((bm, n), lambda i, s: (i, 0)),
                       pl.BlockSpec((None, bm, n), lambda i, s: (s[0], i, 0))]),
        out_shape=[jax.ShapeDtypeStruct((m, n), BF16), jax.ShapeDtypeStruct((N_SHARD, m, n), BF16)],
        compiler_params=_params(1), name=name)(slot, w)


def sum_slots(parts, name, bm=312):
    _, r, n = parts.shape
    bm = bm if r % bm == 0 else r

    def body(p_ref, s_ref):
        acc = p_ref[0]
        for k in range(1, 8):
            acc = acc + p_ref[k]
        s_ref[...] = acc

    return pl.pallas_call(
        body, grid=(r // bm,), in_specs=[pl.BlockSpec((8, bm, n), lambda i: (0, i, 0))],
        out_specs=pl.BlockSpec((bm, n), lambda i: (i, 0)), out_shape=jax.ShapeDtypeStruct((r, n), F32),
        compiler_params=_params(1), name=name)(parts)


def _adamw_math(w, g, m, v):
    m = ADAM_B1 * m + (1.0 - ADAM_B1) * g
    v = ADAM_B2 * v + (1.0 - ADAM_B2) * (g * g)
    m_hat = m / (1.0 - ADAM_B1 ** ADAM_STEP)
    v_hat = v / (1.0 - ADAM_B2 ** ADAM_STEP)
    delta = -ADAM_LR * (m_hat / (jnp.sqrt(v_hat) + ADAM_EPS) + ADAM_WD * w)
    return delta, m, v


def adamw(w, m, v, grads, layer, prev, name, bm=256):
    _, _, n = w.shape
    mm = grads[0].shape[0]
    bm = _row_block(mm, bm)
    ng = len(grads)

    def body(*refs):
        w_ref, m_ref, v_ref = refs[:3]
        g_refs = refs[3:3 + ng]
        g_out, d_out, m_out, v_out = refs[-4:]
        g = g_refs[0][...]
        for gr in g_refs[1:]:
            g = g + gr[...]
        delta, mn, vn = _adamw_math(w_ref[...], g, m_ref[...], v_ref[...])
        g_out[...] = g
        d_out[...] = delta
        m_out[...] = mn
        v_out[...] = vn

    slab = pl.BlockSpec((None, bm, n), lambda i: (layer, i, 0))
    flat = pl.BlockSpec((bm, n), lambda i: (i, 0))
    in_specs = [slab] * 3 + [flat] * ng
    args = [w, m, v, *grads]
    aliases = {}
    if prev is not None:
        in_specs += [ANY] * 4
        aliases = {3 + ng + q: q for q in range(4)}
        args += list(prev)
    return pl.pallas_call(
        body, grid=(mm // bm,), in_specs=in_specs, out_specs=[slab] * 4,
        out_shape=[jax.ShapeDtypeStruct(w.shape, F32)] * 4, input_output_aliases=aliases,
        compiler_params=_params(1), name=name)(*args)


def _pack_rows(parts, rows):
    flat = jnp.concatenate([p.reshape(-1) for p in parts])
    return jnp.pad(flat, (0, rows * 128 - flat.shape[0])).reshape(rows, 128)


def _as_shards(a, rows):
    return a.reshape(N_SHARD, rows, a.shape[-1])


def _local_step(x, pos_col, target, gains, pool_w, pool_scale, gn_gain, conv_w, conv_b, weights, send_grads):
    def gain(l, n):
        return gains[l, n].reshape(1, D_MODEL)

    inv_freq = (ROPE_BASE ** (-jnp.arange(0, RET_QK, 2, dtype=F32) / RET_QK)).reshape(1, RET_QK // 2)
    intra, cross, inner, chunk_decay = _decay_tables()
    dn_rows = D_FF // N_SHARD

    x1 = pool_fwd(x, gain(0, 0), gain(0, 1), pool_w, pool_scale)
    cos, sin = rope_tables(pos_col, inv_freq, x1)
    w_up0, w_dn0 = weights("mlp0", cos)
    w_dn0 = w_dn0.reshape(D_FF, D_MODEL)
    x2, u0, s0, a0, f0, h0 = mlp_fwd(x1, gain(0, 2), gain(0, 3), w_up0, w_dn0, conv_w[0], conv_b[0])
    w_in, w_out = weights("ret", x2)
    w_out = w_out.reshape(RET_HEADS * RET_V, D_MODEL)
    pj, hr = ret_proj(x2, gain(1, 0), w_in, cos, sin)
    o, sprev = ret_core_fwd(pj, intra, cross, inner, chunk_decay)
    x3, yb, out = ret_out_fwd(o, pj, x2, gn_gain, gain(1, 1), w_out)
    w_up1, w_dn1 = weights("mlp1", x3)
    w_dn1 = w_dn1.reshape(D_FF, D_MODEL)
    dx4, u1, s1, a1, f1, h1, loss = mlp_fwd(x3, gain(1, 2), gain(1, 3), w_up1, w_dn1, conv_w[1], conv_b[1], target)

    dx3, du1, df1, dg12, dg13, dcw1, dcb1 = mlp_bwd(
        dx4, f1, x3, u1, s1, gain(1, 2), gain(1, 3), w_up1, w_dn1, conv_w[1])
    dwup1 = grad_matmul(h1, du1, D_MODEL, FF_CHUNK, "grad_w_up_1", interleaved=True)
    dwdn1 = grad_matmul(a1, df1, FF_CHUNK, D_MODEL, "grad_w_down_1")
    tok = send_grads("mlp1", [dwup1, [_as_shards(g, dn_rows) for g in dwdn1]])
    dout, dpj, do, dg11, dgn = ret_out_bwd(dx3, out, o, pj, gn_gain, gain(1, 1), w_out, tok)
    dwout = grad_matmul(yb, dout, 1024, D_MODEL, "grad_w_out")
    dpj = ret_core_bwd(pj, do, sprev, cos, sin, dpj, intra, cross, inner, chunk_decay)
    dwin = grad_matmul(hr, dpj, D_MODEL, RET_IN_SHARD, "grad_w_in")
    tok = send_grads("ret", [dwin, [_as_shards(g, RET_V) for g in dwout]])
    dx2, dg10 = ret_in_bwd(dpj, dx3, x2, gain(1, 0), w_in, tok)
    dx1, du0, df0, dg02, dg03, dcw0, dcb0 = mlp_bwd(
        dx2, f0, x1, u0, s0, gain(0, 2), gain(0, 3), w_up0, w_dn0, conv_w[0])
    dwdn0 = grad_matmul(a0, df0, FF_CHUNK, D_MODEL, "grad_w_down_0")
    tok = send_grads("down0", [[_as_shards(g, dn_rows) for g in dwdn0]])
    dwup0 = grad_matmul(h0, du0, D_MODEL, FF_CHUNK, "grad_w_up_0", interleaved=True, after=tok)
    tok = send_grads("up0", [dwup0])
    dx0, dg00, dg01, dpscale, dpw = pool_bwd(dx1, x, gain(0, 0), gain(0, 1), pool_w, pool_scale, tok)

    rows = lambda g: jnp.sum(g, axis=0, keepdims=True)
    dgains = jnp.concatenate([dg00, dg01, rows(dg02), rows(dg03), dg10, dg11, rows(dg12), rows(dg13)],
                             axis=0).reshape(2, 4, D_MODEL)
    small = {"gains": dgains, "pool_scale": dpscale, "gn": dgn,
             "conv_w": jnp.sum(jnp.stack([dcw0, dcw1]), axis=3),
             "conv_b": jnp.sum(jnp.stack([dcb0, dcb1]), axis=2, keepdims=True), "pool_w": dpw}
    return loss, dx0, small


def kernel(x, positions, norm_gain, pool_w, pool_scale, ret_w_in, ret_gn_gain, ret_w_out, mlp_w_up, mlp_conv_w, mlp_conv_b, mlp_w_down, loss_target, m_norm_gain, m_pool_w, m_pool_scale, m_ret_w_in, m_ret_gn_gain, m_ret_w_out, m_mlp_w_up, m_mlp_conv_w, m_mlp_conv_b, m_mlp_w_down, v_norm_gain, v_pool_w, v_pool_scale, v_ret_w_in, v_ret_gn_gain, v_ret_w_out, v_mlp_w_up, v_mlp_conv_w, v_mlp_conv_b, v_mlp_w_down):
    t = x.shape[1]
    me = 2 * lax.axis_index("x") + lax.axis_index("y")
    me_slot = jnp.reshape(me, (1,)).astype(jnp.int32)

    small_parts = [norm_gain, ret_gn_gain, mlp_conv_w, pool_w]
    small_sizes = [p.size for p in small_parts]
    small_rows = -(-sum(small_sizes) // (128 * 8)) * 8
    gathers = {}

    def start_gather(group, srcs, after, lands=None):
        if lands is None:
            lands = [lax.dynamic_update_index_in_dim(lax.empty((N_SHARD,) + s.shape, s.dtype), s, me, 0)
                     for s in srcs]
        mode = "gather_half" if group == "mlp0" else "gather"
        gathers[group] = (mode, exchange_start(mode, srcs, lands, "gather_start_" + group, after=after))
        return gathers[group][1][-1]

    token = start_gather("small", [_pack_rows(small_parts, small_rows)], None)
    first = [cast_shard(me_slot, mlp_w_up[0], "cast_w_up_0"), cast_shard(me_slot, mlp_w_down[0], "cast_w_down_0")]
    token = start_gather("mlp0", [s for s, _ in first], token, lands=[l for _, l in first])

    def weights(group, after):
        if group == "mlp0":
            tok = start_gather("ret", [ret_w_in[0].astype(BF16), ret_w_out[0].astype(BF16)], after)
            after = start_gather("mlp1", [mlp_w_up[1].astype(BF16), mlp_w_down[1].astype(BF16)], tok)
        mode, started = gathers[group]
        _, lands = exchange_wait(mode, started, after, "gather_wait_" + group)
        if mode == "gather_half":
            forward = exchange_start("forward_half", [], lands, "forward_start_" + group)
            _, lands = exchange_wait("forward_half", forward, forward[-1], "forward_wait_" + group)
        return lands

    sent, early = {}, {}

    def reduced(group, after, names):
        started, own = sent[group]
        _, recv = exchange_wait("scatter", started, after, "scatter_wait_" + group)
        return [plane_sum(me_slot, f, r, "plane_sum_" + nm)
                for f, r, nm in zip(own, recv, names)]

    def swap_start(planes, name):
        return exchange_start("swap", planes, [lax.empty(p.shape, p.dtype) for p in planes], name)

    def send_grads(group, pairs):
        lands = [lax.empty((3,) + b.shape[1:], BF16) for _, b in pairs]
        sent[group] = (exchange_start("scatter", [b for _, b in pairs], lands, "scatter_start_" + group),
                       [f for f, _ in pairs])
        token = sent[group][0][-1]
        if group == "up0":
            early["planes"] = (reduced("mlp1", token, ["w_up_1", "w_down_1"])
                               + reduced("ret", token, ["w_in", "w_out"]))
            early["swap"] = swap_start(early["planes"], "swap_start_a")
            token = early["swap"][-1]
        return token

    (smallg,) = weights("small", token)
    smallg = smallg.reshape(N_SHARD, -1)
    offs = np.cumsum([0] + small_sizes)
    piece = lambda i, shape: smallg[:, offs[i]:offs[i + 1]].reshape((N_SHARD,) + shape)
    gains = piece(0, (2, 4, 256)).transpose(1, 2, 0, 3).reshape(2, 4, D_MODEL)
    gn_full = piece(1, (512,)).reshape(1, RET_HEADS * RET_V)
    cw_full = piece(2, (2, 3, FF_CHUNK)).transpose(1, 0, 2, 3)
    pw_full = piece(3, (4, 64, 256)).transpose(1, 0, 2, 3).reshape(4, 256, 256).astype(BF16)
    cb_full = mlp_conv_b.reshape(2, N_SHARD, 1, FF_CHUNK)

    loss, dx0, small = _local_step(
        x[0], positions.reshape(t, 1).astype(F32), loss_target[0], gains, pw_full, pool_scale, gn_full,
        cw_full, cb_full, weights, send_grads)

    def small_adamw(w, m, v, grads, name):
        w3 = w.reshape(1, -1, w.shape[-1])
        out = adamw(w3, m.reshape(w3.shape), v.reshape(w3.shape), [g.reshape(w3.shape[1:]) for g in grads], 0, None, name)
        return [o.reshape(w.shape) for o in out]

    pw_f = small["pool_w"].reshape(4, N_SHARD, 64, 256).transpose(1, 0, 2, 3).reshape(N_SHARD, 256, 256)
    small = dict(small, loss=loss)
    small_order = ["gains", "pool_scale", "gn", "conv_w", "conv_b", "loss"]
    gsmall_sizes = [small[k].size for k in small_order]
    gsmall_rows = -(-sum(gsmall_sizes) // (128 * 8)) * 8
    gpack = _pack_rows([small[k] for k in small_order], gsmall_rows)
    mine = 2 * me + lax.axis_index("c")
    small_started = exchange_start(
        "everyone", [gpack], [lax.dynamic_update_index_in_dim(lax.empty((8,) + gpack.shape, F32), gpack, mine, 0)],
        "small_start")
    pool_token = send_grads("pool_w", [(pw_f, pw_f.astype(BF16))])

    res = {}
    planes_a, others_a = exchange_wait("swap", early["swap"], [small_started[-1], pool_token], "swap_wait_a")
    res["ret_w_in"] = adamw(ret_w_in, m_ret_w_in, v_ret_w_in, (planes_a[2], others_a[2]), 0, None, "adamw_w_in")
    res["ret_w_out"] = adamw(ret_w_out, m_ret_w_out, v_ret_w_out, (planes_a[3], others_a[3]), 0, None, "adamw_w_out")
    up1 = adamw(mlp_w_up, m_mlp_w_up, v_mlp_w_up, (planes_a[0], others_a[0]), 1, None, "adamw_w_up_1")
    dn1 = adamw(mlp_w_down, m_mlp_w_down, v_mlp_w_down, (planes_a[1], others_a[1]), 1, None, "adamw_w_down_1")

    done_a = [res["ret_w_in"][0], res["ret_w_out"][0], up1[0], dn1[0]]
    planes_b = (reduced("up0", done_a, ["w_up_0"]) + reduced("down0", done_a, ["w_down_0"])
                + reduced("pool_w", done_a, ["pool_w"]))
    swap_b = swap_start(planes_b, "swap_start_b")

    _, (small_recv,) = exchange_wait("everyone", small_started, swap_b[-1], "small_wait")
    gsmall = sum_slots(small_recv, "sum_small").reshape(-1)
    goffs = np.cumsum([0] + gsmall_sizes)
    gpiece = lambda i: gsmall[goffs[i]:goffs[i + 1]].reshape(small[small_order[i]].shape)
    g_gains = lax.dynamic_slice_in_dim(gpiece(0), me * 256, 256, axis=2)
    g_gn = lax.dynamic_slice_in_dim(gpiece(2), me * RET_V, RET_V, axis=1)
    g_cw = lax.dynamic_index_in_dim(gpiece(3), me, 1, keepdims=False)
    res["norm_gain"] = small_adamw(norm_gain, m_norm_gain, v_norm_gain, [g_gains], "adamw_norm_gain")
    res["pool_scale"] = small_adamw(pool_scale, m_pool_scale, v_pool_scale, [gpiece(1)], "adamw_pool_scale")
    res["ret_gn_gain"] = small_adamw(ret_gn_gain, m_ret_gn_gain, v_ret_gn_gain, [g_gn], "adamw_gn_gain")
    res["mlp_conv_w"] = small_adamw(mlp_conv_w, m_mlp_conv_w, v_mlp_conv_w, [g_cw], "adamw_conv_w")
    res["mlp_conv_b"] = small_adamw(mlp_conv_b, m_mlp_conv_b, v_mlp_conv_b, [gpiece(4)], "adamw_conv_b")

    small_done = [res[k][0] for k in ("norm_gain", "pool_scale", "ret_gn_gain", "mlp_conv_w", "mlp_conv_b")]
    planes_b, others_b = exchange_wait("swap", swap_b, small_done, "swap_wait_b")
    res["mlp_w_up"] = adamw(mlp_w_up, m_mlp_w_up, v_mlp_w_up, (planes_b[0], others_b[0]), 0, up1, "adamw_w_up_0")
    res["mlp_w_down"] = adamw(mlp_w_down, m_mlp_w_down, v_mlp_w_down, (planes_b[1], others_b[1]), 0, dn1,
                              "adamw_w_down_0")
    res["pool_w"] = small_adamw(pool_w, m_pool_w, v_pool_w, (planes_b[2], others_b[2]), "adamw_pool_w")

    order = ["norm_gain", "pool_w", "pool_scale", "ret_w_in", "ret_gn_gain", "ret_w_out", "mlp_w_up", "mlp_conv_w",
             "mlp_conv_b", "mlp_w_down"]
    outs = [gpiece(5)[0, 0], dx0.reshape(x.shape)]
    for q in range(4):
        outs += [res[k][q] for k in order]
    return tuple(outs)
```

```python
import numpy as np
import jax
import jax.numpy as jnp
from jax import lax
from jax.experimental import pallas as pl
from jax.experimental.pallas import tpu as pltpu

F32 = jnp.float32
BF16 = jnp.bfloat16

D_MODEL = 1024
D_FF = 2816
FF_CHUNK = 1408
N_SHARD = 4
POOL_WINDOWS = (2, 4, 8, 16)
POOL_DIM = 256
POOL_HALO = 16
RET_HEADS = 4
RET_QK = 256
RET_V = 512
RET_CHUNK = 256
RET_STEP_CHUNKS = 2
RET_IN = 6144
RET_IN_SHARD = 1536
ROPE_BASE = 10000.0
EPS = 1e-6
CONV_HALO = 8

ADAM_LR, ADAM_B1, ADAM_B2, ADAM_EPS, ADAM_WD, ADAM_STEP = 0.001, 0.9, 0.999, 1e-08, 0.01, 10

VMEM_LIMIT = 56 * 1024 * 1024
VMEM_LIMIT_MLP_BWD = 62 * 1024 * 1024
MESH = pl.DeviceIdType.MESH
ANY = pl.BlockSpec(memory_space=pl.ANY)


def _params(n_grid=1, limit=VMEM_LIMIT):
    return pltpu.CompilerParams(dimension_semantics=("arbitrary",) * n_grid, vmem_limit_bytes=limit)


def _dot(a, b):
    return jnp.dot(a, b, preferred_element_type=F32)


def _dot_nt(a, b):
    return lax.dot_general(a, b, (((1,), (1,)), ((), ())), preferred_element_type=F32)


def _dot_tn(a, b):
    return lax.dot_general(a, b, (((0,), (0,)), ((), ())), preferred_element_type=F32)


def _rms_fwd(x, gain):
    r = lax.rsqrt(jnp.mean(x * x, axis=-1, keepdims=True) + EPS)
    xh = x * r
    return xh * gain, xh, r


def _rms_bwd(xh, r, gain, dy):
    dxh = dy * gain
    return r * (dxh - xh * jnp.mean(dxh * xh, axis=-1, keepdims=True))


def _colsum(v):
    return jnp.sum(v, axis=0, keepdims=True)


def _full(shape):
    nd = len(shape)
    return pl.BlockSpec(shape, lambda *_: (0,) * nd)


def rope_tables(pos_col, inv_freq, after):
    t = pos_col.shape[0]
    tm = min(t, 1024)

    def body(p_ref, f_ref, after_ref, c_ref, s_ref):
        ang = p_ref[...] * f_ref[...]
        c_ref[...] = jnp.cos(ang)
        s_ref[...] = jnp.sin(ang)

    return pl.pallas_call(
        body, grid=(t // tm,),
        in_specs=[pl.BlockSpec((tm, 1), lambda i: (i, 0)), _full((1, 128)), ANY],
        out_specs=[pl.BlockSpec((tm, 128), lambda i: (i, 0))] * 2,
        out_shape=[jax.ShapeDtypeStruct((t, 128), F32)] * 2,
        compiler_params=_params(1), name="rope_tables")(pos_col, inv_freq, after)


def _window_sums(ext, backward):
    n = ext.shape[0]
    cur, sums = ext, []
    for g, win in enumerate(POOL_WINDOWS):
        if g > 0:
            cur = cur[:, POOL_DIM:]
        half = win // 2
        cur = cur + pltpu.roll(cur, n - half if backward else half, axis=0)
        sums.append(cur[:, 0:POOL_DIM])
    return sums


def _pool_diff(h_halo, h, row0, tm):
    t_idx = row0 + lax.broadcasted_iota(jnp.int32, (tm, 1), 0)
    sums = _window_sums(jnp.concatenate([h_halo, h], axis=0), backward=False)
    parts, inv_counts = [], []
    for g, win in enumerate(POOL_WINDOWS):
        inv = 1.0 / jnp.minimum(t_idx + 1, win).astype(F32)
        parts.append(sums[g][POOL_HALO:, :] * inv - h[:, g * POOL_DIM:(g + 1) * POOL_DIM])
        inv_counts.append(inv)
    return parts, inv_counts


def pool_fwd(x, g_pre, g_post, pool_w, pool_scale, tm=512):
    t, d = x.shape
    nt = t // tm

    def body(x_ref, g0_ref, g1_ref, w_ref, sc_ref, o_ref, hext):
        i = pl.program_id(0)

        @pl.when(i == 0)
        def _():
            hext[...] = jnp.zeros((POOL_HALO, d), F32)

        xv = x_ref[...]
        h, _, _ = _rms_fwd(xv, g0_ref[...])
        parts, _ = _pool_diff(hext[...], h, i * tm, tm)
        hext[...] = h[tm - POOL_HALO:tm, :]
        ys = [_dot(parts[g].astype(BF16), w_ref[g]) for g in range(len(POOL_WINDOWS))]
        y = jnp.concatenate(ys, axis=-1) * sc_ref[...]
        m, _, _ = _rms_fwd(y, g1_ref[...])
        o_ref[...] = xv + m

    row = pl.BlockSpec((tm, d), lambda i: (i, 0))
    return pl.pallas_call(
        body, grid=(nt,),
        in_specs=[row, _full((1, d)), _full((1, d)), _full(pool_w.shape), _full((1, d))],
        out_specs=row, out_shape=jax.ShapeDtypeStruct((t, d), F32),
        scratch_shapes=[pltpu.VMEM((POOL_HALO, d), F32)],
        compiler_params=_params(1), name="pool_fwd")(x, g_pre, g_post, pool_w, pool_scale)


def pool_bwd(dx1, x, g_pre, g_post, pool_w, pool_scale, after, tm=512):
    t, d = x.shape
    nt = t // tm
    ng = len(POOL_WINDOWS)

    def body(dx1_ref, x_ref, xh_ref, g0_ref, g1_ref, w_ref, sc_ref, after_ref,
             dx_ref, dg0_ref, dg1_ref, dsc_ref, dw_ref, enext):
        i = pl.program_id(0)
        r = nt - 1 - i

        @pl.when(i == 0)
        def _():
            enext[...] = jnp.zeros((POOL_HALO, d), F32)
            dg0_ref[...] = jnp.zeros_like(dg0_ref)
            dg1_ref[...] = jnp.zeros_like(dg1_ref)
            dsc_ref[...] = jnp.zeros_like(dsc_ref)
            dw_ref[...] = jnp.zeros_like(dw_ref)

        g0 = g0_ref[...]
        g1 = g1_ref[...]
        sc = sc_ref[...]
        xv = x_ref[...]
        h, xh, rx = _rms_fwd(xv, g0)
        h_halo, _, _ = _rms_fwd(xh_ref[...], g0)
        parts, inv_counts = _pool_diff(h_halo * jnp.where(r > 0, 1.0, 0.0), h, r * tm, tm)
        parts_b = [p.astype(BF16) for p in parts]
        ypre = jnp.concatenate([_dot(parts_b[g], w_ref[g]) for g in range(ng)], axis=-1)
        _, yh, ry = _rms_fwd(ypre * sc, g1)
        dm = dx1_ref[...]
        dg1_ref[...] += _colsum(dm * yh)
        dy = _rms_bwd(yh, ry, g1, dm)
        dsc_ref[...] += _colsum(dy * ypre)
        dyp = (dy * sc).astype(BF16)
        ddiffs = []
        for g in range(ng):
            cols = slice(g * POOL_DIM, (g + 1) * POOL_DIM)
            dw_ref[g] += _dot_tn(parts_b[g], dyp[:, cols])
            ddiffs.append(_dot_nt(dyp[:, cols], w_ref[g]))
        e = jnp.concatenate([ddiffs[g] * inv_counts[g] for g in range(ng)], axis=-1)
        sums = _window_sums(jnp.concatenate([e, enext[...]], axis=0), backward=True)
        enext[...] = e[0:POOL_HALO, :]
        dh = jnp.concatenate([sums[g][0:tm, :] - ddiffs[g] for g in range(ng)], axis=-1)
        dg0_ref[...] += _colsum(dh * xh)
        dx_ref[...] = dm + _rms_bwd(xh, rx, g0, dh)

    row = pl.BlockSpec((tm, d), lambda i: (nt - 1 - i, 0))
    halo = pl.BlockSpec((POOL_HALO, d), lambda i: (jnp.maximum((nt - 1 - i) * (tm // POOL_HALO) - 1, 0), 0))
    vec = _full((1, d))
    return pl.pallas_call(
        body, grid=(nt,),
        in_specs=[row, row, halo, vec, vec, _full(pool_w.shape), vec, ANY],
        out_specs=[row, vec, vec, vec, _full((ng, POOL_DIM, POOL_DIM))],
        out_shape=[jax.ShapeDtypeStruct((t, d), F32)] + [jax.ShapeDtypeStruct((1, d), F32)] * 3
        + [jax.ShapeDtypeStruct((ng, POOL_DIM, POOL_DIM), F32)],
        scratch_shapes=[pltpu.VMEM((POOL_HALO, d), F32)],
        compiler_params=_params(1), name="pool_bwd")(dx1, x, x, g_pre, g_post, pool_w, pool_scale, after)


def _conv_taps(cw_ref, j):
    return cw_ref[j, 0:1, :], cw_ref[j, 1:2, :], cw_ref[j, 2:3, :]


def _row_block(m, target=256):
    if m <= target:
        return m
    for b in range(target, 7, -8):
        if m % b == 0:
            return b
    return m


def mlp_fwd(x, g_pre, g_post, w_up, w_down, conv_w, conv_b, target=None, tm=256):
    t, d = x.shape
    nt = t // tm
    h8 = CONV_HALO
    with_loss = target is not None
    n_extra = 1 if with_loss else 0

    def body(x_ref, g2_ref, g3_ref, wup_hbm, wdn_hbm, cw_ref, cb_ref, *rest):
        tgt_ref = rest[0] if with_loss else None
        xo_ref, u_ref, s_ref, a_ref, f_ref, h_ref = rest[n_extra:n_extra + 6]
        loss_ref = rest[n_extra + 6] if with_loss else None
        wup_v, wdn_v, tail, sem = rest[-4:]
        i = pl.program_id(0)

        @pl.when(i == 0)
        def _():
            c1 = pltpu.make_async_copy(wup_hbm, wup_v, sem.at[0])
            c2 = pltpu.make_async_copy(wdn_hbm, wdn_v, sem.at[1])
            c1.start()
            c2.start()
            tail[...] = jnp.zeros_like(tail)
            if with_loss:
                loss_ref[...] = jnp.zeros_like(loss_ref)
            c1.wait()
            c2.wait()

        xv = x_ref[...]
        h, _, _ = _rms_fwd(xv, g2_ref[...])
        hb = h.astype(BF16)
        h_ref[...] = hb
        acc = jnp.zeros((tm, d), F32)
        for k in range(2):
            cs = []
            for s in range(2):
                j, cols = k + 2 * s, slice((2 * k + s) * FF_CHUNK, (2 * k + s + 1) * FF_CHUNK)
                uf = _dot(hb, wup_v[j])
                u_ref[:, cols] = uf.astype(BF16)
                ext = jnp.concatenate([tail[j], uf], axis=0)
                tail[j] = uf[tm - h8:tm, :]
                w0, w1, w2 = _conv_taps(cw_ref, j)
                cs.append(cb_ref[j] + w2 * uf + w1 * pltpu.roll(ext, 1, axis=0)[h8:, :]
                          + w0 * pltpu.roll(ext, 2, axis=0)[h8:, :])
            cg, cv = cs
            sg = jax.nn.sigmoid(cg)
            sil = cg * sg
            ab = (sil * cv).astype(BF16)
            a_ref[:, k * FF_CHUNK:(k + 1) * FF_CHUNK] = ab
            s_ref[:, 2 * k * FF_CHUNK:(2 * k + 1) * FF_CHUNK] = sil.astype(BF16)
            s_ref[:, (2 * k + 1) * FF_CHUNK:(2 * k + 2) * FF_CHUNK] = (cv * (sg + sil * (1.0 - sg))).astype(BF16)
            acc = acc + _dot(ab, wdn_v[k * FF_CHUNK:(k + 1) * FF_CHUNK, :])
        f_ref[...] = acc
        y, _, _ = _rms_fwd(acc, g3_ref[...])
        if with_loss:
            err = (xv + y) - tgt_ref[...]
            xo_ref[...] = err * (1.0 / d)
            loss_ref[...] += 0.5 * jnp.sum(jnp.mean(err * err, axis=-1, keepdims=True), axis=0, keepdims=True)
        else:
            xo_ref[...] = xv + y

    row = pl.BlockSpec((tm, d), lambda i: (i, 0))
    wide = pl.BlockSpec((tm, 2 * D_FF), lambda i: (i, 0))
    vec = _full((1, d))
    extra = [target] if with_loss else []
    return pl.pallas_call(
        body, grid=(nt,),
        in_specs=[row, vec, vec, ANY, ANY, _full(conv_w.shape), _full(conv_b.shape)] + [row] * n_extra,
        out_specs=[row, wide, wide, pl.BlockSpec((tm, D_FF), lambda i: (i, 0)), row, row] + [_full((1, 1))] * n_extra,
        out_shape=[jax.ShapeDtypeStruct((t, d), F32), jax.ShapeDtypeStruct((t, 2 * D_FF), BF16),
                   jax.ShapeDtypeStruct((t, 2 * D_FF), BF16), jax.ShapeDtypeStruct((t, D_FF), BF16),
                   jax.ShapeDtypeStruct((t, d), F32), jax.ShapeDtypeStruct((t, d), BF16)]
        + [jax.ShapeDtypeStruct((1, 1), F32)] * n_extra,
        scratch_shapes=[pltpu.VMEM(w_up.shape, BF16), pltpu.VMEM(w_down.shape, BF16),
                        pltpu.VMEM((N_SHARD, h8, FF_CHUNK), F32), pltpu.SemaphoreType.DMA((2,))],
        compiler_params=_params(1), name="mlp_fwd_loss" if with_loss else "mlp_fwd")(
            x, g_pre, g_post, w_up, w_down, conv_w, conv_b, *extra)


def _rowsum8(v):
    return jnp.sum(v.reshape(v.shape[0] // 8, 8, v.shape[1]), axis=0)


def mlp_bwd(dxo, f, x, u, sp, g_pre, g_post, w_up, w_down, conv_w, tm=256):
    t, d = x.shape
    nt = t // tm
    h8 = CONV_HALO

    def body(dxo_ref, f_ref, x_ref, u_ref, s_ref, g2_ref, g3_ref, wup_hbm, wdn_hbm, cw_ref,
             dx_ref, du_ref, df_ref, dg2_ref, dg3_ref, dcw_ref, dcb_ref,
             wup_v, wdn_v, carry, sem):
        @pl.when(pl.program_id(0) == 0)
        def _():
            c1 = pltpu.make_async_copy(wup_hbm, wup_v, sem.at[0])
            c2 = pltpu.make_async_copy(wdn_hbm, wdn_v, sem.at[1])
            c1.start()
            c2.start()
            carry[...] = jnp.zeros_like(carry)
            dg2_ref[...] = jnp.zeros_like(dg2_ref)
            dg3_ref[...] = jnp.zeros_like(dg3_ref)
            dcw_ref[...] = jnp.zeros_like(dcw_ref)
            dcb_ref[...] = jnp.zeros_like(dcb_ref)
            c1.wait()
            c2.wait()

        g3 = g3_ref[...]
        dxo = dxo_ref[...]
        _, fh, rf = _rms_fwd(f_ref[...], g3)
        dg3_ref[...] += _rowsum8(dxo * fh)
        dfb = _rms_bwd(fh, rf, g3, dxo).astype(BF16)
        df_ref[...] = dfb
        dh = jnp.zeros((tm, d), F32)
        for k in range(2):
            da = _dot_nt(dfb, wdn_v[k * FF_CHUNK:(k + 1) * FF_CHUNK, :])
            for s in range(2):
                j = k + 2 * s
                cols = slice((2 * k + s) * FF_CHUNK, (2 * k + s + 1) * FF_CHUNK)
                dc = da * s_ref[:, (2 * k + 1 - s) * FF_CHUNK:(2 * k + 2 - s) * FF_CHUNK].astype(F32)
                uf = u_ref[:, cols].astype(F32)
                ext = jnp.concatenate([dc, carry[j]], axis=0)
                carry[j] = dc[0:h8, :]
                dc1 = pltpu.roll(ext, tm + h8 - 1, axis=0)[0:tm, :]
                dc2 = pltpu.roll(ext, tm + h8 - 2, axis=0)[0:tm, :]
                dcb_ref[j] += _rowsum8(dc)
                dcw_ref[j, 2] += _rowsum8(dc * uf)
                dcw_ref[j, 1] += _rowsum8(dc1 * uf)
                dcw_ref[j, 0] += _rowsum8(dc2 * uf)
                dub = (cw_ref[j, 2:3, :] * dc + cw_ref[j, 1:2, :] * dc1 + cw_ref[j, 0:1, :] * dc2).astype(BF16)
                du_ref[:, cols] = dub
                dh = dh + _dot_nt(dub, wup_v[j])
        g2 = g2_ref[...]
        _, xh, rx = _rms_fwd(x_ref[...], g2)
        dg2_ref[...] += _rowsum8(dh * xh)
        dx_ref[...] = dxo + _rms_bwd(xh, rx, g2, dh)

    row = pl.BlockSpec((tm, d), lambda i: (nt - 1 - i, 0))
    wide = pl.BlockSpec((tm, 2 * D_FF), lambda i: (nt - 1 - i, 0))
    vec = _full((1, d))
    acc = _full((8, d))
    dcw_shape, dcb_shape = (N_SHARD, 3, 8, FF_CHUNK), (N_SHARD, 8, FF_CHUNK)
    return pl.pallas_call(
        body, grid=(nt,),
        in_specs=[row, row, row, wide, wide, vec, vec, ANY, ANY, _full(conv_w.shape)],
        out_specs=[row, wide, row, acc, acc, _full(dcw_shape), _full(dcb_shape)],
        out_shape=[jax.ShapeDtypeStruct((t, d), F32), jax.ShapeDtypeStruct((t, 2 * D_FF), BF16),
                   jax.ShapeDtypeStruct((t, d), BF16),
                   jax.ShapeDtypeStruct((8, d), F32), jax.ShapeDtypeStruct((8, d), F32),
                   jax.ShapeDtypeStruct(dcw_shape, F32), jax.ShapeDtypeStruct(dcb_shape, F32)],
        scratch_shapes=[pltpu.VMEM(w_up.shape, BF16), pltpu.VMEM(w_down.shape, BF16),
                        pltpu.VMEM((N_SHARD, h8, FF_CHUNK), F32), pltpu.SemaphoreType.DMA((2,))],
        compiler_params=_params(1, VMEM_LIMIT_MLP_BWD), name="mlp_bwd")(
            dxo, f, x, u, sp, g_pre, g_post, w_up, w_down, conv_w)


def grad_matmul(a, b, bm, bn, name, tk=2048, interleaved=False, after=None):
    t, m = a.shape
    n = b.shape[1]
    tk = min(tk, t)
    nk = t // tk
    place = (lambda j: (j % 2) * 2 + j // 2) if interleaved else (lambda j: j)
    extra = [] if after is None else [after]

    def body(a_ref, b_ref, *rest):
        o_ref, ob_ref = rest[len(extra):]
        kk = pl.program_id(2)

        @pl.when(kk == 0)
        def _():
            o_ref[...] = jnp.zeros_like(o_ref)

        o_ref[...] += _dot_tn(a_ref[...], b_ref[...])

        @pl.when(kk == nk - 1)
        def _():
            ob_ref[...] = o_ref[...].astype(BF16)

    ospec = pl.BlockSpec((None, bm, bn), lambda j, i, kk: (place(j), i, 0))
    return pl.pallas_call(
        body, grid=(n // bn, m // bm, nk),
        in_specs=[pl.BlockSpec((tk, bm), lambda j, i, kk: (kk, i)),
                  pl.BlockSpec((tk, bn), lambda j, i, kk: (kk, j))]
        + [ANY] * len(extra),
        out_specs=[ospec, ospec],
        out_shape=[jax.ShapeDtypeStruct((n // bn, m, bn), F32), jax.ShapeDtypeStruct((n // bn, m, bn), BF16)],
        compiler_params=_params(3), name=name)(a, b, *extra)


def _decay_tables():
    log_gamma = jnp.log(1.0 - 2.0 ** (-5.0 - jnp.arange(RET_HEADS, dtype=F32)))
    i = jnp.arange(RET_CHUNK, dtype=F32)
    rel = i[:, None] - i[None, :]
    intra = jnp.where(rel >= 0, jnp.exp(jnp.maximum(rel, 0.0) * log_gamma[:, None, None]), 0.0)
    cross = jnp.exp((i + 1.0) * log_gamma[:, None])[:, :, None]
    inner = jnp.exp((RET_CHUNK - 1.0 - i) * log_gamma[:, None])[:, :, None]
    chunk = [float(np.exp(np.float32(RET_CHUNK) * np.log(np.float32(1.0 - 2.0 ** (-5.0 - h))).astype(np.float32)))
             for h in range(RET_HEADS)]
    return intra, cross, inner, chunk


def ret_proj(x, g_pre, w_in, cos, sin, tm=512):
    t, d = x.shape
    nt = t // tm
    per = RET_IN_SHARD // RET_QK

    def body(x_ref, g_ref, win_hbm, c_ref, s_ref, pj_ref, h_ref, win_v, sem):
        @pl.when(pl.program_id(0) == 0)
        def _():
            cp = pltpu.make_async_copy(win_hbm, win_v, sem)
            cp.start()
            cp.wait()

        h, _, _ = _rms_fwd(x_ref[...], g_ref[...])
        hb = h.astype(BF16)
        h_ref[...] = hb
        c = c_ref[...]
        s = s_ref[...]
        for j in range(N_SHARD):
            pjj = _dot(hb, win_v[j])
            for bb in range(per):
                b = per * j + bb
                blk = pjj[:, bb * RET_QK:(bb + 1) * RET_QK]
                if b < 2 * RET_HEADS:
                    x1, x2 = blk[:, :128], blk[:, 128:]
                    o1 = x1 * c - x2 * s
                    o2 = x2 * c + x1 * s
                    if b < RET_HEADS:
                        o1 = o1 * (RET_QK ** -0.5)
                        o2 = o2 * (RET_QK ** -0.5)
                    pj_ref[:, b * RET_QK:b * RET_QK + 128] = o1.astype(BF16)
                    pj_ref[:, b * RET_QK + 128:(b + 1) * RET_QK] = o2.astype(BF16)
                else:
                    pj_ref[:, b * RET_QK:(b + 1) * RET_QK] = blk.astype(BF16)

    row = pl.BlockSpec((tm, d), lambda i: (i, 0))
    tab = pl.BlockSpec((tm, 128), lambda i: (i, 0))
    return pl.pallas_call(
        body, grid=(nt,),
        in_specs=[row, _full((1, d)), ANY, tab, tab],
        out_specs=[pl.BlockSpec((tm, RET_IN), lambda i: (i, 0)), row],
        out_shape=[jax.ShapeDtypeStruct((t, RET_IN), BF16), jax.ShapeDtypeStruct((t, d), BF16)],
        scratch_shapes=[pltpu.VMEM(w_in.shape, BF16), pltpu.SemaphoreType.DMA],
        compiler_params=_params(1), name="ret_proj")(x, g_pre, w_in, cos, sin)


def ret_core_fwd(pj, intra, cross, inner, chunk_decay):
    t = pj.shape[0]
    nc = t // RET_CHUNK
    c = RET_CHUNK
    per = RET_STEP_CHUNKS
    qk_all = RET_HEADS * RET_QK
    v_all = RET_HEADS * RET_V

    def body(q_ref, k_ref, v_ref, dm_ref, cr_ref, in_ref, o_ref, sp_ref, state):
        @pl.when(pl.program_id(0) == 0)
        def _():
            state[...] = jnp.zeros_like(state)

        for h in range(RET_HEADS):
            for cc in range(per):
                rows = slice(cc * c, (cc + 1) * c)
                q = q_ref[rows, h * RET_QK:(h + 1) * RET_QK]
                k = k_ref[rows, h * RET_QK:(h + 1) * RET_QK]
                v = v_ref[rows, h * RET_V:(h + 1) * RET_V]
                sb = state[h].astype(BF16)
                sp_ref[cc, h] = sb
                sc = _dot_nt(q, k) * dm_ref[h]
                o_ref[rows, h * RET_V:(h + 1) * RET_V] = (_dot(sc.astype(BF16), v)
                                                          + _dot(q, sb) * cr_ref[h]).astype(BF16)
                kd = (k.astype(F32) * in_ref[h]).astype(BF16)
                state[h] = state[h] * chunk_decay[h] + _dot_tn(kd, v)

    return pl.pallas_call(
        body, grid=(nc // per,),
        in_specs=[pl.BlockSpec((per * c, qk_all), lambda n: (n, 0)), pl.BlockSpec((per * c, qk_all), lambda n: (n, 1)),
                  pl.BlockSpec((per * c, v_all), lambda n: (n, 1)),
                  _full(intra.shape), _full(cross.shape), _full(inner.shape)],
        out_specs=[pl.BlockSpec((per * c, v_all), lambda n: (n, 0)),
                   pl.BlockSpec((per, RET_HEADS, RET_QK, RET_V), lambda n: (n, 0, 0, 0))],
        out_shape=[jax.ShapeDtypeStruct((t, v_all), BF16),
                   jax.ShapeDtypeStruct((nc, RET_HEADS, RET_QK, RET_V), BF16)],
        scratch_shapes=[pltpu.VMEM((RET_HEADS, RET_QK, RET_V), F32)],
        compiler_params=_params(1), name="ret_core_fwd")(pj, pj, pj, intra, cross, inner)


def _group_norm(o_h):
    mu = jnp.mean(o_h, axis=-1, keepdims=True)
    dev = o_h - mu
    rstd = lax.rsqrt(jnp.mean(dev * dev, axis=-1, keepdims=True) + EPS)
    return dev * rstd, rstd


def ret_out_fwd(o, pj, x, gn_gain, g_post, w_out, tm=512):
    t, d = x.shape
    nt = t // tm
    v_all = RET_HEADS * RET_V

    def body(o_ref, g_ref, x_ref, gn_ref, g1_ref, w_ref, xo_ref, y_ref, out_ref):
        out = jnp.zeros((tm, d), F32)
        for h in range(RET_HEADS):
            cols = slice(h * RET_V, (h + 1) * RET_V)
            ohat, _ = _group_norm(o_ref[:, cols].astype(F32))
            g = g_ref[:, cols].astype(F32)
            yb = (g * jax.nn.sigmoid(g) * (ohat * gn_ref[:, cols])).astype(BF16)
            y_ref[:, cols] = yb
            out = out + _dot(yb, w_ref[cols, :])
        out_ref[...] = out
        m, _, _ = _rms_fwd(out, g1_ref[...])
        xo_ref[...] = x_ref[...] + m

    row = pl.BlockSpec((tm, d), lambda i: (i, 0))
    wide = pl.BlockSpec((tm, v_all), lambda i: (i, 0))
    return pl.pallas_call(
        body, grid=(nt,),
        in_specs=[wide, pl.BlockSpec((tm, v_all), lambda i: (i, 2)), row, _full((1, v_all)), _full((1, d)),
                  _full(w_out.shape)],
        out_specs=[row, wide, row],
        out_shape=[jax.ShapeDtypeStruct((t, d), F32), jax.ShapeDtypeStruct((t, v_all), BF16),
                   jax.ShapeDtypeStruct((t, d), F32)],
        compiler_params=_params(1), name="ret_out_fwd")(o, pj, x, gn_gain, g_post, w_out)


def ret_out_bwd(dxo, out, o, pj, gn_gain, g_post, w_out, after, tm=512):
    t, d = out.shape
    nt = t // tm
    v_all = RET_HEADS * RET_V

    def body(dxo_ref, out_ref, o_ref, g_ref, gn_ref, g1_ref, w_ref, after_ref,
             dout_ref, dgate_ref, do_ref, dg1_ref, dgn_ref):
        @pl.when(pl.program_id(0) == 0)
        def _():
            dg1_ref[...] = jnp.zeros_like(dg1_ref)
            dgn_ref[...] = jnp.zeros_like(dgn_ref)

        g1 = g1_ref[...]
        dxo = dxo_ref[...]
        _, oh_, r_ = _rms_fwd(out_ref[...], g1)
        dg1_ref[...] += _colsum(dxo * oh_)
        doutb = _rms_bwd(oh_, r_, g1, dxo).astype(BF16)
        dout_ref[...] = doutb
        for h in range(RET_HEADS):
            cols = slice(h * RET_V, (h + 1) * RET_V)
            gn = gn_ref[:, cols]
            ohat, rstd = _group_norm(o_ref[:, cols].astype(F32))
            g = g_ref[:, cols].astype(F32)
            sg = jax.nn.sigmoid(g)
            dyh = _dot_nt(doutb, w_ref[cols, :])
            sil = g * sg
            dgate_ref[:, cols] = (dyh * (ohat * gn) * (sg + sil * (1.0 - sg))).astype(BF16)
            don = dyh * sil
            dgn_ref[:, cols] += _colsum(don * ohat)
            dohat = don * gn
            do_ref[:, cols] = (rstd * (dohat - jnp.mean(dohat, axis=-1, keepdims=True)
                                       - ohat * jnp.mean(dohat * ohat, axis=-1, keepdims=True))).astype(BF16)

    row = pl.BlockSpec((tm, d), lambda i: (i, 0))
    wide = pl.BlockSpec((tm, v_all), lambda i: (i, 0))
    gate = pl.BlockSpec((tm, v_all), lambda i: (i, 2))
    return pl.pallas_call(
        body, grid=(nt,),
        in_specs=[row, row, wide, gate, _full((1, v_all)), _full((1, d)), _full(w_out.shape), ANY],
        out_specs=[row, gate, wide, _full((1, d)), _full((1, v_all))],
        out_shape=[jax.ShapeDtypeStruct((t, d), BF16), jax.ShapeDtypeStruct((t, RET_IN), BF16),
                   jax.ShapeDtypeStruct((t, v_all), BF16), jax.ShapeDtypeStruct((1, d), F32),
                   jax.ShapeDtypeStruct((1, v_all), F32)],
        compiler_params=_params(1), name="ret_out_bwd")(dxo, out, o, pj, gn_gain, g_post, w_out, after)


def ret_core_bwd(pj, do, sprev, cos, sin, dpj, intra, cross, inner, chunk_decay):
    t = pj.shape[0]
    nc = t // RET_CHUNK
    c = RET_CHUNK
    per = RET_STEP_CHUNKS
    qk_all = RET_HEADS * RET_QK
    v_all = RET_HEADS * RET_V
    scale = RET_QK ** -0.5

    def body(q_ref, k_ref, v_ref, do_ref, sp_ref, c_ref, s_ref, dm_ref, cr_ref, in_ref, dpj_in, dpj_ref, dstate):
        @pl.when(pl.program_id(0) == 0)
        def _():
            dstate[...] = jnp.zeros_like(dstate)

        for h in range(RET_HEADS):
            for cc in reversed(range(per)):
                rows = slice(cc * c, (cc + 1) * c)
                cs = c_ref[rows, :]
                sn = s_ref[rows, :]
                q = q_ref[rows, h * RET_QK:(h + 1) * RET_QK]
                k = k_ref[rows, h * RET_QK:(h + 1) * RET_QK]
                v = v_ref[rows, h * RET_V:(h + 1) * RET_V]
                doh = do_ref[rows, h * RET_V:(h + 1) * RET_V]
                dm = dm_ref[h]
                ab = (_dot_nt(q, k) * dm).astype(BF16)
                dab = (_dot_nt(doh, v) * dm).astype(BF16)
                dsb = dstate[h].astype(BF16)
                kd = (k.astype(F32) * in_ref[h]).astype(BF16)
                dv = _dot_tn(ab, doh) + _dot(kd, dsb)
                dq = _dot(dab, k) + cr_ref[h] * _dot_nt(doh, sp_ref[cc, h])
                dk = _dot_tn(dab, q) + in_ref[h] * _dot_nt(v, dsb)
                qd = (q.astype(F32) * cr_ref[h]).astype(BF16)
                dstate[h] = dstate[h] * chunk_decay[h] + _dot_tn(qd, doh)
                for base, dd, sc in ((h * RET_QK, dq, scale), (qk_all + h * RET_QK, dk, 1.0)):
                    d1, d2 = dd[:, :128], dd[:, 128:]
                    dpj_ref[rows, base:base + 128] = ((d1 * cs + d2 * sn) * sc).astype(BF16)
                    dpj_ref[rows, base + 128:base + RET_QK] = ((d2 * cs - d1 * sn) * sc).astype(BF16)
                dpj_ref[rows, 2 * qk_all + h * RET_V:2 * qk_all + (h + 1) * RET_V] = dv.astype(BF16)

    rev = lambda n: nc // per - 1 - n
    tab = pl.BlockSpec((per * c, 128), lambda n: (rev(n), 0))
    return pl.pallas_call(
        body, grid=(nc // per,),
        in_specs=[pl.BlockSpec((per * c, qk_all), lambda n: (rev(n), 0)),
                  pl.BlockSpec((per * c, qk_all), lambda n: (rev(n), 1)),
                  pl.BlockSpec((per * c, v_all), lambda n: (rev(n), 1)),
                  pl.BlockSpec((per * c, v_all), lambda n: (rev(n), 0)),
                  pl.BlockSpec((per, RET_HEADS, RET_QK, RET_V), lambda n: (rev(n), 0, 0, 0)),
                  tab, tab, _full(intra.shape), _full(cross.shape), _full(inner.shape), ANY],
        out_specs=pl.BlockSpec((per * c, 2 * qk_all + v_all), lambda n: (rev(n), 0)),
        out_shape=jax.ShapeDtypeStruct((t, RET_IN), BF16),
        scratch_shapes=[pltpu.VMEM((RET_HEADS, RET_QK, RET_V), F32)],
        input_output_aliases={10: 0},
        compiler_params=_params(1), name="ret_core_bwd")(pj, pj, pj, do, sprev, cos, sin, intra, cross, inner, dpj)


def ret_in_bwd(dpj, dres, x, g_pre, w_in, after, tm=512):
    t, d = x.shape
    nt = t // tm

    def body(dpj_ref, dres_ref, x_ref, g_ref, win_hbm, after_ref, dx_ref, dg_ref, win_v, sem):
        @pl.when(pl.program_id(0) == 0)
        def _():
            cp = pltpu.make_async_copy(win_hbm, win_v, sem)
            cp.start()
            dg_ref[...] = jnp.zeros_like(dg_ref)
            cp.wait()

        g = g_ref[...]
        dh = jnp.zeros((tm, d), F32)
        for j in range(N_SHARD):
            dh = dh + _dot_nt(dpj_ref[:, j * RET_IN_SHARD:(j + 1) * RET_IN_SHARD], win_v[j])
        _, xh, rx = _rms_fwd(x_ref[...], g)
        dg_ref[...] += _colsum(dh * xh)
        dx_ref[...] = dres_ref[...] + _rms_bwd(xh, rx, g, dh)

    row = pl.BlockSpec((tm, d), lambda i: (i, 0))
    return pl.pallas_call(
        body, grid=(nt,),
        in_specs=[pl.BlockSpec((tm, RET_IN), lambda i: (i, 0)), row, row, _full((1, d)), ANY, ANY],
        out_specs=[row, _full((1, d))],
        out_shape=[jax.ShapeDtypeStruct((t, d), F32), jax.ShapeDtypeStruct((1, d), F32)],
        scratch_shapes=[pltpu.VMEM(w_in.shape, BF16), pltpu.SemaphoreType.DMA],
        compiler_params=_params(1), name="ret_in_bwd")(dpj, dres, x, g_pre, w_in, after)


_CHIP_FLIPS = ((1, 0), (0, 1), (1, 1))


def _flip(v, b):
    return 1 - v if b else v


_HBM = pl.BlockSpec(memory_space=pltpu.HBM)
_SEM = pl.BlockSpec(memory_space=pltpu.SEMAPHORE)
_EFFECT = pltpu.SideEffectType.DATAFLOW_SIDE_EFFECTING


def _chip_copies(mode, srcs, lands, send_sems, recv_sems):
    x, y, c = lax.axis_index("x"), lax.axis_index("y"), lax.axis_index("c")
    copies = []
    for t in range(len(lands)):
        if mode == "swap":
            copies.append(pltpu.make_async_remote_copy(
                src_ref=srcs[t], dst_ref=lands[t], send_sem=send_sems.at[t], recv_sem=recv_sems.at[t],
                device_id=(x, y, 1 - c), device_id_type=MESH))
            continue
        if mode == "everyone":
            for m in range(1, 8):
                bx, by, bc = (m >> 2) & 1, (m >> 1) & 1, m & 1
                copies.append(pltpu.make_async_remote_copy(
                    src_ref=srcs[t], dst_ref=lands[t].at[4 * x + 2 * y + c], send_sem=send_sems.at[7 * t + m - 1],
                    recv_sem=recv_sems.at[7 * t + m - 1], device_id=(_flip(x, bx), _flip(y, by), _flip(c, bc)),
                    device_id_type=MESH))
            continue
        for k, (bx, by) in enumerate(_CHIP_FLIPS):
            px, py = _flip(x, bx), _flip(y, by)
            target = (px, py, c)
            if mode == "gather":
                src, dst = srcs[t], lands[t].at[2 * x + y]
            elif mode == "gather_half":
                half = pl.ds(c * (srcs[t].shape[0] // 2), srcs[t].shape[0] // 2)
                src, dst = srcs[t].at[half], lands[t].at[2 * x + y, half]
            elif mode == "forward_half":
                half = pl.ds(c * (lands[t].shape[1] // 2), lands[t].shape[1] // 2)
                src = dst = lands[t].at[2 * px + py, half]
                target = (x, y, 1 - c)
            else:
                src, dst = srcs[t].at[2 * px + py], lands[t].at[k]
            copies.append(pltpu.make_async_remote_copy(
                src_ref=src, dst_ref=dst, send_sem=send_sems.at[3 * t + k], recv_sem=recv_sems.at[3 * t + k],
                device_id=target, device_id_type=MESH))
    return copies


def exchange_start(mode, srcs, lands, name, after=None):
    n, ns = len(lands), len(srcs)
    extra = [] if after is None else [after]

    def body(*refs):
        ins, lnd = refs[:ns], refs[ns:ns + n]
        send_sems, recv_sems = refs[ns + n + len(extra)], refs[ns + n + len(extra) + 1]
        token = refs[-1]
        for cp in _chip_copies(mode, ins, lnd, send_sems, recv_sems):
            cp.start()
        token[...] = jnp.zeros(token.shape, token.dtype)

    hbm = lambda a: pltpu.with_memory_space_constraint(a, pltpu.HBM)
    passed = list(srcs) + list(lands)
    n_sem = {"swap": 1, "everyone": 7}.get(mode, 3) * n
    return pl.pallas_call(
        body, name=name,
        out_shape=(pltpu.SemaphoreType.DMA((n_sem,)), pltpu.SemaphoreType.DMA((n_sem,)),
                   *[pltpu.HBM(a.shape, a.dtype) for a in passed], jax.ShapeDtypeStruct((8, 128), F32)),
        in_specs=[_HBM] * (ns + n) + [ANY] * len(extra),
        out_specs=(_SEM, _SEM, *[_HBM] * (ns + n), pl.BlockSpec(memory_space=pltpu.VMEM)),
        input_output_aliases={i: 2 + i for i in range(ns + n)},
        compiler_params=pltpu.CompilerParams(has_side_effects=_EFFECT))(*[hbm(a) for a in passed], *extra)


def exchange_wait(mode, started, after, name):
    send_sems, recv_sems = started[0], started[1]
    passed = list(started[2:-1])
    n = len(passed) if mode == "forward_half" else len(passed) // 2
    ns = len(passed) - n
    after = list(after) if isinstance(after, (list, tuple)) else [after]

    def body(*refs):
        ins, lnd = refs[:ns], refs[ns:ns + n]
        for cp in _chip_copies(mode, ins, lnd, refs[ns + n], refs[ns + n + 1]):
            cp.wait_send()
            cp.wait_recv()

    outs = pl.pallas_call(
        body, name=name, out_shape=tuple(pltpu.HBM(a.shape, a.dtype) for a in passed),
        in_specs=[_HBM] * (ns + n) + [_SEM, _SEM] + [ANY] * len(after), out_specs=tuple([_HBM] * (ns + n)),
        input_output_aliases={i: i for i in range(ns + n)},
        compiler_params=pltpu.CompilerParams(has_side_effects=_EFFECT))(*passed, send_sems, recv_sems, *after)
    return list(outs[:ns]), list(outs[ns:])


def plane_sum(slot, full, recv, name, bm=256):
    _, m, n = full.shape
    bm = _row_block(m, bm)

    def body(slot_ref, o_ref, r_ref, s_ref):
        s_ref[...] = ((o_ref[...] + r_ref[0].astype(F32)) + r_ref[1].astype(F32)) + r_ref[2].astype(F32)

    return pl.pallas_call(
        body,
        grid_spec=pltpu.PrefetchScalarGridSpec(
            num_scalar_prefetch=1, grid=(m // bm,),
            in_specs=[pl.BlockSpec((None, bm, n), lambda i, s: (s[0], i, 0)),
                      pl.BlockSpec((3, bm, n), lambda i, s: (0, i, 0))],
            out_specs=pl.BlockSpec((bm, n), lambda i, s: (i, 0))),
        out_shape=jax.ShapeDtypeStruct((m, n), F32), compiler_params=_params(1), name=name)(slot, full, recv)


def sum_slots(parts, name, bm=312):
    _, r, n = parts.shape
    bm = bm if r % bm == 0 else r

    def body(p_ref, s_ref):
        acc = p_ref[0]
        for k in range(1, 8):
            acc = acc + p_ref[k]
        s_ref[...] = acc

    return pl.pallas_call(
        body, grid=(r // bm,), in_specs=[pl.BlockSpec((8, bm, n), lambda i: (0, i, 0))],
        out_specs=pl.BlockSpec((bm, n), lambda i: (i, 0)), out_shape=jax.ShapeDtypeStruct((r, n), F32),
        compiler_params=_params(1), name=name)(parts)


def _adamw_math(w, g, m, v):
    m = ADAM_B1 * m + (1.0 - ADAM_B1) * g
    v = ADAM_B2 * v + (1.0 - ADAM_B2) * (g * g)
    m_hat = m / (1.0 - ADAM_B1 ** ADAM_STEP)
    v_hat = v / (1.0 - ADAM_B2 ** ADAM_STEP)
    delta = -ADAM_LR * (m_hat / (jnp.sqrt(v_hat) + ADAM_EPS) + ADAM_WD * w)
    return delta, m, v


def adamw(w, m, v, grads, layer, prev, name, bm=256):
    _, _, n = w.shape
    mm = grads[0].shape[0]
    bm = _row_block(mm, bm)
    ng = len(grads)

    def body(*refs):
        w_ref, m_ref, v_ref = refs[:3]
        g_refs = refs[3:3 + ng]
        g_out, d_out, m_out, v_out = refs[-4:]
        g = g_refs[0][...]
        for gr in g_refs[1:]:
            g = g + gr[...]
        delta, mn, vn = _adamw_math(w_ref[...], g, m_ref[...], v_ref[...])
        g_out[...] = g
        d_out[...] = delta
        m_out[...] = mn
        v_out[...] = vn

    slab = pl.BlockSpec((None, bm, n), lambda i: (layer, i, 0))
    flat = pl.BlockSpec((bm, n), lambda i: (i, 0))
    in_specs = [slab] * 3 + [flat] * ng
    args = [w, m, v, *grads]
    aliases = {}
    if prev is not None:
        in_specs += [ANY] * 4
        aliases = {3 + ng + q: q for q in range(4)}
        args += list(prev)
    return pl.pallas_call(
        body, grid=(mm // bm,), in_specs=in_specs, out_specs=[slab] * 4,
        out_shape=[jax.ShapeDtypeStruct(w.shape, F32)] * 4, input_output_aliases=aliases,
        compiler_params=_params(1), name=name)(*args)


def _pack_rows(parts, rows):
    flat = jnp.concatenate([p.reshape(-1) for p in parts])
    return jnp.pad(flat, (0, rows * 128 - flat.shape[0])).reshape(rows, 128)


def _as_shards(a, rows):
    return a.reshape(N_SHARD, rows, a.shape[-1])


def _local_step(x, pos_col, target, gains, pool_w, pool_scale, gn_gain, conv_w, conv_b, weights, send_grads):
    def gain(l, n):
        return gains[l, n].reshape(1, D_MODEL)

    inv_freq = (ROPE_BASE ** (-jnp.arange(0, RET_QK, 2, dtype=F32) / RET_QK)).reshape(1, RET_QK // 2)
    intra, cross, inner, chunk_decay = _decay_tables()
    dn_rows = D_FF // N_SHARD

    x1 = pool_fwd(x, gain(0, 0), gain(0, 1), pool_w, pool_scale)
    cos, sin = rope_tables(pos_col, inv_freq, x1)
    w_up0, w_dn0 = weights("mlp0", cos)
    w_dn0 = w_dn0.reshape(D_FF, D_MODEL)
    x2, u0, s0, a0, f0, h0 = mlp_fwd(x1, gain(0, 2), gain(0, 3), w_up0, w_dn0, conv_w[0], conv_b[0])
    w_in, w_out = weights("ret", x2)
    w_out = w_out.reshape(RET_HEADS * RET_V, D_MODEL)
    pj, hr = ret_proj(x2, gain(1, 0), w_in, cos, sin)
    o, sprev = ret_core_fwd(pj, intra, cross, inner, chunk_decay)
    x3, yb, out = ret_out_fwd(o, pj, x2, gn_gain, gain(1, 1), w_out)
    w_up1, w_dn1 = weights("mlp1", x3)
    w_dn1 = w_dn1.reshape(D_FF, D_MODEL)
    dx4, u1, s1, a1, f1, h1, loss = mlp_fwd(x3, gain(1, 2), gain(1, 3), w_up1, w_dn1, conv_w[1], conv_b[1], target)

    dx3, du1, df1, dg12, dg13, dcw1, dcb1 = mlp_bwd(
        dx4, f1, x3, u1, s1, gain(1, 2), gain(1, 3), w_up1, w_dn1, conv_w[1])
    dwup1 = grad_matmul(h1, du1, D_MODEL, FF_CHUNK, "grad_w_up_1", interleaved=True)
    dwdn1 = grad_matmul(a1, df1, FF_CHUNK, D_MODEL, "grad_w_down_1")
    tok = send_grads("mlp1", [dwup1, [_as_shards(g, dn_rows) for g in dwdn1]])
    dout, dpj, do, dg11, dgn = ret_out_bwd(dx3, out, o, pj, gn_gain, gain(1, 1), w_out, tok)
    dwout = grad_matmul(yb, dout, 1024, D_MODEL, "grad_w_out")
    dpj = ret_core_bwd(pj, do, sprev, cos, sin, dpj, intra, cross, inner, chunk_decay)
    dwin = grad_matmul(hr, dpj, D_MODEL, RET_IN_SHARD, "grad_w_in")
    tok = send_grads("ret", [dwin, [_as_shards(g, RET_V) for g in dwout]])
    dx2, dg10 = ret_in_bwd(dpj, dx3, x2, gain(1, 0), w_in, tok)
    dx1, du0, df0, dg02, dg03, dcw0, dcb0 = mlp_bwd(
        dx2, f0, x1, u0, s0, gain(0, 2), gain(0, 3), w_up0, w_dn0, conv_w[0])
    dwdn0 = grad_matmul(a0, df0, FF_CHUNK, D_MODEL, "grad_w_down_0")
    tok = send_grads("down0", [[_as_shards(g, dn_rows) for g in dwdn0]])
    dwup0 = grad_matmul(h0, du0, D_MODEL, FF_CHUNK, "grad_w_up_0", interleaved=True, after=tok)
    tok = send_grads("up0", [dwup0])
    dx0, dg00, dg01, dpscale, dpw = pool_bwd(dx1, x, gain(0, 0), gain(0, 1), pool_w, pool_scale, tok)

    rows = lambda g: jnp.sum(g, axis=0, keepdims=True)
    dgains = jnp.concatenate([dg00, dg01, rows(dg02), rows(dg03), dg10, dg11, rows(dg12), rows(dg13)],
                             axis=0).reshape(2, 4, D_MODEL)
    small = {"gains": dgains, "pool_scale": dpscale, "gn": dgn,
             "conv_w": jnp.sum(jnp.stack([dcw0, dcw1]), axis=3),
             "conv_b": jnp.sum(jnp.stack([dcb0, dcb1]), axis=2, keepdims=True), "pool_w": dpw}
    return loss, dx0, small


def kernel(x, positions, norm_gain, pool_w, pool_scale, ret_w_in, ret_gn_gain, ret_w_out, mlp_w_up, mlp_conv_w, mlp_conv_b, mlp_w_down, loss_target, m_norm_gain, m_pool_w, m_pool_scale, m_ret_w_in, m_ret_gn_gain, m_ret_w_out, m_mlp_w_up, m_mlp_conv_w, m_mlp_conv_b, m_mlp_w_down, v_norm_gain, v_pool_w, v_pool_scale, v_ret_w_in, v_ret_gn_gain, v_ret_w_out, v_mlp_w_up, v_mlp_conv_w, v_mlp_conv_b, v_mlp_w_down):
    t = x.shape[1]
    me = 2 * lax.axis_index("x") + lax.axis_index("y")
    me_slot = jnp.reshape(me, (1,)).astype(jnp.int32)

    small_parts = [norm_gain, ret_gn_gain, mlp_conv_w, pool_w]
    small_sizes = [p.size for p in small_parts]
    small_rows = -(-sum(small_sizes) // (128 * 8)) * 8
    gathers = {}

    def start_gather(group, srcs, after):
        lands = [lax.dynamic_update_index_in_dim(lax.empty((N_SHARD,) + s.shape, s.dtype), s, me, 0) for s in srcs]
        mode = "gather_half" if group.startswith("mlp0") else "gather"
        gathers[group] = (mode, exchange_start(mode, srcs, lands, "gather_start_" + group, after=after))
        return gathers[group][1][-1]

    token = start_gather("small", [_pack_rows(small_parts, small_rows)], None)
    token = start_gather("mlp0_down", [mlp_w_down[0].astype(BF16)], token)
    token = start_gather("mlp0_up", [(mlp_w_up[0] + token[0, 0]).astype(BF16)], token)

    def weights(group, after):
        if group == "mlp0":
            tok = start_gather("ret", [ret_w_in[0].astype(BF16), ret_w_out[0].astype(BF16)], after)
            after = start_gather("mlp1", [mlp_w_up[1].astype(BF16), mlp_w_down[1].astype(BF16)], tok)
            lands = []
            for part in ("mlp0_up", "mlp0_down"):
                lands += exchange_wait("gather_half", gathers[part][1], after, "gather_wait_" + part)[1]
            forward = exchange_start("forward_half", [], lands, "forward_start_mlp0")
            return exchange_wait("forward_half", forward, forward[-1], "forward_wait_mlp0")[1]
        mode, started = gathers[group]
        return exchange_wait(mode, started, after, "gather_wait_" + group)[1]

    sent, early = {}, {}

    def reduced(group, after, names):
        started, own = sent[group]
        _, recv = exchange_wait("scatter", started, after, "scatter_wait_" + group)
        return [plane_sum(me_slot, f, r, "plane_sum_" + nm)
                for f, r, nm in zip(own, recv, names)]

    def swap_start(planes, name):
        return exchange_start("swap", planes, [lax.empty(p.shape, p.dtype) for p in planes], name)

    def send_grads(group, pairs):
        lands = [lax.empty((3,) + b.shape[1:], BF16) for _, b in pairs]
        sent[group] = (exchange_start("scatter", [b for _, b in pairs], lands, "scatter_start_" + group),
                       [f for f, _ in pairs])
        token = sent[group][0][-1]
        if group == "up0":
            early["planes"] = (reduced("mlp1", token, ["w_up_1", "w_down_1"])
                               + reduced("ret", token, ["w_in", "w_out"]))
            early["swap"] = swap_start(early["planes"], "swap_start_a")
            token = early["swap"][-1]
        return token

    (smallg,) = weights("small", token)
    smallg = smallg.reshape(N_SHARD, -1)
    offs = np.cumsum([0] + small_sizes)
    piece = lambda i, shape: smallg[:, offs[i]:offs[i + 1]].reshape((N_SHARD,) + shape)
    gains = piece(0, (2, 4, 256)).transpose(1, 2, 0, 3).reshape(2, 4, D_MODEL)
    gn_full = piece(1, (512,)).reshape(1, RET_HEADS * RET_V)
    cw_full = piece(2, (2, 3, FF_CHUNK)).transpose(1, 0, 2, 3)
    pw_full = piece(3, (4, 64, 256)).transpose(1, 0, 2, 3).reshape(4, 256, 256).astype(BF16)
    cb_full = mlp_conv_b.reshape(2, N_SHARD, 1, FF_CHUNK)

    loss, dx0, small = _local_step(
        x[0], positions.reshape(t, 1).astype(F32), loss_target[0], gains, pw_full, pool_scale, gn_full,
        cw_full, cb_full, weights, send_grads)

    def small_adamw(w, m, v, grads, name):
        w3 = w.reshape(1, -1, w.shape[-1])
        out = adamw(w3, m.reshape(w3.shape), v.reshape(w3.shape), [g.reshape(w3.shape[1:]) for g in grads], 0, None, name)
        return [o.reshape(w.shape) for o in out]

    pw_f = small["pool_w"].reshape(4, N_SHARD, 64, 256).transpose(1, 0, 2, 3).reshape(N_SHARD, 256, 256)
    small = dict(small, loss=loss)
    small_order = ["gains", "pool_scale", "gn", "conv_w", "conv_b", "loss"]
    gsmall_sizes = [small[k].size for k in small_order]
    gsmall_rows = -(-sum(gsmall_sizes) // (128 * 8)) * 8
    gpack = _pack_rows([small[k] for k in small_order], gsmall_rows)
    mine = 2 * me + lax.axis_index("c")
    small_started = exchange_start(
        "everyone", [gpack], [lax.dynamic_update_index_in_dim(lax.empty((8,) + gpack.shape, F32), gpack, mine, 0)],
        "small_start")
    pool_token = send_grads("pool_w", [(pw_f, pw_f.astype(BF16))])

    res = {}
    planes_a, others_a = exchange_wait("swap", early["swap"], [small_started[-1], pool_token], "swap_wait_a")
    res["ret_w_in"] = adamw(ret_w_in, m_ret_w_in, v_ret_w_in, (planes_a[2], others_a[2]), 0, None, "adamw_w_in")
    res["ret_w_out"] = adamw(ret_w_out, m_ret_w_out, v_ret_w_out, (planes_a[3], others_a[3]), 0, None, "adamw_w_out")
    up1 = adamw(mlp_w_up, m_mlp_w_up, v_mlp_w_up, (planes_a[0], others_a[0]), 1, None, "adamw_w_up_1")
    dn1 = adamw(mlp_w_down, m_mlp_w_down, v_mlp_w_down, (planes_a[1], others_a[1]), 1, None, "adamw_w_down_1")

    done_a = [res["ret_w_in"][0], res["ret_w_out"][0], up1[0], dn1[0]]
    planes_b = (reduced("up0", done_a, ["w_up_0"]) + reduced("down0", done_a, ["w_down_0"])
                + reduced("pool_w", done_a, ["pool_w"]))
    swap_b = swap_start(planes_b, "swap_start_b")

    _, (small_recv,) = exchange_wait("everyone", small_started, swap_b[-1], "small_wait")
    gsmall = sum_slots(small_recv, "sum_small").reshape(-1)
    goffs = np.cumsum([0] + gsmall_sizes)
    gpiece = lambda i: gsmall[goffs[i]:goffs[i + 1]].reshape(small[small_order[i]].shape)
    g_gains = lax.dynamic_slice_in_dim(gpiece(0), me * 256, 256, axis=2)
    g_gn = lax.dynamic_slice_in_dim(gpiece(2), me * RET_V, RET_V, axis=1)
    g_cw = lax.dynamic_index_in_dim(gpiece(3), me, 1, keepdims=False)
    res["norm_gain"] = small_adamw(norm_gain, m_norm_gain, v_norm_gain, [g_gains], "adamw_norm_gain")
    res["pool_scale"] = small_adamw(pool_scale, m_pool_scale, v_pool_scale, [gpiece(1)], "adamw_pool_scale")
    res["ret_gn_gain"] = small_adamw(ret_gn_gain, m_ret_gn_gain, v_ret_gn_gain, [g_gn], "adamw_gn_gain")
    res["mlp_conv_w"] = small_adamw(mlp_conv_w, m_mlp_conv_w, v_mlp_conv_w, [g_cw], "adamw_conv_w")
    res["mlp_conv_b"] = small_adamw(mlp_conv_b, m_mlp_conv_b, v_mlp_conv_b, [gpiece(4)], "adamw_conv_b")

    small_done = [res[k][0] for k in ("norm_gain", "pool_scale", "ret_gn_gain", "mlp_conv_w", "mlp_conv_b")]
    planes_b, others_b = exchange_wait("swap", swap_b, small_done, "swap_wait_b")
    res["mlp_w_up"] = adamw(mlp_w_up, m_mlp_w_up, v_mlp_w_up, (planes_b[0], others_b[0]), 0, up1, "adamw_w_up_0")
    res["mlp_w_down"] = adamw(mlp_w_down, m_mlp_w_down, v_mlp_w_down, (planes_b[1], others_b[1]), 0, dn1,
                              "adamw_w_down_0")
    res["pool_w"] = small_adamw(pool_w, m_pool_w, v_pool_w, (planes_b[2], others_b[2]), "adamw_pool_w")

    order = ["norm_gain", "pool_w", "pool_scale", "ret_w_in", "ret_gn_gain", "ret_w_out", "mlp_w_up", "mlp_conv_w",
             "mlp_conv_b", "mlp_w_down"]
    outs = [gpiece(5)[0, 0], dx0.reshape(x.shape)]
    for q in range(4):
        outs += [res[k][q] for k in order]
    return tuple(outs)
```

```python
import numpy as np
import jax
import jax.numpy as jnp
from jax import lax
from jax.experimental import pallas as pl
from jax.experimental.pallas import tpu as pltpu

F32 = jnp.float32
BF16 = jnp.bfloat16

D_MODEL = 1024
D_FF = 2816
FF_CHUNK = 1408
N_SHARD = 4
POOL_WINDOWS = (2, 4, 8, 16)
POOL_DIM = 256
POOL_HALO = 16
RET_HEADS = 4
RET_QK = 256
RET_V = 512
RET_CHUNK = 256
RET_STEP_CHUNKS = 2
RET_IN = 6144
RET_IN_SHARD = 1536
ROPE_BASE = 10000.0
EPS = 1e-6
CONV_HALO = 8

ADAM_LR, ADAM_B1, ADAM_B2, ADAM_EPS, ADAM_WD, ADAM_STEP = 0.001, 0.9, 0.999, 1e-08, 0.01, 10

VMEM_LIMIT = 56 * 1024 * 1024
VMEM_LIMIT_MLP_BWD = 62 * 1024 * 1024
STREAM_BUFFERS = 3
MESH = pl.DeviceIdType.MESH
ANY = pl.BlockSpec(memory_space=pl.ANY)


def _params(n_grid=1, limit=VMEM_LIMIT):
    return pltpu.CompilerParams(dimension_semantics=("arbitrary",) * n_grid, vmem_limit_bytes=limit)


def _dot(a, b):
    return jnp.dot(a, b, preferred_element_type=F32)


def _dot_nt(a, b):
    return lax.dot_general(a, b, (((1,), (1,)), ((), ())), preferred_element_type=F32)


def _dot_tn(a, b):
    return lax.dot_general(a, b, (((0,), (0,)), ((), ())), preferred_element_type=F32)


def _rms_fwd(x, gain):
    r = lax.rsqrt(jnp.mean(x * x, axis=-1, keepdims=True) + EPS)
    xh = x * r
    return xh * gain, xh, r


def _rms_bwd(xh, r, gain, dy):
    dxh = dy * gain
    return r * (dxh - xh * jnp.mean(dxh * xh, axis=-1, keepdims=True))


def _colsum(v):
    return jnp.sum(v, axis=0, keepdims=True)


def _full(shape):
    nd = len(shape)
    return pl.BlockSpec(shape, lambda *_: (0,) * nd)


def rope_tables(pos_col, inv_freq, after):
    t = pos_col.shape[0]
    tm = min(t, 1024)

    def body(p_ref, f_ref, after_ref, c_ref, s_ref):
        ang = p_ref[...] * f_ref[...]
        c_ref[...] = jnp.cos(ang)
        s_ref[...] = jnp.sin(ang)

    return pl.pallas_call(
        body, grid=(t // tm,),
        in_specs=[pl.BlockSpec((tm, 1), lambda i: (i, 0)), _full((1, 128)), ANY],
        out_specs=[pl.BlockSpec((tm, 128), lambda i: (i, 0))] * 2,
        out_shape=[jax.ShapeDtypeStruct((t, 128), F32)] * 2,
        compiler_params=_params(1), name="rope_tables")(pos_col, inv_freq, after)


def _window_sums(ext, backward):
    n = ext.shape[0]
    cur, sums = ext, []
    for g, win in enumerate(POOL_WINDOWS):
        if g > 0:
            cur = cur[:, POOL_DIM:]
        half = win // 2
        cur = cur + pltpu.roll(cur, n - half if backward else half, axis=0)
        sums.append(cur[:, 0:POOL_DIM])
    return sums


def _pool_diff(h_halo, h, row0, tm):
    t_idx = row0 + lax.broadcasted_iota(jnp.int32, (tm, 1), 0)
    sums = _window_sums(jnp.concatenate([h_halo, h], axis=0), backward=False)
    parts, inv_counts = [], []
    for g, win in enumerate(POOL_WINDOWS):
        inv = 1.0 / jnp.minimum(t_idx + 1, win).astype(F32)
        parts.append(sums[g][POOL_HALO:, :] * inv - h[:, g * POOL_DIM:(g + 1) * POOL_DIM])
        inv_counts.append(inv)
    return parts, inv_counts


def pool_fwd(x, g_pre, g_post, pool_w, pool_scale, tm=512):
    t, d = x.shape
    nt = t // tm

    def body(x_ref, g0_ref, g1_ref, w_ref, sc_ref, o_ref, hext):
        i = pl.program_id(0)

        @pl.when(i == 0)
        def _():
            hext[...] = jnp.zeros((POOL_HALO, d), F32)

        xv = x_ref[...]
        h, _, _ = _rms_fwd(xv, g0_ref[...])
        parts, _ = _pool_diff(hext[...], h, i * tm, tm)
        hext[...] = h[tm - POOL_HALO:tm, :]
        ys = [_dot(parts[g].astype(BF16), w_ref[g]) for g in range(len(POOL_WINDOWS))]
        y = jnp.concatenate(ys, axis=-1) * sc_ref[...]
        m, _, _ = _rms_fwd(y, g1_ref[...])
        o_ref[...] = xv + m

    row = pl.BlockSpec((tm, d), lambda i: (i, 0))
    return pl.pallas_call(
        body, grid=(nt,),
        in_specs=[row, _full((1, d)), _full((1, d)), _full(pool_w.shape), _full((1, d))],
        out_specs=row, out_shape=jax.ShapeDtypeStruct((t, d), F32),
        scratch_shapes=[pltpu.VMEM((POOL_HALO, d), F32)],
        compiler_params=_params(1), name="pool_fwd")(x, g_pre, g_post, pool_w, pool_scale)


def pool_bwd(dx1, x, g_pre, g_post, pool_w, pool_scale, after, tm=512):
    t, d = x.shape
    nt = t // tm
    ng = len(POOL_WINDOWS)

    def body(dx1_ref, x_ref, xh_ref, g0_ref, g1_ref, w_ref, sc_ref, after_ref,
             dx_ref, dg0_ref, dg1_ref, dsc_ref, dw_ref, enext):
        i = pl.program_id(0)
        r = nt - 1 - i

        @pl.when(i == 0)
        def _():
            enext[...] = jnp.zeros((POOL_HALO, d), F32)
            dg0_ref[...] = jnp.zeros_like(dg0_ref)
            dg1_ref[...] = jnp.zeros_like(dg1_ref)
            dsc_ref[...] = jnp.zeros_like(dsc_ref)
            dw_ref[...] = jnp.zeros_like(dw_ref)

        g0 = g0_ref[...]
        g1 = g1_ref[...]
        sc = sc_ref[...]
        xv = x_ref[...]
        h, xh, rx = _rms_fwd(xv, g0)
        h_halo, _, _ = _rms_fwd(xh_ref[...], g0)
        parts, inv_counts = _pool_diff(h_halo * jnp.where(r > 0, 1.0, 0.0), h, r * tm, tm)
        parts_b = [p.astype(BF16) for p in parts]
        ypre = jnp.concatenate([_dot(parts_b[g], w_ref[g]) for g in range(ng)], axis=-1)
        _, yh, ry = _rms_fwd(ypre * sc, g1)
        dm = dx1_ref[...]
        dg1_ref[...] += _colsum(dm * yh)
        dy = _rms_bwd(yh, ry, g1, dm)
        dsc_ref[...] += _colsum(dy * ypre)
        dyp = (dy * sc).astype(BF16)
        ddiffs = []
        for g in range(ng):
            cols = slice(g * POOL_DIM, (g + 1) * POOL_DIM)
            dw_ref[g] += _dot_tn(parts_b[g], dyp[:, cols])
            ddiffs.append(_dot_nt(dyp[:, cols], w_ref[g]))
        e = jnp.concatenate([ddiffs[g] * inv_counts[g] for g in range(ng)], axis=-1)
        sums = _window_sums(jnp.concatenate([e, enext[...]], axis=0), backward=True)
        enext[...] = e[0:POOL_HALO, :]
        dh = jnp.concatenate([sums[g][0:tm, :] - ddiffs[g] for g in range(ng)], axis=-1)
        dg0_ref[...] += _colsum(dh * xh)
        dx_ref[...] = dm + _rms_bwd(xh, rx, g0, dh)

    row = pl.BlockSpec((tm, d), lambda i: (nt - 1 - i, 0))
    halo = pl.BlockSpec((POOL_HALO, d), lambda i: (jnp.maximum((nt - 1 - i) * (tm // POOL_HALO) - 1, 0), 0))
    vec = _full((1, d))
    return pl.pallas_call(
        body, grid=(nt,),
        in_specs=[row, row, halo, vec, vec, _full(pool_w.shape), vec, ANY],
        out_specs=[row, vec, vec, vec, _full((ng, POOL_DIM, POOL_DIM))],
        out_shape=[jax.ShapeDtypeStruct((t, d), F32)] + [jax.ShapeDtypeStruct((1, d), F32)] * 3
        + [jax.ShapeDtypeStruct((ng, POOL_DIM, POOL_DIM), F32)],
        scratch_shapes=[pltpu.VMEM((POOL_HALO, d), F32)],
        compiler_params=_params(1), name="pool_bwd")(dx1, x, x, g_pre, g_post, pool_w, pool_scale, after)


def _conv_taps(cw_ref, j):
    return cw_ref[j, 0:1, :], cw_ref[j, 1:2, :], cw_ref[j, 2:3, :]


def _row_block(m, target=256):
    if m <= target:
        return m
    for b in range(target, 7, -8):
        if m % b == 0:
            return b
    return m


def mlp_fwd(x, g_pre, g_post, w_up, w_down, conv_w, conv_b, target=None, tm=256):
    t, d = x.shape
    nt = t // tm
    h8 = CONV_HALO
    with_loss = target is not None
    n_extra = 1 if with_loss else 0

    def body(x_ref, g2_ref, g3_ref, wup_hbm, wdn_hbm, cw_ref, cb_ref, *rest):
        tgt_ref = rest[0] if with_loss else None
        xo_ref, u_ref, s_ref, a_ref, f_ref, h_ref = rest[n_extra:n_extra + 6]
        loss_ref = rest[n_extra + 6] if with_loss else None
        wup_v, wdn_v, tail, sem = rest[-4:]
        i = pl.program_id(0)

        @pl.when(i == 0)
        def _():
            c1 = pltpu.make_async_copy(wup_hbm, wup_v, sem.at[0])
            c2 = pltpu.make_async_copy(wdn_hbm, wdn_v, sem.at[1])
            c1.start()
            c2.start()
            tail[...] = jnp.zeros_like(tail)
            if with_loss:
                loss_ref[...] = jnp.zeros_like(loss_ref)
            c1.wait()
            c2.wait()

        xv = x_ref[...]
        h, _, _ = _rms_fwd(xv, g2_ref[...])
        hb = h.astype(BF16)
        h_ref[...] = hb
        acc = jnp.zeros((tm, d), F32)
        for k in range(2):
            cs = []
            for s in range(2):
                j, cols = k + 2 * s, slice((2 * k + s) * FF_CHUNK, (2 * k + s + 1) * FF_CHUNK)
                uf = _dot(hb, wup_v[j])
                u_ref[:, cols] = uf.astype(BF16)
                ext = jnp.concatenate([tail[j], uf], axis=0)
                tail[j] = uf[tm - h8:tm, :]
                w0, w1, w2 = _conv_taps(cw_ref, j)
                cs.append(cb_ref[j] + w2 * uf + w1 * pltpu.roll(ext, 1, axis=0)[h8:, :]
                          + w0 * pltpu.roll(ext, 2, axis=0)[h8:, :])
            cg, cv = cs
            sg = jax.nn.sigmoid(cg)
            sil = cg * sg
            ab = (sil * cv).astype(BF16)
            a_ref[:, k * FF_CHUNK:(k + 1) * FF_CHUNK] = ab
            s_ref[:, 2 * k * FF_CHUNK:(2 * k + 1) * FF_CHUNK] = sil.astype(BF16)
            s_ref[:, (2 * k + 1) * FF_CHUNK:(2 * k + 2) * FF_CHUNK] = (cv * (sg + sil * (1.0 - sg))).astype(BF16)
            acc = acc + _dot(ab, wdn_v[k * FF_CHUNK:(k + 1) * FF_CHUNK, :])
        f_ref[...] = acc
        y, _, _ = _rms_fwd(acc, g3_ref[...])
        if with_loss:
            err = (xv + y) - tgt_ref[...]
            xo_ref[...] = err * (1.0 / d)
            loss_ref[...] += 0.5 * jnp.sum(jnp.mean(err * err, axis=-1, keepdims=True), axis=0, keepdims=True)
        else:
            xo_ref[...] = xv + y

    row = pl.BlockSpec((tm, d), lambda i: (i, 0))
    wide = pl.BlockSpec((tm, 2 * D_FF), lambda i: (i, 0))
    vec = _full((1, d))
    extra = [target] if with_loss else []
    return pl.pallas_call(
        body, grid=(nt,),
        in_specs=[row, vec, vec, ANY, ANY, _full(conv_w.shape), _full(conv_b.shape)] + [row] * n_extra,
        out_specs=[row, wide, wide, pl.BlockSpec((tm, D_FF), lambda i: (i, 0)), row, row] + [_full((1, 1))] * n_extra,
        out_shape=[jax.ShapeDtypeStruct((t, d), F32), jax.ShapeDtypeStruct((t, 2 * D_FF), BF16),
                   jax.ShapeDtypeStruct((t, 2 * D_FF), BF16), jax.ShapeDtypeStruct((t, D_FF), BF16),
                   jax.ShapeDtypeStruct((t, d), F32), jax.ShapeDtypeStruct((t, d), BF16)]
        + [jax.ShapeDtypeStruct((1, 1), F32)] * n_extra,
        scratch_shapes=[pltpu.VMEM(w_up.shape, BF16), pltpu.VMEM(w_down.shape, BF16),
                        pltpu.VMEM((N_SHARD, h8, FF_CHUNK), F32), pltpu.SemaphoreType.DMA((2,))],
        compiler_params=_params(1), name="mlp_fwd_loss" if with_loss else "mlp_fwd")(
            x, g_pre, g_post, w_up, w_down, conv_w, conv_b, *extra)


def _rowsum8(v):
    return jnp.sum(v.reshape(v.shape[0] // 8, 8, v.shape[1]), axis=0)


def mlp_bwd(dxo, f, x, u, sp, g_pre, g_post, w_up, w_down, conv_w, tm=256):
    t, d = x.shape
    nt = t // tm
    h8 = CONV_HALO

    def body(dxo_ref, f_ref, x_ref, u_ref, s_ref, g2_ref, g3_ref, wup_hbm, wdn_hbm, cw_ref,
             dx_ref, du_ref, df_ref, dg2_ref, dg3_ref, dcw_ref, dcb_ref,
             wup_v, wdn_v, carry, sem):
        @pl.when(pl.program_id(0) == 0)
        def _():
            c1 = pltpu.make_async_copy(wup_hbm, wup_v, sem.at[0])
            c2 = pltpu.make_async_copy(wdn_hbm, wdn_v, sem.at[1])
            c1.start()
            c2.start()
            carry[...] = jnp.zeros_like(carry)
            dg2_ref[...] = jnp.zeros_like(dg2_ref)
            dg3_ref[...] = jnp.zeros_like(dg3_ref)
            dcw_ref[...] = jnp.zeros_like(dcw_ref)
            dcb_ref[...] = jnp.zeros_like(dcb_ref)
            c1.wait()
            c2.wait()

        g3 = g3_ref[...]
        dxo = dxo_ref[...]
        _, fh, rf = _rms_fwd(f_ref[...], g3)
        dg3_ref[...] += _rowsum8(dxo * fh)
        dfb = _rms_bwd(fh, rf, g3, dxo).astype(BF16)
        df_ref[...] = dfb
        dh = jnp.zeros((tm, d), F32)
        for k in range(2):
            da = _dot_nt(dfb, wdn_v[k * FF_CHUNK:(k + 1) * FF_CHUNK, :])
            for s in range(2):
                j = k + 2 * s
                cols = slice((2 * k + s) * FF_CHUNK, (2 * k + s + 1) * FF_CHUNK)
                dc = da * s_ref[:, (2 * k + 1 - s) * FF_CHUNK:(2 * k + 2 - s) * FF_CHUNK].astype(F32)
                uf = u_ref[:, cols].astype(F32)
                ext = jnp.concatenate([dc, carry[j]], axis=0)
                carry[j] = dc[0:h8, :]
                dc1 = pltpu.roll(ext, tm + h8 - 1, axis=0)[0:tm, :]
                dc2 = pltpu.roll(ext, tm + h8 - 2, axis=0)[0:tm, :]
                dcb_ref[j] += _rowsum8(dc)
                dcw_ref[j, 2] += _rowsum8(dc * uf)
                dcw_ref[j, 1] += _rowsum8(dc1 * uf)
                dcw_ref[j, 0] += _rowsum8(dc2 * uf)
                dub = (cw_ref[j, 2:3, :] * dc + cw_ref[j, 1:2, :] * dc1 + cw_ref[j, 0:1, :] * dc2).astype(BF16)
                du_ref[:, cols] = dub
                dh = dh + _dot_nt(dub, wup_v[j])
        g2 = g2_ref[...]
        _, xh, rx = _rms_fwd(x_ref[...], g2)
        dg2_ref[...] += _rowsum8(dh * xh)
        dx_ref[...] = dxo + _rms_bwd(xh, rx, g2, dh)

    row = pl.BlockSpec((tm, d), lambda i: (nt - 1 - i, 0))
    wide = pl.BlockSpec((tm, 2 * D_FF), lambda i: (nt - 1 - i, 0))
    vec = _full((1, d))
    acc = _full((8, d))
    dcw_shape, dcb_shape = (N_SHARD, 3, 8, FF_CHUNK), (N_SHARD, 8, FF_CHUNK)
    return pl.pallas_call(
        body, grid=(nt,),
        in_specs=[row, row, row, wide, wide, vec, vec, ANY, ANY, _full(conv_w.shape)],
        out_specs=[row, wide, row, acc, acc, _full(dcw_shape), _full(dcb_shape)],
        out_shape=[jax.ShapeDtypeStruct((t, d), F32), jax.ShapeDtypeStruct((t, 2 * D_FF), BF16),
                   jax.ShapeDtypeStruct((t, d), BF16),
                   jax.ShapeDtypeStruct((8, d), F32), jax.ShapeDtypeStruct((8, d), F32),
                   jax.ShapeDtypeStruct(dcw_shape, F32), jax.ShapeDtypeStruct(dcb_shape, F32)],
        scratch_shapes=[pltpu.VMEM(w_up.shape, BF16), pltpu.VMEM(w_down.shape, BF16),
                        pltpu.VMEM((N_SHARD, h8, FF_CHUNK), F32), pltpu.SemaphoreType.DMA((2,))],
        compiler_params=_params(1, VMEM_LIMIT_MLP_BWD), name="mlp_bwd")(
            dxo, f, x, u, sp, g_pre, g_post, w_up, w_down, conv_w)


def grad_matmul(a, b, bm, bn, name, tk=2048, interleaved=False, after=None):
    t, m = a.shape
    n = b.shape[1]
    tk = min(tk, t)
    nk = t // tk
    place = (lambda j: (j % 2) * 2 + j // 2) if interleaved else (lambda j: j)
    extra = [] if after is None else [after]

    def body(a_ref, b_ref, *rest):
        o_ref, ob_ref = rest[len(extra):]
        kk = pl.program_id(2)

        @pl.when(kk == 0)
        def _():
            o_ref[...] = jnp.zeros_like(o_ref)

        o_ref[...] += _dot_tn(a_ref[...], b_ref[...])

        @pl.when(kk == nk - 1)
        def _():
            ob_ref[...] = o_ref[...].astype(BF16)

    ospec = pl.BlockSpec((None, bm, bn), lambda j, i, kk: (place(j), i, 0))
    return pl.pallas_call(
        body, grid=(n // bn, m // bm, nk),
        in_specs=[pl.BlockSpec((tk, bm), lambda j, i, kk: (kk, i)),
                  pl.BlockSpec((tk, bn), lambda j, i, kk: (kk, j))]
        + [ANY] * len(extra),
        out_specs=[ospec, ospec],
        out_shape=[jax.ShapeDtypeStruct((n // bn, m, bn), F32), jax.ShapeDtypeStruct((n // bn, m, bn), BF16)],
        compiler_params=_params(3), name=name)(a, b, *extra)


def _decay_tables():
    log_gamma = jnp.log(1.0 - 2.0 ** (-5.0 - jnp.arange(RET_HEADS, dtype=F32)))
    i = jnp.arange(RET_CHUNK, dtype=F32)
    rel = i[:, None] - i[None, :]
    intra = jnp.where(rel >= 0, jnp.exp(jnp.maximum(rel, 0.0) * log_gamma[:, None, None]), 0.0)
    cross = jnp.exp((i + 1.0) * log_gamma[:, None])[:, :, None]
    inner = jnp.exp((RET_CHUNK - 1.0 - i) * log_gamma[:, None])[:, :, None]
    chunk = [float(np.exp(np.float32(RET_CHUNK) * np.log(np.float32(1.0 - 2.0 ** (-5.0 - h))).astype(np.float32)))
             for h in range(RET_HEADS)]
    return intra, cross, inner, chunk


def ret_proj(x, g_pre, w_in, cos, sin, tm=512):
    t, d = x.shape
    nt = t // tm
    per = RET_IN_SHARD // RET_QK

    def body(x_ref, g_ref, win_hbm, c_ref, s_ref, pj_ref, h_ref, win_v, sem):
        @pl.when(pl.program_id(0) == 0)
        def _():
            cp = pltpu.make_async_copy(win_hbm, win_v, sem)
            cp.start()
            cp.wait()

        h, _, _ = _rms_fwd(x_ref[...], g_ref[...])
        hb = h.astype(BF16)
        h_ref[...] = hb
        c = c_ref[...]
        s = s_ref[...]
        for j in range(N_SHARD):
            pjj = _dot(hb, win_v[j])
            for bb in range(per):
                b = per * j + bb
                blk = pjj[:, bb * RET_QK:(bb + 1) * RET_QK]
                if b < 2 * RET_HEADS:
                    x1, x2 = blk[:, :128], blk[:, 128:]
                    o1 = x1 * c - x2 * s
                    o2 = x2 * c + x1 * s
                    if b < RET_HEADS:
                        o1 = o1 * (RET_QK ** -0.5)
                        o2 = o2 * (RET_QK ** -0.5)
                    pj_ref[:, b * RET_QK:b * RET_QK + 128] = o1.astype(BF16)
                    pj_ref[:, b * RET_QK + 128:(b + 1) * RET_QK] = o2.astype(BF16)
                else:
                    pj_ref[:, b * RET_QK:(b + 1) * RET_QK] = blk.astype(BF16)

    row = pl.BlockSpec((tm, d), lambda i: (i, 0))
    tab = pl.BlockSpec((tm, 128), lambda i: (i, 0))
    return pl.pallas_call(
        body, grid=(nt,),
        in_specs=[row, _full((1, d)), ANY, tab, tab],
        out_specs=[pl.BlockSpec((tm, RET_IN), lambda i: (i, 0)), row],
        out_shape=[jax.ShapeDtypeStruct((t, RET_IN), BF16), jax.ShapeDtypeStruct((t, d), BF16)],
        scratch_shapes=[pltpu.VMEM(w_in.shape, BF16), pltpu.SemaphoreType.DMA],
        compiler_params=_params(1), name="ret_proj")(x, g_pre, w_in, cos, sin)


def ret_core_fwd(pj, intra, cross, inner, chunk_decay):
    t = pj.shape[0]
    nc = t // RET_CHUNK
    c = RET_CHUNK
    per = RET_STEP_CHUNKS
    qk_all = RET_HEADS * RET_QK
    v_all = RET_HEADS * RET_V

    def body(q_ref, k_ref, v_ref, dm_ref, cr_ref, in_ref, o_ref, sp_ref, state):
        @pl.when(pl.program_id(0) == 0)
        def _():
            state[...] = jnp.zeros_like(state)

        for h in range(RET_HEADS):
            for cc in range(per):
                rows = slice(cc * c, (cc + 1) * c)
                q = q_ref[rows, h * RET_QK:(h + 1) * RET_QK]
                k = k_ref[rows, h * RET_QK:(h + 1) * RET_QK]
                v = v_ref[rows, h * RET_V:(h + 1) * RET_V]
                sb = state[h].astype(BF16)
                sp_ref[cc, h] = sb
                sc = _dot_nt(q, k) * dm_ref[h]
                o_ref[rows, h * RET_V:(h + 1) * RET_V] = (_dot(sc.astype(BF16), v)
                                                          + _dot(q, sb) * cr_ref[h]).astype(BF16)
                kd = (k.astype(F32) * in_ref[h]).astype(BF16)
                state[h] = state[h] * chunk_decay[h] + _dot_tn(kd, v)

    return pl.pallas_call(
        body, grid=(nc // per,),
        in_specs=[pl.BlockSpec((per * c, qk_all), lambda n: (n, 0)), pl.BlockSpec((per * c, qk_all), lambda n: (n, 1)),
                  pl.BlockSpec((per * c, v_all), lambda n: (n, 1)),
                  _full(intra.shape), _full(cross.shape), _full(inner.shape)],
        out_specs=[pl.BlockSpec((per * c, v_all), lambda n: (n, 0)),
                   pl.BlockSpec((per, RET_HEADS, RET_QK, RET_V), lambda n: (n, 0, 0, 0))],
        out_shape=[jax.ShapeDtypeStruct((t, v_all), BF16),
                   jax.ShapeDtypeStruct((nc, RET_HEADS, RET_QK, RET_V), BF16)],
        scratch_shapes=[pltpu.VMEM((RET_HEADS, RET_QK, RET_V), F32)],
        compiler_params=_params(1), name="ret_core_fwd")(pj, pj, pj, intra, cross, inner)


def _group_norm(o_h):
    mu = jnp.mean(o_h, axis=-1, keepdims=True)
    dev = o_h - mu
    rstd = lax.rsqrt(jnp.mean(dev * dev, axis=-1, keepdims=True) + EPS)
    return dev * rstd, rstd


def ret_out_fwd(o, pj, x, gn_gain, g_post, w_out, tm=512):
    t, d = x.shape
    nt = t // tm
    v_all = RET_HEADS * RET_V

    def body(o_ref, g_ref, x_ref, gn_ref, g1_ref, w_ref, xo_ref, y_ref, out_ref):
        out = jnp.zeros((tm, d), F32)
        for h in range(RET_HEADS):
            cols = slice(h * RET_V, (h + 1) * RET_V)
            ohat, _ = _group_norm(o_ref[:, cols].astype(F32))
            g = g_ref[:, cols].astype(F32)
            yb = (g * jax.nn.sigmoid(g) * (ohat * gn_ref[:, cols])).astype(BF16)
            y_ref[:, cols] = yb
            out = out + _dot(yb, w_ref[cols, :])
        out_ref[...] = out
        m, _, _ = _rms_fwd(out, g1_ref[...])
        xo_ref[...] = x_ref[...] + m

    row = pl.BlockSpec((tm, d), lambda i: (i, 0))
    wide = pl.BlockSpec((tm, v_all), lambda i: (i, 0))
    return pl.pallas_call(
        body, grid=(nt,),
        in_specs=[wide, pl.BlockSpec((tm, v_all), lambda i: (i, 2)), row, _full((1, v_all)), _full((1, d)),
                  _full(w_out.shape)],
        out_specs=[row, wide, row],
        out_shape=[jax.ShapeDtypeStruct((t, d), F32), jax.ShapeDtypeStruct((t, v_all), BF16),
                   jax.ShapeDtypeStruct((t, d), F32)],
        compiler_params=_params(1), name="ret_out_fwd")(o, pj, x, gn_gain, g_post, w_out)


def ret_out_bwd(dxo, out, o, pj, gn_gain, g_post, w_out, after, tm=512):
    t, d = out.shape
    nt = t // tm
    v_all = RET_HEADS * RET_V

    def body(dxo_ref, out_ref, o_ref, g_ref, gn_ref, g1_ref, w_ref, after_ref,
             dout_ref, dgate_ref, do_ref, dg1_ref, dgn_ref):
        @pl.when(pl.program_id(0) == 0)
        def _():
            dg1_ref[...] = jnp.zeros_like(dg1_ref)
            dgn_ref[...] = jnp.zeros_like(dgn_ref)

        g1 = g1_ref[...]
        dxo = dxo_ref[...]
        _, oh_, r_ = _rms_fwd(out_ref[...], g1)
        dg1_ref[...] += _colsum(dxo * oh_)
        doutb = _rms_bwd(oh_, r_, g1, dxo).astype(BF16)
        dout_ref[...] = doutb
        for h in range(RET_HEADS):
            cols = slice(h * RET_V, (h + 1) * RET_V)
            gn = gn_ref[:, cols]
            ohat, rstd = _group_norm(o_ref[:, cols].astype(F32))
            g = g_ref[:, cols].astype(F32)
            sg = jax.nn.sigmoid(g)
            dyh = _dot_nt(doutb, w_ref[cols, :])
            sil = g * sg
            dgate_ref[:, cols] = (dyh * (ohat * gn) * (sg + sil * (1.0 - sg))).astype(BF16)
            don = dyh * sil
            dgn_ref[:, cols] += _colsum(don * ohat)
            dohat = don * gn
            do_ref[:, cols] = (rstd * (dohat - jnp.mean(dohat, axis=-1, keepdims=True)
                                       - ohat * jnp.mean(dohat * ohat, axis=-1, keepdims=True))).astype(BF16)

    row = pl.BlockSpec((tm, d), lambda i: (i, 0))
    wide = pl.BlockSpec((tm, v_all), lambda i: (i, 0))
    gate = pl.BlockSpec((tm, v_all), lambda i: (i, 2))
    return pl.pallas_call(
        body, grid=(nt,),
        in_specs=[row, row, wide, gate, _full((1, v_all)), _full((1, d)), _full(w_out.shape), ANY],
        out_specs=[row, gate, wide, _full((1, d)), _full((1, v_all))],
        out_shape=[jax.ShapeDtypeStruct((t, d), BF16), jax.ShapeDtypeStruct((t, RET_IN), BF16),
                   jax.ShapeDtypeStruct((t, v_all), BF16), jax.ShapeDtypeStruct((1, d), F32),
                   jax.ShapeDtypeStruct((1, v_all), F32)],
        compiler_params=_params(1), name="ret_out_bwd")(dxo, out, o, pj, gn_gain, g_post, w_out, after)


def ret_core_bwd(pj, do, sprev, cos, sin, dpj, intra, cross, inner, chunk_decay):
    t = pj.shape[0]
    nc = t // RET_CHUNK
    c = RET_CHUNK
    per = RET_STEP_CHUNKS
    qk_all = RET_HEADS * RET_QK
    v_all = RET_HEADS * RET_V
    scale = RET_QK ** -0.5

    def body(q_ref, k_ref, v_ref, do_ref, sp_ref, c_ref, s_ref, dm_ref, cr_ref, in_ref, dpj_in, dpj_ref, dstate):
        @pl.when(pl.program_id(0) == 0)
        def _():
            dstate[...] = jnp.zeros_like(dstate)

        for h in range(RET_HEADS):
            for cc in reversed(range(per)):
                rows = slice(cc * c, (cc + 1) * c)
                cs = c_ref[rows, :]
                sn = s_ref[rows, :]
                q = q_ref[rows, h * RET_QK:(h + 1) * RET_QK]
                k = k_ref[rows, h * RET_QK:(h + 1) * RET_QK]
                v = v_ref[rows, h * RET_V:(h + 1) * RET_V]
                doh = do_ref[rows, h * RET_V:(h + 1) * RET_V]
                dm = dm_ref[h]
                ab = (_dot_nt(q, k) * dm).astype(BF16)
                dab = (_dot_nt(doh, v) * dm).astype(BF16)
                dsb = dstate[h].astype(BF16)
                kd = (k.astype(F32) * in_ref[h]).astype(BF16)
                dv = _dot_tn(ab, doh) + _dot(kd, dsb)
                dq = _dot(dab, k) + cr_ref[h] * _dot_nt(doh, sp_ref[cc, h])
                dk = _dot_tn(dab, q) + in_ref[h] * _dot_nt(v, dsb)
                qd = (q.astype(F32) * cr_ref[h]).astype(BF16)
                dstate[h] = dstate[h] * chunk_decay[h] + _dot_tn(qd, doh)
                for base, dd, sc in ((h * RET_QK, dq, scale), (qk_all + h * RET_QK, dk, 1.0)):
                    d1, d2 = dd[:, :128], dd[:, 128:]
                    dpj_ref[rows, base:base + 128] = ((d1 * cs + d2 * sn) * sc).astype(BF16)
                    dpj_ref[rows, base + 128:base + RET_QK] = ((d2 * cs - d1 * sn) * sc).astype(BF16)
                dpj_ref[rows, 2 * qk_all + h * RET_V:2 * qk_all + (h + 1) * RET_V] = dv.astype(BF16)

    rev = lambda n: nc // per - 1 - n
    tab = pl.BlockSpec((per * c, 128), lambda n: (rev(n), 0))
    return pl.pallas_call(
        body, grid=(nc // per,),
        in_specs=[pl.BlockSpec((per * c, qk_all), lambda n: (rev(n), 0)),
                  pl.BlockSpec((per * c, qk_all), lambda n: (rev(n), 1)),
                  pl.BlockSpec((per * c, v_all), lambda n: (rev(n), 1)),
                  pl.BlockSpec((per * c, v_all), lambda n: (rev(n), 0)),
                  pl.BlockSpec((per, RET_HEADS, RET_QK, RET_V), lambda n: (rev(n), 0, 0, 0)),
                  tab, tab, _full(intra.shape), _full(cross.shape), _full(inner.shape), ANY],
        out_specs=pl.BlockSpec((per * c, 2 * qk_all + v_all), lambda n: (rev(n), 0)),
        out_shape=jax.ShapeDtypeStruct((t, RET_IN), BF16),
        scratch_shapes=[pltpu.VMEM((RET_HEADS, RET_QK, RET_V), F32)],
        input_output_aliases={10: 0},
        compiler_params=_params(1), name="ret_core_bwd")(pj, pj, pj, do, sprev, cos, sin, intra, cross, inner, dpj)


def ret_in_bwd(dpj, dres, x, g_pre, w_in, after, tm=512):
    t, d = x.shape
    nt = t // tm

    def body(dpj_ref, dres_ref, x_ref, g_ref, win_hbm, after_ref, dx_ref, dg_ref, win_v, sem):
        @pl.when(pl.program_id(0) == 0)
        def _():
            cp = pltpu.make_async_copy(win_hbm, win_v, sem)
            cp.start()
            dg_ref[...] = jnp.zeros_like(dg_ref)
            cp.wait()

        g = g_ref[...]
        dh = jnp.zeros((tm, d), F32)
        for j in range(N_SHARD):
            dh = dh + _dot_nt(dpj_ref[:, j * RET_IN_SHARD:(j + 1) * RET_IN_SHARD], win_v[j])
        _, xh, rx = _rms_fwd(x_ref[...], g)
        dg_ref[...] += _colsum(dh * xh)
        dx_ref[...] = dres_ref[...] + _rms_bwd(xh, rx, g, dh)

    row = pl.BlockSpec((tm, d), lambda i: (i, 0))
    return pl.pallas_call(
        body, grid=(nt,),
        in_specs=[pl.BlockSpec((tm, RET_IN), lambda i: (i, 0)), row, row, _full((1, d)), ANY, ANY],
        out_specs=[row, _full((1, d))],
        out_shape=[jax.ShapeDtypeStruct((t, d), F32), jax.ShapeDtypeStruct((1, d), F32)],
        scratch_shapes=[pltpu.VMEM(w_in.shape, BF16), pltpu.SemaphoreType.DMA],
        compiler_params=_params(1), name="ret_in_bwd")(dpj, dres, x, g_pre, w_in, after)


_CHIP_FLIPS = ((1, 0), (0, 1), (1, 1))


def _flip(v, b):
    return 1 - v if b else v


_HBM = pl.BlockSpec(memory_space=pltpu.HBM)
_SEM = pl.BlockSpec(memory_space=pltpu.SEMAPHORE)
_EFFECT = pltpu.SideEffectType.DATAFLOW_SIDE_EFFECTING


def _chip_copies(mode, srcs, lands, send_sems, recv_sems):
    x, y, c = lax.axis_index("x"), lax.axis_index("y"), lax.axis_index("c")
    copies = []
    for t in range(len(lands)):
        if mode == "swap":
            copies.append(pltpu.make_async_remote_copy(
                src_ref=srcs[t], dst_ref=lands[t], send_sem=send_sems.at[t], recv_sem=recv_sems.at[t],
                device_id=(x, y, 1 - c), device_id_type=MESH))
            continue
        if mode == "everyone":
            for m in range(1, 8):
                bx, by, bc = (m >> 2) & 1, (m >> 1) & 1, m & 1
                copies.append(pltpu.make_async_remote_copy(
                    src_ref=srcs[t], dst_ref=lands[t].at[4 * x + 2 * y + c], send_sem=send_sems.at[7 * t + m - 1],
                    recv_sem=recv_sems.at[7 * t + m - 1], device_id=(_flip(x, bx), _flip(y, by), _flip(c, bc)),
                    device_id_type=MESH))
            continue
        for k, (bx, by) in enumerate(_CHIP_FLIPS):
            px, py = _flip(x, bx), _flip(y, by)
            target = (px, py, c)
            if mode == "gather":
                src, dst = srcs[t], lands[t].at[2 * x + y]
            elif mode == "gather_half":
                half = pl.ds(c * (srcs[t].shape[0] // 2), srcs[t].shape[0] // 2)
                src, dst = srcs[t].at[half], lands[t].at[2 * x + y, half]
            elif mode == "forward_half":
                half = pl.ds(c * (lands[t].shape[1] // 2), lands[t].shape[1] // 2)
                src = dst = lands[t].at[2 * px + py, half]
                target = (x, y, 1 - c)
            else:
                src, dst = srcs[t].at[2 * px + py], lands[t].at[k]
            copies.append(pltpu.make_async_remote_copy(
                src_ref=src, dst_ref=dst, send_sem=send_sems.at[3 * t + k], recv_sem=recv_sems.at[3 * t + k],
                device_id=target, device_id_type=MESH))
    return copies


def exchange_start(mode, srcs, lands, name, after=None):
    n, ns = len(lands), len(srcs)
    extra = [] if after is None else [after]

    def body(*refs):
        ins, lnd = refs[:ns], refs[ns:ns + n]
        send_sems, recv_sems = refs[ns + n + len(extra)], refs[ns + n + len(extra) + 1]
        token = refs[-1]
        for cp in _chip_copies(mode, ins, lnd, send_sems, recv_sems):
            cp.start()
        token[...] = jnp.zeros(token.shape, token.dtype)

    hbm = lambda a: pltpu.with_memory_space_constraint(a, pltpu.HBM)
    passed = list(srcs) + list(lands)
    n_sem = {"swap": 1, "everyone": 7}.get(mode, 3) * n
    return pl.pallas_call(
        body, name=name,
        out_shape=(pltpu.SemaphoreType.DMA((n_sem,)), pltpu.SemaphoreType.DMA((n_sem,)),
                   *[pltpu.HBM(a.shape, a.dtype) for a in passed], jax.ShapeDtypeStruct((8, 128), F32)),
        in_specs=[_HBM] * (ns + n) + [ANY] * len(extra),
        out_specs=(_SEM, _SEM, *[_HBM] * (ns + n), pl.BlockSpec(memory_space=pltpu.VMEM)),
        input_output_aliases={i: 2 + i for i in range(ns + n)},
        compiler_params=pltpu.CompilerParams(has_side_effects=_EFFECT))(*[hbm(a) for a in passed], *extra)


def exchange_wait(mode, started, after, name):
    send_sems, recv_sems = started[0], started[1]
    passed = list(started[2:-1])
    n = len(passed) if mode == "forward_half" else len(passed) // 2
    ns = len(passed) - n
    after = list(after) if isinstance(after, (list, tuple)) else [after]

    def body(*refs):
        ins, lnd = refs[:ns], refs[ns:ns + n]
        for cp in _chip_copies(mode, ins, lnd, refs[ns + n], refs[ns + n + 1]):
            cp.wait_send()
            cp.wait_recv()

    outs = pl.pallas_call(
        body, name=name, out_shape=tuple(pltpu.HBM(a.shape, a.dtype) for a in passed),
        in_specs=[_HBM] * (ns + n) + [_SEM, _SEM] + [ANY] * len(after), out_specs=tuple([_HBM] * (ns + n)),
        input_output_aliases={i: i for i in range(ns + n)},
        compiler_params=pltpu.CompilerParams(has_side_effects=_EFFECT))(*passed, send_sems, recv_sems, *after)
    return list(outs[:ns]), list(outs[ns:])


def plane_sum(slot, full, recv, name, bm=256):
    _, m, n = full.shape
    bm = _row_block(m, bm)

    def body(slot_ref, o_ref, r_ref, s_ref):
        s_ref[...] = ((o_ref[...] + r_ref[0].astype(F32)) + r_ref[1].astype(F32)) + r_ref[2].astype(F32)

    return pl.pallas_call(
        body,
        grid_spec=pltpu.PrefetchScalarGridSpec(
            num_scalar_prefetch=1, grid=(m // bm,),
            in_specs=[pl.BlockSpec((None, bm, n), lambda i, s: (s[0], i, 0)),
                      pl.BlockSpec((3, bm, n), lambda i, s: (0, i, 0))],
            out_specs=pl.BlockSpec((bm, n), lambda i, s: (i, 0))),
        out_shape=jax.ShapeDtypeStruct((m, n), F32), compiler_params=_params(1), name=name)(slot, full, recv)


def sum_slots(parts, name, bm=312):
    _, r, n = parts.shape
    bm = bm if r % bm == 0 else r

    def body(p_ref, s_ref):
        acc = p_ref[0]
        for k in range(1, 8):
            acc = acc + p_ref[k]
        s_ref[...] = acc

    return pl.pallas_call(
        body, grid=(r // bm,), in_specs=[pl.BlockSpec((8, bm, n), lambda i: (0, i, 0))],
        out_specs=pl.BlockSpec((bm, n), lambda i: (i, 0)), out_shape=jax.ShapeDtypeStruct((r, n), F32),
        compiler_params=_params(1), name=name)(parts)


def _adamw_math(w, g, m, v):
    m = ADAM_B1 * m + (1.0 - ADAM_B1) * g
    v = ADAM_B2 * v + (1.0 - ADAM_B2) * (g * g)
    m_hat = m / (1.0 - ADAM_B1 ** ADAM_STEP)
    v_hat = v / (1.0 - ADAM_B2 ** ADAM_STEP)
    delta = -ADAM_LR * (m_hat / (jnp.sqrt(v_hat) + ADAM_EPS) + ADAM_WD * w)
    return delta, m, v


def adamw(w, m, v, grads, layer, prev, name, bm=256):
    _, _, n = w.shape
    mm = grads[0].shape[0]
    bm = _row_block(mm, bm)
    ng = len(grads)

    def body(*refs):
        w_ref, m_ref, v_ref = refs[:3]
        g_refs = refs[3:3 + ng]
        g_out, d_out, m_out, v_out = refs[-4:]
        g = g_refs[0][...]
        for gr in g_refs[1:]:
            g = g + gr[...]
        delta, mn, vn = _adamw_math(w_ref[...], g, m_ref[...], v_ref[...])
        g_out[...] = g
        d_out[...] = delta
        m_out[...] = mn
        v_out[...] = vn

    slab = pl.BlockSpec((None, bm, n), lambda i: (layer, i, 0))
    flat = pl.BlockSpec((bm, n), lambda i: (i, 0))
    in_specs = [slab] * 3 + [flat] * ng
    args = [w, m, v, *grads]
    aliases = {}
    if prev is not None:
        in_specs += [ANY] * 4
        aliases = {3 + ng + q: q for q in range(4)}
        args += list(prev)
    return pl.pallas_call(
        body, grid=(mm // bm,), in_specs=in_specs, out_specs=[slab] * 4,
        out_shape=[jax.ShapeDtypeStruct(w.shape, F32)] * 4, input_output_aliases=aliases,
        compiler_params=_params(1), name=name)(*args)


def adamw_streamed(w, m, v, grads, layer, prev, name, bm=256):
    _, _, n = w.shape
    mm = grads[0].shape[0]
    bm = _row_block(mm, bm)
    nb, ng = mm // bm, len(grads)
    n_in, n_prev = 3 + ng, 0 if prev is None else 4

    def body(*refs):
        srcs = [r.at[layer] for r in refs[:3]] + list(refs[3:n_in])
        outs = [r.at[layer] for r in refs[n_in + n_prev:n_in + n_prev + 4]]
        inbuf, outbuf, sem_in, sem_out = refs[-4:]

        def read(i, q):
            slot = i % STREAM_BUFFERS
            return pltpu.make_async_copy(srcs[q].at[pl.ds(i * bm, bm)], inbuf.at[slot, q], sem_in.at[slot, q])

        def write(i, q):
            return pltpu.make_async_copy(outbuf.at[i % 2, q], outs[q].at[pl.ds(i * bm, bm)], sem_out.at[i % 2, q])

        for i in range(min(STREAM_BUFFERS, nb)):
            for q in range(n_in):
                read(i, q).start()
        for i in range(nb):
            slot = i % STREAM_BUFFERS
            for q in range(n_in):
                read(i, q).wait()
            if i >= 2:
                for q in range(4):
                    write(i - 2, q).wait()
            g = inbuf[slot, 3]
            for k in range(1, ng):
                g = g + inbuf[slot, 3 + k]
            delta, mn, vn = _adamw_math(inbuf[slot, 0], g, inbuf[slot, 1], inbuf[slot, 2])
            for q, val in enumerate((g, delta, mn, vn)):
                outbuf[i % 2, q] = val
            for q in range(4):
                write(i, q).start()
            if i + STREAM_BUFFERS < nb:
                for q in range(n_in):
                    read(i + STREAM_BUFFERS, q).start()
        for i in range(max(nb - 2, 0), nb):
            for q in range(4):
                write(i, q).wait()

    args = [w, m, v, *grads] + ([] if prev is None else list(prev))
    return pl.pallas_call(
        body, in_specs=[ANY] * len(args), out_specs=[ANY] * 4,
        out_shape=[jax.ShapeDtypeStruct(w.shape, F32)] * 4,
        input_output_aliases={n_in + q: q for q in range(n_prev)},
        scratch_shapes=[pltpu.VMEM((STREAM_BUFFERS, n_in, bm, n), F32), pltpu.VMEM((2, 4, bm, n), F32),
                        pltpu.SemaphoreType.DMA((STREAM_BUFFERS, n_in)), pltpu.SemaphoreType.DMA((2, 4))],
        compiler_params=_params(0), name=name)(*args)


def _pack_rows(parts, rows):
    flat = jnp.concatenate([p.reshape(-1) for p in parts])
    return jnp.pad(flat, (0, rows * 128 - flat.shape[0])).reshape(rows, 128)


def _as_shards(a, rows):
    return a.reshape(N_SHARD, rows, a.shape[-1])


def _local_step(x, pos_col, target, gains, pool_w, pool_scale, gn_gain, conv_w, conv_b, weights, send_grads):
    def gain(l, n):
        return gains[l, n].reshape(1, D_MODEL)

    inv_freq = (ROPE_BASE ** (-jnp.arange(0, RET_QK, 2, dtype=F32) / RET_QK)).reshape(1, RET_QK // 2)
    intra, cross, inner, chunk_decay = _decay_tables()
    dn_rows = D_FF // N_SHARD

    x1 = pool_fwd(x, gain(0, 0), gain(0, 1), pool_w, pool_scale)
    cos, sin = rope_tables(pos_col, inv_freq, x1)
    w_up0, w_dn0 = weights("mlp0", cos)
    w_dn0 = w_dn0.reshape(D_FF, D_MODEL)
    x2, u0, s0, a0, f0, h0 = mlp_fwd(x1, gain(0, 2), gain(0, 3), w_up0, w_dn0, conv_w[0], conv_b[0])
    w_in, w_out = weights("ret", x2)
    w_out = w_out.reshape(RET_HEADS * RET_V, D_MODEL)
    pj, hr = ret_proj(x2, gain(1, 0), w_in, cos, sin)
    o, sprev = ret_core_fwd(pj, intra, cross, inner, chunk_decay)
    x3, yb, out = ret_out_fwd(o, pj, x2, gn_gain, gain(1, 1), w_out)
    w_up1, w_dn1 = weights("mlp1", x3)
    w_dn1 = w_dn1.reshape(D_FF, D_MODEL)
    dx4, u1, s1, a1, f1, h1, loss = mlp_fwd(x3, gain(1, 2), gain(1, 3), w_up1, w_dn1, conv_w[1], conv_b[1], target)

    dx3, du1, df1, dg12, dg13, dcw1, dcb1 = mlp_bwd(
        dx4, f1, x3, u1, s1, gain(1, 2), gain(1, 3), w_up1, w_dn1, conv_w[1])
    dwup1 = grad_matmul(h1, du1, D_MODEL, FF_CHUNK, "grad_w_up_1", interleaved=True)
    dwdn1 = grad_matmul(a1, df1, FF_CHUNK, D_MODEL, "grad_w_down_1")
    tok = send_grads("mlp1", [dwup1, [_as_shards(g, dn_rows) for g in dwdn1]])
    dout, dpj, do, dg11, dgn = ret_out_bwd(dx3, out, o, pj, gn_gain, gain(1, 1), w_out, tok)
    dwout = grad_matmul(yb, dout, 1024, D_MODEL, "grad_w_out")
    dpj = ret_core_bwd(pj, do, sprev, cos, sin, dpj, intra, cross, inner, chunk_decay)
    dwin = grad_matmul(hr, dpj, D_MODEL, RET_IN_SHARD, "grad_w_in")
    tok = send_grads("ret", [dwin, [_as_shards(g, RET_V) for g in dwout]])
    dx2, dg10 = ret_in_bwd(dpj, dx3, x2, gain(1, 0), w_in, tok)
    dx1, du0, df0, dg02, dg03, dcw0, dcb0 = mlp_bwd(
        dx2, f0, x1, u0, s0, gain(0, 2), gain(0, 3), w_up0, w_dn0, conv_w[0])
    dwdn0 = grad_matmul(a0, df0, FF_CHUNK, D_MODEL, "grad_w_down_0")
    tok = send_grads("down0", [[_as_shards(g, dn_rows) for g in dwdn0]])
    dwup0 = grad_matmul(h0, du0, D_MODEL, FF_CHUNK, "grad_w_up_0", interleaved=True, after=tok)
    tok = send_grads("up0", [dwup0])
    dx0, dg00, dg01, dpscale, dpw = pool_bwd(dx1, x, gain(0, 0), gain(0, 1), pool_w, pool_scale, tok)

    rows = lambda g: jnp.sum(g, axis=0, keepdims=True)
    dgains = jnp.concatenate([dg00, dg01, rows(dg02), rows(dg03), dg10, dg11, rows(dg12), rows(dg13)],
                             axis=0).reshape(2, 4, D_MODEL)
    small = {"gains": dgains, "pool_scale": dpscale, "gn": dgn,
             "conv_w": jnp.sum(jnp.stack([dcw0, dcw1]), axis=3),
             "conv_b": jnp.sum(jnp.stack([dcb0, dcb1]), axis=2, keepdims=True), "pool_w": dpw}
    return loss, dx0, small


def kernel(x, positions, norm_gain, pool_w, pool_scale, ret_w_in, ret_gn_gain, ret_w_out, mlp_w_up, mlp_conv_w, mlp_conv_b, mlp_w_down, loss_target, m_norm_gain, m_pool_w, m_pool_scale, m_ret_w_in, m_ret_gn_gain, m_ret_w_out, m_mlp_w_up, m_mlp_conv_w, m_mlp_conv_b, m_mlp_w_down, v_norm_gain, v_pool_w, v_pool_scale, v_ret_w_in, v_ret_gn_gain, v_ret_w_out, v_mlp_w_up, v_mlp_conv_w, v_mlp_conv_b, v_mlp_w_down):
    t = x.shape[1]
    me = 2 * lax.axis_index("x") + lax.axis_index("y")
    me_slot = jnp.reshape(me, (1,)).astype(jnp.int32)

    small_parts = [norm_gain, ret_gn_gain, mlp_conv_w, pool_w]
    small_sizes = [p.size for p in small_parts]
    small_rows = -(-sum(small_sizes) // (128 * 8)) * 8
    gathers = {}

    def start_gather(group, srcs, after):
        lands = [lax.dynamic_update_index_in_dim(lax.empty((N_SHARD,) + s.shape, s.dtype), s, me, 0) for s in srcs]
        mode = "gather_half" if group == "mlp0" else "gather"
        gathers[group] = (mode, exchange_start(mode, srcs, lands, "gather_start_" + group, after=after))
        return gathers[group][1][-1]

    token = start_gather("small", [_pack_rows(small_parts, small_rows)], None)
    token = start_gather("mlp0", [mlp_w_up[0].astype(BF16), mlp_w_down[0].astype(BF16)], token)

    def weights(group, after):
        if group == "mlp0":
            tok = start_gather("ret", [ret_w_in[0].astype(BF16), ret_w_out[0].astype(BF16)], after)
            after = start_gather("mlp1", [mlp_w_up[1].astype(BF16), mlp_w_down[1].astype(BF16)], tok)
        mode, started = gathers[group]
        _, lands = exchange_wait(mode, started, after, "gather_wait_" + group)
        if mode == "gather_half":
            forward = exchange_start("forward_half", [], lands, "forward_start_" + group)
            _, lands = exchange_wait("forward_half", forward, forward[-1], "forward_wait_" + group)
        return lands

    sent, early = {}, {}

    def reduced(group, after, names):
        started, own = sent[group]
        _, recv = exchange_wait("scatter", started, after, "scatter_wait_" + group)
        return [plane_sum(me_slot, f, r, "plane_sum_" + nm)
                for f, r, nm in zip(own, recv, names)]

    def swap_start(planes, name):
        return exchange_start("swap", planes, [lax.empty(p.shape, p.dtype) for p in planes], name)

    def send_grads(group, pairs):
        lands = [lax.empty((3,) + b.shape[1:], BF16) for _, b in pairs]
        sent[group] = (exchange_start("scatter", [b for _, b in pairs], lands, "scatter_start_" + group),
                       [f for f, _ in pairs])
        token = sent[group][0][-1]
        if group == "up0":
            early["planes"] = (reduced("mlp1", token, ["w_up_1", "w_down_1"])
                               + reduced("ret", token, ["w_in", "w_out"]))
            early["swap"] = swap_start(early["planes"], "swap_start_a")
            token = early["swap"][-1]
        return token

    (smallg,) = weights("small", token)
    smallg = smallg.reshape(N_SHARD, -1)
    offs = np.cumsum([0] + small_sizes)
    piece = lambda i, shape: smallg[:, offs[i]:offs[i + 1]].reshape((N_SHARD,) + shape)
    gains = piece(0, (2, 4, 256)).transpose(1, 2, 0, 3).reshape(2, 4, D_MODEL)
    gn_full = piece(1, (512,)).reshape(1, RET_HEADS * RET_V)
    cw_full = piece(2, (2, 3, FF_CHUNK)).transpose(1, 0, 2, 3)
    pw_full = piece(3, (4, 64, 256)).transpose(1, 0, 2, 3).reshape(4, 256, 256).astype(BF16)
    cb_full = mlp_conv_b.reshape(2, N_SHARD, 1, FF_CHUNK)

    loss, dx0, small = _local_step(
        x[0], positions.reshape(t, 1).astype(F32), loss_target[0], gains, pw_full, pool_scale, gn_full,
        cw_full, cb_full, weights, send_grads)

    def small_adamw(w, m, v, grads, name):
        w3 = w.reshape(1, -1, w.shape[-1])
        out = adamw(w3, m.reshape(w3.shape), v.reshape(w3.shape), [g.reshape(w3.shape[1:]) for g in grads], 0, None, name)
        return [o.reshape(w.shape) for o in out]

    pw_f = small["pool_w"].reshape(4, N_SHARD, 64, 256).transpose(1, 0, 2, 3).reshape(N_SHARD, 256, 256)
    small = dict(small, loss=loss)
    small_order = ["gains", "pool_scale", "gn", "conv_w", "conv_b", "loss"]
    gsmall_sizes = [small[k].size for k in small_order]
    gsmall_rows = -(-sum(gsmall_sizes) // (128 * 8)) * 8
    gpack = _pack_rows([small[k] for k in small_order], gsmall_rows)
    mine = 2 * me + lax.axis_index("c")
    small_started = exchange_start(
        "everyone", [gpack], [lax.dynamic_update_index_in_dim(lax.empty((8,) + gpack.shape, F32), gpack, mine, 0)],
        "small_start")
    pool_token = send_grads("pool_w", [(pw_f, pw_f.astype(BF16))])

    res = {}
    planes_a, others_a = exchange_wait("swap", early["swap"], [small_started[-1], pool_token], "swap_wait_a")
    big = adamw_streamed
    res["ret_w_in"] = big(ret_w_in, m_ret_w_in, v_ret_w_in, (planes_a[2], others_a[2]), 0, None, "adamw_w_in")
    res["ret_w_out"] = big(ret_w_out, m_ret_w_out, v_ret_w_out, (planes_a[3], others_a[3]), 0, None, "adamw_w_out")
    up1 = big(mlp_w_up, m_mlp_w_up, v_mlp_w_up, (planes_a[0], others_a[0]), 1, None, "adamw_w_up_1")
    dn1 = big(mlp_w_down, m_mlp_w_down, v_mlp_w_down, (planes_a[1], others_a[1]), 1, None, "adamw_w_down_1")

    done_a = [res["ret_w_in"][0], res["ret_w_out"][0], up1[0], dn1[0]]
    planes_b = (reduced("up0", done_a, ["w_up_0"]) + reduced("down0", done_a, ["w_down_0"])
                + reduced("pool_w", done_a, ["pool_w"]))
    swap_b = swap_start(planes_b, "swap_start_b")

    _, (small_recv,) = exchange_wait("everyone", small_started, swap_b[-1], "small_wait")
    gsmall = sum_slots(small_recv, "sum_small").reshape(-1)
    goffs = np.cumsum([0] + gsmall_sizes)
    gpiece = lambda i: gsmall[goffs[i]:goffs[i + 1]].reshape(small[small_order[i]].shape)
    g_gains = lax.dynamic_slice_in_dim(gpiece(0), me * 256, 256, axis=2)
    g_gn = lax.dynamic_slice_in_dim(gpiece(2), me * RET_V, RET_V, axis=1)
    g_cw = lax.dynamic_index_in_dim(gpiece(3), me, 1, keepdims=False)
    res["norm_gain"] = small_adamw(norm_gain, m_norm_gain, v_norm_gain, [g_gains], "adamw_norm_gain")
    res["pool_scale"] = small_adamw(pool_scale, m_pool_scale, v_pool_scale, [gpiece(1)], "adamw_pool_scale")
    res["ret_gn_gain"] = small_adamw(ret_gn_gain, m_ret_gn_gain, v_ret_gn_gain, [g_gn], "adamw_gn_gain")
    res["mlp_conv_w"] = small_adamw(mlp_conv_w, m_mlp_conv_w, v_mlp_conv_w, [g_cw], "adamw_conv_w")
    res["mlp_conv_b"] = small_adamw(mlp_conv_b, m_mlp_conv_b, v_mlp_conv_b, [gpiece(4)], "adamw_conv_b")

    small_done = [res[k][0] for k in ("norm_gain", "pool_scale", "ret_gn_gain", "mlp_conv_w", "mlp_conv_b")]
    planes_b, others_b = exchange_wait("swap", swap_b, small_done, "swap_wait_b")
    res["mlp_w_up"] = big(mlp_w_up, m_mlp_w_up, v_mlp_w_up, (planes_b[0], others_b[0]), 0, up1, "adamw_w_up_0")
    res["mlp_w_down"] = big(mlp_w_down, m_mlp_w_down, v_mlp_w_down, (planes_b[1], others_b[1]), 0, dn1,
                            "adamw_w_down_0")
    res["pool_w"] = small_adamw(pool_w, m_pool_w, v_pool_w, (planes_b[2], others_b[2]), "adamw_pool_w")

    order = ["norm_gain", "pool_w", "pool_scale", "ret_w_in", "ret_gn_gain", "ret_w_out", "mlp_w_up", "mlp_conv_w",
             "mlp_conv_b", "mlp_w_down"]
    outs = [gpiece(5)[0, 0], dx0.reshape(x.shape)]
    for q in range(4):
        outs += [res[k][q] for k in order]
    return tuple(outs)
```

```python
import numpy as np
import jax
import jax.numpy as jnp
from jax import lax
from jax.experimental import pallas as pl
from jax.experimental.pallas import tpu as pltpu

F32 = jnp.float32
BF16 = jnp.bfloat16

D_MODEL = 1024
D_FF = 2816
FF_CHUNK = 1408
N_SHARD = 4
POOL_WINDOWS = (2, 4, 8, 16)
POOL_DIM = 256
POOL_HALO = 16
RET_HEADS = 4
RET_QK = 256
RET_V = 512
RET_CHUNK = 256
RET_STEP_CHUNKS = 2
RET_IN = 6144
RET_IN_SHARD = 1536
ROPE_BASE = 10000.0
EPS = 1e-6
CONV_HALO = 8

ADAM_LR, ADAM_B1, ADAM_B2, ADAM_EPS, ADAM_WD, ADAM_STEP = 0.001, 0.9, 0.999, 1e-08, 0.01, 10

VMEM_LIMIT = 56 * 1024 * 1024
VMEM_LIMIT_MLP_BWD = 62 * 1024 * 1024
STREAM_BUFFERS = 4
MESH = pl.DeviceIdType.MESH
ANY = pl.BlockSpec(memory_space=pl.ANY)


def _params(n_grid=1, limit=VMEM_LIMIT):
    return pltpu.CompilerParams(dimension_semantics=("arbitrary",) * n_grid, vmem_limit_bytes=limit)


def _dot(a, b):
    return jnp.dot(a, b, preferred_element_type=F32)


def _dot_nt(a, b):
    return lax.dot_general(a, b, (((1,), (1,)), ((), ())), preferred_element_type=F32)


def _dot_tn(a, b):
    return lax.dot_general(a, b, (((0,), (0,)), ((), ())), preferred_element_type=F32)


def _rms_fwd(x, gain):
    r = lax.rsqrt(jnp.mean(x * x, axis=-1, keepdims=True) + EPS)
    xh = x * r
    return xh * gain, xh, r


def _rms_bwd(xh, r, gain, dy):
    dxh = dy * gain
    return r * (dxh - xh * jnp.mean(dxh * xh, axis=-1, keepdims=True))


def _colsum(v):
    return jnp.sum(v, axis=0, keepdims=True)


def _full(shape):
    nd = len(shape)
    return pl.BlockSpec(shape, lambda *_: (0,) * nd)


def rope_tables(pos_col, inv_freq, after):
    t = pos_col.shape[0]
    tm = min(t, 1024)

    def body(p_ref, f_ref, after_ref, c_ref, s_ref):
        ang = p_ref[...] * f_ref[...]
        c_ref[...] = jnp.cos(ang)
        s_ref[...] = jnp.sin(ang)

    return pl.pallas_call(
        body, grid=(t // tm,),
        in_specs=[pl.BlockSpec((tm, 1), lambda i: (i, 0)), _full((1, 128)), ANY],
        out_specs=[pl.BlockSpec((tm, 128), lambda i: (i, 0))] * 2,
        out_shape=[jax.ShapeDtypeStruct((t, 128), F32)] * 2,
        compiler_params=_params(1), name="rope_tables")(pos_col, inv_freq, after)


def _window_sums(ext, backward):
    n = ext.shape[0]
    cur, sums = ext, []
    for g, win in enumerate(POOL_WINDOWS):
        if g > 0:
            cur = cur[:, POOL_DIM:]
        half = win // 2
        cur = cur + pltpu.roll(cur, n - half if backward else half, axis=0)
        sums.append(cur[:, 0:POOL_DIM])
    return sums


def _pool_diff(h_halo, h, row0, tm):
    t_idx = row0 + lax.broadcasted_iota(jnp.int32, (tm, 1), 0)
    sums = _window_sums(jnp.concatenate([h_halo, h], axis=0), backward=False)
    parts, inv_counts = [], []
    for g, win in enumerate(POOL_WINDOWS):
        inv = 1.0 / jnp.minimum(t_idx + 1, win).astype(F32)
        parts.append(sums[g][POOL_HALO:, :] * inv - h[:, g * POOL_DIM:(g + 1) * POOL_DIM])
        inv_counts.append(inv)
    return parts, inv_counts


def pool_fwd(x, g_pre, g_post, pool_w, pool_scale, tm=512):
    t, d = x.shape
    nt = t // tm

    def body(x_ref, g0_ref, g1_ref, w_ref, sc_ref, o_ref, hext):
        i = pl.program_id(0)

        @pl.when(i == 0)
        def _():
            hext[...] = jnp.zeros((POOL_HALO, d), F32)

        xv = x_ref[...]
        h, _, _ = _rms_fwd(xv, g0_ref[...])
        parts, _ = _pool_diff(hext[...], h, i * tm, tm)
        hext[...] = h[tm - POOL_HALO:tm, :]
        ys = [_dot(parts[g].astype(BF16), w_ref[g]) for g in range(len(POOL_WINDOWS))]
        y = jnp.concatenate(ys, axis=-1) * sc_ref[...]
        m, _, _ = _rms_fwd(y, g1_ref[...])
        o_ref[...] = xv + m

    row = pl.BlockSpec((tm, d), lambda i: (i, 0))
    return pl.pallas_call(
        body, grid=(nt,),
        in_specs=[row, _full((1, d)), _full((1, d)), _full(pool_w.shape), _full((1, d))],
        out_specs=row, out_shape=jax.ShapeDtypeStruct((t, d), F32),
        scratch_shapes=[pltpu.VMEM((POOL_HALO, d), F32)],
        compiler_params=_params(1), name="pool_fwd")(x, g_pre, g_post, pool_w, pool_scale)


def pool_bwd(dx1, x, g_pre, g_post, pool_w, pool_scale, after, tm=512):
    t, d = x.shape
    nt = t // tm
    ng = len(POOL_WINDOWS)

    def body(dx1_ref, x_ref, xh_ref, g0_ref, g1_ref, w_ref, sc_ref, after_ref,
             dx_ref, dg0_ref, dg1_ref, dsc_ref, dw_ref, enext):
        i = pl.program_id(0)
        r = nt - 1 - i

        @pl.when(i == 0)
        def _():
            enext[...] = jnp.zeros((POOL_HALO, d), F32)
            dg0_ref[...] = jnp.zeros_like(dg0_ref)
            dg1_ref[...] = jnp.zeros_like(dg1_ref)
            dsc_ref[...] = jnp.zeros_like(dsc_ref)
            dw_ref[...] = jnp.zeros_like(dw_ref)

        g0 = g0_ref[...]
        g1 = g1_ref[...]
        sc = sc_ref[...]
        xv = x_ref[...]
        h, xh, rx = _rms_fwd(xv, g0)
        h_halo, _, _ = _rms_fwd(xh_ref[...], g0)
        parts, inv_counts = _pool_diff(h_halo * jnp.where(r > 0, 1.0, 0.0), h, r * tm, tm)
        parts_b = [p.astype(BF16) for p in parts]
        ypre = jnp.concatenate([_dot(parts_b[g], w_ref[g]) for g in range(ng)], axis=-1)
        _, yh, ry = _rms_fwd(ypre * sc, g1)
        dm = dx1_ref[...]
        dg1_ref[...] += _colsum(dm * yh)
        dy = _rms_bwd(yh, ry, g1, dm)
        dsc_ref[...] += _colsum(dy * ypre)
        dyp = (dy * sc).astype(BF16)
        ddiffs = []
        for g in range(ng):
            cols = slice(g * POOL_DIM, (g + 1) * POOL_DIM)
            dw_ref[g] += _dot_tn(parts_b[g], dyp[:, cols])
            ddiffs.append(_dot_nt(dyp[:, cols], w_ref[g]))
        e = jnp.concatenate([ddiffs[g] * inv_counts[g] for g in range(ng)], axis=-1)
        sums = _window_sums(jnp.concatenate([e, enext[...]], axis=0), backward=True)
        enext[...] = e[0:POOL_HALO, :]
        dh = jnp.concatenate([sums[g][0:tm, :] - ddiffs[g] for g in range(ng)], axis=-1)
        dg0_ref[...] += _colsum(dh * xh)
        dx_ref[...] = dm + _rms_bwd(xh, rx, g0, dh)

    row = pl.BlockSpec((tm, d), lambda i: (nt - 1 - i, 0))
    halo = pl.BlockSpec((POOL_HALO, d), lambda i: (jnp.maximum((nt - 1 - i) * (tm // POOL_HALO) - 1, 0), 0))
    vec = _full((1, d))
    return pl.pallas_call(
        body, grid=(nt,),
        in_specs=[row, row, halo, vec, vec, _full(pool_w.shape), vec, ANY],
        out_specs=[row, vec, vec, vec, _full((ng, POOL_DIM, POOL_DIM))],
        out_shape=[jax.ShapeDtypeStruct((t, d), F32)] + [jax.ShapeDtypeStruct((1, d), F32)] * 3
        + [jax.ShapeDtypeStruct((ng, POOL_DIM, POOL_DIM), F32)],
        scratch_shapes=[pltpu.VMEM((POOL_HALO, d), F32)],
        compiler_params=_params(1), name="pool_bwd")(dx1, x, x, g_pre, g_post, pool_w, pool_scale, after)


def _conv_taps(cw_ref, j):
    return cw_ref[j, 0:1, :], cw_ref[j, 1:2, :], cw_ref[j, 2:3, :]


def _row_block(m, target=256):
    if m <= target:
        return m
    for b in range(target, 7, -8):
        if m % b == 0:
            return b
    return m


def mlp_fwd(x, g_pre, g_post, w_up, w_down, conv_w, conv_b, target=None, tm=256):
    t, d = x.shape
    nt = t // tm
    h8 = CONV_HALO
    with_loss = target is not None
    n_extra = 1 if with_loss else 0

    def body(x_ref, g2_ref, g3_ref, wup_hbm, wdn_hbm, cw_ref, cb_ref, *rest):
        tgt_ref = rest[0] if with_loss else None
        xo_ref, u_ref, s_ref, a_ref, f_ref, h_ref = rest[n_extra:n_extra + 6]
        loss_ref = rest[n_extra + 6] if with_loss else None
        wup_v, wdn_v, tail, sem = rest[-4:]
        i = pl.program_id(0)

        @pl.when(i == 0)
        def _():
            c1 = pltpu.make_async_copy(wup_hbm, wup_v, sem.at[0])
            c2 = pltpu.make_async_copy(wdn_hbm, wdn_v, sem.at[1])
            c1.start()
            c2.start()
            tail[...] = jnp.zeros_like(tail)
            if with_loss:
                loss_ref[...] = jnp.zeros_like(loss_ref)
            c1.wait()
            c2.wait()

        xv = x_ref[...]
        h, _, _ = _rms_fwd(xv, g2_ref[...])
        hb = h.astype(BF16)
        h_ref[...] = hb
        acc = jnp.zeros((tm, d), F32)
        for k in range(2):
            cs = []
            for s in range(2):
                j, cols = k + 2 * s, slice((2 * k + s) * FF_CHUNK, (2 * k + s + 1) * FF_CHUNK)
                uf = _dot(hb, wup_v[j])
                u_ref[:, cols] = uf.astype(BF16)
                ext = jnp.concatenate([tail[j], uf], axis=0)
                tail[j] = uf[tm - h8:tm, :]
                w0, w1, w2 = _conv_taps(cw_ref, j)
                cs.append(cb_ref[j] + w2 * uf + w1 * pltpu.roll(ext, 1, axis=0)[h8:, :]
                          + w0 * pltpu.roll(ext, 2, axis=0)[h8:, :])
            cg, cv = cs
            sg = jax.nn.sigmoid(cg)
            sil = cg * sg
            ab = (sil * cv).astype(BF16)
            a_ref[:, k * FF_CHUNK:(k + 1) * FF_CHUNK] = ab
            s_ref[:, 2 * k * FF_CHUNK:(2 * k + 1) * FF_CHUNK] = sil.astype(BF16)
            s_ref[:, (2 * k + 1) * FF_CHUNK:(2 * k + 2) * FF_CHUNK] = (cv * (sg + sil * (1.0 - sg))).astype(BF16)
            acc = acc + _dot(ab, wdn_v[k * FF_CHUNK:(k + 1) * FF_CHUNK, :])
        f_ref[...] = acc
        y, _, _ = _rms_fwd(acc, g3_ref[...])
        if with_loss:
            err = (xv + y) - tgt_ref[...]
            xo_ref[...] = err * (1.0 / d)
            loss_ref[...] += 0.5 * jnp.sum(jnp.mean(err * err, axis=-1, keepdims=True), axis=0, keepdims=True)
        else:
            xo_ref[...] = xv + y

    row = pl.BlockSpec((tm, d), lambda i: (i, 0))
    wide = pl.BlockSpec((tm, 2 * D_FF), lambda i: (i, 0))
    vec = _full((1, d))
    extra = [target] if with_loss else []
    return pl.pallas_call(
        body, grid=(nt,),
        in_specs=[row, vec, vec, ANY, ANY, _full(conv_w.shape), _full(conv_b.shape)] + [row] * n_extra,
        out_specs=[row, wide, wide, pl.BlockSpec((tm, D_FF), lambda i: (i, 0)), row, row] + [_full((1, 1))] * n_extra,
        out_shape=[jax.ShapeDtypeStruct((t, d), F32), jax.ShapeDtypeStruct((t, 2 * D_FF), BF16),
                   jax.ShapeDtypeStruct((t, 2 * D_FF), BF16), jax.ShapeDtypeStruct((t, D_FF), BF16),
                   jax.ShapeDtypeStruct((t, d), F32), jax.ShapeDtypeStruct((t, d), BF16)]
        + [jax.ShapeDtypeStruct((1, 1), F32)] * n_extra,
        scratch_shapes=[pltpu.VMEM(w_up.shape, BF16), pltpu.VMEM(w_down.shape, BF16),
                        pltpu.VMEM((N_SHARD, h8, FF_CHUNK), F32), pltpu.SemaphoreType.DMA((2,))],
        compiler_params=_params(1), name="mlp_fwd_loss" if with_loss else "mlp_fwd")(
            x, g_pre, g_post, w_up, w_down, conv_w, conv_b, *extra)


def _rowsum8(v):
    return jnp.sum(v.reshape(v.shape[0] // 8, 8, v.shape[1]), axis=0)


def mlp_bwd(dxo, f, x, u, sp, g_pre, g_post, w_up, w_down, conv_w, tm=256):
    t, d = x.shape
    nt = t // tm
    h8 = CONV_HALO

    def body(dxo_ref, f_ref, x_ref, u_ref, s_ref, g2_ref, g3_ref, wup_hbm, wdn_hbm, cw_ref,
             dx_ref, du_ref, df_ref, dg2_ref, dg3_ref, dcw_ref, dcb_ref,
             wup_v, wdn_v, carry, sem):
        @pl.when(pl.program_id(0) == 0)
        def _():
            c1 = pltpu.make_async_copy(wup_hbm, wup_v, sem.at[0])
            c2 = pltpu.make_async_copy(wdn_hbm, wdn_v, sem.at[1])
            c1.start()
            c2.start()
            carry[...] = jnp.zeros_like(carry)
            dg2_ref[...] = jnp.zeros_like(dg2_ref)
            dg3_ref[...] = jnp.zeros_like(dg3_ref)
            dcw_ref[...] = jnp.zeros_like(dcw_ref)
            dcb_ref[...] = jnp.zeros_like(dcb_ref)
            c1.wait()
            c2.wait()

        g3 = g3_ref[...]
        dxo = dxo_ref[...]
        _, fh, rf = _rms_fwd(f_ref[...], g3)
        dg3_ref[...] += _rowsum8(dxo * fh)
        dfb = _rms_bwd(fh, rf, g3, dxo).astype(BF16)
        df_ref[...] = dfb
        dh = jnp.zeros((tm, d), F32)
        for k in range(2):
            da = _dot_nt(dfb, wdn_v[k * FF_CHUNK:(k + 1) * FF_CHUNK, :])
            for s in range(2):
                j = k + 2 * s
                cols = slice((2 * k + s) * FF_CHUNK, (2 * k + s + 1) * FF_CHUNK)
                dc = da * s_ref[:, (2 * k + 1 - s) * FF_CHUNK:(2 * k + 2 - s) * FF_CHUNK].astype(F32)
                uf = u_ref[:, cols].astype(F32)
                ext = jnp.concatenate([dc, carry[j]], axis=0)
                carry[j] = dc[0:h8, :]
                dc1 = pltpu.roll(ext, tm + h8 - 1, axis=0)[0:tm, :]
                dc2 = pltpu.roll(ext, tm + h8 - 2, axis=0)[0:tm, :]
                dcb_ref[j] += _rowsum8(dc)
                dcw_ref[j, 2] += _rowsum8(dc * uf)
                dcw_ref[j, 1] += _rowsum8(dc1 * uf)
                dcw_ref[j, 0] += _rowsum8(dc2 * uf)
                dub = (cw_ref[j, 2:3, :] * dc + cw_ref[j, 1:2, :] * dc1 + cw_ref[j, 0:1, :] * dc2).astype(BF16)
                du_ref[:, cols] = dub
                dh = dh + _dot_nt(dub, wup_v[j])
        g2 = g2_ref[...]
        _, xh, rx = _rms_fwd(x_ref[...], g2)
        dg2_ref[...] += _rowsum8(dh * xh)
        dx_ref[...] = dxo + _rms_bwd(xh, rx, g2, dh)

    row = pl.BlockSpec((tm, d), lambda i: (nt - 1 - i, 0))
    wide = pl.BlockSpec((tm, 2 * D_FF), lambda i: (nt - 1 - i, 0))
    vec = _full((1, d))
    acc = _full((8, d))
    dcw_shape, dcb_shape = (N_SHARD, 3, 8, FF_CHUNK), (N_SHARD, 8, FF_CHUNK)
    return pl.pallas_call(
        body, grid=(nt,),
        in_specs=[row, row, row, wide, wide, vec, vec, ANY, ANY, _full(conv_w.shape)],
        out_specs=[row, wide, row, acc, acc, _full(dcw_shape), _full(dcb_shape)],
        out_shape=[jax.ShapeDtypeStruct((t, d), F32), jax.ShapeDtypeStruct((t, 2 * D_FF), BF16),
                   jax.ShapeDtypeStruct((t, d), BF16),
                   jax.ShapeDtypeStruct((8, d), F32), jax.ShapeDtypeStruct((8, d), F32),
                   jax.ShapeDtypeStruct(dcw_shape, F32), jax.ShapeDtypeStruct(dcb_shape, F32)],
        scratch_shapes=[pltpu.VMEM(w_up.shape, BF16), pltpu.VMEM(w_down.shape, BF16),
                        pltpu.VMEM((N_SHARD, h8, FF_CHUNK), F32), pltpu.SemaphoreType.DMA((2,))],
        compiler_params=_params(1, VMEM_LIMIT_MLP_BWD), name="mlp_bwd")(
            dxo, f, x, u, sp, g_pre, g_post, w_up, w_down, conv_w)


def grad_matmul(a, b, bm, bn, name, tk=2048, interleaved=False, after=None):
    t, m = a.shape
    n = b.shape[1]
    tk = min(tk, t)
    nk = t // tk
    place = (lambda j: (j % 2) * 2 + j // 2) if interleaved else (lambda j: j)
    extra = [] if after is None else [after]

    def body(a_ref, b_ref, *rest):
        o_ref, ob_ref = rest[len(extra):]
        kk = pl.program_id(2)

        @pl.when(kk == 0)
        def _():
            o_ref[...] = jnp.zeros_like(o_ref)

        o_ref[...] += _dot_tn(a_ref[...], b_ref[...])

        @pl.when(kk == nk - 1)
        def _():
            ob_ref[...] = o_ref[...].astype(BF16)

    ospec = pl.BlockSpec((None, bm, bn), lambda j, i, kk: (place(j), i, 0))
    return pl.pallas_call(
        body, grid=(n // bn, m // bm, nk),
        in_specs=[pl.BlockSpec((tk, bm), lambda j, i, kk: (kk, i)),
                  pl.BlockSpec((tk, bn), lambda j, i, kk: (kk, j))]
        + [ANY] * len(extra),
        out_specs=[ospec, ospec],
        out_shape=[jax.ShapeDtypeStruct((n // bn, m, bn), F32), jax.ShapeDtypeStruct((n // bn, m, bn), BF16)],
        compiler_params=_params(3), name=name)(a, b, *extra)


def _decay_tables():
    log_gamma = jnp.log(1.0 - 2.0 ** (-5.0 - jnp.arange(RET_HEADS, dtype=F32)))
    i = jnp.arange(RET_CHUNK, dtype=F32)
    rel = i[:, None] - i[None, :]
    intra = jnp.where(rel >= 0, jnp.exp(jnp.maximum(rel, 0.0) * log_gamma[:, None, None]), 0.0)
    cross = jnp.exp((i + 1.0) * log_gamma[:, None])[:, :, None]
    inner = jnp.exp((RET_CHUNK - 1.0 - i) * log_gamma[:, None])[:, :, None]
    chunk = [float(np.exp(np.float32(RET_CHUNK) * np.log(np.float32(1.0 - 2.0 ** (-5.0 - h))).astype(np.float32)))
             for h in range(RET_HEADS)]
    return intra, cross, inner, chunk


def ret_proj(x, g_pre, w_in, cos, sin, tm=512):
    t, d = x.shape
    nt = t // tm
    per = RET_IN_SHARD // RET_QK

    def body(x_ref, g_ref, win_hbm, c_ref, s_ref, pj_ref, h_ref, win_v, sem):
        @pl.when(pl.program_id(0) == 0)
        def _():
            cp = pltpu.make_async_copy(win_hbm, win_v, sem)
            cp.start()
            cp.wait()

        h, _, _ = _rms_fwd(x_ref[...], g_ref[...])
        hb = h.astype(BF16)
        h_ref[...] = hb
        c = c_ref[...]
        s = s_ref[...]
        for j in range(N_SHARD):
            pjj = _dot(hb, win_v[j])
            for bb in range(per):
                b = per * j + bb
                blk = pjj[:, bb * RET_QK:(bb + 1) * RET_QK]
                if b < 2 * RET_HEADS:
                    x1, x2 = blk[:, :128], blk[:, 128:]
                    o1 = x1 * c - x2 * s
                    o2 = x2 * c + x1 * s
                    if b < RET_HEADS:
                        o1 = o1 * (RET_QK ** -0.5)
                        o2 = o2 * (RET_QK ** -0.5)
                    pj_ref[:, b * RET_QK:b * RET_QK + 128] = o1.astype(BF16)
                    pj_ref[:, b * RET_QK + 128:(b + 1) * RET_QK] = o2.astype(BF16)
                else:
                    pj_ref[:, b * RET_QK:(b + 1) * RET_QK] = blk.astype(BF16)

    row = pl.BlockSpec((tm, d), lambda i: (i, 0))
    tab = pl.BlockSpec((tm, 128), lambda i: (i, 0))
    return pl.pallas_call(
        body, grid=(nt,),
        in_specs=[row, _full((1, d)), ANY, tab, tab],
        out_specs=[pl.BlockSpec((tm, RET_IN), lambda i: (i, 0)), row],
        out_shape=[jax.ShapeDtypeStruct((t, RET_IN), BF16), jax.ShapeDtypeStruct((t, d), BF16)],
        scratch_shapes=[pltpu.VMEM(w_in.shape, BF16), pltpu.SemaphoreType.DMA],
        compiler_params=_params(1), name="ret_proj")(x, g_pre, w_in, cos, sin)


def ret_core_fwd(pj, intra, cross, inner, chunk_decay):
    t = pj.shape[0]
    nc = t // RET_CHUNK
    c = RET_CHUNK
    per = RET_STEP_CHUNKS
    qk_all = RET_HEADS * RET_QK
    v_all = RET_HEADS * RET_V

    def body(q_ref, k_ref, v_ref, dm_ref, cr_ref, in_ref, o_ref, sp_ref, state):
        @pl.when(pl.program_id(0) == 0)
        def _():
            state[...] = jnp.zeros_like(state)

        for h in range(RET_HEADS):
            for cc in range(per):
                rows = slice(cc * c, (cc + 1) * c)
                q = q_ref[rows, h * RET_QK:(h + 1) * RET_QK]
                k = k_ref[rows, h * RET_QK:(h + 1) * RET_QK]
                v = v_ref[rows, h * RET_V:(h + 1) * RET_V]
                sb = state[h].astype(BF16)
                sp_ref[cc, h] = sb
                sc = _dot_nt(q, k) * dm_ref[h]
                o_ref[rows, h * RET_V:(h + 1) * RET_V] = (_dot(sc.astype(BF16), v)
                                                          + _dot(q, sb) * cr_ref[h]).astype(BF16)
                kd = (k.astype(F32) * in_ref[h]).astype(BF16)
                state[h] = state[h] * chunk_decay[h] + _dot_tn(kd, v)

    return pl.pallas_call(
        body, grid=(nc // per,),
        in_specs=[pl.BlockSpec((per * c, qk_all), lambda n: (n, 0)), pl.BlockSpec((per * c, qk_all), lambda n: (n, 1)),
                  pl.BlockSpec((per * c, v_all), lambda n: (n, 1)),
                  _full(intra.shape), _full(cross.shape), _full(inner.shape)],
        out_specs=[pl.BlockSpec((per * c, v_all), lambda n: (n, 0)),
                   pl.BlockSpec((per, RET_HEADS, RET_QK, RET_V), lambda n: (n, 0, 0, 0))],
        out_shape=[jax.ShapeDtypeStruct((t, v_all), BF16),
                   jax.ShapeDtypeStruct((nc, RET_HEADS, RET_QK, RET_V), BF16)],
        scratch_shapes=[pltpu.VMEM((RET_HEADS, RET_QK, RET_V), F32)],
        compiler_params=_params(1), name="ret_core_fwd")(pj, pj, pj, intra, cross, inner)


def _group_norm(o_h):
    mu = jnp.mean(o_h, axis=-1, keepdims=True)
    dev = o_h - mu
    rstd = lax.rsqrt(jnp.mean(dev * dev, axis=-1, keepdims=True) + EPS)
    return dev * rstd, rstd


def ret_out_fwd(o, pj, x, gn_gain, g_post, w_out, tm=512):
    t, d = x.shape
    nt = t // tm
    v_all = RET_HEADS * RET_V

    def body(o_ref, g_ref, x_ref, gn_ref, g1_ref, w_ref, xo_ref, y_ref, out_ref):
        out = jnp.zeros((tm, d), F32)
        for h in range(RET_HEADS):
            cols = slice(h * RET_V, (h + 1) * RET_V)
            ohat, _ = _group_norm(o_ref[:, cols].astype(F32))
            g = g_ref[:, cols].astype(F32)
            yb = (g * jax.nn.sigmoid(g) * (ohat * gn_ref[:, cols])).astype(BF16)
            y_ref[:, cols] = yb
            out = out + _dot(yb, w_ref[cols, :])
        out_ref[...] = out
        m, _, _ = _rms_fwd(out, g1_ref[...])
        xo_ref[...] = x_ref[...] + m

    row = pl.BlockSpec((tm, d), lambda i: (i, 0))
    wide = pl.BlockSpec((tm, v_all), lambda i: (i, 0))
    return pl.pallas_call(
        body, grid=(nt,),
        in_specs=[wide, pl.BlockSpec((tm, v_all), lambda i: (i, 2)), row, _full((1, v_all)), _full((1, d)),
                  _full(w_out.shape)],
        out_specs=[row, wide, row],
        out_shape=[jax.ShapeDtypeStruct((t, d), F32), jax.ShapeDtypeStruct((t, v_all), BF16),
                   jax.ShapeDtypeStruct((t, d), F32)],
        compiler_params=_params(1), name="ret_out_fwd")(o, pj, x, gn_gain, g_post, w_out)


def ret_out_bwd(dxo, out, o, pj, gn_gain, g_post, w_out, after, tm=512):
    t, d = out.shape
    nt = t // tm
    v_all = RET_HEADS * RET_V

    def body(dxo_ref, out_ref, o_ref, g_ref, gn_ref, g1_ref, w_ref, after_ref,
             dout_ref, dgate_ref, do_ref, dg1_ref, dgn_ref):
        @pl.when(pl.program_id(0) == 0)
        def _():
            dg1_ref[...] = jnp.zeros_like(dg1_ref)
            dgn_ref[...] = jnp.zeros_like(dgn_ref)

        g1 = g1_ref[...]
        dxo = dxo_ref[...]
        _, oh_, r_ = _rms_fwd(out_ref[...], g1)
        dg1_ref[...] += _colsum(dxo * oh_)
        doutb = _rms_bwd(oh_, r_, g1, dxo).astype(BF16)
        dout_ref[...] = doutb
        for h in range(RET_HEADS):
            cols = slice(h * RET_V, (h + 1) * RET_V)
            gn = gn_ref[:, cols]
            ohat, rstd = _group_norm(o_ref[:, cols].astype(F32))
            g = g_ref[:, cols].astype(F32)
            sg = jax.nn.sigmoid(g)
            dyh = _dot_nt(doutb, w_ref[cols, :])
            sil = g * sg
            dgate_ref[:, cols] = (dyh * (ohat * gn) * (sg + sil * (1.0 - sg))).astype(BF16)
            don = dyh * sil
            dgn_ref[:, cols] += _colsum(don * ohat)
            dohat = don * gn
            do_ref[:, cols] = (rstd * (dohat - jnp.mean(dohat, axis=-1, keepdims=True)
                                       - ohat * jnp.mean(dohat * ohat, axis=-1, keepdims=True))).astype(BF16)

    row = pl.BlockSpec((tm, d), lambda i: (i, 0))
    wide = pl.BlockSpec((tm, v_all), lambda i: (i, 0))
    gate = pl.BlockSpec((tm, v_all), lambda i: (i, 2))
    return pl.pallas_call(
        body, grid=(nt,),
        in_specs=[row, row, wide, gate, _full((1, v_all)), _full((1, d)), _full(w_out.shape), ANY],
        out_specs=[row, gate, wide, _full((1, d)), _full((1, v_all))],
        out_shape=[jax.ShapeDtypeStruct((t, d), BF16), jax.ShapeDtypeStruct((t, RET_IN), BF16),
                   jax.ShapeDtypeStruct((t, v_all), BF16), jax.ShapeDtypeStruct((1, d), F32),
                   jax.ShapeDtypeStruct((1, v_all), F32)],
        compiler_params=_params(1), name="ret_out_bwd")(dxo, out, o, pj, gn_gain, g_post, w_out, after)


def ret_core_bwd(pj, do, sprev, cos, sin, dpj, intra, cross, inner, chunk_decay):
    t = pj.shape[0]
    nc = t // RET_CHUNK
    c = RET_CHUNK
    per = RET_STEP_CHUNKS
    qk_all = RET_HEADS * RET_QK
    v_all = RET_HEADS * RET_V
    scale = RET_QK ** -0.5

    def body(q_ref, k_ref, v_ref, do_ref, sp_ref, c_ref, s_ref, dm_ref, cr_ref, in_ref, dpj_in, dpj_ref, dstate):
        @pl.when(pl.program_id(0) == 0)
        def _():
            dstate[...] = jnp.zeros_like(dstate)

        for h in range(RET_HEADS):
            for cc in reversed(range(per)):
                rows = slice(cc * c, (cc + 1) * c)
                cs = c_ref[rows, :]
                sn = s_ref[rows, :]
                q = q_ref[rows, h * RET_QK:(h + 1) * RET_QK]
                k = k_ref[rows, h * RET_QK:(h + 1) * RET_QK]
                v = v_ref[rows, h * RET_V:(h + 1) * RET_V]
                doh = do_ref[rows, h * RET_V:(h + 1) * RET_V]
                dm = dm_ref[h]
                ab = (_dot_nt(q, k) * dm).astype(BF16)
                dab = (_dot_nt(doh, v) * dm).astype(BF16)
                dsb = dstate[h].astype(BF16)
                kd = (k.astype(F32) * in_ref[h]).astype(BF16)
                dv = _dot_tn(ab, doh) + _dot(kd, dsb)
                dq = _dot(dab, k) + cr_ref[h] * _dot_nt(doh, sp_ref[cc, h])
                dk = _dot_tn(dab, q) + in_ref[h] * _dot_nt(v, dsb)
                qd = (q.astype(F32) * cr_ref[h]).astype(BF16)
                dstate[h] = dstate[h] * chunk_decay[h] + _dot_tn(qd, doh)
                for base, dd, sc in ((h * RET_QK, dq, scale), (qk_all + h * RET_QK, dk, 1.0)):
                    d1, d2 = dd[:, :128], dd[:, 128:]
                    dpj_ref[rows, base:base + 128] = ((d1 * cs + d2 * sn) * sc).astype(BF16)
                    dpj_ref[rows, base + 128:base + RET_QK] = ((d2 * cs - d1 * sn) * sc).astype(BF16)
                dpj_ref[rows, 2 * qk_all + h * RET_V:2 * qk_all + (h + 1) * RET_V] = dv.astype(BF16)

    rev = lambda n: nc // per - 1 - n
    tab = pl.BlockSpec((per * c, 128), lambda n: (rev(n), 0))
    return pl.pallas_call(
        body, grid=(nc // per,),
        in_specs=[pl.BlockSpec((per * c, qk_all), lambda n: (rev(n), 0)),
                  pl.BlockSpec((per * c, qk_all), lambda n: (rev(n), 1)),
                  pl.BlockSpec((per * c, v_all), lambda n: (rev(n), 1)),
                  pl.BlockSpec((per * c, v_all), lambda n: (rev(n), 0)),
                  pl.BlockSpec((per, RET_HEADS, RET_QK, RET_V), lambda n: (rev(n), 0, 0, 0)),
                  tab, tab, _full(intra.shape), _full(cross.shape), _full(inner.shape), ANY],
        out_specs=pl.BlockSpec((per * c, 2 * qk_all + v_all), lambda n: (rev(n), 0)),
        out_shape=jax.ShapeDtypeStruct((t, RET_IN), BF16),
        scratch_shapes=[pltpu.VMEM((RET_HEADS, RET_QK, RET_V), F32)],
        input_output_aliases={10: 0},
        compiler_params=_params(1), name="ret_core_bwd")(pj, pj, pj, do, sprev, cos, sin, intra, cross, inner, dpj)


def ret_in_bwd(dpj, dres, x, g_pre, w_in, after, tm=512):
    t, d = x.shape
    nt = t // tm

    def body(dpj_ref, dres_ref, x_ref, g_ref, win_hbm, after_ref, dx_ref, dg_ref, win_v, sem):
        @pl.when(pl.program_id(0) == 0)
        def _():
            cp = pltpu.make_async_copy(win_hbm, win_v, sem)
            cp.start()
            dg_ref[...] = jnp.zeros_like(dg_ref)
            cp.wait()

        g = g_ref[...]
        dh = jnp.zeros((tm, d), F32)
        for j in range(N_SHARD):
            dh = dh + _dot_nt(dpj_ref[:, j * RET_IN_SHARD:(j + 1) * RET_IN_SHARD], win_v[j])
        _, xh, rx = _rms_fwd(x_ref[...], g)
        dg_ref[...] += _colsum(dh * xh)
        dx_ref[...] = dres_ref[...] + _rms_bwd(xh, rx, g, dh)

    row = pl.BlockSpec((tm, d), lambda i: (i, 0))
    return pl.pallas_call(
        body, grid=(nt,),
        in_specs=[pl.BlockSpec((tm, RET_IN), lambda i: (i, 0)), row, row, _full((1, d)), ANY, ANY],
        out_specs=[row, _full((1, d))],
        out_shape=[jax.ShapeDtypeStruct((t, d), F32), jax.ShapeDtypeStruct((1, d), F32)],
        scratch_shapes=[pltpu.VMEM(w_in.shape, BF16), pltpu.SemaphoreType.DMA],
        compiler_params=_params(1), name="ret_in_bwd")(dpj, dres, x, g_pre, w_in, after)


_CHIP_FLIPS = ((1, 0), (0, 1), (1, 1))


def _flip(v, b):
    return 1 - v if b else v


_HBM = pl.BlockSpec(memory_space=pltpu.HBM)
_SEM = pl.BlockSpec(memory_space=pltpu.SEMAPHORE)
_EFFECT = pltpu.SideEffectType.DATAFLOW_SIDE_EFFECTING


def _chip_copies(mode, srcs, lands, send_sems, recv_sems):
    x, y, c = lax.axis_index("x"), lax.axis_index("y"), lax.axis_index("c")
    copies = []
    for t in range(len(lands)):
        if mode == "swap":
            copies.append(pltpu.make_async_remote_copy(
                src_ref=srcs[t], dst_ref=lands[t], send_sem=send_sems.at[t], recv_sem=recv_sems.at[t],
                device_id=(x, y, 1 - c), device_id_type=MESH))
            continue
        if mode == "everyone":
            for m in range(1, 8):
                bx, by, bc = (m >> 2) & 1, (m >> 1) & 1, m & 1
                copies.append(pltpu.make_async_remote_copy(
                    src_ref=srcs[t], dst_ref=lands[t].at[4 * x + 2 * y + c], send_sem=send_sems.at[7 * t + m - 1],
                    recv_sem=recv_sems.at[7 * t + m - 1], device_id=(_flip(x, bx), _flip(y, by), _flip(c, bc)),
                    device_id_type=MESH))
            continue
        for k, (bx, by) in enumerate(_CHIP_FLIPS):
            px, py = _flip(x, bx), _flip(y, by)
            target = (px, py, c)
            if mode == "gather":
                src, dst = srcs[t], lands[t].at[2 * x + y]
            elif mode == "gather_half":
                half = pl.ds(c * (srcs[t].shape[0] // 2), srcs[t].shape[0] // 2)
                src, dst = srcs[t].at[half], lands[t].at[2 * x + y, half]
            elif mode == "forward_half":
                half = pl.ds(c * (lands[t].shape[1] // 2), lands[t].shape[1] // 2)
                src = dst = lands[t].at[2 * px + py, half]
                target = (x, y, 1 - c)
            else:
                src, dst = srcs[t].at[2 * px + py], lands[t].at[k]
            copies.append(pltpu.make_async_remote_copy(
                src_ref=src, dst_ref=dst, send_sem=send_sems.at[3 * t + k], recv_sem=recv_sems.at[3 * t + k],
                device_id=target, device_id_type=MESH))
    return copies


def exchange_start(mode, srcs, lands, name, after=None):
    n, ns = len(lands), len(srcs)
    extra = [] if after is None else [after]

    def body(*refs):
        ins, lnd = refs[:ns], refs[ns:ns + n]
        send_sems, recv_sems = refs[ns + n + len(extra)], refs[ns + n + len(extra) + 1]
        token = refs[-1]
        for cp in _chip_copies(mode, ins, lnd, send_sems, recv_sems):
            cp.start()
        token[...] = jnp.zeros(token.shape, token.dtype)

    hbm = lambda a: pltpu.with_memory_space_constraint(a, pltpu.HBM)
    passed = list(srcs) + list(lands)
    n_sem = {"swap": 1, "everyone": 7}.get(mode, 3) * n
    return pl.pallas_call(
        body, name=name,
        out_shape=(pltpu.SemaphoreType.DMA((n_sem,)), pltpu.SemaphoreType.DMA((n_sem,)),
                   *[pltpu.HBM(a.shape, a.dtype) for a in passed], jax.ShapeDtypeStruct((8, 128), F32)),
        in_specs=[_HBM] * (ns + n) + [ANY] * len(extra),
        out_specs=(_SEM, _SEM, *[_HBM] * (ns + n), pl.BlockSpec(memory_space=pltpu.VMEM)),
        input_output_aliases={i: 2 + i for i in range(ns + n)},
        compiler_params=pltpu.CompilerParams(has_side_effects=_EFFECT))(*[hbm(a) for a in passed], *extra)


def exchange_wait(mode, started, after, name):
    send_sems, recv_sems = started[0], started[1]
    passed = list(started[2:-1])
    n = len(passed) if mode == "forward_half" else len(passed) // 2
    ns = len(passed) - n
    after = list(after) if isinstance(after, (list, tuple)) else [after]

    def body(*refs):
        ins, lnd = refs[:ns], refs[ns:ns + n]
        for cp in _chip_copies(mode, ins, lnd, refs[ns + n], refs[ns + n + 1]):
            cp.wait_send()
            cp.wait_recv()

    outs = pl.pallas_call(
        body, name=name, out_shape=tuple(pltpu.HBM(a.shape, a.dtype) for a in passed),
        in_specs=[_HBM] * (ns + n) + [_SEM, _SEM] + [ANY] * len(after), out_specs=tuple([_HBM] * (ns + n)),
        input_output_aliases={i: i for i in range(ns + n)},
        compiler_params=pltpu.CompilerParams(has_side_effects=_EFFECT))(*passed, send_sems, recv_sems, *after)
    return list(outs[:ns]), list(outs[ns:])


def plane_sum(slot, full, recv, name, bm=256):
    _, m, n = full.shape
    bm = _row_block(m, bm)

    def body(slot_ref, o_ref, r_ref, s_ref):
        s_ref[...] = ((o_ref[...] + r_ref[0].astype(F32)) + r_ref[1].astype(F32)) + r_ref[2].astype(F32)

    return pl.pallas_call(
        body,
        grid_spec=pltpu.PrefetchScalarGridSpec(
            num_scalar_prefetch=1, grid=(m // bm,),
            in_specs=[pl.BlockSpec((None, bm, n), lambda i, s: (s[0], i, 0)),
                      pl.BlockSpec((3, bm, n), lambda i, s: (0, i, 0))],
            out_specs=pl.BlockSpec((bm, n), lambda i, s: (i, 0))),
        out_shape=jax.ShapeDtypeStruct((m, n), F32), compiler_params=_params(1), name=name)(slot, full, recv)


def sum_slots(parts, name, bm=312):
    _, r, n = parts.shape
    bm = bm if r % bm == 0 else r

    def body(p_ref, s_ref):
        acc = p_ref[0]
        for k in range(1, 8):
            acc = acc + p_ref[k]
        s_ref[...] = acc

    return pl.pallas_call(
        body, grid=(r // bm,), in_specs=[pl.BlockSpec((8, bm, n), lambda i: (0, i, 0))],
        out_specs=pl.BlockSpec((bm, n), lambda i: (i, 0)), out_shape=jax.ShapeDtypeStruct((r, n), F32),
        compiler_params=_params(1), name=name)(parts)


def _adamw_math(w, g, m, v):
    m = ADAM_B1 * m + (1.0 - ADAM_B1) * g
    v = ADAM_B2 * v + (1.0 - ADAM_B2) * (g * g)
    m_hat = m / (1.0 - ADAM_B1 ** ADAM_STEP)
    v_hat = v / (1.0 - ADAM_B2 ** ADAM_STEP)
    delta = -ADAM_LR * (m_hat / (jnp.sqrt(v_hat) + ADAM_EPS) + ADAM_WD * w)
    return delta, m, v


def adamw(w, m, v, grads, layer, prev, name, bm=256):
    _, _, n = w.shape
    mm = grads[0].shape[0]
    bm = _row_block(mm, bm)
    ng = len(grads)

    def body(*refs):
        w_ref, m_ref, v_ref = refs[:3]
        g_refs = refs[3:3 + ng]
        g_out, d_out, m_out, v_out = refs[-4:]
        g = g_refs[0][...]
        for gr in g_refs[1:]:
            g = g + gr[...]
        delta, mn, vn = _adamw_math(w_ref[...], g, m_ref[...], v_ref[...])
        g_out[...] = g
        d_out[...] = delta
        m_out[...] = mn
        v_out[...] = vn

    slab = pl.BlockSpec((None, bm, n), lambda i: (layer, i, 0))
    flat = pl.BlockSpec((bm, n), lambda i: (i, 0))
    in_specs = [slab] * 3 + [flat] * ng
    args = [w, m, v, *grads]
    aliases = {}
    if prev is not None:
        in_specs += [ANY] * 4
        aliases = {3 + ng + q: q for q in range(4)}
        args += list(prev)
    return pl.pallas_call(
        body, grid=(mm // bm,), in_specs=in_specs, out_specs=[slab] * 4,
        out_shape=[jax.ShapeDtypeStruct(w.shape, F32)] * 4, input_output_aliases=aliases,
        compiler_params=_params(1), name=name)(*args)


def adamw_streamed(w, m, v, grads, layer, prev, name, bm=128):
    _, _, n = w.shape
    mm = grads[0].shape[0]
    bm = _row_block(mm, bm)
    nb, ng = mm // bm, len(grads)
    n_in, n_prev = 3 + ng, 0 if prev is None else 4

    def body(*refs):
        srcs = [r.at[layer] for r in refs[:3]] + list(refs[3:n_in])
        outs = [r.at[layer] for r in refs[n_in + n_prev:n_in + n_prev + 4]]
        inbuf, outbuf, sem_in, sem_out = refs[-4:]

        def read(i, q):
            slot = i % STREAM_BUFFERS
            return pltpu.make_async_copy(srcs[q].at[pl.ds(i * bm, bm)], inbuf.at[slot, q], sem_in.at[slot, q])

        def write(i, q):
            return pltpu.make_async_copy(outbuf.at[i % 2, q], outs[q].at[pl.ds(i * bm, bm)], sem_out.at[i % 2, q])

        for i in range(min(STREAM_BUFFERS, nb)):
            for q in range(n_in):
                read(i, q).start()
        for i in range(nb):
            slot = i % STREAM_BUFFERS
            for q in range(n_in):
                read(i, q).wait()
            if i >= 2:
                for q in range(4):
                    write(i - 2, q).wait()
            g = inbuf[slot, 3]
            for k in range(1, ng):
                g = g + inbuf[slot, 3 + k]
            delta, mn, vn = _adamw_math(inbuf[slot, 0], g, inbuf[slot, 1], inbuf[slot, 2])
            for q, val in enumerate((g, delta, mn, vn)):
                outbuf[i % 2, q] = val
            for q in range(4):
                write(i, q).start()
            if i + STREAM_BUFFERS < nb:
                for q in range(n_in):
                    read(i + STREAM_BUFFERS, q).start()
        for i in range(max(nb - 2, 0), nb):
            for q in range(4):
                write(i, q).wait()

    args = [w, m, v, *grads] + ([] if prev is None else list(prev))
    return pl.pallas_call(
        body, in_specs=[ANY] * len(args), out_specs=[ANY] * 4,
        out_shape=[jax.ShapeDtypeStruct(w.shape, F32)] * 4,
        input_output_aliases={n_in + q: q for q in range(n_prev)},
        scratch_shapes=[pltpu.VMEM((STREAM_BUFFERS, n_in, bm, n), F32), pltpu.VMEM((2, 4, bm, n), F32),
                        pltpu.SemaphoreType.DMA((STREAM_BUFFERS, n_in)), pltpu.SemaphoreType.DMA((2, 4))],
        compiler_params=_params(0), name=name)(*args)


def _pack_rows(parts, rows):
    flat = jnp.concatenate([p.reshape(-1) for p in parts])
    return jnp.pad(flat, (0, rows * 128 - flat.shape[0])).reshape(rows, 128)


def _as_shards(a, rows):
    return a.reshape(N_SHARD, rows, a.shape[-1])


def _local_step(x, pos_col, target, gains, pool_w, pool_scale, gn_gain, conv_w, conv_b, weights, send_grads):
    def gain(l, n):
        return gains[l, n].reshape(1, D_MODEL)

    inv_freq = (ROPE_BASE ** (-jnp.arange(0, RET_QK, 2, dtype=F32) / RET_QK)).reshape(1, RET_QK // 2)
    intra, cross, inner, chunk_decay = _decay_tables()
    dn_rows = D_FF // N_SHARD

    x1 = pool_fwd(x, gain(0, 0), gain(0, 1), pool_w, pool_scale)
    cos, sin = rope_tables(pos_col, inv_freq, x1)
    w_up0, w_dn0 = weights("mlp0", cos)
    w_dn0 = w_dn0.reshape(D_FF, D_MODEL)
    x2, u0, s0, a0, f0, h0 = mlp_fwd(x1, gain(0, 2), gain(0, 3), w_up0, w_dn0, conv_w[0], conv_b[0])
    w_in, w_out = weights("ret", x2)
    w_out = w_out.reshape(RET_HEADS * RET_V, D_MODEL)
    pj, hr = ret_proj(x2, gain(1, 0), w_in, cos, sin)
    o, sprev = ret_core_fwd(pj, intra, cross, inner, chunk_decay)
    x3, yb, out = ret_out_fwd(o, pj, x2, gn_gain, gain(1, 1), w_out)
    w_up1, w_dn1 = weights("mlp1", x3)
    w_dn1 = w_dn1.reshape(D_FF, D_MODEL)
    dx4, u1, s1, a1, f1, h1, loss = mlp_fwd(x3, gain(1, 2), gain(1, 3), w_up1, w_dn1, conv_w[1], conv_b[1], target)

    dx3, du1, df1, dg12, dg13, dcw1, dcb1 = mlp_bwd(
        dx4, f1, x3, u1, s1, gain(1, 2), gain(1, 3), w_up1, w_dn1, conv_w[1])
    dwup1 = grad_matmul(h1, du1, D_MODEL, FF_CHUNK, "grad_w_up_1", interleaved=True)
    dwdn1 = grad_matmul(a1, df1, FF_CHUNK, D_MODEL, "grad_w_down_1")
    tok = send_grads("mlp1", [dwup1, [_as_shards(g, dn_rows) for g in dwdn1]])
    dout, dpj, do, dg11, dgn = ret_out_bwd(dx3, out, o, pj, gn_gain, gain(1, 1), w_out, tok)
    dwout = grad_matmul(yb, dout, 1024, D_MODEL, "grad_w_out")
    dpj = ret_core_bwd(pj, do, sprev, cos, sin, dpj, intra, cross, inner, chunk_decay)
    dwin = grad_matmul(hr, dpj, D_MODEL, RET_IN_SHARD, "grad_w_in")
    tok = send_grads("ret", [dwin, [_as_shards(g, RET_V) for g in dwout]])
    dx2, dg10 = ret_in_bwd(dpj, dx3, x2, gain(1, 0), w_in, tok)
    dx1, du0, df0, dg02, dg03, dcw0, dcb0 = mlp_bwd(
        dx2, f0, x1, u0, s0, gain(0, 2), gain(0, 3), w_up0, w_dn0, conv_w[0])
    dwdn0 = grad_matmul(a0, df0, FF_CHUNK, D_MODEL, "grad_w_down_0")
    tok = send_grads("down0", [[_as_shards(g, dn_rows) for g in dwdn0]])
    dwup0 = grad_matmul(h0, du0, D_MODEL, FF_CHUNK, "grad_w_up_0", interleaved=True, after=tok)
    tok = send_grads("up0", [dwup0])
    dx0, dg00, dg01, dpscale, dpw = pool_bwd(dx1, x, gain(0, 0), gain(0, 1), pool_w, pool_scale, tok)

    rows = lambda g: jnp.sum(g, axis=0, keepdims=True)
    dgains = jnp.concatenate([dg00, dg01, rows(dg02), rows(dg03), dg10, dg11, rows(dg12), rows(dg13)],
                             axis=0).reshape(2, 4, D_MODEL)
    small = {"gains": dgains, "pool_scale": dpscale, "gn": dgn,
             "conv_w": jnp.sum(jnp.stack([dcw0, dcw1]), axis=3),
             "conv_b": jnp.sum(jnp.stack([dcb0, dcb1]), axis=2, keepdims=True), "pool_w": dpw}
    return loss, dx0, small


def kernel(x, positions, norm_gain, pool_w, pool_scale, ret_w_in, ret_gn_gain, ret_w_out, mlp_w_up, mlp_conv_w, mlp_conv_b, mlp_w_down, loss_target, m_norm_gain, m_pool_w, m_pool_scale, m_ret_w_in, m_ret_gn_gain, m_ret_w_out, m_mlp_w_up, m_mlp_conv_w, m_mlp_conv_b, m_mlp_w_down, v_norm_gain, v_pool_w, v_pool_scale, v_ret_w_in, v_ret_gn_gain, v_ret_w_out, v_mlp_w_up, v_mlp_conv_w, v_mlp_conv_b, v_mlp_w_down):
    t = x.shape[1]
    me = 2 * lax.axis_index("x") + lax.axis_index("y")
    me_slot = jnp.reshape(me, (1,)).astype(jnp.int32)

    small_parts = [norm_gain, ret_gn_gain, mlp_conv_w, pool_w]
    small_sizes = [p.size for p in small_parts]
    small_rows = -(-sum(small_sizes) // (128 * 8)) * 8
    gathers = {}

    def start_gather(group, srcs, after):
        lands = [lax.dynamic_update_index_in_dim(lax.empty((N_SHARD,) + s.shape, s.dtype), s, me, 0) for s in srcs]
        mode = "gather_half" if group == "mlp0" else "gather"
        gathers[group] = (mode, exchange_start(mode, srcs, lands, "gather_start_" + group, after=after))
        return gathers[group][1][-1]

    token = start_gather("small", [_pack_rows(small_parts, small_rows)], None)
    token = start_gather("mlp0", [mlp_w_up[0].astype(BF16), mlp_w_down[0].astype(BF16)], token)

    def weights(group, after):
        if group == "mlp0":
            tok = start_gather("ret", [ret_w_in[0].astype(BF16), ret_w_out[0].astype(BF16)], after)
            after = start_gather("mlp1", [mlp_w_up[1].astype(BF16), mlp_w_down[1].astype(BF16)], tok)
        mode, started = gathers[group]
        _, lands = exchange_wait(mode, started, after, "gather_wait_" + group)
        if mode == "gather_half":
            forward = exchange_start("forward_half", [], lands, "forward_start_" + group)
            _, lands = exchange_wait("forward_half", forward, forward[-1], "forward_wait_" + group)
        return lands

    sent, early = {}, {}

    def reduced(group, after, names):
        started, own = sent[group]
        _, recv = exchange_wait("scatter", started, after, "scatter_wait_" + group)
        return [plane_sum(me_slot, f, r, "plane_sum_" + nm)
                for f, r, nm in zip(own, recv, names)]

    def swap_start(planes, name):
        return exchange_start("swap", planes, [lax.empty(p.shape, p.dtype) for p in planes], name)

    def send_grads(group, pairs):
        lands = [lax.empty((3,) + b.shape[1:], BF16) for _, b in pairs]
        sent[group] = (exchange_start("scatter", [b for _, b in pairs], lands, "scatter_start_" + group),
                       [f for f, _ in pairs])
        token = sent[group][0][-1]
        if group == "up0":
            early["planes"] = (reduced("mlp1", token, ["w_up_1", "w_down_1"])
                               + reduced("ret", token, ["w_in", "w_out"]))
            early["swap"] = swap_start(early["planes"], "swap_start_a")
            token = early["swap"][-1]
        return token

    (smallg,) = weights("small", token)
    smallg = smallg.reshape(N_SHARD, -1)
    offs = np.cumsum([0] + small_sizes)
    piece = lambda i, shape: smallg[:, offs[i]:offs[i + 1]].reshape((N_SHARD,) + shape)
    gains = piece(0, (2, 4, 256)).transpose(1, 2, 0, 3).reshape(2, 4, D_MODEL)
    gn_full = piece(1, (512,)).reshape(1, RET_HEADS * RET_V)
    cw_full = piece(2, (2, 3, FF_CHUNK)).transpose(1, 0, 2, 3)
    pw_full = piece(3, (4, 64, 256)).transpose(1, 0, 2, 3).reshape(4, 256, 256).astype(BF16)
    cb_full = mlp_conv_b.reshape(2, N_SHARD, 1, FF_CHUNK)

    loss, dx0, small = _local_step(
        x[0], positions.reshape(t, 1).astype(F32), loss_target[0], gains, pw_full, pool_scale, gn_full,
        cw_full, cb_full, weights, send_grads)

    def small_adamw(w, m, v, grads, name):
        w3 = w.reshape(1, -1, w.shape[-1])
        out = adamw(w3, m.reshape(w3.shape), v.reshape(w3.shape), [g.reshape(w3.shape[1:]) for g in grads], 0, None, name)
        return [o.reshape(w.shape) for o in out]

    pw_f = small["pool_w"].reshape(4, N_SHARD, 64, 256).transpose(1, 0, 2, 3).reshape(N_SHARD, 256, 256)
    small = dict(small, loss=loss)
    small_order = ["gains", "pool_scale", "gn", "conv_w", "conv_b", "loss"]
    gsmall_sizes = [small[k].size for k in small_order]
    gsmall_rows = -(-sum(gsmall_sizes) // (128 * 8)) * 8
    gpack = _pack_rows([small[k] for k in small_order], gsmall_rows)
    mine = 2 * me + lax.axis_index("c")
    small_started = exchange_start(
        "everyone", [gpack], [lax.dynamic_update_index_in_dim(lax.empty((8,) + gpack.shape, F32), gpack, mine, 0)],
        "small_start")
    pool_token = send_grads("pool_w", [(pw_f, pw_f.astype(BF16))])

    res = {}
    planes_a, others_a = exchange_wait("swap", early["swap"], [small_started[-1], pool_token], "swap_wait_a")
    big = adamw_streamed
    res["ret_w_in"] = big(ret_w_in, m_ret_w_in, v_ret_w_in, (planes_a[2], others_a[2]), 0, None, "adamw_w_in")
    res["ret_w_out"] = big(ret_w_out, m_ret_w_out, v_ret_w_out, (planes_a[3], others_a[3]), 0, None, "adamw_w_out")
    up1 = big(mlp_w_up, m_mlp_w_up, v_mlp_w_up, (planes_a[0], others_a[0]), 1, None, "adamw_w_up_1")
    dn1 = big(mlp_w_down, m_mlp_w_down, v_mlp_w_down, (planes_a[1], others_a[1]), 1, None, "adamw_w_down_1")

    done_a = [res["ret_w_in"][0], res["ret_w_out"][0], up1[0], dn1[0]]
    planes_b = (reduced("up0", done_a, ["w_up_0"]) + reduced("down0", done_a, ["w_down_0"])
                + reduced("pool_w", done_a, ["pool_w"]))
    swap_b = swap_start(planes_b, "swap_start_b")

    _, (small_recv,) = exchange_wait("everyone", small_started, swap_b[-1], "small_wait")
    gsmall = sum_slots(small_recv, "sum_small").reshape(-1)
    goffs = np.cumsum([0] + gsmall_sizes)
    gpiece = lambda i: gsmall[goffs[i]:goffs[i + 1]].reshape(small[small_order[i]].shape)
    g_gains = lax.dynamic_slice_in_dim(gpiece(0), me * 256, 256, axis=2)
    g_gn = lax.dynamic_slice_in_dim(gpiece(2), me * RET_V, RET_V, axis=1)
    g_cw = lax.dynamic_index_in_dim(gpiece(3), me, 1, keepdims=False)
    res["norm_gain"] = small_adamw(norm_gain, m_norm_gain, v_norm_gain, [g_gains], "adamw_norm_gain")
    res["pool_scale"] = small_adamw(pool_scale, m_pool_scale, v_pool_scale, [gpiece(1)], "adamw_pool_scale")
    res["ret_gn_gain"] = small_adamw(ret_gn_gain, m_ret_gn_gain, v_ret_gn_gain, [g_gn], "adamw_gn_gain")
    res["mlp_conv_w"] = small_adamw(mlp_conv_w, m_mlp_conv_w, v_mlp_conv_w, [g_cw], "adamw_conv_w")
    res["mlp_conv_b"] = small_adamw(mlp_conv_b, m_mlp_conv_b, v_mlp_conv_b, [gpiece(4)], "adamw_conv_b")

    small_done = [res[k][0] for k in ("norm_gain", "pool_scale", "ret_gn_gain", "mlp_conv_w", "mlp_conv_b")]
    planes_b, others_b = exchange_wait("swap", swap_b, small_done, "swap_wait_b")
    res["mlp_w_up"] = big(mlp_w_up, m_mlp_w_up, v_mlp_w_up, (planes_b[0], others_b[0]), 0, up1, "adamw_w_up_0")
    res["mlp_w_down"] = big(mlp_w_down, m_mlp_w_down, v_mlp_w_down, (planes_b[1], others_b[1]), 0, dn1,
                            "adamw_w_down_0")
    res["pool_w"] = small_adamw(pool_w, m_pool_w, v_pool_w, (planes_b[2], others_b[2]), "adamw_pool_w")

    order = ["norm_gain", "pool_w", "pool_scale", "ret_w_in", "ret_gn_gain", "ret_w_out", "mlp_w_up", "mlp_conv_w",
             "mlp_conv_b", "mlp_w_down"]
    outs = [gpiece(5)[0, 0], dx0.reshape(x.shape)]
    for q in range(4):
        outs += [res[k][q] for k in order]
    return tuple(outs)
```
